```python
import jax, jax.numpy as jnp
from jax import lax
import numpy as np


D_MODEL = 1024
BATCH = 8
SEQ = 8192
DEPTH = 2

MIX_WIDTH = D_MODEL
POOL_WIDTH = D_MODEL // 4
SG_WIDTH = D_MODEL // 4
SB_WIDTH = D_MODEL // 2
POOL_WINDOWS = (2, 4, 8, 16)
N_POOL_GROUPS = len(POOL_WINDOWS)
POOL_GW = POOL_WIDTH // N_POOL_GROUPS
CHUNK = 128
SG_HEADS = 4
SG_HD = SG_WIDTH // SG_HEADS
SB_HD = 64
SB_HEADS = SB_WIDTH // SB_HD
Q_BLOCK = 128
IN_COLS = POOL_WIDTH + 2 * SG_WIDTH + 3 * SB_WIDTH
D_FF = 4 * D_MODEL
EPS = 1e-6

kernel_name = "hybrid_pool_sgmlp_stickbreak_block"


def rms_norm(x, g):
    xf = x.astype(jnp.float32)
    y = xf * lax.rsqrt(jnp.mean(xf * xf, axis=-1, keepdims=True) + EPS)
    return (y * g.astype(jnp.float32)).astype(x.dtype)


def pool_mixer(h, w_grp, scale):
    B, S, _ = h.shape
    hf = h.astype(jnp.float32)
    cs = jnp.cumsum(hf, axis=1)
    t = jnp.arange(S)
    pooled = []
    for gi, w in enumerate(POOL_WINDOWS):
        c = cs[..., gi * POOL_GW:(gi + 1) * POOL_GW]
        lag = jnp.pad(c, ((0, 0), (w, 0), (0, 0)))[:, :S]
        cnt = jnp.minimum(t + 1, w).astype(jnp.float32)[None, :, None]
        pooled.append((c - lag) / cnt)
    d = (jnp.concatenate(pooled, axis=-1) - hf).astype(h.dtype)
    d = d.reshape(B, S, N_POOL_GROUPS, POOL_GW)
    y = jnp.einsum('bsgc,gcd->bsgd', d, w_grp).reshape(B, S, POOL_WIDTH)
    return y * scale


def spatial_gate(z, g_norm, w_s, b_s):
    B, S, _ = z.shape
    u, v = z[..., :SG_WIDTH], z[..., SG_WIDTH:]
    v = rms_norm(v, g_norm)
    v = v.reshape(B, S // CHUNK, CHUNK, SG_HEADS, SG_HD)
    mask = jnp.tril(jnp.ones((CHUNK, CHUNK), dtype=w_s.dtype))
    sv = jnp.einsum('hts,bnshc->bnthc', w_s * mask, v)
    sv = sv + b_s.T[None, None, :, :, None]
    return u * sv.reshape(B, S, SG_WIDTH)


def stick_breaking_attention(q, k, v):
    B, S, _ = q.shape
    nb = S // Q_BLOCK
    q = q.reshape(B, S, SB_HEADS, SB_HD).transpose(0, 2, 1, 3)
    k = k.reshape(B, S, SB_HEADS, SB_HD).transpose(0, 2, 1, 3)
    v = v.reshape(B, S, SB_HEADS, SB_HD).transpose(0, 2, 1, 3)
    qb = q.reshape(B, SB_HEADS, nb, Q_BLOCK, SB_HD).transpose(2, 0, 1, 3, 4)
    inv_sqrt_d = 1.0 / np.sqrt(SB_HD).astype(np.float32)
    tk = jnp.arange(S)

    def block(args):
        i, qi = args
        z = jnp.einsum('bhqd,bhkd->bhqk', qi, k).astype(jnp.float32) * inv_sqrt_d
        tq = i * Q_BLOCK + jnp.arange(Q_BLOCK)
        causal = (tk[None, :] < tq[:, None])[None, None]
        log_beta = jax.nn.log_sigmoid(z)
        log_1m = jnp.where(causal, jax.nn.log_sigmoid(-z), 0.0)
        after = lax.cumsum(log_1m, axis=3, reverse=True) - log_1m
        a = jnp.where(causal, jnp.exp(log_beta + after), 0.0)
        return jnp.einsum('bhqk,bhkd->bhqd', a.astype(v.dtype), v)

    out = lax.map(block, (jnp.arange(nb), qb))
    return out.transpose(1, 0, 3, 2, 4).reshape(B, S, SB_WIDTH)


def _fwd_setup_inputs(seed: int = 0) -> dict:
    key = jax.random.key(seed)
    ks = jax.random.split(key, 16)
    f32 = jnp.float32
    nrm = lambda k, shape, s: jax.random.normal(k, shape, f32) * s
    return {
        "x": jax.random.normal(ks[0], (BATCH, SEQ, D_MODEL), f32),
        "norm1": 1.0 + nrm(ks[1], (DEPTH, D_MODEL), 0.02),
        "w_in": nrm(ks[2], (DEPTH, D_MODEL, IN_COLS), D_MODEL ** -0.5),
        "pool_w": nrm(ks[3], (DEPTH, N_POOL_GROUPS, POOL_GW, POOL_GW), POOL_GW ** -0.5),
        "pool_scale": 1.0 + nrm(ks[4], (DEPTH, POOL_WIDTH), 0.02),
        "sg_norm": 1.0 + nrm(ks[5], (DEPTH, SG_WIDTH), 0.02),
        "sg_w": nrm(ks[6], (DEPTH, SG_HEADS, CHUNK, CHUNK), 0.5 * CHUNK ** -0.5),
        "sg_b": 1.0 + nrm(ks[7], (DEPTH, SG_HEADS, CHUNK), 0.1),
        "w_out": nrm(ks[8], (DEPTH, MIX_WIDTH, D_MODEL), MIX_WIDTH ** -0.5),
        "norm2": 1.0 + nrm(ks[9], (DEPTH, D_MODEL), 0.02),
        "w_up": nrm(ks[10], (DEPTH, D_MODEL, D_FF), D_MODEL ** -0.5),
        "w_down": nrm(ks[11], (DEPTH, D_FF, D_MODEL), 0.5 * D_FF ** -0.5),
        "final_norm": 1.0 + nrm(ks[12], (D_MODEL,), 0.02),
    }


def _fwd_reference(x, norm1, w_in, pool_w, pool_scale, sg_norm, sg_w, sg_b, w_out, norm2, w_up, w_down, final_norm):
    c1 = POOL_WIDTH
    c2 = c1 + 2 * SG_WIDTH
    c3 = c2 + SB_WIDTH
    c4 = c3 + SB_WIDTH
    for l in range(DEPTH):
        h = rms_norm(x, norm1[l])
        proj = h @ w_in[l]
        a_in, b_in = proj[..., :c1], proj[..., c1:c2]
        q, k, v = proj[..., c2:c3], proj[..., c3:c4], proj[..., c4:]
        ya = pool_mixer(a_in, pool_w[l], pool_scale[l])
        yb = spatial_gate(jax.nn.gelu(b_in), sg_norm[l], sg_w[l], sg_b[l])
        yc = stick_breaking_attention(q, k, v)
        x = x + jnp.concatenate([ya, yb, yc], axis=-1) @ w_out[l]
        h = rms_norm(x, norm2[l])
        x = x + jnp.square(jax.nn.relu(h @ w_up[l])) @ w_down[l]
    return rms_norm(x, final_norm)


import jax as _jax
import jax.numpy as _jnp

TWIN_FORMAT = 'train_step'
FWD_PARAMS = ['x', 'norm1', 'w_in', 'pool_w', 'pool_scale', 'sg_norm', 'sg_w', 'sg_b', 'w_out', 'norm2', 'w_up', 'w_down', 'final_norm']
TWIN_WEIGHTS = ['norm1', 'w_in', 'pool_w', 'pool_scale', 'sg_norm', 'sg_w', 'sg_b', 'w_out', 'norm2', 'w_up', 'w_down', 'final_norm']
TWIN_DIFF_INPUT = 'x'
TWIN_INPUTS = ['x', 'norm1', 'w_in', 'pool_w', 'pool_scale', 'sg_norm', 'sg_w', 'sg_b', 'w_out', 'norm2', 'w_up', 'w_down', 'final_norm', 'loss_target', 'm_norm1', 'm_w_in', 'm_pool_w', 'm_pool_scale', 'm_sg_norm', 'm_sg_w', 'm_sg_b', 'm_w_out', 'm_norm2', 'm_w_up', 'm_w_down', 'm_final_norm', 'v_norm1', 'v_w_in', 'v_pool_w', 'v_pool_scale', 'v_sg_norm', 'v_sg_w', 'v_sg_b', 'v_w_out', 'v_norm2', 'v_w_up', 'v_w_down', 'v_final_norm']
TWIN_OUTPUTS = ['loss', 'grad_x', 'grad_norm1', 'grad_w_in', 'grad_pool_w', 'grad_pool_scale', 'grad_sg_norm', 'grad_sg_w', 'grad_sg_b', 'grad_w_out', 'grad_norm2', 'grad_w_up', 'grad_w_down', 'grad_final_norm', 'delta_norm1', 'delta_w_in', 'delta_pool_w', 'delta_pool_scale', 'delta_sg_norm', 'delta_sg_w', 'delta_sg_b', 'delta_w_out', 'delta_norm2', 'delta_w_up', 'delta_w_down', 'delta_final_norm', 'new_m_norm1', 'new_m_w_in', 'new_m_pool_w', 'new_m_pool_scale', 'new_m_sg_norm', 'new_m_sg_w', 'new_m_sg_b', 'new_m_w_out', 'new_m_norm2', 'new_m_w_up', 'new_m_w_down', 'new_m_final_norm', 'new_v_norm1', 'new_v_w_in', 'new_v_pool_w', 'new_v_pool_scale', 'new_v_sg_norm', 'new_v_sg_w', 'new_v_sg_b', 'new_v_w_out', 'new_v_norm2', 'new_v_w_up', 'new_v_w_down', 'new_v_final_norm']
TWIN_LEAF_KINDS = {'loss': 'loss', 'grad_x': 'grad_x', 'grad_norm1': 'grad_w', 'grad_w_in': 'grad_w', 'grad_pool_w': 'grad_w', 'grad_pool_scale': 'grad_w', 'grad_sg_norm': 'grad_w', 'grad_sg_w': 'grad_w', 'grad_sg_b': 'grad_w', 'grad_w_out': 'grad_w', 'grad_norm2': 'grad_w', 'grad_w_up': 'grad_w', 'grad_w_down': 'grad_w', 'grad_final_norm': 'grad_w', 'delta_norm1': 'delta_w', 'delta_w_in': 'delta_w', 'delta_pool_w': 'delta_w', 'delta_pool_scale': 'delta_w', 'delta_sg_norm': 'delta_w', 'delta_sg_w': 'delta_w', 'delta_sg_b': 'delta_w', 'delta_w_out': 'delta_w', 'delta_norm2': 'delta_w', 'delta_w_up': 'delta_w', 'delta_w_down': 'delta_w', 'delta_final_norm': 'delta_w', 'new_m_norm1': 'new_m', 'new_m_w_in': 'new_m', 'new_m_pool_w': 'new_m', 'new_m_pool_scale': 'new_m', 'new_m_sg_norm': 'new_m', 'new_m_sg_w': 'new_m', 'new_m_sg_b': 'new_m', 'new_m_w_out': 'new_m', 'new_m_norm2': 'new_m', 'new_m_w_up': 'new_m', 'new_m_w_down': 'new_m', 'new_m_final_norm': 'new_m', 'new_v_norm1': 'new_v', 'new_v_w_in': 'new_v', 'new_v_pool_w': 'new_v', 'new_v_pool_scale': 'new_v', 'new_v_sg_norm': 'new_v', 'new_v_sg_w': 'new_v', 'new_v_sg_b': 'new_v', 'new_v_w_out': 'new_v', 'new_v_norm2': 'new_v', 'new_v_w_up': 'new_v', 'new_v_w_down': 'new_v', 'new_v_final_norm': 'new_v'}


def _forward(args):
    return _fwd_reference(*[args[k] for k in FWD_PARAMS])


def _output_shape():
    def fwd():
        inp = _fwd_setup_inputs(0)
        return _fwd_reference(*[inp[k] for k in FWD_PARAMS])
    out = _jax.eval_shape(fwd)
    return out.shape, out.dtype

N_MICROBATCH = 1
ADAM_LR = 0.001
ADAM_B1 = 0.9
ADAM_B2 = 0.999
ADAM_EPS = 1e-08
ADAM_WD = 0.01
ADAM_STEP = 10
PER_EXAMPLE_BATCH_AXIS = {'x': 0, 'loss_target': 0}
SHARED_INPUTS = []
_WEIGHT_DTYPES = {'norm1': _jnp.float32, 'w_in': _jnp.float32, 'pool_w': _jnp.float32, 'pool_scale': _jnp.float32, 'sg_norm': _jnp.float32, 'sg_w': _jnp.float32, 'sg_b': _jnp.float32, 'w_out': _jnp.float32, 'norm2': _jnp.float32, 'w_up': _jnp.float32, 'w_down': _jnp.float32, 'final_norm': _jnp.float32}
MOMENT_SCALE = {'norm1': 1.672939e-01, 'w_in': 1.050267e-01, 'pool_w': 1.689080e-01, 'pool_scale': 1.853340e-01, 'sg_norm': 4.442754e-02, 'sg_w': 6.379618e-02, 'sg_b': 9.500520e-02, 'w_out': 1.480489e-01, 'norm2': 1.239892e-01, 'w_up': 6.149406e-02, 'w_down': 2.451610e-01, 'final_norm': 6.440977e+01}


def _to_microbatches(a, axis):
    t = _jnp.moveaxis(a, axis, 0)
    t = t.reshape((N_MICROBATCH, t.shape[0] // N_MICROBATCH) + t.shape[1:])
    return _jnp.moveaxis(t, 1, axis + 1)


def setup_inputs(seed: int = 0) -> dict:
    inp = _fwd_setup_inputs(seed)
    key = _jax.random.fold_in(_jax.random.key(seed), 7919)
    shape, _ = _output_shape()
    out = dict(inp)
    out["loss_target"] = _jax.random.normal(_jax.random.fold_in(key, 0), shape, _jnp.float32)
    for i, name in enumerate(TWIN_WEIGHTS):
        w = inp[name].astype(_jnp.float32)
        if MOMENT_SCALE is None:
            s = _jnp.sqrt(_jnp.mean(_jnp.square(w)) + 1e-30)
        else:
            s = MOMENT_SCALE[name]
        km, kv = _jax.random.split(_jax.random.fold_in(key, i + 1))
        out[name] = w
        out["m_" + name] = s * _jax.random.normal(km, w.shape, _jnp.float32)
        out["v_" + name] = (s * s) * _jax.random.uniform(kv, w.shape, _jnp.float32, 0.5, 1.5)
    if N_MICROBATCH > 1:
        for name, axis in PER_EXAMPLE_BATCH_AXIS.items():
            out[name] = _to_microbatches(out[name], axis)
    return {'x': out['x'], 'norm1': out['norm1'], 'w_in': out['w_in'], 'pool_w': out['pool_w'], 'pool_scale': out['pool_scale'], 'sg_norm': out['sg_norm'], 'sg_w': out['sg_w'], 'sg_b': out['sg_b'], 'w_out': out['w_out'], 'norm2': out['norm2'], 'w_up': out['w_up'], 'w_down': out['w_down'], 'final_norm': out['final_norm'], 'loss_target': out['loss_target'], 'm_norm1': out['m_norm1'], 'm_w_in': out['m_w_in'], 'm_pool_w': out['m_pool_w'], 'm_pool_scale': out['m_pool_scale'], 'm_sg_norm': out['m_sg_norm'], 'm_sg_w': out['m_sg_w'], 'm_sg_b': out['m_sg_b'], 'm_w_out': out['m_w_out'], 'm_norm2': out['m_norm2'], 'm_w_up': out['m_w_up'], 'm_w_down': out['m_w_down'], 'm_final_norm': out['m_final_norm'], 'v_norm1': out['v_norm1'], 'v_w_in': out['v_w_in'], 'v_pool_w': out['v_pool_w'], 'v_pool_scale': out['v_pool_scale'], 'v_sg_norm': out['v_sg_norm'], 'v_sg_w': out['v_sg_w'], 'v_sg_b': out['v_sg_b'], 'v_w_out': out['v_w_out'], 'v_norm2': out['v_norm2'], 'v_w_up': out['v_w_up'], 'v_w_down': out['v_w_down'], 'v_final_norm': out['v_final_norm']}


def _loss(weights, diff, rest, loss_target):
    with _jax.named_scope("forward"):
        args = {**rest, TWIN_DIFF_INPUT: diff, **{k: w.astype(_WEIGHT_DTYPES[k]) for k, w in weights.items()}}
        y = _forward(args)
    with _jax.named_scope("loss_head"):
        err = _jnp.square(y.astype(_jnp.float32) - loss_target)
        return 0.5 * _jnp.sum(_jnp.mean(err, axis=-1)) if err.ndim else 0.5 * err


def _adamw(w, g, m, v):
    m = ADAM_B1 * m + (1.0 - ADAM_B1) * g
    v = ADAM_B2 * v + (1.0 - ADAM_B2) * _jnp.square(g)
    m_hat = m / (1.0 - ADAM_B1 ** ADAM_STEP)
    v_hat = v / (1.0 - ADAM_B2 ** ADAM_STEP)
    delta = -ADAM_LR * (m_hat / (_jnp.sqrt(v_hat) + ADAM_EPS) + ADAM_WD * w)
    return delta, m, v


def reference(x, norm1, w_in, pool_w, pool_scale, sg_norm, sg_w, sg_b, w_out, norm2, w_up, w_down, final_norm, loss_target, m_norm1, m_w_in, m_pool_w, m_pool_scale, m_sg_norm, m_sg_w, m_sg_b, m_w_out, m_norm2, m_w_up, m_w_down, m_final_norm, v_norm1, v_w_in, v_pool_w, v_pool_scale, v_sg_norm, v_sg_w, v_sg_b, v_w_out, v_norm2, v_w_up, v_w_down, v_final_norm):
    given = dict(x=x, norm1=norm1, w_in=w_in, pool_w=pool_w, pool_scale=pool_scale, sg_norm=sg_norm, sg_w=sg_w, sg_b=sg_b, w_out=w_out, norm2=norm2, w_up=w_up, w_down=w_down, final_norm=final_norm, loss_target=loss_target, m_norm1=m_norm1, m_w_in=m_w_in, m_pool_w=m_pool_w, m_pool_scale=m_pool_scale, m_sg_norm=m_sg_norm, m_sg_w=m_sg_w, m_sg_b=m_sg_b, m_w_out=m_w_out, m_norm2=m_norm2, m_w_up=m_w_up, m_w_down=m_w_down, m_final_norm=m_final_norm, v_norm1=v_norm1, v_w_in=v_w_in, v_pool_w=v_pool_w, v_pool_scale=v_pool_scale, v_sg_norm=v_sg_norm, v_sg_w=v_sg_w, v_sg_b=v_sg_b, v_w_out=v_w_out, v_norm2=v_norm2, v_w_up=v_w_up, v_w_down=v_w_down, v_final_norm=v_final_norm)
    weights = {n: given[n] for n in TWIN_WEIGHTS}
    shared = {n: given[n] for n in SHARED_INPUTS}
    per_example = {n: given[n] for n in ['x']}
    grad_fn = _jax.value_and_grad(_loss, argnums=(0, 1))

    def one_microbatch(ex, loss_target):
        ex = dict(ex)
        diff = ex.pop(TWIN_DIFF_INPUT)
        return grad_fn(weights, diff, {**shared, **ex}, loss_target)

    if N_MICROBATCH == 1:
        loss, (grad_w, grad_x) = one_microbatch(per_example, given["loss_target"])
    else:
        def body(carry, xs):
            loss_sum, grad_sum = carry
            l_k, (gw_k, gx_k) = one_microbatch(xs[0], xs[1])
            with _jax.named_scope("update"):
                return (loss_sum + l_k, _jax.tree.map(_jnp.add, grad_sum, gw_k)), gx_k

        init = (_jnp.zeros((), _jnp.float32), _jax.tree.map(_jnp.zeros_like, weights))
        (loss, grad_w), grad_x = _jax.lax.scan(body, init, (per_example, given["loss_target"]))
    with _jax.named_scope("update"):
        delta_w, new_m, new_v = {}, {}, {}
        for n in TWIN_WEIGHTS:
            delta_w[n], new_m[n], new_v[n] = _adamw(weights[n], grad_w[n], given["m_" + n], given["v_" + n])
    return (loss, grad_x, *[grad_w[n] for n in TWIN_WEIGHTS], *[delta_w[n] for n in TWIN_WEIGHTS],
            *[new_m[n] for n in TWIN_WEIGHTS], *[new_v[n] for n in TWIN_WEIGHTS])
```

```python
import functools

import jax
import jax.numpy as jnp
from jax import lax
from jax.experimental import pallas as pl
from jax.experimental.pallas import tpu as pltpu

F32 = jnp.float32
BF16 = jnp.bfloat16
MESH = pl.DeviceIdType.MESH
AXES = ("x", "y", "c")

EPS = 1e-6
D_MODEL = 1024
POOL_WIDTH = 256
SG_WIDTH = 256
SB_WIDTH = 512
POOL_WINDOWS = (2, 4, 8, 16)
POOL_GW = 64
POOL_HALO = 16
CHUNK = 128
SG_HEADS = 4
SB_HD = 64
IN_COLS = 2304
QKV_OFF = 768
D_FF = 4096
N_CHIPS = 4
LANES = 128
VMEM_LIMIT = 56 * 1024 * 1024
UNDERFLOW = -104.0

ADAM_LR = 0.001
ADAM_B1 = 0.9
ADAM_B2 = 0.999
ADAM_EPS = 1e-08
ADAM_WD = 0.01
ADAM_STEP = 10

HBM_SPEC = pl.BlockSpec(memory_space=pl.ANY)
VMEM_SPEC = pl.BlockSpec(memory_space=pltpu.VMEM)


def _params(**kw):
    return pltpu.CompilerParams(vmem_limit_bytes=VMEM_LIMIT, **kw)


def _tile(n, pref):
    if n <= pref:
        return n
    for t in range(pref - pref % LANES, 0, -LANES):
        if n % t == 0:
            return t
    raise ValueError((n, pref))


def _nn(a, b):
    return jnp.dot(a, b, preferred_element_type=F32)


def _nt(a, b):
    return lax.dot_general(a, b, (((1,), (1,)), ((), ())), preferred_element_type=F32)


def _tn(a, b):
    return lax.dot_general(a, b, (((0,), (0,)), ((), ())), preferred_element_type=F32)


def _rms_fwd(x, g):
    r = lax.rsqrt(jnp.mean(x * x, axis=-1, keepdims=True) + EPS)
    xhat = x * r
    return xhat * g, xhat, r


def _rms_bwd(dy, xhat, r, g):
    dxhat = dy * g
    dx = r * (dxhat - xhat * jnp.mean(dxhat * xhat, axis=-1, keepdims=True))
    return dx, dy * xhat


_GELU_K = 0.7978845608028654
_GELU_C = 0.044715


def _gelu(x):
    return 0.5 * x * (1.0 + jnp.tanh(_GELU_K * (x + _GELU_C * x * x * x)))


def _gelu_grad(x):
    t = jnp.tanh(_GELU_K * (x + _GELU_C * x * x * x))
    return 0.5 * (1.0 + t) + 0.5 * x * (1.0 - t * t) * _GELU_K * (1.0 + 3.0 * _GELU_C * x * x)


def _inproj_fwd(x, g, w):
    T, D = x.shape
    N = w.shape[1]
    tt = _tile(T, 512)

    def body(x_ref, g_ref, w_ref, proj_ref, h_ref, qkv_ref):
        h, _, _ = _rms_fwd(x_ref[...], g_ref[...])
        hb = h.astype(BF16)
        h_ref[...] = hb
        p = _nn(hb, w_ref[...])
        proj_ref[...] = p
        qkv_ref[...] = p[:, QKV_OFF:].astype(BF16)

    return pl.pallas_call(
        body, name="inproj_fwd", grid=(T // tt,),
        in_specs=[pl.BlockSpec((tt, D), lambda i: (i, 0)), pl.BlockSpec((1, D), lambda i: (0, 0)),
                  pl.BlockSpec((D, N), lambda i: (0, 0))],
        out_specs=[pl.BlockSpec((tt, N), lambda i: (i, 0)), pl.BlockSpec((tt, D), lambda i: (i, 0)),
                   pl.BlockSpec((tt, N - QKV_OFF), lambda i: (i, 0))],
        out_shape=[jax.ShapeDtypeStruct((T, N), F32), jax.ShapeDtypeStruct((T, D), BF16),
                   jax.ShapeDtypeStruct((T, N - QKV_OFF), BF16)],
        compiler_params=_params(),
    )(x, g, w)


def _inproj_bwd(pieces, w, x, g, dres):
    T, D = x.shape
    N = w.shape[1]
    tt = _tile(T, 512)
    widths = [p.shape[1] for p in pieces]
    offs = [sum(widths[:k]) for k in range(len(widths))]
    assert sum(widths) == N
    n_p = len(pieces)

    def body(*refs):
        p_refs = refs[:n_p]
        w_ref, x_ref, g_ref, dres_ref, dproj_ref, dx_ref, dg_ref = refs[n_p:]
        for p_ref, o, wd in zip(p_refs, offs, widths):
            dproj_ref[:, o:o + wd] = p_ref[...].astype(BF16)
        dh = _nt(dproj_ref[...], w_ref[...])
        gv = g_ref[...]
        _, xhat, r = _rms_fwd(x_ref[...], gv)
        dx, dgrow = _rms_bwd(dh, xhat, r, gv)
        dx_ref[...] = dres_ref[...] + dx

        @pl.when(pl.program_id(0) == 0)
        def _():
            dg_ref[...] = jnp.zeros_like(dg_ref)

        dg_ref[...] += jnp.sum(dgrow, axis=0, keepdims=True)

    return pl.pallas_call(
        body, name="inproj_bwd", grid=(T // tt,),
        in_specs=[pl.BlockSpec((tt, wd), lambda i: (i, 0)) for wd in widths] + [
            pl.BlockSpec((D, N), lambda i: (0, 0)), pl.BlockSpec((tt, D), lambda i: (i, 0)),
            pl.BlockSpec((1, D), lambda i: (0, 0)), pl.BlockSpec((tt, D), lambda i: (i, 0))],
        out_specs=[pl.BlockSpec((tt, N), lambda i: (i, 0)), pl.BlockSpec((tt, D), lambda i: (i, 0)),
                   pl.BlockSpec((1, D), lambda i: (0, 0))],
        out_shape=[jax.ShapeDtypeStruct((T, N), BF16), jax.ShapeDtypeStruct((T, D), F32),
                   jax.ShapeDtypeStruct((1, D), F32)],
        compiler_params=_params(),
    )(*pieces, w, x, g, dres)


def _pool_select(s2, s4, s8, s16, grp):
    return jnp.where(grp == 0, s2, jnp.where(grp == 1, s4, jnp.where(grp == 2, s8, s16)))


def _pool_count(t_glob, grp):
    win = jnp.where(grp == 0, 2, jnp.where(grp == 1, 4, jnp.where(grp == 2, 8, 16)))
    return jnp.minimum(t_glob + 1, win).astype(F32)


def _pool_diff(a, halo, base, tt):
    n = tt + POOL_HALO
    ext = jnp.concatenate([halo, a], axis=0)
    s2 = ext + pltpu.roll(ext, 1, 0)
    s4 = s2 + pltpu.roll(s2, 2, 0)
    s8 = s4 + pltpu.roll(s4, 4, 0)
    s16 = s8 + pltpu.roll(s8, 8, 0)
    grp = lax.broadcasted_iota(jnp.int32, (n, POOL_WIDTH), 1) // POOL_GW
    t_glob = lax.broadcasted_iota(jnp.int32, (n, POOL_WIDTH), 0) + (base - POOL_HALO)
    pooled = _pool_select(s2, s4, s8, s16, grp) / _pool_count(t_glob, grp)
    return pooled[POOL_HALO:] - a


def _pool_specs(T, tt):
    hb = tt // POOL_HALO
    return [pl.BlockSpec((tt, POOL_WIDTH), lambda i: (i, 0)),
            pl.BlockSpec((POOL_HALO, POOL_WIDTH), lambda i: (jnp.maximum(i * hb - 1, 0), 0))]


def _pool_fwd(proj, wbd, scale):
    T = proj.shape[0]
    tt = _tile(T, 512)

    def body(a_ref, halo_ref, w_ref, sc_ref, y_ref):
        i = pl.program_id(0)
        halo = jnp.where(i > 0, halo_ref[...], 0.0)
        d = _pool_diff(a_ref[...], halo, i * tt, tt)
        y_ref[...] = _nn(d.astype(BF16), w_ref[...]) * sc_ref[...]

    return pl.pallas_call(
        body, name="pool_fwd", grid=(T // tt,),
        in_specs=_pool_specs(T, tt) + [pl.BlockSpec((POOL_WIDTH, POOL_WIDTH), lambda i: (0, 0)),
                                       pl.BlockSpec((1, POOL_WIDTH), lambda i: (0, 0))],
        out_specs=pl.BlockSpec((tt, POOL_WIDTH), lambda i: (i, 0)),
        out_shape=jax.ShapeDtypeStruct((T, POOL_WIDTH), F32),
        compiler_params=_params(),
    )(proj, proj, wbd, scale)


def _pool_bwd(proj, dymix, wbd, scale):
    T = proj.shape[0]
    tt = _tile(T, 512)
    hb = tt // POOL_HALO
    nblk = T // tt
    n = tt + POOL_HALO

    def body(a_ref, halo_ref, dy_ref, dyn_ref, w_ref, sc_ref, da_ref, dw_ref, dsc_ref):
        i = pl.program_id(0)
        halo = jnp.where(i > 0, halo_ref[...], 0.0)
        d = _pool_diff(a_ref[...], halo, i * tt, tt)
        db = d.astype(BF16)
        wv = w_ref[...]
        sc = sc_ref[...]
        dy = dy_ref[...]
        dys = dy * sc

        @pl.when(i == 0)
        def _():
            dw_ref[...] = jnp.zeros_like(dw_ref)
            dsc_ref[...] = jnp.zeros_like(dsc_ref)

        dsc_ref[...] += jnp.sum(dy * _nn(db, wv), axis=0, keepdims=True)
        dw_ref[...] += _tn(db, dys.astype(BF16))
        dyn = jnp.where(i < nblk - 1, dyn_ref[...], 0.0) * sc
        dd = _nt(jnp.concatenate([dys, dyn], axis=0).astype(BF16), wv)
        grp = lax.broadcasted_iota(jnp.int32, (n, POOL_WIDTH), 1) // POOL_GW
        t_glob = lax.broadcasted_iota(jnp.int32, (n, POOL_WIDTH), 0) + i * tt
        e = dd / _pool_count(t_glob, grp)
        r2 = e + pltpu.roll(e, n - 1, 0)
        r4 = r2 + pltpu.roll(r2, n - 2, 0)
        r8 = r4 + pltpu.roll(r4, n - 4, 0)
        r16 = r8 + pltpu.roll(r8, n - 8, 0)
        da_ref[...] = (_pool_select(r2, r4, r8, r16, grp) - dd)[:tt]

    return pl.pallas_call(
        body, name="pool_bwd", grid=(nblk,),
        in_specs=_pool_specs(T, tt) + [
            pl.BlockSpec((tt, POOL_WIDTH), lambda i: (i, 0)),
            pl.BlockSpec((POOL_HALO, POOL_WIDTH), lambda i: (jnp.minimum((i + 1) * hb, T // POOL_HALO - 1), 0)),
            pl.BlockSpec((POOL_WIDTH, POOL_WIDTH), lambda i: (0, 0)), pl.BlockSpec((1, POOL_WIDTH), lambda i: (0, 0))],
        out_specs=[pl.BlockSpec((tt, POOL_WIDTH), lambda i: (i, 0)),
                   pl.BlockSpec((POOL_WIDTH, POOL_WIDTH), lambda i: (0, 0)),
                   pl.BlockSpec((1, POOL_WIDTH), lambda i: (0, 0))],
        out_shape=[jax.ShapeDtypeStruct((T, POOL_WIDTH), F32),
                   jax.ShapeDtypeStruct((POOL_WIDTH, POOL_WIDTH), F32),
                   jax.ShapeDtypeStruct((1, POOL_WIDTH), F32)],
        compiler_params=_params(),
    )(proj, proj, dymix, dymix, wbd, scale)


def _head_select(stacked, grp):
    out = jnp.where(grp == 0, stacked[0:CHUNK], 0.0)
    for h in range(1, SG_HEADS):
        out = out + jnp.where(grp == h, stacked[h * CHUNK:(h + 1) * CHUNK], 0.0)
    return out


def _sg_specs(tt):
    return [pl.BlockSpec((tt, SG_WIDTH), lambda i: (i, 1)), pl.BlockSpec((tt, SG_WIDTH), lambda i: (i, 2))]


def _sg_fwd(proj, wm, bias, g):
    T = proj.shape[0]
    tt = _tile(T, 512)

    def body(u_ref, v_ref, wm_ref, b_ref, g_ref, y_ref):
        zu = _gelu(u_ref[...])
        vn, _, _ = _rms_fwd(_gelu(v_ref[...]), g_ref[...])
        grp = lax.broadcasted_iota(jnp.int32, (CHUNK, SG_WIDTH), 1) // SB_HD
        for n in range(tt // CHUNK):
            rows = slice(n * CHUNK, (n + 1) * CHUNK)
            sv = _head_select(_nn(wm_ref[...], vn[rows].astype(BF16)), grp) + b_ref[...]
            y_ref[rows, :] = zu[rows] * sv

    return pl.pallas_call(
        body, name="sg_fwd", grid=(T // tt,),
        in_specs=_sg_specs(tt) + [pl.BlockSpec((SG_HEADS * CHUNK, CHUNK), lambda i: (0, 0)),
                                  pl.BlockSpec((CHUNK, SG_WIDTH), lambda i: (0, 0)),
                                  pl.BlockSpec((1, SG_WIDTH), lambda i: (0, 0))],
        out_specs=pl.BlockSpec((tt, SG_WIDTH), lambda i: (i, 0)),
        out_shape=jax.ShapeDtypeStruct((T, SG_WIDTH), F32),
        compiler_params=_params(),
    )(proj, proj, wm, bias, g)


def _sg_bwd(proj, dymix, wm, wmt, bias, g):
    T = proj.shape[0]
    tt = _tile(T, 512)
    nblk = T // tt

    def body(u_ref, v_ref, dy_ref, wm_ref, wmt_ref, b_ref, g_ref,
             du_ref, dv_ref, dw_ref, db_ref, dg_ref, dvn_ref, dbias_ref):
        i = pl.program_id(0)
        up, vp = u_ref[...], v_ref[...]
        gv = g_ref[...]
        zu, zv = _gelu(up), _gelu(vp)
        vn, xhat, r = _rms_fwd(zv, gv)
        gu = _gelu_grad(up)
        grp = lax.broadcasted_iota(jnp.int32, (CHUNK, SG_WIDTH), 1) // SB_HD

        @pl.when(i == 0)
        def _():
            dw_ref[...] = jnp.zeros_like(dw_ref)
            dbias_ref[...] = jnp.zeros_like(dbias_ref)
            dg_ref[...] = jnp.zeros_like(dg_ref)

        for n in range(tt // CHUNK):
            rows = slice(n * CHUNK, (n + 1) * CHUNK)
            vc = vn[rows].astype(BF16)
            sv = _head_select(_nn(wm_ref[...], vc), grp) + b_ref[...]
            dy = dy_ref[rows, :]
            du_ref[rows, :] = dy * sv * gu[rows]
            dsv = dy * zu[rows]
            dsvb = dsv.astype(BF16)
            dvn_ref[rows, :] = _head_select(_nn(wmt_ref[...], dsvb), grp)
            stacked = jnp.concatenate([jnp.where(grp == h, dsv, 0.0) for h in range(SG_HEADS)], axis=0)
            dw_ref[...] += _nt(stacked.astype(BF16), vc)
            dbias_ref[...] += dsv

        dzv, dgrow = _rms_bwd(dvn_ref[...], xhat, r, gv)
        dg_ref[...] += jnp.sum(dgrow, axis=0, keepdims=True)
        dv_ref[...] = dzv * _gelu_grad(vp)

        @pl.when(i == nblk - 1)
        def _():
            t_i = lax.broadcasted_iota(jnp.int32, (SG_HEADS * CHUNK, CHUNK), 0) % CHUNK
            s_i = lax.broadcasted_iota(jnp.int32, (SG_HEADS * CHUNK, CHUNK), 1)
            dw_ref[...] = jnp.where(s_i <= t_i, dw_ref[...], 0.0)
            lane = lax.broadcasted_iota(jnp.int32, (CHUNK, LANES), 1)
            acc = jnp.zeros((CHUNK, LANES), F32)
            for h in range(SG_HEADS):
                tot = jnp.sum(jnp.where(grp == h, dbias_ref[...], 0.0), axis=1, keepdims=True)
                acc = acc + jnp.where(lane == h, tot, 0.0)
            db_ref[...] = acc

    return pl.pallas_call(
        body, name="sg_bwd", grid=(nblk,),
        in_specs=_sg_specs(tt) + [pl.BlockSpec((tt, SG_WIDTH), lambda i: (i, 1)),
                                  pl.BlockSpec((SG_HEADS * CHUNK, CHUNK), lambda i: (0, 0)),
                                  pl.BlockSpec((SG_HEADS * CHUNK, CHUNK), lambda i: (0, 0)),
                                  pl.BlockSpec((CHUNK, SG_WIDTH), lambda i: (0, 0)),
                                  pl.BlockSpec((1, SG_WIDTH), lambda i: (0, 0))],
        out_specs=[pl.BlockSpec((tt, SG_WIDTH), lambda i: (i, 0)), pl.BlockSpec((tt, SG_WIDTH), lambda i: (i, 0)),
                   pl.BlockSpec((SG_HEADS * CHUNK, CHUNK), lambda i: (0, 0)),
                   pl.BlockSpec((CHUNK, LANES), lambda i: (0, 0)), pl.BlockSpec((1, SG_WIDTH), lambda i: (0, 0))],
        out_shape=[jax.ShapeDtypeStruct((T, SG_WIDTH), F32), jax.ShapeDtypeStruct((T, SG_WIDTH), F32),
                   jax.ShapeDtypeStruct((SG_HEADS * CHUNK, CHUNK), F32),
                   jax.ShapeDtypeStruct((CHUNK, LANES), F32), jax.ShapeDtypeStruct((1, SG_WIDTH), F32)],
        scratch_shapes=[pltpu.VMEM((tt, SG_WIDTH), F32), pltpu.VMEM((CHUNK, SG_WIDTH), F32)],
        compiler_params=_params(),
    )(proj, proj, dymix, wm, wmt, bias, g)


def _split_dot(x, u):
    hi = x.astype(BF16)
    lo = (x - hi.astype(F32)).astype(BF16)
    return _nn(hi, u) + _nn(lo, u)


def _sb_logits(qh, kb):
    z = _nt(qh, kb) * (1.0 / 8.0)
    sp = jnp.log1p(jnp.exp(-jnp.abs(z)))
    return jnp.minimum(z, 0.0) - sp, -jnp.maximum(z, 0.0) - sp


def _attn_qkv_specs(tq, T):
    base = (IN_COLS - 3 * SB_WIDTH - QKV_OFF) // LANES
    nb = SB_WIDTH // LANES
    return [pl.BlockSpec((tq, LANES), lambda p, i: (i, base + p)),
            pl.BlockSpec((T, LANES), lambda p, i: (0, base + nb + p)),
            pl.BlockSpec((T, LANES), lambda p, i: (0, base + 2 * nb + p))]


def _attn_fwd(qkv):
    T = qkv.shape[0]
    tq = _tile(T, 256)

    def body(q_ref, k_ref, v_ref, o_ref):
        i = pl.program_id(1)
        lane = lax.broadcasted_iota(jnp.int32, (tq, LANES), 1)
        row = lax.broadcasted_iota(jnp.int32, (tq, tq), 0)
        col = lax.broadcasted_iota(jnp.int32, (tq, tq), 1)
        after = jnp.where(row > col, 1.0, 0.0).astype(BF16)
        valid = col < row
        q = q_ref[...].astype(F32)
        outs = []
        for hh in range(2):
            qh = jnp.where((lane // SB_HD) == hh, q, 0.0).astype(BF16)

            def tile(j, carry, acc, diag, qh=qh):
                ks = pl.ds(pl.multiple_of(j * tq, tq), tq)
                lb, lm = _sb_logits(qh, k_ref[ks, :])
                if diag:
                    lm = jnp.where(valid, lm, 0.0)
                a = jnp.exp(lb + _split_dot(lm, after) + carry)
                if diag:
                    a = jnp.where(valid, a, 0.0)
                acc = acc + _nn(a.astype(BF16), v_ref[ks, :])
                return carry + jnp.sum(lm, axis=1, keepdims=True), acc

            carry, acc = tile(i, jnp.zeros((tq, 1), F32), jnp.zeros((tq, LANES), F32), True)

            def cond(st):
                return jnp.logical_and(st[0] >= 0, st[3] > UNDERFLOW)

            def step(st, tile=tile):
                j, carry, acc, _ = st
                carry, acc = tile(j, carry, acc, False)
                return j - 1, carry, acc, jnp.max(carry)

            _, _, acc, _ = lax.while_loop(cond, step, (i - 1, carry, acc, jnp.max(carry)))
            outs.append(acc)
        o_ref[...] = jnp.where(lane < SB_HD, outs[0], outs[1])

    return pl.pallas_call(
        body, name="attn_fwd", grid=(SB_WIDTH // LANES, T // tq),
        in_specs=_attn_qkv_specs(tq, T),
        out_specs=pl.BlockSpec((tq, LANES), lambda p, i: (i, p)),
        out_shape=jax.ShapeDtypeStruct((T, SB_WIDTH), F32),
        compiler_params=_params(),
    )(qkv, qkv, qkv)


def _attn_bwd(qkv, o, dymix):
    T = qkv.shape[0]
    tq = _tile(T, 256)
    nq = T // tq
    yc_blk = (POOL_WIDTH + SG_WIDTH) // LANES

    def body(q_ref, k_ref, v_ref, o_ref, do_ref, dq_ref, dk_ref, dv_ref):
        i = pl.program_id(1)
        lane = lax.broadcasted_iota(jnp.int32, (tq, LANES), 1)
        row = lax.broadcasted_iota(jnp.int32, (tq, tq), 0)
        col = lax.broadcasted_iota(jnp.int32, (tq, tq), 1)
        after = jnp.where(row > col, 1.0, 0.0).astype(BF16)
        from_here = jnp.where(row >= col, 1.0, 0.0).astype(BF16)
        valid = col < row

        @pl.when(i == 0)
        def _():
            dk_ref[...] = jnp.zeros_like(dk_ref)
            dv_ref[...] = jnp.zeros_like(dv_ref)

        q = q_ref[...].astype(F32)
        ov = o_ref[...]
        dov = do_ref[...]
        outs = []
        for hh in range(2):
            head = (lane // SB_HD) == hh
            qh = jnp.where(head, q, 0.0).astype(BF16)
            dohb = jnp.where(head, dov, 0.0).astype(BF16)
            delta = jnp.sum(dohb.astype(F32) * ov, axis=1, keepdims=True)

            def tile(j, c_a, c_r, dqa, diag, qh=qh, dohb=dohb, delta=delta):
                ks = pl.ds(pl.multiple_of(j * tq, tq), tq)
                kb = k_ref[ks, :]
                lb, lm = _sb_logits(qh, kb)
                if diag:
                    lm = jnp.where(valid, lm, 0.0)
                a = jnp.exp(lb + _split_dot(lm, after) + c_a)
                if diag:
                    a = jnp.where(valid, a, 0.0)
                ab = a.astype(BF16)
                sig = jnp.exp(lb)
                g = _nt(dohb, v_ref[ks, :]) * ab.astype(F32)
                left = delta - (c_r + _split_dot(g, from_here))
                dz = (g * (1.0 - sig) - left * sig) * (1.0 / 8.0)
                if diag:
                    dz = jnp.where(valid, dz, 0.0)
                dzb = dz.astype(BF16)
                dk_ref[ks, :] += _tn(dzb, qh)
                dv_ref[ks, :] += _tn(ab, dohb)
                return (c_a + jnp.sum(lm, axis=1, keepdims=True), c_r + jnp.sum(g, axis=1, keepdims=True),
                        dqa + _nn(dzb, kb))

            zero = jnp.zeros((tq, 1), F32)
            c_a, c_r, dqa = tile(i, zero, zero, jnp.zeros((tq, LANES), F32), True)

            def cond(st):
                return jnp.logical_and(st[0] >= 0, st[4] > UNDERFLOW)

            def step(st, tile=tile):
                j, c_a, c_r, dqa, _ = st
                c_a, c_r, dqa = tile(j, c_a, c_r, dqa, False)
                return j - 1, c_a, c_r, dqa, jnp.max(c_a)

            _, _, _, dqa, _ = lax.while_loop(cond, step, (i - 1, c_a, c_r, dqa, jnp.max(c_a)))
            outs.append(dqa)
        dq_ref[...] = jnp.where(lane < SB_HD, outs[0], outs[1])

    return pl.pallas_call(
        body, name="attn_bwd", grid=(SB_WIDTH // LANES, nq),
        in_specs=_attn_qkv_specs(tq, T) + [pl.BlockSpec((tq, LANES), lambda p, i: (i, p)),
                                           pl.BlockSpec((tq, LANES), lambda p, i: (i, yc_blk + p))],
        out_specs=[pl.BlockSpec((tq, LANES), lambda p, i: (i, p)), pl.BlockSpec((T, LANES), lambda p, i: (0, p)),
                   pl.BlockSpec((T, LANES), lambda p, i: (0, p))],
        out_shape=[jax.ShapeDtypeStruct((T, SB_WIDTH), F32)] * 3,
        compiler_params=_params(),
    )(qkv, qkv, qkv, o, dymix)


def _outproj_fwd(x, ya, yb, yc, w):
    T, D = x.shape
    tt = _tile(T, 512)

    def body(x_ref, ya_ref, yb_ref, yc_ref, w_ref, x1_ref, ymix_ref):
        ymix_ref[:, 0:POOL_WIDTH] = ya_ref[...].astype(BF16)
        ymix_ref[:, POOL_WIDTH:POOL_WIDTH + SG_WIDTH] = yb_ref[...].astype(BF16)
        ymix_ref[:, POOL_WIDTH + SG_WIDTH:] = yc_ref[...].astype(BF16)
        x1_ref[...] = x_ref[...] + _nn(ymix_ref[...], w_ref[...])

    row = lambda width: pl.BlockSpec((tt, width), lambda i: (i, 0))
    return pl.pallas_call(
        body, name="outproj_fwd", grid=(T // tt,),
        in_specs=[row(D), row(POOL_WIDTH), row(SG_WIDTH), row(SB_WIDTH), pl.BlockSpec((D, D), lambda i: (0, 0))],
        out_specs=[row(D), row(D)],
        out_shape=[jax.ShapeDtypeStruct((T, D), F32), jax.ShapeDtypeStruct((T, D), BF16)],
        compiler_params=_params(),
    )(x, ya, yb, yc, w)


def _nt_matmul(a, w):
    T, N = a.shape
    K = w.shape[0]
    tt = _tile(T, 512)

    def body(a_ref, w_ref, o_ref):
        o_ref[...] = _nt(a_ref[...].astype(BF16), w_ref[...])

    return pl.pallas_call(
        body, name="nt_matmul", grid=(T // tt,),
        in_specs=[pl.BlockSpec((tt, N), lambda i: (i, 0)), pl.BlockSpec((K, N), lambda i: (0, 0))],
        out_specs=pl.BlockSpec((tt, K), lambda i: (i, 0)),
        out_shape=jax.ShapeDtypeStruct((T, K), F32),
        compiler_params=_params(),
    )(a, w)


def _tn_matmul(a, b, name, n_split=1):
    T, K = a.shape
    N = b.shape[1]
    tk = _tile(K, 1024)
    tn = _tile(N // n_split, 1024)
    tt = _tile(T, 512)
    nper = N // n_split // tn
    nt = T // tt

    def body(a_ref, b_ref, o_ref):
        @pl.when(pl.program_id(2) == 0)
        def _():
            o_ref[...] = jnp.zeros_like(o_ref)

        o_ref[...] += _tn(a_ref[...], b_ref[...].astype(BF16))

    return pl.pallas_call(
        body, name=name, grid=(K // tk, N // tn, nt),
        in_specs=[pl.BlockSpec((tt, tk), lambda k, n, t: (t, k)), pl.BlockSpec((tt, tn), lambda k, n, t: (t, n))],
        out_specs=pl.BlockSpec((None, tk, tn), lambda k, n, t: (n // nper, k, n % nper)),
        out_shape=jax.ShapeDtypeStruct((n_split, K, N // n_split), F32),
        compiler_params=_params(),
    )(a, b)


def _mlp_fwd(x, g, w_up, w_down):
    T, D = x.shape
    F = w_up.shape[1]
    tt = _tile(T, 1024)
    fc = _tile(F, 512)
    nc = F // fc

    def body(x_ref, g_ref, wu_ref, wd_ref, y_ref, h_ref, u_ref, a_ref):
        c = pl.program_id(1)

        @pl.when(c == 0)
        def _():
            xv = x_ref[...]
            h, _, _ = _rms_fwd(xv, g_ref[...])
            h_ref[...] = h.astype(BF16)
            y_ref[...] = xv

        u = _nn(h_ref[...], wu_ref[...])
        u_ref[...] = u.astype(BF16)
        a = jnp.square(jnp.maximum(u, 0.0)).astype(BF16)
        a_ref[...] = a
        y_ref[...] += _nn(a, wd_ref[...])

    return pl.pallas_call(
        body, name="mlp_fwd", grid=(T // tt, nc),
        in_specs=[pl.BlockSpec((tt, D), lambda i, c: (i, 0)), pl.BlockSpec((1, D), lambda i, c: (0, 0)),
                  pl.BlockSpec((D, fc), lambda i, c: (0, c)), pl.BlockSpec((fc, D), lambda i, c: (c, 0))],
        out_specs=[pl.BlockSpec((tt, D), lambda i, c: (i, 0)), pl.BlockSpec((tt, D), lambda i, c: (i, 0)),
                   pl.BlockSpec((tt, fc), lambda i, c: (i, c)), pl.BlockSpec((tt, fc), lambda i, c: (i, c))],
        out_shape=[jax.ShapeDtypeStruct((T, D), F32), jax.ShapeDtypeStruct((T, D), BF16),
                   jax.ShapeDtypeStruct((T, F), BF16), jax.ShapeDtypeStruct((T, F), BF16)],
        compiler_params=_params(),
    )(x, g, w_up, w_down)


def _mlp_bwd(dy, x, g, u, w_up, w_down):
    T, D = x.shape
    F = w_up.shape[1]
    tt = _tile(T, 1024)
    fc = _tile(F, 512)
    nc = F // fc

    def body(dy_ref, x_ref, g_ref, u_ref, wu_ref, wd_ref, dx_ref, du_ref, dg_ref, dyb_ref, dh_ref):
        i, c = pl.program_id(0), pl.program_id(1)

        @pl.when(c == 0)
        def _():
            dyb_ref[...] = dy_ref[...].astype(BF16)
            dh_ref[...] = jnp.zeros_like(dh_ref)

        @pl.when(jnp.logical_and(i == 0, c == 0))
        def _():
            dg_ref[...] = jnp.zeros_like(dg_ref)

        da = _nt(dyb_ref[...], wd_ref[...])
        du = (da * (2.0 * jnp.maximum(u_ref[...].astype(F32), 0.0))).astype(BF16)
        du_ref[...] = du
        dh_ref[...] += _nt(du, wu_ref[...])

        @pl.when(c == nc - 1)
        def _():
            gv = g_ref[...]
            _, xhat, r = _rms_fwd(x_ref[...], gv)
            dx, dgrow = _rms_bwd(dh_ref[...], xhat, r, gv)
            dx_ref[...] = dy_ref[...] + dx
            dg_ref[...] += jnp.sum(dgrow, axis=0, keepdims=True)

    return pl.pallas_call(
        body, name="mlp_bwd", grid=(T // tt, nc),
        in_specs=[pl.BlockSpec((tt, D), lambda i, c: (i, 0)), pl.BlockSpec((tt, D), lambda i, c: (i, 0)),
                  pl.BlockSpec((1, D), lambda i, c: (0, 0)), pl.BlockSpec((tt, fc), lambda i, c: (i, c)),
                  pl.BlockSpec((D, fc), lambda i, c: (0, c)), pl.BlockSpec((fc, D), lambda i, c: (c, 0))],
        out_specs=[pl.BlockSpec((tt, D), lambda i, c: (i, 0)), pl.BlockSpec((tt, fc), lambda i, c: (i, c)),
                   pl.BlockSpec((1, D), lambda i, c: (0, 0))],
        out_shape=[jax.ShapeDtypeStruct((T, D), F32), jax.ShapeDtypeStruct((T, F), BF16),
                   jax.ShapeDtypeStruct((1, D), F32)],
        scratch_shapes=[pltpu.VMEM((tt, D), BF16), pltpu.VMEM((tt, D), F32)],
        compiler_params=_params(),
    )(dy, x, g, u, w_up, w_down)


def _loss_head(x, g, target):
    T, D = x.shape
    tt = _tile(T, 512)

    def body(x_ref, g_ref, t_ref, loss_ref, dx_ref, dg_ref):
        gv = g_ref[...]
        y, xhat, r = _rms_fwd(x_ref[...], gv)
        err = y - t_ref[...]
        dx, dgrow = _rms_bwd(err * (1.0 / D), xhat, r, gv)
        dx_ref[...] = dx

        @pl.when(pl.program_id(0) == 0)
        def _():
            loss_ref[...] = jnp.zeros_like(loss_ref)
            dg_ref[...] = jnp.zeros_like(dg_ref)

        loss_ref[...] += 0.5 * jnp.sum(jnp.mean(err * err, axis=-1, keepdims=True), axis=0, keepdims=True)
        dg_ref[...] += jnp.sum(dgrow, axis=0, keepdims=True)

    return pl.pallas_call(
        body, name="loss_head", grid=(T // tt,),
        in_specs=[pl.BlockSpec((tt, D), lambda i: (i, 0)), pl.BlockSpec((1, D), lambda i: (0, 0)),
                  pl.BlockSpec((tt, D), lambda i: (i, 0))],
        out_specs=[pl.BlockSpec((1, LANES), lambda i: (0, 0)), pl.BlockSpec((tt, D), lambda i: (i, 0)),
                   pl.BlockSpec((1, D), lambda i: (0, 0))],
        out_shape=[jax.ShapeDtypeStruct((1, LANES), F32), jax.ShapeDtypeStruct((T, D), F32),
                   jax.ShapeDtypeStruct((1, D), F32)],
        compiler_params=_params(),
    )(x, g, target)


def _rows(shape, pref=512):
    last = shape[-1]
    rows = 1
    for s in shape[:-1]:
        rows *= s
    tr = rows
    for cand in (pref, 256, 128, 64, 32, 16, 8):
        if rows % cand == 0:
            tr = cand
            break
    return rows, last, tr


def _elementwise(fn, name, ins, n_out, out_dtype=F32):
    shape = ins[0].shape
    rows, last, tr = _rows(shape)
    flat = [a.reshape(rows, last) for a in ins]
    n_in = len(ins)

    def body(*refs):
        res = fn(*[r[...] for r in refs[:n_in]])
        if n_out == 1:
            res = (res,)
        for r, v in zip(refs[n_in:], res):
            r[...] = v.astype(r.dtype)

    spec = pl.BlockSpec((tr, last), lambda i: (i, 0))
    outs = pl.pallas_call(
        body, name=name, grid=(rows // tr,),
        in_specs=[spec] * n_in, out_specs=[spec] * n_out,
        out_shape=[jax.ShapeDtypeStruct((rows, last), out_dtype)] * n_out,
        compiler_params=_params(),
    )(*flat)
    return [o.reshape(shape) for o in outs]


def _adamw(w, g, m, v):
    m = ADAM_B1 * m + (1.0 - ADAM_B1) * g
    v = ADAM_B2 * v + (1.0 - ADAM_B2) * jnp.square(g)
    m_hat = m / (1.0 - ADAM_B1 ** ADAM_STEP)
    v_hat = v / (1.0 - ADAM_B2 ** ADAM_STEP)
    delta = -ADAM_LR * (m_hat / (jnp.sqrt(v_hat) + ADAM_EPS) + ADAM_WD * w)
    return delta, m, v


def _place():
    x, y, c = lax.axis_index("x"), lax.axis_index("y"), lax.axis_index("c")
    chips = [(1 - x, y), (x, 1 - y), (1 - x, 1 - y)]
    return x, y, c, chips


def _remote(src, dst, ssem, rsem, k, dev):
    return pltpu.make_async_remote_copy(src_ref=src, dst_ref=dst, send_sem=ssem.at[k], recv_sem=rsem.at[k],
                                        device_id=dev, device_id_type=MESH)


def _gather_weights(shards):
    n = len(shards)
    halves = [s.shape[1] // 2 for s in shards]

    def body(*refs):
        src, out = refs[:n], refs[n:2 * n]
        ssem, rsem, lsem = refs[2 * n:]
        x, y, c, chips = _place()
        me_q = 2 * x + y
        sib = (x, y, 1 - c)

        def half(a, q, cc):
            return out[a].at[q, :, pl.ds(cc * halves[a], halves[a]), :]

        local = [pltpu.make_async_copy(src[a], out[a].at[me_q], lsem.at[a]) for a in range(n)]
        for cp in local:
            cp.start()
        first = []
        for a in range(n):
            mine = src[a].at[:, pl.ds(c * halves[a], halves[a]), :]
            for r, chip in enumerate(chips):
                first.append(_remote(mine, half(a, me_q, c), ssem, rsem, a * 3 + r, (*chip, c)))
        for cp in first:
            cp.start()
        passed = []
        for a in range(n):
            for r, chip in enumerate(chips):
                q = 2 * chip[0] + chip[1]
                k = a * 3 + r
                _remote(half(a, q, c), half(a, q, c), ssem, rsem, k, (*chip, c)).wait_recv()
                cp = _remote(half(a, q, c), half(a, q, c), ssem, rsem, 3 * n + k, sib)
                cp.start()
                passed.append(cp)
        for a in range(n):
            for r, chip in enumerate(chips):
                q = 2 * chip[0] + chip[1]
                _remote(half(a, q, 1 - c), half(a, q, 1 - c), ssem, rsem, 3 * n + a * 3 + r, sib).wait_recv()
        for cp in first + passed:
            cp.wait_send()
        for cp in local:
            cp.wait()

    return pl.pallas_call(
        body, name="gather_weights",
        in_specs=[HBM_SPEC] * n, out_specs=[HBM_SPEC] * n,
        out_shape=[jax.ShapeDtypeStruct((N_CHIPS,) + s.shape, s.dtype) for s in shards],
        scratch_shapes=[pltpu.SemaphoreType.DMA((6 * n,)), pltpu.SemaphoreType.DMA((6 * n,)),
                        pltpu.SemaphoreType.DMA((n,))],
        compiler_params=_params(has_side_effects=True),
    )(*shards)


def _swap_halves(grads):
    n = len(grads)
    halves = [g.shape[1] // 2 for g in grads]

    def body(*refs):
        src, out = refs[:n], refs[n:2 * n]
        ssem, rsem = refs[2 * n:]
        x, y, c, _ = _place()
        cps = [_remote(src[a].at[:, pl.ds((1 - c) * halves[a], halves[a]), :], out[a], ssem, rsem, a, (x, y, 1 - c))
               for a in range(n)]
        for cp in cps:
            cp.start()
        for cp in cps:
            cp.wait()

    return pl.pallas_call(
        body, name="swap_halves",
        in_specs=[HBM_SPEC] * n, out_specs=[HBM_SPEC] * n,
        out_shape=[jax.ShapeDtypeStruct((N_CHIPS, h, g.shape[2]), F32) for g, h in zip(grads, halves)],
        scratch_shapes=[pltpu.SemaphoreType.DMA((n,)), pltpu.SemaphoreType.DMA((n,))],
        compiler_params=_params(has_side_effects=True),
    )(*grads)


def _scatter_chips(parts):
    n = len(parts)

    def body(*refs):
        src, out = refs[:n], refs[n:2 * n]
        ssem, rsem = refs[2 * n:]
        x, y, c, chips = _place()
        cps = []
        for a in range(n):
            for r, chip in enumerate(chips):
                cps.append(_remote(src[a].at[2 * chip[0] + chip[1]], out[a].at[r], ssem, rsem, a * 3 + r, (*chip, c)))
        for cp in cps:
            cp.start()
        for cp in cps:
            cp.wait()

    return pl.pallas_call(
        body, name="scatter_chips",
        in_specs=[HBM_SPEC] * n, out_specs=[HBM_SPEC] * n,
        out_shape=[jax.ShapeDtypeStruct((3,) + p.shape[1:], F32) for p in parts],
        scratch_shapes=[pltpu.SemaphoreType.DMA((3 * n,)), pltpu.SemaphoreType.DMA((3 * n,))],
        compiler_params=_params(has_side_effects=True),
    )(*parts)


def _join_halves(reduced, layout):
    n = len(reduced)
    n_out = max(o for o, _ in layout) + 1
    shapes = [None] * n_out
    for (o, _), r in zip(layout, reduced):
        shapes[o] = (2, 2 * r.shape[0], r.shape[1])

    def body(*refs):
        src, out = refs[:n], refs[n:n + n_out]
        ssem, rsem, lsem = refs[n + n_out:]
        x, y, c, _ = _place()

        def dst(a, cc):
            o, layer = layout[a]
            h = src[a].shape[0]
            return out[o].at[layer, pl.ds(cc * h, h), :]

        local = [pltpu.make_async_copy(src[a], dst(a, c), lsem.at[a]) for a in range(n)]
        cps = [_remote(src[a], dst(a, c), ssem, rsem, a, (x, y, 1 - c)) for a in range(n)]
        for cp in local + cps:
            cp.start()
        for a in range(n):
            _remote(src[a], dst(a, 1 - c), ssem, rsem, a, (x, y, 1 - c)).wait_recv()
        for cp in cps:
            cp.wait_send()
        for cp in local:
            cp.wait()

    return pl.pallas_call(
        body, name="join_halves",
        in_specs=[HBM_SPEC] * n, out_specs=[HBM_SPEC] * n_out,
        out_shape=[jax.ShapeDtypeStruct(s, F32) for s in shapes],
        scratch_shapes=[pltpu.SemaphoreType.DMA((n,)), pltpu.SemaphoreType.DMA((n,)), pltpu.SemaphoreType.DMA((n,))],
        compiler_params=_params(has_side_effects=True),
    )(*reduced)


def _allreduce_small(buf):
    R, L = buf.shape

    def body(buf_ref, out_ref, pair_ref, chip_ref, ssem, rsem):
        x, y, c, chips = _place()
        me_q = 2 * x + y
        pair_ref[c] = buf_ref[...]
        to_sib = _remote(buf_ref, pair_ref.at[c], ssem, rsem, 0, (x, y, 1 - c))
        to_sib.start()
        _remote(buf_ref, pair_ref.at[1 - c], ssem, rsem, 0, (x, y, 1 - c)).wait_recv()
        chip_ref[me_q] = pair_ref[0] + pair_ref[1]
        cps = [_remote(chip_ref.at[me_q], chip_ref.at[me_q], ssem, rsem, 1 + r, (*chip, c))
               for r, chip in enumerate(chips)]
        for cp in cps:
            cp.start()
        for r, chip in enumerate(chips):
            q = 2 * chip[0] + chip[1]
            _remote(chip_ref.at[q], chip_ref.at[q], ssem, rsem, 1 + r, (*chip, c)).wait_recv()
        out_ref[...] = (chip_ref[0] + chip_ref[1]) + (chip_ref[2] + chip_ref[3])
        to_sib.wait_send()
        for cp in cps:
            cp.wait_send()

    return pl.pallas_call(
        body, name="allreduce_small",
        in_specs=[VMEM_SPEC], out_specs=VMEM_SPEC,
        out_shape=jax.ShapeDtypeStruct((R, L), F32),
        scratch_shapes=[pltpu.VMEM((2, R, L), F32), pltpu.VMEM((N_CHIPS, R, L), F32),
                        pltpu.SemaphoreType.DMA((4,)), pltpu.SemaphoreType.DMA((4,))],
        compiler_params=_params(has_side_effects=True),
    )(buf)


def _pack(arrays):
    flat = jnp.concatenate([a.reshape(-1) for a in arrays])
    pad = (-flat.shape[0]) % (8 * LANES)
    return jnp.pad(flat, (0, pad)).reshape(-1, LANES)


def _unpack(buf, like):
    flat = buf.reshape(-1)
    out, off = [], 0
    for a in like:
        out.append(flat[off:off + a.size].reshape(a.shape))
        off += a.size
    return out


def _block_diag(pw):
    rows = []
    for gi in range(len(POOL_WINDOWS)):
        blocks = [pw[gi] if gj == gi else jnp.zeros_like(pw[gi]) for gj in range(len(POOL_WINDOWS))]
        rows.append(jnp.concatenate(blocks, axis=1))
    return jnp.concatenate(rows, axis=0)


def kernel(x, norm1, w_in, pool_w, pool_scale, sg_norm, sg_w, sg_b, w_out, norm2, w_up, w_down, final_norm, loss_target, m_norm1, m_w_in, m_pool_w, m_pool_scale, m_sg_norm, m_sg_w, m_sg_b, m_w_out, m_norm2, m_w_up, m_w_down, m_final_norm, v_norm1, v_w_in, v_pool_w, v_pool_scale, v_sg_norm, v_sg_w, v_sg_b, v_w_out, v_norm2, v_w_up, v_w_down, v_final_norm):
    depth = norm1.shape[0]
    T = x.shape[1]
    xs = x.reshape(T, D_MODEL)
    target = loss_target.reshape(T, D_MODEL)

    big = [w_in, w_out, w_up, w_down]
    g_in, g_out, g_up, g_down = _gather_weights([w.astype(BF16) for w in big])
    wi = [jnp.concatenate([g_in[q, l] for q in range(N_CHIPS)], axis=1) for l in range(depth)]
    wo = [g_out[:, l].reshape(D_MODEL, D_MODEL) for l in range(depth)]
    wu = [jnp.concatenate([g_up[q, l] for q in range(N_CHIPS)], axis=1) for l in range(depth)]
    wd = [g_down[:, l].reshape(D_FF, D_MODEL) for l in range(depth)]

    tril = jnp.tril(jnp.ones((CHUNK, CHUNK), F32))
    saved = []
    cur = xs
    for l in range(depth):
        wbd = _block_diag(pool_w[l]).astype(BF16)
        wm = sg_w[l] * tril
        wm_s = wm.reshape(SG_HEADS * CHUNK, CHUNK).astype(BF16)
        wmt_s = jnp.swapaxes(wm, 1, 2).reshape(SG_HEADS * CHUNK, CHUNK).astype(BF16)
        bias = jnp.repeat(sg_b[l].T, SB_HD, axis=1)
        n1, n2 = norm1[l][None], norm2[l][None]
        psc, sgn = pool_scale[l][None], sg_norm[l][None]
        proj, h, qkv = _inproj_fwd(cur, n1, wi[l])
        ya = _pool_fwd(proj, wbd, psc)
        yb = _sg_fwd(proj, wm_s, bias, sgn)
        yc = _attn_fwd(qkv)
        x1, ymix = _outproj_fwd(cur, ya, yb, yc, wo[l])
        x2, h2, u, act = _mlp_fwd(x1, n2, wu[l], wd[l])
        saved.append(dict(x0=cur, x1=x1, proj=proj, h=h, qkv=qkv, yc=yc, ymix=ymix, h2=h2, u=u, act=act,
                          wbd=wbd, wm_s=wm_s, wmt_s=wmt_s, bias=bias, n1=n1, n2=n2, psc=psc, sgn=sgn))
        cur = x2

    loss_row, dcur, d_final = _loss_head(cur, final_norm[None], target)
    loss = lax.psum(loss_row[0, 0], AXES)

    big_grads = [None] * depth
    small = [None] * depth
    for l in reversed(range(depth)):
        s = saved[l]
        dx1, du, d_n2 = _mlp_bwd(dcur, s["x1"], s["n2"], s["u"], wu[l], wd[l])
        g_up_l = _tn_matmul(s["h2"], du, "grad_w_up", n_split=N_CHIPS)
        g_down_l = _tn_matmul(s["act"], dcur, "grad_w_down")[0].reshape(N_CHIPS, D_FF // N_CHIPS, D_MODEL)
        dymix = _nt_matmul(dx1, wo[l])
        g_out_l = _tn_matmul(s["ymix"], dx1, "grad_w_out")[0].reshape(N_CHIPS, D_MODEL // N_CHIPS, D_MODEL)
        da_in, d_wbd, d_psc = _pool_bwd(s["proj"], dymix, s["wbd"], s["psc"])
        du_pre, dv_pre, d_wm, d_bias, d_sgn = _sg_bwd(s["proj"], dymix, s["wm_s"], s["wmt_s"], s["bias"], s["sgn"])
        dq, dk, dv = _attn_bwd(s["qkv"], s["yc"], dymix)
        dproj, dx0, d_n1 = _inproj_bwd([da_in, du_pre, dv_pre, dq, dk, dv], wi[l], s["x0"], s["n1"], dx1)
        g_in_l = _tn_matmul(s["h"], dproj, "grad_w_in")[0]
        g_in_l = g_in_l.reshape(D_MODEL, N_CHIPS, IN_COLS // N_CHIPS).transpose(1, 0, 2)
        big_grads[l] = [g_in_l, g_out_l, g_up_l, g_down_l]
        d_pw = jnp.stack([d_wbd[gi * POOL_GW:(gi + 1) * POOL_GW, gi * POOL_GW:(gi + 1) * POOL_GW]
                          for gi in range(len(POOL_WINDOWS))])
        small[l] = dict(norm1=d_n1[0], pool_w=d_pw, pool_scale=d_psc[0], sg_norm=d_sgn[0],
                        sg_w=d_wm.reshape(SG_HEADS, CHUNK, CHUNK), sg_b=d_bias[:, :SG_HEADS].T, norm2=d_n2[0])
        dcur = dx0
    grad_x = dcur.reshape(x.shape)

    flat = [g for l in range(depth) for g in big_grads[l]]
    layout = [(a, l) for l in range(depth) for a in range(4)]
    got = _swap_halves(flat)
    c_idx = lax.axis_index("c")
    q_idx = 2 * lax.axis_index("x") + lax.axis_index("y")
    parts = []
    for g, o in zip(flat, got):
        h = o.shape[1]
        mine = lax.dynamic_slice_in_dim(g, c_idx * h, h, axis=1)
        parts.append(_elementwise(lambda a, b: a + b, "add_pair", [mine, o], 1)[0])
    arrived = _scatter_chips(parts)
    reduced = []
    for p, r in zip(parts, arrived):
        own = lax.dynamic_index_in_dim(p, q_idx, axis=0, keepdims=False)
        reduced.append(_elementwise(lambda a, b, c, d: (a + b) + (c + d), "add_chips", [own, r[0], r[1], r[2]], 1)[0])
    gw_in, gw_out, gw_up, gw_down = _join_halves(reduced, layout)

    names = ["norm1", "pool_w", "pool_scale", "sg_norm", "sg_w", "sg_b", "norm2"]
    small_w = [norm1, pool_w, pool_scale, sg_norm, sg_w, sg_b, norm2, final_norm]
    small_m = [m_norm1, m_pool_w, m_pool_scale, m_sg_norm, m_sg_w, m_sg_b, m_norm2, m_final_norm]
    small_v = [v_norm1, v_pool_w, v_pool_scale, v_sg_norm, v_sg_w, v_sg_b, v_norm2, v_final_norm]
    small_g = [jnp.stack([small[l][k] for l in range(depth)]) for k in names] + [d_final[0]]
    g_packed = _allreduce_small(_pack(small_g))
    s_delta, s_m, s_v = _elementwise(_adamw, "adamw_small", [_pack(small_w), g_packed, _pack(small_m), _pack(small_v)], 3)
    gs = dict(zip(names + ["final_norm"], _unpack(g_packed, small_w)))
    ds = dict(zip(names + ["final_norm"], _unpack(s_delta, small_w)))
    ms = dict(zip(names + ["final_norm"], _unpack(s_m, small_w)))
    vs = dict(zip(names + ["final_norm"], _unpack(s_v, small_w)))

    big_g = dict(w_in=gw_in, w_out=gw_out, w_up=gw_up, w_down=gw_down)
    big_w = dict(w_in=(w_in, m_w_in, v_w_in), w_out=(w_out, m_w_out, v_w_out),
                 w_up=(w_up, m_w_up, v_w_up), w_down=(w_down, m_w_down, v_w_down))
    for k, (w, m, v) in big_w.items():
        ds[k], ms[k], vs[k] = _elementwise(_adamw, "adamw_" + k, [w, big_g[k], m, v], 3)
        gs[k] = big_g[k]

    order = ["norm1", "w_in", "pool_w", "pool_scale", "sg_norm", "sg_w", "sg_b", "w_out", "norm2", "w_up", "w_down",
             "final_norm"]
    return (loss, grad_x, *[gs[k] for k in order], *[ds[k] for k in order], *[ms[k] for k in order],
            *[vs[k] for k in order])
```

```python
import functools

import jax
import jax.numpy as jnp
from jax import lax
from jax.experimental import pallas as pl
from jax.experimental.pallas import tpu as pltpu

F32 = jnp.float32
BF16 = jnp.bfloat16
MESH = pl.DeviceIdType.MESH
AXES = ("x", "y", "c")

EPS = 1e-6
D_MODEL = 1024
POOL_WIDTH = 256
SG_WIDTH = 256
SB_WIDTH = 512
POOL_WINDOWS = (2, 4, 8, 16)
POOL_GW = 64
POOL_HALO = 16
CHUNK = 128
SG_HEADS = 4
SB_HD = 64
IN_COLS = 2304
QKV_OFF = 768
D_FF = 4096
N_CHIPS = 4
LANES = 128
VMEM_LIMIT = 56 * 1024 * 1024
UNDERFLOW = -104.0

ADAM_LR = 0.001
ADAM_B1 = 0.9
ADAM_B2 = 0.999
ADAM_EPS = 1e-08
ADAM_WD = 0.01
ADAM_STEP = 10

HBM_SPEC = pl.BlockSpec(memory_space=pl.ANY)
VMEM_SPEC = pl.BlockSpec(memory_space=pltpu.VMEM)


def _params(**kw):
    return pltpu.CompilerParams(vmem_limit_bytes=VMEM_LIMIT, **kw)


def _tile(n, pref):
    if n <= pref:
        return n
    for t in range(pref - pref % LANES, 0, -LANES):
        if n % t == 0:
            return t
    raise ValueError((n, pref))


def _nn(a, b):
    return jnp.dot(a, b, preferred_element_type=F32)


def _nt(a, b):
    return lax.dot_general(a, b, (((1,), (1,)), ((), ())), preferred_element_type=F32)


def _tn(a, b):
    return lax.dot_general(a, b, (((0,), (0,)), ((), ())), preferred_element_type=F32)


def _rms_fwd(x, g):
    r = lax.rsqrt(jnp.mean(x * x, axis=-1, keepdims=True) + EPS)
    xhat = x * r
    return xhat * g, xhat, r


def _rms_bwd(dy, xhat, r, g):
    dxhat = dy * g
    dx = r * (dxhat - xhat * jnp.mean(dxhat * xhat, axis=-1, keepdims=True))
    return dx, dy * xhat


_GELU_K = 0.7978845608028654
_GELU_C = 0.044715


def _gelu(x):
    return 0.5 * x * (1.0 + jnp.tanh(_GELU_K * (x + _GELU_C * x * x * x)))


def _gelu_grad(x):
    t = jnp.tanh(_GELU_K * (x + _GELU_C * x * x * x))
    return 0.5 * (1.0 + t) + 0.5 * x * (1.0 - t * t) * _GELU_K * (1.0 + 3.0 * _GELU_C * x * x)


def _inproj_fwd(x, g, w):
    T, D = x.shape
    N = w.shape[1]
    tt = _tile(T, 512)

    def body(x_ref, g_ref, w_ref, proj_ref, h_ref, qkv_ref):
        h, _, _ = _rms_fwd(x_ref[...], g_ref[...])
        hb = h.astype(BF16)
        h_ref[...] = hb
        p = _nn(hb, w_ref[...])
        proj_ref[...] = p
        qkv_ref[...] = p[:, QKV_OFF:].astype(BF16)

    return pl.pallas_call(
        body, name="inproj_fwd", grid=(T // tt,),
        in_specs=[pl.BlockSpec((tt, D), lambda i: (i, 0)), pl.BlockSpec((1, D), lambda i: (0, 0)),
                  pl.BlockSpec((D, N), lambda i: (0, 0))],
        out_specs=[pl.BlockSpec((tt, N), lambda i: (i, 0)), pl.BlockSpec((tt, D), lambda i: (i, 0)),
                   pl.BlockSpec((tt, N - QKV_OFF), lambda i: (i, 0))],
        out_shape=[jax.ShapeDtypeStruct((T, N), F32), jax.ShapeDtypeStruct((T, D), BF16),
                   jax.ShapeDtypeStruct((T, N - QKV_OFF), BF16)],
        compiler_params=_params(),
    )(x, g, w)


def _inproj_bwd(pieces, w, x, g, dres):
    T, D = x.shape
    N = w.shape[1]
    tt = _tile(T, 512)
    widths = [p.shape[1] for p in pieces]
    offs = [sum(widths[:k]) for k in range(len(widths))]
    assert sum(widths) == N
    n_p = len(pieces)

    def body(*refs):
        p_refs = refs[:n_p]
        w_ref, x_ref, g_ref, dres_ref, dproj_ref, dx_ref, dg_ref = refs[n_p:]
        for p_ref, o, wd in zip(p_refs, offs, widths):
            dproj_ref[:, o:o + wd] = p_ref[...].astype(BF16)
        dh = _nt(dproj_ref[...], w_ref[...])
        gv = g_ref[...]
        _, xhat, r = _rms_fwd(x_ref[...], gv)
        dx, dgrow = _rms_bwd(dh, xhat, r, gv)
        dx_ref[...] = dres_ref[...] + dx

        @pl.when(pl.program_id(0) == 0)
        def _():
            dg_ref[...] = jnp.zeros_like(dg_ref)

        dg_ref[...] += jnp.sum(dgrow, axis=0, keepdims=True)

    return pl.pallas_call(
        body, name="inproj_bwd", grid=(T // tt,),
        in_specs=[pl.BlockSpec((tt, wd), lambda i: (i, 0)) for wd in widths] + [
            pl.BlockSpec((D, N), lambda i: (0, 0)), pl.BlockSpec((tt, D), lambda i: (i, 0)),
            pl.BlockSpec((1, D), lambda i: (0, 0)), pl.BlockSpec((tt, D), lambda i: (i, 0))],
        out_specs=[pl.BlockSpec((tt, N), lambda i: (i, 0)), pl.BlockSpec((tt, D), lambda i: (i, 0)),
                   pl.BlockSpec((1, D), lambda i: (0, 0))],
        out_shape=[jax.ShapeDtypeStruct((T, N), BF16), jax.ShapeDtypeStruct((T, D), F32),
                   jax.ShapeDtypeStruct((1, D), F32)],
        compiler_params=_params(),
    )(*pieces, w, x, g, dres)


def _pool_select(s2, s4, s8, s16, grp):
    return jnp.where(grp == 0, s2, jnp.where(grp == 1, s4, jnp.where(grp == 2, s8, s16)))


def _pool_count(t_glob, grp):
    win = jnp.where(grp == 0, 2, jnp.where(grp == 1, 4, jnp.where(grp == 2, 8, 16)))
    return jnp.minimum(t_glob + 1, win).astype(F32)


def _pool_diff(a, halo, base, tt):
    n = tt + POOL_HALO
    ext = jnp.concatenate([halo, a], axis=0)
    s2 = ext + pltpu.roll(ext, 1, 0)
    s4 = s2 + pltpu.roll(s2, 2, 0)
    s8 = s4 + pltpu.roll(s4, 4, 0)
    s16 = s8 + pltpu.roll(s8, 8, 0)
    grp = lax.broadcasted_iota(jnp.int32, (n, POOL_WIDTH), 1) // POOL_GW
    t_glob = lax.broadcasted_iota(jnp.int32, (n, POOL_WIDTH), 0) + (base - POOL_HALO)
    pooled = _pool_select(s2, s4, s8, s16, grp) / _pool_count(t_glob, grp)
    return pooled[POOL_HALO:] - a


def _pool_specs(T, tt):
    hb = tt // POOL_HALO
    return [pl.BlockSpec((tt, POOL_WIDTH), lambda i: (i, 0)),
            pl.BlockSpec((POOL_HALO, POOL_WIDTH), lambda i: (jnp.maximum(i * hb - 1, 0), 0))]


def _pool_fwd(proj, wbd, scale):
    T = proj.shape[0]
    tt = _tile(T, 512)

    def body(a_ref, halo_ref, w_ref, sc_ref, y_ref):
        i = pl.program_id(0)
        halo = jnp.where(i > 0, halo_ref[...], 0.0)
        d = _pool_diff(a_ref[...], halo, i * tt, tt)
        y_ref[...] = _nn(d.astype(BF16), w_ref[...]) * sc_ref[...]

    return pl.pallas_call(
        body, name="pool_fwd", grid=(T // tt,),
        in_specs=_pool_specs(T, tt) + [pl.BlockSpec((POOL_WIDTH, POOL_WIDTH), lambda i: (0, 0)),
                                       pl.BlockSpec((1, POOL_WIDTH), lambda i: (0, 0))],
        out_specs=pl.BlockSpec((tt, POOL_WIDTH), lambda i: (i, 0)),
        out_shape=jax.ShapeDtypeStruct((T, POOL_WIDTH), F32),
        compiler_params=_params(),
    )(proj, proj, wbd, scale)


def _pool_bwd(proj, dymix, wbd, scale):
    T = proj.shape[0]
    tt = _tile(T, 512)
    hb = tt // POOL_HALO
    nblk = T // tt
    n = tt + POOL_HALO

    def body(a_ref, halo_ref, dy_ref, dyn_ref, w_ref, sc_ref, da_ref, dw_ref, dsc_ref):
        i = pl.program_id(0)
        halo = jnp.where(i > 0, halo_ref[...], 0.0)
        d = _pool_diff(a_ref[...], halo, i * tt, tt)
        db = d.astype(BF16)
        wv = w_ref[...]
        sc = sc_ref[...]
        dy = dy_ref[...]
        dys = dy * sc

        @pl.when(i == 0)
        def _():
            dw_ref[...] = jnp.zeros_like(dw_ref)
            dsc_ref[...] = jnp.zeros_like(dsc_ref)

        dsc_ref[...] += jnp.sum(dy * _nn(db, wv), axis=0, keepdims=True)
        dw_ref[...] += _tn(db, dys.astype(BF16))
        dyn = jnp.where(i < nblk - 1, dyn_ref[...], 0.0) * sc
        dd = _nt(jnp.concatenate([dys, dyn], axis=0).astype(BF16), wv)
        grp = lax.broadcasted_iota(jnp.int32, (n, POOL_WIDTH), 1) // POOL_GW
        t_glob = lax.broadcasted_iota(jnp.int32, (n, POOL_WIDTH), 0) + i * tt
        e = dd / _pool_count(t_glob, grp)
        r2 = e + pltpu.roll(e, n - 1, 0)
        r4 = r2 + pltpu.roll(r2, n - 2, 0)
        r8 = r4 + pltpu.roll(r4, n - 4, 0)
        r16 = r8 + pltpu.roll(r8, n - 8, 0)
        da_ref[...] = (_pool_select(r2, r4, r8, r16, grp) - dd)[:tt]

    return pl.pallas_call(
        body, name="pool_bwd", grid=(nblk,),
        in_specs=_pool_specs(T, tt) + [
            pl.BlockSpec((tt, POOL_WIDTH), lambda i: (i, 0)),
            pl.BlockSpec((POOL_HALO, POOL_WIDTH), lambda i: (jnp.minimum((i + 1) * hb, T // POOL_HALO - 1), 0)),
            pl.BlockSpec((POOL_WIDTH, POOL_WIDTH), lambda i: (0, 0)), pl.BlockSpec((1, POOL_WIDTH), lambda i: (0, 0))],
        out_specs=[pl.BlockSpec((tt, POOL_WIDTH), lambda i: (i, 0)),
                   pl.BlockSpec((POOL_WIDTH, POOL_WIDTH), lambda i: (0, 0)),
                   pl.BlockSpec((1, POOL_WIDTH), lambda i: (0, 0))],
        out_shape=[jax.ShapeDtypeStruct((T, POOL_WIDTH), F32),
                   jax.ShapeDtypeStruct((POOL_WIDTH, POOL_WIDTH), F32),
                   jax.ShapeDtypeStruct((1, POOL_WIDTH), F32)],
        compiler_params=_params(),
    )(proj, proj, dymix, dymix, wbd, scale)


def _head_select(stacked, grp):
    out = jnp.where(grp == 0, stacked[0:CHUNK], 0.0)
    for h in range(1, SG_HEADS):
        out = out + jnp.where(grp == h, stacked[h * CHUNK:(h + 1) * CHUNK], 0.0)
    return out


def _sg_specs(tt):
    return [pl.BlockSpec((tt, SG_WIDTH), lambda i: (i, 1)), pl.BlockSpec((tt, SG_WIDTH), lambda i: (i, 2))]


def _sg_fwd(proj, wm, bias, g):
    T = proj.shape[0]
    tt = _tile(T, 512)

    def body(u_ref, v_ref, wm_ref, b_ref, g_ref, y_ref):
        zu = _gelu(u_ref[...])
        vn, _, _ = _rms_fwd(_gelu(v_ref[...]), g_ref[...])
        grp = lax.broadcasted_iota(jnp.int32, (CHUNK, SG_WIDTH), 1) // SB_HD
        for n in range(tt // CHUNK):
            rows = slice(n * CHUNK, (n + 1) * CHUNK)
            sv = _head_select(_nn(wm_ref[...], vn[rows].astype(BF16)), grp) + b_ref[...]
            y_ref[rows, :] = zu[rows] * sv

    return pl.pallas_call(
        body, name="sg_fwd", grid=(T // tt,),
        in_specs=_sg_specs(tt) + [pl.BlockSpec((SG_HEADS * CHUNK, CHUNK), lambda i: (0, 0)),
                                  pl.BlockSpec((CHUNK, SG_WIDTH), lambda i: (0, 0)),
                                  pl.BlockSpec((1, SG_WIDTH), lambda i: (0, 0))],
        out_specs=pl.BlockSpec((tt, SG_WIDTH), lambda i: (i, 0)),
        out_shape=jax.ShapeDtypeStruct((T, SG_WIDTH), F32),
        compiler_params=_params(),
    )(proj, proj, wm, bias, g)


def _sg_bwd(proj, dymix, wm, wmt, bias, g):
    T = proj.shape[0]
    tt = _tile(T, 512)
    nblk = T // tt

    def body(u_ref, v_ref, dy_ref, wm_ref, wmt_ref, b_ref, g_ref,
             du_ref, dv_ref, dw_ref, db_ref, dg_ref, dvn_ref, dbias_ref):
        i = pl.program_id(0)
        up, vp = u_ref[...], v_ref[...]
        gv = g_ref[...]
        zu, zv = _gelu(up), _gelu(vp)
        vn, xhat, r = _rms_fwd(zv, gv)
        gu = _gelu_grad(up)
        grp = lax.broadcasted_iota(jnp.int32, (CHUNK, SG_WIDTH), 1) // SB_HD

        @pl.when(i == 0)
        def _():
            dw_ref[...] = jnp.zeros_like(dw_ref)
            dbias_ref[...] = jnp.zeros_like(dbias_ref)
            dg_ref[...] = jnp.zeros_like(dg_ref)

        for n in range(tt // CHUNK):
            rows = slice(n * CHUNK, (n + 1) * CHUNK)
            vc = vn[rows].astype(BF16)
            sv = _head_select(_nn(wm_ref[...], vc), grp) + b_ref[...]
            dy = dy_ref[rows, :]
            du_ref[rows, :] = dy * sv * gu[rows]
            dsv = dy * zu[rows]
            dsvb = dsv.astype(BF16)
            dvn_ref[rows, :] = _head_select(_nn(wmt_ref[...], dsvb), grp)
            stacked = jnp.concatenate([jnp.where(grp == h, dsv, 0.0) for h in range(SG_HEADS)], axis=0)
            dw_ref[...] += _nt(stacked.astype(BF16), vc)
            dbias_ref[...] += dsv

        dzv, dgrow = _rms_bwd(dvn_ref[...], xhat, r, gv)
        dg_ref[...] += jnp.sum(dgrow, axis=0, keepdims=True)
        dv_ref[...] = dzv * _gelu_grad(vp)

        @pl.when(i == nblk - 1)
        def _():
            t_i = lax.broadcasted_iota(jnp.int32, (SG_HEADS * CHUNK, CHUNK), 0) % CHUNK
            s_i = lax.broadcasted_iota(jnp.int32, (SG_HEADS * CHUNK, CHUNK), 1)
            dw_ref[...] = jnp.where(s_i <= t_i, dw_ref[...], 0.0)
            lane = lax.broadcasted_iota(jnp.int32, (CHUNK, LANES), 1)
            acc = jnp.zeros((CHUNK, LANES), F32)
            for h in range(SG_HEADS):
                tot = jnp.sum(jnp.where(grp == h, dbias_ref[...], 0.0), axis=1, keepdims=True)
                acc = acc + jnp.where(lane == h, tot, 0.0)
            db_ref[...] = acc

    return pl.pallas_call(
        body, name="sg_bwd", grid=(nblk,),
        in_specs=_sg_specs(tt) + [pl.BlockSpec((tt, SG_WIDTH), lambda i: (i, 1)),
                                  pl.BlockSpec((SG_HEADS * CHUNK, CHUNK), lambda i: (0, 0)),
                                  pl.BlockSpec((SG_HEADS * CHUNK, CHUNK), lambda i: (0, 0)),
                                  pl.BlockSpec((CHUNK, SG_WIDTH), lambda i: (0, 0)),
                                  pl.BlockSpec((1, SG_WIDTH), lambda i: (0, 0))],
        out_specs=[pl.BlockSpec((tt, SG_WIDTH), lambda i: (i, 0)), pl.BlockSpec((tt, SG_WIDTH), lambda i: (i, 0)),
                   pl.BlockSpec((SG_HEADS * CHUNK, CHUNK), lambda i: (0, 0)),
                   pl.BlockSpec((CHUNK, LANES), lambda i: (0, 0)), pl.BlockSpec((1, SG_WIDTH), lambda i: (0, 0))],
        out_shape=[jax.ShapeDtypeStruct((T, SG_WIDTH), F32), jax.ShapeDtypeStruct((T, SG_WIDTH), F32),
                   jax.ShapeDtypeStruct((SG_HEADS * CHUNK, CHUNK), F32),
                   jax.ShapeDtypeStruct((CHUNK, LANES), F32), jax.ShapeDtypeStruct((1, SG_WIDTH), F32)],
        scratch_shapes=[pltpu.VMEM((tt, SG_WIDTH), F32), pltpu.VMEM((CHUNK, SG_WIDTH), F32)],
        compiler_params=_params(),
    )(proj, proj, dymix, wm, wmt, bias, g)


def _split_dot(x, u):
    hi = x.astype(BF16)
    lo = (x - hi.astype(F32)).astype(BF16)
    return _nn(hi, u) + _nn(lo, u)


def _sb_logits(qh, kb):
    z = _nt(qh, kb) * (1.0 / 8.0)
    sp = jnp.log1p(jnp.exp(-jnp.abs(z)))
    return jnp.minimum(z, 0.0) - sp, -jnp.maximum(z, 0.0) - sp


def _attn_qkv_specs(tq, T):
    base = (IN_COLS - 3 * SB_WIDTH - QKV_OFF) // LANES
    nb = SB_WIDTH // LANES
    return [pl.BlockSpec((tq, LANES), lambda p, i: (i, base + p)),
            pl.BlockSpec((T, LANES), lambda p, i: (0, base + nb + p)),
            pl.BlockSpec((T, LANES), lambda p, i: (0, base + 2 * nb + p))]


def _attn_fwd(qkv):
    T = qkv.shape[0]
    tq = _tile(T, 256)

    def body(q_ref, k_ref, v_ref, o_ref):
        i = pl.program_id(1)
        lane = lax.broadcasted_iota(jnp.int32, (tq, LANES), 1)
        row = lax.broadcasted_iota(jnp.int32, (tq, tq), 0)
        col = lax.broadcasted_iota(jnp.int32, (tq, tq), 1)
        after = jnp.where(row > col, 1.0, 0.0).astype(BF16)
        valid = col < row
        q = q_ref[...].astype(F32)
        qh = [jnp.where((lane // SB_HD) == hh, q, 0.0).astype(BF16) for hh in range(2)]

        def tile(j, state, diag):
            ks = pl.ds(pl.multiple_of(j * tq, tq), tq)
            kb, vb = k_ref[ks, :], v_ref[ks, :]
            new = []
            for hh in range(2):
                carry, acc = state[hh]
                lb, lm = _sb_logits(qh[hh], kb)
                if diag:
                    lm = jnp.where(valid, lm, 0.0)
                a = jnp.exp(lb + _split_dot(lm, after) + carry)
                if diag:
                    a = jnp.where(valid, a, 0.0)
                new.append((carry + jnp.sum(lm, axis=1, keepdims=True), acc + _nn(a.astype(BF16), vb)))
            return tuple(new)

        def live(state):
            return jnp.maximum(jnp.max(state[0][0]), jnp.max(state[1][0]))

        zero = (jnp.zeros((tq, 1), F32), jnp.zeros((tq, LANES), F32))
        state = tile(i, (zero, zero), True)

        def cond(st):
            return jnp.logical_and(st[0] >= 0, st[2] > UNDERFLOW)

        def step(st):
            state = tile(st[0], st[1], False)
            return st[0] - 1, state, live(state)

        _, state, _ = lax.while_loop(cond, step, (i - 1, state, live(state)))
        o_ref[...] = jnp.where(lane < SB_HD, state[0][1], state[1][1])

    return pl.pallas_call(
        body, name="attn_fwd", grid=(SB_WIDTH // LANES, T // tq),
        in_specs=_attn_qkv_specs(tq, T),
        out_specs=pl.BlockSpec((tq, LANES), lambda p, i: (i, p)),
        out_shape=jax.ShapeDtypeStruct((T, SB_WIDTH), F32),
        compiler_params=_params(),
    )(qkv, qkv, qkv)


def _attn_bwd(qkv, o, dymix):
    T = qkv.shape[0]
    tq = _tile(T, 256)
    nq = T // tq
    yc_blk = (POOL_WIDTH + SG_WIDTH) // LANES

    def body(q_ref, k_ref, v_ref, o_ref, do_ref, dq_ref, dk_ref, dv_ref):
        i = pl.program_id(1)
        lane = lax.broadcasted_iota(jnp.int32, (tq, LANES), 1)
        row = lax.broadcasted_iota(jnp.int32, (tq, tq), 0)
        col = lax.broadcasted_iota(jnp.int32, (tq, tq), 1)
        after = jnp.where(row > col, 1.0, 0.0).astype(BF16)
        from_here = jnp.where(row >= col, 1.0, 0.0).astype(BF16)
        valid = col < row

        @pl.when(i == 0)
        def _():
            dk_ref[...] = jnp.zeros_like(dk_ref)
            dv_ref[...] = jnp.zeros_like(dv_ref)

        q = q_ref[...].astype(F32)
        ov = o_ref[...]
        dov = do_ref[...]
        heads = [(lane // SB_HD) == hh for hh in range(2)]
        qh = [jnp.where(h, q, 0.0).astype(BF16) for h in heads]
        dohb = [jnp.where(h, dov, 0.0).astype(BF16) for h in heads]
        delta = [jnp.sum(d.astype(F32) * ov, axis=1, keepdims=True) for d in dohb]

        def tile(j, state, diag):
            ks = pl.ds(pl.multiple_of(j * tq, tq), tq)
            kb, vb = k_ref[ks, :], v_ref[ks, :]
            new, dk, dv = [], None, None
            for hh in range(2):
                c_a, c_r, dqa = state[hh]
                lb, lm = _sb_logits(qh[hh], kb)
                if diag:
                    lm = jnp.where(valid, lm, 0.0)
                a = jnp.exp(lb + _split_dot(lm, after) + c_a)
                if diag:
                    a = jnp.where(valid, a, 0.0)
                ab = a.astype(BF16)
                sig = jnp.exp(lb)
                g = _nt(dohb[hh], vb) * ab.astype(F32)
                left = delta[hh] - (c_r + _split_dot(g, from_here))
                dz = (g * (1.0 - sig) - left * sig) * (1.0 / 8.0)
                if diag:
                    dz = jnp.where(valid, dz, 0.0)
                dzb = dz.astype(BF16)
                dk_h, dv_h = _tn(dzb, qh[hh]), _tn(ab, dohb[hh])
                dk, dv = (dk_h, dv_h) if hh == 0 else (dk + dk_h, dv + dv_h)
                new.append((c_a + jnp.sum(lm, axis=1, keepdims=True), c_r + jnp.sum(g, axis=1, keepdims=True),
                            dqa + _nn(dzb, kb)))
            dk_ref[ks, :] += dk
            dv_ref[ks, :] += dv
            return tuple(new)

        def live(state):
            return jnp.maximum(jnp.max(state[0][0]), jnp.max(state[1][0]))

        zero = (jnp.zeros((tq, 1), F32), jnp.zeros((tq, 1), F32), jnp.zeros((tq, LANES), F32))
        state = tile(i, (zero, zero), True)

        def cond(st):
            return jnp.logical_and(st[0] >= 0, st[2] > UNDERFLOW)

        def step(st):
            state = tile(st[0], st[1], False)
            return st[0] - 1, state, live(state)

        _, state, _ = lax.while_loop(cond, step, (i - 1, state, live(state)))
        dq_ref[...] = jnp.where(lane < SB_HD, state[0][2], state[1][2])

    return pl.pallas_call(
        body, name="attn_bwd", grid=(SB_WIDTH // LANES, nq),
        in_specs=_attn_qkv_specs(tq, T) + [pl.BlockSpec((tq, LANES), lambda p, i: (i, p)),
                                           pl.BlockSpec((tq, LANES), lambda p, i: (i, yc_blk + p))],
        out_specs=[pl.BlockSpec((tq, LANES), lambda p, i: (i, p)), pl.BlockSpec((T, LANES), lambda p, i: (0, p)),
                   pl.BlockSpec((T, LANES), lambda p, i: (0, p))],
        out_shape=[jax.ShapeDtypeStruct((T, SB_WIDTH), F32)] * 3,
        compiler_params=_params(),
    )(qkv, qkv, qkv, o, dymix)


def _outproj_fwd(x, ya, yb, yc, w):
    T, D = x.shape
    tt = _tile(T, 512)

    def body(x_ref, ya_ref, yb_ref, yc_ref, w_ref, x1_ref, ymix_ref):
        ymix_ref[:, 0:POOL_WIDTH] = ya_ref[...].astype(BF16)
        ymix_ref[:, POOL_WIDTH:POOL_WIDTH + SG_WIDTH] = yb_ref[...].astype(BF16)
        ymix_ref[:, POOL_WIDTH + SG_WIDTH:] = yc_ref[...].astype(BF16)
        x1_ref[...] = x_ref[...] + _nn(ymix_ref[...], w_ref[...])

    row = lambda width: pl.BlockSpec((tt, width), lambda i: (i, 0))
    return pl.pallas_call(
        body, name="outproj_fwd", grid=(T // tt,),
        in_specs=[row(D), row(POOL_WIDTH), row(SG_WIDTH), row(SB_WIDTH), pl.BlockSpec((D, D), lambda i: (0, 0))],
        out_specs=[row(D), row(D)],
        out_shape=[jax.ShapeDtypeStruct((T, D), F32), jax.ShapeDtypeStruct((T, D), BF16)],
        compiler_params=_params(),
    )(x, ya, yb, yc, w)


def _nt_matmul(a, w):
    T, N = a.shape
    K = w.shape[0]
    tt = _tile(T, 512)

    def body(a_ref, w_ref, o_ref):
        o_ref[...] = _nt(a_ref[...].astype(BF16), w_ref[...])

    return pl.pallas_call(
        body, name="nt_matmul", grid=(T // tt,),
        in_specs=[pl.BlockSpec((tt, N), lambda i: (i, 0)), pl.BlockSpec((K, N), lambda i: (0, 0))],
        out_specs=pl.BlockSpec((tt, K), lambda i: (i, 0)),
        out_shape=jax.ShapeDtypeStruct((T, K), F32),
        compiler_params=_params(),
    )(a, w)


def _tn_matmul(a, b, name, n_split=1):
    T, K = a.shape
    N = b.shape[1]
    tk = _tile(K, 1024)
    tn = _tile(N // n_split, 1024)
    tt = _tile(T, 512)
    nper = N // n_split // tn
    nt = T // tt

    def body(a_ref, b_ref, o_ref):
        @pl.when(pl.program_id(2) == 0)
        def _():
            o_ref[...] = jnp.zeros_like(o_ref)

        o_ref[...] += _tn(a_ref[...], b_ref[...].astype(BF16))

    return pl.pallas_call(
        body, name=name, grid=(K // tk, N // tn, nt),
        in_specs=[pl.BlockSpec((tt, tk), lambda k, n, t: (t, k)), pl.BlockSpec((tt, tn), lambda k, n, t: (t, n))],
        out_specs=pl.BlockSpec((None, tk, tn), lambda k, n, t: (n // nper, k, n % nper)),
        out_shape=jax.ShapeDtypeStruct((n_split, K, N // n_split), F32),
        compiler_params=_params(),
    )(a, b)


def _mlp_fwd(x, g, w_up, w_down):
    T, D = x.shape
    F = w_up.shape[1]
    tt = _tile(T, 1024)
    fc = _tile(F, 512)
    nc = F // fc

    def body(x_ref, g_ref, wu_ref, wd_ref, y_ref, h_ref, u_ref, a_ref):
        c = pl.program_id(1)

        @pl.when(c == 0)
        def _():
            xv = x_ref[...]
            h, _, _ = _rms_fwd(xv, g_ref[...])
            h_ref[...] = h.astype(BF16)
            y_ref[...] = xv

        u = _nn(h_ref[...], wu_ref[...])
        u_ref[...] = u.astype(BF16)
        a = jnp.square(jnp.maximum(u, 0.0)).astype(BF16)
        a_ref[...] = a
        y_ref[...] += _nn(a, wd_ref[...])

    return pl.pallas_call(
        body, name="mlp_fwd", grid=(T // tt, nc),
        in_specs=[pl.BlockSpec((tt, D), lambda i, c: (i, 0)), pl.BlockSpec((1, D), lambda i, c: (0, 0)),
                  pl.BlockSpec((D, fc), lambda i, c: (0, c)), pl.BlockSpec((fc, D), lambda i, c: (c, 0))],
        out_specs=[pl.BlockSpec((tt, D), lambda i, c: (i, 0)), pl.BlockSpec((tt, D), lambda i, c: (i, 0)),
                   pl.BlockSpec((tt, fc), lambda i, c: (i, c)), pl.BlockSpec((tt, fc), lambda i, c: (i, c))],
        out_shape=[jax.ShapeDtypeStruct((T, D), F32), jax.ShapeDtypeStruct((T, D), BF16),
                   jax.ShapeDtypeStruct((T, F), BF16), jax.ShapeDtypeStruct((T, F), BF16)],
        compiler_params=_params(),
    )(x, g, w_up, w_down)


def _mlp_bwd(dy, x, g, u, w_up, w_down):
    T, D = x.shape
    F = w_up.shape[1]
    tt = _tile(T, 1024)
    fc = _tile(F, 512)
    nc = F // fc

    def body(dy_ref, x_ref, g_ref, u_ref, wu_ref, wd_ref, dx_ref, du_ref, dg_ref, dyb_ref, dh_ref):
        i, c = pl.program_id(0), pl.program_id(1)

        @pl.when(c == 0)
        def _():
            dyb_ref[...] = dy_ref[...].astype(BF16)
            dh_ref[...] = jnp.zeros_like(dh_ref)

        @pl.when(jnp.logical_and(i == 0, c == 0))
        def _():
            dg_ref[...] = jnp.zeros_like(dg_ref)

        da = _nt(dyb_ref[...], wd_ref[...])
        du = (da * (2.0 * jnp.maximum(u_ref[...].astype(F32), 0.0))).astype(BF16)
        du_ref[...] = du
        dh_ref[...] += _nt(du, wu_ref[...])

        @pl.when(c == nc - 1)
        def _():
            gv = g_ref[...]
            _, xhat, r = _rms_fwd(x_ref[...], gv)
            dx, dgrow = _rms_bwd(dh_ref[...], xhat, r, gv)
            dx_ref[...] = dy_ref[...] + dx
            dg_ref[...] += jnp.sum(dgrow, axis=0, keepdims=True)

    return pl.pallas_call(
        body, name="mlp_bwd", grid=(T // tt, nc),
        in_specs=[pl.BlockSpec((tt, D), lambda i, c: (i, 0)), pl.BlockSpec((tt, D), lambda i, c: (i, 0)),
                  pl.BlockSpec((1, D), lambda i, c: (0, 0)), pl.BlockSpec((tt, fc), lambda i, c: (i, c)),
                  pl.BlockSpec((D, fc), lambda i, c: (0, c)), pl.BlockSpec((fc, D), lambda i, c: (c, 0))],
        out_specs=[pl.BlockSpec((tt, D), lambda i, c: (i, 0)), pl.BlockSpec((tt, fc), lambda i, c: (i, c)),
                   pl.BlockSpec((1, D), lambda i, c: (0, 0))],
        out_shape=[jax.ShapeDtypeStruct((T, D), F32), jax.ShapeDtypeStruct((T, F), BF16),
                   jax.ShapeDtypeStruct((1, D), F32)],
        scratch_shapes=[pltpu.VMEM((tt, D), BF16), pltpu.VMEM((tt, D), F32)],
        compiler_params=_params(),
    )(dy, x, g, u, w_up, w_down)


def _loss_head(x, g, target):
    T, D = x.shape
    tt = _tile(T, 512)

    def body(x_ref, g_ref, t_ref, loss_ref, dx_ref, dg_ref):
        gv = g_ref[...]
        y, xhat, r = _rms_fwd(x_ref[...], gv)
        err = y - t_ref[...]
        dx, dgrow = _rms_bwd(err * (1.0 / D), xhat, r, gv)
        dx_ref[...] = dx

        @pl.when(pl.program_id(0) == 0)
        def _():
            loss_ref[...] = jnp.zeros_like(loss_ref)
            dg_ref[...] = jnp.zeros_like(dg_ref)

        loss_ref[...] += 0.5 * jnp.sum(jnp.mean(err * err, axis=-1, keepdims=True), axis=0, keepdims=True)
        dg_ref[...] += jnp.sum(dgrow, axis=0, keepdims=True)

    return pl.pallas_call(
        body, name="loss_head", grid=(T // tt,),
        in_specs=[pl.BlockSpec((tt, D), lambda i: (i, 0)), pl.BlockSpec((1, D), lambda i: (0, 0)),
                  pl.BlockSpec((tt, D), lambda i: (i, 0))],
        out_specs=[pl.BlockSpec((1, LANES), lambda i: (0, 0)), pl.BlockSpec((tt, D), lambda i: (i, 0)),
                   pl.BlockSpec((1, D), lambda i: (0, 0))],
        out_shape=[jax.ShapeDtypeStruct((1, LANES), F32), jax.ShapeDtypeStruct((T, D), F32),
                   jax.ShapeDtypeStruct((1, D), F32)],
        compiler_params=_params(),
    )(x, g, target)


def _rows(shape, pref=512):
    last = shape[-1]
    rows = 1
    for s in shape[:-1]:
        rows *= s
    tr = rows
    if rows * last > 256 * 1024:
        for cand in (pref, 256, 128, 64, 32, 16, 8):
            if rows % cand == 0:
                tr = cand
                break
    return rows, last, tr


def _elementwise(fn, name, ins, n_out, out_dtype=F32):
    shape = ins[0].shape
    rows, last, tr = _rows(shape)
    flat = [a.reshape(rows, last) for a in ins]
    n_in = len(ins)

    def body(*refs):
        res = fn(*[r[...] for r in refs[:n_in]])
        if n_out == 1:
            res = (res,)
        for r, v in zip(refs[n_in:], res):
            r[...] = v.astype(r.dtype)

    spec = pl.BlockSpec((tr, last), lambda i: (i, 0))
    outs = pl.pallas_call(
        body, name=name, grid=(rows // tr,),
        in_specs=[spec] * n_in, out_specs=[spec] * n_out,
        out_shape=[jax.ShapeDtypeStruct((rows, last), out_dtype)] * n_out,
        compiler_params=_params(),
    )(*flat)
    return [o.reshape(shape) for o in outs]


def _adamw(w, g, m, v):
    m = ADAM_B1 * m + (1.0 - ADAM_B1) * g
    v = ADAM_B2 * v + (1.0 - ADAM_B2) * jnp.square(g)
    m_hat = m / (1.0 - ADAM_B1 ** ADAM_STEP)
    v_hat = v / (1.0 - ADAM_B2 ** ADAM_STEP)
    delta = -ADAM_LR * (m_hat / (jnp.sqrt(v_hat) + ADAM_EPS) + ADAM_WD * w)
    return delta, m, v


def _place():
    x, y, c = lax.axis_index("x"), lax.axis_index("y"), lax.axis_index("c")
    chips = [(1 - x, y), (x, 1 - y), (1 - x, 1 - y)]
    return x, y, c, chips


def _remote(src, dst, ssem, rsem, k, dev):
    return pltpu.make_async_remote_copy(src_ref=src, dst_ref=dst, send_sem=ssem.at[k], recv_sem=rsem.at[k],
                                        device_id=dev, device_id_type=MESH)


def _gather_weights(shards):
    n = len(shards)
    halves = [s.shape[1] // 2 for s in shards]

    def body(*refs):
        src, out = refs[:n], refs[n:2 * n]
        ssem, rsem = refs[2 * n:]
        x, y, c, chips = _place()
        me_q = 2 * x + y
        sib = (x, y, 1 - c)

        def half(a, q, cc):
            return out[a].at[q, :, pl.ds(cc * halves[a], halves[a]), :]

        first = []
        for a in range(n):
            mine = src[a].at[:, pl.ds(c * halves[a], halves[a]), :]
            for r, chip in enumerate(chips):
                first.append(_remote(mine, half(a, me_q, c), ssem, rsem, a * 3 + r, (*chip, c)))
        for cp in first:
            cp.start()
        passed = []
        for a in range(n):
            for r, chip in enumerate(chips):
                q = 2 * chip[0] + chip[1]
                k = a * 3 + r
                _remote(half(a, q, c), half(a, q, c), ssem, rsem, k, (*chip, c)).wait_recv()
                cp = _remote(half(a, q, c), half(a, q, c), ssem, rsem, 3 * n + k, sib)
                cp.start()
                passed.append(cp)
        for a in range(n):
            for r, chip in enumerate(chips):
                q = 2 * chip[0] + chip[1]
                _remote(half(a, q, 1 - c), half(a, q, 1 - c), ssem, rsem, 3 * n + a * 3 + r, sib).wait_recv()
        for cp in first + passed:
            cp.wait_send()

    return pl.pallas_call(
        body, name="gather_weights",
        in_specs=[HBM_SPEC] * n, out_specs=[HBM_SPEC] * n,
        out_shape=[jax.ShapeDtypeStruct((N_CHIPS,) + s.shape, s.dtype) for s in shards],
        scratch_shapes=[pltpu.SemaphoreType.DMA((6 * n,)), pltpu.SemaphoreType.DMA((6 * n,))],
        compiler_params=_params(has_side_effects=True),
    )(*shards)


def _swap_halves(grads):
    n = len(grads)
    halves = [g.shape[1] // 2 for g in grads]

    def body(*refs):
        src, out = refs[:n], refs[n:2 * n]
        ssem, rsem = refs[2 * n:]
        x, y, c, _ = _place()
        cps = [_remote(src[a].at[:, pl.ds((1 - c) * halves[a], halves[a]), :], out[a], ssem, rsem, a, (x, y, 1 - c))
               for a in range(n)]
        for cp in cps:
            cp.start()
        for cp in cps:
            cp.wait()

    return pl.pallas_call(
        body, name="swap_halves",
        in_specs=[HBM_SPEC] * n, out_specs=[HBM_SPEC] * n,
        out_shape=[jax.ShapeDtypeStruct((N_CHIPS, h, g.shape[2]), F32) for g, h in zip(grads, halves)],
        scratch_shapes=[pltpu.SemaphoreType.DMA((n,)), pltpu.SemaphoreType.DMA((n,))],
        compiler_params=_params(has_side_effects=True),
    )(*grads)


def _scatter_chips(parts):
    n = len(parts)

    def body(*refs):
        src, out = refs[:n], refs[n:2 * n]
        ssem, rsem = refs[2 * n:]
        x, y, c, chips = _place()
        cps = []
        for a in range(n):
            for r, chip in enumerate(chips):
                cps.append(_remote(src[a].at[2 * chip[0] + chip[1]], out[a].at[r], ssem, rsem, a * 3 + r, (*chip, c)))
        for cp in cps:
            cp.start()
        for cp in cps:
            cp.wait()

    return pl.pallas_call(
        body, name="scatter_chips",
        in_specs=[HBM_SPEC] * n, out_specs=[HBM_SPEC] * n,
        out_shape=[jax.ShapeDtypeStruct((3,) + p.shape[1:], F32) for p in parts],
        scratch_shapes=[pltpu.SemaphoreType.DMA((3 * n,)), pltpu.SemaphoreType.DMA((3 * n,))],
        compiler_params=_params(has_side_effects=True),
    )(*parts)


def _swap_reduced(reduced):
    n = len(reduced)

    def body(*refs):
        src, out = refs[:n], refs[n:2 * n]
        ssem, rsem = refs[2 * n:]
        x, y, c, _ = _place()
        cps = [_remote(src[a], out[a], ssem, rsem, a, (x, y, 1 - c)) for a in range(n)]
        for cp in cps:
            cp.start()
        for cp in cps:
            cp.wait()

    return pl.pallas_call(
        body, name="swap_reduced",
        in_specs=[HBM_SPEC] * n, out_specs=[HBM_SPEC] * n,
        out_shape=[jax.ShapeDtypeStruct(r.shape, F32) for r in reduced],
        scratch_shapes=[pltpu.SemaphoreType.DMA((n,)), pltpu.SemaphoreType.DMA((n,))],
        compiler_params=_params(has_side_effects=True),
    )(*reduced)


def _allreduce_small(buf):
    R, L = buf.shape

    def body(buf_ref, out_ref, pair_ref, chip_ref, ssem, rsem):
        x, y, c, chips = _place()
        me_q = 2 * x + y
        pair_ref[c] = buf_ref[...]
        to_sib = _remote(buf_ref, pair_ref.at[c], ssem, rsem, 0, (x, y, 1 - c))
        to_sib.start()
        _remote(buf_ref, pair_ref.at[1 - c], ssem, rsem, 0, (x, y, 1 - c)).wait_recv()
        chip_ref[me_q] = pair_ref[0] + pair_ref[1]
        cps = [_remote(chip_ref.at[me_q], chip_ref.at[me_q], ssem, rsem, 1 + r, (*chip, c))
               for r, chip in enumerate(chips)]
        for cp in cps:
            cp.start()
        for r, chip in enumerate(chips):
            q = 2 * chip[0] + chip[1]
            _remote(chip_ref.at[q], chip_ref.at[q], ssem, rsem, 1 + r, (*chip, c)).wait_recv()
        out_ref[...] = (chip_ref[0] + chip_ref[1]) + (chip_ref[2] + chip_ref[3])
        to_sib.wait_send()
        for cp in cps:
            cp.wait_send()

    return pl.pallas_call(
        body, name="allreduce_small",
        in_specs=[VMEM_SPEC], out_specs=VMEM_SPEC,
        out_shape=jax.ShapeDtypeStruct((R, L), F32),
        scratch_shapes=[pltpu.VMEM((2, R, L), F32), pltpu.VMEM((N_CHIPS, R, L), F32),
                        pltpu.SemaphoreType.DMA((4,)), pltpu.SemaphoreType.DMA((4,))],
        compiler_params=_params(has_side_effects=True),
    )(buf)


def _pack(arrays):
    flat = jnp.concatenate([a.reshape(-1) for a in arrays])
    pad = (-flat.shape[0]) % (8 * LANES)
    return jnp.pad(flat, (0, pad)).reshape(-1, LANES)


def _unpack(buf, like):
    flat = buf.reshape(-1)
    out, off = [], 0
    for a in like:
        out.append(flat[off:off + a.size].reshape(a.shape))
        off += a.size
    return out


def _block_diag(pw):
    rows = []
    for gi in range(len(POOL_WINDOWS)):
        blocks = [pw[gi] if gj == gi else jnp.zeros_like(pw[gi]) for gj in range(len(POOL_WINDOWS))]
        rows.append(jnp.concatenate(blocks, axis=1))
    return jnp.concatenate(rows, axis=0)


def kernel(x, norm1, w_in, pool_w, pool_scale, sg_norm, sg_w, sg_b, w_out, norm2, w_up, w_down, final_norm, loss_target, m_norm1, m_w_in, m_pool_w, m_pool_scale, m_sg_norm, m_sg_w, m_sg_b, m_w_out, m_norm2, m_w_up, m_w_down, m_final_norm, v_norm1, v_w_in, v_pool_w, v_pool_scale, v_sg_norm, v_sg_w, v_sg_b, v_w_out, v_norm2, v_w_up, v_w_down, v_final_norm):
    depth = norm1.shape[0]
    T = x.shape[1]
    xs = x.reshape(T, D_MODEL)
    target = loss_target.reshape(T, D_MODEL)

    c_idx = lax.axis_index("c")
    q_idx = 2 * lax.axis_index("x") + lax.axis_index("y")
    own = [w.astype(BF16) for w in (w_in, w_out, w_up, w_down)]
    gathered = _gather_weights(own)

    def full(a, l, axis):
        blocks = [jnp.where(q_idx == q, own[a][l], gathered[a][q, l]) for q in range(N_CHIPS)]
        return jnp.concatenate(blocks, axis=axis)

    wi = [full(0, l, 1) for l in range(depth)]
    wo = [full(1, l, 0) for l in range(depth)]
    wu = [full(2, l, 1) for l in range(depth)]
    wd = [full(3, l, 0) for l in range(depth)]

    tril = jnp.tril(jnp.ones((CHUNK, CHUNK), F32))
    saved = []
    cur = xs
    for l in range(depth):
        wbd = _block_diag(pool_w[l]).astype(BF16)
        wm = sg_w[l] * tril
        wm_s = wm.reshape(SG_HEADS * CHUNK, CHUNK).astype(BF16)
        wmt_s = jnp.swapaxes(wm, 1, 2).reshape(SG_HEADS * CHUNK, CHUNK).astype(BF16)
        bias = jnp.repeat(sg_b[l].T, SB_HD, axis=1)
        n1, n2 = norm1[l][None], norm2[l][None]
        psc, sgn = pool_scale[l][None], sg_norm[l][None]
        proj, h, qkv = _inproj_fwd(cur, n1, wi[l])
        ya = _pool_fwd(proj, wbd, psc)
        yb = _sg_fwd(proj, wm_s, bias, sgn)
        yc = _attn_fwd(qkv)
        x1, ymix = _outproj_fwd(cur, ya, yb, yc, wo[l])
        x2, h2, u, act = _mlp_fwd(x1, n2, wu[l], wd[l])
        saved.append(dict(x0=cur, x1=x1, proj=proj, h=h, qkv=qkv, yc=yc, ymix=ymix, h2=h2, u=u, act=act,
                          wbd=wbd, wm_s=wm_s, wmt_s=wmt_s, bias=bias, n1=n1, n2=n2, psc=psc, sgn=sgn))
        cur = x2

    loss_row, dcur, d_final = _loss_head(cur, final_norm[None], target)
    loss = lax.psum(loss_row[0, 0], AXES)

    big_grads = [None] * depth
    small = [None] * depth
    for l in reversed(range(depth)):
        s = saved[l]
        dx1, du, d_n2 = _mlp_bwd(dcur, s["x1"], s["n2"], s["u"], wu[l], wd[l])
        g_up_l = _tn_matmul(s["h2"], du, "grad_w_up", n_split=N_CHIPS)
        g_down_l = _tn_matmul(s["act"], dcur, "grad_w_down")[0].reshape(N_CHIPS, D_FF // N_CHIPS, D_MODEL)
        dymix = _nt_matmul(dx1, wo[l])
        g_out_l = _tn_matmul(s["ymix"], dx1, "grad_w_out")[0].reshape(N_CHIPS, D_MODEL // N_CHIPS, D_MODEL)
        da_in, d_wbd, d_psc = _pool_bwd(s["proj"], dymix, s["wbd"], s["psc"])
        du_pre, dv_pre, d_wm, d_bias, d_sgn = _sg_bwd(s["proj"], dymix, s["wm_s"], s["wmt_s"], s["bias"], s["sgn"])
        dq, dk, dv = _attn_bwd(s["qkv"], s["yc"], dymix)
        dproj, dx0, d_n1 = _inproj_bwd([da_in, du_pre, dv_pre, dq, dk, dv], wi[l], s["x0"], s["n1"], dx1)
        g_in_l = _tn_matmul(s["h"], dproj, "grad_w_in")[0]
        g_in_l = g_in_l.reshape(D_MODEL, N_CHIPS, IN_COLS // N_CHIPS).transpose(1, 0, 2)
        big_grads[l] = [g_in_l, g_out_l, g_up_l, g_down_l]
        d_pw = jnp.stack([d_wbd[gi * POOL_GW:(gi + 1) * POOL_GW, gi * POOL_GW:(gi + 1) * POOL_GW]
                          for gi in range(len(POOL_WINDOWS))])
        small[l] = dict(norm1=d_n1[0], pool_w=d_pw, pool_scale=d_psc[0], sg_norm=d_sgn[0],
                        sg_w=d_wm.reshape(SG_HEADS, CHUNK, CHUNK), sg_b=d_bias[:, :SG_HEADS].T, norm2=d_n2[0])
        dcur = dx0
    grad_x = dcur.reshape(x.shape)

    flat = [g for l in range(depth) for g in big_grads[l]]
    got = _swap_halves(flat)
    parts = []
    for g, o in zip(flat, got):
        h = o.shape[1]
        mine = lax.dynamic_slice_in_dim(g, c_idx * h, h, axis=1)
        parts.append(_elementwise(lambda a, b: a + b, "add_pair", [mine, o], 1)[0])
    arrived = _scatter_chips(parts)
    reduced = []
    for p, r in zip(parts, arrived):
        own = lax.dynamic_index_in_dim(p, q_idx, axis=0, keepdims=False)
        reduced.append(_elementwise(lambda a, b, c, d: (a + b) + (c + d), "add_chips", [own, r[0], r[1], r[2]], 1)[0])
    theirs = _swap_reduced(reduced)

    def joined(a):
        layers = []
        for l in range(depth):
            mine, other = reduced[4 * l + a], theirs[4 * l + a]
            layers.append(jnp.where(c_idx == 0, jnp.concatenate([mine, other]), jnp.concatenate([other, mine])))
        return jnp.stack(layers)

    gw_in, gw_out, gw_up, gw_down = [joined(a) for a in range(4)]

    names = ["norm1", "pool_w", "pool_scale", "sg_norm", "sg_w", "sg_b", "norm2"]
    small_w = [norm1, pool_w, pool_scale, sg_norm, sg_w, sg_b, norm2, final_norm]
    small_m = [m_norm1, m_pool_w, m_pool_scale, m_sg_norm, m_sg_w, m_sg_b, m_norm2, m_final_norm]
    small_v = [v_norm1, v_pool_w, v_pool_scale, v_sg_norm, v_sg_w, v_sg_b, v_norm2, v_final_norm]
    small_g = [jnp.stack([small[l][k] for l in range(depth)]) for k in names] + [d_final[0]]
    g_packed = _allreduce_small(_pack(small_g))
    s_delta, s_m, s_v = _elementwise(_adamw, "adamw_small", [_pack(small_w), g_packed, _pack(small_m), _pack(small_v)], 3)
    gs = dict(zip(names + ["final_norm"], _unpack(g_packed, small_w)))
    ds = dict(zip(names + ["final_norm"], _unpack(s_delta, small_w)))
    ms = dict(zip(names + ["final_norm"], _unpack(s_m, small_w)))
    vs = dict(zip(names + ["final_norm"], _unpack(s_v, small_w)))

    big_g = dict(w_in=gw_in, w_out=gw_out, w_up=gw_up, w_down=gw_down)
    big_w = dict(w_in=(w_in, m_w_in, v_w_in), w_out=(w_out, m_w_out, v_w_out),
                 w_up=(w_up, m_w_up, v_w_up), w_down=(w_down, m_w_down, v_w_down))
    for k, (w, m, v) in big_w.items():
        ds[k], ms[k], vs[k] = _elementwise(_adamw, "adamw_" + k, [w, big_g[k], m, v], 3)
        gs[k] = big_g[k]

    order = ["norm1", "w_in", "pool_w", "pool_scale", "sg_norm", "sg_w", "sg_b", "w_out", "norm2", "w_up", "w_down",
             "final_norm"]
    return (loss, grad_x, *[gs[k] for k in order], *[ds[k] for k in order], *[ms[k] for k in order],
            *[vs[k] for k in order])
```

```python
import functools

import jax
import jax.numpy as jnp
from jax import lax
from jax.experimental import pallas as pl
from jax.experimental.pallas import tpu as pltpu

F32 = jnp.float32
BF16 = jnp.bfloat16
MESH = pl.DeviceIdType.MESH
AXES = ("x", "y", "c")

EPS = 1e-6
D_MODEL = 1024
POOL_WIDTH = 256
SG_WIDTH = 256
SB_WIDTH = 512
POOL_WINDOWS = (2, 4, 8, 16)
POOL_GW = 64
POOL_HALO = 16
CHUNK = 128
SG_HEADS = 4
SB_HD = 64
IN_COLS = 2304
QKV_OFF = 768
D_FF = 4096
N_CHIPS = 4
LANES = 128
VMEM_LIMIT = 56 * 1024 * 1024
ATTN_TILE = 256
UNDERFLOW = -104.0

ADAM_LR = 0.001
ADAM_B1 = 0.9
ADAM_B2 = 0.999
ADAM_EPS = 1e-08
ADAM_WD = 0.01
ADAM_STEP = 10

HBM_SPEC = pl.BlockSpec(memory_space=pl.ANY)
VMEM_SPEC = pl.BlockSpec(memory_space=pltpu.VMEM)


def _params(**kw):
    return pltpu.CompilerParams(vmem_limit_bytes=VMEM_LIMIT, **kw)


def _tile(n, pref):
    if n <= pref:
        return n
    for t in range(pref - pref % LANES, 0, -LANES):
        if n % t == 0:
            return t
    raise ValueError((n, pref))


def _nn(a, b):
    return jnp.dot(a, b, preferred_element_type=F32)


def _nt(a, b):
    return lax.dot_general(a, b, (((1,), (1,)), ((), ())), preferred_element_type=F32)


def _tn(a, b):
    return lax.dot_general(a, b, (((0,), (0,)), ((), ())), preferred_element_type=F32)


def _rms_fwd(x, g):
    r = lax.rsqrt(jnp.mean(x * x, axis=-1, keepdims=True) + EPS)
    xhat = x * r
    return xhat * g, xhat, r


def _rms_bwd(dy, xhat, r, g):
    dxhat = dy * g
    dx = r * (dxhat - xhat * jnp.mean(dxhat * xhat, axis=-1, keepdims=True))
    return dx, dy * xhat


_GELU_K = 0.7978845608028654
_GELU_C = 0.044715


def _gelu(x):
    return 0.5 * x * (1.0 + jnp.tanh(_GELU_K * (x + _GELU_C * x * x * x)))


def _gelu_grad(x):
    t = jnp.tanh(_GELU_K * (x + _GELU_C * x * x * x))
    return 0.5 * (1.0 + t) + 0.5 * x * (1.0 - t * t) * _GELU_K * (1.0 + 3.0 * _GELU_C * x * x)


def _inproj_fwd(x, g, w):
    T, D = x.shape
    N = w.shape[1]
    tt = _tile(T, 512)

    def body(x_ref, g_ref, w_ref, proj_ref, h_ref, qkv_ref):
        h, _, _ = _rms_fwd(x_ref[...], g_ref[...])
        hb = h.astype(BF16)
        h_ref[...] = hb
        p = _nn(hb, w_ref[...])
        proj_ref[...] = p
        qkv_ref[...] = p[:, QKV_OFF:].astype(BF16)

    return pl.pallas_call(
        body, name="inproj_fwd", grid=(T // tt,),
        in_specs=[pl.BlockSpec((tt, D), lambda i: (i, 0)), pl.BlockSpec((1, D), lambda i: (0, 0)),
                  pl.BlockSpec((D, N), lambda i: (0, 0))],
        out_specs=[pl.BlockSpec((tt, N), lambda i: (i, 0)), pl.BlockSpec((tt, D), lambda i: (i, 0)),
                   pl.BlockSpec((tt, N - QKV_OFF), lambda i: (i, 0))],
        out_shape=[jax.ShapeDtypeStruct((T, N), F32), jax.ShapeDtypeStruct((T, D), BF16),
                   jax.ShapeDtypeStruct((T, N - QKV_OFF), BF16)],
        compiler_params=_params(),
    )(x, g, w)


def _inproj_bwd(pieces, w, x, g, dres):
    T, D = x.shape
    N = w.shape[1]
    tt = _tile(T, 512)
    widths = [p.shape[1] for p in pieces]
    offs = [sum(widths[:k]) for k in range(len(widths))]
    assert sum(widths) == N
    n_p = len(pieces)

    def body(*refs):
        p_refs = refs[:n_p]
        w_ref, x_ref, g_ref, dres_ref, dproj_ref, dx_ref, dg_ref = refs[n_p:]
        for p_ref, o, wd in zip(p_refs, offs, widths):
            dproj_ref[:, o:o + wd] = p_ref[...].astype(BF16)
        dh = _nt(dproj_ref[...], w_ref[...])
        gv = g_ref[...]
        _, xhat, r = _rms_fwd(x_ref[...], gv)
        dx, dgrow = _rms_bwd(dh, xhat, r, gv)
        dx_ref[...] = dres_ref[...] + dx

        @pl.when(pl.program_id(0) == 0)
        def _():
            dg_ref[...] = jnp.zeros_like(dg_ref)

        dg_ref[...] += jnp.sum(dgrow, axis=0, keepdims=True)

    return pl.pallas_call(
        body, name="inproj_bwd", grid=(T // tt,),
        in_specs=[pl.BlockSpec((tt, wd), lambda i: (i, 0)) for wd in widths] + [
            pl.BlockSpec((D, N), lambda i: (0, 0)), pl.BlockSpec((tt, D), lambda i: (i, 0)),
            pl.BlockSpec((1, D), lambda i: (0, 0)), pl.BlockSpec((tt, D), lambda i: (i, 0))],
        out_specs=[pl.BlockSpec((tt, N), lambda i: (i, 0)), pl.BlockSpec((tt, D), lambda i: (i, 0)),
                   pl.BlockSpec((1, D), lambda i: (0, 0))],
        out_shape=[jax.ShapeDtypeStruct((T, N), BF16), jax.ShapeDtypeStruct((T, D), F32),
                   jax.ShapeDtypeStruct((1, D), F32)],
        compiler_params=_params(),
    )(*pieces, w, x, g, dres)


def _pool_select(s2, s4, s8, s16, grp):
    return jnp.where(grp == 0, s2, jnp.where(grp == 1, s4, jnp.where(grp == 2, s8, s16)))


def _pool_count(t_glob, grp):
    win = jnp.where(grp == 0, 2, jnp.where(grp == 1, 4, jnp.where(grp == 2, 8, 16)))
    return jnp.minimum(t_glob + 1, win).astype(F32)


def _pool_diff(a, halo, base, tt):
    n = tt + POOL_HALO
    ext = jnp.concatenate([halo, a], axis=0)
    s2 = ext + pltpu.roll(ext, 1, 0)
    s4 = s2 + pltpu.roll(s2, 2, 0)
    s8 = s4 + pltpu.roll(s4, 4, 0)
    s16 = s8 + pltpu.roll(s8, 8, 0)
    grp = lax.broadcasted_iota(jnp.int32, (n, POOL_WIDTH), 1) // POOL_GW
    t_glob = lax.broadcasted_iota(jnp.int32, (n, POOL_WIDTH), 0) + (base - POOL_HALO)
    pooled = _pool_select(s2, s4, s8, s16, grp) / _pool_count(t_glob, grp)
    return pooled[POOL_HALO:] - a


def _pool_specs(T, tt):
    hb = tt // POOL_HALO
    return [pl.BlockSpec((tt, POOL_WIDTH), lambda i: (i, 0)),
            pl.BlockSpec((POOL_HALO, POOL_WIDTH), lambda i: (jnp.maximum(i * hb - 1, 0), 0))]


def _pool_fwd(proj, wbd, scale):
    T = proj.shape[0]
    tt = _tile(T, 512)

    def body(a_ref, halo_ref, w_ref, sc_ref, y_ref):
        i = pl.program_id(0)
        halo = jnp.where(i > 0, halo_ref[...], 0.0)
        d = _pool_diff(a_ref[...], halo, i * tt, tt)
        y_ref[...] = _nn(d.astype(BF16), w_ref[...]) * sc_ref[...]

    return pl.pallas_call(
        body, name="pool_fwd", grid=(T // tt,),
        in_specs=_pool_specs(T, tt) + [pl.BlockSpec((POOL_WIDTH, POOL_WIDTH), lambda i: (0, 0)),
                                       pl.BlockSpec((1, POOL_WIDTH), lambda i: (0, 0))],
        out_specs=pl.BlockSpec((tt, POOL_WIDTH), lambda i: (i, 0)),
        out_shape=jax.ShapeDtypeStruct((T, POOL_WIDTH), F32),
        compiler_params=_params(),
    )(proj, proj, wbd, scale)


def _pool_bwd(proj, dymix, wbd, scale):
    T = proj.shape[0]
    tt = _tile(T, 512)
    hb = tt // POOL_HALO
    nblk = T // tt
    n = tt + POOL_HALO

    def body(a_ref, halo_ref, dy_ref, dyn_ref, w_ref, sc_ref, da_ref, dw_ref, dsc_ref):
        i = pl.program_id(0)
        halo = jnp.where(i > 0, halo_ref[...], 0.0)
        d = _pool_diff(a_ref[...], halo, i * tt, tt)
        db = d.astype(BF16)
        wv = w_ref[...]
        sc = sc_ref[...]
        dy = dy_ref[...]
        dys = dy * sc

        @pl.when(i == 0)
        def _():
            dw_ref[...] = jnp.zeros_like(dw_ref)
            dsc_ref[...] = jnp.zeros_like(dsc_ref)

        dsc_ref[...] += jnp.sum(dy * _nn(db, wv), axis=0, keepdims=True)
        dw_ref[...] += _tn(db, dys.astype(BF16))
        dyn = jnp.where(i < nblk - 1, dyn_ref[...], 0.0) * sc
        dd = _nt(jnp.concatenate([dys, dyn], axis=0).astype(BF16), wv)
        grp = lax.broadcasted_iota(jnp.int32, (n, POOL_WIDTH), 1) // POOL_GW
        t_glob = lax.broadcasted_iota(jnp.int32, (n, POOL_WIDTH), 0) + i * tt
        e = dd / _pool_count(t_glob, grp)
        r2 = e + pltpu.roll(e, n - 1, 0)
        r4 = r2 + pltpu.roll(r2, n - 2, 0)
        r8 = r4 + pltpu.roll(r4, n - 4, 0)
        r16 = r8 + pltpu.roll(r8, n - 8, 0)
        da_ref[...] = (_pool_select(r2, r4, r8, r16, grp) - dd)[:tt]

    return pl.pallas_call(
        body, name="pool_bwd", grid=(nblk,),
        in_specs=_pool_specs(T, tt) + [
            pl.BlockSpec((tt, POOL_WIDTH), lambda i: (i, 0)),
            pl.BlockSpec((POOL_HALO, POOL_WIDTH), lambda i: (jnp.minimum((i + 1) * hb, T // POOL_HALO - 1), 0)),
            pl.BlockSpec((POOL_WIDTH, POOL_WIDTH), lambda i: (0, 0)), pl.BlockSpec((1, POOL_WIDTH), lambda i: (0, 0))],
        out_specs=[pl.BlockSpec((tt, POOL_WIDTH), lambda i: (i, 0)),
                   pl.BlockSpec((POOL_WIDTH, POOL_WIDTH), lambda i: (0, 0)),
                   pl.BlockSpec((1, POOL_WIDTH), lambda i: (0, 0))],
        out_shape=[jax.ShapeDtypeStruct((T, POOL_WIDTH), F32),
                   jax.ShapeDtypeStruct((POOL_WIDTH, POOL_WIDTH), F32),
                   jax.ShapeDtypeStruct((1, POOL_WIDTH), F32)],
        compiler_params=_params(),
    )(proj, proj, dymix, dymix, wbd, scale)


def _head_select(stacked, grp):
    out = jnp.where(grp == 0, stacked[0:CHUNK], 0.0)
    for h in range(1, SG_HEADS):
        out = out + jnp.where(grp == h, stacked[h * CHUNK:(h + 1) * CHUNK], 0.0)
    return out


def _sg_specs(tt):
    return [pl.BlockSpec((tt, SG_WIDTH), lambda i: (i, 1)), pl.BlockSpec((tt, SG_WIDTH), lambda i: (i, 2))]


def _sg_fwd(proj, wm, bias, g):
    T = proj.shape[0]
    tt = _tile(T, 512)

    def body(u_ref, v_ref, wm_ref, b_ref, g_ref, y_ref):
        zu = _gelu(u_ref[...])
        vn, _, _ = _rms_fwd(_gelu(v_ref[...]), g_ref[...])
        grp = lax.broadcasted_iota(jnp.int32, (CHUNK, SG_WIDTH), 1) // SB_HD
        for n in range(tt // CHUNK):
            rows = slice(n * CHUNK, (n + 1) * CHUNK)
            sv = _head_select(_nn(wm_ref[...], vn[rows].astype(BF16)), grp) + b_ref[...]
            y_ref[rows, :] = zu[rows] * sv

    return pl.pallas_call(
        body, name="sg_fwd", grid=(T // tt,),
        in_specs=_sg_specs(tt) + [pl.BlockSpec((SG_HEADS * CHUNK, CHUNK), lambda i: (0, 0)),
                                  pl.BlockSpec((CHUNK, SG_WIDTH), lambda i: (0, 0)),
                                  pl.BlockSpec((1, SG_WIDTH), lambda i: (0, 0))],
        out_specs=pl.BlockSpec((tt, SG_WIDTH), lambda i: (i, 0)),
        out_shape=jax.ShapeDtypeStruct((T, SG_WIDTH), F32),
        compiler_params=_params(),
    )(proj, proj, wm, bias, g)


def _sg_bwd(proj, dymix, wm, wmt, bias, g):
    T = proj.shape[0]
    tt = _tile(T, 512)
    nblk = T // tt

    def body(u_ref, v_ref, dy_ref, wm_ref, wmt_ref, b_ref, g_ref,
             du_ref, dv_ref, dw_ref, db_ref, dg_ref, dvn_ref, dbias_ref):
        i = pl.program_id(0)
        up, vp = u_ref[...], v_ref[...]
        gv = g_ref[...]
        zu, zv = _gelu(up), _gelu(vp)
        vn, xhat, r = _rms_fwd(zv, gv)
        gu = _gelu_grad(up)
        grp = lax.broadcasted_iota(jnp.int32, (CHUNK, SG_WIDTH), 1) // SB_HD

        @pl.when(i == 0)
        def _():
            dw_ref[...] = jnp.zeros_like(dw_ref)
            dbias_ref[...] = jnp.zeros_like(dbias_ref)
            dg_ref[...] = jnp.zeros_like(dg_ref)

        for n in range(tt // CHUNK):
            rows = slice(n * CHUNK, (n + 1) * CHUNK)
            vc = vn[rows].astype(BF16)
            sv = _head_select(_nn(wm_ref[...], vc), grp) + b_ref[...]
            dy = dy_ref[rows, :]
            du_ref[rows, :] = dy * sv * gu[rows]
            dsv = dy * zu[rows]
            dsvb = dsv.astype(BF16)
            dvn_ref[rows, :] = _head_select(_nn(wmt_ref[...], dsvb), grp)
            stacked = jnp.concatenate([jnp.where(grp == h, dsv, 0.0) for h in range(SG_HEADS)], axis=0)
            dw_ref[...] += _nt(stacked.astype(BF16), vc)
            dbias_ref[...] += dsv

        dzv, dgrow = _rms_bwd(dvn_ref[...], xhat, r, gv)
        dg_ref[...] += jnp.sum(dgrow, axis=0, keepdims=True)
        dv_ref[...] = dzv * _gelu_grad(vp)

        @pl.when(i == nblk - 1)
        def _():
            t_i = lax.broadcasted_iota(jnp.int32, (SG_HEADS * CHUNK, CHUNK), 0) % CHUNK
            s_i = lax.broadcasted_iota(jnp.int32, (SG_HEADS * CHUNK, CHUNK), 1)
            dw_ref[...] = jnp.where(s_i <= t_i, dw_ref[...], 0.0)
            lane = lax.broadcasted_iota(jnp.int32, (CHUNK, LANES), 1)
            acc = jnp.zeros((CHUNK, LANES), F32)
            for h in range(SG_HEADS):
                tot = jnp.sum(jnp.where(grp == h, dbias_ref[...], 0.0), axis=1, keepdims=True)
                acc = acc + jnp.where(lane == h, tot, 0.0)
            db_ref[...] = acc

    return pl.pallas_call(
        body, name="sg_bwd", grid=(nblk,),
        in_specs=_sg_specs(tt) + [pl.BlockSpec((tt, SG_WIDTH), lambda i: (i, 1)),
                                  pl.BlockSpec((SG_HEADS * CHUNK, CHUNK), lambda i: (0, 0)),
                                  pl.BlockSpec((SG_HEADS * CHUNK, CHUNK), lambda i: (0, 0)),
                                  pl.BlockSpec((CHUNK, SG_WIDTH), lambda i: (0, 0)),
                                  pl.BlockSpec((1, SG_WIDTH), lambda i: (0, 0))],
        out_specs=[pl.BlockSpec((tt, SG_WIDTH), lambda i: (i, 0)), pl.BlockSpec((tt, SG_WIDTH), lambda i: (i, 0)),
                   pl.BlockSpec((SG_HEADS * CHUNK, CHUNK), lambda i: (0, 0)),
                   pl.BlockSpec((CHUNK, LANES), lambda i: (0, 0)), pl.BlockSpec((1, SG_WIDTH), lambda i: (0, 0))],
        out_shape=[jax.ShapeDtypeStruct((T, SG_WIDTH), F32), jax.ShapeDtypeStruct((T, SG_WIDTH), F32),
                   jax.ShapeDtypeStruct((SG_HEADS * CHUNK, CHUNK), F32),
                   jax.ShapeDtypeStruct((CHUNK, LANES), F32), jax.ShapeDtypeStruct((1, SG_WIDTH), F32)],
        scratch_shapes=[pltpu.VMEM((tt, SG_WIDTH), F32), pltpu.VMEM((CHUNK, SG_WIDTH), F32)],
        compiler_params=_params(),
    )(proj, proj, dymix, wm, wmt, bias, g)


def _split_dot(x, u):
    hi = x.astype(BF16)
    lo = (x - hi.astype(F32)).astype(BF16)
    return _nn(hi, u) + _nn(lo, u)


def _sb_logits(qh, kb):
    z = _nt(qh, kb) * (1.0 / 8.0)
    sp = jnp.log1p(jnp.exp(-jnp.abs(z)))
    return jnp.minimum(z, 0.0) - sp, -jnp.maximum(z, 0.0) - sp


def _attn_qkv_specs(tq, T):
    base = (IN_COLS - 3 * SB_WIDTH - QKV_OFF) // LANES
    nb = SB_WIDTH // LANES
    return [pl.BlockSpec((tq, LANES), lambda p, i: (i, base + p)),
            pl.BlockSpec((T, LANES), lambda p, i: (0, base + nb + p)),
            pl.BlockSpec((T, LANES), lambda p, i: (0, base + 2 * nb + p))]


class _Hosted:
    def __init__(self, ins, out_shapes, n_sems, copies):
        self.ins, self.out_shapes, self.n_sems, self.copies = ins, out_shapes, n_sems, copies

    @property
    def n(self):
        return len(self.ins)

    def sems(self):
        return [pltpu.SemaphoreType.DMA((self.n_sems,)), pltpu.SemaphoreType.DMA((self.n_sems,))]

    def start(self, src, dst, ssem, rsem):
        for send, _ in self.copies(src, dst, ssem, rsem):
            send.start()

    def wait(self, src, dst, ssem, rsem):
        for send, recv in self.copies(src, dst, ssem, rsem):
            recv.wait_recv()
            send.wait_send()


def _host(hosted, refs, n_in, n_out, first, last):
    if hosted is None:
        return refs, lambda: None, lambda: None
    n = hosted.n
    own_in, h_in = refs[:n_in], refs[n_in:n_in + n]
    own_out, h_out = refs[n_in + n:n_in + n + n_out], refs[n_in + n + n_out:n_in + 2 * n + n_out]
    rest = refs[n_in + 2 * n + n_out:]
    ssem, rsem = rest[-2:]

    def start():
        @pl.when(first)
        def _():
            hosted.start(h_in, h_out, ssem, rsem)

    def wait():
        @pl.when(last)
        def _():
            hosted.wait(h_in, h_out, ssem, rsem)

    return own_in + own_out + rest[:-2], start, wait


def _attn_fwd(qkv, hosted=None):
    T = qkv.shape[0]
    tq = _tile(T, ATTN_TILE)
    n_p, nq = SB_WIDTH // LANES, T // tq

    def body(*refs):
        p, i = pl.program_id(0), pl.program_id(1)
        (q_ref, k_ref, v_ref, o_ref), start, wait = _host(
            hosted, refs, 3, 1, jnp.logical_and(p == 0, i == 0), jnp.logical_and(p == n_p - 1, i == nq - 1))
        start()
        lane = lax.broadcasted_iota(jnp.int32, (tq, LANES), 1)
        row = lax.broadcasted_iota(jnp.int32, (tq, tq), 0)
        col = lax.broadcasted_iota(jnp.int32, (tq, tq), 1)
        after = jnp.where(row > col, 1.0, 0.0).astype(BF16)
        valid = col < row
        q = q_ref[...].astype(F32)
        qh = [jnp.where((lane // SB_HD) == hh, q, 0.0).astype(BF16) for hh in range(2)]

        def tile(j, state, diag):
            ks = pl.ds(pl.multiple_of(j * tq, tq), tq)
            kb, vb = k_ref[ks, :], v_ref[ks, :]
            new = []
            for hh in range(2):
                carry, acc = state[hh]
                lb, lm = _sb_logits(qh[hh], kb)
                if diag:
                    lm = jnp.where(valid, lm, 0.0)
                a = jnp.exp(lb + _split_dot(lm, after) + carry)
                if diag:
                    a = jnp.where(valid, a, 0.0)
                new.append((carry + jnp.sum(lm, axis=1, keepdims=True), acc + _nn(a.astype(BF16), vb)))
            return tuple(new)

        def live(state):
            return jnp.maximum(jnp.max(state[0][0]), jnp.max(state[1][0]))

        zero = (jnp.zeros((tq, 1), F32), jnp.zeros((tq, LANES), F32))
        state = tile(i, (zero, zero), True)

        def cond(st):
            return jnp.logical_and(st[0] >= 0, st[2] > UNDERFLOW)

        def step(st):
            state = tile(st[0], st[1], False)
            return st[0] - 1, state, live(state)

        _, state, _ = lax.while_loop(cond, step, (i - 1, state, live(state)))
        o_ref[...] = jnp.where(lane < SB_HD, state[0][1], state[1][1])
        wait()

    h_ins = hosted.ins if hosted else []
    res = pl.pallas_call(
        body, name="attn_fwd_hosting" if hosted else "attn_fwd", grid=(n_p, nq),
        in_specs=_attn_qkv_specs(tq, T) + [HBM_SPEC] * len(h_ins),
        out_specs=[pl.BlockSpec((tq, LANES), lambda p, i: (i, p))] + [HBM_SPEC] * len(h_ins),
        out_shape=[jax.ShapeDtypeStruct((T, SB_WIDTH), F32)] + (hosted.out_shapes if hosted else []),
        scratch_shapes=hosted.sems() if hosted else [],
        compiler_params=_params(has_side_effects=hosted is not None),
    )(qkv, qkv, qkv, *h_ins)
    return res[0], res[1:]


def _attn_bwd(qkv, o, dymix, hosted=None):
    T = qkv.shape[0]
    tq = _tile(T, ATTN_TILE)
    n_p, nq = SB_WIDTH // LANES, T // tq
    yc_blk = (POOL_WIDTH + SG_WIDTH) // LANES

    def body(*refs):
        p, i = pl.program_id(0), pl.program_id(1)
        (q_ref, k_ref, v_ref, o_ref, do_ref, dq_ref, dk_ref, dv_ref), start, wait = _host(
            hosted, refs, 5, 3, jnp.logical_and(p == 0, i == 0), jnp.logical_and(p == n_p - 1, i == nq - 1))
        start()
        lane = lax.broadcasted_iota(jnp.int32, (tq, LANES), 1)
        row = lax.broadcasted_iota(jnp.int32, (tq, tq), 0)
        col = lax.broadcasted_iota(jnp.int32, (tq, tq), 1)
        after = jnp.where(row > col, 1.0, 0.0).astype(BF16)
        from_here = jnp.where(row >= col, 1.0, 0.0).astype(BF16)
        valid = col < row

        @pl.when(i == 0)
        def _():
            dk_ref[...] = jnp.zeros_like(dk_ref)
            dv_ref[...] = jnp.zeros_like(dv_ref)

        q = q_ref[...].astype(F32)
        ov = o_ref[...]
        dov = do_ref[...]
        heads = [(lane // SB_HD) == hh for hh in range(2)]
        qh = [jnp.where(h, q, 0.0).astype(BF16) for h in heads]
        dohb = [jnp.where(h, dov, 0.0).astype(BF16) for h in heads]
        delta = [jnp.sum(d.astype(F32) * ov, axis=1, keepdims=True) for d in dohb]

        def tile(j, state, diag):
            ks = pl.ds(pl.multiple_of(j * tq, tq), tq)
            kb, vb = k_ref[ks, :], v_ref[ks, :]
            new, dk, dv = [], None, None
            for hh in range(2):
                c_a, c_r, dqa = state[hh]
                lb, lm = _sb_logits(qh[hh], kb)
                if diag:
                    lm = jnp.where(valid, lm, 0.0)
                a = jnp.exp(lb + _split_dot(lm, after) + c_a)
                if diag:
                    a = jnp.where(valid, a, 0.0)
                ab = a.astype(BF16)
                sig = jnp.exp(lb)
                g = _nt(dohb[hh], vb) * ab.astype(F32)
                left = delta[hh] - (c_r + _split_dot(g, from_here))
                dz = (g * (1.0 - sig) - left * sig) * (1.0 / 8.0)
                if diag:
                    dz = jnp.where(valid, dz, 0.0)
                dzb = dz.astype(BF16)
                dk_h, dv_h = _tn(dzb, qh[hh]), _tn(ab, dohb[hh])
                dk, dv = (dk_h, dv_h) if hh == 0 else (dk + dk_h, dv + dv_h)
                new.append((c_a + jnp.sum(lm, axis=1, keepdims=True), c_r + jnp.sum(g, axis=1, keepdims=True),
                            dqa + _nn(dzb, kb)))
            dk_ref[ks, :] += dk
            dv_ref[ks, :] += dv
            return tuple(new)

        def live(state):
            return jnp.maximum(jnp.max(state[0][0]), jnp.max(state[1][0]))

        zero = (jnp.zeros((tq, 1), F32), jnp.zeros((tq, 1), F32), jnp.zeros((tq, LANES), F32))
        state = tile(i, (zero, zero), True)

        def cond(st):
            return jnp.logical_and(st[0] >= 0, st[2] > UNDERFLOW)

        def step(st):
            state = tile(st[0], st[1], False)
            return st[0] - 1, state, live(state)

        _, state, _ = lax.while_loop(cond, step, (i - 1, state, live(state)))
        dq_ref[...] = jnp.where(lane < SB_HD, state[0][2], state[1][2])
        wait()

    h_ins = hosted.ins if hosted else []
    res = pl.pallas_call(
        body, name="attn_bwd_hosting" if hosted else "attn_bwd", grid=(n_p, nq),
        in_specs=_attn_qkv_specs(tq, T) + [pl.BlockSpec((tq, LANES), lambda p, i: (i, p)),
                                           pl.BlockSpec((tq, LANES), lambda p, i: (i, yc_blk + p))]
        + [HBM_SPEC] * len(h_ins),
        out_specs=[pl.BlockSpec((tq, LANES), lambda p, i: (i, p)), pl.BlockSpec((T, LANES), lambda p, i: (0, p)),
                   pl.BlockSpec((T, LANES), lambda p, i: (0, p))] + [HBM_SPEC] * len(h_ins),
        out_shape=[jax.ShapeDtypeStruct((T, SB_WIDTH), F32)] * 3 + (hosted.out_shapes if hosted else []),
        scratch_shapes=hosted.sems() if hosted else [],
        compiler_params=_params(has_side_effects=hosted is not None),
    )(qkv, qkv, qkv, o, dymix, *h_ins)
    return res[:3], res[3:]


def _outproj_fwd(x, ya, yb, yc, w):
    T, D = x.shape
    tt = _tile(T, 512)

    def body(x_ref, ya_ref, yb_ref, yc_ref, w_ref, x1_ref, ymix_ref):
        ymix_ref[:, 0:POOL_WIDTH] = ya_ref[...].astype(BF16)
        ymix_ref[:, POOL_WIDTH:POOL_WIDTH + SG_WIDTH] = yb_ref[...].astype(BF16)
        ymix_ref[:, POOL_WIDTH + SG_WIDTH:] = yc_ref[...].astype(BF16)
        x1_ref[...] = x_ref[...] + _nn(ymix_ref[...], w_ref[...])

    row = lambda width: pl.BlockSpec((tt, width), lambda i: (i, 0))
    return pl.pallas_call(
        body, name="outproj_fwd", grid=(T // tt,),
        in_specs=[row(D), row(POOL_WIDTH), row(SG_WIDTH), row(SB_WIDTH), pl.BlockSpec((D, D), lambda i: (0, 0))],
        out_specs=[row(D), row(D)],
        out_shape=[jax.ShapeDtypeStruct((T, D), F32), jax.ShapeDtypeStruct((T, D), BF16)],
        compiler_params=_params(),
    )(x, ya, yb, yc, w)


def _nt_matmul(a, w):
    T, N = a.shape
    K = w.shape[0]
    tt = _tile(T, 512)

    def body(a_ref, w_ref, o_ref):
        o_ref[...] = _nt(a_ref[...].astype(BF16), w_ref[...])

    return pl.pallas_call(
        body, name="nt_matmul", grid=(T // tt,),
        in_specs=[pl.BlockSpec((tt, N), lambda i: (i, 0)), pl.BlockSpec((K, N), lambda i: (0, 0))],
        out_specs=pl.BlockSpec((tt, K), lambda i: (i, 0)),
        out_shape=jax.ShapeDtypeStruct((T, K), F32),
        compiler_params=_params(),
    )(a, w)


def _tn_matmul(a, b, name, n_split=1):
    T, K = a.shape
    N = b.shape[1]
    tk = _tile(K, 1024)
    tn = _tile(N // n_split, 1024)
    tt = _tile(T, 512)
    nper = N // n_split // tn
    nt = T // tt

    def body(a_ref, b_ref, o_ref):
        @pl.when(pl.program_id(2) == 0)
        def _():
            o_ref[...] = jnp.zeros_like(o_ref)

        o_ref[...] += _tn(a_ref[...], b_ref[...].astype(BF16))

    return pl.pallas_call(
        body, name=name, grid=(K // tk, N // tn, nt),
        in_specs=[pl.BlockSpec((tt, tk), lambda k, n, t: (t, k)), pl.BlockSpec((tt, tn), lambda k, n, t: (t, n))],
        out_specs=pl.BlockSpec((None, tk, tn), lambda k, n, t: (n // nper, k, n % nper)),
        out_shape=jax.ShapeDtypeStruct((n_split, K, N // n_split), F32),
        compiler_params=_params(),
    )(a, b)


def _mlp_fwd(x, g, w_up, w_down):
    T, D = x.shape
    F = w_up.shape[1]
    tt = _tile(T, 1024)
    fc = _tile(F, 512)
    nc = F // fc

    def body(x_ref, g_ref, wu_ref, wd_ref, y_ref, h_ref, u_ref, a_ref):
        c = pl.program_id(1)

        @pl.when(c == 0)
        def _():
            xv = x_ref[...]
            h, _, _ = _rms_fwd(xv, g_ref[...])
            h_ref[...] = h.astype(BF16)
            y_ref[...] = xv

        u = _nn(h_ref[...], wu_ref[...])
        u_ref[...] = u.astype(BF16)
        a = jnp.square(jnp.maximum(u, 0.0)).astype(BF16)
        a_ref[...] = a
        y_ref[...] += _nn(a, wd_ref[...])

    return pl.pallas_call(
        body, name="mlp_fwd", grid=(T // tt, nc),
        in_specs=[pl.BlockSpec((tt, D), lambda i, c: (i, 0)), pl.BlockSpec((1, D), lambda i, c: (0, 0)),
                  pl.BlockSpec((D, fc), lambda i, c: (0, c)), pl.BlockSpec((fc, D), lambda i, c: (c, 0))],
        out_specs=[pl.BlockSpec((tt, D), lambda i, c: (i, 0)), pl.BlockSpec((tt, D), lambda i, c: (i, 0)),
                   pl.BlockSpec((tt, fc), lambda i, c: (i, c)), pl.BlockSpec((tt, fc), lambda i, c: (i, c))],
        out_shape=[jax.ShapeDtypeStruct((T, D), F32), jax.ShapeDtypeStruct((T, D), BF16),
                   jax.ShapeDtypeStruct((T, F), BF16), jax.ShapeDtypeStruct((T, F), BF16)],
        compiler_params=_params(),
    )(x, g, w_up, w_down)


def _mlp_bwd(dy, x, g, u, w_up, w_down):
    T, D = x.shape
    F = w_up.shape[1]
    tt = _tile(T, 1024)
    fc = _tile(F, 512)
    nc = F // fc

    def body(dy_ref, x_ref, g_ref, u_ref, wu_ref, wd_ref, dx_ref, du_ref, dg_ref, dyb_ref, dh_ref):
        i, c = pl.program_id(0), pl.program_id(1)

        @pl.when(c == 0)
        def _():
            dyb_ref[...] = dy_ref[...].astype(BF16)
            dh_ref[...] = jnp.zeros_like(dh_ref)

        @pl.when(jnp.logical_and(i == 0, c == 0))
        def _():
            dg_ref[...] = jnp.zeros_like(dg_ref)

        da = _nt(dyb_ref[...], wd_ref[...])
        du = (da * (2.0 * jnp.maximum(u_ref[...].astype(F32), 0.0))).astype(BF16)
        du_ref[...] = du
        dh_ref[...] += _nt(du, wu_ref[...])

        @pl.when(c == nc - 1)
        def _():
            gv = g_ref[...]
            _, xhat, r = _rms_fwd(x_ref[...], gv)
            dx, dgrow = _rms_bwd(dh_ref[...], xhat, r, gv)
            dx_ref[...] = dy_ref[...] + dx
            dg_ref[...] += jnp.sum(dgrow, axis=0, keepdims=True)

    return pl.pallas_call(
        body, name="mlp_bwd", grid=(T // tt, nc),
        in_specs=[pl.BlockSpec((tt, D), lambda i, c: (i, 0)), pl.BlockSpec((tt, D), lambda i, c: (i, 0)),
                  pl.BlockSpec((1, D), lambda i, c: (0, 0)), pl.BlockSpec((tt, fc), lambda i, c: (i, c)),
                  pl.BlockSpec((D, fc), lambda i, c: (0, c)), pl.BlockSpec((fc, D), lambda i, c: (c, 0))],
        out_specs=[pl.BlockSpec((tt, D), lambda i, c: (i, 0)), pl.BlockSpec((tt, fc), lambda i, c: (i, c)),
                   pl.BlockSpec((1, D), lambda i, c: (0, 0))],
        out_shape=[jax.ShapeDtypeStruct((T, D), F32), jax.ShapeDtypeStruct((T, F), BF16),
                   jax.ShapeDtypeStruct((1, D), F32)],
        scratch_shapes=[pltpu.VMEM((tt, D), BF16), pltpu.VMEM((tt, D), F32)],
        compiler_params=_params(),
    )(dy, x, g, u, w_up, w_down)


def _loss_head(x, g, target):
    T, D = x.shape
    tt = _tile(T, 512)

    def body(x_ref, g_ref, t_ref, loss_ref, dx_ref, dg_ref):
        gv = g_ref[...]
        y, xhat, r = _rms_fwd(x_ref[...], gv)
        err = y - t_ref[...]
        dx, dgrow = _rms_bwd(err * (1.0 / D), xhat, r, gv)
        dx_ref[...] = dx

        @pl.when(pl.program_id(0) == 0)
        def _():
            loss_ref[...] = jnp.zeros_like(loss_ref)
            dg_ref[...] = jnp.zeros_like(dg_ref)

        loss_ref[...] += 0.5 * jnp.sum(jnp.mean(err * err, axis=-1, keepdims=True), axis=0, keepdims=True)
        dg_ref[...] += jnp.sum(dgrow, axis=0, keepdims=True)

    return pl.pallas_call(
        body, name="loss_head", grid=(T // tt,),
        in_specs=[pl.BlockSpec((tt, D), lambda i: (i, 0)), pl.BlockSpec((1, D), lambda i: (0, 0)),
                  pl.BlockSpec((tt, D), lambda i: (i, 0))],
        out_specs=[pl.BlockSpec((1, LANES), lambda i: (0, 0)), pl.BlockSpec((tt, D), lambda i: (i, 0)),
                   pl.BlockSpec((1, D), lambda i: (0, 0))],
        out_shape=[jax.ShapeDtypeStruct((1, LANES), F32), jax.ShapeDtypeStruct((T, D), F32),
                   jax.ShapeDtypeStruct((1, D), F32)],
        compiler_params=_params(),
    )(x, g, target)


def _rows(shape, pref=512):
    last = shape[-1]
    rows = 1
    for s in shape[:-1]:
        rows *= s
    tr = rows
    if rows * last > 256 * 1024:
        for cand in (pref, 256, 128, 64, 32, 16, 8):
            if rows % cand == 0:
                tr = cand
                break
    return rows, last, tr


def _elementwise(fn, name, ins, n_out, out_dtype=F32):
    shape = ins[0].shape
    rows, last, tr = _rows(shape)
    flat = [a.reshape(rows, last) for a in ins]
    n_in = len(ins)

    def body(*refs):
        res = fn(*[r[...] for r in refs[:n_in]])
        if n_out == 1:
            res = (res,)
        for r, v in zip(refs[n_in:], res):
            r[...] = v.astype(r.dtype)

    spec = pl.BlockSpec((tr, last), lambda i: (i, 0))
    outs = pl.pallas_call(
        body, name=name, grid=(rows // tr,),
        in_specs=[spec] * n_in, out_specs=[spec] * n_out,
        out_shape=[jax.ShapeDtypeStruct((rows, last), out_dtype)] * n_out,
        compiler_params=_params(),
    )(*flat)
    return [o.reshape(shape) for o in outs]


def _adamw(w, g, m, v):
    m = ADAM_B1 * m + (1.0 - ADAM_B1) * g
    v = ADAM_B2 * v + (1.0 - ADAM_B2) * jnp.square(g)
    m_hat = m / (1.0 - ADAM_B1 ** ADAM_STEP)
    v_hat = v / (1.0 - ADAM_B2 ** ADAM_STEP)
    delta = -ADAM_LR * (m_hat / (jnp.sqrt(v_hat) + ADAM_EPS) + ADAM_WD * w)
    return delta, m, v


def _place():
    x, y, c = lax.axis_index("x"), lax.axis_index("y"), lax.axis_index("c")
    chips = [(1 - x, y), (x, 1 - y), (1 - x, 1 - y)]
    return x, y, c, chips


def _remote(src, dst, ssem, rsem, k, dev):
    return pltpu.make_async_remote_copy(src_ref=src, dst_ref=dst, send_sem=ssem.at[k], recv_sem=rsem.at[k],
                                        device_id=dev, device_id_type=MESH)


def _gather_weights(shards):
    n = len(shards)
    halves = [s.shape[1] // 2 for s in shards]

    def body(*refs):
        src, out = refs[:n], refs[n:2 * n]
        ssem, rsem = refs[2 * n:]
        x, y, c, chips = _place()
        me_q = 2 * x + y
        sib = (x, y, 1 - c)

        def half(a, q, cc):
            return out[a].at[q, :, pl.ds(cc * halves[a], halves[a]), :]

        first = []
        for a in range(n):
            mine = src[a].at[:, pl.ds(c * halves[a], halves[a]), :]
            for r, chip in enumerate(chips):
                first.append(_remote(mine, half(a, me_q, c), ssem, rsem, a * 3 + r, (*chip, c)))
        for cp in first:
            cp.start()
        passed = []
        for a in range(n):
            for r, chip in enumerate(chips):
                q = 2 * chip[0] + chip[1]
                k = a * 3 + r
                _remote(half(a, q, c), half(a, q, c), ssem, rsem, k, (*chip, c)).wait_recv()
                cp = _remote(half(a, q, c), half(a, q, c), ssem, rsem, 3 * n + k, sib)
                cp.start()
                passed.append(cp)
        for a in range(n):
            for r, chip in enumerate(chips):
                q = 2 * chip[0] + chip[1]
                _remote(half(a, q, 1 - c), half(a, q, 1 - c), ssem, rsem, 3 * n + a * 3 + r, sib).wait_recv()
        for cp in first + passed:
            cp.wait_send()

    return pl.pallas_call(
        body, name="gather_weights",
        in_specs=[HBM_SPEC] * n, out_specs=[HBM_SPEC] * n,
        out_shape=[jax.ShapeDtypeStruct((N_CHIPS,) + s.shape, s.dtype) for s in shards],
        scratch_shapes=[pltpu.SemaphoreType.DMA((6 * n,)), pltpu.SemaphoreType.DMA((6 * n,))],
        compiler_params=_params(has_side_effects=True),
    )(*shards)


def _gather_over_ici(shards):
    n = len(shards)
    halves = [s.shape[1] // 2 for s in shards]

    def copies(src, out, ssem, rsem):
        x, y, c, chips = _place()
        me_q = 2 * x + y
        res = []
        for a in range(n):
            rows = pl.ds(c * halves[a], halves[a])
            mine = src[a].at[:, rows, :]
            for r, chip in enumerate(chips):
                dev = (*chip, c)
                res.append((_remote(mine, out[a].at[me_q, :, rows, :], ssem, rsem, a * 3 + r, dev),
                            _remote(mine, out[a].at[2 * chip[0] + chip[1], :, rows, :], ssem, rsem, a * 3 + r, dev)))
        return res

    return _Hosted(list(shards), [jax.ShapeDtypeStruct((N_CHIPS,) + s.shape, s.dtype) for s in shards], 3 * n, copies)


def _pass_to_sibling(gathered):
    n = len(gathered)
    halves = [g.shape[2] // 2 for g in gathered]

    def body(*refs):
        out = refs[n:2 * n]
        ssem, rsem = refs[2 * n:]
        x, y, c, chips = _place()
        sib = (x, y, 1 - c)

        def half(a, q, cc):
            return out[a].at[q, :, pl.ds(cc * halves[a], halves[a]), :]

        cps = []
        for a in range(n):
            for r, chip in enumerate(chips):
                q = 2 * chip[0] + chip[1]
                cps.append(_remote(half(a, q, c), half(a, q, c), ssem, rsem, a * 3 + r, sib))
        for cp in cps:
            cp.start()
        for a in range(n):
            for r, chip in enumerate(chips):
                q = 2 * chip[0] + chip[1]
                _remote(half(a, q, 1 - c), half(a, q, 1 - c), ssem, rsem, a * 3 + r, sib).wait_recv()
        for cp in cps:
            cp.wait_send()

    return pl.pallas_call(
        body, name="pass_to_sibling",
        in_specs=[HBM_SPEC] * n, out_specs=[HBM_SPEC] * n,
        out_shape=[jax.ShapeDtypeStruct(g.shape, g.dtype) for g in gathered],
        input_output_aliases={a: a for a in range(n)},
        scratch_shapes=[pltpu.SemaphoreType.DMA((3 * n,)), pltpu.SemaphoreType.DMA((3 * n,))],
        compiler_params=_params(has_side_effects=True),
    )(*gathered)


def _scatter_over_ici(parts):
    n = len(parts)

    def copies(src, out, ssem, rsem):
        x, y, c, chips = _place()
        res = []
        for a in range(n):
            for r, chip in enumerate(chips):
                cp = _remote(src[a].at[2 * chip[0] + chip[1]], out[a].at[r], ssem, rsem, a * 3 + r, (*chip, c))
                res.append((cp, cp))
        return res

    return _Hosted(list(parts), [jax.ShapeDtypeStruct((3,) + p.shape[1:], F32) for p in parts], 3 * n, copies)


def _swap_halves(grads):
    n = len(grads)
    halves = [g.shape[1] // 2 for g in grads]

    def body(*refs):
        src, out = refs[:n], refs[n:2 * n]
        ssem, rsem = refs[2 * n:]
        x, y, c, _ = _place()
        cps = [_remote(src[a].at[:, pl.ds((1 - c) * halves[a], halves[a]), :], out[a], ssem, rsem, a, (x, y, 1 - c))
               for a in range(n)]
        for cp in cps:
            cp.start()
        for cp in cps:
            cp.wait()

    return pl.pallas_call(
        body, name="swap_halves",
        in_specs=[HBM_SPEC] * n, out_specs=[HBM_SPEC] * n,
        out_shape=[jax.ShapeDtypeStruct((N_CHIPS, h, g.shape[2]), F32) for g, h in zip(grads, halves)],
        scratch_shapes=[pltpu.SemaphoreType.DMA((n,)), pltpu.SemaphoreType.DMA((n,))],
        compiler_params=_params(has_side_effects=True),
    )(*grads)


def _scatter_chips(parts):
    n = len(parts)

    def body(*refs):
        src, out = refs[:n], refs[n:2 * n]
        ssem, rsem = refs[2 * n:]
        x, y, c, chips = _place()
        cps = []
        for a in range(n):
            for r, chip in enumerate(chips):
                cps.append(_remote(src[a].at[2 * chip[0] + chip[1]], out[a].at[r], ssem, rsem, a * 3 + r, (*chip, c)))
        for cp in cps:
            cp.start()
        for cp in cps:
            cp.wait()

    return pl.pallas_call(
        body, name="scatter_chips",
        in_specs=[HBM_SPEC] * n, out_specs=[HBM_SPEC] * n,
        out_shape=[jax.ShapeDtypeStruct((3,) + p.shape[1:], F32) for p in parts],
        scratch_shapes=[pltpu.SemaphoreType.DMA((3 * n,)), pltpu.SemaphoreType.DMA((3 * n,))],
        compiler_params=_params(has_side_effects=True),
    )(*parts)


def _swap_reduced(reduced):
    n = len(reduced)

    def body(*refs):
        src, out = refs[:n], refs[n:2 * n]
        ssem, rsem = refs[2 * n:]
        x, y, c, _ = _place()
        cps = [_remote(src[a], out[a], ssem, rsem, a, (x, y, 1 - c)) for a in range(n)]
        for cp in cps:
            cp.start()
        for cp in cps:
            cp.wait()

    return pl.pallas_call(
        body, name="swap_reduced",
        in_specs=[HBM_SPEC] * n, out_specs=[HBM_SPEC] * n,
        out_shape=[jax.ShapeDtypeStruct(r.shape, F32) for r in reduced],
        scratch_shapes=[pltpu.SemaphoreType.DMA((n,)), pltpu.SemaphoreType.DMA((n,))],
        compiler_params=_params(has_side_effects=True),
    )(*reduced)


def _allreduce_small(buf):
    R, L = buf.shape

    def body(buf_ref, out_ref, pair_ref, chip_ref, ssem, rsem):
        x, y, c, chips = _place()
        me_q = 2 * x + y
        pair_ref[c] = buf_ref[...]
        to_sib = _remote(buf_ref, pair_ref.at[c], ssem, rsem, 0, (x, y, 1 - c))
        to_sib.start()
        _remote(buf_ref, pair_ref.at[1 - c], ssem, rsem, 0, (x, y, 1 - c)).wait_recv()
        chip_ref[me_q] = pair_ref[0] + pair_ref[1]
        cps = [_remote(chip_ref.at[me_q], chip_ref.at[me_q], ssem, rsem, 1 + r, (*chip, c))
               for r, chip in enumerate(chips)]
        for cp in cps:
            cp.start()
        for r, chip in enumerate(chips):
            q = 2 * chip[0] + chip[1]
            _remote(chip_ref.at[q], chip_ref.at[q], ssem, rsem, 1 + r, (*chip, c)).wait_recv()
        out_ref[...] = (chip_ref[0] + chip_ref[1]) + (chip_ref[2] + chip_ref[3])
        to_sib.wait_send()
        for cp in cps:
            cp.wait_send()

    return pl.pallas_call(
        body, name="allreduce_small",
        in_specs=[VMEM_SPEC], out_specs=VMEM_SPEC,
        out_shape=jax.ShapeDtypeStruct((R, L), F32),
        scratch_shapes=[pltpu.VMEM((2, R, L), F32), pltpu.VMEM((N_CHIPS, R, L), F32),
                        pltpu.SemaphoreType.DMA((4,)), pltpu.SemaphoreType.DMA((4,))],
        compiler_params=_params(has_side_effects=True),
    )(buf)


def _pack(arrays):
    flat = jnp.concatenate([a.reshape(-1) for a in arrays])
    pad = (-flat.shape[0]) % (8 * LANES)
    return jnp.pad(flat, (0, pad)).reshape(-1, LANES)


def _unpack(buf, like):
    flat = buf.reshape(-1)
    out, off = [], 0
    for a in like:
        out.append(flat[off:off + a.size].reshape(a.shape))
        off += a.size
    return out


def _block_diag(pw):
    rows = []
    for gi in range(len(POOL_WINDOWS)):
        blocks = [pw[gi] if gj == gi else jnp.zeros_like(pw[gi]) for gj in range(len(POOL_WINDOWS))]
        rows.append(jnp.concatenate(blocks, axis=1))
    return jnp.concatenate(rows, axis=0)


def kernel(x, norm1, w_in, pool_w, pool_scale, sg_norm, sg_w, sg_b, w_out, norm2, w_up, w_down, final_norm, loss_target, m_norm1, m_w_in, m_pool_w, m_pool_scale, m_sg_norm, m_sg_w, m_sg_b, m_w_out, m_norm2, m_w_up, m_w_down, m_final_norm, v_norm1, v_w_in, v_pool_w, v_pool_scale, v_sg_norm, v_sg_w, v_sg_b, v_w_out, v_norm2, v_w_up, v_w_down, v_final_norm):
    depth = norm1.shape[0]
    T = x.shape[1]
    xs = x.reshape(T, D_MODEL)
    target = loss_target.reshape(T, D_MODEL)

    c_idx = lax.axis_index("c")
    q_idx = 2 * lax.axis_index("x") + lax.axis_index("y")
    own = [w.astype(BF16) for w in (w_in, w_out, w_up, w_down)]
    first_in = _gather_weights([own[0][:1]])[0]
    rest = None

    def full(a, l, axis):
        if a == 0:
            got = first_in[:, 0] if l == 0 else rest[0][:, l - 1]
        else:
            got = rest[a][:, l]
        blocks = [jnp.where(q_idx == q, own[a][l], got[q]) for q in range(N_CHIPS)]
        return jnp.concatenate(blocks, axis=axis)

    def reduce_pairs(grads):
        parts = []
        for g, o in zip(grads, _swap_halves(grads)):
            h = o.shape[1]
            mine = lax.dynamic_slice_in_dim(g, c_idx * h, h, axis=1)
            parts.append(_elementwise(lambda a, b: a + b, "add_pair", [mine, o], 1)[0])
        return parts

    def reduce_chips(parts, arrived):
        out = []
        for p, r in zip(parts, arrived):
            mine = lax.dynamic_index_in_dim(p, q_idx, axis=0, keepdims=False)
            out.append(_elementwise(lambda a, b, c, d: (a + b) + (c + d), "add_chips", [mine, r[0], r[1], r[2]], 1)[0])
        return out

    tril = jnp.tril(jnp.ones((CHUNK, CHUNK), F32))
    saved = []
    cur = xs
    wi, wo, wu, wd = {}, {}, {}, {}
    for l in range(depth):
        wbd = _block_diag(pool_w[l]).astype(BF16)
        wm = sg_w[l] * tril
        wm_s = wm.reshape(SG_HEADS * CHUNK, CHUNK).astype(BF16)
        wmt_s = jnp.swapaxes(wm, 1, 2).reshape(SG_HEADS * CHUNK, CHUNK).astype(BF16)
        bias = jnp.repeat(sg_b[l].T, SB_HD, axis=1)
        n1, n2 = norm1[l][None], norm2[l][None]
        psc, sgn = pool_scale[l][None], sg_norm[l][None]
        wi[l] = full(0, l, 1)
        proj, h, qkv = _inproj_fwd(cur, n1, wi[l])
        ya = _pool_fwd(proj, wbd, psc)
        yb = _sg_fwd(proj, wm_s, bias, sgn)
        if l == 0:
            yc, over_ici = _attn_fwd(qkv, _gather_over_ici([own[0][1:], own[1], own[2], own[3]]))
            rest = _pass_to_sibling(over_ici)
        else:
            yc, _ = _attn_fwd(qkv)
        wo[l], wu[l], wd[l] = full(1, l, 0), full(2, l, 1), full(3, l, 0)
        x1, ymix = _outproj_fwd(cur, ya, yb, yc, wo[l])
        x2, h2, u, act = _mlp_fwd(x1, n2, wu[l], wd[l])
        saved.append(dict(x0=cur, x1=x1, proj=proj, h=h, qkv=qkv, yc=yc, ymix=ymix, h2=h2, u=u, act=act,
                          wbd=wbd, wm_s=wm_s, wmt_s=wmt_s, bias=bias, n1=n1, n2=n2, psc=psc, sgn=sgn))
        cur = x2

    loss_row, dcur, d_final = _loss_head(cur, final_norm[None], target)
    loss = lax.psum(loss_row[0, 0], AXES)

    small = [None] * depth
    ready = []
    reduced = {}
    for l in reversed(range(depth)):
        s = saved[l]
        dx1, du, d_n2 = _mlp_bwd(dcur, s["x1"], s["n2"], s["u"], wu[l], wd[l])
        ready.append((2, l, _tn_matmul(s["h2"], du, "grad_w_up", n_split=N_CHIPS)))
        ready.append((3, l, _tn_matmul(s["act"], dcur, "grad_w_down")[0].reshape(N_CHIPS, D_FF // N_CHIPS, D_MODEL)))
        dymix = _nt_matmul(dx1, wo[l])
        ready.append((1, l, _tn_matmul(s["ymix"], dx1, "grad_w_out")[0].reshape(N_CHIPS, D_MODEL // N_CHIPS, D_MODEL)))
        da_in, d_wbd, d_psc = _pool_bwd(s["proj"], dymix, s["wbd"], s["psc"])
        du_pre, dv_pre, d_wm, d_bias, d_sgn = _sg_bwd(s["proj"], dymix, s["wm_s"], s["wmt_s"], s["bias"], s["sgn"])
        if l == 0:
            parts = reduce_pairs([g for _, _, g in ready])
            (dq, dk, dv), arrived = _attn_bwd(s["qkv"], s["yc"], dymix, _scatter_over_ici(parts))
            for (a, ll, _), r in zip(ready, reduce_chips(parts, arrived)):
                reduced[(a, ll)] = r
            ready = []
        else:
            (dq, dk, dv), _ = _attn_bwd(s["qkv"], s["yc"], dymix)
        dproj, dx0, d_n1 = _inproj_bwd([da_in, du_pre, dv_pre, dq, dk, dv], wi[l], s["x0"], s["n1"], dx1)
        g_in_l = _tn_matmul(s["h"], dproj, "grad_w_in")[0]
        ready.append((0, l, g_in_l.reshape(D_MODEL, N_CHIPS, IN_COLS // N_CHIPS).transpose(1, 0, 2)))
        d_pw = jnp.stack([d_wbd[gi * POOL_GW:(gi + 1) * POOL_GW, gi * POOL_GW:(gi + 1) * POOL_GW]
                          for gi in range(len(POOL_WINDOWS))])
        small[l] = dict(norm1=d_n1[0], pool_w=d_pw, pool_scale=d_psc[0], sg_norm=d_sgn[0],
                        sg_w=d_wm.reshape(SG_HEADS, CHUNK, CHUNK), sg_b=d_bias[:, :SG_HEADS].T, norm2=d_n2[0])
        dcur = dx0
    grad_x = dcur.reshape(x.shape)

    parts = reduce_pairs([g for _, _, g in ready])
    for (a, ll, _), r in zip(ready, reduce_chips(parts, _scatter_chips(parts))):
        reduced[(a, ll)] = r
    keys = sorted(reduced)
    theirs = dict(zip(keys, _swap_reduced([reduced[k] for k in keys])))

    def joined(a):
        layers = []
        for l in range(depth):
            mine, other = reduced[(a, l)], theirs[(a, l)]
            layers.append(jnp.where(c_idx == 0, jnp.concatenate([mine, other]), jnp.concatenate([other, mine])))
        return jnp.stack(layers)

    gw_in, gw_out, gw_up, gw_down = [joined(a) for a in range(4)]

    names = ["norm1", "pool_w", "pool_scale", "sg_norm", "sg_w", "sg_b", "norm2"]
    small_w = [norm1, pool_w, pool_scale, sg_norm, sg_w, sg_b, norm2, final_norm]
    small_m = [m_norm1, m_pool_w, m_pool_scale, m_sg_norm, m_sg_w, m_sg_b, m_norm2, m_final_norm]
    small_v = [v_norm1, v_pool_w, v_pool_scale, v_sg_norm, v_sg_w, v_sg_b, v_norm2, v_final_norm]
    small_g = [jnp.stack([small[l][k] for l in range(depth)]) for k in names] + [d_final[0]]
    g_packed = _allreduce_small(_pack(small_g))
    s_delta, s_m, s_v = _elementwise(_adamw, "adamw_small", [_pack(small_w), g_packed, _pack(small_m), _pack(small_v)], 3)
    gs = dict(zip(names + ["final_norm"], _unpack(g_packed, small_w)))
    ds = dict(zip(names + ["final_norm"], _unpack(s_delta, small_w)))
    ms = dict(zip(names + ["final_norm"], _unpack(s_m, small_w)))
    vs = dict(zip(names + ["final_norm"], _unpack(s_v, small_w)))

    big_g = dict(w_in=gw_in, w_out=gw_out, w_up=gw_up, w_down=gw_down)
    big_w = dict(w_in=(w_in, m_w_in, v_w_in), w_out=(w_out, m_w_out, v_w_out),
                 w_up=(w_up, m_w_up, v_w_up), w_down=(w_down, m_w_down, v_w_down))
    for k, (w, m, v) in big_w.items():
        ds[k], ms[k], vs[k] = _elementwise(_adamw, "adamw_" + k, [w, big_g[k], m, v], 3)
        gs[k] = big_g[k]

    order = ["norm1", "w_in", "pool_w", "pool_scale", "sg_norm", "sg_w", "sg_b", "w_out", "norm2", "w_up", "w_down",
             "final_norm"]
    return (loss, grad_x, *[gs[k] for k in order], *[ds[k] for k in order], *[ms[k] for k in order],
            *[vs[k] for k in order])
```

```python
import functools

import jax
import jax.numpy as jnp
from jax import lax
from jax.experimental import pallas as pl
from jax.experimental.pallas import tpu as pltpu

F32 = jnp.float32
BF16 = jnp.bfloat16
MESH = pl.DeviceIdType.MESH
AXES = ("x", "y", "c")

EPS = 1e-6
D_MODEL = 1024
POOL_WIDTH = 256
SG_WIDTH = 256
SB_WIDTH = 512
POOL_WINDOWS = (2, 4, 8, 16)
POOL_GW = 64
POOL_HALO = 16
CHUNK = 128
SG_HEADS = 4
SB_HD = 64
IN_COLS = 2304
QKV_OFF = 768
D_FF = 4096
N_CHIPS = 4
LANES = 128
VMEM_LIMIT = 56 * 1024 * 1024
ATTN_TILE = 256
UNDERFLOW = -104.0

ADAM_LR = 0.001
ADAM_B1 = 0.9
ADAM_B2 = 0.999
ADAM_EPS = 1e-08
ADAM_WD = 0.01
ADAM_STEP = 10

HBM_SPEC = pl.BlockSpec(memory_space=pl.ANY)
VMEM_SPEC = pl.BlockSpec(memory_space=pltpu.VMEM)


def _params(**kw):
    return pltpu.CompilerParams(vmem_limit_bytes=VMEM_LIMIT, **kw)


def _tile(n, pref):
    if n <= pref:
        return n
    for t in range(pref - pref % LANES, 0, -LANES):
        if n % t == 0:
            return t
    raise ValueError((n, pref))


def _nn(a, b):
    return jnp.dot(a, b, preferred_element_type=F32)


def _nt(a, b):
    return lax.dot_general(a, b, (((1,), (1,)), ((), ())), preferred_element_type=F32)


def _tn(a, b):
    return lax.dot_general(a, b, (((0,), (0,)), ((), ())), preferred_element_type=F32)


def _rms_fwd(x, g):
    r = lax.rsqrt(jnp.mean(x * x, axis=-1, keepdims=True) + EPS)
    xhat = x * r
    return xhat * g, xhat, r


def _rms_bwd(dy, xhat, r, g):
    dxhat = dy * g
    dx = r * (dxhat - xhat * jnp.mean(dxhat * xhat, axis=-1, keepdims=True))
    return dx, dy * xhat


_GELU_K = 0.7978845608028654
_GELU_C = 0.044715


def _gelu(x):
    return 0.5 * x * (1.0 + jnp.tanh(_GELU_K * (x + _GELU_C * x * x * x)))


def _gelu_grad(x):
    t = jnp.tanh(_GELU_K * (x + _GELU_C * x * x * x))
    return 0.5 * (1.0 + t) + 0.5 * x * (1.0 - t * t) * _GELU_K * (1.0 + 3.0 * _GELU_C * x * x)


def _inproj_fwd(x, g, w):
    T, D = x.shape
    N = w.shape[1]
    tt = _tile(T, 512)

    def body(x_ref, g_ref, w_ref, proj_ref, h_ref, qkv_ref):
        h, _, _ = _rms_fwd(x_ref[...], g_ref[...])
        hb = h.astype(BF16)
        h_ref[...] = hb
        p = _nn(hb, w_ref[...])
        proj_ref[...] = p
        qkv_ref[...] = p[:, QKV_OFF:].astype(BF16)

    return pl.pallas_call(
        body, name="inproj_fwd", grid=(T // tt,),
        in_specs=[pl.BlockSpec((tt, D), lambda i: (i, 0)), pl.BlockSpec((1, D), lambda i: (0, 0)),
                  pl.BlockSpec((D, N), lambda i: (0, 0))],
        out_specs=[pl.BlockSpec((tt, N), lambda i: (i, 0)), pl.BlockSpec((tt, D), lambda i: (i, 0)),
                   pl.BlockSpec((tt, N - QKV_OFF), lambda i: (i, 0))],
        out_shape=[jax.ShapeDtypeStruct((T, N), F32), jax.ShapeDtypeStruct((T, D), BF16),
                   jax.ShapeDtypeStruct((T, N - QKV_OFF), BF16)],
        compiler_params=_params(),
    )(x, g, w)


def _inproj_bwd(pieces, w, x, g, dres):
    T, D = x.shape
    N = w.shape[1]
    tt = _tile(T, 512)
    widths = [p.shape[1] for p in pieces]
    offs = [sum(widths[:k]) for k in range(len(widths))]
    assert sum(widths) == N
    n_p = len(pieces)

    def body(*refs):
        p_refs = refs[:n_p]
        w_ref, x_ref, g_ref, dres_ref, dproj_ref, dx_ref, dg_ref = refs[n_p:]
        for p_ref, o, wd in zip(p_refs, offs, widths):
            dproj_ref[:, o:o + wd] = p_ref[...].astype(BF16)
        dh = _nt(dproj_ref[...], w_ref[...])
        gv = g_ref[...]
        _, xhat, r = _rms_fwd(x_ref[...], gv)
        dx, dgrow = _rms_bwd(dh, xhat, r, gv)
        dx_ref[...] = dres_ref[...] + dx

        @pl.when(pl.program_id(0) == 0)
        def _():
            dg_ref[...] = jnp.zeros_like(dg_ref)

        dg_ref[...] += jnp.sum(dgrow, axis=0, keepdims=True)

    return pl.pallas_call(
        body, name="inproj_bwd", grid=(T // tt,),
        in_specs=[pl.BlockSpec((tt, wd), lambda i: (i, 0)) for wd in widths] + [
            pl.BlockSpec((D, N), lambda i: (0, 0)), pl.BlockSpec((tt, D), lambda i: (i, 0)),
            pl.BlockSpec((1, D), lambda i: (0, 0)), pl.BlockSpec((tt, D), lambda i: (i, 0))],
        out_specs=[pl.BlockSpec((tt, N), lambda i: (i, 0)), pl.BlockSpec((tt, D), lambda i: (i, 0)),
                   pl.BlockSpec((1, D), lambda i: (0, 0))],
        out_shape=[jax.ShapeDtypeStruct((T, N), BF16), jax.ShapeDtypeStruct((T, D), F32),
                   jax.ShapeDtypeStruct((1, D), F32)],
        compiler_params=_params(),
    )(*pieces, w, x, g, dres)


def _pool_select(s2, s4, s8, s16, grp):
    return jnp.where(grp == 0, s2, jnp.where(grp == 1, s4, jnp.where(grp == 2, s8, s16)))


def _pool_count(t_glob, grp):
    win = jnp.where(grp == 0, 2, jnp.where(grp == 1, 4, jnp.where(grp == 2, 8, 16)))
    return jnp.minimum(t_glob + 1, win).astype(F32)


def _pool_diff(a, halo, base, tt):
    n = tt + POOL_HALO
    ext = jnp.concatenate([halo, a], axis=0)
    s2 = ext + pltpu.roll(ext, 1, 0)
    s4 = s2 + pltpu.roll(s2, 2, 0)
    s8 = s4 + pltpu.roll(s4, 4, 0)
    s16 = s8 + pltpu.roll(s8, 8, 0)
    grp = lax.broadcasted_iota(jnp.int32, (n, POOL_WIDTH), 1) // POOL_GW
    t_glob = lax.broadcasted_iota(jnp.int32, (n, POOL_WIDTH), 0) + (base - POOL_HALO)
    pooled = _pool_select(s2, s4, s8, s16, grp) / _pool_count(t_glob, grp)
    return pooled[POOL_HALO:] - a


def _pool_specs(T, tt):
    hb = tt // POOL_HALO
    return [pl.BlockSpec((tt, POOL_WIDTH), lambda i: (i, 0)),
            pl.BlockSpec((POOL_HALO, POOL_WIDTH), lambda i: (jnp.maximum(i * hb - 1, 0), 0))]


def _pool_fwd(proj, wbd, scale):
    T = proj.shape[0]
    tt = _tile(T, 512)

    def body(a_ref, halo_ref, w_ref, sc_ref, y_ref):
        i = pl.program_id(0)
        halo = jnp.where(i > 0, halo_ref[...], 0.0)
        d = _pool_diff(a_ref[...], halo, i * tt, tt)
        y_ref[...] = _nn(d.astype(BF16), w_ref[...]) * sc_ref[...]

    return pl.pallas_call(
        body, name="pool_fwd", grid=(T // tt,),
        in_specs=_pool_specs(T, tt) + [pl.BlockSpec((POOL_WIDTH, POOL_WIDTH), lambda i: (0, 0)),
                                       pl.BlockSpec((1, POOL_WIDTH), lambda i: (0, 0))],
        out_specs=pl.BlockSpec((tt, POOL_WIDTH), lambda i: (i, 0)),
        out_shape=jax.ShapeDtypeStruct((T, POOL_WIDTH), F32),
        compiler_params=_params(),
    )(proj, proj, wbd, scale)


def _pool_bwd(proj, dymix, wbd, scale):
    T = proj.shape[0]
    tt = _tile(T, 512)
    hb = tt // POOL_HALO
    nblk = T // tt
    n = tt + POOL_HALO

    def body(a_ref, halo_ref, dy_ref, dyn_ref, w_ref, sc_ref, da_ref, dw_ref, dsc_ref):
        i = pl.program_id(0)
        halo = jnp.where(i > 0, halo_ref[...], 0.0)
        d = _pool_diff(a_ref[...], halo, i * tt, tt)
        db = d.astype(BF16)
        wv = w_ref[...]
        sc = sc_ref[...]
        dy = dy_ref[...]
        dys = dy * sc

        @pl.when(i == 0)
        def _():
            dw_ref[...] = jnp.zeros_like(dw_ref)
            dsc_ref[...] = jnp.zeros_like(dsc_ref)

        dsc_ref[...] += jnp.sum(dy * _nn(db, wv), axis=0, keepdims=True)
        dw_ref[...] += _tn(db, dys.astype(BF16))
        dyn = jnp.where(i < nblk - 1, dyn_ref[...], 0.0) * sc
        dd = _nt(jnp.concatenate([dys, dyn], axis=0).astype(BF16), wv)
        grp = lax.broadcasted_iota(jnp.int32, (n, POOL_WIDTH), 1) // POOL_GW
        t_glob = lax.broadcasted_iota(jnp.int32, (n, POOL_WIDTH), 0) + i * tt
        e = dd / _pool_count(t_glob, grp)
        r2 = e + pltpu.roll(e, n - 1, 0)
        r4 = r2 + pltpu.roll(r2, n - 2, 0)
        r8 = r4 + pltpu.roll(r4, n - 4, 0)
        r16 = r8 + pltpu.roll(r8, n - 8, 0)
        da_ref[...] = (_pool_select(r2, r4, r8, r16, grp) - dd)[:tt]

    return pl.pallas_call(
        body, name="pool_bwd", grid=(nblk,),
        in_specs=_pool_specs(T, tt) + [
            pl.BlockSpec((tt, POOL_WIDTH), lambda i: (i, 0)),
            pl.BlockSpec((POOL_HALO, POOL_WIDTH), lambda i: (jnp.minimum((i + 1) * hb, T // POOL_HALO - 1), 0)),
            pl.BlockSpec((POOL_WIDTH, POOL_WIDTH), lambda i: (0, 0)), pl.BlockSpec((1, POOL_WIDTH), lambda i: (0, 0))],
        out_specs=[pl.BlockSpec((tt, POOL_WIDTH), lambda i: (i, 0)),
                   pl.BlockSpec((POOL_WIDTH, POOL_WIDTH), lambda i: (0, 0)),
                   pl.BlockSpec((1, POOL_WIDTH), lambda i: (0, 0))],
        out_shape=[jax.ShapeDtypeStruct((T, POOL_WIDTH), F32),
                   jax.ShapeDtypeStruct((POOL_WIDTH, POOL_WIDTH), F32),
                   jax.ShapeDtypeStruct((1, POOL_WIDTH), F32)],
        compiler_params=_params(),
    )(proj, proj, dymix, dymix, wbd, scale)


def _head_select(stacked, grp):
    out = jnp.where(grp == 0, stacked[0:CHUNK], 0.0)
    for h in range(1, SG_HEADS):
        out = out + jnp.where(grp == h, stacked[h * CHUNK:(h + 1) * CHUNK], 0.0)
    return out


def _sg_specs(tt):
    return [pl.BlockSpec((tt, SG_WIDTH), lambda i: (i, 1)), pl.BlockSpec((tt, SG_WIDTH), lambda i: (i, 2))]


def _sg_fwd(proj, wm, bias, g):
    T = proj.shape[0]
    tt = _tile(T, 512)

    def body(u_ref, v_ref, wm_ref, b_ref, g_ref, y_ref):
        zu = _gelu(u_ref[...])
        vn, _, _ = _rms_fwd(_gelu(v_ref[...]), g_ref[...])
        grp = lax.broadcasted_iota(jnp.int32, (CHUNK, SG_WIDTH), 1) // SB_HD
        for n in range(tt // CHUNK):
            rows = slice(n * CHUNK, (n + 1) * CHUNK)
            sv = _head_select(_nn(wm_ref[...], vn[rows].astype(BF16)), grp) + b_ref[...]
            y_ref[rows, :] = zu[rows] * sv

    return pl.pallas_call(
        body, name="sg_fwd", grid=(T // tt,),
        in_specs=_sg_specs(tt) + [pl.BlockSpec((SG_HEADS * CHUNK, CHUNK), lambda i: (0, 0)),
                                  pl.BlockSpec((CHUNK, SG_WIDTH), lambda i: (0, 0)),
                                  pl.BlockSpec((1, SG_WIDTH), lambda i: (0, 0))],
        out_specs=pl.BlockSpec((tt, SG_WIDTH), lambda i: (i, 0)),
        out_shape=jax.ShapeDtypeStruct((T, SG_WIDTH), F32),
        compiler_params=_params(),
    )(proj, proj, wm, bias, g)


def _sg_bwd(proj, dymix, wm, wmt, bias, g):
    T = proj.shape[0]
    tt = _tile(T, 512)
    nblk = T // tt

    def body(u_ref, v_ref, dy_ref, wm_ref, wmt_ref, b_ref, g_ref,
             du_ref, dv_ref, dw_ref, db_ref, dg_ref, dvn_ref, dbias_ref):
        i = pl.program_id(0)
        up, vp = u_ref[...], v_ref[...]
        gv = g_ref[...]
        zu, zv = _gelu(up), _gelu(vp)
        vn, xhat, r = _rms_fwd(zv, gv)
        gu = _gelu_grad(up)
        grp = lax.broadcasted_iota(jnp.int32, (CHUNK, SG_WIDTH), 1) // SB_HD

        @pl.when(i == 0)
        def _():
            dw_ref[...] = jnp.zeros_like(dw_ref)
            dbias_ref[...] = jnp.zeros_like(dbias_ref)
            dg_ref[...] = jnp.zeros_like(dg_ref)

        for n in range(tt // CHUNK):
            rows = slice(n * CHUNK, (n + 1) * CHUNK)
            vc = vn[rows].astype(BF16)
            sv = _head_select(_nn(wm_ref[...], vc), grp) + b_ref[...]
            dy = dy_ref[rows, :]
            du_ref[rows, :] = dy * sv * gu[rows]
            dsv = dy * zu[rows]
            dsvb = dsv.astype(BF16)
            dvn_ref[rows, :] = _head_select(_nn(wmt_ref[...], dsvb), grp)
            stacked = jnp.concatenate([jnp.where(grp == h, dsv, 0.0) for h in range(SG_HEADS)], axis=0)
            dw_ref[...] += _nt(stacked.astype(BF16), vc)
            dbias_ref[...] += dsv

        dzv, dgrow = _rms_bwd(dvn_ref[...], xhat, r, gv)
        dg_ref[...] += jnp.sum(dgrow, axis=0, keepdims=True)
        dv_ref[...] = dzv * _gelu_grad(vp)

        @pl.when(i == nblk - 1)
        def _():
            t_i = lax.broadcasted_iota(jnp.int32, (SG_HEADS * CHUNK, CHUNK), 0) % CHUNK
            s_i = lax.broadcasted_iota(jnp.int32, (SG_HEADS * CHUNK, CHUNK), 1)
            dw_ref[...] = jnp.where(s_i <= t_i, dw_ref[...], 0.0)
            lane = lax.broadcasted_iota(jnp.int32, (CHUNK, LANES), 1)
            acc = jnp.zeros((CHUNK, LANES), F32)
            for h in range(SG_HEADS):
                tot = jnp.sum(jnp.where(grp == h, dbias_ref[...], 0.0), axis=1, keepdims=True)
                acc = acc + jnp.where(lane == h, tot, 0.0)
            db_ref[...] = acc

    return pl.pallas_call(
        body, name="sg_bwd", grid=(nblk,),
        in_specs=_sg_specs(tt) + [pl.BlockSpec((tt, SG_WIDTH), lambda i: (i, 1)),
                                  pl.BlockSpec((SG_HEADS * CHUNK, CHUNK), lambda i: (0, 0)),
                                  pl.BlockSpec((SG_HEADS * CHUNK, CHUNK), lambda i: (0, 0)),
                                  pl.BlockSpec((CHUNK, SG_WIDTH), lambda i: (0, 0)),
                                  pl.BlockSpec((1, SG_WIDTH), lambda i: (0, 0))],
        out_specs=[pl.BlockSpec((tt, SG_WIDTH), lambda i: (i, 0)), pl.BlockSpec((tt, SG_WIDTH), lambda i: (i, 0)),
                   pl.BlockSpec((SG_HEADS * CHUNK, CHUNK), lambda i: (0, 0)),
                   pl.BlockSpec((CHUNK, LANES), lambda i: (0, 0)), pl.BlockSpec((1, SG_WIDTH), lambda i: (0, 0))],
        out_shape=[jax.ShapeDtypeStruct((T, SG_WIDTH), F32), jax.ShapeDtypeStruct((T, SG_WIDTH), F32),
                   jax.ShapeDtypeStruct((SG_HEADS * CHUNK, CHUNK), F32),
                   jax.ShapeDtypeStruct((CHUNK, LANES), F32), jax.ShapeDtypeStruct((1, SG_WIDTH), F32)],
        scratch_shapes=[pltpu.VMEM((tt, SG_WIDTH), F32), pltpu.VMEM((CHUNK, SG_WIDTH), F32)],
        compiler_params=_params(),
    )(proj, proj, dymix, wm, wmt, bias, g)


def _split_dot(x, u):
    hi = x.astype(BF16)
    lo = (x - hi.astype(F32)).astype(BF16)
    return _nn(hi, u) + _nn(lo, u)


def _sb_logits(qh, kb):
    z = _nt(qh, kb) * (1.0 / 8.0)
    sp = jnp.log1p(jnp.exp(-jnp.abs(z)))
    return jnp.minimum(z, 0.0) - sp, -jnp.maximum(z, 0.0) - sp


def _attn_qkv_specs(tq, T):
    base = (IN_COLS - 3 * SB_WIDTH - QKV_OFF) // LANES
    nb = SB_WIDTH // LANES
    return [pl.BlockSpec((tq, LANES), lambda p, i: (i, base + p)),
            pl.BlockSpec((T, LANES), lambda p, i: (0, base + nb + p)),
            pl.BlockSpec((T, LANES), lambda p, i: (0, base + 2 * nb + p))]


class _Hosted:
    def __init__(self, ins, out_shapes, n_sems, copies):
        self.ins, self.out_shapes, self.n_sems, self.copies = ins, out_shapes, n_sems, copies

    @property
    def n(self):
        return len(self.ins)

    def sems(self):
        return [pltpu.SemaphoreType.DMA((self.n_sems,)), pltpu.SemaphoreType.DMA((self.n_sems,))]

    def start(self, src, dst, ssem, rsem):
        for send, _ in self.copies(src, dst, ssem, rsem):
            send.start()

    def wait(self, src, dst, ssem, rsem):
        for send, recv in self.copies(src, dst, ssem, rsem):
            recv.wait_recv()
            send.wait_send()


def _host(hosted, refs, n_in, n_out, first, last):
    if hosted is None:
        return refs, lambda: None, lambda: None
    n = hosted.n
    own_in, h_in = refs[:n_in], refs[n_in:n_in + n]
    own_out, h_out = refs[n_in + n:n_in + n + n_out], refs[n_in + n + n_out:n_in + 2 * n + n_out]
    rest = refs[n_in + 2 * n + n_out:]
    ssem, rsem = rest[-2:]

    def start():
        @pl.when(first)
        def _():
            hosted.start(h_in, h_out, ssem, rsem)

    def wait():
        @pl.when(last)
        def _():
            hosted.wait(h_in, h_out, ssem, rsem)

    return own_in + own_out + rest[:-2], start, wait


def _attn_fwd(qkv, hosted=None):
    T = qkv.shape[0]
    tq = _tile(T, ATTN_TILE)
    n_p, nq = SB_WIDTH // LANES, T // tq

    def body(*refs):
        p, i = pl.program_id(0), pl.program_id(1)
        (q_ref, k_ref, v_ref, o_ref), start, wait = _host(
            hosted, refs, 3, 1, jnp.logical_and(p == 0, i == 0), jnp.logical_and(p == n_p - 1, i == nq - 1))
        start()
        lane = lax.broadcasted_iota(jnp.int32, (tq, LANES), 1)
        row = lax.broadcasted_iota(jnp.int32, (tq, tq), 0)
        col = lax.broadcasted_iota(jnp.int32, (tq, tq), 1)
        after = jnp.where(row > col, 1.0, 0.0).astype(BF16)
        valid = col < row
        q = q_ref[...].astype(F32)
        qh = [jnp.where((lane // SB_HD) == hh, q, 0.0).astype(BF16) for hh in range(2)]

        def tile(j, state, mask):
            ks = pl.ds(pl.multiple_of(j * tq, tq), tq)
            kb, vb = k_ref[ks, :], v_ref[ks, :]
            new = []
            for hh in range(2):
                carry, acc = state[hh]
                lb, lm = _sb_logits(qh[hh], kb)
                if mask is not None:
                    lm = jnp.where(mask, lm, 0.0)
                a = jnp.exp(lb + _split_dot(lm, after) + carry)
                if mask is not None:
                    a = jnp.where(mask, a, 0.0)
                new.append((carry + jnp.sum(lm, axis=1, keepdims=True), acc + _nn(a.astype(BF16), vb)))
            return tuple(new)

        def live(state):
            return jnp.maximum(jnp.max(state[0][0]), jnp.max(state[1][0]))

        zero = (jnp.zeros((tq, 1), F32), jnp.zeros((tq, LANES), F32))
        state = tile(i, (zero, zero), valid)
        state = tile(jnp.maximum(i - 1, 0), state, jnp.broadcast_to(i > 0, (tq, tq)))

        def cond(st):
            return jnp.logical_and(st[0] >= 0, st[2] > UNDERFLOW)

        def step(st):
            state = tile(st[0], st[1], None)
            return st[0] - 1, state, live(state)

        _, state, _ = lax.while_loop(cond, step, (i - 2, state, live(state)))
        o_ref[...] = jnp.where(lane < SB_HD, state[0][1], state[1][1])
        wait()

    h_ins = hosted.ins if hosted else []
    res = pl.pallas_call(
        body, name="attn_fwd_hosting" if hosted else "attn_fwd", grid=(n_p, nq),
        in_specs=_attn_qkv_specs(tq, T) + [HBM_SPEC] * len(h_ins),
        out_specs=[pl.BlockSpec((tq, LANES), lambda p, i: (i, p))] + [HBM_SPEC] * len(h_ins),
        out_shape=[jax.ShapeDtypeStruct((T, SB_WIDTH), F32)] + (hosted.out_shapes if hosted else []),
        scratch_shapes=hosted.sems() if hosted else [],
        compiler_params=_params(has_side_effects=hosted is not None),
    )(qkv, qkv, qkv, *h_ins)
    return res[0], res[1:]


def _attn_bwd(qkv, o, dymix, hosted=None):
    T = qkv.shape[0]
    tq = _tile(T, ATTN_TILE)
    n_p, nq = SB_WIDTH // LANES, T // tq
    yc_blk = (POOL_WIDTH + SG_WIDTH) // LANES

    def body(*refs):
        p, i = pl.program_id(0), pl.program_id(1)
        (q_ref, k_ref, v_ref, o_ref, do_ref, dq_ref, dk_ref, dv_ref), start, wait = _host(
            hosted, refs, 5, 3, jnp.logical_and(p == 0, i == 0), jnp.logical_and(p == n_p - 1, i == nq - 1))
        start()
        lane = lax.broadcasted_iota(jnp.int32, (tq, LANES), 1)
        row = lax.broadcasted_iota(jnp.int32, (tq, tq), 0)
        col = lax.broadcasted_iota(jnp.int32, (tq, tq), 1)
        after = jnp.where(row > col, 1.0, 0.0).astype(BF16)
        from_here = jnp.where(row >= col, 1.0, 0.0).astype(BF16)
        valid = col < row

        @pl.when(i == 0)
        def _():
            dk_ref[...] = jnp.zeros_like(dk_ref)
            dv_ref[...] = jnp.zeros_like(dv_ref)

        q = q_ref[...].astype(F32)
        ov = o_ref[...]
        dov = do_ref[...]
        heads = [(lane // SB_HD) == hh for hh in range(2)]
        qh = [jnp.where(h, q, 0.0).astype(BF16) for h in heads]
        dohb = [jnp.where(h, dov, 0.0).astype(BF16) for h in heads]
        delta = [jnp.sum(d.astype(F32) * ov, axis=1, keepdims=True) for d in dohb]

        def tile(j, state, mask):
            ks = pl.ds(pl.multiple_of(j * tq, tq), tq)
            kb, vb = k_ref[ks, :], v_ref[ks, :]
            new, dk, dv = [], None, None
            for hh in range(2):
                c_a, c_r, dqa = state[hh]
                lb, lm = _sb_logits(qh[hh], kb)
                if mask is not None:
                    lm = jnp.where(mask, lm, 0.0)
                a = jnp.exp(lb + _split_dot(lm, after) + c_a)
                if mask is not None:
                    a = jnp.where(mask, a, 0.0)
                ab = a.astype(BF16)
                sig = jnp.exp(lb)
                g = _nt(dohb[hh], vb) * ab.astype(F32)
                left = delta[hh] - (c_r + _split_dot(g, from_here))
                dz = (g * (1.0 - sig) - left * sig) * (1.0 / 8.0)
                if mask is not None:
                    dz = jnp.where(mask, dz, 0.0)
                dzb = dz.astype(BF16)
                dk_h, dv_h = _tn(dzb, qh[hh]), _tn(ab, dohb[hh])
                dk, dv = (dk_h, dv_h) if hh == 0 else (dk + dk_h, dv + dv_h)
                new.append((c_a + jnp.sum(lm, axis=1, keepdims=True), c_r + jnp.sum(g, axis=1, keepdims=True),
                            dqa + _nn(dzb, kb)))
            dk_ref[ks, :] += dk
            dv_ref[ks, :] += dv
            return tuple(new)

        def live(state):
            return jnp.maximum(jnp.max(state[0][0]), jnp.max(state[1][0]))

        zero = (jnp.zeros((tq, 1), F32), jnp.zeros((tq, 1), F32), jnp.zeros((tq, LANES), F32))
        state = tile(i, (zero, zero), valid)
        state = tile(jnp.maximum(i - 1, 0), state, jnp.broadcast_to(i > 0, (tq, tq)))

        def cond(st):
            return jnp.logical_and(st[0] >= 0, st[2] > UNDERFLOW)

        def step(st):
            state = tile(st[0], st[1], None)
            return st[0] - 1, state, live(state)

        _, state, _ = lax.while_loop(cond, step, (i - 2, state, live(state)))
        dq_ref[...] = jnp.where(lane < SB_HD, state[0][2], state[1][2])
        wait()

    h_ins = hosted.ins if hosted else []
    res = pl.pallas_call(
        body, name="attn_bwd_hosting" if hosted else "attn_bwd", grid=(n_p, nq),
        in_specs=_attn_qkv_specs(tq, T) + [pl.BlockSpec((tq, LANES), lambda p, i: (i, p)),
                                           pl.BlockSpec((tq, LANES), lambda p, i: (i, yc_blk + p))]
        + [HBM_SPEC] * len(h_ins),
        out_specs=[pl.BlockSpec((tq, LANES), lambda p, i: (i, p)), pl.BlockSpec((T, LANES), lambda p, i: (0, p)),
                   pl.BlockSpec((T, LANES), lambda p, i: (0, p))] + [HBM_SPEC] * len(h_ins),
        out_shape=[jax.ShapeDtypeStruct((T, SB_WIDTH), F32)] * 3 + (hosted.out_shapes if hosted else []),
        scratch_shapes=hosted.sems() if hosted else [],
        compiler_params=_params(has_side_effects=hosted is not None),
    )(qkv, qkv, qkv, o, dymix, *h_ins)
    return res[:3], res[3:]


def _outproj_fwd(x, ya, yb, yc, w):
    T, D = x.shape
    tt = _tile(T, 512)

    def body(x_ref, ya_ref, yb_ref, yc_ref, w_ref, x1_ref, ymix_ref):
        ymix_ref[:, 0:POOL_WIDTH] = ya_ref[...].astype(BF16)
        ymix_ref[:, POOL_WIDTH:POOL_WIDTH + SG_WIDTH] = yb_ref[...].astype(BF16)
        ymix_ref[:, POOL_WIDTH + SG_WIDTH:] = yc_ref[...].astype(BF16)
        x1_ref[...] = x_ref[...] + _nn(ymix_ref[...], w_ref[...])

    row = lambda width: pl.BlockSpec((tt, width), lambda i: (i, 0))
    return pl.pallas_call(
        body, name="outproj_fwd", grid=(T // tt,),
        in_specs=[row(D), row(POOL_WIDTH), row(SG_WIDTH), row(SB_WIDTH), pl.BlockSpec((D, D), lambda i: (0, 0))],
        out_specs=[row(D), row(D)],
        out_shape=[jax.ShapeDtypeStruct((T, D), F32), jax.ShapeDtypeStruct((T, D), BF16)],
        compiler_params=_params(),
    )(x, ya, yb, yc, w)


def _nt_matmul(a, w):
    T, N = a.shape
    K = w.shape[0]
    tt = _tile(T, 512)

    def body(a_ref, w_ref, o_ref):
        o_ref[...] = _nt(a_ref[...].astype(BF16), w_ref[...])

    return pl.pallas_call(
        body, name="nt_matmul", grid=(T // tt,),
        in_specs=[pl.BlockSpec((tt, N), lambda i: (i, 0)), pl.BlockSpec((K, N), lambda i: (0, 0))],
        out_specs=pl.BlockSpec((tt, K), lambda i: (i, 0)),
        out_shape=jax.ShapeDtypeStruct((T, K), F32),
        compiler_params=_params(),
    )(a, w)


def _tn_matmul(a, b, name, n_split=1):
    T, K = a.shape
    N = b.shape[1]
    tk = _tile(K, 1024)
    tn = _tile(N // n_split, 1024)
    tt = _tile(T, 1024)
    nper = N // n_split // tn
    nt = T // tt

    def body(a_ref, b_ref, o_ref):
        @pl.when(pl.program_id(2) == 0)
        def _():
            o_ref[...] = jnp.zeros_like(o_ref)

        o_ref[...] += _tn(a_ref[...], b_ref[...].astype(BF16))

    return pl.pallas_call(
        body, name=name, grid=(K // tk, N // tn, nt),
        in_specs=[pl.BlockSpec((tt, tk), lambda k, n, t: (t, k)), pl.BlockSpec((tt, tn), lambda k, n, t: (t, n))],
        out_specs=pl.BlockSpec((None, tk, tn), lambda k, n, t: (n // nper, k, n % nper)),
        out_shape=jax.ShapeDtypeStruct((n_split, K, N // n_split), F32),
        compiler_params=_params(),
    )(a, b)


def _mlp_fwd(x, g, w_up, w_down):
    T, D = x.shape
    F = w_up.shape[1]
    tt = _tile(T, 512)
    fc = _tile(F, 1024)
    nc = F // fc

    def body(x_ref, g_ref, wu_ref, wd_ref, y_ref, h_ref, u_ref, a_ref):
        c = pl.program_id(1)

        @pl.when(c == 0)
        def _():
            xv = x_ref[...]
            h, _, _ = _rms_fwd(xv, g_ref[...])
            h_ref[...] = h.astype(BF16)
            y_ref[...] = xv

        u = _nn(h_ref[...], wu_ref[...])
        u_ref[...] = u.astype(BF16)
        a = jnp.square(jnp.maximum(u, 0.0)).astype(BF16)
        a_ref[...] = a
        y_ref[...] += _nn(a, wd_ref[...])

    return pl.pallas_call(
        body, name="mlp_fwd", grid=(T // tt, nc),
        in_specs=[pl.BlockSpec((tt, D), lambda i, c: (i, 0)), pl.BlockSpec((1, D), lambda i, c: (0, 0)),
                  pl.BlockSpec((D, fc), lambda i, c: (0, c)), pl.BlockSpec((fc, D), lambda i, c: (c, 0))],
        out_specs=[pl.BlockSpec((tt, D), lambda i, c: (i, 0)), pl.BlockSpec((tt, D), lambda i, c: (i, 0)),
                   pl.BlockSpec((tt, fc), lambda i, c: (i, c)), pl.BlockSpec((tt, fc), lambda i, c: (i, c))],
        out_shape=[jax.ShapeDtypeStruct((T, D), F32), jax.ShapeDtypeStruct((T, D), BF16),
                   jax.ShapeDtypeStruct((T, F), BF16), jax.ShapeDtypeStruct((T, F), BF16)],
        compiler_params=_params(),
    )(x, g, w_up, w_down)


def _mlp_bwd(dy, x, g, u, w_up, w_down):
    T, D = x.shape
    F = w_up.shape[1]
    tt = _tile(T, 512)
    fc = _tile(F, 1024)
    nc = F // fc

    def body(dy_ref, x_ref, g_ref, u_ref, wu_ref, wd_ref, dx_ref, du_ref, dg_ref, dyb_ref, dh_ref):
        i, c = pl.program_id(0), pl.program_id(1)

        @pl.when(c == 0)
        def _():
            dyb_ref[...] = dy_ref[...].astype(BF16)
            dh_ref[...] = jnp.zeros_like(dh_ref)

        @pl.when(jnp.logical_and(i == 0, c == 0))
        def _():
            dg_ref[...] = jnp.zeros_like(dg_ref)

        da = _nt(dyb_ref[...], wd_ref[...])
        du = (da * (2.0 * jnp.maximum(u_ref[...].astype(F32), 0.0))).astype(BF16)
        du_ref[...] = du
        dh_ref[...] += _nt(du, wu_ref[...])

        @pl.when(c == nc - 1)
        def _():
            gv = g_ref[...]
            _, xhat, r = _rms_fwd(x_ref[...], gv)
            dx, dgrow = _rms_bwd(dh_ref[...], xhat, r, gv)
            dx_ref[...] = dy_ref[...] + dx
            dg_ref[...] += jnp.sum(dgrow, axis=0, keepdims=True)

    return pl.pallas_call(
        body, name="mlp_bwd", grid=(T // tt, nc),
        in_specs=[pl.BlockSpec((tt, D), lambda i, c: (i, 0)), pl.BlockSpec((tt, D), lambda i, c: (i, 0)),
                  pl.BlockSpec((1, D), lambda i, c: (0, 0)), pl.BlockSpec((tt, fc), lambda i, c: (i, c)),
                  pl.BlockSpec((D, fc), lambda i, c: (0, c)), pl.BlockSpec((fc, D), lambda i, c: (c, 0))],
        out_specs=[pl.BlockSpec((tt, D), lambda i, c: (i, 0)), pl.BlockSpec((tt, fc), lambda i, c: (i, c)),
                   pl.BlockSpec((1, D), lambda i, c: (0, 0))],
        out_shape=[jax.ShapeDtypeStruct((T, D), F32), jax.ShapeDtypeStruct((T, F), BF16),
                   jax.ShapeDtypeStruct((1, D), F32)],
        scratch_shapes=[pltpu.VMEM((tt, D), BF16), pltpu.VMEM((tt, D), F32)],
        compiler_params=_params(),
    )(dy, x, g, u, w_up, w_down)


def _loss_head(x, g, target):
    T, D = x.shape
    tt = _tile(T, 512)

    def body(x_ref, g_ref, t_ref, loss_ref, dx_ref, dg_ref):
        gv = g_ref[...]
        y, xhat, r = _rms_fwd(x_ref[...], gv)
        err = y - t_ref[...]
        dx, dgrow = _rms_bwd(err * (1.0 / D), xhat, r, gv)
        dx_ref[...] = dx

        @pl.when(pl.program_id(0) == 0)
        def _():
            loss_ref[...] = jnp.zeros_like(loss_ref)
            dg_ref[...] = jnp.zeros_like(dg_ref)

        loss_ref[...] += 0.5 * jnp.sum(jnp.mean(err * err, axis=-1, keepdims=True), axis=0, keepdims=True)
        dg_ref[...] += jnp.sum(dgrow, axis=0, keepdims=True)

    return pl.pallas_call(
        body, name="loss_head", grid=(T // tt,),
        in_specs=[pl.BlockSpec((tt, D), lambda i: (i, 0)), pl.BlockSpec((1, D), lambda i: (0, 0)),
                  pl.BlockSpec((tt, D), lambda i: (i, 0))],
        out_specs=[pl.BlockSpec((1, LANES), lambda i: (0, 0)), pl.BlockSpec((tt, D), lambda i: (i, 0)),
                   pl.BlockSpec((1, D), lambda i: (0, 0))],
        out_shape=[jax.ShapeDtypeStruct((1, LANES), F32), jax.ShapeDtypeStruct((T, D), F32),
                   jax.ShapeDtypeStruct((1, D), F32)],
        compiler_params=_params(),
    )(x, g, target)


def _rows(shape, pref=512):
    last = shape[-1]
    rows = 1
    for s in shape[:-1]:
        rows *= s
    tr = rows
    if rows * last > 256 * 1024:
        for cand in (pref, 256, 128, 64, 32, 16, 8):
            if rows % cand == 0:
                tr = cand
                break
    return rows, last, tr


def _elementwise(fn, name, ins, n_out, out_dtype=F32):
    shape = ins[0].shape
    rows, last, tr = _rows(shape)
    flat = [a.reshape(rows, last) for a in ins]
    n_in = len(ins)

    def body(*refs):
        res = fn(*[r[...] for r in refs[:n_in]])
        if n_out == 1:
            res = (res,)
        for r, v in zip(refs[n_in:], res):
            r[...] = v.astype(r.dtype)

    spec = pl.BlockSpec((tr, last), lambda i: (i, 0))
    outs = pl.pallas_call(
        body, name=name, grid=(rows // tr,),
        in_specs=[spec] * n_in, out_specs=[spec] * n_out,
        out_shape=[jax.ShapeDtypeStruct((rows, last), out_dtype)] * n_out,
        compiler_params=_params(),
    )(*flat)
    return [o.reshape(shape) for o in outs]


def _adamw(w, g, m, v):
    m = ADAM_B1 * m + (1.0 - ADAM_B1) * g
    v = ADAM_B2 * v + (1.0 - ADAM_B2) * jnp.square(g)
    m_hat = m / (1.0 - ADAM_B1 ** ADAM_STEP)
    v_hat = v / (1.0 - ADAM_B2 ** ADAM_STEP)
    delta = -ADAM_LR * (m_hat / (jnp.sqrt(v_hat) + ADAM_EPS) + ADAM_WD * w)
    return delta, m, v


def _place():
    x, y, c = lax.axis_index("x"), lax.axis_index("y"), lax.axis_index("c")
    chips = [(1 - x, y), (x, 1 - y), (1 - x, 1 - y)]
    return x, y, c, chips


def _remote(src, dst, ssem, rsem, k, dev):
    return pltpu.make_async_remote_copy(src_ref=src, dst_ref=dst, send_sem=ssem.at[k], recv_sem=rsem.at[k],
                                        device_id=dev, device_id_type=MESH)


def _gather_weights(shards):
    n = len(shards)
    halves = [s.shape[1] // 2 for s in shards]

    def body(*refs):
        src, out = refs[:n], refs[n:2 * n]
        ssem, rsem = refs[2 * n:]
        x, y, c, chips = _place()
        me_q = 2 * x + y
        sib = (x, y, 1 - c)

        def half(a, q, cc):
            return out[a].at[q, :, pl.ds(cc * halves[a], halves[a]), :]

        first = []
        for a in range(n):
            mine = src[a].at[:, pl.ds(c * halves[a], halves[a]), :]
            for r, chip in enumerate(chips):
                first.append(_remote(mine, half(a, me_q, c), ssem, rsem, a * 3 + r, (*chip, c)))
        for cp in first:
            cp.start()
        passed = []
        for a in range(n):
            for r, chip in enumerate(chips):
                q = 2 * chip[0] + chip[1]
                k = a * 3 + r
                _remote(half(a, q, c), half(a, q, c), ssem, rsem, k, (*chip, c)).wait_recv()
                cp = _remote(half(a, q, c), half(a, q, c), ssem, rsem, 3 * n + k, sib)
                cp.start()
                passed.append(cp)
        for a in range(n):
            for r, chip in enumerate(chips):
                q = 2 * chip[0] + chip[1]
                _remote(half(a, q, 1 - c), half(a, q, 1 - c), ssem, rsem, 3 * n + a * 3 + r, sib).wait_recv()
        for cp in first + passed:
            cp.wait_send()

    return pl.pallas_call(
        body, name="gather_weights",
        in_specs=[HBM_SPEC] * n, out_specs=[HBM_SPEC] * n,
        out_shape=[jax.ShapeDtypeStruct((N_CHIPS,) + s.shape, s.dtype) for s in shards],
        scratch_shapes=[pltpu.SemaphoreType.DMA((6 * n,)), pltpu.SemaphoreType.DMA((6 * n,))],
        compiler_params=_params(has_side_effects=True),
    )(*shards)


def _gather_over_ici(shards):
    n = len(shards)
    halves = [s.shape[1] // 2 for s in shards]

    def copies(src, out, ssem, rsem):
        x, y, c, chips = _place()
        me_q = 2 * x + y
        res = []
        for a in range(n):
            rows = pl.ds(c * halves[a], halves[a])
            mine = src[a].at[:, rows, :]
            for r, chip in enumerate(chips):
                dev = (*chip, c)
                res.append((_remote(mine, out[a].at[me_q, :, rows, :], ssem, rsem, a * 3 + r, dev),
                            _remote(mine, out[a].at[2 * chip[0] + chip[1], :, rows, :], ssem, rsem, a * 3 + r, dev)))
        return res

    return _Hosted(list(shards), [jax.ShapeDtypeStruct((N_CHIPS,) + s.shape, s.dtype) for s in shards], 3 * n, copies)


def _pass_to_sibling(gathered):
    n = len(gathered)
    halves = [g.shape[2] // 2 for g in gathered]

    def body(*refs):
        out = refs[n:2 * n]
        ssem, rsem = refs[2 * n:]
        x, y, c, chips = _place()
        sib = (x, y, 1 - c)

        def half(a, q, cc):
            return out[a].at[q, :, pl.ds(cc * halves[a], halves[a]), :]

        cps = []
        for a in range(n):
            for r, chip in enumerate(chips):
                q = 2 * chip[0] + chip[1]
                cps.append(_remote(half(a, q, c), half(a, q, c), ssem, rsem, a * 3 + r, sib))
        for cp in cps:
            cp.start()
        for a in range(n):
            for r, chip in enumerate(chips):
                q = 2 * chip[0] + chip[1]
                _remote(half(a, q, 1 - c), half(a, q, 1 - c), ssem, rsem, a * 3 + r, sib).wait_recv()
        for cp in cps:
            cp.wait_send()

    return pl.pallas_call(
        body, name="pass_to_sibling",
        in_specs=[HBM_SPEC] * n, out_specs=[HBM_SPEC] * n,
        out_shape=[jax.ShapeDtypeStruct(g.shape, g.dtype) for g in gathered],
        input_output_aliases={a: a for a in range(n)},
        scratch_shapes=[pltpu.SemaphoreType.DMA((3 * n,)), pltpu.SemaphoreType.DMA((3 * n,))],
        compiler_params=_params(has_side_effects=True),
    )(*gathered)


def _scatter_over_ici(parts):
    n = len(parts)

    def copies(src, out, ssem, rsem):
        x, y, c, chips = _place()
        res = []
        for a in range(n):
            for r, chip in enumerate(chips):
                cp = _remote(src[a].at[2 * chip[0] + chip[1]], out[a].at[r], ssem, rsem, a * 3 + r, (*chip, c))
                res.append((cp, cp))
        return res

    return _Hosted(list(parts), [jax.ShapeDtypeStruct((3,) + p.shape[1:], F32) for p in parts], 3 * n, copies)


def _swap_halves(grads):
    n = len(grads)
    halves = [g.shape[1] // 2 for g in grads]

    def body(*refs):
        src, out = refs[:n], refs[n:2 * n]
        ssem, rsem = refs[2 * n:]
        x, y, c, _ = _place()
        cps = [_remote(src[a].at[:, pl.ds((1 - c) * halves[a], halves[a]), :], out[a], ssem, rsem, a, (x, y, 1 - c))
               for a in range(n)]
        for cp in cps:
            cp.start()
        for cp in cps:
            cp.wait()

    return pl.pallas_call(
        body, name="swap_halves",
        in_specs=[HBM_SPEC] * n, out_specs=[HBM_SPEC] * n,
        out_shape=[jax.ShapeDtypeStruct((N_CHIPS, h, g.shape[2]), F32) for g, h in zip(grads, halves)],
        scratch_shapes=[pltpu.SemaphoreType.DMA((n,)), pltpu.SemaphoreType.DMA((n,))],
        compiler_params=_params(has_side_effects=True),
    )(*grads)


def _scatter_chips(parts):
    n = len(parts)

    def body(*refs):
        src, out = refs[:n], refs[n:2 * n]
        ssem, rsem = refs[2 * n:]
        x, y, c, chips = _place()
        cps = []
        for a in range(n):
            for r, chip in enumerate(chips):
                cps.append(_remote(src[a].at[2 * chip[0] + chip[1]], out[a].at[r], ssem, rsem, a * 3 + r, (*chip, c)))
        for cp in cps:
            cp.start()
        for cp in cps:
            cp.wait()

    return pl.pallas_call(
        body, name="scatter_chips",
        in_specs=[HBM_SPEC] * n, out_specs=[HBM_SPEC] * n,
        out_shape=[jax.ShapeDtypeStruct((3,) + p.shape[1:], F32) for p in parts],
        scratch_shapes=[pltpu.SemaphoreType.DMA((3 * n,)), pltpu.SemaphoreType.DMA((3 * n,))],
        compiler_params=_params(has_side_effects=True),
    )(*parts)


def _swap_reduced(reduced):
    n = len(reduced)

    def body(*refs):
        src, out = refs[:n], refs[n:2 * n]
        ssem, rsem = refs[2 * n:]
        x, y, c, _ = _place()
        cps = [_remote(src[a], out[a], ssem, rsem, a, (x, y, 1 - c)) for a in range(n)]
        for cp in cps:
            cp.start()
        for cp in cps:
            cp.wait()

    return pl.pallas_call(
        body, name="swap_reduced",
        in_specs=[HBM_SPEC] * n, out_specs=[HBM_SPEC] * n,
        out_shape=[jax.ShapeDtypeStruct(r.shape, F32) for r in reduced],
        scratch_shapes=[pltpu.SemaphoreType.DMA((n,)), pltpu.SemaphoreType.DMA((n,))],
        compiler_params=_params(has_side_effects=True),
    )(*reduced)


def _allreduce_small(buf):
    R, L = buf.shape

    def body(buf_ref, out_ref, pair_ref, chip_ref, ssem, rsem):
        x, y, c, chips = _place()
        me_q = 2 * x + y
        pair_ref[c] = buf_ref[...]
        to_sib = _remote(buf_ref, pair_ref.at[c], ssem, rsem, 0, (x, y, 1 - c))
        to_sib.start()
        _remote(buf_ref, pair_ref.at[1 - c], ssem, rsem, 0, (x, y, 1 - c)).wait_recv()
        chip_ref[me_q] = pair_ref[0] + pair_ref[1]
        cps = [_remote(chip_ref.at[me_q], chip_ref.at[me_q], ssem, rsem, 1 + r, (*chip, c))
               for r, chip in enumerate(chips)]
        for cp in cps:
            cp.start()
        for r, chip in enumerate(chips):
            q = 2 * chip[0] + chip[1]
            _remote(chip_ref.at[q], chip_ref.at[q], ssem, rsem, 1 + r, (*chip, c)).wait_recv()
        out_ref[...] = (chip_ref[0] + chip_ref[1]) + (chip_ref[2] + chip_ref[3])
        to_sib.wait_send()
        for cp in cps:
            cp.wait_send()

    return pl.pallas_call(
        body, name="allreduce_small",
        in_specs=[VMEM_SPEC], out_specs=VMEM_SPEC,
        out_shape=jax.ShapeDtypeStruct((R, L), F32),
        scratch_shapes=[pltpu.VMEM((2, R, L), F32), pltpu.VMEM((N_CHIPS, R, L), F32),
                        pltpu.SemaphoreType.DMA((4,)), pltpu.SemaphoreType.DMA((4,))],
        compiler_params=_params(has_side_effects=True),
    )(buf)


def _pack(arrays):
    flat = jnp.concatenate([a.reshape(-1) for a in arrays])
    pad = (-flat.shape[0]) % (8 * LANES)
    return jnp.pad(flat, (0, pad)).reshape(-1, LANES)


def _unpack(buf, like):
    flat = buf.reshape(-1)
    out, off = [], 0
    for a in like:
        out.append(flat[off:off + a.size].reshape(a.shape))
        off += a.size
    return out


def _block_diag(pw):
    rows = []
    for gi in range(len(POOL_WINDOWS)):
        blocks = [pw[gi] if gj == gi else jnp.zeros_like(pw[gi]) for gj in range(len(POOL_WINDOWS))]
        rows.append(jnp.concatenate(blocks, axis=1))
    return jnp.concatenate(rows, axis=0)


def kernel(x, norm1, w_in, pool_w, pool_scale, sg_norm, sg_w, sg_b, w_out, norm2, w_up, w_down, final_norm, loss_target, m_norm1, m_w_in, m_pool_w, m_pool_scale, m_sg_norm, m_sg_w, m_sg_b, m_w_out, m_norm2, m_w_up, m_w_down, m_final_norm, v_norm1, v_w_in, v_pool_w, v_pool_scale, v_sg_norm, v_sg_w, v_sg_b, v_w_out, v_norm2, v_w_up, v_w_down, v_final_norm):
    depth = norm1.shape[0]
    T = x.shape[1]
    xs = x.reshape(T, D_MODEL)
    target = loss_target.reshape(T, D_MODEL)

    c_idx = lax.axis_index("c")
    q_idx = 2 * lax.axis_index("x") + lax.axis_index("y")
    own = [w.astype(BF16) for w in (w_in, w_out, w_up, w_down)]
    first_in = _gather_weights([own[0][:1]])[0]
    rest = None

    def full(a, l, axis):
        if a == 0:
            got = first_in[:, 0] if l == 0 else rest[0][:, l - 1]
        else:
            got = rest[a][:, l]
        blocks = [jnp.where(q_idx == q, own[a][l], got[q]) for q in range(N_CHIPS)]
        return jnp.concatenate(blocks, axis=axis)

    def reduce_pairs(grads):
        parts = []
        for g, o in zip(grads, _swap_halves(grads)):
            h = o.shape[1]
            mine = lax.dynamic_slice_in_dim(g, c_idx * h, h, axis=1)
            parts.append(_elementwise(lambda a, b: a + b, "add_pair", [mine, o], 1)[0])
        return parts

    def reduce_chips(parts, arrived):
        out = []
        for p, r in zip(parts, arrived):
            mine = lax.dynamic_index_in_dim(p, q_idx, axis=0, keepdims=False)
            out.append(_elementwise(lambda a, b, c, d: (a + b) + (c + d), "add_chips", [mine, r[0], r[1], r[2]], 1)[0])
        return out

    tril = jnp.tril(jnp.ones((CHUNK, CHUNK), F32))
    saved = []
    cur = xs
    wi, wo, wu, wd = {}, {}, {}, {}
    for l in range(depth):
        wbd = _block_diag(pool_w[l]).astype(BF16)
        wm = sg_w[l] * tril
        wm_s = wm.reshape(SG_HEADS * CHUNK, CHUNK).astype(BF16)
        wmt_s = jnp.swapaxes(wm, 1, 2).reshape(SG_HEADS * CHUNK, CHUNK).astype(BF16)
        bias = jnp.repeat(sg_b[l].T, SB_HD, axis=1)
        n1, n2 = norm1[l][None], norm2[l][None]
        psc, sgn = pool_scale[l][None], sg_norm[l][None]
        wi[l] = full(0, l, 1)
        proj, h, qkv = _inproj_fwd(cur, n1, wi[l])
        ya = _pool_fwd(proj, wbd, psc)
        yb = _sg_fwd(proj, wm_s, bias, sgn)
        if l == 0:
            yc, over_ici = _attn_fwd(qkv, _gather_over_ici([own[0][1:], own[1], own[2], own[3]]))
            rest = _pass_to_sibling(over_ici)
        else:
            yc, _ = _attn_fwd(qkv)
        wo[l], wu[l], wd[l] = full(1, l, 0), full(2, l, 1), full(3, l, 0)
        x1, ymix = _outproj_fwd(cur, ya, yb, yc, wo[l])
        x2, h2, u, act = _mlp_fwd(x1, n2, wu[l], wd[l])
        saved.append(dict(x0=cur, x1=x1, proj=proj, h=h, qkv=qkv, yc=yc, ymix=ymix, h2=h2, u=u, act=act,
                          wbd=wbd, wm_s=wm_s, wmt_s=wmt_s, bias=bias, n1=n1, n2=n2, psc=psc, sgn=sgn))
        cur = x2

    loss_row, dcur, d_final = _loss_head(cur, final_norm[None], target)

    small = [None] * depth
    ready = []
    reduced = {}
    for l in reversed(range(depth)):
        s = saved[l]
        dx1, du, d_n2 = _mlp_bwd(dcur, s["x1"], s["n2"], s["u"], wu[l], wd[l])
        ready.append((2, l, _tn_matmul(s["h2"], du, "grad_w_up", n_split=N_CHIPS)))
        ready.append((3, l, _tn_matmul(s["act"], dcur, "grad_w_down")[0].reshape(N_CHIPS, D_FF // N_CHIPS, D_MODEL)))
        dymix = _nt_matmul(dx1, wo[l])
        ready.append((1, l, _tn_matmul(s["ymix"], dx1, "grad_w_out")[0].reshape(N_CHIPS, D_MODEL // N_CHIPS, D_MODEL)))
        da_in, d_wbd, d_psc = _pool_bwd(s["proj"], dymix, s["wbd"], s["psc"])
        du_pre, dv_pre, d_wm, d_bias, d_sgn = _sg_bwd(s["proj"], dymix, s["wm_s"], s["wmt_s"], s["bias"], s["sgn"])
        if l == 0:
            parts = reduce_pairs([g for _, _, g in ready])
            (dq, dk, dv), arrived = _attn_bwd(s["qkv"], s["yc"], dymix, _scatter_over_ici(parts))
            for (a, ll, _), r in zip(ready, reduce_chips(parts, arrived)):
                reduced[(a, ll)] = r
            ready = []
        else:
            (dq, dk, dv), _ = _attn_bwd(s["qkv"], s["yc"], dymix)
        dproj, dx0, d_n1 = _inproj_bwd([da_in, du_pre, dv_pre, dq, dk, dv], wi[l], s["x0"], s["n1"], dx1)
        g_in_l = _tn_matmul(s["h"], dproj, "grad_w_in")[0]
        ready.append((0, l, g_in_l.reshape(D_MODEL, N_CHIPS, IN_COLS // N_CHIPS).transpose(1, 0, 2)))
        d_pw = jnp.stack([d_wbd[gi * POOL_GW:(gi + 1) * POOL_GW, gi * POOL_GW:(gi + 1) * POOL_GW]
                          for gi in range(len(POOL_WINDOWS))])
        small[l] = dict(norm1=d_n1[0], pool_w=d_pw, pool_scale=d_psc[0], sg_norm=d_sgn[0],
                        sg_w=d_wm.reshape(SG_HEADS, CHUNK, CHUNK), sg_b=d_bias[:, :SG_HEADS].T, norm2=d_n2[0])
        dcur = dx0
    grad_x = dcur.reshape(x.shape)

    parts = reduce_pairs([g for _, _, g in ready])
    for (a, ll, _), r in zip(ready, reduce_chips(parts, _scatter_chips(parts))):
        reduced[(a, ll)] = r
    keys = sorted(reduced)
    theirs = dict(zip(keys, _swap_reduced([reduced[k] for k in keys])))

    def joined(a):
        layers = []
        for l in range(depth):
            mine, other = reduced[(a, l)], theirs[(a, l)]
            layers.append(jnp.where(c_idx == 0, jnp.concatenate([mine, other]), jnp.concatenate([other, mine])))
        return jnp.stack(layers)

    gw_in, gw_out, gw_up, gw_down = [joined(a) for a in range(4)]

    names = ["norm1", "pool_w", "pool_scale", "sg_norm", "sg_w", "sg_b", "norm2"]
    slot = jnp.zeros((1,), F32)
    small_w = [norm1, pool_w, pool_scale, sg_norm, sg_w, sg_b, norm2, final_norm, slot]
    small_m = [m_norm1, m_pool_w, m_pool_scale, m_sg_norm, m_sg_w, m_sg_b, m_norm2, m_final_norm, slot]
    small_v = [v_norm1, v_pool_w, v_pool_scale, v_sg_norm, v_sg_w, v_sg_b, v_norm2, v_final_norm, slot]
    small_g = [jnp.stack([small[l][k] for l in range(depth)]) for k in names] + [d_final[0], loss_row[0, :1]]
    g_packed = _allreduce_small(_pack(small_g))
    loss = _unpack(g_packed, small_w)[-1][0]
    s_delta, s_m, s_v = _elementwise(_adamw, "adamw_small", [_pack(small_w), g_packed, _pack(small_m), _pack(small_v)], 3)
    gs = dict(zip(names + ["final_norm"], _unpack(g_packed, small_w)))
    ds = dict(zip(names + ["final_norm"], _unpack(s_delta, small_w)))
    ms = dict(zip(names + ["final_norm"], _unpack(s_m, small_w)))
    vs = dict(zip(names + ["final_norm"], _unpack(s_v, small_w)))

    big_g = dict(w_in=gw_in, w_out=gw_out, w_up=gw_up, w_down=gw_down)
    big_w = dict(w_in=(w_in, m_w_in, v_w_in), w_out=(w_out, m_w_out, v_w_out),
                 w_up=(w_up, m_w_up, v_w_up), w_down=(w_down, m_w_down, v_w_down))
    for k, (w, m, v) in big_w.items():
        ds[k], ms[k], vs[k] = _elementwise(_adamw, "adamw_" + k, [w, big_g[k], m, v], 3)
        gs[k] = big_g[k]

    order = ["norm1", "w_in", "pool_w", "pool_scale", "sg_norm", "sg_w", "sg_b", "w_out", "norm2", "w_up", "w_down",
             "final_norm"]
    return (loss, grad_x, *[gs[k] for k in order], *[ds[k] for k in order], *[ms[k] for k in order],
            *[vs[k] for k in order])
```

```python
import functools

import jax
import jax.numpy as jnp
from jax import lax
from jax.experimental import pallas as pl
from jax.experimental.pallas import tpu as pltpu

F32 = jnp.float32
BF16 = jnp.bfloat16
MESH = pl.DeviceIdType.MESH
AXES = ("x", "y", "c")

EPS = 1e-6
D_MODEL = 1024
POOL_WIDTH = 256
SG_WIDTH = 256
SB_WIDTH = 512
POOL_WINDOWS = (2, 4, 8, 16)
POOL_GW = 64
POOL_HALO = 16
CHUNK = 128
SG_HEADS = 4
SB_HD = 64
SB_SCALE = 0.125
IN_COLS = 2304
QKV_OFF = 768
D_FF = 4096
N_CHIPS = 4
LANES = 128
VMEM_LIMIT = 56 * 1024 * 1024
ATTN_TILE = 256
UNDERFLOW = -104.0

ADAM_LR = 0.001
ADAM_B1 = 0.9
ADAM_B2 = 0.999
ADAM_EPS = 1e-08
ADAM_WD = 0.01
ADAM_STEP = 10

HBM_SPEC = pl.BlockSpec(memory_space=pl.ANY)
VMEM_SPEC = pl.BlockSpec(memory_space=pltpu.VMEM)


def _params(**kw):
    return pltpu.CompilerParams(vmem_limit_bytes=VMEM_LIMIT, **kw)


def _tile(n, pref):
    if n <= pref:
        return n
    for t in range(pref - pref % LANES, 0, -LANES):
        if n % t == 0:
            return t
    raise ValueError((n, pref))


def _nn(a, b):
    return jnp.dot(a, b, preferred_element_type=F32)


def _nt(a, b):
    return lax.dot_general(a, b, (((1,), (1,)), ((), ())), preferred_element_type=F32)


def _tn(a, b):
    return lax.dot_general(a, b, (((0,), (0,)), ((), ())), preferred_element_type=F32)


def _rms_fwd(x, g):
    r = lax.rsqrt(jnp.mean(x * x, axis=-1, keepdims=True) + EPS)
    xhat = x * r
    return xhat * g, xhat, r


def _rms_bwd(dy, xhat, r, g):
    dxhat = dy * g
    dx = r * (dxhat - xhat * jnp.mean(dxhat * xhat, axis=-1, keepdims=True))
    return dx, dy * xhat


_GELU_K = 0.7978845608028654
_GELU_C = 0.044715


def _gelu(x):
    return 0.5 * x * (1.0 + jnp.tanh(_GELU_K * (x + _GELU_C * x * x * x)))


def _gelu_grad(x):
    t = jnp.tanh(_GELU_K * (x + _GELU_C * x * x * x))
    return 0.5 * (1.0 + t) + 0.5 * x * (1.0 - t * t) * _GELU_K * (1.0 + 3.0 * _GELU_C * x * x)


def _inproj_fwd(x, g, w):
    T, D = x.shape
    N = w.shape[1]
    tt = _tile(T, 512)

    def body(x_ref, g_ref, w_ref, proj_ref, h_ref, qkv_ref):
        h, _, _ = _rms_fwd(x_ref[...], g_ref[...])
        hb = h.astype(BF16)
        h_ref[...] = hb
        p = _nn(hb, w_ref[...])
        proj_ref[...] = p
        qkv_ref[...] = p[:, QKV_OFF:].astype(BF16)

    return pl.pallas_call(
        body, name="inproj_fwd", grid=(T // tt,),
        in_specs=[pl.BlockSpec((tt, D), lambda i: (i, 0)), pl.BlockSpec((1, D), lambda i: (0, 0)),
                  pl.BlockSpec((D, N), lambda i: (0, 0))],
        out_specs=[pl.BlockSpec((tt, N), lambda i: (i, 0)), pl.BlockSpec((tt, D), lambda i: (i, 0)),
                   pl.BlockSpec((tt, N - QKV_OFF), lambda i: (i, 0))],
        out_shape=[jax.ShapeDtypeStruct((T, N), F32), jax.ShapeDtypeStruct((T, D), BF16),
                   jax.ShapeDtypeStruct((T, N - QKV_OFF), BF16)],
        compiler_params=_params(),
    )(x, g, w)


def _inproj_bwd(pieces, w, x, g, dres):
    T, D = x.shape
    N = w.shape[1]
    tt = _tile(T, 512)
    widths = [p.shape[1] for p in pieces]
    offs = [sum(widths[:k]) for k in range(len(widths))]
    assert sum(widths) == N
    n_p = len(pieces)

    def body(*refs):
        p_refs = refs[:n_p]
        w_ref, x_ref, g_ref, dres_ref, dproj_ref, dx_ref, dg_ref = refs[n_p:]
        for p_ref, o, wd in zip(p_refs, offs, widths):
            dproj_ref[:, o:o + wd] = p_ref[...].astype(BF16)
        dh = _nt(dproj_ref[...], w_ref[...])
        gv = g_ref[...]
        _, xhat, r = _rms_fwd(x_ref[...], gv)
        dx, dgrow = _rms_bwd(dh, xhat, r, gv)
        dx_ref[...] = dres_ref[...] + dx

        @pl.when(pl.program_id(0) == 0)
        def _():
            dg_ref[...] = jnp.zeros_like(dg_ref)

        dg_ref[...] += jnp.sum(dgrow, axis=0, keepdims=True)

    return pl.pallas_call(
        body, name="inproj_bwd", grid=(T // tt,),
        in_specs=[pl.BlockSpec((tt, wd), lambda i: (i, 0)) for wd in widths] + [
            pl.BlockSpec((D, N), lambda i: (0, 0)), pl.BlockSpec((tt, D), lambda i: (i, 0)),
            pl.BlockSpec((1, D), lambda i: (0, 0)), pl.BlockSpec((tt, D), lambda i: (i, 0))],
        out_specs=[pl.BlockSpec((tt, N), lambda i: (i, 0)), pl.BlockSpec((tt, D), lambda i: (i, 0)),
                   pl.BlockSpec((1, D), lambda i: (0, 0))],
        out_shape=[jax.ShapeDtypeStruct((T, N), BF16), jax.ShapeDtypeStruct((T, D), F32),
                   jax.ShapeDtypeStruct((1, D), F32)],
        compiler_params=_params(),
    )(*pieces, w, x, g, dres)


def _pool_select(s2, s4, s8, s16, grp):
    return jnp.where(grp == 0, s2, jnp.where(grp == 1, s4, jnp.where(grp == 2, s8, s16)))


def _pool_count(t_glob, grp):
    win = jnp.where(grp == 0, 2, jnp.where(grp == 1, 4, jnp.where(grp == 2, 8, 16)))
    return jnp.minimum(t_glob + 1, win).astype(F32)


def _pool_diff(a, halo, base, tt):
    n = tt + POOL_HALO
    ext = jnp.concatenate([halo, a], axis=0)
    s2 = ext + pltpu.roll(ext, 1, 0)
    s4 = s2 + pltpu.roll(s2, 2, 0)
    s8 = s4 + pltpu.roll(s4, 4, 0)
    s16 = s8 + pltpu.roll(s8, 8, 0)
    grp = lax.broadcasted_iota(jnp.int32, (n, POOL_WIDTH), 1) // POOL_GW
    t_glob = lax.broadcasted_iota(jnp.int32, (n, POOL_WIDTH), 0) + (base - POOL_HALO)
    pooled = _pool_select(s2, s4, s8, s16, grp) / _pool_count(t_glob, grp)
    return pooled[POOL_HALO:] - a


def _pool_specs(T, tt):
    hb = tt // POOL_HALO
    return [pl.BlockSpec((tt, POOL_WIDTH), lambda i: (i, 0)),
            pl.BlockSpec((POOL_HALO, POOL_WIDTH), lambda i: (jnp.maximum(i * hb - 1, 0), 0))]


def _pool_fwd(proj, wbd, scale):
    T = proj.shape[0]
    tt = _tile(T, 512)

    def body(a_ref, halo_ref, w_ref, sc_ref, y_ref):
        i = pl.program_id(0)
        halo = jnp.where(i > 0, halo_ref[...], 0.0)
        d = _pool_diff(a_ref[...], halo, i * tt, tt)
        y_ref[...] = _nn(d.astype(BF16), w_ref[...]) * sc_ref[...]

    return pl.pallas_call(
        body, name="pool_fwd", grid=(T // tt,),
        in_specs=_pool_specs(T, tt) + [pl.BlockSpec((POOL_WIDTH, POOL_WIDTH), lambda i: (0, 0)),
                                       pl.BlockSpec((1, POOL_WIDTH), lambda i: (0, 0))],
        out_specs=pl.BlockSpec((tt, POOL_WIDTH), lambda i: (i, 0)),
        out_shape=jax.ShapeDtypeStruct((T, POOL_WIDTH), F32),
        compiler_params=_params(),
    )(proj, proj, wbd, scale)


def _pool_bwd(proj, dymix, wbd, scale):
    T = proj.shape[0]
    tt = _tile(T, 512)
    hb = tt // POOL_HALO
    nblk = T // tt
    n = tt + POOL_HALO

    def body(a_ref, halo_ref, dy_ref, dyn_ref, w_ref, sc_ref, da_ref, dw_ref, dsc_ref):
        i = pl.program_id(0)
        halo = jnp.where(i > 0, halo_ref[...], 0.0)
        d = _pool_diff(a_ref[...], halo, i * tt, tt)
        db = d.astype(BF16)
        wv = w_ref[...]
        sc = sc_ref[...]
        dy = dy_ref[...]
        dys = dy * sc

        @pl.when(i == 0)
        def _():
            dw_ref[...] = jnp.zeros_like(dw_ref)
            dsc_ref[...] = jnp.zeros_like(dsc_ref)

        dsc_ref[...] += jnp.sum(dy * _nn(db, wv), axis=0, keepdims=True)
        dw_ref[...] += _tn(db, dys.astype(BF16))
        dyn = jnp.where(i < nblk - 1, dyn_ref[...], 0.0) * sc
        dd = _nt(jnp.concatenate([dys, dyn], axis=0).astype(BF16), wv)
        grp = lax.broadcasted_iota(jnp.int32, (n, POOL_WIDTH), 1) // POOL_GW
        t_glob = lax.broadcasted_iota(jnp.int32, (n, POOL_WIDTH), 0) + i * tt
        e = dd / _pool_count(t_glob, grp)
        r2 = e + pltpu.roll(e, n - 1, 0)
        r4 = r2 + pltpu.roll(r2, n - 2, 0)
        r8 = r4 + pltpu.roll(r4, n - 4, 0)
        r16 = r8 + pltpu.roll(r8, n - 8, 0)
        da_ref[...] = (_pool_select(r2, r4, r8, r16, grp) - dd)[:tt]

    return pl.pallas_call(
        body, name="pool_bwd", grid=(nblk,),
        in_specs=_pool_specs(T, tt) + [
            pl.BlockSpec((tt, POOL_WIDTH), lambda i: (i, 0)),
            pl.BlockSpec((POOL_HALO, POOL_WIDTH), lambda i: (jnp.minimum((i + 1) * hb, T // POOL_HALO - 1), 0)),
            pl.BlockSpec((POOL_WIDTH, POOL_WIDTH), lambda i: (0, 0)), pl.BlockSpec((1, POOL_WIDTH), lambda i: (0, 0))],
        out_specs=[pl.BlockSpec((tt, POOL_WIDTH), lambda i: (i, 0)),
                   pl.BlockSpec((POOL_WIDTH, POOL_WIDTH), lambda i: (0, 0)),
                   pl.BlockSpec((1, POOL_WIDTH), lambda i: (0, 0))],
        out_shape=[jax.ShapeDtypeStruct((T, POOL_WIDTH), F32),
                   jax.ShapeDtypeStruct((POOL_WIDTH, POOL_WIDTH), F32),
                   jax.ShapeDtypeStruct((1, POOL_WIDTH), F32)],
        compiler_params=_params(),
    )(proj, proj, dymix, dymix, wbd, scale)


def _head_select(stacked, grp):
    out = jnp.where(grp == 0, stacked[0:CHUNK], 0.0)
    for h in range(1, SG_HEADS):
        out = out + jnp.where(grp == h, stacked[h * CHUNK:(h + 1) * CHUNK], 0.0)
    return out


def _sg_specs(tt):
    return [pl.BlockSpec((tt, SG_WIDTH), lambda i: (i, 1)), pl.BlockSpec((tt, SG_WIDTH), lambda i: (i, 2))]


def _sg_fwd(proj, wm, bias, g):
    T = proj.shape[0]
    tt = _tile(T, 512)

    def body(u_ref, v_ref, wm_ref, b_ref, g_ref, y_ref):
        zu = _gelu(u_ref[...])
        vn, _, _ = _rms_fwd(_gelu(v_ref[...]), g_ref[...])
        grp = lax.broadcasted_iota(jnp.int32, (CHUNK, SG_WIDTH), 1) // SB_HD
        for n in range(tt // CHUNK):
            rows = slice(n * CHUNK, (n + 1) * CHUNK)
            sv = _head_select(_nn(wm_ref[...], vn[rows].astype(BF16)), grp) + b_ref[...]
            y_ref[rows, :] = zu[rows] * sv

    return pl.pallas_call(
        body, name="sg_fwd", grid=(T // tt,),
        in_specs=_sg_specs(tt) + [pl.BlockSpec((SG_HEADS * CHUNK, CHUNK), lambda i: (0, 0)),
                                  pl.BlockSpec((CHUNK, SG_WIDTH), lambda i: (0, 0)),
                                  pl.BlockSpec((1, SG_WIDTH), lambda i: (0, 0))],
        out_specs=pl.BlockSpec((tt, SG_WIDTH), lambda i: (i, 0)),
        out_shape=jax.ShapeDtypeStruct((T, SG_WIDTH), F32),
        compiler_params=_params(),
    )(proj, proj, wm, bias, g)


def _sg_bwd(proj, dymix, wm, wmt, bias, g):
    T = proj.shape[0]
    tt = _tile(T, 512)
    nblk = T // tt

    def body(u_ref, v_ref, dy_ref, wm_ref, wmt_ref, b_ref, g_ref,
             du_ref, dv_ref, dw_ref, db_ref, dg_ref, dvn_ref, dbias_ref):
        i = pl.program_id(0)
        up, vp = u_ref[...], v_ref[...]
        gv = g_ref[...]
        zu, zv = _gelu(up), _gelu(vp)
        vn, xhat, r = _rms_fwd(zv, gv)
        gu = _gelu_grad(up)
        grp = lax.broadcasted_iota(jnp.int32, (CHUNK, SG_WIDTH), 1) // SB_HD

        @pl.when(i == 0)
        def _():
            dw_ref[...] = jnp.zeros_like(dw_ref)
            dbias_ref[...] = jnp.zeros_like(dbias_ref)
            dg_ref[...] = jnp.zeros_like(dg_ref)

        for n in range(tt // CHUNK):
            rows = slice(n * CHUNK, (n + 1) * CHUNK)
            vc = vn[rows].astype(BF16)
            sv = _head_select(_nn(wm_ref[...], vc), grp) + b_ref[...]
            dy = dy_ref[rows, :]
            du_ref[rows, :] = dy * sv * gu[rows]
            dsv = dy * zu[rows]
            dsvb = dsv.astype(BF16)
            dvn_ref[rows, :] = _head_select(_nn(wmt_ref[...], dsvb), grp)
            stacked = jnp.concatenate([jnp.where(grp == h, dsv, 0.0) for h in range(SG_HEADS)], axis=0)
            dw_ref[...] += _nt(stacked.astype(BF16), vc)
            dbias_ref[...] += dsv

        dzv, dgrow = _rms_bwd(dvn_ref[...], xhat, r, gv)
        dg_ref[...] += jnp.sum(dgrow, axis=0, keepdims=True)
        dv_ref[...] = dzv * _gelu_grad(vp)

        @pl.when(i == nblk - 1)
        def _():
            t_i = lax.broadcasted_iota(jnp.int32, (SG_HEADS * CHUNK, CHUNK), 0) % CHUNK
            s_i = lax.broadcasted_iota(jnp.int32, (SG_HEADS * CHUNK, CHUNK), 1)
            dw_ref[...] = jnp.where(s_i <= t_i, dw_ref[...], 0.0)
            lane = lax.broadcasted_iota(jnp.int32, (CHUNK, LANES), 1)
            acc = jnp.zeros((CHUNK, LANES), F32)
            for h in range(SG_HEADS):
                tot = jnp.sum(jnp.where(grp == h, dbias_ref[...], 0.0), axis=1, keepdims=True)
                acc = acc + jnp.where(lane == h, tot, 0.0)
            db_ref[...] = acc

    return pl.pallas_call(
        body, name="sg_bwd", grid=(nblk,),
        in_specs=_sg_specs(tt) + [pl.BlockSpec((tt, SG_WIDTH), lambda i: (i, 1)),
                                  pl.BlockSpec((SG_HEADS * CHUNK, CHUNK), lambda i: (0, 0)),
                                  pl.BlockSpec((SG_HEADS * CHUNK, CHUNK), lambda i: (0, 0)),
                                  pl.BlockSpec((CHUNK, SG_WIDTH), lambda i: (0, 0)),
                                  pl.BlockSpec((1, SG_WIDTH), lambda i: (0, 0))],
        out_specs=[pl.BlockSpec((tt, SG_WIDTH), lambda i: (i, 0)), pl.BlockSpec((tt, SG_WIDTH), lambda i: (i, 0)),
                   pl.BlockSpec((SG_HEADS * CHUNK, CHUNK), lambda i: (0, 0)),
                   pl.BlockSpec((CHUNK, LANES), lambda i: (0, 0)), pl.BlockSpec((1, SG_WIDTH), lambda i: (0, 0))],
        out_shape=[jax.ShapeDtypeStruct((T, SG_WIDTH), F32), jax.ShapeDtypeStruct((T, SG_WIDTH), F32),
                   jax.ShapeDtypeStruct((SG_HEADS * CHUNK, CHUNK), F32),
                   jax.ShapeDtypeStruct((CHUNK, LANES), F32), jax.ShapeDtypeStruct((1, SG_WIDTH), F32)],
        scratch_shapes=[pltpu.VMEM((tt, SG_WIDTH), F32), pltpu.VMEM((CHUNK, SG_WIDTH), F32)],
        compiler_params=_params(),
    )(proj, proj, dymix, wm, wmt, bias, g)


def _split_dot(x, u):
    hi = x.astype(BF16)
    lo = (x - hi.astype(F32)).astype(BF16)
    return _nn(hi, u) + _nn(lo, u)


def _sb_logits(qs, kb):
    z = _nt(qs, kb)
    lb = jnp.minimum(z, 0.0) - jnp.log1p(jnp.exp(-jnp.abs(z)))
    return lb, lb - z


def _attn_qkv_specs(tq, T):
    base = (IN_COLS - 3 * SB_WIDTH - QKV_OFF) // LANES
    nb = SB_WIDTH // LANES
    return [pl.BlockSpec((tq, LANES), lambda p, i: (i, base + p)),
            pl.BlockSpec((T, LANES), lambda p, i: (0, base + nb + p)),
            pl.BlockSpec((T, LANES), lambda p, i: (0, base + 2 * nb + p))]


class _Hosted:
    def __init__(self, ins, out_shapes, n_sems, copies):
        self.ins, self.out_shapes, self.n_sems, self.copies = ins, out_shapes, n_sems, copies

    @property
    def n(self):
        return len(self.ins)

    def sems(self):
        return [pltpu.SemaphoreType.DMA((self.n_sems,)), pltpu.SemaphoreType.DMA((self.n_sems,))]

    def start(self, src, dst, ssem, rsem):
        for send, _ in self.copies(src, dst, ssem, rsem):
            send.start()

    def wait(self, src, dst, ssem, rsem):
        for send, recv in self.copies(src, dst, ssem, rsem):
            recv.wait_recv()
            send.wait_send()


def _host(hosted, refs, n_in, n_out, first, last):
    if hosted is None:
        return refs, lambda: None, lambda: None
    n = hosted.n
    own_in, h_in = refs[:n_in], refs[n_in:n_in + n]
    own_out, h_out = refs[n_in + n:n_in + n + n_out], refs[n_in + n + n_out:n_in + 2 * n + n_out]
    rest = refs[n_in + 2 * n + n_out:]
    ssem, rsem = rest[-2:]

    def start():
        @pl.when(first)
        def _():
            hosted.start(h_in, h_out, ssem, rsem)

    def wait():
        @pl.when(last)
        def _():
            hosted.wait(h_in, h_out, ssem, rsem)

    return own_in + own_out + rest[:-2], start, wait


def _attn_fwd(qkv, hosted=None):
    T = qkv.shape[0]
    tq = _tile(T, ATTN_TILE)
    n_p, nq = SB_WIDTH // LANES, T // tq

    def body(*refs):
        p, i = pl.program_id(0), pl.program_id(1)
        (q_ref, k_ref, v_ref, o_ref), start, wait = _host(
            hosted, refs, 3, 1, jnp.logical_and(p == 0, i == 0), jnp.logical_and(p == n_p - 1, i == nq - 1))
        start()
        lane = lax.broadcasted_iota(jnp.int32, (tq, LANES), 1)
        row = lax.broadcasted_iota(jnp.int32, (tq, tq), 0)
        col = lax.broadcasted_iota(jnp.int32, (tq, tq), 1)
        after = jnp.where(row > col, 1.0, 0.0).astype(BF16)
        valid = col < row
        q = q_ref[...].astype(F32)
        qh = [jnp.where((lane // SB_HD) == hh, q * SB_SCALE, 0.0).astype(BF16) for hh in range(2)]

        def tile(j, state, mask):
            ks = pl.ds(pl.multiple_of(j * tq, tq), tq)
            kb, vb = k_ref[ks, :], v_ref[ks, :]
            new = []
            for hh in range(2):
                carry, acc = state[hh]
                lb, lm = _sb_logits(qh[hh], kb)
                if mask is not None:
                    lm = jnp.where(mask, lm, 0.0)
                a = jnp.exp(lb + _split_dot(lm, after) + carry)
                if mask is not None:
                    a = jnp.where(mask, a, 0.0)
                new.append((carry + jnp.sum(lm, axis=1, keepdims=True), acc + _nn(a.astype(BF16), vb)))
            return tuple(new)

        def live(state):
            return jnp.maximum(jnp.max(state[0][0]), jnp.max(state[1][0]))

        zero = (jnp.zeros((tq, 1), F32), jnp.zeros((tq, LANES), F32))
        state = tile(i, (zero, zero), valid)
        state = tile(jnp.maximum(i - 1, 0), state, jnp.broadcast_to(i > 0, (tq, tq)))

        def cond(st):
            return jnp.logical_and(st[0] >= 0, st[2] > UNDERFLOW)

        def step(st):
            state = tile(st[0], st[1], None)
            return st[0] - 1, state, live(state)

        _, state, _ = lax.while_loop(cond, step, (i - 2, state, live(state)))
        o_ref[...] = jnp.where(lane < SB_HD, state[0][1], state[1][1])
        wait()

    h_ins = hosted.ins if hosted else []
    res = pl.pallas_call(
        body, name="attn_fwd_hosting" if hosted else "attn_fwd", grid=(n_p, nq),
        in_specs=_attn_qkv_specs(tq, T) + [HBM_SPEC] * len(h_ins),
        out_specs=[pl.BlockSpec((tq, LANES), lambda p, i: (i, p))] + [HBM_SPEC] * len(h_ins),
        out_shape=[jax.ShapeDtypeStruct((T, SB_WIDTH), F32)] + (hosted.out_shapes if hosted else []),
        scratch_shapes=hosted.sems() if hosted else [],
        compiler_params=_params(has_side_effects=hosted is not None),
    )(qkv, qkv, qkv, *h_ins)
    return res[0], res[1:]


def _attn_bwd(qkv, o, dymix, hosted=None):
    T = qkv.shape[0]
    tq = _tile(T, ATTN_TILE)
    n_p, nq = SB_WIDTH // LANES, T // tq
    yc_blk = (POOL_WIDTH + SG_WIDTH) // LANES

    def body(*refs):
        p, i = pl.program_id(0), pl.program_id(1)
        (q_ref, k_ref, v_ref, o_ref, do_ref, dq_ref, dk_ref, dv_ref), start, wait = _host(
            hosted, refs, 5, 3, jnp.logical_and(p == 0, i == 0), jnp.logical_and(p == n_p - 1, i == nq - 1))
        start()
        lane = lax.broadcasted_iota(jnp.int32, (tq, LANES), 1)
        row = lax.broadcasted_iota(jnp.int32, (tq, tq), 0)
        col = lax.broadcasted_iota(jnp.int32, (tq, tq), 1)
        after = jnp.where(row > col, 1.0, 0.0).astype(BF16)
        from_here = jnp.where(row >= col, 1.0, 0.0).astype(BF16)
        valid = col < row

        @pl.when(i == 0)
        def _():
            dk_ref[...] = jnp.zeros_like(dk_ref)
            dv_ref[...] = jnp.zeros_like(dv_ref)

        q = q_ref[...].astype(F32)
        ov = o_ref[...]
        dov = do_ref[...]
        heads = [(lane // SB_HD) == hh for hh in range(2)]
        qh = [jnp.where(h, q * SB_SCALE, 0.0).astype(BF16) for h in heads]
        dohb = [jnp.where(h, dov, 0.0).astype(BF16) for h in heads]
        delta = [jnp.sum(d.astype(F32) * ov, axis=1, keepdims=True) for d in dohb]

        def tile(j, state, mask):
            ks = pl.ds(pl.multiple_of(j * tq, tq), tq)
            kb, vb = k_ref[ks, :], v_ref[ks, :]
            new, dk, dv = [], None, None
            for hh in range(2):
                c_a, c_r, dqa = state[hh]
                lb, lm = _sb_logits(qh[hh], kb)
                if mask is not None:
                    lm = jnp.where(mask, lm, 0.0)
                a = jnp.exp(lb + _split_dot(lm, after) + c_a)
                if mask is not None:
                    a = jnp.where(mask, a, 0.0)
                ab = a.astype(BF16)
                sig = jnp.exp(lb)
                g = _nt(dohb[hh], vb) * ab.astype(F32)
                left = delta[hh] - (c_r + _split_dot(g, from_here))
                dz = g * (1.0 - sig) - left * sig
                if mask is not None:
                    dz = jnp.where(mask, dz, 0.0)
                dzb = dz.astype(BF16)
                dk_h, dv_h = _tn(dzb, qh[hh]), _tn(ab, dohb[hh])
                dk, dv = (dk_h, dv_h) if hh == 0 else (dk + dk_h, dv + dv_h)
                new.append((c_a + jnp.sum(lm, axis=1, keepdims=True), c_r + jnp.sum(g, axis=1, keepdims=True),
                            dqa + _nn(dzb, kb)))
            dk_ref[ks, :] += dk
            dv_ref[ks, :] += dv
            return tuple(new)

        def live(state):
            return jnp.maximum(jnp.max(state[0][0]), jnp.max(state[1][0]))

        zero = (jnp.zeros((tq, 1), F32), jnp.zeros((tq, 1), F32), jnp.zeros((tq, LANES), F32))
        state = tile(i, (zero, zero), valid)
        state = tile(jnp.maximum(i - 1, 0), state, jnp.broadcast_to(i > 0, (tq, tq)))

        def cond(st):
            return jnp.logical_and(st[0] >= 0, st[2] > UNDERFLOW)

        def step(st):
            state = tile(st[0], st[1], None)
            return st[0] - 1, state, live(state)

        _, state, _ = lax.while_loop(cond, step, (i - 2, state, live(state)))
        dq_ref[...] = jnp.where(lane < SB_HD, state[0][2], state[1][2]) * SB_SCALE
        wait()

    h_ins = hosted.ins if hosted else []
    res = pl.pallas_call(
        body, name="attn_bwd_hosting" if hosted else "attn_bwd", grid=(n_p, nq),
        in_specs=_attn_qkv_specs(tq, T) + [pl.BlockSpec((tq, LANES), lambda p, i: (i, p)),
                                           pl.BlockSpec((tq, LANES), lambda p, i: (i, yc_blk + p))]
        + [HBM_SPEC] * len(h_ins),
        out_specs=[pl.BlockSpec((tq, LANES), lambda p, i: (i, p)), pl.BlockSpec((T, LANES), lambda p, i: (0, p)),
                   pl.BlockSpec((T, LANES), lambda p, i: (0, p))] + [HBM_SPEC] * len(h_ins),
        out_shape=[jax.ShapeDtypeStruct((T, SB_WIDTH), F32)] * 3 + (hosted.out_shapes if hosted else []),
        scratch_shapes=hosted.sems() if hosted else [],
        compiler_params=_params(has_side_effects=hosted is not None),
    )(qkv, qkv, qkv, o, dymix, *h_ins)
    return res[:3], res[3:]


def _outproj_fwd(x, ya, yb, yc, w):
    T, D = x.shape
    tt = _tile(T, 512)

    def body(x_ref, ya_ref, yb_ref, yc_ref, w_ref, x1_ref, ymix_ref):
        ymix_ref[:, 0:POOL_WIDTH] = ya_ref[...].astype(BF16)
        ymix_ref[:, POOL_WIDTH:POOL_WIDTH + SG_WIDTH] = yb_ref[...].astype(BF16)
        ymix_ref[:, POOL_WIDTH + SG_WIDTH:] = yc_ref[...].astype(BF16)
        x1_ref[...] = x_ref[...] + _nn(ymix_ref[...], w_ref[...])

    row = lambda width: pl.BlockSpec((tt, width), lambda i: (i, 0))
    return pl.pallas_call(
        body, name="outproj_fwd", grid=(T // tt,),
        in_specs=[row(D), row(POOL_WIDTH), row(SG_WIDTH), row(SB_WIDTH), pl.BlockSpec((D, D), lambda i: (0, 0))],
        out_specs=[row(D), row(D)],
        out_shape=[jax.ShapeDtypeStruct((T, D), F32), jax.ShapeDtypeStruct((T, D), BF16)],
        compiler_params=_params(),
    )(x, ya, yb, yc, w)


def _nt_matmul(a, w):
    T, N = a.shape
    K = w.shape[0]
    tt = _tile(T, 512)

    def body(a_ref, w_ref, o_ref):
        o_ref[...] = _nt(a_ref[...].astype(BF16), w_ref[...])

    return pl.pallas_call(
        body, name="nt_matmul", grid=(T // tt,),
        in_specs=[pl.BlockSpec((tt, N), lambda i: (i, 0)), pl.BlockSpec((K, N), lambda i: (0, 0))],
        out_specs=pl.BlockSpec((tt, K), lambda i: (i, 0)),
        out_shape=jax.ShapeDtypeStruct((T, K), F32),
        compiler_params=_params(),
    )(a, w)


def _tn_matmul(a, b, name, n_split=1):
    T, K = a.shape
    N = b.shape[1]
    tk = _tile(K, 1024)
    tn = _tile(N // n_split, 1024)
    tt = _tile(T, 2048)
    nper = N // n_split // tn
    nt = T // tt

    def body(a_ref, b_ref, o_ref):
        @pl.when(pl.program_id(2) == 0)
        def _():
            o_ref[...] = jnp.zeros_like(o_ref)

        o_ref[...] += _tn(a_ref[...], b_ref[...].astype(BF16))

    return pl.pallas_call(
        body, name=name, grid=(K // tk, N // tn, nt),
        in_specs=[pl.BlockSpec((tt, tk), lambda k, n, t: (t, k)), pl.BlockSpec((tt, tn), lambda k, n, t: (t, n))],
        out_specs=pl.BlockSpec((None, tk, tn), lambda k, n, t: (n // nper, k, n % nper)),
        out_shape=jax.ShapeDtypeStruct((n_split, K, N // n_split), F32),
        compiler_params=_params(),
    )(a, b)


def _mlp_fwd(x, g, w_up, w_down, hosted=None):
    T, D = x.shape
    F = w_up.shape[1]
    tt = _tile(T, 1024)
    fc = _tile(F, 512)
    nc = F // fc
    nt = T // tt

    def body(*refs):
        i, c = pl.program_id(0), pl.program_id(1)
        (x_ref, g_ref, wu_ref, wd_ref, y_ref, h_ref, u_ref, a_ref), start, wait = _host(
            hosted, refs, 4, 4, jnp.logical_and(i == 0, c == 0), jnp.logical_and(i == nt - 1, c == nc - 1))
        start()

        @pl.when(c == 0)
        def _():
            xv = x_ref[...]
            h, _, _ = _rms_fwd(xv, g_ref[...])
            h_ref[...] = h.astype(BF16)
            y_ref[...] = xv

        u = _nn(h_ref[...], wu_ref[...])
        u_ref[...] = u.astype(BF16)
        a = jnp.square(jnp.maximum(u, 0.0)).astype(BF16)
        a_ref[...] = a
        y_ref[...] += _nn(a, wd_ref[...])
        wait()

    h_ins = hosted.ins if hosted else []
    res = pl.pallas_call(
        body, name="mlp_fwd_hosting" if hosted else "mlp_fwd", grid=(nt, nc),
        in_specs=[pl.BlockSpec((tt, D), lambda i, c: (i, 0)), pl.BlockSpec((1, D), lambda i, c: (0, 0)),
                  pl.BlockSpec((D, fc), lambda i, c: (0, c)), pl.BlockSpec((fc, D), lambda i, c: (c, 0))]
        + [HBM_SPEC] * len(h_ins),
        out_specs=[pl.BlockSpec((tt, D), lambda i, c: (i, 0)), pl.BlockSpec((tt, D), lambda i, c: (i, 0)),
                   pl.BlockSpec((tt, fc), lambda i, c: (i, c)), pl.BlockSpec((tt, fc), lambda i, c: (i, c))]
        + [HBM_SPEC] * len(h_ins),
        out_shape=[jax.ShapeDtypeStruct((T, D), F32), jax.ShapeDtypeStruct((T, D), BF16),
                   jax.ShapeDtypeStruct((T, F), BF16), jax.ShapeDtypeStruct((T, F), BF16)]
        + (hosted.out_shapes if hosted else []),
        scratch_shapes=hosted.sems() if hosted else [],
        compiler_params=_params(has_side_effects=hosted is not None),
    )(x, g, w_up, w_down, *h_ins)
    return res[:4], res[4:]


def _mlp_bwd(dy, x, g, u, w_up, w_down, hosted=None):
    T, D = x.shape
    F = w_up.shape[1]
    tt = _tile(T, 1024)
    fc = _tile(F, 512)
    nc = F // fc
    nt = T // tt

    def body(*refs):
        i, c = pl.program_id(0), pl.program_id(1)
        (dy_ref, x_ref, g_ref, u_ref, wu_ref, wd_ref, dx_ref, du_ref, dg_ref, dyb_ref, dh_ref), start, wait = _host(
            hosted, refs, 6, 3, jnp.logical_and(i == 0, c == 0), jnp.logical_and(i == nt - 1, c == nc - 1))
        start()

        @pl.when(c == 0)
        def _():
            dyb_ref[...] = dy_ref[...].astype(BF16)
            dh_ref[...] = jnp.zeros_like(dh_ref)

        @pl.when(jnp.logical_and(i == 0, c == 0))
        def _():
            dg_ref[...] = jnp.zeros_like(dg_ref)

        da = _nt(dyb_ref[...], wd_ref[...])
        du = (da * (2.0 * jnp.maximum(u_ref[...].astype(F32), 0.0))).astype(BF16)
        du_ref[...] = du
        dh_ref[...] += _nt(du, wu_ref[...])

        @pl.when(c == nc - 1)
        def _():
            gv = g_ref[...]
            _, xhat, r = _rms_fwd(x_ref[...], gv)
            dx, dgrow = _rms_bwd(dh_ref[...], xhat, r, gv)
            dx_ref[...] = dy_ref[...] + dx
            dg_ref[...] += jnp.sum(dgrow, axis=0, keepdims=True)

        wait()

    h_ins = hosted.ins if hosted else []
    res = pl.pallas_call(
        body, name="mlp_bwd_hosting" if hosted else "mlp_bwd", grid=(nt, nc),
        in_specs=[pl.BlockSpec((tt, D), lambda i, c: (i, 0)), pl.BlockSpec((tt, D), lambda i, c: (i, 0)),
                  pl.BlockSpec((1, D), lambda i, c: (0, 0)), pl.BlockSpec((tt, fc), lambda i, c: (i, c)),
                  pl.BlockSpec((D, fc), lambda i, c: (0, c)), pl.BlockSpec((fc, D), lambda i, c: (c, 0))]
        + [HBM_SPEC] * len(h_ins),
        out_specs=[pl.BlockSpec((tt, D), lambda i, c: (i, 0)), pl.BlockSpec((tt, fc), lambda i, c: (i, c)),
                   pl.BlockSpec((1, D), lambda i, c: (0, 0))] + [HBM_SPEC] * len(h_ins),
        out_shape=[jax.ShapeDtypeStruct((T, D), F32), jax.ShapeDtypeStruct((T, F), BF16),
                   jax.ShapeDtypeStruct((1, D), F32)] + (hosted.out_shapes if hosted else []),
        scratch_shapes=[pltpu.VMEM((tt, D), BF16), pltpu.VMEM((tt, D), F32)] + (hosted.sems() if hosted else []),
        compiler_params=_params(has_side_effects=hosted is not None),
    )(dy, x, g, u, w_up, w_down, *h_ins)
    return res[:3], res[3:]


def _loss_head(x, g, target):
    T, D = x.shape
    tt = _tile(T, 512)

    def body(x_ref, g_ref, t_ref, loss_ref, dx_ref, dg_ref):
        gv = g_ref[...]
        y, xhat, r = _rms_fwd(x_ref[...], gv)
        err = y - t_ref[...]
        dx, dgrow = _rms_bwd(err * (1.0 / D), xhat, r, gv)
        dx_ref[...] = dx

        @pl.when(pl.program_id(0) == 0)
        def _():
            loss_ref[...] = jnp.zeros_like(loss_ref)
            dg_ref[...] = jnp.zeros_like(dg_ref)

        loss_ref[...] += 0.5 * jnp.sum(jnp.mean(err * err, axis=-1, keepdims=True), axis=0, keepdims=True)
        dg_ref[...] += jnp.sum(dgrow, axis=0, keepdims=True)

    return pl.pallas_call(
        body, name="loss_head", grid=(T // tt,),
        in_specs=[pl.BlockSpec((tt, D), lambda i: (i, 0)), pl.BlockSpec((1, D), lambda i: (0, 0)),
                  pl.BlockSpec((tt, D), lambda i: (i, 0))],
        out_specs=[pl.BlockSpec((1, LANES), lambda i: (0, 0)), pl.BlockSpec((tt, D), lambda i: (i, 0)),
                   pl.BlockSpec((1, D), lambda i: (0, 0))],
        out_shape=[jax.ShapeDtypeStruct((1, LANES), F32), jax.ShapeDtypeStruct((T, D), F32),
                   jax.ShapeDtypeStruct((1, D), F32)],
        compiler_params=_params(),
    )(x, g, target)


def _rows(shape, pref=512):
    last = shape[-1]
    rows = 1
    for s in shape[:-1]:
        rows *= s
    tr = rows
    if rows * last > 256 * 1024:
        for cand in (pref, 256, 128, 64, 32, 16, 8):
            if rows % cand == 0:
                tr = cand
                break
    return rows, last, tr


def _elementwise(fn, name, ins, n_out, out_dtype=F32):
    shape = ins[0].shape
    rows, last, tr = _rows(shape)
    flat = [a.reshape(rows, last) for a in ins]
    n_in = len(ins)

    def body(*refs):
        res = fn(*[r[...] for r in refs[:n_in]])
        if n_out == 1:
            res = (res,)
        for r, v in zip(refs[n_in:], res):
            r[...] = v.astype(r.dtype)

    spec = pl.BlockSpec((tr, last), lambda i: (i, 0))
    outs = pl.pallas_call(
        body, name=name, grid=(rows // tr,),
        in_specs=[spec] * n_in, out_specs=[spec] * n_out,
        out_shape=[jax.ShapeDtypeStruct((rows, last), out_dtype)] * n_out,
        compiler_params=_params(),
    )(*flat)
    return [o.reshape(shape) for o in outs]


def _add_pair(g, o, c_idx):
    nq, R, C = g.shape
    h = R // 2
    tr = _tile(h, 512)
    nb = h // tr

    def body(c_ref, g_ref, o_ref, out_ref):
        out_ref[...] = g_ref[...] + o_ref[...]

    return pl.pallas_call(
        body, name="add_pair",
        grid_spec=pltpu.PrefetchScalarGridSpec(
            num_scalar_prefetch=1, grid=(nq, nb),
            in_specs=[pl.BlockSpec((None, tr, C), lambda q, i, c: (q, c[0] * nb + i, 0)),
                      pl.BlockSpec((None, tr, C), lambda q, i, c: (q, i, 0))],
            out_specs=pl.BlockSpec((None, tr, C), lambda q, i, c: (q, i, 0))),
        out_shape=jax.ShapeDtypeStruct((nq, h, C), F32),
        compiler_params=_params(),
    )(c_idx.astype(jnp.int32).reshape(1), g, o)


def _add_chips(p, r, q_idx):
    _, H, C = p.shape
    tr = _tile(H, 512)

    def body(q_ref, p_ref, r0_ref, r1_ref, r2_ref, out_ref):
        out_ref[...] = (p_ref[...] + r0_ref[...]) + (r1_ref[...] + r2_ref[...])

    def arrived(k):
        return pl.BlockSpec((None, tr, C), lambda i, q: (k, i, 0))

    return pl.pallas_call(
        body, name="add_chips",
        grid_spec=pltpu.PrefetchScalarGridSpec(
            num_scalar_prefetch=1, grid=(H // tr,),
            in_specs=[pl.BlockSpec((None, tr, C), lambda i, q: (q[0], i, 0)), arrived(0), arrived(1), arrived(2)],
            out_specs=pl.BlockSpec((tr, C), lambda i, q: (i, 0))),
        out_shape=jax.ShapeDtypeStruct((H, C), F32),
        compiler_params=_params(),
    )(q_idx.astype(jnp.int32).reshape(1), p, r, r, r)


def _adamw(w, g, m, v):
    m = ADAM_B1 * m + (1.0 - ADAM_B1) * g
    v = ADAM_B2 * v + (1.0 - ADAM_B2) * jnp.square(g)
    m_hat = m / (1.0 - ADAM_B1 ** ADAM_STEP)
    v_hat = v / (1.0 - ADAM_B2 ** ADAM_STEP)
    delta = -ADAM_LR * (m_hat / (jnp.sqrt(v_hat) + ADAM_EPS) + ADAM_WD * w)
    return delta, m, v


def _place():
    x, y, c = lax.axis_index("x"), lax.axis_index("y"), lax.axis_index("c")
    chips = [(1 - x, y), (x, 1 - y), (1 - x, 1 - y)]
    return x, y, c, chips


def _remote(src, dst, ssem, rsem, k, dev):
    return pltpu.make_async_remote_copy(src_ref=src, dst_ref=dst, send_sem=ssem.at[k], recv_sem=rsem.at[k],
                                        device_id=dev, device_id_type=MESH)


def _gather_weights(shards):
    n = len(shards)
    halves = [s.shape[1] // 2 for s in shards]

    def body(*refs):
        src, out = refs[:n], refs[n:2 * n]
        ssem, rsem = refs[2 * n:]
        x, y, c, chips = _place()
        me_q = 2 * x + y
        sib = (x, y, 1 - c)

        def half(a, q, cc):
            return out[a].at[q, :, pl.ds(cc * halves[a], halves[a]), :]

        first = []
        for a in range(n):
            mine = src[a].at[:, pl.ds(c * halves[a], halves[a]), :]
            for r, chip in enumerate(chips):
                first.append(_remote(mine, half(a, me_q, c), ssem, rsem, a * 3 + r, (*chip, c)))
        for cp in first:
            cp.start()
        passed = []
        for a in range(n):
            for r, chip in enumerate(chips):
                q = 2 * chip[0] + chip[1]
                k = a * 3 + r
                _remote(half(a, q, c), half(a, q, c), ssem, rsem, k, (*chip, c)).wait_recv()
                cp = _remote(half(a, q, c), half(a, q, c), ssem, rsem, 3 * n + k, sib)
                cp.start()
                passed.append(cp)
        for a in range(n):
            for r, chip in enumerate(chips):
                q = 2 * chip[0] + chip[1]
                _remote(half(a, q, 1 - c), half(a, q, 1 - c), ssem, rsem, 3 * n + a * 3 + r, sib).wait_recv()
        for cp in first + passed:
            cp.wait_send()

    return pl.pallas_call(
        body, name="gather_weights",
        in_specs=[HBM_SPEC] * n, out_specs=[HBM_SPEC] * n,
        out_shape=[jax.ShapeDtypeStruct((N_CHIPS,) + s.shape, s.dtype) for s in shards],
        scratch_shapes=[pltpu.SemaphoreType.DMA((6 * n,)), pltpu.SemaphoreType.DMA((6 * n,))],
        compiler_params=_params(has_side_effects=True),
    )(*shards)


def _gather_over_ici(shards):
    n = len(shards)
    halves = [s.shape[1] // 2 for s in shards]

    def copies(src, out, ssem, rsem):
        x, y, c, chips = _place()
        me_q = 2 * x + y
        res = []
        for a in range(n):
            rows = pl.ds(c * halves[a], halves[a])
            mine = src[a].at[:, rows, :]
            for r, chip in enumerate(chips):
                dev = (*chip, c)
                res.append((_remote(mine, out[a].at[me_q, :, rows, :], ssem, rsem, a * 3 + r, dev),
                            _remote(mine, out[a].at[2 * chip[0] + chip[1], :, rows, :], ssem, rsem, a * 3 + r, dev)))
        return res

    return _Hosted(list(shards), [jax.ShapeDtypeStruct((N_CHIPS,) + s.shape, s.dtype) for s in shards], 3 * n, copies)


def _pass_to_sibling(gathered):
    n = len(gathered)
    halves = [g.shape[2] // 2 for g in gathered]

    def body(*refs):
        out = refs[n:2 * n]
        ssem, rsem = refs[2 * n:]
        x, y, c, chips = _place()
        sib = (x, y, 1 - c)

        def half(a, q, cc):
            return out[a].at[q, :, pl.ds(cc * halves[a], halves[a]), :]

        cps = []
        for a in range(n):
            for r, chip in enumerate(chips):
                q = 2 * chip[0] + chip[1]
                cps.append(_remote(half(a, q, c), half(a, q, c), ssem, rsem, a * 3 + r, sib))
        for cp in cps:
            cp.start()
        for a in range(n):
            for r, chip in enumerate(chips):
                q = 2 * chip[0] + chip[1]
                _remote(half(a, q, 1 - c), half(a, q, 1 - c), ssem, rsem, a * 3 + r, sib).wait_recv()
        for cp in cps:
            cp.wait_send()

    return pl.pallas_call(
        body, name="pass_to_sibling",
        in_specs=[HBM_SPEC] * n, out_specs=[HBM_SPEC] * n,
        out_shape=[jax.ShapeDtypeStruct(g.shape, g.dtype) for g in gathered],
        input_output_aliases={a: a for a in range(n)},
        scratch_shapes=[pltpu.SemaphoreType.DMA((3 * n,)), pltpu.SemaphoreType.DMA((3 * n,))],
        compiler_params=_params(has_side_effects=True),
    )(*gathered)


def _scatter_over_ici(parts):
    n = len(parts)

    def copies(src, out, ssem, rsem):
        x, y, c, chips = _place()
        res = []
        for a in range(n):
            for r, chip in enumerate(chips):
                cp = _remote(src[a].at[2 * chip[0] + chip[1]], out[a].at[r], ssem, rsem, a * 3 + r, (*chip, c))
                res.append((cp, cp))
        return res

    return _Hosted(list(parts), [jax.ShapeDtypeStruct((3,) + p.shape[1:], F32) for p in parts], 3 * n, copies)


def _swap_over_d2d(grads):
    n = len(grads)
    halves = [g.shape[1] // 2 for g in grads]

    def copies(src, out, ssem, rsem):
        x, y, c, _ = _place()
        res = []
        for a in range(n):
            cp = _remote(src[a].at[:, pl.ds((1 - c) * halves[a], halves[a]), :], out[a], ssem, rsem, a, (x, y, 1 - c))
            res.append((cp, cp))
        return res

    return _Hosted(list(grads), [jax.ShapeDtypeStruct((N_CHIPS, h, g.shape[2]), F32) for g, h in zip(grads, halves)],
                   n, copies)


def _swap_halves(grads):
    n = len(grads)
    halves = [g.shape[1] // 2 for g in grads]

    def body(*refs):
        src, out = refs[:n], refs[n:2 * n]
        ssem, rsem = refs[2 * n:]
        x, y, c, _ = _place()
        cps = [_remote(src[a].at[:, pl.ds((1 - c) * halves[a], halves[a]), :], out[a], ssem, rsem, a, (x, y, 1 - c))
               for a in range(n)]
        for cp in cps:
            cp.start()
        for cp in cps:
            cp.wait()

    return pl.pallas_call(
        body, name="swap_halves",
        in_specs=[HBM_SPEC] * n, out_specs=[HBM_SPEC] * n,
        out_shape=[jax.ShapeDtypeStruct((N_CHIPS, h, g.shape[2]), F32) for g, h in zip(grads, halves)],
        scratch_shapes=[pltpu.SemaphoreType.DMA((n,)), pltpu.SemaphoreType.DMA((n,))],
        compiler_params=_params(has_side_effects=True),
    )(*grads)


def _scatter_chips(parts):
    n = len(parts)

    def body(*refs):
        src, out = refs[:n], refs[n:2 * n]
        ssem, rsem = refs[2 * n:]
        x, y, c, chips = _place()
        cps = []
        for a in range(n):
            for r, chip in enumerate(chips):
                cps.append(_remote(src[a].at[2 * chip[0] + chip[1]], out[a].at[r], ssem, rsem, a * 3 + r, (*chip, c)))
        for cp in cps:
            cp.start()
        for cp in cps:
            cp.wait()

    return pl.pallas_call(
        body, name="scatter_chips",
        in_specs=[HBM_SPEC] * n, out_specs=[HBM_SPEC] * n,
        out_shape=[jax.ShapeDtypeStruct((3,) + p.shape[1:], F32) for p in parts],
        scratch_shapes=[pltpu.SemaphoreType.DMA((3 * n,)), pltpu.SemaphoreType.DMA((3 * n,))],
        compiler_params=_params(has_side_effects=True),
    )(*parts)


def _swap_reduced(reduced):
    n = len(reduced)

    def body(*refs):
        src, out = refs[:n], refs[n:2 * n]
        ssem, rsem = refs[2 * n:]
        x, y, c, _ = _place()
        cps = [_remote(src[a], out[a], ssem, rsem, a, (x, y, 1 - c)) for a in range(n)]
        for cp in cps:
            cp.start()
        for cp in cps:
            cp.wait()

    return pl.pallas_call(
        body, name="swap_reduced",
        in_specs=[HBM_SPEC] * n, out_specs=[HBM_SPEC] * n,
        out_shape=[jax.ShapeDtypeStruct(r.shape, F32) for r in reduced],
        scratch_shapes=[pltpu.SemaphoreType.DMA((n,)), pltpu.SemaphoreType.DMA((n,))],
        compiler_params=_params(has_side_effects=True),
    )(*reduced)


def _allreduce_small(buf):
    R, L = buf.shape

    def body(buf_ref, out_ref, pair_ref, chip_ref, ssem, rsem):
        x, y, c, chips = _place()
        me_q = 2 * x + y
        pair_ref[c] = buf_ref[...]
        to_sib = _remote(buf_ref, pair_ref.at[c], ssem, rsem, 0, (x, y, 1 - c))
        to_sib.start()
        _remote(buf_ref, pair_ref.at[1 - c], ssem, rsem, 0, (x, y, 1 - c)).wait_recv()
        chip_ref[me_q] = pair_ref[0] + pair_ref[1]
        cps = [_remote(chip_ref.at[me_q], chip_ref.at[me_q], ssem, rsem, 1 + r, (*chip, c))
               for r, chip in enumerate(chips)]
        for cp in cps:
            cp.start()
        for r, chip in enumerate(chips):
            q = 2 * chip[0] + chip[1]
            _remote(chip_ref.at[q], chip_ref.at[q], ssem, rsem, 1 + r, (*chip, c)).wait_recv()
        out_ref[...] = (chip_ref[0] + chip_ref[1]) + (chip_ref[2] + chip_ref[3])
        to_sib.wait_send()
        for cp in cps:
            cp.wait_send()

    return pl.pallas_call(
        body, name="allreduce_small",
        in_specs=[VMEM_SPEC], out_specs=VMEM_SPEC,
        out_shape=jax.ShapeDtypeStruct((R, L), F32),
        scratch_shapes=[pltpu.VMEM((2, R, L), F32), pltpu.VMEM((N_CHIPS, R, L), F32),
                        pltpu.SemaphoreType.DMA((4,)), pltpu.SemaphoreType.DMA((4,))],
        compiler_params=_params(has_side_effects=True),
    )(buf)


def _pack(arrays):
    flat = jnp.concatenate([a.reshape(-1) for a in arrays])
    pad = (-flat.shape[0]) % (8 * LANES)
    return jnp.pad(flat, (0, pad)).reshape(-1, LANES)


def _unpack(buf, like):
    flat = buf.reshape(-1)
    out, off = [], 0
    for a in like:
        out.append(flat[off:off + a.size].reshape(a.shape))
        off += a.size
    return out


def _block_diag(pw):
    rows = []
    for gi in range(len(POOL_WINDOWS)):
        blocks = [pw[gi] if gj == gi else jnp.zeros_like(pw[gi]) for gj in range(len(POOL_WINDOWS))]
        rows.append(jnp.concatenate(blocks, axis=1))
    return jnp.concatenate(rows, axis=0)


def kernel(x, norm1, w_in, pool_w, pool_scale, sg_norm, sg_w, sg_b, w_out, norm2, w_up, w_down, final_norm, loss_target, m_norm1, m_w_in, m_pool_w, m_pool_scale, m_sg_norm, m_sg_w, m_sg_b, m_w_out, m_norm2, m_w_up, m_w_down, m_final_norm, v_norm1, v_w_in, v_pool_w, v_pool_scale, v_sg_norm, v_sg_w, v_sg_b, v_w_out, v_norm2, v_w_up, v_w_down, v_final_norm):
    depth = norm1.shape[0]
    T = x.shape[1]
    xs = x.reshape(T, D_MODEL)
    target = loss_target.reshape(T, D_MODEL)

    assert depth == 2
    c_idx = lax.axis_index("c")
    q_idx = 2 * lax.axis_index("x") + lax.axis_index("y")
    own = [w.astype(BF16) for w in (w_in, w_out, w_up, w_down)]
    gathered = {(0, 0): _gather_weights([own[0][:1]])[0]}

    def full(a, l, axis):
        blocks = [jnp.where(q_idx == q, own[a][l], gathered[(a, l)][q, 0]) for q in range(N_CHIPS)]
        return jnp.concatenate(blocks, axis=axis)

    def gather_behind(call, keys):
        res, over_ici = call(_gather_over_ici([own[a][l:l + 1] for a, l in keys]))
        gathered.update(zip(keys, _pass_to_sibling(over_ici)))
        return res

    tril = jnp.tril(jnp.ones((CHUNK, CHUNK), F32))
    saved = []
    cur = xs
    wi, wo, wu, wd = {}, {}, {}, {}
    for l in range(depth):
        wbd = _block_diag(pool_w[l]).astype(BF16)
        wm = sg_w[l] * tril
        wm_s = wm.reshape(SG_HEADS * CHUNK, CHUNK).astype(BF16)
        wmt_s = jnp.swapaxes(wm, 1, 2).reshape(SG_HEADS * CHUNK, CHUNK).astype(BF16)
        bias = jnp.repeat(sg_b[l].T, SB_HD, axis=1)
        n1, n2 = norm1[l][None], norm2[l][None]
        psc, sgn = pool_scale[l][None], sg_norm[l][None]
        wi[l] = full(0, l, 1)
        proj, h, qkv = _inproj_fwd(cur, n1, wi[l])
        ya = _pool_fwd(proj, wbd, psc)
        yb = _sg_fwd(proj, wm_s, bias, sgn)
        if l == 0:
            yc = gather_behind(lambda hosted: _attn_fwd(qkv, hosted), [(1, 0), (2, 0), (3, 0)])
        else:
            yc, _ = _attn_fwd(qkv)
        wo[l], wu[l], wd[l] = full(1, l, 0), full(2, l, 1), full(3, l, 0)
        x1, ymix = _outproj_fwd(cur, ya, yb, yc, wo[l])
        if l == 0:
            x2, h2, u, act = gather_behind(lambda hosted: _mlp_fwd(x1, n2, wu[l], wd[l], hosted),
                                           [(0, 1), (1, 1), (2, 1), (3, 1)])
        else:
            (x2, h2, u, act), _ = _mlp_fwd(x1, n2, wu[l], wd[l])
        saved.append(dict(x0=cur, x1=x1, proj=proj, h=h, qkv=qkv, yc=yc, ymix=ymix, h2=h2, u=u, act=act,
                          wbd=wbd, wm_s=wm_s, wmt_s=wmt_s, bias=bias, n1=n1, n2=n2, psc=psc, sgn=sgn))
        cur = x2

    loss_row, dcur, d_final = _loss_head(cur, final_norm[None], target)

    small = [None] * depth
    grads, parts, reduced = {}, {}, {}

    def pair_up(keys, swapped):
        parts.update({k: _add_pair(grads[k], o, c_idx) for k, o in zip(keys, swapped)})

    def chip_up(keys, arrived):
        reduced.update({k: _add_chips(parts[k], r, q_idx) for k, r in zip(keys, arrived)})

    for l in reversed(range(depth)):
        s = saved[l]
        if l == 0:
            keys = [(2, 1), (3, 1)]
            (dx1, du, d_n2), arrived = _mlp_bwd(dcur, s["x1"], s["n2"], s["u"], wu[l], wd[l],
                                                _scatter_over_ici([parts[k] for k in keys]))
            chip_up(keys, arrived)
        else:
            (dx1, du, d_n2), _ = _mlp_bwd(dcur, s["x1"], s["n2"], s["u"], wu[l], wd[l])
        grads[(2, l)] = _tn_matmul(s["h2"], du, "grad_w_up", n_split=N_CHIPS)
        grads[(3, l)] = _tn_matmul(s["act"], dcur, "grad_w_down")[0].reshape(N_CHIPS, D_FF // N_CHIPS, D_MODEL)
        dymix = _nt_matmul(dx1, wo[l])
        grads[(1, l)] = _tn_matmul(s["ymix"], dx1, "grad_w_out")[0].reshape(N_CHIPS, D_MODEL // N_CHIPS, D_MODEL)
        da_in, d_wbd, d_psc = _pool_bwd(s["proj"], dymix, s["wbd"], s["psc"])
        du_pre, dv_pre, d_wm, d_bias, d_sgn = _sg_bwd(s["proj"], dymix, s["wm_s"], s["wmt_s"], s["bias"], s["sgn"])
        if l == 0:
            keys = [(0, 1), (1, 0), (2, 0), (3, 0)]
            pair_up(keys, _swap_halves([grads[k] for k in keys]))
            keys = [(1, 1)] + keys
            (dq, dk, dv), arrived = _attn_bwd(s["qkv"], s["yc"], dymix, _scatter_over_ici([parts[k] for k in keys]))
            chip_up(keys, arrived)
        else:
            keys = [(1, l), (2, l), (3, l)]
            (dq, dk, dv), swapped = _attn_bwd(s["qkv"], s["yc"], dymix, _swap_over_d2d([grads[k] for k in keys]))
            pair_up(keys, swapped)
        dproj, dx0, d_n1 = _inproj_bwd([da_in, du_pre, dv_pre, dq, dk, dv], wi[l], s["x0"], s["n1"], dx1)
        g_in_l = _tn_matmul(s["h"], dproj, "grad_w_in")[0]
        grads[(0, l)] = g_in_l.reshape(D_MODEL, N_CHIPS, IN_COLS // N_CHIPS).transpose(1, 0, 2)
        d_pw = jnp.stack([d_wbd[gi * POOL_GW:(gi + 1) * POOL_GW, gi * POOL_GW:(gi + 1) * POOL_GW]
                          for gi in range(len(POOL_WINDOWS))])
        small[l] = dict(norm1=d_n1[0], pool_w=d_pw, pool_scale=d_psc[0], sg_norm=d_sgn[0],
                        sg_w=d_wm.reshape(SG_HEADS, CHUNK, CHUNK), sg_b=d_bias[:, :SG_HEADS].T, norm2=d_n2[0])
        dcur = dx0
    grad_x = dcur.reshape(x.shape)

    keys = [(0, 0)]
    pair_up(keys, _swap_halves([grads[k] for k in keys]))
    chip_up(keys, _scatter_chips([parts[k] for k in keys]))
    keys = sorted(reduced)
    theirs = dict(zip(keys, _swap_reduced([reduced[k] for k in keys])))

    def joined(a):
        layers = []
        for l in range(depth):
            mine, other = reduced[(a, l)], theirs[(a, l)]
            layers.append(jnp.where(c_idx == 0, jnp.concatenate([mine, other]), jnp.concatenate([other, mine])))
        return jnp.stack(layers)

    gw_in, gw_out, gw_up, gw_down = [joined(a) for a in range(4)]

    names = ["norm1", "pool_w", "pool_scale", "sg_norm", "sg_w", "sg_b", "norm2"]
    slot = jnp.zeros((1,), F32)
    small_w = [norm1, pool_w, pool_scale, sg_norm, sg_w, sg_b, norm2, final_norm, slot]
    small_m = [m_norm1, m_pool_w, m_pool_scale, m_sg_norm, m_sg_w, m_sg_b, m_norm2, m_final_norm, slot]
    small_v = [v_norm1, v_pool_w, v_pool_scale, v_sg_norm, v_sg_w, v_sg_b, v_norm2, v_final_norm, slot]
    small_g = [jnp.stack([small[l][k] for l in range(depth)]) for k in names] + [d_final[0], loss_row[0, :1]]
    g_packed = _allreduce_small(_pack(small_g))
    loss = _unpack(g_packed, small_w)[-1][0]
    s_delta, s_m, s_v = _elementwise(_adamw, "adamw_small", [_pack(small_w), g_packed, _pack(small_m), _pack(small_v)], 3)
    gs = dict(zip(names + ["final_norm"], _unpack(g_packed, small_w)))
    ds = dict(zip(names + ["final_norm"], _unpack(s_delta, small_w)))
    ms = dict(zip(names + ["final_norm"], _unpack(s_m, small_w)))
    vs = dict(zip(names + ["final_norm"], _unpack(s_v, small_w)))

    big_g = dict(w_in=gw_in, w_out=gw_out, w_up=gw_up, w_down=gw_down)
    big_w = dict(w_in=(w_in, m_w_in, v_w_in), w_out=(w_out, m_w_out, v_w_out),
                 w_up=(w_up, m_w_up, v_w_up), w_down=(w_down, m_w_down, v_w_down))
    for k, (w, m, v) in big_w.items():
        ds[k], ms[k], vs[k] = _elementwise(_adamw, "adamw_" + k, [w, big_g[k], m, v], 3)
        gs[k] = big_g[k]

    order = ["norm1", "w_in", "pool_w", "pool_scale", "sg_norm", "sg_w", "sg_b", "w_out", "norm2", "w_up", "w_down",
             "final_norm"]
    return (loss, grad_x, *[gs[k] for k in order], *[ds[k] for k in order], *[ms[k] for k in order],
            *[vs[k] for k in order])
```

```python
import functools

import jax
import jax.numpy as jnp
from jax import lax
from jax.experimental import pallas as pl
from jax.experimental.pallas import tpu as pltpu

F32 = jnp.float32
BF16 = jnp.bfloat16
MESH = pl.DeviceIdType.MESH
AXES = ("x", "y", "c")

EPS = 1e-6
D_MODEL = 1024
POOL_WIDTH = 256
SG_WIDTH = 256
SB_WIDTH = 512
POOL_WINDOWS = (2, 4, 8, 16)
POOL_GW = 64
POOL_HALO = 16
CHUNK = 128
SG_HEADS = 4
SB_HD = 64
SB_SCALE = 0.125
IN_COLS = 2304
QKV_OFF = 768
D_FF = 4096
N_CHIPS = 4
LANES = 128
VMEM_LIMIT = 56 * 1024 * 1024
ATTN_TILE = 256
UNDERFLOW = -104.0

ADAM_LR = 0.001
ADAM_B1 = 0.9
ADAM_B2 = 0.999
ADAM_EPS = 1e-08
ADAM_WD = 0.01
ADAM_STEP = 10

HBM_SPEC = pl.BlockSpec(memory_space=pl.ANY)
VMEM_SPEC = pl.BlockSpec(memory_space=pltpu.VMEM)


def _params(**kw):
    return pltpu.CompilerParams(vmem_limit_bytes=VMEM_LIMIT, **kw)


def _tile(n, pref):
    if n <= pref:
        return n
    for t in range(pref - pref % LANES, 0, -LANES):
        if n % t == 0:
            return t
    raise ValueError((n, pref))


def _nn(a, b):
    return jnp.dot(a, b, preferred_element_type=F32)


def _nt(a, b):
    return lax.dot_general(a, b, (((1,), (1,)), ((), ())), preferred_element_type=F32)


def _tn(a, b):
    return lax.dot_general(a, b, (((0,), (0,)), ((), ())), preferred_element_type=F32)


def _rms_fwd(x, g):
    r = lax.rsqrt(jnp.mean(x * x, axis=-1, keepdims=True) + EPS)
    xhat = x * r
    return xhat * g, xhat, r


def _rms_bwd(dy, xhat, r, g):
    dxhat = dy * g
    dx = r * (dxhat - xhat * jnp.mean(dxhat * xhat, axis=-1, keepdims=True))
    return dx, dy * xhat


_GELU_K = 0.7978845608028654
_GELU_C = 0.044715


def _gelu(x):
    return 0.5 * x * (1.0 + jnp.tanh(_GELU_K * (x + _GELU_C * x * x * x)))


def _gelu_grad(x):
    t = jnp.tanh(_GELU_K * (x + _GELU_C * x * x * x))
    return 0.5 * (1.0 + t) + 0.5 * x * (1.0 - t * t) * _GELU_K * (1.0 + 3.0 * _GELU_C * x * x)


def _inproj_fwd(x, g, w):
    T, D = x.shape
    N = w.shape[1]
    tt = _tile(T, 512)

    def body(x_ref, g_ref, w_ref, proj_ref, h_ref, qkv_ref):
        h, _, _ = _rms_fwd(x_ref[...], g_ref[...])
        hb = h.astype(BF16)
        h_ref[...] = hb
        p = _nn(hb, w_ref[...])
        proj_ref[...] = p
        qkv_ref[...] = p[:, QKV_OFF:].astype(BF16)

    return pl.pallas_call(
        body, name="inproj_fwd", grid=(T // tt,),
        in_specs=[pl.BlockSpec((tt, D), lambda i: (i, 0)), pl.BlockSpec((1, D), lambda i: (0, 0)),
                  pl.BlockSpec((D, N), lambda i: (0, 0))],
        out_specs=[pl.BlockSpec((tt, N), lambda i: (i, 0)), pl.BlockSpec((tt, D), lambda i: (i, 0)),
                   pl.BlockSpec((tt, N - QKV_OFF), lambda i: (i, 0))],
        out_shape=[jax.ShapeDtypeStruct((T, N), F32), jax.ShapeDtypeStruct((T, D), BF16),
                   jax.ShapeDtypeStruct((T, N - QKV_OFF), BF16)],
        compiler_params=_params(),
    )(x, g, w)


def _inproj_bwd(pieces, w, x, g, dres):
    T, D = x.shape
    N = w.shape[1]
    tt = _tile(T, 512)
    widths = [p.shape[1] for p in pieces]
    offs = [sum(widths[:k]) for k in range(len(widths))]
    assert sum(widths) == N
    n_p = len(pieces)

    def body(*refs):
        p_refs = refs[:n_p]
        w_ref, x_ref, g_ref, dres_ref, dproj_ref, dx_ref, dg_ref = refs[n_p:]
        for p_ref, o, wd in zip(p_refs, offs, widths):
            dproj_ref[:, o:o + wd] = p_ref[...].astype(BF16)
        dh = _nt(dproj_ref[...], w_ref[...])
        gv = g_ref[...]
        _, xhat, r = _rms_fwd(x_ref[...], gv)
        dx, dgrow = _rms_bwd(dh, xhat, r, gv)
        dx_ref[...] = dres_ref[...] + dx

        @pl.when(pl.program_id(0) == 0)
        def _():
            dg_ref[...] = jnp.zeros_like(dg_ref)

        dg_ref[...] += jnp.sum(dgrow, axis=0, keepdims=True)

    return pl.pallas_call(
        body, name="inproj_bwd", grid=(T // tt,),
        in_specs=[pl.BlockSpec((tt, wd), lambda i: (i, 0)) for wd in widths] + [
            pl.BlockSpec((D, N), lambda i: (0, 0)), pl.BlockSpec((tt, D), lambda i: (i, 0)),
            pl.BlockSpec((1, D), lambda i: (0, 0)), pl.BlockSpec((tt, D), lambda i: (i, 0))],
        out_specs=[pl.BlockSpec((tt, N), lambda i: (i, 0)), pl.BlockSpec((tt, D), lambda i: (i, 0)),
                   pl.BlockSpec((1, D), lambda i: (0, 0))],
        out_shape=[jax.ShapeDtypeStruct((T, N), BF16), jax.ShapeDtypeStruct((T, D), F32),
                   jax.ShapeDtypeStruct((1, D), F32)],
        compiler_params=_params(),
    )(*pieces, w, x, g, dres)


def _pool_select(s2, s4, s8, s16, grp):
    return jnp.where(grp == 0, s2, jnp.where(grp == 1, s4, jnp.where(grp == 2, s8, s16)))


def _pool_count(t_glob, grp):
    win = jnp.where(grp == 0, 2, jnp.where(grp == 1, 4, jnp.where(grp == 2, 8, 16)))
    return jnp.minimum(t_glob + 1, win).astype(F32)


def _pool_diff(a, halo, base, tt):
    n = tt + POOL_HALO
    ext = jnp.concatenate([halo, a], axis=0)
    s2 = ext + pltpu.roll(ext, 1, 0)
    s4 = s2 + pltpu.roll(s2, 2, 0)
    s8 = s4 + pltpu.roll(s4, 4, 0)
    s16 = s8 + pltpu.roll(s8, 8, 0)
    grp = lax.broadcasted_iota(jnp.int32, (n, POOL_WIDTH), 1) // POOL_GW
    t_glob = lax.broadcasted_iota(jnp.int32, (n, POOL_WIDTH), 0) + (base - POOL_HALO)
    pooled = _pool_select(s2, s4, s8, s16, grp) / _pool_count(t_glob, grp)
    return pooled[POOL_HALO:] - a


def _pool_specs(T, tt):
    hb = tt // POOL_HALO
    return [pl.BlockSpec((tt, POOL_WIDTH), lambda i: (i, 0)),
            pl.BlockSpec((POOL_HALO, POOL_WIDTH), lambda i: (jnp.maximum(i * hb - 1, 0), 0))]


def _pool_fwd(proj, wbd, scale):
    T = proj.shape[0]
    tt = _tile(T, 512)

    def body(a_ref, halo_ref, w_ref, sc_ref, y_ref):
        i = pl.program_id(0)
        halo = jnp.where(i > 0, halo_ref[...], 0.0)
        d = _pool_diff(a_ref[...], halo, i * tt, tt)
        y_ref[...] = _nn(d.astype(BF16), w_ref[...]) * sc_ref[...]

    return pl.pallas_call(
        body, name="pool_fwd", grid=(T // tt,),
        in_specs=_pool_specs(T, tt) + [pl.BlockSpec((POOL_WIDTH, POOL_WIDTH), lambda i: (0, 0)),
                                       pl.BlockSpec((1, POOL_WIDTH), lambda i: (0, 0))],
        out_specs=pl.BlockSpec((tt, POOL_WIDTH), lambda i: (i, 0)),
        out_shape=jax.ShapeDtypeStruct((T, POOL_WIDTH), F32),
        compiler_params=_params(),
    )(proj, proj, wbd, scale)


def _pool_bwd(proj, dymix, wbd, scale):
    T = proj.shape[0]
    tt = _tile(T, 512)
    hb = tt // POOL_HALO
    nblk = T // tt
    n = tt + POOL_HALO

    def body(a_ref, halo_ref, dy_ref, dyn_ref, w_ref, sc_ref, da_ref, dw_ref, dsc_ref):
        i = pl.program_id(0)
        halo = jnp.where(i > 0, halo_ref[...], 0.0)
        d = _pool_diff(a_ref[...], halo, i * tt, tt)
        db = d.astype(BF16)
        wv = w_ref[...]
        sc = sc_ref[...]
        dy = dy_ref[...]
        dys = dy * sc

        @pl.when(i == 0)
        def _():
            dw_ref[...] = jnp.zeros_like(dw_ref)
            dsc_ref[...] = jnp.zeros_like(dsc_ref)

        dsc_ref[...] += jnp.sum(dy * _nn(db, wv), axis=0, keepdims=True)
        dw_ref[...] += _tn(db, dys.astype(BF16))
        dyn = jnp.where(i < nblk - 1, dyn_ref[...], 0.0) * sc
        dd = _nt(jnp.concatenate([dys, dyn], axis=0).astype(BF16), wv)
        grp = lax.broadcasted_iota(jnp.int32, (n, POOL_WIDTH), 1) // POOL_GW
        t_glob = lax.broadcasted_iota(jnp.int32, (n, POOL_WIDTH), 0) + i * tt
        e = dd / _pool_count(t_glob, grp)
        r2 = e + pltpu.roll(e, n - 1, 0)
        r4 = r2 + pltpu.roll(r2, n - 2, 0)
        r8 = r4 + pltpu.roll(r4, n - 4, 0)
        r16 = r8 + pltpu.roll(r8, n - 8, 0)
        da_ref[...] = (_pool_select(r2, r4, r8, r16, grp) - dd)[:tt]

    return pl.pallas_call(
        body, name="pool_bwd", grid=(nblk,),
        in_specs=_pool_specs(T, tt) + [
            pl.BlockSpec((tt, POOL_WIDTH), lambda i: (i, 0)),
            pl.BlockSpec((POOL_HALO, POOL_WIDTH), lambda i: (jnp.minimum((i + 1) * hb, T // POOL_HALO - 1), 0)),
            pl.BlockSpec((POOL_WIDTH, POOL_WIDTH), lambda i: (0, 0)), pl.BlockSpec((1, POOL_WIDTH), lambda i: (0, 0))],
        out_specs=[pl.BlockSpec((tt, POOL_WIDTH), lambda i: (i, 0)),
                   pl.BlockSpec((POOL_WIDTH, POOL_WIDTH), lambda i: (0, 0)),
                   pl.BlockSpec((1, POOL_WIDTH), lambda i: (0, 0))],
        out_shape=[jax.ShapeDtypeStruct((T, POOL_WIDTH), F32),
                   jax.ShapeDtypeStruct((POOL_WIDTH, POOL_WIDTH), F32),
                   jax.ShapeDtypeStruct((1, POOL_WIDTH), F32)],
        compiler_params=_params(),
    )(proj, proj, dymix, dymix, wbd, scale)


def _head_select(stacked, grp):
    out = jnp.where(grp == 0, stacked[0:CHUNK], 0.0)
    for h in range(1, SG_HEADS):
        out = out + jnp.where(grp == h, stacked[h * CHUNK:(h + 1) * CHUNK], 0.0)
    return out


def _sg_specs(tt):
    return [pl.BlockSpec((tt, SG_WIDTH), lambda i: (i, 1)), pl.BlockSpec((tt, SG_WIDTH), lambda i: (i, 2))]


def _sg_fwd(proj, wm, bias, g):
    T = proj.shape[0]
    tt = _tile(T, 512)

    def body(u_ref, v_ref, wm_ref, b_ref, g_ref, y_ref):
        zu = _gelu(u_ref[...])
        vn, _, _ = _rms_fwd(_gelu(v_ref[...]), g_ref[...])
        grp = lax.broadcasted_iota(jnp.int32, (CHUNK, SG_WIDTH), 1) // SB_HD
        for n in range(tt // CHUNK):
            rows = slice(n * CHUNK, (n + 1) * CHUNK)
            sv = _head_select(_nn(wm_ref[...], vn[rows].astype(BF16)), grp) + b_ref[...]
            y_ref[rows, :] = zu[rows] * sv

    return pl.pallas_call(
        body, name="sg_fwd", grid=(T // tt,),
        in_specs=_sg_specs(tt) + [pl.BlockSpec((SG_HEADS * CHUNK, CHUNK), lambda i: (0, 0)),
                                  pl.BlockSpec((CHUNK, SG_WIDTH), lambda i: (0, 0)),
                                  pl.BlockSpec((1, SG_WIDTH), lambda i: (0, 0))],
        out_specs=pl.BlockSpec((tt, SG_WIDTH), lambda i: (i, 0)),
        out_shape=jax.ShapeDtypeStruct((T, SG_WIDTH), F32),
        compiler_params=_params(),
    )(proj, proj, wm, bias, g)


def _sg_bwd(proj, dymix, wm, wmt, bias, g):
    T = proj.shape[0]
    tt = _tile(T, 512)
    nblk = T // tt

    def body(u_ref, v_ref, dy_ref, wm_ref, wmt_ref, b_ref, g_ref,
             du_ref, dv_ref, dw_ref, db_ref, dg_ref, dvn_ref, dbias_ref):
        i = pl.program_id(0)
        up, vp = u_ref[...], v_ref[...]
        gv = g_ref[...]
        zu, zv = _gelu(up), _gelu(vp)
        vn, xhat, r = _rms_fwd(zv, gv)
        gu = _gelu_grad(up)
        grp = lax.broadcasted_iota(jnp.int32, (CHUNK, SG_WIDTH), 1) // SB_HD

        @pl.when(i == 0)
        def _():
            dw_ref[...] = jnp.zeros_like(dw_ref)
            dbias_ref[...] = jnp.zeros_like(dbias_ref)
            dg_ref[...] = jnp.zeros_like(dg_ref)

        for n in range(tt // CHUNK):
            rows = slice(n * CHUNK, (n + 1) * CHUNK)
            vc = vn[rows].astype(BF16)
            sv = _head_select(_nn(wm_ref[...], vc), grp) + b_ref[...]
            dy = dy_ref[rows, :]
            du_ref[rows, :] = dy * sv * gu[rows]
            dsv = dy * zu[rows]
            dsvb = dsv.astype(BF16)
            dvn_ref[rows, :] = _head_select(_nn(wmt_ref[...], dsvb), grp)
            stacked = jnp.concatenate([jnp.where(grp == h, dsv, 0.0) for h in range(SG_HEADS)], axis=0)
            dw_ref[...] += _nt(stacked.astype(BF16), vc)
            dbias_ref[...] += dsv

        dzv, dgrow = _rms_bwd(dvn_ref[...], xhat, r, gv)
        dg_ref[...] += jnp.sum(dgrow, axis=0, keepdims=True)
        dv_ref[...] = dzv * _gelu_grad(vp)

        @pl.when(i == nblk - 1)
        def _():
            t_i = lax.broadcasted_iota(jnp.int32, (SG_HEADS * CHUNK, CHUNK), 0) % CHUNK
            s_i = lax.broadcasted_iota(jnp.int32, (SG_HEADS * CHUNK, CHUNK), 1)
            dw_ref[...] = jnp.where(s_i <= t_i, dw_ref[...], 0.0)
            lane = lax.broadcasted_iota(jnp.int32, (CHUNK, LANES), 1)
            acc = jnp.zeros((CHUNK, LANES), F32)
            for h in range(SG_HEADS):
                tot = jnp.sum(jnp.where(grp == h, dbias_ref[...], 0.0), axis=1, keepdims=True)
                acc = acc + jnp.where(lane == h, tot, 0.0)
            db_ref[...] = acc

    return pl.pallas_call(
        body, name="sg_bwd", grid=(nblk,),
        in_specs=_sg_specs(tt) + [pl.BlockSpec((tt, SG_WIDTH), lambda i: (i, 1)),
                                  pl.BlockSpec((SG_HEADS * CHUNK, CHUNK), lambda i: (0, 0)),
                                  pl.BlockSpec((SG_HEADS * CHUNK, CHUNK), lambda i: (0, 0)),
                                  pl.BlockSpec((CHUNK, SG_WIDTH), lambda i: (0, 0)),
                                  pl.BlockSpec((1, SG_WIDTH), lambda i: (0, 0))],
        out_specs=[pl.BlockSpec((tt, SG_WIDTH), lambda i: (i, 0)), pl.BlockSpec((tt, SG_WIDTH), lambda i: (i, 0)),
                   pl.BlockSpec((SG_HEADS * CHUNK, CHUNK), lambda i: (0, 0)),
                   pl.BlockSpec((CHUNK, LANES), lambda i: (0, 0)), pl.BlockSpec((1, SG_WIDTH), lambda i: (0, 0))],
        out_shape=[jax.ShapeDtypeStruct((T, SG_WIDTH), F32), jax.ShapeDtypeStruct((T, SG_WIDTH), F32),
                   jax.ShapeDtypeStruct((SG_HEADS * CHUNK, CHUNK), F32),
                   jax.ShapeDtypeStruct((CHUNK, LANES), F32), jax.ShapeDtypeStruct((1, SG_WIDTH), F32)],
        scratch_shapes=[pltpu.VMEM((tt, SG_WIDTH), F32), pltpu.VMEM((CHUNK, SG_WIDTH), F32)],
        compiler_params=_params(),
    )(proj, proj, dymix, wm, wmt, bias, g)


def _split_dot(x, u):
    hi = x.astype(BF16)
    lo = (x - hi.astype(F32)).astype(BF16)
    return _nn(hi, u) + _nn(lo, u)


def _sb_logits(qs, kb):
    z = _nt(qs, kb)
    lb = jnp.minimum(z, 0.0) - jnp.log(1.0 + jnp.exp(-jnp.abs(z)))
    return lb, lb - z


def _attn_qkv_specs(tq, T):
    base = (IN_COLS - 3 * SB_WIDTH - QKV_OFF) // LANES
    nb = SB_WIDTH // LANES
    return [pl.BlockSpec((tq, LANES), lambda p, i: (i, base + p)),
            pl.BlockSpec((T, LANES), lambda p, i: (0, base + nb + p)),
            pl.BlockSpec((T, LANES), lambda p, i: (0, base + 2 * nb + p))]


class _Hosted:
    def __init__(self, ins, out_shapes, n_sems, copies):
        self.ins, self.out_shapes, self.n_sems, self.copies = ins, out_shapes, n_sems, copies

    @property
    def n(self):
        return len(self.ins)

    def sems(self):
        return [pltpu.SemaphoreType.DMA((self.n_sems,)), pltpu.SemaphoreType.DMA((self.n_sems,))]

    def start(self, src, dst, ssem, rsem):
        for send, _ in self.copies(src, dst, ssem, rsem):
            send.start()

    def wait(self, src, dst, ssem, rsem):
        for send, recv in self.copies(src, dst, ssem, rsem):
            recv.wait_recv()
            send.wait_send()


def _host(hosted, refs, n_in, n_out, first, last):
    if hosted is None:
        return refs, lambda: None, lambda: None
    n = hosted.n
    own_in, h_in = refs[:n_in], refs[n_in:n_in + n]
    own_out, h_out = refs[n_in + n:n_in + n + n_out], refs[n_in + n + n_out:n_in + 2 * n + n_out]
    rest = refs[n_in + 2 * n + n_out:]
    ssem, rsem = rest[-2:]

    def start():
        @pl.when(first)
        def _():
            hosted.start(h_in, h_out, ssem, rsem)

    def wait():
        @pl.when(last)
        def _():
            hosted.wait(h_in, h_out, ssem, rsem)

    return own_in + own_out + rest[:-2], start, wait


def _attn_fwd(qkv, hosted=None):
    T = qkv.shape[0]
    tq = _tile(T, ATTN_TILE)
    n_p, nq = SB_WIDTH // LANES, T // tq

    def body(*refs):
        p, i = pl.program_id(0), pl.program_id(1)
        (q_ref, k_ref, v_ref, o_ref), start, wait = _host(
            hosted, refs, 3, 1, jnp.logical_and(p == 0, i == 0), jnp.logical_and(p == n_p - 1, i == nq - 1))
        start()
        lane = lax.broadcasted_iota(jnp.int32, (tq, LANES), 1)
        row = lax.broadcasted_iota(jnp.int32, (tq, tq), 0)
        col = lax.broadcasted_iota(jnp.int32, (tq, tq), 1)
        after = jnp.where(row > col, 1.0, 0.0).astype(BF16)
        valid = col < row
        q = q_ref[...].astype(F32)
        qh = [jnp.where((lane // SB_HD) == hh, q * SB_SCALE, 0.0).astype(BF16) for hh in range(2)]

        def tile(j, state, mask):
            ks = pl.ds(pl.multiple_of(j * tq, tq), tq)
            kb, vb = k_ref[ks, :], v_ref[ks, :]
            new = []
            for hh in range(2):
                carry, acc = state[hh]
                lb, lm = _sb_logits(qh[hh], kb)
                if mask is not None:
                    lm = jnp.where(mask, lm, 0.0)
                a = jnp.exp(lb + _nn(lm.astype(BF16), after) + carry)
                if mask is not None:
                    a = jnp.where(mask, a, 0.0)
                new.append((carry + jnp.sum(lm, axis=1, keepdims=True), acc + _nn(a.astype(BF16), vb)))
            return tuple(new)

        def live(state):
            return jnp.maximum(jnp.max(state[0][0]), jnp.max(state[1][0]))

        zero = (jnp.zeros((tq, 1), F32), jnp.zeros((tq, LANES), F32))
        state = tile(i, (zero, zero), valid)
        state = tile(jnp.maximum(i - 1, 0), state, jnp.broadcast_to(i > 0, (tq, tq)))

        def cond(st):
            return jnp.logical_and(st[0] >= 0, st[2] > UNDERFLOW)

        def step(st):
            state = tile(st[0], st[1], None)
            return st[0] - 1, state, live(state)

        _, state, _ = lax.while_loop(cond, step, (i - 2, state, live(state)))
        o_ref[...] = jnp.where(lane < SB_HD, state[0][1], state[1][1])
        wait()

    h_ins = hosted.ins if hosted else []
    res = pl.pallas_call(
        body, name="attn_fwd_hosting" if hosted else "attn_fwd", grid=(n_p, nq),
        in_specs=_attn_qkv_specs(tq, T) + [HBM_SPEC] * len(h_ins),
        out_specs=[pl.BlockSpec((tq, LANES), lambda p, i: (i, p))] + [HBM_SPEC] * len(h_ins),
        out_shape=[jax.ShapeDtypeStruct((T, SB_WIDTH), F32)] + (hosted.out_shapes if hosted else []),
        scratch_shapes=hosted.sems() if hosted else [],
        compiler_params=_params(has_side_effects=hosted is not None),
    )(qkv, qkv, qkv, *h_ins)
    return res[0], res[1:]


def _attn_bwd(qkv, o, dymix, hosted=None):
    T = qkv.shape[0]
    tq = _tile(T, ATTN_TILE)
    n_p, nq = SB_WIDTH // LANES, T // tq
    yc_blk = (POOL_WIDTH + SG_WIDTH) // LANES

    def body(*refs):
        p, i = pl.program_id(0), pl.program_id(1)
        (q_ref, k_ref, v_ref, o_ref, do_ref, dq_ref, dk_ref, dv_ref), start, wait = _host(
            hosted, refs, 5, 3, jnp.logical_and(p == 0, i == 0), jnp.logical_and(p == n_p - 1, i == nq - 1))
        start()
        lane = lax.broadcasted_iota(jnp.int32, (tq, LANES), 1)
        row = lax.broadcasted_iota(jnp.int32, (tq, tq), 0)
        col = lax.broadcasted_iota(jnp.int32, (tq, tq), 1)
        after = jnp.where(row > col, 1.0, 0.0).astype(BF16)
        from_here = jnp.where(row >= col, 1.0, 0.0).astype(BF16)
        valid = col < row

        @pl.when(i == 0)
        def _():
            dk_ref[...] = jnp.zeros_like(dk_ref)
            dv_ref[...] = jnp.zeros_like(dv_ref)

        q = q_ref[...].astype(F32)
        ov = o_ref[...]
        dov = do_ref[...]
        heads = [(lane // SB_HD) == hh for hh in range(2)]
        qh = [jnp.where(h, q * SB_SCALE, 0.0).astype(BF16) for h in heads]
        dohb = [jnp.where(h, dov, 0.0).astype(BF16) for h in heads]
        delta = [jnp.sum(d.astype(F32) * ov, axis=1, keepdims=True) for d in dohb]

        def tile(j, state, mask):
            ks = pl.ds(pl.multiple_of(j * tq, tq), tq)
            kb, vb = k_ref[ks, :], v_ref[ks, :]
            new, dk, dv = [], None, None
            for hh in range(2):
                c_a, c_r, dqa = state[hh]
                lb, lm = _sb_logits(qh[hh], kb)
                if mask is not None:
                    lm = jnp.where(mask, lm, 0.0)
                a = jnp.exp(lb + _nn(lm.astype(BF16), after) + c_a)
                if mask is not None:
                    a = jnp.where(mask, a, 0.0)
                ab = a.astype(BF16)
                sig = jnp.exp(lb)
                g = _nt(dohb[hh], vb) * ab.astype(F32)
                left = delta[hh] - (c_r + _split_dot(g, from_here))
                dz = g * (1.0 - sig) - left * sig
                if mask is not None:
                    dz = jnp.where(mask, dz, 0.0)
                dzb = dz.astype(BF16)
                dk_h, dv_h = _tn(dzb, qh[hh]), _tn(ab, dohb[hh])
                dk, dv = (dk_h, dv_h) if hh == 0 else (dk + dk_h, dv + dv_h)
                new.append((c_a + jnp.sum(lm, axis=1, keepdims=True), c_r + jnp.sum(g, axis=1, keepdims=True),
                            dqa + _nn(dzb, kb)))
            dk_ref[ks, :] += dk
            dv_ref[ks, :] += dv
            return tuple(new)

        def live(state):
            return jnp.maximum(jnp.max(state[0][0]), jnp.max(state[1][0]))

        zero = (jnp.zeros((tq, 1), F32), jnp.zeros((tq, 1), F32), jnp.zeros((tq, LANES), F32))
        state = tile(i, (zero, zero), valid)
        state = tile(jnp.maximum(i - 1, 0), state, jnp.broadcast_to(i > 0, (tq, tq)))

        def cond(st):
            return jnp.logical_and(st[0] >= 0, st[2] > UNDERFLOW)

        def step(st):
            state = tile(st[0], st[1], None)
            return st[0] - 1, state, live(state)

        _, state, _ = lax.while_loop(cond, step, (i - 2, state, live(state)))
        dq_ref[...] = jnp.where(lane < SB_HD, state[0][2], state[1][2]) * SB_SCALE
        wait()

    h_ins = hosted.ins if hosted else []
    res = pl.pallas_call(
        body, name="attn_bwd_hosting" if hosted else "attn_bwd", grid=(n_p, nq),
        in_specs=_attn_qkv_specs(tq, T) + [pl.BlockSpec((tq, LANES), lambda p, i: (i, p)),
                                           pl.BlockSpec((tq, LANES), lambda p, i: (i, yc_blk + p))]
        + [HBM_SPEC] * len(h_ins),
        out_specs=[pl.BlockSpec((tq, LANES), lambda p, i: (i, p)), pl.BlockSpec((T, LANES), lambda p, i: (0, p)),
                   pl.BlockSpec((T, LANES), lambda p, i: (0, p))] + [HBM_SPEC] * len(h_ins),
        out_shape=[jax.ShapeDtypeStruct((T, SB_WIDTH), F32)] * 3 + (hosted.out_shapes if hosted else []),
        scratch_shapes=hosted.sems() if hosted else [],
        compiler_params=_params(has_side_effects=hosted is not None),
    )(qkv, qkv, qkv, o, dymix, *h_ins)
    return res[:3], res[3:]


def _outproj_fwd(x, ya, yb, yc, w):
    T, D = x.shape
    tt = _tile(T, 512)

    def body(x_ref, ya_ref, yb_ref, yc_ref, w_ref, x1_ref, ymix_ref):
        ymix_ref[:, 0:POOL_WIDTH] = ya_ref[...].astype(BF16)
        ymix_ref[:, POOL_WIDTH:POOL_WIDTH + SG_WIDTH] = yb_ref[...].astype(BF16)
        ymix_ref[:, POOL_WIDTH + SG_WIDTH:] = yc_ref[...].astype(BF16)
        x1_ref[...] = x_ref[...] + _nn(ymix_ref[...], w_ref[...])

    row = lambda width: pl.BlockSpec((tt, width), lambda i: (i, 0))
    return pl.pallas_call(
        body, name="outproj_fwd", grid=(T // tt,),
        in_specs=[row(D), row(POOL_WIDTH), row(SG_WIDTH), row(SB_WIDTH), pl.BlockSpec((D, D), lambda i: (0, 0))],
        out_specs=[row(D), row(D)],
        out_shape=[jax.ShapeDtypeStruct((T, D), F32), jax.ShapeDtypeStruct((T, D), BF16)],
        compiler_params=_params(),
    )(x, ya, yb, yc, w)


def _nt_matmul(a, w):
    T, N = a.shape
    K = w.shape[0]
    tt = _tile(T, 512)

    def body(a_ref, w_ref, o_ref):
        o_ref[...] = _nt(a_ref[...].astype(BF16), w_ref[...])

    return pl.pallas_call(
        body, name="nt_matmul", grid=(T // tt,),
        in_specs=[pl.BlockSpec((tt, N), lambda i: (i, 0)), pl.BlockSpec((K, N), lambda i: (0, 0))],
        out_specs=pl.BlockSpec((tt, K), lambda i: (i, 0)),
        out_shape=jax.ShapeDtypeStruct((T, K), F32),
        compiler_params=_params(),
    )(a, w)


def _tn_matmul(a, b, name, n_split=1):
    T, K = a.shape
    N = b.shape[1]
    tk = _tile(K, 1024)
    tn = _tile(N // n_split, 1024)
    tt = _tile(T, 2048)
    nper = N // n_split // tn
    nt = T // tt

    def body(a_ref, b_ref, o_ref):
        @pl.when(pl.program_id(2) == 0)
        def _():
            o_ref[...] = jnp.zeros_like(o_ref)

        o_ref[...] += _tn(a_ref[...], b_ref[...].astype(BF16))

    return pl.pallas_call(
        body, name=name, grid=(K // tk, N // tn, nt),
        in_specs=[pl.BlockSpec((tt, tk), lambda k, n, t: (t, k)), pl.BlockSpec((tt, tn), lambda k, n, t: (t, n))],
        out_specs=pl.BlockSpec((None, tk, tn), lambda k, n, t: (n // nper, k, n % nper)),
        out_shape=jax.ShapeDtypeStruct((n_split, K, N // n_split), F32),
        compiler_params=_params(),
    )(a, b)


def _mlp_fwd(x, g, w_up, w_down, hosted=None):
    T, D = x.shape
    F = w_up.shape[1]
    tt = _tile(T, 1024)
    fc = _tile(F, 512)
    nc = F // fc
    nt = T // tt

    def body(*refs):
        i, c = pl.program_id(0), pl.program_id(1)
        (x_ref, g_ref, wu_ref, wd_ref, y_ref, h_ref, u_ref, a_ref), start, wait = _host(
            hosted, refs, 4, 4, jnp.logical_and(i == 0, c == 0), jnp.logical_and(i == nt - 1, c == nc - 1))
        start()

        @pl.when(c == 0)
        def _():
            xv = x_ref[...]
            h, _, _ = _rms_fwd(xv, g_ref[...])
            h_ref[...] = h.astype(BF16)
            y_ref[...] = xv

        u = _nn(h_ref[...], wu_ref[...])
        u_ref[...] = u.astype(BF16)
        a = jnp.square(jnp.maximum(u, 0.0)).astype(BF16)
        a_ref[...] = a
        y_ref[...] += _nn(a, wd_ref[...])
        wait()

    h_ins = hosted.ins if hosted else []
    res = pl.pallas_call(
        body, name="mlp_fwd_hosting" if hosted else "mlp_fwd", grid=(nt, nc),
        in_specs=[pl.BlockSpec((tt, D), lambda i, c: (i, 0)), pl.BlockSpec((1, D), lambda i, c: (0, 0)),
                  pl.BlockSpec((D, fc), lambda i, c: (0, c)), pl.BlockSpec((fc, D), lambda i, c: (c, 0))]
        + [HBM_SPEC] * len(h_ins),
        out_specs=[pl.BlockSpec((tt, D), lambda i, c: (i, 0)), pl.BlockSpec((tt, D), lambda i, c: (i, 0)),
                   pl.BlockSpec((tt, fc), lambda i, c: (i, c)), pl.BlockSpec((tt, fc), lambda i, c: (i, c))]
        + [HBM_SPEC] * len(h_ins),
        out_shape=[jax.ShapeDtypeStruct((T, D), F32), jax.ShapeDtypeStruct((T, D), BF16),
                   jax.ShapeDtypeStruct((T, F), BF16), jax.ShapeDtypeStruct((T, F), BF16)]
        + (hosted.out_shapes if hosted else []),
        scratch_shapes=hosted.sems() if hosted else [],
        compiler_params=_params(has_side_effects=hosted is not None),
    )(x, g, w_up, w_down, *h_ins)
    return res[:4], res[4:]


def _mlp_bwd(dy, x, g, u, w_up, w_down, hosted=None):
    T, D = x.shape
    F = w_up.shape[1]
    tt = _tile(T, 1024)
    fc = _tile(F, 512)
    nc = F // fc
    nt = T // tt

    def body(*refs):
        i, c = pl.program_id(0), pl.program_id(1)
        (dy_ref, x_ref, g_ref, u_ref, wu_ref, wd_ref, dx_ref, du_ref, dg_ref, dyb_ref, dh_ref), start, wait = _host(
            hosted, refs, 6, 3, jnp.logical_and(i == 0, c == 0), jnp.logical_and(i == nt - 1, c == nc - 1))
        start()

        @pl.when(c == 0)
        def _():
            dyb_ref[...] = dy_ref[...].astype(BF16)
            dh_ref[...] = jnp.zeros_like(dh_ref)

        @pl.when(jnp.logical_and(i == 0, c == 0))
        def _():
            dg_ref[...] = jnp.zeros_like(dg_ref)

        da = _nt(dyb_ref[...], wd_ref[...])
        du = (da * (2.0 * jnp.maximum(u_ref[...].astype(F32), 0.0))).astype(BF16)
        du_ref[...] = du
        dh_ref[...] += _nt(du, wu_ref[...])

        @pl.when(c == nc - 1)
        def _():
            gv = g_ref[...]
            _, xhat, r = _rms_fwd(x_ref[...], gv)
            dx, dgrow = _rms_bwd(dh_ref[...], xhat, r, gv)
            dx_ref[...] = dy_ref[...] + dx
            dg_ref[...] += jnp.sum(dgrow, axis=0, keepdims=True)

        wait()

    h_ins = hosted.ins if hosted else []
    res = pl.pallas_call(
        body, name="mlp_bwd_hosting" if hosted else "mlp_bwd", grid=(nt, nc),
        in_specs=[pl.BlockSpec((tt, D), lambda i, c: (i, 0)), pl.BlockSpec((tt, D), lambda i, c: (i, 0)),
                  pl.BlockSpec((1, D), lambda i, c: (0, 0)), pl.BlockSpec((tt, fc), lambda i, c: (i, c)),
                  pl.BlockSpec((D, fc), lambda i, c: (0, c)), pl.BlockSpec((fc, D), lambda i, c: (c, 0))]
        + [HBM_SPEC] * len(h_ins),
        out_specs=[pl.BlockSpec((tt, D), lambda i, c: (i, 0)), pl.BlockSpec((tt, fc), lambda i, c: (i, c)),
                   pl.BlockSpec((1, D), lambda i, c: (0, 0))] + [HBM_SPEC] * len(h_ins),
        out_shape=[jax.ShapeDtypeStruct((T, D), F32), jax.ShapeDtypeStruct((T, F), BF16),
                   jax.ShapeDtypeStruct((1, D), F32)] + (hosted.out_shapes if hosted else []),
        scratch_shapes=[pltpu.VMEM((tt, D), BF16), pltpu.VMEM((tt, D), F32)] + (hosted.sems() if hosted else []),
        compiler_params=_params(has_side_effects=hosted is not None),
    )(dy, x, g, u, w_up, w_down, *h_ins)
    return res[:3], res[3:]


def _loss_head(x, g, target):
    T, D = x.shape
    tt = _tile(T, 512)

    def body(x_ref, g_ref, t_ref, loss_ref, dx_ref, dg_ref):
        gv = g_ref[...]
        y, xhat, r = _rms_fwd(x_ref[...], gv)
        err = y - t_ref[...]
        dx, dgrow = _rms_bwd(err * (1.0 / D), xhat, r, gv)
        dx_ref[...] = dx

        @pl.when(pl.program_id(0) == 0)
        def _():
            loss_ref[...] = jnp.zeros_like(loss_ref)
            dg_ref[...] = jnp.zeros_like(dg_ref)

        loss_ref[...] += 0.5 * jnp.sum(jnp.mean(err * err, axis=-1, keepdims=True), axis=0, keepdims=True)
        dg_ref[...] += jnp.sum(dgrow, axis=0, keepdims=True)

    return pl.pallas_call(
        body, name="loss_head", grid=(T // tt,),
        in_specs=[pl.BlockSpec((tt, D), lambda i: (i, 0)), pl.BlockSpec((1, D), lambda i: (0, 0)),
                  pl.BlockSpec((tt, D), lambda i: (i, 0))],
        out_specs=[pl.BlockSpec((1, LANES), lambda i: (0, 0)), pl.BlockSpec((tt, D), lambda i: (i, 0)),
                   pl.BlockSpec((1, D), lambda i: (0, 0))],
        out_shape=[jax.ShapeDtypeStruct((1, LANES), F32), jax.ShapeDtypeStruct((T, D), F32),
                   jax.ShapeDtypeStruct((1, D), F32)],
        compiler_params=_params(),
    )(x, g, target)


def _rows(shape, pref=512):
    last = shape[-1]
    rows = 1
    for s in shape[:-1]:
        rows *= s
    tr = rows
    if rows * last > 256 * 1024:
        for cand in (pref, 256, 128, 64, 32, 16, 8):
            if rows % cand == 0:
                tr = cand
                break
    return rows, last, tr


def _elementwise(fn, name, ins, n_out, out_dtype=F32):
    shape = ins[0].shape
    rows, last, tr = _rows(shape)
    flat = [a.reshape(rows, last) for a in ins]
    n_in = len(ins)

    def body(*refs):
        res = fn(*[r[...] for r in refs[:n_in]])
        if n_out == 1:
            res = (res,)
        for r, v in zip(refs[n_in:], res):
            r[...] = v.astype(r.dtype)

    spec = pl.BlockSpec((tr, last), lambda i: (i, 0))
    outs = pl.pallas_call(
        body, name=name, grid=(rows // tr,),
        in_specs=[spec] * n_in, out_specs=[spec] * n_out,
        out_shape=[jax.ShapeDtypeStruct((rows, last), out_dtype)] * n_out,
        compiler_params=_params(),
    )(*flat)
    return [o.reshape(shape) for o in outs]


def _add_pair(g, o, c_idx):
    nq, R, C = g.shape
    h = R // 2
    tr = _tile(h, 512)
    nb = h // tr

    def body(c_ref, g_ref, o_ref, out_ref):
        out_ref[...] = g_ref[...] + o_ref[...]

    return pl.pallas_call(
        body, name="add_pair",
        grid_spec=pltpu.PrefetchScalarGridSpec(
            num_scalar_prefetch=1, grid=(nq, nb),
            in_specs=[pl.BlockSpec((None, tr, C), lambda q, i, c: (q, c[0] * nb + i, 0)),
                      pl.BlockSpec((None, tr, C), lambda q, i, c: (q, i, 0))],
            out_specs=pl.BlockSpec((None, tr, C), lambda q, i, c: (q, i, 0))),
        out_shape=jax.ShapeDtypeStruct((nq, h, C), F32),
        compiler_params=_params(),
    )(c_idx.astype(jnp.int32).reshape(1), g, o)


def _add_chips(p, r, q_idx):
    _, H, C = p.shape
    tr = _tile(H, 512)

    def body(q_ref, p_ref, r0_ref, r1_ref, r2_ref, out_ref):
        out_ref[...] = (p_ref[...] + r0_ref[...]) + (r1_ref[...] + r2_ref[...])

    def arrived(k):
        return pl.BlockSpec((None, tr, C), lambda i, q: (k, i, 0))

    return pl.pallas_call(
        body, name="add_chips",
        grid_spec=pltpu.PrefetchScalarGridSpec(
            num_scalar_prefetch=1, grid=(H // tr,),
            in_specs=[pl.BlockSpec((None, tr, C), lambda i, q: (q[0], i, 0)), arrived(0), arrived(1), arrived(2)],
            out_specs=pl.BlockSpec((tr, C), lambda i, q: (i, 0))),
        out_shape=jax.ShapeDtypeStruct((H, C), F32),
        compiler_params=_params(),
    )(q_idx.astype(jnp.int32).reshape(1), p, r, r, r)


def _adamw(w, g, m, v):
    m = ADAM_B1 * m + (1.0 - ADAM_B1) * g
    v = ADAM_B2 * v + (1.0 - ADAM_B2) * jnp.square(g)
    m_hat = m / (1.0 - ADAM_B1 ** ADAM_STEP)
    v_hat = v / (1.0 - ADAM_B2 ** ADAM_STEP)
    delta = -ADAM_LR * (m_hat / (jnp.sqrt(v_hat) + ADAM_EPS) + ADAM_WD * w)
    return delta, m, v


def _place():
    x, y, c = lax.axis_index("x"), lax.axis_index("y"), lax.axis_index("c")
    chips = [(1 - x, y), (x, 1 - y), (1 - x, 1 - y)]
    return x, y, c, chips


def _remote(src, dst, ssem, rsem, k, dev):
    return pltpu.make_async_remote_copy(src_ref=src, dst_ref=dst, send_sem=ssem.at[k], recv_sem=rsem.at[k],
                                        device_id=dev, device_id_type=MESH)


def _gather_weights(shards):
    n = len(shards)
    halves = [s.shape[1] // 2 for s in shards]

    def body(*refs):
        src, out = refs[:n], refs[n:2 * n]
        ssem, rsem = refs[2 * n:]
        x, y, c, chips = _place()
        me_q = 2 * x + y
        sib = (x, y, 1 - c)

        def half(a, q, cc):
            return out[a].at[q, :, pl.ds(cc * halves[a], halves[a]), :]

        first = []
        for a in range(n):
            mine = src[a].at[:, pl.ds(c * halves[a], halves[a]), :]
            for r, chip in enumerate(chips):
                first.append(_remote(mine, half(a, me_q, c), ssem, rsem, a * 3 + r, (*chip, c)))
        for cp in first:
            cp.start()
        passed = []
        for a in range(n):
            for r, chip in enumerate(chips):
                q = 2 * chip[0] + chip[1]
                k = a * 3 + r
                _remote(half(a, q, c), half(a, q, c), ssem, rsem, k, (*chip, c)).wait_recv()
                cp = _remote(half(a, q, c), half(a, q, c), ssem, rsem, 3 * n + k, sib)
                cp.start()
                passed.append(cp)
        for a in range(n):
            for r, chip in enumerate(chips):
                q = 2 * chip[0] + chip[1]
                _remote(half(a, q, 1 - c), half(a, q, 1 - c), ssem, rsem, 3 * n + a * 3 + r, sib).wait_recv()
        for cp in first + passed:
            cp.wait_send()

    return pl.pallas_call(
        body, name="gather_weights",
        in_specs=[HBM_SPEC] * n, out_specs=[HBM_SPEC] * n,
        out_shape=[jax.ShapeDtypeStruct((N_CHIPS,) + s.shape, s.dtype) for s in shards],
        scratch_shapes=[pltpu.SemaphoreType.DMA((6 * n,)), pltpu.SemaphoreType.DMA((6 * n,))],
        compiler_params=_params(has_side_effects=True),
    )(*shards)


def _gather_over_ici(shards):
    n = len(shards)
    halves = [s.shape[1] // 2 for s in shards]

    def copies(src, out, ssem, rsem):
        x, y, c, chips = _place()
        me_q = 2 * x + y
        res = []
        for a in range(n):
            rows = pl.ds(c * halves[a], halves[a])
            mine = src[a].at[:, rows, :]
            for r, chip in enumerate(chips):
                dev = (*chip, c)
                res.append((_remote(mine, out[a].at[me_q, :, rows, :], ssem, rsem, a * 3 + r, dev),
                            _remote(mine, out[a].at[2 * chip[0] + chip[1], :, rows, :], ssem, rsem, a * 3 + r, dev)))
        return res

    return _Hosted(list(shards), [jax.ShapeDtypeStruct((N_CHIPS,) + s.shape, s.dtype) for s in shards], 3 * n, copies)


def _pass_to_sibling(gathered):
    n = len(gathered)
    halves = [g.shape[2] // 2 for g in gathered]

    def body(*refs):
        out = refs[n:2 * n]
        ssem, rsem = refs[2 * n:]
        x, y, c, chips = _place()
        sib = (x, y, 1 - c)

        def half(a, q, cc):
            return out[a].at[q, :, pl.ds(cc * halves[a], halves[a]), :]

        cps = []
        for a in range(n):
            for r, chip in enumerate(chips):
                q = 2 * chip[0] + chip[1]
                cps.append(_remote(half(a, q, c), half(a, q, c), ssem, rsem, a * 3 + r, sib))
        for cp in cps:
            cp.start()
        for a in range(n):
            for r, chip in enumerate(chips):
                q = 2 * chip[0] + chip[1]
                _remote(half(a, q, 1 - c), half(a, q, 1 - c), ssem, rsem, a * 3 + r, sib).wait_recv()
        for cp in cps:
            cp.wait_send()

    return pl.pallas_call(
        body, name="pass_to_sibling",
        in_specs=[HBM_SPEC] * n, out_specs=[HBM_SPEC] * n,
        out_shape=[jax.ShapeDtypeStruct(g.shape, g.dtype) for g in gathered],
        input_output_aliases={a: a for a in range(n)},
        scratch_shapes=[pltpu.SemaphoreType.DMA((3 * n,)), pltpu.SemaphoreType.DMA((3 * n,))],
        compiler_params=_params(has_side_effects=True),
    )(*gathered)


def _scatter_over_ici(parts):
    n = len(parts)

    def copies(src, out, ssem, rsem):
        x, y, c, chips = _place()
        res = []
        for a in range(n):
            for r, chip in enumerate(chips):
                cp = _remote(src[a].at[2 * chip[0] + chip[1]], out[a].at[r], ssem, rsem, a * 3 + r, (*chip, c))
                res.append((cp, cp))
        return res

    return _Hosted(list(parts), [jax.ShapeDtypeStruct((3,) + p.shape[1:], F32) for p in parts], 3 * n, copies)


def _swap_over_d2d(grads):
    n = len(grads)
    halves = [g.shape[1] // 2 for g in grads]

    def copies(src, out, ssem, rsem):
        x, y, c, _ = _place()
        res = []
        for a in range(n):
            cp = _remote(src[a].at[:, pl.ds((1 - c) * halves[a], halves[a]), :], out[a], ssem, rsem, a, (x, y, 1 - c))
            res.append((cp, cp))
        return res

    return _Hosted(list(grads), [jax.ShapeDtypeStruct((N_CHIPS, h, g.shape[2]), F32) for g, h in zip(grads, halves)],
                   n, copies)


def _swap_halves(grads):
    n = len(grads)
    halves = [g.shape[1] // 2 for g in grads]

    def body(*refs):
        src, out = refs[:n], refs[n:2 * n]
        ssem, rsem = refs[2 * n:]
        x, y, c, _ = _place()
        cps = [_remote(src[a].at[:, pl.ds((1 - c) * halves[a], halves[a]), :], out[a], ssem, rsem, a, (x, y, 1 - c))
               for a in range(n)]
        for cp in cps:
            cp.start()
        for cp in cps:
            cp.wait()

    return pl.pallas_call(
        body, name="swap_halves",
        in_specs=[HBM_SPEC] * n, out_specs=[HBM_SPEC] * n,
        out_shape=[jax.ShapeDtypeStruct((N_CHIPS, h, g.shape[2]), F32) for g, h in zip(grads, halves)],
        scratch_shapes=[pltpu.SemaphoreType.DMA((n,)), pltpu.SemaphoreType.DMA((n,))],
        compiler_params=_params(has_side_effects=True),
    )(*grads)


def _scatter_chips(parts):
    n = len(parts)

    def body(*refs):
        src, out = refs[:n], refs[n:2 * n]
        ssem, rsem = refs[2 * n:]
        x, y, c, chips = _place()
        cps = []
        for a in range(n):
            for r, chip in enumerate(chips):
                cps.append(_remote(src[a].at[2 * chip[0] + chip[1]], out[a].at[r], ssem, rsem, a * 3 + r, (*chip, c)))
        for cp in cps:
            cp.start()
        for cp in cps:
            cp.wait()

    return pl.pallas_call(
        body, name="scatter_chips",
        in_specs=[HBM_SPEC] * n, out_specs=[HBM_SPEC] * n,
        out_shape=[jax.ShapeDtypeStruct((3,) + p.shape[1:], F32) for p in parts],
        scratch_shapes=[pltpu.SemaphoreType.DMA((3 * n,)), pltpu.SemaphoreType.DMA((3 * n,))],
        compiler_params=_params(has_side_effects=True),
    )(*parts)


def _swap_reduced(reduced):
    n = len(reduced)

    def body(*refs):
        src, out = refs[:n], refs[n:2 * n]
        ssem, rsem = refs[2 * n:]
        x, y, c, _ = _place()
        cps = [_remote(src[a], out[a], ssem, rsem, a, (x, y, 1 - c)) for a in range(n)]
        for cp in cps:
            cp.start()
        for cp in cps:
            cp.wait()

    return pl.pallas_call(
        body, name="swap_reduced",
        in_specs=[HBM_SPEC] * n, out_specs=[HBM_SPEC] * n,
        out_shape=[jax.ShapeDtypeStruct(r.shape, F32) for r in reduced],
        scratch_shapes=[pltpu.SemaphoreType.DMA((n,)), pltpu.SemaphoreType.DMA((n,))],
        compiler_params=_params(has_side_effects=True),
    )(*reduced)


def _allreduce_small(buf):
    R, L = buf.shape

    def body(buf_ref, out_ref, pair_ref, chip_ref, ssem, rsem):
        x, y, c, chips = _place()
        me_q = 2 * x + y
        pair_ref[c] = buf_ref[...]
        to_sib = _remote(buf_ref, pair_ref.at[c], ssem, rsem, 0, (x, y, 1 - c))
        to_sib.start()
        _remote(buf_ref, pair_ref.at[1 - c], ssem, rsem, 0, (x, y, 1 - c)).wait_recv()
        chip_ref[me_q] = pair_ref[0] + pair_ref[1]
        cps = [_remote(chip_ref.at[me_q], chip_ref.at[me_q], ssem, rsem, 1 + r, (*chip, c))
               for r, chip in enumerate(chips)]
        for cp in cps:
            cp.start()
        for r, chip in enumerate(chips):
            q = 2 * chip[0] + chip[1]
            _remote(chip_ref.at[q], chip_ref.at[q], ssem, rsem, 1 + r, (*chip, c)).wait_recv()
        out_ref[...] = (chip_ref[0] + chip_ref[1]) + (chip_ref[2] + chip_ref[3])
        to_sib.wait_send()
        for cp in cps:
            cp.wait_send()

    return pl.pallas_call(
        body, name="allreduce_small",
        in_specs=[VMEM_SPEC], out_specs=VMEM_SPEC,
        out_shape=jax.ShapeDtypeStruct((R, L), F32),
        scratch_shapes=[pltpu.VMEM((2, R, L), F32), pltpu.VMEM((N_CHIPS, R, L), F32),
                        pltpu.SemaphoreType.DMA((4,)), pltpu.SemaphoreType.DMA((4,))],
        compiler_params=_params(has_side_effects=True),
    )(buf)


def _pack(arrays):
    flat = jnp.concatenate([a.reshape(-1) for a in arrays])
    pad = (-flat.shape[0]) % (8 * LANES)
    return jnp.pad(flat, (0, pad)).reshape(-1, LANES)


def _unpack(buf, like):
    flat = buf.reshape(-1)
    out, off = [], 0
    for a in like:
        out.append(flat[off:off + a.size].reshape(a.shape))
        off += a.size
    return out


def _block_diag(pw):
    rows = []
    for gi in range(len(POOL_WINDOWS)):
        blocks = [pw[gi] if gj == gi else jnp.zeros_like(pw[gi]) for gj in range(len(POOL_WINDOWS))]
        rows.append(jnp.concatenate(blocks, axis=1))
    return jnp.concatenate(rows, axis=0)


def kernel(x, norm1, w_in, pool_w, pool_scale, sg_norm, sg_w, sg_b, w_out, norm2, w_up, w_down, final_norm, loss_target, m_norm1, m_w_in, m_pool_w, m_pool_scale, m_sg_norm, m_sg_w, m_sg_b, m_w_out, m_norm2, m_w_up, m_w_down, m_final_norm, v_norm1, v_w_in, v_pool_w, v_pool_scale, v_sg_norm, v_sg_w, v_sg_b, v_w_out, v_norm2, v_w_up, v_w_down, v_final_norm):
    depth = norm1.shape[0]
    T = x.shape[1]
    xs = x.reshape(T, D_MODEL)
    target = loss_target.reshape(T, D_MODEL)

    assert depth == 2
    c_idx = lax.axis_index("c")
    q_idx = 2 * lax.axis_index("x") + lax.axis_index("y")
    own = [w.astype(BF16) for w in (w_in, w_out, w_up, w_down)]
    gathered = {(0, 0): _gather_weights([own[0][:1]])[0]}

    def full(a, l, axis):
        blocks = [jnp.where(q_idx == q, own[a][l], gathered[(a, l)][q, 0]) for q in range(N_CHIPS)]
        return jnp.concatenate(blocks, axis=axis)

    def gather_behind(call, keys):
        res, over_ici = call(_gather_over_ici([own[a][l:l + 1] for a, l in keys]))
        gathered.update(zip(keys, _pass_to_sibling(over_ici)))
        return res

    tril = jnp.tril(jnp.ones((CHUNK, CHUNK), F32))
    saved = []
    cur = xs
    wi, wo, wu, wd = {}, {}, {}, {}
    for l in range(depth):
        wbd = _block_diag(pool_w[l]).astype(BF16)
        wm = sg_w[l] * tril
        wm_s = wm.reshape(SG_HEADS * CHUNK, CHUNK).astype(BF16)
        wmt_s = jnp.swapaxes(wm, 1, 2).reshape(SG_HEADS * CHUNK, CHUNK).astype(BF16)
        bias = jnp.repeat(sg_b[l].T, SB_HD, axis=1)
        n1, n2 = norm1[l][None], norm2[l][None]
        psc, sgn = pool_scale[l][None], sg_norm[l][None]
        wi[l] = full(0, l, 1)
        proj, h, qkv = _inproj_fwd(cur, n1, wi[l])
        ya = _pool_fwd(proj, wbd, psc)
        yb = _sg_fwd(proj, wm_s, bias, sgn)
        if l == 0:
            yc = gather_behind(lambda hosted: _attn_fwd(qkv, hosted), [(1, 0), (2, 0), (3, 0)])
        else:
            yc, _ = _attn_fwd(qkv)
        wo[l], wu[l], wd[l] = full(1, l, 0), full(2, l, 1), full(3, l, 0)
        x1, ymix = _outproj_fwd(cur, ya, yb, yc, wo[l])
        if l == 0:
            x2, h2, u, act = gather_behind(lambda hosted: _mlp_fwd(x1, n2, wu[l], wd[l], hosted),
                                           [(0, 1), (1, 1), (2, 1), (3, 1)])
        else:
            (x2, h2, u, act), _ = _mlp_fwd(x1, n2, wu[l], wd[l])
        saved.append(dict(x0=cur, x1=x1, proj=proj, h=h, qkv=qkv, yc=yc, ymix=ymix, h2=h2, u=u, act=act,
                          wbd=wbd, wm_s=wm_s, wmt_s=wmt_s, bias=bias, n1=n1, n2=n2, psc=psc, sgn=sgn))
        cur = x2

    loss_row, dcur, d_final = _loss_head(cur, final_norm[None], target)

    small = [None] * depth
    grads, parts, reduced = {}, {}, {}

    def pair_up(keys, swapped):
        parts.update({k: _add_pair(grads[k], o, c_idx) for k, o in zip(keys, swapped)})

    def chip_up(keys, arrived):
        reduced.update({k: _add_chips(parts[k], r, q_idx) for k, r in zip(keys, arrived)})

    for l in reversed(range(depth)):
        s = saved[l]
        if l == 0:
            keys = [(2, 1), (3, 1)]
            (dx1, du, d_n2), arrived = _mlp_bwd(dcur, s["x1"], s["n2"], s["u"], wu[l], wd[l],
                                                _scatter_over_ici([parts[k] for k in keys]))
            chip_up(keys, arrived)
        else:
            (dx1, du, d_n2), _ = _mlp_bwd(dcur, s["x1"], s["n2"], s["u"], wu[l], wd[l])
        grads[(2, l)] = _tn_matmul(s["h2"], du, "grad_w_up", n_split=N_CHIPS)
        grads[(3, l)] = _tn_matmul(s["act"], dcur, "grad_w_down")[0].reshape(N_CHIPS, D_FF // N_CHIPS, D_MODEL)
        dymix = _nt_matmul(dx1, wo[l])
        grads[(1, l)] = _tn_matmul(s["ymix"], dx1, "grad_w_out")[0].reshape(N_CHIPS, D_MODEL // N_CHIPS, D_MODEL)
        da_in, d_wbd, d_psc = _pool_bwd(s["proj"], dymix, s["wbd"], s["psc"])
        du_pre, dv_pre, d_wm, d_bias, d_sgn = _sg_bwd(s["proj"], dymix, s["wm_s"], s["wmt_s"], s["bias"], s["sgn"])
        if l == 0:
            keys = [(0, 1), (1, 0), (2, 0), (3, 0)]
            pair_up(keys, _swap_halves([grads[k] for k in keys]))
            keys = [(1, 1)] + keys
            (dq, dk, dv), arrived = _attn_bwd(s["qkv"], s["yc"], dymix, _scatter_over_ici([parts[k] for k in keys]))
            chip_up(keys, arrived)
        else:
            keys = [(1, l), (2, l), (3, l)]
            (dq, dk, dv), swapped = _attn_bwd(s["qkv"], s["yc"], dymix, _swap_over_d2d([grads[k] for k in keys]))
            pair_up(keys, swapped)
        dproj, dx0, d_n1 = _inproj_bwd([da_in, du_pre, dv_pre, dq, dk, dv], wi[l], s["x0"], s["n1"], dx1)
        g_in_l = _tn_matmul(s["h"], dproj, "grad_w_in")[0]
        grads[(0, l)] = g_in_l.reshape(D_MODEL, N_CHIPS, IN_COLS // N_CHIPS).transpose(1, 0, 2)
        d_pw = jnp.stack([d_wbd[gi * POOL_GW:(gi + 1) * POOL_GW, gi * POOL_GW:(gi + 1) * POOL_GW]
                          for gi in range(len(POOL_WINDOWS))])
        small[l] = dict(norm1=d_n1[0], pool_w=d_pw, pool_scale=d_psc[0], sg_norm=d_sgn[0],
                        sg_w=d_wm.reshape(SG_HEADS, CHUNK, CHUNK), sg_b=d_bias[:, :SG_HEADS].T, norm2=d_n2[0])
        dcur = dx0
    grad_x = dcur.reshape(x.shape)

    keys = [(0, 0)]
    pair_up(keys, _swap_halves([grads[k] for k in keys]))
    chip_up(keys, _scatter_chips([parts[k] for k in keys]))
    keys = sorted(reduced)
    theirs = dict(zip(keys, _swap_reduced([reduced[k] for k in keys])))

    def joined(a):
        layers = []
        for l in range(depth):
            mine, other = reduced[(a, l)], theirs[(a, l)]
            layers.append(jnp.where(c_idx == 0, jnp.concatenate([mine, other]), jnp.concatenate([other, mine])))
        return jnp.stack(layers)

    gw_in, gw_out, gw_up, gw_down = [joined(a) for a in range(4)]

    names = ["norm1", "pool_w", "pool_scale", "sg_norm", "sg_w", "sg_b", "norm2"]
    slot = jnp.zeros((1,), F32)
    small_w = [norm1, pool_w, pool_scale, sg_norm, sg_w, sg_b, norm2, final_norm, slot]
    small_m = [m_norm1, m_pool_w, m_pool_scale, m_sg_norm, m_sg_w, m_sg_b, m_norm2, m_final_norm, slot]
    small_v = [v_norm1, v_pool_w, v_pool_scale, v_sg_norm, v_sg_w, v_sg_b, v_norm2, v_final_norm, slot]
    small_g = [jnp.stack([small[l][k] for l in range(depth)]) for k in names] + [d_final[0], loss_row[0, :1]]
    g_packed = _allreduce_small(_pack(small_g))
    loss = _unpack(g_packed, small_w)[-1][0]
    s_delta, s_m, s_v = _elementwise(_adamw, "adamw_small", [_pack(small_w), g_packed, _pack(small_m), _pack(small_v)], 3)
    gs = dict(zip(names + ["final_norm"], _unpack(g_packed, small_w)))
    ds = dict(zip(names + ["final_norm"], _unpack(s_delta, small_w)))
    ms = dict(zip(names + ["final_norm"], _unpack(s_m, small_w)))
    vs = dict(zip(names + ["final_norm"], _unpack(s_v, small_w)))

    big_g = dict(w_in=gw_in, w_out=gw_out, w_up=gw_up, w_down=gw_down)
    big_w = dict(w_in=(w_in, m_w_in, v_w_in), w_out=(w_out, m_w_out, v_w_out),
                 w_up=(w_up, m_w_up, v_w_up), w_down=(w_down, m_w_down, v_w_down))
    for k, (w, m, v) in big_w.items():
        ds[k], ms[k], vs[k] = _elementwise(_adamw, "adamw_" + k, [w, big_g[k], m, v], 3)
        gs[k] = big_g[k]

    order = ["norm1", "w_in", "pool_w", "pool_scale", "sg_norm", "sg_w", "sg_b", "w_out", "norm2", "w_up", "w_down",
             "final_norm"]
    return (loss, grad_x, *[gs[k] for k in order], *[ds[k] for k in order], *[ms[k] for k in order],
            *[vs[k] for k in order])
```

```python
import functools

import jax
import jax.numpy as jnp
from jax import lax
from jax.experimental import pallas as pl
from jax.experimental.pallas import tpu as pltpu

F32 = jnp.float32
BF16 = jnp.bfloat16
MESH = pl.DeviceIdType.MESH
AXES = ("x", "y", "c")

EPS = 1e-6
D_MODEL = 1024
POOL_WIDTH = 256
SG_WIDTH = 256
SB_WIDTH = 512
POOL_WINDOWS = (2, 4, 8, 16)
POOL_GW = 64
POOL_HALO = 16
CHUNK = 128
SG_HEADS = 4
SB_HD = 64
SB_SCALE = 0.125
IN_COLS = 2304
QKV_OFF = 768
D_FF = 4096
N_CHIPS = 4
LANES = 128
VMEM_LIMIT = 56 * 1024 * 1024
ATTN_TILE = 256
UNDERFLOW = -104.0

ADAM_LR = 0.001
ADAM_B1 = 0.9
ADAM_B2 = 0.999
ADAM_EPS = 1e-08
ADAM_WD = 0.01
ADAM_STEP = 10

HBM_SPEC = pl.BlockSpec(memory_space=pl.ANY)
VMEM_SPEC = pl.BlockSpec(memory_space=pltpu.VMEM)


def _params(**kw):
    return pltpu.CompilerParams(vmem_limit_bytes=VMEM_LIMIT, **kw)


def _tile(n, pref):
    if n <= pref:
        return n
    for t in range(pref - pref % LANES, 0, -LANES):
        if n % t == 0:
            return t
    raise ValueError((n, pref))


def _nn(a, b):
    return jnp.dot(a, b, preferred_element_type=F32)


def _nt(a, b):
    return lax.dot_general(a, b, (((1,), (1,)), ((), ())), preferred_element_type=F32)


def _tn(a, b):
    return lax.dot_general(a, b, (((0,), (0,)), ((), ())), preferred_element_type=F32)


def _rms_fwd(x, g):
    r = lax.rsqrt(jnp.mean(x * x, axis=-1, keepdims=True) + EPS)
    xhat = x * r
    return xhat * g, xhat, r


def _rms_bwd(dy, xhat, r, g):
    dxhat = dy * g
    dx = r * (dxhat - xhat * jnp.mean(dxhat * xhat, axis=-1, keepdims=True))
    return dx, dy * xhat


_GELU_K = 0.7978845608028654
_GELU_C = 0.044715


def _gelu(x):
    return 0.5 * x * (1.0 + jnp.tanh(_GELU_K * (x + _GELU_C * x * x * x)))


def _gelu_grad(x):
    t = jnp.tanh(_GELU_K * (x + _GELU_C * x * x * x))
    return 0.5 * (1.0 + t) + 0.5 * x * (1.0 - t * t) * _GELU_K * (1.0 + 3.0 * _GELU_C * x * x)


def _inproj_fwd(x, g, w):
    T, D = x.shape
    N = w.shape[1]
    tt = _tile(T, 512)

    def body(x_ref, g_ref, w_ref, proj_ref, h_ref, qkv_ref):
        h, _, _ = _rms_fwd(x_ref[...], g_ref[...])
        hb = h.astype(BF16)
        h_ref[...] = hb
        p = _nn(hb, w_ref[...])
        proj_ref[...] = p[:, :QKV_OFF]
        qkv_ref[...] = p[:, QKV_OFF:].astype(BF16)

    return pl.pallas_call(
        body, name="inproj_fwd", grid=(T // tt,),
        in_specs=[pl.BlockSpec((tt, D), lambda i: (i, 0)), pl.BlockSpec((1, D), lambda i: (0, 0)),
                  pl.BlockSpec((D, N), lambda i: (0, 0))],
        out_specs=[pl.BlockSpec((tt, QKV_OFF), lambda i: (i, 0)), pl.BlockSpec((tt, D), lambda i: (i, 0)),
                   pl.BlockSpec((tt, N - QKV_OFF), lambda i: (i, 0))],
        out_shape=[jax.ShapeDtypeStruct((T, QKV_OFF), F32), jax.ShapeDtypeStruct((T, D), BF16),
                   jax.ShapeDtypeStruct((T, N - QKV_OFF), BF16)],
        compiler_params=_params(),
    )(x, g, w)


def _inproj_bwd(pieces, w, x, g, dres):
    T, D = x.shape
    N = w.shape[1]
    tt = _tile(T, 512)
    widths = [p.shape[1] for p in pieces]
    offs = [sum(widths[:k]) for k in range(len(widths))]
    assert sum(widths) == N
    n_p = len(pieces)

    def body(*refs):
        p_refs = refs[:n_p]
        w_ref, x_ref, g_ref, dres_ref, dproj_ref, dx_ref, dg_ref = refs[n_p:]
        for p_ref, o, wd in zip(p_refs, offs, widths):
            dproj_ref[:, o:o + wd] = p_ref[...].astype(BF16)
        dh = _nt(dproj_ref[...], w_ref[...])
        gv = g_ref[...]
        _, xhat, r = _rms_fwd(x_ref[...], gv)
        dx, dgrow = _rms_bwd(dh, xhat, r, gv)
        dx_ref[...] = dres_ref[...] + dx

        @pl.when(pl.program_id(0) == 0)
        def _():
            dg_ref[...] = jnp.zeros_like(dg_ref)

        dg_ref[...] += jnp.sum(dgrow, axis=0, keepdims=True)

    return pl.pallas_call(
        body, name="inproj_bwd", grid=(T // tt,),
        in_specs=[pl.BlockSpec((tt, wd), lambda i: (i, 0)) for wd in widths] + [
            pl.BlockSpec((D, N), lambda i: (0, 0)), pl.BlockSpec((tt, D), lambda i: (i, 0)),
            pl.BlockSpec((1, D), lambda i: (0, 0)), pl.BlockSpec((tt, D), lambda i: (i, 0))],
        out_specs=[pl.BlockSpec((tt, N), lambda i: (i, 0)), pl.BlockSpec((tt, D), lambda i: (i, 0)),
                   pl.BlockSpec((1, D), lambda i: (0, 0))],
        out_shape=[jax.ShapeDtypeStruct((T, N), BF16), jax.ShapeDtypeStruct((T, D), F32),
                   jax.ShapeDtypeStruct((1, D), F32)],
        compiler_params=_params(),
    )(*pieces, w, x, g, dres)


def _pool_select(s2, s4, s8, s16, grp):
    return jnp.where(grp == 0, s2, jnp.where(grp == 1, s4, jnp.where(grp == 2, s8, s16)))


def _pool_count(t_glob, grp):
    win = jnp.where(grp == 0, 2, jnp.where(grp == 1, 4, jnp.where(grp == 2, 8, 16)))
    return jnp.minimum(t_glob + 1, win).astype(F32)


def _pool_diff(a, halo, base, tt):
    n = tt + POOL_HALO
    ext = jnp.concatenate([halo, a], axis=0)
    s2 = ext + pltpu.roll(ext, 1, 0)
    s4 = s2 + pltpu.roll(s2, 2, 0)
    s8 = s4 + pltpu.roll(s4, 4, 0)
    s16 = s8 + pltpu.roll(s8, 8, 0)
    grp = lax.broadcasted_iota(jnp.int32, (n, POOL_WIDTH), 1) // POOL_GW
    t_glob = lax.broadcasted_iota(jnp.int32, (n, POOL_WIDTH), 0) + (base - POOL_HALO)
    pooled = _pool_select(s2, s4, s8, s16, grp) / _pool_count(t_glob, grp)
    return pooled[POOL_HALO:] - a


def _pool_specs(T, tt):
    hb = tt // POOL_HALO
    return [pl.BlockSpec((tt, POOL_WIDTH), lambda i: (i, 0)),
            pl.BlockSpec((POOL_HALO, POOL_WIDTH), lambda i: (jnp.maximum(i * hb - 1, 0), 0))]


def _pool_fwd(proj, wbd, scale):
    T = proj.shape[0]
    tt = _tile(T, 512)

    def body(a_ref, halo_ref, w_ref, sc_ref, y_ref):
        i = pl.program_id(0)
        halo = jnp.where(i > 0, halo_ref[...], 0.0)
        d = _pool_diff(a_ref[...], halo, i * tt, tt)
        y_ref[...] = _nn(d.astype(BF16), w_ref[...]) * sc_ref[...]

    return pl.pallas_call(
        body, name="pool_fwd", grid=(T // tt,),
        in_specs=_pool_specs(T, tt) + [pl.BlockSpec((POOL_WIDTH, POOL_WIDTH), lambda i: (0, 0)),
                                       pl.BlockSpec((1, POOL_WIDTH), lambda i: (0, 0))],
        out_specs=pl.BlockSpec((tt, POOL_WIDTH), lambda i: (i, 0)),
        out_shape=jax.ShapeDtypeStruct((T, POOL_WIDTH), F32),
        compiler_params=_params(),
    )(proj, proj, wbd, scale)


def _pool_bwd(proj, dymix, wbd, scale):
    T = proj.shape[0]
    tt = _tile(T, 512)
    hb = tt // POOL_HALO
    nblk = T // tt
    n = tt + POOL_HALO

    def body(a_ref, halo_ref, dy_ref, dyn_ref, w_ref, sc_ref, da_ref, dw_ref, dsc_ref):
        i = pl.program_id(0)
        halo = jnp.where(i > 0, halo_ref[...], 0.0)
        d = _pool_diff(a_ref[...], halo, i * tt, tt)
        db = d.astype(BF16)
        wv = w_ref[...]
        sc = sc_ref[...]
        dy = dy_ref[...]
        dys = dy * sc

        @pl.when(i == 0)
        def _():
            dw_ref[...] = jnp.zeros_like(dw_ref)
            dsc_ref[...] = jnp.zeros_like(dsc_ref)

        dsc_ref[...] += jnp.sum(dy * _nn(db, wv), axis=0, keepdims=True)
        dw_ref[...] += _tn(db, dys.astype(BF16))
        dyn = jnp.where(i < nblk - 1, dyn_ref[...], 0.0) * sc
        dd = _nt(jnp.concatenate([dys, dyn], axis=0).astype(BF16), wv)
        grp = lax.broadcasted_iota(jnp.int32, (n, POOL_WIDTH), 1) // POOL_GW
        t_glob = lax.broadcasted_iota(jnp.int32, (n, POOL_WIDTH), 0) + i * tt
        e = dd / _pool_count(t_glob, grp)
        r2 = e + pltpu.roll(e, n - 1, 0)
        r4 = r2 + pltpu.roll(r2, n - 2, 0)
        r8 = r4 + pltpu.roll(r4, n - 4, 0)
        r16 = r8 + pltpu.roll(r8, n - 8, 0)
        da_ref[...] = (_pool_select(r2, r4, r8, r16, grp) - dd)[:tt]

    return pl.pallas_call(
        body, name="pool_bwd", grid=(nblk,),
        in_specs=_pool_specs(T, tt) + [
            pl.BlockSpec((tt, POOL_WIDTH), lambda i: (i, 0)),
            pl.BlockSpec((POOL_HALO, POOL_WIDTH), lambda i: (jnp.minimum((i + 1) * hb, T // POOL_HALO - 1), 0)),
            pl.BlockSpec((POOL_WIDTH, POOL_WIDTH), lambda i: (0, 0)), pl.BlockSpec((1, POOL_WIDTH), lambda i: (0, 0))],
        out_specs=[pl.BlockSpec((tt, POOL_WIDTH), lambda i: (i, 0)),
                   pl.BlockSpec((POOL_WIDTH, POOL_WIDTH), lambda i: (0, 0)),
                   pl.BlockSpec((1, POOL_WIDTH), lambda i: (0, 0))],
        out_shape=[jax.ShapeDtypeStruct((T, POOL_WIDTH), F32),
                   jax.ShapeDtypeStruct((POOL_WIDTH, POOL_WIDTH), F32),
                   jax.ShapeDtypeStruct((1, POOL_WIDTH), F32)],
        compiler_params=_params(),
    )(proj, proj, dymix, dymix, wbd, scale)


def _head_select(stacked, grp):
    out = jnp.where(grp == 0, stacked[0:CHUNK], 0.0)
    for h in range(1, SG_HEADS):
        out = out + jnp.where(grp == h, stacked[h * CHUNK:(h + 1) * CHUNK], 0.0)
    return out


def _sg_specs(tt):
    return [pl.BlockSpec((tt, SG_WIDTH), lambda i: (i, 1)), pl.BlockSpec((tt, SG_WIDTH), lambda i: (i, 2))]


def _sg_fwd(proj, wm, bias, g):
    T = proj.shape[0]
    tt = _tile(T, 512)

    def body(u_ref, v_ref, wm_ref, b_ref, g_ref, y_ref):
        zu = _gelu(u_ref[...])
        vn, _, _ = _rms_fwd(_gelu(v_ref[...]), g_ref[...])
        grp = lax.broadcasted_iota(jnp.int32, (CHUNK, SG_WIDTH), 1) // SB_HD
        for n in range(tt // CHUNK):
            rows = slice(n * CHUNK, (n + 1) * CHUNK)
            sv = _head_select(_nn(wm_ref[...], vn[rows].astype(BF16)), grp) + b_ref[...]
            y_ref[rows, :] = zu[rows] * sv

    return pl.pallas_call(
        body, name="sg_fwd", grid=(T // tt,),
        in_specs=_sg_specs(tt) + [pl.BlockSpec((SG_HEADS * CHUNK, CHUNK), lambda i: (0, 0)),
                                  pl.BlockSpec((CHUNK, SG_WIDTH), lambda i: (0, 0)),
                                  pl.BlockSpec((1, SG_WIDTH), lambda i: (0, 0))],
        out_specs=pl.BlockSpec((tt, SG_WIDTH), lambda i: (i, 0)),
        out_shape=jax.ShapeDtypeStruct((T, SG_WIDTH), F32),
        compiler_params=_params(),
    )(proj, proj, wm, bias, g)


def _sg_bwd(proj, dymix, wm, wmt, bias, g, hosted=None):
    T = proj.shape[0]
    tt = _tile(T, 512)
    nblk = T // tt

    def body(*refs):
        i = pl.program_id(0)
        (u_ref, v_ref, dy_ref, wm_ref, wmt_ref, b_ref, g_ref, du_ref, dv_ref, dw_ref, db_ref, dg_ref,
         dvn_ref, dbias_ref), start, wait = _host(hosted, refs, 7, 5, i == 0, i == nblk - 1)
        start()
        up, vp = u_ref[...], v_ref[...]
        gv = g_ref[...]
        zu, zv = _gelu(up), _gelu(vp)
        vn, xhat, r = _rms_fwd(zv, gv)
        gu = _gelu_grad(up)
        grp = lax.broadcasted_iota(jnp.int32, (CHUNK, SG_WIDTH), 1) // SB_HD

        @pl.when(i == 0)
        def _():
            dw_ref[...] = jnp.zeros_like(dw_ref)
            dbias_ref[...] = jnp.zeros_like(dbias_ref)
            dg_ref[...] = jnp.zeros_like(dg_ref)

        for n in range(tt // CHUNK):
            rows = slice(n * CHUNK, (n + 1) * CHUNK)
            vc = vn[rows].astype(BF16)
            sv = _head_select(_nn(wm_ref[...], vc), grp) + b_ref[...]
            dy = dy_ref[rows, :]
            du_ref[rows, :] = dy * sv * gu[rows]
            dsv = dy * zu[rows]
            dsvb = dsv.astype(BF16)
            dvn_ref[rows, :] = _head_select(_nn(wmt_ref[...], dsvb), grp)
            stacked = jnp.concatenate([jnp.where(grp == h, dsv, 0.0) for h in range(SG_HEADS)], axis=0)
            dw_ref[...] += _nt(stacked.astype(BF16), vc)
            dbias_ref[...] += dsv

        dzv, dgrow = _rms_bwd(dvn_ref[...], xhat, r, gv)
        dg_ref[...] += jnp.sum(dgrow, axis=0, keepdims=True)
        dv_ref[...] = dzv * _gelu_grad(vp)

        @pl.when(i == nblk - 1)
        def _():
            t_i = lax.broadcasted_iota(jnp.int32, (SG_HEADS * CHUNK, CHUNK), 0) % CHUNK
            s_i = lax.broadcasted_iota(jnp.int32, (SG_HEADS * CHUNK, CHUNK), 1)
            dw_ref[...] = jnp.where(s_i <= t_i, dw_ref[...], 0.0)
            lane = lax.broadcasted_iota(jnp.int32, (CHUNK, LANES), 1)
            acc = jnp.zeros((CHUNK, LANES), F32)
            for h in range(SG_HEADS):
                tot = jnp.sum(jnp.where(grp == h, dbias_ref[...], 0.0), axis=1, keepdims=True)
                acc = acc + jnp.where(lane == h, tot, 0.0)
            db_ref[...] = acc

        wait()

    h_ins = hosted.ins if hosted else []
    res = pl.pallas_call(
        body, name="sg_bwd_hosting" if hosted else "sg_bwd", grid=(nblk,),
        in_specs=_sg_specs(tt) + [pl.BlockSpec((tt, SG_WIDTH), lambda i: (i, 1)),
                                  pl.BlockSpec((SG_HEADS * CHUNK, CHUNK), lambda i: (0, 0)),
                                  pl.BlockSpec((SG_HEADS * CHUNK, CHUNK), lambda i: (0, 0)),
                                  pl.BlockSpec((CHUNK, SG_WIDTH), lambda i: (0, 0)),
                                  pl.BlockSpec((1, SG_WIDTH), lambda i: (0, 0))] + [HBM_SPEC] * len(h_ins),
        out_specs=[pl.BlockSpec((tt, SG_WIDTH), lambda i: (i, 0)), pl.BlockSpec((tt, SG_WIDTH), lambda i: (i, 0)),
                   pl.BlockSpec((SG_HEADS * CHUNK, CHUNK), lambda i: (0, 0)),
                   pl.BlockSpec((CHUNK, LANES), lambda i: (0, 0)), pl.BlockSpec((1, SG_WIDTH), lambda i: (0, 0))]
        + [HBM_SPEC] * len(h_ins),
        out_shape=[jax.ShapeDtypeStruct((T, SG_WIDTH), F32), jax.ShapeDtypeStruct((T, SG_WIDTH), F32),
                   jax.ShapeDtypeStruct((SG_HEADS * CHUNK, CHUNK), F32),
                   jax.ShapeDtypeStruct((CHUNK, LANES), F32), jax.ShapeDtypeStruct((1, SG_WIDTH), F32)]
        + (hosted.out_shapes if hosted else []),
        scratch_shapes=[pltpu.VMEM((tt, SG_WIDTH), F32), pltpu.VMEM((CHUNK, SG_WIDTH), F32)]
        + (hosted.sems() if hosted else []),
        compiler_params=_params(has_side_effects=hosted is not None),
    )(proj, proj, dymix, wm, wmt, bias, g, *h_ins)
    return res[:5], res[5:]


def _split_dot(x, u):
    hi = x.astype(BF16)
    lo = (x - hi.astype(F32)).astype(BF16)
    return _nn(hi, u) + _nn(lo, u)


def _sb_logits(qs, kb):
    z = _nt(qs, kb)
    lb = jnp.minimum(z, 0.0) - jnp.log(1.0 + jnp.exp(-jnp.abs(z)))
    return lb, lb - z


def _attn_qkv_specs(tq, T):
    base = (IN_COLS - 3 * SB_WIDTH - QKV_OFF) // LANES
    nb = SB_WIDTH // LANES
    return [pl.BlockSpec((tq, LANES), lambda p, i: (i, base + p)),
            pl.BlockSpec((T, LANES), lambda p, i: (0, base + nb + p)),
            pl.BlockSpec((T, LANES), lambda p, i: (0, base + 2 * nb + p))]


class _Hosted:
    def __init__(self, ins, out_shapes, n_sems, copies):
        self.ins, self.out_shapes, self.n_sems, self.copies = ins, out_shapes, n_sems, copies

    @property
    def n(self):
        return len(self.ins)

    def sems(self):
        return [pltpu.SemaphoreType.DMA((self.n_sems,)), pltpu.SemaphoreType.DMA((self.n_sems,))]

    def start(self, src, dst, ssem, rsem):
        for send, _ in self.copies(src, dst, ssem, rsem):
            send.start()

    def wait(self, src, dst, ssem, rsem):
        for send, recv in self.copies(src, dst, ssem, rsem):
            recv.wait_recv()
            send.wait_send()


def _host(hosted, refs, n_in, n_out, first, last):
    if hosted is None:
        return refs, lambda: None, lambda: None
    n = hosted.n
    own_in, h_in = refs[:n_in], refs[n_in:n_in + n]
    own_out, h_out = refs[n_in + n:n_in + n + n_out], refs[n_in + n + n_out:n_in + 2 * n + n_out]
    rest = refs[n_in + 2 * n + n_out:]
    ssem, rsem = rest[-2:]

    def start():
        @pl.when(first)
        def _():
            hosted.start(h_in, h_out, ssem, rsem)

    def wait():
        @pl.when(last)
        def _():
            hosted.wait(h_in, h_out, ssem, rsem)

    return own_in + own_out + rest[:-2], start, wait


def _attn_fwd(qkv, hosted=None):
    T = qkv.shape[0]
    tq = _tile(T, ATTN_TILE)
    n_p, nq = SB_WIDTH // LANES, T // tq

    def body(*refs):
        p, i = pl.program_id(0), pl.program_id(1)
        (q_ref, k_ref, v_ref, o_ref), start, wait = _host(
            hosted, refs, 3, 1, jnp.logical_and(p == 0, i == 0), jnp.logical_and(p == n_p - 1, i == nq - 1))
        start()
        lane = lax.broadcasted_iota(jnp.int32, (tq, LANES), 1)
        row = lax.broadcasted_iota(jnp.int32, (tq, tq), 0)
        col = lax.broadcasted_iota(jnp.int32, (tq, tq), 1)
        after = jnp.where(row > col, 1.0, 0.0).astype(BF16)
        valid = col < row
        q = q_ref[...].astype(F32)
        qh = [jnp.where((lane // SB_HD) == hh, q * SB_SCALE, 0.0).astype(BF16) for hh in range(2)]

        def tile(j, state, mask):
            ks = pl.ds(pl.multiple_of(j * tq, tq), tq)
            kb, vb = k_ref[ks, :], v_ref[ks, :]
            new = []
            for hh in range(2):
                carry, acc = state[hh]
                lb, lm = _sb_logits(qh[hh], kb)
                if mask is not None:
                    lm = jnp.where(mask, lm, 0.0)
                a = jnp.exp(lb + _nn(lm.astype(BF16), after) + carry)
                if mask is not None:
                    a = jnp.where(mask, a, 0.0)
                new.append((carry + jnp.sum(lm, axis=1, keepdims=True), acc + _nn(a.astype(BF16), vb)))
            return tuple(new)

        def live(state):
            return jnp.maximum(jnp.max(state[0][0]), jnp.max(state[1][0]))

        zero = (jnp.zeros((tq, 1), F32), jnp.zeros((tq, LANES), F32))
        state = tile(i, (zero, zero), valid)
        state = tile(jnp.maximum(i - 1, 0), state, jnp.broadcast_to(i > 0, (tq, tq)))

        def cond(st):
            return jnp.logical_and(st[0] >= 0, st[2] > UNDERFLOW)

        def step(st):
            state = tile(st[0], st[1], None)
            return st[0] - 1, state, live(state)

        _, state, _ = lax.while_loop(cond, step, (i - 2, state, live(state)))
        o_ref[...] = jnp.where(lane < SB_HD, state[0][1], state[1][1])
        wait()

    h_ins = hosted.ins if hosted else []
    res = pl.pallas_call(
        body, name="attn_fwd_hosting" if hosted else "attn_fwd", grid=(n_p, nq),
        in_specs=_attn_qkv_specs(tq, T) + [HBM_SPEC] * len(h_ins),
        out_specs=[pl.BlockSpec((tq, LANES), lambda p, i: (i, p))] + [HBM_SPEC] * len(h_ins),
        out_shape=[jax.ShapeDtypeStruct((T, SB_WIDTH), F32)] + (hosted.out_shapes if hosted else []),
        scratch_shapes=hosted.sems() if hosted else [],
        compiler_params=_params(has_side_effects=hosted is not None),
    )(qkv, qkv, qkv, *h_ins)
    return res[0], res[1:]


def _attn_bwd(qkv, o, dymix, hosted=None):
    T = qkv.shape[0]
    tq = _tile(T, ATTN_TILE)
    n_p, nq = SB_WIDTH // LANES, T // tq
    yc_blk = (POOL_WIDTH + SG_WIDTH) // LANES

    def body(*refs):
        p, i = pl.program_id(0), pl.program_id(1)
        (q_ref, k_ref, v_ref, o_ref, do_ref, dq_ref, dk_ref, dv_ref), start, wait = _host(
            hosted, refs, 5, 3, jnp.logical_and(p == 0, i == 0), jnp.logical_and(p == n_p - 1, i == nq - 1))
        start()
        lane = lax.broadcasted_iota(jnp.int32, (tq, LANES), 1)
        row = lax.broadcasted_iota(jnp.int32, (tq, tq), 0)
        col = lax.broadcasted_iota(jnp.int32, (tq, tq), 1)
        after = jnp.where(row > col, 1.0, 0.0).astype(BF16)
        from_here = jnp.where(row >= col, 1.0, 0.0).astype(BF16)
        valid = col < row

        @pl.when(i == 0)
        def _():
            dk_ref[...] = jnp.zeros_like(dk_ref)
            dv_ref[...] = jnp.zeros_like(dv_ref)

        q = q_ref[...].astype(F32)
        ov = o_ref[...]
        dov = do_ref[...]
        heads = [(lane // SB_HD) == hh for hh in range(2)]
        qh = [jnp.where(h, q * SB_SCALE, 0.0).astype(BF16) for h in heads]
        dohb = [jnp.where(h, dov, 0.0).astype(BF16) for h in heads]
        delta = [jnp.sum(d.astype(F32) * ov, axis=1, keepdims=True) for d in dohb]

        def tile(j, state, mask):
            ks = pl.ds(pl.multiple_of(j * tq, tq), tq)
            kb, vb = k_ref[ks, :], v_ref[ks, :]
            new, dk, dv = [], None, None
            for hh in range(2):
                c_a, c_r, dqa = state[hh]
                lb, lm = _sb_logits(qh[hh], kb)
                if mask is not None:
                    lm = jnp.where(mask, lm, 0.0)
                a = jnp.exp(lb + _nn(lm.astype(BF16), after) + c_a)
                if mask is not None:
                    a = jnp.where(mask, a, 0.0)
                ab = a.astype(BF16)
                sig = jnp.exp(lb)
                g = _nt(dohb[hh], vb) * ab.astype(F32)
                left = delta[hh] - (c_r + _split_dot(g, from_here))
                dz = g * (1.0 - sig) - left * sig
                if mask is not None:
                    dz = jnp.where(mask, dz, 0.0)
                dzb = dz.astype(BF16)
                dk_h, dv_h = _tn(dzb, qh[hh]), _tn(ab, dohb[hh])
                dk, dv = (dk_h, dv_h) if hh == 0 else (dk + dk_h, dv + dv_h)
                new.append((c_a + jnp.sum(lm, axis=1, keepdims=True), c_r + jnp.sum(g, axis=1, keepdims=True),
                            dqa + _nn(dzb, kb)))
            dk_ref[ks, :] += dk
            dv_ref[ks, :] += dv
            return tuple(new)

        def live(state):
            return jnp.maximum(jnp.max(state[0][0]), jnp.max(state[1][0]))

        zero = (jnp.zeros((tq, 1), F32), jnp.zeros((tq, 1), F32), jnp.zeros((tq, LANES), F32))
        state = tile(i, (zero, zero), valid)
        state = tile(jnp.maximum(i - 1, 0), state, jnp.broadcast_to(i > 0, (tq, tq)))

        def cond(st):
            return jnp.logical_and(st[0] >= 0, st[2] > UNDERFLOW)

        def step(st):
            state = tile(st[0], st[1], None)
            return st[0] - 1, state, live(state)

        _, state, _ = lax.while_loop(cond, step, (i - 2, state, live(state)))
        dq_ref[...] = jnp.where(lane < SB_HD, state[0][2], state[1][2]) * SB_SCALE
        wait()

    h_ins = hosted.ins if hosted else []
    res = pl.pallas_call(
        body, name="attn_bwd_hosting" if hosted else "attn_bwd", grid=(n_p, nq),
        in_specs=_attn_qkv_specs(tq, T) + [pl.BlockSpec((tq, LANES), lambda p, i: (i, p)),
                                           pl.BlockSpec((tq, LANES), lambda p, i: (i, yc_blk + p))]
        + [HBM_SPEC] * len(h_ins),
        out_specs=[pl.BlockSpec((tq, LANES), lambda p, i: (i, p)), pl.BlockSpec((T, LANES), lambda p, i: (0, p)),
                   pl.BlockSpec((T, LANES), lambda p, i: (0, p))] + [HBM_SPEC] * len(h_ins),
        out_shape=[jax.ShapeDtypeStruct((T, SB_WIDTH), F32)] * 3 + (hosted.out_shapes if hosted else []),
        scratch_shapes=hosted.sems() if hosted else [],
        compiler_params=_params(has_side_effects=hosted is not None),
    )(qkv, qkv, qkv, o, dymix, *h_ins)
    return res[:3], res[3:]


def _outproj_fwd(x, ya, yb, yc, w):
    T, D = x.shape
    tt = _tile(T, 512)

    def body(x_ref, ya_ref, yb_ref, yc_ref, w_ref, x1_ref, ymix_ref):
        ymix_ref[:, 0:POOL_WIDTH] = ya_ref[...].astype(BF16)
        ymix_ref[:, POOL_WIDTH:POOL_WIDTH + SG_WIDTH] = yb_ref[...].astype(BF16)
        ymix_ref[:, POOL_WIDTH + SG_WIDTH:] = yc_ref[...].astype(BF16)
        x1_ref[...] = x_ref[...] + _nn(ymix_ref[...], w_ref[...])

    row = lambda width: pl.BlockSpec((tt, width), lambda i: (i, 0))
    return pl.pallas_call(
        body, name="outproj_fwd", grid=(T // tt,),
        in_specs=[row(D), row(POOL_WIDTH), row(SG_WIDTH), row(SB_WIDTH), pl.BlockSpec((D, D), lambda i: (0, 0))],
        out_specs=[row(D), row(D)],
        out_shape=[jax.ShapeDtypeStruct((T, D), F32), jax.ShapeDtypeStruct((T, D), BF16)],
        compiler_params=_params(),
    )(x, ya, yb, yc, w)


def _nt_matmul(a, w):
    T, N = a.shape
    K = w.shape[0]
    tt = _tile(T, 512)

    def body(a_ref, w_ref, o_ref):
        o_ref[...] = _nt(a_ref[...].astype(BF16), w_ref[...])

    return pl.pallas_call(
        body, name="nt_matmul", grid=(T // tt,),
        in_specs=[pl.BlockSpec((tt, N), lambda i: (i, 0)), pl.BlockSpec((K, N), lambda i: (0, 0))],
        out_specs=pl.BlockSpec((tt, K), lambda i: (i, 0)),
        out_shape=jax.ShapeDtypeStruct((T, K), F32),
        compiler_params=_params(),
    )(a, w)


def _tn_matmul(a, b, name, n_split=1):
    T, K = a.shape
    N = b.shape[1]
    tk = _tile(K, 1024)
    tn = _tile(N // n_split, 1024)
    tt = _tile(T, 2048)
    nper = N // n_split // tn
    nt = T // tt

    def body(a_ref, b_ref, o_ref):
        @pl.when(pl.program_id(2) == 0)
        def _():
            o_ref[...] = jnp.zeros_like(o_ref)

        o_ref[...] += _tn(a_ref[...], b_ref[...].astype(BF16))

    return pl.pallas_call(
        body, name=name, grid=(K // tk, N // tn, nt),
        in_specs=[pl.BlockSpec((tt, tk), lambda k, n, t: (t, k)), pl.BlockSpec((tt, tn), lambda k, n, t: (t, n))],
        out_specs=pl.BlockSpec((None, tk, tn), lambda k, n, t: (n // nper, k, n % nper)),
        out_shape=jax.ShapeDtypeStruct((n_split, K, N // n_split), F32),
        compiler_params=_params(),
    )(a, b)


def _mlp_fwd(x, g, w_up, w_down, hosted=None):
    T, D = x.shape
    F = w_up.shape[1]
    tt = _tile(T, 1024)
    fc = _tile(F, 512)
    nc = F // fc
    nt = T // tt

    def body(*refs):
        i, c = pl.program_id(0), pl.program_id(1)
        (x_ref, g_ref, wu_ref, wd_ref, y_ref, h_ref, u_ref, a_ref), start, wait = _host(
            hosted, refs, 4, 4, jnp.logical_and(i == 0, c == 0), jnp.logical_and(i == nt - 1, c == nc - 1))
        start()

        @pl.when(c == 0)
        def _():
            xv = x_ref[...]
            h, _, _ = _rms_fwd(xv, g_ref[...])
            h_ref[...] = h.astype(BF16)
            y_ref[...] = xv

        u = _nn(h_ref[...], wu_ref[...])
        u_ref[...] = u.astype(BF16)
        a = jnp.square(jnp.maximum(u, 0.0)).astype(BF16)
        a_ref[...] = a
        y_ref[...] += _nn(a, wd_ref[...])
        wait()

    h_ins = hosted.ins if hosted else []
    res = pl.pallas_call(
        body, name="mlp_fwd_hosting" if hosted else "mlp_fwd", grid=(nt, nc),
        in_specs=[pl.BlockSpec((tt, D), lambda i, c: (i, 0)), pl.BlockSpec((1, D), lambda i, c: (0, 0)),
                  pl.BlockSpec((D, fc), lambda i, c: (0, c)), pl.BlockSpec((fc, D), lambda i, c: (c, 0))]
        + [HBM_SPEC] * len(h_ins),
        out_specs=[pl.BlockSpec((tt, D), lambda i, c: (i, 0)), pl.BlockSpec((tt, D), lambda i, c: (i, 0)),
                   pl.BlockSpec((tt, fc), lambda i, c: (i, c)), pl.BlockSpec((tt, fc), lambda i, c: (i, c))]
        + [HBM_SPEC] * len(h_ins),
        out_shape=[jax.ShapeDtypeStruct((T, D), F32), jax.ShapeDtypeStruct((T, D), BF16),
                   jax.ShapeDtypeStruct((T, F), BF16), jax.ShapeDtypeStruct((T, F), BF16)]
        + (hosted.out_shapes if hosted else []),
        scratch_shapes=hosted.sems() if hosted else [],
        compiler_params=_params(has_side_effects=hosted is not None),
    )(x, g, w_up, w_down, *h_ins)
    return res[:4], res[4:]


def _mlp_bwd(dy, x, g, u, w_up, w_down, hosted=None):
    T, D = x.shape
    F = w_up.shape[1]
    tt = _tile(T, 1024)
    fc = _tile(F, 512)
    nc = F // fc
    nt = T // tt

    def body(*refs):
        i, c = pl.program_id(0), pl.program_id(1)
        (dy_ref, x_ref, g_ref, u_ref, wu_ref, wd_ref, dx_ref, du_ref, dg_ref, dyb_ref, dh_ref), start, wait = _host(
            hosted, refs, 6, 3, jnp.logical_and(i == 0, c == 0), jnp.logical_and(i == nt - 1, c == nc - 1))
        start()

        @pl.when(c == 0)
        def _():
            dyb_ref[...] = dy_ref[...].astype(BF16)
            dh_ref[...] = jnp.zeros_like(dh_ref)

        @pl.when(jnp.logical_and(i == 0, c == 0))
        def _():
            dg_ref[...] = jnp.zeros_like(dg_ref)

        da = _nt(dyb_ref[...], wd_ref[...])
        du = (da * (2.0 * jnp.maximum(u_ref[...].astype(F32), 0.0))).astype(BF16)
        du_ref[...] = du
        dh_ref[...] += _nt(du, wu_ref[...])

        @pl.when(c == nc - 1)
        def _():
            gv = g_ref[...]
            _, xhat, r = _rms_fwd(x_ref[...], gv)
            dx, dgrow = _rms_bwd(dh_ref[...], xhat, r, gv)
            dx_ref[...] = dy_ref[...] + dx
            dg_ref[...] += jnp.sum(dgrow, axis=0, keepdims=True)

        wait()

    h_ins = hosted.ins if hosted else []
    res = pl.pallas_call(
        body, name="mlp_bwd_hosting" if hosted else "mlp_bwd", grid=(nt, nc),
        in_specs=[pl.BlockSpec((tt, D), lambda i, c: (i, 0)), pl.BlockSpec((tt, D), lambda i, c: (i, 0)),
                  pl.BlockSpec((1, D), lambda i, c: (0, 0)), pl.BlockSpec((tt, fc), lambda i, c: (i, c)),
                  pl.BlockSpec((D, fc), lambda i, c: (0, c)), pl.BlockSpec((fc, D), lambda i, c: (c, 0))]
        + [HBM_SPEC] * len(h_ins),
        out_specs=[pl.BlockSpec((tt, D), lambda i, c: (i, 0)), pl.BlockSpec((tt, fc), lambda i, c: (i, c)),
                   pl.BlockSpec((1, D), lambda i, c: (0, 0))] + [HBM_SPEC] * len(h_ins),
        out_shape=[jax.ShapeDtypeStruct((T, D), F32), jax.ShapeDtypeStruct((T, F), BF16),
                   jax.ShapeDtypeStruct((1, D), F32)] + (hosted.out_shapes if hosted else []),
        scratch_shapes=[pltpu.VMEM((tt, D), BF16), pltpu.VMEM((tt, D), F32)] + (hosted.sems() if hosted else []),
        compiler_params=_params(has_side_effects=hosted is not None),
    )(dy, x, g, u, w_up, w_down, *h_ins)
    return res[:3], res[3:]


def _loss_head(x, g, target):
    T, D = x.shape
    tt = _tile(T, 512)

    def body(x_ref, g_ref, t_ref, loss_ref, dx_ref, dg_ref):
        gv = g_ref[...]
        y, xhat, r = _rms_fwd(x_ref[...], gv)
        err = y - t_ref[...]
        dx, dgrow = _rms_bwd(err * (1.0 / D), xhat, r, gv)
        dx_ref[...] = dx

        @pl.when(pl.program_id(0) == 0)
        def _():
            loss_ref[...] = jnp.zeros_like(loss_ref)
            dg_ref[...] = jnp.zeros_like(dg_ref)

        loss_ref[...] += 0.5 * jnp.sum(jnp.mean(err * err, axis=-1, keepdims=True), axis=0, keepdims=True)
        dg_ref[...] += jnp.sum(dgrow, axis=0, keepdims=True)

    return pl.pallas_call(
        body, name="loss_head", grid=(T // tt,),
        in_specs=[pl.BlockSpec((tt, D), lambda i: (i, 0)), pl.BlockSpec((1, D), lambda i: (0, 0)),
                  pl.BlockSpec((tt, D), lambda i: (i, 0))],
        out_specs=[pl.BlockSpec((1, LANES), lambda i: (0, 0)), pl.BlockSpec((tt, D), lambda i: (i, 0)),
                   pl.BlockSpec((1, D), lambda i: (0, 0))],
        out_shape=[jax.ShapeDtypeStruct((1, LANES), F32), jax.ShapeDtypeStruct((T, D), F32),
                   jax.ShapeDtypeStruct((1, D), F32)],
        compiler_params=_params(),
    )(x, g, target)


def _rows(shape, pref=512):
    last = shape[-1]
    rows = 1
    for s in shape[:-1]:
        rows *= s
    tr = rows
    if rows * last > 256 * 1024:
        for cand in (pref, 256, 128, 64, 32, 16, 8):
            if rows % cand == 0:
                tr = cand
                break
    return rows, last, tr


def _elementwise(fn, name, ins, n_out, out_dtype=F32):
    shape = ins[0].shape
    rows, last, tr = _rows(shape)
    flat = [a.reshape(rows, last) for a in ins]
    n_in = len(ins)

    def body(*refs):
        res = fn(*[r[...] for r in refs[:n_in]])
        if n_out == 1:
            res = (res,)
        for r, v in zip(refs[n_in:], res):
            r[...] = v.astype(r.dtype)

    spec = pl.BlockSpec((tr, last), lambda i: (i, 0))
    outs = pl.pallas_call(
        body, name=name, grid=(rows // tr,),
        in_specs=[spec] * n_in, out_specs=[spec] * n_out,
        out_shape=[jax.ShapeDtypeStruct((rows, last), out_dtype)] * n_out,
        compiler_params=_params(),
    )(*flat)
    return [o.reshape(shape) for o in outs]


def _add_pair(g, o, c_idx):
    nq, R, C = g.shape
    h = R // 2
    tr = _tile(h, 512)
    nb = h // tr

    def body(c_ref, g_ref, o_ref, out_ref):
        out_ref[...] = g_ref[...] + o_ref[...]

    return pl.pallas_call(
        body, name="add_pair",
        grid_spec=pltpu.PrefetchScalarGridSpec(
            num_scalar_prefetch=1, grid=(nq, nb),
            in_specs=[pl.BlockSpec((None, tr, C), lambda q, i, c: (q, c[0] * nb + i, 0)),
                      pl.BlockSpec((None, tr, C), lambda q, i, c: (q, i, 0))],
            out_specs=pl.BlockSpec((None, tr, C), lambda q, i, c: (q, i, 0))),
        out_shape=jax.ShapeDtypeStruct((nq, h, C), F32),
        compiler_params=_params(),
    )(c_idx.astype(jnp.int32).reshape(1), g, o)


def _add_chips(p, r, q_idx):
    _, H, C = p.shape
    tr = _tile(H, 512)

    def body(q_ref, p_ref, r0_ref, r1_ref, r2_ref, out_ref):
        out_ref[...] = (p_ref[...] + r0_ref[...]) + (r1_ref[...] + r2_ref[...])

    def arrived(k):
        return pl.BlockSpec((None, tr, C), lambda i, q: (k, i, 0))

    return pl.pallas_call(
        body, name="add_chips",
        grid_spec=pltpu.PrefetchScalarGridSpec(
            num_scalar_prefetch=1, grid=(H // tr,),
            in_specs=[pl.BlockSpec((None, tr, C), lambda i, q: (q[0], i, 0)), arrived(0), arrived(1), arrived(2)],
            out_specs=pl.BlockSpec((tr, C), lambda i, q: (i, 0))),
        out_shape=jax.ShapeDtypeStruct((H, C), F32),
        compiler_params=_params(),
    )(q_idx.astype(jnp.int32).reshape(1), p, r, r, r)


def _adamw(w, g, m, v):
    m = ADAM_B1 * m + (1.0 - ADAM_B1) * g
    v = ADAM_B2 * v + (1.0 - ADAM_B2) * jnp.square(g)
    m_hat = m / (1.0 - ADAM_B1 ** ADAM_STEP)
    v_hat = v / (1.0 - ADAM_B2 ** ADAM_STEP)
    delta = -ADAM_LR * (m_hat / (jnp.sqrt(v_hat) + ADAM_EPS) + ADAM_WD * w)
    return delta, m, v


def _place():
    x, y, c = lax.axis_index("x"), lax.axis_index("y"), lax.axis_index("c")
    chips = [(1 - x, y), (x, 1 - y), (1 - x, 1 - y)]
    return x, y, c, chips


def _remote(src, dst, ssem, rsem, k, dev):
    return pltpu.make_async_remote_copy(src_ref=src, dst_ref=dst, send_sem=ssem.at[k], recv_sem=rsem.at[k],
                                        device_id=dev, device_id_type=MESH)


def _gather_weights(shards):
    n = len(shards)
    halves = [s.shape[1] // 2 for s in shards]

    def body(*refs):
        src, out = refs[:n], refs[n:2 * n]
        ssem, rsem = refs[2 * n:]
        x, y, c, chips = _place()
        me_q = 2 * x + y
        sib = (x, y, 1 - c)

        def half(a, q, cc):
            return out[a].at[q, :, pl.ds(cc * halves[a], halves[a]), :]

        first = []
        for a in range(n):
            mine = src[a].at[:, pl.ds(c * halves[a], halves[a]), :]
            for r, chip in enumerate(chips):
                first.append(_remote(mine, half(a, me_q, c), ssem, rsem, a * 3 + r, (*chip, c)))
        for cp in first:
            cp.start()
        passed = []
        for a in range(n):
            for r, chip in enumerate(chips):
                q = 2 * chip[0] + chip[1]
                k = a * 3 + r
                _remote(half(a, q, c), half(a, q, c), ssem, rsem, k, (*chip, c)).wait_recv()
                cp = _remote(half(a, q, c), half(a, q, c), ssem, rsem, 3 * n + k, sib)
                cp.start()
                passed.append(cp)
        for a in range(n):
            for r, chip in enumerate(chips):
                q = 2 * chip[0] + chip[1]
                _remote(half(a, q, 1 - c), half(a, q, 1 - c), ssem, rsem, 3 * n + a * 3 + r, sib).wait_recv()
        for cp in first + passed:
            cp.wait_send()

    return pl.pallas_call(
        body, name="gather_weights",
        in_specs=[HBM_SPEC] * n, out_specs=[HBM_SPEC] * n,
        out_shape=[jax.ShapeDtypeStruct((N_CHIPS,) + s.shape, s.dtype) for s in shards],
        scratch_shapes=[pltpu.SemaphoreType.DMA((6 * n,)), pltpu.SemaphoreType.DMA((6 * n,))],
        compiler_params=_params(has_side_effects=True),
    )(*shards)


def _gather_over_ici(shards):
    n = len(shards)
    halves = [s.shape[1] // 2 for s in shards]

    def copies(src, out, ssem, rsem):
        x, y, c, chips = _place()
        me_q = 2 * x + y
        res = []
        for a in range(n):
            rows = pl.ds(c * halves[a], halves[a])
            mine = src[a].at[:, rows, :]
            for r, chip in enumerate(chips):
                dev = (*chip, c)
                res.append((_remote(mine, out[a].at[me_q, :, rows, :], ssem, rsem, a * 3 + r, dev),
                            _remote(mine, out[a].at[2 * chip[0] + chip[1], :, rows, :], ssem, rsem, a * 3 + r, dev)))
        return res

    return _Hosted(list(shards), [jax.ShapeDtypeStruct((N_CHIPS,) + s.shape, s.dtype) for s in shards], 3 * n, copies)


def _pass_to_sibling(gathered):
    n = len(gathered)
    halves = [g.shape[2] // 2 for g in gathered]

    def body(*refs):
        out = refs[n:2 * n]
        ssem, rsem = refs[2 * n:]
        x, y, c, chips = _place()
        sib = (x, y, 1 - c)

        def half(a, q, cc):
            return out[a].at[q, :, pl.ds(cc * halves[a], halves[a]), :]

        cps = []
        for a in range(n):
            for r, chip in enumerate(chips):
                q = 2 * chip[0] + chip[1]
                cps.append(_remote(half(a, q, c), half(a, q, c), ssem, rsem, a * 3 + r, sib))
        for cp in cps:
            cp.start()
        for a in range(n):
            for r, chip in enumerate(chips):
                q = 2 * chip[0] + chip[1]
                _remote(half(a, q, 1 - c), half(a, q, 1 - c), ssem, rsem, a * 3 + r, sib).wait_recv()
        for cp in cps:
            cp.wait_send()

    return pl.pallas_call(
        body, name="pass_to_sibling",
        in_specs=[HBM_SPEC] * n, out_specs=[HBM_SPEC] * n,
        out_shape=[jax.ShapeDtypeStruct(g.shape, g.dtype) for g in gathered],
        input_output_aliases={a: a for a in range(n)},
        scratch_shapes=[pltpu.SemaphoreType.DMA((3 * n,)), pltpu.SemaphoreType.DMA((3 * n,))],
        compiler_params=_params(has_side_effects=True),
    )(*gathered)


def _scatter_over_ici(parts):
    n = len(parts)

    def copies(src, out, ssem, rsem):
        x, y, c, chips = _place()
        res = []
        for a in range(n):
            for r, chip in enumerate(chips):
                cp = _remote(src[a].at[2 * chip[0] + chip[1]], out[a].at[r], ssem, rsem, a * 3 + r, (*chip, c))
                res.append((cp, cp))
        return res

    return _Hosted(list(parts), [jax.ShapeDtypeStruct((3,) + p.shape[1:], F32) for p in parts], 3 * n, copies)


def _swap_over_d2d(grads):
    n = len(grads)
    halves = [g.shape[1] // 2 for g in grads]

    def copies(src, out, ssem, rsem):
        x, y, c, _ = _place()
        res = []
        for a in range(n):
            cp = _remote(src[a].at[:, pl.ds((1 - c) * halves[a], halves[a]), :], out[a], ssem, rsem, a, (x, y, 1 - c))
            res.append((cp, cp))
        return res

    return _Hosted(list(grads), [jax.ShapeDtypeStruct((N_CHIPS, h, g.shape[2]), F32) for g, h in zip(grads, halves)],
                   n, copies)


def _swap_halves(grads):
    n = len(grads)
    halves = [g.shape[1] // 2 for g in grads]

    def body(*refs):
        src, out = refs[:n], refs[n:2 * n]
        ssem, rsem = refs[2 * n:]
        x, y, c, _ = _place()
        cps = [_remote(src[a].at[:, pl.ds((1 - c) * halves[a], halves[a]), :], out[a], ssem, rsem, a, (x, y, 1 - c))
               for a in range(n)]
        for cp in cps:
            cp.start()
        for cp in cps:
            cp.wait()

    return pl.pallas_call(
        body, name="swap_halves",
        in_specs=[HBM_SPEC] * n, out_specs=[HBM_SPEC] * n,
        out_shape=[jax.ShapeDtypeStruct((N_CHIPS, h, g.shape[2]), F32) for g, h in zip(grads, halves)],
        scratch_shapes=[pltpu.SemaphoreType.DMA((n,)), pltpu.SemaphoreType.DMA((n,))],
        compiler_params=_params(has_side_effects=True),
    )(*grads)


def _scatter_chips(parts):
    n = len(parts)

    def body(*refs):
        src, out = refs[:n], refs[n:2 * n]
        ssem, rsem = refs[2 * n:]
        x, y, c, chips = _place()
        cps = []
        for a in range(n):
            for r, chip in enumerate(chips):
                cps.append(_remote(src[a].at[2 * chip[0] + chip[1]], out[a].at[r], ssem, rsem, a * 3 + r, (*chip, c)))
        for cp in cps:
            cp.start()
        for cp in cps:
            cp.wait()

    return pl.pallas_call(
        body, name="scatter_chips",
        in_specs=[HBM_SPEC] * n, out_specs=[HBM_SPEC] * n,
        out_shape=[jax.ShapeDtypeStruct((3,) + p.shape[1:], F32) for p in parts],
        scratch_shapes=[pltpu.SemaphoreType.DMA((3 * n,)), pltpu.SemaphoreType.DMA((3 * n,))],
        compiler_params=_params(has_side_effects=True),
    )(*parts)


def _swap_reduced(reduced):
    n = len(reduced)

    def body(*refs):
        src, out = refs[:n], refs[n:2 * n]
        ssem, rsem = refs[2 * n:]
        x, y, c, _ = _place()
        cps = [_remote(src[a], out[a], ssem, rsem, a, (x, y, 1 - c)) for a in range(n)]
        for cp in cps:
            cp.start()
        for cp in cps:
            cp.wait()

    return pl.pallas_call(
        body, name="swap_reduced",
        in_specs=[HBM_SPEC] * n, out_specs=[HBM_SPEC] * n,
        out_shape=[jax.ShapeDtypeStruct(r.shape, F32) for r in reduced],
        scratch_shapes=[pltpu.SemaphoreType.DMA((n,)), pltpu.SemaphoreType.DMA((n,))],
        compiler_params=_params(has_side_effects=True),
    )(*reduced)


def _allreduce_small(buf):
    R, L = buf.shape

    def body(buf_ref, out_ref, pair_ref, chip_ref, ssem, rsem):
        x, y, c, chips = _place()
        me_q = 2 * x + y
        pair_ref[c] = buf_ref[...]
        to_sib = _remote(buf_ref, pair_ref.at[c], ssem, rsem, 0, (x, y, 1 - c))
        to_sib.start()
        _remote(buf_ref, pair_ref.at[1 - c], ssem, rsem, 0, (x, y, 1 - c)).wait_recv()
        chip_ref[me_q] = pair_ref[0] + pair_ref[1]
        cps = [_remote(chip_ref.at[me_q], chip_ref.at[me_q], ssem, rsem, 1 + r, (*chip, c))
               for r, chip in enumerate(chips)]
        for cp in cps:
            cp.start()
        for r, chip in enumerate(chips):
            q = 2 * chip[0] + chip[1]
            _remote(chip_ref.at[q], chip_ref.at[q], ssem, rsem, 1 + r, (*chip, c)).wait_recv()
        out_ref[...] = (chip_ref[0] + chip_ref[1]) + (chip_ref[2] + chip_ref[3])
        to_sib.wait_send()
        for cp in cps:
            cp.wait_send()

    return pl.pallas_call(
        body, name="allreduce_small",
        in_specs=[VMEM_SPEC], out_specs=VMEM_SPEC,
        out_shape=jax.ShapeDtypeStruct((R, L), F32),
        scratch_shapes=[pltpu.VMEM((2, R, L), F32), pltpu.VMEM((N_CHIPS, R, L), F32),
                        pltpu.SemaphoreType.DMA((4,)), pltpu.SemaphoreType.DMA((4,))],
        compiler_params=_params(has_side_effects=True),
    )(buf)


def _pack(arrays):
    flat = jnp.concatenate([a.reshape(-1) for a in arrays])
    pad = (-flat.shape[0]) % (8 * LANES)
    return jnp.pad(flat, (0, pad)).reshape(-1, LANES)


def _unpack(buf, like):
    flat = buf.reshape(-1)
    out, off = [], 0
    for a in like:
        out.append(flat[off:off + a.size].reshape(a.shape))
        off += a.size
    return out


def _block_diag(pw):
    rows = []
    for gi in range(len(POOL_WINDOWS)):
        blocks = [pw[gi] if gj == gi else jnp.zeros_like(pw[gi]) for gj in range(len(POOL_WINDOWS))]
        rows.append(jnp.concatenate(blocks, axis=1))
    return jnp.concatenate(rows, axis=0)


def kernel(x, norm1, w_in, pool_w, pool_scale, sg_norm, sg_w, sg_b, w_out, norm2, w_up, w_down, final_norm, loss_target, m_norm1, m_w_in, m_pool_w, m_pool_scale, m_sg_norm, m_sg_w, m_sg_b, m_w_out, m_norm2, m_w_up, m_w_down, m_final_norm, v_norm1, v_w_in, v_pool_w, v_pool_scale, v_sg_norm, v_sg_w, v_sg_b, v_w_out, v_norm2, v_w_up, v_w_down, v_final_norm):
    depth = norm1.shape[0]
    T = x.shape[1]
    xs = x.reshape(T, D_MODEL)
    target = loss_target.reshape(T, D_MODEL)

    assert depth == 2
    c_idx = lax.axis_index("c")
    q_idx = 2 * lax.axis_index("x") + lax.axis_index("y")
    own = [w.astype(BF16) for w in (w_in, w_out, w_up, w_down)]
    gathered = {(0, 0): _gather_weights([own[0][:1]])[0]}

    def full(a, l, axis):
        blocks = [jnp.where(q_idx == q, own[a][l], gathered[(a, l)][q, 0]) for q in range(N_CHIPS)]
        return jnp.concatenate(blocks, axis=axis)

    def gather_behind(call, keys):
        res, over_ici = call(_gather_over_ici([own[a][l:l + 1] for a, l in keys]))
        gathered.update(zip(keys, _pass_to_sibling(over_ici)))
        return res

    tril = jnp.tril(jnp.ones((CHUNK, CHUNK), F32))
    saved = []
    cur = xs
    wi, wo, wu, wd = {}, {}, {}, {}
    for l in range(depth):
        wbd = _block_diag(pool_w[l]).astype(BF16)
        wm = sg_w[l] * tril
        wm_s = wm.reshape(SG_HEADS * CHUNK, CHUNK).astype(BF16)
        wmt_s = jnp.swapaxes(wm, 1, 2).reshape(SG_HEADS * CHUNK, CHUNK).astype(BF16)
        bias = jnp.repeat(sg_b[l].T, SB_HD, axis=1)
        n1, n2 = norm1[l][None], norm2[l][None]
        psc, sgn = pool_scale[l][None], sg_norm[l][None]
        wi[l] = full(0, l, 1)
        proj, h, qkv = _inproj_fwd(cur, n1, wi[l])
        ya = _pool_fwd(proj, wbd, psc)
        yb = _sg_fwd(proj, wm_s, bias, sgn)
        if l == 0:
            yc = gather_behind(lambda hosted: _attn_fwd(qkv, hosted), [(1, 0), (2, 0), (3, 0)])
        else:
            yc, _ = _attn_fwd(qkv)
        wo[l], wu[l], wd[l] = full(1, l, 0), full(2, l, 1), full(3, l, 0)
        x1, ymix = _outproj_fwd(cur, ya, yb, yc, wo[l])
        if l == 0:
            x2, h2, u, act = gather_behind(lambda hosted: _mlp_fwd(x1, n2, wu[l], wd[l], hosted),
                                           [(0, 1), (1, 1), (2, 1), (3, 1)])
        else:
            (x2, h2, u, act), _ = _mlp_fwd(x1, n2, wu[l], wd[l])
        saved.append(dict(x0=cur, x1=x1, proj=proj, h=h, qkv=qkv, yc=yc, ymix=ymix, h2=h2, u=u, act=act,
                          wbd=wbd, wm_s=wm_s, wmt_s=wmt_s, bias=bias, n1=n1, n2=n2, psc=psc, sgn=sgn))
        cur = x2

    loss_row, dcur, d_final = _loss_head(cur, final_norm[None], target)

    small = [None] * depth
    grads, parts, reduced = {}, {}, {}

    def pair_up(keys, swapped):
        parts.update({k: _add_pair(grads[k], o, c_idx) for k, o in zip(keys, swapped)})

    def chip_up(keys, arrived):
        reduced.update({k: _add_chips(parts[k], r, q_idx) for k, r in zip(keys, arrived)})

    for l in reversed(range(depth)):
        s = saved[l]
        if l == 0:
            keys = [(2, 1), (3, 1)]
            (dx1, du, d_n2), arrived = _mlp_bwd(dcur, s["x1"], s["n2"], s["u"], wu[l], wd[l],
                                                _scatter_over_ici([parts[k] for k in keys]))
            chip_up(keys, arrived)
        else:
            (dx1, du, d_n2), _ = _mlp_bwd(dcur, s["x1"], s["n2"], s["u"], wu[l], wd[l])
        grads[(2, l)] = _tn_matmul(s["h2"], du, "grad_w_up", n_split=N_CHIPS)
        grads[(3, l)] = _tn_matmul(s["act"], dcur, "grad_w_down")[0].reshape(N_CHIPS, D_FF // N_CHIPS, D_MODEL)
        dymix = _nt_matmul(dx1, wo[l])
        grads[(1, l)] = _tn_matmul(s["ymix"], dx1, "grad_w_out")[0].reshape(N_CHIPS, D_MODEL // N_CHIPS, D_MODEL)
        da_in, d_wbd, d_psc = _pool_bwd(s["proj"], dymix, s["wbd"], s["psc"])
        if l == 0:
            keys = [(0, 1), (1, 0), (2, 0), (3, 0)]
            (du_pre, dv_pre, d_wm, d_bias, d_sgn), swapped = _sg_bwd(
                s["proj"], dymix, s["wm_s"], s["wmt_s"], s["bias"], s["sgn"], _swap_over_d2d([grads[k] for k in keys]))
            pair_up(keys, swapped)
            keys = [(1, 1)] + keys
            (dq, dk, dv), arrived = _attn_bwd(s["qkv"], s["yc"], dymix, _scatter_over_ici([parts[k] for k in keys]))
            chip_up(keys, arrived)
        else:
            (du_pre, dv_pre, d_wm, d_bias, d_sgn), _ = _sg_bwd(s["proj"], dymix, s["wm_s"], s["wmt_s"], s["bias"], s["sgn"])
            keys = [(1, l), (2, l), (3, l)]
            (dq, dk, dv), swapped = _attn_bwd(s["qkv"], s["yc"], dymix, _swap_over_d2d([grads[k] for k in keys]))
            pair_up(keys, swapped)
        dproj, dx0, d_n1 = _inproj_bwd([da_in, du_pre, dv_pre, dq, dk, dv], wi[l], s["x0"], s["n1"], dx1)
        g_in_l = _tn_matmul(s["h"], dproj, "grad_w_in")[0]
        grads[(0, l)] = g_in_l.reshape(D_MODEL, N_CHIPS, IN_COLS // N_CHIPS).transpose(1, 0, 2)
        d_pw = jnp.stack([d_wbd[gi * POOL_GW:(gi + 1) * POOL_GW, gi * POOL_GW:(gi + 1) * POOL_GW]
                          for gi in range(len(POOL_WINDOWS))])
        small[l] = dict(norm1=d_n1[0], pool_w=d_pw, pool_scale=d_psc[0], sg_norm=d_sgn[0],
                        sg_w=d_wm.reshape(SG_HEADS, CHUNK, CHUNK), sg_b=d_bias[:, :SG_HEADS].T, norm2=d_n2[0])
        dcur = dx0
    grad_x = dcur.reshape(x.shape)

    keys = [(0, 0)]
    pair_up(keys, _swap_halves([grads[k] for k in keys]))
    chip_up(keys, _scatter_chips([parts[k] for k in keys]))
    keys = sorted(reduced)
    theirs = dict(zip(keys, _swap_reduced([reduced[k] for k in keys])))

    def joined(a):
        layers = []
        for l in range(depth):
            mine, other = reduced[(a, l)], theirs[(a, l)]
            layers.append(jnp.where(c_idx == 0, jnp.concatenate([mine, other]), jnp.concatenate([other, mine])))
        return jnp.stack(layers)

    gw_in, gw_out, gw_up, gw_down = [joined(a) for a in range(4)]

    names = ["norm1", "pool_w", "pool_scale", "sg_norm", "sg_w", "sg_b", "norm2"]
    slot = jnp.zeros((1,), F32)
    small_w = [norm1, pool_w, pool_scale, sg_norm, sg_w, sg_b, norm2, final_norm, slot]
    small_m = [m_norm1, m_pool_w, m_pool_scale, m_sg_norm, m_sg_w, m_sg_b, m_norm2, m_final_norm, slot]
    small_v = [v_norm1, v_pool_w, v_pool_scale, v_sg_norm, v_sg_w, v_sg_b, v_norm2, v_final_norm, slot]
    small_g = [jnp.stack([small[l][k] for l in range(depth)]) for k in names] + [d_final[0], loss_row[0, :1]]
    g_packed = _allreduce_small(_pack(small_g))
    loss = _unpack(g_packed, small_w)[-1][0]
    s_delta, s_m, s_v = _elementwise(_adamw, "adamw_small", [_pack(small_w), g_packed, _pack(small_m), _pack(small_v)], 3)
    gs = dict(zip(names + ["final_norm"], _unpack(g_packed, small_w)))
    ds = dict(zip(names + ["final_norm"], _unpack(s_delta, small_w)))
    ms = dict(zip(names + ["final_norm"], _unpack(s_m, small_w)))
    vs = dict(zip(names + ["final_norm"], _unpack(s_v, small_w)))

    big_g = dict(w_in=gw_in, w_out=gw_out, w_up=gw_up, w_down=gw_down)
    big_w = dict(w_in=(w_in, m_w_in, v_w_in), w_out=(w_out, m_w_out, v_w_out),
                 w_up=(w_up, m_w_up, v_w_up), w_down=(w_down, m_w_down, v_w_down))
    for k, (w, m, v) in big_w.items():
        ds[k], ms[k], vs[k] = _elementwise(_adamw, "adamw_" + k, [w, big_g[k], m, v], 3)
        gs[k] = big_g[k]

    order = ["norm1", "w_in", "pool_w", "pool_scale", "sg_norm", "sg_w", "sg_b", "w_out", "norm2", "w_up", "w_down",
             "final_norm"]
    return (loss, grad_x, *[gs[k] for k in order], *[ds[k] for k in order], *[ms[k] for k in order],
            *[vs[k] for k in order])
```

```python
import functools

import jax
import jax.numpy as jnp
from jax import lax
from jax.experimental import pallas as pl
from jax.experimental.pallas import tpu as pltpu

F32 = jnp.float32
BF16 = jnp.bfloat16
MESH = pl.DeviceIdType.MESH
AXES = ("x", "y", "c")

EPS = 1e-6
D_MODEL = 1024
POOL_WIDTH = 256
SG_WIDTH = 256
SB_WIDTH = 512
POOL_WINDOWS = (2, 4, 8, 16)
POOL_GW = 64
POOL_HALO = 16
CHUNK = 128
SG_HEADS = 4
SB_HD = 64
SB_SCALE = 0.125
IN_COLS = 2304
QKV_OFF = 768
D_FF = 4096
N_CHIPS = 4
LANES = 128
VMEM_LIMIT = 56 * 1024 * 1024
ATTN_TILE = 256
UNDERFLOW = -104.0

ADAM_LR = 0.001
ADAM_B1 = 0.9
ADAM_B2 = 0.999
ADAM_EPS = 1e-08
ADAM_WD = 0.01
ADAM_STEP = 10

HBM_SPEC = pl.BlockSpec(memory_space=pl.ANY)
VMEM_SPEC = pl.BlockSpec(memory_space=pltpu.VMEM)


def _params(**kw):
    return pltpu.CompilerParams(vmem_limit_bytes=VMEM_LIMIT, **kw)


def _tile(n, pref):
    if n <= pref:
        return n
    for t in range(pref - pref % LANES, 0, -LANES):
        if n % t == 0:
            return t
    raise ValueError((n, pref))


def _nn(a, b):
    return jnp.dot(a, b, preferred_element_type=F32)


def _nt(a, b):
    return lax.dot_general(a, b, (((1,), (1,)), ((), ())), preferred_element_type=F32)


def _tn(a, b):
    return lax.dot_general(a, b, (((0,), (0,)), ((), ())), preferred_element_type=F32)


def _rms_fwd(x, g):
    r = lax.rsqrt(jnp.mean(x * x, axis=-1, keepdims=True) + EPS)
    xhat = x * r
    return xhat * g, xhat, r


def _rms_bwd(dy, xhat, r, g):
    dxhat = dy * g
    dx = r * (dxhat - xhat * jnp.mean(dxhat * xhat, axis=-1, keepdims=True))
    return dx, dy * xhat


_GELU_K = 0.7978845608028654
_GELU_C = 0.044715


def _gelu(x):
    return 0.5 * x * (1.0 + jnp.tanh(_GELU_K * (x + _GELU_C * x * x * x)))


def _gelu_grad(x):
    t = jnp.tanh(_GELU_K * (x + _GELU_C * x * x * x))
    return 0.5 * (1.0 + t) + 0.5 * x * (1.0 - t * t) * _GELU_K * (1.0 + 3.0 * _GELU_C * x * x)


def _inproj_fwd(x, g, w):
    T, D = x.shape
    N = w.shape[1]
    tt = _tile(T, 512)

    def body(x_ref, g_ref, w_ref, proj_ref, h_ref, qkv_ref):
        h, _, _ = _rms_fwd(x_ref[...], g_ref[...])
        hb = h.astype(BF16)
        h_ref[...] = hb
        p = _nn(hb, w_ref[...])
        proj_ref[...] = p[:, :QKV_OFF]
        qkv_ref[...] = p[:, QKV_OFF:].astype(BF16)

    return pl.pallas_call(
        body, name="inproj_fwd", grid=(T // tt,),
        in_specs=[pl.BlockSpec((tt, D), lambda i: (i, 0)), pl.BlockSpec((1, D), lambda i: (0, 0)),
                  pl.BlockSpec((D, N), lambda i: (0, 0))],
        out_specs=[pl.BlockSpec((tt, QKV_OFF), lambda i: (i, 0)), pl.BlockSpec((tt, D), lambda i: (i, 0)),
                   pl.BlockSpec((tt, N - QKV_OFF), lambda i: (i, 0))],
        out_shape=[jax.ShapeDtypeStruct((T, QKV_OFF), F32), jax.ShapeDtypeStruct((T, D), BF16),
                   jax.ShapeDtypeStruct((T, N - QKV_OFF), BF16)],
        compiler_params=_params(),
    )(x, g, w)


def _inproj_bwd(pieces, w, x, g, dres):
    T, D = x.shape
    N = w.shape[1]
    tt = _tile(T, 512)
    widths = [p.shape[1] for p in pieces]
    offs = [sum(widths[:k]) for k in range(len(widths))]
    assert sum(widths) == N
    n_p = len(pieces)

    def body(*refs):
        p_refs = refs[:n_p]
        w_ref, x_ref, g_ref, dres_ref, dproj_ref, dx_ref, dg_ref = refs[n_p:]
        for p_ref, o, wd in zip(p_refs, offs, widths):
            dproj_ref[:, o:o + wd] = p_ref[...].astype(BF16)
        dh = _nt(dproj_ref[...], w_ref[...])
        gv = g_ref[...]
        _, xhat, r = _rms_fwd(x_ref[...], gv)
        dx, dgrow = _rms_bwd(dh, xhat, r, gv)
        dx_ref[...] = dres_ref[...] + dx

        @pl.when(pl.program_id(0) == 0)
        def _():
            dg_ref[...] = jnp.zeros_like(dg_ref)

        dg_ref[...] += jnp.sum(dgrow, axis=0, keepdims=True)

    return pl.pallas_call(
        body, name="inproj_bwd", grid=(T // tt,),
        in_specs=[pl.BlockSpec((tt, wd), lambda i: (i, 0)) for wd in widths] + [
            pl.BlockSpec((D, N), lambda i: (0, 0)), pl.BlockSpec((tt, D), lambda i: (i, 0)),
            pl.BlockSpec((1, D), lambda i: (0, 0)), pl.BlockSpec((tt, D), lambda i: (i, 0))],
        out_specs=[pl.BlockSpec((tt, N), lambda i: (i, 0)), pl.BlockSpec((tt, D), lambda i: (i, 0)),
                   pl.BlockSpec((1, D), lambda i: (0, 0))],
        out_shape=[jax.ShapeDtypeStruct((T, N), BF16), jax.ShapeDtypeStruct((T, D), F32),
                   jax.ShapeDtypeStruct((1, D), F32)],
        compiler_params=_params(),
    )(*pieces, w, x, g, dres)


def _pool_select(s2, s4, s8, s16, grp):
    return jnp.where(grp == 0, s2, jnp.where(grp == 1, s4, jnp.where(grp == 2, s8, s16)))


def _pool_count(t_glob, grp):
    win = jnp.where(grp == 0, 2, jnp.where(grp == 1, 4, jnp.where(grp == 2, 8, 16)))
    return jnp.minimum(t_glob + 1, win).astype(F32)


def _pool_diff(a, halo, base, tt):
    n = tt + POOL_HALO
    ext = jnp.concatenate([halo, a], axis=0)
    s2 = ext + pltpu.roll(ext, 1, 0)
    s4 = s2 + pltpu.roll(s2, 2, 0)
    s8 = s4 + pltpu.roll(s4, 4, 0)
    s16 = s8 + pltpu.roll(s8, 8, 0)
    grp = lax.broadcasted_iota(jnp.int32, (n, POOL_WIDTH), 1) // POOL_GW
    t_glob = lax.broadcasted_iota(jnp.int32, (n, POOL_WIDTH), 0) + (base - POOL_HALO)
    pooled = _pool_select(s2, s4, s8, s16, grp) / _pool_count(t_glob, grp)
    return pooled[POOL_HALO:] - a


def _pool_specs(T, tt):
    hb = tt // POOL_HALO
    return [pl.BlockSpec((tt, POOL_WIDTH), lambda i: (i, 0)),
            pl.BlockSpec((POOL_HALO, POOL_WIDTH), lambda i: (jnp.maximum(i * hb - 1, 0), 0))]


def _pool_fwd(proj, wbd, scale):
    T = proj.shape[0]
    tt = _tile(T, 512)

    def body(a_ref, halo_ref, w_ref, sc_ref, y_ref):
        i = pl.program_id(0)
        halo = jnp.where(i > 0, halo_ref[...], 0.0)
        d = _pool_diff(a_ref[...], halo, i * tt, tt)
        y_ref[...] = _nn(d.astype(BF16), w_ref[...]) * sc_ref[...]

    return pl.pallas_call(
        body, name="pool_fwd", grid=(T // tt,),
        in_specs=_pool_specs(T, tt) + [pl.BlockSpec((POOL_WIDTH, POOL_WIDTH), lambda i: (0, 0)),
                                       pl.BlockSpec((1, POOL_WIDTH), lambda i: (0, 0))],
        out_specs=pl.BlockSpec((tt, POOL_WIDTH), lambda i: (i, 0)),
        out_shape=jax.ShapeDtypeStruct((T, POOL_WIDTH), F32),
        compiler_params=_params(),
    )(proj, proj, wbd, scale)


def _pool_bwd(proj, dymix, wbd, scale):
    T = proj.shape[0]
    tt = _tile(T, 512)
    hb = tt // POOL_HALO
    nblk = T // tt
    n = tt + POOL_HALO

    def body(a_ref, halo_ref, dy_ref, dyn_ref, w_ref, sc_ref, da_ref, dw_ref, dsc_ref):
        i = pl.program_id(0)
        halo = jnp.where(i > 0, halo_ref[...], 0.0)
        d = _pool_diff(a_ref[...], halo, i * tt, tt)
        db = d.astype(BF16)
        wv = w_ref[...]
        sc = sc_ref[...]
        dy = dy_ref[...]
        dys = dy * sc

        @pl.when(i == 0)
        def _():
            dw_ref[...] = jnp.zeros_like(dw_ref)
            dsc_ref[...] = jnp.zeros_like(dsc_ref)

        dsc_ref[...] += jnp.sum(dy * _nn(db, wv), axis=0, keepdims=True)
        dw_ref[...] += _tn(db, dys.astype(BF16))
        dyn = jnp.where(i < nblk - 1, dyn_ref[...], 0.0) * sc
        dd = _nt(jnp.concatenate([dys, dyn], axis=0).astype(BF16), wv)
        grp = lax.broadcasted_iota(jnp.int32, (n, POOL_WIDTH), 1) // POOL_GW
        t_glob = lax.broadcasted_iota(jnp.int32, (n, POOL_WIDTH), 0) + i * tt
        e = dd / _pool_count(t_glob, grp)
        r2 = e + pltpu.roll(e, n - 1, 0)
        r4 = r2 + pltpu.roll(r2, n - 2, 0)
        r8 = r4 + pltpu.roll(r4, n - 4, 0)
        r16 = r8 + pltpu.roll(r8, n - 8, 0)
        da_ref[...] = (_pool_select(r2, r4, r8, r16, grp) - dd)[:tt]

    return pl.pallas_call(
        body, name="pool_bwd", grid=(nblk,),
        in_specs=_pool_specs(T, tt) + [
            pl.BlockSpec((tt, POOL_WIDTH), lambda i: (i, 0)),
            pl.BlockSpec((POOL_HALO, POOL_WIDTH), lambda i: (jnp.minimum((i + 1) * hb, T // POOL_HALO - 1), 0)),
            pl.BlockSpec((POOL_WIDTH, POOL_WIDTH), lambda i: (0, 0)), pl.BlockSpec((1, POOL_WIDTH), lambda i: (0, 0))],
        out_specs=[pl.BlockSpec((tt, POOL_WIDTH), lambda i: (i, 0)),
                   pl.BlockSpec((POOL_WIDTH, POOL_WIDTH), lambda i: (0, 0)),
                   pl.BlockSpec((1, POOL_WIDTH), lambda i: (0, 0))],
        out_shape=[jax.ShapeDtypeStruct((T, POOL_WIDTH), F32),
                   jax.ShapeDtypeStruct((POOL_WIDTH, POOL_WIDTH), F32),
                   jax.ShapeDtypeStruct((1, POOL_WIDTH), F32)],
        compiler_params=_params(),
    )(proj, proj, dymix, dymix, wbd, scale)


def _head_select(stacked, grp):
    out = jnp.where(grp == 0, stacked[0:CHUNK], 0.0)
    for h in range(1, SG_HEADS):
        out = out + jnp.where(grp == h, stacked[h * CHUNK:(h + 1) * CHUNK], 0.0)
    return out


def _sg_specs(tt):
    return [pl.BlockSpec((tt, SG_WIDTH), lambda i: (i, 1)), pl.BlockSpec((tt, SG_WIDTH), lambda i: (i, 2))]


def _sg_fwd(proj, wm, bias, g):
    T = proj.shape[0]
    tt = _tile(T, 512)

    def body(u_ref, v_ref, wm_ref, b_ref, g_ref, y_ref):
        zu = _gelu(u_ref[...])
        vn, _, _ = _rms_fwd(_gelu(v_ref[...]), g_ref[...])
        grp = lax.broadcasted_iota(jnp.int32, (CHUNK, SG_WIDTH), 1) // SB_HD
        for n in range(tt // CHUNK):
            rows = slice(n * CHUNK, (n + 1) * CHUNK)
            sv = _head_select(_nn(wm_ref[...], vn[rows].astype(BF16)), grp) + b_ref[...]
            y_ref[rows, :] = zu[rows] * sv

    return pl.pallas_call(
        body, name="sg_fwd", grid=(T // tt,),
        in_specs=_sg_specs(tt) + [pl.BlockSpec((SG_HEADS * CHUNK, CHUNK), lambda i: (0, 0)),
                                  pl.BlockSpec((CHUNK, SG_WIDTH), lambda i: (0, 0)),
                                  pl.BlockSpec((1, SG_WIDTH), lambda i: (0, 0))],
        out_specs=pl.BlockSpec((tt, SG_WIDTH), lambda i: (i, 0)),
        out_shape=jax.ShapeDtypeStruct((T, SG_WIDTH), F32),
        compiler_params=_params(),
    )(proj, proj, wm, bias, g)


def _sg_bwd(proj, dymix, wm, wmt, bias, g, hosted=None):
    T = proj.shape[0]
    tt = _tile(T, 512)
    nblk = T // tt

    def body(*refs):
        i = pl.program_id(0)
        (u_ref, v_ref, dy_ref, wm_ref, wmt_ref, b_ref, g_ref, du_ref, dv_ref, dw_ref, db_ref, dg_ref,
         dvn_ref, dbias_ref), start, wait = _host(hosted, refs, 7, 5, i == 0, i == nblk - 1)
        start()
        up, vp = u_ref[...], v_ref[...]
        gv = g_ref[...]
        zu, zv = _gelu(up), _gelu(vp)
        vn, xhat, r = _rms_fwd(zv, gv)
        gu = _gelu_grad(up)
        grp = lax.broadcasted_iota(jnp.int32, (CHUNK, SG_WIDTH), 1) // SB_HD

        @pl.when(i == 0)
        def _():
            dw_ref[...] = jnp.zeros_like(dw_ref)
            dbias_ref[...] = jnp.zeros_like(dbias_ref)
            dg_ref[...] = jnp.zeros_like(dg_ref)

        for n in range(tt // CHUNK):
            rows = slice(n * CHUNK, (n + 1) * CHUNK)
            vc = vn[rows].astype(BF16)
            sv = _head_select(_nn(wm_ref[...], vc), grp) + b_ref[...]
            dy = dy_ref[rows, :]
            du_ref[rows, :] = dy * sv * gu[rows]
            dsv = dy * zu[rows]
            dsvb = dsv.astype(BF16)
            dvn_ref[rows, :] = _head_select(_nn(wmt_ref[...], dsvb), grp)
            stacked = jnp.concatenate([jnp.where(grp == h, dsv, 0.0) for h in range(SG_HEADS)], axis=0)
            dw_ref[...] += _nt(stacked.astype(BF16), vc)
            dbias_ref[...] += dsv

        dzv, dgrow = _rms_bwd(dvn_ref[...], xhat, r, gv)
        dg_ref[...] += jnp.sum(dgrow, axis=0, keepdims=True)
        dv_ref[...] = dzv * _gelu_grad(vp)

        @pl.when(i == nblk - 1)
        def _():
            t_i = lax.broadcasted_iota(jnp.int32, (SG_HEADS * CHUNK, CHUNK), 0) % CHUNK
            s_i = lax.broadcasted_iota(jnp.int32, (SG_HEADS * CHUNK, CHUNK), 1)
            dw_ref[...] = jnp.where(s_i <= t_i, dw_ref[...], 0.0)
            lane = lax.broadcasted_iota(jnp.int32, (CHUNK, LANES), 1)
            acc = jnp.zeros((CHUNK, LANES), F32)
            for h in range(SG_HEADS):
                tot = jnp.sum(jnp.where(grp == h, dbias_ref[...], 0.0), axis=1, keepdims=True)
                acc = acc + jnp.where(lane == h, tot, 0.0)
            db_ref[...] = acc

        wait()

    h_ins = hosted.ins if hosted else []
    res = pl.pallas_call(
        body, name="sg_bwd_hosting" if hosted else "sg_bwd", grid=(nblk,),
        in_specs=_sg_specs(tt) + [pl.BlockSpec((tt, SG_WIDTH), lambda i: (i, 1)),
                                  pl.BlockSpec((SG_HEADS * CHUNK, CHUNK), lambda i: (0, 0)),
                                  pl.BlockSpec((SG_HEADS * CHUNK, CHUNK), lambda i: (0, 0)),
                                  pl.BlockSpec((CHUNK, SG_WIDTH), lambda i: (0, 0)),
                                  pl.BlockSpec((1, SG_WIDTH), lambda i: (0, 0))] + [HBM_SPEC] * len(h_ins),
        out_specs=[pl.BlockSpec((tt, SG_WIDTH), lambda i: (i, 0)), pl.BlockSpec((tt, SG_WIDTH), lambda i: (i, 0)),
                   pl.BlockSpec((SG_HEADS * CHUNK, CHUNK), lambda i: (0, 0)),
                   pl.BlockSpec((CHUNK, LANES), lambda i: (0, 0)), pl.BlockSpec((1, SG_WIDTH), lambda i: (0, 0))]
        + [HBM_SPEC] * len(h_ins),
        out_shape=[jax.ShapeDtypeStruct((T, SG_WIDTH), F32), jax.ShapeDtypeStruct((T, SG_WIDTH), F32),
                   jax.ShapeDtypeStruct((SG_HEADS * CHUNK, CHUNK), F32),
                   jax.ShapeDtypeStruct((CHUNK, LANES), F32), jax.ShapeDtypeStruct((1, SG_WIDTH), F32)]
        + (hosted.out_shapes if hosted else []),
        scratch_shapes=[pltpu.VMEM((tt, SG_WIDTH), F32), pltpu.VMEM((CHUNK, SG_WIDTH), F32)]
        + (hosted.sems() if hosted else []),
        compiler_params=_params(has_side_effects=hosted is not None),
    )(proj, proj, dymix, wm, wmt, bias, g, *h_ins)
    return res[:5], res[5:]


def _split_dot(x, u):
    hi = x.astype(BF16)
    lo = (x - hi.astype(F32)).astype(BF16)
    return _nn(hi, u) + _nn(lo, u)


def _sb_logits(z):
    lb = jnp.minimum(z, 0.0) - jnp.log(1.0 + jnp.exp(-jnp.abs(z)))
    return lb, lb - z


ATTN_STRIP = 32


def _by_strips(n_rows, fn):
    parts = None
    for r in range(0, n_rows, ATTN_STRIP):
        res = fn(slice(r, r + ATTN_STRIP))
        parts = [[v] for v in res] if parts is None else [p + [v] for p, v in zip(parts, res)]
    return [jnp.concatenate(p, axis=0) for p in parts]


def _attn_qkv_specs(tq, T):
    base = (IN_COLS - 3 * SB_WIDTH - QKV_OFF) // LANES
    nb = SB_WIDTH // LANES
    return [pl.BlockSpec((tq, LANES), lambda p, i: (i, base + p)),
            pl.BlockSpec((T, LANES), lambda p, i: (0, base + nb + p)),
            pl.BlockSpec((T, LANES), lambda p, i: (0, base + 2 * nb + p))]


class _Hosted:
    def __init__(self, ins, out_shapes, n_sems, copies):
        self.ins, self.out_shapes, self.n_sems, self.copies = ins, out_shapes, n_sems, copies

    @property
    def n(self):
        return len(self.ins)

    def sems(self):
        return [pltpu.SemaphoreType.DMA((self.n_sems,)), pltpu.SemaphoreType.DMA((self.n_sems,))]

    def start(self, src, dst, ssem, rsem):
        for send, _ in self.copies(src, dst, ssem, rsem):
            send.start()

    def wait(self, src, dst, ssem, rsem):
        for send, recv in self.copies(src, dst, ssem, rsem):
            recv.wait_recv()
            send.wait_send()


def _host(hosted, refs, n_in, n_out, first, last):
    if hosted is None:
        return refs, lambda: None, lambda: None
    n = hosted.n
    own_in, h_in = refs[:n_in], refs[n_in:n_in + n]
    own_out, h_out = refs[n_in + n:n_in + n + n_out], refs[n_in + n + n_out:n_in + 2 * n + n_out]
    rest = refs[n_in + 2 * n + n_out:]
    ssem, rsem = rest[-2:]

    def start():
        @pl.when(first)
        def _():
            hosted.start(h_in, h_out, ssem, rsem)

    def wait():
        @pl.when(last)
        def _():
            hosted.wait(h_in, h_out, ssem, rsem)

    return own_in + own_out + rest[:-2], start, wait


def _attn_fwd(qkv, hosted=None):
    T = qkv.shape[0]
    tq = _tile(T, ATTN_TILE)
    n_p, nq = SB_WIDTH // LANES, T // tq

    def body(*refs):
        p, i = pl.program_id(0), pl.program_id(1)
        (q_ref, k_ref, v_ref, o_ref), start, wait = _host(
            hosted, refs, 3, 1, jnp.logical_and(p == 0, i == 0), jnp.logical_and(p == n_p - 1, i == nq - 1))
        start()
        lane = lax.broadcasted_iota(jnp.int32, (tq, LANES), 1)
        row = lax.broadcasted_iota(jnp.int32, (tq, tq), 0)
        col = lax.broadcasted_iota(jnp.int32, (tq, tq), 1)
        after = jnp.where(row > col, 1.0, 0.0).astype(BF16)
        valid = col < row
        q = q_ref[...].astype(F32)
        qh = [jnp.where((lane // SB_HD) == hh, q * SB_SCALE, 0.0).astype(BF16) for hh in range(2)]

        def tiles(todo, state):
            chains = [(t, hh) for t in range(len(todo)) for hh in range(2)]
            kv = []
            for j, _ in todo:
                ks = pl.ds(pl.multiple_of(j * tq, tq), tq)
                kv.append((k_ref[ks, :], v_ref[ks, :]))
            z = {(t, hh): _nt(qh[hh], kv[t][0]) for t, hh in chains}
            lb, lmb, lm_sum = {}, {}, {}
            for t, hh in chains:
                def logits(rows, z=z[(t, hh)], mask=todo[t][1]):
                    lb, lm = _sb_logits(z[rows])
                    if mask is not None:
                        lm = jnp.where(mask[rows], lm, 0.0)
                    return lb, lm.astype(BF16), jnp.sum(lm, axis=1, keepdims=True)

                lb[(t, hh)], lmb[(t, hh)], lm_sum[(t, hh)] = _by_strips(tq, logits)
            x = {c: _nn(lmb[c], after) for c in chains}
            carry = [state[hh][0] for hh in range(2)]
            acc = [state[hh][1] for hh in range(2)]
            for t, hh in chains:
                def weights(rows, lb=lb[(t, hh)], x=x[(t, hh)], carry=carry[hh], mask=todo[t][1]):
                    a = jnp.exp(lb[rows] + x[rows] + carry[rows])
                    if mask is not None:
                        a = jnp.where(mask[rows], a, 0.0)
                    return (a.astype(BF16),)

                (ab,) = _by_strips(tq, weights)
                acc[hh] = acc[hh] + _nn(ab, kv[t][1])
                carry[hh] = carry[hh] + lm_sum[(t, hh)]
            return tuple((carry[hh], acc[hh]) for hh in range(2))

        def live(state):
            return jnp.maximum(jnp.max(state[0][0]), jnp.max(state[1][0]))

        zero = (jnp.zeros((tq, 1), F32), jnp.zeros((tq, LANES), F32))
        state = tiles([(i, valid), (jnp.maximum(i - 1, 0), jnp.broadcast_to(i > 0, (tq, tq)))], (zero, zero))

        def cond(st):
            return jnp.logical_and(st[0] >= 0, st[2] > UNDERFLOW)

        def step(st):
            state = tiles([(st[0], None)], st[1])
            return st[0] - 1, state, live(state)

        _, state, _ = lax.while_loop(cond, step, (i - 2, state, live(state)))
        o_ref[...] = jnp.where(lane < SB_HD, state[0][1], state[1][1])
        wait()

    h_ins = hosted.ins if hosted else []
    res = pl.pallas_call(
        body, name="attn_fwd_hosting" if hosted else "attn_fwd", grid=(n_p, nq),
        in_specs=_attn_qkv_specs(tq, T) + [HBM_SPEC] * len(h_ins),
        out_specs=[pl.BlockSpec((tq, LANES), lambda p, i: (i, p))] + [HBM_SPEC] * len(h_ins),
        out_shape=[jax.ShapeDtypeStruct((T, SB_WIDTH), F32)] + (hosted.out_shapes if hosted else []),
        scratch_shapes=hosted.sems() if hosted else [],
        compiler_params=_params(has_side_effects=hosted is not None),
    )(qkv, qkv, qkv, *h_ins)
    return res[0], res[1:]


def _attn_bwd(qkv, o, dymix, hosted=None):
    T = qkv.shape[0]
    tq = _tile(T, ATTN_TILE)
    n_p, nq = SB_WIDTH // LANES, T // tq
    yc_blk = (POOL_WIDTH + SG_WIDTH) // LANES

    def body(*refs):
        p, i = pl.program_id(0), pl.program_id(1)
        (q_ref, k_ref, v_ref, o_ref, do_ref, dq_ref, dk_ref, dv_ref), start, wait = _host(
            hosted, refs, 5, 3, jnp.logical_and(p == 0, i == 0), jnp.logical_and(p == n_p - 1, i == nq - 1))
        start()
        lane = lax.broadcasted_iota(jnp.int32, (tq, LANES), 1)
        row = lax.broadcasted_iota(jnp.int32, (tq, tq), 0)
        col = lax.broadcasted_iota(jnp.int32, (tq, tq), 1)
        after = jnp.where(row > col, 1.0, 0.0).astype(BF16)
        from_here = jnp.where(row >= col, 1.0, 0.0).astype(BF16)
        from_here2 = jnp.concatenate([from_here, from_here], axis=0)
        valid = col < row

        @pl.when(i == 0)
        def _():
            dk_ref[...] = jnp.zeros_like(dk_ref)
            dv_ref[...] = jnp.zeros_like(dv_ref)

        q = q_ref[...].astype(F32)
        ov = o_ref[...]
        dov = do_ref[...]
        heads = [(lane // SB_HD) == hh for hh in range(2)]
        qh = [jnp.where(h, q * SB_SCALE, 0.0).astype(BF16) for h in heads]
        dohb = [jnp.where(h, dov, 0.0).astype(BF16) for h in heads]
        delta = [jnp.sum(d.astype(F32) * ov, axis=1, keepdims=True) for d in dohb]

        def tiles(todo, state):
            chains = [(t, hh) for t in range(len(todo)) for hh in range(2)]
            kv, where = [], []
            for j, _ in todo:
                ks = pl.ds(pl.multiple_of(j * tq, tq), tq)
                where.append(ks)
                kv.append((k_ref[ks, :], v_ref[ks, :]))
            z = {(t, hh): _nt(qh[hh], kv[t][0]) for t, hh in chains}
            da = {(t, hh): _nt(dohb[hh], kv[t][1]) for t, hh in chains}
            lb, lmb, lm_sum = {}, {}, {}
            for t, hh in chains:
                def logits(rows, z=z[(t, hh)], mask=todo[t][1]):
                    lb, lm = _sb_logits(z[rows])
                    if mask is not None:
                        lm = jnp.where(mask[rows], lm, 0.0)
                    return lb, lm.astype(BF16), jnp.sum(lm, axis=1, keepdims=True)

                lb[(t, hh)], lmb[(t, hh)], lm_sum[(t, hh)] = _by_strips(tq, logits)
            x = {c: _nn(lmb[c], after) for c in chains}
            c_a = [state[hh][0] for hh in range(2)]
            ab, g, g_split, g_sum = {}, {}, {}, {}
            for t, hh in chains:
                def weights(rows, lb=lb[(t, hh)], x=x[(t, hh)], da=da[(t, hh)], c_a=c_a[hh], mask=todo[t][1]):
                    a = jnp.exp(lb[rows] + x[rows] + c_a[rows])
                    if mask is not None:
                        a = jnp.where(mask[rows], a, 0.0)
                    ab = a.astype(BF16)
                    g = da[rows] * ab.astype(F32)
                    hi = g.astype(BF16)
                    lo = (g - hi.astype(F32)).astype(BF16)
                    return ab, g, jnp.concatenate([hi, lo], axis=1), jnp.sum(g, axis=1, keepdims=True)

                ab[(t, hh)], g[(t, hh)], g_split[(t, hh)], g_sum[(t, hh)] = _by_strips(tq, weights)
                c_a[hh] = c_a[hh] + lm_sum[(t, hh)]
            right = {c: _nn(g_split[c], from_here2) for c in chains}
            c_r = [state[hh][1] for hh in range(2)]
            dzb = {}
            for t, hh in chains:
                def logit_grads(rows, lb=lb[(t, hh)], g=g[(t, hh)], right=right[(t, hh)], c_r=c_r[hh], hh=hh,
                                mask=todo[t][1]):
                    sig = jnp.exp(lb[rows])
                    left = delta[hh][rows] - (c_r[rows] + right[rows])
                    dz = g[rows] * (1.0 - sig) - left * sig
                    if mask is not None:
                        dz = jnp.where(mask[rows], dz, 0.0)
                    return (dz.astype(BF16),)

                (dzb[(t, hh)],) = _by_strips(tq, logit_grads)
                c_r[hh] = c_r[hh] + g_sum[(t, hh)]
            dqa = [state[hh][2] for hh in range(2)]
            for t in range(len(todo)):
                dk_ref[where[t], :] += _tn(dzb[(t, 0)], qh[0]) + _tn(dzb[(t, 1)], qh[1])
                dv_ref[where[t], :] += _tn(ab[(t, 0)], dohb[0]) + _tn(ab[(t, 1)], dohb[1])
                for hh in range(2):
                    dqa[hh] = dqa[hh] + _nn(dzb[(t, hh)], kv[t][0])
            return tuple((c_a[hh], c_r[hh], dqa[hh]) for hh in range(2))

        def live(state):
            return jnp.maximum(jnp.max(state[0][0]), jnp.max(state[1][0]))

        zero = (jnp.zeros((tq, 1), F32), jnp.zeros((tq, 1), F32), jnp.zeros((tq, LANES), F32))
        state = tiles([(i, valid), (jnp.maximum(i - 1, 0), jnp.broadcast_to(i > 0, (tq, tq)))], (zero, zero))

        def cond(st):
            return jnp.logical_and(st[0] >= 0, st[2] > UNDERFLOW)

        def step(st):
            state = tiles([(st[0], None)], st[1])
            return st[0] - 1, state, live(state)

        _, state, _ = lax.while_loop(cond, step, (i - 2, state, live(state)))
        dq_ref[...] = jnp.where(lane < SB_HD, state[0][2], state[1][2]) * SB_SCALE
        wait()

    h_ins = hosted.ins if hosted else []
    res = pl.pallas_call(
        body, name="attn_bwd_hosting" if hosted else "attn_bwd", grid=(n_p, nq),
        in_specs=_attn_qkv_specs(tq, T) + [pl.BlockSpec((tq, LANES), lambda p, i: (i, p)),
                                           pl.BlockSpec((tq, LANES), lambda p, i: (i, yc_blk + p))]
        + [HBM_SPEC] * len(h_ins),
        out_specs=[pl.BlockSpec((tq, LANES), lambda p, i: (i, p)), pl.BlockSpec((T, LANES), lambda p, i: (0, p)),
                   pl.BlockSpec((T, LANES), lambda p, i: (0, p))] + [HBM_SPEC] * len(h_ins),
        out_shape=[jax.ShapeDtypeStruct((T, SB_WIDTH), F32)] * 3 + (hosted.out_shapes if hosted else []),
        scratch_shapes=hosted.sems() if hosted else [],
        compiler_params=_params(has_side_effects=hosted is not None),
    )(qkv, qkv, qkv, o, dymix, *h_ins)
    return res[:3], res[3:]


def _outproj_fwd(x, ya, yb, yc, w):
    T, D = x.shape
    tt = _tile(T, 512)

    def body(x_ref, ya_ref, yb_ref, yc_ref, w_ref, x1_ref, ymix_ref):
        ymix_ref[:, 0:POOL_WIDTH] = ya_ref[...].astype(BF16)
        ymix_ref[:, POOL_WIDTH:POOL_WIDTH + SG_WIDTH] = yb_ref[...].astype(BF16)
        ymix_ref[:, POOL_WIDTH + SG_WIDTH:] = yc_ref[...].astype(BF16)
        x1_ref[...] = x_ref[...] + _nn(ymix_ref[...], w_ref[...])

    row = lambda width: pl.BlockSpec((tt, width), lambda i: (i, 0))
    return pl.pallas_call(
        body, name="outproj_fwd", grid=(T // tt,),
        in_specs=[row(D), row(POOL_WIDTH), row(SG_WIDTH), row(SB_WIDTH), pl.BlockSpec((D, D), lambda i: (0, 0))],
        out_specs=[row(D), row(D)],
        out_shape=[jax.ShapeDtypeStruct((T, D), F32), jax.ShapeDtypeStruct((T, D), BF16)],
        compiler_params=_params(),
    )(x, ya, yb, yc, w)


def _nt_matmul(a, w):
    T, N = a.shape
    K = w.shape[0]
    tt = _tile(T, 512)

    def body(a_ref, w_ref, o_ref):
        o_ref[...] = _nt(a_ref[...].astype(BF16), w_ref[...])

    return pl.pallas_call(
        body, name="nt_matmul", grid=(T // tt,),
        in_specs=[pl.BlockSpec((tt, N), lambda i: (i, 0)), pl.BlockSpec((K, N), lambda i: (0, 0))],
        out_specs=pl.BlockSpec((tt, K), lambda i: (i, 0)),
        out_shape=jax.ShapeDtypeStruct((T, K), F32),
        compiler_params=_params(),
    )(a, w)


def _tn_matmul(a, b, name, n_split=1):
    T, K = a.shape
    N = b.shape[1]
    tk = _tile(K, 1024)
    tn = _tile(N // n_split, 1024)
    tt = _tile(T, 2048)
    nper = N // n_split // tn
    nt = T // tt

    def body(a_ref, b_ref, o_ref):
        @pl.when(pl.program_id(2) == 0)
        def _():
            o_ref[...] = jnp.zeros_like(o_ref)

        o_ref[...] += _tn(a_ref[...], b_ref[...].astype(BF16))

    return pl.pallas_call(
        body, name=name, grid=(K // tk, N // tn, nt),
        in_specs=[pl.BlockSpec((tt, tk), lambda k, n, t: (t, k)), pl.BlockSpec((tt, tn), lambda k, n, t: (t, n))],
        out_specs=pl.BlockSpec((None, tk, tn), lambda k, n, t: (n // nper, k, n % nper)),
        out_shape=jax.ShapeDtypeStruct((n_split, K, N // n_split), F32),
        compiler_params=_params(),
    )(a, b)


def _mlp_fwd(x, g, w_up, w_down, hosted=None):
    T, D = x.shape
    F = w_up.shape[1]
    tt = _tile(T, 1024)
    fc = _tile(F, 512)
    nc = F // fc
    nt = T // tt

    def body(*refs):
        i, c = pl.program_id(0), pl.program_id(1)
        (x_ref, g_ref, wu_ref, wd_ref, y_ref, h_ref, u_ref, a_ref), start, wait = _host(
            hosted, refs, 4, 4, jnp.logical_and(i == 0, c == 0), jnp.logical_and(i == nt - 1, c == nc - 1))
        start()

        @pl.when(c == 0)
        def _():
            xv = x_ref[...]
            h, _, _ = _rms_fwd(xv, g_ref[...])
            h_ref[...] = h.astype(BF16)
            y_ref[...] = xv

        u = _nn(h_ref[...], wu_ref[...])
        u_ref[...] = u.astype(BF16)
        a = jnp.square(jnp.maximum(u, 0.0)).astype(BF16)
        a_ref[...] = a
        y_ref[...] += _nn(a, wd_ref[...])
        wait()

    h_ins = hosted.ins if hosted else []
    res = pl.pallas_call(
        body, name="mlp_fwd_hosting" if hosted else "mlp_fwd", grid=(nt, nc),
        in_specs=[pl.BlockSpec((tt, D), lambda i, c: (i, 0)), pl.BlockSpec((1, D), lambda i, c: (0, 0)),
                  pl.BlockSpec((D, fc), lambda i, c: (0, c)), pl.BlockSpec((fc, D), lambda i, c: (c, 0))]
        + [HBM_SPEC] * len(h_ins),
        out_specs=[pl.BlockSpec((tt, D), lambda i, c: (i, 0)), pl.BlockSpec((tt, D), lambda i, c: (i, 0)),
                   pl.BlockSpec((tt, fc), lambda i, c: (i, c)), pl.BlockSpec((tt, fc), lambda i, c: (i, c))]
        + [HBM_SPEC] * len(h_ins),
        out_shape=[jax.ShapeDtypeStruct((T, D), F32), jax.ShapeDtypeStruct((T, D), BF16),
                   jax.ShapeDtypeStruct((T, F), BF16), jax.ShapeDtypeStruct((T, F), BF16)]
        + (hosted.out_shapes if hosted else []),
        scratch_shapes=hosted.sems() if hosted else [],
        compiler_params=_params(has_side_effects=hosted is not None),
    )(x, g, w_up, w_down, *h_ins)
    return res[:4], res[4:]


def _mlp_bwd(dy, x, g, u, w_up, w_down, hosted=None):
    T, D = x.shape
    F = w_up.shape[1]
    tt = _tile(T, 1024)
    fc = _tile(F, 512)
    nc = F // fc
    nt = T // tt

    def body(*refs):
        i, c = pl.program_id(0), pl.program_id(1)
        (dy_ref, x_ref, g_ref, u_ref, wu_ref, wd_ref, dx_ref, du_ref, dg_ref, dyb_ref, dh_ref), start, wait = _host(
            hosted, refs, 6, 3, jnp.logical_and(i == 0, c == 0), jnp.logical_and(i == nt - 1, c == nc - 1))
        start()

        @pl.when(c == 0)
        def _():
            dyb_ref[...] = dy_ref[...].astype(BF16)
            dh_ref[...] = jnp.zeros_like(dh_ref)

        @pl.when(jnp.logical_and(i == 0, c == 0))
        def _():
            dg_ref[...] = jnp.zeros_like(dg_ref)

        da = _nt(dyb_ref[...], wd_ref[...])
        du = (da * (2.0 * jnp.maximum(u_ref[...].astype(F32), 0.0))).astype(BF16)
        du_ref[...] = du
        dh_ref[...] += _nt(du, wu_ref[...])

        @pl.when(c == nc - 1)
        def _():
            gv = g_ref[...]
            _, xhat, r = _rms_fwd(x_ref[...], gv)
            dx, dgrow = _rms_bwd(dh_ref[...], xhat, r, gv)
            dx_ref[...] = dy_ref[...] + dx
            dg_ref[...] += jnp.sum(dgrow, axis=0, keepdims=True)

        wait()

    h_ins = hosted.ins if hosted else []
    res = pl.pallas_call(
        body, name="mlp_bwd_hosting" if hosted else "mlp_bwd", grid=(nt, nc),
        in_specs=[pl.BlockSpec((tt, D), lambda i, c: (i, 0)), pl.BlockSpec((tt, D), lambda i, c: (i, 0)),
                  pl.BlockSpec((1, D), lambda i, c: (0, 0)), pl.BlockSpec((tt, fc), lambda i, c: (i, c)),
                  pl.BlockSpec((D, fc), lambda i, c: (0, c)), pl.BlockSpec((fc, D), lambda i, c: (c, 0))]
        + [HBM_SPEC] * len(h_ins),
        out_specs=[pl.BlockSpec((tt, D), lambda i, c: (i, 0)), pl.BlockSpec((tt, fc), lambda i, c: (i, c)),
                   pl.BlockSpec((1, D), lambda i, c: (0, 0))] + [HBM_SPEC] * len(h_ins),
        out_shape=[jax.ShapeDtypeStruct((T, D), F32), jax.ShapeDtypeStruct((T, F), BF16),
                   jax.ShapeDtypeStruct((1, D), F32)] + (hosted.out_shapes if hosted else []),
        scratch_shapes=[pltpu.VMEM((tt, D), BF16), pltpu.VMEM((tt, D), F32)] + (hosted.sems() if hosted else []),
        compiler_params=_params(has_side_effects=hosted is not None),
    )(dy, x, g, u, w_up, w_down, *h_ins)
    return res[:3], res[3:]


def _loss_head(x, g, target):
    T, D = x.shape
    tt = _tile(T, 512)

    def body(x_ref, g_ref, t_ref, loss_ref, dx_ref, dg_ref):
        gv = g_ref[...]
        y, xhat, r = _rms_fwd(x_ref[...], gv)
        err = y - t_ref[...]
        dx, dgrow = _rms_bwd(err * (1.0 / D), xhat, r, gv)
        dx_ref[...] = dx

        @pl.when(pl.program_id(0) == 0)
        def _():
            loss_ref[...] = jnp.zeros_like(loss_ref)
            dg_ref[...] = jnp.zeros_like(dg_ref)

        loss_ref[...] += 0.5 * jnp.sum(jnp.mean(err * err, axis=-1, keepdims=True), axis=0, keepdims=True)
        dg_ref[...] += jnp.sum(dgrow, axis=0, keepdims=True)

    return pl.pallas_call(
        body, name="loss_head", grid=(T // tt,),
        in_specs=[pl.BlockSpec((tt, D), lambda i: (i, 0)), pl.BlockSpec((1, D), lambda i: (0, 0)),
                  pl.BlockSpec((tt, D), lambda i: (i, 0))],
        out_specs=[pl.BlockSpec((1, LANES), lambda i: (0, 0)), pl.BlockSpec((tt, D), lambda i: (i, 0)),
                   pl.BlockSpec((1, D), lambda i: (0, 0))],
        out_shape=[jax.ShapeDtypeStruct((1, LANES), F32), jax.ShapeDtypeStruct((T, D), F32),
                   jax.ShapeDtypeStruct((1, D), F32)],
        compiler_params=_params(),
    )(x, g, target)


def _rows(shape, pref=512):
    last = shape[-1]
    rows = 1
    for s in shape[:-1]:
        rows *= s
    tr = rows
    if rows * last > 256 * 1024:
        for cand in (pref, 256, 128, 64, 32, 16, 8):
            if rows % cand == 0:
                tr = cand
                break
    return rows, last, tr


def _elementwise(fn, name, ins, n_out, out_dtype=F32):
    shape = ins[0].shape
    rows, last, tr = _rows(shape)
    flat = [a.reshape(rows, last) for a in ins]
    n_in = len(ins)

    def body(*refs):
        res = fn(*[r[...] for r in refs[:n_in]])
        if n_out == 1:
            res = (res,)
        for r, v in zip(refs[n_in:], res):
            r[...] = v.astype(r.dtype)

    spec = pl.BlockSpec((tr, last), lambda i: (i, 0))
    outs = pl.pallas_call(
        body, name=name, grid=(rows // tr,),
        in_specs=[spec] * n_in, out_specs=[spec] * n_out,
        out_shape=[jax.ShapeDtypeStruct((rows, last), out_dtype)] * n_out,
        compiler_params=_params(),
    )(*flat)
    return [o.reshape(shape) for o in outs]


def _add_pair(g, o, c_idx):
    nq, R, C = g.shape
    h = R // 2
    tr = _tile(h, 512)
    nb = h // tr

    def body(c_ref, g_ref, o_ref, out_ref):
        out_ref[...] = g_ref[...] + o_ref[...]

    return pl.pallas_call(
        body, name="add_pair",
        grid_spec=pltpu.PrefetchScalarGridSpec(
            num_scalar_prefetch=1, grid=(nq, nb),
            in_specs=[pl.BlockSpec((None, tr, C), lambda q, i, c: (q, c[0] * nb + i, 0)),
                      pl.BlockSpec((None, tr, C), lambda q, i, c: (q, i, 0))],
            out_specs=pl.BlockSpec((None, tr, C), lambda q, i, c: (q, i, 0))),
        out_shape=jax.ShapeDtypeStruct((nq, h, C), F32),
        compiler_params=_params(),
    )(c_idx.astype(jnp.int32).reshape(1), g, o)


def _add_chips(p, r, q_idx):
    _, H, C = p.shape
    tr = _tile(H, 512)

    def body(q_ref, p_ref, r0_ref, r1_ref, r2_ref, out_ref):
        out_ref[...] = (p_ref[...] + r0_ref[...]) + (r1_ref[...] + r2_ref[...])

    def arrived(k):
        return pl.BlockSpec((None, tr, C), lambda i, q: (k, i, 0))

    return pl.pallas_call(
        body, name="add_chips",
        grid_spec=pltpu.PrefetchScalarGridSpec(
            num_scalar_prefetch=1, grid=(H // tr,),
            in_specs=[pl.BlockSpec((None, tr, C), lambda i, q: (q[0], i, 0)), arrived(0), arrived(1), arrived(2)],
            out_specs=pl.BlockSpec((tr, C), lambda i, q: (i, 0))),
        out_shape=jax.ShapeDtypeStruct((H, C), F32),
        compiler_params=_params(),
    )(q_idx.astype(jnp.int32).reshape(1), p, r, r, r)


def _adamw(w, g, m, v):
    m = ADAM_B1 * m + (1.0 - ADAM_B1) * g
    v = ADAM_B2 * v + (1.0 - ADAM_B2) * jnp.square(g)
    m_hat = m / (1.0 - ADAM_B1 ** ADAM_STEP)
    v_hat = v / (1.0 - ADAM_B2 ** ADAM_STEP)
    delta = -ADAM_LR * (m_hat / (jnp.sqrt(v_hat) + ADAM_EPS) + ADAM_WD * w)
    return delta, m, v


def _place():
    x, y, c = lax.axis_index("x"), lax.axis_index("y"), lax.axis_index("c")
    chips = [(1 - x, y), (x, 1 - y), (1 - x, 1 - y)]
    return x, y, c, chips


def _remote(src, dst, ssem, rsem, k, dev):
    return pltpu.make_async_remote_copy(src_ref=src, dst_ref=dst, send_sem=ssem.at[k], recv_sem=rsem.at[k],
                                        device_id=dev, device_id_type=MESH)


def _gather_weights(shards):
    n = len(shards)
    halves = [s.shape[1] // 2 for s in shards]

    def body(*refs):
        src, out = refs[:n], refs[n:2 * n]
        ssem, rsem = refs[2 * n:]
        x, y, c, chips = _place()
        me_q = 2 * x + y
        sib = (x, y, 1 - c)

        def half(a, q, cc):
            return out[a].at[q, :, pl.ds(cc * halves[a], halves[a]), :]

        first = []
        for a in range(n):
            mine = src[a].at[:, pl.ds(c * halves[a], halves[a]), :]
            for r, chip in enumerate(chips):
                first.append(_remote(mine, half(a, me_q, c), ssem, rsem, a * 3 + r, (*chip, c)))
        for cp in first:
            cp.start()
        passed = []
        for a in range(n):
            for r, chip in enumerate(chips):
                q = 2 * chip[0] + chip[1]
                k = a * 3 + r
                _remote(half(a, q, c), half(a, q, c), ssem, rsem, k, (*chip, c)).wait_recv()
                cp = _remote(half(a, q, c), half(a, q, c), ssem, rsem, 3 * n + k, sib)
                cp.start()
                passed.append(cp)
        for a in range(n):
            for r, chip in enumerate(chips):
                q = 2 * chip[0] + chip[1]
                _remote(half(a, q, 1 - c), half(a, q, 1 - c), ssem, rsem, 3 * n + a * 3 + r, sib).wait_recv()
        for cp in first + passed:
            cp.wait_send()

    return pl.pallas_call(
        body, name="gather_weights",
        in_specs=[HBM_SPEC] * n, out_specs=[HBM_SPEC] * n,
        out_shape=[jax.ShapeDtypeStruct((N_CHIPS,) + s.shape, s.dtype) for s in shards],
        scratch_shapes=[pltpu.SemaphoreType.DMA((6 * n,)), pltpu.SemaphoreType.DMA((6 * n,))],
        compiler_params=_params(has_side_effects=True),
    )(*shards)


def _gather_over_ici(shards):
    n = len(shards)
    halves = [s.shape[1] // 2 for s in shards]

    def copies(src, out, ssem, rsem):
        x, y, c, chips = _place()
        me_q = 2 * x + y
        res = []
        for a in range(n):
            rows = pl.ds(c * halves[a], halves[a])
            mine = src[a].at[:, rows, :]
            for r, chip in enumerate(chips):
                dev = (*chip, c)
                res.append((_remote(mine, out[a].at[me_q, :, rows, :], ssem, rsem, a * 3 + r, dev),
                            _remote(mine, out[a].at[2 * chip[0] + chip[1], :, rows, :], ssem, rsem, a * 3 + r, dev)))
        return res

    return _Hosted(list(shards), [jax.ShapeDtypeStruct((N_CHIPS,) + s.shape, s.dtype) for s in shards], 3 * n, copies)


def _pass_to_sibling(gathered):
    n = len(gathered)
    halves = [g.shape[2] // 2 for g in gathered]

    def body(*refs):
        out = refs[n:2 * n]
        ssem, rsem = refs[2 * n:]
        x, y, c, chips = _place()
        sib = (x, y, 1 - c)

        def half(a, q, cc):
            return out[a].at[q, :, pl.ds(cc * halves[a], halves[a]), :]

        cps = []
        for a in range(n):
            for r, chip in enumerate(chips):
                q = 2 * chip[0] + chip[1]
                cps.append(_remote(half(a, q, c), half(a, q, c), ssem, rsem, a * 3 + r, sib))
        for cp in cps:
            cp.start()
        for a in range(n):
            for r, chip in enumerate(chips):
                q = 2 * chip[0] + chip[1]
                _remote(half(a, q, 1 - c), half(a, q, 1 - c), ssem, rsem, a * 3 + r, sib).wait_recv()
        for cp in cps:
            cp.wait_send()

    return pl.pallas_call(
        body, name="pass_to_sibling",
        in_specs=[HBM_SPEC] * n, out_specs=[HBM_SPEC] * n,
        out_shape=[jax.ShapeDtypeStruct(g.shape, g.dtype) for g in gathered],
        input_output_aliases={a: a for a in range(n)},
        scratch_shapes=[pltpu.SemaphoreType.DMA((3 * n,)), pltpu.SemaphoreType.DMA((3 * n,))],
        compiler_params=_params(has_side_effects=True),
    )(*gathered)


def _scatter_over_ici(parts):
    n = len(parts)

    def copies(src, out, ssem, rsem):
        x, y, c, chips = _place()
        res = []
        for a in range(n):
            for r, chip in enumerate(chips):
                cp = _remote(src[a].at[2 * chip[0] + chip[1]], out[a].at[r], ssem, rsem, a * 3 + r, (*chip, c))
                res.append((cp, cp))
        return res

    return _Hosted(list(parts), [jax.ShapeDtypeStruct((3,) + p.shape[1:], F32) for p in parts], 3 * n, copies)


def _swap_over_d2d(grads):
    n = len(grads)
    halves = [g.shape[1] // 2 for g in grads]

    def copies(src, out, ssem, rsem):
        x, y, c, _ = _place()
        res = []
        for a in range(n):
            cp = _remote(src[a].at[:, pl.ds((1 - c) * halves[a], halves[a]), :], out[a], ssem, rsem, a, (x, y, 1 - c))
            res.append((cp, cp))
        return res

    return _Hosted(list(grads), [jax.ShapeDtypeStruct((N_CHIPS, h, g.shape[2]), F32) for g, h in zip(grads, halves)],
                   n, copies)


def _swap_halves(grads):
    n = len(grads)
    halves = [g.shape[1] // 2 for g in grads]

    def body(*refs):
        src, out = refs[:n], refs[n:2 * n]
        ssem, rsem = refs[2 * n:]
        x, y, c, _ = _place()
        cps = [_remote(src[a].at[:, pl.ds((1 - c) * halves[a], halves[a]), :], out[a], ssem, rsem, a, (x, y, 1 - c))
               for a in range(n)]
        for cp in cps:
            cp.start()
        for cp in cps:
            cp.wait()

    return pl.pallas_call(
        body, name="swap_halves",
        in_specs=[HBM_SPEC] * n, out_specs=[HBM_SPEC] * n,
        out_shape=[jax.ShapeDtypeStruct((N_CHIPS, h, g.shape[2]), F32) for g, h in zip(grads, halves)],
        scratch_shapes=[pltpu.SemaphoreType.DMA((n,)), pltpu.SemaphoreType.DMA((n,))],
        compiler_params=_params(has_side_effects=True),
    )(*grads)


def _scatter_chips(parts):
    n = len(parts)

    def body(*refs):
        src, out = refs[:n], refs[n:2 * n]
        ssem, rsem = refs[2 * n:]
        x, y, c, chips = _place()
        cps = []
        for a in range(n):
            for r, chip in enumerate(chips):
                cps.append(_remote(src[a].at[2 * chip[0] + chip[1]], out[a].at[r], ssem, rsem, a * 3 + r, (*chip, c)))
        for cp in cps:
            cp.start()
        for cp in cps:
            cp.wait()

    return pl.pallas_call(
        body, name="scatter_chips",
        in_specs=[HBM_SPEC] * n, out_specs=[HBM_SPEC] * n,
        out_shape=[jax.ShapeDtypeStruct((3,) + p.shape[1:], F32) for p in parts],
        scratch_shapes=[pltpu.SemaphoreType.DMA((3 * n,)), pltpu.SemaphoreType.DMA((3 * n,))],
        compiler_params=_params(has_side_effects=True),
    )(*parts)


def _swap_reduced(reduced):
    n = len(reduced)

    def body(*refs):
        src, out = refs[:n], refs[n:2 * n]
        ssem, rsem = refs[2 * n:]
        x, y, c, _ = _place()
        cps = [_remote(src[a], out[a], ssem, rsem, a, (x, y, 1 - c)) for a in range(n)]
        for cp in cps:
            cp.start()
        for cp in cps:
            cp.wait()

    return pl.pallas_call(
        body, name="swap_reduced",
        in_specs=[HBM_SPEC] * n, out_specs=[HBM_SPEC] * n,
        out_shape=[jax.ShapeDtypeStruct(r.shape, F32) for r in reduced],
        scratch_shapes=[pltpu.SemaphoreType.DMA((n,)), pltpu.SemaphoreType.DMA((n,))],
        compiler_params=_params(has_side_effects=True),
    )(*reduced)


def _allreduce_small(buf):
    R, L = buf.shape

    def body(buf_ref, out_ref, pair_ref, chip_ref, ssem, rsem):
        x, y, c, chips = _place()
        me_q = 2 * x + y
        pair_ref[c] = buf_ref[...]
        to_sib = _remote(buf_ref, pair_ref.at[c], ssem, rsem, 0, (x, y, 1 - c))
        to_sib.start()
        _remote(buf_ref, pair_ref.at[1 - c], ssem, rsem, 0, (x, y, 1 - c)).wait_recv()
        chip_ref[me_q] = pair_ref[0] + pair_ref[1]
        cps = [_remote(chip_ref.at[me_q], chip_ref.at[me_q], ssem, rsem, 1 + r, (*chip, c))
               for r, chip in enumerate(chips)]
        for cp in cps:
            cp.start()
        for r, chip in enumerate(chips):
            q = 2 * chip[0] + chip[1]
            _remote(chip_ref.at[q], chip_ref.at[q], ssem, rsem, 1 + r, (*chip, c)).wait_recv()
        out_ref[...] = (chip_ref[0] + chip_ref[1]) + (chip_ref[2] + chip_ref[3])
        to_sib.wait_send()
        for cp in cps:
            cp.wait_send()

    return pl.pallas_call(
        body, name="allreduce_small",
        in_specs=[VMEM_SPEC], out_specs=VMEM_SPEC,
        out_shape=jax.ShapeDtypeStruct((R, L), F32),
        scratch_shapes=[pltpu.VMEM((2, R, L), F32), pltpu.VMEM((N_CHIPS, R, L), F32),
                        pltpu.SemaphoreType.DMA((4,)), pltpu.SemaphoreType.DMA((4,))],
        compiler_params=_params(has_side_effects=True),
    )(buf)


def _pack(arrays):
    flat = jnp.concatenate([a.reshape(-1) for a in arrays])
    pad = (-flat.shape[0]) % (8 * LANES)
    return jnp.pad(flat, (0, pad)).reshape(-1, LANES)


def _unpack(buf, like):
    flat = buf.reshape(-1)
    out, off = [], 0
    for a in like:
        out.append(flat[off:off + a.size].reshape(a.shape))
        off += a.size
    return out


def _block_diag(pw):
    rows = []
    for gi in range(len(POOL_WINDOWS)):
        blocks = [pw[gi] if gj == gi else jnp.zeros_like(pw[gi]) for gj in range(len(POOL_WINDOWS))]
        rows.append(jnp.concatenate(blocks, axis=1))
    return jnp.concatenate(rows, axis=0)


def kernel(x, norm1, w_in, pool_w, pool_scale, sg_norm, sg_w, sg_b, w_out, norm2, w_up, w_down, final_norm, loss_target, m_norm1, m_w_in, m_pool_w, m_pool_scale, m_sg_norm, m_sg_w, m_sg_b, m_w_out, m_norm2, m_w_up, m_w_down, m_final_norm, v_norm1, v_w_in, v_pool_w, v_pool_scale, v_sg_norm, v_sg_w, v_sg_b, v_w_out, v_norm2, v_w_up, v_w_down, v_final_norm):
    depth = norm1.shape[0]
    T = x.shape[1]
    xs = x.reshape(T, D_MODEL)
    target = loss_target.reshape(T, D_MODEL)

    assert depth == 2
    c_idx = lax.axis_index("c")
    q_idx = 2 * lax.axis_index("x") + lax.axis_index("y")
    own = [w.astype(BF16) for w in (w_in, w_out, w_up, w_down)]
    gathered = {(0, 0): _gather_weights([own[0][:1]])[0]}

    def full(a, l, axis):
        blocks = [jnp.where(q_idx == q, own[a][l], gathered[(a, l)][q, 0]) for q in range(N_CHIPS)]
        return jnp.concatenate(blocks, axis=axis)

    def gather_behind(call, keys):
        res, over_ici = call(_gather_over_ici([own[a][l:l + 1] for a, l in keys]))
        gathered.update(zip(keys, _pass_to_sibling(over_ici)))
        return res

    tril = jnp.tril(jnp.ones((CHUNK, CHUNK), F32))
    saved = []
    cur = xs
    wi, wo, wu, wd = {}, {}, {}, {}
    for l in range(depth):
        wbd = _block_diag(pool_w[l]).astype(BF16)
        wm = sg_w[l] * tril
        wm_s = wm.reshape(SG_HEADS * CHUNK, CHUNK).astype(BF16)
        wmt_s = jnp.swapaxes(wm, 1, 2).reshape(SG_HEADS * CHUNK, CHUNK).astype(BF16)
        bias = jnp.repeat(sg_b[l].T, SB_HD, axis=1)
        n1, n2 = norm1[l][None], norm2[l][None]
        psc, sgn = pool_scale[l][None], sg_norm[l][None]
        wi[l] = full(0, l, 1)
        proj, h, qkv = _inproj_fwd(cur, n1, wi[l])
        ya = _pool_fwd(proj, wbd, psc)
        yb = _sg_fwd(proj, wm_s, bias, sgn)
        if l == 0:
            yc = gather_behind(lambda hosted: _attn_fwd(qkv, hosted), [(1, 0), (2, 0), (3, 0)])
        else:
            yc, _ = _attn_fwd(qkv)
        wo[l], wu[l], wd[l] = full(1, l, 0), full(2, l, 1), full(3, l, 0)
        x1, ymix = _outproj_fwd(cur, ya, yb, yc, wo[l])
        if l == 0:
            x2, h2, u, act = gather_behind(lambda hosted: _mlp_fwd(x1, n2, wu[l], wd[l], hosted),
                                           [(0, 1), (1, 1), (2, 1), (3, 1)])
        else:
            (x2, h2, u, act), _ = _mlp_fwd(x1, n2, wu[l], wd[l])
        saved.append(dict(x0=cur, x1=x1, proj=proj, h=h, qkv=qkv, yc=yc, ymix=ymix, h2=h2, u=u, act=act,
                          wbd=wbd, wm_s=wm_s, wmt_s=wmt_s, bias=bias, n1=n1, n2=n2, psc=psc, sgn=sgn))
        cur = x2

    loss_row, dcur, d_final = _loss_head(cur, final_norm[None], target)

    small = [None] * depth
    grads, parts, reduced = {}, {}, {}

    def pair_up(keys, swapped):
        parts.update({k: _add_pair(grads[k], o, c_idx) for k, o in zip(keys, swapped)})

    def chip_up(keys, arrived):
        reduced.update({k: _add_chips(parts[k], r, q_idx) for k, r in zip(keys, arrived)})

    for l in reversed(range(depth)):
        s = saved[l]
        if l == 0:
            keys = [(2, 1), (3, 1)]
            (dx1, du, d_n2), arrived = _mlp_bwd(dcur, s["x1"], s["n2"], s["u"], wu[l], wd[l],
                                                _scatter_over_ici([parts[k] for k in keys]))
            chip_up(keys, arrived)
        else:
            (dx1, du, d_n2), _ = _mlp_bwd(dcur, s["x1"], s["n2"], s["u"], wu[l], wd[l])
        grads[(2, l)] = _tn_matmul(s["h2"], du, "grad_w_up", n_split=N_CHIPS)
        grads[(3, l)] = _tn_matmul(s["act"], dcur, "grad_w_down")[0].reshape(N_CHIPS, D_FF // N_CHIPS, D_MODEL)
        dymix = _nt_matmul(dx1, wo[l])
        grads[(1, l)] = _tn_matmul(s["ymix"], dx1, "grad_w_out")[0].reshape(N_CHIPS, D_MODEL // N_CHIPS, D_MODEL)
        da_in, d_wbd, d_psc = _pool_bwd(s["proj"], dymix, s["wbd"], s["psc"])
        if l == 0:
            keys = [(0, 1), (1, 0), (2, 0), (3, 0)]
            (du_pre, dv_pre, d_wm, d_bias, d_sgn), swapped = _sg_bwd(
                s["proj"], dymix, s["wm_s"], s["wmt_s"], s["bias"], s["sgn"], _swap_over_d2d([grads[k] for k in keys]))
            pair_up(keys, swapped)
            keys = [(1, 1)] + keys
            (dq, dk, dv), arrived = _attn_bwd(s["qkv"], s["yc"], dymix, _scatter_over_ici([parts[k] for k in keys]))
            chip_up(keys, arrived)
        else:
            (du_pre, dv_pre, d_wm, d_bias, d_sgn), _ = _sg_bwd(s["proj"], dymix, s["wm_s"], s["wmt_s"], s["bias"], s["sgn"])
            keys = [(1, l), (2, l), (3, l)]
            (dq, dk, dv), swapped = _attn_bwd(s["qkv"], s["yc"], dymix, _swap_over_d2d([grads[k] for k in keys]))
            pair_up(keys, swapped)
        dproj, dx0, d_n1 = _inproj_bwd([da_in, du_pre, dv_pre, dq, dk, dv], wi[l], s["x0"], s["n1"], dx1)
        g_in_l = _tn_matmul(s["h"], dproj, "grad_w_in")[0]
        grads[(0, l)] = g_in_l.reshape(D_MODEL, N_CHIPS, IN_COLS // N_CHIPS).transpose(1, 0, 2)
        d_pw = jnp.stack([d_wbd[gi * POOL_GW:(gi + 1) * POOL_GW, gi * POOL_GW:(gi + 1) * POOL_GW]
                          for gi in range(len(POOL_WINDOWS))])
        small[l] = dict(norm1=d_n1[0], pool_w=d_pw, pool_scale=d_psc[0], sg_norm=d_sgn[0],
                        sg_w=d_wm.reshape(SG_HEADS, CHUNK, CHUNK), sg_b=d_bias[:, :SG_HEADS].T, norm2=d_n2[0])
        dcur = dx0
    grad_x = dcur.reshape(x.shape)

    keys = [(0, 0)]
    pair_up(keys, _swap_halves([grads[k] for k in keys]))
    chip_up(keys, _scatter_chips([parts[k] for k in keys]))
    keys = sorted(reduced)
    theirs = dict(zip(keys, _swap_reduced([reduced[k] for k in keys])))

    def joined(a):
        layers = []
        for l in range(depth):
            mine, other = reduced[(a, l)], theirs[(a, l)]
            layers.append(jnp.where(c_idx == 0, jnp.concatenate([mine, other]), jnp.concatenate([other, mine])))
        return jnp.stack(layers)

    gw_in, gw_out, gw_up, gw_down = [joined(a) for a in range(4)]

    names = ["norm1", "pool_w", "pool_scale", "sg_norm", "sg_w", "sg_b", "norm2"]
    slot = jnp.zeros((1,), F32)
    small_w = [norm1, pool_w, pool_scale, sg_norm, sg_w, sg_b, norm2, final_norm, slot]
    small_m = [m_norm1, m_pool_w, m_pool_scale, m_sg_norm, m_sg_w, m_sg_b, m_norm2, m_final_norm, slot]
    small_v = [v_norm1, v_pool_w, v_pool_scale, v_sg_norm, v_sg_w, v_sg_b, v_norm2, v_final_norm, slot]
    small_g = [jnp.stack([small[l][k] for l in range(depth)]) for k in names] + [d_final[0], loss_row[0, :1]]
    g_packed = _allreduce_small(_pack(small_g))
    loss = _unpack(g_packed, small_w)[-1][0]
    s_delta, s_m, s_v = _elementwise(_adamw, "adamw_small", [_pack(small_w), g_packed, _pack(small_m), _pack(small_v)], 3)
    gs = dict(zip(names + ["final_norm"], _unpack(g_packed, small_w)))
    ds = dict(zip(names + ["final_norm"], _unpack(s_delta, small_w)))
    ms = dict(zip(names + ["final_norm"], _unpack(s_m, small_w)))
    vs = dict(zip(names + ["final_norm"], _unpack(s_v, small_w)))

    big_g = dict(w_in=gw_in, w_out=gw_out, w_up=gw_up, w_down=gw_down)
    big_w = dict(w_in=(w_in, m_w_in, v_w_in), w_out=(w_out, m_w_out, v_w_out),
                 w_up=(w_up, m_w_up, v_w_up), w_down=(w_down, m_w_down, v_w_down))
    for k, (w, m, v) in big_w.items():
        ds[k], ms[k], vs[k] = _elementwise(_adamw, "adamw_" + k, [w, big_g[k], m, v], 3)
        gs[k] = big_g[k]

    order = ["norm1", "w_in", "pool_w", "pool_scale", "sg_norm", "sg_w", "sg_b", "w_out", "norm2", "w_up", "w_down",
             "final_norm"]
    return (loss, grad_x, *[gs[k] for k in order], *[ds[k] for k in order], *[ms[k] for k in order],
            *[vs[k] for k in order])
```

```python
import functools

import jax
import jax.numpy as jnp
from jax import lax
from jax.experimental import pallas as pl
from jax.experimental.pallas import tpu as pltpu

F32 = jnp.float32
BF16 = jnp.bfloat16
MESH = pl.DeviceIdType.MESH
AXES = ("x", "y", "c")

EPS = 1e-6
D_MODEL = 1024
POOL_WIDTH = 256
SG_WIDTH = 256
SB_WIDTH = 512
POOL_WINDOWS = (2, 4, 8, 16)
POOL_GW = 64
POOL_HALO = 16
CHUNK = 128
SG_HEADS = 4
SB_HD = 64
SB_SCALE = 0.125
IN_COLS = 2304
QKV_OFF = 768
D_FF = 4096
N_CHIPS = 4
LANES = 128
VMEM_LIMIT = 56 * 1024 * 1024
ATTN_TILE = 256
UNDERFLOW = -104.0

ADAM_LR = 0.001
ADAM_B1 = 0.9
ADAM_B2 = 0.999
ADAM_EPS = 1e-08
ADAM_WD = 0.01
ADAM_STEP = 10

HBM_SPEC = pl.BlockSpec(memory_space=pl.ANY)
VMEM_SPEC = pl.BlockSpec(memory_space=pltpu.VMEM)


def _params(**kw):
    return pltpu.CompilerParams(vmem_limit_bytes=VMEM_LIMIT, **kw)


def _tile(n, pref):
    if n <= pref:
        return n
    for t in range(pref - pref % LANES, 0, -LANES):
        if n % t == 0:
            return t
    raise ValueError((n, pref))


def _nn(a, b):
    return jnp.dot(a, b, preferred_element_type=F32)


def _nt(a, b):
    return lax.dot_general(a, b, (((1,), (1,)), ((), ())), preferred_element_type=F32)


def _tn(a, b):
    return lax.dot_general(a, b, (((0,), (0,)), ((), ())), preferred_element_type=F32)


def _rms_fwd(x, g):
    r = lax.rsqrt(jnp.mean(x * x, axis=-1, keepdims=True) + EPS)
    xhat = x * r
    return xhat * g, xhat, r


def _rms_bwd(dy, xhat, r, g):
    dxhat = dy * g
    dx = r * (dxhat - xhat * jnp.mean(dxhat * xhat, axis=-1, keepdims=True))
    return dx, dy * xhat


_GELU_K = 0.7978845608028654
_GELU_C = 0.044715


def _gelu(x):
    return 0.5 * x * (1.0 + jnp.tanh(_GELU_K * (x + _GELU_C * x * x * x)))


def _gelu_grad(x):
    t = jnp.tanh(_GELU_K * (x + _GELU_C * x * x * x))
    return 0.5 * (1.0 + t) + 0.5 * x * (1.0 - t * t) * _GELU_K * (1.0 + 3.0 * _GELU_C * x * x)


def _inproj_fwd(x, g, w):
    T, D = x.shape
    N = w.shape[1]
    tt = _tile(T, 512)

    def body(x_ref, g_ref, w_ref, proj_ref, h_ref, qkv_ref):
        h, _, _ = _rms_fwd(x_ref[...], g_ref[...])
        hb = h.astype(BF16)
        h_ref[...] = hb
        p = _nn(hb, w_ref[...])
        proj_ref[...] = p[:, :QKV_OFF]
        qkv_ref[...] = p[:, QKV_OFF:].astype(BF16)

    return pl.pallas_call(
        body, name="inproj_fwd", grid=(T // tt,),
        in_specs=[pl.BlockSpec((tt, D), lambda i: (i, 0)), pl.BlockSpec((1, D), lambda i: (0, 0)),
                  pl.BlockSpec((D, N), lambda i: (0, 0))],
        out_specs=[pl.BlockSpec((tt, QKV_OFF), lambda i: (i, 0)), pl.BlockSpec((tt, D), lambda i: (i, 0)),
                   pl.BlockSpec((tt, N - QKV_OFF), lambda i: (i, 0))],
        out_shape=[jax.ShapeDtypeStruct((T, QKV_OFF), F32), jax.ShapeDtypeStruct((T, D), BF16),
                   jax.ShapeDtypeStruct((T, N - QKV_OFF), BF16)],
        compiler_params=_params(),
    )(x, g, w)


def _inproj_bwd(pieces, w, x, g, dres):
    T, D = x.shape
    N = w.shape[1]
    tt = _tile(T, 512)
    widths = [p.shape[1] for p in pieces]
    offs = [sum(widths[:k]) for k in range(len(widths))]
    assert sum(widths) == N
    n_p = len(pieces)

    def body(*refs):
        p_refs = refs[:n_p]
        w_ref, x_ref, g_ref, dres_ref, dproj_ref, dx_ref, dg_ref = refs[n_p:]
        for p_ref, o, wd in zip(p_refs, offs, widths):
            dproj_ref[:, o:o + wd] = p_ref[...].astype(BF16)
        dh = _nt(dproj_ref[...], w_ref[...])
        gv = g_ref[...]
        _, xhat, r = _rms_fwd(x_ref[...], gv)
        dx, dgrow = _rms_bwd(dh, xhat, r, gv)
        dx_ref[...] = dres_ref[...] + dx

        @pl.when(pl.program_id(0) == 0)
        def _():
            dg_ref[...] = jnp.zeros_like(dg_ref)

        dg_ref[...] += jnp.sum(dgrow, axis=0, keepdims=True)

    return pl.pallas_call(
        body, name="inproj_bwd", grid=(T // tt,),
        in_specs=[pl.BlockSpec((tt, wd), lambda i: (i, 0)) for wd in widths] + [
            pl.BlockSpec((D, N), lambda i: (0, 0)), pl.BlockSpec((tt, D), lambda i: (i, 0)),
            pl.BlockSpec((1, D), lambda i: (0, 0)), pl.BlockSpec((tt, D), lambda i: (i, 0))],
        out_specs=[pl.BlockSpec((tt, N), lambda i: (i, 0)), pl.BlockSpec((tt, D), lambda i: (i, 0)),
                   pl.BlockSpec((1, D), lambda i: (0, 0))],
        out_shape=[jax.ShapeDtypeStruct((T, N), BF16), jax.ShapeDtypeStruct((T, D), F32),
                   jax.ShapeDtypeStruct((1, D), F32)],
        compiler_params=_params(),
    )(*pieces, w, x, g, dres)


def _pool_select(s2, s4, s8, s16, grp):
    return jnp.where(grp == 0, s2, jnp.where(grp == 1, s4, jnp.where(grp == 2, s8, s16)))


def _pool_count(t_glob, grp):
    win = jnp.where(grp == 0, 2, jnp.where(grp == 1, 4, jnp.where(grp == 2, 8, 16)))
    return jnp.minimum(t_glob + 1, win).astype(F32)


def _pool_diff(a, halo, base, tt):
    n = tt + POOL_HALO
    ext = jnp.concatenate([halo, a], axis=0)
    s2 = ext + pltpu.roll(ext, 1, 0)
    s4 = s2 + pltpu.roll(s2, 2, 0)
    s8 = s4 + pltpu.roll(s4, 4, 0)
    s16 = s8 + pltpu.roll(s8, 8, 0)
    grp = lax.broadcasted_iota(jnp.int32, (n, POOL_WIDTH), 1) // POOL_GW
    t_glob = lax.broadcasted_iota(jnp.int32, (n, POOL_WIDTH), 0) + (base - POOL_HALO)
    pooled = _pool_select(s2, s4, s8, s16, grp) / _pool_count(t_glob, grp)
    return pooled[POOL_HALO:] - a


def _pool_specs(T, tt):
    hb = tt // POOL_HALO
    return [pl.BlockSpec((tt, POOL_WIDTH), lambda i: (i, 0)),
            pl.BlockSpec((POOL_HALO, POOL_WIDTH), lambda i: (jnp.maximum(i * hb - 1, 0), 0))]


def _pool_fwd(proj, wbd, scale):
    T = proj.shape[0]
    tt = _tile(T, 512)

    def body(a_ref, halo_ref, w_ref, sc_ref, y_ref):
        i = pl.program_id(0)
        halo = jnp.where(i > 0, halo_ref[...], 0.0)
        d = _pool_diff(a_ref[...], halo, i * tt, tt)
        y_ref[...] = _nn(d.astype(BF16), w_ref[...]) * sc_ref[...]

    return pl.pallas_call(
        body, name="pool_fwd", grid=(T // tt,),
        in_specs=_pool_specs(T, tt) + [pl.BlockSpec((POOL_WIDTH, POOL_WIDTH), lambda i: (0, 0)),
                                       pl.BlockSpec((1, POOL_WIDTH), lambda i: (0, 0))],
        out_specs=pl.BlockSpec((tt, POOL_WIDTH), lambda i: (i, 0)),
        out_shape=jax.ShapeDtypeStruct((T, POOL_WIDTH), F32),
        compiler_params=_params(),
    )(proj, proj, wbd, scale)


def _pool_bwd(proj, dymix, wbd, scale):
    T = proj.shape[0]
    tt = _tile(T, 512)
    hb = tt // POOL_HALO
    nblk = T // tt
    n = tt + POOL_HALO

    def body(a_ref, halo_ref, dy_ref, dyn_ref, w_ref, sc_ref, da_ref, dw_ref, dsc_ref):
        i = pl.program_id(0)
        halo = jnp.where(i > 0, halo_ref[...], 0.0)
        d = _pool_diff(a_ref[...], halo, i * tt, tt)
        db = d.astype(BF16)
        wv = w_ref[...]
        sc = sc_ref[...]
        dy = dy_ref[...]
        dys = dy * sc

        @pl.when(i == 0)
        def _():
            dw_ref[...] = jnp.zeros_like(dw_ref)
            dsc_ref[...] = jnp.zeros_like(dsc_ref)

        dsc_ref[...] += jnp.sum(dy * _nn(db, wv), axis=0, keepdims=True)
        dw_ref[...] += _tn(db, dys.astype(BF16))
        dyn = jnp.where(i < nblk - 1, dyn_ref[...], 0.0) * sc
        dd = _nt(jnp.concatenate([dys, dyn], axis=0).astype(BF16), wv)
        grp = lax.broadcasted_iota(jnp.int32, (n, POOL_WIDTH), 1) // POOL_GW
        t_glob = lax.broadcasted_iota(jnp.int32, (n, POOL_WIDTH), 0) + i * tt
        e = dd / _pool_count(t_glob, grp)
        r2 = e + pltpu.roll(e, n - 1, 0)
        r4 = r2 + pltpu.roll(r2, n - 2, 0)
        r8 = r4 + pltpu.roll(r4, n - 4, 0)
        r16 = r8 + pltpu.roll(r8, n - 8, 0)
        da_ref[...] = (_pool_select(r2, r4, r8, r16, grp) - dd)[:tt]

    return pl.pallas_call(
        body, name="pool_bwd", grid=(nblk,),
        in_specs=_pool_specs(T, tt) + [
            pl.BlockSpec((tt, POOL_WIDTH), lambda i: (i, 0)),
            pl.BlockSpec((POOL_HALO, POOL_WIDTH), lambda i: (jnp.minimum((i + 1) * hb, T // POOL_HALO - 1), 0)),
            pl.BlockSpec((POOL_WIDTH, POOL_WIDTH), lambda i: (0, 0)), pl.BlockSpec((1, POOL_WIDTH), lambda i: (0, 0))],
        out_specs=[pl.BlockSpec((tt, POOL_WIDTH), lambda i: (i, 0)),
                   pl.BlockSpec((POOL_WIDTH, POOL_WIDTH), lambda i: (0, 0)),
                   pl.BlockSpec((1, POOL_WIDTH), lambda i: (0, 0))],
        out_shape=[jax.ShapeDtypeStruct((T, POOL_WIDTH), F32),
                   jax.ShapeDtypeStruct((POOL_WIDTH, POOL_WIDTH), F32),
                   jax.ShapeDtypeStruct((1, POOL_WIDTH), F32)],
        compiler_params=_params(),
    )(proj, proj, dymix, dymix, wbd, scale)


def _head_select(stacked, grp):
    out = jnp.where(grp == 0, stacked[0:CHUNK], 0.0)
    for h in range(1, SG_HEADS):
        out = out + jnp.where(grp == h, stacked[h * CHUNK:(h + 1) * CHUNK], 0.0)
    return out


def _sg_specs(tt):
    return [pl.BlockSpec((tt, SG_WIDTH), lambda i: (i, 1)), pl.BlockSpec((tt, SG_WIDTH), lambda i: (i, 2))]


def _sg_fwd(proj, wm, bias, g):
    T = proj.shape[0]
    tt = _tile(T, 512)

    def body(u_ref, v_ref, wm_ref, b_ref, g_ref, y_ref):
        zu = _gelu(u_ref[...])
        vn, _, _ = _rms_fwd(_gelu(v_ref[...]), g_ref[...])
        grp = lax.broadcasted_iota(jnp.int32, (CHUNK, SG_WIDTH), 1) // SB_HD
        for n in range(tt // CHUNK):
            rows = slice(n * CHUNK, (n + 1) * CHUNK)
            sv = _head_select(_nn(wm_ref[...], vn[rows].astype(BF16)), grp) + b_ref[...]
            y_ref[rows, :] = zu[rows] * sv

    return pl.pallas_call(
        body, name="sg_fwd", grid=(T // tt,),
        in_specs=_sg_specs(tt) + [pl.BlockSpec((SG_HEADS * CHUNK, CHUNK), lambda i: (0, 0)),
                                  pl.BlockSpec((CHUNK, SG_WIDTH), lambda i: (0, 0)),
                                  pl.BlockSpec((1, SG_WIDTH), lambda i: (0, 0))],
        out_specs=pl.BlockSpec((tt, SG_WIDTH), lambda i: (i, 0)),
        out_shape=jax.ShapeDtypeStruct((T, SG_WIDTH), F32),
        compiler_params=_params(),
    )(proj, proj, wm, bias, g)


def _sg_bwd(proj, dymix, wm, wmt, bias, g, hosted=None):
    T = proj.shape[0]
    tt = _tile(T, 512)
    nblk = T // tt

    def body(*refs):
        i = pl.program_id(0)
        (u_ref, v_ref, dy_ref, wm_ref, wmt_ref, b_ref, g_ref, du_ref, dv_ref, dw_ref, db_ref, dg_ref,
         dvn_ref, dbias_ref), start, wait = _host(hosted, refs, 7, 5, i == 0, i == nblk - 1)
        start()
        up, vp = u_ref[...], v_ref[...]
        gv = g_ref[...]
        zu, zv = _gelu(up), _gelu(vp)
        vn, xhat, r = _rms_fwd(zv, gv)
        gu = _gelu_grad(up)
        grp = lax.broadcasted_iota(jnp.int32, (CHUNK, SG_WIDTH), 1) // SB_HD

        @pl.when(i == 0)
        def _():
            dw_ref[...] = jnp.zeros_like(dw_ref)
            dbias_ref[...] = jnp.zeros_like(dbias_ref)
            dg_ref[...] = jnp.zeros_like(dg_ref)

        for n in range(tt // CHUNK):
            rows = slice(n * CHUNK, (n + 1) * CHUNK)
            vc = vn[rows].astype(BF16)
            sv = _head_select(_nn(wm_ref[...], vc), grp) + b_ref[...]
            dy = dy_ref[rows, :]
            du_ref[rows, :] = dy * sv * gu[rows]
            dsv = dy * zu[rows]
            dsvb = dsv.astype(BF16)
            dvn_ref[rows, :] = _head_select(_nn(wmt_ref[...], dsvb), grp)
            stacked = jnp.concatenate([jnp.where(grp == h, dsv, 0.0) for h in range(SG_HEADS)], axis=0)
            dw_ref[...] += _nt(stacked.astype(BF16), vc)
            dbias_ref[...] += dsv

        dzv, dgrow = _rms_bwd(dvn_ref[...], xhat, r, gv)
        dg_ref[...] += jnp.sum(dgrow, axis=0, keepdims=True)
        dv_ref[...] = dzv * _gelu_grad(vp)

        @pl.when(i == nblk - 1)
        def _():
            t_i = lax.broadcasted_iota(jnp.int32, (SG_HEADS * CHUNK, CHUNK), 0) % CHUNK
            s_i = lax.broadcasted_iota(jnp.int32, (SG_HEADS * CHUNK, CHUNK), 1)
            dw_ref[...] = jnp.where(s_i <= t_i, dw_ref[...], 0.0)
            lane = lax.broadcasted_iota(jnp.int32, (CHUNK, LANES), 1)
            acc = jnp.zeros((CHUNK, LANES), F32)
            for h in range(SG_HEADS):
                tot = jnp.sum(jnp.where(grp == h, dbias_ref[...], 0.0), axis=1, keepdims=True)
                acc = acc + jnp.where(lane == h, tot, 0.0)
            db_ref[...] = acc

        wait()

    h_ins = hosted.ins if hosted else []
    res = pl.pallas_call(
        body, name="sg_bwd_hosting" if hosted else "sg_bwd", grid=(nblk,),
        in_specs=_sg_specs(tt) + [pl.BlockSpec((tt, SG_WIDTH), lambda i: (i, 1)),
                                  pl.BlockSpec((SG_HEADS * CHUNK, CHUNK), lambda i: (0, 0)),
                                  pl.BlockSpec((SG_HEADS * CHUNK, CHUNK), lambda i: (0, 0)),
                                  pl.BlockSpec((CHUNK, SG_WIDTH), lambda i: (0, 0)),
                                  pl.BlockSpec((1, SG_WIDTH), lambda i: (0, 0))] + [HBM_SPEC] * len(h_ins),
        out_specs=[pl.BlockSpec((tt, SG_WIDTH), lambda i: (i, 0)), pl.BlockSpec((tt, SG_WIDTH), lambda i: (i, 0)),
                   pl.BlockSpec((SG_HEADS * CHUNK, CHUNK), lambda i: (0, 0)),
                   pl.BlockSpec((CHUNK, LANES), lambda i: (0, 0)), pl.BlockSpec((1, SG_WIDTH), lambda i: (0, 0))]
        + [HBM_SPEC] * len(h_ins),
        out_shape=[jax.ShapeDtypeStruct((T, SG_WIDTH), F32), jax.ShapeDtypeStruct((T, SG_WIDTH), F32),
                   jax.ShapeDtypeStruct((SG_HEADS * CHUNK, CHUNK), F32),
                   jax.ShapeDtypeStruct((CHUNK, LANES), F32), jax.ShapeDtypeStruct((1, SG_WIDTH), F32)]
        + (hosted.out_shapes if hosted else []),
        scratch_shapes=[pltpu.VMEM((tt, SG_WIDTH), F32), pltpu.VMEM((CHUNK, SG_WIDTH), F32)]
        + (hosted.sems() if hosted else []),
        compiler_params=_params(has_side_effects=hosted is not None),
    )(proj, proj, dymix, wm, wmt, bias, g, *h_ins)
    return res[:5], res[5:]


def _split_dot(x, u):
    hi = x.astype(BF16)
    lo = (x - hi.astype(F32)).astype(BF16)
    return _nn(hi, u) + _nn(lo, u)


def _sb_logits(z):
    lb = jnp.minimum(z, 0.0) - jnp.log(1.0 + jnp.exp(-jnp.abs(z)))
    return lb, lb - z


ATTN_STRIP = 32


def _by_strips(n_rows, fn):
    parts = None
    for r in range(0, n_rows, ATTN_STRIP):
        res = fn(slice(r, r + ATTN_STRIP))
        parts = [[v] for v in res] if parts is None else [p + [v] for p, v in zip(parts, res)]
    return [jnp.concatenate(p, axis=0) for p in parts]


def _attn_qkv_specs(tq, T):
    base = (IN_COLS - 3 * SB_WIDTH - QKV_OFF) // LANES
    nb = SB_WIDTH // LANES
    return [pl.BlockSpec((tq, LANES), lambda p, i: (i, base + p)),
            pl.BlockSpec((T, LANES), lambda p, i: (0, base + nb + p)),
            pl.BlockSpec((T, LANES), lambda p, i: (0, base + 2 * nb + p))]


class _Hosted:
    def __init__(self, ins, out_shapes, n_sems, copies):
        self.ins, self.out_shapes, self.n_sems, self.copies = ins, out_shapes, n_sems, copies

    @property
    def n(self):
        return len(self.ins)

    def sems(self):
        return [pltpu.SemaphoreType.DMA((self.n_sems,)), pltpu.SemaphoreType.DMA((self.n_sems,))]

    def start(self, src, dst, ssem, rsem):
        for send, _ in self.copies(src, dst, ssem, rsem):
            send.start()

    def wait(self, src, dst, ssem, rsem):
        for send, recv in self.copies(src, dst, ssem, rsem):
            recv.wait_recv()
            send.wait_send()


def _host(hosted, refs, n_in, n_out, first, last):
    if hosted is None:
        return refs, lambda: None, lambda: None
    n = hosted.n
    own_in, h_in = refs[:n_in], refs[n_in:n_in + n]
    own_out, h_out = refs[n_in + n:n_in + n + n_out], refs[n_in + n + n_out:n_in + 2 * n + n_out]
    rest = refs[n_in + 2 * n + n_out:]
    ssem, rsem = rest[-2:]

    def start():
        if first is True:
            hosted.start(h_in, h_out, ssem, rsem)
        else:
            pl.when(first)(lambda: hosted.start(h_in, h_out, ssem, rsem))

    def wait():
        if last is True:
            hosted.wait(h_in, h_out, ssem, rsem)
        else:
            pl.when(last)(lambda: hosted.wait(h_in, h_out, ssem, rsem))

    return own_in + own_out + rest[:-2], start, wait


def _attn_fwd(qkv, hosted=None):
    T = qkv.shape[0]
    tq = _tile(T, ATTN_TILE)
    n_p, nq = SB_WIDTH // LANES, T // tq

    def body(*refs):
        p, i = pl.program_id(0), pl.program_id(1)
        (q_ref, k_ref, v_ref, o_ref), start, wait = _host(
            hosted, refs, 3, 1, jnp.logical_and(p == 0, i == 0), jnp.logical_and(p == n_p - 1, i == nq - 1))
        start()
        lane = lax.broadcasted_iota(jnp.int32, (tq, LANES), 1)
        row = lax.broadcasted_iota(jnp.int32, (tq, tq), 0)
        col = lax.broadcasted_iota(jnp.int32, (tq, tq), 1)
        after = jnp.where(row > col, 1.0, 0.0).astype(BF16)
        valid = col < row
        q = q_ref[...].astype(F32)
        qh = [jnp.where((lane // SB_HD) == hh, q * SB_SCALE, 0.0).astype(BF16) for hh in range(2)]

        def tiles(todo, state):
            chains = [(t, hh) for t in range(len(todo)) for hh in range(2)]
            kv = []
            for j, _ in todo:
                ks = pl.ds(pl.multiple_of(j * tq, tq), tq)
                kv.append((k_ref[ks, :], v_ref[ks, :]))
            z = {(t, hh): _nt(qh[hh], kv[t][0]) for t, hh in chains}
            lb, lmb, lm_sum = {}, {}, {}
            for t, hh in chains:
                def logits(rows, z=z[(t, hh)], mask=todo[t][1]):
                    lb, lm = _sb_logits(z[rows])
                    if mask is not None:
                        lm = jnp.where(mask[rows], lm, 0.0)
                    return lb, lm.astype(BF16), jnp.sum(lm, axis=1, keepdims=True)

                lb[(t, hh)], lmb[(t, hh)], lm_sum[(t, hh)] = _by_strips(tq, logits)
            x = {c: _nn(lmb[c], after) for c in chains}
            carry = [state[hh][0] for hh in range(2)]
            acc = [state[hh][1] for hh in range(2)]
            for t, hh in chains:
                def weights(rows, lb=lb[(t, hh)], x=x[(t, hh)], carry=carry[hh], mask=todo[t][1]):
                    a = jnp.exp(lb[rows] + x[rows] + carry[rows])
                    if mask is not None:
                        a = jnp.where(mask[rows], a, 0.0)
                    return (a.astype(BF16),)

                (ab,) = _by_strips(tq, weights)
                acc[hh] = acc[hh] + _nn(ab, kv[t][1])
                carry[hh] = carry[hh] + lm_sum[(t, hh)]
            return tuple((carry[hh], acc[hh]) for hh in range(2))

        def live(state):
            return jnp.maximum(jnp.max(state[0][0]), jnp.max(state[1][0]))

        zero = (jnp.zeros((tq, 1), F32), jnp.zeros((tq, LANES), F32))
        state = tiles([(i, valid), (jnp.maximum(i - 1, 0), jnp.broadcast_to(i > 0, (tq, tq)))], (zero, zero))

        def cond(st):
            return jnp.logical_and(st[0] >= 0, st[2] > UNDERFLOW)

        def step(st):
            state = tiles([(st[0], None)], st[1])
            return st[0] - 1, state, live(state)

        _, state, _ = lax.while_loop(cond, step, (i - 2, state, live(state)))
        o_ref[...] = jnp.where(lane < SB_HD, state[0][1], state[1][1])
        wait()

    h_ins = hosted.ins if hosted else []
    res = pl.pallas_call(
        body, name="attn_fwd_hosting" if hosted else "attn_fwd", grid=(n_p, nq),
        in_specs=_attn_qkv_specs(tq, T) + [HBM_SPEC] * len(h_ins),
        out_specs=[pl.BlockSpec((tq, LANES), lambda p, i: (i, p))] + [HBM_SPEC] * len(h_ins),
        out_shape=[jax.ShapeDtypeStruct((T, SB_WIDTH), F32)] + (hosted.out_shapes if hosted else []),
        scratch_shapes=hosted.sems() if hosted else [],
        compiler_params=_params(has_side_effects=hosted is not None),
    )(qkv, qkv, qkv, *h_ins)
    return res[0], res[1:]


def _attn_bwd(qkv, o, dymix, hosted=None):
    T = qkv.shape[0]
    tq = _tile(T, ATTN_TILE)
    n_p, nq = SB_WIDTH // LANES, T // tq
    yc_blk = (POOL_WIDTH + SG_WIDTH) // LANES

    def body(*refs):
        p, i = pl.program_id(0), pl.program_id(1)
        (q_ref, k_ref, v_ref, o_ref, do_ref, dq_ref, dk_ref, dv_ref), start, wait = _host(
            hosted, refs, 5, 3, jnp.logical_and(p == 0, i == 0), jnp.logical_and(p == n_p - 1, i == nq - 1))
        start()
        lane = lax.broadcasted_iota(jnp.int32, (tq, LANES), 1)
        row = lax.broadcasted_iota(jnp.int32, (tq, tq), 0)
        col = lax.broadcasted_iota(jnp.int32, (tq, tq), 1)
        after = jnp.where(row > col, 1.0, 0.0).astype(BF16)
        from_here = jnp.where(row >= col, 1.0, 0.0).astype(BF16)
        from_here2 = jnp.concatenate([from_here, from_here], axis=0)
        valid = col < row

        @pl.when(i == 0)
        def _():
            dk_ref[...] = jnp.zeros_like(dk_ref)
            dv_ref[...] = jnp.zeros_like(dv_ref)

        q = q_ref[...].astype(F32)
        ov = o_ref[...]
        dov = do_ref[...]
        heads = [(lane // SB_HD) == hh for hh in range(2)]
        qh = [jnp.where(h, q * SB_SCALE, 0.0).astype(BF16) for h in heads]
        dohb = [jnp.where(h, dov, 0.0).astype(BF16) for h in heads]
        delta = [jnp.sum(d.astype(F32) * ov, axis=1, keepdims=True) for d in dohb]

        def tiles(todo, state):
            chains = [(t, hh) for t in range(len(todo)) for hh in range(2)]
            kv, where = [], []
            for j, _ in todo:
                ks = pl.ds(pl.multiple_of(j * tq, tq), tq)
                where.append(ks)
                kv.append((k_ref[ks, :], v_ref[ks, :]))
            z = {(t, hh): _nt(qh[hh], kv[t][0]) for t, hh in chains}
            da = {(t, hh): _nt(dohb[hh], kv[t][1]) for t, hh in chains}
            lb, lmb, lm_sum = {}, {}, {}
            for t, hh in chains:
                def logits(rows, z=z[(t, hh)], mask=todo[t][1]):
                    lb, lm = _sb_logits(z[rows])
                    if mask is not None:
                        lm = jnp.where(mask[rows], lm, 0.0)
                    return lb, lm.astype(BF16), jnp.sum(lm, axis=1, keepdims=True)

                lb[(t, hh)], lmb[(t, hh)], lm_sum[(t, hh)] = _by_strips(tq, logits)
            x = {c: _nn(lmb[c], after) for c in chains}
            c_a = [state[hh][0] for hh in range(2)]
            ab, g, g_split, g_sum = {}, {}, {}, {}
            for t, hh in chains:
                def weights(rows, lb=lb[(t, hh)], x=x[(t, hh)], da=da[(t, hh)], c_a=c_a[hh], mask=todo[t][1]):
                    a = jnp.exp(lb[rows] + x[rows] + c_a[rows])
                    if mask is not None:
                        a = jnp.where(mask[rows], a, 0.0)
                    ab = a.astype(BF16)
                    g = da[rows] * ab.astype(F32)
                    hi = g.astype(BF16)
                    lo = (g - hi.astype(F32)).astype(BF16)
                    return ab, g, jnp.concatenate([hi, lo], axis=1), jnp.sum(g, axis=1, keepdims=True)

                ab[(t, hh)], g[(t, hh)], g_split[(t, hh)], g_sum[(t, hh)] = _by_strips(tq, weights)
                c_a[hh] = c_a[hh] + lm_sum[(t, hh)]
            right = {c: _nn(g_split[c], from_here2) for c in chains}
            c_r = [state[hh][1] for hh in range(2)]
            dzb = {}
            for t, hh in chains:
                def logit_grads(rows, lb=lb[(t, hh)], g=g[(t, hh)], right=right[(t, hh)], c_r=c_r[hh], hh=hh,
                                mask=todo[t][1]):
                    sig = jnp.exp(lb[rows])
                    left = delta[hh][rows] - (c_r[rows] + right[rows])
                    dz = g[rows] * (1.0 - sig) - left * sig
                    if mask is not None:
                        dz = jnp.where(mask[rows], dz, 0.0)
                    return (dz.astype(BF16),)

                (dzb[(t, hh)],) = _by_strips(tq, logit_grads)
                c_r[hh] = c_r[hh] + g_sum[(t, hh)]
            dqa = [state[hh][2] for hh in range(2)]
            for t in range(len(todo)):
                dk_ref[where[t], :] += _tn(dzb[(t, 0)], qh[0]) + _tn(dzb[(t, 1)], qh[1])
                dv_ref[where[t], :] += _tn(ab[(t, 0)], dohb[0]) + _tn(ab[(t, 1)], dohb[1])
                for hh in range(2):
                    dqa[hh] = dqa[hh] + _nn(dzb[(t, hh)], kv[t][0])
            return tuple((c_a[hh], c_r[hh], dqa[hh]) for hh in range(2))

        def live(state):
            return jnp.maximum(jnp.max(state[0][0]), jnp.max(state[1][0]))

        zero = (jnp.zeros((tq, 1), F32), jnp.zeros((tq, 1), F32), jnp.zeros((tq, LANES), F32))
        state = tiles([(i, valid), (jnp.maximum(i - 1, 0), jnp.broadcast_to(i > 0, (tq, tq)))], (zero, zero))

        def cond(st):
            return jnp.logical_and(st[0] >= 0, st[2] > UNDERFLOW)

        def step(st):
            state = tiles([(st[0], None)], st[1])
            return st[0] - 1, state, live(state)

        _, state, _ = lax.while_loop(cond, step, (i - 2, state, live(state)))
        dq_ref[...] = jnp.where(lane < SB_HD, state[0][2], state[1][2]) * SB_SCALE
        wait()

    h_ins = hosted.ins if hosted else []
    res = pl.pallas_call(
        body, name="attn_bwd_hosting" if hosted else "attn_bwd", grid=(n_p, nq),
        in_specs=_attn_qkv_specs(tq, T) + [pl.BlockSpec((tq, LANES), lambda p, i: (i, p)),
                                           pl.BlockSpec((tq, LANES), lambda p, i: (i, yc_blk + p))]
        + [HBM_SPEC] * len(h_ins),
        out_specs=[pl.BlockSpec((tq, LANES), lambda p, i: (i, p)), pl.BlockSpec((T, LANES), lambda p, i: (0, p)),
                   pl.BlockSpec((T, LANES), lambda p, i: (0, p))] + [HBM_SPEC] * len(h_ins),
        out_shape=[jax.ShapeDtypeStruct((T, SB_WIDTH), F32)] * 3 + (hosted.out_shapes if hosted else []),
        scratch_shapes=hosted.sems() if hosted else [],
        compiler_params=_params(has_side_effects=hosted is not None),
    )(qkv, qkv, qkv, o, dymix, *h_ins)
    return res[:3], res[3:]


def _outproj_fwd(x, ya, yb, yc, w):
    T, D = x.shape
    tt = _tile(T, 512)

    def body(x_ref, ya_ref, yb_ref, yc_ref, w_ref, x1_ref, ymix_ref):
        ymix_ref[:, 0:POOL_WIDTH] = ya_ref[...].astype(BF16)
        ymix_ref[:, POOL_WIDTH:POOL_WIDTH + SG_WIDTH] = yb_ref[...].astype(BF16)
        ymix_ref[:, POOL_WIDTH + SG_WIDTH:] = yc_ref[...].astype(BF16)
        x1_ref[...] = x_ref[...] + _nn(ymix_ref[...], w_ref[...])

    row = lambda width: pl.BlockSpec((tt, width), lambda i: (i, 0))
    return pl.pallas_call(
        body, name="outproj_fwd", grid=(T // tt,),
        in_specs=[row(D), row(POOL_WIDTH), row(SG_WIDTH), row(SB_WIDTH), pl.BlockSpec((D, D), lambda i: (0, 0))],
        out_specs=[row(D), row(D)],
        out_shape=[jax.ShapeDtypeStruct((T, D), F32), jax.ShapeDtypeStruct((T, D), BF16)],
        compiler_params=_params(),
    )(x, ya, yb, yc, w)


def _nt_matmul(a, w):
    T, N = a.shape
    K = w.shape[0]
    tt = _tile(T, 512)

    def body(a_ref, w_ref, o_ref):
        o_ref[...] = _nt(a_ref[...].astype(BF16), w_ref[...])

    return pl.pallas_call(
        body, name="nt_matmul", grid=(T // tt,),
        in_specs=[pl.BlockSpec((tt, N), lambda i: (i, 0)), pl.BlockSpec((K, N), lambda i: (0, 0))],
        out_specs=pl.BlockSpec((tt, K), lambda i: (i, 0)),
        out_shape=jax.ShapeDtypeStruct((T, K), F32),
        compiler_params=_params(),
    )(a, w)


def _tn_matmul(a, b, name, n_split=1, hosted=None):
    T, K = a.shape
    N = b.shape[1]
    tk = _tile(K, 1024)
    tn = _tile(N // n_split, 1024)
    tt = _tile(T, 2048)
    nper = N // n_split // tn
    nk, nn, nt = K // tk, N // tn, T // tt

    def body(*refs):
        k, n, t = pl.program_id(0), pl.program_id(1), pl.program_id(2)
        (a_ref, b_ref, o_ref), start, wait = _host(
            hosted, refs, 2, 1, jnp.logical_and(jnp.logical_and(k == 0, n == 0), t == 0),
            jnp.logical_and(jnp.logical_and(k == nk - 1, n == nn - 1), t == nt - 1))
        start()

        @pl.when(t == 0)
        def _():
            o_ref[...] = jnp.zeros_like(o_ref)

        o_ref[...] += _tn(a_ref[...], b_ref[...].astype(BF16))
        wait()

    h_ins = hosted.ins if hosted else []
    res = pl.pallas_call(
        body, name=name + "_hosting" if hosted else name, grid=(nk, nn, nt),
        in_specs=[pl.BlockSpec((tt, tk), lambda k, n, t: (t, k)), pl.BlockSpec((tt, tn), lambda k, n, t: (t, n))]
        + [HBM_SPEC] * len(h_ins),
        out_specs=[pl.BlockSpec((None, tk, tn), lambda k, n, t: (n // nper, k, n % nper))] + [HBM_SPEC] * len(h_ins),
        out_shape=[jax.ShapeDtypeStruct((n_split, K, N // n_split), F32)] + (hosted.out_shapes if hosted else []),
        scratch_shapes=hosted.sems() if hosted else [],
        compiler_params=_params(has_side_effects=hosted is not None),
    )(a, b, *h_ins)
    return (res[0], res[1:]) if hosted else res[0]


def _mlp_fwd(x, g, w_up, w_down, hosted=None):
    T, D = x.shape
    F = w_up.shape[1]
    tt = _tile(T, 1024)
    fc = _tile(F, 512)
    nc = F // fc
    nt = T // tt

    def body(*refs):
        i, c = pl.program_id(0), pl.program_id(1)
        (x_ref, g_ref, wu_ref, wd_ref, y_ref, h_ref, u_ref, a_ref), start, wait = _host(
            hosted, refs, 4, 4, jnp.logical_and(i == 0, c == 0), jnp.logical_and(i == nt - 1, c == nc - 1))
        start()

        @pl.when(c == 0)
        def _():
            xv = x_ref[...]
            h, _, _ = _rms_fwd(xv, g_ref[...])
            h_ref[...] = h.astype(BF16)
            y_ref[...] = xv

        u = _nn(h_ref[...], wu_ref[...])
        u_ref[...] = u.astype(BF16)
        a = jnp.square(jnp.maximum(u, 0.0)).astype(BF16)
        a_ref[...] = a
        y_ref[...] += _nn(a, wd_ref[...])
        wait()

    h_ins = hosted.ins if hosted else []
    res = pl.pallas_call(
        body, name="mlp_fwd_hosting" if hosted else "mlp_fwd", grid=(nt, nc),
        in_specs=[pl.BlockSpec((tt, D), lambda i, c: (i, 0)), pl.BlockSpec((1, D), lambda i, c: (0, 0)),
                  pl.BlockSpec((D, fc), lambda i, c: (0, c)), pl.BlockSpec((fc, D), lambda i, c: (c, 0))]
        + [HBM_SPEC] * len(h_ins),
        out_specs=[pl.BlockSpec((tt, D), lambda i, c: (i, 0)), pl.BlockSpec((tt, D), lambda i, c: (i, 0)),
                   pl.BlockSpec((tt, fc), lambda i, c: (i, c)), pl.BlockSpec((tt, fc), lambda i, c: (i, c))]
        + [HBM_SPEC] * len(h_ins),
        out_shape=[jax.ShapeDtypeStruct((T, D), F32), jax.ShapeDtypeStruct((T, D), BF16),
                   jax.ShapeDtypeStruct((T, F), BF16), jax.ShapeDtypeStruct((T, F), BF16)]
        + (hosted.out_shapes if hosted else []),
        scratch_shapes=hosted.sems() if hosted else [],
        compiler_params=_params(has_side_effects=hosted is not None),
    )(x, g, w_up, w_down, *h_ins)
    return res[:4], res[4:]


def _mlp_bwd(dy, x, g, u, w_up, w_down, hosted=None):
    T, D = x.shape
    F = w_up.shape[1]
    tt = _tile(T, 1024)
    fc = _tile(F, 512)
    nc = F // fc
    nt = T // tt

    def body(*refs):
        i, c = pl.program_id(0), pl.program_id(1)
        (dy_ref, x_ref, g_ref, u_ref, wu_ref, wd_ref, dx_ref, du_ref, dg_ref, dyb_ref, dh_ref), start, wait = _host(
            hosted, refs, 6, 3, jnp.logical_and(i == 0, c == 0), jnp.logical_and(i == nt - 1, c == nc - 1))
        start()

        @pl.when(c == 0)
        def _():
            dyb_ref[...] = dy_ref[...].astype(BF16)
            dh_ref[...] = jnp.zeros_like(dh_ref)

        @pl.when(jnp.logical_and(i == 0, c == 0))
        def _():
            dg_ref[...] = jnp.zeros_like(dg_ref)

        da = _nt(dyb_ref[...], wd_ref[...])
        du = (da * (2.0 * jnp.maximum(u_ref[...].astype(F32), 0.0))).astype(BF16)
        du_ref[...] = du
        dh_ref[...] += _nt(du, wu_ref[...])

        @pl.when(c == nc - 1)
        def _():
            gv = g_ref[...]
            _, xhat, r = _rms_fwd(x_ref[...], gv)
            dx, dgrow = _rms_bwd(dh_ref[...], xhat, r, gv)
            dx_ref[...] = dy_ref[...] + dx
            dg_ref[...] += jnp.sum(dgrow, axis=0, keepdims=True)

        wait()

    h_ins = hosted.ins if hosted else []
    res = pl.pallas_call(
        body, name="mlp_bwd_hosting" if hosted else "mlp_bwd", grid=(nt, nc),
        in_specs=[pl.BlockSpec((tt, D), lambda i, c: (i, 0)), pl.BlockSpec((tt, D), lambda i, c: (i, 0)),
                  pl.BlockSpec((1, D), lambda i, c: (0, 0)), pl.BlockSpec((tt, fc), lambda i, c: (i, c)),
                  pl.BlockSpec((D, fc), lambda i, c: (0, c)), pl.BlockSpec((fc, D), lambda i, c: (c, 0))]
        + [HBM_SPEC] * len(h_ins),
        out_specs=[pl.BlockSpec((tt, D), lambda i, c: (i, 0)), pl.BlockSpec((tt, fc), lambda i, c: (i, c)),
                   pl.BlockSpec((1, D), lambda i, c: (0, 0))] + [HBM_SPEC] * len(h_ins),
        out_shape=[jax.ShapeDtypeStruct((T, D), F32), jax.ShapeDtypeStruct((T, F), BF16),
                   jax.ShapeDtypeStruct((1, D), F32)] + (hosted.out_shapes if hosted else []),
        scratch_shapes=[pltpu.VMEM((tt, D), BF16), pltpu.VMEM((tt, D), F32)] + (hosted.sems() if hosted else []),
        compiler_params=_params(has_side_effects=hosted is not None),
    )(dy, x, g, u, w_up, w_down, *h_ins)
    return res[:3], res[3:]


def _loss_head(x, g, target):
    T, D = x.shape
    tt = _tile(T, 512)

    def body(x_ref, g_ref, t_ref, loss_ref, dx_ref, dg_ref):
        gv = g_ref[...]
        y, xhat, r = _rms_fwd(x_ref[...], gv)
        err = y - t_ref[...]
        dx, dgrow = _rms_bwd(err * (1.0 / D), xhat, r, gv)
        dx_ref[...] = dx

        @pl.when(pl.program_id(0) == 0)
        def _():
            loss_ref[...] = jnp.zeros_like(loss_ref)
            dg_ref[...] = jnp.zeros_like(dg_ref)

        loss_ref[...] += 0.5 * jnp.sum(jnp.mean(err * err, axis=-1, keepdims=True), axis=0, keepdims=True)
        dg_ref[...] += jnp.sum(dgrow, axis=0, keepdims=True)

    return pl.pallas_call(
        body, name="loss_head", grid=(T // tt,),
        in_specs=[pl.BlockSpec((tt, D), lambda i: (i, 0)), pl.BlockSpec((1, D), lambda i: (0, 0)),
                  pl.BlockSpec((tt, D), lambda i: (i, 0))],
        out_specs=[pl.BlockSpec((1, LANES), lambda i: (0, 0)), pl.BlockSpec((tt, D), lambda i: (i, 0)),
                   pl.BlockSpec((1, D), lambda i: (0, 0))],
        out_shape=[jax.ShapeDtypeStruct((1, LANES), F32), jax.ShapeDtypeStruct((T, D), F32),
                   jax.ShapeDtypeStruct((1, D), F32)],
        compiler_params=_params(),
    )(x, g, target)


def _rows(shape, pref=512):
    last = shape[-1]
    rows = 1
    for s in shape[:-1]:
        rows *= s
    tr = rows
    if rows * last > 256 * 1024:
        for cand in (pref, 256, 128, 64, 32, 16, 8):
            if rows % cand == 0:
                tr = cand
                break
    return rows, last, tr


def _elementwise(fn, name, ins, n_out, out_dtype=F32):
    shape = ins[0].shape
    rows, last, tr = _rows(shape)
    flat = [a.reshape(rows, last) for a in ins]
    n_in = len(ins)

    def body(*refs):
        res = fn(*[r[...] for r in refs[:n_in]])
        if n_out == 1:
            res = (res,)
        for r, v in zip(refs[n_in:], res):
            r[...] = v.astype(r.dtype)

    spec = pl.BlockSpec((tr, last), lambda i: (i, 0))
    outs = pl.pallas_call(
        body, name=name, grid=(rows // tr,),
        in_specs=[spec] * n_in, out_specs=[spec] * n_out,
        out_shape=[jax.ShapeDtypeStruct((rows, last), out_dtype)] * n_out,
        compiler_params=_params(),
    )(*flat)
    return [o.reshape(shape) for o in outs]


def _add_pair(g, o, c_idx):
    nq, R, C = g.shape
    h = R // 2
    tr = _tile(h, 512)
    nb = h // tr

    def body(c_ref, g_ref, o_ref, out_ref):
        out_ref[...] = g_ref[...] + o_ref[...]

    return pl.pallas_call(
        body, name="add_pair",
        grid_spec=pltpu.PrefetchScalarGridSpec(
            num_scalar_prefetch=1, grid=(nq, nb),
            in_specs=[pl.BlockSpec((None, tr, C), lambda q, i, c: (q, c[0] * nb + i, 0)),
                      pl.BlockSpec((None, tr, C), lambda q, i, c: (q, i, 0))],
            out_specs=pl.BlockSpec((None, tr, C), lambda q, i, c: (q, i, 0))),
        out_shape=jax.ShapeDtypeStruct((nq, h, C), F32),
        compiler_params=_params(),
    )(c_idx.astype(jnp.int32).reshape(1), g, o)


def _add_chips(p, r, q_idx):
    _, H, C = p.shape
    tr = _tile(H, 512)

    def body(q_ref, p_ref, r0_ref, r1_ref, r2_ref, out_ref):
        out_ref[...] = (p_ref[...] + r0_ref[...]) + (r1_ref[...] + r2_ref[...])

    def arrived(k):
        return pl.BlockSpec((None, tr, C), lambda i, q: (k, i, 0))

    return pl.pallas_call(
        body, name="add_chips",
        grid_spec=pltpu.PrefetchScalarGridSpec(
            num_scalar_prefetch=1, grid=(H // tr,),
            in_specs=[pl.BlockSpec((None, tr, C), lambda i, q: (q[0], i, 0)), arrived(0), arrived(1), arrived(2)],
            out_specs=pl.BlockSpec((tr, C), lambda i, q: (i, 0))),
        out_shape=jax.ShapeDtypeStruct((H, C), F32),
        compiler_params=_params(),
    )(q_idx.astype(jnp.int32).reshape(1), p, r, r, r)


def _adamw(w, g, m, v):
    m = ADAM_B1 * m + (1.0 - ADAM_B1) * g
    v = ADAM_B2 * v + (1.0 - ADAM_B2) * jnp.square(g)
    m_hat = m / (1.0 - ADAM_B1 ** ADAM_STEP)
    v_hat = v / (1.0 - ADAM_B2 ** ADAM_STEP)
    delta = -ADAM_LR * (m_hat / (jnp.sqrt(v_hat) + ADAM_EPS) + ADAM_WD * w)
    return delta, m, v


def _place():
    x, y, c = lax.axis_index("x"), lax.axis_index("y"), lax.axis_index("c")
    chips = [(1 - x, y), (x, 1 - y), (1 - x, 1 - y)]
    return x, y, c, chips


def _remote(src, dst, ssem, rsem, k, dev):
    return pltpu.make_async_remote_copy(src_ref=src, dst_ref=dst, send_sem=ssem.at[k], recv_sem=rsem.at[k],
                                        device_id=dev, device_id_type=MESH)


def _gather_weights(shards):
    n = len(shards)
    halves = [s.shape[1] // 2 for s in shards]

    def body(*refs):
        src, out = refs[:n], refs[n:2 * n]
        ssem, rsem = refs[2 * n:]
        x, y, c, chips = _place()
        me_q = 2 * x + y
        sib = (x, y, 1 - c)

        def half(a, q, cc):
            return out[a].at[q, :, pl.ds(cc * halves[a], halves[a]), :]

        first = []
        for a in range(n):
            mine = src[a].at[:, pl.ds(c * halves[a], halves[a]), :]
            for r, chip in enumerate(chips):
                first.append(_remote(mine, half(a, me_q, c), ssem, rsem, a * 3 + r, (*chip, c)))
        for cp in first:
            cp.start()
        passed = []
        for a in range(n):
            for r, chip in enumerate(chips):
                q = 2 * chip[0] + chip[1]
                k = a * 3 + r
                _remote(half(a, q, c), half(a, q, c), ssem, rsem, k, (*chip, c)).wait_recv()
                cp = _remote(half(a, q, c), half(a, q, c), ssem, rsem, 3 * n + k, sib)
                cp.start()
                passed.append(cp)
        for a in range(n):
            for r, chip in enumerate(chips):
                q = 2 * chip[0] + chip[1]
                _remote(half(a, q, 1 - c), half(a, q, 1 - c), ssem, rsem, 3 * n + a * 3 + r, sib).wait_recv()
        for cp in first + passed:
            cp.wait_send()

    return pl.pallas_call(
        body, name="gather_weights",
        in_specs=[HBM_SPEC] * n, out_specs=[HBM_SPEC] * n,
        out_shape=[jax.ShapeDtypeStruct((N_CHIPS,) + s.shape, s.dtype) for s in shards],
        scratch_shapes=[pltpu.SemaphoreType.DMA((6 * n,)), pltpu.SemaphoreType.DMA((6 * n,))],
        compiler_params=_params(has_side_effects=True),
    )(*shards)


def _gather_over_ici(shards):
    n = len(shards)
    halves = [s.shape[1] // 2 for s in shards]

    def copies(src, out, ssem, rsem):
        x, y, c, chips = _place()
        me_q = 2 * x + y
        res = []
        for a in range(n):
            rows = pl.ds(c * halves[a], halves[a])
            mine = src[a].at[:, rows, :]
            for r, chip in enumerate(chips):
                dev = (*chip, c)
                res.append((_remote(mine, out[a].at[me_q, :, rows, :], ssem, rsem, a * 3 + r, dev),
                            _remote(mine, out[a].at[2 * chip[0] + chip[1], :, rows, :], ssem, rsem, a * 3 + r, dev)))
        return res

    return _Hosted(list(shards), [jax.ShapeDtypeStruct((N_CHIPS,) + s.shape, s.dtype) for s in shards], 3 * n, copies)


def _pass_to_sibling(gathered):
    n = len(gathered)
    halves = [g.shape[2] // 2 for g in gathered]

    def body(*refs):
        out = refs[n:2 * n]
        ssem, rsem = refs[2 * n:]
        x, y, c, chips = _place()
        sib = (x, y, 1 - c)

        def half(a, q, cc):
            return out[a].at[q, :, pl.ds(cc * halves[a], halves[a]), :]

        cps = []
        for a in range(n):
            for r, chip in enumerate(chips):
                q = 2 * chip[0] + chip[1]
                cps.append(_remote(half(a, q, c), half(a, q, c), ssem, rsem, a * 3 + r, sib))
        for cp in cps:
            cp.start()
        for a in range(n):
            for r, chip in enumerate(chips):
                q = 2 * chip[0] + chip[1]
                _remote(half(a, q, 1 - c), half(a, q, 1 - c), ssem, rsem, a * 3 + r, sib).wait_recv()
        for cp in cps:
            cp.wait_send()

    return pl.pallas_call(
        body, name="pass_to_sibling",
        in_specs=[HBM_SPEC] * n, out_specs=[HBM_SPEC] * n,
        out_shape=[jax.ShapeDtypeStruct(g.shape, g.dtype) for g in gathered],
        input_output_aliases={a: a for a in range(n)},
        scratch_shapes=[pltpu.SemaphoreType.DMA((3 * n,)), pltpu.SemaphoreType.DMA((3 * n,))],
        compiler_params=_params(has_side_effects=True),
    )(*gathered)


def _scatter_over_ici(parts):
    n = len(parts)

    def copies(src, out, ssem, rsem):
        x, y, c, chips = _place()
        res = []
        for a in range(n):
            for r, chip in enumerate(chips):
                cp = _remote(src[a].at[2 * chip[0] + chip[1]], out[a].at[r], ssem, rsem, a * 3 + r, (*chip, c))
                res.append((cp, cp))
        return res

    return _Hosted(list(parts), [jax.ShapeDtypeStruct((3,) + p.shape[1:], F32) for p in parts], 3 * n, copies)


def _swap_over_d2d(grads):
    n = len(grads)
    halves = [g.shape[1] // 2 for g in grads]

    def copies(src, out, ssem, rsem):
        x, y, c, _ = _place()
        res = []
        for a in range(n):
            cp = _remote(src[a].at[:, pl.ds((1 - c) * halves[a], halves[a]), :], out[a], ssem, rsem, a, (x, y, 1 - c))
            res.append((cp, cp))
        return res

    return _Hosted(list(grads), [jax.ShapeDtypeStruct((N_CHIPS, h, g.shape[2]), F32) for g, h in zip(grads, halves)],
                   n, copies)


def _swap_halves(grads):
    n = len(grads)
    halves = [g.shape[1] // 2 for g in grads]

    def body(*refs):
        src, out = refs[:n], refs[n:2 * n]
        ssem, rsem = refs[2 * n:]
        x, y, c, _ = _place()
        cps = [_remote(src[a].at[:, pl.ds((1 - c) * halves[a], halves[a]), :], out[a], ssem, rsem, a, (x, y, 1 - c))
               for a in range(n)]
        for cp in cps:
            cp.start()
        for cp in cps:
            cp.wait()

    return pl.pallas_call(
        body, name="swap_halves",
        in_specs=[HBM_SPEC] * n, out_specs=[HBM_SPEC] * n,
        out_shape=[jax.ShapeDtypeStruct((N_CHIPS, h, g.shape[2]), F32) for g, h in zip(grads, halves)],
        scratch_shapes=[pltpu.SemaphoreType.DMA((n,)), pltpu.SemaphoreType.DMA((n,))],
        compiler_params=_params(has_side_effects=True),
    )(*grads)


def _scatter_chips(parts):
    n = len(parts)

    def body(*refs):
        src, out = refs[:n], refs[n:2 * n]
        ssem, rsem = refs[2 * n:]
        x, y, c, chips = _place()
        cps = []
        for a in range(n):
            for r, chip in enumerate(chips):
                cps.append(_remote(src[a].at[2 * chip[0] + chip[1]], out[a].at[r], ssem, rsem, a * 3 + r, (*chip, c)))
        for cp in cps:
            cp.start()
        for cp in cps:
            cp.wait()

    return pl.pallas_call(
        body, name="scatter_chips",
        in_specs=[HBM_SPEC] * n, out_specs=[HBM_SPEC] * n,
        out_shape=[jax.ShapeDtypeStruct((3,) + p.shape[1:], F32) for p in parts],
        scratch_shapes=[pltpu.SemaphoreType.DMA((3 * n,)), pltpu.SemaphoreType.DMA((3 * n,))],
        compiler_params=_params(has_side_effects=True),
    )(*parts)


def _swap_reduced_over_d2d(reduced):
    n = len(reduced)

    def copies(src, out, ssem, rsem):
        x, y, c, _ = _place()
        res = []
        for a in range(n):
            cp = _remote(src[a], out[a], ssem, rsem, a, (x, y, 1 - c))
            res.append((cp, cp))
        return res

    return _Hosted(list(reduced), [jax.ShapeDtypeStruct(r.shape, F32) for r in reduced], n, copies)


def _swap_reduced(reduced):
    n = len(reduced)

    def body(*refs):
        src, out = refs[:n], refs[n:2 * n]
        ssem, rsem = refs[2 * n:]
        x, y, c, _ = _place()
        cps = [_remote(src[a], out[a], ssem, rsem, a, (x, y, 1 - c)) for a in range(n)]
        for cp in cps:
            cp.start()
        for cp in cps:
            cp.wait()

    return pl.pallas_call(
        body, name="swap_reduced",
        in_specs=[HBM_SPEC] * n, out_specs=[HBM_SPEC] * n,
        out_shape=[jax.ShapeDtypeStruct(r.shape, F32) for r in reduced],
        scratch_shapes=[pltpu.SemaphoreType.DMA((n,)), pltpu.SemaphoreType.DMA((n,))],
        compiler_params=_params(has_side_effects=True),
    )(*reduced)


def _allreduce_small(buf, hosted=None):
    R, L = buf.shape

    def body(*refs):
        (buf_ref, out_ref, pair_ref, chip_ref, ssem, rsem), start, wait = _host(hosted, refs, 1, 1, True, True)
        start()
        x, y, c, chips = _place()
        me_q = 2 * x + y
        pair_ref[c] = buf_ref[...]
        to_sib = _remote(buf_ref, pair_ref.at[c], ssem, rsem, 0, (x, y, 1 - c))
        to_sib.start()
        _remote(buf_ref, pair_ref.at[1 - c], ssem, rsem, 0, (x, y, 1 - c)).wait_recv()
        chip_ref[me_q] = pair_ref[0] + pair_ref[1]
        cps = [_remote(chip_ref.at[me_q], chip_ref.at[me_q], ssem, rsem, 1 + r, (*chip, c))
               for r, chip in enumerate(chips)]
        for cp in cps:
            cp.start()
        for r, chip in enumerate(chips):
            q = 2 * chip[0] + chip[1]
            _remote(chip_ref.at[q], chip_ref.at[q], ssem, rsem, 1 + r, (*chip, c)).wait_recv()
        out_ref[...] = (chip_ref[0] + chip_ref[1]) + (chip_ref[2] + chip_ref[3])
        to_sib.wait_send()
        for cp in cps:
            cp.wait_send()
        wait()

    h_ins = hosted.ins if hosted else []
    res = pl.pallas_call(
        body, name="allreduce_small",
        in_specs=[VMEM_SPEC] + [HBM_SPEC] * len(h_ins), out_specs=[VMEM_SPEC] + [HBM_SPEC] * len(h_ins),
        out_shape=[jax.ShapeDtypeStruct((R, L), F32)] + (hosted.out_shapes if hosted else []),
        scratch_shapes=[pltpu.VMEM((2, R, L), F32), pltpu.VMEM((N_CHIPS, R, L), F32),
                        pltpu.SemaphoreType.DMA((4,)), pltpu.SemaphoreType.DMA((4,))]
        + (hosted.sems() if hosted else []),
        compiler_params=_params(has_side_effects=True),
    )(buf, *h_ins)
    return res[0], res[1:]


def _pack(arrays):
    flat = jnp.concatenate([a.reshape(-1) for a in arrays])
    pad = (-flat.shape[0]) % (8 * LANES)
    return jnp.pad(flat, (0, pad)).reshape(-1, LANES)


def _unpack(buf, like):
    flat = buf.reshape(-1)
    out, off = [], 0
    for a in like:
        out.append(flat[off:off + a.size].reshape(a.shape))
        off += a.size
    return out


def _block_diag(pw):
    rows = []
    for gi in range(len(POOL_WINDOWS)):
        blocks = [pw[gi] if gj == gi else jnp.zeros_like(pw[gi]) for gj in range(len(POOL_WINDOWS))]
        rows.append(jnp.concatenate(blocks, axis=1))
    return jnp.concatenate(rows, axis=0)


def kernel(x, norm1, w_in, pool_w, pool_scale, sg_norm, sg_w, sg_b, w_out, norm2, w_up, w_down, final_norm, loss_target, m_norm1, m_w_in, m_pool_w, m_pool_scale, m_sg_norm, m_sg_w, m_sg_b, m_w_out, m_norm2, m_w_up, m_w_down, m_final_norm, v_norm1, v_w_in, v_pool_w, v_pool_scale, v_sg_norm, v_sg_w, v_sg_b, v_w_out, v_norm2, v_w_up, v_w_down, v_final_norm):
    depth = norm1.shape[0]
    T = x.shape[1]
    xs = x.reshape(T, D_MODEL)
    target = loss_target.reshape(T, D_MODEL)

    assert depth == 2
    c_idx = lax.axis_index("c")
    q_idx = 2 * lax.axis_index("x") + lax.axis_index("y")
    own = [w.astype(BF16) for w in (w_in, w_out, w_up, w_down)]
    gathered = {(0, 0): _gather_weights([own[0][:1]])[0]}

    def full(a, l, axis):
        blocks = [jnp.where(q_idx == q, own[a][l], gathered[(a, l)][q, 0]) for q in range(N_CHIPS)]
        return jnp.concatenate(blocks, axis=axis)

    def gather_behind(call, keys):
        res, over_ici = call(_gather_over_ici([own[a][l:l + 1] for a, l in keys]))
        gathered.update(zip(keys, _pass_to_sibling(over_ici)))
        return res

    tril = jnp.tril(jnp.ones((CHUNK, CHUNK), F32))
    saved = []
    cur = xs
    wi, wo, wu, wd = {}, {}, {}, {}
    for l in range(depth):
        wbd = _block_diag(pool_w[l]).astype(BF16)
        wm = sg_w[l] * tril
        wm_s = wm.reshape(SG_HEADS * CHUNK, CHUNK).astype(BF16)
        wmt_s = jnp.swapaxes(wm, 1, 2).reshape(SG_HEADS * CHUNK, CHUNK).astype(BF16)
        bias = jnp.repeat(sg_b[l].T, SB_HD, axis=1)
        n1, n2 = norm1[l][None], norm2[l][None]
        psc, sgn = pool_scale[l][None], sg_norm[l][None]
        wi[l] = full(0, l, 1)
        proj, h, qkv = _inproj_fwd(cur, n1, wi[l])
        ya = _pool_fwd(proj, wbd, psc)
        yb = _sg_fwd(proj, wm_s, bias, sgn)
        if l == 0:
            yc = gather_behind(lambda hosted: _attn_fwd(qkv, hosted), [(1, 0), (2, 0), (3, 0)])
        else:
            yc, _ = _attn_fwd(qkv)
        wo[l], wu[l], wd[l] = full(1, l, 0), full(2, l, 1), full(3, l, 0)
        x1, ymix = _outproj_fwd(cur, ya, yb, yc, wo[l])
        if l == 0:
            x2, h2, u, act = gather_behind(lambda hosted: _mlp_fwd(x1, n2, wu[l], wd[l], hosted),
                                           [(0, 1), (1, 1), (2, 1), (3, 1)])
        else:
            (x2, h2, u, act), _ = _mlp_fwd(x1, n2, wu[l], wd[l])
        saved.append(dict(x0=cur, x1=x1, proj=proj, h=h, qkv=qkv, yc=yc, ymix=ymix, h2=h2, u=u, act=act,
                          wbd=wbd, wm_s=wm_s, wmt_s=wmt_s, bias=bias, n1=n1, n2=n2, psc=psc, sgn=sgn))
        cur = x2

    loss_row, dcur, d_final = _loss_head(cur, final_norm[None], target)

    small = [None] * depth
    grads, parts, reduced = {}, {}, {}

    def pair_up(keys, swapped):
        parts.update({k: _add_pair(grads[k], o, c_idx) for k, o in zip(keys, swapped)})

    def chip_up(keys, arrived):
        reduced.update({k: _add_chips(parts[k], r, q_idx) for k, r in zip(keys, arrived)})

    for l in reversed(range(depth)):
        s = saved[l]
        if l == 0:
            keys = [(2, 1), (3, 1)]
            (dx1, du, d_n2), arrived = _mlp_bwd(dcur, s["x1"], s["n2"], s["u"], wu[l], wd[l],
                                                _scatter_over_ici([parts[k] for k in keys]))
            chip_up(keys, arrived)
        else:
            (dx1, du, d_n2), _ = _mlp_bwd(dcur, s["x1"], s["n2"], s["u"], wu[l], wd[l])
        grads[(2, l)] = _tn_matmul(s["h2"], du, "grad_w_up", n_split=N_CHIPS)
        grads[(3, l)] = _tn_matmul(s["act"], dcur, "grad_w_down")[0].reshape(N_CHIPS, D_FF // N_CHIPS, D_MODEL)
        dymix = _nt_matmul(dx1, wo[l])
        grads[(1, l)] = _tn_matmul(s["ymix"], dx1, "grad_w_out")[0].reshape(N_CHIPS, D_MODEL // N_CHIPS, D_MODEL)
        da_in, d_wbd, d_psc = _pool_bwd(s["proj"], dymix, s["wbd"], s["psc"])
        if l == 0:
            keys = [(0, 1), (1, 0), (2, 0), (3, 0)]
            (du_pre, dv_pre, d_wm, d_bias, d_sgn), swapped = _sg_bwd(
                s["proj"], dymix, s["wm_s"], s["wmt_s"], s["bias"], s["sgn"], _swap_over_d2d([grads[k] for k in keys]))
            pair_up(keys, swapped)
            keys = [(1, 1)] + keys
            (dq, dk, dv), arrived = _attn_bwd(s["qkv"], s["yc"], dymix, _scatter_over_ici([parts[k] for k in keys]))
            chip_up(keys, arrived)
        else:
            (du_pre, dv_pre, d_wm, d_bias, d_sgn), _ = _sg_bwd(s["proj"], dymix, s["wm_s"], s["wmt_s"], s["bias"], s["sgn"])
            keys = [(1, l), (2, l), (3, l)]
            (dq, dk, dv), swapped = _attn_bwd(s["qkv"], s["yc"], dymix, _swap_over_d2d([grads[k] for k in keys]))
            pair_up(keys, swapped)
        dproj, dx0, d_n1 = _inproj_bwd([da_in, du_pre, dv_pre, dq, dk, dv], wi[l], s["x0"], s["n1"], dx1)
        if l == 0:
            keys = sorted(reduced)
            g_in_l, swapped = _tn_matmul(s["h"], dproj, "grad_w_in",
                                         hosted=_swap_reduced_over_d2d([reduced[k] for k in keys]))
            theirs = dict(zip(keys, swapped))
        else:
            g_in_l = _tn_matmul(s["h"], dproj, "grad_w_in")
        grads[(0, l)] = g_in_l[0].reshape(D_MODEL, N_CHIPS, IN_COLS // N_CHIPS).transpose(1, 0, 2)
        d_pw = jnp.stack([d_wbd[gi * POOL_GW:(gi + 1) * POOL_GW, gi * POOL_GW:(gi + 1) * POOL_GW]
                          for gi in range(len(POOL_WINDOWS))])
        small[l] = dict(norm1=d_n1[0], pool_w=d_pw, pool_scale=d_psc[0], sg_norm=d_sgn[0],
                        sg_w=d_wm.reshape(SG_HEADS, CHUNK, CHUNK), sg_b=d_bias[:, :SG_HEADS].T, norm2=d_n2[0])
        dcur = dx0
    grad_x = dcur.reshape(x.shape)

    names = ["norm1", "pool_w", "pool_scale", "sg_norm", "sg_w", "sg_b", "norm2"]
    slot = jnp.zeros((1,), F32)
    small_w = [norm1, pool_w, pool_scale, sg_norm, sg_w, sg_b, norm2, final_norm, slot]
    small_m = [m_norm1, m_pool_w, m_pool_scale, m_sg_norm, m_sg_w, m_sg_b, m_norm2, m_final_norm, slot]
    small_v = [v_norm1, v_pool_w, v_pool_scale, v_sg_norm, v_sg_w, v_sg_b, v_norm2, v_final_norm, slot]
    small_g = [jnp.stack([small[l][k] for l in range(depth)]) for k in names] + [d_final[0], loss_row[0, :1]]
    keys = [(0, 0)]
    pair_up(keys, _swap_halves([grads[k] for k in keys]))
    g_packed, arrived = _allreduce_small(_pack(small_g), _scatter_over_ici([parts[k] for k in keys]))
    chip_up(keys, arrived)
    theirs.update(zip(keys, _swap_reduced([reduced[k] for k in keys])))

    def joined(a):
        layers = []
        for l in range(depth):
            mine, other = reduced[(a, l)], theirs[(a, l)]
            layers.append(jnp.where(c_idx == 0, jnp.concatenate([mine, other]), jnp.concatenate([other, mine])))
        return jnp.stack(layers)

    gw_in, gw_out, gw_up, gw_down = [joined(a) for a in range(4)]

    loss = _unpack(g_packed, small_w)[-1][0]
    s_delta, s_m, s_v = _elementwise(_adamw, "adamw_small", [_pack(small_w), g_packed, _pack(small_m), _pack(small_v)], 3)
    gs = dict(zip(names + ["final_norm"], _unpack(g_packed, small_w)))
    ds = dict(zip(names + ["final_norm"], _unpack(s_delta, small_w)))
    ms = dict(zip(names + ["final_norm"], _unpack(s_m, small_w)))
    vs = dict(zip(names + ["final_norm"], _unpack(s_v, small_w)))

    big_g = dict(w_in=gw_in, w_out=gw_out, w_up=gw_up, w_down=gw_down)
    big_w = dict(w_in=(w_in, m_w_in, v_w_in), w_out=(w_out, m_w_out, v_w_out),
                 w_up=(w_up, m_w_up, v_w_up), w_down=(w_down, m_w_down, v_w_down))
    for k, (w, m, v) in big_w.items():
        operands = [w, big_g[k], m, v]
        if k == "w_in":
            operands = [jnp.swapaxes(o, 1, 2) for o in operands]
        ds[k], ms[k], vs[k] = _elementwise(_adamw, "adamw_" + k, operands, 3)
        if k == "w_in":
            ds[k], ms[k], vs[k] = [jnp.swapaxes(o, 1, 2) for o in (ds[k], ms[k], vs[k])]
        gs[k] = big_g[k]

    order = ["norm1", "w_in", "pool_w", "pool_scale", "sg_norm", "sg_w", "sg_b", "w_out", "norm2", "w_up", "w_down",
             "final_norm"]
    return (loss, grad_x, *[gs[k] for k in order], *[ds[k] for k in order], *[ms[k] for k in order],
            *[vs[k] for k in order])
```

```python
import functools

import jax
import jax.numpy as jnp
from jax import lax
from jax.experimental import pallas as pl
from jax.experimental.pallas import tpu as pltpu

F32 = jnp.float32
BF16 = jnp.bfloat16
MESH = pl.DeviceIdType.MESH
AXES = ("x", "y", "c")

EPS = 1e-6
D_MODEL = 1024
POOL_WIDTH = 256
SG_WIDTH = 256
SB_WIDTH = 512
POOL_WINDOWS = (2, 4, 8, 16)
POOL_GW = 64
POOL_HALO = 16
CHUNK = 128
SG_HEADS = 4
SB_HD = 64
SB_SCALE = 0.125
IN_COLS = 2304
QKV_OFF = 768
D_FF = 4096
N_CHIPS = 4
LANES = 128
VMEM_LIMIT = 56 * 1024 * 1024
MLP_CHUNK = 512
ATTN_TILE = 256
UNDERFLOW = -104.0

ADAM_LR = 0.001
ADAM_B1 = 0.9
ADAM_B2 = 0.999
ADAM_EPS = 1e-08
ADAM_WD = 0.01
ADAM_STEP = 10

HBM_SPEC = pl.BlockSpec(memory_space=pl.ANY)
VMEM_SPEC = pl.BlockSpec(memory_space=pltpu.VMEM)


def _params(**kw):
    return pltpu.CompilerParams(vmem_limit_bytes=VMEM_LIMIT, **kw)


def _tile(n, pref):
    if n <= pref:
        return n
    for t in range(pref - pref % LANES, 0, -LANES):
        if n % t == 0:
            return t
    raise ValueError((n, pref))


def _nn(a, b):
    return jnp.dot(a, b, preferred_element_type=F32)


def _nt(a, b):
    return lax.dot_general(a, b, (((1,), (1,)), ((), ())), preferred_element_type=F32)


def _tn(a, b):
    return lax.dot_general(a, b, (((0,), (0,)), ((), ())), preferred_element_type=F32)


def _rms_fwd(x, g):
    r = lax.rsqrt(jnp.mean(x * x, axis=-1, keepdims=True) + EPS)
    xhat = x * r
    return xhat * g, xhat, r


def _rms_bwd(dy, xhat, r, g):
    dxhat = dy * g
    dx = r * (dxhat - xhat * jnp.mean(dxhat * xhat, axis=-1, keepdims=True))
    return dx, dy * xhat


_GELU_K = 0.7978845608028654
_GELU_C = 0.044715


def _gelu(x):
    return 0.5 * x * (1.0 + jnp.tanh(_GELU_K * (x + _GELU_C * x * x * x)))


def _gelu_grad(x):
    t = jnp.tanh(_GELU_K * (x + _GELU_C * x * x * x))
    return 0.5 * (1.0 + t) + 0.5 * x * (1.0 - t * t) * _GELU_K * (1.0 + 3.0 * _GELU_C * x * x)


def _inproj_fwd(x, g, w):
    T, D = x.shape
    N = w.shape[1]
    tt = _tile(T, 512)

    def body(x_ref, g_ref, w_ref, proj_ref, h_ref, qkv_ref):
        h, _, _ = _rms_fwd(x_ref[...], g_ref[...])
        hb = h.astype(BF16)
        h_ref[...] = hb
        p = _nn(hb, w_ref[...])
        proj_ref[...] = p[:, :QKV_OFF]
        qkv_ref[...] = p[:, QKV_OFF:].astype(BF16)

    return pl.pallas_call(
        body, name="inproj_fwd", grid=(T // tt,),
        in_specs=[pl.BlockSpec((tt, D), lambda i: (i, 0)), pl.BlockSpec((1, D), lambda i: (0, 0)),
                  pl.BlockSpec((D, N), lambda i: (0, 0))],
        out_specs=[pl.BlockSpec((tt, QKV_OFF), lambda i: (i, 0)), pl.BlockSpec((tt, D), lambda i: (i, 0)),
                   pl.BlockSpec((tt, N - QKV_OFF), lambda i: (i, 0))],
        out_shape=[jax.ShapeDtypeStruct((T, QKV_OFF), F32), jax.ShapeDtypeStruct((T, D), BF16),
                   jax.ShapeDtypeStruct((T, N - QKV_OFF), BF16)],
        compiler_params=_params(),
    )(x, g, w)


def _inproj_bwd(pieces, w, x, g, dres):
    T, D = x.shape
    N = w.shape[1]
    tt = _tile(T, 512)
    widths = [p.shape[1] for p in pieces]
    offs = [sum(widths[:k]) for k in range(len(widths))]
    assert sum(widths) == N
    n_p = len(pieces)

    def body(*refs):
        p_refs = refs[:n_p]
        w_ref, x_ref, g_ref, dres_ref, dproj_ref, dx_ref, dg_ref = refs[n_p:]
        for p_ref, o, wd in zip(p_refs, offs, widths):
            dproj_ref[:, o:o + wd] = p_ref[...].astype(BF16)
        dh = _nt(dproj_ref[...], w_ref[...])
        gv = g_ref[...]
        _, xhat, r = _rms_fwd(x_ref[...], gv)
        dx, dgrow = _rms_bwd(dh, xhat, r, gv)
        dx_ref[...] = dres_ref[...] + dx

        @pl.when(pl.program_id(0) == 0)
        def _():
            dg_ref[...] = jnp.zeros_like(dg_ref)

        dg_ref[...] += jnp.sum(dgrow, axis=0, keepdims=True)

    return pl.pallas_call(
        body, name="inproj_bwd", grid=(T // tt,),
        in_specs=[pl.BlockSpec((tt, wd), lambda i: (i, 0)) for wd in widths] + [
            pl.BlockSpec((D, N), lambda i: (0, 0)), pl.BlockSpec((tt, D), lambda i: (i, 0)),
            pl.BlockSpec((1, D), lambda i: (0, 0)), pl.BlockSpec((tt, D), lambda i: (i, 0))],
        out_specs=[pl.BlockSpec((tt, N), lambda i: (i, 0)), pl.BlockSpec((tt, D), lambda i: (i, 0)),
                   pl.BlockSpec((1, D), lambda i: (0, 0))],
        out_shape=[jax.ShapeDtypeStruct((T, N), BF16), jax.ShapeDtypeStruct((T, D), F32),
                   jax.ShapeDtypeStruct((1, D), F32)],
        compiler_params=_params(),
    )(*pieces, w, x, g, dres)


def _pool_select(s2, s4, s8, s16, grp):
    return jnp.where(grp == 0, s2, jnp.where(grp == 1, s4, jnp.where(grp == 2, s8, s16)))


def _pool_count(t_glob, grp):
    win = jnp.where(grp == 0, 2, jnp.where(grp == 1, 4, jnp.where(grp == 2, 8, 16)))
    return jnp.minimum(t_glob + 1, win).astype(F32)


def _pool_diff(a, halo, base, tt):
    n = tt + POOL_HALO
    ext = jnp.concatenate([halo, a], axis=0)
    s2 = ext + pltpu.roll(ext, 1, 0)
    s4 = s2 + pltpu.roll(s2, 2, 0)
    s8 = s4 + pltpu.roll(s4, 4, 0)
    s16 = s8 + pltpu.roll(s8, 8, 0)
    grp = lax.broadcasted_iota(jnp.int32, (n, POOL_WIDTH), 1) // POOL_GW
    t_glob = lax.broadcasted_iota(jnp.int32, (n, POOL_WIDTH), 0) + (base - POOL_HALO)
    pooled = _pool_select(s2, s4, s8, s16, grp) / _pool_count(t_glob, grp)
    return pooled[POOL_HALO:] - a


def _pool_specs(T, tt):
    hb = tt // POOL_HALO
    return [pl.BlockSpec((tt, POOL_WIDTH), lambda i: (i, 0)),
            pl.BlockSpec((POOL_HALO, POOL_WIDTH), lambda i: (jnp.maximum(i * hb - 1, 0), 0))]


def _pool_fwd(proj, wbd, scale):
    T = proj.shape[0]
    tt = _tile(T, 512)

    def body(a_ref, halo_ref, w_ref, sc_ref, y_ref):
        i = pl.program_id(0)
        halo = jnp.where(i > 0, halo_ref[...], 0.0)
        d = _pool_diff(a_ref[...], halo, i * tt, tt)
        y_ref[...] = _nn(d.astype(BF16), w_ref[...]) * sc_ref[...]

    return pl.pallas_call(
        body, name="pool_fwd", grid=(T // tt,),
        in_specs=_pool_specs(T, tt) + [pl.BlockSpec((POOL_WIDTH, POOL_WIDTH), lambda i: (0, 0)),
                                       pl.BlockSpec((1, POOL_WIDTH), lambda i: (0, 0))],
        out_specs=pl.BlockSpec((tt, POOL_WIDTH), lambda i: (i, 0)),
        out_shape=jax.ShapeDtypeStruct((T, POOL_WIDTH), F32),
        compiler_params=_params(),
    )(proj, proj, wbd, scale)


def _pool_bwd(proj, dymix, wbd, scale):
    T = proj.shape[0]
    tt = _tile(T, 512)
    hb = tt // POOL_HALO
    nblk = T // tt
    n = tt + POOL_HALO

    def body(a_ref, halo_ref, dy_ref, dyn_ref, w_ref, sc_ref, da_ref, dw_ref, dsc_ref):
        i = pl.program_id(0)
        halo = jnp.where(i > 0, halo_ref[...], 0.0)
        d = _pool_diff(a_ref[...], halo, i * tt, tt)
        db = d.astype(BF16)
        wv = w_ref[...]
        sc = sc_ref[...]
        dy = dy_ref[...]
        dys = dy * sc

        @pl.when(i == 0)
        def _():
            dw_ref[...] = jnp.zeros_like(dw_ref)
            dsc_ref[...] = jnp.zeros_like(dsc_ref)

        dsc_ref[...] += jnp.sum(dy * _nn(db, wv), axis=0, keepdims=True)
        dw_ref[...] += _tn(db, dys.astype(BF16))
        dyn = jnp.where(i < nblk - 1, dyn_ref[...], 0.0) * sc
        dd = _nt(jnp.concatenate([dys, dyn], axis=0).astype(BF16), wv)
        grp = lax.broadcasted_iota(jnp.int32, (n, POOL_WIDTH), 1) // POOL_GW
        t_glob = lax.broadcasted_iota(jnp.int32, (n, POOL_WIDTH), 0) + i * tt
        e = dd / _pool_count(t_glob, grp)
        r2 = e + pltpu.roll(e, n - 1, 0)
        r4 = r2 + pltpu.roll(r2, n - 2, 0)
        r8 = r4 + pltpu.roll(r4, n - 4, 0)
        r16 = r8 + pltpu.roll(r8, n - 8, 0)
        da_ref[...] = (_pool_select(r2, r4, r8, r16, grp) - dd)[:tt].astype(BF16)

    return pl.pallas_call(
        body, name="pool_bwd", grid=(nblk,),
        in_specs=_pool_specs(T, tt) + [
            pl.BlockSpec((tt, POOL_WIDTH), lambda i: (i, 0)),
            pl.BlockSpec((POOL_HALO, POOL_WIDTH), lambda i: (jnp.minimum((i + 1) * hb, T // POOL_HALO - 1), 0)),
            pl.BlockSpec((POOL_WIDTH, POOL_WIDTH), lambda i: (0, 0)), pl.BlockSpec((1, POOL_WIDTH), lambda i: (0, 0))],
        out_specs=[pl.BlockSpec((tt, POOL_WIDTH), lambda i: (i, 0)),
                   pl.BlockSpec((POOL_WIDTH, POOL_WIDTH), lambda i: (0, 0)),
                   pl.BlockSpec((1, POOL_WIDTH), lambda i: (0, 0))],
        out_shape=[jax.ShapeDtypeStruct((T, POOL_WIDTH), BF16),
                   jax.ShapeDtypeStruct((POOL_WIDTH, POOL_WIDTH), F32),
                   jax.ShapeDtypeStruct((1, POOL_WIDTH), F32)],
        compiler_params=_params(),
    )(proj, proj, dymix, dymix, wbd, scale)


def _head_select(stacked, grp):
    out = jnp.where(grp == 0, stacked[0:CHUNK], 0.0)
    for h in range(1, SG_HEADS):
        out = out + jnp.where(grp == h, stacked[h * CHUNK:(h + 1) * CHUNK], 0.0)
    return out


def _sg_specs(tt):
    return [pl.BlockSpec((tt, SG_WIDTH), lambda i: (i, 1)), pl.BlockSpec((tt, SG_WIDTH), lambda i: (i, 2))]


def _sg_fwd(proj, wm, bias, g):
    T = proj.shape[0]
    tt = _tile(T, 512)

    def body(u_ref, v_ref, wm_ref, b_ref, g_ref, y_ref):
        zu = _gelu(u_ref[...])
        vn, _, _ = _rms_fwd(_gelu(v_ref[...]), g_ref[...])
        grp = lax.broadcasted_iota(jnp.int32, (CHUNK, SG_WIDTH), 1) // SB_HD
        for n in range(tt // CHUNK):
            rows = slice(n * CHUNK, (n + 1) * CHUNK)
            sv = _head_select(_nn(wm_ref[...], vn[rows].astype(BF16)), grp) + b_ref[...]
            y_ref[rows, :] = zu[rows] * sv

    return pl.pallas_call(
        body, name="sg_fwd", grid=(T // tt,),
        in_specs=_sg_specs(tt) + [pl.BlockSpec((SG_HEADS * CHUNK, CHUNK), lambda i: (0, 0)),
                                  pl.BlockSpec((CHUNK, SG_WIDTH), lambda i: (0, 0)),
                                  pl.BlockSpec((1, SG_WIDTH), lambda i: (0, 0))],
        out_specs=pl.BlockSpec((tt, SG_WIDTH), lambda i: (i, 0)),
        out_shape=jax.ShapeDtypeStruct((T, SG_WIDTH), F32),
        compiler_params=_params(),
    )(proj, proj, wm, bias, g)


def _sg_bwd(proj, dymix, wm, wmt, bias, g, hosted=None):
    T = proj.shape[0]
    tt = _tile(T, 512)
    nblk = T // tt

    def body(*refs):
        i = pl.program_id(0)
        (u_ref, v_ref, dy_ref, wm_ref, wmt_ref, b_ref, g_ref, du_ref, dv_ref, dw_ref, db_ref, dg_ref,
         dvn_ref, dbias_ref), start, wait = _host(hosted, refs, 7, 5, i == 0, i == nblk - 1)
        start()
        up, vp = u_ref[...], v_ref[...]
        gv = g_ref[...]
        zu, zv = _gelu(up), _gelu(vp)
        vn, xhat, r = _rms_fwd(zv, gv)
        gu = _gelu_grad(up)
        grp = lax.broadcasted_iota(jnp.int32, (CHUNK, SG_WIDTH), 1) // SB_HD

        @pl.when(i == 0)
        def _():
            dw_ref[...] = jnp.zeros_like(dw_ref)
            dbias_ref[...] = jnp.zeros_like(dbias_ref)
            dg_ref[...] = jnp.zeros_like(dg_ref)

        for n in range(tt // CHUNK):
            rows = slice(n * CHUNK, (n + 1) * CHUNK)
            vc = vn[rows].astype(BF16)
            sv = _head_select(_nn(wm_ref[...], vc), grp) + b_ref[...]
            dy = dy_ref[rows, :]
            du_ref[rows, :] = (dy * sv * gu[rows]).astype(BF16)
            dsv = dy * zu[rows]
            dsvb = dsv.astype(BF16)
            dvn_ref[rows, :] = _head_select(_nn(wmt_ref[...], dsvb), grp)
            stacked = jnp.concatenate([jnp.where(grp == h, dsv, 0.0) for h in range(SG_HEADS)], axis=0)
            dw_ref[...] += _nt(stacked.astype(BF16), vc)
            dbias_ref[...] += dsv

        dzv, dgrow = _rms_bwd(dvn_ref[...], xhat, r, gv)
        dg_ref[...] += jnp.sum(dgrow, axis=0, keepdims=True)
        dv_ref[...] = (dzv * _gelu_grad(vp)).astype(BF16)

        @pl.when(i == nblk - 1)
        def _():
            t_i = lax.broadcasted_iota(jnp.int32, (SG_HEADS * CHUNK, CHUNK), 0) % CHUNK
            s_i = lax.broadcasted_iota(jnp.int32, (SG_HEADS * CHUNK, CHUNK), 1)
            dw_ref[...] = jnp.where(s_i <= t_i, dw_ref[...], 0.0)
            lane = lax.broadcasted_iota(jnp.int32, (CHUNK, LANES), 1)
            acc = jnp.zeros((CHUNK, LANES), F32)
            for h in range(SG_HEADS):
                tot = jnp.sum(jnp.where(grp == h, dbias_ref[...], 0.0), axis=1, keepdims=True)
                acc = acc + jnp.where(lane == h, tot, 0.0)
            db_ref[...] = acc

        wait()

    h_ins = hosted.ins if hosted else []
    res = pl.pallas_call(
        body, name="sg_bwd_hosting" if hosted else "sg_bwd", grid=(nblk,),
        in_specs=_sg_specs(tt) + [pl.BlockSpec((tt, SG_WIDTH), lambda i: (i, 1)),
                                  pl.BlockSpec((SG_HEADS * CHUNK, CHUNK), lambda i: (0, 0)),
                                  pl.BlockSpec((SG_HEADS * CHUNK, CHUNK), lambda i: (0, 0)),
                                  pl.BlockSpec((CHUNK, SG_WIDTH), lambda i: (0, 0)),
                                  pl.BlockSpec((1, SG_WIDTH), lambda i: (0, 0))] + [HBM_SPEC] * len(h_ins),
        out_specs=[pl.BlockSpec((tt, SG_WIDTH), lambda i: (i, 0)), pl.BlockSpec((tt, SG_WIDTH), lambda i: (i, 0)),
                   pl.BlockSpec((SG_HEADS * CHUNK, CHUNK), lambda i: (0, 0)),
                   pl.BlockSpec((CHUNK, LANES), lambda i: (0, 0)), pl.BlockSpec((1, SG_WIDTH), lambda i: (0, 0))]
        + [HBM_SPEC] * len(h_ins),
        out_shape=[jax.ShapeDtypeStruct((T, SG_WIDTH), BF16), jax.ShapeDtypeStruct((T, SG_WIDTH), BF16),
                   jax.ShapeDtypeStruct((SG_HEADS * CHUNK, CHUNK), F32),
                   jax.ShapeDtypeStruct((CHUNK, LANES), F32), jax.ShapeDtypeStruct((1, SG_WIDTH), F32)]
        + (hosted.out_shapes if hosted else []),
        scratch_shapes=[pltpu.VMEM((tt, SG_WIDTH), F32), pltpu.VMEM((CHUNK, SG_WIDTH), F32)]
        + (hosted.sems() if hosted else []),
        compiler_params=_params(has_side_effects=hosted is not None),
    )(proj, proj, dymix, wm, wmt, bias, g, *h_ins)
    return res[:5], res[5:]


def _split_dot(x, u):
    hi = x.astype(BF16)
    lo = (x - hi.astype(F32)).astype(BF16)
    return _nn(hi, u) + _nn(lo, u)


def _sb_logits(z):
    lb = jnp.minimum(z, 0.0) - jnp.log(1.0 + jnp.exp(-jnp.abs(z)))
    return lb, lb - z


ATTN_STRIP = 32


def _by_strips(n_rows, fn):
    parts = None
    for r in range(0, n_rows, ATTN_STRIP):
        res = fn(slice(r, r + ATTN_STRIP))
        parts = [[v] for v in res] if parts is None else [p + [v] for p, v in zip(parts, res)]
    return [jnp.concatenate(p, axis=0) for p in parts]


def _attn_qkv_specs(tq, T):
    base = (IN_COLS - 3 * SB_WIDTH - QKV_OFF) // LANES
    nb = SB_WIDTH // LANES
    return [pl.BlockSpec((tq, LANES), lambda p, i: (i, base + p)),
            pl.BlockSpec((T, LANES), lambda p, i: (0, base + nb + p)),
            pl.BlockSpec((T, LANES), lambda p, i: (0, base + 2 * nb + p))]


class _Hosted:
    def __init__(self, ins, out_shapes, n_sems, copies):
        self.ins, self.out_shapes, self.n_sems, self.copies = ins, out_shapes, n_sems, copies

    @property
    def n(self):
        return len(self.ins)

    def sems(self):
        return [pltpu.SemaphoreType.DMA((self.n_sems,)), pltpu.SemaphoreType.DMA((self.n_sems,))]

    def start(self, src, dst, ssem, rsem):
        for send, _ in self.copies(src, dst, ssem, rsem):
            send.start()

    def wait(self, src, dst, ssem, rsem):
        for send, recv in self.copies(src, dst, ssem, rsem):
            recv.wait_recv()
            send.wait_send()


def _host(hosted, refs, n_in, n_out, first, last):
    if hosted is None:
        return refs, lambda: None, lambda: None
    n = hosted.n
    own_in, h_in = refs[:n_in], refs[n_in:n_in + n]
    own_out, h_out = refs[n_in + n:n_in + n + n_out], refs[n_in + n + n_out:n_in + 2 * n + n_out]
    rest = refs[n_in + 2 * n + n_out:]
    ssem, rsem = rest[-2:]

    def start():
        if first is True:
            hosted.start(h_in, h_out, ssem, rsem)
        else:
            pl.when(first)(lambda: hosted.start(h_in, h_out, ssem, rsem))

    def wait():
        if last is True:
            hosted.wait(h_in, h_out, ssem, rsem)
        else:
            pl.when(last)(lambda: hosted.wait(h_in, h_out, ssem, rsem))

    return own_in + own_out + rest[:-2], start, wait


def _attn_fwd(qkv, hosted=None):
    T = qkv.shape[0]
    tq = _tile(T, ATTN_TILE)
    n_p, nq = SB_WIDTH // LANES, T // tq

    def body(*refs):
        p, i = pl.program_id(0), pl.program_id(1)
        (q_ref, k_ref, v_ref, o_ref), start, wait = _host(
            hosted, refs, 3, 1, jnp.logical_and(p == 0, i == 0), jnp.logical_and(p == n_p - 1, i == nq - 1))
        start()
        lane = lax.broadcasted_iota(jnp.int32, (tq, LANES), 1)
        row = lax.broadcasted_iota(jnp.int32, (tq, tq), 0)
        col = lax.broadcasted_iota(jnp.int32, (tq, tq), 1)
        after = jnp.where(row > col, 1.0, 0.0).astype(BF16)
        valid = col < row
        q = q_ref[...].astype(F32)
        qh = [jnp.where((lane // SB_HD) == hh, q * SB_SCALE, 0.0).astype(BF16) for hh in range(2)]

        def tiles(todo, state):
            chains = [(t, hh) for t in range(len(todo)) for hh in range(2)]
            kv = []
            for j, _ in todo:
                ks = pl.ds(pl.multiple_of(j * tq, tq), tq)
                kv.append((k_ref[ks, :], v_ref[ks, :]))
            z = {(t, hh): _nt(qh[hh], kv[t][0]) for t, hh in chains}
            lb, lmb, lm_sum = {}, {}, {}
            for t, hh in chains:
                def logits(rows, z=z[(t, hh)], mask=todo[t][1]):
                    lb, lm = _sb_logits(z[rows])
                    if mask is not None:
                        lm = jnp.where(mask[rows], lm, 0.0)
                    return lb, lm.astype(BF16), jnp.sum(lm, axis=1, keepdims=True)

                lb[(t, hh)], lmb[(t, hh)], lm_sum[(t, hh)] = _by_strips(tq, logits)
            x = {c: _nn(lmb[c], after) for c in chains}
            carry = [state[hh][0] for hh in range(2)]
            acc = [state[hh][1] for hh in range(2)]
            for t, hh in chains:
                def weights(rows, lb=lb[(t, hh)], x=x[(t, hh)], carry=carry[hh], mask=todo[t][1]):
                    a = jnp.exp(lb[rows] + x[rows] + carry[rows])
                    if mask is not None:
                        a = jnp.where(mask[rows], a, 0.0)
                    return (a.astype(BF16),)

                (ab,) = _by_strips(tq, weights)
                acc[hh] = acc[hh] + _nn(ab, kv[t][1])
                carry[hh] = carry[hh] + lm_sum[(t, hh)]
            return tuple((carry[hh], acc[hh]) for hh in range(2))

        def live(state):
            return jnp.maximum(jnp.max(state[0][0]), jnp.max(state[1][0]))

        zero = (jnp.zeros((tq, 1), F32), jnp.zeros((tq, LANES), F32))
        state = tiles([(i, valid), (jnp.maximum(i - 1, 0), jnp.broadcast_to(i > 0, (tq, tq)))], (zero, zero))

        def cond(st):
            return jnp.logical_and(st[0] >= 0, st[2] > UNDERFLOW)

        def step(st):
            state = tiles([(st[0], None)], st[1])
            return st[0] - 1, state, live(state)

        _, state, _ = lax.while_loop(cond, step, (i - 2, state, live(state)))
        o_ref[...] = jnp.where(lane < SB_HD, state[0][1], state[1][1])
        wait()

    h_ins = hosted.ins if hosted else []
    res = pl.pallas_call(
        body, name="attn_fwd_hosting" if hosted else "attn_fwd", grid=(n_p, nq),
        in_specs=_attn_qkv_specs(tq, T) + [HBM_SPEC] * len(h_ins),
        out_specs=[pl.BlockSpec((tq, LANES), lambda p, i: (i, p))] + [HBM_SPEC] * len(h_ins),
        out_shape=[jax.ShapeDtypeStruct((T, SB_WIDTH), F32)] + (hosted.out_shapes if hosted else []),
        scratch_shapes=hosted.sems() if hosted else [],
        compiler_params=_params(has_side_effects=hosted is not None),
    )(qkv, qkv, qkv, *h_ins)
    return res[0], res[1:]


def _attn_bwd(qkv, o, dymix, hosted=None):
    T = qkv.shape[0]
    tq = _tile(T, ATTN_TILE)
    n_p, nq = SB_WIDTH // LANES, T // tq
    yc_blk = (POOL_WIDTH + SG_WIDTH) // LANES

    def body(*refs):
        p, i = pl.program_id(0), pl.program_id(1)
        (q_ref, k_ref, v_ref, o_ref, do_ref, dq_ref, dk_ref, dv_ref), start, wait = _host(
            hosted, refs, 5, 3, jnp.logical_and(p == 0, i == 0), jnp.logical_and(p == n_p - 1, i == nq - 1))
        start()
        lane = lax.broadcasted_iota(jnp.int32, (tq, LANES), 1)
        row = lax.broadcasted_iota(jnp.int32, (tq, tq), 0)
        col = lax.broadcasted_iota(jnp.int32, (tq, tq), 1)
        after = jnp.where(row > col, 1.0, 0.0).astype(BF16)
        from_here = jnp.where(row >= col, 1.0, 0.0).astype(BF16)
        from_here2 = jnp.concatenate([from_here, from_here], axis=0)
        valid = col < row

        @pl.when(i == 0)
        def _():
            dk_ref[...] = jnp.zeros_like(dk_ref)
            dv_ref[...] = jnp.zeros_like(dv_ref)

        q = q_ref[...].astype(F32)
        ov = o_ref[...]
        dov = do_ref[...]
        heads = [(lane // SB_HD) == hh for hh in range(2)]
        qh = [jnp.where(h, q * SB_SCALE, 0.0).astype(BF16) for h in heads]
        dohb = [jnp.where(h, dov, 0.0).astype(BF16) for h in heads]
        delta = [jnp.sum(d.astype(F32) * ov, axis=1, keepdims=True) for d in dohb]

        def tiles(todo, state):
            chains = [(t, hh) for t in range(len(todo)) for hh in range(2)]
            kv, where = [], []
            for j, _ in todo:
                ks = pl.ds(pl.multiple_of(j * tq, tq), tq)
                where.append(ks)
                kv.append((k_ref[ks, :], v_ref[ks, :]))
            z = {(t, hh): _nt(qh[hh], kv[t][0]) for t, hh in chains}
            da = {(t, hh): _nt(dohb[hh], kv[t][1]) for t, hh in chains}
            lb, lmb, lm_sum = {}, {}, {}
            for t, hh in chains:
                def logits(rows, z=z[(t, hh)], mask=todo[t][1]):
                    lb, lm = _sb_logits(z[rows])
                    if mask is not None:
                        lm = jnp.where(mask[rows], lm, 0.0)
                    return lb, lm.astype(BF16), jnp.sum(lm, axis=1, keepdims=True)

                lb[(t, hh)], lmb[(t, hh)], lm_sum[(t, hh)] = _by_strips(tq, logits)
            x = {c: _nn(lmb[c], after) for c in chains}
            c_a = [state[hh][0] for hh in range(2)]
            ab, g, g_split, g_sum = {}, {}, {}, {}
            for t, hh in chains:
                def weights(rows, lb=lb[(t, hh)], x=x[(t, hh)], da=da[(t, hh)], c_a=c_a[hh], mask=todo[t][1]):
                    a = jnp.exp(lb[rows] + x[rows] + c_a[rows])
                    if mask is not None:
                        a = jnp.where(mask[rows], a, 0.0)
                    ab = a.astype(BF16)
                    g = da[rows] * ab.astype(F32)
                    hi = g.astype(BF16)
                    lo = (g - hi.astype(F32)).astype(BF16)
                    return ab, g, jnp.concatenate([hi, lo], axis=1), jnp.sum(g, axis=1, keepdims=True)

                ab[(t, hh)], g[(t, hh)], g_split[(t, hh)], g_sum[(t, hh)] = _by_strips(tq, weights)
                c_a[hh] = c_a[hh] + lm_sum[(t, hh)]
            right = {c: _nn(g_split[c], from_here2) for c in chains}
            c_r = [state[hh][1] for hh in range(2)]
            dzb = {}
            for t, hh in chains:
                def logit_grads(rows, lb=lb[(t, hh)], g=g[(t, hh)], right=right[(t, hh)], c_r=c_r[hh], hh=hh,
                                mask=todo[t][1]):
                    sig = jnp.exp(lb[rows])
                    left = delta[hh][rows] - (c_r[rows] + right[rows])
                    dz = g[rows] * (1.0 - sig) - left * sig
                    if mask is not None:
                        dz = jnp.where(mask[rows], dz, 0.0)
                    return (dz.astype(BF16),)

                (dzb[(t, hh)],) = _by_strips(tq, logit_grads)
                c_r[hh] = c_r[hh] + g_sum[(t, hh)]
            dqa = [state[hh][2] for hh in range(2)]
            for t in range(len(todo)):
                dk_ref[where[t], :] += _tn(dzb[(t, 0)], qh[0]) + _tn(dzb[(t, 1)], qh[1])
                dv_ref[where[t], :] += _tn(ab[(t, 0)], dohb[0]) + _tn(ab[(t, 1)], dohb[1])
                for hh in range(2):
                    dqa[hh] = dqa[hh] + _nn(dzb[(t, hh)], kv[t][0])
            return tuple((c_a[hh], c_r[hh], dqa[hh]) for hh in range(2))

        def live(state):
            return jnp.maximum(jnp.max(state[0][0]), jnp.max(state[1][0]))

        zero = (jnp.zeros((tq, 1), F32), jnp.zeros((tq, 1), F32), jnp.zeros((tq, LANES), F32))
        state = tiles([(i, valid), (jnp.maximum(i - 1, 0), jnp.broadcast_to(i > 0, (tq, tq)))], (zero, zero))

        def cond(st):
            return jnp.logical_and(st[0] >= 0, st[2] > UNDERFLOW)

        def step(st):
            state = tiles([(st[0], None)], st[1])
            return st[0] - 1, state, live(state)

        _, state, _ = lax.while_loop(cond, step, (i - 2, state, live(state)))
        dq_ref[...] = (jnp.where(lane < SB_HD, state[0][2], state[1][2]) * SB_SCALE).astype(BF16)
        wait()

    h_ins = hosted.ins if hosted else []
    res = pl.pallas_call(
        body, name="attn_bwd_hosting" if hosted else "attn_bwd", grid=(n_p, nq),
        in_specs=_attn_qkv_specs(tq, T) + [pl.BlockSpec((tq, LANES), lambda p, i: (i, p)),
                                           pl.BlockSpec((tq, LANES), lambda p, i: (i, yc_blk + p))]
        + [HBM_SPEC] * len(h_ins),
        out_specs=[pl.BlockSpec((tq, LANES), lambda p, i: (i, p)), pl.BlockSpec((T, LANES), lambda p, i: (0, p)),
                   pl.BlockSpec((T, LANES), lambda p, i: (0, p))] + [HBM_SPEC] * len(h_ins),
        out_shape=[jax.ShapeDtypeStruct((T, SB_WIDTH), BF16)] + [jax.ShapeDtypeStruct((T, SB_WIDTH), F32)] * 2
        + (hosted.out_shapes if hosted else []),
        scratch_shapes=hosted.sems() if hosted else [],
        compiler_params=_params(has_side_effects=hosted is not None),
    )(qkv, qkv, qkv, o, dymix, *h_ins)
    return res[:3], res[3:]


def _outproj_fwd(x, ya, yb, yc, w):
    T, D = x.shape
    tt = _tile(T, 512)

    def body(x_ref, ya_ref, yb_ref, yc_ref, w_ref, x1_ref, ymix_ref):
        ymix_ref[:, 0:POOL_WIDTH] = ya_ref[...].astype(BF16)
        ymix_ref[:, POOL_WIDTH:POOL_WIDTH + SG_WIDTH] = yb_ref[...].astype(BF16)
        ymix_ref[:, POOL_WIDTH + SG_WIDTH:] = yc_ref[...].astype(BF16)
        x1_ref[...] = x_ref[...] + _nn(ymix_ref[...], w_ref[...])

    row = lambda width: pl.BlockSpec((tt, width), lambda i: (i, 0))
    return pl.pallas_call(
        body, name="outproj_fwd", grid=(T // tt,),
        in_specs=[row(D), row(POOL_WIDTH), row(SG_WIDTH), row(SB_WIDTH), pl.BlockSpec((D, D), lambda i: (0, 0))],
        out_specs=[row(D), row(D)],
        out_shape=[jax.ShapeDtypeStruct((T, D), F32), jax.ShapeDtypeStruct((T, D), BF16)],
        compiler_params=_params(),
    )(x, ya, yb, yc, w)


def _nt_matmul(a, w):
    T, N = a.shape
    K = w.shape[0]
    tt = _tile(T, 512)

    def body(a_ref, w_ref, o_ref):
        o_ref[...] = _nt(a_ref[...].astype(BF16), w_ref[...])

    return pl.pallas_call(
        body, name="nt_matmul", grid=(T // tt,),
        in_specs=[pl.BlockSpec((tt, N), lambda i: (i, 0)), pl.BlockSpec((K, N), lambda i: (0, 0))],
        out_specs=pl.BlockSpec((tt, K), lambda i: (i, 0)),
        out_shape=jax.ShapeDtypeStruct((T, K), F32),
        compiler_params=_params(),
    )(a, w)


def _tn_matmul(a, b, name, n_split=1, hosted=None):
    T, K = a.shape
    N = b.shape[1]
    tk = _tile(K, 1024)
    tn = _tile(N // n_split, 1024)
    tt = _tile(T, 2048)
    nper = N // n_split // tn
    nk, nn, nt = K // tk, N // tn, T // tt

    def body(*refs):
        k, n, t = pl.program_id(0), pl.program_id(1), pl.program_id(2)
        (a_ref, b_ref, o_ref), start, wait = _host(
            hosted, refs, 2, 1, jnp.logical_and(jnp.logical_and(k == 0, n == 0), t == 0),
            jnp.logical_and(jnp.logical_and(k == nk - 1, n == nn - 1), t == nt - 1))
        start()

        @pl.when(t == 0)
        def _():
            o_ref[...] = jnp.zeros_like(o_ref)

        o_ref[...] += _tn(a_ref[...], b_ref[...].astype(BF16))
        wait()

    h_ins = hosted.ins if hosted else []
    res = pl.pallas_call(
        body, name=name + "_hosting" if hosted else name, grid=(nk, nn, nt),
        in_specs=[pl.BlockSpec((tt, tk), lambda k, n, t: (t, k)), pl.BlockSpec((tt, tn), lambda k, n, t: (t, n))]
        + [HBM_SPEC] * len(h_ins),
        out_specs=[pl.BlockSpec((None, tk, tn), lambda k, n, t: (n // nper, k, n % nper))] + [HBM_SPEC] * len(h_ins),
        out_shape=[jax.ShapeDtypeStruct((n_split, K, N // n_split), F32)] + (hosted.out_shapes if hosted else []),
        scratch_shapes=hosted.sems() if hosted else [],
        compiler_params=_params(has_side_effects=hosted is not None),
    )(a, b, *h_ins)
    return (res[0], res[1:]) if hosted else res[0]


def _mlp_fwd(x, g, w_up, w_down, hosted=None):
    T, D = x.shape
    nc, _, fc = w_up.shape
    F = nc * fc
    tt = _tile(T, 1024)
    nt = T // tt

    def body(*refs):
        i, c = pl.program_id(0), pl.program_id(1)
        (x_ref, g_ref, wu_ref, wd_ref, y_ref, h_ref, u_ref, a_ref), start, wait = _host(
            hosted, refs, 4, 4, jnp.logical_and(i == 0, c == 0), jnp.logical_and(i == nt - 1, c == nc - 1))
        start()

        @pl.when(c == 0)
        def _():
            xv = x_ref[...]
            h, _, _ = _rms_fwd(xv, g_ref[...])
            h_ref[...] = h.astype(BF16)
            y_ref[...] = xv

        u = _nn(h_ref[...], wu_ref[...])
        u_ref[...] = u.astype(BF16)
        a = jnp.square(jnp.maximum(u, 0.0)).astype(BF16)
        a_ref[...] = a
        y_ref[...] += _nn(a, wd_ref[...])
        wait()

    h_ins = hosted.ins if hosted else []
    res = pl.pallas_call(
        body, name="mlp_fwd_hosting" if hosted else "mlp_fwd", grid=(nt, nc),
        in_specs=[pl.BlockSpec((tt, D), lambda i, c: (i, 0)), pl.BlockSpec((1, D), lambda i, c: (0, 0)),
                  pl.BlockSpec((None, D, fc), lambda i, c: (c, 0, 0)), pl.BlockSpec((fc, D), lambda i, c: (c, 0))]
        + [HBM_SPEC] * len(h_ins),
        out_specs=[pl.BlockSpec((tt, D), lambda i, c: (i, 0)), pl.BlockSpec((tt, D), lambda i, c: (i, 0)),
                   pl.BlockSpec((tt, fc), lambda i, c: (i, c)), pl.BlockSpec((tt, fc), lambda i, c: (i, c))]
        + [HBM_SPEC] * len(h_ins),
        out_shape=[jax.ShapeDtypeStruct((T, D), F32), jax.ShapeDtypeStruct((T, D), BF16),
                   jax.ShapeDtypeStruct((T, F), BF16), jax.ShapeDtypeStruct((T, F), BF16)]
        + (hosted.out_shapes if hosted else []),
        scratch_shapes=hosted.sems() if hosted else [],
        compiler_params=_params(has_side_effects=hosted is not None),
    )(x, g, w_up, w_down, *h_ins)
    return res[:4], res[4:]


def _mlp_bwd(dy, x, g, u, w_up, w_down, hosted=None):
    T, D = x.shape
    nc, _, fc = w_up.shape
    F = nc * fc
    tt = _tile(T, 1024)
    nt = T // tt

    def body(*refs):
        i, c = pl.program_id(0), pl.program_id(1)
        (dy_ref, x_ref, g_ref, u_ref, wu_ref, wd_ref, dx_ref, du_ref, dg_ref, dyb_ref, dh_ref), start, wait = _host(
            hosted, refs, 6, 3, jnp.logical_and(i == 0, c == 0), jnp.logical_and(i == nt - 1, c == nc - 1))
        start()

        @pl.when(c == 0)
        def _():
            dyb_ref[...] = dy_ref[...].astype(BF16)
            dh_ref[...] = jnp.zeros_like(dh_ref)

        @pl.when(jnp.logical_and(i == 0, c == 0))
        def _():
            dg_ref[...] = jnp.zeros_like(dg_ref)

        da = _nt(dyb_ref[...], wd_ref[...])
        du = (da * (2.0 * jnp.maximum(u_ref[...].astype(F32), 0.0))).astype(BF16)
        du_ref[...] = du
        dh_ref[...] += _nt(du, wu_ref[...])

        @pl.when(c == nc - 1)
        def _():
            gv = g_ref[...]
            _, xhat, r = _rms_fwd(x_ref[...], gv)
            dx, dgrow = _rms_bwd(dh_ref[...], xhat, r, gv)
            dx_ref[...] = dy_ref[...] + dx
            dg_ref[...] += jnp.sum(dgrow, axis=0, keepdims=True)

        wait()

    h_ins = hosted.ins if hosted else []
    res = pl.pallas_call(
        body, name="mlp_bwd_hosting" if hosted else "mlp_bwd", grid=(nt, nc),
        in_specs=[pl.BlockSpec((tt, D), lambda i, c: (i, 0)), pl.BlockSpec((tt, D), lambda i, c: (i, 0)),
                  pl.BlockSpec((1, D), lambda i, c: (0, 0)), pl.BlockSpec((tt, fc), lambda i, c: (i, c)),
                  pl.BlockSpec((None, D, fc), lambda i, c: (c, 0, 0)), pl.BlockSpec((fc, D), lambda i, c: (c, 0))]
        + [HBM_SPEC] * len(h_ins),
        out_specs=[pl.BlockSpec((tt, D), lambda i, c: (i, 0)), pl.BlockSpec((tt, fc), lambda i, c: (i, c)),
                   pl.BlockSpec((1, D), lambda i, c: (0, 0))] + [HBM_SPEC] * len(h_ins),
        out_shape=[jax.ShapeDtypeStruct((T, D), F32), jax.ShapeDtypeStruct((T, F), BF16),
                   jax.ShapeDtypeStruct((1, D), F32)] + (hosted.out_shapes if hosted else []),
        scratch_shapes=[pltpu.VMEM((tt, D), BF16), pltpu.VMEM((tt, D), F32)] + (hosted.sems() if hosted else []),
        compiler_params=_params(has_side_effects=hosted is not None),
    )(dy, x, g, u, w_up, w_down, *h_ins)
    return res[:3], res[3:]


def _loss_head(x, g, target):
    T, D = x.shape
    tt = _tile(T, 512)

    def body(x_ref, g_ref, t_ref, loss_ref, dx_ref, dg_ref):
        gv = g_ref[...]
        y, xhat, r = _rms_fwd(x_ref[...], gv)
        err = y - t_ref[...]
        dx, dgrow = _rms_bwd(err * (1.0 / D), xhat, r, gv)
        dx_ref[...] = dx

        @pl.when(pl.program_id(0) == 0)
        def _():
            loss_ref[...] = jnp.zeros_like(loss_ref)
            dg_ref[...] = jnp.zeros_like(dg_ref)

        loss_ref[...] += 0.5 * jnp.sum(jnp.mean(err * err, axis=-1, keepdims=True), axis=0, keepdims=True)
        dg_ref[...] += jnp.sum(dgrow, axis=0, keepdims=True)

    return pl.pallas_call(
        body, name="loss_head", grid=(T // tt,),
        in_specs=[pl.BlockSpec((tt, D), lambda i: (i, 0)), pl.BlockSpec((1, D), lambda i: (0, 0)),
                  pl.BlockSpec((tt, D), lambda i: (i, 0))],
        out_specs=[pl.BlockSpec((1, LANES), lambda i: (0, 0)), pl.BlockSpec((tt, D), lambda i: (i, 0)),
                   pl.BlockSpec((1, D), lambda i: (0, 0))],
        out_shape=[jax.ShapeDtypeStruct((1, LANES), F32), jax.ShapeDtypeStruct((T, D), F32),
                   jax.ShapeDtypeStruct((1, D), F32)],
        compiler_params=_params(),
    )(x, g, target)


def _rows(shape, pref=512):
    last = shape[-1]
    rows = 1
    for s in shape[:-1]:
        rows *= s
    tr = rows
    if rows * last > 256 * 1024:
        for cand in (pref, 256, 128, 64, 32, 16, 8):
            if rows % cand == 0:
                tr = cand
                break
    return rows, last, tr


def _elementwise(fn, name, ins, n_out, out_dtype=F32):
    shape = ins[0].shape
    rows, last, tr = _rows(shape)
    flat = [a.reshape(rows, last) for a in ins]
    n_in = len(ins)

    def body(*refs):
        res = fn(*[r[...] for r in refs[:n_in]])
        if n_out == 1:
            res = (res,)
        for r, v in zip(refs[n_in:], res):
            r[...] = v.astype(r.dtype)

    spec = pl.BlockSpec((tr, last), lambda i: (i, 0))
    outs = pl.pallas_call(
        body, name=name, grid=(rows // tr,),
        in_specs=[spec] * n_in, out_specs=[spec] * n_out,
        out_shape=[jax.ShapeDtypeStruct((rows, last), out_dtype)] * n_out,
        compiler_params=_params(),
    )(*flat)
    return [o.reshape(shape) for o in outs]


def _add_pair(g, o, c_idx):
    nq, R, C = g.shape
    h = R // 2
    tr = _tile(h, 512)
    nb = h // tr

    def body(c_ref, g_ref, o_ref, out_ref):
        out_ref[...] = g_ref[...] + o_ref[...]

    return pl.pallas_call(
        body, name="add_pair",
        grid_spec=pltpu.PrefetchScalarGridSpec(
            num_scalar_prefetch=1, grid=(nq, nb),
            in_specs=[pl.BlockSpec((None, tr, C), lambda q, i, c: (q, c[0] * nb + i, 0)),
                      pl.BlockSpec((None, tr, C), lambda q, i, c: (q, i, 0))],
            out_specs=pl.BlockSpec((None, tr, C), lambda q, i, c: (q, i, 0))),
        out_shape=jax.ShapeDtypeStruct((nq, h, C), F32),
        compiler_params=_params(),
    )(c_idx.astype(jnp.int32).reshape(1), g, o)


def _add_chips(p, r, q_idx):
    _, H, C = p.shape
    tr = _tile(H, 512)

    def body(q_ref, p_ref, r0_ref, r1_ref, r2_ref, out_ref):
        out_ref[...] = (p_ref[...] + r0_ref[...]) + (r1_ref[...] + r2_ref[...])

    def arrived(k):
        return pl.BlockSpec((None, tr, C), lambda i, q: (k, i, 0))

    return pl.pallas_call(
        body, name="add_chips",
        grid_spec=pltpu.PrefetchScalarGridSpec(
            num_scalar_prefetch=1, grid=(H // tr,),
            in_specs=[pl.BlockSpec((None, tr, C), lambda i, q: (q[0], i, 0)), arrived(0), arrived(1), arrived(2)],
            out_specs=pl.BlockSpec((tr, C), lambda i, q: (i, 0))),
        out_shape=jax.ShapeDtypeStruct((H, C), F32),
        compiler_params=_params(),
    )(q_idx.astype(jnp.int32).reshape(1), p, r, r, r)


def _adamw(w, g, m, v):
    m = ADAM_B1 * m + (1.0 - ADAM_B1) * g
    v = ADAM_B2 * v + (1.0 - ADAM_B2) * jnp.square(g)
    m_hat = m / (1.0 - ADAM_B1 ** ADAM_STEP)
    v_hat = v / (1.0 - ADAM_B2 ** ADAM_STEP)
    delta = -ADAM_LR * (m_hat / (jnp.sqrt(v_hat) + ADAM_EPS) + ADAM_WD * w)
    return delta, m, v


def _place():
    x, y, c = lax.axis_index("x"), lax.axis_index("y"), lax.axis_index("c")
    chips = [(1 - x, y), (x, 1 - y), (1 - x, 1 - y)]
    return x, y, c, chips


def _remote(src, dst, ssem, rsem, k, dev):
    return pltpu.make_async_remote_copy(src_ref=src, dst_ref=dst, send_sem=ssem.at[k], recv_sem=rsem.at[k],
                                        device_id=dev, device_id_type=MESH)


def _gather_weights(shards):
    n = len(shards)
    halves = [s.shape[1] // 2 for s in shards]

    def body(*refs):
        src, out = refs[:n], refs[n:2 * n]
        ssem, rsem = refs[2 * n:]
        x, y, c, chips = _place()
        me_q = 2 * x + y
        sib = (x, y, 1 - c)

        def half(a, q, cc):
            return out[a].at[q, :, pl.ds(cc * halves[a], halves[a]), :]

        first = []
        for a in range(n):
            mine = src[a].at[:, pl.ds(c * halves[a], halves[a]), :]
            for r, chip in enumerate(chips):
                first.append(_remote(mine, half(a, me_q, c), ssem, rsem, a * 3 + r, (*chip, c)))
        for cp in first:
            cp.start()
        passed = []
        for a in range(n):
            for r, chip in enumerate(chips):
                q = 2 * chip[0] + chip[1]
                k = a * 3 + r
                _remote(half(a, q, c), half(a, q, c), ssem, rsem, k, (*chip, c)).wait_recv()
                cp = _remote(half(a, q, c), half(a, q, c), ssem, rsem, 3 * n + k, sib)
                cp.start()
                passed.append(cp)
        for a in range(n):
            for r, chip in enumerate(chips):
                q = 2 * chip[0] + chip[1]
                _remote(half(a, q, 1 - c), half(a, q, 1 - c), ssem, rsem, 3 * n + a * 3 + r, sib).wait_recv()
        for cp in first + passed:
            cp.wait_send()

    return pl.pallas_call(
        body, name="gather_weights",
        in_specs=[HBM_SPEC] * n, out_specs=[HBM_SPEC] * n,
        out_shape=[jax.ShapeDtypeStruct((N_CHIPS,) + s.shape, s.dtype) for s in shards],
        scratch_shapes=[pltpu.SemaphoreType.DMA((6 * n,)), pltpu.SemaphoreType.DMA((6 * n,))],
        compiler_params=_params(has_side_effects=True),
    )(*shards)


def _gather_over_ici(shards):
    n = len(shards)
    halves = [s.shape[1] // 2 for s in shards]

    def copies(src, out, ssem, rsem):
        x, y, c, chips = _place()
        me_q = 2 * x + y
        res = []
        for a in range(n):
            rows = pl.ds(c * halves[a], halves[a])
            mine = src[a].at[:, rows, :]
            for r, chip in enumerate(chips):
                dev = (*chip, c)
                res.append((_remote(mine, out[a].at[me_q, :, rows, :], ssem, rsem, a * 3 + r, dev),
                            _remote(mine, out[a].at[2 * chip[0] + chip[1], :, rows, :], ssem, rsem, a * 3 + r, dev)))
        return res

    return _Hosted(list(shards), [jax.ShapeDtypeStruct((N_CHIPS,) + s.shape, s.dtype) for s in shards], 3 * n, copies)


def _pass_to_sibling(gathered):
    n = len(gathered)
    halves = [g.shape[2] // 2 for g in gathered]

    def body(*refs):
        out = refs[n:2 * n]
        ssem, rsem = refs[2 * n:]
        x, y, c, chips = _place()
        sib = (x, y, 1 - c)

        def half(a, q, cc):
            return out[a].at[q, :, pl.ds(cc * halves[a], halves[a]), :]

        cps = []
        for a in range(n):
            for r, chip in enumerate(chips):
                q = 2 * chip[0] + chip[1]
                cps.append(_remote(half(a, q, c), half(a, q, c), ssem, rsem, a * 3 + r, sib))
        for cp in cps:
            cp.start()
        for a in range(n):
            for r, chip in enumerate(chips):
                q = 2 * chip[0] + chip[1]
                _remote(half(a, q, 1 - c), half(a, q, 1 - c), ssem, rsem, a * 3 + r, sib).wait_recv()
        for cp in cps:
            cp.wait_send()

    return pl.pallas_call(
        body, name="pass_to_sibling",
        in_specs=[HBM_SPEC] * n, out_specs=[HBM_SPEC] * n,
        out_shape=[jax.ShapeDtypeStruct(g.shape, g.dtype) for g in gathered],
        input_output_aliases={a: a for a in range(n)},
        scratch_shapes=[pltpu.SemaphoreType.DMA((3 * n,)), pltpu.SemaphoreType.DMA((3 * n,))],
        compiler_params=_params(has_side_effects=True),
    )(*gathered)


def _scatter_over_ici(parts):
    n = len(parts)

    def copies(src, out, ssem, rsem):
        x, y, c, chips = _place()
        res = []
        for a in range(n):
            for r, chip in enumerate(chips):
                cp = _remote(src[a].at[2 * chip[0] + chip[1]], out[a].at[r], ssem, rsem, a * 3 + r, (*chip, c))
                res.append((cp, cp))
        return res

    return _Hosted(list(parts), [jax.ShapeDtypeStruct((3,) + p.shape[1:], F32) for p in parts], 3 * n, copies)


def _swap_over_d2d(grads):
    n = len(grads)
    halves = [g.shape[1] // 2 for g in grads]

    def copies(src, out, ssem, rsem):
        x, y, c, _ = _place()
        res = []
        for a in range(n):
            cp = _remote(src[a].at[:, pl.ds((1 - c) * halves[a], halves[a]), :], out[a], ssem, rsem, a, (x, y, 1 - c))
            res.append((cp, cp))
        return res

    return _Hosted(list(grads), [jax.ShapeDtypeStruct((N_CHIPS, h, g.shape[2]), F32) for g, h in zip(grads, halves)],
                   n, copies)


def _swap_halves(grads):
    n = len(grads)
    halves = [g.shape[1] // 2 for g in grads]

    def body(*refs):
        src, out = refs[:n], refs[n:2 * n]
        ssem, rsem = refs[2 * n:]
        x, y, c, _ = _place()
        cps = [_remote(src[a].at[:, pl.ds((1 - c) * halves[a], halves[a]), :], out[a], ssem, rsem, a, (x, y, 1 - c))
               for a in range(n)]
        for cp in cps:
            cp.start()
        for cp in cps:
            cp.wait()

    return pl.pallas_call(
        body, name="swap_halves",
        in_specs=[HBM_SPEC] * n, out_specs=[HBM_SPEC] * n,
        out_shape=[jax.ShapeDtypeStruct((N_CHIPS, h, g.shape[2]), F32) for g, h in zip(grads, halves)],
        scratch_shapes=[pltpu.SemaphoreType.DMA((n,)), pltpu.SemaphoreType.DMA((n,))],
        compiler_params=_params(has_side_effects=True),
    )(*grads)


def _scatter_chips(parts):
    n = len(parts)

    def body(*refs):
        src, out = refs[:n], refs[n:2 * n]
        ssem, rsem = refs[2 * n:]
        x, y, c, chips = _place()
        cps = []
        for a in range(n):
            for r, chip in enumerate(chips):
                cps.append(_remote(src[a].at[2 * chip[0] + chip[1]], out[a].at[r], ssem, rsem, a * 3 + r, (*chip, c)))
        for cp in cps:
            cp.start()
        for cp in cps:
            cp.wait()

    return pl.pallas_call(
        body, name="scatter_chips",
        in_specs=[HBM_SPEC] * n, out_specs=[HBM_SPEC] * n,
        out_shape=[jax.ShapeDtypeStruct((3,) + p.shape[1:], F32) for p in parts],
        scratch_shapes=[pltpu.SemaphoreType.DMA((3 * n,)), pltpu.SemaphoreType.DMA((3 * n,))],
        compiler_params=_params(has_side_effects=True),
    )(*parts)


def _swap_reduced_over_d2d(reduced):
    n = len(reduced)

    def copies(src, out, ssem, rsem):
        x, y, c, _ = _place()
        res = []
        for a in range(n):
            cp = _remote(src[a], out[a], ssem, rsem, a, (x, y, 1 - c))
            res.append((cp, cp))
        return res

    return _Hosted(list(reduced), [jax.ShapeDtypeStruct(r.shape, F32) for r in reduced], n, copies)


def _swap_reduced(reduced):
    n = len(reduced)

    def body(*refs):
        src, out = refs[:n], refs[n:2 * n]
        ssem, rsem = refs[2 * n:]
        x, y, c, _ = _place()
        cps = [_remote(src[a], out[a], ssem, rsem, a, (x, y, 1 - c)) for a in range(n)]
        for cp in cps:
            cp.start()
        for cp in cps:
            cp.wait()

    return pl.pallas_call(
        body, name="swap_reduced",
        in_specs=[HBM_SPEC] * n, out_specs=[HBM_SPEC] * n,
        out_shape=[jax.ShapeDtypeStruct(r.shape, F32) for r in reduced],
        scratch_shapes=[pltpu.SemaphoreType.DMA((n,)), pltpu.SemaphoreType.DMA((n,))],
        compiler_params=_params(has_side_effects=True),
    )(*reduced)


def _allreduce_small(buf, hosted=None):
    R, L = buf.shape

    def body(*refs):
        (buf_ref, out_ref, pair_ref, chip_ref, ssem, rsem), start, wait = _host(hosted, refs, 1, 1, True, True)
        start()
        x, y, c, chips = _place()
        me_q = 2 * x + y
        pair_ref[c] = buf_ref[...]
        to_sib = _remote(buf_ref, pair_ref.at[c], ssem, rsem, 0, (x, y, 1 - c))
        to_sib.start()
        _remote(buf_ref, pair_ref.at[1 - c], ssem, rsem, 0, (x, y, 1 - c)).wait_recv()
        chip_ref[me_q] = pair_ref[0] + pair_ref[1]
        cps = [_remote(chip_ref.at[me_q], chip_ref.at[me_q], ssem, rsem, 1 + r, (*chip, c))
               for r, chip in enumerate(chips)]
        for cp in cps:
            cp.start()
        for r, chip in enumerate(chips):
            q = 2 * chip[0] + chip[1]
            _remote(chip_ref.at[q], chip_ref.at[q], ssem, rsem, 1 + r, (*chip, c)).wait_recv()
        out_ref[...] = (chip_ref[0] + chip_ref[1]) + (chip_ref[2] + chip_ref[3])
        to_sib.wait_send()
        for cp in cps:
            cp.wait_send()
        wait()

    h_ins = hosted.ins if hosted else []
    res = pl.pallas_call(
        body, name="allreduce_small",
        in_specs=[VMEM_SPEC] + [HBM_SPEC] * len(h_ins), out_specs=[VMEM_SPEC] + [HBM_SPEC] * len(h_ins),
        out_shape=[jax.ShapeDtypeStruct((R, L), F32)] + (hosted.out_shapes if hosted else []),
        scratch_shapes=[pltpu.VMEM((2, R, L), F32), pltpu.VMEM((N_CHIPS, R, L), F32),
                        pltpu.SemaphoreType.DMA((4,)), pltpu.SemaphoreType.DMA((4,))]
        + (hosted.sems() if hosted else []),
        compiler_params=_params(has_side_effects=True),
    )(buf, *h_ins)
    return res[0], res[1:]


def _pack(arrays):
    flat = jnp.concatenate([a.reshape(-1) for a in arrays])
    pad = (-flat.shape[0]) % (8 * LANES)
    return jnp.pad(flat, (0, pad)).reshape(-1, LANES)


def _unpack(buf, like):
    flat = buf.reshape(-1)
    out, off = [], 0
    for a in like:
        out.append(flat[off:off + a.size].reshape(a.shape))
        off += a.size
    return out


def _block_diag(pw):
    rows = []
    for gi in range(len(POOL_WINDOWS)):
        blocks = [pw[gi] if gj == gi else jnp.zeros_like(pw[gi]) for gj in range(len(POOL_WINDOWS))]
        rows.append(jnp.concatenate(blocks, axis=1))
    return jnp.concatenate(rows, axis=0)


def kernel(x, norm1, w_in, pool_w, pool_scale, sg_norm, sg_w, sg_b, w_out, norm2, w_up, w_down, final_norm, loss_target, m_norm1, m_w_in, m_pool_w, m_pool_scale, m_sg_norm, m_sg_w, m_sg_b, m_w_out, m_norm2, m_w_up, m_w_down, m_final_norm, v_norm1, v_w_in, v_pool_w, v_pool_scale, v_sg_norm, v_sg_w, v_sg_b, v_w_out, v_norm2, v_w_up, v_w_down, v_final_norm):
    depth = norm1.shape[0]
    T = x.shape[1]
    xs = x.reshape(T, D_MODEL)
    target = loss_target.reshape(T, D_MODEL)

    assert depth == 2
    c_idx = lax.axis_index("c")
    q_idx = 2 * lax.axis_index("x") + lax.axis_index("y")
    own = [w.astype(BF16) for w in (w_in, w_out, w_up, w_down)]
    gathered = {(0, 0): _gather_weights([own[0][:1]])[0]}

    def full(a, l, axis):
        blocks = [jnp.where(q_idx == q, own[a][l], gathered[(a, l)][q, 0]) for q in range(N_CHIPS)]
        return jnp.concatenate(blocks, axis=axis)

    def gather_behind(call, keys):
        res, over_ici = call(_gather_over_ici([own[a][l:l + 1] for a, l in keys]))
        gathered.update(zip(keys, _pass_to_sibling(over_ici)))
        return res

    tril = jnp.tril(jnp.ones((CHUNK, CHUNK), F32))
    saved = []
    cur = xs
    wi, wo, wu, wd = {}, {}, {}, {}
    for l in range(depth):
        wbd = _block_diag(pool_w[l]).astype(BF16)
        wm = sg_w[l] * tril
        wm_s = wm.reshape(SG_HEADS * CHUNK, CHUNK).astype(BF16)
        wmt_s = jnp.swapaxes(wm, 1, 2).reshape(SG_HEADS * CHUNK, CHUNK).astype(BF16)
        bias = jnp.repeat(sg_b[l].T, SB_HD, axis=1)
        n1, n2 = norm1[l][None], norm2[l][None]
        psc, sgn = pool_scale[l][None], sg_norm[l][None]
        wi[l] = full(0, l, 1)
        proj, h, qkv = _inproj_fwd(cur, n1, wi[l])
        ya = _pool_fwd(proj, wbd, psc)
        yb = _sg_fwd(proj, wm_s, bias, sgn)
        if l == 0:
            yc = gather_behind(lambda hosted: _attn_fwd(qkv, hosted), [(1, 0), (2, 0), (3, 0)])
        else:
            yc, _ = _attn_fwd(qkv)
        wo[l], wd[l] = full(1, l, 0), full(3, l, 0)
        wu[l] = full(2, l, 0).reshape(N_CHIPS, D_MODEL, -1, MLP_CHUNK).transpose(0, 2, 1, 3).reshape(
            -1, D_MODEL, MLP_CHUNK)
        x1, ymix = _outproj_fwd(cur, ya, yb, yc, wo[l])
        if l == 0:
            x2, h2, u, act = gather_behind(lambda hosted: _mlp_fwd(x1, n2, wu[l], wd[l], hosted),
                                           [(0, 1), (1, 1), (2, 1), (3, 1)])
        else:
            (x2, h2, u, act), _ = _mlp_fwd(x1, n2, wu[l], wd[l])
        saved.append(dict(x0=cur, x1=x1, proj=proj, h=h, qkv=qkv, yc=yc, ymix=ymix, h2=h2, u=u, act=act,
                          wbd=wbd, wm_s=wm_s, wmt_s=wmt_s, bias=bias, n1=n1, n2=n2, psc=psc, sgn=sgn))
        cur = x2

    loss_row, dcur, d_final = _loss_head(cur, final_norm[None], target)

    small = [None] * depth
    grads, parts, reduced = {}, {}, {}

    def pair_up(keys, swapped):
        parts.update({k: _add_pair(grads[k], o, c_idx) for k, o in zip(keys, swapped)})

    def chip_up(keys, arrived):
        reduced.update({k: _add_chips(parts[k], r, q_idx) for k, r in zip(keys, arrived)})

    for l in reversed(range(depth)):
        s = saved[l]
        if l == 0:
            keys = [(2, 1), (3, 1)]
            (dx1, du, d_n2), arrived = _mlp_bwd(dcur, s["x1"], s["n2"], s["u"], wu[l], wd[l],
                                                _scatter_over_ici([parts[k] for k in keys]))
            chip_up(keys, arrived)
        else:
            (dx1, du, d_n2), _ = _mlp_bwd(dcur, s["x1"], s["n2"], s["u"], wu[l], wd[l])
        grads[(2, l)] = _tn_matmul(s["h2"], du, "grad_w_up", n_split=N_CHIPS)
        grads[(3, l)] = _tn_matmul(s["act"], dcur, "grad_w_down")[0].reshape(N_CHIPS, D_FF // N_CHIPS, D_MODEL)
        dymix = _nt_matmul(dx1, wo[l])
        grads[(1, l)] = _tn_matmul(s["ymix"], dx1, "grad_w_out")[0].reshape(N_CHIPS, D_MODEL // N_CHIPS, D_MODEL)
        da_in, d_wbd, d_psc = _pool_bwd(s["proj"], dymix, s["wbd"], s["psc"])
        if l == 0:
            keys = [(0, 1), (1, 0), (2, 0), (3, 0)]
            (du_pre, dv_pre, d_wm, d_bias, d_sgn), swapped = _sg_bwd(
                s["proj"], dymix, s["wm_s"], s["wmt_s"], s["bias"], s["sgn"], _swap_over_d2d([grads[k] for k in keys]))
            pair_up(keys, swapped)
            keys = [(1, 1)] + keys
            (dq, dk, dv), arrived = _attn_bwd(s["qkv"], s["yc"], dymix, _scatter_over_ici([parts[k] for k in keys]))
            chip_up(keys, arrived)
        else:
            (du_pre, dv_pre, d_wm, d_bias, d_sgn), _ = _sg_bwd(s["proj"], dymix, s["wm_s"], s["wmt_s"], s["bias"], s["sgn"])
            keys = [(1, l), (2, l), (3, l)]
            (dq, dk, dv), swapped = _attn_bwd(s["qkv"], s["yc"], dymix, _swap_over_d2d([grads[k] for k in keys]))
            pair_up(keys, swapped)
        dproj, dx0, d_n1 = _inproj_bwd([da_in, du_pre, dv_pre, dq, dk, dv], wi[l], s["x0"], s["n1"], dx1)
        if l == 0:
            keys = sorted(reduced)
            g_in_l, swapped = _tn_matmul(s["h"], dproj, "grad_w_in",
                                         hosted=_swap_reduced_over_d2d([reduced[k] for k in keys]))
            theirs = dict(zip(keys, swapped))
        else:
            g_in_l = _tn_matmul(s["h"], dproj, "grad_w_in")
        grads[(0, l)] = g_in_l[0].reshape(D_MODEL, N_CHIPS, IN_COLS // N_CHIPS).transpose(1, 0, 2)
        d_pw = jnp.stack([d_wbd[gi * POOL_GW:(gi + 1) * POOL_GW, gi * POOL_GW:(gi + 1) * POOL_GW]
                          for gi in range(len(POOL_WINDOWS))])
        small[l] = dict(norm1=d_n1[0], pool_w=d_pw, pool_scale=d_psc[0], sg_norm=d_sgn[0],
                        sg_w=d_wm.reshape(SG_HEADS, CHUNK, CHUNK), sg_b=d_bias[:, :SG_HEADS].T, norm2=d_n2[0])
        dcur = dx0
    grad_x = dcur.reshape(x.shape)

    names = ["norm1", "pool_w", "pool_scale", "sg_norm", "sg_w", "sg_b", "norm2"]
    slot = jnp.zeros((1,), F32)
    small_w = [norm1, pool_w, pool_scale, sg_norm, sg_w, sg_b, norm2, final_norm, slot]
    small_m = [m_norm1, m_pool_w, m_pool_scale, m_sg_norm, m_sg_w, m_sg_b, m_norm2, m_final_norm, slot]
    small_v = [v_norm1, v_pool_w, v_pool_scale, v_sg_norm, v_sg_w, v_sg_b, v_norm2, v_final_norm, slot]
    small_g = [jnp.stack([small[l][k] for l in range(depth)]) for k in names] + [d_final[0], loss_row[0, :1]]
    keys = [(0, 0)]
    pair_up(keys, _swap_halves([grads[k] for k in keys]))
    g_packed, arrived = _allreduce_small(_pack(small_g), _scatter_over_ici([parts[k] for k in keys]))
    chip_up(keys, arrived)
    theirs.update(zip(keys, _swap_reduced([reduced[k] for k in keys])))

    def joined(a):
        layers = []
        for l in range(depth):
            mine, other = reduced[(a, l)], theirs[(a, l)]
            layers.append(jnp.where(c_idx == 0, jnp.concatenate([mine, other]), jnp.concatenate([other, mine])))
        return jnp.stack(layers)

    gw_in, gw_out, gw_up, gw_down = [joined(a) for a in range(4)]

    loss = _unpack(g_packed, small_w)[-1][0]
    s_delta, s_m, s_v = _elementwise(_adamw, "adamw_small", [_pack(small_w), g_packed, _pack(small_m), _pack(small_v)], 3)
    gs = dict(zip(names + ["final_norm"], _unpack(g_packed, small_w)))
    ds = dict(zip(names + ["final_norm"], _unpack(s_delta, small_w)))
    ms = dict(zip(names + ["final_norm"], _unpack(s_m, small_w)))
    vs = dict(zip(names + ["final_norm"], _unpack(s_v, small_w)))

    big_g = dict(w_in=gw_in, w_out=gw_out, w_up=gw_up, w_down=gw_down)
    big_w = dict(w_in=(w_in, m_w_in, v_w_in), w_out=(w_out, m_w_out, v_w_out),
                 w_up=(w_up, m_w_up, v_w_up), w_down=(w_down, m_w_down, v_w_down))
    for k, (w, m, v) in big_w.items():
        operands = [w, big_g[k], m, v]
        if k == "w_in":
            operands = [jnp.swapaxes(o, 1, 2) for o in operands]
        ds[k], ms[k], vs[k] = _elementwise(_adamw, "adamw_" + k, operands, 3)
        if k == "w_in":
            ds[k], ms[k], vs[k] = [jnp.swapaxes(o, 1, 2) for o in (ds[k], ms[k], vs[k])]
        gs[k] = big_g[k]

    order = ["norm1", "w_in", "pool_w", "pool_scale", "sg_norm", "sg_w", "sg_b", "w_out", "norm2", "w_up", "w_down",
             "final_norm"]
    return (loss, grad_x, *[gs[k] for k in order], *[ds[k] for k in order], *[ms[k] for k in order],
            *[vs[k] for k in order])
```

```python
import functools

import jax
import jax.numpy as jnp
from jax import lax
from jax.experimental import pallas as pl
from jax.experimental.pallas import tpu as pltpu

F32 = jnp.float32
BF16 = jnp.bfloat16
MESH = pl.DeviceIdType.MESH
AXES = ("x", "y", "c")

EPS = 1e-6
D_MODEL = 1024
POOL_WIDTH = 256
SG_WIDTH = 256
SB_WIDTH = 512
POOL_WINDOWS = (2, 4, 8, 16)
POOL_GW = 64
POOL_HALO = 16
CHUNK = 128
SG_HEADS = 4
SB_HD = 64
SB_SCALE = 0.125
IN_COLS = 2304
QKV_OFF = 768
D_FF = 4096
N_CHIPS = 4
LANES = 128
VMEM_LIMIT = 56 * 1024 * 1024
MLP_CHUNK = 512
ATTN_TILE = 256
UNDERFLOW = -104.0

ADAM_LR = 0.001
ADAM_B1 = 0.9
ADAM_B2 = 0.999
ADAM_EPS = 1e-08
ADAM_WD = 0.01
ADAM_STEP = 10

HBM_SPEC = pl.BlockSpec(memory_space=pl.ANY)
VMEM_SPEC = pl.BlockSpec(memory_space=pltpu.VMEM)


def _params(**kw):
    return pltpu.CompilerParams(vmem_limit_bytes=VMEM_LIMIT, **kw)


def _tile(n, pref):
    if n <= pref:
        return n
    for t in range(pref - pref % LANES, 0, -LANES):
        if n % t == 0:
            return t
    raise ValueError((n, pref))


def _nn(a, b):
    return jnp.dot(a, b, preferred_element_type=F32)


def _nt(a, b):
    return lax.dot_general(a, b, (((1,), (1,)), ((), ())), preferred_element_type=F32)


def _tn(a, b):
    return lax.dot_general(a, b, (((0,), (0,)), ((), ())), preferred_element_type=F32)


def _rms_fwd(x, g):
    r = lax.rsqrt(jnp.mean(x * x, axis=-1, keepdims=True) + EPS)
    xhat = x * r
    return xhat * g, xhat, r


def _rms_bwd(dy, xhat, r, g):
    dxhat = dy * g
    dx = r * (dxhat - xhat * jnp.mean(dxhat * xhat, axis=-1, keepdims=True))
    return dx, dy * xhat


_GELU_K = 0.7978845608028654
_GELU_C = 0.044715


def _gelu(x):
    return 0.5 * x * (1.0 + jnp.tanh(_GELU_K * (x + _GELU_C * x * x * x)))


def _gelu_grad(x):
    t = jnp.tanh(_GELU_K * (x + _GELU_C * x * x * x))
    return 0.5 * (1.0 + t) + 0.5 * x * (1.0 - t * t) * _GELU_K * (1.0 + 3.0 * _GELU_C * x * x)


def _inproj_fwd(x, g, w):
    T, D = x.shape
    N = w.shape[1]
    tt = _tile(T, 512)

    def body(x_ref, g_ref, w_ref, proj_ref, h_ref, qkv_ref):
        h, _, _ = _rms_fwd(x_ref[...], g_ref[...])
        hb = h.astype(BF16)
        h_ref[...] = hb
        p = _nn(hb, w_ref[...])
        proj_ref[...] = p[:, :QKV_OFF]
        qkv_ref[...] = p[:, QKV_OFF:].astype(BF16)

    return pl.pallas_call(
        body, name="inproj_fwd", grid=(T // tt,),
        in_specs=[pl.BlockSpec((tt, D), lambda i: (i, 0)), pl.BlockSpec((1, D), lambda i: (0, 0)),
                  pl.BlockSpec((D, N), lambda i: (0, 0))],
        out_specs=[pl.BlockSpec((tt, QKV_OFF), lambda i: (i, 0)), pl.BlockSpec((tt, D), lambda i: (i, 0)),
                   pl.BlockSpec((tt, N - QKV_OFF), lambda i: (i, 0))],
        out_shape=[jax.ShapeDtypeStruct((T, QKV_OFF), F32), jax.ShapeDtypeStruct((T, D), BF16),
                   jax.ShapeDtypeStruct((T, N - QKV_OFF), BF16)],
        compiler_params=_params(),
    )(x, g, w)


def _inproj_bwd(pieces, w, x, g, dres):
    T, D = x.shape
    N = w.shape[1]
    tt = _tile(T, 512)
    widths = [p.shape[1] for p in pieces]
    offs = [sum(widths[:k]) for k in range(len(widths))]
    assert sum(widths) == N
    n_p = len(pieces)

    def body(*refs):
        p_refs = refs[:n_p]
        w_ref, x_ref, g_ref, dres_ref, dproj_ref, dx_ref, dg_ref = refs[n_p:]
        for p_ref, o, wd in zip(p_refs, offs, widths):
            dproj_ref[:, o:o + wd] = p_ref[...].astype(BF16)
        dh = _nt(dproj_ref[...], w_ref[...])
        gv = g_ref[...]
        _, xhat, r = _rms_fwd(x_ref[...], gv)
        dx, dgrow = _rms_bwd(dh, xhat, r, gv)
        dx_ref[...] = dres_ref[...] + dx

        @pl.when(pl.program_id(0) == 0)
        def _():
            dg_ref[...] = jnp.zeros_like(dg_ref)

        dg_ref[...] += jnp.sum(dgrow, axis=0, keepdims=True)

    return pl.pallas_call(
        body, name="inproj_bwd", grid=(T // tt,),
        in_specs=[pl.BlockSpec((tt, wd), lambda i: (i, 0)) for wd in widths] + [
            pl.BlockSpec((D, N), lambda i: (0, 0)), pl.BlockSpec((tt, D), lambda i: (i, 0)),
            pl.BlockSpec((1, D), lambda i: (0, 0)), pl.BlockSpec((tt, D), lambda i: (i, 0))],
        out_specs=[pl.BlockSpec((tt, N), lambda i: (i, 0)), pl.BlockSpec((tt, D), lambda i: (i, 0)),
                   pl.BlockSpec((1, D), lambda i: (0, 0))],
        out_shape=[jax.ShapeDtypeStruct((T, N), BF16), jax.ShapeDtypeStruct((T, D), F32),
                   jax.ShapeDtypeStruct((1, D), F32)],
        compiler_params=_params(),
    )(*pieces, w, x, g, dres)


def _pool_select(s2, s4, s8, s16, grp):
    return jnp.where(grp == 0, s2, jnp.where(grp == 1, s4, jnp.where(grp == 2, s8, s16)))


def _pool_count(t_glob, grp):
    win = jnp.where(grp == 0, 2, jnp.where(grp == 1, 4, jnp.where(grp == 2, 8, 16)))
    return jnp.minimum(t_glob + 1, win).astype(F32)


def _pool_diff(a, halo, base, tt):
    n = tt + POOL_HALO
    ext = jnp.concatenate([halo, a], axis=0)
    s2 = ext + pltpu.roll(ext, 1, 0)
    s4 = s2 + pltpu.roll(s2, 2, 0)
    s8 = s4 + pltpu.roll(s4, 4, 0)
    s16 = s8 + pltpu.roll(s8, 8, 0)
    grp = lax.broadcasted_iota(jnp.int32, (n, POOL_WIDTH), 1) // POOL_GW
    t_glob = lax.broadcasted_iota(jnp.int32, (n, POOL_WIDTH), 0) + (base - POOL_HALO)
    pooled = _pool_select(s2, s4, s8, s16, grp) / _pool_count(t_glob, grp)
    return pooled[POOL_HALO:] - a


def _pool_specs(T, tt):
    hb = tt // POOL_HALO
    return [pl.BlockSpec((tt, POOL_WIDTH), lambda i: (i, 0)),
            pl.BlockSpec((POOL_HALO, POOL_WIDTH), lambda i: (jnp.maximum(i * hb - 1, 0), 0))]


def _pool_fwd(proj, wbd, scale):
    T = proj.shape[0]
    tt = _tile(T, 512)

    def body(a_ref, halo_ref, w_ref, sc_ref, y_ref):
        i = pl.program_id(0)
        halo = jnp.where(i > 0, halo_ref[...], 0.0)
        d = _pool_diff(a_ref[...], halo, i * tt, tt)
        y_ref[...] = _nn(d.astype(BF16), w_ref[...]) * sc_ref[...]

    return pl.pallas_call(
        body, name="pool_fwd", grid=(T // tt,),
        in_specs=_pool_specs(T, tt) + [pl.BlockSpec((POOL_WIDTH, POOL_WIDTH), lambda i: (0, 0)),
                                       pl.BlockSpec((1, POOL_WIDTH), lambda i: (0, 0))],
        out_specs=pl.BlockSpec((tt, POOL_WIDTH), lambda i: (i, 0)),
        out_shape=jax.ShapeDtypeStruct((T, POOL_WIDTH), F32),
        compiler_params=_params(),
    )(proj, proj, wbd, scale)


def _pool_bwd(proj, dymix, wbd, scale):
    T = proj.shape[0]
    tt = _tile(T, 512)
    hb = tt // POOL_HALO
    nblk = T // tt
    n = tt + POOL_HALO

    def body(a_ref, halo_ref, dy_ref, dyn_ref, w_ref, sc_ref, da_ref, dw_ref, dsc_ref):
        i = pl.program_id(0)
        halo = jnp.where(i > 0, halo_ref[...], 0.0)
        d = _pool_diff(a_ref[...], halo, i * tt, tt)
        db = d.astype(BF16)
        wv = w_ref[...]
        sc = sc_ref[...]
        dy = dy_ref[...]
        dys = dy * sc

        @pl.when(i == 0)
        def _():
            dw_ref[...] = jnp.zeros_like(dw_ref)
            dsc_ref[...] = jnp.zeros_like(dsc_ref)

        dsc_ref[...] += jnp.sum(dy * _nn(db, wv), axis=0, keepdims=True)
        dw_ref[...] += _tn(db, dys.astype(BF16))
        dyn = jnp.where(i < nblk - 1, dyn_ref[...], 0.0) * sc
        dd = _nt(jnp.concatenate([dys, dyn], axis=0).astype(BF16), wv)
        grp = lax.broadcasted_iota(jnp.int32, (n, POOL_WIDTH), 1) // POOL_GW
        t_glob = lax.broadcasted_iota(jnp.int32, (n, POOL_WIDTH), 0) + i * tt
        e = dd / _pool_count(t_glob, grp)
        r2 = e + pltpu.roll(e, n - 1, 0)
        r4 = r2 + pltpu.roll(r2, n - 2, 0)
        r8 = r4 + pltpu.roll(r4, n - 4, 0)
        r16 = r8 + pltpu.roll(r8, n - 8, 0)
        da_ref[...] = (_pool_select(r2, r4, r8, r16, grp) - dd)[:tt].astype(BF16)

    return pl.pallas_call(
        body, name="pool_bwd", grid=(nblk,),
        in_specs=_pool_specs(T, tt) + [
            pl.BlockSpec((tt, POOL_WIDTH), lambda i: (i, 0)),
            pl.BlockSpec((POOL_HALO, POOL_WIDTH), lambda i: (jnp.minimum((i + 1) * hb, T // POOL_HALO - 1), 0)),
            pl.BlockSpec((POOL_WIDTH, POOL_WIDTH), lambda i: (0, 0)), pl.BlockSpec((1, POOL_WIDTH), lambda i: (0, 0))],
        out_specs=[pl.BlockSpec((tt, POOL_WIDTH), lambda i: (i, 0)),
                   pl.BlockSpec((POOL_WIDTH, POOL_WIDTH), lambda i: (0, 0)),
                   pl.BlockSpec((1, POOL_WIDTH), lambda i: (0, 0))],
        out_shape=[jax.ShapeDtypeStruct((T, POOL_WIDTH), BF16),
                   jax.ShapeDtypeStruct((POOL_WIDTH, POOL_WIDTH), F32),
                   jax.ShapeDtypeStruct((1, POOL_WIDTH), F32)],
        compiler_params=_params(),
    )(proj, proj, dymix, dymix, wbd, scale)


def _head_select(stacked, grp):
    out = jnp.where(grp == 0, stacked[0:CHUNK], 0.0)
    for h in range(1, SG_HEADS):
        out = out + jnp.where(grp == h, stacked[h * CHUNK:(h + 1) * CHUNK], 0.0)
    return out


def _sg_specs(tt):
    return [pl.BlockSpec((tt, SG_WIDTH), lambda i: (i, 1)), pl.BlockSpec((tt, SG_WIDTH), lambda i: (i, 2))]


def _sg_fwd(proj, wm, bias, g):
    T = proj.shape[0]
    tt = _tile(T, 512)

    def body(u_ref, v_ref, wm_ref, b_ref, g_ref, y_ref):
        zu = _gelu(u_ref[...])
        vn, _, _ = _rms_fwd(_gelu(v_ref[...]), g_ref[...])
        grp = lax.broadcasted_iota(jnp.int32, (CHUNK, SG_WIDTH), 1) // SB_HD
        for n in range(tt // CHUNK):
            rows = slice(n * CHUNK, (n + 1) * CHUNK)
            sv = _head_select(_nn(wm_ref[...], vn[rows].astype(BF16)), grp) + b_ref[...]
            y_ref[rows, :] = zu[rows] * sv

    return pl.pallas_call(
        body, name="sg_fwd", grid=(T // tt,),
        in_specs=_sg_specs(tt) + [pl.BlockSpec((SG_HEADS * CHUNK, CHUNK), lambda i: (0, 0)),
                                  pl.BlockSpec((CHUNK, SG_WIDTH), lambda i: (0, 0)),
                                  pl.BlockSpec((1, SG_WIDTH), lambda i: (0, 0))],
        out_specs=pl.BlockSpec((tt, SG_WIDTH), lambda i: (i, 0)),
        out_shape=jax.ShapeDtypeStruct((T, SG_WIDTH), F32),
        compiler_params=_params(),
    )(proj, proj, wm, bias, g)


def _sg_bwd(proj, dymix, wm, wmt, bias, g, hosted=None):
    T = proj.shape[0]
    tt = _tile(T, 512)
    nblk = T // tt

    def body(*refs):
        i = pl.program_id(0)
        (u_ref, v_ref, dy_ref, wm_ref, wmt_ref, b_ref, g_ref, du_ref, dv_ref, dw_ref, db_ref, dg_ref,
         dvn_ref, dbias_ref), start, wait = _host(hosted, refs, 7, 5, i == 0, i == nblk - 1)
        start()
        up, vp = u_ref[...], v_ref[...]
        gv = g_ref[...]
        zu, zv = _gelu(up), _gelu(vp)
        vn, xhat, r = _rms_fwd(zv, gv)
        gu = _gelu_grad(up)
        grp = lax.broadcasted_iota(jnp.int32, (CHUNK, SG_WIDTH), 1) // SB_HD

        @pl.when(i == 0)
        def _():
            dw_ref[...] = jnp.zeros_like(dw_ref)
            dbias_ref[...] = jnp.zeros_like(dbias_ref)
            dg_ref[...] = jnp.zeros_like(dg_ref)

        for n in range(tt // CHUNK):
            rows = slice(n * CHUNK, (n + 1) * CHUNK)
            vc = vn[rows].astype(BF16)
            sv = _head_select(_nn(wm_ref[...], vc), grp) + b_ref[...]
            dy = dy_ref[rows, :]
            du_ref[rows, :] = (dy * sv * gu[rows]).astype(BF16)
            dsv = dy * zu[rows]
            dsvb = dsv.astype(BF16)
            dvn_ref[rows, :] = _head_select(_nn(wmt_ref[...], dsvb), grp)
            stacked = jnp.concatenate([jnp.where(grp == h, dsv, 0.0) for h in range(SG_HEADS)], axis=0)
            dw_ref[...] += _nt(stacked.astype(BF16), vc)
            dbias_ref[...] += dsv

        dzv, dgrow = _rms_bwd(dvn_ref[...], xhat, r, gv)
        dg_ref[...] += jnp.sum(dgrow, axis=0, keepdims=True)
        dv_ref[...] = (dzv * _gelu_grad(vp)).astype(BF16)

        @pl.when(i == nblk - 1)
        def _():
            t_i = lax.broadcasted_iota(jnp.int32, (SG_HEADS * CHUNK, CHUNK), 0) % CHUNK
            s_i = lax.broadcasted_iota(jnp.int32, (SG_HEADS * CHUNK, CHUNK), 1)
            dw_ref[...] = jnp.where(s_i <= t_i, dw_ref[...], 0.0)
            lane = lax.broadcasted_iota(jnp.int32, (CHUNK, LANES), 1)
            acc = jnp.zeros((CHUNK, LANES), F32)
            for h in range(SG_HEADS):
                tot = jnp.sum(jnp.where(grp == h, dbias_ref[...], 0.0), axis=1, keepdims=True)
                acc = acc + jnp.where(lane == h, tot, 0.0)
            db_ref[...] = acc

        wait()

    h_ins = hosted.ins if hosted else []
    res = pl.pallas_call(
        body, name="sg_bwd_hosting" if hosted else "sg_bwd", grid=(nblk,),
        in_specs=_sg_specs(tt) + [pl.BlockSpec((tt, SG_WIDTH), lambda i: (i, 1)),
                                  pl.BlockSpec((SG_HEADS * CHUNK, CHUNK), lambda i: (0, 0)),
                                  pl.BlockSpec((SG_HEADS * CHUNK, CHUNK), lambda i: (0, 0)),
                                  pl.BlockSpec((CHUNK, SG_WIDTH), lambda i: (0, 0)),
                                  pl.BlockSpec((1, SG_WIDTH), lambda i: (0, 0))] + [HBM_SPEC] * len(h_ins),
        out_specs=[pl.BlockSpec((tt, SG_WIDTH), lambda i: (i, 0)), pl.BlockSpec((tt, SG_WIDTH), lambda i: (i, 0)),
                   pl.BlockSpec((SG_HEADS * CHUNK, CHUNK), lambda i: (0, 0)),
                   pl.BlockSpec((CHUNK, LANES), lambda i: (0, 0)), pl.BlockSpec((1, SG_WIDTH), lambda i: (0, 0))]
        + [HBM_SPEC] * len(h_ins),
        out_shape=[jax.ShapeDtypeStruct((T, SG_WIDTH), BF16), jax.ShapeDtypeStruct((T, SG_WIDTH), BF16),
                   jax.ShapeDtypeStruct((SG_HEADS * CHUNK, CHUNK), F32),
                   jax.ShapeDtypeStruct((CHUNK, LANES), F32), jax.ShapeDtypeStruct((1, SG_WIDTH), F32)]
        + (hosted.out_shapes if hosted else []),
        scratch_shapes=[pltpu.VMEM((tt, SG_WIDTH), F32), pltpu.VMEM((CHUNK, SG_WIDTH), F32)]
        + (hosted.sems() if hosted else []),
        compiler_params=_params(has_side_effects=hosted is not None),
    )(proj, proj, dymix, wm, wmt, bias, g, *h_ins)
    return res[:5], res[5:]


def _split_dot(x, u):
    hi = x.astype(BF16)
    lo = (x - hi.astype(F32)).astype(BF16)
    return _nn(hi, u) + _nn(lo, u)


def _sb_logits(z):
    lb = jnp.minimum(z, 0.0) - jnp.log(1.0 + jnp.exp(-jnp.abs(z)))
    return lb, lb - z


ATTN_STRIP = 32


def _by_strips(n_rows, fn):
    parts = None
    for r in range(0, n_rows, ATTN_STRIP):
        res = fn(slice(r, r + ATTN_STRIP))
        parts = [[v] for v in res] if parts is None else [p + [v] for p, v in zip(parts, res)]
    return [jnp.concatenate(p, axis=0) for p in parts]


def _attn_qkv_specs(tq, T):
    base = (IN_COLS - 3 * SB_WIDTH - QKV_OFF) // LANES
    nb = SB_WIDTH // LANES
    return [pl.BlockSpec((tq, LANES), lambda p, i: (i, base + p)),
            pl.BlockSpec((T, LANES), lambda p, i: (0, base + nb + p)),
            pl.BlockSpec((T, LANES), lambda p, i: (0, base + 2 * nb + p))]


class _Hosted:
    def __init__(self, ins, out_shapes, n_sems, copies, in_place=False):
        self.ins, self.out_shapes, self.n_sems, self.copies = ins, out_shapes, n_sems, copies
        self.in_place = in_place

    @property
    def n(self):
        return len(self.ins)

    def aliases(self, n_in, n_out):
        return {n_in + k: n_out + k for k in range(self.n)} if self.in_place else {}

    def sems(self):
        return [pltpu.SemaphoreType.DMA((self.n_sems,)), pltpu.SemaphoreType.DMA((self.n_sems,))]

    def start(self, src, dst, ssem, rsem):
        for send, _ in self.copies(src, dst, ssem, rsem):
            send.start()

    def wait(self, src, dst, ssem, rsem):
        for send, recv in self.copies(src, dst, ssem, rsem):
            recv.wait_recv()
            send.wait_send()


def _host(hosted, refs, n_in, n_out, first, last):
    if hosted is None:
        return refs, lambda: None, lambda: None
    n = hosted.n
    own_in, h_in = refs[:n_in], refs[n_in:n_in + n]
    own_out, h_out = refs[n_in + n:n_in + n + n_out], refs[n_in + n + n_out:n_in + 2 * n + n_out]
    rest = refs[n_in + 2 * n + n_out:]
    ssem, rsem = rest[-2:]

    def start():
        if first is True:
            hosted.start(h_in, h_out, ssem, rsem)
        else:
            pl.when(first)(lambda: hosted.start(h_in, h_out, ssem, rsem))

    def wait():
        if last is True:
            hosted.wait(h_in, h_out, ssem, rsem)
        else:
            pl.when(last)(lambda: hosted.wait(h_in, h_out, ssem, rsem))

    return own_in + own_out + rest[:-2], start, wait


def _attn_fwd(qkv, hosted=None):
    T = qkv.shape[0]
    tq = _tile(T, ATTN_TILE)
    n_p, nq = SB_WIDTH // LANES, T // tq

    def body(*refs):
        p, i = pl.program_id(0), pl.program_id(1)
        (q_ref, k_ref, v_ref, o_ref), start, wait = _host(
            hosted, refs, 3, 1, jnp.logical_and(p == 0, i == 0), jnp.logical_and(p == n_p - 1, i == nq - 1))
        start()
        lane = lax.broadcasted_iota(jnp.int32, (tq, LANES), 1)
        row = lax.broadcasted_iota(jnp.int32, (tq, tq), 0)
        col = lax.broadcasted_iota(jnp.int32, (tq, tq), 1)
        after = jnp.where(row > col, 1.0, 0.0).astype(BF16)
        valid = col < row
        q = q_ref[...].astype(F32)
        qh = [jnp.where((lane // SB_HD) == hh, q * SB_SCALE, 0.0).astype(BF16) for hh in range(2)]

        def tiles(todo, state):
            chains = [(t, hh) for t in range(len(todo)) for hh in range(2)]
            kv = []
            for j, _ in todo:
                ks = pl.ds(pl.multiple_of(j * tq, tq), tq)
                kv.append((k_ref[ks, :], v_ref[ks, :]))
            z = {(t, hh): _nt(qh[hh], kv[t][0]) for t, hh in chains}
            lb, lmb, lm_sum = {}, {}, {}
            for t, hh in chains:
                def logits(rows, z=z[(t, hh)], mask=todo[t][1]):
                    lb, lm = _sb_logits(z[rows])
                    if mask is not None:
                        lm = jnp.where(mask[rows], lm, 0.0)
                    return lb, lm.astype(BF16), jnp.sum(lm, axis=1, keepdims=True)

                lb[(t, hh)], lmb[(t, hh)], lm_sum[(t, hh)] = _by_strips(tq, logits)
            x = {c: _nn(lmb[c], after) for c in chains}
            carry = [state[hh][0] for hh in range(2)]
            acc = [state[hh][1] for hh in range(2)]
            for t, hh in chains:
                def weights(rows, lb=lb[(t, hh)], x=x[(t, hh)], carry=carry[hh], mask=todo[t][1]):
                    a = jnp.exp(lb[rows] + x[rows] + carry[rows])
                    if mask is not None:
                        a = jnp.where(mask[rows], a, 0.0)
                    return (a.astype(BF16),)

                (ab,) = _by_strips(tq, weights)
                acc[hh] = acc[hh] + _nn(ab, kv[t][1])
                carry[hh] = carry[hh] + lm_sum[(t, hh)]
            return tuple((carry[hh], acc[hh]) for hh in range(2))

        def live(state):
            return jnp.maximum(jnp.max(state[0][0]), jnp.max(state[1][0]))

        zero = (jnp.zeros((tq, 1), F32), jnp.zeros((tq, LANES), F32))
        state = tiles([(i, valid), (jnp.maximum(i - 1, 0), jnp.broadcast_to(i > 0, (tq, tq)))], (zero, zero))

        def cond(st):
            return jnp.logical_and(st[0] >= 0, st[2] > UNDERFLOW)

        def step(st):
            state = tiles([(st[0], None)], st[1])
            return st[0] - 1, state, live(state)

        _, state, _ = lax.while_loop(cond, step, (i - 2, state, live(state)))
        o_ref[...] = jnp.where(lane < SB_HD, state[0][1], state[1][1])
        wait()

    h_ins = hosted.ins if hosted else []
    res = pl.pallas_call(
        body, name="attn_fwd_hosting" if hosted else "attn_fwd", grid=(n_p, nq),
        in_specs=_attn_qkv_specs(tq, T) + [HBM_SPEC] * len(h_ins),
        out_specs=[pl.BlockSpec((tq, LANES), lambda p, i: (i, p))] + [HBM_SPEC] * len(h_ins),
        out_shape=[jax.ShapeDtypeStruct((T, SB_WIDTH), F32)] + (hosted.out_shapes if hosted else []),
        input_output_aliases=hosted.aliases(3, 1) if hosted else {},
        scratch_shapes=hosted.sems() if hosted else [],
        compiler_params=_params(has_side_effects=hosted is not None),
    )(qkv, qkv, qkv, *h_ins)
    return res[0], res[1:]


def _attn_bwd(qkv, o, dymix, hosted=None):
    T = qkv.shape[0]
    tq = _tile(T, ATTN_TILE)
    n_p, nq = SB_WIDTH // LANES, T // tq
    yc_blk = (POOL_WIDTH + SG_WIDTH) // LANES

    def body(*refs):
        p, i = pl.program_id(0), pl.program_id(1)
        (q_ref, k_ref, v_ref, o_ref, do_ref, dq_ref, dk_ref, dv_ref), start, wait = _host(
            hosted, refs, 5, 3, jnp.logical_and(p == 0, i == 0), jnp.logical_and(p == n_p - 1, i == nq - 1))
        start()
        lane = lax.broadcasted_iota(jnp.int32, (tq, LANES), 1)
        row = lax.broadcasted_iota(jnp.int32, (tq, tq), 0)
        col = lax.broadcasted_iota(jnp.int32, (tq, tq), 1)
        after = jnp.where(row > col, 1.0, 0.0).astype(BF16)
        from_here = jnp.where(row >= col, 1.0, 0.0).astype(BF16)
        from_here2 = jnp.concatenate([from_here, from_here], axis=0)
        valid = col < row

        @pl.when(i == 0)
        def _():
            dk_ref[...] = jnp.zeros_like(dk_ref)
            dv_ref[...] = jnp.zeros_like(dv_ref)

        q = q_ref[...].astype(F32)
        ov = o_ref[...]
        dov = do_ref[...]
        heads = [(lane // SB_HD) == hh for hh in range(2)]
        qh = [jnp.where(h, q * SB_SCALE, 0.0).astype(BF16) for h in heads]
        dohb = [jnp.where(h, dov, 0.0).astype(BF16) for h in heads]
        delta = [jnp.sum(d.astype(F32) * ov, axis=1, keepdims=True) for d in dohb]

        def tiles(todo, state):
            chains = [(t, hh) for t in range(len(todo)) for hh in range(2)]
            kv, where = [], []
            for j, _ in todo:
                ks = pl.ds(pl.multiple_of(j * tq, tq), tq)
                where.append(ks)
                kv.append((k_ref[ks, :], v_ref[ks, :]))
            z = {(t, hh): _nt(qh[hh], kv[t][0]) for t, hh in chains}
            da = {(t, hh): _nt(dohb[hh], kv[t][1]) for t, hh in chains}
            lb, lmb, lm_sum = {}, {}, {}
            for t, hh in chains:
                def logits(rows, z=z[(t, hh)], mask=todo[t][1]):
                    lb, lm = _sb_logits(z[rows])
                    if mask is not None:
                        lm = jnp.where(mask[rows], lm, 0.0)
                    return lb, lm.astype(BF16), jnp.sum(lm, axis=1, keepdims=True)

                lb[(t, hh)], lmb[(t, hh)], lm_sum[(t, hh)] = _by_strips(tq, logits)
            x = {c: _nn(lmb[c], after) for c in chains}
            c_a = [state[hh][0] for hh in range(2)]
            ab, g, g_split, g_sum = {}, {}, {}, {}
            for t, hh in chains:
                def weights(rows, lb=lb[(t, hh)], x=x[(t, hh)], da=da[(t, hh)], c_a=c_a[hh], mask=todo[t][1]):
                    a = jnp.exp(lb[rows] + x[rows] + c_a[rows])
                    if mask is not None:
                        a = jnp.where(mask[rows], a, 0.0)
                    ab = a.astype(BF16)
                    g = da[rows] * ab.astype(F32)
                    hi = g.astype(BF16)
                    lo = (g - hi.astype(F32)).astype(BF16)
                    return ab, g, jnp.concatenate([hi, lo], axis=1), jnp.sum(g, axis=1, keepdims=True)

                ab[(t, hh)], g[(t, hh)], g_split[(t, hh)], g_sum[(t, hh)] = _by_strips(tq, weights)
                c_a[hh] = c_a[hh] + lm_sum[(t, hh)]
            right = {c: _nn(g_split[c], from_here2) for c in chains}
            c_r = [state[hh][1] for hh in range(2)]
            dzb = {}
            for t, hh in chains:
                def logit_grads(rows, lb=lb[(t, hh)], g=g[(t, hh)], right=right[(t, hh)], c_r=c_r[hh], hh=hh,
                                mask=todo[t][1]):
                    sig = jnp.exp(lb[rows])
                    left = delta[hh][rows] - (c_r[rows] + right[rows])
                    dz = g[rows] * (1.0 - sig) - left * sig
                    if mask is not None:
                        dz = jnp.where(mask[rows], dz, 0.0)
                    return (dz.astype(BF16),)

                (dzb[(t, hh)],) = _by_strips(tq, logit_grads)
                c_r[hh] = c_r[hh] + g_sum[(t, hh)]
            dqa = [state[hh][2] for hh in range(2)]
            for t in range(len(todo)):
                dk_ref[where[t], :] += _tn(dzb[(t, 0)], qh[0]) + _tn(dzb[(t, 1)], qh[1])
                dv_ref[where[t], :] += _tn(ab[(t, 0)], dohb[0]) + _tn(ab[(t, 1)], dohb[1])
                for hh in range(2):
                    dqa[hh] = dqa[hh] + _nn(dzb[(t, hh)], kv[t][0])
            return tuple((c_a[hh], c_r[hh], dqa[hh]) for hh in range(2))

        def live(state):
            return jnp.maximum(jnp.max(state[0][0]), jnp.max(state[1][0]))

        zero = (jnp.zeros((tq, 1), F32), jnp.zeros((tq, 1), F32), jnp.zeros((tq, LANES), F32))
        state = tiles([(i, valid), (jnp.maximum(i - 1, 0), jnp.broadcast_to(i > 0, (tq, tq)))], (zero, zero))

        def cond(st):
            return jnp.logical_and(st[0] >= 0, st[2] > UNDERFLOW)

        def step(st):
            state = tiles([(st[0], None)], st[1])
            return st[0] - 1, state, live(state)

        _, state, _ = lax.while_loop(cond, step, (i - 2, state, live(state)))
        dq_ref[...] = (jnp.where(lane < SB_HD, state[0][2], state[1][2]) * SB_SCALE).astype(BF16)
        wait()

    h_ins = hosted.ins if hosted else []
    res = pl.pallas_call(
        body, name="attn_bwd_hosting" if hosted else "attn_bwd", grid=(n_p, nq),
        in_specs=_attn_qkv_specs(tq, T) + [pl.BlockSpec((tq, LANES), lambda p, i: (i, p)),
                                           pl.BlockSpec((tq, LANES), lambda p, i: (i, yc_blk + p))]
        + [HBM_SPEC] * len(h_ins),
        out_specs=[pl.BlockSpec((tq, LANES), lambda p, i: (i, p)), pl.BlockSpec((T, LANES), lambda p, i: (0, p)),
                   pl.BlockSpec((T, LANES), lambda p, i: (0, p))] + [HBM_SPEC] * len(h_ins),
        out_shape=[jax.ShapeDtypeStruct((T, SB_WIDTH), BF16)] + [jax.ShapeDtypeStruct((T, SB_WIDTH), F32)] * 2
        + (hosted.out_shapes if hosted else []),
        scratch_shapes=hosted.sems() if hosted else [],
        compiler_params=_params(has_side_effects=hosted is not None),
    )(qkv, qkv, qkv, o, dymix, *h_ins)
    return res[:3], res[3:]


def _outproj_fwd(x, ya, yb, yc, w, hosted=None):
    T, D = x.shape
    tt = _tile(T, 512)
    nt = T // tt

    def body(*refs):
        i = pl.program_id(0)
        (x_ref, ya_ref, yb_ref, yc_ref, w_ref, x1_ref, ymix_ref), start, wait = _host(
            hosted, refs, 5, 2, i == 0, i == nt - 1)
        start()
        ymix_ref[:, 0:POOL_WIDTH] = ya_ref[...].astype(BF16)
        ymix_ref[:, POOL_WIDTH:POOL_WIDTH + SG_WIDTH] = yb_ref[...].astype(BF16)
        ymix_ref[:, POOL_WIDTH + SG_WIDTH:] = yc_ref[...].astype(BF16)
        x1_ref[...] = x_ref[...] + _nn(ymix_ref[...], w_ref[...])
        wait()

    row = lambda width: pl.BlockSpec((tt, width), lambda i: (i, 0))
    h_ins = hosted.ins if hosted else []
    res = pl.pallas_call(
        body, name="outproj_fwd_hosting" if hosted else "outproj_fwd", grid=(nt,),
        in_specs=[row(D), row(POOL_WIDTH), row(SG_WIDTH), row(SB_WIDTH), pl.BlockSpec((D, D), lambda i: (0, 0))]
        + [HBM_SPEC] * len(h_ins),
        out_specs=[row(D), row(D)] + [HBM_SPEC] * len(h_ins),
        out_shape=[jax.ShapeDtypeStruct((T, D), F32), jax.ShapeDtypeStruct((T, D), BF16)]
        + (hosted.out_shapes if hosted else []),
        input_output_aliases=hosted.aliases(5, 2) if hosted else {},
        scratch_shapes=hosted.sems() if hosted else [],
        compiler_params=_params(has_side_effects=hosted is not None),
    )(x, ya, yb, yc, w, *h_ins)
    return res[:2], res[2:]


def _nt_matmul(a, w):
    T, N = a.shape
    K = w.shape[0]
    tt = _tile(T, 512)

    def body(a_ref, w_ref, o_ref):
        o_ref[...] = _nt(a_ref[...].astype(BF16), w_ref[...])

    return pl.pallas_call(
        body, name="nt_matmul", grid=(T // tt,),
        in_specs=[pl.BlockSpec((tt, N), lambda i: (i, 0)), pl.BlockSpec((K, N), lambda i: (0, 0))],
        out_specs=pl.BlockSpec((tt, K), lambda i: (i, 0)),
        out_shape=jax.ShapeDtypeStruct((T, K), F32),
        compiler_params=_params(),
    )(a, w)


def _tn_matmul(a, b, name, n_split=1, hosted=None):
    T, K = a.shape
    N = b.shape[1]
    tk = _tile(K, 1024)
    tn = _tile(N // n_split, 1024)
    tt = _tile(T, 2048)
    nper = N // n_split // tn
    nk, nn, nt = K // tk, N // tn, T // tt

    def body(*refs):
        k, n, t = pl.program_id(0), pl.program_id(1), pl.program_id(2)
        (a_ref, b_ref, o_ref), start, wait = _host(
            hosted, refs, 2, 1, jnp.logical_and(jnp.logical_and(k == 0, n == 0), t == 0),
            jnp.logical_and(jnp.logical_and(k == nk - 1, n == nn - 1), t == nt - 1))
        start()

        @pl.when(t == 0)
        def _():
            o_ref[...] = jnp.zeros_like(o_ref)

        o_ref[...] += _tn(a_ref[...], b_ref[...].astype(BF16))
        wait()

    h_ins = hosted.ins if hosted else []
    res = pl.pallas_call(
        body, name=name + "_hosting" if hosted else name, grid=(nk, nn, nt),
        in_specs=[pl.BlockSpec((tt, tk), lambda k, n, t: (t, k)), pl.BlockSpec((tt, tn), lambda k, n, t: (t, n))]
        + [HBM_SPEC] * len(h_ins),
        out_specs=[pl.BlockSpec((None, tk, tn), lambda k, n, t: (n // nper, k, n % nper))] + [HBM_SPEC] * len(h_ins),
        out_shape=[jax.ShapeDtypeStruct((n_split, K, N // n_split), F32)] + (hosted.out_shapes if hosted else []),
        scratch_shapes=hosted.sems() if hosted else [],
        compiler_params=_params(has_side_effects=hosted is not None),
    )(a, b, *h_ins)
    return (res[0], res[1:]) if hosted else res[0]


def _mlp_fwd(x, g, w_up, w_down, hosted=None):
    T, D = x.shape
    F = w_up.shape[1]
    tt = _tile(T, 1024)
    fc = _tile(F, MLP_CHUNK)
    nc = F // fc
    nt = T // tt

    def body(*refs):
        i, c = pl.program_id(0), pl.program_id(1)
        (x_ref, g_ref, wu_ref, wd_ref, y_ref, h_ref, u_ref, a_ref), start, wait = _host(
            hosted, refs, 4, 4, jnp.logical_and(i == 0, c == 0), jnp.logical_and(i == nt - 1, c == nc - 1))
        start()

        @pl.when(c == 0)
        def _():
            xv = x_ref[...]
            h, _, _ = _rms_fwd(xv, g_ref[...])
            h_ref[...] = h.astype(BF16)
            y_ref[...] = xv

        u = _nn(h_ref[...], wu_ref[...])
        u_ref[...] = u.astype(BF16)
        a = jnp.square(jnp.maximum(u, 0.0)).astype(BF16)
        a_ref[...] = a
        y_ref[...] += _nn(a, wd_ref[...])
        wait()

    h_ins = hosted.ins if hosted else []
    res = pl.pallas_call(
        body, name="mlp_fwd_hosting" if hosted else "mlp_fwd", grid=(nt, nc),
        in_specs=[pl.BlockSpec((tt, D), lambda i, c: (i, 0)), pl.BlockSpec((1, D), lambda i, c: (0, 0)),
                  pl.BlockSpec((D, fc), lambda i, c: (0, c)), pl.BlockSpec((fc, D), lambda i, c: (c, 0))]
        + [HBM_SPEC] * len(h_ins),
        out_specs=[pl.BlockSpec((tt, D), lambda i, c: (i, 0)), pl.BlockSpec((tt, D), lambda i, c: (i, 0)),
                   pl.BlockSpec((tt, fc), lambda i, c: (i, c)), pl.BlockSpec((tt, fc), lambda i, c: (i, c))]
        + [HBM_SPEC] * len(h_ins),
        out_shape=[jax.ShapeDtypeStruct((T, D), F32), jax.ShapeDtypeStruct((T, D), BF16),
                   jax.ShapeDtypeStruct((T, F), BF16), jax.ShapeDtypeStruct((T, F), BF16)]
        + (hosted.out_shapes if hosted else []),
        scratch_shapes=hosted.sems() if hosted else [],
        compiler_params=_params(has_side_effects=hosted is not None),
    )(x, g, w_up, w_down, *h_ins)
    return res[:4], res[4:]


def _mlp_bwd(dy, x, g, u, w_up, w_down, hosted=None):
    T, D = x.shape
    F = w_up.shape[1]
    tt = _tile(T, 1024)
    fc = _tile(F, MLP_CHUNK)
    nc = F // fc
    nt = T // tt

    def body(*refs):
        i, c = pl.program_id(0), pl.program_id(1)
        (dy_ref, x_ref, g_ref, u_ref, wu_ref, wd_ref, dx_ref, du_ref, dg_ref, dyb_ref, dh_ref), start, wait = _host(
            hosted, refs, 6, 3, jnp.logical_and(i == 0, c == 0), jnp.logical_and(i == nt - 1, c == nc - 1))
        start()

        @pl.when(c == 0)
        def _():
            dyb_ref[...] = dy_ref[...].astype(BF16)
            dh_ref[...] = jnp.zeros_like(dh_ref)

        @pl.when(jnp.logical_and(i == 0, c == 0))
        def _():
            dg_ref[...] = jnp.zeros_like(dg_ref)

        da = _nt(dyb_ref[...], wd_ref[...])
        du = (da * (2.0 * jnp.maximum(u_ref[...].astype(F32), 0.0))).astype(BF16)
        du_ref[...] = du
        dh_ref[...] += _nt(du, wu_ref[...])

        @pl.when(c == nc - 1)
        def _():
            gv = g_ref[...]
            _, xhat, r = _rms_fwd(x_ref[...], gv)
            dx, dgrow = _rms_bwd(dh_ref[...], xhat, r, gv)
            dx_ref[...] = dy_ref[...] + dx
            dg_ref[...] += jnp.sum(dgrow, axis=0, keepdims=True)

        wait()

    h_ins = hosted.ins if hosted else []
    res = pl.pallas_call(
        body, name="mlp_bwd_hosting" if hosted else "mlp_bwd", grid=(nt, nc),
        in_specs=[pl.BlockSpec((tt, D), lambda i, c: (i, 0)), pl.BlockSpec((tt, D), lambda i, c: (i, 0)),
                  pl.BlockSpec((1, D), lambda i, c: (0, 0)), pl.BlockSpec((tt, fc), lambda i, c: (i, c)),
                  pl.BlockSpec((D, fc), lambda i, c: (0, c)), pl.BlockSpec((fc, D), lambda i, c: (c, 0))]
        + [HBM_SPEC] * len(h_ins),
        out_specs=[pl.BlockSpec((tt, D), lambda i, c: (i, 0)), pl.BlockSpec((tt, fc), lambda i, c: (i, c)),
                   pl.BlockSpec((1, D), lambda i, c: (0, 0))] + [HBM_SPEC] * len(h_ins),
        out_shape=[jax.ShapeDtypeStruct((T, D), F32), jax.ShapeDtypeStruct((T, F), BF16),
                   jax.ShapeDtypeStruct((1, D), F32)] + (hosted.out_shapes if hosted else []),
        scratch_shapes=[pltpu.VMEM((tt, D), BF16), pltpu.VMEM((tt, D), F32)] + (hosted.sems() if hosted else []),
        compiler_params=_params(has_side_effects=hosted is not None),
    )(dy, x, g, u, w_up, w_down, *h_ins)
    return res[:3], res[3:]


def _loss_head(x, g, target):
    T, D = x.shape
    tt = _tile(T, 512)

    def body(x_ref, g_ref, t_ref, loss_ref, dx_ref, dg_ref):
        gv = g_ref[...]
        y, xhat, r = _rms_fwd(x_ref[...], gv)
        err = y - t_ref[...]
        dx, dgrow = _rms_bwd(err * (1.0 / D), xhat, r, gv)
        dx_ref[...] = dx

        @pl.when(pl.program_id(0) == 0)
        def _():
            loss_ref[...] = jnp.zeros_like(loss_ref)
            dg_ref[...] = jnp.zeros_like(dg_ref)

        loss_ref[...] += 0.5 * jnp.sum(jnp.mean(err * err, axis=-1, keepdims=True), axis=0, keepdims=True)
        dg_ref[...] += jnp.sum(dgrow, axis=0, keepdims=True)

    return pl.pallas_call(
        body, name="loss_head", grid=(T // tt,),
        in_specs=[pl.BlockSpec((tt, D), lambda i: (i, 0)), pl.BlockSpec((1, D), lambda i: (0, 0)),
                  pl.BlockSpec((tt, D), lambda i: (i, 0))],
        out_specs=[pl.BlockSpec((1, LANES), lambda i: (0, 0)), pl.BlockSpec((tt, D), lambda i: (i, 0)),
                   pl.BlockSpec((1, D), lambda i: (0, 0))],
        out_shape=[jax.ShapeDtypeStruct((1, LANES), F32), jax.ShapeDtypeStruct((T, D), F32),
                   jax.ShapeDtypeStruct((1, D), F32)],
        compiler_params=_params(),
    )(x, g, target)


def _rows(shape, pref=512):
    last = shape[-1]
    rows = 1
    for s in shape[:-1]:
        rows *= s
    tr = rows
    if rows * last > 256 * 1024:
        for cand in (pref, 256, 128, 64, 32, 16, 8):
            if rows % cand == 0:
                tr = cand
                break
    return rows, last, tr


def _elementwise(fn, name, ins, n_out, out_dtype=F32):
    shape = ins[0].shape
    rows, last, tr = _rows(shape)
    flat = [a.reshape(rows, last) for a in ins]
    n_in = len(ins)

    def body(*refs):
        res = fn(*[r[...] for r in refs[:n_in]])
        if n_out == 1:
            res = (res,)
        for r, v in zip(refs[n_in:], res):
            r[...] = v.astype(r.dtype)

    spec = pl.BlockSpec((tr, last), lambda i: (i, 0))
    outs = pl.pallas_call(
        body, name=name, grid=(rows // tr,),
        in_specs=[spec] * n_in, out_specs=[spec] * n_out,
        out_shape=[jax.ShapeDtypeStruct((rows, last), out_dtype)] * n_out,
        compiler_params=_params(),
    )(*flat)
    return [o.reshape(shape) for o in outs]


def _add_pair(g, o, c_idx):
    nq, R, C = g.shape
    h = R // 2
    tr = _tile(h, 512)
    nb = h // tr

    def body(c_ref, g_ref, o_ref, out_ref):
        out_ref[...] = g_ref[...] + o_ref[...]

    return pl.pallas_call(
        body, name="add_pair",
        grid_spec=pltpu.PrefetchScalarGridSpec(
            num_scalar_prefetch=1, grid=(nq, nb),
            in_specs=[pl.BlockSpec((None, tr, C), lambda q, i, c: (q, c[0] * nb + i, 0)),
                      pl.BlockSpec((None, tr, C), lambda q, i, c: (q, i, 0))],
            out_specs=pl.BlockSpec((None, tr, C), lambda q, i, c: (q, i, 0))),
        out_shape=jax.ShapeDtypeStruct((nq, h, C), F32),
        compiler_params=_params(),
    )(c_idx.astype(jnp.int32).reshape(1), g, o)


def _add_chips(p, r, q_idx):
    _, H, C = p.shape
    tr = _tile(H, 512)

    def body(q_ref, p_ref, r0_ref, r1_ref, r2_ref, out_ref):
        out_ref[...] = (p_ref[...] + r0_ref[...]) + (r1_ref[...] + r2_ref[...])

    def arrived(k):
        return pl.BlockSpec((None, tr, C), lambda i, q: (k, i, 0))

    return pl.pallas_call(
        body, name="add_chips",
        grid_spec=pltpu.PrefetchScalarGridSpec(
            num_scalar_prefetch=1, grid=(H // tr,),
            in_specs=[pl.BlockSpec((None, tr, C), lambda i, q: (q[0], i, 0)), arrived(0), arrived(1), arrived(2)],
            out_specs=pl.BlockSpec((tr, C), lambda i, q: (i, 0))),
        out_shape=jax.ShapeDtypeStruct((H, C), F32),
        compiler_params=_params(),
    )(q_idx.astype(jnp.int32).reshape(1), p, r, r, r)


def _adamw(w, g, m, v):
    m = ADAM_B1 * m + (1.0 - ADAM_B1) * g
    v = ADAM_B2 * v + (1.0 - ADAM_B2) * jnp.square(g)
    m_hat = m / (1.0 - ADAM_B1 ** ADAM_STEP)
    v_hat = v / (1.0 - ADAM_B2 ** ADAM_STEP)
    delta = -ADAM_LR * (m_hat / (jnp.sqrt(v_hat) + ADAM_EPS) + ADAM_WD * w)
    return delta, m, v


def _place():
    x, y, c = lax.axis_index("x"), lax.axis_index("y"), lax.axis_index("c")
    chips = [(1 - x, y), (x, 1 - y), (1 - x, 1 - y)]
    return x, y, c, chips


def _remote(src, dst, ssem, rsem, k, dev):
    return pltpu.make_async_remote_copy(src_ref=src, dst_ref=dst, send_sem=ssem.at[k], recv_sem=rsem.at[k],
                                        device_id=dev, device_id_type=MESH)


def _gather_weights(shards):
    n = len(shards)
    halves = [s.shape[1] // 2 for s in shards]

    def body(*refs):
        src, out = refs[:n], refs[n:2 * n]
        ssem, rsem = refs[2 * n:]
        x, y, c, chips = _place()
        me_q = 2 * x + y
        sib = (x, y, 1 - c)

        def half(a, q, cc):
            return out[a].at[q, :, pl.ds(cc * halves[a], halves[a]), :]

        first = []
        for a in range(n):
            mine = src[a].at[:, pl.ds(c * halves[a], halves[a]), :]
            for r, chip in enumerate(chips):
                first.append(_remote(mine, half(a, me_q, c), ssem, rsem, a * 3 + r, (*chip, c)))
        for cp in first:
            cp.start()
        passed = []
        for a in range(n):
            for r, chip in enumerate(chips):
                q = 2 * chip[0] + chip[1]
                k = a * 3 + r
                _remote(half(a, q, c), half(a, q, c), ssem, rsem, k, (*chip, c)).wait_recv()
                cp = _remote(half(a, q, c), half(a, q, c), ssem, rsem, 3 * n + k, sib)
                cp.start()
                passed.append(cp)
        for a in range(n):
            for r, chip in enumerate(chips):
                q = 2 * chip[0] + chip[1]
                _remote(half(a, q, 1 - c), half(a, q, 1 - c), ssem, rsem, 3 * n + a * 3 + r, sib).wait_recv()
        for cp in first + passed:
            cp.wait_send()

    return pl.pallas_call(
        body, name="gather_weights",
        in_specs=[HBM_SPEC] * n, out_specs=[HBM_SPEC] * n,
        out_shape=[jax.ShapeDtypeStruct((N_CHIPS,) + s.shape, s.dtype) for s in shards],
        scratch_shapes=[pltpu.SemaphoreType.DMA((6 * n,)), pltpu.SemaphoreType.DMA((6 * n,))],
        compiler_params=_params(has_side_effects=True),
    )(*shards)


def _gather_over_ici(shards):
    n = len(shards)
    halves = [s.shape[1] // 2 for s in shards]

    def copies(src, out, ssem, rsem):
        x, y, c, chips = _place()
        me_q = 2 * x + y
        res = []
        for a in range(n):
            rows = pl.ds(c * halves[a], halves[a])
            mine = src[a].at[:, rows, :]
            for r, chip in enumerate(chips):
                dev = (*chip, c)
                res.append((_remote(mine, out[a].at[me_q, :, rows, :], ssem, rsem, a * 3 + r, dev),
                            _remote(mine, out[a].at[2 * chip[0] + chip[1], :, rows, :], ssem, rsem, a * 3 + r, dev)))
        return res

    return _Hosted(list(shards), [jax.ShapeDtypeStruct((N_CHIPS,) + s.shape, s.dtype) for s in shards], 3 * n, copies)


def _pass_over_d2d(gathered):
    n = len(gathered)
    halves = [g.shape[2] // 2 for g in gathered]

    def copies(_, out, ssem, rsem):
        x, y, c, chips = _place()
        sib = (x, y, 1 - c)
        res = []
        for a in range(n):
            for r, chip in enumerate(chips):
                q = 2 * chip[0] + chip[1]
                mine = out[a].at[q, :, pl.ds(c * halves[a], halves[a]), :]
                theirs = out[a].at[q, :, pl.ds((1 - c) * halves[a], halves[a]), :]
                res.append((_remote(mine, mine, ssem, rsem, a * 3 + r, sib),
                            _remote(theirs, theirs, ssem, rsem, a * 3 + r, sib)))
        return res

    return _Hosted(list(gathered), [jax.ShapeDtypeStruct(g.shape, g.dtype) for g in gathered], 3 * n, copies,
                   in_place=True)


def _pass_to_sibling(gathered):
    n = len(gathered)
    halves = [g.shape[2] // 2 for g in gathered]

    def body(*refs):
        out = refs[n:2 * n]
        ssem, rsem = refs[2 * n:]
        x, y, c, chips = _place()
        sib = (x, y, 1 - c)

        def half(a, q, cc):
            return out[a].at[q, :, pl.ds(cc * halves[a], halves[a]), :]

        cps = []
        for a in range(n):
            for r, chip in enumerate(chips):
                q = 2 * chip[0] + chip[1]
                cps.append(_remote(half(a, q, c), half(a, q, c), ssem, rsem, a * 3 + r, sib))
        for cp in cps:
            cp.start()
        for a in range(n):
            for r, chip in enumerate(chips):
                q = 2 * chip[0] + chip[1]
                _remote(half(a, q, 1 - c), half(a, q, 1 - c), ssem, rsem, a * 3 + r, sib).wait_recv()
        for cp in cps:
            cp.wait_send()

    return pl.pallas_call(
        body, name="pass_to_sibling",
        in_specs=[HBM_SPEC] * n, out_specs=[HBM_SPEC] * n,
        out_shape=[jax.ShapeDtypeStruct(g.shape, g.dtype) for g in gathered],
        input_output_aliases={a: a for a in range(n)},
        scratch_shapes=[pltpu.SemaphoreType.DMA((3 * n,)), pltpu.SemaphoreType.DMA((3 * n,))],
        compiler_params=_params(has_side_effects=True),
    )(*gathered)


def _scatter_over_ici(parts):
    n = len(parts)

    def copies(src, out, ssem, rsem):
        x, y, c, chips = _place()
        res = []
        for a in range(n):
            for r, chip in enumerate(chips):
                cp = _remote(src[a].at[2 * chip[0] + chip[1]], out[a].at[r], ssem, rsem, a * 3 + r, (*chip, c))
                res.append((cp, cp))
        return res

    return _Hosted(list(parts), [jax.ShapeDtypeStruct((3,) + p.shape[1:], F32) for p in parts], 3 * n, copies)


def _swap_over_d2d(grads):
    n = len(grads)
    halves = [g.shape[1] // 2 for g in grads]

    def copies(src, out, ssem, rsem):
        x, y, c, _ = _place()
        res = []
        for a in range(n):
            cp = _remote(src[a].at[:, pl.ds((1 - c) * halves[a], halves[a]), :], out[a], ssem, rsem, a, (x, y, 1 - c))
            res.append((cp, cp))
        return res

    return _Hosted(list(grads), [jax.ShapeDtypeStruct((N_CHIPS, h, g.shape[2]), F32) for g, h in zip(grads, halves)],
                   n, copies)


def _swap_halves(grads):
    n = len(grads)
    halves = [g.shape[1] // 2 for g in grads]

    def body(*refs):
        src, out = refs[:n], refs[n:2 * n]
        ssem, rsem = refs[2 * n:]
        x, y, c, _ = _place()
        cps = [_remote(src[a].at[:, pl.ds((1 - c) * halves[a], halves[a]), :], out[a], ssem, rsem, a, (x, y, 1 - c))
               for a in range(n)]
        for cp in cps:
            cp.start()
        for cp in cps:
            cp.wait()

    return pl.pallas_call(
        body, name="swap_halves",
        in_specs=[HBM_SPEC] * n, out_specs=[HBM_SPEC] * n,
        out_shape=[jax.ShapeDtypeStruct((N_CHIPS, h, g.shape[2]), F32) for g, h in zip(grads, halves)],
        scratch_shapes=[pltpu.SemaphoreType.DMA((n,)), pltpu.SemaphoreType.DMA((n,))],
        compiler_params=_params(has_side_effects=True),
    )(*grads)


def _scatter_chips(parts):
    n = len(parts)

    def body(*refs):
        src, out = refs[:n], refs[n:2 * n]
        ssem, rsem = refs[2 * n:]
        x, y, c, chips = _place()
        cps = []
        for a in range(n):
            for r, chip in enumerate(chips):
                cps.append(_remote(src[a].at[2 * chip[0] + chip[1]], out[a].at[r], ssem, rsem, a * 3 + r, (*chip, c)))
        for cp in cps:
            cp.start()
        for cp in cps:
            cp.wait()

    return pl.pallas_call(
        body, name="scatter_chips",
        in_specs=[HBM_SPEC] * n, out_specs=[HBM_SPEC] * n,
        out_shape=[jax.ShapeDtypeStruct((3,) + p.shape[1:], F32) for p in parts],
        scratch_shapes=[pltpu.SemaphoreType.DMA((3 * n,)), pltpu.SemaphoreType.DMA((3 * n,))],
        compiler_params=_params(has_side_effects=True),
    )(*parts)


def _swap_reduced_over_d2d(reduced):
    n = len(reduced)

    def copies(src, out, ssem, rsem):
        x, y, c, _ = _place()
        res = []
        for a in range(n):
            cp = _remote(src[a], out[a], ssem, rsem, a, (x, y, 1 - c))
            res.append((cp, cp))
        return res

    return _Hosted(list(reduced), [jax.ShapeDtypeStruct(r.shape, F32) for r in reduced], n, copies)


def _swap_reduced(reduced):
    n = len(reduced)

    def body(*refs):
        src, out = refs[:n], refs[n:2 * n]
        ssem, rsem = refs[2 * n:]
        x, y, c, _ = _place()
        cps = [_remote(src[a], out[a], ssem, rsem, a, (x, y, 1 - c)) for a in range(n)]
        for cp in cps:
            cp.start()
        for cp in cps:
            cp.wait()

    return pl.pallas_call(
        body, name="swap_reduced",
        in_specs=[HBM_SPEC] * n, out_specs=[HBM_SPEC] * n,
        out_shape=[jax.ShapeDtypeStruct(r.shape, F32) for r in reduced],
        scratch_shapes=[pltpu.SemaphoreType.DMA((n,)), pltpu.SemaphoreType.DMA((n,))],
        compiler_params=_params(has_side_effects=True),
    )(*reduced)


def _allreduce_small(buf, hosted=None):
    R, L = buf.shape

    def body(*refs):
        (buf_ref, out_ref, pair_ref, chip_ref, ssem, rsem), start, wait = _host(hosted, refs, 1, 1, True, True)
        start()
        x, y, c, chips = _place()
        me_q = 2 * x + y
        pair_ref[c] = buf_ref[...]
        to_sib = _remote(buf_ref, pair_ref.at[c], ssem, rsem, 0, (x, y, 1 - c))
        to_sib.start()
        _remote(buf_ref, pair_ref.at[1 - c], ssem, rsem, 0, (x, y, 1 - c)).wait_recv()
        chip_ref[me_q] = pair_ref[0] + pair_ref[1]
        cps = [_remote(chip_ref.at[me_q], chip_ref.at[me_q], ssem, rsem, 1 + r, (*chip, c))
               for r, chip in enumerate(chips)]
        for cp in cps:
            cp.start()
        for r, chip in enumerate(chips):
            q = 2 * chip[0] + chip[1]
            _remote(chip_ref.at[q], chip_ref.at[q], ssem, rsem, 1 + r, (*chip, c)).wait_recv()
        out_ref[...] = (chip_ref[0] + chip_ref[1]) + (chip_ref[2] + chip_ref[3])
        to_sib.wait_send()
        for cp in cps:
            cp.wait_send()
        wait()

    h_ins = hosted.ins if hosted else []
    res = pl.pallas_call(
        body, name="allreduce_small",
        in_specs=[VMEM_SPEC] + [HBM_SPEC] * len(h_ins), out_specs=[VMEM_SPEC] + [HBM_SPEC] * len(h_ins),
        out_shape=[jax.ShapeDtypeStruct((R, L), F32)] + (hosted.out_shapes if hosted else []),
        scratch_shapes=[pltpu.VMEM((2, R, L), F32), pltpu.VMEM((N_CHIPS, R, L), F32),
                        pltpu.SemaphoreType.DMA((4,)), pltpu.SemaphoreType.DMA((4,))]
        + (hosted.sems() if hosted else []),
        compiler_params=_params(has_side_effects=True),
    )(buf, *h_ins)
    return res[0], res[1:]


def _pack(arrays):
    flat = jnp.concatenate([a.reshape(-1) for a in arrays])
    pad = (-flat.shape[0]) % (8 * LANES)
    return jnp.pad(flat, (0, pad)).reshape(-1, LANES)


def _unpack(buf, like):
    flat = buf.reshape(-1)
    out, off = [], 0
    for a in like:
        out.append(flat[off:off + a.size].reshape(a.shape))
        off += a.size
    return out


def _block_diag(pw):
    rows = []
    for gi in range(len(POOL_WINDOWS)):
        blocks = [pw[gi] if gj == gi else jnp.zeros_like(pw[gi]) for gj in range(len(POOL_WINDOWS))]
        rows.append(jnp.concatenate(blocks, axis=1))
    return jnp.concatenate(rows, axis=0)


def kernel(x, norm1, w_in, pool_w, pool_scale, sg_norm, sg_w, sg_b, w_out, norm2, w_up, w_down, final_norm, loss_target, m_norm1, m_w_in, m_pool_w, m_pool_scale, m_sg_norm, m_sg_w, m_sg_b, m_w_out, m_norm2, m_w_up, m_w_down, m_final_norm, v_norm1, v_w_in, v_pool_w, v_pool_scale, v_sg_norm, v_sg_w, v_sg_b, v_w_out, v_norm2, v_w_up, v_w_down, v_final_norm):
    depth = norm1.shape[0]
    T = x.shape[1]
    xs = x.reshape(T, D_MODEL)
    target = loss_target.reshape(T, D_MODEL)

    assert depth == 2
    c_idx = lax.axis_index("c")
    q_idx = 2 * lax.axis_index("x") + lax.axis_index("y")
    own = [w.astype(BF16) for w in (w_in, w_out, w_up, w_down)]
    gathered = {(0, 0): _gather_weights([own[0][:1]])[0]}

    def full(a, l, axis):
        blocks = [jnp.where(q_idx == q, own[a][l], gathered[(a, l)][q, 0]) for q in range(N_CHIPS)]
        return jnp.concatenate(blocks, axis=axis)

    half_way = {}

    def gather_behind(call, keys, at_once):
        res, over_ici = call(_gather_over_ici([own[a][l:l + 1] for a, l in keys]))
        gathered.update(zip(keys[:at_once], _pass_to_sibling(over_ici[:at_once])))
        half_way.update(zip(keys[at_once:], over_ici[at_once:]))
        return res

    def pass_behind(call, keys):
        res, done = call(_pass_over_d2d([half_way.pop(k) for k in keys]))
        gathered.update(zip(keys, done))
        return res

    tril = jnp.tril(jnp.ones((CHUNK, CHUNK), F32))
    saved = []
    cur = xs
    wi, wo, wu, wd = {}, {}, {}, {}
    for l in range(depth):
        wbd = _block_diag(pool_w[l]).astype(BF16)
        wm = sg_w[l] * tril
        wm_s = wm.reshape(SG_HEADS * CHUNK, CHUNK).astype(BF16)
        wmt_s = jnp.swapaxes(wm, 1, 2).reshape(SG_HEADS * CHUNK, CHUNK).astype(BF16)
        bias = jnp.repeat(sg_b[l].T, SB_HD, axis=1)
        n1, n2 = norm1[l][None], norm2[l][None]
        psc, sgn = pool_scale[l][None], sg_norm[l][None]
        wi[l] = full(0, l, 1)
        proj, h, qkv = _inproj_fwd(cur, n1, wi[l])
        ya = _pool_fwd(proj, wbd, psc)
        yb = _sg_fwd(proj, wm_s, bias, sgn)
        if l == 0:
            yc = gather_behind(lambda hosted: _attn_fwd(qkv, hosted), [(1, 0), (2, 0), (3, 0)], 1)
            wo[l] = full(1, l, 0)
            x1, ymix = pass_behind(lambda hosted: _outproj_fwd(cur, ya, yb, yc, wo[l], hosted), [(2, 0), (3, 0)])
        else:
            yc = pass_behind(lambda hosted: _attn_fwd(qkv, hosted), [(1, l), (2, l), (3, l)])
            wo[l] = full(1, l, 0)
            (x1, ymix), _ = _outproj_fwd(cur, ya, yb, yc, wo[l])
        wu[l], wd[l] = full(2, l, 1), full(3, l, 0)
        if l == 0:
            x2, h2, u, act = gather_behind(lambda hosted: _mlp_fwd(x1, n2, wu[l], wd[l], hosted),
                                           [(0, 1), (1, 1), (2, 1), (3, 1)], 1)
        else:
            (x2, h2, u, act), _ = _mlp_fwd(x1, n2, wu[l], wd[l])
        saved.append(dict(x0=cur, x1=x1, proj=proj, h=h, qkv=qkv, yc=yc, ymix=ymix, h2=h2, u=u, act=act,
                          wbd=wbd, wm_s=wm_s, wmt_s=wmt_s, bias=bias, n1=n1, n2=n2, psc=psc, sgn=sgn))
        cur = x2

    loss_row, dcur, d_final = _loss_head(cur, final_norm[None], target)

    small = [None] * depth
    grads, parts, reduced = {}, {}, {}

    def pair_up(keys, swapped):
        parts.update({k: _add_pair(grads[k], o, c_idx) for k, o in zip(keys, swapped)})

    def chip_up(keys, arrived):
        reduced.update({k: _add_chips(parts[k], r, q_idx) for k, r in zip(keys, arrived)})

    for l in reversed(range(depth)):
        s = saved[l]
        if l == 0:
            keys = [(2, 1), (3, 1)]
            (dx1, du, d_n2), arrived = _mlp_bwd(dcur, s["x1"], s["n2"], s["u"], wu[l], wd[l],
                                                _scatter_over_ici([parts[k] for k in keys]))
            chip_up(keys, arrived)
        else:
            (dx1, du, d_n2), _ = _mlp_bwd(dcur, s["x1"], s["n2"], s["u"], wu[l], wd[l])
        grads[(2, l)] = _tn_matmul(s["h2"], du, "grad_w_up", n_split=N_CHIPS)
        grads[(3, l)] = _tn_matmul(s["act"], dcur, "grad_w_down")[0].reshape(N_CHIPS, D_FF // N_CHIPS, D_MODEL)
        dymix = _nt_matmul(dx1, wo[l])
        grads[(1, l)] = _tn_matmul(s["ymix"], dx1, "grad_w_out")[0].reshape(N_CHIPS, D_MODEL // N_CHIPS, D_MODEL)
        da_in, d_wbd, d_psc = _pool_bwd(s["proj"], dymix, s["wbd"], s["psc"])
        if l == 0:
            keys = [(0, 1), (1, 0), (2, 0), (3, 0)]
            (du_pre, dv_pre, d_wm, d_bias, d_sgn), swapped = _sg_bwd(
                s["proj"], dymix, s["wm_s"], s["wmt_s"], s["bias"], s["sgn"], _swap_over_d2d([grads[k] for k in keys]))
            pair_up(keys, swapped)
            keys = [(1, 1)] + keys
            (dq, dk, dv), arrived = _attn_bwd(s["qkv"], s["yc"], dymix, _scatter_over_ici([parts[k] for k in keys]))
            chip_up(keys, arrived)
        else:
            (du_pre, dv_pre, d_wm, d_bias, d_sgn), _ = _sg_bwd(s["proj"], dymix, s["wm_s"], s["wmt_s"], s["bias"], s["sgn"])
            keys = [(1, l), (2, l), (3, l)]
            (dq, dk, dv), swapped = _attn_bwd(s["qkv"], s["yc"], dymix, _swap_over_d2d([grads[k] for k in keys]))
            pair_up(keys, swapped)
        dproj, dx0, d_n1 = _inproj_bwd([da_in, du_pre, dv_pre, dq, dk, dv], wi[l], s["x0"], s["n1"], dx1)
        if l == 0:
            keys = sorted(reduced)
            g_in_l, swapped = _tn_matmul(s["h"], dproj, "grad_w_in",
                                         hosted=_swap_reduced_over_d2d([reduced[k] for k in keys]))
            theirs = dict(zip(keys, swapped))
        else:
            g_in_l = _tn_matmul(s["h"], dproj, "grad_w_in")
        grads[(0, l)] = g_in_l[0].reshape(D_MODEL, N_CHIPS, IN_COLS // N_CHIPS).transpose(1, 0, 2)
        d_pw = jnp.stack([d_wbd[gi * POOL_GW:(gi + 1) * POOL_GW, gi * POOL_GW:(gi + 1) * POOL_GW]
                          for gi in range(len(POOL_WINDOWS))])
        small[l] = dict(norm1=d_n1[0], pool_w=d_pw, pool_scale=d_psc[0], sg_norm=d_sgn[0],
                        sg_w=d_wm.reshape(SG_HEADS, CHUNK, CHUNK), sg_b=d_bias[:, :SG_HEADS].T, norm2=d_n2[0])
        dcur = dx0
    grad_x = dcur.reshape(x.shape)

    names = ["norm1", "pool_w", "pool_scale", "sg_norm", "sg_w", "sg_b", "norm2"]
    slot = jnp.zeros((1,), F32)
    small_w = [norm1, pool_w, pool_scale, sg_norm, sg_w, sg_b, norm2, final_norm, slot]
    small_m = [m_norm1, m_pool_w, m_pool_scale, m_sg_norm, m_sg_w, m_sg_b, m_norm2, m_final_norm, slot]
    small_v = [v_norm1, v_pool_w, v_pool_scale, v_sg_norm, v_sg_w, v_sg_b, v_norm2, v_final_norm, slot]
    small_g = [jnp.stack([small[l][k] for l in range(depth)]) for k in names] + [d_final[0], loss_row[0, :1]]
    keys = [(0, 0)]
    pair_up(keys, _swap_halves([grads[k] for k in keys]))
    g_packed, arrived = _allreduce_small(_pack(small_g), _scatter_over_ici([parts[k] for k in keys]))
    chip_up(keys, arrived)
    theirs.update(zip(keys, _swap_reduced([reduced[k] for k in keys])))

    def joined(a):
        layers = []
        for l in range(depth):
            mine, other = reduced[(a, l)], theirs[(a, l)]
            layers.append(jnp.where(c_idx == 0, jnp.concatenate([mine, other]), jnp.concatenate([other, mine])))
        return jnp.stack(layers)

    gw_in, gw_out, gw_up, gw_down = [joined(a) for a in range(4)]

    loss = _unpack(g_packed, small_w)[-1][0]
    s_delta, s_m, s_v = _elementwise(_adamw, "adamw_small", [_pack(small_w), g_packed, _pack(small_m), _pack(small_v)], 3)
    gs = dict(zip(names + ["final_norm"], _unpack(g_packed, small_w)))
    ds = dict(zip(names + ["final_norm"], _unpack(s_delta, small_w)))
    ms = dict(zip(names + ["final_norm"], _unpack(s_m, small_w)))
    vs = dict(zip(names + ["final_norm"], _unpack(s_v, small_w)))

    big_g = dict(w_in=gw_in, w_out=gw_out, w_up=gw_up, w_down=gw_down)
    big_w = dict(w_in=(w_in, m_w_in, v_w_in), w_out=(w_out, m_w_out, v_w_out),
                 w_up=(w_up, m_w_up, v_w_up), w_down=(w_down, m_w_down, v_w_down))
    for k, (w, m, v) in big_w.items():
        operands = [w, big_g[k], m, v]
        if k == "w_in":
            operands = [jnp.swapaxes(o, 1, 2) for o in operands]
        ds[k], ms[k], vs[k] = _elementwise(_adamw, "adamw_" + k, operands, 3)
        if k == "w_in":
            ds[k], ms[k], vs[k] = [jnp.swapaxes(o, 1, 2) for o in (ds[k], ms[k], vs[k])]
        gs[k] = big_g[k]

    order = ["norm1", "w_in", "pool_w", "pool_scale", "sg_norm", "sg_w", "sg_b", "w_out", "norm2", "w_up", "w_down",
             "final_norm"]
    return (loss, grad_x, *[gs[k] for k in order], *[ds[k] for k in order], *[ms[k] for k in order],
            *[vs[k] for k in order])
```

```python
import functools

import jax
import jax.numpy as jnp
from jax import lax
from jax.experimental import pallas as pl
from jax.experimental.pallas import tpu as pltpu

F32 = jnp.float32
BF16 = jnp.bfloat16
MESH = pl.DeviceIdType.MESH
AXES = ("x", "y", "c")

EPS = 1e-6
D_MODEL = 1024
POOL_WIDTH = 256
SG_WIDTH = 256
SB_WIDTH = 512
POOL_WINDOWS = (2, 4, 8, 16)
POOL_GW = 64
POOL_HALO = 16
CHUNK = 128
SG_HEADS = 4
SB_HD = 64
SB_SCALE = 0.125
IN_COLS = 2304
QKV_OFF = 768
D_FF = 4096
N_CHIPS = 4
LANES = 128
VMEM_LIMIT = 56 * 1024 * 1024
MLP_CHUNK = 512
ATTN_TILE = 256
UNDERFLOW = -104.0

ADAM_LR = 0.001
ADAM_B1 = 0.9
ADAM_B2 = 0.999
ADAM_EPS = 1e-08
ADAM_WD = 0.01
ADAM_STEP = 10

HBM_SPEC = pl.BlockSpec(memory_space=pl.ANY)
VMEM_SPEC = pl.BlockSpec(memory_space=pltpu.VMEM)


def _params(**kw):
    return pltpu.CompilerParams(vmem_limit_bytes=VMEM_LIMIT, **kw)


def _tile(n, pref):
    if n <= pref:
        return n
    for t in range(pref - pref % LANES, 0, -LANES):
        if n % t == 0:
            return t
    raise ValueError((n, pref))


def _nn(a, b):
    return jnp.dot(a, b, preferred_element_type=F32)


def _nt(a, b):
    return lax.dot_general(a, b, (((1,), (1,)), ((), ())), preferred_element_type=F32)


def _tn(a, b):
    return lax.dot_general(a, b, (((0,), (0,)), ((), ())), preferred_element_type=F32)


def _rms_fwd(x, g):
    r = lax.rsqrt(jnp.mean(x * x, axis=-1, keepdims=True) + EPS)
    xhat = x * r
    return xhat * g, xhat, r


def _rms_bwd(dy, xhat, r, g):
    dxhat = dy * g
    dx = r * (dxhat - xhat * jnp.mean(dxhat * xhat, axis=-1, keepdims=True))
    return dx, dy * xhat


_GELU_K = 0.7978845608028654
_GELU_C = 0.044715


def _gelu(x):
    return 0.5 * x * (1.0 + jnp.tanh(_GELU_K * (x + _GELU_C * x * x * x)))


def _gelu_grad(x):
    t = jnp.tanh(_GELU_K * (x + _GELU_C * x * x * x))
    return 0.5 * (1.0 + t) + 0.5 * x * (1.0 - t * t) * _GELU_K * (1.0 + 3.0 * _GELU_C * x * x)


def _inproj_fwd(x, g, w):
    T, D = x.shape
    N = w.shape[1]
    tt = _tile(T, 512)

    def body(x_ref, g_ref, w_ref, proj_ref, h_ref, qkv_ref):
        h, _, _ = _rms_fwd(x_ref[...], g_ref[...])
        hb = h.astype(BF16)
        h_ref[...] = hb
        p = _nn(hb, w_ref[...])
        proj_ref[...] = p[:, :QKV_OFF]
        qkv_ref[...] = p[:, QKV_OFF:].astype(BF16)

    return pl.pallas_call(
        body, name="inproj_fwd", grid=(T // tt,),
        in_specs=[pl.BlockSpec((tt, D), lambda i: (i, 0)), pl.BlockSpec((1, D), lambda i: (0, 0)),
                  pl.BlockSpec((D, N), lambda i: (0, 0))],
        out_specs=[pl.BlockSpec((tt, QKV_OFF), lambda i: (i, 0)), pl.BlockSpec((tt, D), lambda i: (i, 0)),
                   pl.BlockSpec((tt, N - QKV_OFF), lambda i: (i, 0))],
        out_shape=[jax.ShapeDtypeStruct((T, QKV_OFF), F32), jax.ShapeDtypeStruct((T, D), BF16),
                   jax.ShapeDtypeStruct((T, N - QKV_OFF), BF16)],
        compiler_params=_params(),
    )(x, g, w)


def _inproj_bwd(pieces, w, x, g, dres):
    T, D = x.shape
    N = w.shape[1]
    tt = _tile(T, 512)
    widths = [p.shape[1] for p in pieces]
    offs = [sum(widths[:k]) for k in range(len(widths))]
    assert sum(widths) == N
    n_p = len(pieces)

    def body(*refs):
        p_refs = refs[:n_p]
        w_ref, x_ref, g_ref, dres_ref, dproj_ref, dx_ref, dg_ref = refs[n_p:]
        for p_ref, o, wd in zip(p_refs, offs, widths):
            dproj_ref[:, o:o + wd] = p_ref[...].astype(BF16)
        dh = _nt(dproj_ref[...], w_ref[...])
        gv = g_ref[...]
        _, xhat, r = _rms_fwd(x_ref[...], gv)
        dx, dgrow = _rms_bwd(dh, xhat, r, gv)
        dx_ref[...] = dres_ref[...] + dx

        @pl.when(pl.program_id(0) == 0)
        def _():
            dg_ref[...] = jnp.zeros_like(dg_ref)

        dg_ref[...] += jnp.sum(dgrow, axis=0, keepdims=True)

    return pl.pallas_call(
        body, name="inproj_bwd", grid=(T // tt,),
        in_specs=[pl.BlockSpec((tt, wd), lambda i: (i, 0)) for wd in widths] + [
            pl.BlockSpec((D, N), lambda i: (0, 0)), pl.BlockSpec((tt, D), lambda i: (i, 0)),
            pl.BlockSpec((1, D), lambda i: (0, 0)), pl.BlockSpec((tt, D), lambda i: (i, 0))],
        out_specs=[pl.BlockSpec((tt, N), lambda i: (i, 0)), pl.BlockSpec((tt, D), lambda i: (i, 0)),
                   pl.BlockSpec((1, D), lambda i: (0, 0))],
        out_shape=[jax.ShapeDtypeStruct((T, N), BF16), jax.ShapeDtypeStruct((T, D), F32),
                   jax.ShapeDtypeStruct((1, D), F32)],
        compiler_params=_params(),
    )(*pieces, w, x, g, dres)


def _pool_select(s2, s4, s8, s16, grp):
    return jnp.where(grp == 0, s2, jnp.where(grp == 1, s4, jnp.where(grp == 2, s8, s16)))


def _pool_count(t_glob, grp):
    win = jnp.where(grp == 0, 2, jnp.where(grp == 1, 4, jnp.where(grp == 2, 8, 16)))
    return jnp.minimum(t_glob + 1, win).astype(F32)


def _pool_diff(a, halo, base, tt):
    n = tt + POOL_HALO
    ext = jnp.concatenate([halo, a], axis=0)
    s2 = ext + pltpu.roll(ext, 1, 0)
    s4 = s2 + pltpu.roll(s2, 2, 0)
    s8 = s4 + pltpu.roll(s4, 4, 0)
    s16 = s8 + pltpu.roll(s8, 8, 0)
    grp = lax.broadcasted_iota(jnp.int32, (n, POOL_WIDTH), 1) // POOL_GW
    t_glob = lax.broadcasted_iota(jnp.int32, (n, POOL_WIDTH), 0) + (base - POOL_HALO)
    pooled = _pool_select(s2, s4, s8, s16, grp) / _pool_count(t_glob, grp)
    return pooled[POOL_HALO:] - a


def _pool_specs(T, tt):
    hb = tt // POOL_HALO
    return [pl.BlockSpec((tt, POOL_WIDTH), lambda i: (i, 0)),
            pl.BlockSpec((POOL_HALO, POOL_WIDTH), lambda i: (jnp.maximum(i * hb - 1, 0), 0))]


def _pool_fwd(proj, wbd, scale):
    T = proj.shape[0]
    tt = _tile(T, 512)

    def body(a_ref, halo_ref, w_ref, sc_ref, y_ref):
        i = pl.program_id(0)
        halo = jnp.where(i > 0, halo_ref[...], 0.0)
        d = _pool_diff(a_ref[...], halo, i * tt, tt)
        y_ref[...] = _nn(d.astype(BF16), w_ref[...]) * sc_ref[...]

    return pl.pallas_call(
        body, name="pool_fwd", grid=(T // tt,),
        in_specs=_pool_specs(T, tt) + [pl.BlockSpec((POOL_WIDTH, POOL_WIDTH), lambda i: (0, 0)),
                                       pl.BlockSpec((1, POOL_WIDTH), lambda i: (0, 0))],
        out_specs=pl.BlockSpec((tt, POOL_WIDTH), lambda i: (i, 0)),
        out_shape=jax.ShapeDtypeStruct((T, POOL_WIDTH), F32),
        compiler_params=_params(),
    )(proj, proj, wbd, scale)


def _pool_bwd(proj, dymix, wbd, scale):
    T = proj.shape[0]
    tt = _tile(T, 512)
    hb = tt // POOL_HALO
    nblk = T // tt
    n = tt + POOL_HALO

    def body(a_ref, halo_ref, dy_ref, dyn_ref, w_ref, sc_ref, da_ref, dw_ref, dsc_ref):
        i = pl.program_id(0)
        halo = jnp.where(i > 0, halo_ref[...], 0.0)
        d = _pool_diff(a_ref[...], halo, i * tt, tt)
        db = d.astype(BF16)
        wv = w_ref[...]
        sc = sc_ref[...]
        dy = dy_ref[...]
        dys = dy * sc

        @pl.when(i == 0)
        def _():
            dw_ref[...] = jnp.zeros_like(dw_ref)
            dsc_ref[...] = jnp.zeros_like(dsc_ref)

        dsc_ref[...] += jnp.sum(dy * _nn(db, wv), axis=0, keepdims=True)
        dw_ref[...] += _tn(db, dys.astype(BF16))
        dyn = jnp.where(i < nblk - 1, dyn_ref[...], 0.0) * sc
        dd = _nt(jnp.concatenate([dys, dyn], axis=0).astype(BF16), wv)
        grp = lax.broadcasted_iota(jnp.int32, (n, POOL_WIDTH), 1) // POOL_GW
        t_glob = lax.broadcasted_iota(jnp.int32, (n, POOL_WIDTH), 0) + i * tt
        e = dd / _pool_count(t_glob, grp)
        r2 = e + pltpu.roll(e, n - 1, 0)
        r4 = r2 + pltpu.roll(r2, n - 2, 0)
        r8 = r4 + pltpu.roll(r4, n - 4, 0)
        r16 = r8 + pltpu.roll(r8, n - 8, 0)
        da_ref[...] = (_pool_select(r2, r4, r8, r16, grp) - dd)[:tt].astype(BF16)

    return pl.pallas_call(
        body, name="pool_bwd", grid=(nblk,),
        in_specs=_pool_specs(T, tt) + [
            pl.BlockSpec((tt, POOL_WIDTH), lambda i: (i, 0)),
            pl.BlockSpec((POOL_HALO, POOL_WIDTH), lambda i: (jnp.minimum((i + 1) * hb, T // POOL_HALO - 1), 0)),
            pl.BlockSpec((POOL_WIDTH, POOL_WIDTH), lambda i: (0, 0)), pl.BlockSpec((1, POOL_WIDTH), lambda i: (0, 0))],
        out_specs=[pl.BlockSpec((tt, POOL_WIDTH), lambda i: (i, 0)),
                   pl.BlockSpec((POOL_WIDTH, POOL_WIDTH), lambda i: (0, 0)),
                   pl.BlockSpec((1, POOL_WIDTH), lambda i: (0, 0))],
        out_shape=[jax.ShapeDtypeStruct((T, POOL_WIDTH), BF16),
                   jax.ShapeDtypeStruct((POOL_WIDTH, POOL_WIDTH), F32),
                   jax.ShapeDtypeStruct((1, POOL_WIDTH), F32)],
        compiler_params=_params(),
    )(proj, proj, dymix, dymix, wbd, scale)


def _head_select(stacked, grp):
    out = jnp.where(grp == 0, stacked[0:CHUNK], 0.0)
    for h in range(1, SG_HEADS):
        out = out + jnp.where(grp == h, stacked[h * CHUNK:(h + 1) * CHUNK], 0.0)
    return out


def _sg_specs(tt):
    return [pl.BlockSpec((tt, SG_WIDTH), lambda i: (i, 1)), pl.BlockSpec((tt, SG_WIDTH), lambda i: (i, 2))]


def _sg_fwd(proj, wm, bias, g):
    T = proj.shape[0]
    tt = _tile(T, 512)

    def body(u_ref, v_ref, wm_ref, b_ref, g_ref, y_ref):
        zu = _gelu(u_ref[...])
        vn, _, _ = _rms_fwd(_gelu(v_ref[...]), g_ref[...])
        grp = lax.broadcasted_iota(jnp.int32, (CHUNK, SG_WIDTH), 1) // SB_HD
        for n in range(tt // CHUNK):
            rows = slice(n * CHUNK, (n + 1) * CHUNK)
            sv = _head_select(_nn(wm_ref[...], vn[rows].astype(BF16)), grp) + b_ref[...]
            y_ref[rows, :] = zu[rows] * sv

    return pl.pallas_call(
        body, name="sg_fwd", grid=(T // tt,),
        in_specs=_sg_specs(tt) + [pl.BlockSpec((SG_HEADS * CHUNK, CHUNK), lambda i: (0, 0)),
                                  pl.BlockSpec((CHUNK, SG_WIDTH), lambda i: (0, 0)),
                                  pl.BlockSpec((1, SG_WIDTH), lambda i: (0, 0))],
        out_specs=pl.BlockSpec((tt, SG_WIDTH), lambda i: (i, 0)),
        out_shape=jax.ShapeDtypeStruct((T, SG_WIDTH), F32),
        compiler_params=_params(),
    )(proj, proj, wm, bias, g)


def _sg_bwd(proj, dymix, wm, wmt, bias, g, hosted=None):
    T = proj.shape[0]
    tt = _tile(T, 512)
    nblk = T // tt

    def body(*refs):
        i = pl.program_id(0)
        (u_ref, v_ref, dy_ref, wm_ref, wmt_ref, b_ref, g_ref, du_ref, dv_ref, dw_ref, db_ref, dg_ref,
         dvn_ref, dbias_ref), start, wait = _host(hosted, refs, 7, 5, i == 0, i == nblk - 1)
        start()
        up, vp = u_ref[...], v_ref[...]
        gv = g_ref[...]
        zu, zv = _gelu(up), _gelu(vp)
        vn, xhat, r = _rms_fwd(zv, gv)
        gu = _gelu_grad(up)
        grp = lax.broadcasted_iota(jnp.int32, (CHUNK, SG_WIDTH), 1) // SB_HD

        @pl.when(i == 0)
        def _():
            dw_ref[...] = jnp.zeros_like(dw_ref)
            dbias_ref[...] = jnp.zeros_like(dbias_ref)
            dg_ref[...] = jnp.zeros_like(dg_ref)

        for n in range(tt // CHUNK):
            rows = slice(n * CHUNK, (n + 1) * CHUNK)
            vc = vn[rows].astype(BF16)
            sv = _head_select(_nn(wm_ref[...], vc), grp) + b_ref[...]
            dy = dy_ref[rows, :]
            du_ref[rows, :] = (dy * sv * gu[rows]).astype(BF16)
            dsv = dy * zu[rows]
            dsvb = dsv.astype(BF16)
            dvn_ref[rows, :] = _head_select(_nn(wmt_ref[...], dsvb), grp)
            stacked = jnp.concatenate([jnp.where(grp == h, dsv, 0.0) for h in range(SG_HEADS)], axis=0)
            dw_ref[...] += _nt(stacked.astype(BF16), vc)
            dbias_ref[...] += dsv

        dzv, dgrow = _rms_bwd(dvn_ref[...], xhat, r, gv)
        dg_ref[...] += jnp.sum(dgrow, axis=0, keepdims=True)
        dv_ref[...] = (dzv * _gelu_grad(vp)).astype(BF16)

        @pl.when(i == nblk - 1)
        def _():
            t_i = lax.broadcasted_iota(jnp.int32, (SG_HEADS * CHUNK, CHUNK), 0) % CHUNK
            s_i = lax.broadcasted_iota(jnp.int32, (SG_HEADS * CHUNK, CHUNK), 1)
            dw_ref[...] = jnp.where(s_i <= t_i, dw_ref[...], 0.0)
            lane = lax.broadcasted_iota(jnp.int32, (CHUNK, LANES), 1)
            acc = jnp.zeros((CHUNK, LANES), F32)
            for h in range(SG_HEADS):
                tot = jnp.sum(jnp.where(grp == h, dbias_ref[...], 0.0), axis=1, keepdims=True)
                acc = acc + jnp.where(lane == h, tot, 0.0)
            db_ref[...] = acc

        wait()

    h_ins = hosted.ins if hosted else []
    res = pl.pallas_call(
        body, name="sg_bwd_hosting" if hosted else "sg_bwd", grid=(nblk,),
        in_specs=_sg_specs(tt) + [pl.BlockSpec((tt, SG_WIDTH), lambda i: (i, 1)),
                                  pl.BlockSpec((SG_HEADS * CHUNK, CHUNK), lambda i: (0, 0)),
                                  pl.BlockSpec((SG_HEADS * CHUNK, CHUNK), lambda i: (0, 0)),
                                  pl.BlockSpec((CHUNK, SG_WIDTH), lambda i: (0, 0)),
                                  pl.BlockSpec((1, SG_WIDTH), lambda i: (0, 0))] + [HBM_SPEC] * len(h_ins),
        out_specs=[pl.BlockSpec((tt, SG_WIDTH), lambda i: (i, 0)), pl.BlockSpec((tt, SG_WIDTH), lambda i: (i, 0)),
                   pl.BlockSpec((SG_HEADS * CHUNK, CHUNK), lambda i: (0, 0)),
                   pl.BlockSpec((CHUNK, LANES), lambda i: (0, 0)), pl.BlockSpec((1, SG_WIDTH), lambda i: (0, 0))]
        + [HBM_SPEC] * len(h_ins),
        out_shape=[jax.ShapeDtypeStruct((T, SG_WIDTH), BF16), jax.ShapeDtypeStruct((T, SG_WIDTH), BF16),
                   jax.ShapeDtypeStruct((SG_HEADS * CHUNK, CHUNK), F32),
                   jax.ShapeDtypeStruct((CHUNK, LANES), F32), jax.ShapeDtypeStruct((1, SG_WIDTH), F32)]
        + (hosted.out_shapes if hosted else []),
        scratch_shapes=[pltpu.VMEM((tt, SG_WIDTH), F32), pltpu.VMEM((CHUNK, SG_WIDTH), F32)]
        + (hosted.sems() if hosted else []),
        compiler_params=_params(has_side_effects=hosted is not None),
    )(proj, proj, dymix, wm, wmt, bias, g, *h_ins)
    return res[:5], res[5:]


def _split_dot(x, u):
    hi = x.astype(BF16)
    lo = (x - hi.astype(F32)).astype(BF16)
    return _nn(hi, u) + _nn(lo, u)


def _sb_logits(z):
    lb = jnp.minimum(z, 0.0) - jnp.log(1.0 + jnp.exp(-jnp.abs(z)))
    return lb, lb - z


ATTN_STRIP = 32


def _by_strips(n_rows, fn):
    parts = None
    for r in range(0, n_rows, ATTN_STRIP):
        res = fn(slice(r, r + ATTN_STRIP))
        parts = [[v] for v in res] if parts is None else [p + [v] for p, v in zip(parts, res)]
    return [jnp.concatenate(p, axis=0) for p in parts]


def _attn_qkv_specs(tq, T):
    base = (IN_COLS - 3 * SB_WIDTH - QKV_OFF) // LANES
    nb = SB_WIDTH // LANES
    return [pl.BlockSpec((tq, LANES), lambda p, i: (i, base + p)),
            pl.BlockSpec((T, LANES), lambda p, i: (0, base + nb + p)),
            pl.BlockSpec((T, LANES), lambda p, i: (0, base + 2 * nb + p))]


class _Hosted:
    def __init__(self, ins, out_shapes, n_sems, copies, in_place=False):
        self.ins, self.out_shapes, self.n_sems, self.copies = ins, out_shapes, n_sems, copies
        self.in_place = in_place

    @property
    def n(self):
        return len(self.ins)

    def aliases(self, n_in, n_out):
        return {n_in + k: n_out + k for k in range(self.n)} if self.in_place else {}

    def sems(self):
        return [pltpu.SemaphoreType.DMA((self.n_sems,)), pltpu.SemaphoreType.DMA((self.n_sems,))]

    def start(self, src, dst, ssem, rsem):
        for send, _ in self.copies(src, dst, ssem, rsem):
            send.start()

    def wait(self, src, dst, ssem, rsem):
        for send, recv in self.copies(src, dst, ssem, rsem):
            recv.wait_recv()
            send.wait_send()


def _host(hosted, refs, n_in, n_out, first, last):
    if hosted is None:
        return refs, lambda: None, lambda: None
    n = hosted.n
    own_in, h_in = refs[:n_in], refs[n_in:n_in + n]
    own_out, h_out = refs[n_in + n:n_in + n + n_out], refs[n_in + n + n_out:n_in + 2 * n + n_out]
    rest = refs[n_in + 2 * n + n_out:]
    ssem, rsem = rest[-2:]

    def start():
        if first is True:
            hosted.start(h_in, h_out, ssem, rsem)
        else:
            pl.when(first)(lambda: hosted.start(h_in, h_out, ssem, rsem))

    def wait():
        if last is True:
            hosted.wait(h_in, h_out, ssem, rsem)
        else:
            pl.when(last)(lambda: hosted.wait(h_in, h_out, ssem, rsem))

    return own_in + own_out + rest[:-2], start, wait


def _attn_fwd(qkv, hosted=None):
    T = qkv.shape[0]
    tq = _tile(T, ATTN_TILE)
    n_p, nq = SB_WIDTH // LANES, T // tq

    def body(*refs):
        p, i = pl.program_id(0), pl.program_id(1)
        (q_ref, k_ref, v_ref, o_ref), start, wait = _host(
            hosted, refs, 3, 1, jnp.logical_and(p == 0, i == 0), jnp.logical_and(p == n_p - 1, i == nq - 1))
        start()
        lane = lax.broadcasted_iota(jnp.int32, (tq, LANES), 1)
        row = lax.broadcasted_iota(jnp.int32, (tq, tq), 0)
        col = lax.broadcasted_iota(jnp.int32, (tq, tq), 1)
        after = jnp.where(row > col, 1.0, 0.0).astype(BF16)
        valid = col < row
        q = q_ref[...].astype(F32)
        qh = [jnp.where((lane // SB_HD) == hh, q * SB_SCALE, 0.0).astype(BF16) for hh in range(2)]

        def tiles(todo, state):
            chains = [(t, hh) for t in range(len(todo)) for hh in range(2)]
            kv = []
            for j, _ in todo:
                ks = pl.ds(pl.multiple_of(j * tq, tq), tq)
                kv.append((k_ref[ks, :], v_ref[ks, :]))
            z = {(t, hh): _nt(qh[hh], kv[t][0]) for t, hh in chains}
            lb, lmb, lm_sum = {}, {}, {}
            for t, hh in chains:
                def logits(rows, z=z[(t, hh)], mask=todo[t][1]):
                    lb, lm = _sb_logits(z[rows])
                    if mask is not None:
                        lm = jnp.where(mask[rows], lm, 0.0)
                    return lb, lm.astype(BF16), jnp.sum(lm, axis=1, keepdims=True)

                lb[(t, hh)], lmb[(t, hh)], lm_sum[(t, hh)] = _by_strips(tq, logits)
            x = {c: _nn(lmb[c], after) for c in chains}
            carry = [state[hh][0] for hh in range(2)]
            acc = [state[hh][1] for hh in range(2)]
            for t, hh in chains:
                def weights(rows, lb=lb[(t, hh)], x=x[(t, hh)], carry=carry[hh], mask=todo[t][1]):
                    a = jnp.exp(lb[rows] + x[rows] + carry[rows])
                    if mask is not None:
                        a = jnp.where(mask[rows], a, 0.0)
                    return (a.astype(BF16),)

                (ab,) = _by_strips(tq, weights)
                acc[hh] = acc[hh] + _nn(ab, kv[t][1])
                carry[hh] = carry[hh] + lm_sum[(t, hh)]
            return tuple((carry[hh], acc[hh]) for hh in range(2))

        def live(state):
            return jnp.maximum(jnp.max(state[0][0]), jnp.max(state[1][0]))

        zero = (jnp.zeros((tq, 1), F32), jnp.zeros((tq, LANES), F32))
        state = tiles([(i, valid), (jnp.maximum(i - 1, 0), jnp.broadcast_to(i > 0, (tq, tq)))], (zero, zero))

        def cond(st):
            return jnp.logical_and(st[0] >= 0, st[2] > UNDERFLOW)

        def step(st):
            state = tiles([(st[0], None)], st[1])
            return st[0] - 1, state, live(state)

        _, state, _ = lax.while_loop(cond, step, (i - 2, state, live(state)))
        o_ref[...] = jnp.where(lane < SB_HD, state[0][1], state[1][1])
        wait()

    h_ins = hosted.ins if hosted else []
    res = pl.pallas_call(
        body, name="attn_fwd_hosting" if hosted else "attn_fwd", grid=(n_p, nq),
        in_specs=_attn_qkv_specs(tq, T) + [HBM_SPEC] * len(h_ins),
        out_specs=[pl.BlockSpec((tq, LANES), lambda p, i: (i, p))] + [HBM_SPEC] * len(h_ins),
        out_shape=[jax.ShapeDtypeStruct((T, SB_WIDTH), F32)] + (hosted.out_shapes if hosted else []),
        input_output_aliases=hosted.aliases(3, 1) if hosted else {},
        scratch_shapes=hosted.sems() if hosted else [],
        compiler_params=_params(has_side_effects=hosted is not None),
    )(qkv, qkv, qkv, *h_ins)
    return res[0], res[1:]


def _attn_bwd(qkv, o, dymix, hosted=None):
    T = qkv.shape[0]
    tq = _tile(T, ATTN_TILE)
    n_p, nq = SB_WIDTH // LANES, T // tq
    yc_blk = (POOL_WIDTH + SG_WIDTH) // LANES

    def body(*refs):
        p, i = pl.program_id(0), pl.program_id(1)
        (q_ref, k_ref, v_ref, o_ref, do_ref, dq_ref, dk_ref, dv_ref), start, wait = _host(
            hosted, refs, 5, 3, jnp.logical_and(p == 0, i == 0), jnp.logical_and(p == n_p - 1, i == nq - 1))
        start()
        lane = lax.broadcasted_iota(jnp.int32, (tq, LANES), 1)
        row = lax.broadcasted_iota(jnp.int32, (tq, tq), 0)
        col = lax.broadcasted_iota(jnp.int32, (tq, tq), 1)
        after = jnp.where(row > col, 1.0, 0.0).astype(BF16)
        from_here = jnp.where(row >= col, 1.0, 0.0).astype(BF16)
        from_here2 = jnp.concatenate([from_here, from_here], axis=0)
        valid = col < row

        @pl.when(i == 0)
        def _():
            dk_ref[...] = jnp.zeros_like(dk_ref)
            dv_ref[...] = jnp.zeros_like(dv_ref)

        q = q_ref[...].astype(F32)
        ov = o_ref[...]
        dov = do_ref[...]
        heads = [(lane // SB_HD) == hh for hh in range(2)]
        qh = [jnp.where(h, q * SB_SCALE, 0.0).astype(BF16) for h in heads]
        dohb = [jnp.where(h, dov, 0.0).astype(BF16) for h in heads]
        delta = [jnp.sum(d.astype(F32) * ov, axis=1, keepdims=True) for d in dohb]

        def tiles(todo, state):
            chains = [(t, hh) for t in range(len(todo)) for hh in range(2)]
            kv, where = [], []
            for j, _ in todo:
                ks = pl.ds(pl.multiple_of(j * tq, tq), tq)
                where.append(ks)
                kv.append((k_ref[ks, :], v_ref[ks, :]))
            z = {(t, hh): _nt(qh[hh], kv[t][0]) for t, hh in chains}
            da = {(t, hh): _nt(dohb[hh], kv[t][1]) for t, hh in chains}
            lb, lmb, lm_sum = {}, {}, {}
            for t, hh in chains:
                def logits(rows, z=z[(t, hh)], mask=todo[t][1]):
                    lb, lm = _sb_logits(z[rows])
                    if mask is not None:
                        lm = jnp.where(mask[rows], lm, 0.0)
                    return lb, lm.astype(BF16), jnp.sum(lm, axis=1, keepdims=True)

                lb[(t, hh)], lmb[(t, hh)], lm_sum[(t, hh)] = _by_strips(tq, logits)
            x = {c: _nn(lmb[c], after) for c in chains}
            c_a = [state[hh][0] for hh in range(2)]
            ab, g, g_split, g_sum = {}, {}, {}, {}
            for t, hh in chains:
                def weights(rows, lb=lb[(t, hh)], x=x[(t, hh)], da=da[(t, hh)], c_a=c_a[hh], mask=todo[t][1]):
                    a = jnp.exp(lb[rows] + x[rows] + c_a[rows])
                    if mask is not None:
                        a = jnp.where(mask[rows], a, 0.0)
                    ab = a.astype(BF16)
                    g = da[rows] * ab.astype(F32)
                    hi = g.astype(BF16)
                    lo = (g - hi.astype(F32)).astype(BF16)
                    return ab, g, jnp.concatenate([hi, lo], axis=1), jnp.sum(g, axis=1, keepdims=True)

                ab[(t, hh)], g[(t, hh)], g_split[(t, hh)], g_sum[(t, hh)] = _by_strips(tq, weights)
                c_a[hh] = c_a[hh] + lm_sum[(t, hh)]
            right = {c: _nn(g_split[c], from_here2) for c in chains}
            c_r = [state[hh][1] for hh in range(2)]
            dzb = {}
            for t, hh in chains:
                def logit_grads(rows, lb=lb[(t, hh)], g=g[(t, hh)], right=right[(t, hh)], c_r=c_r[hh], hh=hh,
                                mask=todo[t][1]):
                    sig = jnp.exp(lb[rows])
                    left = delta[hh][rows] - (c_r[rows] + right[rows])
                    dz = g[rows] * (1.0 - sig) - left * sig
                    if mask is not None:
                        dz = jnp.where(mask[rows], dz, 0.0)
                    return (dz.astype(BF16),)

                (dzb[(t, hh)],) = _by_strips(tq, logit_grads)
                c_r[hh] = c_r[hh] + g_sum[(t, hh)]
            dqa = [state[hh][2] for hh in range(2)]
            for t in range(len(todo)):
                dk_ref[where[t], :] += _tn(dzb[(t, 0)], qh[0]) + _tn(dzb[(t, 1)], qh[1])
                dv_ref[where[t], :] += _tn(ab[(t, 0)], dohb[0]) + _tn(ab[(t, 1)], dohb[1])
                for hh in range(2):
                    dqa[hh] = dqa[hh] + _nn(dzb[(t, hh)], kv[t][0])
            return tuple((c_a[hh], c_r[hh], dqa[hh]) for hh in range(2))

        def live(state):
            return jnp.maximum(jnp.max(state[0][0]), jnp.max(state[1][0]))

        zero = (jnp.zeros((tq, 1), F32), jnp.zeros((tq, 1), F32), jnp.zeros((tq, LANES), F32))
        state = tiles([(i, valid), (jnp.maximum(i - 1, 0), jnp.broadcast_to(i > 0, (tq, tq)))], (zero, zero))

        def cond(st):
            return jnp.logical_and(st[0] >= 0, st[2] > UNDERFLOW)

        def step(st):
            state = tiles([(st[0], None)], st[1])
            return st[0] - 1, state, live(state)

        _, state, _ = lax.while_loop(cond, step, (i - 2, state, live(state)))
        dq_ref[...] = (jnp.where(lane < SB_HD, state[0][2], state[1][2]) * SB_SCALE).astype(BF16)
        wait()

    h_ins = hosted.ins if hosted else []
    res = pl.pallas_call(
        body, name="attn_bwd_hosting" if hosted else "attn_bwd", grid=(n_p, nq),
        in_specs=_attn_qkv_specs(tq, T) + [pl.BlockSpec((tq, LANES), lambda p, i: (i, p)),
                                           pl.BlockSpec((tq, LANES), lambda p, i: (i, yc_blk + p))]
        + [HBM_SPEC] * len(h_ins),
        out_specs=[pl.BlockSpec((tq, LANES), lambda p, i: (i, p)), pl.BlockSpec((T, LANES), lambda p, i: (0, p)),
                   pl.BlockSpec((T, LANES), lambda p, i: (0, p))] + [HBM_SPEC] * len(h_ins),
        out_shape=[jax.ShapeDtypeStruct((T, SB_WIDTH), BF16)] + [jax.ShapeDtypeStruct((T, SB_WIDTH), F32)] * 2
        + (hosted.out_shapes if hosted else []),
        scratch_shapes=hosted.sems() if hosted else [],
        compiler_params=_params(has_side_effects=hosted is not None),
    )(qkv, qkv, qkv, o, dymix, *h_ins)
    return res[:3], res[3:]


def _outproj_fwd(x, ya, yb, yc, w, hosted=None):
    T, D = x.shape
    tt = _tile(T, 512)
    nt = T // tt

    def body(*refs):
        i = pl.program_id(0)
        (x_ref, ya_ref, yb_ref, yc_ref, w_ref, x1_ref, ymix_ref), start, wait = _host(
            hosted, refs, 5, 2, i == 0, i == nt - 1)
        start()
        ymix_ref[:, 0:POOL_WIDTH] = ya_ref[...].astype(BF16)
        ymix_ref[:, POOL_WIDTH:POOL_WIDTH + SG_WIDTH] = yb_ref[...].astype(BF16)
        ymix_ref[:, POOL_WIDTH + SG_WIDTH:] = yc_ref[...].astype(BF16)
        x1_ref[...] = x_ref[...] + _nn(ymix_ref[...], w_ref[...])
        wait()

    row = lambda width: pl.BlockSpec((tt, width), lambda i: (i, 0))
    h_ins = hosted.ins if hosted else []
    res = pl.pallas_call(
        body, name="outproj_fwd_hosting" if hosted else "outproj_fwd", grid=(nt,),
        in_specs=[row(D), row(POOL_WIDTH), row(SG_WIDTH), row(SB_WIDTH), pl.BlockSpec((D, D), lambda i: (0, 0))]
        + [HBM_SPEC] * len(h_ins),
        out_specs=[row(D), row(D)] + [HBM_SPEC] * len(h_ins),
        out_shape=[jax.ShapeDtypeStruct((T, D), F32), jax.ShapeDtypeStruct((T, D), BF16)]
        + (hosted.out_shapes if hosted else []),
        input_output_aliases=hosted.aliases(5, 2) if hosted else {},
        scratch_shapes=hosted.sems() if hosted else [],
        compiler_params=_params(has_side_effects=hosted is not None),
    )(x, ya, yb, yc, w, *h_ins)
    return res[:2], res[2:]


def _nt_matmul(a, w):
    T, N = a.shape
    K = w.shape[0]
    tt = _tile(T, 512)

    def body(a_ref, w_ref, o_ref):
        o_ref[...] = _nt(a_ref[...].astype(BF16), w_ref[...])

    return pl.pallas_call(
        body, name="nt_matmul", grid=(T // tt,),
        in_specs=[pl.BlockSpec((tt, N), lambda i: (i, 0)), pl.BlockSpec((K, N), lambda i: (0, 0))],
        out_specs=pl.BlockSpec((tt, K), lambda i: (i, 0)),
        out_shape=jax.ShapeDtypeStruct((T, K), F32),
        compiler_params=_params(),
    )(a, w)


def _tn_matmul(a, b, name, n_split=1, hosted=None):
    T, K = a.shape
    N = b.shape[1]
    tk = _tile(K, 1024)
    tn = _tile(N // n_split, 1024)
    tt = _tile(T, 2048)
    nper = N // n_split // tn
    nk, nn, nt = K // tk, N // tn, T // tt

    def body(*refs):
        k, n, t = pl.program_id(0), pl.program_id(1), pl.program_id(2)
        (a_ref, b_ref, o_ref), start, wait = _host(
            hosted, refs, 2, 1, jnp.logical_and(jnp.logical_and(k == 0, n == 0), t == 0),
            jnp.logical_and(jnp.logical_and(k == nk - 1, n == nn - 1), t == nt - 1))
        start()

        @pl.when(t == 0)
        def _():
            o_ref[...] = jnp.zeros_like(o_ref)

        o_ref[...] += _tn(a_ref[...], b_ref[...].astype(BF16))
        wait()

    h_ins = hosted.ins if hosted else []
    res = pl.pallas_call(
        body, name=name + "_hosting" if hosted else name, grid=(nk, nn, nt),
        in_specs=[pl.BlockSpec((tt, tk), lambda k, n, t: (t, k)), pl.BlockSpec((tt, tn), lambda k, n, t: (t, n))]
        + [HBM_SPEC] * len(h_ins),
        out_specs=[pl.BlockSpec((None, tk, tn), lambda k, n, t: (n // nper, k, n % nper))] + [HBM_SPEC] * len(h_ins),
        out_shape=[jax.ShapeDtypeStruct((n_split, K, N // n_split), F32)] + (hosted.out_shapes if hosted else []),
        scratch_shapes=hosted.sems() if hosted else [],
        compiler_params=_params(has_side_effects=hosted is not None),
    )(a, b, *h_ins)
    return (res[0], res[1:]) if hosted else res[0]


def _mlp_fwd(x, g, w_up, w_down, hosted=None):
    T, D = x.shape
    F = w_up.shape[1]
    tt = _tile(T, 1024)
    fc = _tile(F, MLP_CHUNK)
    nc = F // fc
    nt = T // tt

    def body(*refs):
        i, c = pl.program_id(0), pl.program_id(1)
        (x_ref, g_ref, wu_ref, wd_ref, y_ref, h_ref, a_ref), start, wait = _host(
            hosted, refs, 4, 3, jnp.logical_and(i == 0, c == 0), jnp.logical_and(i == nt - 1, c == nc - 1))
        start()

        @pl.when(c == 0)
        def _():
            xv = x_ref[...]
            h, _, _ = _rms_fwd(xv, g_ref[...])
            h_ref[...] = h.astype(BF16)
            y_ref[...] = xv

        a = jnp.square(jnp.maximum(_nn(h_ref[...], wu_ref[...]), 0.0)).astype(BF16)
        a_ref[...] = a
        y_ref[...] += _nn(a, wd_ref[...])
        wait()

    h_ins = hosted.ins if hosted else []
    res = pl.pallas_call(
        body, name="mlp_fwd_hosting" if hosted else "mlp_fwd", grid=(nt, nc),
        in_specs=[pl.BlockSpec((tt, D), lambda i, c: (i, 0)), pl.BlockSpec((1, D), lambda i, c: (0, 0)),
                  pl.BlockSpec((D, fc), lambda i, c: (0, c)), pl.BlockSpec((fc, D), lambda i, c: (c, 0))]
        + [HBM_SPEC] * len(h_ins),
        out_specs=[pl.BlockSpec((tt, D), lambda i, c: (i, 0)), pl.BlockSpec((tt, D), lambda i, c: (i, 0)),
                   pl.BlockSpec((tt, fc), lambda i, c: (i, c))]
        + [HBM_SPEC] * len(h_ins),
        out_shape=[jax.ShapeDtypeStruct((T, D), F32), jax.ShapeDtypeStruct((T, D), BF16),
                   jax.ShapeDtypeStruct((T, F), BF16)]
        + (hosted.out_shapes if hosted else []),
        scratch_shapes=hosted.sems() if hosted else [],
        compiler_params=_params(has_side_effects=hosted is not None),
    )(x, g, w_up, w_down, *h_ins)
    return res[:3], res[3:]


def _mlp_bwd(dy, x, g, u, w_up, w_down, hosted=None):
    T, D = x.shape
    F = w_up.shape[1]
    tt = _tile(T, 1024)
    fc = _tile(F, MLP_CHUNK)
    nc = F // fc
    nt = T // tt

    def body(*refs):
        i, c = pl.program_id(0), pl.program_id(1)
        (dy_ref, x_ref, g_ref, u_ref, wu_ref, wd_ref, dx_ref, du_ref, dg_ref, dyb_ref, dh_ref), start, wait = _host(
            hosted, refs, 6, 3, jnp.logical_and(i == 0, c == 0), jnp.logical_and(i == nt - 1, c == nc - 1))
        start()

        @pl.when(c == 0)
        def _():
            dyb_ref[...] = dy_ref[...].astype(BF16)
            dh_ref[...] = jnp.zeros_like(dh_ref)

        @pl.when(jnp.logical_and(i == 0, c == 0))
        def _():
            dg_ref[...] = jnp.zeros_like(dg_ref)

        da = _nt(dyb_ref[...], wd_ref[...])
        du = (da * (2.0 * jnp.sqrt(u_ref[...].astype(F32)))).astype(BF16)
        du_ref[...] = du
        dh_ref[...] += _nt(du, wu_ref[...])

        @pl.when(c == nc - 1)
        def _():
            gv = g_ref[...]
            _, xhat, r = _rms_fwd(x_ref[...], gv)
            dx, dgrow = _rms_bwd(dh_ref[...], xhat, r, gv)
            dx_ref[...] = dy_ref[...] + dx
            dg_ref[...] += jnp.sum(dgrow, axis=0, keepdims=True)

        wait()

    h_ins = hosted.ins if hosted else []
    res = pl.pallas_call(
        body, name="mlp_bwd_hosting" if hosted else "mlp_bwd", grid=(nt, nc),
        in_specs=[pl.BlockSpec((tt, D), lambda i, c: (i, 0)), pl.BlockSpec((tt, D), lambda i, c: (i, 0)),
                  pl.BlockSpec((1, D), lambda i, c: (0, 0)), pl.BlockSpec((tt, fc), lambda i, c: (i, c)),
                  pl.BlockSpec((D, fc), lambda i, c: (0, c)), pl.BlockSpec((fc, D), lambda i, c: (c, 0))]
        + [HBM_SPEC] * len(h_ins),
        out_specs=[pl.BlockSpec((tt, D), lambda i, c: (i, 0)), pl.BlockSpec((tt, fc), lambda i, c: (i, c)),
                   pl.BlockSpec((1, D), lambda i, c: (0, 0))] + [HBM_SPEC] * len(h_ins),
        out_shape=[jax.ShapeDtypeStruct((T, D), F32), jax.ShapeDtypeStruct((T, F), BF16),
                   jax.ShapeDtypeStruct((1, D), F32)] + (hosted.out_shapes if hosted else []),
        scratch_shapes=[pltpu.VMEM((tt, D), BF16), pltpu.VMEM((tt, D), F32)] + (hosted.sems() if hosted else []),
        compiler_params=_params(has_side_effects=hosted is not None),
    )(dy, x, g, u, w_up, w_down, *h_ins)
    return res[:3], res[3:]


def _loss_head(x, g, target):
    T, D = x.shape
    tt = _tile(T, 512)

    def body(x_ref, g_ref, t_ref, loss_ref, dx_ref, dg_ref):
        gv = g_ref[...]
        y, xhat, r = _rms_fwd(x_ref[...], gv)
        err = y - t_ref[...]
        dx, dgrow = _rms_bwd(err * (1.0 / D), xhat, r, gv)
        dx_ref[...] = dx

        @pl.when(pl.program_id(0) == 0)
        def _():
            loss_ref[...] = jnp.zeros_like(loss_ref)
            dg_ref[...] = jnp.zeros_like(dg_ref)

        loss_ref[...] += 0.5 * jnp.sum(jnp.mean(err * err, axis=-1, keepdims=True), axis=0, keepdims=True)
        dg_ref[...] += jnp.sum(dgrow, axis=0, keepdims=True)

    return pl.pallas_call(
        body, name="loss_head", grid=(T // tt,),
        in_specs=[pl.BlockSpec((tt, D), lambda i: (i, 0)), pl.BlockSpec((1, D), lambda i: (0, 0)),
                  pl.BlockSpec((tt, D), lambda i: (i, 0))],
        out_specs=[pl.BlockSpec((1, LANES), lambda i: (0, 0)), pl.BlockSpec((tt, D), lambda i: (i, 0)),
                   pl.BlockSpec((1, D), lambda i: (0, 0))],
        out_shape=[jax.ShapeDtypeStruct((1, LANES), F32), jax.ShapeDtypeStruct((T, D), F32),
                   jax.ShapeDtypeStruct((1, D), F32)],
        compiler_params=_params(),
    )(x, g, target)


def _rows(shape, pref=512):
    last = shape[-1]
    rows = 1
    for s in shape[:-1]:
        rows *= s
    tr = rows
    if rows * last > 256 * 1024:
        for cand in (pref, 256, 128, 64, 32, 16, 8):
            if rows % cand == 0:
                tr = cand
                break
    return rows, last, tr


def _elementwise(fn, name, ins, n_out, out_dtype=F32):
    shape = ins[0].shape
    rows, last, tr = _rows(shape)
    flat = [a.reshape(rows, last) for a in ins]
    n_in = len(ins)

    def body(*refs):
        res = fn(*[r[...] for r in refs[:n_in]])
        if n_out == 1:
            res = (res,)
        for r, v in zip(refs[n_in:], res):
            r[...] = v.astype(r.dtype)

    spec = pl.BlockSpec((tr, last), lambda i: (i, 0))
    outs = pl.pallas_call(
        body, name=name, grid=(rows // tr,),
        in_specs=[spec] * n_in, out_specs=[spec] * n_out,
        out_shape=[jax.ShapeDtypeStruct((rows, last), out_dtype)] * n_out,
        compiler_params=_params(),
    )(*flat)
    return [o.reshape(shape) for o in outs]


def _add_pair(g, o, c_idx):
    nq, R, C = g.shape
    h = R // 2
    tr = _tile(h, 512)
    nb = h // tr

    def body(c_ref, g_ref, o_ref, out_ref):
        out_ref[...] = g_ref[...] + o_ref[...]

    return pl.pallas_call(
        body, name="add_pair",
        grid_spec=pltpu.PrefetchScalarGridSpec(
            num_scalar_prefetch=1, grid=(nq, nb),
            in_specs=[pl.BlockSpec((None, tr, C), lambda q, i, c: (q, c[0] * nb + i, 0)),
                      pl.BlockSpec((None, tr, C), lambda q, i, c: (q, i, 0))],
            out_specs=pl.BlockSpec((None, tr, C), lambda q, i, c: (q, i, 0))),
        out_shape=jax.ShapeDtypeStruct((nq, h, C), F32),
        compiler_params=_params(),
    )(c_idx.astype(jnp.int32).reshape(1), g, o)


def _add_chips(p, r, q_idx):
    _, H, C = p.shape
    tr = _tile(H, 512)

    def body(q_ref, p_ref, r0_ref, r1_ref, r2_ref, out_ref):
        out_ref[...] = (p_ref[...] + r0_ref[...]) + (r1_ref[...] + r2_ref[...])

    def arrived(k):
        return pl.BlockSpec((None, tr, C), lambda i, q: (k, i, 0))

    return pl.pallas_call(
        body, name="add_chips",
        grid_spec=pltpu.PrefetchScalarGridSpec(
            num_scalar_prefetch=1, grid=(H // tr,),
            in_specs=[pl.BlockSpec((None, tr, C), lambda i, q: (q[0], i, 0)), arrived(0), arrived(1), arrived(2)],
            out_specs=pl.BlockSpec((tr, C), lambda i, q: (i, 0))),
        out_shape=jax.ShapeDtypeStruct((H, C), F32),
        compiler_params=_params(),
    )(q_idx.astype(jnp.int32).reshape(1), p, r, r, r)


def _adamw(w, g, m, v):
    m = ADAM_B1 * m + (1.0 - ADAM_B1) * g
    v = ADAM_B2 * v + (1.0 - ADAM_B2) * jnp.square(g)
    m_hat = m / (1.0 - ADAM_B1 ** ADAM_STEP)
    v_hat = v / (1.0 - ADAM_B2 ** ADAM_STEP)
    delta = -ADAM_LR * (m_hat / (jnp.sqrt(v_hat) + ADAM_EPS) + ADAM_WD * w)
    return delta, m, v


def _place():
    x, y, c = lax.axis_index("x"), lax.axis_index("y"), lax.axis_index("c")
    chips = [(1 - x, y), (x, 1 - y), (1 - x, 1 - y)]
    return x, y, c, chips


def _remote(src, dst, ssem, rsem, k, dev):
    return pltpu.make_async_remote_copy(src_ref=src, dst_ref=dst, send_sem=ssem.at[k], recv_sem=rsem.at[k],
                                        device_id=dev, device_id_type=MESH)


def _gather_weights(shards):
    n = len(shards)
    halves = [s.shape[1] // 2 for s in shards]

    def body(*refs):
        src, out = refs[:n], refs[n:2 * n]
        ssem, rsem = refs[2 * n:]
        x, y, c, chips = _place()
        me_q = 2 * x + y
        sib = (x, y, 1 - c)

        def half(a, q, cc):
            return out[a].at[q, :, pl.ds(cc * halves[a], halves[a]), :]

        first = []
        for a in range(n):
            mine = src[a].at[:, pl.ds(c * halves[a], halves[a]), :]
            for r, chip in enumerate(chips):
                first.append(_remote(mine, half(a, me_q, c), ssem, rsem, a * 3 + r, (*chip, c)))
        for cp in first:
            cp.start()
        passed = []
        for a in range(n):
            for r, chip in enumerate(chips):
                q = 2 * chip[0] + chip[1]
                k = a * 3 + r
                _remote(half(a, q, c), half(a, q, c), ssem, rsem, k, (*chip, c)).wait_recv()
                cp = _remote(half(a, q, c), half(a, q, c), ssem, rsem, 3 * n + k, sib)
                cp.start()
                passed.append(cp)
        for a in range(n):
            for r, chip in enumerate(chips):
                q = 2 * chip[0] + chip[1]
                _remote(half(a, q, 1 - c), half(a, q, 1 - c), ssem, rsem, 3 * n + a * 3 + r, sib).wait_recv()
        for cp in first + passed:
            cp.wait_send()

    return pl.pallas_call(
        body, name="gather_weights",
        in_specs=[HBM_SPEC] * n, out_specs=[HBM_SPEC] * n,
        out_shape=[jax.ShapeDtypeStruct((N_CHIPS,) + s.shape, s.dtype) for s in shards],
        scratch_shapes=[pltpu.SemaphoreType.DMA((6 * n,)), pltpu.SemaphoreType.DMA((6 * n,))],
        compiler_params=_params(has_side_effects=True),
    )(*shards)


def _gather_over_ici(shards):
    n = len(shards)
    halves = [s.shape[1] // 2 for s in shards]

    def copies(src, out, ssem, rsem):
        x, y, c, chips = _place()
        me_q = 2 * x + y
        res = []
        for a in range(n):
            rows = pl.ds(c * halves[a], halves[a])
            mine = src[a].at[:, rows, :]
            for r, chip in enumerate(chips):
                dev = (*chip, c)
                res.append((_remote(mine, out[a].at[me_q, :, rows, :], ssem, rsem, a * 3 + r, dev),
                            _remote(mine, out[a].at[2 * chip[0] + chip[1], :, rows, :], ssem, rsem, a * 3 + r, dev)))
        return res

    return _Hosted(list(shards), [jax.ShapeDtypeStruct((N_CHIPS,) + s.shape, s.dtype) for s in shards], 3 * n, copies)


def _pass_over_d2d(gathered):
    n = len(gathered)
    halves = [g.shape[2] // 2 for g in gathered]

    def copies(_, out, ssem, rsem):
        x, y, c, chips = _place()
        sib = (x, y, 1 - c)
        res = []
        for a in range(n):
            for r, chip in enumerate(chips):
                q = 2 * chip[0] + chip[1]
                mine = out[a].at[q, :, pl.ds(c * halves[a], halves[a]), :]
                theirs = out[a].at[q, :, pl.ds((1 - c) * halves[a], halves[a]), :]
                res.append((_remote(mine, mine, ssem, rsem, a * 3 + r, sib),
                            _remote(theirs, theirs, ssem, rsem, a * 3 + r, sib)))
        return res

    return _Hosted(list(gathered), [jax.ShapeDtypeStruct(g.shape, g.dtype) for g in gathered], 3 * n, copies,
                   in_place=True)


def _pass_to_sibling(gathered):
    n = len(gathered)
    halves = [g.shape[2] // 2 for g in gathered]

    def body(*refs):
        out = refs[n:2 * n]
        ssem, rsem = refs[2 * n:]
        x, y, c, chips = _place()
        sib = (x, y, 1 - c)

        def half(a, q, cc):
            return out[a].at[q, :, pl.ds(cc * halves[a], halves[a]), :]

        cps = []
        for a in range(n):
            for r, chip in enumerate(chips):
                q = 2 * chip[0] + chip[1]
                cps.append(_remote(half(a, q, c), half(a, q, c), ssem, rsem, a * 3 + r, sib))
        for cp in cps:
            cp.start()
        for a in range(n):
            for r, chip in enumerate(chips):
                q = 2 * chip[0] + chip[1]
                _remote(half(a, q, 1 - c), half(a, q, 1 - c), ssem, rsem, a * 3 + r, sib).wait_recv()
        for cp in cps:
            cp.wait_send()

    return pl.pallas_call(
        body, name="pass_to_sibling",
        in_specs=[HBM_SPEC] * n, out_specs=[HBM_SPEC] * n,
        out_shape=[jax.ShapeDtypeStruct(g.shape, g.dtype) for g in gathered],
        input_output_aliases={a: a for a in range(n)},
        scratch_shapes=[pltpu.SemaphoreType.DMA((3 * n,)), pltpu.SemaphoreType.DMA((3 * n,))],
        compiler_params=_params(has_side_effects=True),
    )(*gathered)


def _scatter_over_ici(parts):
    n = len(parts)

    def copies(src, out, ssem, rsem):
        x, y, c, chips = _place()
        res = []
        for a in range(n):
            for r, chip in enumerate(chips):
                cp = _remote(src[a].at[2 * chip[0] + chip[1]], out[a].at[r], ssem, rsem, a * 3 + r, (*chip, c))
                res.append((cp, cp))
        return res

    return _Hosted(list(parts), [jax.ShapeDtypeStruct((3,) + p.shape[1:], F32) for p in parts], 3 * n, copies)


def _swap_over_d2d(grads):
    n = len(grads)
    halves = [g.shape[1] // 2 for g in grads]

    def copies(src, out, ssem, rsem):
        x, y, c, _ = _place()
        res = []
        for a in range(n):
            cp = _remote(src[a].at[:, pl.ds((1 - c) * halves[a], halves[a]), :], out[a], ssem, rsem, a, (x, y, 1 - c))
            res.append((cp, cp))
        return res

    return _Hosted(list(grads), [jax.ShapeDtypeStruct((N_CHIPS, h, g.shape[2]), F32) for g, h in zip(grads, halves)],
                   n, copies)


def _swap_halves(grads):
    n = len(grads)
    halves = [g.shape[1] // 2 for g in grads]

    def body(*refs):
        src, out = refs[:n], refs[n:2 * n]
        ssem, rsem = refs[2 * n:]
        x, y, c, _ = _place()
        cps = [_remote(src[a].at[:, pl.ds((1 - c) * halves[a], halves[a]), :], out[a], ssem, rsem, a, (x, y, 1 - c))
               for a in range(n)]
        for cp in cps:
            cp.start()
        for cp in cps:
            cp.wait()

    return pl.pallas_call(
        body, name="swap_halves",
        in_specs=[HBM_SPEC] * n, out_specs=[HBM_SPEC] * n,
        out_shape=[jax.ShapeDtypeStruct((N_CHIPS, h, g.shape[2]), F32) for g, h in zip(grads, halves)],
        scratch_shapes=[pltpu.SemaphoreType.DMA((n,)), pltpu.SemaphoreType.DMA((n,))],
        compiler_params=_params(has_side_effects=True),
    )(*grads)


def _scatter_chips(parts):
    n = len(parts)

    def body(*refs):
        src, out = refs[:n], refs[n:2 * n]
        ssem, rsem = refs[2 * n:]
        x, y, c, chips = _place()
        cps = []
        for a in range(n):
            for r, chip in enumerate(chips):
                cps.append(_remote(src[a].at[2 * chip[0] + chip[1]], out[a].at[r], ssem, rsem, a * 3 + r, (*chip, c)))
        for cp in cps:
            cp.start()
        for cp in cps:
            cp.wait()

    return pl.pallas_call(
        body, name="scatter_chips",
        in_specs=[HBM_SPEC] * n, out_specs=[HBM_SPEC] * n,
        out_shape=[jax.ShapeDtypeStruct((3,) + p.shape[1:], F32) for p in parts],
        scratch_shapes=[pltpu.SemaphoreType.DMA((3 * n,)), pltpu.SemaphoreType.DMA((3 * n,))],
        compiler_params=_params(has_side_effects=True),
    )(*parts)


def _swap_reduced_over_d2d(reduced):
    n = len(reduced)

    def copies(src, out, ssem, rsem):
        x, y, c, _ = _place()
        res = []
        for a in range(n):
            cp = _remote(src[a], out[a], ssem, rsem, a, (x, y, 1 - c))
            res.append((cp, cp))
        return res

    return _Hosted(list(reduced), [jax.ShapeDtypeStruct(r.shape, F32) for r in reduced], n, copies)


def _swap_reduced(reduced):
    n = len(reduced)

    def body(*refs):
        src, out = refs[:n], refs[n:2 * n]
        ssem, rsem = refs[2 * n:]
        x, y, c, _ = _place()
        cps = [_remote(src[a], out[a], ssem, rsem, a, (x, y, 1 - c)) for a in range(n)]
        for cp in cps:
            cp.start()
        for cp in cps:
            cp.wait()

    return pl.pallas_call(
        body, name="swap_reduced",
        in_specs=[HBM_SPEC] * n, out_specs=[HBM_SPEC] * n,
        out_shape=[jax.ShapeDtypeStruct(r.shape, F32) for r in reduced],
        scratch_shapes=[pltpu.SemaphoreType.DMA((n,)), pltpu.SemaphoreType.DMA((n,))],
        compiler_params=_params(has_side_effects=True),
    )(*reduced)


def _allreduce_small(buf, hosted=None):
    R, L = buf.shape

    def body(*refs):
        (buf_ref, out_ref, pair_ref, chip_ref, ssem, rsem), start, wait = _host(hosted, refs, 1, 1, True, True)
        start()
        x, y, c, chips = _place()
        me_q = 2 * x + y
        pair_ref[c] = buf_ref[...]
        to_sib = _remote(buf_ref, pair_ref.at[c], ssem, rsem, 0, (x, y, 1 - c))
        to_sib.start()
        _remote(buf_ref, pair_ref.at[1 - c], ssem, rsem, 0, (x, y, 1 - c)).wait_recv()
        chip_ref[me_q] = pair_ref[0] + pair_ref[1]
        cps = [_remote(chip_ref.at[me_q], chip_ref.at[me_q], ssem, rsem, 1 + r, (*chip, c))
               for r, chip in enumerate(chips)]
        for cp in cps:
            cp.start()
        for r, chip in enumerate(chips):
            q = 2 * chip[0] + chip[1]
            _remote(chip_ref.at[q], chip_ref.at[q], ssem, rsem, 1 + r, (*chip, c)).wait_recv()
        out_ref[...] = (chip_ref[0] + chip_ref[1]) + (chip_ref[2] + chip_ref[3])
        to_sib.wait_send()
        for cp in cps:
            cp.wait_send()
        wait()

    h_ins = hosted.ins if hosted else []
    res = pl.pallas_call(
        body, name="allreduce_small",
        in_specs=[VMEM_SPEC] + [HBM_SPEC] * len(h_ins), out_specs=[VMEM_SPEC] + [HBM_SPEC] * len(h_ins),
        out_shape=[jax.ShapeDtypeStruct((R, L), F32)] + (hosted.out_shapes if hosted else []),
        scratch_shapes=[pltpu.VMEM((2, R, L), F32), pltpu.VMEM((N_CHIPS, R, L), F32),
                        pltpu.SemaphoreType.DMA((4,)), pltpu.SemaphoreType.DMA((4,))]
        + (hosted.sems() if hosted else []),
        compiler_params=_params(has_side_effects=True),
    )(buf, *h_ins)
    return res[0], res[1:]


def _pack(arrays):
    flat = jnp.concatenate([a.reshape(-1) for a in arrays])
    pad = (-flat.shape[0]) % (8 * LANES)
    return jnp.pad(flat, (0, pad)).reshape(-1, LANES)


def _unpack(buf, like):
    flat = buf.reshape(-1)
    out, off = [], 0
    for a in like:
        out.append(flat[off:off + a.size].reshape(a.shape))
        off += a.size
    return out


def _block_diag(pw):
    rows = []
    for gi in range(len(POOL_WINDOWS)):
        blocks = [pw[gi] if gj == gi else jnp.zeros_like(pw[gi]) for gj in range(len(POOL_WINDOWS))]
        rows.append(jnp.concatenate(blocks, axis=1))
    return jnp.concatenate(rows, axis=0)


def kernel(x, norm1, w_in, pool_w, pool_scale, sg_norm, sg_w, sg_b, w_out, norm2, w_up, w_down, final_norm, loss_target, m_norm1, m_w_in, m_pool_w, m_pool_scale, m_sg_norm, m_sg_w, m_sg_b, m_w_out, m_norm2, m_w_up, m_w_down, m_final_norm, v_norm1, v_w_in, v_pool_w, v_pool_scale, v_sg_norm, v_sg_w, v_sg_b, v_w_out, v_norm2, v_w_up, v_w_down, v_final_norm):
    depth = norm1.shape[0]
    T = x.shape[1]
    xs = x.reshape(T, D_MODEL)
    target = loss_target.reshape(T, D_MODEL)

    assert depth == 2
    c_idx = lax.axis_index("c")
    q_idx = 2 * lax.axis_index("x") + lax.axis_index("y")
    own = [w.astype(BF16) for w in (w_in, w_out, w_up, w_down)]
    gathered = {(0, 0): _gather_weights([own[0][:1]])[0]}

    def full(a, l, axis):
        blocks = [jnp.where(q_idx == q, own[a][l], gathered[(a, l)][q, 0]) for q in range(N_CHIPS)]
        return jnp.concatenate(blocks, axis=axis)

    half_way = {}

    def gather_behind(call, keys, at_once):
        res, over_ici = call(_gather_over_ici([own[a][l:l + 1] for a, l in keys]))
        gathered.update(zip(keys[:at_once], _pass_to_sibling(over_ici[:at_once])))
        half_way.update(zip(keys[at_once:], over_ici[at_once:]))
        return res

    def pass_behind(call, keys):
        res, done = call(_pass_over_d2d([half_way.pop(k) for k in keys]))
        gathered.update(zip(keys, done))
        return res

    tril = jnp.tril(jnp.ones((CHUNK, CHUNK), F32))
    saved = []
    cur = xs
    wi, wo, wu, wd = {}, {}, {}, {}
    for l in range(depth):
        wbd = _block_diag(pool_w[l]).astype(BF16)
        wm = sg_w[l] * tril
        wm_s = wm.reshape(SG_HEADS * CHUNK, CHUNK).astype(BF16)
        wmt_s = jnp.swapaxes(wm, 1, 2).reshape(SG_HEADS * CHUNK, CHUNK).astype(BF16)
        bias = jnp.repeat(sg_b[l].T, SB_HD, axis=1)
        n1, n2 = norm1[l][None], norm2[l][None]
        psc, sgn = pool_scale[l][None], sg_norm[l][None]
        wi[l] = full(0, l, 1)
        proj, h, qkv = _inproj_fwd(cur, n1, wi[l])
        ya = _pool_fwd(proj, wbd, psc)
        yb = _sg_fwd(proj, wm_s, bias, sgn)
        if l == 0:
            yc = gather_behind(lambda hosted: _attn_fwd(qkv, hosted), [(1, 0), (2, 0), (3, 0)], 1)
            wo[l] = full(1, l, 0)
            x1, ymix = pass_behind(lambda hosted: _outproj_fwd(cur, ya, yb, yc, wo[l], hosted), [(2, 0), (3, 0)])
        else:
            yc = pass_behind(lambda hosted: _attn_fwd(qkv, hosted), [(1, l), (2, l), (3, l)])
            wo[l] = full(1, l, 0)
            (x1, ymix), _ = _outproj_fwd(cur, ya, yb, yc, wo[l])
        wu[l], wd[l] = full(2, l, 1), full(3, l, 0)
        if l == 0:
            x2, h2, act = gather_behind(lambda hosted: _mlp_fwd(x1, n2, wu[l], wd[l], hosted),
                                        [(0, 1), (1, 1), (2, 1), (3, 1)], 1)
        else:
            (x2, h2, act), _ = _mlp_fwd(x1, n2, wu[l], wd[l])
        saved.append(dict(x0=cur, x1=x1, proj=proj, h=h, qkv=qkv, yc=yc, ymix=ymix, h2=h2, act=act,
                          wbd=wbd, wm_s=wm_s, wmt_s=wmt_s, bias=bias, n1=n1, n2=n2, psc=psc, sgn=sgn))
        cur = x2

    loss_row, dcur, d_final = _loss_head(cur, final_norm[None], target)

    small = [None] * depth
    grads, parts, reduced = {}, {}, {}

    def pair_up(keys, swapped):
        parts.update({k: _add_pair(grads[k], o, c_idx) for k, o in zip(keys, swapped)})

    def chip_up(keys, arrived):
        reduced.update({k: _add_chips(parts[k], r, q_idx) for k, r in zip(keys, arrived)})

    for l in reversed(range(depth)):
        s = saved[l]
        if l == 0:
            keys = [(2, 1), (3, 1)]
            (dx1, du, d_n2), arrived = _mlp_bwd(dcur, s["x1"], s["n2"], s["act"], wu[l], wd[l],
                                                _scatter_over_ici([parts[k] for k in keys]))
            chip_up(keys, arrived)
        else:
            (dx1, du, d_n2), _ = _mlp_bwd(dcur, s["x1"], s["n2"], s["act"], wu[l], wd[l])
        grads[(2, l)] = _tn_matmul(s["h2"], du, "grad_w_up", n_split=N_CHIPS)
        grads[(3, l)] = _tn_matmul(s["act"], dcur, "grad_w_down")[0].reshape(N_CHIPS, D_FF // N_CHIPS, D_MODEL)
        dymix = _nt_matmul(dx1, wo[l])
        grads[(1, l)] = _tn_matmul(s["ymix"], dx1, "grad_w_out")[0].reshape(N_CHIPS, D_MODEL // N_CHIPS, D_MODEL)
        da_in, d_wbd, d_psc = _pool_bwd(s["proj"], dymix, s["wbd"], s["psc"])
        if l == 0:
            keys = [(0, 1), (1, 0), (2, 0), (3, 0)]
            (du_pre, dv_pre, d_wm, d_bias, d_sgn), swapped = _sg_bwd(
                s["proj"], dymix, s["wm_s"], s["wmt_s"], s["bias"], s["sgn"], _swap_over_d2d([grads[k] for k in keys]))
            pair_up(keys, swapped)
            keys = [(1, 1)] + keys
            (dq, dk, dv), arrived = _attn_bwd(s["qkv"], s["yc"], dymix, _scatter_over_ici([parts[k] for k in keys]))
            chip_up(keys, arrived)
        else:
            (du_pre, dv_pre, d_wm, d_bias, d_sgn), _ = _sg_bwd(s["proj"], dymix, s["wm_s"], s["wmt_s"], s["bias"], s["sgn"])
            keys = [(1, l), (2, l), (3, l)]
            (dq, dk, dv), swapped = _attn_bwd(s["qkv"], s["yc"], dymix, _swap_over_d2d([grads[k] for k in keys]))
            pair_up(keys, swapped)
        dproj, dx0, d_n1 = _inproj_bwd([da_in, du_pre, dv_pre, dq, dk, dv], wi[l], s["x0"], s["n1"], dx1)
        if l == 0:
            keys = sorted(reduced)
            g_in_l, swapped = _tn_matmul(s["h"], dproj, "grad_w_in",
                                         hosted=_swap_reduced_over_d2d([reduced[k] for k in keys]))
            theirs = dict(zip(keys, swapped))
        else:
            g_in_l = _tn_matmul(s["h"], dproj, "grad_w_in")
        grads[(0, l)] = g_in_l[0].reshape(D_MODEL, N_CHIPS, IN_COLS // N_CHIPS).transpose(1, 0, 2)
        d_pw = jnp.stack([d_wbd[gi * POOL_GW:(gi + 1) * POOL_GW, gi * POOL_GW:(gi + 1) * POOL_GW]
                          for gi in range(len(POOL_WINDOWS))])
        small[l] = dict(norm1=d_n1[0], pool_w=d_pw, pool_scale=d_psc[0], sg_norm=d_sgn[0],
                        sg_w=d_wm.reshape(SG_HEADS, CHUNK, CHUNK), sg_b=d_bias[:, :SG_HEADS].T, norm2=d_n2[0])
        dcur = dx0
    grad_x = dcur.reshape(x.shape)

    names = ["norm1", "pool_w", "pool_scale", "sg_norm", "sg_w", "sg_b", "norm2"]
    slot = jnp.zeros((1,), F32)
    small_w = [norm1, pool_w, pool_scale, sg_norm, sg_w, sg_b, norm2, final_norm, slot]
    small_m = [m_norm1, m_pool_w, m_pool_scale, m_sg_norm, m_sg_w, m_sg_b, m_norm2, m_final_norm, slot]
    small_v = [v_norm1, v_pool_w, v_pool_scale, v_sg_norm, v_sg_w, v_sg_b, v_norm2, v_final_norm, slot]
    small_g = [jnp.stack([small[l][k] for l in range(depth)]) for k in names] + [d_final[0], loss_row[0, :1]]
    keys = [(0, 0)]
    pair_up(keys, _swap_halves([grads[k] for k in keys]))
    g_packed, arrived = _allreduce_small(_pack(small_g), _scatter_over_ici([parts[k] for k in keys]))
    chip_up(keys, arrived)
    theirs.update(zip(keys, _swap_reduced([reduced[k] for k in keys])))

    def joined(a):
        layers = []
        for l in range(depth):
            mine, other = reduced[(a, l)], theirs[(a, l)]
            layers.append(jnp.where(c_idx == 0, jnp.concatenate([mine, other]), jnp.concatenate([other, mine])))
        return jnp.stack(layers)

    gw_in, gw_out, gw_up, gw_down = [joined(a) for a in range(4)]

    loss = _unpack(g_packed, small_w)[-1][0]
    s_delta, s_m, s_v = _elementwise(_adamw, "adamw_small", [_pack(small_w), g_packed, _pack(small_m), _pack(small_v)], 3)
    gs = dict(zip(names + ["final_norm"], _unpack(g_packed, small_w)))
    ds = dict(zip(names + ["final_norm"], _unpack(s_delta, small_w)))
    ms = dict(zip(names + ["final_norm"], _unpack(s_m, small_w)))
    vs = dict(zip(names + ["final_norm"], _unpack(s_v, small_w)))

    big_g = dict(w_in=gw_in, w_out=gw_out, w_up=gw_up, w_down=gw_down)
    big_w = dict(w_in=(w_in, m_w_in, v_w_in), w_out=(w_out, m_w_out, v_w_out),
                 w_up=(w_up, m_w_up, v_w_up), w_down=(w_down, m_w_down, v_w_down))
    for k, (w, m, v) in big_w.items():
        operands = [w, big_g[k], m, v]
        if k == "w_in":
            operands = [jnp.swapaxes(o, 1, 2) for o in operands]
        ds[k], ms[k], vs[k] = _elementwise(_adamw, "adamw_" + k, operands, 3)
        if k == "w_in":
            ds[k], ms[k], vs[k] = [jnp.swapaxes(o, 1, 2) for o in (ds[k], ms[k], vs[k])]
        gs[k] = big_g[k]

    order = ["norm1", "w_in", "pool_w", "pool_scale", "sg_norm", "sg_w", "sg_b", "w_out", "norm2", "w_up", "w_down",
             "final_norm"]
    return (loss, grad_x, *[gs[k] for k in order], *[ds[k] for k in order], *[ms[k] for k in order],
            *[vs[k] for k in order])
```

```python
import functools

import jax
import jax.numpy as jnp
from jax import lax
from jax.experimental import pallas as pl
from jax.experimental.pallas import tpu as pltpu

F32 = jnp.float32
BF16 = jnp.bfloat16
MESH = pl.DeviceIdType.MESH
AXES = ("x", "y", "c")

EPS = 1e-6
D_MODEL = 1024
POOL_WIDTH = 256
SG_WIDTH = 256
SB_WIDTH = 512
POOL_WINDOWS = (2, 4, 8, 16)
POOL_GW = 64
POOL_HALO = 16
CHUNK = 128
SG_HEADS = 4
SB_HD = 64
SB_SCALE = 0.125
IN_COLS = 2304
QKV_OFF = 768
D_FF = 4096
N_CHIPS = 4
LANES = 128
VMEM_LIMIT = 56 * 1024 * 1024
MLP_CHUNK = 512
ATTN_TILE = 256
UNDERFLOW = -104.0

ADAM_LR = 0.001
ADAM_B1 = 0.9
ADAM_B2 = 0.999
ADAM_EPS = 1e-08
ADAM_WD = 0.01
ADAM_STEP = 10

HBM_SPEC = pl.BlockSpec(memory_space=pl.ANY)
VMEM_SPEC = pl.BlockSpec(memory_space=pltpu.VMEM)


def _params(**kw):
    return pltpu.CompilerParams(vmem_limit_bytes=VMEM_LIMIT, **kw)


def _tile(n, pref):
    if n <= pref:
        return n
    for t in range(pref - pref % LANES, 0, -LANES):
        if n % t == 0:
            return t
    raise ValueError((n, pref))


def _nn(a, b):
    return jnp.dot(a, b, preferred_element_type=F32)


def _nt(a, b):
    return lax.dot_general(a, b, (((1,), (1,)), ((), ())), preferred_element_type=F32)


def _tn(a, b):
    return lax.dot_general(a, b, (((0,), (0,)), ((), ())), preferred_element_type=F32)


def _rms_fwd(x, g):
    r = lax.rsqrt(jnp.mean(x * x, axis=-1, keepdims=True) + EPS)
    xhat = x * r
    return xhat * g, xhat, r


def _rms_bwd(dy, xhat, r, g):
    dxhat = dy * g
    dx = r * (dxhat - xhat * jnp.mean(dxhat * xhat, axis=-1, keepdims=True))
    return dx, dy * xhat


_GELU_K = 0.7978845608028654
_GELU_C = 0.044715


def _gelu(x):
    return 0.5 * x * (1.0 + jnp.tanh(_GELU_K * (x + _GELU_C * x * x * x)))


def _gelu_grad(x):
    t = jnp.tanh(_GELU_K * (x + _GELU_C * x * x * x))
    return 0.5 * (1.0 + t) + 0.5 * x * (1.0 - t * t) * _GELU_K * (1.0 + 3.0 * _GELU_C * x * x)


def _inproj_fwd(x, g, w):
    T, D = x.shape
    N = w.shape[1]
    tt = _tile(T, 512)

    def body(x_ref, g_ref, w_ref, proj_ref, h_ref, qkv_ref):
        h, _, _ = _rms_fwd(x_ref[...], g_ref[...])
        hb = h.astype(BF16)
        h_ref[...] = hb
        p = _nn(hb, w_ref[...])
        proj_ref[...] = p[:, :QKV_OFF]
        qkv_ref[...] = p[:, QKV_OFF:].astype(BF16)

    return pl.pallas_call(
        body, name="inproj_fwd", grid=(T // tt,),
        in_specs=[pl.BlockSpec((tt, D), lambda i: (i, 0)), pl.BlockSpec((1, D), lambda i: (0, 0)),
                  pl.BlockSpec((D, N), lambda i: (0, 0))],
        out_specs=[pl.BlockSpec((tt, QKV_OFF), lambda i: (i, 0)), pl.BlockSpec((tt, D), lambda i: (i, 0)),
                   pl.BlockSpec((tt, N - QKV_OFF), lambda i: (i, 0))],
        out_shape=[jax.ShapeDtypeStruct((T, QKV_OFF), F32), jax.ShapeDtypeStruct((T, D), BF16),
                   jax.ShapeDtypeStruct((T, N - QKV_OFF), BF16)],
        compiler_params=_params(),
    )(x, g, w)


def _inproj_bwd(pieces, w, x, g, dres):
    T, D = x.shape
    N = w.shape[1]
    tt = _tile(T, 512)
    widths = [p.shape[1] for p in pieces]
    offs = [sum(widths[:k]) for k in range(len(widths))]
    assert sum(widths) == N
    n_p = len(pieces)

    def body(*refs):
        p_refs = refs[:n_p]
        w_ref, x_ref, g_ref, dres_ref, dproj_ref, dx_ref, dg_ref = refs[n_p:]
        for p_ref, o, wd in zip(p_refs, offs, widths):
            dproj_ref[:, o:o + wd] = p_ref[...].astype(BF16)
        dh = _nt(dproj_ref[...], w_ref[...])
        gv = g_ref[...]
        _, xhat, r = _rms_fwd(x_ref[...], gv)
        dx, dgrow = _rms_bwd(dh, xhat, r, gv)
        dx_ref[...] = dres_ref[...] + dx

        @pl.when(pl.program_id(0) == 0)
        def _():
            dg_ref[...] = jnp.zeros_like(dg_ref)

        dg_ref[...] += jnp.sum(dgrow, axis=0, keepdims=True)

    return pl.pallas_call(
        body, name="inproj_bwd", grid=(T // tt,),
        in_specs=[pl.BlockSpec((tt, wd), lambda i: (i, 0)) for wd in widths] + [
            pl.BlockSpec((D, N), lambda i: (0, 0)), pl.BlockSpec((tt, D), lambda i: (i, 0)),
            pl.BlockSpec((1, D), lambda i: (0, 0)), pl.BlockSpec((tt, D), lambda i: (i, 0))],
        out_specs=[pl.BlockSpec((tt, N), lambda i: (i, 0)), pl.BlockSpec((tt, D), lambda i: (i, 0)),
                   pl.BlockSpec((1, D), lambda i: (0, 0))],
        out_shape=[jax.ShapeDtypeStruct((T, N), BF16), jax.ShapeDtypeStruct((T, D), F32),
                   jax.ShapeDtypeStruct((1, D), F32)],
        compiler_params=_params(),
    )(*pieces, w, x, g, dres)


def _pool_select(s2, s4, s8, s16, grp):
    return jnp.where(grp == 0, s2, jnp.where(grp == 1, s4, jnp.where(grp == 2, s8, s16)))


def _pool_count(t_glob, grp):
    win = jnp.where(grp == 0, 2, jnp.where(grp == 1, 4, jnp.where(grp == 2, 8, 16)))
    return jnp.minimum(t_glob + 1, win).astype(F32)


def _pool_diff(a, halo, base, tt):
    n = tt + POOL_HALO
    ext = jnp.concatenate([halo, a], axis=0)
    s2 = ext + pltpu.roll(ext, 1, 0)
    s4 = s2 + pltpu.roll(s2, 2, 0)
    s8 = s4 + pltpu.roll(s4, 4, 0)
    s16 = s8 + pltpu.roll(s8, 8, 0)
    grp = lax.broadcasted_iota(jnp.int32, (n, POOL_WIDTH), 1) // POOL_GW
    t_glob = lax.broadcasted_iota(jnp.int32, (n, POOL_WIDTH), 0) + (base - POOL_HALO)
    pooled = _pool_select(s2, s4, s8, s16, grp) / _pool_count(t_glob, grp)
    return pooled[POOL_HALO:] - a


def _pool_specs(T, tt):
    hb = tt // POOL_HALO
    return [pl.BlockSpec((tt, POOL_WIDTH), lambda i: (i, 0)),
            pl.BlockSpec((POOL_HALO, POOL_WIDTH), lambda i: (jnp.maximum(i * hb - 1, 0), 0))]


def _pool_fwd(proj, wbd, scale):
    T = proj.shape[0]
    tt = _tile(T, 512)

    def body(a_ref, halo_ref, w_ref, sc_ref, y_ref):
        i = pl.program_id(0)
        halo = jnp.where(i > 0, halo_ref[...], 0.0)
        d = _pool_diff(a_ref[...], halo, i * tt, tt)
        y_ref[...] = _nn(d.astype(BF16), w_ref[...]) * sc_ref[...]

    return pl.pallas_call(
        body, name="pool_fwd", grid=(T // tt,),
        in_specs=_pool_specs(T, tt) + [pl.BlockSpec((POOL_WIDTH, POOL_WIDTH), lambda i: (0, 0)),
                                       pl.BlockSpec((1, POOL_WIDTH), lambda i: (0, 0))],
        out_specs=pl.BlockSpec((tt, POOL_WIDTH), lambda i: (i, 0)),
        out_shape=jax.ShapeDtypeStruct((T, POOL_WIDTH), F32),
        compiler_params=_params(),
    )(proj, proj, wbd, scale)


def _pool_bwd(proj, dymix, wbd, scale):
    T = proj.shape[0]
    tt = _tile(T, 512)
    hb = tt // POOL_HALO
    nblk = T // tt
    n = tt + POOL_HALO

    def body(a_ref, halo_ref, dy_ref, dyn_ref, w_ref, sc_ref, da_ref, dw_ref, dsc_ref):
        i = pl.program_id(0)
        halo = jnp.where(i > 0, halo_ref[...], 0.0)
        d = _pool_diff(a_ref[...], halo, i * tt, tt)
        db = d.astype(BF16)
        wv = w_ref[...]
        sc = sc_ref[...]
        dy = dy_ref[...]
        dys = dy * sc

        @pl.when(i == 0)
        def _():
            dw_ref[...] = jnp.zeros_like(dw_ref)
            dsc_ref[...] = jnp.zeros_like(dsc_ref)

        dsc_ref[...] += jnp.sum(dy * _nn(db, wv), axis=0, keepdims=True)
        dw_ref[...] += _tn(db, dys.astype(BF16))
        dyn = jnp.where(i < nblk - 1, dyn_ref[...], 0.0) * sc
        dd = _nt(jnp.concatenate([dys, dyn], axis=0).astype(BF16), wv)
        grp = lax.broadcasted_iota(jnp.int32, (n, POOL_WIDTH), 1) // POOL_GW
        t_glob = lax.broadcasted_iota(jnp.int32, (n, POOL_WIDTH), 0) + i * tt
        e = dd / _pool_count(t_glob, grp)
        r2 = e + pltpu.roll(e, n - 1, 0)
        r4 = r2 + pltpu.roll(r2, n - 2, 0)
        r8 = r4 + pltpu.roll(r4, n - 4, 0)
        r16 = r8 + pltpu.roll(r8, n - 8, 0)
        da_ref[...] = (_pool_select(r2, r4, r8, r16, grp) - dd)[:tt].astype(BF16)

    return pl.pallas_call(
        body, name="pool_bwd", grid=(nblk,),
        in_specs=_pool_specs(T, tt) + [
            pl.BlockSpec((tt, POOL_WIDTH), lambda i: (i, 0)),
            pl.BlockSpec((POOL_HALO, POOL_WIDTH), lambda i: (jnp.minimum((i + 1) * hb, T // POOL_HALO - 1), 0)),
            pl.BlockSpec((POOL_WIDTH, POOL_WIDTH), lambda i: (0, 0)), pl.BlockSpec((1, POOL_WIDTH), lambda i: (0, 0))],
        out_specs=[pl.BlockSpec((tt, POOL_WIDTH), lambda i: (i, 0)),
                   pl.BlockSpec((POOL_WIDTH, POOL_WIDTH), lambda i: (0, 0)),
                   pl.BlockSpec((1, POOL_WIDTH), lambda i: (0, 0))],
        out_shape=[jax.ShapeDtypeStruct((T, POOL_WIDTH), BF16),
                   jax.ShapeDtypeStruct((POOL_WIDTH, POOL_WIDTH), F32),
                   jax.ShapeDtypeStruct((1, POOL_WIDTH), F32)],
        compiler_params=_params(),
    )(proj, proj, dymix, dymix, wbd, scale)


def _head_select(stacked, grp):
    out = jnp.where(grp == 0, stacked[0:CHUNK], 0.0)
    for h in range(1, SG_HEADS):
        out = out + jnp.where(grp == h, stacked[h * CHUNK:(h + 1) * CHUNK], 0.0)
    return out


def _sg_specs(tt):
    return [pl.BlockSpec((tt, SG_WIDTH), lambda i: (i, 1)), pl.BlockSpec((tt, SG_WIDTH), lambda i: (i, 2))]


def _sg_fwd(proj, wm, bias, g):
    T = proj.shape[0]
    tt = _tile(T, 512)

    def body(u_ref, v_ref, wm_ref, b_ref, g_ref, y_ref):
        zu = _gelu(u_ref[...])
        vn, _, _ = _rms_fwd(_gelu(v_ref[...]), g_ref[...])
        grp = lax.broadcasted_iota(jnp.int32, (CHUNK, SG_WIDTH), 1) // SB_HD
        for n in range(tt // CHUNK):
            rows = slice(n * CHUNK, (n + 1) * CHUNK)
            sv = _head_select(_nn(wm_ref[...], vn[rows].astype(BF16)), grp) + b_ref[...]
            y_ref[rows, :] = zu[rows] * sv

    return pl.pallas_call(
        body, name="sg_fwd", grid=(T // tt,),
        in_specs=_sg_specs(tt) + [pl.BlockSpec((SG_HEADS * CHUNK, CHUNK), lambda i: (0, 0)),
                                  pl.BlockSpec((CHUNK, SG_WIDTH), lambda i: (0, 0)),
                                  pl.BlockSpec((1, SG_WIDTH), lambda i: (0, 0))],
        out_specs=pl.BlockSpec((tt, SG_WIDTH), lambda i: (i, 0)),
        out_shape=jax.ShapeDtypeStruct((T, SG_WIDTH), F32),
        compiler_params=_params(),
    )(proj, proj, wm, bias, g)


def _sg_bwd(proj, dymix, wm, wmt, bias, g, hosted=None):
    T = proj.shape[0]
    tt = _tile(T, 512)
    nblk = T // tt

    def body(*refs):
        i = pl.program_id(0)
        (u_ref, v_ref, dy_ref, wm_ref, wmt_ref, b_ref, g_ref, du_ref, dv_ref, dw_ref, db_ref, dg_ref,
         dvn_ref, dbias_ref), start, wait = _host(hosted, refs, 7, 5, i == 0, i == nblk - 1)
        start()
        up, vp = u_ref[...], v_ref[...]
        gv = g_ref[...]
        zu, zv = _gelu(up), _gelu(vp)
        vn, xhat, r = _rms_fwd(zv, gv)
        gu = _gelu_grad(up)
        grp = lax.broadcasted_iota(jnp.int32, (CHUNK, SG_WIDTH), 1) // SB_HD

        @pl.when(i == 0)
        def _():
            dw_ref[...] = jnp.zeros_like(dw_ref)
            dbias_ref[...] = jnp.zeros_like(dbias_ref)
            dg_ref[...] = jnp.zeros_like(dg_ref)

        for n in range(tt // CHUNK):
            rows = slice(n * CHUNK, (n + 1) * CHUNK)
            vc = vn[rows].astype(BF16)
            sv = _head_select(_nn(wm_ref[...], vc), grp) + b_ref[...]
            dy = dy_ref[rows, :]
            du_ref[rows, :] = (dy * sv * gu[rows]).astype(BF16)
            dsv = dy * zu[rows]
            dsvb = dsv.astype(BF16)
            dvn_ref[rows, :] = _head_select(_nn(wmt_ref[...], dsvb), grp)
            stacked = jnp.concatenate([jnp.where(grp == h, dsv, 0.0) for h in range(SG_HEADS)], axis=0)
            dw_ref[...] += _nt(stacked.astype(BF16), vc)
            dbias_ref[...] += dsv

        dzv, dgrow = _rms_bwd(dvn_ref[...], xhat, r, gv)
        dg_ref[...] += jnp.sum(dgrow, axis=0, keepdims=True)
        dv_ref[...] = (dzv * _gelu_grad(vp)).astype(BF16)

        @pl.when(i == nblk - 1)
        def _():
            t_i = lax.broadcasted_iota(jnp.int32, (SG_HEADS * CHUNK, CHUNK), 0) % CHUNK
            s_i = lax.broadcasted_iota(jnp.int32, (SG_HEADS * CHUNK, CHUNK), 1)
            dw_ref[...] = jnp.where(s_i <= t_i, dw_ref[...], 0.0)
            lane = lax.broadcasted_iota(jnp.int32, (CHUNK, LANES), 1)
            acc = jnp.zeros((CHUNK, LANES), F32)
            for h in range(SG_HEADS):
                tot = jnp.sum(jnp.where(grp == h, dbias_ref[...], 0.0), axis=1, keepdims=True)
                acc = acc + jnp.where(lane == h, tot, 0.0)
            db_ref[...] = acc

        wait()

    h_ins = hosted.ins if hosted else []
    res = pl.pallas_call(
        body, name="sg_bwd_hosting" if hosted else "sg_bwd", grid=(nblk,),
        in_specs=_sg_specs(tt) + [pl.BlockSpec((tt, SG_WIDTH), lambda i: (i, 1)),
                                  pl.BlockSpec((SG_HEADS * CHUNK, CHUNK), lambda i: (0, 0)),
                                  pl.BlockSpec((SG_HEADS * CHUNK, CHUNK), lambda i: (0, 0)),
                                  pl.BlockSpec((CHUNK, SG_WIDTH), lambda i: (0, 0)),
                                  pl.BlockSpec((1, SG_WIDTH), lambda i: (0, 0))] + [HBM_SPEC] * len(h_ins),
        out_specs=[pl.BlockSpec((tt, SG_WIDTH), lambda i: (i, 0)), pl.BlockSpec((tt, SG_WIDTH), lambda i: (i, 0)),
                   pl.BlockSpec((SG_HEADS * CHUNK, CHUNK), lambda i: (0, 0)),
                   pl.BlockSpec((CHUNK, LANES), lambda i: (0, 0)), pl.BlockSpec((1, SG_WIDTH), lambda i: (0, 0))]
        + [HBM_SPEC] * len(h_ins),
        out_shape=[jax.ShapeDtypeStruct((T, SG_WIDTH), BF16), jax.ShapeDtypeStruct((T, SG_WIDTH), BF16),
                   jax.ShapeDtypeStruct((SG_HEADS * CHUNK, CHUNK), F32),
                   jax.ShapeDtypeStruct((CHUNK, LANES), F32), jax.ShapeDtypeStruct((1, SG_WIDTH), F32)]
        + (hosted.out_shapes if hosted else []),
        scratch_shapes=[pltpu.VMEM((tt, SG_WIDTH), F32), pltpu.VMEM((CHUNK, SG_WIDTH), F32)]
        + (hosted.sems() if hosted else []),
        compiler_params=_params(has_side_effects=hosted is not None),
    )(proj, proj, dymix, wm, wmt, bias, g, *h_ins)
    return res[:5], res[5:]


def _split_dot(x, u):
    hi = x.astype(BF16)
    lo = (x - hi.astype(F32)).astype(BF16)
    return _nn(hi, u) + _nn(lo, u)


def _sb_logits(z):
    lb = jnp.minimum(z, 0.0) - jnp.log(1.0 + jnp.exp(-jnp.abs(z)))
    return lb, lb - z


ATTN_STRIP = 32
ATTN_SUBS = 2


def _by_strips(n_rows, fn):
    parts = None
    for r in range(0, n_rows, ATTN_STRIP):
        res = fn(slice(r, r + ATTN_STRIP))
        parts = [[v] for v in res] if parts is None else [p + [v] for p, v in zip(parts, res)]
    return [jnp.concatenate(p, axis=0) for p in parts]


def _attn_qkv_specs(tq, T):
    base = (IN_COLS - 3 * SB_WIDTH - QKV_OFF) // LANES
    nb = SB_WIDTH // LANES
    return [pl.BlockSpec((tq, LANES), lambda p, i: (i, base + p)),
            pl.BlockSpec((T, LANES), lambda p, i: (0, base + nb + p)),
            pl.BlockSpec((T, LANES), lambda p, i: (0, base + 2 * nb + p))]


class _Hosted:
    def __init__(self, ins, out_shapes, n_sems, copies, in_place=False):
        self.ins, self.out_shapes, self.n_sems, self.copies = ins, out_shapes, n_sems, copies
        self.in_place = in_place

    @property
    def n(self):
        return len(self.ins)

    def aliases(self, n_in, n_out):
        return {n_in + k: n_out + k for k in range(self.n)} if self.in_place else {}

    def sems(self):
        return [pltpu.SemaphoreType.DMA((self.n_sems,)), pltpu.SemaphoreType.DMA((self.n_sems,))]

    def start(self, src, dst, ssem, rsem):
        for send, _ in self.copies(src, dst, ssem, rsem):
            send.start()

    def wait(self, src, dst, ssem, rsem):
        for send, recv in self.copies(src, dst, ssem, rsem):
            recv.wait_recv()
            send.wait_send()


def _host(hosted, refs, n_in, n_out, first, last):
    if hosted is None:
        return refs, lambda: None, lambda: None
    n = hosted.n
    own_in, h_in = refs[:n_in], refs[n_in:n_in + n]
    own_out, h_out = refs[n_in + n:n_in + n + n_out], refs[n_in + n + n_out:n_in + 2 * n + n_out]
    rest = refs[n_in + 2 * n + n_out:]
    ssem, rsem = rest[-2:]

    def start():
        if first is True:
            hosted.start(h_in, h_out, ssem, rsem)
        else:
            pl.when(first)(lambda: hosted.start(h_in, h_out, ssem, rsem))

    def wait():
        if last is True:
            hosted.wait(h_in, h_out, ssem, rsem)
        else:
            pl.when(last)(lambda: hosted.wait(h_in, h_out, ssem, rsem))

    return own_in + own_out + rest[:-2], start, wait


def _attn_fwd(qkv, hosted=None):
    T = qkv.shape[0]
    tk = _tile(T, ATTN_TILE)
    n_sub = ATTN_SUBS if T % (ATTN_SUBS * tk) == 0 else 1
    tq = n_sub * tk
    n_p, nq = SB_WIDTH // LANES, T // tq

    def body(*refs):
        p, i = pl.program_id(0), pl.program_id(1)
        (q_ref, k_ref, v_ref, o_ref), start, wait = _host(
            hosted, refs, 3, 1, jnp.logical_and(p == 0, i == 0), jnp.logical_and(p == n_p - 1, i == nq - 1))
        start()
        lane = lax.broadcasted_iota(jnp.int32, (tk, LANES), 1)
        row = lax.broadcasted_iota(jnp.int32, (tk, tk), 0)
        col = lax.broadcasted_iota(jnp.int32, (tk, tk), 1)
        after = jnp.where(row > col, 1.0, 0.0).astype(BF16)
        valid = col < row
        qh = {}
        for sb in range(n_sub):
            q = q_ref[sb * tk:(sb + 1) * tk, :].astype(F32)
            for hh in range(2):
                qh[(sb, hh)] = jnp.where((lane // SB_HD) == hh, q * SB_SCALE, 0.0).astype(BF16)

        def tiles(todo, state):
            chains = [(n, hh) for n in range(len(todo)) for hh in range(2)]
            kv = []
            for _, j, _ in todo:
                ks = pl.ds(pl.multiple_of(j * tk, tk), tk)
                kv.append((k_ref[ks, :], v_ref[ks, :]))
            z = {(n, hh): _nt(qh[(todo[n][0], hh)], kv[n][0]) for n, hh in chains}
            lb, lmb, lm_sum = {}, {}, {}
            for n, hh in chains:
                def logits(rows, z=z[(n, hh)], mask=todo[n][2]):
                    lb, lm = _sb_logits(z[rows])
                    if mask is not None:
                        lm = jnp.where(mask[rows], lm, 0.0)
                    return lb, lm.astype(BF16), jnp.sum(lm, axis=1, keepdims=True)

                lb[(n, hh)], lmb[(n, hh)], lm_sum[(n, hh)] = _by_strips(tk, logits)
            x = {c: _nn(lmb[c], after) for c in chains}
            new = dict(state)
            for n, hh in chains:
                key = (todo[n][0], hh)
                carry, acc = new[key]

                def weights(rows, lb=lb[(n, hh)], x=x[(n, hh)], carry=carry, mask=todo[n][2]):
                    a = jnp.exp(lb[rows] + x[rows] + carry[rows])
                    if mask is not None:
                        a = jnp.where(mask[rows], a, 0.0)
                    return (a.astype(BF16),)

                (ab,) = _by_strips(tk, weights)
                new[key] = (carry + lm_sum[(n, hh)], acc + _nn(ab, kv[n][1]))
            return new

        def live(state, sb):
            return jnp.maximum(jnp.max(state[(sb, 0)][0]), jnp.max(state[(sb, 1)][0]))

        first = n_sub * i
        zero = (jnp.zeros((tk, 1), F32), jnp.zeros((tk, LANES), F32))
        todo = []
        for sb in range(n_sub):
            gate = jnp.broadcast_to(first > 0, (tk, tk)) if sb == 0 else None
            todo += [(sb, first + sb, valid), (sb, jnp.maximum(first + sb - 1, 0), gate)]
        state = tiles(todo, {(sb, hh): zero for sb in range(n_sub) for hh in range(2)})
        for sb in range(n_sub):
            def cond(st):
                return jnp.logical_and(st[0] >= 0, st[2] > UNDERFLOW)

            def step(st, sb=sb):
                mine = tiles([(sb, st[0], None)], st[1])
                return st[0] - 1, mine, live(mine, sb)

            mine = {k: v for k, v in state.items() if k[0] == sb}
            _, mine, _ = lax.while_loop(cond, step, (first + sb - 2, mine, live(mine, sb)))
            o_ref[sb * tk:(sb + 1) * tk, :] = jnp.where(lane < SB_HD, mine[(sb, 0)][1], mine[(sb, 1)][1])
        wait()

    h_ins = hosted.ins if hosted else []
    res = pl.pallas_call(
        body, name="attn_fwd_hosting" if hosted else "attn_fwd", grid=(n_p, nq),
        in_specs=_attn_qkv_specs(tq, T) + [HBM_SPEC] * len(h_ins),
        out_specs=[pl.BlockSpec((tq, LANES), lambda p, i: (i, p))] + [HBM_SPEC] * len(h_ins),
        out_shape=[jax.ShapeDtypeStruct((T, SB_WIDTH), F32)] + (hosted.out_shapes if hosted else []),
        input_output_aliases=hosted.aliases(3, 1) if hosted else {},
        scratch_shapes=hosted.sems() if hosted else [],
        compiler_params=_params(has_side_effects=hosted is not None),
    )(qkv, qkv, qkv, *h_ins)
    return res[0], res[1:]


def _attn_bwd(qkv, o, dymix, hosted=None):
    T = qkv.shape[0]
    tk = _tile(T, ATTN_TILE)
    n_sub = ATTN_SUBS if T % (ATTN_SUBS * tk) == 0 else 1
    tq = n_sub * tk
    n_p, nq = SB_WIDTH // LANES, T // tq
    yc_blk = (POOL_WIDTH + SG_WIDTH) // LANES

    def body(*refs):
        p, i = pl.program_id(0), pl.program_id(1)
        (q_ref, k_ref, v_ref, o_ref, do_ref, dq_ref, dk_ref, dv_ref), start, wait = _host(
            hosted, refs, 5, 3, jnp.logical_and(p == 0, i == 0), jnp.logical_and(p == n_p - 1, i == nq - 1))
        start()
        lane = lax.broadcasted_iota(jnp.int32, (tk, LANES), 1)
        row = lax.broadcasted_iota(jnp.int32, (tk, tk), 0)
        col = lax.broadcasted_iota(jnp.int32, (tk, tk), 1)
        after = jnp.where(row > col, 1.0, 0.0).astype(BF16)
        from_here = jnp.where(row >= col, 1.0, 0.0).astype(BF16)
        from_here2 = jnp.concatenate([from_here, from_here], axis=0)
        valid = col < row

        @pl.when(i == 0)
        def _():
            dk_ref[...] = jnp.zeros_like(dk_ref)
            dv_ref[...] = jnp.zeros_like(dv_ref)

        qh, dohb, delta = {}, {}, {}
        for sb in range(n_sub):
            rows = slice(sb * tk, (sb + 1) * tk)
            q, ov, dov = q_ref[rows, :].astype(F32), o_ref[rows, :], do_ref[rows, :]
            for hh in range(2):
                head = (lane // SB_HD) == hh
                qh[(sb, hh)] = jnp.where(head, q * SB_SCALE, 0.0).astype(BF16)
                dohb[(sb, hh)] = jnp.where(head, dov, 0.0).astype(BF16)
                delta[(sb, hh)] = jnp.sum(dohb[(sb, hh)].astype(F32) * ov, axis=1, keepdims=True)

        def tiles(todo, state):
            chains = [(n, hh) for n in range(len(todo)) for hh in range(2)]
            kv, where = [], []
            for _, j, _ in todo:
                ks = pl.ds(pl.multiple_of(j * tk, tk), tk)
                where.append(ks)
                kv.append((k_ref[ks, :], v_ref[ks, :]))
            z = {(n, hh): _nt(qh[(todo[n][0], hh)], kv[n][0]) for n, hh in chains}
            da = {(n, hh): _nt(dohb[(todo[n][0], hh)], kv[n][1]) for n, hh in chains}
            lb, lmb, lm_sum = {}, {}, {}
            for n, hh in chains:
                def logits(rows, z=z[(n, hh)], mask=todo[n][2]):
                    lb, lm = _sb_logits(z[rows])
                    if mask is not None:
                        lm = jnp.where(mask[rows], lm, 0.0)
                    return lb, lm.astype(BF16), jnp.sum(lm, axis=1, keepdims=True)

                lb[(n, hh)], lmb[(n, hh)], lm_sum[(n, hh)] = _by_strips(tk, logits)
            x = {c: _nn(lmb[c], after) for c in chains}
            c_a = {k: v[0] for k, v in state.items()}
            ab, g, g_split, g_sum = {}, {}, {}, {}
            for n, hh in chains:
                key = (todo[n][0], hh)

                def weights(rows, lb=lb[(n, hh)], x=x[(n, hh)], da=da[(n, hh)], c_a=c_a[key], mask=todo[n][2]):
                    a = jnp.exp(lb[rows] + x[rows] + c_a[rows])
                    if mask is not None:
                        a = jnp.where(mask[rows], a, 0.0)
                    ab = a.astype(BF16)
                    g = da[rows] * ab.astype(F32)
                    hi = g.astype(BF16)
                    lo = (g - hi.astype(F32)).astype(BF16)
                    return ab, g, jnp.concatenate([hi, lo], axis=1), jnp.sum(g, axis=1, keepdims=True)

                ab[(n, hh)], g[(n, hh)], g_split[(n, hh)], g_sum[(n, hh)] = _by_strips(tk, weights)
                c_a[key] = c_a[key] + lm_sum[(n, hh)]
            right = {c: _nn(g_split[c], from_here2) for c in chains}
            c_r = {k: v[1] for k, v in state.items()}
            dzb = {}
            for n, hh in chains:
                key = (todo[n][0], hh)

                def logit_grads(rows, lb=lb[(n, hh)], g=g[(n, hh)], right=right[(n, hh)], c_r=c_r[key],
                                delta=delta[key], mask=todo[n][2]):
                    sig = jnp.exp(lb[rows])
                    left = delta[rows] - (c_r[rows] + right[rows])
                    dz = g[rows] * (1.0 - sig) - left * sig
                    if mask is not None:
                        dz = jnp.where(mask[rows], dz, 0.0)
                    return (dz.astype(BF16),)

                (dzb[(n, hh)],) = _by_strips(tk, logit_grads)
                c_r[key] = c_r[key] + g_sum[(n, hh)]
            dqa = {k: v[2] for k, v in state.items()}
            for n in range(len(todo)):
                sb = todo[n][0]
                dk_ref[where[n], :] += _tn(dzb[(n, 0)], qh[(sb, 0)]) + _tn(dzb[(n, 1)], qh[(sb, 1)])
                dv_ref[where[n], :] += _tn(ab[(n, 0)], dohb[(sb, 0)]) + _tn(ab[(n, 1)], dohb[(sb, 1)])
                for hh in range(2):
                    dqa[(sb, hh)] = dqa[(sb, hh)] + _nn(dzb[(n, hh)], kv[n][0])
            return {k: (c_a[k], c_r[k], dqa[k]) for k in state}

        def live(state, sb):
            return jnp.maximum(jnp.max(state[(sb, 0)][0]), jnp.max(state[(sb, 1)][0]))

        first = n_sub * i
        zero = (jnp.zeros((tk, 1), F32), jnp.zeros((tk, 1), F32), jnp.zeros((tk, LANES), F32))
        todo = []
        for sb in range(n_sub):
            gate = jnp.broadcast_to(first > 0, (tk, tk)) if sb == 0 else None
            todo += [(sb, first + sb, valid), (sb, jnp.maximum(first + sb - 1, 0), gate)]
        state = tiles(todo, {(sb, hh): zero for sb in range(n_sub) for hh in range(2)})
        for sb in range(n_sub):
            def cond(st):
                return jnp.logical_and(st[0] >= 0, st[2] > UNDERFLOW)

            def step(st, sb=sb):
                mine = tiles([(sb, st[0], None)], st[1])
                return st[0] - 1, mine, live(mine, sb)

            mine = {k: v for k, v in state.items() if k[0] == sb}
            _, mine, _ = lax.while_loop(cond, step, (first + sb - 2, mine, live(mine, sb)))
            dq_ref[sb * tk:(sb + 1) * tk, :] = (
                jnp.where(lane < SB_HD, mine[(sb, 0)][2], mine[(sb, 1)][2]) * SB_SCALE).astype(BF16)
        wait()

    h_ins = hosted.ins if hosted else []
    res = pl.pallas_call(
        body, name="attn_bwd_hosting" if hosted else "attn_bwd", grid=(n_p, nq),
        in_specs=_attn_qkv_specs(tq, T) + [pl.BlockSpec((tq, LANES), lambda p, i: (i, p)),
                                           pl.BlockSpec((tq, LANES), lambda p, i: (i, yc_blk + p))]
        + [HBM_SPEC] * len(h_ins),
        out_specs=[pl.BlockSpec((tq, LANES), lambda p, i: (i, p)), pl.BlockSpec((T, LANES), lambda p, i: (0, p)),
                   pl.BlockSpec((T, LANES), lambda p, i: (0, p))] + [HBM_SPEC] * len(h_ins),
        out_shape=[jax.ShapeDtypeStruct((T, SB_WIDTH), BF16)] + [jax.ShapeDtypeStruct((T, SB_WIDTH), F32)] * 2
        + (hosted.out_shapes if hosted else []),
        scratch_shapes=hosted.sems() if hosted else [],
        compiler_params=_params(has_side_effects=hosted is not None),
    )(qkv, qkv, qkv, o, dymix, *h_ins)
    return res[:3], res[3:]


def _outproj_fwd(x, ya, yb, yc, w, hosted=None):
    T, D = x.shape
    tt = _tile(T, 512)
    nt = T // tt

    def body(*refs):
        i = pl.program_id(0)
        (x_ref, ya_ref, yb_ref, yc_ref, w_ref, x1_ref, ymix_ref), start, wait = _host(
            hosted, refs, 5, 2, i == 0, i == nt - 1)
        start()
        ymix_ref[:, 0:POOL_WIDTH] = ya_ref[...].astype(BF16)
        ymix_ref[:, POOL_WIDTH:POOL_WIDTH + SG_WIDTH] = yb_ref[...].astype(BF16)
        ymix_ref[:, POOL_WIDTH + SG_WIDTH:] = yc_ref[...].astype(BF16)
        x1_ref[...] = x_ref[...] + _nn(ymix_ref[...], w_ref[...])
        wait()

    row = lambda width: pl.BlockSpec((tt, width), lambda i: (i, 0))
    h_ins = hosted.ins if hosted else []
    res = pl.pallas_call(
        body, name="outproj_fwd_hosting" if hosted else "outproj_fwd", grid=(nt,),
        in_specs=[row(D), row(POOL_WIDTH), row(SG_WIDTH), row(SB_WIDTH), pl.BlockSpec((D, D), lambda i: (0, 0))]
        + [HBM_SPEC] * len(h_ins),
        out_specs=[row(D), row(D)] + [HBM_SPEC] * len(h_ins),
        out_shape=[jax.ShapeDtypeStruct((T, D), F32), jax.ShapeDtypeStruct((T, D), BF16)]
        + (hosted.out_shapes if hosted else []),
        input_output_aliases=hosted.aliases(5, 2) if hosted else {},
        scratch_shapes=hosted.sems() if hosted else [],
        compiler_params=_params(has_side_effects=hosted is not None),
    )(x, ya, yb, yc, w, *h_ins)
    return res[:2], res[2:]


def _nt_matmul(a, w):
    T, N = a.shape
    K = w.shape[0]
    tt = _tile(T, 512)

    def body(a_ref, w_ref, o_ref):
        o_ref[...] = _nt(a_ref[...].astype(BF16), w_ref[...])

    return pl.pallas_call(
        body, name="nt_matmul", grid=(T // tt,),
        in_specs=[pl.BlockSpec((tt, N), lambda i: (i, 0)), pl.BlockSpec((K, N), lambda i: (0, 0))],
        out_specs=pl.BlockSpec((tt, K), lambda i: (i, 0)),
        out_shape=jax.ShapeDtypeStruct((T, K), F32),
        compiler_params=_params(),
    )(a, w)


def _tn_matmul(a, b, name, n_split=1, hosted=None):
    T, K = a.shape
    N = b.shape[1]
    tk = _tile(K, 1024)
    tn = _tile(N // n_split, 1024)
    tt = _tile(T, 2048)
    nper = N // n_split // tn
    nk, nn, nt = K // tk, N // tn, T // tt

    def body(*refs):
        k, n, t = pl.program_id(0), pl.program_id(1), pl.program_id(2)
        (a_ref, b_ref, o_ref), start, wait = _host(
            hosted, refs, 2, 1, jnp.logical_and(jnp.logical_and(k == 0, n == 0), t == 0),
            jnp.logical_and(jnp.logical_and(k == nk - 1, n == nn - 1), t == nt - 1))
        start()

        @pl.when(t == 0)
        def _():
            o_ref[...] = jnp.zeros_like(o_ref)

        o_ref[...] += _tn(a_ref[...], b_ref[...].astype(BF16))
        wait()

    h_ins = hosted.ins if hosted else []
    res = pl.pallas_call(
        body, name=name + "_hosting" if hosted else name, grid=(nk, nn, nt),
        in_specs=[pl.BlockSpec((tt, tk), lambda k, n, t: (t, k)), pl.BlockSpec((tt, tn), lambda k, n, t: (t, n))]
        + [HBM_SPEC] * len(h_ins),
        out_specs=[pl.BlockSpec((None, tk, tn), lambda k, n, t: (n // nper, k, n % nper))] + [HBM_SPEC] * len(h_ins),
        out_shape=[jax.ShapeDtypeStruct((n_split, K, N // n_split), F32)] + (hosted.out_shapes if hosted else []),
        scratch_shapes=hosted.sems() if hosted else [],
        compiler_params=_params(has_side_effects=hosted is not None),
    )(a, b, *h_ins)
    return (res[0], res[1:]) if hosted else res[0]


def _mlp_fwd(x, g, w_up, w_down, hosted=None):
    T, D = x.shape
    F = w_up.shape[1]
    tt = _tile(T, 1024)
    fc = _tile(F, MLP_CHUNK)
    nc = F // fc
    nt = T // tt

    def body(*refs):
        i, c = pl.program_id(0), pl.program_id(1)
        (x_ref, g_ref, wu_ref, wd_ref, y_ref, h_ref, u_ref, a_ref), start, wait = _host(
            hosted, refs, 4, 4, jnp.logical_and(i == 0, c == 0), jnp.logical_and(i == nt - 1, c == nc - 1))
        start()

        @pl.when(c == 0)
        def _():
            xv = x_ref[...]
            h, _, _ = _rms_fwd(xv, g_ref[...])
            h_ref[...] = h.astype(BF16)
            y_ref[...] = xv

        u = _nn(h_ref[...], wu_ref[...])
        u_ref[...] = u.astype(BF16)
        a = jnp.square(jnp.maximum(u, 0.0)).astype(BF16)
        a_ref[...] = a
        y_ref[...] += _nn(a, wd_ref[...])
        wait()

    h_ins = hosted.ins if hosted else []
    res = pl.pallas_call(
        body, name="mlp_fwd_hosting" if hosted else "mlp_fwd", grid=(nt, nc),
        in_specs=[pl.BlockSpec((tt, D), lambda i, c: (i, 0)), pl.BlockSpec((1, D), lambda i, c: (0, 0)),
                  pl.BlockSpec((D, fc), lambda i, c: (0, c)), pl.BlockSpec((fc, D), lambda i, c: (c, 0))]
        + [HBM_SPEC] * len(h_ins),
        out_specs=[pl.BlockSpec((tt, D), lambda i, c: (i, 0)), pl.BlockSpec((tt, D), lambda i, c: (i, 0)),
                   pl.BlockSpec((tt, fc), lambda i, c: (i, c)), pl.BlockSpec((tt, fc), lambda i, c: (i, c))]
        + [HBM_SPEC] * len(h_ins),
        out_shape=[jax.ShapeDtypeStruct((T, D), F32), jax.ShapeDtypeStruct((T, D), BF16),
                   jax.ShapeDtypeStruct((T, F), BF16), jax.ShapeDtypeStruct((T, F), BF16)]
        + (hosted.out_shapes if hosted else []),
        scratch_shapes=hosted.sems() if hosted else [],
        compiler_params=_params(has_side_effects=hosted is not None),
    )(x, g, w_up, w_down, *h_ins)
    return res[:4], res[4:]


def _mlp_bwd(dy, x, g, u, w_up, w_down, hosted=None):
    T, D = x.shape
    F = w_up.shape[1]
    tt = _tile(T, 1024)
    fc = _tile(F, MLP_CHUNK)
    nc = F // fc
    nt = T // tt

    def body(*refs):
        i, c = pl.program_id(0), pl.program_id(1)
        (dy_ref, x_ref, g_ref, u_ref, wu_ref, wd_ref, dx_ref, du_ref, dg_ref, dyb_ref, dh_ref), start, wait = _host(
            hosted, refs, 6, 3, jnp.logical_and(i == 0, c == 0), jnp.logical_and(i == nt - 1, c == nc - 1))
        start()

        @pl.when(c == 0)
        def _():
            dyb_ref[...] = dy_ref[...].astype(BF16)
            dh_ref[...] = jnp.zeros_like(dh_ref)

        @pl.when(jnp.logical_and(i == 0, c == 0))
        def _():
            dg_ref[...] = jnp.zeros_like(dg_ref)

        da = _nt(dyb_ref[...], wd_ref[...])
        du = (da * (2.0 * jnp.maximum(u_ref[...].astype(F32), 0.0))).astype(BF16)
        du_ref[...] = du
        dh_ref[...] += _nt(du, wu_ref[...])

        @pl.when(c == nc - 1)
        def _():
            gv = g_ref[...]
            _, xhat, r = _rms_fwd(x_ref[...], gv)
            dx, dgrow = _rms_bwd(dh_ref[...], xhat, r, gv)
            dx_ref[...] = dy_ref[...] + dx
            dg_ref[...] += jnp.sum(dgrow, axis=0, keepdims=True)

        wait()

    h_ins = hosted.ins if hosted else []
    res = pl.pallas_call(
        body, name="mlp_bwd_hosting" if hosted else "mlp_bwd", grid=(nt, nc),
        in_specs=[pl.BlockSpec((tt, D), lambda i, c: (i, 0)), pl.BlockSpec((tt, D), lambda i, c: (i, 0)),
                  pl.BlockSpec((1, D), lambda i, c: (0, 0)), pl.BlockSpec((tt, fc), lambda i, c: (i, c)),
                  pl.BlockSpec((D, fc), lambda i, c: (0, c)), pl.BlockSpec((fc, D), lambda i, c: (c, 0))]
        + [HBM_SPEC] * len(h_ins),
        out_specs=[pl.BlockSpec((tt, D), lambda i, c: (i, 0)), pl.BlockSpec((tt, fc), lambda i, c: (i, c)),
                   pl.BlockSpec((1, D), lambda i, c: (0, 0))] + [HBM_SPEC] * len(h_ins),
        out_shape=[jax.ShapeDtypeStruct((T, D), F32), jax.ShapeDtypeStruct((T, F), BF16),
                   jax.ShapeDtypeStruct((1, D), F32)] + (hosted.out_shapes if hosted else []),
        scratch_shapes=[pltpu.VMEM((tt, D), BF16), pltpu.VMEM((tt, D), F32)] + (hosted.sems() if hosted else []),
        compiler_params=_params(has_side_effects=hosted is not None),
    )(dy, x, g, u, w_up, w_down, *h_ins)
    return res[:3], res[3:]


def _loss_head(x, g, target):
    T, D = x.shape
    tt = _tile(T, 512)

    def body(x_ref, g_ref, t_ref, loss_ref, dx_ref, dg_ref):
        gv = g_ref[...]
        y, xhat, r = _rms_fwd(x_ref[...], gv)
        err = y - t_ref[...]
        dx, dgrow = _rms_bwd(err * (1.0 / D), xhat, r, gv)
        dx_ref[...] = dx

        @pl.when(pl.program_id(0) == 0)
        def _():
            loss_ref[...] = jnp.zeros_like(loss_ref)
            dg_ref[...] = jnp.zeros_like(dg_ref)

        loss_ref[...] += 0.5 * jnp.sum(jnp.mean(err * err, axis=-1, keepdims=True), axis=0, keepdims=True)
        dg_ref[...] += jnp.sum(dgrow, axis=0, keepdims=True)

    return pl.pallas_call(
        body, name="loss_head", grid=(T // tt,),
        in_specs=[pl.BlockSpec((tt, D), lambda i: (i, 0)), pl.BlockSpec((1, D), lambda i: (0, 0)),
                  pl.BlockSpec((tt, D), lambda i: (i, 0))],
        out_specs=[pl.BlockSpec((1, LANES), lambda i: (0, 0)), pl.BlockSpec((tt, D), lambda i: (i, 0)),
                   pl.BlockSpec((1, D), lambda i: (0, 0))],
        out_shape=[jax.ShapeDtypeStruct((1, LANES), F32), jax.ShapeDtypeStruct((T, D), F32),
                   jax.ShapeDtypeStruct((1, D), F32)],
        compiler_params=_params(),
    )(x, g, target)


def _rows(shape, pref=512):
    last = shape[-1]
    rows = 1
    for s in shape[:-1]:
        rows *= s
    tr = rows
    if rows * last > 256 * 1024:
        for cand in (pref, 256, 128, 64, 32, 16, 8):
            if rows % cand == 0:
                tr = cand
                break
    return rows, last, tr


def _elementwise(fn, name, ins, n_out, out_dtype=F32):
    shape = ins[0].shape
    rows, last, tr = _rows(shape)
    flat = [a.reshape(rows, last) for a in ins]
    n_in = len(ins)

    def body(*refs):
        res = fn(*[r[...] for r in refs[:n_in]])
        if n_out == 1:
            res = (res,)
        for r, v in zip(refs[n_in:], res):
            r[...] = v.astype(r.dtype)

    spec = pl.BlockSpec((tr, last), lambda i: (i, 0))
    outs = pl.pallas_call(
        body, name=name, grid=(rows // tr,),
        in_specs=[spec] * n_in, out_specs=[spec] * n_out,
        out_shape=[jax.ShapeDtypeStruct((rows, last), out_dtype)] * n_out,
        compiler_params=_params(),
    )(*flat)
    return [o.reshape(shape) for o in outs]


def _add_pair(g, o, c_idx):
    nq, R, C = g.shape
    h = R // 2
    tr = _tile(h, 512)
    nb = h // tr

    def body(c_ref, g_ref, o_ref, out_ref):
        out_ref[...] = g_ref[...] + o_ref[...]

    return pl.pallas_call(
        body, name="add_pair",
        grid_spec=pltpu.PrefetchScalarGridSpec(
            num_scalar_prefetch=1, grid=(nq, nb),
            in_specs=[pl.BlockSpec((None, tr, C), lambda q, i, c: (q, c[0] * nb + i, 0)),
                      pl.BlockSpec((None, tr, C), lambda q, i, c: (q, i, 0))],
            out_specs=pl.BlockSpec((None, tr, C), lambda q, i, c: (q, i, 0))),
        out_shape=jax.ShapeDtypeStruct((nq, h, C), F32),
        compiler_params=_params(),
    )(c_idx.astype(jnp.int32).reshape(1), g, o)


def _add_chips(p, r, q_idx):
    _, H, C = p.shape
    tr = _tile(H, 512)

    def body(q_ref, p_ref, r0_ref, r1_ref, r2_ref, out_ref):
        out_ref[...] = (p_ref[...] + r0_ref[...]) + (r1_ref[...] + r2_ref[...])

    def arrived(k):
        return pl.BlockSpec((None, tr, C), lambda i, q: (k, i, 0))

    return pl.pallas_call(
        body, name="add_chips",
        grid_spec=pltpu.PrefetchScalarGridSpec(
            num_scalar_prefetch=1, grid=(H // tr,),
            in_specs=[pl.BlockSpec((None, tr, C), lambda i, q: (q[0], i, 0)), arrived(0), arrived(1), arrived(2)],
            out_specs=pl.BlockSpec((tr, C), lambda i, q: (i, 0))),
        out_shape=jax.ShapeDtypeStruct((H, C), F32),
        compiler_params=_params(),
    )(q_idx.astype(jnp.int32).reshape(1), p, r, r, r)


def _adamw(w, g, m, v):
    m = ADAM_B1 * m + (1.0 - ADAM_B1) * g
    v = ADAM_B2 * v + (1.0 - ADAM_B2) * jnp.square(g)
    m_hat = m / (1.0 - ADAM_B1 ** ADAM_STEP)
    v_hat = v / (1.0 - ADAM_B2 ** ADAM_STEP)
    delta = -ADAM_LR * (m_hat / (jnp.sqrt(v_hat) + ADAM_EPS) + ADAM_WD * w)
    return delta, m, v


def _place():
    x, y, c = lax.axis_index("x"), lax.axis_index("y"), lax.axis_index("c")
    chips = [(1 - x, y), (x, 1 - y), (1 - x, 1 - y)]
    return x, y, c, chips


def _remote(src, dst, ssem, rsem, k, dev):
    return pltpu.make_async_remote_copy(src_ref=src, dst_ref=dst, send_sem=ssem.at[k], recv_sem=rsem.at[k],
                                        device_id=dev, device_id_type=MESH)


def _gather_weights(shards):
    n = len(shards)
    halves = [s.shape[1] // 2 for s in shards]

    def body(*refs):
        src, out = refs[:n], refs[n:2 * n]
        ssem, rsem = refs[2 * n:]
        x, y, c, chips = _place()
        me_q = 2 * x + y
        sib = (x, y, 1 - c)

        def half(a, q, cc):
            return out[a].at[q, :, pl.ds(cc * halves[a], halves[a]), :]

        first = []
        for a in range(n):
            mine = src[a].at[:, pl.ds(c * halves[a], halves[a]), :]
            for r, chip in enumerate(chips):
                first.append(_remote(mine, half(a, me_q, c), ssem, rsem, a * 3 + r, (*chip, c)))
        for cp in first:
            cp.start()
        passed = []
        for a in range(n):
            for r, chip in enumerate(chips):
                q = 2 * chip[0] + chip[1]
                k = a * 3 + r
                _remote(half(a, q, c), half(a, q, c), ssem, rsem, k, (*chip, c)).wait_recv()
                cp = _remote(half(a, q, c), half(a, q, c), ssem, rsem, 3 * n + k, sib)
                cp.start()
                passed.append(cp)
        for a in range(n):
            for r, chip in enumerate(chips):
                q = 2 * chip[0] + chip[1]
                _remote(half(a, q, 1 - c), half(a, q, 1 - c), ssem, rsem, 3 * n + a * 3 + r, sib).wait_recv()
        for cp in first + passed:
            cp.wait_send()

    return pl.pallas_call(
        body, name="gather_weights",
        in_specs=[HBM_SPEC] * n, out_specs=[HBM_SPEC] * n,
        out_shape=[jax.ShapeDtypeStruct((N_CHIPS,) + s.shape, s.dtype) for s in shards],
        scratch_shapes=[pltpu.SemaphoreType.DMA((6 * n,)), pltpu.SemaphoreType.DMA((6 * n,))],
        compiler_params=_params(has_side_effects=True),
    )(*shards)


def _gather_over_ici(shards):
    n = len(shards)
    halves = [s.shape[1] // 2 for s in shards]

    def copies(src, out, ssem, rsem):
        x, y, c, chips = _place()
        me_q = 2 * x + y
        res = []
        for a in range(n):
            rows = pl.ds(c * halves[a], halves[a])
            mine = src[a].at[:, rows, :]
            for r, chip in enumerate(chips):
                dev = (*chip, c)
                res.append((_remote(mine, out[a].at[me_q, :, rows, :], ssem, rsem, a * 3 + r, dev),
                            _remote(mine, out[a].at[2 * chip[0] + chip[1], :, rows, :], ssem, rsem, a * 3 + r, dev)))
        return res

    return _Hosted(list(shards), [jax.ShapeDtypeStruct((N_CHIPS,) + s.shape, s.dtype) for s in shards], 3 * n, copies)


def _pass_over_d2d(gathered):
    n = len(gathered)
    halves = [g.shape[2] // 2 for g in gathered]

    def copies(_, out, ssem, rsem):
        x, y, c, chips = _place()
        sib = (x, y, 1 - c)
        res = []
        for a in range(n):
            for r, chip in enumerate(chips):
                q = 2 * chip[0] + chip[1]
                mine = out[a].at[q, :, pl.ds(c * halves[a], halves[a]), :]
                theirs = out[a].at[q, :, pl.ds((1 - c) * halves[a], halves[a]), :]
                res.append((_remote(mine, mine, ssem, rsem, a * 3 + r, sib),
                            _remote(theirs, theirs, ssem, rsem, a * 3 + r, sib)))
        return res

    return _Hosted(list(gathered), [jax.ShapeDtypeStruct(g.shape, g.dtype) for g in gathered], 3 * n, copies,
                   in_place=True)


def _pass_to_sibling(gathered):
    n = len(gathered)
    halves = [g.shape[2] // 2 for g in gathered]

    def body(*refs):
        out = refs[n:2 * n]
        ssem, rsem = refs[2 * n:]
        x, y, c, chips = _place()
        sib = (x, y, 1 - c)

        def half(a, q, cc):
            return out[a].at[q, :, pl.ds(cc * halves[a], halves[a]), :]

        cps = []
        for a in range(n):
            for r, chip in enumerate(chips):
                q = 2 * chip[0] + chip[1]
                cps.append(_remote(half(a, q, c), half(a, q, c), ssem, rsem, a * 3 + r, sib))
        for cp in cps:
            cp.start()
        for a in range(n):
            for r, chip in enumerate(chips):
                q = 2 * chip[0] + chip[1]
                _remote(half(a, q, 1 - c), half(a, q, 1 - c), ssem, rsem, a * 3 + r, sib).wait_recv()
        for cp in cps:
            cp.wait_send()

    return pl.pallas_call(
        body, name="pass_to_sibling",
        in_specs=[HBM_SPEC] * n, out_specs=[HBM_SPEC] * n,
        out_shape=[jax.ShapeDtypeStruct(g.shape, g.dtype) for g in gathered],
        input_output_aliases={a: a for a in range(n)},
        scratch_shapes=[pltpu.SemaphoreType.DMA((3 * n,)), pltpu.SemaphoreType.DMA((3 * n,))],
        compiler_params=_params(has_side_effects=True),
    )(*gathered)


def _scatter_over_ici(parts):
    n = len(parts)

    def copies(src, out, ssem, rsem):
        x, y, c, chips = _place()
        res = []
        for a in range(n):
            for r, chip in enumerate(chips):
                cp = _remote(src[a].at[2 * chip[0] + chip[1]], out[a].at[r], ssem, rsem, a * 3 + r, (*chip, c))
                res.append((cp, cp))
        return res

    return _Hosted(list(parts), [jax.ShapeDtypeStruct((3,) + p.shape[1:], F32) for p in parts], 3 * n, copies)


def _swap_over_d2d(grads):
    n = len(grads)
    halves = [g.shape[1] // 2 for g in grads]

    def copies(src, out, ssem, rsem):
        x, y, c, _ = _place()
        res = []
        for a in range(n):
            cp = _remote(src[a].at[:, pl.ds((1 - c) * halves[a], halves[a]), :], out[a], ssem, rsem, a, (x, y, 1 - c))
            res.append((cp, cp))
        return res

    return _Hosted(list(grads), [jax.ShapeDtypeStruct((N_CHIPS, h, g.shape[2]), F32) for g, h in zip(grads, halves)],
                   n, copies)


def _swap_halves(grads):
    n = len(grads)
    halves = [g.shape[1] // 2 for g in grads]

    def body(*refs):
        src, out = refs[:n], refs[n:2 * n]
        ssem, rsem = refs[2 * n:]
        x, y, c, _ = _place()
        cps = [_remote(src[a].at[:, pl.ds((1 - c) * halves[a], halves[a]), :], out[a], ssem, rsem, a, (x, y, 1 - c))
               for a in range(n)]
        for cp in cps:
            cp.start()
        for cp in cps:
            cp.wait()

    return pl.pallas_call(
        body, name="swap_halves",
        in_specs=[HBM_SPEC] * n, out_specs=[HBM_SPEC] * n,
        out_shape=[jax.ShapeDtypeStruct((N_CHIPS, h, g.shape[2]), F32) for g, h in zip(grads, halves)],
        scratch_shapes=[pltpu.SemaphoreType.DMA((n,)), pltpu.SemaphoreType.DMA((n,))],
        compiler_params=_params(has_side_effects=True),
    )(*grads)


def _scatter_chips(parts):
    n = len(parts)

    def body(*refs):
        src, out = refs[:n], refs[n:2 * n]
        ssem, rsem = refs[2 * n:]
        x, y, c, chips = _place()
        cps = []
        for a in range(n):
            for r, chip in enumerate(chips):
                cps.append(_remote(src[a].at[2 * chip[0] + chip[1]], out[a].at[r], ssem, rsem, a * 3 + r, (*chip, c)))
        for cp in cps:
            cp.start()
        for cp in cps:
            cp.wait()

    return pl.pallas_call(
        body, name="scatter_chips",
        in_specs=[HBM_SPEC] * n, out_specs=[HBM_SPEC] * n,
        out_shape=[jax.ShapeDtypeStruct((3,) + p.shape[1:], F32) for p in parts],
        scratch_shapes=[pltpu.SemaphoreType.DMA((3 * n,)), pltpu.SemaphoreType.DMA((3 * n,))],
        compiler_params=_params(has_side_effects=True),
    )(*parts)


def _swap_reduced_over_d2d(reduced):
    n = len(reduced)

    def copies(src, out, ssem, rsem):
        x, y, c, _ = _place()
        res = []
        for a in range(n):
            cp = _remote(src[a], out[a], ssem, rsem, a, (x, y, 1 - c))
            res.append((cp, cp))
        return res

    return _Hosted(list(reduced), [jax.ShapeDtypeStruct(r.shape, F32) for r in reduced], n, copies)


def _swap_reduced(reduced):
    n = len(reduced)

    def body(*refs):
        src, out = refs[:n], refs[n:2 * n]
        ssem, rsem = refs[2 * n:]
        x, y, c, _ = _place()
        cps = [_remote(src[a], out[a], ssem, rsem, a, (x, y, 1 - c)) for a in range(n)]
        for cp in cps:
            cp.start()
        for cp in cps:
            cp.wait()

    return pl.pallas_call(
        body, name="swap_reduced",
        in_specs=[HBM_SPEC] * n, out_specs=[HBM_SPEC] * n,
        out_shape=[jax.ShapeDtypeStruct(r.shape, F32) for r in reduced],
        scratch_shapes=[pltpu.SemaphoreType.DMA((n,)), pltpu.SemaphoreType.DMA((n,))],
        compiler_params=_params(has_side_effects=True),
    )(*reduced)


def _allreduce_small(buf, hosted=None):
    R, L = buf.shape

    def body(*refs):
        (buf_ref, out_ref, pair_ref, chip_ref, ssem, rsem), start, wait = _host(hosted, refs, 1, 1, True, True)
        start()
        x, y, c, chips = _place()
        me_q = 2 * x + y
        pair_ref[c] = buf_ref[...]
        to_sib = _remote(buf_ref, pair_ref.at[c], ssem, rsem, 0, (x, y, 1 - c))
        to_sib.start()
        _remote(buf_ref, pair_ref.at[1 - c], ssem, rsem, 0, (x, y, 1 - c)).wait_recv()
        chip_ref[me_q] = pair_ref[0] + pair_ref[1]
        cps = [_remote(chip_ref.at[me_q], chip_ref.at[me_q], ssem, rsem, 1 + r, (*chip, c))
               for r, chip in enumerate(chips)]
        for cp in cps:
            cp.start()
        for r, chip in enumerate(chips):
            q = 2 * chip[0] + chip[1]
            _remote(chip_ref.at[q], chip_ref.at[q], ssem, rsem, 1 + r, (*chip, c)).wait_recv()
        out_ref[...] = (chip_ref[0] + chip_ref[1]) + (chip_ref[2] + chip_ref[3])
        to_sib.wait_send()
        for cp in cps:
            cp.wait_send()
        wait()

    h_ins = hosted.ins if hosted else []
    res = pl.pallas_call(
        body, name="allreduce_small",
        in_specs=[VMEM_SPEC] + [HBM_SPEC] * len(h_ins), out_specs=[VMEM_SPEC] + [HBM_SPEC] * len(h_ins),
        out_shape=[jax.ShapeDtypeStruct((R, L), F32)] + (hosted.out_shapes if hosted else []),
        scratch_shapes=[pltpu.VMEM((2, R, L), F32), pltpu.VMEM((N_CHIPS, R, L), F32),
                        pltpu.SemaphoreType.DMA((4,)), pltpu.SemaphoreType.DMA((4,))]
        + (hosted.sems() if hosted else []),
        compiler_params=_params(has_side_effects=True),
    )(buf, *h_ins)
    return res[0], res[1:]


def _pack(arrays):
    flat = jnp.concatenate([a.reshape(-1) for a in arrays])
    pad = (-flat.shape[0]) % (8 * LANES)
    return jnp.pad(flat, (0, pad)).reshape(-1, LANES)


def _unpack(buf, like):
    flat = buf.reshape(-1)
    out, off = [], 0
    for a in like:
        out.append(flat[off:off + a.size].reshape(a.shape))
        off += a.size
    return out


def _block_diag(pw):
    rows = []
    for gi in range(len(POOL_WINDOWS)):
        blocks = [pw[gi] if gj == gi else jnp.zeros_like(pw[gi]) for gj in range(len(POOL_WINDOWS))]
        rows.append(jnp.concatenate(blocks, axis=1))
    return jnp.concatenate(rows, axis=0)


def kernel(x, norm1, w_in, pool_w, pool_scale, sg_norm, sg_w, sg_b, w_out, norm2, w_up, w_down, final_norm, loss_target, m_norm1, m_w_in, m_pool_w, m_pool_scale, m_sg_norm, m_sg_w, m_sg_b, m_w_out, m_norm2, m_w_up, m_w_down, m_final_norm, v_norm1, v_w_in, v_pool_w, v_pool_scale, v_sg_norm, v_sg_w, v_sg_b, v_w_out, v_norm2, v_w_up, v_w_down, v_final_norm):
    depth = norm1.shape[0]
    T = x.shape[1]
    xs = x.reshape(T, D_MODEL)
    target = loss_target.reshape(T, D_MODEL)

    assert depth == 2
    c_idx = lax.axis_index("c")
    q_idx = 2 * lax.axis_index("x") + lax.axis_index("y")
    own = [w.astype(BF16) for w in (w_in, w_out, w_up, w_down)]
    gathered = {(0, 0): _gather_weights([own[0][:1]])[0]}

    def full(a, l, axis):
        blocks = [jnp.where(q_idx == q, own[a][l], gathered[(a, l)][q, 0]) for q in range(N_CHIPS)]
        return jnp.concatenate(blocks, axis=axis)

    half_way = {}

    def gather_behind(call, keys, at_once):
        res, over_ici = call(_gather_over_ici([own[a][l:l + 1] for a, l in keys]))
        gathered.update(zip(keys[:at_once], _pass_to_sibling(over_ici[:at_once])))
        half_way.update(zip(keys[at_once:], over_ici[at_once:]))
        return res

    def pass_behind(call, keys):
        res, done = call(_pass_over_d2d([half_way.pop(k) for k in keys]))
        gathered.update(zip(keys, done))
        return res

    tril = jnp.tril(jnp.ones((CHUNK, CHUNK), F32))
    saved = []
    cur = xs
    wi, wo, wu, wd = {}, {}, {}, {}
    for l in range(depth):
        wbd = _block_diag(pool_w[l]).astype(BF16)
        wm = sg_w[l] * tril
        wm_s = wm.reshape(SG_HEADS * CHUNK, CHUNK).astype(BF16)
        wmt_s = jnp.swapaxes(wm, 1, 2).reshape(SG_HEADS * CHUNK, CHUNK).astype(BF16)
        bias = jnp.repeat(sg_b[l].T, SB_HD, axis=1)
        n1, n2 = norm1[l][None], norm2[l][None]
        psc, sgn = pool_scale[l][None], sg_norm[l][None]
        wi[l] = full(0, l, 1)
        proj, h, qkv = _inproj_fwd(cur, n1, wi[l])
        ya = _pool_fwd(proj, wbd, psc)
        yb = _sg_fwd(proj, wm_s, bias, sgn)
        if l == 0:
            yc = gather_behind(lambda hosted: _attn_fwd(qkv, hosted), [(1, 0), (2, 0), (3, 0)], 1)
            wo[l] = full(1, l, 0)
            x1, ymix = pass_behind(lambda hosted: _outproj_fwd(cur, ya, yb, yc, wo[l], hosted), [(2, 0), (3, 0)])
        else:
            yc = pass_behind(lambda hosted: _attn_fwd(qkv, hosted), [(1, l), (2, l), (3, l)])
            wo[l] = full(1, l, 0)
            (x1, ymix), _ = _outproj_fwd(cur, ya, yb, yc, wo[l])
        wu[l], wd[l] = full(2, l, 1), full(3, l, 0)
        if l == 0:
            x2, h2, u, act = gather_behind(lambda hosted: _mlp_fwd(x1, n2, wu[l], wd[l], hosted),
                                           [(0, 1), (1, 1), (2, 1), (3, 1)], 1)
        else:
            (x2, h2, u, act), _ = _mlp_fwd(x1, n2, wu[l], wd[l])
        saved.append(dict(x0=cur, x1=x1, proj=proj, h=h, qkv=qkv, yc=yc, ymix=ymix, h2=h2, u=u, act=act,
                          wbd=wbd, wm_s=wm_s, wmt_s=wmt_s, bias=bias, n1=n1, n2=n2, psc=psc, sgn=sgn))
        cur = x2

    loss_row, dcur, d_final = _loss_head(cur, final_norm[None], target)

    small = [None] * depth
    grads, parts, reduced = {}, {}, {}

    def pair_up(keys, swapped):
        parts.update({k: _add_pair(grads[k], o, c_idx) for k, o in zip(keys, swapped)})

    def chip_up(keys, arrived):
        reduced.update({k: _add_chips(parts[k], r, q_idx) for k, r in zip(keys, arrived)})

    for l in reversed(range(depth)):
        s = saved[l]
        if l == 0:
            keys = [(2, 1), (3, 1)]
            (dx1, du, d_n2), arrived = _mlp_bwd(dcur, s["x1"], s["n2"], s["u"], wu[l], wd[l],
                                                _scatter_over_ici([parts[k] for k in keys]))
            chip_up(keys, arrived)
        else:
            (dx1, du, d_n2), _ = _mlp_bwd(dcur, s["x1"], s["n2"], s["u"], wu[l], wd[l])
        grads[(2, l)] = _tn_matmul(s["h2"], du, "grad_w_up", n_split=N_CHIPS)
        grads[(3, l)] = _tn_matmul(s["act"], dcur, "grad_w_down")[0].reshape(N_CHIPS, D_FF // N_CHIPS, D_MODEL)
        dymix = _nt_matmul(dx1, wo[l])
        grads[(1, l)] = _tn_matmul(s["ymix"], dx1, "grad_w_out")[0].reshape(N_CHIPS, D_MODEL // N_CHIPS, D_MODEL)
        da_in, d_wbd, d_psc = _pool_bwd(s["proj"], dymix, s["wbd"], s["psc"])
        if l == 0:
            keys = [(0, 1), (1, 0), (2, 0), (3, 0)]
            (du_pre, dv_pre, d_wm, d_bias, d_sgn), swapped = _sg_bwd(
                s["proj"], dymix, s["wm_s"], s["wmt_s"], s["bias"], s["sgn"], _swap_over_d2d([grads[k] for k in keys]))
            pair_up(keys, swapped)
            keys = [(1, 1)] + keys
            (dq, dk, dv), arrived = _attn_bwd(s["qkv"], s["yc"], dymix, _scatter_over_ici([parts[k] for k in keys]))
            chip_up(keys, arrived)
        else:
            (du_pre, dv_pre, d_wm, d_bias, d_sgn), _ = _sg_bwd(s["proj"], dymix, s["wm_s"], s["wmt_s"], s["bias"], s["sgn"])
            keys = [(1, l), (2, l), (3, l)]
            (dq, dk, dv), swapped = _attn_bwd(s["qkv"], s["yc"], dymix, _swap_over_d2d([grads[k] for k in keys]))
            pair_up(keys, swapped)
        dproj, dx0, d_n1 = _inproj_bwd([da_in, du_pre, dv_pre, dq, dk, dv], wi[l], s["x0"], s["n1"], dx1)
        if l == 0:
            keys = sorted(reduced)
            g_in_l, swapped = _tn_matmul(s["h"], dproj, "grad_w_in",
                                         hosted=_swap_reduced_over_d2d([reduced[k] for k in keys]))
            theirs = dict(zip(keys, swapped))
        else:
            g_in_l = _tn_matmul(s["h"], dproj, "grad_w_in")
        grads[(0, l)] = g_in_l[0].reshape(D_MODEL, N_CHIPS, IN_COLS // N_CHIPS).transpose(1, 0, 2)
        d_pw = jnp.stack([d_wbd[gi * POOL_GW:(gi + 1) * POOL_GW, gi * POOL_GW:(gi + 1) * POOL_GW]
                          for gi in range(len(POOL_WINDOWS))])
        small[l] = dict(norm1=d_n1[0], pool_w=d_pw, pool_scale=d_psc[0], sg_norm=d_sgn[0],
                        sg_w=d_wm.reshape(SG_HEADS, CHUNK, CHUNK), sg_b=d_bias[:, :SG_HEADS].T, norm2=d_n2[0])
        dcur = dx0
    grad_x = dcur.reshape(x.shape)

    names = ["norm1", "pool_w", "pool_scale", "sg_norm", "sg_w", "sg_b", "norm2"]
    slot = jnp.zeros((1,), F32)
    small_w = [norm1, pool_w, pool_scale, sg_norm, sg_w, sg_b, norm2, final_norm, slot]
    small_m = [m_norm1, m_pool_w, m_pool_scale, m_sg_norm, m_sg_w, m_sg_b, m_norm2, m_final_norm, slot]
    small_v = [v_norm1, v_pool_w, v_pool_scale, v_sg_norm, v_sg_w, v_sg_b, v_norm2, v_final_norm, slot]
    small_g = [jnp.stack([small[l][k] for l in range(depth)]) for k in names] + [d_final[0], loss_row[0, :1]]
    keys = [(0, 0)]
    pair_up(keys, _swap_halves([grads[k] for k in keys]))
    g_packed, arrived = _allreduce_small(_pack(small_g), _scatter_over_ici([parts[k] for k in keys]))
    chip_up(keys, arrived)
    theirs.update(zip(keys, _swap_reduced([reduced[k] for k in keys])))

    def joined(a):
        layers = []
        for l in range(depth):
            mine, other = reduced[(a, l)], theirs[(a, l)]
            layers.append(jnp.where(c_idx == 0, jnp.concatenate([mine, other]), jnp.concatenate([other, mine])))
        return jnp.stack(layers)

    gw_in, gw_out, gw_up, gw_down = [joined(a) for a in range(4)]

    loss = _unpack(g_packed, small_w)[-1][0]
    s_delta, s_m, s_v = _elementwise(_adamw, "adamw_small", [_pack(small_w), g_packed, _pack(small_m), _pack(small_v)], 3)
    gs = dict(zip(names + ["final_norm"], _unpack(g_packed, small_w)))
    ds = dict(zip(names + ["final_norm"], _unpack(s_delta, small_w)))
    ms = dict(zip(names + ["final_norm"], _unpack(s_m, small_w)))
    vs = dict(zip(names + ["final_norm"], _unpack(s_v, small_w)))

    big_g = dict(w_in=gw_in, w_out=gw_out, w_up=gw_up, w_down=gw_down)
    big_w = dict(w_in=(w_in, m_w_in, v_w_in), w_out=(w_out, m_w_out, v_w_out),
                 w_up=(w_up, m_w_up, v_w_up), w_down=(w_down, m_w_down, v_w_down))
    for k, (w, m, v) in big_w.items():
        operands = [w, big_g[k], m, v]
        if k == "w_in":
            operands = [jnp.swapaxes(o, 1, 2) for o in operands]
        ds[k], ms[k], vs[k] = _elementwise(_adamw, "adamw_" + k, operands, 3)
        if k == "w_in":
            ds[k], ms[k], vs[k] = [jnp.swapaxes(o, 1, 2) for o in (ds[k], ms[k], vs[k])]
        gs[k] = big_g[k]

    order = ["norm1", "w_in", "pool_w", "pool_scale", "sg_norm", "sg_w", "sg_b", "w_out", "norm2", "w_up", "w_down",
             "final_norm"]
    return (loss, grad_x, *[gs[k] for k in order], *[ds[k] for k in order], *[ms[k] for k in order],
            *[vs[k] for k in order])
```

```python
import functools

import jax
import jax.numpy as jnp
from jax import lax
from jax.experimental import pallas as pl
from jax.experimental.pallas import tpu as pltpu

F32 = jnp.float32
BF16 = jnp.bfloat16
MESH = pl.DeviceIdType.MESH
AXES = ("x", "y", "c")

EPS = 1e-6
D_MODEL = 1024
POOL_WIDTH = 256
SG_WIDTH = 256
SB_WIDTH = 512
POOL_WINDOWS = (2, 4, 8, 16)
POOL_GW = 64
POOL_HALO = 16
CHUNK = 128
SG_HEADS = 4
SB_HD = 64
SB_SCALE = 0.125
IN_COLS = 2304
QKV_OFF = 768
D_FF = 4096
N_CHIPS = 4
LANES = 128
VMEM_LIMIT = 56 * 1024 * 1024
MLP_CHUNK = 512
ATTN_TILE = 256
UNDERFLOW = -104.0

ADAM_LR = 0.001
ADAM_B1 = 0.9
ADAM_B2 = 0.999
ADAM_EPS = 1e-08
ADAM_WD = 0.01
ADAM_STEP = 10

HBM_SPEC = pl.BlockSpec(memory_space=pl.ANY)
VMEM_SPEC = pl.BlockSpec(memory_space=pltpu.VMEM)


def _params(**kw):
    return pltpu.CompilerParams(vmem_limit_bytes=VMEM_LIMIT, **kw)


def _tile(n, pref):
    if n <= pref:
        return n
    for t in range(pref - pref % LANES, 0, -LANES):
        if n % t == 0:
            return t
    raise ValueError((n, pref))


def _nn(a, b):
    return jnp.dot(a, b, preferred_element_type=F32)


def _nt(a, b):
    return lax.dot_general(a, b, (((1,), (1,)), ((), ())), preferred_element_type=F32)


def _tn(a, b):
    return lax.dot_general(a, b, (((0,), (0,)), ((), ())), preferred_element_type=F32)


def _rms_fwd(x, g):
    r = lax.rsqrt(jnp.mean(x * x, axis=-1, keepdims=True) + EPS)
    xhat = x * r
    return xhat * g, xhat, r


def _rms_bwd(dy, xhat, r, g):
    dxhat = dy * g
    dx = r * (dxhat - xhat * jnp.mean(dxhat * xhat, axis=-1, keepdims=True))
    return dx, dy * xhat


_GELU_K = 0.7978845608028654
_GELU_C = 0.044715


def _gelu(x):
    return 0.5 * x * (1.0 + jnp.tanh(_GELU_K * (x + _GELU_C * x * x * x)))


def _gelu_grad(x):
    t = jnp.tanh(_GELU_K * (x + _GELU_C * x * x * x))
    return 0.5 * (1.0 + t) + 0.5 * x * (1.0 - t * t) * _GELU_K * (1.0 + 3.0 * _GELU_C * x * x)


def _inproj_fwd(x, g, w):
    T, D = x.shape
    N = w.shape[1]
    tt = _tile(T, 512)

    def body(x_ref, g_ref, w_ref, proj_ref, h_ref, qkv_ref):
        h, _, _ = _rms_fwd(x_ref[...], g_ref[...])
        hb = h.astype(BF16)
        h_ref[...] = hb
        p = _nn(hb, w_ref[...])
        proj_ref[...] = p[:, :QKV_OFF]
        qkv_ref[...] = p[:, QKV_OFF:].astype(BF16)

    return pl.pallas_call(
        body, name="inproj_fwd", grid=(T // tt,),
        in_specs=[pl.BlockSpec((tt, D), lambda i: (i, 0)), pl.BlockSpec((1, D), lambda i: (0, 0)),
                  pl.BlockSpec((D, N), lambda i: (0, 0))],
        out_specs=[pl.BlockSpec((tt, QKV_OFF), lambda i: (i, 0)), pl.BlockSpec((tt, D), lambda i: (i, 0)),
                   pl.BlockSpec((tt, N - QKV_OFF), lambda i: (i, 0))],
        out_shape=[jax.ShapeDtypeStruct((T, QKV_OFF), F32), jax.ShapeDtypeStruct((T, D), BF16),
                   jax.ShapeDtypeStruct((T, N - QKV_OFF), BF16)],
        compiler_params=_params(),
    )(x, g, w)


def _inproj_bwd(pieces, w, x, g, dres):
    T, D = x.shape
    N = w.shape[1]
    tt = _tile(T, 512)
    widths = [p.shape[1] for p in pieces]
    offs = [sum(widths[:k]) for k in range(len(widths))]
    assert sum(widths) == N
    n_p = len(pieces)

    def body(*refs):
        p_refs = refs[:n_p]
        w_ref, x_ref, g_ref, dres_ref, dproj_ref, dx_ref, dg_ref = refs[n_p:]
        for p_ref, o, wd in zip(p_refs, offs, widths):
            dproj_ref[:, o:o + wd] = p_ref[...].astype(BF16)
        dh = _nt(dproj_ref[...], w_ref[...])
        gv = g_ref[...]
        _, xhat, r = _rms_fwd(x_ref[...], gv)
        dx, dgrow = _rms_bwd(dh, xhat, r, gv)
        dx_ref[...] = dres_ref[...] + dx

        @pl.when(pl.program_id(0) == 0)
        def _():
            dg_ref[...] = jnp.zeros_like(dg_ref)

        dg_ref[...] += jnp.sum(dgrow, axis=0, keepdims=True)

    return pl.pallas_call(
        body, name="inproj_bwd", grid=(T // tt,),
        in_specs=[pl.BlockSpec((tt, wd), lambda i: (i, 0)) for wd in widths] + [
            pl.BlockSpec((D, N), lambda i: (0, 0)), pl.BlockSpec((tt, D), lambda i: (i, 0)),
            pl.BlockSpec((1, D), lambda i: (0, 0)), pl.BlockSpec((tt, D), lambda i: (i, 0))],
        out_specs=[pl.BlockSpec((tt, N), lambda i: (i, 0)), pl.BlockSpec((tt, D), lambda i: (i, 0)),
                   pl.BlockSpec((1, D), lambda i: (0, 0))],
        out_shape=[jax.ShapeDtypeStruct((T, N), BF16), jax.ShapeDtypeStruct((T, D), F32),
                   jax.ShapeDtypeStruct((1, D), F32)],
        compiler_params=_params(),
    )(*pieces, w, x, g, dres)


def _pool_select(s2, s4, s8, s16, grp):
    return jnp.where(grp == 0, s2, jnp.where(grp == 1, s4, jnp.where(grp == 2, s8, s16)))


def _pool_count(t_glob, grp):
    win = jnp.where(grp == 0, 2, jnp.where(grp == 1, 4, jnp.where(grp == 2, 8, 16)))
    return jnp.minimum(t_glob + 1, win).astype(F32)


def _pool_diff(a, halo, base, tt):
    n = tt + POOL_HALO
    ext = jnp.concatenate([halo, a], axis=0)
    s2 = ext + pltpu.roll(ext, 1, 0)
    s4 = s2 + pltpu.roll(s2, 2, 0)
    s8 = s4 + pltpu.roll(s4, 4, 0)
    s16 = s8 + pltpu.roll(s8, 8, 0)
    grp = lax.broadcasted_iota(jnp.int32, (n, POOL_WIDTH), 1) // POOL_GW
    t_glob = lax.broadcasted_iota(jnp.int32, (n, POOL_WIDTH), 0) + (base - POOL_HALO)
    pooled = _pool_select(s2, s4, s8, s16, grp) / _pool_count(t_glob, grp)
    return pooled[POOL_HALO:] - a


def _pool_specs(T, tt):
    hb = tt // POOL_HALO
    return [pl.BlockSpec((tt, POOL_WIDTH), lambda i: (i, 0)),
            pl.BlockSpec((POOL_HALO, POOL_WIDTH), lambda i: (jnp.maximum(i * hb - 1, 0), 0))]


def _pool_fwd(proj, wbd, scale):
    T = proj.shape[0]
    tt = _tile(T, 512)

    def body(a_ref, halo_ref, w_ref, sc_ref, y_ref):
        i = pl.program_id(0)
        halo = jnp.where(i > 0, halo_ref[...], 0.0)
        d = _pool_diff(a_ref[...], halo, i * tt, tt)
        y_ref[...] = _nn(d.astype(BF16), w_ref[...]) * sc_ref[...]

    return pl.pallas_call(
        body, name="pool_fwd", grid=(T // tt,),
        in_specs=_pool_specs(T, tt) + [pl.BlockSpec((POOL_WIDTH, POOL_WIDTH), lambda i: (0, 0)),
                                       pl.BlockSpec((1, POOL_WIDTH), lambda i: (0, 0))],
        out_specs=pl.BlockSpec((tt, POOL_WIDTH), lambda i: (i, 0)),
        out_shape=jax.ShapeDtypeStruct((T, POOL_WIDTH), F32),
        compiler_params=_params(),
    )(proj, proj, wbd, scale)


def _pool_bwd(proj, dymix, wbd, scale):
    T = proj.shape[0]
    tt = _tile(T, 512)
    hb = tt // POOL_HALO
    nblk = T // tt
    n = tt + POOL_HALO

    def body(a_ref, halo_ref, dy_ref, dyn_ref, w_ref, sc_ref, da_ref, dw_ref, dsc_ref):
        i = pl.program_id(0)
        halo = jnp.where(i > 0, halo_ref[...], 0.0)
        d = _pool_diff(a_ref[...], halo, i * tt, tt)
        db = d.astype(BF16)
        wv = w_ref[...]
        sc = sc_ref[...]
        dy = dy_ref[...]
        dys = dy * sc

        @pl.when(i == 0)
        def _():
            dw_ref[...] = jnp.zeros_like(dw_ref)
            dsc_ref[...] = jnp.zeros_like(dsc_ref)

        dsc_ref[...] += jnp.sum(dy * _nn(db, wv), axis=0, keepdims=True)
        dw_ref[...] += _tn(db, dys.astype(BF16))
        dyn = jnp.where(i < nblk - 1, dyn_ref[...], 0.0) * sc
        dd = _nt(jnp.concatenate([dys, dyn], axis=0).astype(BF16), wv)
        grp = lax.broadcasted_iota(jnp.int32, (n, POOL_WIDTH), 1) // POOL_GW
        t_glob = lax.broadcasted_iota(jnp.int32, (n, POOL_WIDTH), 0) + i * tt
        e = dd / _pool_count(t_glob, grp)
        r2 = e + pltpu.roll(e, n - 1, 0)
        r4 = r2 + pltpu.roll(r2, n - 2, 0)
        r8 = r4 + pltpu.roll(r4, n - 4, 0)
        r16 = r8 + pltpu.roll(r8, n - 8, 0)
        da_ref[...] = (_pool_select(r2, r4, r8, r16, grp) - dd)[:tt].astype(BF16)

    return pl.pallas_call(
        body, name="pool_bwd", grid=(nblk,),
        in_specs=_pool_specs(T, tt) + [
            pl.BlockSpec((tt, POOL_WIDTH), lambda i: (i, 0)),
            pl.BlockSpec((POOL_HALO, POOL_WIDTH), lambda i: (jnp.minimum((i + 1) * hb, T // POOL_HALO - 1), 0)),
            pl.BlockSpec((POOL_WIDTH, POOL_WIDTH), lambda i: (0, 0)), pl.BlockSpec((1, POOL_WIDTH), lambda i: (0, 0))],
        out_specs=[pl.BlockSpec((tt, POOL_WIDTH), lambda i: (i, 0)),
                   pl.BlockSpec((POOL_WIDTH, POOL_WIDTH), lambda i: (0, 0)),
                   pl.BlockSpec((1, POOL_WIDTH), lambda i: (0, 0))],
        out_shape=[jax.ShapeDtypeStruct((T, POOL_WIDTH), BF16),
                   jax.ShapeDtypeStruct((POOL_WIDTH, POOL_WIDTH), F32),
                   jax.ShapeDtypeStruct((1, POOL_WIDTH), F32)],
        compiler_params=_params(),
    )(proj, proj, dymix, dymix, wbd, scale)


def _head_select(stacked, grp):
    out = jnp.where(grp == 0, stacked[0:CHUNK], 0.0)
    for h in range(1, SG_HEADS):
        out = out + jnp.where(grp == h, stacked[h * CHUNK:(h + 1) * CHUNK], 0.0)
    return out


def _sg_specs(tt):
    return [pl.BlockSpec((tt, SG_WIDTH), lambda i: (i, 1)), pl.BlockSpec((tt, SG_WIDTH), lambda i: (i, 2))]


def _sg_fwd(proj, wm, bias, g):
    T = proj.shape[0]
    tt = _tile(T, 512)

    def body(u_ref, v_ref, wm_ref, b_ref, g_ref, y_ref):
        zu = _gelu(u_ref[...])
        vn, _, _ = _rms_fwd(_gelu(v_ref[...]), g_ref[...])
        grp = lax.broadcasted_iota(jnp.int32, (CHUNK, SG_WIDTH), 1) // SB_HD
        for n in range(tt // CHUNK):
            rows = slice(n * CHUNK, (n + 1) * CHUNK)
            sv = _head_select(_nn(wm_ref[...], vn[rows].astype(BF16)), grp) + b_ref[...]
            y_ref[rows, :] = zu[rows] * sv

    return pl.pallas_call(
        body, name="sg_fwd", grid=(T // tt,),
        in_specs=_sg_specs(tt) + [pl.BlockSpec((SG_HEADS * CHUNK, CHUNK), lambda i: (0, 0)),
                                  pl.BlockSpec((CHUNK, SG_WIDTH), lambda i: (0, 0)),
                                  pl.BlockSpec((1, SG_WIDTH), lambda i: (0, 0))],
        out_specs=pl.BlockSpec((tt, SG_WIDTH), lambda i: (i, 0)),
        out_shape=jax.ShapeDtypeStruct((T, SG_WIDTH), F32),
        compiler_params=_params(),
    )(proj, proj, wm, bias, g)


def _sg_bwd(proj, dymix, wm, wmt, bias, g, hosted=None):
    T = proj.shape[0]
    tt = _tile(T, 512)
    nblk = T // tt

    def body(*refs):
        i = pl.program_id(0)
        (u_ref, v_ref, dy_ref, wm_ref, wmt_ref, b_ref, g_ref, du_ref, dv_ref, dw_ref, db_ref, dg_ref,
         dvn_ref, dbias_ref), start, wait = _host(hosted, refs, 7, 5, i == 0, i == nblk - 1)
        start()
        up, vp = u_ref[...], v_ref[...]
        gv = g_ref[...]
        zu, zv = _gelu(up), _gelu(vp)
        vn, xhat, r = _rms_fwd(zv, gv)
        gu = _gelu_grad(up)
        grp = lax.broadcasted_iota(jnp.int32, (CHUNK, SG_WIDTH), 1) // SB_HD

        @pl.when(i == 0)
        def _():
            dw_ref[...] = jnp.zeros_like(dw_ref)
            dbias_ref[...] = jnp.zeros_like(dbias_ref)
            dg_ref[...] = jnp.zeros_like(dg_ref)

        for n in range(tt // CHUNK):
            rows = slice(n * CHUNK, (n + 1) * CHUNK)
            vc = vn[rows].astype(BF16)
            sv = _head_select(_nn(wm_ref[...], vc), grp) + b_ref[...]
            dy = dy_ref[rows, :]
            du_ref[rows, :] = (dy * sv * gu[rows]).astype(BF16)
            dsv = dy * zu[rows]
            dsvb = dsv.astype(BF16)
            dvn_ref[rows, :] = _head_select(_nn(wmt_ref[...], dsvb), grp)
            stacked = jnp.concatenate([jnp.where(grp == h, dsv, 0.0) for h in range(SG_HEADS)], axis=0)
            dw_ref[...] += _nt(stacked.astype(BF16), vc)
            dbias_ref[...] += dsv

        dzv, dgrow = _rms_bwd(dvn_ref[...], xhat, r, gv)
        dg_ref[...] += jnp.sum(dgrow, axis=0, keepdims=True)
        dv_ref[...] = (dzv * _gelu_grad(vp)).astype(BF16)

        @pl.when(i == nblk - 1)
        def _():
            t_i = lax.broadcasted_iota(jnp.int32, (SG_HEADS * CHUNK, CHUNK), 0) % CHUNK
            s_i = lax.broadcasted_iota(jnp.int32, (SG_HEADS * CHUNK, CHUNK), 1)
            dw_ref[...] = jnp.where(s_i <= t_i, dw_ref[...], 0.0)
            lane = lax.broadcasted_iota(jnp.int32, (CHUNK, LANES), 1)
            acc = jnp.zeros((CHUNK, LANES), F32)
            for h in range(SG_HEADS):
                tot = jnp.sum(jnp.where(grp == h, dbias_ref[...], 0.0), axis=1, keepdims=True)
                acc = acc + jnp.where(lane == h, tot, 0.0)
            db_ref[...] = acc

        wait()

    h_ins = hosted.ins if hosted else []
    res = pl.pallas_call(
        body, name="sg_bwd_hosting" if hosted else "sg_bwd", grid=(nblk,),
        in_specs=_sg_specs(tt) + [pl.BlockSpec((tt, SG_WIDTH), lambda i: (i, 1)),
                                  pl.BlockSpec((SG_HEADS * CHUNK, CHUNK), lambda i: (0, 0)),
                                  pl.BlockSpec((SG_HEADS * CHUNK, CHUNK), lambda i: (0, 0)),
                                  pl.BlockSpec((CHUNK, SG_WIDTH), lambda i: (0, 0)),
                                  pl.BlockSpec((1, SG_WIDTH), lambda i: (0, 0))] + [HBM_SPEC] * len(h_ins),
        out_specs=[pl.BlockSpec((tt, SG_WIDTH), lambda i: (i, 0)), pl.BlockSpec((tt, SG_WIDTH), lambda i: (i, 0)),
                   pl.BlockSpec((SG_HEADS * CHUNK, CHUNK), lambda i: (0, 0)),
                   pl.BlockSpec((CHUNK, LANES), lambda i: (0, 0)), pl.BlockSpec((1, SG_WIDTH), lambda i: (0, 0))]
        + [HBM_SPEC] * len(h_ins),
        out_shape=[jax.ShapeDtypeStruct((T, SG_WIDTH), BF16), jax.ShapeDtypeStruct((T, SG_WIDTH), BF16),
                   jax.ShapeDtypeStruct((SG_HEADS * CHUNK, CHUNK), F32),
                   jax.ShapeDtypeStruct((CHUNK, LANES), F32), jax.ShapeDtypeStruct((1, SG_WIDTH), F32)]
        + (hosted.out_shapes if hosted else []),
        scratch_shapes=[pltpu.VMEM((tt, SG_WIDTH), F32), pltpu.VMEM((CHUNK, SG_WIDTH), F32)]
        + (hosted.sems() if hosted else []),
        compiler_params=_params(has_side_effects=hosted is not None),
    )(proj, proj, dymix, wm, wmt, bias, g, *h_ins)
    return res[:5], res[5:]


def _split_dot(x, u):
    hi = x.astype(BF16)
    lo = (x - hi.astype(F32)).astype(BF16)
    return _nn(hi, u) + _nn(lo, u)


def _sb_logits(z):
    lb = jnp.minimum(z, 0.0) - jnp.log(1.0 + jnp.exp(-jnp.abs(z)))
    return lb, lb - z


ATTN_STRIP = 32
ATTN_SUBS = 2


def _by_strips(n_rows, fn):
    parts = None
    for r in range(0, n_rows, ATTN_STRIP):
        res = fn(slice(r, r + ATTN_STRIP))
        parts = [[v] for v in res] if parts is None else [p + [v] for p, v in zip(parts, res)]
    return [jnp.concatenate(p, axis=0) for p in parts]


def _attn_qkv_specs(tq, T):
    base = (IN_COLS - 3 * SB_WIDTH - QKV_OFF) // LANES
    nb = SB_WIDTH // LANES
    return [pl.BlockSpec((tq, LANES), lambda p, i: (i, base + p)),
            pl.BlockSpec((T, LANES), lambda p, i: (0, base + nb + p)),
            pl.BlockSpec((T, LANES), lambda p, i: (0, base + 2 * nb + p))]


class _Hosted:
    def __init__(self, ins, out_shapes, n_sems, copies, in_place=False):
        self.ins, self.out_shapes, self.n_sems, self.copies = ins, out_shapes, n_sems, copies
        self.in_place = in_place

    @property
    def n(self):
        return len(self.ins)

    def aliases(self, n_in, n_out):
        return {n_in + k: n_out + k for k in range(self.n)} if self.in_place else {}

    def sems(self):
        return [pltpu.SemaphoreType.DMA((self.n_sems,)), pltpu.SemaphoreType.DMA((self.n_sems,))]

    def start(self, src, dst, ssem, rsem):
        for send, _ in self.copies(src, dst, ssem, rsem):
            send.start()

    def wait(self, src, dst, ssem, rsem):
        for send, recv in self.copies(src, dst, ssem, rsem):
            recv.wait_recv()
            send.wait_send()


def _host(hosted, refs, n_in, n_out, first, last):
    if hosted is None:
        return refs, lambda: None, lambda: None
    n = hosted.n
    own_in, h_in = refs[:n_in], refs[n_in:n_in + n]
    own_out, h_out = refs[n_in + n:n_in + n + n_out], refs[n_in + n + n_out:n_in + 2 * n + n_out]
    rest = refs[n_in + 2 * n + n_out:]
    ssem, rsem = rest[-2:]

    def start():
        if first is True:
            hosted.start(h_in, h_out, ssem, rsem)
        else:
            pl.when(first)(lambda: hosted.start(h_in, h_out, ssem, rsem))

    def wait():
        if last is True:
            hosted.wait(h_in, h_out, ssem, rsem)
        else:
            pl.when(last)(lambda: hosted.wait(h_in, h_out, ssem, rsem))

    return own_in + own_out + rest[:-2], start, wait


def _attn_fwd(qkv, hosted=None):
    T = qkv.shape[0]
    tk = _tile(T, ATTN_TILE)
    n_sub = ATTN_SUBS if T % (ATTN_SUBS * tk) == 0 else 1
    tq = n_sub * tk
    n_p, nq = SB_WIDTH // LANES, T // tq

    def body(*refs):
        p, i = pl.program_id(0), pl.program_id(1)
        (q_ref, k_ref, v_ref, o_ref), start, wait = _host(
            hosted, refs, 3, 1, jnp.logical_and(p == 0, i == 0), jnp.logical_and(p == n_p - 1, i == nq - 1))
        start()
        lane = lax.broadcasted_iota(jnp.int32, (tk, LANES), 1)
        row = lax.broadcasted_iota(jnp.int32, (tk, tk), 0)
        col = lax.broadcasted_iota(jnp.int32, (tk, tk), 1)
        after = jnp.where(row > col, 1.0, 0.0).astype(BF16)
        valid = col < row
        qh = {}
        for sb in range(n_sub):
            q = q_ref[sb * tk:(sb + 1) * tk, :].astype(F32)
            for hh in range(2):
                qh[(sb, hh)] = jnp.where((lane // SB_HD) == hh, q * SB_SCALE, 0.0).astype(BF16)

        def tiles(todo, state):
            chains = [(n, hh) for n in range(len(todo)) for hh in range(2)]
            kv = []
            for _, j, _ in todo:
                ks = pl.ds(pl.multiple_of(j * tk, tk), tk)
                kv.append((k_ref[ks, :], v_ref[ks, :]))
            z = {(n, hh): _nt(qh[(todo[n][0], hh)], kv[n][0]) for n, hh in chains}
            lb, lmb, lm_sum = {}, {}, {}
            for n, hh in chains:
                def logits(rows, z=z[(n, hh)], mask=todo[n][2]):
                    lb, lm = _sb_logits(z[rows])
                    if mask is not None:
                        lm = jnp.where(mask[rows], lm, 0.0)
                    return lb, lm.astype(BF16), jnp.sum(lm, axis=1, keepdims=True)

                lb[(n, hh)], lmb[(n, hh)], lm_sum[(n, hh)] = _by_strips(tk, logits)
            x = {c: _nn(lmb[c], after) for c in chains}
            new = dict(state)
            for n, hh in chains:
                key = (todo[n][0], hh)
                carry, acc = new[key]

                def weights(rows, lb=lb[(n, hh)], x=x[(n, hh)], carry=carry, mask=todo[n][2]):
                    a = jnp.exp(lb[rows] + x[rows] + carry[rows])
                    if mask is not None:
                        a = jnp.where(mask[rows], a, 0.0)
                    return (a.astype(BF16),)

                (ab,) = _by_strips(tk, weights)
                new[key] = (carry + lm_sum[(n, hh)], acc + _nn(ab, kv[n][1]))
            return new

        def live(state, sb):
            return jnp.maximum(jnp.max(state[(sb, 0)][0]), jnp.max(state[(sb, 1)][0]))

        first = n_sub * i
        zero = (jnp.zeros((tk, 1), F32), jnp.zeros((tk, LANES), F32))
        todo = []
        for sb in range(n_sub):
            gate = jnp.broadcast_to(first > 0, (tk, tk)) if sb == 0 else None
            todo += [(sb, first + sb, valid), (sb, jnp.maximum(first + sb - 1, 0), gate)]
        state = tiles(todo, {(sb, hh): zero for sb in range(n_sub) for hh in range(2)})
        for sb in range(n_sub):
            def cond(st):
                return jnp.logical_and(st[0] >= 0, st[2] > UNDERFLOW)

            def step(st, sb=sb):
                mine = tiles([(sb, st[0], None)], st[1])
                return st[0] - 1, mine, live(mine, sb)

            mine = {k: v for k, v in state.items() if k[0] == sb}
            _, mine, _ = lax.while_loop(cond, step, (first + sb - 2, mine, live(mine, sb)))
            o_ref[sb * tk:(sb + 1) * tk, :] = jnp.where(lane < SB_HD, mine[(sb, 0)][1], mine[(sb, 1)][1])
        wait()

    h_ins = hosted.ins if hosted else []
    res = pl.pallas_call(
        body, name="attn_fwd_hosting" if hosted else "attn_fwd", grid=(n_p, nq),
        in_specs=_attn_qkv_specs(tq, T) + [HBM_SPEC] * len(h_ins),
        out_specs=[pl.BlockSpec((tq, LANES), lambda p, i: (i, p))] + [HBM_SPEC] * len(h_ins),
        out_shape=[jax.ShapeDtypeStruct((T, SB_WIDTH), F32)] + (hosted.out_shapes if hosted else []),
        input_output_aliases=hosted.aliases(3, 1) if hosted else {},
        scratch_shapes=hosted.sems() if hosted else [],
        compiler_params=_params(has_side_effects=hosted is not None),
    )(qkv, qkv, qkv, *h_ins)
    return res[0], res[1:]


def _attn_bwd(qkv, o, dymix, hosted=None):
    T = qkv.shape[0]
    tk = _tile(T, ATTN_TILE)
    n_sub = ATTN_SUBS if T % (ATTN_SUBS * tk) == 0 else 1
    tq = n_sub * tk
    n_p, nq = SB_WIDTH // LANES, T // tq
    yc_blk = (POOL_WIDTH + SG_WIDTH) // LANES

    def body(*refs):
        p, i = pl.program_id(0), pl.program_id(1)
        (q_ref, k_ref, v_ref, o_ref, do_ref, dq_ref, dk_ref, dv_ref), start, wait = _host(
            hosted, refs, 5, 3, jnp.logical_and(p == 0, i == 0), jnp.logical_and(p == n_p - 1, i == nq - 1))
        start()
        lane = lax.broadcasted_iota(jnp.int32, (tk, LANES), 1)
        row = lax.broadcasted_iota(jnp.int32, (tk, tk), 0)
        col = lax.broadcasted_iota(jnp.int32, (tk, tk), 1)
        after = jnp.where(row > col, 1.0, 0.0).astype(BF16)
        from_here = jnp.where(row >= col, 1.0, 0.0).astype(BF16)
        from_here2 = jnp.concatenate([from_here, from_here], axis=0)
        valid = col < row

        @pl.when(i == 0)
        def _():
            dk_ref[...] = jnp.zeros_like(dk_ref)
            dv_ref[...] = jnp.zeros_like(dv_ref)

        qh, dohb, delta = {}, {}, {}
        for sb in range(n_sub):
            rows = slice(sb * tk, (sb + 1) * tk)
            q, ov, dov = q_ref[rows, :].astype(F32), o_ref[rows, :], do_ref[rows, :]
            for hh in range(2):
                head = (lane // SB_HD) == hh
                qh[(sb, hh)] = jnp.where(head, q * SB_SCALE, 0.0).astype(BF16)
                dohb[(sb, hh)] = jnp.where(head, dov, 0.0).astype(BF16)
                delta[(sb, hh)] = jnp.sum(dohb[(sb, hh)].astype(F32) * ov, axis=1, keepdims=True)

        def tiles(todo, state):
            chains = [(n, hh) for n in range(len(todo)) for hh in range(2)]
            kv, where = [], []
            for _, j, _ in todo:
                ks = pl.ds(pl.multiple_of(j * tk, tk), tk)
                where.append(ks)
                kv.append((k_ref[ks, :], v_ref[ks, :]))
            z = {(n, hh): _nt(qh[(todo[n][0], hh)], kv[n][0]) for n, hh in chains}
            da = {(n, hh): _nt(dohb[(todo[n][0], hh)], kv[n][1]) for n, hh in chains}
            lb, lmb, lm_sum = {}, {}, {}
            for n, hh in chains:
                def logits(rows, z=z[(n, hh)], mask=todo[n][2]):
                    lb, lm = _sb_logits(z[rows])
                    if mask is not None:
                        lm = jnp.where(mask[rows], lm, 0.0)
                    return lb, lm.astype(BF16), jnp.sum(lm, axis=1, keepdims=True)

                lb[(n, hh)], lmb[(n, hh)], lm_sum[(n, hh)] = _by_strips(tk, logits)
            x = {c: _nn(lmb[c], after) for c in chains}
            c_a = {k: v[0] for k, v in state.items()}
            ab, g, g_split, g_sum = {}, {}, {}, {}
            for n, hh in chains:
                key = (todo[n][0], hh)

                def weights(rows, lb=lb[(n, hh)], x=x[(n, hh)], da=da[(n, hh)], c_a=c_a[key], mask=todo[n][2]):
                    a = jnp.exp(lb[rows] + x[rows] + c_a[rows])
                    if mask is not None:
                        a = jnp.where(mask[rows], a, 0.0)
                    ab = a.astype(BF16)
                    g = da[rows] * ab.astype(F32)
                    hi = g.astype(BF16)
                    lo = (g - hi.astype(F32)).astype(BF16)
                    return ab, g, jnp.concatenate([hi, lo], axis=1), jnp.sum(g, axis=1, keepdims=True)

                ab[(n, hh)], g[(n, hh)], g_split[(n, hh)], g_sum[(n, hh)] = _by_strips(tk, weights)
                c_a[key] = c_a[key] + lm_sum[(n, hh)]
            right = {c: _nn(g_split[c], from_here2) for c in chains}
            c_r = {k: v[1] for k, v in state.items()}
            dzb = {}
            for n, hh in chains:
                key = (todo[n][0], hh)

                def logit_grads(rows, lb=lb[(n, hh)], g=g[(n, hh)], right=right[(n, hh)], c_r=c_r[key],
                                delta=delta[key], mask=todo[n][2]):
                    sig = jnp.exp(lb[rows])
                    left = delta[rows] - (c_r[rows] + right[rows])
                    dz = g[rows] * (1.0 - sig) - left * sig
                    if mask is not None:
                        dz = jnp.where(mask[rows], dz, 0.0)
                    return (dz.astype(BF16),)

                (dzb[(n, hh)],) = _by_strips(tk, logit_grads)
                c_r[key] = c_r[key] + g_sum[(n, hh)]
            dqa = {k: v[2] for k, v in state.items()}
            for n in range(len(todo)):
                sb = todo[n][0]
                dk_ref[where[n], :] += _tn(dzb[(n, 0)], qh[(sb, 0)]) + _tn(dzb[(n, 1)], qh[(sb, 1)])
                dv_ref[where[n], :] += _tn(ab[(n, 0)], dohb[(sb, 0)]) + _tn(ab[(n, 1)], dohb[(sb, 1)])
                for hh in range(2):
                    dqa[(sb, hh)] = dqa[(sb, hh)] + _nn(dzb[(n, hh)], kv[n][0])
            return {k: (c_a[k], c_r[k], dqa[k]) for k in state}

        def live(state, sb):
            return jnp.maximum(jnp.max(state[(sb, 0)][0]), jnp.max(state[(sb, 1)][0]))

        first = n_sub * i
        zero = (jnp.zeros((tk, 1), F32), jnp.zeros((tk, 1), F32), jnp.zeros((tk, LANES), F32))
        todo = []
        for sb in range(n_sub):
            gate = jnp.broadcast_to(first > 0, (tk, tk)) if sb == 0 else None
            todo += [(sb, first + sb, valid), (sb, jnp.maximum(first + sb - 1, 0), gate)]
        state = tiles(todo, {(sb, hh): zero for sb in range(n_sub) for hh in range(2)})
        for sb in range(n_sub):
            def cond(st):
                return jnp.logical_and(st[0] >= 0, st[2] > UNDERFLOW)

            def step(st, sb=sb):
                mine = tiles([(sb, st[0], None)], st[1])
                return st[0] - 1, mine, live(mine, sb)

            mine = {k: v for k, v in state.items() if k[0] == sb}
            _, mine, _ = lax.while_loop(cond, step, (first + sb - 2, mine, live(mine, sb)))
            dq_ref[sb * tk:(sb + 1) * tk, :] = (
                jnp.where(lane < SB_HD, mine[(sb, 0)][2], mine[(sb, 1)][2]) * SB_SCALE).astype(BF16)
        wait()

    h_ins = hosted.ins if hosted else []
    res = pl.pallas_call(
        body, name="attn_bwd_hosting" if hosted else "attn_bwd", grid=(n_p, nq),
        in_specs=_attn_qkv_specs(tq, T) + [pl.BlockSpec((tq, LANES), lambda p, i: (i, p)),
                                           pl.BlockSpec((tq, LANES), lambda p, i: (i, yc_blk + p))]
        + [HBM_SPEC] * len(h_ins),
        out_specs=[pl.BlockSpec((tq, LANES), lambda p, i: (i, p)), pl.BlockSpec((T, LANES), lambda p, i: (0, p)),
                   pl.BlockSpec((T, LANES), lambda p, i: (0, p))] + [HBM_SPEC] * len(h_ins),
        out_shape=[jax.ShapeDtypeStruct((T, SB_WIDTH), BF16)] + [jax.ShapeDtypeStruct((T, SB_WIDTH), F32)] * 2
        + (hosted.out_shapes if hosted else []),
        scratch_shapes=hosted.sems() if hosted else [],
        compiler_params=_params(has_side_effects=hosted is not None),
    )(qkv, qkv, qkv, o, dymix, *h_ins)
    return res[:3], res[3:]


def _outproj_fwd(x, ya, yb, yc, w, hosted=None):
    T, D = x.shape
    tt = _tile(T, 512)
    nt = T // tt

    def body(*refs):
        i = pl.program_id(0)
        (x_ref, ya_ref, yb_ref, yc_ref, w_ref, x1_ref, ymix_ref), start, wait = _host(
            hosted, refs, 5, 2, i == 0, i == nt - 1)
        start()
        ymix_ref[:, 0:POOL_WIDTH] = ya_ref[...].astype(BF16)
        ymix_ref[:, POOL_WIDTH:POOL_WIDTH + SG_WIDTH] = yb_ref[...].astype(BF16)
        ymix_ref[:, POOL_WIDTH + SG_WIDTH:] = yc_ref[...].astype(BF16)
        x1_ref[...] = x_ref[...] + _nn(ymix_ref[...], w_ref[...])
        wait()

    row = lambda width: pl.BlockSpec((tt, width), lambda i: (i, 0))
    h_ins = hosted.ins if hosted else []
    res = pl.pallas_call(
        body, name="outproj_fwd_hosting" if hosted else "outproj_fwd", grid=(nt,),
        in_specs=[row(D), row(POOL_WIDTH), row(SG_WIDTH), row(SB_WIDTH), pl.BlockSpec((D, D), lambda i: (0, 0))]
        + [HBM_SPEC] * len(h_ins),
        out_specs=[row(D), row(D)] + [HBM_SPEC] * len(h_ins),
        out_shape=[jax.ShapeDtypeStruct((T, D), F32), jax.ShapeDtypeStruct((T, D), BF16)]
        + (hosted.out_shapes if hosted else []),
        input_output_aliases=hosted.aliases(5, 2) if hosted else {},
        scratch_shapes=hosted.sems() if hosted else [],
        compiler_params=_params(has_side_effects=hosted is not None),
    )(x, ya, yb, yc, w, *h_ins)
    return res[:2], res[2:]


def _nt_matmul(a, w):
    T, N = a.shape
    K = w.shape[0]
    tt = _tile(T, 512)

    def body(a_ref, w_ref, o_ref):
        o_ref[...] = _nt(a_ref[...].astype(BF16), w_ref[...])

    return pl.pallas_call(
        body, name="nt_matmul", grid=(T // tt,),
        in_specs=[pl.BlockSpec((tt, N), lambda i: (i, 0)), pl.BlockSpec((K, N), lambda i: (0, 0))],
        out_specs=pl.BlockSpec((tt, K), lambda i: (i, 0)),
        out_shape=jax.ShapeDtypeStruct((T, K), F32),
        compiler_params=_params(),
    )(a, w)


def _tn_matmul(a, b, name, n_split=1, hosted=None):
    T, K = a.shape
    N = b.shape[1]
    tk = _tile(K, 1024)
    tn = _tile(N // n_split, 1024)
    tt = _tile(T, 2048)
    nper = N // n_split // tn
    nk, nn, nt = K // tk, N // tn, T // tt

    def body(*refs):
        k, n, t = pl.program_id(0), pl.program_id(1), pl.program_id(2)
        (a_ref, b_ref, o_ref), start, wait = _host(
            hosted, refs, 2, 1, jnp.logical_and(jnp.logical_and(k == 0, n == 0), t == 0),
            jnp.logical_and(jnp.logical_and(k == nk - 1, n == nn - 1), t == nt - 1))
        start()

        @pl.when(t == 0)
        def _():
            o_ref[...] = jnp.zeros_like(o_ref)

        o_ref[...] += _tn(a_ref[...], b_ref[...].astype(BF16))
        wait()

    h_ins = hosted.ins if hosted else []
    res = pl.pallas_call(
        body, name=name + "_hosting" if hosted else name, grid=(nk, nn, nt),
        in_specs=[pl.BlockSpec((tt, tk), lambda k, n, t: (t, k)), pl.BlockSpec((tt, tn), lambda k, n, t: (t, n))]
        + [HBM_SPEC] * len(h_ins),
        out_specs=[pl.BlockSpec((None, tk, tn), lambda k, n, t: (n // nper, k, n % nper))] + [HBM_SPEC] * len(h_ins),
        out_shape=[jax.ShapeDtypeStruct((n_split, K, N // n_split), F32)] + (hosted.out_shapes if hosted else []),
        scratch_shapes=hosted.sems() if hosted else [],
        compiler_params=_params(has_side_effects=hosted is not None),
    )(a, b, *h_ins)
    return (res[0], res[1:]) if hosted else res[0]


def _resident(shape):
    return pl.BlockSpec(shape, lambda *_: (0,) * len(shape), pipeline_mode=pl.Buffered(1))


def _mlp_fwd(x, g, w_up, w_down, hosted=None):
    T, D = x.shape
    F = w_up.shape[0]
    tt = _tile(T, 1024)
    fc = _tile(F, MLP_CHUNK)
    nc = F // fc
    nt = T // tt

    def body(*refs):
        i, c = pl.program_id(0), pl.program_id(1)
        (x_ref, g_ref, wu_ref, wd_ref, y_ref, h_ref, u_ref, a_ref), start, wait = _host(
            hosted, refs, 4, 4, jnp.logical_and(i == 0, c == 0), jnp.logical_and(i == nt - 1, c == nc - 1))
        start()

        @pl.when(c == 0)
        def _():
            xv = x_ref[...]
            h, _, _ = _rms_fwd(xv, g_ref[...])
            h_ref[...] = h.astype(BF16)
            y_ref[...] = xv

        chunk = pl.ds(pl.multiple_of(c * fc, fc), fc)
        u = _nt(h_ref[...], wu_ref[chunk, :])
        u_ref[...] = u.astype(BF16)
        a = jnp.square(jnp.maximum(u, 0.0)).astype(BF16)
        a_ref[...] = a
        y_ref[...] += _nn(a, wd_ref[chunk, :])
        wait()

    h_ins = hosted.ins if hosted else []
    res = pl.pallas_call(
        body, name="mlp_fwd_hosting" if hosted else "mlp_fwd", grid=(nt, nc),
        in_specs=[pl.BlockSpec((tt, D), lambda i, c: (i, 0)), pl.BlockSpec((1, D), lambda i, c: (0, 0)),
                  _resident((F, D)), _resident((F, D))]
        + [HBM_SPEC] * len(h_ins),
        out_specs=[pl.BlockSpec((tt, D), lambda i, c: (i, 0)), pl.BlockSpec((tt, D), lambda i, c: (i, 0)),
                   pl.BlockSpec((tt, fc), lambda i, c: (i, c)), pl.BlockSpec((tt, fc), lambda i, c: (i, c))]
        + [HBM_SPEC] * len(h_ins),
        out_shape=[jax.ShapeDtypeStruct((T, D), F32), jax.ShapeDtypeStruct((T, D), BF16),
                   jax.ShapeDtypeStruct((T, F), BF16), jax.ShapeDtypeStruct((T, F), BF16)]
        + (hosted.out_shapes if hosted else []),
        scratch_shapes=hosted.sems() if hosted else [],
        compiler_params=_params(has_side_effects=hosted is not None),
    )(x, g, w_up, w_down, *h_ins)
    return res[:4], res[4:]


def _mlp_bwd(dy, x, g, u, w_up, w_down, hosted=None):
    T, D = x.shape
    F = w_up.shape[0]
    tt = _tile(T, 512)
    fc = _tile(F, MLP_CHUNK)
    nc = F // fc
    nt = T // tt

    def body(*refs):
        i, c = pl.program_id(0), pl.program_id(1)
        (dy_ref, x_ref, g_ref, u_ref, wu_ref, wd_ref, dx_ref, du_ref, dg_ref, dyb_ref, dh_ref), start, wait = _host(
            hosted, refs, 6, 3, jnp.logical_and(i == 0, c == 0), jnp.logical_and(i == nt - 1, c == nc - 1))
        start()

        @pl.when(c == 0)
        def _():
            dyb_ref[...] = dy_ref[...].astype(BF16)
            dh_ref[...] = jnp.zeros_like(dh_ref)

        @pl.when(jnp.logical_and(i == 0, c == 0))
        def _():
            dg_ref[...] = jnp.zeros_like(dg_ref)

        chunk = pl.ds(pl.multiple_of(c * fc, fc), fc)
        da = _nt(dyb_ref[...], wd_ref[chunk, :])
        du = (da * (2.0 * jnp.maximum(u_ref[...].astype(F32), 0.0))).astype(BF16)
        du_ref[...] = du
        dh_ref[...] += _nn(du, wu_ref[chunk, :])

        @pl.when(c == nc - 1)
        def _():
            gv = g_ref[...]
            _, xhat, r = _rms_fwd(x_ref[...], gv)
            dx, dgrow = _rms_bwd(dh_ref[...], xhat, r, gv)
            dx_ref[...] = dy_ref[...] + dx
            dg_ref[...] += jnp.sum(dgrow, axis=0, keepdims=True)

        wait()

    h_ins = hosted.ins if hosted else []
    res = pl.pallas_call(
        body, name="mlp_bwd_hosting" if hosted else "mlp_bwd", grid=(nt, nc),
        in_specs=[pl.BlockSpec((tt, D), lambda i, c: (i, 0)), pl.BlockSpec((tt, D), lambda i, c: (i, 0)),
                  pl.BlockSpec((1, D), lambda i, c: (0, 0)), pl.BlockSpec((tt, fc), lambda i, c: (i, c)),
                  _resident((F, D)), _resident((F, D))]
        + [HBM_SPEC] * len(h_ins),
        out_specs=[pl.BlockSpec((tt, D), lambda i, c: (i, 0)), pl.BlockSpec((tt, fc), lambda i, c: (i, c)),
                   pl.BlockSpec((1, D), lambda i, c: (0, 0))] + [HBM_SPEC] * len(h_ins),
        out_shape=[jax.ShapeDtypeStruct((T, D), F32), jax.ShapeDtypeStruct((T, F), BF16),
                   jax.ShapeDtypeStruct((1, D), F32)] + (hosted.out_shapes if hosted else []),
        scratch_shapes=[pltpu.VMEM((tt, D), BF16), pltpu.VMEM((tt, D), F32)] + (hosted.sems() if hosted else []),
        compiler_params=_params(has_side_effects=hosted is not None),
    )(dy, x, g, u, w_up, w_down, *h_ins)
    return res[:3], res[3:]


def _loss_head(x, g, target):
    T, D = x.shape
    tt = _tile(T, 512)

    def body(x_ref, g_ref, t_ref, loss_ref, dx_ref, dg_ref):
        gv = g_ref[...]
        y, xhat, r = _rms_fwd(x_ref[...], gv)
        err = y - t_ref[...]
        dx, dgrow = _rms_bwd(err * (1.0 / D), xhat, r, gv)
        dx_ref[...] = dx

        @pl.when(pl.program_id(0) == 0)
        def _():
            loss_ref[...] = jnp.zeros_like(loss_ref)
            dg_ref[...] = jnp.zeros_like(dg_ref)

        loss_ref[...] += 0.5 * jnp.sum(jnp.mean(err * err, axis=-1, keepdims=True), axis=0, keepdims=True)
        dg_ref[...] += jnp.sum(dgrow, axis=0, keepdims=True)

    return pl.pallas_call(
        body, name="loss_head", grid=(T // tt,),
        in_specs=[pl.BlockSpec((tt, D), lambda i: (i, 0)), pl.BlockSpec((1, D), lambda i: (0, 0)),
                  pl.BlockSpec((tt, D), lambda i: (i, 0))],
        out_specs=[pl.BlockSpec((1, LANES), lambda i: (0, 0)), pl.BlockSpec((tt, D), lambda i: (i, 0)),
                   pl.BlockSpec((1, D), lambda i: (0, 0))],
        out_shape=[jax.ShapeDtypeStruct((1, LANES), F32), jax.ShapeDtypeStruct((T, D), F32),
                   jax.ShapeDtypeStruct((1, D), F32)],
        compiler_params=_params(),
    )(x, g, target)


def _rows(shape, pref=512):
    last = shape[-1]
    rows = 1
    for s in shape[:-1]:
        rows *= s
    tr = rows
    if rows * last > 256 * 1024:
        for cand in (pref, 256, 128, 64, 32, 16, 8):
            if rows % cand == 0:
                tr = cand
                break
    return rows, last, tr


def _elementwise(fn, name, ins, n_out, out_dtype=F32):
    shape = ins[0].shape
    rows, last, tr = _rows(shape)
    flat = [a.reshape(rows, last) for a in ins]
    n_in = len(ins)

    def body(*refs):
        res = fn(*[r[...] for r in refs[:n_in]])
        if n_out == 1:
            res = (res,)
        for r, v in zip(refs[n_in:], res):
            r[...] = v.astype(r.dtype)

    spec = pl.BlockSpec((tr, last), lambda i: (i, 0))
    outs = pl.pallas_call(
        body, name=name, grid=(rows // tr,),
        in_specs=[spec] * n_in, out_specs=[spec] * n_out,
        out_shape=[jax.ShapeDtypeStruct((rows, last), out_dtype)] * n_out,
        compiler_params=_params(),
    )(*flat)
    return [o.reshape(shape) for o in outs]


def _add_pair(g, o, c_idx):
    nq, R, C = g.shape
    h = R // 2
    tr = _tile(h, 512)
    nb = h // tr

    def body(c_ref, g_ref, o_ref, out_ref):
        out_ref[...] = g_ref[...] + o_ref[...]

    return pl.pallas_call(
        body, name="add_pair",
        grid_spec=pltpu.PrefetchScalarGridSpec(
            num_scalar_prefetch=1, grid=(nq, nb),
            in_specs=[pl.BlockSpec((None, tr, C), lambda q, i, c: (q, c[0] * nb + i, 0)),
                      pl.BlockSpec((None, tr, C), lambda q, i, c: (q, i, 0))],
            out_specs=pl.BlockSpec((None, tr, C), lambda q, i, c: (q, i, 0))),
        out_shape=jax.ShapeDtypeStruct((nq, h, C), F32),
        compiler_params=_params(),
    )(c_idx.astype(jnp.int32).reshape(1), g, o)


def _add_chips(p, r, q_idx):
    _, H, C = p.shape
    tr = _tile(H, 512)

    def body(q_ref, p_ref, r0_ref, r1_ref, r2_ref, out_ref):
        out_ref[...] = (p_ref[...] + r0_ref[...]) + (r1_ref[...] + r2_ref[...])

    def arrived(k):
        return pl.BlockSpec((None, tr, C), lambda i, q: (k, i, 0))

    return pl.pallas_call(
        body, name="add_chips",
        grid_spec=pltpu.PrefetchScalarGridSpec(
            num_scalar_prefetch=1, grid=(H // tr,),
            in_specs=[pl.BlockSpec((None, tr, C), lambda i, q: (q[0], i, 0)), arrived(0), arrived(1), arrived(2)],
            out_specs=pl.BlockSpec((tr, C), lambda i, q: (i, 0))),
        out_shape=jax.ShapeDtypeStruct((H, C), F32),
        compiler_params=_params(),
    )(q_idx.astype(jnp.int32).reshape(1), p, r, r, r)


def _adamw(w, g, m, v):
    m = ADAM_B1 * m + (1.0 - ADAM_B1) * g
    v = ADAM_B2 * v + (1.0 - ADAM_B2) * jnp.square(g)
    m_hat = m / (1.0 - ADAM_B1 ** ADAM_STEP)
    v_hat = v / (1.0 - ADAM_B2 ** ADAM_STEP)
    delta = -ADAM_LR * (m_hat / (jnp.sqrt(v_hat) + ADAM_EPS) + ADAM_WD * w)
    return delta, m, v


def _place():
    x, y, c = lax.axis_index("x"), lax.axis_index("y"), lax.axis_index("c")
    chips = [(1 - x, y), (x, 1 - y), (1 - x, 1 - y)]
    return x, y, c, chips


def _remote(src, dst, ssem, rsem, k, dev):
    return pltpu.make_async_remote_copy(src_ref=src, dst_ref=dst, send_sem=ssem.at[k], recv_sem=rsem.at[k],
                                        device_id=dev, device_id_type=MESH)


def _gather_weights(shards):
    n = len(shards)
    halves = [s.shape[1] // 2 for s in shards]

    def body(*refs):
        src, out = refs[:n], refs[n:2 * n]
        ssem, rsem = refs[2 * n:]
        x, y, c, chips = _place()
        me_q = 2 * x + y
        sib = (x, y, 1 - c)

        def half(a, q, cc):
            return out[a].at[q, :, pl.ds(cc * halves[a], halves[a]), :]

        first = []
        for a in range(n):
            mine = src[a].at[:, pl.ds(c * halves[a], halves[a]), :]
            for r, chip in enumerate(chips):
                first.append(_remote(mine, half(a, me_q, c), ssem, rsem, a * 3 + r, (*chip, c)))
        for cp in first:
            cp.start()
        passed = []
        for a in range(n):
            for r, chip in enumerate(chips):
                q = 2 * chip[0] + chip[1]
                k = a * 3 + r
                _remote(half(a, q, c), half(a, q, c), ssem, rsem, k, (*chip, c)).wait_recv()
                cp = _remote(half(a, q, c), half(a, q, c), ssem, rsem, 3 * n + k, sib)
                cp.start()
                passed.append(cp)
        for a in range(n):
            for r, chip in enumerate(chips):
                q = 2 * chip[0] + chip[1]
                _remote(half(a, q, 1 - c), half(a, q, 1 - c), ssem, rsem, 3 * n + a * 3 + r, sib).wait_recv()
        for cp in first + passed:
            cp.wait_send()

    return pl.pallas_call(
        body, name="gather_weights",
        in_specs=[HBM_SPEC] * n, out_specs=[HBM_SPEC] * n,
        out_shape=[jax.ShapeDtypeStruct((N_CHIPS,) + s.shape, s.dtype) for s in shards],
        scratch_shapes=[pltpu.SemaphoreType.DMA((6 * n,)), pltpu.SemaphoreType.DMA((6 * n,))],
        compiler_params=_params(has_side_effects=True),
    )(*shards)


def _gather_over_ici(shards):
    n = len(shards)
    halves = [s.shape[1] // 2 for s in shards]

    def copies(src, out, ssem, rsem):
        x, y, c, chips = _place()
        me_q = 2 * x + y
        res = []
        for a in range(n):
            rows = pl.ds(c * halves[a], halves[a])
            mine = src[a].at[:, rows, :]
            for r, chip in enumerate(chips):
                dev = (*chip, c)
                res.append((_remote(mine, out[a].at[me_q, :, rows, :], ssem, rsem, a * 3 + r, dev),
                            _remote(mine, out[a].at[2 * chip[0] + chip[1], :, rows, :], ssem, rsem, a * 3 + r, dev)))
        return res

    return _Hosted(list(shards), [jax.ShapeDtypeStruct((N_CHIPS,) + s.shape, s.dtype) for s in shards], 3 * n, copies)


def _pass_over_d2d(gathered):
    n = len(gathered)
    halves = [g.shape[2] // 2 for g in gathered]

    def copies(_, out, ssem, rsem):
        x, y, c, chips = _place()
        sib = (x, y, 1 - c)
        res = []
        for a in range(n):
            for r, chip in enumerate(chips):
                q = 2 * chip[0] + chip[1]
                mine = out[a].at[q, :, pl.ds(c * halves[a], halves[a]), :]
                theirs = out[a].at[q, :, pl.ds((1 - c) * halves[a], halves[a]), :]
                res.append((_remote(mine, mine, ssem, rsem, a * 3 + r, sib),
                            _remote(theirs, theirs, ssem, rsem, a * 3 + r, sib)))
        return res

    return _Hosted(list(gathered), [jax.ShapeDtypeStruct(g.shape, g.dtype) for g in gathered], 3 * n, copies,
                   in_place=True)


def _pass_to_sibling(gathered):
    n = len(gathered)
    halves = [g.shape[2] // 2 for g in gathered]

    def body(*refs):
        out = refs[n:2 * n]
        ssem, rsem = refs[2 * n:]
        x, y, c, chips = _place()
        sib = (x, y, 1 - c)

        def half(a, q, cc):
            return out[a].at[q, :, pl.ds(cc * halves[a], halves[a]), :]

        cps = []
        for a in range(n):
            for r, chip in enumerate(chips):
                q = 2 * chip[0] + chip[1]
                cps.append(_remote(half(a, q, c), half(a, q, c), ssem, rsem, a * 3 + r, sib))
        for cp in cps:
            cp.start()
        for a in range(n):
            for r, chip in enumerate(chips):
                q = 2 * chip[0] + chip[1]
                _remote(half(a, q, 1 - c), half(a, q, 1 - c), ssem, rsem, a * 3 + r, sib).wait_recv()
        for cp in cps:
            cp.wait_send()

    return pl.pallas_call(
        body, name="pass_to_sibling",
        in_specs=[HBM_SPEC] * n, out_specs=[HBM_SPEC] * n,
        out_shape=[jax.ShapeDtypeStruct(g.shape, g.dtype) for g in gathered],
        input_output_aliases={a: a for a in range(n)},
        scratch_shapes=[pltpu.SemaphoreType.DMA((3 * n,)), pltpu.SemaphoreType.DMA((3 * n,))],
        compiler_params=_params(has_side_effects=True),
    )(*gathered)


def _scatter_over_ici(parts):
    n = len(parts)

    def copies(src, out, ssem, rsem):
        x, y, c, chips = _place()
        res = []
        for a in range(n):
            for r, chip in enumerate(chips):
                cp = _remote(src[a].at[2 * chip[0] + chip[1]], out[a].at[r], ssem, rsem, a * 3 + r, (*chip, c))
                res.append((cp, cp))
        return res

    return _Hosted(list(parts), [jax.ShapeDtypeStruct((3,) + p.shape[1:], F32) for p in parts], 3 * n, copies)


def _swap_over_d2d(grads):
    n = len(grads)
    halves = [g.shape[1] // 2 for g in grads]

    def copies(src, out, ssem, rsem):
        x, y, c, _ = _place()
        res = []
        for a in range(n):
            cp = _remote(src[a].at[:, pl.ds((1 - c) * halves[a], halves[a]), :], out[a], ssem, rsem, a, (x, y, 1 - c))
            res.append((cp, cp))
        return res

    return _Hosted(list(grads), [jax.ShapeDtypeStruct((N_CHIPS, h, g.shape[2]), F32) for g, h in zip(grads, halves)],
                   n, copies)


def _swap_halves(grads):
    n = len(grads)
    halves = [g.shape[1] // 2 for g in grads]

    def body(*refs):
        src, out = refs[:n], refs[n:2 * n]
        ssem, rsem = refs[2 * n:]
        x, y, c, _ = _place()
        cps = [_remote(src[a].at[:, pl.ds((1 - c) * halves[a], halves[a]), :], out[a], ssem, rsem, a, (x, y, 1 - c))
               for a in range(n)]
        for cp in cps:
            cp.start()
        for cp in cps:
            cp.wait()

    return pl.pallas_call(
        body, name="swap_halves",
        in_specs=[HBM_SPEC] * n, out_specs=[HBM_SPEC] * n,
        out_shape=[jax.ShapeDtypeStruct((N_CHIPS, h, g.shape[2]), F32) for g, h in zip(grads, halves)],
        scratch_shapes=[pltpu.SemaphoreType.DMA((n,)), pltpu.SemaphoreType.DMA((n,))],
        compiler_params=_params(has_side_effects=True),
    )(*grads)


def _scatter_chips(parts):
    n = len(parts)

    def body(*refs):
        src, out = refs[:n], refs[n:2 * n]
        ssem, rsem = refs[2 * n:]
        x, y, c, chips = _place()
        cps = []
        for a in range(n):
            for r, chip in enumerate(chips):
                cps.append(_remote(src[a].at[2 * chip[0] + chip[1]], out[a].at[r], ssem, rsem, a * 3 + r, (*chip, c)))
        for cp in cps:
            cp.start()
        for cp in cps:
            cp.wait()

    return pl.pallas_call(
        body, name="scatter_chips",
        in_specs=[HBM_SPEC] * n, out_specs=[HBM_SPEC] * n,
        out_shape=[jax.ShapeDtypeStruct((3,) + p.shape[1:], F32) for p in parts],
        scratch_shapes=[pltpu.SemaphoreType.DMA((3 * n,)), pltpu.SemaphoreType.DMA((3 * n,))],
        compiler_params=_params(has_side_effects=True),
    )(*parts)


def _swap_reduced_over_d2d(reduced):
    n = len(reduced)

    def copies(src, out, ssem, rsem):
        x, y, c, _ = _place()
        res = []
        for a in range(n):
            cp = _remote(src[a], out[a], ssem, rsem, a, (x, y, 1 - c))
            res.append((cp, cp))
        return res

    return _Hosted(list(reduced), [jax.ShapeDtypeStruct(r.shape, F32) for r in reduced], n, copies)


def _swap_reduced(reduced):
    n = len(reduced)

    def body(*refs):
        src, out = refs[:n], refs[n:2 * n]
        ssem, rsem = refs[2 * n:]
        x, y, c, _ = _place()
        cps = [_remote(src[a], out[a], ssem, rsem, a, (x, y, 1 - c)) for a in range(n)]
        for cp in cps:
            cp.start()
        for cp in cps:
            cp.wait()

    return pl.pallas_call(
        body, name="swap_reduced",
        in_specs=[HBM_SPEC] * n, out_specs=[HBM_SPEC] * n,
        out_shape=[jax.ShapeDtypeStruct(r.shape, F32) for r in reduced],
        scratch_shapes=[pltpu.SemaphoreType.DMA((n,)), pltpu.SemaphoreType.DMA((n,))],
        compiler_params=_params(has_side_effects=True),
    )(*reduced)


def _allreduce_small(buf, hosted=None):
    R, L = buf.shape

    def body(*refs):
        (buf_ref, out_ref, pair_ref, chip_ref, ssem, rsem), start, wait = _host(hosted, refs, 1, 1, True, True)
        start()
        x, y, c, chips = _place()
        me_q = 2 * x + y
        pair_ref[c] = buf_ref[...]
        to_sib = _remote(buf_ref, pair_ref.at[c], ssem, rsem, 0, (x, y, 1 - c))
        to_sib.start()
        _remote(buf_ref, pair_ref.at[1 - c], ssem, rsem, 0, (x, y, 1 - c)).wait_recv()
        chip_ref[me_q] = pair_ref[0] + pair_ref[1]
        cps = [_remote(chip_ref.at[me_q], chip_ref.at[me_q], ssem, rsem, 1 + r, (*chip, c))
               for r, chip in enumerate(chips)]
        for cp in cps:
            cp.start()
        for r, chip in enumerate(chips):
            q = 2 * chip[0] + chip[1]
            _remote(chip_ref.at[q], chip_ref.at[q], ssem, rsem, 1 + r, (*chip, c)).wait_recv()
        out_ref[...] = (chip_ref[0] + chip_ref[1]) + (chip_ref[2] + chip_ref[3])
        to_sib.wait_send()
        for cp in cps:
            cp.wait_send()
        wait()

    h_ins = hosted.ins if hosted else []
    res = pl.pallas_call(
        body, name="allreduce_small",
        in_specs=[VMEM_SPEC] + [HBM_SPEC] * len(h_ins), out_specs=[VMEM_SPEC] + [HBM_SPEC] * len(h_ins),
        out_shape=[jax.ShapeDtypeStruct((R, L), F32)] + (hosted.out_shapes if hosted else []),
        scratch_shapes=[pltpu.VMEM((2, R, L), F32), pltpu.VMEM((N_CHIPS, R, L), F32),
                        pltpu.SemaphoreType.DMA((4,)), pltpu.SemaphoreType.DMA((4,))]
        + (hosted.sems() if hosted else []),
        compiler_params=_params(has_side_effects=True),
    )(buf, *h_ins)
    return res[0], res[1:]


def _pack(arrays):
    flat = jnp.concatenate([a.reshape(-1) for a in arrays])
    pad = (-flat.shape[0]) % (8 * LANES)
    return jnp.pad(flat, (0, pad)).reshape(-1, LANES)


def _unpack(buf, like):
    flat = buf.reshape(-1)
    out, off = [], 0
    for a in like:
        out.append(flat[off:off + a.size].reshape(a.shape))
        off += a.size
    return out


def _block_diag(pw):
    rows = []
    for gi in range(len(POOL_WINDOWS)):
        blocks = [pw[gi] if gj == gi else jnp.zeros_like(pw[gi]) for gj in range(len(POOL_WINDOWS))]
        rows.append(jnp.concatenate(blocks, axis=1))
    return jnp.concatenate(rows, axis=0)


def kernel(x, norm1, w_in, pool_w, pool_scale, sg_norm, sg_w, sg_b, w_out, norm2, w_up, w_down, final_norm, loss_target, m_norm1, m_w_in, m_pool_w, m_pool_scale, m_sg_norm, m_sg_w, m_sg_b, m_w_out, m_norm2, m_w_up, m_w_down, m_final_norm, v_norm1, v_w_in, v_pool_w, v_pool_scale, v_sg_norm, v_sg_w, v_sg_b, v_w_out, v_norm2, v_w_up, v_w_down, v_final_norm):
    depth = norm1.shape[0]
    T = x.shape[1]
    xs = x.reshape(T, D_MODEL)
    target = loss_target.reshape(T, D_MODEL)

    assert depth == 2
    c_idx = lax.axis_index("c")
    q_idx = 2 * lax.axis_index("x") + lax.axis_index("y")
    own = [w.astype(BF16) for w in (w_in, w_out, w_up, w_down)]
    gathered = {(0, 0): _gather_weights([own[0][:1]])[0]}

    def full(a, l, axis, transposed=False):
        blocks = [jnp.where(q_idx == q, own[a][l], gathered[(a, l)][q, 0]) for q in range(N_CHIPS)]
        return jnp.concatenate([b.T for b in blocks] if transposed else blocks, axis=axis)

    half_way = {}

    def gather_behind(call, keys, at_once):
        res, over_ici = call(_gather_over_ici([own[a][l:l + 1] for a, l in keys]))
        gathered.update(zip(keys[:at_once], _pass_to_sibling(over_ici[:at_once])))
        half_way.update(zip(keys[at_once:], over_ici[at_once:]))
        return res

    def pass_behind(call, keys):
        res, done = call(_pass_over_d2d([half_way.pop(k) for k in keys]))
        gathered.update(zip(keys, done))
        return res

    tril = jnp.tril(jnp.ones((CHUNK, CHUNK), F32))
    saved = []
    cur = xs
    wi, wo, wu, wd = {}, {}, {}, {}
    for l in range(depth):
        wbd = _block_diag(pool_w[l]).astype(BF16)
        wm = sg_w[l] * tril
        wm_s = wm.reshape(SG_HEADS * CHUNK, CHUNK).astype(BF16)
        wmt_s = jnp.swapaxes(wm, 1, 2).reshape(SG_HEADS * CHUNK, CHUNK).astype(BF16)
        bias = jnp.repeat(sg_b[l].T, SB_HD, axis=1)
        n1, n2 = norm1[l][None], norm2[l][None]
        psc, sgn = pool_scale[l][None], sg_norm[l][None]
        wi[l] = full(0, l, 1)
        proj, h, qkv = _inproj_fwd(cur, n1, wi[l])
        ya = _pool_fwd(proj, wbd, psc)
        yb = _sg_fwd(proj, wm_s, bias, sgn)
        if l == 0:
            yc = gather_behind(lambda hosted: _attn_fwd(qkv, hosted), [(1, 0), (2, 0), (3, 0)], 1)
            wo[l] = full(1, l, 0)
            x1, ymix = pass_behind(lambda hosted: _outproj_fwd(cur, ya, yb, yc, wo[l], hosted), [(2, 0), (3, 0)])
        else:
            yc = pass_behind(lambda hosted: _attn_fwd(qkv, hosted), [(1, l), (2, l), (3, l)])
            wo[l] = full(1, l, 0)
            (x1, ymix), _ = _outproj_fwd(cur, ya, yb, yc, wo[l])
        wu[l], wd[l] = full(2, l, 0, transposed=True), full(3, l, 0)
        if l == 0:
            x2, h2, u, act = gather_behind(lambda hosted: _mlp_fwd(x1, n2, wu[l], wd[l], hosted),
                                           [(0, 1), (1, 1), (2, 1), (3, 1)], 1)
        else:
            (x2, h2, u, act), _ = _mlp_fwd(x1, n2, wu[l], wd[l])
        saved.append(dict(x0=cur, x1=x1, proj=proj, h=h, qkv=qkv, yc=yc, ymix=ymix, h2=h2, u=u, act=act,
                          wbd=wbd, wm_s=wm_s, wmt_s=wmt_s, bias=bias, n1=n1, n2=n2, psc=psc, sgn=sgn))
        cur = x2

    loss_row, dcur, d_final = _loss_head(cur, final_norm[None], target)

    small = [None] * depth
    grads, parts, reduced = {}, {}, {}

    def pair_up(keys, swapped):
        parts.update({k: _add_pair(grads[k], o, c_idx) for k, o in zip(keys, swapped)})

    def chip_up(keys, arrived):
        reduced.update({k: _add_chips(parts[k], r, q_idx) for k, r in zip(keys, arrived)})

    for l in reversed(range(depth)):
        s = saved[l]
        if l == 0:
            keys = [(2, 1), (3, 1)]
            (dx1, du, d_n2), arrived = _mlp_bwd(dcur, s["x1"], s["n2"], s["u"], wu[l], wd[l],
                                                _scatter_over_ici([parts[k] for k in keys]))
            chip_up(keys, arrived)
        else:
            (dx1, du, d_n2), _ = _mlp_bwd(dcur, s["x1"], s["n2"], s["u"], wu[l], wd[l])
        grads[(2, l)] = _tn_matmul(s["h2"], du, "grad_w_up", n_split=N_CHIPS)
        grads[(3, l)] = _tn_matmul(s["act"], dcur, "grad_w_down")[0].reshape(N_CHIPS, D_FF // N_CHIPS, D_MODEL)
        dymix = _nt_matmul(dx1, wo[l])
        grads[(1, l)] = _tn_matmul(s["ymix"], dx1, "grad_w_out")[0].reshape(N_CHIPS, D_MODEL // N_CHIPS, D_MODEL)
        da_in, d_wbd, d_psc = _pool_bwd(s["proj"], dymix, s["wbd"], s["psc"])
        if l == 0:
            keys = [(0, 1), (1, 0), (2, 0), (3, 0)]
            (du_pre, dv_pre, d_wm, d_bias, d_sgn), swapped = _sg_bwd(
                s["proj"], dymix, s["wm_s"], s["wmt_s"], s["bias"], s["sgn"], _swap_over_d2d([grads[k] for k in keys]))
            pair_up(keys, swapped)
            keys = [(1, 1)] + keys
            (dq, dk, dv), arrived = _attn_bwd(s["qkv"], s["yc"], dymix, _scatter_over_ici([parts[k] for k in keys]))
            chip_up(keys, arrived)
        else:
            (du_pre, dv_pre, d_wm, d_bias, d_sgn), _ = _sg_bwd(s["proj"], dymix, s["wm_s"], s["wmt_s"], s["bias"], s["sgn"])
            keys = [(1, l), (2, l), (3, l)]
            (dq, dk, dv), swapped = _attn_bwd(s["qkv"], s["yc"], dymix, _swap_over_d2d([grads[k] for k in keys]))
            pair_up(keys, swapped)
        dproj, dx0, d_n1 = _inproj_bwd([da_in, du_pre, dv_pre, dq, dk, dv], wi[l], s["x0"], s["n1"], dx1)
        if l == 0:
            keys = sorted(reduced)
            g_in_l, swapped = _tn_matmul(s["h"], dproj, "grad_w_in",
                                         hosted=_swap_reduced_over_d2d([reduced[k] for k in keys]))
            theirs = dict(zip(keys, swapped))
        else:
            g_in_l = _tn_matmul(s["h"], dproj, "grad_w_in")
        grads[(0, l)] = g_in_l[0].reshape(D_MODEL, N_CHIPS, IN_COLS // N_CHIPS).transpose(1, 0, 2)
        d_pw = jnp.stack([d_wbd[gi * POOL_GW:(gi + 1) * POOL_GW, gi * POOL_GW:(gi + 1) * POOL_GW]
                          for gi in range(len(POOL_WINDOWS))])
        small[l] = dict(norm1=d_n1[0], pool_w=d_pw, pool_scale=d_psc[0], sg_norm=d_sgn[0],
                        sg_w=d_wm.reshape(SG_HEADS, CHUNK, CHUNK), sg_b=d_bias[:, :SG_HEADS].T, norm2=d_n2[0])
        dcur = dx0
    grad_x = dcur.reshape(x.shape)

    names = ["norm1", "pool_w", "pool_scale", "sg_norm", "sg_w", "sg_b", "norm2"]
    slot = jnp.zeros((1,), F32)
    small_w = [norm1, pool_w, pool_scale, sg_norm, sg_w, sg_b, norm2, final_norm, slot]
    small_m = [m_norm1, m_pool_w, m_pool_scale, m_sg_norm, m_sg_w, m_sg_b, m_norm2, m_final_norm, slot]
    small_v = [v_norm1, v_pool_w, v_pool_scale, v_sg_norm, v_sg_w, v_sg_b, v_norm2, v_final_norm, slot]
    small_g = [jnp.stack([small[l][k] for l in range(depth)]) for k in names] + [d_final[0], loss_row[0, :1]]
    keys = [(0, 0)]
    pair_up(keys, _swap_halves([grads[k] for k in keys]))
    g_packed, arrived = _allreduce_small(_pack(small_g), _scatter_over_ici([parts[k] for k in keys]))
    chip_up(keys, arrived)
    theirs.update(zip(keys, _swap_reduced([reduced[k] for k in keys])))

    def joined(a):
        layers = []
        for l in range(depth):
            mine, other = reduced[(a, l)], theirs[(a, l)]
            layers.append(jnp.where(c_idx == 0, jnp.concatenate([mine, other]), jnp.concatenate([other, mine])))
        return jnp.stack(layers)

    gw_in, gw_out, gw_up, gw_down = [joined(a) for a in range(4)]

    loss = _unpack(g_packed, small_w)[-1][0]
    s_delta, s_m, s_v = _elementwise(_adamw, "adamw_small", [_pack(small_w), g_packed, _pack(small_m), _pack(small_v)], 3)
    gs = dict(zip(names + ["final_norm"], _unpack(g_packed, small_w)))
    ds = dict(zip(names + ["final_norm"], _unpack(s_delta, small_w)))
    ms = dict(zip(names + ["final_norm"], _unpack(s_m, small_w)))
    vs = dict(zip(names + ["final_norm"], _unpack(s_v, small_w)))

    big_g = dict(w_in=gw_in, w_out=gw_out, w_up=gw_up, w_down=gw_down)
    big_w = dict(w_in=(w_in, m_w_in, v_w_in), w_out=(w_out, m_w_out, v_w_out),
                 w_up=(w_up, m_w_up, v_w_up), w_down=(w_down, m_w_down, v_w_down))
    for k, (w, m, v) in big_w.items():
        operands = [w, big_g[k], m, v]
        if k == "w_in":
            operands = [jnp.swapaxes(o, 1, 2) for o in operands]
        ds[k], ms[k], vs[k] = _elementwise(_adamw, "adamw_" + k, operands, 3)
        if k == "w_in":
            ds[k], ms[k], vs[k] = [jnp.swapaxes(o, 1, 2) for o in (ds[k], ms[k], vs[k])]
        gs[k] = big_g[k]

    order = ["norm1", "w_in", "pool_w", "pool_scale", "sg_norm", "sg_w", "sg_b", "w_out", "norm2", "w_up", "w_down",
             "final_norm"]
    return (loss, grad_x, *[gs[k] for k in order], *[ds[k] for k in order], *[ms[k] for k in order],
            *[vs[k] for k in order])
```

```python
import functools

import jax
import jax.numpy as jnp
from jax import lax
from jax.experimental import pallas as pl
from jax.experimental.pallas import tpu as pltpu

F32 = jnp.float32
BF16 = jnp.bfloat16
MESH = pl.DeviceIdType.MESH
AXES = ("x", "y", "c")

EPS = 1e-6
D_MODEL = 1024
POOL_WIDTH = 256
SG_WIDTH = 256
SB_WIDTH = 512
POOL_WINDOWS = (2, 4, 8, 16)
POOL_GW = 64
POOL_HALO = 16
CHUNK = 128
SG_HEADS = 4
SB_HD = 64
SB_SCALE = 0.125
IN_COLS = 2304
QKV_OFF = 768
D_FF = 4096
N_CHIPS = 4
LANES = 128
VMEM_LIMIT = 56 * 1024 * 1024
MLP_CHUNK = 512
ATTN_TILE = 256
UNDERFLOW = -104.0

ADAM_LR = 0.001
ADAM_B1 = 0.9
ADAM_B2 = 0.999
ADAM_EPS = 1e-08
ADAM_WD = 0.01
ADAM_STEP = 10

HBM_SPEC = pl.BlockSpec(memory_space=pl.ANY)
VMEM_SPEC = pl.BlockSpec(memory_space=pltpu.VMEM)


def _params(**kw):
    return pltpu.CompilerParams(vmem_limit_bytes=VMEM_LIMIT, **kw)


def _tile(n, pref):
    if n <= pref:
        return n
    for t in range(pref - pref % LANES, 0, -LANES):
        if n % t == 0:
            return t
    raise ValueError((n, pref))


def _nn(a, b):
    return jnp.dot(a, b, preferred_element_type=F32)


def _nt(a, b):
    return lax.dot_general(a, b, (((1,), (1,)), ((), ())), preferred_element_type=F32)


def _tn(a, b):
    return lax.dot_general(a, b, (((0,), (0,)), ((), ())), preferred_element_type=F32)


def _rms_fwd(x, g):
    r = lax.rsqrt(jnp.mean(x * x, axis=-1, keepdims=True) + EPS)
    xhat = x * r
    return xhat * g, xhat, r


def _rms_bwd(dy, xhat, r, g):
    dxhat = dy * g
    dx = r * (dxhat - xhat * jnp.mean(dxhat * xhat, axis=-1, keepdims=True))
    return dx, dy * xhat


_GELU_K = 0.7978845608028654
_GELU_C = 0.044715


def _gelu(x):
    return 0.5 * x * (1.0 + jnp.tanh(_GELU_K * (x + _GELU_C * x * x * x)))


def _gelu_grad(x):
    t = jnp.tanh(_GELU_K * (x + _GELU_C * x * x * x))
    return 0.5 * (1.0 + t) + 0.5 * x * (1.0 - t * t) * _GELU_K * (1.0 + 3.0 * _GELU_C * x * x)


def _inproj_fwd(x, g, w):
    T, D = x.shape
    N = w.shape[1]
    tt = _tile(T, 512)

    def body(x_ref, g_ref, w_ref, proj_ref, h_ref, qkv_ref):
        h, _, _ = _rms_fwd(x_ref[...], g_ref[...])
        hb = h.astype(BF16)
        h_ref[...] = hb
        p = _nn(hb, w_ref[...])
        proj_ref[...] = p[:, :QKV_OFF]
        qkv_ref[...] = p[:, QKV_OFF:].astype(BF16)

    return pl.pallas_call(
        body, name="inproj_fwd", grid=(T // tt,),
        in_specs=[pl.BlockSpec((tt, D), lambda i: (i, 0)), pl.BlockSpec((1, D), lambda i: (0, 0)),
                  pl.BlockSpec((D, N), lambda i: (0, 0))],
        out_specs=[pl.BlockSpec((tt, QKV_OFF), lambda i: (i, 0)), pl.BlockSpec((tt, D), lambda i: (i, 0)),
                   pl.BlockSpec((tt, N - QKV_OFF), lambda i: (i, 0))],
        out_shape=[jax.ShapeDtypeStruct((T, QKV_OFF), F32), jax.ShapeDtypeStruct((T, D), BF16),
                   jax.ShapeDtypeStruct((T, N - QKV_OFF), BF16)],
        compiler_params=_params(),
    )(x, g, w)


def _inproj_bwd(pieces, w, x, g, dres):
    T, D = x.shape
    N = w.shape[1]
    tt = _tile(T, 512)
    widths = [p.shape[1] for p in pieces]
    offs = [sum(widths[:k]) for k in range(len(widths))]
    assert sum(widths) == N
    n_p = len(pieces)

    def body(*refs):
        p_refs = refs[:n_p]
        w_ref, x_ref, g_ref, dres_ref, dproj_ref, dx_ref, dg_ref = refs[n_p:]
        for p_ref, o, wd in zip(p_refs, offs, widths):
            dproj_ref[:, o:o + wd] = p_ref[...].astype(BF16)
        dh = _nt(dproj_ref[...], w_ref[...])
        gv = g_ref[...]
        _, xhat, r = _rms_fwd(x_ref[...], gv)
        dx, dgrow = _rms_bwd(dh, xhat, r, gv)
        dx_ref[...] = dres_ref[...] + dx

        @pl.when(pl.program_id(0) == 0)
        def _():
            dg_ref[...] = jnp.zeros_like(dg_ref)

        dg_ref[...] += jnp.sum(dgrow, axis=0, keepdims=True)

    return pl.pallas_call(
        body, name="inproj_bwd", grid=(T // tt,),
        in_specs=[pl.BlockSpec((tt, wd), lambda i: (i, 0)) for wd in widths] + [
            pl.BlockSpec((D, N), lambda i: (0, 0)), pl.BlockSpec((tt, D), lambda i: (i, 0)),
            pl.BlockSpec((1, D), lambda i: (0, 0)), pl.BlockSpec((tt, D), lambda i: (i, 0))],
        out_specs=[pl.BlockSpec((tt, N), lambda i: (i, 0)), pl.BlockSpec((tt, D), lambda i: (i, 0)),
                   pl.BlockSpec((1, D), lambda i: (0, 0))],
        out_shape=[jax.ShapeDtypeStruct((T, N), BF16), jax.ShapeDtypeStruct((T, D), F32),
                   jax.ShapeDtypeStruct((1, D), F32)],
        compiler_params=_params(),
    )(*pieces, w, x, g, dres)


def _pool_select(s2, s4, s8, s16, grp):
    return jnp.where(grp == 0, s2, jnp.where(grp == 1, s4, jnp.where(grp == 2, s8, s16)))


def _pool_count(t_glob, grp):
    win = jnp.where(grp == 0, 2, jnp.where(grp == 1, 4, jnp.where(grp == 2, 8, 16)))
    return jnp.minimum(t_glob + 1, win).astype(F32)


def _pool_diff(a, halo, base, tt):
    n = tt + POOL_HALO
    ext = jnp.concatenate([halo, a], axis=0)
    s2 = ext + pltpu.roll(ext, 1, 0)
    s4 = s2 + pltpu.roll(s2, 2, 0)
    s8 = s4 + pltpu.roll(s4, 4, 0)
    s16 = s8 + pltpu.roll(s8, 8, 0)
    grp = lax.broadcasted_iota(jnp.int32, (n, POOL_WIDTH), 1) // POOL_GW
    t_glob = lax.broadcasted_iota(jnp.int32, (n, POOL_WIDTH), 0) + (base - POOL_HALO)
    pooled = _pool_select(s2, s4, s8, s16, grp) / _pool_count(t_glob, grp)
    return pooled[POOL_HALO:] - a


def _pool_specs(T, tt):
    hb = tt // POOL_HALO
    return [pl.BlockSpec((tt, POOL_WIDTH), lambda i: (i, 0)),
            pl.BlockSpec((POOL_HALO, POOL_WIDTH), lambda i: (jnp.maximum(i * hb - 1, 0), 0))]


def _pool_fwd(proj, wbd, scale):
    T = proj.shape[0]
    tt = _tile(T, 512)

    def body(a_ref, halo_ref, w_ref, sc_ref, y_ref):
        i = pl.program_id(0)
        halo = jnp.where(i > 0, halo_ref[...], 0.0)
        d = _pool_diff(a_ref[...], halo, i * tt, tt)
        y_ref[...] = _nn(d.astype(BF16), w_ref[...]) * sc_ref[...]

    return pl.pallas_call(
        body, name="pool_fwd", grid=(T // tt,),
        in_specs=_pool_specs(T, tt) + [pl.BlockSpec((POOL_WIDTH, POOL_WIDTH), lambda i: (0, 0)),
                                       pl.BlockSpec((1, POOL_WIDTH), lambda i: (0, 0))],
        out_specs=pl.BlockSpec((tt, POOL_WIDTH), lambda i: (i, 0)),
        out_shape=jax.ShapeDtypeStruct((T, POOL_WIDTH), F32),
        compiler_params=_params(),
    )(proj, proj, wbd, scale)


def _pool_bwd(proj, dymix, wbd, scale):
    T = proj.shape[0]
    tt = _tile(T, 512)
    hb = tt // POOL_HALO
    nblk = T // tt
    n = tt + POOL_HALO

    def body(a_ref, halo_ref, dy_ref, dyn_ref, w_ref, sc_ref, da_ref, dw_ref, dsc_ref):
        i = pl.program_id(0)
        halo = jnp.where(i > 0, halo_ref[...], 0.0)
        d = _pool_diff(a_ref[...], halo, i * tt, tt)
        db = d.astype(BF16)
        wv = w_ref[...]
        sc = sc_ref[...]
        dy = dy_ref[...]
        dys = dy * sc

        @pl.when(i == 0)
        def _():
            dw_ref[...] = jnp.zeros_like(dw_ref)
            dsc_ref[...] = jnp.zeros_like(dsc_ref)

        dsc_ref[...] += jnp.sum(dy * _nn(db, wv), axis=0, keepdims=True)
        dw_ref[...] += _tn(db, dys.astype(BF16))
        dyn = jnp.where(i < nblk - 1, dyn_ref[...], 0.0) * sc
        dd = _nt(jnp.concatenate([dys, dyn], axis=0).astype(BF16), wv)
        grp = lax.broadcasted_iota(jnp.int32, (n, POOL_WIDTH), 1) // POOL_GW
        t_glob = lax.broadcasted_iota(jnp.int32, (n, POOL_WIDTH), 0) + i * tt
        e = dd / _pool_count(t_glob, grp)
        r2 = e + pltpu.roll(e, n - 1, 0)
        r4 = r2 + pltpu.roll(r2, n - 2, 0)
        r8 = r4 + pltpu.roll(r4, n - 4, 0)
        r16 = r8 + pltpu.roll(r8, n - 8, 0)
        da_ref[...] = (_pool_select(r2, r4, r8, r16, grp) - dd)[:tt].astype(BF16)

    return pl.pallas_call(
        body, name="pool_bwd", grid=(nblk,),
        in_specs=_pool_specs(T, tt) + [
            pl.BlockSpec((tt, POOL_WIDTH), lambda i: (i, 0)),
            pl.BlockSpec((POOL_HALO, POOL_WIDTH), lambda i: (jnp.minimum((i + 1) * hb, T // POOL_HALO - 1), 0)),
            pl.BlockSpec((POOL_WIDTH, POOL_WIDTH), lambda i: (0, 0)), pl.BlockSpec((1, POOL_WIDTH), lambda i: (0, 0))],
        out_specs=[pl.BlockSpec((tt, POOL_WIDTH), lambda i: (i, 0)),
                   pl.BlockSpec((POOL_WIDTH, POOL_WIDTH), lambda i: (0, 0)),
                   pl.BlockSpec((1, POOL_WIDTH), lambda i: (0, 0))],
        out_shape=[jax.ShapeDtypeStruct((T, POOL_WIDTH), BF16),
                   jax.ShapeDtypeStruct((POOL_WIDTH, POOL_WIDTH), F32),
                   jax.ShapeDtypeStruct((1, POOL_WIDTH), F32)],
        compiler_params=_params(),
    )(proj, proj, dymix, dymix, wbd, scale)


def _head_select(stacked, grp):
    out = jnp.where(grp == 0, stacked[0:CHUNK], 0.0)
    for h in range(1, SG_HEADS):
        out = out + jnp.where(grp == h, stacked[h * CHUNK:(h + 1) * CHUNK], 0.0)
    return out


def _sg_specs(tt):
    return [pl.BlockSpec((tt, SG_WIDTH), lambda i: (i, 1)), pl.BlockSpec((tt, SG_WIDTH), lambda i: (i, 2))]


def _sg_fwd(proj, wm, bias, g):
    T = proj.shape[0]
    tt = _tile(T, 512)

    def body(u_ref, v_ref, wm_ref, b_ref, g_ref, y_ref):
        zu = _gelu(u_ref[...])
        vn, _, _ = _rms_fwd(_gelu(v_ref[...]), g_ref[...])
        grp = lax.broadcasted_iota(jnp.int32, (CHUNK, SG_WIDTH), 1) // SB_HD
        for n in range(tt // CHUNK):
            rows = slice(n * CHUNK, (n + 1) * CHUNK)
            sv = _head_select(_nn(wm_ref[...], vn[rows].astype(BF16)), grp) + b_ref[...]
            y_ref[rows, :] = zu[rows] * sv

    return pl.pallas_call(
        body, name="sg_fwd", grid=(T // tt,),
        in_specs=_sg_specs(tt) + [pl.BlockSpec((SG_HEADS * CHUNK, CHUNK), lambda i: (0, 0)),
                                  pl.BlockSpec((CHUNK, SG_WIDTH), lambda i: (0, 0)),
                                  pl.BlockSpec((1, SG_WIDTH), lambda i: (0, 0))],
        out_specs=pl.BlockSpec((tt, SG_WIDTH), lambda i: (i, 0)),
        out_shape=jax.ShapeDtypeStruct((T, SG_WIDTH), F32),
        compiler_params=_params(),
    )(proj, proj, wm, bias, g)


def _sg_bwd(proj, dymix, wm, wmt, bias, g, hosted=None):
    T = proj.shape[0]
    tt = _tile(T, 512)
    nblk = T // tt

    def body(*refs):
        i = pl.program_id(0)
        (u_ref, v_ref, dy_ref, wm_ref, wmt_ref, b_ref, g_ref, du_ref, dv_ref, dw_ref, db_ref, dg_ref,
         dvn_ref, dbias_ref), start, wait = _host(hosted, refs, 7, 5, i == 0, i == nblk - 1)
        start()
        up, vp = u_ref[...], v_ref[...]
        gv = g_ref[...]
        zu, zv = _gelu(up), _gelu(vp)
        vn, xhat, r = _rms_fwd(zv, gv)
        gu = _gelu_grad(up)
        grp = lax.broadcasted_iota(jnp.int32, (CHUNK, SG_WIDTH), 1) // SB_HD

        @pl.when(i == 0)
        def _():
            dw_ref[...] = jnp.zeros_like(dw_ref)
            dbias_ref[...] = jnp.zeros_like(dbias_ref)
            dg_ref[...] = jnp.zeros_like(dg_ref)

        for n in range(tt // CHUNK):
            rows = slice(n * CHUNK, (n + 1) * CHUNK)
            vc = vn[rows].astype(BF16)
            sv = _head_select(_nn(wm_ref[...], vc), grp) + b_ref[...]
            dy = dy_ref[rows, :]
            du_ref[rows, :] = (dy * sv * gu[rows]).astype(BF16)
            dsv = dy * zu[rows]
            dsvb = dsv.astype(BF16)
            dvn_ref[rows, :] = _head_select(_nn(wmt_ref[...], dsvb), grp)
            stacked = jnp.concatenate([jnp.where(grp == h, dsv, 0.0) for h in range(SG_HEADS)], axis=0)
            dw_ref[...] += _nt(stacked.astype(BF16), vc)
            dbias_ref[...] += dsv

        dzv, dgrow = _rms_bwd(dvn_ref[...], xhat, r, gv)
        dg_ref[...] += jnp.sum(dgrow, axis=0, keepdims=True)
        dv_ref[...] = (dzv * _gelu_grad(vp)).astype(BF16)

        @pl.when(i == nblk - 1)
        def _():
            t_i = lax.broadcasted_iota(jnp.int32, (SG_HEADS * CHUNK, CHUNK), 0) % CHUNK
            s_i = lax.broadcasted_iota(jnp.int32, (SG_HEADS * CHUNK, CHUNK), 1)
            dw_ref[...] = jnp.where(s_i <= t_i, dw_ref[...], 0.0)
            lane = lax.broadcasted_iota(jnp.int32, (CHUNK, LANES), 1)
            acc = jnp.zeros((CHUNK, LANES), F32)
            for h in range(SG_HEADS):
                tot = jnp.sum(jnp.where(grp == h, dbias_ref[...], 0.0), axis=1, keepdims=True)
                acc = acc + jnp.where(lane == h, tot, 0.0)
            db_ref[...] = acc

        wait()

    h_ins = hosted.ins if hosted else []
    res = pl.pallas_call(
        body, name="sg_bwd_hosting" if hosted else "sg_bwd", grid=(nblk,),
        in_specs=_sg_specs(tt) + [pl.BlockSpec((tt, SG_WIDTH), lambda i: (i, 1)),
                                  pl.BlockSpec((SG_HEADS * CHUNK, CHUNK), lambda i: (0, 0)),
                                  pl.BlockSpec((SG_HEADS * CHUNK, CHUNK), lambda i: (0, 0)),
                                  pl.BlockSpec((CHUNK, SG_WIDTH), lambda i: (0, 0)),
                                  pl.BlockSpec((1, SG_WIDTH), lambda i: (0, 0))] + [HBM_SPEC] * len(h_ins),
        out_specs=[pl.BlockSpec((tt, SG_WIDTH), lambda i: (i, 0)), pl.BlockSpec((tt, SG_WIDTH), lambda i: (i, 0)),
                   pl.BlockSpec((SG_HEADS * CHUNK, CHUNK), lambda i: (0, 0)),
                   pl.BlockSpec((CHUNK, LANES), lambda i: (0, 0)), pl.BlockSpec((1, SG_WIDTH), lambda i: (0, 0))]
        + [HBM_SPEC] * len(h_ins),
        out_shape=[jax.ShapeDtypeStruct((T, SG_WIDTH), BF16), jax.ShapeDtypeStruct((T, SG_WIDTH), BF16),
                   jax.ShapeDtypeStruct((SG_HEADS * CHUNK, CHUNK), F32),
                   jax.ShapeDtypeStruct((CHUNK, LANES), F32), jax.ShapeDtypeStruct((1, SG_WIDTH), F32)]
        + (hosted.out_shapes if hosted else []),
        scratch_shapes=[pltpu.VMEM((tt, SG_WIDTH), F32), pltpu.VMEM((CHUNK, SG_WIDTH), F32)]
        + (hosted.sems() if hosted else []),
        compiler_params=_params(has_side_effects=hosted is not None),
    )(proj, proj, dymix, wm, wmt, bias, g, *h_ins)
    return res[:5], res[5:]


def _split_dot(x, u):
    hi = x.astype(BF16)
    lo = (x - hi.astype(F32)).astype(BF16)
    return _nn(hi, u) + _nn(lo, u)


def _sb_logits(z):
    lb = jnp.minimum(z, 0.0) - jnp.log(1.0 + jnp.exp(-jnp.abs(z)))
    return lb, lb - z


ATTN_STRIP = 32
ATTN_SUBS = 2


def _by_strips(n_rows, fn):
    parts = None
    for r in range(0, n_rows, ATTN_STRIP):
        res = fn(slice(r, r + ATTN_STRIP))
        parts = [[v] for v in res] if parts is None else [p + [v] for p, v in zip(parts, res)]
    return [jnp.concatenate(p, axis=0) for p in parts]


def _attn_qkv_specs(tq, T):
    base = (IN_COLS - 3 * SB_WIDTH - QKV_OFF) // LANES
    nb = SB_WIDTH // LANES
    return [pl.BlockSpec((tq, LANES), lambda p, i: (i, base + p)),
            pl.BlockSpec((T, LANES), lambda p, i: (0, base + nb + p)),
            pl.BlockSpec((T, LANES), lambda p, i: (0, base + 2 * nb + p))]


class _Hosted:
    def __init__(self, ins, out_shapes, n_sems, copies, in_place=False):
        self.ins, self.out_shapes, self.n_sems, self.copies = ins, out_shapes, n_sems, copies
        self.in_place = in_place

    @property
    def n(self):
        return len(self.ins)

    def aliases(self, n_in, n_out):
        return {n_in + k: n_out + k for k in range(self.n)} if self.in_place else {}

    def sems(self):
        return [pltpu.SemaphoreType.DMA((self.n_sems,)), pltpu.SemaphoreType.DMA((self.n_sems,))]

    def start(self, src, dst, ssem, rsem):
        for send, _ in self.copies(src, dst, ssem, rsem):
            send.start()

    def wait(self, src, dst, ssem, rsem):
        for send, recv in self.copies(src, dst, ssem, rsem):
            recv.wait_recv()
            send.wait_send()


def _host(hosted, refs, n_in, n_out, first, last):
    if hosted is None:
        return refs, lambda: None, lambda: None
    n = hosted.n
    own_in, h_in = refs[:n_in], refs[n_in:n_in + n]
    own_out, h_out = refs[n_in + n:n_in + n + n_out], refs[n_in + n + n_out:n_in + 2 * n + n_out]
    rest = refs[n_in + 2 * n + n_out:]
    ssem, rsem = rest[-2:]

    def start():
        if first is True:
            hosted.start(h_in, h_out, ssem, rsem)
        else:
            pl.when(first)(lambda: hosted.start(h_in, h_out, ssem, rsem))

    def wait():
        if last is True:
            hosted.wait(h_in, h_out, ssem, rsem)
        else:
            pl.when(last)(lambda: hosted.wait(h_in, h_out, ssem, rsem))

    return own_in + own_out + rest[:-2], start, wait


def _attn_fwd(qkv, hosted=None):
    T = qkv.shape[0]
    tk = _tile(T, ATTN_TILE)
    n_sub = ATTN_SUBS if T % (ATTN_SUBS * tk) == 0 else 1
    tq = n_sub * tk
    n_p, nq = SB_WIDTH // LANES, T // tq

    def body(*refs):
        p, i = pl.program_id(0), pl.program_id(1)
        (q_ref, k_ref, v_ref, o_ref), start, wait = _host(
            hosted, refs, 3, 1, jnp.logical_and(p == 0, i == 0), jnp.logical_and(p == n_p - 1, i == nq - 1))
        start()
        lane = lax.broadcasted_iota(jnp.int32, (tk, LANES), 1)
        row = lax.broadcasted_iota(jnp.int32, (tk, tk), 0)
        col = lax.broadcasted_iota(jnp.int32, (tk, tk), 1)
        after = jnp.where(row > col, 1.0, 0.0).astype(BF16)
        valid = col < row
        qh = {}
        for sb in range(n_sub):
            q = q_ref[sb * tk:(sb + 1) * tk, :].astype(F32)
            for hh in range(2):
                qh[(sb, hh)] = jnp.where((lane // SB_HD) == hh, q * SB_SCALE, 0.0).astype(BF16)

        def tiles(todo, state):
            chains = [(n, hh) for n in range(len(todo)) for hh in range(2)]
            kv = []
            for _, j, _ in todo:
                ks = pl.ds(pl.multiple_of(j * tk, tk), tk)
                kv.append((k_ref[ks, :], v_ref[ks, :]))
            z = {(n, hh): _nt(qh[(todo[n][0], hh)], kv[n][0]) for n, hh in chains}
            lb, lmb, lm_sum = {}, {}, {}
            for n, hh in chains:
                def logits(rows, z=z[(n, hh)], mask=todo[n][2]):
                    lb, lm = _sb_logits(z[rows])
                    if mask is not None:
                        lm = jnp.where(mask[rows], lm, 0.0)
                    return lb, lm.astype(BF16), jnp.sum(lm, axis=1, keepdims=True)

                lb[(n, hh)], lmb[(n, hh)], lm_sum[(n, hh)] = _by_strips(tk, logits)
            x = {c: _nn(lmb[c], after) for c in chains}
            new = dict(state)
            for n, hh in chains:
                key = (todo[n][0], hh)
                carry, acc = new[key]

                def weights(rows, lb=lb[(n, hh)], x=x[(n, hh)], carry=carry, mask=todo[n][2]):
                    a = jnp.exp(lb[rows] + x[rows] + carry[rows])
                    if mask is not None:
                        a = jnp.where(mask[rows], a, 0.0)
                    return (a.astype(BF16),)

                (ab,) = _by_strips(tk, weights)
                new[key] = (carry + lm_sum[(n, hh)], acc + _nn(ab, kv[n][1]))
            return new

        def live(state, sb):
            return jnp.maximum(jnp.max(state[(sb, 0)][0]), jnp.max(state[(sb, 1)][0]))

        first = n_sub * i
        zero = (jnp.zeros((tk, 1), F32), jnp.zeros((tk, LANES), F32))
        todo = []
        for sb in range(n_sub):
            gate = jnp.broadcast_to(first > 0, (tk, tk)) if sb == 0 else None
            todo += [(sb, first + sb, valid), (sb, jnp.maximum(first + sb - 1, 0), gate)]
        state = tiles(todo, {(sb, hh): zero for sb in range(n_sub) for hh in range(2)})
        for sb in range(n_sub):
            def cond(st):
                return jnp.logical_and(st[0] >= 0, st[2] > UNDERFLOW)

            def step(st, sb=sb):
                mine = tiles([(sb, st[0], None)], st[1])
                return st[0] - 1, mine, live(mine, sb)

            mine = {k: v for k, v in state.items() if k[0] == sb}
            _, mine, _ = lax.while_loop(cond, step, (first + sb - 2, mine, live(mine, sb)))
            o_ref[sb * tk:(sb + 1) * tk, :] = jnp.where(lane < SB_HD, mine[(sb, 0)][1], mine[(sb, 1)][1])
        wait()

    h_ins = hosted.ins if hosted else []
    res = pl.pallas_call(
        body, name="attn_fwd_hosting" if hosted else "attn_fwd", grid=(n_p, nq),
        in_specs=_attn_qkv_specs(tq, T) + [HBM_SPEC] * len(h_ins),
        out_specs=[pl.BlockSpec((tq, LANES), lambda p, i: (i, p))] + [HBM_SPEC] * len(h_ins),
        out_shape=[jax.ShapeDtypeStruct((T, SB_WIDTH), F32)] + (hosted.out_shapes if hosted else []),
        input_output_aliases=hosted.aliases(3, 1) if hosted else {},
        scratch_shapes=hosted.sems() if hosted else [],
        compiler_params=_params(has_side_effects=hosted is not None),
    )(qkv, qkv, qkv, *h_ins)
    return res[0], res[1:]


def _attn_bwd(qkv, o, dymix, hosted=None):
    T = qkv.shape[0]
    tk = _tile(T, ATTN_TILE)
    n_sub = ATTN_SUBS if T % (ATTN_SUBS * tk) == 0 else 1
    tq = n_sub * tk
    n_p, nq = SB_WIDTH // LANES, T // tq
    yc_blk = (POOL_WIDTH + SG_WIDTH) // LANES

    def body(*refs):
        p, i = pl.program_id(0), pl.program_id(1)
        (q_ref, k_ref, v_ref, o_ref, do_ref, dq_ref, dk_ref, dv_ref), start, wait = _host(
            hosted, refs, 5, 3, jnp.logical_and(p == 0, i == 0), jnp.logical_and(p == n_p - 1, i == nq - 1))
        start()
        lane = lax.broadcasted_iota(jnp.int32, (tk, LANES), 1)
        row = lax.broadcasted_iota(jnp.int32, (tk, tk), 0)
        col = lax.broadcasted_iota(jnp.int32, (tk, tk), 1)
        after = jnp.where(row > col, 1.0, 0.0).astype(BF16)
        from_here = jnp.where(row >= col, 1.0, 0.0).astype(BF16)
        from_here2 = jnp.concatenate([from_here, from_here], axis=0)
        valid = col < row

        @pl.when(i == 0)
        def _():
            dk_ref[...] = jnp.zeros_like(dk_ref)
            dv_ref[...] = jnp.zeros_like(dv_ref)

        qh, dohb, delta = {}, {}, {}
        for sb in range(n_sub):
            rows = slice(sb * tk, (sb + 1) * tk)
            q, ov, dov = q_ref[rows, :].astype(F32), o_ref[rows, :], do_ref[rows, :]
            for hh in range(2):
                head = (lane // SB_HD) == hh
                qh[(sb, hh)] = jnp.where(head, q * SB_SCALE, 0.0).astype(BF16)
                dohb[(sb, hh)] = jnp.where(head, dov, 0.0).astype(BF16)
                delta[(sb, hh)] = jnp.sum(dohb[(sb, hh)].astype(F32) * ov, axis=1, keepdims=True)

        def tiles(todo, state):
            chains = [(n, hh) for n in range(len(todo)) for hh in range(2)]
            kv, where = [], []
            for _, j, _ in todo:
                ks = pl.ds(pl.multiple_of(j * tk, tk), tk)
                where.append(ks)
                kv.append((k_ref[ks, :], v_ref[ks, :]))
            z = {(n, hh): _nt(qh[(todo[n][0], hh)], kv[n][0]) for n, hh in chains}
            da = {(n, hh): _nt(dohb[(todo[n][0], hh)], kv[n][1]) for n, hh in chains}
            lb, lmb, lm_sum = {}, {}, {}
            for n, hh in chains:
                def logits(rows, z=z[(n, hh)], mask=todo[n][2]):
                    lb, lm = _sb_logits(z[rows])
                    if mask is not None:
                        lm = jnp.where(mask[rows], lm, 0.0)
                    return lb, lm.astype(BF16), jnp.sum(lm, axis=1, keepdims=True)

                lb[(n, hh)], lmb[(n, hh)], lm_sum[(n, hh)] = _by_strips(tk, logits)
            x = {c: _nn(lmb[c], after) for c in chains}
            c_a = {k: v[0] for k, v in state.items()}
            ab, g, g_split, g_sum = {}, {}, {}, {}
            for n, hh in chains:
                key = (todo[n][0], hh)

                def weights(rows, lb=lb[(n, hh)], x=x[(n, hh)], da=da[(n, hh)], c_a=c_a[key], mask=todo[n][2]):
                    a = jnp.exp(lb[rows] + x[rows] + c_a[rows])
                    if mask is not None:
                        a = jnp.where(mask[rows], a, 0.0)
                    ab = a.astype(BF16)
                    g = da[rows] * ab.astype(F32)
                    hi = g.astype(BF16)
                    lo = (g - hi.astype(F32)).astype(BF16)
                    return ab, g, jnp.concatenate([hi, lo], axis=1), jnp.sum(g, axis=1, keepdims=True)

                ab[(n, hh)], g[(n, hh)], g_split[(n, hh)], g_sum[(n, hh)] = _by_strips(tk, weights)
                c_a[key] = c_a[key] + lm_sum[(n, hh)]
            right = {c: _nn(g_split[c], from_here2) for c in chains}
            c_r = {k: v[1] for k, v in state.items()}
            dzb = {}
            for n, hh in chains:
                key = (todo[n][0], hh)

                def logit_grads(rows, lb=lb[(n, hh)], g=g[(n, hh)], right=right[(n, hh)], c_r=c_r[key],
                                delta=delta[key], mask=todo[n][2]):
                    sig = jnp.exp(lb[rows])
                    left = delta[rows] - (c_r[rows] + right[rows])
                    dz = g[rows] * (1.0 - sig) - left * sig
                    if mask is not None:
                        dz = jnp.where(mask[rows], dz, 0.0)
                    return (dz.astype(BF16),)

                (dzb[(n, hh)],) = _by_strips(tk, logit_grads)
                c_r[key] = c_r[key] + g_sum[(n, hh)]
            dqa = {k: v[2] for k, v in state.items()}
            for n in range(len(todo)):
                sb = todo[n][0]
                dk_ref[where[n], :] += _tn(dzb[(n, 0)], qh[(sb, 0)]) + _tn(dzb[(n, 1)], qh[(sb, 1)])
                dv_ref[where[n], :] += _tn(ab[(n, 0)], dohb[(sb, 0)]) + _tn(ab[(n, 1)], dohb[(sb, 1)])
                for hh in range(2):
                    dqa[(sb, hh)] = dqa[(sb, hh)] + _nn(dzb[(n, hh)], kv[n][0])
            return {k: (c_a[k], c_r[k], dqa[k]) for k in state}

        def live(state, sb):
            return jnp.maximum(jnp.max(state[(sb, 0)][0]), jnp.max(state[(sb, 1)][0]))

        first = n_sub * i
        zero = (jnp.zeros((tk, 1), F32), jnp.zeros((tk, 1), F32), jnp.zeros((tk, LANES), F32))
        todo = []
        for sb in range(n_sub):
            gate = jnp.broadcast_to(first > 0, (tk, tk)) if sb == 0 else None
            todo += [(sb, first + sb, valid), (sb, jnp.maximum(first + sb - 1, 0), gate)]
        state = tiles(todo, {(sb, hh): zero for sb in range(n_sub) for hh in range(2)})
        for sb in range(n_sub):
            def cond(st):
                return jnp.logical_and(st[0] >= 0, st[2] > UNDERFLOW)

            def step(st, sb=sb):
                mine = tiles([(sb, st[0], None)], st[1])
                return st[0] - 1, mine, live(mine, sb)

            mine = {k: v for k, v in state.items() if k[0] == sb}
            _, mine, _ = lax.while_loop(cond, step, (first + sb - 2, mine, live(mine, sb)))
            dq_ref[sb * tk:(sb + 1) * tk, :] = (
                jnp.where(lane < SB_HD, mine[(sb, 0)][2], mine[(sb, 1)][2]) * SB_SCALE).astype(BF16)
        wait()

    h_ins = hosted.ins if hosted else []
    res = pl.pallas_call(
        body, name="attn_bwd_hosting" if hosted else "attn_bwd", grid=(n_p, nq),
        in_specs=_attn_qkv_specs(tq, T) + [pl.BlockSpec((tq, LANES), lambda p, i: (i, p)),
                                           pl.BlockSpec((tq, LANES), lambda p, i: (i, yc_blk + p))]
        + [HBM_SPEC] * len(h_ins),
        out_specs=[pl.BlockSpec((tq, LANES), lambda p, i: (i, p)), pl.BlockSpec((T, LANES), lambda p, i: (0, p)),
                   pl.BlockSpec((T, LANES), lambda p, i: (0, p))] + [HBM_SPEC] * len(h_ins),
        out_shape=[jax.ShapeDtypeStruct((T, SB_WIDTH), BF16)] + [jax.ShapeDtypeStruct((T, SB_WIDTH), F32)] * 2
        + (hosted.out_shapes if hosted else []),
        scratch_shapes=hosted.sems() if hosted else [],
        compiler_params=_params(has_side_effects=hosted is not None),
    )(qkv, qkv, qkv, o, dymix, *h_ins)
    return res[:3], res[3:]


def _outproj_fwd(x, ya, yb, yc, w, hosted=None):
    T, D = x.shape
    tt = _tile(T, 512)
    nt = T // tt

    def body(*refs):
        i = pl.program_id(0)
        (x_ref, ya_ref, yb_ref, yc_ref, w_ref, x1_ref, ymix_ref), start, wait = _host(
            hosted, refs, 5, 2, i == 0, i == nt - 1)
        start()
        ymix_ref[:, 0:POOL_WIDTH] = ya_ref[...].astype(BF16)
        ymix_ref[:, POOL_WIDTH:POOL_WIDTH + SG_WIDTH] = yb_ref[...].astype(BF16)
        ymix_ref[:, POOL_WIDTH + SG_WIDTH:] = yc_ref[...].astype(BF16)
        x1_ref[...] = x_ref[...] + _nn(ymix_ref[...], w_ref[...])
        wait()

    row = lambda width: pl.BlockSpec((tt, width), lambda i: (i, 0))
    h_ins = hosted.ins if hosted else []
    res = pl.pallas_call(
        body, name="outproj_fwd_hosting" if hosted else "outproj_fwd", grid=(nt,),
        in_specs=[row(D), row(POOL_WIDTH), row(SG_WIDTH), row(SB_WIDTH), pl.BlockSpec((D, D), lambda i: (0, 0))]
        + [HBM_SPEC] * len(h_ins),
        out_specs=[row(D), row(D)] + [HBM_SPEC] * len(h_ins),
        out_shape=[jax.ShapeDtypeStruct((T, D), F32), jax.ShapeDtypeStruct((T, D), BF16)]
        + (hosted.out_shapes if hosted else []),
        input_output_aliases=hosted.aliases(5, 2) if hosted else {},
        scratch_shapes=hosted.sems() if hosted else [],
        compiler_params=_params(has_side_effects=hosted is not None),
    )(x, ya, yb, yc, w, *h_ins)
    return res[:2], res[2:]


def _nt_matmul(a, w):
    T, N = a.shape
    K = w.shape[0]
    tt = _tile(T, 512)

    def body(a_ref, w_ref, o_ref):
        o_ref[...] = _nt(a_ref[...].astype(BF16), w_ref[...])

    return pl.pallas_call(
        body, name="nt_matmul", grid=(T // tt,),
        in_specs=[pl.BlockSpec((tt, N), lambda i: (i, 0)), pl.BlockSpec((K, N), lambda i: (0, 0))],
        out_specs=pl.BlockSpec((tt, K), lambda i: (i, 0)),
        out_shape=jax.ShapeDtypeStruct((T, K), F32),
        compiler_params=_params(),
    )(a, w)


def _tn_matmul(a, b, name, n_split=1, hosted=None):
    T, K = a.shape
    N = b.shape[1]
    tk = _tile(K, 1024)
    tn = _tile(N // n_split, 1024)
    tt = _tile(T, 2048)
    nper = N // n_split // tn
    nk, nn, nt = K // tk, N // tn, T // tt

    def body(*refs):
        k, n, t = pl.program_id(0), pl.program_id(1), pl.program_id(2)
        (a_ref, b_ref, o_ref), start, wait = _host(
            hosted, refs, 2, 1, jnp.logical_and(jnp.logical_and(k == 0, n == 0), t == 0),
            jnp.logical_and(jnp.logical_and(k == nk - 1, n == nn - 1), t == nt - 1))
        start()

        @pl.when(t == 0)
        def _():
            o_ref[...] = jnp.zeros_like(o_ref)

        o_ref[...] += _tn(a_ref[...], b_ref[...].astype(BF16))
        wait()

    h_ins = hosted.ins if hosted else []
    res = pl.pallas_call(
        body, name=name + "_hosting" if hosted else name, grid=(nk, nn, nt),
        in_specs=[pl.BlockSpec((tt, tk), lambda k, n, t: (t, k)), pl.BlockSpec((tt, tn), lambda k, n, t: (t, n))]
        + [HBM_SPEC] * len(h_ins),
        out_specs=[pl.BlockSpec((None, tk, tn), lambda k, n, t: (n // nper, k, n % nper))] + [HBM_SPEC] * len(h_ins),
        out_shape=[jax.ShapeDtypeStruct((n_split, K, N // n_split), F32)] + (hosted.out_shapes if hosted else []),
        scratch_shapes=hosted.sems() if hosted else [],
        compiler_params=_params(has_side_effects=hosted is not None),
    )(a, b, *h_ins)
    return (res[0], res[1:]) if hosted else res[0]


def _mlp_fwd(x, g, w_up, w_down, hosted=None):
    T, D = x.shape
    F = w_up.shape[1]
    tt = _tile(T, 1024)
    fc = _tile(F, MLP_CHUNK)
    nc = F // fc
    nt = T // tt

    def body(*refs):
        i, c = pl.program_id(0), pl.program_id(1)
        (x_ref, g_ref, wu_ref, wd_ref, y_ref, h_ref, u_ref, a_ref), start, wait = _host(
            hosted, refs, 4, 4, jnp.logical_and(i == 0, c == 0), jnp.logical_and(i == nt - 1, c == nc - 1))
        start()

        @pl.when(c == 0)
        def _():
            xv = x_ref[...]
            h, _, _ = _rms_fwd(xv, g_ref[...])
            h_ref[...] = h.astype(BF16)
            y_ref[...] = xv

        u = _nn(h_ref[...], wu_ref[...])
        u_ref[...] = u.astype(BF16)
        a = jnp.square(jnp.maximum(u, 0.0)).astype(BF16)
        a_ref[...] = a
        y_ref[...] += _nn(a, wd_ref[...])
        wait()

    h_ins = hosted.ins if hosted else []
    res = pl.pallas_call(
        body, name="mlp_fwd_hosting" if hosted else "mlp_fwd", grid=(nt, nc),
        in_specs=[pl.BlockSpec((tt, D), lambda i, c: (i, 0)), pl.BlockSpec((1, D), lambda i, c: (0, 0)),
                  pl.BlockSpec((D, fc), lambda i, c: (0, c)), pl.BlockSpec((fc, D), lambda i, c: (c, 0))]
        + [HBM_SPEC] * len(h_ins),
        out_specs=[pl.BlockSpec((tt, D), lambda i, c: (i, 0)), pl.BlockSpec((tt, D), lambda i, c: (i, 0)),
                   pl.BlockSpec((tt, fc), lambda i, c: (i, c)), pl.BlockSpec((tt, fc), lambda i, c: (i, c))]
        + [HBM_SPEC] * len(h_ins),
        out_shape=[jax.ShapeDtypeStruct((T, D), F32), jax.ShapeDtypeStruct((T, D), BF16),
                   jax.ShapeDtypeStruct((T, F), BF16), jax.ShapeDtypeStruct((T, F), BF16)]
        + (hosted.out_shapes if hosted else []),
        scratch_shapes=hosted.sems() if hosted else [],
        compiler_params=_params(has_side_effects=hosted is not None),
    )(x, g, w_up, w_down, *h_ins)
    return res[:4], res[4:]


def _mlp_bwd(dy, x, g, u, w_up, w_down, hosted=None):
    T, D = x.shape
    F = w_up.shape[1]
    tt = _tile(T, 1024)
    fc = _tile(F, MLP_CHUNK)
    nc = F // fc
    nt = T // tt

    def body(*refs):
        i, c = pl.program_id(0), pl.program_id(1)
        (dy_ref, x_ref, g_ref, u_ref, wu_ref, wd_ref, dx_ref, du_ref, dg_ref, dyb_ref, dh_ref), start, wait = _host(
            hosted, refs, 6, 3, jnp.logical_and(i == 0, c == 0), jnp.logical_and(i == nt - 1, c == nc - 1))
        start()

        @pl.when(c == 0)
        def _():
            dyb_ref[...] = dy_ref[...].astype(BF16)
            dh_ref[...] = jnp.zeros_like(dh_ref)

        @pl.when(jnp.logical_and(i == 0, c == 0))
        def _():
            dg_ref[...] = jnp.zeros_like(dg_ref)

        da = _nt(dyb_ref[...], wd_ref[...])
        du = (da * (2.0 * jnp.maximum(u_ref[...].astype(F32), 0.0))).astype(BF16)
        du_ref[...] = du
        dh_ref[...] += _nt(du, wu_ref[...])

        @pl.when(c == nc - 1)
        def _():
            gv = g_ref[...]
            _, xhat, r = _rms_fwd(x_ref[...], gv)
            dx, dgrow = _rms_bwd(dh_ref[...], xhat, r, gv)
            dx_ref[...] = dy_ref[...] + dx
            dg_ref[...] += jnp.sum(dgrow, axis=0, keepdims=True)

        wait()

    h_ins = hosted.ins if hosted else []
    res = pl.pallas_call(
        body, name="mlp_bwd_hosting" if hosted else "mlp_bwd", grid=(nt, nc),
        in_specs=[pl.BlockSpec((tt, D), lambda i, c: (i, 0)), pl.BlockSpec((tt, D), lambda i, c: (i, 0)),
                  pl.BlockSpec((1, D), lambda i, c: (0, 0)), pl.BlockSpec((tt, fc), lambda i, c: (i, c)),
                  pl.BlockSpec((D, fc), lambda i, c: (0, c)), pl.BlockSpec((fc, D), lambda i, c: (c, 0))]
        + [HBM_SPEC] * len(h_ins),
        out_specs=[pl.BlockSpec((tt, D), lambda i, c: (i, 0)), pl.BlockSpec((tt, fc), lambda i, c: (i, c)),
                   pl.BlockSpec((1, D), lambda i, c: (0, 0))] + [HBM_SPEC] * len(h_ins),
        out_shape=[jax.ShapeDtypeStruct((T, D), F32), jax.ShapeDtypeStruct((T, F), BF16),
                   jax.ShapeDtypeStruct((1, D), F32)] + (hosted.out_shapes if hosted else []),
        scratch_shapes=[pltpu.VMEM((tt, D), BF16), pltpu.VMEM((tt, D), F32)] + (hosted.sems() if hosted else []),
        compiler_params=_params(has_side_effects=hosted is not None),
    )(dy, x, g, u, w_up, w_down, *h_ins)
    return res[:3], res[3:]


def _loss_head(x, g, target):
    T, D = x.shape
    tt = _tile(T, 512)

    def body(x_ref, g_ref, t_ref, loss_ref, dx_ref, dg_ref):
        gv = g_ref[...]
        y, xhat, r = _rms_fwd(x_ref[...], gv)
        err = y - t_ref[...]
        dx, dgrow = _rms_bwd(err * (1.0 / D), xhat, r, gv)
        dx_ref[...] = dx

        @pl.when(pl.program_id(0) == 0)
        def _():
            loss_ref[...] = jnp.zeros_like(loss_ref)
            dg_ref[...] = jnp.zeros_like(dg_ref)

        loss_ref[...] += 0.5 * jnp.sum(jnp.mean(err * err, axis=-1, keepdims=True), axis=0, keepdims=True)
        dg_ref[...] += jnp.sum(dgrow, axis=0, keepdims=True)

    return pl.pallas_call(
        body, name="loss_head", grid=(T // tt,),
        in_specs=[pl.BlockSpec((tt, D), lambda i: (i, 0)), pl.BlockSpec((1, D), lambda i: (0, 0)),
                  pl.BlockSpec((tt, D), lambda i: (i, 0))],
        out_specs=[pl.BlockSpec((1, LANES), lambda i: (0, 0)), pl.BlockSpec((tt, D), lambda i: (i, 0)),
                   pl.BlockSpec((1, D), lambda i: (0, 0))],
        out_shape=[jax.ShapeDtypeStruct((1, LANES), F32), jax.ShapeDtypeStruct((T, D), F32),
                   jax.ShapeDtypeStruct((1, D), F32)],
        compiler_params=_params(),
    )(x, g, target)


def _rows(shape, pref=512):
    last = shape[-1]
    rows = 1
    for s in shape[:-1]:
        rows *= s
    tr = rows
    if rows * last > 256 * 1024:
        for cand in (pref, 256, 128, 64, 32, 16, 8):
            if rows % cand == 0:
                tr = cand
                break
    return rows, last, tr


def _elementwise(fn, name, ins, n_out, out_dtype=F32):
    shape = ins[0].shape
    rows, last, tr = _rows(shape)
    flat = [a.reshape(rows, last) for a in ins]
    n_in = len(ins)

    def body(*refs):
        res = fn(*[r[...] for r in refs[:n_in]])
        if n_out == 1:
            res = (res,)
        for r, v in zip(refs[n_in:], res):
            r[...] = v.astype(r.dtype)

    spec = pl.BlockSpec((tr, last), lambda i: (i, 0))
    outs = pl.pallas_call(
        body, name=name, grid=(rows // tr,),
        in_specs=[spec] * n_in, out_specs=[spec] * n_out,
        out_shape=[jax.ShapeDtypeStruct((rows, last), out_dtype)] * n_out,
        compiler_params=_params(),
    )(*flat)
    return [o.reshape(shape) for o in outs]


def _add_pair(g, o, c_idx):
    nq, R, C = g.shape
    h = R // 2
    tr = _tile(h, 512)
    nb = h // tr

    def body(c_ref, g_ref, o_ref, out_ref):
        out_ref[...] = g_ref[...] + o_ref[...]

    return pl.pallas_call(
        body, name="add_pair",
        grid_spec=pltpu.PrefetchScalarGridSpec(
            num_scalar_prefetch=1, grid=(nq, nb),
            in_specs=[pl.BlockSpec((None, tr, C), lambda q, i, c: (q, c[0] * nb + i, 0)),
                      pl.BlockSpec((None, tr, C), lambda q, i, c: (q, i, 0))],
            out_specs=pl.BlockSpec((None, tr, C), lambda q, i, c: (q, i, 0))),
        out_shape=jax.ShapeDtypeStruct((nq, h, C), F32),
        compiler_params=_params(),
    )(c_idx.astype(jnp.int32).reshape(1), g, o)


def _add_chips(p, r, q_idx):
    _, H, C = p.shape
    tr = _tile(H, 512)

    def body(q_ref, p_ref, r0_ref, r1_ref, r2_ref, out_ref):
        out_ref[...] = (p_ref[...] + r0_ref[...]) + (r1_ref[...] + r2_ref[...])

    def arrived(k):
        return pl.BlockSpec((None, tr, C), lambda i, q: (k, i, 0))

    return pl.pallas_call(
        body, name="add_chips",
        grid_spec=pltpu.PrefetchScalarGridSpec(
            num_scalar_prefetch=1, grid=(H // tr,),
            in_specs=[pl.BlockSpec((None, tr, C), lambda i, q: (q[0], i, 0)), arrived(0), arrived(1), arrived(2)],
            out_specs=pl.BlockSpec((tr, C), lambda i, q: (i, 0))),
        out_shape=jax.ShapeDtypeStruct((H, C), F32),
        compiler_params=_params(),
    )(q_idx.astype(jnp.int32).reshape(1), p, r, r, r)


def _adamw(w, g, m, v):
    m = ADAM_B1 * m + (1.0 - ADAM_B1) * g
    v = ADAM_B2 * v + (1.0 - ADAM_B2) * jnp.square(g)
    m_hat = m / (1.0 - ADAM_B1 ** ADAM_STEP)
    v_hat = v / (1.0 - ADAM_B2 ** ADAM_STEP)
    delta = -ADAM_LR * (m_hat / (jnp.sqrt(v_hat) + ADAM_EPS) + ADAM_WD * w)
    return delta, m, v


def _place():
    x, y, c = lax.axis_index("x"), lax.axis_index("y"), lax.axis_index("c")
    chips = [(1 - x, y), (x, 1 - y), (1 - x, 1 - y)]
    return x, y, c, chips


def _remote(src, dst, ssem, rsem, k, dev):
    return pltpu.make_async_remote_copy(src_ref=src, dst_ref=dst, send_sem=ssem.at[k], recv_sem=rsem.at[k],
                                        device_id=dev, device_id_type=MESH)


def _gather_weights(shards):
    n = len(shards)
    halves = [s.shape[1] // 2 for s in shards]

    def body(*refs):
        src, out = refs[:n], refs[n:2 * n]
        ssem, rsem = refs[2 * n:]
        x, y, c, chips = _place()
        me_q = 2 * x + y
        sib = (x, y, 1 - c)

        def half(a, q, cc):
            return out[a].at[q, :, pl.ds(cc * halves[a], halves[a]), :]

        first = []
        for a in range(n):
            mine = src[a].at[:, pl.ds(c * halves[a], halves[a]), :]
            for r, chip in enumerate(chips):
                first.append(_remote(mine, half(a, me_q, c), ssem, rsem, a * 3 + r, (*chip, c)))
        for cp in first:
            cp.start()
        passed = []
        for a in range(n):
            for r, chip in enumerate(chips):
                q = 2 * chip[0] + chip[1]
                k = a * 3 + r
                _remote(half(a, q, c), half(a, q, c), ssem, rsem, k, (*chip, c)).wait_recv()
                cp = _remote(half(a, q, c), half(a, q, c), ssem, rsem, 3 * n + k, sib)
                cp.start()
                passed.append(cp)
        for a in range(n):
            for r, chip in enumerate(chips):
                q = 2 * chip[0] + chip[1]
                _remote(half(a, q, 1 - c), half(a, q, 1 - c), ssem, rsem, 3 * n + a * 3 + r, sib).wait_recv()
        for cp in first + passed:
            cp.wait_send()

    return pl.pallas_call(
        body, name="gather_weights",
        in_specs=[HBM_SPEC] * n, out_specs=[HBM_SPEC] * n,
        out_shape=[jax.ShapeDtypeStruct((N_CHIPS,) + s.shape, s.dtype) for s in shards],
        scratch_shapes=[pltpu.SemaphoreType.DMA((6 * n,)), pltpu.SemaphoreType.DMA((6 * n,))],
        compiler_params=_params(has_side_effects=True),
    )(*shards)


def _gather_over_ici(shards):
    n = len(shards)
    halves = [s.shape[1] // 2 for s in shards]

    def copies(src, out, ssem, rsem):
        x, y, c, chips = _place()
        me_q = 2 * x + y
        res = []
        for a in range(n):
            rows = pl.ds(c * halves[a], halves[a])
            mine = src[a].at[:, rows, :]
            for r, chip in enumerate(chips):
                dev = (*chip, c)
                res.append((_remote(mine, out[a].at[me_q, :, rows, :], ssem, rsem, a * 3 + r, dev),
                            _remote(mine, out[a].at[2 * chip[0] + chip[1], :, rows, :], ssem, rsem, a * 3 + r, dev)))
        return res

    return _Hosted(list(shards), [jax.ShapeDtypeStruct((N_CHIPS,) + s.shape, s.dtype) for s in shards], 3 * n, copies)


def _pass_over_d2d(gathered):
    n = len(gathered)
    halves = [g.shape[2] // 2 for g in gathered]

    def copies(_, out, ssem, rsem):
        x, y, c, chips = _place()
        sib = (x, y, 1 - c)
        res = []
        for a in range(n):
            for r, chip in enumerate(chips):
                q = 2 * chip[0] + chip[1]
                mine = out[a].at[q, :, pl.ds(c * halves[a], halves[a]), :]
                theirs = out[a].at[q, :, pl.ds((1 - c) * halves[a], halves[a]), :]
                res.append((_remote(mine, mine, ssem, rsem, a * 3 + r, sib),
                            _remote(theirs, theirs, ssem, rsem, a * 3 + r, sib)))
        return res

    return _Hosted(list(gathered), [jax.ShapeDtypeStruct(g.shape, g.dtype) for g in gathered], 3 * n, copies,
                   in_place=True)


def _pass_to_sibling(gathered):
    n = len(gathered)
    halves = [g.shape[2] // 2 for g in gathered]

    def body(*refs):
        out = refs[n:2 * n]
        ssem, rsem = refs[2 * n:]
        x, y, c, chips = _place()
        sib = (x, y, 1 - c)

        def half(a, q, cc):
            return out[a].at[q, :, pl.ds(cc * halves[a], halves[a]), :]

        cps = []
        for a in range(n):
            for r, chip in enumerate(chips):
                q = 2 * chip[0] + chip[1]
                cps.append(_remote(half(a, q, c), half(a, q, c), ssem, rsem, a * 3 + r, sib))
        for cp in cps:
            cp.start()
        for a in range(n):
            for r, chip in enumerate(chips):
                q = 2 * chip[0] + chip[1]
                _remote(half(a, q, 1 - c), half(a, q, 1 - c), ssem, rsem, a * 3 + r, sib).wait_recv()
        for cp in cps:
            cp.wait_send()

    return pl.pallas_call(
        body, name="pass_to_sibling",
        in_specs=[HBM_SPEC] * n, out_specs=[HBM_SPEC] * n,
        out_shape=[jax.ShapeDtypeStruct(g.shape, g.dtype) for g in gathered],
        input_output_aliases={a: a for a in range(n)},
        scratch_shapes=[pltpu.SemaphoreType.DMA((3 * n,)), pltpu.SemaphoreType.DMA((3 * n,))],
        compiler_params=_params(has_side_effects=True),
    )(*gathered)


def _scatter_over_ici(parts):
    n = len(parts)

    def copies(src, out, ssem, rsem):
        x, y, c, chips = _place()
        res = []
        for a in range(n):
            for r, chip in enumerate(chips):
                cp = _remote(src[a].at[2 * chip[0] + chip[1]], out[a].at[r], ssem, rsem, a * 3 + r, (*chip, c))
                res.append((cp, cp))
        return res

    return _Hosted(list(parts), [jax.ShapeDtypeStruct((3,) + p.shape[1:], F32) for p in parts], 3 * n, copies)


def _swap_over_d2d(grads):
    n = len(grads)
    halves = [g.shape[1] // 2 for g in grads]

    def copies(src, out, ssem, rsem):
        x, y, c, _ = _place()
        res = []
        for a in range(n):
            cp = _remote(src[a].at[:, pl.ds((1 - c) * halves[a], halves[a]), :], out[a], ssem, rsem, a, (x, y, 1 - c))
            res.append((cp, cp))
        return res

    return _Hosted(list(grads), [jax.ShapeDtypeStruct((N_CHIPS, h, g.shape[2]), F32) for g, h in zip(grads, halves)],
                   n, copies)


def _swap_halves(grads):
    n = len(grads)
    halves = [g.shape[1] // 2 for g in grads]

    def body(*refs):
        src, out = refs[:n], refs[n:2 * n]
        ssem, rsem = refs[2 * n:]
        x, y, c, _ = _place()
        cps = [_remote(src[a].at[:, pl.ds((1 - c) * halves[a], halves[a]), :], out[a], ssem, rsem, a, (x, y, 1 - c))
               for a in range(n)]
        for cp in cps:
            cp.start()
        for cp in cps:
            cp.wait()

    return pl.pallas_call(
        body, name="swap_halves",
        in_specs=[HBM_SPEC] * n, out_specs=[HBM_SPEC] * n,
        out_shape=[jax.ShapeDtypeStruct((N_CHIPS, h, g.shape[2]), F32) for g, h in zip(grads, halves)],
        scratch_shapes=[pltpu.SemaphoreType.DMA((n,)), pltpu.SemaphoreType.DMA((n,))],
        compiler_params=_params(has_side_effects=True),
    )(*grads)


def _scatter_chips(parts):
    n = len(parts)

    def body(*refs):
        src, out = refs[:n], refs[n:2 * n]
        ssem, rsem = refs[2 * n:]
        x, y, c, chips = _place()
        cps = []
        for a in range(n):
            for r, chip in enumerate(chips):
                cps.append(_remote(src[a].at[2 * chip[0] + chip[1]], out[a].at[r], ssem, rsem, a * 3 + r, (*chip, c)))
        for cp in cps:
            cp.start()
        for cp in cps:
            cp.wait()

    return pl.pallas_call(
        body, name="scatter_chips",
        in_specs=[HBM_SPEC] * n, out_specs=[HBM_SPEC] * n,
        out_shape=[jax.ShapeDtypeStruct((3,) + p.shape[1:], F32) for p in parts],
        scratch_shapes=[pltpu.SemaphoreType.DMA((3 * n,)), pltpu.SemaphoreType.DMA((3 * n,))],
        compiler_params=_params(has_side_effects=True),
    )(*parts)


def _swap_reduced_over_d2d(reduced):
    n = len(reduced)

    def copies(src, out, ssem, rsem):
        x, y, c, _ = _place()
        res = []
        for a in range(n):
            cp = _remote(src[a], out[a], ssem, rsem, a, (x, y, 1 - c))
            res.append((cp, cp))
        return res

    return _Hosted(list(reduced), [jax.ShapeDtypeStruct(r.shape, F32) for r in reduced], n, copies)


def _swap_reduced(reduced):
    n = len(reduced)

    def body(*refs):
        src, out = refs[:n], refs[n:2 * n]
        ssem, rsem = refs[2 * n:]
        x, y, c, _ = _place()
        cps = [_remote(src[a], out[a], ssem, rsem, a, (x, y, 1 - c)) for a in range(n)]
        for cp in cps:
            cp.start()
        for cp in cps:
            cp.wait()

    return pl.pallas_call(
        body, name="swap_reduced",
        in_specs=[HBM_SPEC] * n, out_specs=[HBM_SPEC] * n,
        out_shape=[jax.ShapeDtypeStruct(r.shape, F32) for r in reduced],
        scratch_shapes=[pltpu.SemaphoreType.DMA((n,)), pltpu.SemaphoreType.DMA((n,))],
        compiler_params=_params(has_side_effects=True),
    )(*reduced)


def _allreduce_small(buf, hosted=None):
    R, L = buf.shape

    def body(*refs):
        (buf_ref, out_ref, pair_ref, chip_ref, ssem, rsem), start, wait = _host(hosted, refs, 1, 1, True, True)
        start()
        x, y, c, chips = _place()
        me_q = 2 * x + y
        pair_ref[c] = buf_ref[...]
        to_sib = _remote(buf_ref, pair_ref.at[c], ssem, rsem, 0, (x, y, 1 - c))
        to_sib.start()
        _remote(buf_ref, pair_ref.at[1 - c], ssem, rsem, 0, (x, y, 1 - c)).wait_recv()
        chip_ref[me_q] = pair_ref[0] + pair_ref[1]
        cps = [_remote(chip_ref.at[me_q], chip_ref.at[me_q], ssem, rsem, 1 + r, (*chip, c))
               for r, chip in enumerate(chips)]
        for cp in cps:
            cp.start()
        for r, chip in enumerate(chips):
            q = 2 * chip[0] + chip[1]
            _remote(chip_ref.at[q], chip_ref.at[q], ssem, rsem, 1 + r, (*chip, c)).wait_recv()
        out_ref[...] = (chip_ref[0] + chip_ref[1]) + (chip_ref[2] + chip_ref[3])
        to_sib.wait_send()
        for cp in cps:
            cp.wait_send()
        wait()

    h_ins = hosted.ins if hosted else []
    res = pl.pallas_call(
        body, name="allreduce_small",
        in_specs=[VMEM_SPEC] + [HBM_SPEC] * len(h_ins), out_specs=[VMEM_SPEC] + [HBM_SPEC] * len(h_ins),
        out_shape=[jax.ShapeDtypeStruct((R, L), F32)] + (hosted.out_shapes if hosted else []),
        scratch_shapes=[pltpu.VMEM((2, R, L), F32), pltpu.VMEM((N_CHIPS, R, L), F32),
                        pltpu.SemaphoreType.DMA((4,)), pltpu.SemaphoreType.DMA((4,))]
        + (hosted.sems() if hosted else []),
        compiler_params=_params(has_side_effects=True),
    )(buf, *h_ins)
    return res[0], res[1:]


def _pack(arrays):
    flat = jnp.concatenate([a.reshape(-1) for a in arrays])
    pad = (-flat.shape[0]) % (8 * LANES)
    return jnp.pad(flat, (0, pad)).reshape(-1, LANES)


def _unpack(buf, like):
    flat = buf.reshape(-1)
    out, off = [], 0
    for a in like:
        out.append(flat[off:off + a.size].reshape(a.shape))
        off += a.size
    return out


def _block_diag(pw):
    rows = []
    for gi in range(len(POOL_WINDOWS)):
        blocks = [pw[gi] if gj == gi else jnp.zeros_like(pw[gi]) for gj in range(len(POOL_WINDOWS))]
        rows.append(jnp.concatenate(blocks, axis=1))
    return jnp.concatenate(rows, axis=0)


def kernel(x, norm1, w_in, pool_w, pool_scale, sg_norm, sg_w, sg_b, w_out, norm2, w_up, w_down, final_norm, loss_target, m_norm1, m_w_in, m_pool_w, m_pool_scale, m_sg_norm, m_sg_w, m_sg_b, m_w_out, m_norm2, m_w_up, m_w_down, m_final_norm, v_norm1, v_w_in, v_pool_w, v_pool_scale, v_sg_norm, v_sg_w, v_sg_b, v_w_out, v_norm2, v_w_up, v_w_down, v_final_norm):
    depth = norm1.shape[0]
    T = x.shape[1]
    xs = x.reshape(T, D_MODEL)
    target = loss_target.reshape(T, D_MODEL)

    assert depth == 2
    c_idx = lax.axis_index("c")
    q_idx = 2 * lax.axis_index("x") + lax.axis_index("y")
    own = [w.astype(BF16) for w in (w_in, w_out, w_up, w_down)]
    gathered = {(0, 0): _gather_weights([own[0][:1]])[0]}

    def full(a, l, axis):
        blocks = lax.dynamic_update_slice(gathered[(a, l)], own[a][l][None, None], (q_idx, 0, 0, 0))[:, 0]
        if axis == 0:
            return blocks.reshape(-1, blocks.shape[-1])
        return jnp.concatenate([blocks[q] for q in range(N_CHIPS)], axis=axis)

    half_way = {}

    def gather_behind(call, keys, at_once):
        res, over_ici = call(_gather_over_ici([own[a][l:l + 1] for a, l in keys]))
        gathered.update(zip(keys[:at_once], _pass_to_sibling(over_ici[:at_once])))
        half_way.update(zip(keys[at_once:], over_ici[at_once:]))
        return res

    def pass_behind(call, keys):
        res, done = call(_pass_over_d2d([half_way.pop(k) for k in keys]))
        gathered.update(zip(keys, done))
        return res

    tril = jnp.tril(jnp.ones((CHUNK, CHUNK), F32))
    saved = []
    cur = xs
    wi, wo, wu, wd = {}, {}, {}, {}
    for l in range(depth):
        wbd = _block_diag(pool_w[l]).astype(BF16)
        wm = sg_w[l] * tril
        wm_s = wm.reshape(SG_HEADS * CHUNK, CHUNK).astype(BF16)
        wmt_s = jnp.swapaxes(wm, 1, 2).reshape(SG_HEADS * CHUNK, CHUNK).astype(BF16)
        bias = jnp.repeat(sg_b[l].T, SB_HD, axis=1)
        n1, n2 = norm1[l][None], norm2[l][None]
        psc, sgn = pool_scale[l][None], sg_norm[l][None]
        wi[l] = full(0, l, 1)
        proj, h, qkv = _inproj_fwd(cur, n1, wi[l])
        ya = _pool_fwd(proj, wbd, psc)
        yb = _sg_fwd(proj, wm_s, bias, sgn)
        if l == 0:
            yc = gather_behind(lambda hosted: _attn_fwd(qkv, hosted), [(1, 0), (2, 0), (3, 0)], 1)
            wo[l] = full(1, l, 0)
            x1, ymix = pass_behind(lambda hosted: _outproj_fwd(cur, ya, yb, yc, wo[l], hosted), [(2, 0), (3, 0)])
        else:
            yc = pass_behind(lambda hosted: _attn_fwd(qkv, hosted), [(1, l), (2, l), (3, l)])
            wo[l] = full(1, l, 0)
            (x1, ymix), _ = _outproj_fwd(cur, ya, yb, yc, wo[l])
        wu[l], wd[l] = full(2, l, 1), full(3, l, 0)
        if l == 0:
            x2, h2, u, act = gather_behind(lambda hosted: _mlp_fwd(x1, n2, wu[l], wd[l], hosted),
                                           [(0, 1), (1, 1), (2, 1), (3, 1)], 1)
        else:
            (x2, h2, u, act), _ = _mlp_fwd(x1, n2, wu[l], wd[l])
        saved.append(dict(x0=cur, x1=x1, proj=proj, h=h, qkv=qkv, yc=yc, ymix=ymix, h2=h2, u=u, act=act,
                          wbd=wbd, wm_s=wm_s, wmt_s=wmt_s, bias=bias, n1=n1, n2=n2, psc=psc, sgn=sgn))
        cur = x2

    loss_row, dcur, d_final = _loss_head(cur, final_norm[None], target)

    small = [None] * depth
    grads, parts, reduced = {}, {}, {}

    def pair_up(keys, swapped):
        parts.update({k: _add_pair(grads[k], o, c_idx) for k, o in zip(keys, swapped)})

    def chip_up(keys, arrived):
        reduced.update({k: _add_chips(parts[k], r, q_idx) for k, r in zip(keys, arrived)})

    for l in reversed(range(depth)):
        s = saved[l]
        if l == 0:
            keys = [(2, 1), (3, 1)]
            (dx1, du, d_n2), arrived = _mlp_bwd(dcur, s["x1"], s["n2"], s["u"], wu[l], wd[l],
                                                _scatter_over_ici([parts[k] for k in keys]))
            chip_up(keys, arrived)
        else:
            (dx1, du, d_n2), _ = _mlp_bwd(dcur, s["x1"], s["n2"], s["u"], wu[l], wd[l])
        grads[(2, l)] = _tn_matmul(s["h2"], du, "grad_w_up", n_split=N_CHIPS)
        grads[(3, l)] = _tn_matmul(s["act"], dcur, "grad_w_down")[0].reshape(N_CHIPS, D_FF // N_CHIPS, D_MODEL)
        dymix = _nt_matmul(dx1, wo[l])
        grads[(1, l)] = _tn_matmul(s["ymix"], dx1, "grad_w_out")[0].reshape(N_CHIPS, D_MODEL // N_CHIPS, D_MODEL)
        da_in, d_wbd, d_psc = _pool_bwd(s["proj"], dymix, s["wbd"], s["psc"])
        if l == 0:
            keys = [(0, 1), (1, 0), (2, 0), (3, 0)]
            (du_pre, dv_pre, d_wm, d_bias, d_sgn), swapped = _sg_bwd(
                s["proj"], dymix, s["wm_s"], s["wmt_s"], s["bias"], s["sgn"], _swap_over_d2d([grads[k] for k in keys]))
            pair_up(keys, swapped)
            keys = [(1, 1)] + keys
            (dq, dk, dv), arrived = _attn_bwd(s["qkv"], s["yc"], dymix, _scatter_over_ici([parts[k] for k in keys]))
            chip_up(keys, arrived)
        else:
            (du_pre, dv_pre, d_wm, d_bias, d_sgn), _ = _sg_bwd(s["proj"], dymix, s["wm_s"], s["wmt_s"], s["bias"], s["sgn"])
            keys = [(1, l), (2, l), (3, l)]
            (dq, dk, dv), swapped = _attn_bwd(s["qkv"], s["yc"], dymix, _swap_over_d2d([grads[k] for k in keys]))
            pair_up(keys, swapped)
        dproj, dx0, d_n1 = _inproj_bwd([da_in, du_pre, dv_pre, dq, dk, dv], wi[l], s["x0"], s["n1"], dx1)
        if l == 0:
            keys = sorted(reduced)
            g_in_l, swapped = _tn_matmul(s["h"], dproj, "grad_w_in",
                                         hosted=_swap_reduced_over_d2d([reduced[k] for k in keys]))
            theirs = dict(zip(keys, swapped))
        else:
            g_in_l = _tn_matmul(s["h"], dproj, "grad_w_in")
        grads[(0, l)] = g_in_l[0].reshape(D_MODEL, N_CHIPS, IN_COLS // N_CHIPS).transpose(1, 0, 2)
        d_pw = jnp.stack([d_wbd[gi * POOL_GW:(gi + 1) * POOL_GW, gi * POOL_GW:(gi + 1) * POOL_GW]
                          for gi in range(len(POOL_WINDOWS))])
        small[l] = dict(norm1=d_n1[0], pool_w=d_pw, pool_scale=d_psc[0], sg_norm=d_sgn[0],
                        sg_w=d_wm.reshape(SG_HEADS, CHUNK, CHUNK), sg_b=d_bias[:, :SG_HEADS].T, norm2=d_n2[0])
        dcur = dx0
    grad_x = dcur.reshape(x.shape)

    names = ["norm1", "pool_w", "pool_scale", "sg_norm", "sg_w", "sg_b", "norm2"]
    slot = jnp.zeros((1,), F32)
    small_w = [norm1, pool_w, pool_scale, sg_norm, sg_w, sg_b, norm2, final_norm, slot]
    small_m = [m_norm1, m_pool_w, m_pool_scale, m_sg_norm, m_sg_w, m_sg_b, m_norm2, m_final_norm, slot]
    small_v = [v_norm1, v_pool_w, v_pool_scale, v_sg_norm, v_sg_w, v_sg_b, v_norm2, v_final_norm, slot]
    small_g = [jnp.stack([small[l][k] for l in range(depth)]) for k in names] + [d_final[0], loss_row[0, :1]]
    keys = [(0, 0)]
    pair_up(keys, _swap_halves([grads[k] for k in keys]))
    g_packed, arrived = _allreduce_small(_pack(small_g), _scatter_over_ici([parts[k] for k in keys]))
    chip_up(keys, arrived)
    theirs.update(zip(keys, _swap_reduced([reduced[k] for k in keys])))

    def joined(a):
        layers = []
        for l in range(depth):
            mine, other = reduced[(a, l)], theirs[(a, l)]
            layers.append(jnp.where(c_idx == 0, jnp.concatenate([mine, other]), jnp.concatenate([other, mine])))
        return jnp.stack(layers)

    gw_in, gw_out, gw_up, gw_down = [joined(a) for a in range(4)]

    loss = _unpack(g_packed, small_w)[-1][0]
    s_delta, s_m, s_v = _elementwise(_adamw, "adamw_small", [_pack(small_w), g_packed, _pack(small_m), _pack(small_v)], 3)
    gs = dict(zip(names + ["final_norm"], _unpack(g_packed, small_w)))
    ds = dict(zip(names + ["final_norm"], _unpack(s_delta, small_w)))
    ms = dict(zip(names + ["final_norm"], _unpack(s_m, small_w)))
    vs = dict(zip(names + ["final_norm"], _unpack(s_v, small_w)))

    big_g = dict(w_in=gw_in, w_out=gw_out, w_up=gw_up, w_down=gw_down)
    big_w = dict(w_in=(w_in, m_w_in, v_w_in), w_out=(w_out, m_w_out, v_w_out),
                 w_up=(w_up, m_w_up, v_w_up), w_down=(w_down, m_w_down, v_w_down))
    for k, (w, m, v) in big_w.items():
        operands = [w, big_g[k], m, v]
        if k == "w_in":
            operands = [jnp.swapaxes(o, 1, 2) for o in operands]
        ds[k], ms[k], vs[k] = _elementwise(_adamw, "adamw_" + k, operands, 3)
        if k == "w_in":
            ds[k], ms[k], vs[k] = [jnp.swapaxes(o, 1, 2) for o in (ds[k], ms[k], vs[k])]
        gs[k] = big_g[k]

    order = ["norm1", "w_in", "pool_w", "pool_scale", "sg_norm", "sg_w", "sg_b", "w_out", "norm2", "w_up", "w_down",
             "final_norm"]
    return (loss, grad_x, *[gs[k] for k in order], *[ds[k] for k in order], *[ms[k] for k in order],
            *[vs[k] for k in order])
```

```python
import functools

import jax
import jax.numpy as jnp
from jax import lax
from jax.experimental import pallas as pl
from jax.experimental.pallas import tpu as pltpu

F32 = jnp.float32
BF16 = jnp.bfloat16
MESH = pl.DeviceIdType.MESH
AXES = ("x", "y", "c")

EPS = 1e-6
D_MODEL = 1024
POOL_WIDTH = 256
SG_WIDTH = 256
SB_WIDTH = 512
POOL_WINDOWS = (2, 4, 8, 16)
POOL_GW = 64
POOL_HALO = 16
CHUNK = 128
SG_HEADS = 4
SB_HD = 64
SB_SCALE = 0.125
IN_COLS = 2304
QKV_OFF = 768
D_FF = 4096
N_CHIPS = 4
LANES = 128
VMEM_LIMIT = 56 * 1024 * 1024
MLP_CHUNK = 512
ATTN_TILE = 256
UNDERFLOW = -104.0

ADAM_LR = 0.001
ADAM_B1 = 0.9
ADAM_B2 = 0.999
ADAM_EPS = 1e-08
ADAM_WD = 0.01
ADAM_STEP = 10

HBM_SPEC = pl.BlockSpec(memory_space=pl.ANY)
VMEM_SPEC = pl.BlockSpec(memory_space=pltpu.VMEM)


def _params(**kw):
    return pltpu.CompilerParams(vmem_limit_bytes=VMEM_LIMIT, **kw)


def _tile(n, pref):
    if n <= pref:
        return n
    for t in range(pref - pref % LANES, 0, -LANES):
        if n % t == 0:
            return t
    raise ValueError((n, pref))


def _nn(a, b):
    return jnp.dot(a, b, preferred_element_type=F32)


def _nt(a, b):
    return lax.dot_general(a, b, (((1,), (1,)), ((), ())), preferred_element_type=F32)


def _tn(a, b):
    return lax.dot_general(a, b, (((0,), (0,)), ((), ())), preferred_element_type=F32)


def _rms_fwd(x, g):
    r = lax.rsqrt(jnp.mean(x * x, axis=-1, keepdims=True) + EPS)
    xhat = x * r
    return xhat * g, xhat, r


def _rms_bwd(dy, xhat, r, g):
    dxhat = dy * g
    dx = r * (dxhat - xhat * jnp.mean(dxhat * xhat, axis=-1, keepdims=True))
    return dx, dy * xhat


_GELU_K = 0.7978845608028654
_GELU_C = 0.044715


def _gelu(x):
    return 0.5 * x * (1.0 + jnp.tanh(_GELU_K * (x + _GELU_C * x * x * x)))


def _gelu_grad(x):
    t = jnp.tanh(_GELU_K * (x + _GELU_C * x * x * x))
    return 0.5 * (1.0 + t) + 0.5 * x * (1.0 - t * t) * _GELU_K * (1.0 + 3.0 * _GELU_C * x * x)


def _inproj_fwd(x, g, w):
    T, D = x.shape
    N = w.shape[1]
    tt = _tile(T, 512)

    def body(x_ref, g_ref, w_ref, proj_ref, h_ref, qkv_ref):
        h, _, _ = _rms_fwd(x_ref[...], g_ref[...])
        hb = h.astype(BF16)
        h_ref[...] = hb
        p = _nn(hb, w_ref[...])
        proj_ref[...] = p[:, :QKV_OFF]
        qkv_ref[...] = p[:, QKV_OFF:].astype(BF16)

    return pl.pallas_call(
        body, name="inproj_fwd", grid=(T // tt,),
        in_specs=[pl.BlockSpec((tt, D), lambda i: (i, 0)), pl.BlockSpec((1, D), lambda i: (0, 0)),
                  pl.BlockSpec((D, N), lambda i: (0, 0))],
        out_specs=[pl.BlockSpec((tt, QKV_OFF), lambda i: (i, 0)), pl.BlockSpec((tt, D), lambda i: (i, 0)),
                   pl.BlockSpec((tt, N - QKV_OFF), lambda i: (i, 0))],
        out_shape=[jax.ShapeDtypeStruct((T, QKV_OFF), F32), jax.ShapeDtypeStruct((T, D), BF16),
                   jax.ShapeDtypeStruct((T, N - QKV_OFF), BF16)],
        compiler_params=_params(),
    )(x, g, w)


def _inproj_bwd(pieces, w, x, g, dres, hosted=None):
    T, D = x.shape
    N = w.shape[1]
    tt = _tile(T, 512)
    nt = T // tt
    widths = [p.shape[1] for p in pieces]
    offs = [sum(widths[:k]) for k in range(len(widths))]
    assert sum(widths) == N
    n_p = len(pieces)

    def body(*refs):
        i = pl.program_id(0)
        own, start, wait = _host(hosted, refs, n_p + 4, 2, i == 0, i == nt - 1)
        start()
        p_refs = own[:n_p]
        w_ref, x_ref, g_ref, dres_ref, dx_ref, dg_ref, dproj_ref = own[n_p:]
        for p_ref, o, wd in zip(p_refs, offs, widths):
            dproj_ref[:, o:o + wd] = p_ref[...].astype(BF16)
        dh = _nt(dproj_ref[...], w_ref[...])
        gv = g_ref[...]
        _, xhat, r = _rms_fwd(x_ref[...], gv)
        dx, dgrow = _rms_bwd(dh, xhat, r, gv)
        dx_ref[...] = dres_ref[...] + dx

        @pl.when(i == 0)
        def _():
            dg_ref[...] = jnp.zeros_like(dg_ref)

        dg_ref[...] += jnp.sum(dgrow, axis=0, keepdims=True)
        wait()

    h_ins = hosted.ins if hosted else []
    res = pl.pallas_call(
        body, name="inproj_bwd_hosting" if hosted else "inproj_bwd", grid=(nt,),
        in_specs=[pl.BlockSpec((tt, wd), lambda i: (i, 0)) for wd in widths] + [
            pl.BlockSpec((D, N), lambda i: (0, 0)), pl.BlockSpec((tt, D), lambda i: (i, 0)),
            pl.BlockSpec((1, D), lambda i: (0, 0)), pl.BlockSpec((tt, D), lambda i: (i, 0))] + [HBM_SPEC] * len(h_ins),
        out_specs=[pl.BlockSpec((tt, D), lambda i: (i, 0)), pl.BlockSpec((1, D), lambda i: (0, 0))]
        + [HBM_SPEC] * len(h_ins),
        out_shape=[jax.ShapeDtypeStruct((T, D), F32), jax.ShapeDtypeStruct((1, D), F32)]
        + (hosted.out_shapes if hosted else []),
        scratch_shapes=[pltpu.VMEM((tt, N), BF16)] + (hosted.sems() if hosted else []),
        compiler_params=_params(has_side_effects=hosted is not None),
    )(*pieces, w, x, g, dres, *h_ins)
    return res[:2], res[2:]


def _inproj_grad(h, pieces, hosted=None):
    T, D = h.shape
    tt = _tile(T, 1024)
    nt = T // tt
    widths = [p.shape[1] for p in pieces]
    offs = [sum(widths[:k]) for k in range(len(widths))]
    N = sum(widths)
    n_p = len(pieces)

    def body(*refs):
        t = pl.program_id(0)
        own, start, wait = _host(hosted, refs, n_p + 1, 1, t == 0, t == nt - 1)
        start()
        h_ref, p_refs, o_ref = own[0], own[1:1 + n_p], own[1 + n_p]

        @pl.when(t == 0)
        def _():
            o_ref[...] = jnp.zeros_like(o_ref)

        hv = h_ref[...]
        for p_ref, o, wd in zip(p_refs, offs, widths):
            o_ref[:, o:o + wd] += _tn(hv, p_ref[...].astype(BF16))
        wait()

    h_ins = hosted.ins if hosted else []
    res = pl.pallas_call(
        body, name="grad_w_in_hosting" if hosted else "grad_w_in", grid=(nt,),
        in_specs=[pl.BlockSpec((tt, D), lambda t: (t, 0))] + [pl.BlockSpec((tt, wd), lambda t: (t, 0)) for wd in widths]
        + [HBM_SPEC] * len(h_ins),
        out_specs=[pl.BlockSpec((None, D, N), lambda t: (0, 0, 0))] + [HBM_SPEC] * len(h_ins),
        out_shape=[jax.ShapeDtypeStruct((1, D, N), F32)] + (hosted.out_shapes if hosted else []),
        scratch_shapes=hosted.sems() if hosted else [],
        compiler_params=_params(has_side_effects=hosted is not None),
    )(h, *pieces, *h_ins)
    return res[0], res[1:]


def _pool_select(s2, s4, s8, s16, grp):
    return jnp.where(grp == 0, s2, jnp.where(grp == 1, s4, jnp.where(grp == 2, s8, s16)))


def _pool_count(t_glob, grp):
    win = jnp.where(grp == 0, 2, jnp.where(grp == 1, 4, jnp.where(grp == 2, 8, 16)))
    return jnp.minimum(t_glob + 1, win).astype(F32)


def _pool_diff(a, halo, base, tt):
    n = tt + POOL_HALO
    ext = jnp.concatenate([halo, a], axis=0)
    s2 = ext + pltpu.roll(ext, 1, 0)
    s4 = s2 + pltpu.roll(s2, 2, 0)
    s8 = s4 + pltpu.roll(s4, 4, 0)
    s16 = s8 + pltpu.roll(s8, 8, 0)
    grp = lax.broadcasted_iota(jnp.int32, (n, POOL_WIDTH), 1) // POOL_GW
    t_glob = lax.broadcasted_iota(jnp.int32, (n, POOL_WIDTH), 0) + (base - POOL_HALO)
    pooled = _pool_select(s2, s4, s8, s16, grp) / _pool_count(t_glob, grp)
    return pooled[POOL_HALO:] - a


def _pool_specs(T, tt):
    hb = tt // POOL_HALO
    return [pl.BlockSpec((tt, POOL_WIDTH), lambda i: (i, 0)),
            pl.BlockSpec((POOL_HALO, POOL_WIDTH), lambda i: (jnp.maximum(i * hb - 1, 0), 0))]


def _pool_fwd(proj, wbd, scale):
    T = proj.shape[0]
    tt = _tile(T, 512)

    def body(a_ref, halo_ref, w_ref, sc_ref, y_ref):
        i = pl.program_id(0)
        halo = jnp.where(i > 0, halo_ref[...], 0.0)
        d = _pool_diff(a_ref[...], halo, i * tt, tt)
        y_ref[...] = _nn(d.astype(BF16), w_ref[...]) * sc_ref[...]

    return pl.pallas_call(
        body, name="pool_fwd", grid=(T // tt,),
        in_specs=_pool_specs(T, tt) + [pl.BlockSpec((POOL_WIDTH, POOL_WIDTH), lambda i: (0, 0)),
                                       pl.BlockSpec((1, POOL_WIDTH), lambda i: (0, 0))],
        out_specs=pl.BlockSpec((tt, POOL_WIDTH), lambda i: (i, 0)),
        out_shape=jax.ShapeDtypeStruct((T, POOL_WIDTH), F32),
        compiler_params=_params(),
    )(proj, proj, wbd, scale)


def _pool_bwd(proj, dymix, wbd, scale):
    T = proj.shape[0]
    tt = _tile(T, 512)
    hb = tt // POOL_HALO
    nblk = T // tt
    n = tt + POOL_HALO

    def body(a_ref, halo_ref, dy_ref, dyn_ref, w_ref, sc_ref, da_ref, dw_ref, dsc_ref):
        i = pl.program_id(0)
        halo = jnp.where(i > 0, halo_ref[...], 0.0)
        d = _pool_diff(a_ref[...], halo, i * tt, tt)
        db = d.astype(BF16)
        wv = w_ref[...]
        sc = sc_ref[...]
        dy = dy_ref[...]
        dys = dy * sc

        @pl.when(i == 0)
        def _():
            dw_ref[...] = jnp.zeros_like(dw_ref)
            dsc_ref[...] = jnp.zeros_like(dsc_ref)

        dsc_ref[...] += jnp.sum(dy * _nn(db, wv), axis=0, keepdims=True)
        dw_ref[...] += _tn(db, dys.astype(BF16))
        dyn = jnp.where(i < nblk - 1, dyn_ref[...], 0.0) * sc
        dd = _nt(jnp.concatenate([dys, dyn], axis=0).astype(BF16), wv)
        grp = lax.broadcasted_iota(jnp.int32, (n, POOL_WIDTH), 1) // POOL_GW
        t_glob = lax.broadcasted_iota(jnp.int32, (n, POOL_WIDTH), 0) + i * tt
        e = dd / _pool_count(t_glob, grp)
        r2 = e + pltpu.roll(e, n - 1, 0)
        r4 = r2 + pltpu.roll(r2, n - 2, 0)
        r8 = r4 + pltpu.roll(r4, n - 4, 0)
        r16 = r8 + pltpu.roll(r8, n - 8, 0)
        da_ref[...] = (_pool_select(r2, r4, r8, r16, grp) - dd)[:tt].astype(BF16)

    return pl.pallas_call(
        body, name="pool_bwd", grid=(nblk,),
        in_specs=_pool_specs(T, tt) + [
            pl.BlockSpec((tt, POOL_WIDTH), lambda i: (i, 0)),
            pl.BlockSpec((POOL_HALO, POOL_WIDTH), lambda i: (jnp.minimum((i + 1) * hb, T // POOL_HALO - 1), 0)),
            pl.BlockSpec((POOL_WIDTH, POOL_WIDTH), lambda i: (0, 0)), pl.BlockSpec((1, POOL_WIDTH), lambda i: (0, 0))],
        out_specs=[pl.BlockSpec((tt, POOL_WIDTH), lambda i: (i, 0)),
                   pl.BlockSpec((POOL_WIDTH, POOL_WIDTH), lambda i: (0, 0)),
                   pl.BlockSpec((1, POOL_WIDTH), lambda i: (0, 0))],
        out_shape=[jax.ShapeDtypeStruct((T, POOL_WIDTH), BF16),
                   jax.ShapeDtypeStruct((POOL_WIDTH, POOL_WIDTH), F32),
                   jax.ShapeDtypeStruct((1, POOL_WIDTH), F32)],
        compiler_params=_params(),
    )(proj, proj, dymix, dymix, wbd, scale)


def _head_select(stacked, grp):
    out = jnp.where(grp == 0, stacked[0:CHUNK], 0.0)
    for h in range(1, SG_HEADS):
        out = out + jnp.where(grp == h, stacked[h * CHUNK:(h + 1) * CHUNK], 0.0)
    return out


def _sg_specs(tt):
    return [pl.BlockSpec((tt, SG_WIDTH), lambda i: (i, 1)), pl.BlockSpec((tt, SG_WIDTH), lambda i: (i, 2))]


def _sg_fwd(proj, wm, bias, g):
    T = proj.shape[0]
    tt = _tile(T, 512)

    def body(u_ref, v_ref, wm_ref, b_ref, g_ref, y_ref):
        zu = _gelu(u_ref[...])
        vn, _, _ = _rms_fwd(_gelu(v_ref[...]), g_ref[...])
        grp = lax.broadcasted_iota(jnp.int32, (CHUNK, SG_WIDTH), 1) // SB_HD
        for n in range(tt // CHUNK):
            rows = slice(n * CHUNK, (n + 1) * CHUNK)
            sv = _head_select(_nn(wm_ref[...], vn[rows].astype(BF16)), grp) + b_ref[...]
            y_ref[rows, :] = zu[rows] * sv

    return pl.pallas_call(
        body, name="sg_fwd", grid=(T // tt,),
        in_specs=_sg_specs(tt) + [pl.BlockSpec((SG_HEADS * CHUNK, CHUNK), lambda i: (0, 0)),
                                  pl.BlockSpec((CHUNK, SG_WIDTH), lambda i: (0, 0)),
                                  pl.BlockSpec((1, SG_WIDTH), lambda i: (0, 0))],
        out_specs=pl.BlockSpec((tt, SG_WIDTH), lambda i: (i, 0)),
        out_shape=jax.ShapeDtypeStruct((T, SG_WIDTH), F32),
        compiler_params=_params(),
    )(proj, proj, wm, bias, g)


def _sg_bwd(proj, dymix, wm, wmt, bias, g, hosted=None):
    T = proj.shape[0]
    tt = _tile(T, 512)
    nblk = T // tt

    def body(*refs):
        i = pl.program_id(0)
        (u_ref, v_ref, dy_ref, wm_ref, wmt_ref, b_ref, g_ref, du_ref, dv_ref, dw_ref, db_ref, dg_ref,
         dvn_ref, dbias_ref), start, wait = _host(hosted, refs, 7, 5, i == 0, i == nblk - 1)
        start()
        up, vp = u_ref[...], v_ref[...]
        gv = g_ref[...]
        zu, zv = _gelu(up), _gelu(vp)
        vn, xhat, r = _rms_fwd(zv, gv)
        gu = _gelu_grad(up)
        grp = lax.broadcasted_iota(jnp.int32, (CHUNK, SG_WIDTH), 1) // SB_HD

        @pl.when(i == 0)
        def _():
            dw_ref[...] = jnp.zeros_like(dw_ref)
            dbias_ref[...] = jnp.zeros_like(dbias_ref)
            dg_ref[...] = jnp.zeros_like(dg_ref)

        for n in range(tt // CHUNK):
            rows = slice(n * CHUNK, (n + 1) * CHUNK)
            vc = vn[rows].astype(BF16)
            sv = _head_select(_nn(wm_ref[...], vc), grp) + b_ref[...]
            dy = dy_ref[rows, :]
            du_ref[rows, :] = (dy * sv * gu[rows]).astype(BF16)
            dsv = dy * zu[rows]
            dsvb = dsv.astype(BF16)
            dvn_ref[rows, :] = _head_select(_nn(wmt_ref[...], dsvb), grp)
            stacked = jnp.concatenate([jnp.where(grp == h, dsv, 0.0) for h in range(SG_HEADS)], axis=0)
            dw_ref[...] += _nt(stacked.astype(BF16), vc)
            dbias_ref[...] += dsv

        dzv, dgrow = _rms_bwd(dvn_ref[...], xhat, r, gv)
        dg_ref[...] += jnp.sum(dgrow, axis=0, keepdims=True)
        dv_ref[...] = (dzv * _gelu_grad(vp)).astype(BF16)

        @pl.when(i == nblk - 1)
        def _():
            t_i = lax.broadcasted_iota(jnp.int32, (SG_HEADS * CHUNK, CHUNK), 0) % CHUNK
            s_i = lax.broadcasted_iota(jnp.int32, (SG_HEADS * CHUNK, CHUNK), 1)
            dw_ref[...] = jnp.where(s_i <= t_i, dw_ref[...], 0.0)
            lane = lax.broadcasted_iota(jnp.int32, (CHUNK, LANES), 1)
            acc = jnp.zeros((CHUNK, LANES), F32)
            for h in range(SG_HEADS):
                tot = jnp.sum(jnp.where(grp == h, dbias_ref[...], 0.0), axis=1, keepdims=True)
                acc = acc + jnp.where(lane == h, tot, 0.0)
            db_ref[...] = acc

        wait()

    h_ins = hosted.ins if hosted else []
    res = pl.pallas_call(
        body, name="sg_bwd_hosting" if hosted else "sg_bwd", grid=(nblk,),
        in_specs=_sg_specs(tt) + [pl.BlockSpec((tt, SG_WIDTH), lambda i: (i, 1)),
                                  pl.BlockSpec((SG_HEADS * CHUNK, CHUNK), lambda i: (0, 0)),
                                  pl.BlockSpec((SG_HEADS * CHUNK, CHUNK), lambda i: (0, 0)),
                                  pl.BlockSpec((CHUNK, SG_WIDTH), lambda i: (0, 0)),
                                  pl.BlockSpec((1, SG_WIDTH), lambda i: (0, 0))] + [HBM_SPEC] * len(h_ins),
        out_specs=[pl.BlockSpec((tt, SG_WIDTH), lambda i: (i, 0)), pl.BlockSpec((tt, SG_WIDTH), lambda i: (i, 0)),
                   pl.BlockSpec((SG_HEADS * CHUNK, CHUNK), lambda i: (0, 0)),
                   pl.BlockSpec((CHUNK, LANES), lambda i: (0, 0)), pl.BlockSpec((1, SG_WIDTH), lambda i: (0, 0))]
        + [HBM_SPEC] * len(h_ins),
        out_shape=[jax.ShapeDtypeStruct((T, SG_WIDTH), BF16), jax.ShapeDtypeStruct((T, SG_WIDTH), BF16),
                   jax.ShapeDtypeStruct((SG_HEADS * CHUNK, CHUNK), F32),
                   jax.ShapeDtypeStruct((CHUNK, LANES), F32), jax.ShapeDtypeStruct((1, SG_WIDTH), F32)]
        + (hosted.out_shapes if hosted else []),
        scratch_shapes=[pltpu.VMEM((tt, SG_WIDTH), F32), pltpu.VMEM((CHUNK, SG_WIDTH), F32)]
        + (hosted.sems() if hosted else []),
        compiler_params=_params(has_side_effects=hosted is not None),
    )(proj, proj, dymix, wm, wmt, bias, g, *h_ins)
    return res[:5], res[5:]


def _split_dot(x, u):
    hi = x.astype(BF16)
    lo = (x - hi.astype(F32)).astype(BF16)
    return _nn(hi, u) + _nn(lo, u)


def _sb_logits(z):
    lb = jnp.minimum(z, 0.0) - jnp.log(1.0 + jnp.exp(-jnp.abs(z)))
    return lb, lb - z


ATTN_STRIP = 32
ATTN_SUBS = 2


def _by_strips(n_rows, fn):
    parts = None
    for r in range(0, n_rows, ATTN_STRIP):
        res = fn(slice(r, r + ATTN_STRIP))
        parts = [[v] for v in res] if parts is None else [p + [v] for p, v in zip(parts, res)]
    return [jnp.concatenate(p, axis=0) for p in parts]


def _attn_qkv_specs(tq, T):
    base = (IN_COLS - 3 * SB_WIDTH - QKV_OFF) // LANES
    nb = SB_WIDTH // LANES
    return [pl.BlockSpec((tq, LANES), lambda p, i: (i, base + p)),
            pl.BlockSpec((T, LANES), lambda p, i: (0, base + nb + p)),
            pl.BlockSpec((T, LANES), lambda p, i: (0, base + 2 * nb + p))]


class _Hosted:
    def __init__(self, ins, out_shapes, n_sems, copies, in_place=False):
        self.ins, self.out_shapes, self.n_sems, self.copies = ins, out_shapes, n_sems, copies
        self.in_place = in_place

    @property
    def n(self):
        return len(self.ins)

    def aliases(self, n_in, n_out):
        return {n_in + k: n_out + k for k in range(self.n)} if self.in_place else {}

    def sems(self):
        return [pltpu.SemaphoreType.DMA((self.n_sems,)), pltpu.SemaphoreType.DMA((self.n_sems,))]

    def start(self, src, dst, ssem, rsem):
        for send, _ in self.copies(src, dst, ssem, rsem):
            send.start()

    def wait(self, src, dst, ssem, rsem):
        for send, recv in self.copies(src, dst, ssem, rsem):
            recv.wait_recv()
            send.wait_send()


def _host(hosted, refs, n_in, n_out, first, last):
    if hosted is None:
        return refs, lambda: None, lambda: None
    n = hosted.n
    own_in, h_in = refs[:n_in], refs[n_in:n_in + n]
    own_out, h_out = refs[n_in + n:n_in + n + n_out], refs[n_in + n + n_out:n_in + 2 * n + n_out]
    rest = refs[n_in + 2 * n + n_out:]
    ssem, rsem = rest[-2:]

    def start():
        if first is True:
            hosted.start(h_in, h_out, ssem, rsem)
        else:
            pl.when(first)(lambda: hosted.start(h_in, h_out, ssem, rsem))

    def wait():
        if last is True:
            hosted.wait(h_in, h_out, ssem, rsem)
        else:
            pl.when(last)(lambda: hosted.wait(h_in, h_out, ssem, rsem))

    return own_in + own_out + rest[:-2], start, wait


def _attn_fwd(qkv, hosted=None):
    T = qkv.shape[0]
    tk = _tile(T, ATTN_TILE)
    n_sub = ATTN_SUBS if T % (ATTN_SUBS * tk) == 0 else 1
    tq = n_sub * tk
    n_p, nq = SB_WIDTH // LANES, T // tq

    def body(*refs):
        p, i = pl.program_id(0), pl.program_id(1)
        (q_ref, k_ref, v_ref, o_ref), start, wait = _host(
            hosted, refs, 3, 1, jnp.logical_and(p == 0, i == 0), jnp.logical_and(p == n_p - 1, i == nq - 1))
        start()
        lane = lax.broadcasted_iota(jnp.int32, (tk, LANES), 1)
        row = lax.broadcasted_iota(jnp.int32, (tk, tk), 0)
        col = lax.broadcasted_iota(jnp.int32, (tk, tk), 1)
        after = jnp.where(row > col, 1.0, 0.0).astype(BF16)
        valid = col < row
        qh = {}
        for sb in range(n_sub):
            q = q_ref[sb * tk:(sb + 1) * tk, :].astype(F32)
            for hh in range(2):
                qh[(sb, hh)] = jnp.where((lane // SB_HD) == hh, q * SB_SCALE, 0.0).astype(BF16)

        def tiles(todo, state):
            chains = [(n, hh) for n in range(len(todo)) for hh in range(2)]
            kv = []
            for _, j, _ in todo:
                ks = pl.ds(pl.multiple_of(j * tk, tk), tk)
                kv.append((k_ref[ks, :], v_ref[ks, :]))
            z = {(n, hh): _nt(qh[(todo[n][0], hh)], kv[n][0]) for n, hh in chains}
            lb, lmb, lm_sum = {}, {}, {}
            for n, hh in chains:
                def logits(rows, z=z[(n, hh)], mask=todo[n][2]):
                    lb, lm = _sb_logits(z[rows])
                    if mask is not None:
                        lm = jnp.where(mask[rows], lm, 0.0)
                    return lb, lm.astype(BF16), jnp.sum(lm, axis=1, keepdims=True)

                lb[(n, hh)], lmb[(n, hh)], lm_sum[(n, hh)] = _by_strips(tk, logits)
            x = {c: _nn(lmb[c], after) for c in chains}
            new = dict(state)
            for n, hh in chains:
                key = (todo[n][0], hh)
                carry, acc = new[key]

                def weights(rows, lb=lb[(n, hh)], x=x[(n, hh)], carry=carry, mask=todo[n][2]):
                    a = jnp.exp(lb[rows] + x[rows] + carry[rows])
                    if mask is not None:
                        a = jnp.where(mask[rows], a, 0.0)
                    return (a.astype(BF16),)

                (ab,) = _by_strips(tk, weights)
                new[key] = (carry + lm_sum[(n, hh)], acc + _nn(ab, kv[n][1]))
            return new

        def live(state, sb):
            return jnp.maximum(jnp.max(state[(sb, 0)][0]), jnp.max(state[(sb, 1)][0]))

        first = n_sub * i
        zero = (jnp.zeros((tk, 1), F32), jnp.zeros((tk, LANES), F32))
        todo = []
        for sb in range(n_sub):
            gate = jnp.broadcast_to(first > 0, (tk, tk)) if sb == 0 else None
            todo += [(sb, first + sb, valid), (sb, jnp.maximum(first + sb - 1, 0), gate)]
        state = tiles(todo, {(sb, hh): zero for sb in range(n_sub) for hh in range(2)})
        for sb in range(n_sub):
            def cond(st):
                return jnp.logical_and(st[0] >= 0, st[2] > UNDERFLOW)

            def step(st, sb=sb):
                mine = tiles([(sb, st[0], None)], st[1])
                return st[0] - 1, mine, live(mine, sb)

            mine = {k: v for k, v in state.items() if k[0] == sb}
            _, mine, _ = lax.while_loop(cond, step, (first + sb - 2, mine, live(mine, sb)))
            o_ref[sb * tk:(sb + 1) * tk, :] = jnp.where(lane < SB_HD, mine[(sb, 0)][1], mine[(sb, 1)][1])
        wait()

    h_ins = hosted.ins if hosted else []
    res = pl.pallas_call(
        body, name="attn_fwd_hosting" if hosted else "attn_fwd", grid=(n_p, nq),
        in_specs=_attn_qkv_specs(tq, T) + [HBM_SPEC] * len(h_ins),
        out_specs=[pl.BlockSpec((tq, LANES), lambda p, i: (i, p))] + [HBM_SPEC] * len(h_ins),
        out_shape=[jax.ShapeDtypeStruct((T, SB_WIDTH), F32)] + (hosted.out_shapes if hosted else []),
        input_output_aliases=hosted.aliases(3, 1) if hosted else {},
        scratch_shapes=hosted.sems() if hosted else [],
        compiler_params=_params(has_side_effects=hosted is not None),
    )(qkv, qkv, qkv, *h_ins)
    return res[0], res[1:]


def _attn_bwd(qkv, o, dymix, hosted=None):
    T = qkv.shape[0]
    tk = _tile(T, ATTN_TILE)
    n_sub = ATTN_SUBS if T % (ATTN_SUBS * tk) == 0 else 1
    tq = n_sub * tk
    n_p, nq = SB_WIDTH // LANES, T // tq
    yc_blk = (POOL_WIDTH + SG_WIDTH) // LANES

    def body(*refs):
        p, i = pl.program_id(0), pl.program_id(1)
        (q_ref, k_ref, v_ref, o_ref, do_ref, dq_ref, dk_ref, dv_ref), start, wait = _host(
            hosted, refs, 5, 3, jnp.logical_and(p == 0, i == 0), jnp.logical_and(p == n_p - 1, i == nq - 1))
        start()
        lane = lax.broadcasted_iota(jnp.int32, (tk, LANES), 1)
        row = lax.broadcasted_iota(jnp.int32, (tk, tk), 0)
        col = lax.broadcasted_iota(jnp.int32, (tk, tk), 1)
        after = jnp.where(row > col, 1.0, 0.0).astype(BF16)
        from_here = jnp.where(row >= col, 1.0, 0.0).astype(BF16)
        from_here2 = jnp.concatenate([from_here, from_here], axis=0)
        valid = col < row

        @pl.when(i == 0)
        def _():
            dk_ref[...] = jnp.zeros_like(dk_ref)
            dv_ref[...] = jnp.zeros_like(dv_ref)

        qh, dohb, delta = {}, {}, {}
        for sb in range(n_sub):
            rows = slice(sb * tk, (sb + 1) * tk)
            q, ov, dov = q_ref[rows, :].astype(F32), o_ref[rows, :], do_ref[rows, :]
            for hh in range(2):
                head = (lane // SB_HD) == hh
                qh[(sb, hh)] = jnp.where(head, q * SB_SCALE, 0.0).astype(BF16)
                dohb[(sb, hh)] = jnp.where(head, dov, 0.0).astype(BF16)
                delta[(sb, hh)] = jnp.sum(dohb[(sb, hh)].astype(F32) * ov, axis=1, keepdims=True)

        def tiles(todo, state):
            chains = [(n, hh) for n in range(len(todo)) for hh in range(2)]
            kv, where = [], []
            for _, j, _ in todo:
                ks = pl.ds(pl.multiple_of(j * tk, tk), tk)
                where.append(ks)
                kv.append((k_ref[ks, :], v_ref[ks, :]))
            z = {(n, hh): _nt(qh[(todo[n][0], hh)], kv[n][0]) for n, hh in chains}
            da = {(n, hh): _nt(dohb[(todo[n][0], hh)], kv[n][1]) for n, hh in chains}
            lb, lmb, lm_sum = {}, {}, {}
            for n, hh in chains:
                def logits(rows, z=z[(n, hh)], mask=todo[n][2]):
                    lb, lm = _sb_logits(z[rows])
                    if mask is not None:
                        lm = jnp.where(mask[rows], lm, 0.0)
                    return lb, lm.astype(BF16), jnp.sum(lm, axis=1, keepdims=True)

                lb[(n, hh)], lmb[(n, hh)], lm_sum[(n, hh)] = _by_strips(tk, logits)
            x = {c: _nn(lmb[c], after) for c in chains}
            c_a = {k: v[0] for k, v in state.items()}
            ab, g, g_split, g_sum = {}, {}, {}, {}
            for n, hh in chains:
                key = (todo[n][0], hh)

                def weights(rows, lb=lb[(n, hh)], x=x[(n, hh)], da=da[(n, hh)], c_a=c_a[key], mask=todo[n][2]):
                    a = jnp.exp(lb[rows] + x[rows] + c_a[rows])
                    if mask is not None:
                        a = jnp.where(mask[rows], a, 0.0)
                    ab = a.astype(BF16)
                    g = da[rows] * ab.astype(F32)
                    hi = g.astype(BF16)
                    lo = (g - hi.astype(F32)).astype(BF16)
                    return ab, g, jnp.concatenate([hi, lo], axis=1), jnp.sum(g, axis=1, keepdims=True)

                ab[(n, hh)], g[(n, hh)], g_split[(n, hh)], g_sum[(n, hh)] = _by_strips(tk, weights)
                c_a[key] = c_a[key] + lm_sum[(n, hh)]
            right = {c: _nn(g_split[c], from_here2) for c in chains}
            c_r = {k: v[1] for k, v in state.items()}
            dzb = {}
            for n, hh in chains:
                key = (todo[n][0], hh)

                def logit_grads(rows, lb=lb[(n, hh)], g=g[(n, hh)], right=right[(n, hh)], c_r=c_r[key],
                                delta=delta[key], mask=todo[n][2]):
                    sig = jnp.exp(lb[rows])
                    left = delta[rows] - (c_r[rows] + right[rows])
                    dz = g[rows] * (1.0 - sig) - left * sig
                    if mask is not None:
                        dz = jnp.where(mask[rows], dz, 0.0)
                    return (dz.astype(BF16),)

                (dzb[(n, hh)],) = _by_strips(tk, logit_grads)
                c_r[key] = c_r[key] + g_sum[(n, hh)]
            dqa = {k: v[2] for k, v in state.items()}
            for n in range(len(todo)):
                sb = todo[n][0]
                dk_ref[where[n], :] += _tn(dzb[(n, 0)], qh[(sb, 0)]) + _tn(dzb[(n, 1)], qh[(sb, 1)])
                dv_ref[where[n], :] += _tn(ab[(n, 0)], dohb[(sb, 0)]) + _tn(ab[(n, 1)], dohb[(sb, 1)])
                for hh in range(2):
                    dqa[(sb, hh)] = dqa[(sb, hh)] + _nn(dzb[(n, hh)], kv[n][0])
            return {k: (c_a[k], c_r[k], dqa[k]) for k in state}

        def live(state, sb):
            return jnp.maximum(jnp.max(state[(sb, 0)][0]), jnp.max(state[(sb, 1)][0]))

        first = n_sub * i
        zero = (jnp.zeros((tk, 1), F32), jnp.zeros((tk, 1), F32), jnp.zeros((tk, LANES), F32))
        todo = []
        for sb in range(n_sub):
            gate = jnp.broadcast_to(first > 0, (tk, tk)) if sb == 0 else None
            todo += [(sb, first + sb, valid), (sb, jnp.maximum(first + sb - 1, 0), gate)]
        state = tiles(todo, {(sb, hh): zero for sb in range(n_sub) for hh in range(2)})
        for sb in range(n_sub):
            def cond(st):
                return jnp.logical_and(st[0] >= 0, st[2] > UNDERFLOW)

            def step(st, sb=sb):
                mine = tiles([(sb, st[0], None)], st[1])
                return st[0] - 1, mine, live(mine, sb)

            mine = {k: v for k, v in state.items() if k[0] == sb}
            _, mine, _ = lax.while_loop(cond, step, (first + sb - 2, mine, live(mine, sb)))
            dq_ref[sb * tk:(sb + 1) * tk, :] = (
                jnp.where(lane < SB_HD, mine[(sb, 0)][2], mine[(sb, 1)][2]) * SB_SCALE).astype(BF16)
        wait()

    h_ins = hosted.ins if hosted else []
    res = pl.pallas_call(
        body, name="attn_bwd_hosting" if hosted else "attn_bwd", grid=(n_p, nq),
        in_specs=_attn_qkv_specs(tq, T) + [pl.BlockSpec((tq, LANES), lambda p, i: (i, p)),
                                           pl.BlockSpec((tq, LANES), lambda p, i: (i, yc_blk + p))]
        + [HBM_SPEC] * len(h_ins),
        out_specs=[pl.BlockSpec((tq, LANES), lambda p, i: (i, p)), pl.BlockSpec((T, LANES), lambda p, i: (0, p)),
                   pl.BlockSpec((T, LANES), lambda p, i: (0, p))] + [HBM_SPEC] * len(h_ins),
        out_shape=[jax.ShapeDtypeStruct((T, SB_WIDTH), BF16)] + [jax.ShapeDtypeStruct((T, SB_WIDTH), F32)] * 2
        + (hosted.out_shapes if hosted else []),
        scratch_shapes=hosted.sems() if hosted else [],
        compiler_params=_params(has_side_effects=hosted is not None),
    )(qkv, qkv, qkv, o, dymix, *h_ins)
    return res[:3], res[3:]


def _outproj_fwd(x, ya, yb, yc, w, hosted=None):
    T, D = x.shape
    tt = _tile(T, 512)
    nt = T // tt

    def body(*refs):
        i = pl.program_id(0)
        (x_ref, ya_ref, yb_ref, yc_ref, w_ref, x1_ref, ymix_ref), start, wait = _host(
            hosted, refs, 5, 2, i == 0, i == nt - 1)
        start()
        ymix_ref[:, 0:POOL_WIDTH] = ya_ref[...].astype(BF16)
        ymix_ref[:, POOL_WIDTH:POOL_WIDTH + SG_WIDTH] = yb_ref[...].astype(BF16)
        ymix_ref[:, POOL_WIDTH + SG_WIDTH:] = yc_ref[...].astype(BF16)
        x1_ref[...] = x_ref[...] + _nn(ymix_ref[...], w_ref[...])
        wait()

    row = lambda width: pl.BlockSpec((tt, width), lambda i: (i, 0))
    h_ins = hosted.ins if hosted else []
    res = pl.pallas_call(
        body, name="outproj_fwd_hosting" if hosted else "outproj_fwd", grid=(nt,),
        in_specs=[row(D), row(POOL_WIDTH), row(SG_WIDTH), row(SB_WIDTH), pl.BlockSpec((D, D), lambda i: (0, 0))]
        + [HBM_SPEC] * len(h_ins),
        out_specs=[row(D), row(D)] + [HBM_SPEC] * len(h_ins),
        out_shape=[jax.ShapeDtypeStruct((T, D), F32), jax.ShapeDtypeStruct((T, D), BF16)]
        + (hosted.out_shapes if hosted else []),
        input_output_aliases=hosted.aliases(5, 2) if hosted else {},
        scratch_shapes=hosted.sems() if hosted else [],
        compiler_params=_params(has_side_effects=hosted is not None),
    )(x, ya, yb, yc, w, *h_ins)
    return res[:2], res[2:]


def _nt_matmul(a, w):
    T, N = a.shape
    K = w.shape[0]
    tt = _tile(T, 512)

    def body(a_ref, w_ref, o_ref):
        o_ref[...] = _nt(a_ref[...].astype(BF16), w_ref[...])

    return pl.pallas_call(
        body, name="nt_matmul", grid=(T // tt,),
        in_specs=[pl.BlockSpec((tt, N), lambda i: (i, 0)), pl.BlockSpec((K, N), lambda i: (0, 0))],
        out_specs=pl.BlockSpec((tt, K), lambda i: (i, 0)),
        out_shape=jax.ShapeDtypeStruct((T, K), F32),
        compiler_params=_params(),
    )(a, w)


def _tn_matmul(a, b, name, n_split=1, hosted=None):
    T, K = a.shape
    N = b.shape[1]
    tk = _tile(K, 1024)
    tn = _tile(N // n_split, 1024)
    tt = _tile(T, 2048)
    nper = N // n_split // tn
    nk, nn, nt = K // tk, N // tn, T // tt

    def body(*refs):
        k, n, t = pl.program_id(0), pl.program_id(1), pl.program_id(2)
        (a_ref, b_ref, o_ref), start, wait = _host(
            hosted, refs, 2, 1, jnp.logical_and(jnp.logical_and(k == 0, n == 0), t == 0),
            jnp.logical_and(jnp.logical_and(k == nk - 1, n == nn - 1), t == nt - 1))
        start()

        @pl.when(t == 0)
        def _():
            o_ref[...] = jnp.zeros_like(o_ref)

        o_ref[...] += _tn(a_ref[...], b_ref[...].astype(BF16))
        wait()

    h_ins = hosted.ins if hosted else []
    res = pl.pallas_call(
        body, name=name + "_hosting" if hosted else name, grid=(nk, nn, nt),
        in_specs=[pl.BlockSpec((tt, tk), lambda k, n, t: (t, k)), pl.BlockSpec((tt, tn), lambda k, n, t: (t, n))]
        + [HBM_SPEC] * len(h_ins),
        out_specs=[pl.BlockSpec((None, tk, tn), lambda k, n, t: (n // nper, k, n % nper))] + [HBM_SPEC] * len(h_ins),
        out_shape=[jax.ShapeDtypeStruct((n_split, K, N // n_split), F32)] + (hosted.out_shapes if hosted else []),
        scratch_shapes=hosted.sems() if hosted else [],
        compiler_params=_params(has_side_effects=hosted is not None),
    )(a, b, *h_ins)
    return (res[0], res[1:]) if hosted else res[0]


def _mlp_fwd(x, g, w_up, w_down, hosted=None):
    T, D = x.shape
    F = w_up.shape[1]
    tt = _tile(T, 1024)
    fc = _tile(F, MLP_CHUNK)
    nc = F // fc
    nt = T // tt

    def body(*refs):
        i, c = pl.program_id(0), pl.program_id(1)
        (x_ref, g_ref, wu_ref, wd_ref, y_ref, h_ref, u_ref, a_ref), start, wait = _host(
            hosted, refs, 4, 4, jnp.logical_and(i == 0, c == 0), jnp.logical_and(i == nt - 1, c == nc - 1))
        start()

        @pl.when(c == 0)
        def _():
            xv = x_ref[...]
            h, _, _ = _rms_fwd(xv, g_ref[...])
            h_ref[...] = h.astype(BF16)
            y_ref[...] = xv

        u = _nn(h_ref[...], wu_ref[...])
        u_ref[...] = u.astype(BF16)
        a = jnp.square(jnp.maximum(u, 0.0)).astype(BF16)
        a_ref[...] = a
        y_ref[...] += _nn(a, wd_ref[...])
        wait()

    h_ins = hosted.ins if hosted else []
    res = pl.pallas_call(
        body, name="mlp_fwd_hosting" if hosted else "mlp_fwd", grid=(nt, nc),
        in_specs=[pl.BlockSpec((tt, D), lambda i, c: (i, 0)), pl.BlockSpec((1, D), lambda i, c: (0, 0)),
                  pl.BlockSpec((D, fc), lambda i, c: (0, c)), pl.BlockSpec((fc, D), lambda i, c: (c, 0))]
        + [HBM_SPEC] * len(h_ins),
        out_specs=[pl.BlockSpec((tt, D), lambda i, c: (i, 0)), pl.BlockSpec((tt, D), lambda i, c: (i, 0)),
                   pl.BlockSpec((tt, fc), lambda i, c: (i, c)), pl.BlockSpec((tt, fc), lambda i, c: (i, c))]
        + [HBM_SPEC] * len(h_ins),
        out_shape=[jax.ShapeDtypeStruct((T, D), F32), jax.ShapeDtypeStruct((T, D), BF16),
                   jax.ShapeDtypeStruct((T, F), BF16), jax.ShapeDtypeStruct((T, F), BF16)]
        + (hosted.out_shapes if hosted else []),
        scratch_shapes=hosted.sems() if hosted else [],
        compiler_params=_params(has_side_effects=hosted is not None),
    )(x, g, w_up, w_down, *h_ins)
    return res[:4], res[4:]


def _mlp_bwd(dy, x, g, u, w_up, w_down, hosted=None):
    T, D = x.shape
    F = w_up.shape[1]
    tt = _tile(T, 1024)
    fc = _tile(F, MLP_CHUNK)
    nc = F // fc
    nt = T // tt

    def body(*refs):
        i, c = pl.program_id(0), pl.program_id(1)
        (dy_ref, x_ref, g_ref, u_ref, wu_ref, wd_ref, dx_ref, du_ref, dg_ref, dyb_ref, dh_ref), start, wait = _host(
            hosted, refs, 6, 3, jnp.logical_and(i == 0, c == 0), jnp.logical_and(i == nt - 1, c == nc - 1))
        start()

        @pl.when(c == 0)
        def _():
            dyb_ref[...] = dy_ref[...].astype(BF16)
            dh_ref[...] = jnp.zeros_like(dh_ref)

        @pl.when(jnp.logical_and(i == 0, c == 0))
        def _():
            dg_ref[...] = jnp.zeros_like(dg_ref)

        da = _nt(dyb_ref[...], wd_ref[...])
        du = (da * (2.0 * jnp.maximum(u_ref[...].astype(F32), 0.0))).astype(BF16)
        du_ref[...] = du
        dh_ref[...] += _nt(du, wu_ref[...])

        @pl.when(c == nc - 1)
        def _():
            gv = g_ref[...]
            _, xhat, r = _rms_fwd(x_ref[...], gv)
            dx, dgrow = _rms_bwd(dh_ref[...], xhat, r, gv)
            dx_ref[...] = dy_ref[...] + dx
            dg_ref[...] += jnp.sum(dgrow, axis=0, keepdims=True)

        wait()

    h_ins = hosted.ins if hosted else []
    res = pl.pallas_call(
        body, name="mlp_bwd_hosting" if hosted else "mlp_bwd", grid=(nt, nc),
        in_specs=[pl.BlockSpec((tt, D), lambda i, c: (i, 0)), pl.BlockSpec((tt, D), lambda i, c: (i, 0)),
                  pl.BlockSpec((1, D), lambda i, c: (0, 0)), pl.BlockSpec((tt, fc), lambda i, c: (i, c)),
                  pl.BlockSpec((D, fc), lambda i, c: (0, c)), pl.BlockSpec((fc, D), lambda i, c: (c, 0))]
        + [HBM_SPEC] * len(h_ins),
        out_specs=[pl.BlockSpec((tt, D), lambda i, c: (i, 0)), pl.BlockSpec((tt, fc), lambda i, c: (i, c)),
                   pl.BlockSpec((1, D), lambda i, c: (0, 0))] + [HBM_SPEC] * len(h_ins),
        out_shape=[jax.ShapeDtypeStruct((T, D), F32), jax.ShapeDtypeStruct((T, F), BF16),
                   jax.ShapeDtypeStruct((1, D), F32)] + (hosted.out_shapes if hosted else []),
        scratch_shapes=[pltpu.VMEM((tt, D), BF16), pltpu.VMEM((tt, D), F32)] + (hosted.sems() if hosted else []),
        compiler_params=_params(has_side_effects=hosted is not None),
    )(dy, x, g, u, w_up, w_down, *h_ins)
    return res[:3], res[3:]


def _loss_head(x, g, target):
    T, D = x.shape
    tt = _tile(T, 512)

    def body(x_ref, g_ref, t_ref, loss_ref, dx_ref, dg_ref):
        gv = g_ref[...]
        y, xhat, r = _rms_fwd(x_ref[...], gv)
        err = y - t_ref[...]
        dx, dgrow = _rms_bwd(err * (1.0 / D), xhat, r, gv)
        dx_ref[...] = dx

        @pl.when(pl.program_id(0) == 0)
        def _():
            loss_ref[...] = jnp.zeros_like(loss_ref)
            dg_ref[...] = jnp.zeros_like(dg_ref)

        loss_ref[...] += 0.5 * jnp.sum(jnp.mean(err * err, axis=-1, keepdims=True), axis=0, keepdims=True)
        dg_ref[...] += jnp.sum(dgrow, axis=0, keepdims=True)

    return pl.pallas_call(
        body, name="loss_head", grid=(T // tt,),
        in_specs=[pl.BlockSpec((tt, D), lambda i: (i, 0)), pl.BlockSpec((1, D), lambda i: (0, 0)),
                  pl.BlockSpec((tt, D), lambda i: (i, 0))],
        out_specs=[pl.BlockSpec((1, LANES), lambda i: (0, 0)), pl.BlockSpec((tt, D), lambda i: (i, 0)),
                   pl.BlockSpec((1, D), lambda i: (0, 0))],
        out_shape=[jax.ShapeDtypeStruct((1, LANES), F32), jax.ShapeDtypeStruct((T, D), F32),
                   jax.ShapeDtypeStruct((1, D), F32)],
        compiler_params=_params(),
    )(x, g, target)


def _rows(shape, pref=512):
    last = shape[-1]
    rows = 1
    for s in shape[:-1]:
        rows *= s
    tr = rows
    if rows * last > 256 * 1024:
        for cand in (pref, 256, 128, 64, 32, 16, 8):
            if rows % cand == 0:
                tr = cand
                break
    return rows, last, tr


def _elementwise(fn, name, ins, n_out, out_dtype=F32):
    shape = ins[0].shape
    rows, last, tr = _rows(shape)
    flat = [a.reshape(rows, last) for a in ins]
    n_in = len(ins)

    def body(*refs):
        res = fn(*[r[...] for r in refs[:n_in]])
        if n_out == 1:
            res = (res,)
        for r, v in zip(refs[n_in:], res):
            r[...] = v.astype(r.dtype)

    spec = pl.BlockSpec((tr, last), lambda i: (i, 0))
    outs = pl.pallas_call(
        body, name=name, grid=(rows // tr,),
        in_specs=[spec] * n_in, out_specs=[spec] * n_out,
        out_shape=[jax.ShapeDtypeStruct((rows, last), out_dtype)] * n_out,
        compiler_params=_params(),
    )(*flat)
    return [o.reshape(shape) for o in outs]


def _add_pair(g, o, c_idx):
    nq, R, C = g.shape
    h = R // 2
    tr = _tile(h, 512)
    nb = h // tr

    def body(c_ref, g_ref, o_ref, out_ref):
        out_ref[...] = g_ref[...] + o_ref[...]

    return pl.pallas_call(
        body, name="add_pair",
        grid_spec=pltpu.PrefetchScalarGridSpec(
            num_scalar_prefetch=1, grid=(nq, nb),
            in_specs=[pl.BlockSpec((None, tr, C), lambda q, i, c: (q, c[0] * nb + i, 0)),
                      pl.BlockSpec((None, tr, C), lambda q, i, c: (q, i, 0))],
            out_specs=pl.BlockSpec((None, tr, C), lambda q, i, c: (q, i, 0))),
        out_shape=jax.ShapeDtypeStruct((nq, h, C), F32),
        compiler_params=_params(),
    )(c_idx.astype(jnp.int32).reshape(1), g, o)


def _add_chips(p, r, q_idx):
    _, H, C = p.shape
    tr = _tile(H, 512)

    def body(q_ref, p_ref, r0_ref, r1_ref, r2_ref, out_ref):
        out_ref[...] = (p_ref[...] + r0_ref[...]) + (r1_ref[...] + r2_ref[...])

    def arrived(k):
        return pl.BlockSpec((None, tr, C), lambda i, q: (k, i, 0))

    return pl.pallas_call(
        body, name="add_chips",
        grid_spec=pltpu.PrefetchScalarGridSpec(
            num_scalar_prefetch=1, grid=(H // tr,),
            in_specs=[pl.BlockSpec((None, tr, C), lambda i, q: (q[0], i, 0)), arrived(0), arrived(1), arrived(2)],
            out_specs=pl.BlockSpec((tr, C), lambda i, q: (i, 0))),
        out_shape=jax.ShapeDtypeStruct((H, C), F32),
        compiler_params=_params(),
    )(q_idx.astype(jnp.int32).reshape(1), p, r, r, r)


def _adamw(w, g, m, v):
    m = ADAM_B1 * m + (1.0 - ADAM_B1) * g
    v = ADAM_B2 * v + (1.0 - ADAM_B2) * jnp.square(g)
    m_hat = m / (1.0 - ADAM_B1 ** ADAM_STEP)
    v_hat = v / (1.0 - ADAM_B2 ** ADAM_STEP)
    delta = -ADAM_LR * (m_hat / (jnp.sqrt(v_hat) + ADAM_EPS) + ADAM_WD * w)
    return delta, m, v


def _place():
    x, y, c = lax.axis_index("x"), lax.axis_index("y"), lax.axis_index("c")
    chips = [(1 - x, y), (x, 1 - y), (1 - x, 1 - y)]
    return x, y, c, chips


def _remote(src, dst, ssem, rsem, k, dev):
    return pltpu.make_async_remote_copy(src_ref=src, dst_ref=dst, send_sem=ssem.at[k], recv_sem=rsem.at[k],
                                        device_id=dev, device_id_type=MESH)


def _gather_weights(shards):
    n = len(shards)
    halves = [s.shape[1] // 2 for s in shards]

    def body(*refs):
        src, out = refs[:n], refs[n:2 * n]
        ssem, rsem = refs[2 * n:]
        x, y, c, chips = _place()
        me_q = 2 * x + y
        sib = (x, y, 1 - c)

        def half(a, q, cc):
            return out[a].at[q, :, pl.ds(cc * halves[a], halves[a]), :]

        first = []
        for a in range(n):
            mine = src[a].at[:, pl.ds(c * halves[a], halves[a]), :]
            for r, chip in enumerate(chips):
                first.append(_remote(mine, half(a, me_q, c), ssem, rsem, a * 3 + r, (*chip, c)))
        for cp in first:
            cp.start()
        passed = []
        for a in range(n):
            for r, chip in enumerate(chips):
                q = 2 * chip[0] + chip[1]
                k = a * 3 + r
                _remote(half(a, q, c), half(a, q, c), ssem, rsem, k, (*chip, c)).wait_recv()
                cp = _remote(half(a, q, c), half(a, q, c), ssem, rsem, 3 * n + k, sib)
                cp.start()
                passed.append(cp)
        for a in range(n):
            for r, chip in enumerate(chips):
                q = 2 * chip[0] + chip[1]
                _remote(half(a, q, 1 - c), half(a, q, 1 - c), ssem, rsem, 3 * n + a * 3 + r, sib).wait_recv()
        for cp in first + passed:
            cp.wait_send()

    return pl.pallas_call(
        body, name="gather_weights",
        in_specs=[HBM_SPEC] * n, out_specs=[HBM_SPEC] * n,
        out_shape=[jax.ShapeDtypeStruct((N_CHIPS,) + s.shape, s.dtype) for s in shards],
        scratch_shapes=[pltpu.SemaphoreType.DMA((6 * n,)), pltpu.SemaphoreType.DMA((6 * n,))],
        compiler_params=_params(has_side_effects=True),
    )(*shards)


def _gather_over_ici(shards):
    n = len(shards)
    halves = [s.shape[1] // 2 for s in shards]

    def copies(src, out, ssem, rsem):
        x, y, c, chips = _place()
        me_q = 2 * x + y
        res = []
        for a in range(n):
            rows = pl.ds(c * halves[a], halves[a])
            mine = src[a].at[:, rows, :]
            for r, chip in enumerate(chips):
                dev = (*chip, c)
                res.append((_remote(mine, out[a].at[me_q, :, rows, :], ssem, rsem, a * 3 + r, dev),
                            _remote(mine, out[a].at[2 * chip[0] + chip[1], :, rows, :], ssem, rsem, a * 3 + r, dev)))
        return res

    return _Hosted(list(shards), [jax.ShapeDtypeStruct((N_CHIPS,) + s.shape, s.dtype) for s in shards], 3 * n, copies)


def _pass_over_d2d(gathered):
    n = len(gathered)
    halves = [g.shape[2] // 2 for g in gathered]

    def copies(_, out, ssem, rsem):
        x, y, c, chips = _place()
        sib = (x, y, 1 - c)
        res = []
        for a in range(n):
            for r, chip in enumerate(chips):
                q = 2 * chip[0] + chip[1]
                mine = out[a].at[q, :, pl.ds(c * halves[a], halves[a]), :]
                theirs = out[a].at[q, :, pl.ds((1 - c) * halves[a], halves[a]), :]
                res.append((_remote(mine, mine, ssem, rsem, a * 3 + r, sib),
                            _remote(theirs, theirs, ssem, rsem, a * 3 + r, sib)))
        return res

    return _Hosted(list(gathered), [jax.ShapeDtypeStruct(g.shape, g.dtype) for g in gathered], 3 * n, copies,
                   in_place=True)


def _pass_to_sibling(gathered):
    n = len(gathered)
    halves = [g.shape[2] // 2 for g in gathered]

    def body(*refs):
        out = refs[n:2 * n]
        ssem, rsem = refs[2 * n:]
        x, y, c, chips = _place()
        sib = (x, y, 1 - c)

        def half(a, q, cc):
            return out[a].at[q, :, pl.ds(cc * halves[a], halves[a]), :]

        cps = []
        for a in range(n):
            for r, chip in enumerate(chips):
                q = 2 * chip[0] + chip[1]
                cps.append(_remote(half(a, q, c), half(a, q, c), ssem, rsem, a * 3 + r, sib))
        for cp in cps:
            cp.start()
        for a in range(n):
            for r, chip in enumerate(chips):
                q = 2 * chip[0] + chip[1]
                _remote(half(a, q, 1 - c), half(a, q, 1 - c), ssem, rsem, a * 3 + r, sib).wait_recv()
        for cp in cps:
            cp.wait_send()

    return pl.pallas_call(
        body, name="pass_to_sibling",
        in_specs=[HBM_SPEC] * n, out_specs=[HBM_SPEC] * n,
        out_shape=[jax.ShapeDtypeStruct(g.shape, g.dtype) for g in gathered],
        input_output_aliases={a: a for a in range(n)},
        scratch_shapes=[pltpu.SemaphoreType.DMA((3 * n,)), pltpu.SemaphoreType.DMA((3 * n,))],
        compiler_params=_params(has_side_effects=True),
    )(*gathered)


def _scatter_over_ici(parts):
    n = len(parts)

    def copies(src, out, ssem, rsem):
        x, y, c, chips = _place()
        res = []
        for a in range(n):
            for r, chip in enumerate(chips):
                cp = _remote(src[a].at[2 * chip[0] + chip[1]], out[a].at[r], ssem, rsem, a * 3 + r, (*chip, c))
                res.append((cp, cp))
        return res

    return _Hosted(list(parts), [jax.ShapeDtypeStruct((3,) + p.shape[1:], F32) for p in parts], 3 * n, copies)


def _swap_over_d2d(grads):
    n = len(grads)
    halves = [g.shape[1] // 2 for g in grads]

    def copies(src, out, ssem, rsem):
        x, y, c, _ = _place()
        res = []
        for a in range(n):
            cp = _remote(src[a].at[:, pl.ds((1 - c) * halves[a], halves[a]), :], out[a], ssem, rsem, a, (x, y, 1 - c))
            res.append((cp, cp))
        return res

    return _Hosted(list(grads), [jax.ShapeDtypeStruct((N_CHIPS, h, g.shape[2]), F32) for g, h in zip(grads, halves)],
                   n, copies)


def _swap_halves(grads):
    n = len(grads)
    halves = [g.shape[1] // 2 for g in grads]

    def body(*refs):
        src, out = refs[:n], refs[n:2 * n]
        ssem, rsem = refs[2 * n:]
        x, y, c, _ = _place()
        cps = [_remote(src[a].at[:, pl.ds((1 - c) * halves[a], halves[a]), :], out[a], ssem, rsem, a, (x, y, 1 - c))
               for a in range(n)]
        for cp in cps:
            cp.start()
        for cp in cps:
            cp.wait()

    return pl.pallas_call(
        body, name="swap_halves",
        in_specs=[HBM_SPEC] * n, out_specs=[HBM_SPEC] * n,
        out_shape=[jax.ShapeDtypeStruct((N_CHIPS, h, g.shape[2]), F32) for g, h in zip(grads, halves)],
        scratch_shapes=[pltpu.SemaphoreType.DMA((n,)), pltpu.SemaphoreType.DMA((n,))],
        compiler_params=_params(has_side_effects=True),
    )(*grads)


def _scatter_chips(parts):
    n = len(parts)

    def body(*refs):
        src, out = refs[:n], refs[n:2 * n]
        ssem, rsem = refs[2 * n:]
        x, y, c, chips = _place()
        cps = []
        for a in range(n):
            for r, chip in enumerate(chips):
                cps.append(_remote(src[a].at[2 * chip[0] + chip[1]], out[a].at[r], ssem, rsem, a * 3 + r, (*chip, c)))
        for cp in cps:
            cp.start()
        for cp in cps:
            cp.wait()

    return pl.pallas_call(
        body, name="scatter_chips",
        in_specs=[HBM_SPEC] * n, out_specs=[HBM_SPEC] * n,
        out_shape=[jax.ShapeDtypeStruct((3,) + p.shape[1:], F32) for p in parts],
        scratch_shapes=[pltpu.SemaphoreType.DMA((3 * n,)), pltpu.SemaphoreType.DMA((3 * n,))],
        compiler_params=_params(has_side_effects=True),
    )(*parts)


def _swap_reduced_over_d2d(reduced):
    n = len(reduced)

    def copies(src, out, ssem, rsem):
        x, y, c, _ = _place()
        res = []
        for a in range(n):
            cp = _remote(src[a], out[a], ssem, rsem, a, (x, y, 1 - c))
            res.append((cp, cp))
        return res

    return _Hosted(list(reduced), [jax.ShapeDtypeStruct(r.shape, F32) for r in reduced], n, copies)


def _swap_reduced(reduced):
    n = len(reduced)

    def body(*refs):
        src, out = refs[:n], refs[n:2 * n]
        ssem, rsem = refs[2 * n:]
        x, y, c, _ = _place()
        cps = [_remote(src[a], out[a], ssem, rsem, a, (x, y, 1 - c)) for a in range(n)]
        for cp in cps:
            cp.start()
        for cp in cps:
            cp.wait()

    return pl.pallas_call(
        body, name="swap_reduced",
        in_specs=[HBM_SPEC] * n, out_specs=[HBM_SPEC] * n,
        out_shape=[jax.ShapeDtypeStruct(r.shape, F32) for r in reduced],
        scratch_shapes=[pltpu.SemaphoreType.DMA((n,)), pltpu.SemaphoreType.DMA((n,))],
        compiler_params=_params(has_side_effects=True),
    )(*reduced)


def _allreduce_small(buf, hosted=None):
    R, L = buf.shape

    def body(*refs):
        (buf_ref, out_ref, pair_ref, chip_ref, ssem, rsem), start, wait = _host(hosted, refs, 1, 1, True, True)
        start()
        x, y, c, chips = _place()
        me_q = 2 * x + y
        pair_ref[c] = buf_ref[...]
        to_sib = _remote(buf_ref, pair_ref.at[c], ssem, rsem, 0, (x, y, 1 - c))
        to_sib.start()
        _remote(buf_ref, pair_ref.at[1 - c], ssem, rsem, 0, (x, y, 1 - c)).wait_recv()
        chip_ref[me_q] = pair_ref[0] + pair_ref[1]
        cps = [_remote(chip_ref.at[me_q], chip_ref.at[me_q], ssem, rsem, 1 + r, (*chip, c))
               for r, chip in enumerate(chips)]
        for cp in cps:
            cp.start()
        for r, chip in enumerate(chips):
            q = 2 * chip[0] + chip[1]
            _remote(chip_ref.at[q], chip_ref.at[q], ssem, rsem, 1 + r, (*chip, c)).wait_recv()
        out_ref[...] = (chip_ref[0] + chip_ref[1]) + (chip_ref[2] + chip_ref[3])
        to_sib.wait_send()
        for cp in cps:
            cp.wait_send()
        wait()

    h_ins = hosted.ins if hosted else []
    res = pl.pallas_call(
        body, name="allreduce_small",
        in_specs=[VMEM_SPEC] + [HBM_SPEC] * len(h_ins), out_specs=[VMEM_SPEC] + [HBM_SPEC] * len(h_ins),
        out_shape=[jax.ShapeDtypeStruct((R, L), F32)] + (hosted.out_shapes if hosted else []),
        scratch_shapes=[pltpu.VMEM((2, R, L), F32), pltpu.VMEM((N_CHIPS, R, L), F32),
                        pltpu.SemaphoreType.DMA((4,)), pltpu.SemaphoreType.DMA((4,))]
        + (hosted.sems() if hosted else []),
        compiler_params=_params(has_side_effects=True),
    )(buf, *h_ins)
    return res[0], res[1:]


def _pack(arrays):
    flat = jnp.concatenate([a.reshape(-1) for a in arrays])
    pad = (-flat.shape[0]) % (8 * LANES)
    return jnp.pad(flat, (0, pad)).reshape(-1, LANES)


def _unpack(buf, like):
    flat = buf.reshape(-1)
    out, off = [], 0
    for a in like:
        out.append(flat[off:off + a.size].reshape(a.shape))
        off += a.size
    return out


def _block_diag(pw):
    rows = []
    for gi in range(len(POOL_WINDOWS)):
        blocks = [pw[gi] if gj == gi else jnp.zeros_like(pw[gi]) for gj in range(len(POOL_WINDOWS))]
        rows.append(jnp.concatenate(blocks, axis=1))
    return jnp.concatenate(rows, axis=0)


def kernel(x, norm1, w_in, pool_w, pool_scale, sg_norm, sg_w, sg_b, w_out, norm2, w_up, w_down, final_norm, loss_target, m_norm1, m_w_in, m_pool_w, m_pool_scale, m_sg_norm, m_sg_w, m_sg_b, m_w_out, m_norm2, m_w_up, m_w_down, m_final_norm, v_norm1, v_w_in, v_pool_w, v_pool_scale, v_sg_norm, v_sg_w, v_sg_b, v_w_out, v_norm2, v_w_up, v_w_down, v_final_norm):
    depth = norm1.shape[0]
    T = x.shape[1]
    xs = x.reshape(T, D_MODEL)
    target = loss_target.reshape(T, D_MODEL)

    assert depth == 2
    c_idx = lax.axis_index("c")
    q_idx = 2 * lax.axis_index("x") + lax.axis_index("y")
    own = [w.astype(BF16) for w in (w_in, w_out, w_up, w_down)]
    gathered = {(0, 0): _gather_weights([own[0][:1]])[0]}

    def full(a, l, axis):
        blocks = lax.dynamic_update_slice(gathered[(a, l)], own[a][l][None, None], (q_idx, 0, 0, 0))[:, 0]
        if axis == 0:
            return blocks.reshape(-1, blocks.shape[-1])
        return jnp.concatenate([blocks[q] for q in range(N_CHIPS)], axis=axis)

    half_way = {}

    def gather_behind(call, keys, at_once):
        res, over_ici = call(_gather_over_ici([own[a][l:l + 1] for a, l in keys]))
        gathered.update(zip(keys[:at_once], _pass_to_sibling(over_ici[:at_once])))
        half_way.update(zip(keys[at_once:], over_ici[at_once:]))
        return res

    def pass_behind(call, keys):
        res, done = call(_pass_over_d2d([half_way.pop(k) for k in keys]))
        gathered.update(zip(keys, done))
        return res

    tril = jnp.tril(jnp.ones((CHUNK, CHUNK), F32))
    saved = []
    cur = xs
    wi, wo, wu, wd = {}, {}, {}, {}
    for l in range(depth):
        wbd = _block_diag(pool_w[l]).astype(BF16)
        wm = sg_w[l] * tril
        wm_s = wm.reshape(SG_HEADS * CHUNK, CHUNK).astype(BF16)
        wmt_s = jnp.swapaxes(wm, 1, 2).reshape(SG_HEADS * CHUNK, CHUNK).astype(BF16)
        bias = jnp.repeat(sg_b[l].T, SB_HD, axis=1)
        n1, n2 = norm1[l][None], norm2[l][None]
        psc, sgn = pool_scale[l][None], sg_norm[l][None]
        wi[l] = full(0, l, 1)
        proj, h, qkv = _inproj_fwd(cur, n1, wi[l])
        ya = _pool_fwd(proj, wbd, psc)
        yb = _sg_fwd(proj, wm_s, bias, sgn)
        if l == 0:
            yc = gather_behind(lambda hosted: _attn_fwd(qkv, hosted), [(1, 0), (2, 0), (3, 0)], 1)
            wo[l] = full(1, l, 0)
            x1, ymix = pass_behind(lambda hosted: _outproj_fwd(cur, ya, yb, yc, wo[l], hosted), [(2, 0), (3, 0)])
        else:
            yc = pass_behind(lambda hosted: _attn_fwd(qkv, hosted), [(1, l), (2, l), (3, l)])
            wo[l] = full(1, l, 0)
            (x1, ymix), _ = _outproj_fwd(cur, ya, yb, yc, wo[l])
        wu[l], wd[l] = full(2, l, 1), full(3, l, 0)
        if l == 0:
            x2, h2, u, act = gather_behind(lambda hosted: _mlp_fwd(x1, n2, wu[l], wd[l], hosted),
                                           [(0, 1), (1, 1), (2, 1), (3, 1)], 1)
        else:
            (x2, h2, u, act), _ = _mlp_fwd(x1, n2, wu[l], wd[l])
        saved.append(dict(x0=cur, x1=x1, proj=proj, h=h, qkv=qkv, yc=yc, ymix=ymix, h2=h2, u=u, act=act,
                          wbd=wbd, wm_s=wm_s, wmt_s=wmt_s, bias=bias, n1=n1, n2=n2, psc=psc, sgn=sgn))
        cur = x2

    loss_row, dcur, d_final = _loss_head(cur, final_norm[None], target)

    small = [None] * depth
    grads, parts, reduced = {}, {}, {}

    def pair_up(keys, swapped):
        parts.update({k: _add_pair(grads[k], o, c_idx) for k, o in zip(keys, swapped)})

    def chip_up(keys, arrived):
        reduced.update({k: _add_chips(parts[k], r, q_idx) for k, r in zip(keys, arrived)})

    for l in reversed(range(depth)):
        s = saved[l]
        if l == 0:
            keys = [(2, 1), (3, 1)]
            (dx1, du, d_n2), arrived = _mlp_bwd(dcur, s["x1"], s["n2"], s["u"], wu[l], wd[l],
                                                _scatter_over_ici([parts[k] for k in keys]))
            chip_up(keys, arrived)
        else:
            (dx1, du, d_n2), _ = _mlp_bwd(dcur, s["x1"], s["n2"], s["u"], wu[l], wd[l])
        grads[(2, l)] = _tn_matmul(s["h2"], du, "grad_w_up", n_split=N_CHIPS)
        grads[(3, l)] = _tn_matmul(s["act"], dcur, "grad_w_down")[0].reshape(N_CHIPS, D_FF // N_CHIPS, D_MODEL)
        dymix = _nt_matmul(dx1, wo[l])
        grads[(1, l)] = _tn_matmul(s["ymix"], dx1, "grad_w_out")[0].reshape(N_CHIPS, D_MODEL // N_CHIPS, D_MODEL)
        da_in, d_wbd, d_psc = _pool_bwd(s["proj"], dymix, s["wbd"], s["psc"])
        if l == 0:
            keys = [(0, 1), (1, 0), (2, 0), (3, 0)]
            (du_pre, dv_pre, d_wm, d_bias, d_sgn), swapped = _sg_bwd(
                s["proj"], dymix, s["wm_s"], s["wmt_s"], s["bias"], s["sgn"], _swap_over_d2d([grads[k] for k in keys]))
            pair_up(keys, swapped)
            keys = [(1, 1)] + keys
            (dq, dk, dv), arrived = _attn_bwd(s["qkv"], s["yc"], dymix, _scatter_over_ici([parts[k] for k in keys]))
            chip_up(keys, arrived)
        else:
            (du_pre, dv_pre, d_wm, d_bias, d_sgn), _ = _sg_bwd(s["proj"], dymix, s["wm_s"], s["wmt_s"], s["bias"], s["sgn"])
            keys = [(1, l), (2, l), (3, l)]
            (dq, dk, dv), swapped = _attn_bwd(s["qkv"], s["yc"], dymix, _swap_over_d2d([grads[k] for k in keys]))
            pair_up(keys, swapped)
        pieces = [da_in, du_pre, dv_pre, dq, dk, dv]
        if l == 0:
            keys = sorted(reduced)
            g_in_l, swapped = _inproj_grad(s["h"], pieces, _swap_reduced_over_d2d([reduced[k] for k in keys]))
            theirs = dict(zip(keys, swapped))
        else:
            g_in_l, _ = _inproj_grad(s["h"], pieces)
        grads[(0, l)] = g_in_l[0].reshape(D_MODEL, N_CHIPS, IN_COLS // N_CHIPS).transpose(1, 0, 2)
        if l == 0:
            keys = [(0, 0)]
            pair_up(keys, _swap_halves([grads[k] for k in keys]))
            (dx0, d_n1), arrived = _inproj_bwd(pieces, wi[l], s["x0"], s["n1"], dx1,
                                               _scatter_over_ici([parts[k] for k in keys]))
            chip_up(keys, arrived)
        else:
            (dx0, d_n1), _ = _inproj_bwd(pieces, wi[l], s["x0"], s["n1"], dx1)
        d_pw = jnp.stack([d_wbd[gi * POOL_GW:(gi + 1) * POOL_GW, gi * POOL_GW:(gi + 1) * POOL_GW]
                          for gi in range(len(POOL_WINDOWS))])
        small[l] = dict(norm1=d_n1[0], pool_w=d_pw, pool_scale=d_psc[0], sg_norm=d_sgn[0],
                        sg_w=d_wm.reshape(SG_HEADS, CHUNK, CHUNK), sg_b=d_bias[:, :SG_HEADS].T, norm2=d_n2[0])
        dcur = dx0
    grad_x = dcur.reshape(x.shape)

    names =["norm1", "pool_w", "pool_scale", "sg_norm", "sg_w", "sg_b", "norm2"]
    slot = jnp.zeros((1,), F32)
    small_w = [norm1, pool_w, pool_scale, sg_norm, sg_w, sg_b, norm2, final_norm, slot]
    small_m = [m_norm1, m_pool_w, m_pool_scale, m_sg_norm, m_sg_w, m_sg_b, m_norm2, m_final_norm, slot]
    small_v = [v_norm1, v_pool_w, v_pool_scale, v_sg_norm, v_sg_w, v_sg_b, v_norm2, v_final_norm, slot]
    small_g = [jnp.stack([small[l][k] for l in range(depth)]) for k in names] + [d_final[0], loss_row[0, :1]]
    keys = [(0, 0)]
    g_packed, _ = _allreduce_small(_pack(small_g))
    theirs.update(zip(keys, _swap_reduced([reduced[k] for k in keys])))

    def joined(a):
        layers = []
        for l in range(depth):
            mine, other = reduced[(a, l)], theirs[(a, l)]
            layers.append(jnp.where(c_idx == 0, jnp.concatenate([mine, other]), jnp.concatenate([other, mine])))
        return jnp.stack(layers)

    gw_in, gw_out, gw_up, gw_down = [joined(a) for a in range(4)]

    loss = _unpack(g_packed, small_w)[-1][0]
    s_delta, s_m, s_v = _elementwise(_adamw, "adamw_small", [_pack(small_w), g_packed, _pack(small_m), _pack(small_v)], 3)
    gs = dict(zip(names + ["final_norm"], _unpack(g_packed, small_w)))
    ds = dict(zip(names + ["final_norm"], _unpack(s_delta, small_w)))
    ms = dict(zip(names + ["final_norm"], _unpack(s_m, small_w)))
    vs = dict(zip(names + ["final_norm"], _unpack(s_v, small_w)))

    big_g = dict(w_in=gw_in, w_out=gw_out, w_up=gw_up, w_down=gw_down)
    big_w = dict(w_in=(w_in, m_w_in, v_w_in), w_out=(w_out, m_w_out, v_w_out),
                 w_up=(w_up, m_w_up, v_w_up), w_down=(w_down, m_w_down, v_w_down))
    for k, (w, m, v) in big_w.items():
        operands = [w, big_g[k], m, v]
        if k == "w_in":
            operands = [jnp.swapaxes(o, 1, 2) for o in operands]
        ds[k], ms[k], vs[k] = _elementwise(_adamw, "adamw_" + k, operands, 3)
        if k == "w_in":
            ds[k], ms[k], vs[k] = [jnp.swapaxes(o, 1, 2) for o in (ds[k], ms[k], vs[k])]
        gs[k] = big_g[k]

    order = ["norm1", "w_in", "pool_w", "pool_scale", "sg_norm", "sg_w", "sg_b", "w_out", "norm2", "w_up", "w_down",
             "final_norm"]
    return (loss, grad_x, *[gs[k] for k in order], *[ds[k] for k in order], *[ms[k] for k in order],
            *[vs[k] for k in order])
```

```python
import functools

import jax
import jax.numpy as jnp
from jax import lax
from jax.experimental import pallas as pl
from jax.experimental.pallas import tpu as pltpu

F32 = jnp.float32
BF16 = jnp.bfloat16
MESH = pl.DeviceIdType.MESH
AXES = ("x", "y", "c")

EPS = 1e-6
D_MODEL = 1024
POOL_WIDTH = 256
SG_WIDTH = 256
SB_WIDTH = 512
POOL_WINDOWS = (2, 4, 8, 16)
POOL_GW = 64
POOL_HALO = 16
CHUNK = 128
SG_HEADS = 4
SB_HD = 64
SB_SCALE = 0.125
IN_COLS = 2304
QKV_OFF = 768
D_FF = 4096
N_CHIPS = 4
LANES = 128
VMEM_LIMIT = 56 * 1024 * 1024
MLP_CHUNK = 512
ATTN_TILE = 256
UNDERFLOW = -104.0

ADAM_LR = 0.001
ADAM_B1 = 0.9
ADAM_B2 = 0.999
ADAM_EPS = 1e-08
ADAM_WD = 0.01
ADAM_STEP = 10

HBM_SPEC = pl.BlockSpec(memory_space=pl.ANY)
VMEM_SPEC = pl.BlockSpec(memory_space=pltpu.VMEM)


def _params(**kw):
    return pltpu.CompilerParams(vmem_limit_bytes=VMEM_LIMIT, **kw)


def _tile(n, pref):
    if n <= pref:
        return n
    for t in range(pref - pref % LANES, 0, -LANES):
        if n % t == 0:
            return t
    raise ValueError((n, pref))


def _nn(a, b):
    return jnp.dot(a, b, preferred_element_type=F32)


def _nt(a, b):
    return lax.dot_general(a, b, (((1,), (1,)), ((), ())), preferred_element_type=F32)


def _tn(a, b):
    return lax.dot_general(a, b, (((0,), (0,)), ((), ())), preferred_element_type=F32)


def _rms_fwd(x, g):
    r = lax.rsqrt(jnp.mean(x * x, axis=-1, keepdims=True) + EPS)
    xhat = x * r
    return xhat * g, xhat, r


def _rms_bwd(dy, xhat, r, g):
    dxhat = dy * g
    dx = r * (dxhat - xhat * jnp.mean(dxhat * xhat, axis=-1, keepdims=True))
    return dx, dy * xhat


_GELU_K = 0.7978845608028654
_GELU_C = 0.044715


def _gelu(x):
    return 0.5 * x * (1.0 + jnp.tanh(_GELU_K * (x + _GELU_C * x * x * x)))


def _gelu_grad(x):
    t = jnp.tanh(_GELU_K * (x + _GELU_C * x * x * x))
    return 0.5 * (1.0 + t) + 0.5 * x * (1.0 - t * t) * _GELU_K * (1.0 + 3.0 * _GELU_C * x * x)


def _inproj_fwd(x, g, w):
    T, D = x.shape
    N = w.shape[1]
    tt = _tile(T, 512)

    def body(x_ref, g_ref, w_ref, proj_ref, h_ref, qkv_ref):
        h, _, _ = _rms_fwd(x_ref[...], g_ref[...])
        hb = h.astype(BF16)
        h_ref[...] = hb
        p = _nn(hb, w_ref[...])
        proj_ref[...] = p[:, :QKV_OFF]
        qkv_ref[...] = p[:, QKV_OFF:].astype(BF16)

    return pl.pallas_call(
        body, name="inproj_fwd", grid=(T // tt,),
        in_specs=[pl.BlockSpec((tt, D), lambda i: (i, 0)), pl.BlockSpec((1, D), lambda i: (0, 0)),
                  pl.BlockSpec((D, N), lambda i: (0, 0))],
        out_specs=[pl.BlockSpec((tt, QKV_OFF), lambda i: (i, 0)), pl.BlockSpec((tt, D), lambda i: (i, 0)),
                   pl.BlockSpec((tt, N - QKV_OFF), lambda i: (i, 0))],
        out_shape=[jax.ShapeDtypeStruct((T, QKV_OFF), F32), jax.ShapeDtypeStruct((T, D), BF16),
                   jax.ShapeDtypeStruct((T, N - QKV_OFF), BF16)],
        compiler_params=_params(),
    )(x, g, w)


def _inproj_bwd(pieces, w, x, g, dres, hosted=None):
    T, D = x.shape
    N = w.shape[1]
    tt = _tile(T, 512)
    nt = T // tt
    widths = [p.shape[1] for p in pieces]
    offs = [sum(widths[:k]) for k in range(len(widths))]
    assert sum(widths) == N
    n_p = len(pieces)

    def body(*refs):
        i = pl.program_id(0)
        own, start, wait = _host(hosted, refs, n_p + 4, 2, i == 0, i == nt - 1)
        start()
        p_refs = own[:n_p]
        w_ref, x_ref, g_ref, dres_ref, dx_ref, dg_ref, dproj_ref = own[n_p:]
        for p_ref, o, wd in zip(p_refs, offs, widths):
            dproj_ref[:, o:o + wd] = p_ref[...].astype(BF16)
        dh = _nt(dproj_ref[...], w_ref[...])
        gv = g_ref[...]
        _, xhat, r = _rms_fwd(x_ref[...], gv)
        dx, dgrow = _rms_bwd(dh, xhat, r, gv)
        dx_ref[...] = dres_ref[...] + dx

        @pl.when(i == 0)
        def _():
            dg_ref[...] = jnp.zeros_like(dg_ref)

        dg_ref[...] += jnp.sum(dgrow, axis=0, keepdims=True)
        wait()

    h_ins = hosted.ins if hosted else []
    res = pl.pallas_call(
        body, name="inproj_bwd_hosting" if hosted else "inproj_bwd", grid=(nt,),
        in_specs=[pl.BlockSpec((tt, wd), lambda i: (i, 0)) for wd in widths] + [
            pl.BlockSpec((D, N), lambda i: (0, 0)), pl.BlockSpec((tt, D), lambda i: (i, 0)),
            pl.BlockSpec((1, D), lambda i: (0, 0)), pl.BlockSpec((tt, D), lambda i: (i, 0))] + [HBM_SPEC] * len(h_ins),
        out_specs=[pl.BlockSpec((tt, D), lambda i: (i, 0)), pl.BlockSpec((1, D), lambda i: (0, 0))]
        + [HBM_SPEC] * len(h_ins),
        out_shape=[jax.ShapeDtypeStruct((T, D), F32), jax.ShapeDtypeStruct((1, D), F32)]
        + (hosted.out_shapes if hosted else []),
        scratch_shapes=[pltpu.VMEM((tt, N), BF16)] + (hosted.sems() if hosted else []),
        compiler_params=_params(has_side_effects=hosted is not None),
    )(*pieces, w, x, g, dres, *h_ins)
    return res[:2], res[2:]


def _inproj_grad(h, pieces, hosted=None):
    T, D = h.shape
    tt = _tile(T, 1024)
    nt = T // tt
    widths = [p.shape[1] for p in pieces]
    offs = [sum(widths[:k]) for k in range(len(widths))]
    N = sum(widths)
    n_p = len(pieces)

    def body(*refs):
        t = pl.program_id(0)
        own, start, wait = _host(hosted, refs, n_p + 1, 1, t == 0, t == nt - 1)
        start()
        h_ref, p_refs, o_ref = own[0], own[1:1 + n_p], own[1 + n_p]

        @pl.when(t == 0)
        def _():
            o_ref[...] = jnp.zeros_like(o_ref)

        hv = h_ref[...]
        for p_ref, o, wd in zip(p_refs, offs, widths):
            o_ref[:, o:o + wd] += _tn(hv, p_ref[...].astype(BF16))
        wait()

    h_ins = hosted.ins if hosted else []
    res = pl.pallas_call(
        body, name="grad_w_in_hosting" if hosted else "grad_w_in", grid=(nt,),
        in_specs=[pl.BlockSpec((tt, D), lambda t: (t, 0))] + [pl.BlockSpec((tt, wd), lambda t: (t, 0)) for wd in widths]
        + [HBM_SPEC] * len(h_ins),
        out_specs=[pl.BlockSpec((None, D, N), lambda t: (0, 0, 0))] + [HBM_SPEC] * len(h_ins),
        out_shape=[jax.ShapeDtypeStruct((1, D, N), F32)] + (hosted.out_shapes if hosted else []),
        scratch_shapes=hosted.sems() if hosted else [],
        compiler_params=_params(has_side_effects=hosted is not None),
    )(h, *pieces, *h_ins)
    return res[0], res[1:]


def _pool_select(s2, s4, s8, s16, grp):
    return jnp.where(grp == 0, s2, jnp.where(grp == 1, s4, jnp.where(grp == 2, s8, s16)))


def _pool_count(t_glob, grp):
    win = jnp.where(grp == 0, 2, jnp.where(grp == 1, 4, jnp.where(grp == 2, 8, 16)))
    return jnp.minimum(t_glob + 1, win).astype(F32)


def _pool_diff(a, halo, base, tt):
    n = tt + POOL_HALO
    ext = jnp.concatenate([halo, a], axis=0)
    s2 = ext + pltpu.roll(ext, 1, 0)
    s4 = s2 + pltpu.roll(s2, 2, 0)
    s8 = s4 + pltpu.roll(s4, 4, 0)
    s16 = s8 + pltpu.roll(s8, 8, 0)
    grp = lax.broadcasted_iota(jnp.int32, (n, POOL_WIDTH), 1) // POOL_GW
    t_glob = lax.broadcasted_iota(jnp.int32, (n, POOL_WIDTH), 0) + (base - POOL_HALO)
    pooled = _pool_select(s2, s4, s8, s16, grp) / _pool_count(t_glob, grp)
    return pooled[POOL_HALO:] - a


def _pool_specs(T, tt):
    hb = tt // POOL_HALO
    return [pl.BlockSpec((tt, POOL_WIDTH), lambda i: (i, 0)),
            pl.BlockSpec((POOL_HALO, POOL_WIDTH), lambda i: (jnp.maximum(i * hb - 1, 0), 0))]


def _pool_fwd(proj, wbd, scale):
    T = proj.shape[0]
    tt = _tile(T, 512)

    def body(a_ref, halo_ref, w_ref, sc_ref, y_ref):
        i = pl.program_id(0)
        halo = jnp.where(i > 0, halo_ref[...], 0.0)
        d = _pool_diff(a_ref[...], halo, i * tt, tt)
        y_ref[...] = _nn(d.astype(BF16), w_ref[...]) * sc_ref[...]

    return pl.pallas_call(
        body, name="pool_fwd", grid=(T // tt,),
        in_specs=_pool_specs(T, tt) + [pl.BlockSpec((POOL_WIDTH, POOL_WIDTH), lambda i: (0, 0)),
                                       pl.BlockSpec((1, POOL_WIDTH), lambda i: (0, 0))],
        out_specs=pl.BlockSpec((tt, POOL_WIDTH), lambda i: (i, 0)),
        out_shape=jax.ShapeDtypeStruct((T, POOL_WIDTH), F32),
        compiler_params=_params(),
    )(proj, proj, wbd, scale)


def _pool_bwd(proj, dymix, wbd, scale):
    T = proj.shape[0]
    tt = _tile(T, 512)
    hb = tt // POOL_HALO
    nblk = T // tt
    n = tt + POOL_HALO

    def body(a_ref, halo_ref, dy_ref, dyn_ref, w_ref, sc_ref, da_ref, dw_ref, dsc_ref):
        i = pl.program_id(0)
        halo = jnp.where(i > 0, halo_ref[...], 0.0)
        d = _pool_diff(a_ref[...], halo, i * tt, tt)
        db = d.astype(BF16)
        wv = w_ref[...]
        sc = sc_ref[...]
        dy = dy_ref[...]
        dys = dy * sc

        @pl.when(i == 0)
        def _():
            dw_ref[...] = jnp.zeros_like(dw_ref)
            dsc_ref[...] = jnp.zeros_like(dsc_ref)

        dsc_ref[...] += jnp.sum(dy * _nn(db, wv), axis=0, keepdims=True)
        dw_ref[...] += _tn(db, dys.astype(BF16))
        dyn = jnp.where(i < nblk - 1, dyn_ref[...], 0.0) * sc
        dd = _nt(jnp.concatenate([dys, dyn], axis=0).astype(BF16), wv)
        grp = lax.broadcasted_iota(jnp.int32, (n, POOL_WIDTH), 1) // POOL_GW
        t_glob = lax.broadcasted_iota(jnp.int32, (n, POOL_WIDTH), 0) + i * tt
        e = dd / _pool_count(t_glob, grp)
        r2 = e + pltpu.roll(e, n - 1, 0)
        r4 = r2 + pltpu.roll(r2, n - 2, 0)
        r8 = r4 + pltpu.roll(r4, n - 4, 0)
        r16 = r8 + pltpu.roll(r8, n - 8, 0)
        da_ref[...] = (_pool_select(r2, r4, r8, r16, grp) - dd)[:tt].astype(BF16)

    return pl.pallas_call(
        body, name="pool_bwd", grid=(nblk,),
        in_specs=_pool_specs(T, tt) + [
            pl.BlockSpec((tt, POOL_WIDTH), lambda i: (i, 0)),
            pl.BlockSpec((POOL_HALO, POOL_WIDTH), lambda i: (jnp.minimum((i + 1) * hb, T // POOL_HALO - 1), 0)),
            pl.BlockSpec((POOL_WIDTH, POOL_WIDTH), lambda i: (0, 0)), pl.BlockSpec((1, POOL_WIDTH), lambda i: (0, 0))],
        out_specs=[pl.BlockSpec((tt, POOL_WIDTH), lambda i: (i, 0)),
                   pl.BlockSpec((POOL_WIDTH, POOL_WIDTH), lambda i: (0, 0)),
                   pl.BlockSpec((1, POOL_WIDTH), lambda i: (0, 0))],
        out_shape=[jax.ShapeDtypeStruct((T, POOL_WIDTH), BF16),
                   jax.ShapeDtypeStruct((POOL_WIDTH, POOL_WIDTH), F32),
                   jax.ShapeDtypeStruct((1, POOL_WIDTH), F32)],
        compiler_params=_params(),
    )(proj, proj, dymix, dymix, wbd, scale)


def _head_select(stacked, grp):
    out = jnp.where(grp == 0, stacked[0:CHUNK], 0.0)
    for h in range(1, SG_HEADS):
        out = out + jnp.where(grp == h, stacked[h * CHUNK:(h + 1) * CHUNK], 0.0)
    return out


def _sg_specs(tt):
    return [pl.BlockSpec((tt, SG_WIDTH), lambda i: (i, 1)), pl.BlockSpec((tt, SG_WIDTH), lambda i: (i, 2))]


def _sg_fwd(proj, wm, bias, g):
    T = proj.shape[0]
    tt = _tile(T, 512)

    def body(u_ref, v_ref, wm_ref, b_ref, g_ref, y_ref):
        zu = _gelu(u_ref[...])
        vn, _, _ = _rms_fwd(_gelu(v_ref[...]), g_ref[...])
        grp = lax.broadcasted_iota(jnp.int32, (CHUNK, SG_WIDTH), 1) // SB_HD
        for n in range(tt // CHUNK):
            rows = slice(n * CHUNK, (n + 1) * CHUNK)
            sv = _head_select(_nn(wm_ref[...], vn[rows].astype(BF16)), grp) + b_ref[...]
            y_ref[rows, :] = zu[rows] * sv

    return pl.pallas_call(
        body, name="sg_fwd", grid=(T // tt,),
        in_specs=_sg_specs(tt) + [pl.BlockSpec((SG_HEADS * CHUNK, CHUNK), lambda i: (0, 0)),
                                  pl.BlockSpec((CHUNK, SG_WIDTH), lambda i: (0, 0)),
                                  pl.BlockSpec((1, SG_WIDTH), lambda i: (0, 0))],
        out_specs=pl.BlockSpec((tt, SG_WIDTH), lambda i: (i, 0)),
        out_shape=jax.ShapeDtypeStruct((T, SG_WIDTH), F32),
        compiler_params=_params(),
    )(proj, proj, wm, bias, g)


def _sg_bwd(proj, dymix, wm, wmt, bias, g, hosted=None):
    T = proj.shape[0]
    tt = _tile(T, 512)
    nblk = T // tt

    def body(*refs):
        i = pl.program_id(0)
        (u_ref, v_ref, dy_ref, wm_ref, wmt_ref, b_ref, g_ref, du_ref, dv_ref, dw_ref, db_ref, dg_ref,
         dvn_ref, dbias_ref), start, wait = _host(hosted, refs, 7, 5, i == 0, i == nblk - 1)
        start()
        up, vp = u_ref[...], v_ref[...]
        gv = g_ref[...]
        zu, zv = _gelu(up), _gelu(vp)
        vn, xhat, r = _rms_fwd(zv, gv)
        gu = _gelu_grad(up)
        grp = lax.broadcasted_iota(jnp.int32, (CHUNK, SG_WIDTH), 1) // SB_HD

        @pl.when(i == 0)
        def _():
            dw_ref[...] = jnp.zeros_like(dw_ref)
            dbias_ref[...] = jnp.zeros_like(dbias_ref)
            dg_ref[...] = jnp.zeros_like(dg_ref)

        for n in range(tt // CHUNK):
            rows = slice(n * CHUNK, (n + 1) * CHUNK)
            vc = vn[rows].astype(BF16)
            sv = _head_select(_nn(wm_ref[...], vc), grp) + b_ref[...]
            dy = dy_ref[rows, :]
            du_ref[rows, :] = (dy * sv * gu[rows]).astype(BF16)
            dsv = dy * zu[rows]
            dsvb = dsv.astype(BF16)
            dvn_ref[rows, :] = _head_select(_nn(wmt_ref[...], dsvb), grp)
            stacked = jnp.concatenate([jnp.where(grp == h, dsv, 0.0) for h in range(SG_HEADS)], axis=0)
            dw_ref[...] += _nt(stacked.astype(BF16), vc)
            dbias_ref[...] += dsv

        dzv, dgrow = _rms_bwd(dvn_ref[...], xhat, r, gv)
        dg_ref[...] += jnp.sum(dgrow, axis=0, keepdims=True)
        dv_ref[...] = (dzv * _gelu_grad(vp)).astype(BF16)

        @pl.when(i == nblk - 1)
        def _():
            t_i = lax.broadcasted_iota(jnp.int32, (SG_HEADS * CHUNK, CHUNK), 0) % CHUNK
            s_i = lax.broadcasted_iota(jnp.int32, (SG_HEADS * CHUNK, CHUNK), 1)
            dw_ref[...] = jnp.where(s_i <= t_i, dw_ref[...], 0.0)
            lane = lax.broadcasted_iota(jnp.int32, (CHUNK, LANES), 1)
            acc = jnp.zeros((CHUNK, LANES), F32)
            for h in range(SG_HEADS):
                tot = jnp.sum(jnp.where(grp == h, dbias_ref[...], 0.0), axis=1, keepdims=True)
                acc = acc + jnp.where(lane == h, tot, 0.0)
            db_ref[...] = acc

        wait()

    h_ins = hosted.ins if hosted else []
    res = pl.pallas_call(
        body, name="sg_bwd_hosting" if hosted else "sg_bwd", grid=(nblk,),
        in_specs=_sg_specs(tt) + [pl.BlockSpec((tt, SG_WIDTH), lambda i: (i, 1)),
                                  pl.BlockSpec((SG_HEADS * CHUNK, CHUNK), lambda i: (0, 0)),
                                  pl.BlockSpec((SG_HEADS * CHUNK, CHUNK), lambda i: (0, 0)),
                                  pl.BlockSpec((CHUNK, SG_WIDTH), lambda i: (0, 0)),
                                  pl.BlockSpec((1, SG_WIDTH), lambda i: (0, 0))] + [HBM_SPEC] * len(h_ins),
        out_specs=[pl.BlockSpec((tt, SG_WIDTH), lambda i: (i, 0)), pl.BlockSpec((tt, SG_WIDTH), lambda i: (i, 0)),
                   pl.BlockSpec((SG_HEADS * CHUNK, CHUNK), lambda i: (0, 0)),
                   pl.BlockSpec((CHUNK, LANES), lambda i: (0, 0)), pl.BlockSpec((1, SG_WIDTH), lambda i: (0, 0))]
        + [HBM_SPEC] * len(h_ins),
        out_shape=[jax.ShapeDtypeStruct((T, SG_WIDTH), BF16), jax.ShapeDtypeStruct((T, SG_WIDTH), BF16),
                   jax.ShapeDtypeStruct((SG_HEADS * CHUNK, CHUNK), F32),
                   jax.ShapeDtypeStruct((CHUNK, LANES), F32), jax.ShapeDtypeStruct((1, SG_WIDTH), F32)]
        + (hosted.out_shapes if hosted else []),
        scratch_shapes=[pltpu.VMEM((tt, SG_WIDTH), F32), pltpu.VMEM((CHUNK, SG_WIDTH), F32)]
        + (hosted.sems() if hosted else []),
        compiler_params=_params(has_side_effects=hosted is not None),
    )(proj, proj, dymix, wm, wmt, bias, g, *h_ins)
    return res[:5], res[5:]


def _split_dot(x, u):
    hi = x.astype(BF16)
    lo = (x - hi.astype(F32)).astype(BF16)
    return _nn(hi, u) + _nn(lo, u)


def _sb_logits(z):
    lb = jnp.minimum(z, 0.0) - jnp.log(1.0 + jnp.exp(-jnp.abs(z)))
    return lb, lb - z


ATTN_STRIP = 32
ATTN_SUBS = 2


def _by_strips(n_rows, fn):
    parts = None
    for r in range(0, n_rows, ATTN_STRIP):
        res = fn(slice(r, r + ATTN_STRIP))
        parts = [[v] for v in res] if parts is None else [p + [v] for p, v in zip(parts, res)]
    return [jnp.concatenate(p, axis=0) for p in parts]


def _attn_qkv_specs(tq, T):
    base = (IN_COLS - 3 * SB_WIDTH - QKV_OFF) // LANES
    nb = SB_WIDTH // LANES
    return [pl.BlockSpec((tq, LANES), lambda p, i: (i, base + p)),
            pl.BlockSpec((T, LANES), lambda p, i: (0, base + nb + p)),
            pl.BlockSpec((T, LANES), lambda p, i: (0, base + 2 * nb + p))]


class _Hosted:
    def __init__(self, ins, out_shapes, n_sems, copies, in_place=False):
        self.ins, self.out_shapes, self.n_sems, self.copies = ins, out_shapes, n_sems, copies
        self.in_place = in_place

    @property
    def n(self):
        return len(self.ins)

    def aliases(self, n_in, n_out):
        return {n_in + k: n_out + k for k in range(self.n)} if self.in_place else {}

    def sems(self):
        return [pltpu.SemaphoreType.DMA((self.n_sems,)), pltpu.SemaphoreType.DMA((self.n_sems,))]

    def start(self, src, dst, ssem, rsem):
        for send, _ in self.copies(src, dst, ssem, rsem):
            send.start()

    def wait(self, src, dst, ssem, rsem):
        for send, recv in self.copies(src, dst, ssem, rsem):
            recv.wait_recv()
            send.wait_send()


def _host(hosted, refs, n_in, n_out, first, last):
    if hosted is None:
        return refs, lambda: None, lambda: None
    n = hosted.n
    own_in, h_in = refs[:n_in], refs[n_in:n_in + n]
    own_out, h_out = refs[n_in + n:n_in + n + n_out], refs[n_in + n + n_out:n_in + 2 * n + n_out]
    rest = refs[n_in + 2 * n + n_out:]
    ssem, rsem = rest[-2:]

    def start():
        if first is True:
            hosted.start(h_in, h_out, ssem, rsem)
        else:
            pl.when(first)(lambda: hosted.start(h_in, h_out, ssem, rsem))

    def wait():
        if last is True:
            hosted.wait(h_in, h_out, ssem, rsem)
        else:
            pl.when(last)(lambda: hosted.wait(h_in, h_out, ssem, rsem))

    return own_in + own_out + rest[:-2], start, wait


def _attn_fwd(qkv, hosted=None):
    T = qkv.shape[0]
    tk = _tile(T, ATTN_TILE)
    n_sub = ATTN_SUBS if T % (ATTN_SUBS * tk) == 0 else 1
    tq = n_sub * tk
    n_p, nq = SB_WIDTH // LANES, T // tq

    def body(*refs):
        p, i = pl.program_id(0), pl.program_id(1)
        (q_ref, k_ref, v_ref, o_ref), start, wait = _host(
            hosted, refs, 3, 1, jnp.logical_and(p == 0, i == 0), jnp.logical_and(p == n_p - 1, i == nq - 1))
        start()
        lane = lax.broadcasted_iota(jnp.int32, (tk, LANES), 1)
        row = lax.broadcasted_iota(jnp.int32, (tk, tk), 0)
        col = lax.broadcasted_iota(jnp.int32, (tk, tk), 1)
        after = jnp.where(row > col, 1.0, 0.0).astype(BF16)
        valid = col < row
        qh = {}
        for sb in range(n_sub):
            q = q_ref[sb * tk:(sb + 1) * tk, :].astype(F32)
            for hh in range(2):
                qh[(sb, hh)] = jnp.where((lane // SB_HD) == hh, q * SB_SCALE, 0.0).astype(BF16)

        def tiles(todo, state):
            chains = [(n, hh) for n in range(len(todo)) for hh in range(2)]
            kv = []
            for _, j, _ in todo:
                ks = pl.ds(pl.multiple_of(j * tk, tk), tk)
                kv.append((k_ref[ks, :], v_ref[ks, :]))
            z = {(n, hh): _nt(qh[(todo[n][0], hh)], kv[n][0]) for n, hh in chains}
            lb, lmb, lm_sum = {}, {}, {}
            for n, hh in chains:
                def logits(rows, z=z[(n, hh)], mask=todo[n][2]):
                    lb, lm = _sb_logits(z[rows])
                    if mask is not None:
                        lm = jnp.where(mask[rows], lm, 0.0)
                    return lb, lm.astype(BF16), jnp.sum(lm, axis=1, keepdims=True)

                lb[(n, hh)], lmb[(n, hh)], lm_sum[(n, hh)] = _by_strips(tk, logits)
            x = {c: _nn(lmb[c], after) for c in chains}
            new = dict(state)
            for n, hh in chains:
                key = (todo[n][0], hh)
                carry, acc = new[key]

                def weights(rows, lb=lb[(n, hh)], x=x[(n, hh)], carry=carry, mask=todo[n][2]):
                    a = jnp.exp(lb[rows] + x[rows] + carry[rows])
                    if mask is not None:
                        a = jnp.where(mask[rows], a, 0.0)
                    return (a.astype(BF16),)

                (ab,) = _by_strips(tk, weights)
                new[key] = (carry + lm_sum[(n, hh)], acc + _nn(ab, kv[n][1]))
            return new

        def live(state, sb):
            return jnp.maximum(jnp.max(state[(sb, 0)][0]), jnp.max(state[(sb, 1)][0]))

        first = n_sub * i
        zero = (jnp.zeros((tk, 1), F32), jnp.zeros((tk, LANES), F32))
        todo = []
        for sb in range(n_sub):
            gate = jnp.broadcast_to(first > 0, (tk, tk)) if sb == 0 else None
            todo += [(sb, first + sb, valid), (sb, jnp.maximum(first + sb - 1, 0), gate)]
        state = tiles(todo, {(sb, hh): zero for sb in range(n_sub) for hh in range(2)})
        for sb in range(n_sub):
            def cond(st):
                return jnp.logical_and(st[0] >= 0, st[2] > UNDERFLOW)

            def step(st, sb=sb):
                mine = tiles([(sb, st[0], None)], st[1])
                return st[0] - 1, mine, live(mine, sb)

            mine = {k: v for k, v in state.items() if k[0] == sb}
            _, mine, _ = lax.while_loop(cond, step, (first + sb - 2, mine, live(mine, sb)))
            o_ref[sb * tk:(sb + 1) * tk, :] = jnp.where(lane < SB_HD, mine[(sb, 0)][1], mine[(sb, 1)][1])
        wait()

    h_ins = hosted.ins if hosted else []
    res = pl.pallas_call(
        body, name="attn_fwd_hosting" if hosted else "attn_fwd", grid=(n_p, nq),
        in_specs=_attn_qkv_specs(tq, T) + [HBM_SPEC] * len(h_ins),
        out_specs=[pl.BlockSpec((tq, LANES), lambda p, i: (i, p))] + [HBM_SPEC] * len(h_ins),
        out_shape=[jax.ShapeDtypeStruct((T, SB_WIDTH), F32)] + (hosted.out_shapes if hosted else []),
        input_output_aliases=hosted.aliases(3, 1) if hosted else {},
        scratch_shapes=hosted.sems() if hosted else [],
        compiler_params=_params(has_side_effects=hosted is not None),
    )(qkv, qkv, qkv, *h_ins)
    return res[0], res[1:]


def _attn_bwd(qkv, o, dymix, hosted=None):
    T = qkv.shape[0]
    tk = _tile(T, ATTN_TILE)
    n_sub = ATTN_SUBS if T % (ATTN_SUBS * tk) == 0 else 1
    tq = n_sub * tk
    n_p, nq = SB_WIDTH // LANES, T // tq
    yc_blk = (POOL_WIDTH + SG_WIDTH) // LANES

    def body(*refs):
        p, i = pl.program_id(0), pl.program_id(1)
        (q_ref, k_ref, v_ref, o_ref, do_ref, dq_ref, dk_ref, dv_ref), start, wait = _host(
            hosted, refs, 5, 3, jnp.logical_and(p == 0, i == 0), jnp.logical_and(p == n_p - 1, i == nq - 1))
        start()
        lane = lax.broadcasted_iota(jnp.int32, (tk, LANES), 1)
        row = lax.broadcasted_iota(jnp.int32, (tk, tk), 0)
        col = lax.broadcasted_iota(jnp.int32, (tk, tk), 1)
        after = jnp.where(row > col, 1.0, 0.0).astype(BF16)
        from_here = jnp.where(row >= col, 1.0, 0.0).astype(BF16)
        from_here2 = jnp.concatenate([from_here, from_here], axis=0)
        valid = col < row

        @pl.when(i == 0)
        def _():
            dk_ref[...] = jnp.zeros_like(dk_ref)
            dv_ref[...] = jnp.zeros_like(dv_ref)

        qh, dohb, delta = {}, {}, {}
        for sb in range(n_sub):
            rows = slice(sb * tk, (sb + 1) * tk)
            q, ov, dov = q_ref[rows, :].astype(F32), o_ref[rows, :], do_ref[rows, :]
            for hh in range(2):
                head = (lane // SB_HD) == hh
                qh[(sb, hh)] = jnp.where(head, q * SB_SCALE, 0.0).astype(BF16)
                dohb[(sb, hh)] = jnp.where(head, dov, 0.0).astype(BF16)
                delta[(sb, hh)] = jnp.sum(dohb[(sb, hh)].astype(F32) * ov, axis=1, keepdims=True)

        def tiles(todo, state):
            chains = [(n, hh) for n in range(len(todo)) for hh in range(2)]
            kv, where = [], []
            for _, j, _ in todo:
                ks = pl.ds(pl.multiple_of(j * tk, tk), tk)
                where.append(ks)
                kv.append((k_ref[ks, :], v_ref[ks, :]))
            z = {(n, hh): _nt(qh[(todo[n][0], hh)], kv[n][0]) for n, hh in chains}
            da = {(n, hh): _nt(dohb[(todo[n][0], hh)], kv[n][1]) for n, hh in chains}
            lb, lmb, lm_sum = {}, {}, {}
            for n, hh in chains:
                def logits(rows, z=z[(n, hh)], mask=todo[n][2]):
                    lb, lm = _sb_logits(z[rows])
                    if mask is not None:
                        lm = jnp.where(mask[rows], lm, 0.0)
                    return lb, lm.astype(BF16), jnp.sum(lm, axis=1, keepdims=True)

                lb[(n, hh)], lmb[(n, hh)], lm_sum[(n, hh)] = _by_strips(tk, logits)
            x = {c: _nn(lmb[c], after) for c in chains}
            c_a = {k: v[0] for k, v in state.items()}
            ab, g, g_split, g_sum = {}, {}, {}, {}
            for n, hh in chains:
                key = (todo[n][0], hh)

                def weights(rows, lb=lb[(n, hh)], x=x[(n, hh)], da=da[(n, hh)], c_a=c_a[key], mask=todo[n][2]):
                    a = jnp.exp(lb[rows] + x[rows] + c_a[rows])
                    if mask is not None:
                        a = jnp.where(mask[rows], a, 0.0)
                    ab = a.astype(BF16)
                    g = da[rows] * ab.astype(F32)
                    hi = g.astype(BF16)
                    lo = (g - hi.astype(F32)).astype(BF16)
                    return ab, g, jnp.concatenate([hi, lo], axis=1), jnp.sum(g, axis=1, keepdims=True)

                ab[(n, hh)], g[(n, hh)], g_split[(n, hh)], g_sum[(n, hh)] = _by_strips(tk, weights)
                c_a[key] = c_a[key] + lm_sum[(n, hh)]
            right = {c: _nn(g_split[c], from_here2) for c in chains}
            c_r = {k: v[1] for k, v in state.items()}
            dzb = {}
            for n, hh in chains:
                key = (todo[n][0], hh)

                def logit_grads(rows, lb=lb[(n, hh)], g=g[(n, hh)], right=right[(n, hh)], c_r=c_r[key],
                                delta=delta[key], mask=todo[n][2]):
                    sig = jnp.exp(lb[rows])
                    left = delta[rows] - (c_r[rows] + right[rows])
                    dz = g[rows] * (1.0 - sig) - left * sig
                    if mask is not None:
                        dz = jnp.where(mask[rows], dz, 0.0)
                    return (dz.astype(BF16),)

                (dzb[(n, hh)],) = _by_strips(tk, logit_grads)
                c_r[key] = c_r[key] + g_sum[(n, hh)]
            dqa = {k: v[2] for k, v in state.items()}
            for n in range(len(todo)):
                sb = todo[n][0]
                dk_ref[where[n], :] += _tn(dzb[(n, 0)], qh[(sb, 0)]) + _tn(dzb[(n, 1)], qh[(sb, 1)])
                dv_ref[where[n], :] += _tn(ab[(n, 0)], dohb[(sb, 0)]) + _tn(ab[(n, 1)], dohb[(sb, 1)])
                for hh in range(2):
                    dqa[(sb, hh)] = dqa[(sb, hh)] + _nn(dzb[(n, hh)], kv[n][0])
            return {k: (c_a[k], c_r[k], dqa[k]) for k in state}

        def live(state, sb):
            return jnp.maximum(jnp.max(state[(sb, 0)][0]), jnp.max(state[(sb, 1)][0]))

        first = n_sub * i
        zero = (jnp.zeros((tk, 1), F32), jnp.zeros((tk, 1), F32), jnp.zeros((tk, LANES), F32))
        todo = []
        for sb in range(n_sub):
            gate = jnp.broadcast_to(first > 0, (tk, tk)) if sb == 0 else None
            todo += [(sb, first + sb, valid), (sb, jnp.maximum(first + sb - 1, 0), gate)]
        state = tiles(todo, {(sb, hh): zero for sb in range(n_sub) for hh in range(2)})
        for sb in range(n_sub):
            def cond(st):
                return jnp.logical_and(st[0] >= 0, st[2] > UNDERFLOW)

            def step(st, sb=sb):
                mine = tiles([(sb, st[0], None)], st[1])
                return st[0] - 1, mine, live(mine, sb)

            mine = {k: v for k, v in state.items() if k[0] == sb}
            _, mine, _ = lax.while_loop(cond, step, (first + sb - 2, mine, live(mine, sb)))
            dq_ref[sb * tk:(sb + 1) * tk, :] = (
                jnp.where(lane < SB_HD, mine[(sb, 0)][2], mine[(sb, 1)][2]) * SB_SCALE).astype(BF16)
        wait()

    h_ins = hosted.ins if hosted else []
    res = pl.pallas_call(
        body, name="attn_bwd_hosting" if hosted else "attn_bwd", grid=(n_p, nq),
        in_specs=_attn_qkv_specs(tq, T) + [pl.BlockSpec((tq, LANES), lambda p, i: (i, p)),
                                           pl.BlockSpec((tq, LANES), lambda p, i: (i, yc_blk + p))]
        + [HBM_SPEC] * len(h_ins),
        out_specs=[pl.BlockSpec((tq, LANES), lambda p, i: (i, p)), pl.BlockSpec((T, LANES), lambda p, i: (0, p)),
                   pl.BlockSpec((T, LANES), lambda p, i: (0, p))] + [HBM_SPEC] * len(h_ins),
        out_shape=[jax.ShapeDtypeStruct((T, SB_WIDTH), BF16)] + [jax.ShapeDtypeStruct((T, SB_WIDTH), F32)] * 2
        + (hosted.out_shapes if hosted else []),
        scratch_shapes=hosted.sems() if hosted else [],
        compiler_params=_params(has_side_effects=hosted is not None),
    )(qkv, qkv, qkv, o, dymix, *h_ins)
    return res[:3], res[3:]


def _outproj_fwd(x, ya, yb, yc, w, hosted=None):
    T, D = x.shape
    tt = _tile(T, 512)
    nt = T // tt

    def body(*refs):
        i = pl.program_id(0)
        (x_ref, ya_ref, yb_ref, yc_ref, w_ref, x1_ref, ymix_ref), start, wait = _host(
            hosted, refs, 5, 2, i == 0, i == nt - 1)
        start()
        ymix_ref[:, 0:POOL_WIDTH] = ya_ref[...].astype(BF16)
        ymix_ref[:, POOL_WIDTH:POOL_WIDTH + SG_WIDTH] = yb_ref[...].astype(BF16)
        ymix_ref[:, POOL_WIDTH + SG_WIDTH:] = yc_ref[...].astype(BF16)
        x1_ref[...] = x_ref[...] + _nn(ymix_ref[...], w_ref[...])
        wait()

    row = lambda width: pl.BlockSpec((tt, width), lambda i: (i, 0))
    h_ins = hosted.ins if hosted else []
    res = pl.pallas_call(
        body, name="outproj_fwd_hosting" if hosted else "outproj_fwd", grid=(nt,),
        in_specs=[row(D), row(POOL_WIDTH), row(SG_WIDTH), row(SB_WIDTH), pl.BlockSpec((D, D), lambda i: (0, 0))]
        + [HBM_SPEC] * len(h_ins),
        out_specs=[row(D), row(D)] + [HBM_SPEC] * len(h_ins),
        out_shape=[jax.ShapeDtypeStruct((T, D), F32), jax.ShapeDtypeStruct((T, D), BF16)]
        + (hosted.out_shapes if hosted else []),
        input_output_aliases=hosted.aliases(5, 2) if hosted else {},
        scratch_shapes=hosted.sems() if hosted else [],
        compiler_params=_params(has_side_effects=hosted is not None),
    )(x, ya, yb, yc, w, *h_ins)
    return res[:2], res[2:]


def _nt_matmul(a, w):
    T, N = a.shape
    K = w.shape[0]
    tt = _tile(T, 512)

    def body(a_ref, w_ref, o_ref):
        o_ref[...] = _nt(a_ref[...].astype(BF16), w_ref[...])

    return pl.pallas_call(
        body, name="nt_matmul", grid=(T // tt,),
        in_specs=[pl.BlockSpec((tt, N), lambda i: (i, 0)), pl.BlockSpec((K, N), lambda i: (0, 0))],
        out_specs=pl.BlockSpec((tt, K), lambda i: (i, 0)),
        out_shape=jax.ShapeDtypeStruct((T, K), F32),
        compiler_params=_params(),
    )(a, w)


def _tn_matmul(a, b, name, n_split=1, hosted=None):
    T, K = a.shape
    N = b.shape[1]
    tk = _tile(K, 1024)
    tn = _tile(N // n_split, 1024)
    tt = _tile(T, 2048)
    nper = N // n_split // tn
    nk, nn, nt = K // tk, N // tn, T // tt

    def body(*refs):
        k, n, t = pl.program_id(0), pl.program_id(1), pl.program_id(2)
        (a_ref, b_ref, o_ref), start, wait = _host(
            hosted, refs, 2, 1, jnp.logical_and(jnp.logical_and(k == 0, n == 0), t == 0),
            jnp.logical_and(jnp.logical_and(k == nk - 1, n == nn - 1), t == nt - 1))
        start()

        @pl.when(t == 0)
        def _():
            o_ref[...] = jnp.zeros_like(o_ref)

        o_ref[...] += _tn(a_ref[...], b_ref[...].astype(BF16))
        wait()

    h_ins = hosted.ins if hosted else []
    res = pl.pallas_call(
        body, name=name + "_hosting" if hosted else name, grid=(nk, nn, nt),
        in_specs=[pl.BlockSpec((tt, tk), lambda k, n, t: (t, k)), pl.BlockSpec((tt, tn), lambda k, n, t: (t, n))]
        + [HBM_SPEC] * len(h_ins),
        out_specs=[pl.BlockSpec((None, tk, tn), lambda k, n, t: (n // nper, k, n % nper))] + [HBM_SPEC] * len(h_ins),
        out_shape=[jax.ShapeDtypeStruct((n_split, K, N // n_split), F32)] + (hosted.out_shapes if hosted else []),
        scratch_shapes=hosted.sems() if hosted else [],
        compiler_params=_params(has_side_effects=hosted is not None),
    )(a, b, *h_ins)
    return (res[0], res[1:]) if hosted else res[0]


def _mlp_fwd(x, g, w_up, w_down, hosted=None):
    T, D = x.shape
    n_blk, _, width = w_up.shape
    F = n_blk * width
    tt = _tile(T, 1024)
    fc = _tile(width, MLP_CHUNK)
    per = width // fc
    nc = F // fc
    nt = T // tt

    def body(*refs):
        i, c = pl.program_id(0), pl.program_id(1)
        (x_ref, g_ref, wu_ref, wd_ref, y_ref, h_ref, u_ref, a_ref), start, wait = _host(
            hosted, refs, 4, 4, jnp.logical_and(i == 0, c == 0), jnp.logical_and(i == nt - 1, c == nc - 1))
        start()

        @pl.when(c == 0)
        def _():
            xv = x_ref[...]
            h, _, _ = _rms_fwd(xv, g_ref[...])
            h_ref[...] = h.astype(BF16)
            y_ref[...] = xv

        u = _nn(h_ref[...], wu_ref[...])
        u_ref[...] = u.astype(BF16)
        a = jnp.square(jnp.maximum(u, 0.0)).astype(BF16)
        a_ref[...] = a
        y_ref[...] += _nn(a, wd_ref[...])
        wait()

    h_ins = hosted.ins if hosted else []
    res = pl.pallas_call(
        body, name="mlp_fwd_hosting" if hosted else "mlp_fwd", grid=(nt, nc),
        in_specs=[pl.BlockSpec((tt, D), lambda i, c: (i, 0)), pl.BlockSpec((1, D), lambda i, c: (0, 0)),
                  pl.BlockSpec((None, D, fc), lambda i, c: (c // per, 0, c % per)),
                  pl.BlockSpec((fc, D), lambda i, c: (c, 0))]
        + [HBM_SPEC] * len(h_ins),
        out_specs=[pl.BlockSpec((tt, D), lambda i, c: (i, 0)), pl.BlockSpec((tt, D), lambda i, c: (i, 0)),
                   pl.BlockSpec((tt, fc), lambda i, c: (i, c)), pl.BlockSpec((tt, fc), lambda i, c: (i, c))]
        + [HBM_SPEC] * len(h_ins),
        out_shape=[jax.ShapeDtypeStruct((T, D), F32), jax.ShapeDtypeStruct((T, D), BF16),
                   jax.ShapeDtypeStruct((T, F), BF16), jax.ShapeDtypeStruct((T, F), BF16)]
        + (hosted.out_shapes if hosted else []),
        scratch_shapes=hosted.sems() if hosted else [],
        compiler_params=_params(has_side_effects=hosted is not None),
    )(x, g, w_up, w_down, *h_ins)
    return res[:4], res[4:]


def _mlp_bwd(dy, x, g, u, w_up, w_down, hosted=None):
    T, D = x.shape
    n_blk, _, width = w_up.shape
    F = n_blk * width
    tt = _tile(T, 1024)
    fc = _tile(width, MLP_CHUNK)
    per = width // fc
    nc = F // fc
    nt = T // tt

    def body(*refs):
        i, c = pl.program_id(0), pl.program_id(1)
        (dy_ref, x_ref, g_ref, u_ref, wu_ref, wd_ref, dx_ref, du_ref, dg_ref, dyb_ref, dh_ref), start, wait = _host(
            hosted, refs, 6, 3, jnp.logical_and(i == 0, c == 0), jnp.logical_and(i == nt - 1, c == nc - 1))
        start()

        @pl.when(c == 0)
        def _():
            dyb_ref[...] = dy_ref[...].astype(BF16)
            dh_ref[...] = jnp.zeros_like(dh_ref)

        @pl.when(jnp.logical_and(i == 0, c == 0))
        def _():
            dg_ref[...] = jnp.zeros_like(dg_ref)

        da = _nt(dyb_ref[...], wd_ref[...])
        du = (da * (2.0 * jnp.maximum(u_ref[...].astype(F32), 0.0))).astype(BF16)
        du_ref[...] = du
        dh_ref[...] += _nt(du, wu_ref[...])

        @pl.when(c == nc - 1)
        def _():
            gv = g_ref[...]
            _, xhat, r = _rms_fwd(x_ref[...], gv)
            dx, dgrow = _rms_bwd(dh_ref[...], xhat, r, gv)
            dx_ref[...] = dy_ref[...] + dx
            dg_ref[...] += jnp.sum(dgrow, axis=0, keepdims=True)

        wait()

    h_ins = hosted.ins if hosted else []
    res = pl.pallas_call(
        body, name="mlp_bwd_hosting" if hosted else "mlp_bwd", grid=(nt, nc),
        in_specs=[pl.BlockSpec((tt, D), lambda i, c: (i, 0)), pl.BlockSpec((tt, D), lambda i, c: (i, 0)),
                  pl.BlockSpec((1, D), lambda i, c: (0, 0)), pl.BlockSpec((tt, fc), lambda i, c: (i, c)),
                  pl.BlockSpec((None, D, fc), lambda i, c: (c // per, 0, c % per)),
                  pl.BlockSpec((fc, D), lambda i, c: (c, 0))]
        + [HBM_SPEC] * len(h_ins),
        out_specs=[pl.BlockSpec((tt, D), lambda i, c: (i, 0)), pl.BlockSpec((tt, fc), lambda i, c: (i, c)),
                   pl.BlockSpec((1, D), lambda i, c: (0, 0))] + [HBM_SPEC] * len(h_ins),
        out_shape=[jax.ShapeDtypeStruct((T, D), F32), jax.ShapeDtypeStruct((T, F), BF16),
                   jax.ShapeDtypeStruct((1, D), F32)] + (hosted.out_shapes if hosted else []),
        scratch_shapes=[pltpu.VMEM((tt, D), BF16), pltpu.VMEM((tt, D), F32)] + (hosted.sems() if hosted else []),
        compiler_params=_params(has_side_effects=hosted is not None),
    )(dy, x, g, u, w_up, w_down, *h_ins)
    return res[:3], res[3:]


def _loss_head(x, g, target):
    T, D = x.shape
    tt = _tile(T, 512)

    def body(x_ref, g_ref, t_ref, loss_ref, dx_ref, dg_ref):
        gv = g_ref[...]
        y, xhat, r = _rms_fwd(x_ref[...], gv)
        err = y - t_ref[...]
        dx, dgrow = _rms_bwd(err * (1.0 / D), xhat, r, gv)
        dx_ref[...] = dx

        @pl.when(pl.program_id(0) == 0)
        def _():
            loss_ref[...] = jnp.zeros_like(loss_ref)
            dg_ref[...] = jnp.zeros_like(dg_ref)

        loss_ref[...] += 0.5 * jnp.sum(jnp.mean(err * err, axis=-1, keepdims=True), axis=0, keepdims=True)
        dg_ref[...] += jnp.sum(dgrow, axis=0, keepdims=True)

    return pl.pallas_call(
        body, name="loss_head", grid=(T // tt,),
        in_specs=[pl.BlockSpec((tt, D), lambda i: (i, 0)), pl.BlockSpec((1, D), lambda i: (0, 0)),
                  pl.BlockSpec((tt, D), lambda i: (i, 0))],
        out_specs=[pl.BlockSpec((1, LANES), lambda i: (0, 0)), pl.BlockSpec((tt, D), lambda i: (i, 0)),
                   pl.BlockSpec((1, D), lambda i: (0, 0))],
        out_shape=[jax.ShapeDtypeStruct((1, LANES), F32), jax.ShapeDtypeStruct((T, D), F32),
                   jax.ShapeDtypeStruct((1, D), F32)],
        compiler_params=_params(),
    )(x, g, target)


def _rows(shape, pref=512):
    last = shape[-1]
    rows = 1
    for s in shape[:-1]:
        rows *= s
    tr = rows
    if rows * last > 256 * 1024:
        for cand in (pref, 256, 128, 64, 32, 16, 8):
            if rows % cand == 0:
                tr = cand
                break
    return rows, last, tr


def _elementwise(fn, name, ins, n_out, out_dtype=F32):
    shape = ins[0].shape
    rows, last, tr = _rows(shape)
    flat = [a.reshape(rows, last) for a in ins]
    n_in = len(ins)

    def body(*refs):
        res = fn(*[r[...] for r in refs[:n_in]])
        if n_out == 1:
            res = (res,)
        for r, v in zip(refs[n_in:], res):
            r[...] = v.astype(r.dtype)

    spec = pl.BlockSpec((tr, last), lambda i: (i, 0))
    outs = pl.pallas_call(
        body, name=name, grid=(rows // tr,),
        in_specs=[spec] * n_in, out_specs=[spec] * n_out,
        out_shape=[jax.ShapeDtypeStruct((rows, last), out_dtype)] * n_out,
        compiler_params=_params(),
    )(*flat)
    return [o.reshape(shape) for o in outs]


def _add_pair(g, o, c_idx):
    nq, R, C = g.shape
    h = R // 2
    tr = _tile(h, 512)
    nb = h // tr

    def body(c_ref, g_ref, o_ref, out_ref):
        out_ref[...] = g_ref[...] + o_ref[...]

    return pl.pallas_call(
        body, name="add_pair",
        grid_spec=pltpu.PrefetchScalarGridSpec(
            num_scalar_prefetch=1, grid=(nq, nb),
            in_specs=[pl.BlockSpec((None, tr, C), lambda q, i, c: (q, c[0] * nb + i, 0)),
                      pl.BlockSpec((None, tr, C), lambda q, i, c: (q, i, 0))],
            out_specs=pl.BlockSpec((None, tr, C), lambda q, i, c: (q, i, 0))),
        out_shape=jax.ShapeDtypeStruct((nq, h, C), F32),
        compiler_params=_params(),
    )(c_idx.astype(jnp.int32).reshape(1), g, o)


def _add_chips(p, r, q_idx):
    _, H, C = p.shape
    tr = _tile(H, 512)

    def body(q_ref, p_ref, r0_ref, r1_ref, r2_ref, out_ref):
        out_ref[...] = (p_ref[...] + r0_ref[...]) + (r1_ref[...] + r2_ref[...])

    def arrived(k):
        return pl.BlockSpec((None, tr, C), lambda i, q: (k, i, 0))

    return pl.pallas_call(
        body, name="add_chips",
        grid_spec=pltpu.PrefetchScalarGridSpec(
            num_scalar_prefetch=1, grid=(H // tr,),
            in_specs=[pl.BlockSpec((None, tr, C), lambda i, q: (q[0], i, 0)), arrived(0), arrived(1), arrived(2)],
            out_specs=pl.BlockSpec((tr, C), lambda i, q: (i, 0))),
        out_shape=jax.ShapeDtypeStruct((H, C), F32),
        compiler_params=_params(),
    )(q_idx.astype(jnp.int32).reshape(1), p, r, r, r)


def _adamw(w, g, m, v):
    m = ADAM_B1 * m + (1.0 - ADAM_B1) * g
    v = ADAM_B2 * v + (1.0 - ADAM_B2) * jnp.square(g)
    m_hat = m / (1.0 - ADAM_B1 ** ADAM_STEP)
    v_hat = v / (1.0 - ADAM_B2 ** ADAM_STEP)
    delta = -ADAM_LR * (m_hat / (jnp.sqrt(v_hat) + ADAM_EPS) + ADAM_WD * w)
    return delta, m, v


def _place():
    x, y, c = lax.axis_index("x"), lax.axis_index("y"), lax.axis_index("c")
    chips = [(1 - x, y), (x, 1 - y), (1 - x, 1 - y)]
    return x, y, c, chips


def _remote(src, dst, ssem, rsem, k, dev):
    return pltpu.make_async_remote_copy(src_ref=src, dst_ref=dst, send_sem=ssem.at[k], recv_sem=rsem.at[k],
                                        device_id=dev, device_id_type=MESH)


def _gather_weights(shards):
    n = len(shards)
    halves = [s.shape[1] // 2 for s in shards]

    def body(*refs):
        src, out = refs[:n], refs[n:2 * n]
        ssem, rsem = refs[2 * n:]
        x, y, c, chips = _place()
        me_q = 2 * x + y
        sib = (x, y, 1 - c)

        def half(a, q, cc):
            return out[a].at[q, :, pl.ds(cc * halves[a], halves[a]), :]

        first = []
        for a in range(n):
            mine = src[a].at[:, pl.ds(c * halves[a], halves[a]), :]
            for r, chip in enumerate(chips):
                first.append(_remote(mine, half(a, me_q, c), ssem, rsem, a * 3 + r, (*chip, c)))
        for cp in first:
            cp.start()
        passed = []
        for a in range(n):
            for r, chip in enumerate(chips):
                q = 2 * chip[0] + chip[1]
                k = a * 3 + r
                _remote(half(a, q, c), half(a, q, c), ssem, rsem, k, (*chip, c)).wait_recv()
                cp = _remote(half(a, q, c), half(a, q, c), ssem, rsem, 3 * n + k, sib)
                cp.start()
                passed.append(cp)
        for a in range(n):
            for r, chip in enumerate(chips):
                q = 2 * chip[0] + chip[1]
                _remote(half(a, q, 1 - c), half(a, q, 1 - c), ssem, rsem, 3 * n + a * 3 + r, sib).wait_recv()
        for cp in first + passed:
            cp.wait_send()

    return pl.pallas_call(
        body, name="gather_weights",
        in_specs=[HBM_SPEC] * n, out_specs=[HBM_SPEC] * n,
        out_shape=[jax.ShapeDtypeStruct((N_CHIPS,) + s.shape, s.dtype) for s in shards],
        scratch_shapes=[pltpu.SemaphoreType.DMA((6 * n,)), pltpu.SemaphoreType.DMA((6 * n,))],
        compiler_params=_params(has_side_effects=True),
    )(*shards)


def _gather_over_ici(shards):
    n = len(shards)
    halves = [s.shape[1] // 2 for s in shards]

    def copies(src, out, ssem, rsem):
        x, y, c, chips = _place()
        me_q = 2 * x + y
        res = []
        for a in range(n):
            rows = pl.ds(c * halves[a], halves[a])
            mine = src[a].at[:, rows, :]
            for r, chip in enumerate(chips):
                dev = (*chip, c)
                res.append((_remote(mine, out[a].at[me_q, :, rows, :], ssem, rsem, a * 3 + r, dev),
                            _remote(mine, out[a].at[2 * chip[0] + chip[1], :, rows, :], ssem, rsem, a * 3 + r, dev)))
        return res

    return _Hosted(list(shards), [jax.ShapeDtypeStruct((N_CHIPS,) + s.shape, s.dtype) for s in shards], 3 * n, copies)


def _pass_over_d2d(gathered):
    n = len(gathered)
    halves = [g.shape[2] // 2 for g in gathered]

    def copies(_, out, ssem, rsem):
        x, y, c, chips = _place()
        sib = (x, y, 1 - c)
        res = []
        for a in range(n):
            for r, chip in enumerate(chips):
                q = 2 * chip[0] + chip[1]
                mine = out[a].at[q, :, pl.ds(c * halves[a], halves[a]), :]
                theirs = out[a].at[q, :, pl.ds((1 - c) * halves[a], halves[a]), :]
                res.append((_remote(mine, mine, ssem, rsem, a * 3 + r, sib),
                            _remote(theirs, theirs, ssem, rsem, a * 3 + r, sib)))
        return res

    return _Hosted(list(gathered), [jax.ShapeDtypeStruct(g.shape, g.dtype) for g in gathered], 3 * n, copies,
                   in_place=True)


def _pass_to_sibling(gathered):
    n = len(gathered)
    halves = [g.shape[2] // 2 for g in gathered]

    def body(*refs):
        out = refs[n:2 * n]
        ssem, rsem = refs[2 * n:]
        x, y, c, chips = _place()
        sib = (x, y, 1 - c)

        def half(a, q, cc):
            return out[a].at[q, :, pl.ds(cc * halves[a], halves[a]), :]

        cps = []
        for a in range(n):
            for r, chip in enumerate(chips):
                q = 2 * chip[0] + chip[1]
                cps.append(_remote(half(a, q, c), half(a, q, c), ssem, rsem, a * 3 + r, sib))
        for cp in cps:
            cp.start()
        for a in range(n):
            for r, chip in enumerate(chips):
                q = 2 * chip[0] + chip[1]
                _remote(half(a, q, 1 - c), half(a, q, 1 - c), ssem, rsem, a * 3 + r, sib).wait_recv()
        for cp in cps:
            cp.wait_send()

    return pl.pallas_call(
        body, name="pass_to_sibling",
        in_specs=[HBM_SPEC] * n, out_specs=[HBM_SPEC] * n,
        out_shape=[jax.ShapeDtypeStruct(g.shape, g.dtype) for g in gathered],
        input_output_aliases={a: a for a in range(n)},
        scratch_shapes=[pltpu.SemaphoreType.DMA((3 * n,)), pltpu.SemaphoreType.DMA((3 * n,))],
        compiler_params=_params(has_side_effects=True),
    )(*gathered)


def _scatter_over_ici(parts):
    n = len(parts)

    def copies(src, out, ssem, rsem):
        x, y, c, chips = _place()
        res = []
        for a in range(n):
            for r, chip in enumerate(chips):
                cp = _remote(src[a].at[2 * chip[0] + chip[1]], out[a].at[r], ssem, rsem, a * 3 + r, (*chip, c))
                res.append((cp, cp))
        return res

    return _Hosted(list(parts), [jax.ShapeDtypeStruct((3,) + p.shape[1:], F32) for p in parts], 3 * n, copies)


def _swap_over_d2d(grads):
    n = len(grads)
    halves = [g.shape[1] // 2 for g in grads]

    def copies(src, out, ssem, rsem):
        x, y, c, _ = _place()
        res = []
        for a in range(n):
            cp = _remote(src[a].at[:, pl.ds((1 - c) * halves[a], halves[a]), :], out[a], ssem, rsem, a, (x, y, 1 - c))
            res.append((cp, cp))
        return res

    return _Hosted(list(grads), [jax.ShapeDtypeStruct((N_CHIPS, h, g.shape[2]), F32) for g, h in zip(grads, halves)],
                   n, copies)


def _swap_halves(grads):
    n = len(grads)
    halves = [g.shape[1] // 2 for g in grads]

    def body(*refs):
        src, out = refs[:n], refs[n:2 * n]
        ssem, rsem = refs[2 * n:]
        x, y, c, _ = _place()
        cps = [_remote(src[a].at[:, pl.ds((1 - c) * halves[a], halves[a]), :], out[a], ssem, rsem, a, (x, y, 1 - c))
               for a in range(n)]
        for cp in cps:
            cp.start()
        for cp in cps:
            cp.wait()

    return pl.pallas_call(
        body, name="swap_halves",
        in_specs=[HBM_SPEC] * n, out_specs=[HBM_SPEC] * n,
        out_shape=[jax.ShapeDtypeStruct((N_CHIPS, h, g.shape[2]), F32) for g, h in zip(grads, halves)],
        scratch_shapes=[pltpu.SemaphoreType.DMA((n,)), pltpu.SemaphoreType.DMA((n,))],
        compiler_params=_params(has_side_effects=True),
    )(*grads)


def _scatter_chips(parts):
    n = len(parts)

    def body(*refs):
        src, out = refs[:n], refs[n:2 * n]
        ssem, rsem = refs[2 * n:]
        x, y, c, chips = _place()
        cps = []
        for a in range(n):
            for r, chip in enumerate(chips):
                cps.append(_remote(src[a].at[2 * chip[0] + chip[1]], out[a].at[r], ssem, rsem, a * 3 + r, (*chip, c)))
        for cp in cps:
            cp.start()
        for cp in cps:
            cp.wait()

    return pl.pallas_call(
        body, name="scatter_chips",
        in_specs=[HBM_SPEC] * n, out_specs=[HBM_SPEC] * n,
        out_shape=[jax.ShapeDtypeStruct((3,) + p.shape[1:], F32) for p in parts],
        scratch_shapes=[pltpu.SemaphoreType.DMA((3 * n,)), pltpu.SemaphoreType.DMA((3 * n,))],
        compiler_params=_params(has_side_effects=True),
    )(*parts)


def _swap_reduced_over_d2d(reduced):
    n = len(reduced)

    def copies(src, out, ssem, rsem):
        x, y, c, _ = _place()
        res = []
        for a in range(n):
            cp = _remote(src[a], out[a], ssem, rsem, a, (x, y, 1 - c))
            res.append((cp, cp))
        return res

    return _Hosted(list(reduced), [jax.ShapeDtypeStruct(r.shape, F32) for r in reduced], n, copies)


def _swap_reduced(reduced):
    n = len(reduced)

    def body(*refs):
        src, out = refs[:n], refs[n:2 * n]
        ssem, rsem = refs[2 * n:]
        x, y, c, _ = _place()
        cps = [_remote(src[a], out[a], ssem, rsem, a, (x, y, 1 - c)) for a in range(n)]
        for cp in cps:
            cp.start()
        for cp in cps:
            cp.wait()

    return pl.pallas_call(
        body, name="swap_reduced",
        in_specs=[HBM_SPEC] * n, out_specs=[HBM_SPEC] * n,
        out_shape=[jax.ShapeDtypeStruct(r.shape, F32) for r in reduced],
        scratch_shapes=[pltpu.SemaphoreType.DMA((n,)), pltpu.SemaphoreType.DMA((n,))],
        compiler_params=_params(has_side_effects=True),
    )(*reduced)


def _allreduce_small(buf, hosted=None):
    R, L = buf.shape

    def body(*refs):
        (buf_ref, out_ref, pair_ref, chip_ref, ssem, rsem), start, wait = _host(hosted, refs, 1, 1, True, True)
        start()
        x, y, c, chips = _place()
        me_q = 2 * x + y
        pair_ref[c] = buf_ref[...]
        to_sib = _remote(buf_ref, pair_ref.at[c], ssem, rsem, 0, (x, y, 1 - c))
        to_sib.start()
        _remote(buf_ref, pair_ref.at[1 - c], ssem, rsem, 0, (x, y, 1 - c)).wait_recv()
        chip_ref[me_q] = pair_ref[0] + pair_ref[1]
        cps = [_remote(chip_ref.at[me_q], chip_ref.at[me_q], ssem, rsem, 1 + r, (*chip, c))
               for r, chip in enumerate(chips)]
        for cp in cps:
            cp.start()
        for r, chip in enumerate(chips):
            q = 2 * chip[0] + chip[1]
            _remote(chip_ref.at[q], chip_ref.at[q], ssem, rsem, 1 + r, (*chip, c)).wait_recv()
        out_ref[...] = (chip_ref[0] + chip_ref[1]) + (chip_ref[2] + chip_ref[3])
        to_sib.wait_send()
        for cp in cps:
            cp.wait_send()
        wait()

    h_ins = hosted.ins if hosted else []
    res = pl.pallas_call(
        body, name="allreduce_small",
        in_specs=[VMEM_SPEC] + [HBM_SPEC] * len(h_ins), out_specs=[VMEM_SPEC] + [HBM_SPEC] * len(h_ins),
        out_shape=[jax.ShapeDtypeStruct((R, L), F32)] + (hosted.out_shapes if hosted else []),
        scratch_shapes=[pltpu.VMEM((2, R, L), F32), pltpu.VMEM((N_CHIPS, R, L), F32),
                        pltpu.SemaphoreType.DMA((4,)), pltpu.SemaphoreType.DMA((4,))]
        + (hosted.sems() if hosted else []),
        compiler_params=_params(has_side_effects=True),
    )(buf, *h_ins)
    return res[0], res[1:]


def _pack(arrays):
    flat = jnp.concatenate([a.reshape(-1) for a in arrays])
    pad = (-flat.shape[0]) % (8 * LANES)
    return jnp.pad(flat, (0, pad)).reshape(-1, LANES)


def _unpack(buf, like):
    flat = buf.reshape(-1)
    out, off = [], 0
    for a in like:
        out.append(flat[off:off + a.size].reshape(a.shape))
        off += a.size
    return out


def _block_diag(pw):
    rows = []
    for gi in range(len(POOL_WINDOWS)):
        blocks = [pw[gi] if gj == gi else jnp.zeros_like(pw[gi]) for gj in range(len(POOL_WINDOWS))]
        rows.append(jnp.concatenate(blocks, axis=1))
    return jnp.concatenate(rows, axis=0)


def kernel(x, norm1, w_in, pool_w, pool_scale, sg_norm, sg_w, sg_b, w_out, norm2, w_up, w_down, final_norm, loss_target, m_norm1, m_w_in, m_pool_w, m_pool_scale, m_sg_norm, m_sg_w, m_sg_b, m_w_out, m_norm2, m_w_up, m_w_down, m_final_norm, v_norm1, v_w_in, v_pool_w, v_pool_scale, v_sg_norm, v_sg_w, v_sg_b, v_w_out, v_norm2, v_w_up, v_w_down, v_final_norm):
    depth = norm1.shape[0]
    T = x.shape[1]
    xs = x.reshape(T, D_MODEL)
    target = loss_target.reshape(T, D_MODEL)

    assert depth == 2
    c_idx = lax.axis_index("c")
    q_idx = 2 * lax.axis_index("x") + lax.axis_index("y")
    own = [w.astype(BF16) for w in (w_in, w_out, w_up, w_down)]
    gathered = {(0, 0): _gather_weights([own[0][:1]])[0]}

    def full(a, l, axis):
        blocks = lax.dynamic_update_slice(gathered[(a, l)], own[a][l][None, None], (q_idx, 0, 0, 0))[:, 0]
        if axis is None:
            return blocks
        if axis == 0:
            return blocks.reshape(-1, blocks.shape[-1])
        return jnp.concatenate([blocks[q] for q in range(N_CHIPS)], axis=axis)

    half_way = {}

    def gather_behind(call, keys, at_once):
        res, over_ici = call(_gather_over_ici([own[a][l:l + 1] for a, l in keys]))
        gathered.update(zip(keys[:at_once], _pass_to_sibling(over_ici[:at_once])))
        half_way.update(zip(keys[at_once:], over_ici[at_once:]))
        return res

    def pass_behind(call, keys):
        res, done = call(_pass_over_d2d([half_way.pop(k) for k in keys]))
        gathered.update(zip(keys, done))
        return res

    tril = jnp.tril(jnp.ones((CHUNK, CHUNK), F32))
    saved = []
    cur = xs
    wi, wo, wu, wd = {}, {}, {}, {}
    for l in range(depth):
        wbd = _block_diag(pool_w[l]).astype(BF16)
        wm = sg_w[l] * tril
        wm_s = wm.reshape(SG_HEADS * CHUNK, CHUNK).astype(BF16)
        wmt_s = jnp.swapaxes(wm, 1, 2).reshape(SG_HEADS * CHUNK, CHUNK).astype(BF16)
        bias = jnp.repeat(sg_b[l].T, SB_HD, axis=1)
        n1, n2 = norm1[l][None], norm2[l][None]
        psc, sgn = pool_scale[l][None], sg_norm[l][None]
        wi[l] = full(0, l, 1)
        proj, h, qkv = _inproj_fwd(cur, n1, wi[l])
        ya = _pool_fwd(proj, wbd, psc)
        yb = _sg_fwd(proj, wm_s, bias, sgn)
        if l == 0:
            yc = gather_behind(lambda hosted: _attn_fwd(qkv, hosted), [(1, 0), (2, 0), (3, 0)], 1)
            wo[l] = full(1, l, 0)
            x1, ymix = pass_behind(lambda hosted: _outproj_fwd(cur, ya, yb, yc, wo[l], hosted), [(2, 0), (3, 0)])
        else:
            yc = pass_behind(lambda hosted: _attn_fwd(qkv, hosted), [(1, l), (2, l), (3, l)])
            wo[l] = full(1, l, 0)
            (x1, ymix), _ = _outproj_fwd(cur, ya, yb, yc, wo[l])
        wu[l], wd[l] = full(2, l, None), full(3, l, 0)
        if l == 0:
            x2, h2, u, act = gather_behind(lambda hosted: _mlp_fwd(x1, n2, wu[l], wd[l], hosted),
                                           [(0, 1), (1, 1), (2, 1), (3, 1)], 1)
        else:
            (x2, h2, u, act), _ = _mlp_fwd(x1, n2, wu[l], wd[l])
        saved.append(dict(x0=cur, x1=x1, proj=proj, h=h, qkv=qkv, yc=yc, ymix=ymix, h2=h2, u=u, act=act,
                          wbd=wbd, wm_s=wm_s, wmt_s=wmt_s, bias=bias, n1=n1, n2=n2, psc=psc, sgn=sgn))
        cur = x2

    loss_row, dcur, d_final = _loss_head(cur, final_norm[None], target)

    small = [None] * depth
    grads, parts, reduced = {}, {}, {}

    def pair_up(keys, swapped):
        parts.update({k: _add_pair(grads[k], o, c_idx) for k, o in zip(keys, swapped)})

    def chip_up(keys, arrived):
        reduced.update({k: _add_chips(parts[k], r, q_idx) for k, r in zip(keys, arrived)})

    for l in reversed(range(depth)):
        s = saved[l]
        if l == 0:
            keys = [(2, 1), (3, 1)]
            (dx1, du, d_n2), arrived = _mlp_bwd(dcur, s["x1"], s["n2"], s["u"], wu[l], wd[l],
                                                _scatter_over_ici([parts[k] for k in keys]))
            chip_up(keys, arrived)
        else:
            (dx1, du, d_n2), _ = _mlp_bwd(dcur, s["x1"], s["n2"], s["u"], wu[l], wd[l])
        grads[(2, l)] = _tn_matmul(s["h2"], du, "grad_w_up", n_split=N_CHIPS)
        grads[(3, l)] = _tn_matmul(s["act"], dcur, "grad_w_down")[0].reshape(N_CHIPS, D_FF // N_CHIPS, D_MODEL)
        dymix = _nt_matmul(dx1, wo[l])
        grads[(1, l)] = _tn_matmul(s["ymix"], dx1, "grad_w_out")[0].reshape(N_CHIPS, D_MODEL // N_CHIPS, D_MODEL)
        da_in, d_wbd, d_psc = _pool_bwd(s["proj"], dymix, s["wbd"], s["psc"])
        if l == 0:
            keys = [(0, 1), (1, 0), (2, 0), (3, 0)]
            (du_pre, dv_pre, d_wm, d_bias, d_sgn), swapped = _sg_bwd(
                s["proj"], dymix, s["wm_s"], s["wmt_s"], s["bias"], s["sgn"], _swap_over_d2d([grads[k] for k in keys]))
            pair_up(keys, swapped)
            keys = [(1, 1)] + keys
            (dq, dk, dv), arrived = _attn_bwd(s["qkv"], s["yc"], dymix, _scatter_over_ici([parts[k] for k in keys]))
            chip_up(keys, arrived)
        else:
            (du_pre, dv_pre, d_wm, d_bias, d_sgn), _ = _sg_bwd(s["proj"], dymix, s["wm_s"], s["wmt_s"], s["bias"], s["sgn"])
            keys = [(1, l), (2, l), (3, l)]
            (dq, dk, dv), swapped = _attn_bwd(s["qkv"], s["yc"], dymix, _swap_over_d2d([grads[k] for k in keys]))
            pair_up(keys, swapped)
        pieces = [da_in, du_pre, dv_pre, dq, dk, dv]
        if l == 0:
            keys = sorted(reduced)
            g_in_l, swapped = _inproj_grad(s["h"], pieces, _swap_reduced_over_d2d([reduced[k] for k in keys]))
            theirs = dict(zip(keys, swapped))
        else:
            g_in_l, _ = _inproj_grad(s["h"], pieces)
        grads[(0, l)] = g_in_l[0].reshape(D_MODEL, N_CHIPS, IN_COLS // N_CHIPS).transpose(1, 0, 2)
        if l == 0:
            keys = [(0, 0)]
            pair_up(keys, _swap_halves([grads[k] for k in keys]))
            (dx0, d_n1), arrived = _inproj_bwd(pieces, wi[l], s["x0"], s["n1"], dx1,
                                               _scatter_over_ici([parts[k] for k in keys]))
            chip_up(keys, arrived)
        else:
            (dx0, d_n1), _ = _inproj_bwd(pieces, wi[l], s["x0"], s["n1"], dx1)
        d_pw = jnp.stack([d_wbd[gi * POOL_GW:(gi + 1) * POOL_GW, gi * POOL_GW:(gi + 1) * POOL_GW]
                          for gi in range(len(POOL_WINDOWS))])
        small[l] = dict(norm1=d_n1[0], pool_w=d_pw, pool_scale=d_psc[0], sg_norm=d_sgn[0],
                        sg_w=d_wm.reshape(SG_HEADS, CHUNK, CHUNK), sg_b=d_bias[:, :SG_HEADS].T, norm2=d_n2[0])
        dcur = dx0
    grad_x = dcur.reshape(x.shape)

    names =["norm1", "pool_w", "pool_scale", "sg_norm", "sg_w", "sg_b", "norm2"]
    slot = jnp.zeros((1,), F32)
    small_w = [norm1, pool_w, pool_scale, sg_norm, sg_w, sg_b, norm2, final_norm, slot]
    small_m = [m_norm1, m_pool_w, m_pool_scale, m_sg_norm, m_sg_w, m_sg_b, m_norm2, m_final_norm, slot]
    small_v = [v_norm1, v_pool_w, v_pool_scale, v_sg_norm, v_sg_w, v_sg_b, v_norm2, v_final_norm, slot]
    small_g = [jnp.stack([small[l][k] for l in range(depth)]) for k in names] + [d_final[0], loss_row[0, :1]]
    keys = [(0, 0)]
    g_packed, _ = _allreduce_small(_pack(small_g))
    theirs.update(zip(keys, _swap_reduced([reduced[k] for k in keys])))

    def joined(a):
        layers = []
        for l in range(depth):
            mine, other = reduced[(a, l)], theirs[(a, l)]
            layers.append(jnp.where(c_idx == 0, jnp.concatenate([mine, other]), jnp.concatenate([other, mine])))
        return jnp.stack(layers)

    gw_in, gw_out, gw_up, gw_down = [joined(a) for a in range(4)]

    loss = _unpack(g_packed, small_w)[-1][0]
    s_delta, s_m, s_v = _elementwise(_adamw, "adamw_small", [_pack(small_w), g_packed, _pack(small_m), _pack(small_v)], 3)
    gs = dict(zip(names + ["final_norm"], _unpack(g_packed, small_w)))
    ds = dict(zip(names + ["final_norm"], _unpack(s_delta, small_w)))
    ms = dict(zip(names + ["final_norm"], _unpack(s_m, small_w)))
    vs = dict(zip(names + ["final_norm"], _unpack(s_v, small_w)))

    big_g = dict(w_in=gw_in, w_out=gw_out, w_up=gw_up, w_down=gw_down)
    big_w = dict(w_in=(w_in, m_w_in, v_w_in), w_out=(w_out, m_w_out, v_w_out),
                 w_up=(w_up, m_w_up, v_w_up), w_down=(w_down, m_w_down, v_w_down))
    for k, (w, m, v) in big_w.items():
        operands = [w, big_g[k], m, v]
        if k == "w_in":
            operands = [jnp.swapaxes(o, 1, 2) for o in operands]
        ds[k], ms[k], vs[k] = _elementwise(_adamw, "adamw_" + k, operands, 3)
        if k == "w_in":
            ds[k], ms[k], vs[k] = [jnp.swapaxes(o, 1, 2) for o in (ds[k], ms[k], vs[k])]
        gs[k] = big_g[k]

    order = ["norm1", "w_in", "pool_w", "pool_scale", "sg_norm", "sg_w", "sg_b", "w_out", "norm2", "w_up", "w_down",
             "final_norm"]
    return (loss, grad_x, *[gs[k] for k in order], *[ds[k] for k in order], *[ms[k] for k in order],
            *[vs[k] for k in order])
```

```python
import functools

import jax
import jax.numpy as jnp
from jax import lax
from jax.experimental import pallas as pl
from jax.experimental.pallas import tpu as pltpu

F32 = jnp.float32
BF16 = jnp.bfloat16
MESH = pl.DeviceIdType.MESH
AXES = ("x", "y", "c")

EPS = 1e-6
D_MODEL = 1024
POOL_WIDTH = 256
SG_WIDTH = 256
SB_WIDTH = 512
POOL_WINDOWS = (2, 4, 8, 16)
POOL_GW = 64
POOL_HALO = 16
CHUNK = 128
SG_HEADS = 4
SB_HD = 64
SB_SCALE = 0.125
IN_COLS = 2304
QKV_OFF = 768
D_FF = 4096
N_CHIPS = 4
LANES = 128
VMEM_LIMIT = 56 * 1024 * 1024
MLP_CHUNK = 512
ATTN_TILE = 256
UNDERFLOW = -104.0

ADAM_LR = 0.001
ADAM_B1 = 0.9
ADAM_B2 = 0.999
ADAM_EPS = 1e-08
ADAM_WD = 0.01
ADAM_STEP = 10

HBM_SPEC = pl.BlockSpec(memory_space=pl.ANY)
VMEM_SPEC = pl.BlockSpec(memory_space=pltpu.VMEM)


def _params(**kw):
    return pltpu.CompilerParams(vmem_limit_bytes=VMEM_LIMIT, **kw)


def _tile(n, pref):
    if n <= pref:
        return n
    for t in range(pref - pref % LANES, 0, -LANES):
        if n % t == 0:
            return t
    raise ValueError((n, pref))


def _nn(a, b):
    return jnp.dot(a, b, preferred_element_type=F32)


def _nt(a, b):
    return lax.dot_general(a, b, (((1,), (1,)), ((), ())), preferred_element_type=F32)


def _tn(a, b):
    return lax.dot_general(a, b, (((0,), (0,)), ((), ())), preferred_element_type=F32)


def _rms_fwd(x, g):
    r = lax.rsqrt(jnp.mean(x * x, axis=-1, keepdims=True) + EPS)
    xhat = x * r
    return xhat * g, xhat, r


def _rms_bwd(dy, xhat, r, g):
    dxhat = dy * g
    dx = r * (dxhat - xhat * jnp.mean(dxhat * xhat, axis=-1, keepdims=True))
    return dx, dy * xhat


_GELU_K = 0.7978845608028654
_GELU_C = 0.044715


def _gelu(x):
    return 0.5 * x * (1.0 + jnp.tanh(_GELU_K * (x + _GELU_C * x * x * x)))


def _gelu_grad(x):
    t = jnp.tanh(_GELU_K * (x + _GELU_C * x * x * x))
    return 0.5 * (1.0 + t) + 0.5 * x * (1.0 - t * t) * _GELU_K * (1.0 + 3.0 * _GELU_C * x * x)


def _inproj_fwd(x, g, w):
    T, D = x.shape
    N = w.shape[1]
    tt = _tile(T, 512)

    def body(x_ref, g_ref, w_ref, proj_ref, h_ref, qkv_ref):
        h, _, _ = _rms_fwd(x_ref[...], g_ref[...])
        hb = h.astype(BF16)
        h_ref[...] = hb
        p = _nn(hb, w_ref[...])
        proj_ref[...] = p[:, :QKV_OFF]
        qkv_ref[...] = p[:, QKV_OFF:].astype(BF16)

    return pl.pallas_call(
        body, name="inproj_fwd", grid=(T // tt,),
        in_specs=[pl.BlockSpec((tt, D), lambda i: (i, 0)), pl.BlockSpec((1, D), lambda i: (0, 0)),
                  pl.BlockSpec((D, N), lambda i: (0, 0))],
        out_specs=[pl.BlockSpec((tt, QKV_OFF), lambda i: (i, 0)), pl.BlockSpec((tt, D), lambda i: (i, 0)),
                   pl.BlockSpec((tt, N - QKV_OFF), lambda i: (i, 0))],
        out_shape=[jax.ShapeDtypeStruct((T, QKV_OFF), F32), jax.ShapeDtypeStruct((T, D), BF16),
                   jax.ShapeDtypeStruct((T, N - QKV_OFF), BF16)],
        compiler_params=_params(),
    )(x, g, w)


def _inproj_bwd(pieces, w, x, g, dres, hosted=None):
    T, D = x.shape
    N = w.shape[1]
    tt = _tile(T, 512)
    nt = T // tt
    widths = [p.shape[1] for p in pieces]
    offs = [sum(widths[:k]) for k in range(len(widths))]
    assert sum(widths) == N
    n_p = len(pieces)

    def body(*refs):
        i = pl.program_id(0)
        own, start, wait = _host(hosted, refs, n_p + 4, 2, i == 0, i == nt - 1)
        start()
        p_refs = own[:n_p]
        w_ref, x_ref, g_ref, dres_ref, dx_ref, dg_ref, dproj_ref = own[n_p:]
        for p_ref, o, wd in zip(p_refs, offs, widths):
            dproj_ref[:, o:o + wd] = p_ref[...].astype(BF16)
        dh = _nt(dproj_ref[...], w_ref[...])
        gv = g_ref[...]
        _, xhat, r = _rms_fwd(x_ref[...], gv)
        dx, dgrow = _rms_bwd(dh, xhat, r, gv)
        dx_ref[...] = dres_ref[...] + dx

        @pl.when(i == 0)
        def _():
            dg_ref[...] = jnp.zeros_like(dg_ref)

        dg_ref[...] += jnp.sum(dgrow, axis=0, keepdims=True)
        wait()

    h_ins = hosted.ins if hosted else []
    res = pl.pallas_call(
        body, name="inproj_bwd_hosting" if hosted else "inproj_bwd", grid=(nt,),
        in_specs=[pl.BlockSpec((tt, wd), lambda i: (i, 0)) for wd in widths] + [
            pl.BlockSpec((D, N), lambda i: (0, 0)), pl.BlockSpec((tt, D), lambda i: (i, 0)),
            pl.BlockSpec((1, D), lambda i: (0, 0)), pl.BlockSpec((tt, D), lambda i: (i, 0))] + [HBM_SPEC] * len(h_ins),
        out_specs=[pl.BlockSpec((tt, D), lambda i: (i, 0)), pl.BlockSpec((1, D), lambda i: (0, 0))]
        + [HBM_SPEC] * len(h_ins),
        out_shape=[jax.ShapeDtypeStruct((T, D), F32), jax.ShapeDtypeStruct((1, D), F32)]
        + (hosted.out_shapes if hosted else []),
        scratch_shapes=[pltpu.VMEM((tt, N), BF16)] + (hosted.sems() if hosted else []),
        compiler_params=_params(has_side_effects=hosted is not None),
    )(*pieces, w, x, g, dres, *h_ins)
    return res[:2], res[2:]


def _inproj_grad(h, pieces, hosted=None):
    T, D = h.shape
    tt = _tile(T, 1024)
    nt = T // tt
    widths = [p.shape[1] for p in pieces]
    offs = [sum(widths[:k]) for k in range(len(widths))]
    N = sum(widths)
    n_p = len(pieces)

    def body(*refs):
        t = pl.program_id(0)
        own, start, wait = _host(hosted, refs, n_p + 1, 1, t == 0, t == nt - 1)
        start()
        h_ref, p_refs, o_ref = own[0], own[1:1 + n_p], own[1 + n_p]

        @pl.when(t == 0)
        def _():
            o_ref[...] = jnp.zeros_like(o_ref)

        hv = h_ref[...]
        for p_ref, o, wd in zip(p_refs, offs, widths):
            o_ref[:, o:o + wd] += _tn(hv, p_ref[...].astype(BF16))
        wait()

    h_ins = hosted.ins if hosted else []
    res = pl.pallas_call(
        body, name="grad_w_in_hosting" if hosted else "grad_w_in", grid=(nt,),
        in_specs=[pl.BlockSpec((tt, D), lambda t: (t, 0))] + [pl.BlockSpec((tt, wd), lambda t: (t, 0)) for wd in widths]
        + [HBM_SPEC] * len(h_ins),
        out_specs=[pl.BlockSpec((None, D, N), lambda t: (0, 0, 0))] + [HBM_SPEC] * len(h_ins),
        out_shape=[jax.ShapeDtypeStruct((1, D, N), F32)] + (hosted.out_shapes if hosted else []),
        scratch_shapes=hosted.sems() if hosted else [],
        compiler_params=_params(has_side_effects=hosted is not None),
    )(h, *pieces, *h_ins)
    return res[0], res[1:]


def _pool_select(s2, s4, s8, s16, grp):
    return jnp.where(grp == 0, s2, jnp.where(grp == 1, s4, jnp.where(grp == 2, s8, s16)))


def _pool_count(t_glob, grp):
    win = jnp.where(grp == 0, 2, jnp.where(grp == 1, 4, jnp.where(grp == 2, 8, 16)))
    return jnp.minimum(t_glob + 1, win).astype(F32)


def _pool_diff(a, halo, base, tt):
    n = tt + POOL_HALO
    ext = jnp.concatenate([halo, a], axis=0)
    s2 = ext + pltpu.roll(ext, 1, 0)
    s4 = s2 + pltpu.roll(s2, 2, 0)
    s8 = s4 + pltpu.roll(s4, 4, 0)
    s16 = s8 + pltpu.roll(s8, 8, 0)
    grp = lax.broadcasted_iota(jnp.int32, (n, POOL_WIDTH), 1) // POOL_GW
    t_glob = lax.broadcasted_iota(jnp.int32, (n, POOL_WIDTH), 0) + (base - POOL_HALO)
    pooled = _pool_select(s2, s4, s8, s16, grp) / _pool_count(t_glob, grp)
    return pooled[POOL_HALO:] - a


def _pool_specs(T, tt):
    hb = tt // POOL_HALO
    return [pl.BlockSpec((tt, POOL_WIDTH), lambda i: (i, 0)),
            pl.BlockSpec((POOL_HALO, POOL_WIDTH), lambda i: (jnp.maximum(i * hb - 1, 0), 0))]


def _pool_fwd(proj, wbd, scale):
    T = proj.shape[0]
    tt = _tile(T, 512)

    def body(a_ref, halo_ref, w_ref, sc_ref, y_ref):
        i = pl.program_id(0)
        halo = jnp.where(i > 0, halo_ref[...], 0.0)
        d = _pool_diff(a_ref[...], halo, i * tt, tt)
        y_ref[...] = _nn(d.astype(BF16), w_ref[...]) * sc_ref[...]

    return pl.pallas_call(
        body, name="pool_fwd", grid=(T // tt,),
        in_specs=_pool_specs(T, tt) + [pl.BlockSpec((POOL_WIDTH, POOL_WIDTH), lambda i: (0, 0)),
                                       pl.BlockSpec((1, POOL_WIDTH), lambda i: (0, 0))],
        out_specs=pl.BlockSpec((tt, POOL_WIDTH), lambda i: (i, 0)),
        out_shape=jax.ShapeDtypeStruct((T, POOL_WIDTH), F32),
        compiler_params=_params(),
    )(proj, proj, wbd, scale)


def _pool_bwd(proj, dymix, wbd, scale):
    T = proj.shape[0]
    tt = _tile(T, 512)
    hb = tt // POOL_HALO
    nblk = T // tt
    n = tt + POOL_HALO

    def body(a_ref, halo_ref, dy_ref, dyn_ref, w_ref, sc_ref, da_ref, dw_ref, dsc_ref):
        i = pl.program_id(0)
        halo = jnp.where(i > 0, halo_ref[...], 0.0)
        d = _pool_diff(a_ref[...], halo, i * tt, tt)
        db = d.astype(BF16)
        wv = w_ref[...]
        sc = sc_ref[...]
        dy = dy_ref[...]
        dys = dy * sc

        @pl.when(i == 0)
        def _():
            dw_ref[...] = jnp.zeros_like(dw_ref)
            dsc_ref[...] = jnp.zeros_like(dsc_ref)

        dsc_ref[...] += jnp.sum(dy * _nn(db, wv), axis=0, keepdims=True)
        dw_ref[...] += _tn(db, dys.astype(BF16))
        dyn = jnp.where(i < nblk - 1, dyn_ref[...], 0.0) * sc
        dd = _nt(jnp.concatenate([dys, dyn], axis=0).astype(BF16), wv)
        grp = lax.broadcasted_iota(jnp.int32, (n, POOL_WIDTH), 1) // POOL_GW
        t_glob = lax.broadcasted_iota(jnp.int32, (n, POOL_WIDTH), 0) + i * tt
        e = dd / _pool_count(t_glob, grp)
        r2 = e + pltpu.roll(e, n - 1, 0)
        r4 = r2 + pltpu.roll(r2, n - 2, 0)
        r8 = r4 + pltpu.roll(r4, n - 4, 0)
        r16 = r8 + pltpu.roll(r8, n - 8, 0)
        da_ref[...] = (_pool_select(r2, r4, r8, r16, grp) - dd)[:tt].astype(BF16)

    return pl.pallas_call(
        body, name="pool_bwd", grid=(nblk,),
        in_specs=_pool_specs(T, tt) + [
            pl.BlockSpec((tt, POOL_WIDTH), lambda i: (i, 0)),
            pl.BlockSpec((POOL_HALO, POOL_WIDTH), lambda i: (jnp.minimum((i + 1) * hb, T // POOL_HALO - 1), 0)),
            pl.BlockSpec((POOL_WIDTH, POOL_WIDTH), lambda i: (0, 0)), pl.BlockSpec((1, POOL_WIDTH), lambda i: (0, 0))],
        out_specs=[pl.BlockSpec((tt, POOL_WIDTH), lambda i: (i, 0)),
                   pl.BlockSpec((POOL_WIDTH, POOL_WIDTH), lambda i: (0, 0)),
                   pl.BlockSpec((1, POOL_WIDTH), lambda i: (0, 0))],
        out_shape=[jax.ShapeDtypeStruct((T, POOL_WIDTH), BF16),
                   jax.ShapeDtypeStruct((POOL_WIDTH, POOL_WIDTH), F32),
                   jax.ShapeDtypeStruct((1, POOL_WIDTH), F32)],
        compiler_params=_params(),
    )(proj, proj, dymix, dymix, wbd, scale)


def _head_select(stacked, grp):
    out = jnp.where(grp == 0, stacked[0:CHUNK], 0.0)
    for h in range(1, SG_HEADS):
        out = out + jnp.where(grp == h, stacked[h * CHUNK:(h + 1) * CHUNK], 0.0)
    return out


def _sg_specs(tt):
    return [pl.BlockSpec((tt, SG_WIDTH), lambda i: (i, 1)), pl.BlockSpec((tt, SG_WIDTH), lambda i: (i, 2))]


def _sg_fwd(proj, wm, bias, g):
    T = proj.shape[0]
    tt = _tile(T, 512)

    def body(u_ref, v_ref, wm_ref, b_ref, g_ref, y_ref):
        zu = _gelu(u_ref[...])
        vn, _, _ = _rms_fwd(_gelu(v_ref[...]), g_ref[...])
        grp = lax.broadcasted_iota(jnp.int32, (CHUNK, SG_WIDTH), 1) // SB_HD
        for n in range(tt // CHUNK):
            rows = slice(n * CHUNK, (n + 1) * CHUNK)
            sv = _head_select(_nn(wm_ref[...], vn[rows].astype(BF16)), grp) + b_ref[...]
            y_ref[rows, :] = zu[rows] * sv

    return pl.pallas_call(
        body, name="sg_fwd", grid=(T // tt,),
        in_specs=_sg_specs(tt) + [pl.BlockSpec((SG_HEADS * CHUNK, CHUNK), lambda i: (0, 0)),
                                  pl.BlockSpec((CHUNK, SG_WIDTH), lambda i: (0, 0)),
                                  pl.BlockSpec((1, SG_WIDTH), lambda i: (0, 0))],
        out_specs=pl.BlockSpec((tt, SG_WIDTH), lambda i: (i, 0)),
        out_shape=jax.ShapeDtypeStruct((T, SG_WIDTH), F32),
        compiler_params=_params(),
    )(proj, proj, wm, bias, g)


def _sg_bwd(proj, dymix, wm, wmt, bias, g, hosted=None):
    T = proj.shape[0]
    tt = _tile(T, 512)
    nblk = T // tt

    def body(*refs):
        i = pl.program_id(0)
        (u_ref, v_ref, dy_ref, wm_ref, wmt_ref, b_ref, g_ref, du_ref, dv_ref, dw_ref, db_ref, dg_ref,
         dvn_ref, dbias_ref), start, wait = _host(hosted, refs, 7, 5, i == 0, i == nblk - 1)
        start()
        up, vp = u_ref[...], v_ref[...]
        gv = g_ref[...]
        zu, zv = _gelu(up), _gelu(vp)
        vn, xhat, r = _rms_fwd(zv, gv)
        gu = _gelu_grad(up)
        grp = lax.broadcasted_iota(jnp.int32, (CHUNK, SG_WIDTH), 1) // SB_HD

        @pl.when(i == 0)
        def _():
            dw_ref[...] = jnp.zeros_like(dw_ref)
            dbias_ref[...] = jnp.zeros_like(dbias_ref)
            dg_ref[...] = jnp.zeros_like(dg_ref)

        for n in range(tt // CHUNK):
            rows = slice(n * CHUNK, (n + 1) * CHUNK)
            vc = vn[rows].astype(BF16)
            sv = _head_select(_nn(wm_ref[...], vc), grp) + b_ref[...]
            dy = dy_ref[rows, :]
            du_ref[rows, :] = (dy * sv * gu[rows]).astype(BF16)
            dsv = dy * zu[rows]
            dsvb = dsv.astype(BF16)
            dvn_ref[rows, :] = _head_select(_nn(wmt_ref[...], dsvb), grp)
            stacked = jnp.concatenate([jnp.where(grp == h, dsv, 0.0) for h in range(SG_HEADS)], axis=0)
            dw_ref[...] += _nt(stacked.astype(BF16), vc)
            dbias_ref[...] += dsv

        dzv, dgrow = _rms_bwd(dvn_ref[...], xhat, r, gv)
        dg_ref[...] += jnp.sum(dgrow, axis=0, keepdims=True)
        dv_ref[...] = (dzv * _gelu_grad(vp)).astype(BF16)

        @pl.when(i == nblk - 1)
        def _():
            t_i = lax.broadcasted_iota(jnp.int32, (SG_HEADS * CHUNK, CHUNK), 0) % CHUNK
            s_i = lax.broadcasted_iota(jnp.int32, (SG_HEADS * CHUNK, CHUNK), 1)
            dw_ref[...] = jnp.where(s_i <= t_i, dw_ref[...], 0.0)
            lane = lax.broadcasted_iota(jnp.int32, (CHUNK, LANES), 1)
            acc = jnp.zeros((CHUNK, LANES), F32)
            for h in range(SG_HEADS):
                tot = jnp.sum(jnp.where(grp == h, dbias_ref[...], 0.0), axis=1, keepdims=True)
                acc = acc + jnp.where(lane == h, tot, 0.0)
            db_ref[...] = acc

        wait()

    h_ins = hosted.ins if hosted else []
    res = pl.pallas_call(
        body, name="sg_bwd_hosting" if hosted else "sg_bwd", grid=(nblk,),
        in_specs=_sg_specs(tt) + [pl.BlockSpec((tt, SG_WIDTH), lambda i: (i, 1)),
                                  pl.BlockSpec((SG_HEADS * CHUNK, CHUNK), lambda i: (0, 0)),
                                  pl.BlockSpec((SG_HEADS * CHUNK, CHUNK), lambda i: (0, 0)),
                                  pl.BlockSpec((CHUNK, SG_WIDTH), lambda i: (0, 0)),
                                  pl.BlockSpec((1, SG_WIDTH), lambda i: (0, 0))] + [HBM_SPEC] * len(h_ins),
        out_specs=[pl.BlockSpec((tt, SG_WIDTH), lambda i: (i, 0)), pl.BlockSpec((tt, SG_WIDTH), lambda i: (i, 0)),
                   pl.BlockSpec((SG_HEADS * CHUNK, CHUNK), lambda i: (0, 0)),
                   pl.BlockSpec((CHUNK, LANES), lambda i: (0, 0)), pl.BlockSpec((1, SG_WIDTH), lambda i: (0, 0))]
        + [HBM_SPEC] * len(h_ins),
        out_shape=[jax.ShapeDtypeStruct((T, SG_WIDTH), BF16), jax.ShapeDtypeStruct((T, SG_WIDTH), BF16),
                   jax.ShapeDtypeStruct((SG_HEADS * CHUNK, CHUNK), F32),
                   jax.ShapeDtypeStruct((CHUNK, LANES), F32), jax.ShapeDtypeStruct((1, SG_WIDTH), F32)]
        + (hosted.out_shapes if hosted else []),
        scratch_shapes=[pltpu.VMEM((tt, SG_WIDTH), F32), pltpu.VMEM((CHUNK, SG_WIDTH), F32)]
        + (hosted.sems() if hosted else []),
        compiler_params=_params(has_side_effects=hosted is not None),
    )(proj, proj, dymix, wm, wmt, bias, g, *h_ins)
    return res[:5], res[5:]


def _split_dot(x, u):
    hi = x.astype(BF16)
    lo = (x - hi.astype(F32)).astype(BF16)
    return _nn(hi, u) + _nn(lo, u)


def _sb_logits(z):
    lb = jnp.minimum(z, 0.0) - jnp.log(1.0 + jnp.exp(-jnp.abs(z)))
    return lb, lb - z


ATTN_STRIP = 32
ATTN_SUBS = 2


def _by_strips(n_rows, fn):
    parts = None
    for r in range(0, n_rows, ATTN_STRIP):
        res = fn(slice(r, r + ATTN_STRIP))
        parts = [[v] for v in res] if parts is None else [p + [v] for p, v in zip(parts, res)]
    return [jnp.concatenate(p, axis=0) for p in parts]


def _attn_qkv_specs(tq, T):
    base = (IN_COLS - 3 * SB_WIDTH - QKV_OFF) // LANES
    nb = SB_WIDTH // LANES
    return [pl.BlockSpec((tq, LANES), lambda p, i: (i, base + p)),
            pl.BlockSpec((T, LANES), lambda p, i: (0, base + nb + p)),
            pl.BlockSpec((T, LANES), lambda p, i: (0, base + 2 * nb + p))]


class _Hosted:
    def __init__(self, ins, out_shapes, n_sems, copies, in_place=False):
        self.ins, self.out_shapes, self.n_sems, self.copies = ins, out_shapes, n_sems, copies
        self.in_place = in_place

    @property
    def n(self):
        return len(self.ins)

    def aliases(self, n_in, n_out):
        return {n_in + k: n_out + k for k in range(self.n)} if self.in_place else {}

    def sems(self):
        return [pltpu.SemaphoreType.DMA((self.n_sems,)), pltpu.SemaphoreType.DMA((self.n_sems,))]

    def start(self, src, dst, ssem, rsem):
        for send, _ in self.copies(src, dst, ssem, rsem):
            send.start()

    def wait(self, src, dst, ssem, rsem):
        for send, recv in self.copies(src, dst, ssem, rsem):
            recv.wait_recv()
            send.wait_send()


def _host(hosted, refs, n_in, n_out, first, last):
    if hosted is None:
        return refs, lambda: None, lambda: None
    n = hosted.n
    own_in, h_in = refs[:n_in], refs[n_in:n_in + n]
    own_out, h_out = refs[n_in + n:n_in + n + n_out], refs[n_in + n + n_out:n_in + 2 * n + n_out]
    rest = refs[n_in + 2 * n + n_out:]
    ssem, rsem = rest[-2:]

    def start():
        if first is True:
            hosted.start(h_in, h_out, ssem, rsem)
        else:
            pl.when(first)(lambda: hosted.start(h_in, h_out, ssem, rsem))

    def wait():
        if last is True:
            hosted.wait(h_in, h_out, ssem, rsem)
        else:
            pl.when(last)(lambda: hosted.wait(h_in, h_out, ssem, rsem))

    return own_in + own_out + rest[:-2], start, wait


def _attn_fwd(qkv, hosted=None):
    T = qkv.shape[0]
    tk = _tile(T, ATTN_TILE)
    n_sub = ATTN_SUBS if T % (ATTN_SUBS * tk) == 0 else 1
    tq = n_sub * tk
    n_p, nq = SB_WIDTH // LANES, T // tq

    def body(*refs):
        p, i = pl.program_id(0), pl.program_id(1)
        (q_ref, k_ref, v_ref, o_ref), start, wait = _host(
            hosted, refs, 3, 1, jnp.logical_and(p == 0, i == 0), jnp.logical_and(p == n_p - 1, i == nq - 1))
        start()
        lane = lax.broadcasted_iota(jnp.int32, (tk, LANES), 1)
        row = lax.broadcasted_iota(jnp.int32, (tk, tk), 0)
        col = lax.broadcasted_iota(jnp.int32, (tk, tk), 1)
        after = jnp.where(row > col, 1.0, 0.0).astype(BF16)
        valid = col < row
        qh = {}
        for sb in range(n_sub):
            q = q_ref[sb * tk:(sb + 1) * tk, :].astype(F32)
            for hh in range(2):
                qh[(sb, hh)] = jnp.where((lane // SB_HD) == hh, q * SB_SCALE, 0.0).astype(BF16)

        def tiles(todo, state):
            chains = [(n, hh) for n in range(len(todo)) for hh in range(2)]
            kv = []
            for _, j, _ in todo:
                ks = pl.ds(pl.multiple_of(j * tk, tk), tk)
                kv.append((k_ref[ks, :], v_ref[ks, :]))
            z = {(n, hh): _nt(qh[(todo[n][0], hh)], kv[n][0]) for n, hh in chains}
            lb, lmb, lm_sum = {}, {}, {}
            for n, hh in chains:
                def logits(rows, z=z[(n, hh)], mask=todo[n][2]):
                    lb, lm = _sb_logits(z[rows])
                    if mask is not None:
                        lm = jnp.where(mask[rows], lm, 0.0)
                    return lb, lm.astype(BF16), jnp.sum(lm, axis=1, keepdims=True)

                lb[(n, hh)], lmb[(n, hh)], lm_sum[(n, hh)] = _by_strips(tk, logits)
            x = {c: _nn(lmb[c], after) for c in chains}
            new = dict(state)
            for n, hh in chains:
                key = (todo[n][0], hh)
                carry, acc = new[key]

                def weights(rows, lb=lb[(n, hh)], x=x[(n, hh)], carry=carry, mask=todo[n][2]):
                    a = jnp.exp(lb[rows] + x[rows] + carry[rows])
                    if mask is not None:
                        a = jnp.where(mask[rows], a, 0.0)
                    return (a.astype(BF16),)

                (ab,) = _by_strips(tk, weights)
                new[key] = (carry + lm_sum[(n, hh)], acc + _nn(ab, kv[n][1]))
            return new

        def live(state, sb):
            return jnp.maximum(jnp.max(state[(sb, 0)][0]), jnp.max(state[(sb, 1)][0]))

        first = n_sub * i
        zero = (jnp.zeros((tk, 1), F32), jnp.zeros((tk, LANES), F32))
        todo = []
        for sb in range(n_sub):
            gate = jnp.broadcast_to(first > 0, (tk, tk)) if sb == 0 else None
            todo += [(sb, first + sb, valid), (sb, jnp.maximum(first + sb - 1, 0), gate)]
        state = tiles(todo, {(sb, hh): zero for sb in range(n_sub) for hh in range(2)})
        for sb in range(n_sub):
            def cond(st):
                return jnp.logical_and(st[0] >= 0, st[2] > UNDERFLOW)

            def step(st, sb=sb):
                mine = tiles([(sb, st[0], None)], st[1])
                return st[0] - 1, mine, live(mine, sb)

            mine = {k: v for k, v in state.items() if k[0] == sb}
            _, mine, _ = lax.while_loop(cond, step, (first + sb - 2, mine, live(mine, sb)))
            o_ref[sb * tk:(sb + 1) * tk, :] = jnp.where(lane < SB_HD, mine[(sb, 0)][1], mine[(sb, 1)][1])
        wait()

    h_ins = hosted.ins if hosted else []
    res = pl.pallas_call(
        body, name="attn_fwd_hosting" if hosted else "attn_fwd", grid=(n_p, nq),
        in_specs=_attn_qkv_specs(tq, T) + [HBM_SPEC] * len(h_ins),
        out_specs=[pl.BlockSpec((tq, LANES), lambda p, i: (i, p))] + [HBM_SPEC] * len(h_ins),
        out_shape=[jax.ShapeDtypeStruct((T, SB_WIDTH), F32)] + (hosted.out_shapes if hosted else []),
        input_output_aliases=hosted.aliases(3, 1) if hosted else {},
        scratch_shapes=hosted.sems() if hosted else [],
        compiler_params=_params(has_side_effects=hosted is not None),
    )(qkv, qkv, qkv, *h_ins)
    return res[0], res[1:]


def _attn_bwd(qkv, o, dymix, hosted=None):
    T = qkv.shape[0]
    tk = _tile(T, ATTN_TILE)
    n_sub = ATTN_SUBS if T % (ATTN_SUBS * tk) == 0 else 1
    tq = n_sub * tk
    n_p, nq = SB_WIDTH // LANES, T // tq
    yc_blk = (POOL_WIDTH + SG_WIDTH) // LANES

    def body(*refs):
        p, i = pl.program_id(0), pl.program_id(1)
        (q_ref, k_ref, v_ref, o_ref, do_ref, dq_ref, dk_ref, dv_ref), start, wait = _host(
            hosted, refs, 5, 3, jnp.logical_and(p == 0, i == 0), jnp.logical_and(p == n_p - 1, i == nq - 1))
        start()
        lane = lax.broadcasted_iota(jnp.int32, (tk, LANES), 1)
        row = lax.broadcasted_iota(jnp.int32, (tk, tk), 0)
        col = lax.broadcasted_iota(jnp.int32, (tk, tk), 1)
        after = jnp.where(row > col, 1.0, 0.0).astype(BF16)
        from_here = jnp.where(row >= col, 1.0, 0.0).astype(BF16)
        from_here2 = jnp.concatenate([from_here, from_here], axis=0)
        valid = col < row

        @pl.when(i == 0)
        def _():
            dk_ref[...] = jnp.zeros_like(dk_ref)
            dv_ref[...] = jnp.zeros_like(dv_ref)

        qh, dohb, delta = {}, {}, {}
        for sb in range(n_sub):
            rows = slice(sb * tk, (sb + 1) * tk)
            q, ov, dov = q_ref[rows, :].astype(F32), o_ref[rows, :], do_ref[rows, :]
            for hh in range(2):
                head = (lane // SB_HD) == hh
                qh[(sb, hh)] = jnp.where(head, q * SB_SCALE, 0.0).astype(BF16)
                dohb[(sb, hh)] = jnp.where(head, dov, 0.0).astype(BF16)
                delta[(sb, hh)] = jnp.sum(dohb[(sb, hh)].astype(F32) * ov, axis=1, keepdims=True)

        def tiles(todo, state):
            chains = [(n, hh) for n in range(len(todo)) for hh in range(2)]
            kv, where = [], []
            for _, j, _ in todo:
                ks = pl.ds(pl.multiple_of(j * tk, tk), tk)
                where.append(ks)
                kv.append((k_ref[ks, :], v_ref[ks, :]))
            z = {(n, hh): _nt(qh[(todo[n][0], hh)], kv[n][0]) for n, hh in chains}
            da = {(n, hh): _nt(dohb[(todo[n][0], hh)], kv[n][1]) for n, hh in chains}
            lb, lmb, lm_sum = {}, {}, {}
            for n, hh in chains:
                def logits(rows, z=z[(n, hh)], mask=todo[n][2]):
                    lb, lm = _sb_logits(z[rows])
                    if mask is not None:
                        lm = jnp.where(mask[rows], lm, 0.0)
                    return lb, lm.astype(BF16), jnp.sum(lm, axis=1, keepdims=True)

                lb[(n, hh)], lmb[(n, hh)], lm_sum[(n, hh)] = _by_strips(tk, logits)
            x = {c: _nn(lmb[c], after) for c in chains}
            c_a = {k: v[0] for k, v in state.items()}
            ab, g, g_split, g_sum = {}, {}, {}, {}
            for n, hh in chains:
                key = (todo[n][0], hh)

                def weights(rows, lb=lb[(n, hh)], x=x[(n, hh)], da=da[(n, hh)], c_a=c_a[key], mask=todo[n][2]):
                    a = jnp.exp(lb[rows] + x[rows] + c_a[rows])
                    if mask is not None:
                        a = jnp.where(mask[rows], a, 0.0)
                    ab = a.astype(BF16)
                    g = da[rows] * ab.astype(F32)
                    hi = g.astype(BF16)
                    lo = (g - hi.astype(F32)).astype(BF16)
                    return ab, g, jnp.concatenate([hi, lo], axis=1), jnp.sum(g, axis=1, keepdims=True)

                ab[(n, hh)], g[(n, hh)], g_split[(n, hh)], g_sum[(n, hh)] = _by_strips(tk, weights)
                c_a[key] = c_a[key] + lm_sum[(n, hh)]
            right = {c: _nn(g_split[c], from_here2) for c in chains}
            c_r = {k: v[1] for k, v in state.items()}
            dzb = {}
            for n, hh in chains:
                key = (todo[n][0], hh)

                def logit_grads(rows, lb=lb[(n, hh)], g=g[(n, hh)], right=right[(n, hh)], c_r=c_r[key],
                                delta=delta[key], mask=todo[n][2]):
                    sig = jnp.exp(lb[rows])
                    left = delta[rows] - (c_r[rows] + right[rows])
                    dz = g[rows] * (1.0 - sig) - left * sig
                    if mask is not None:
                        dz = jnp.where(mask[rows], dz, 0.0)
                    return (dz.astype(BF16),)

                (dzb[(n, hh)],) = _by_strips(tk, logit_grads)
                c_r[key] = c_r[key] + g_sum[(n, hh)]
            dqa = {k: v[2] for k, v in state.items()}
            for n in range(len(todo)):
                sb = todo[n][0]
                dk_ref[where[n], :] += _tn(dzb[(n, 0)], qh[(sb, 0)]) + _tn(dzb[(n, 1)], qh[(sb, 1)])
                dv_ref[where[n], :] += _tn(ab[(n, 0)], dohb[(sb, 0)]) + _tn(ab[(n, 1)], dohb[(sb, 1)])
                for hh in range(2):
                    dqa[(sb, hh)] = dqa[(sb, hh)] + _nn(dzb[(n, hh)], kv[n][0])
            return {k: (c_a[k], c_r[k], dqa[k]) for k in state}

        def live(state, sb):
            return jnp.maximum(jnp.max(state[(sb, 0)][0]), jnp.max(state[(sb, 1)][0]))

        first = n_sub * i
        zero = (jnp.zeros((tk, 1), F32), jnp.zeros((tk, 1), F32), jnp.zeros((tk, LANES), F32))
        todo = []
        for sb in range(n_sub):
            gate = jnp.broadcast_to(first > 0, (tk, tk)) if sb == 0 else None
            todo += [(sb, first + sb, valid), (sb, jnp.maximum(first + sb - 1, 0), gate)]
        state = tiles(todo, {(sb, hh): zero for sb in range(n_sub) for hh in range(2)})
        for sb in range(n_sub):
            def cond(st):
                return jnp.logical_and(st[0] >= 0, st[2] > UNDERFLOW)

            def step(st, sb=sb):
                mine = tiles([(sb, st[0], None)], st[1])
                return st[0] - 1, mine, live(mine, sb)

            mine = {k: v for k, v in state.items() if k[0] == sb}
            _, mine, _ = lax.while_loop(cond, step, (first + sb - 2, mine, live(mine, sb)))
            dq_ref[sb * tk:(sb + 1) * tk, :] = (
                jnp.where(lane < SB_HD, mine[(sb, 0)][2], mine[(sb, 1)][2]) * SB_SCALE).astype(BF16)
        wait()

    h_ins = hosted.ins if hosted else []
    res = pl.pallas_call(
        body, name="attn_bwd_hosting" if hosted else "attn_bwd", grid=(n_p, nq),
        in_specs=_attn_qkv_specs(tq, T) + [pl.BlockSpec((tq, LANES), lambda p, i: (i, p)),
                                           pl.BlockSpec((tq, LANES), lambda p, i: (i, yc_blk + p))]
        + [HBM_SPEC] * len(h_ins),
        out_specs=[pl.BlockSpec((tq, LANES), lambda p, i: (i, p)), pl.BlockSpec((T, LANES), lambda p, i: (0, p)),
                   pl.BlockSpec((T, LANES), lambda p, i: (0, p))] + [HBM_SPEC] * len(h_ins),
        out_shape=[jax.ShapeDtypeStruct((T, SB_WIDTH), BF16)] + [jax.ShapeDtypeStruct((T, SB_WIDTH), F32)] * 2
        + (hosted.out_shapes if hosted else []),
        scratch_shapes=hosted.sems() if hosted else [],
        compiler_params=_params(has_side_effects=hosted is not None),
    )(qkv, qkv, qkv, o, dymix, *h_ins)
    return res[:3], res[3:]


def _outproj_fwd(x, ya, yb, yc, w, hosted=None):
    T, D = x.shape
    tt = _tile(T, 512)
    nt = T // tt

    def body(*refs):
        i = pl.program_id(0)
        (x_ref, ya_ref, yb_ref, yc_ref, w_ref, x1_ref, ymix_ref), start, wait = _host(
            hosted, refs, 5, 2, i == 0, i == nt - 1)
        start()
        ymix_ref[:, 0:POOL_WIDTH] = ya_ref[...].astype(BF16)
        ymix_ref[:, POOL_WIDTH:POOL_WIDTH + SG_WIDTH] = yb_ref[...].astype(BF16)
        ymix_ref[:, POOL_WIDTH + SG_WIDTH:] = yc_ref[...].astype(BF16)
        x1_ref[...] = x_ref[...] + _nn(ymix_ref[...], w_ref[...])
        wait()

    row = lambda width: pl.BlockSpec((tt, width), lambda i: (i, 0))
    h_ins = hosted.ins if hosted else []
    res = pl.pallas_call(
        body, name="outproj_fwd_hosting" if hosted else "outproj_fwd", grid=(nt,),
        in_specs=[row(D), row(POOL_WIDTH), row(SG_WIDTH), row(SB_WIDTH), pl.BlockSpec((D, D), lambda i: (0, 0))]
        + [HBM_SPEC] * len(h_ins),
        out_specs=[row(D), row(D)] + [HBM_SPEC] * len(h_ins),
        out_shape=[jax.ShapeDtypeStruct((T, D), F32), jax.ShapeDtypeStruct((T, D), BF16)]
        + (hosted.out_shapes if hosted else []),
        input_output_aliases=hosted.aliases(5, 2) if hosted else {},
        scratch_shapes=hosted.sems() if hosted else [],
        compiler_params=_params(has_side_effects=hosted is not None),
    )(x, ya, yb, yc, w, *h_ins)
    return res[:2], res[2:]


def _nt_matmul(a, w):
    T, N = a.shape
    K = w.shape[0]
    tt = _tile(T, 512)

    def body(a_ref, w_ref, o_ref):
        o_ref[...] = _nt(a_ref[...].astype(BF16), w_ref[...])

    return pl.pallas_call(
        body, name="nt_matmul", grid=(T // tt,),
        in_specs=[pl.BlockSpec((tt, N), lambda i: (i, 0)), pl.BlockSpec((K, N), lambda i: (0, 0))],
        out_specs=pl.BlockSpec((tt, K), lambda i: (i, 0)),
        out_shape=jax.ShapeDtypeStruct((T, K), F32),
        compiler_params=_params(),
    )(a, w)


def _tn_matmul(a, b, name, n_split=1, hosted=None):
    T, K = a.shape
    N = b.shape[1]
    tk = _tile(K, 1024)
    tn = _tile(N // n_split, 1024)
    tt = _tile(T, 2048)
    nper = N // n_split // tn
    nk, nn, nt = K // tk, N // tn, T // tt

    def body(*refs):
        k, n, t = pl.program_id(0), pl.program_id(1), pl.program_id(2)
        (a_ref, b_ref, o_ref), start, wait = _host(
            hosted, refs, 2, 1, jnp.logical_and(jnp.logical_and(k == 0, n == 0), t == 0),
            jnp.logical_and(jnp.logical_and(k == nk - 1, n == nn - 1), t == nt - 1))
        start()

        @pl.when(t == 0)
        def _():
            o_ref[...] = jnp.zeros_like(o_ref)

        o_ref[...] += _tn(a_ref[...], b_ref[...].astype(BF16))
        wait()

    h_ins = hosted.ins if hosted else []
    res = pl.pallas_call(
        body, name=name + "_hosting" if hosted else name, grid=(nk, nn, nt),
        in_specs=[pl.BlockSpec((tt, tk), lambda k, n, t: (t, k)), pl.BlockSpec((tt, tn), lambda k, n, t: (t, n))]
        + [HBM_SPEC] * len(h_ins),
        out_specs=[pl.BlockSpec((None, tk, tn), lambda k, n, t: (n // nper, k, n % nper))] + [HBM_SPEC] * len(h_ins),
        out_shape=[jax.ShapeDtypeStruct((n_split, K, N // n_split), F32)] + (hosted.out_shapes if hosted else []),
        scratch_shapes=hosted.sems() if hosted else [],
        compiler_params=_params(has_side_effects=hosted is not None),
    )(a, b, *h_ins)
    return (res[0], res[1:]) if hosted else res[0]


def _mlp_fwd(x, g, w_up, w_down, hosted=None):
    T, D = x.shape
    n_blk, _, width = w_up.shape
    F = n_blk * width
    tt = _tile(T, 1024)
    fc = _tile(width, MLP_CHUNK)
    per = width // fc
    nc = F // fc
    nt = T // tt

    def body(*refs):
        i, c = pl.program_id(0), pl.program_id(1)
        (x_ref, g_ref, wu_ref, wd_ref, y_ref, h_ref, u_ref, a_ref), start, wait = _host(
            hosted, refs, 4, 4, jnp.logical_and(i == 0, c == 0), jnp.logical_and(i == nt - 1, c == nc - 1))
        start()

        @pl.when(c == 0)
        def _():
            xv = x_ref[...]
            h, _, _ = _rms_fwd(xv, g_ref[...])
            h_ref[...] = h.astype(BF16)
            y_ref[...] = xv

        u = _nn(h_ref[...], wu_ref[...])
        u_ref[...] = u.astype(BF16)
        a = jnp.square(jnp.maximum(u, 0.0)).astype(BF16)
        a_ref[...] = a
        y_ref[...] += _nn(a, wd_ref[...])
        wait()

    h_ins = hosted.ins if hosted else []
    res = pl.pallas_call(
        body, name="mlp_fwd_hosting" if hosted else "mlp_fwd", grid=(nt, nc),
        in_specs=[pl.BlockSpec((tt, D), lambda i, c: (i, 0)), pl.BlockSpec((1, D), lambda i, c: (0, 0)),
                  pl.BlockSpec((None, D, fc), lambda i, c: (c // per, 0, c % per)),
                  pl.BlockSpec((fc, D), lambda i, c: (c, 0))]
        + [HBM_SPEC] * len(h_ins),
        out_specs=[pl.BlockSpec((tt, D), lambda i, c: (i, 0)), pl.BlockSpec((tt, D), lambda i, c: (i, 0)),
                   pl.BlockSpec((tt, fc), lambda i, c: (i, c)), pl.BlockSpec((tt, fc), lambda i, c: (i, c))]
        + [HBM_SPEC] * len(h_ins),
        out_shape=[jax.ShapeDtypeStruct((T, D), F32), jax.ShapeDtypeStruct((T, D), BF16),
                   jax.ShapeDtypeStruct((T, F), BF16), jax.ShapeDtypeStruct((T, F), BF16)]
        + (hosted.out_shapes if hosted else []),
        scratch_shapes=hosted.sems() if hosted else [],
        compiler_params=_params(has_side_effects=hosted is not None),
    )(x, g, w_up, w_down, *h_ins)
    return res[:4], res[4:]


def _mlp_bwd(dy, x, g, u, w_up, w_down, hosted=None):
    T, D = x.shape
    n_blk, _, width = w_up.shape
    F = n_blk * width
    tt = _tile(T, 1024)
    fc = _tile(width, MLP_CHUNK)
    per = width // fc
    nc = F // fc
    nt = T // tt

    def body(*refs):
        i, c = pl.program_id(0), pl.program_id(1)
        (dy_ref, x_ref, g_ref, u_ref, wu_ref, wd_ref, dx_ref, du_ref, dg_ref, dyb_ref, dh_ref), start, wait = _host(
            hosted, refs, 6, 3, jnp.logical_and(i == 0, c == 0), jnp.logical_and(i == nt - 1, c == nc - 1))
        start()

        @pl.when(c == 0)
        def _():
            dyb_ref[...] = dy_ref[...].astype(BF16)
            dh_ref[...] = jnp.zeros_like(dh_ref)

        @pl.when(jnp.logical_and(i == 0, c == 0))
        def _():
            dg_ref[...] = jnp.zeros_like(dg_ref)

        da = _nt(dyb_ref[...], wd_ref[...])
        du = (da * (2.0 * jnp.maximum(u_ref[...].astype(F32), 0.0))).astype(BF16)
        du_ref[...] = du
        dh_ref[...] += _nt(du, wu_ref[...])

        @pl.when(c == nc - 1)
        def _():
            gv = g_ref[...]
            _, xhat, r = _rms_fwd(x_ref[...], gv)
            dx, dgrow = _rms_bwd(dh_ref[...], xhat, r, gv)
            dx_ref[...] = dy_ref[...] + dx
            dg_ref[...] += jnp.sum(dgrow, axis=0, keepdims=True)

        wait()

    h_ins = hosted.ins if hosted else []
    res = pl.pallas_call(
        body, name="mlp_bwd_hosting" if hosted else "mlp_bwd", grid=(nt, nc),
        in_specs=[pl.BlockSpec((tt, D), lambda i, c: (i, 0)), pl.BlockSpec((tt, D), lambda i, c: (i, 0)),
                  pl.BlockSpec((1, D), lambda i, c: (0, 0)), pl.BlockSpec((tt, fc), lambda i, c: (i, c)),
                  pl.BlockSpec((None, D, fc), lambda i, c: (c // per, 0, c % per)),
                  pl.BlockSpec((fc, D), lambda i, c: (c, 0))]
        + [HBM_SPEC] * len(h_ins),
        out_specs=[pl.BlockSpec((tt, D), lambda i, c: (i, 0)), pl.BlockSpec((tt, fc), lambda i, c: (i, c)),
                   pl.BlockSpec((1, D), lambda i, c: (0, 0))] + [HBM_SPEC] * len(h_ins),
        out_shape=[jax.ShapeDtypeStruct((T, D), F32), jax.ShapeDtypeStruct((T, F), BF16),
                   jax.ShapeDtypeStruct((1, D), F32)] + (hosted.out_shapes if hosted else []),
        scratch_shapes=[pltpu.VMEM((tt, D), BF16), pltpu.VMEM((tt, D), F32)] + (hosted.sems() if hosted else []),
        compiler_params=_params(has_side_effects=hosted is not None),
    )(dy, x, g, u, w_up, w_down, *h_ins)
    return res[:3], res[3:]


def _loss_head(x, g, target):
    T, D = x.shape
    tt = _tile(T, 512)

    def body(x_ref, g_ref, t_ref, loss_ref, dx_ref, dg_ref):
        gv = g_ref[...]
        y, xhat, r = _rms_fwd(x_ref[...], gv)
        err = y - t_ref[...]
        dx, dgrow = _rms_bwd(err * (1.0 / D), xhat, r, gv)
        dx_ref[...] = dx

        @pl.when(pl.program_id(0) == 0)
        def _():
            loss_ref[...] = jnp.zeros_like(loss_ref)
            dg_ref[...] = jnp.zeros_like(dg_ref)

        loss_ref[...] += 0.5 * jnp.sum(jnp.mean(err * err, axis=-1, keepdims=True), axis=0, keepdims=True)
        dg_ref[...] += jnp.sum(dgrow, axis=0, keepdims=True)

    return pl.pallas_call(
        body, name="loss_head", grid=(T // tt,),
        in_specs=[pl.BlockSpec((tt, D), lambda i: (i, 0)), pl.BlockSpec((1, D), lambda i: (0, 0)),
                  pl.BlockSpec((tt, D), lambda i: (i, 0))],
        out_specs=[pl.BlockSpec((1, LANES), lambda i: (0, 0)), pl.BlockSpec((tt, D), lambda i: (i, 0)),
                   pl.BlockSpec((1, D), lambda i: (0, 0))],
        out_shape=[jax.ShapeDtypeStruct((1, LANES), F32), jax.ShapeDtypeStruct((T, D), F32),
                   jax.ShapeDtypeStruct((1, D), F32)],
        compiler_params=_params(),
    )(x, g, target)


def _rows(shape, pref=512):
    last = shape[-1]
    rows = 1
    for s in shape[:-1]:
        rows *= s
    tr = rows
    if rows * last > 256 * 1024:
        for cand in (pref, 256, 128, 64, 32, 16, 8):
            if rows % cand == 0:
                tr = cand
                break
    return rows, last, tr


def _elementwise(fn, name, ins, n_out, out_dtype=F32):
    shape = ins[0].shape
    rows, last, tr = _rows(shape)
    flat = [a.reshape(rows, last) for a in ins]
    n_in = len(ins)

    def body(*refs):
        res = fn(*[r[...] for r in refs[:n_in]])
        if n_out == 1:
            res = (res,)
        for r, v in zip(refs[n_in:], res):
            r[...] = v.astype(r.dtype)

    spec = pl.BlockSpec((tr, last), lambda i: (i, 0))
    outs = pl.pallas_call(
        body, name=name, grid=(rows // tr,),
        in_specs=[spec] * n_in, out_specs=[spec] * n_out,
        out_shape=[jax.ShapeDtypeStruct((rows, last), out_dtype)] * n_out,
        compiler_params=_params(),
    )(*flat)
    return [o.reshape(shape) for o in outs]


def _add_pairs(gs, os, c_idx):
    n = len(gs)
    halves = [(g.shape[1] // 2, g.shape[2]) for g in gs]

    def body(c_ref, *refs):
        for a in range(n):
            refs[2 * n + a][...] = refs[2 * a][...] + refs[2 * a + 1][...]

    in_specs = []
    for h, C in halves:
        in_specs += [pl.BlockSpec((None, h, C), lambda q, c: (q, c[0], 0)), pl.BlockSpec((None, h, C), lambda q, c: (q, 0, 0))]
    return pl.pallas_call(
        body, name="add_pairs",
        grid_spec=pltpu.PrefetchScalarGridSpec(
            num_scalar_prefetch=1, grid=(N_CHIPS,), in_specs=in_specs,
            out_specs=[pl.BlockSpec((None, h, C), lambda q, c: (q, 0, 0)) for h, C in halves]),
        out_shape=[jax.ShapeDtypeStruct((N_CHIPS, h, C), F32) for h, C in halves],
        compiler_params=_params(),
    )(c_idx.astype(jnp.int32).reshape(1), *[x for pair in zip(gs, os) for x in pair])


def _add_chips(ps, rs, q_idx):
    n = len(ps)
    steps = 2
    blocks = [(p.shape[1] // steps, p.shape[2]) for p in ps]

    def body(q_ref, *refs):
        for a in range(n):
            p_ref, r0_ref, r1_ref, r2_ref = refs[4 * a:4 * a + 4]
            refs[4 * n + a][...] = (p_ref[...] + r0_ref[...]) + (r1_ref[...] + r2_ref[...])

    def arrived(tr, C, k):
        return pl.BlockSpec((None, tr, C), lambda i, q: (k, i, 0))

    in_specs, operands = [], []
    for (tr, C), p, r in zip(blocks, ps, rs):
        in_specs += [pl.BlockSpec((None, tr, C), lambda i, q: (q[0], i, 0)), arrived(tr, C, 0), arrived(tr, C, 1),
                     arrived(tr, C, 2)]
        operands += [p, r, r, r]
    return pl.pallas_call(
        body, name="add_chips",
        grid_spec=pltpu.PrefetchScalarGridSpec(
            num_scalar_prefetch=1, grid=(steps,), in_specs=in_specs,
            out_specs=[pl.BlockSpec((tr, C), lambda i, q: (i, 0)) for tr, C in blocks]),
        out_shape=[jax.ShapeDtypeStruct((p.shape[1], p.shape[2]), F32) for p in ps],
        compiler_params=_params(),
    )(q_idx.astype(jnp.int32).reshape(1), *operands)


def _adamw(w, g, m, v):
    m = ADAM_B1 * m + (1.0 - ADAM_B1) * g
    v = ADAM_B2 * v + (1.0 - ADAM_B2) * jnp.square(g)
    m_hat = m / (1.0 - ADAM_B1 ** ADAM_STEP)
    v_hat = v / (1.0 - ADAM_B2 ** ADAM_STEP)
    delta = -ADAM_LR * (m_hat / (jnp.sqrt(v_hat) + ADAM_EPS) + ADAM_WD * w)
    return delta, m, v


def _place():
    x, y, c = lax.axis_index("x"), lax.axis_index("y"), lax.axis_index("c")
    chips = [(1 - x, y), (x, 1 - y), (1 - x, 1 - y)]
    return x, y, c, chips


def _remote(src, dst, ssem, rsem, k, dev):
    return pltpu.make_async_remote_copy(src_ref=src, dst_ref=dst, send_sem=ssem.at[k], recv_sem=rsem.at[k],
                                        device_id=dev, device_id_type=MESH)


def _gather_weights(shards):
    n = len(shards)
    halves = [s.shape[1] // 2 for s in shards]

    def body(*refs):
        src, out = refs[:n], refs[n:2 * n]
        ssem, rsem = refs[2 * n:]
        x, y, c, chips = _place()
        me_q = 2 * x + y
        sib = (x, y, 1 - c)

        def half(a, q, cc):
            return out[a].at[q, :, pl.ds(cc * halves[a], halves[a]), :]

        first = []
        for a in range(n):
            mine = src[a].at[:, pl.ds(c * halves[a], halves[a]), :]
            for r, chip in enumerate(chips):
                first.append(_remote(mine, half(a, me_q, c), ssem, rsem, a * 3 + r, (*chip, c)))
        for cp in first:
            cp.start()
        passed = []
        for a in range(n):
            for r, chip in enumerate(chips):
                q = 2 * chip[0] + chip[1]
                k = a * 3 + r
                _remote(half(a, q, c), half(a, q, c), ssem, rsem, k, (*chip, c)).wait_recv()
                cp = _remote(half(a, q, c), half(a, q, c), ssem, rsem, 3 * n + k, sib)
                cp.start()
                passed.append(cp)
        for a in range(n):
            for r, chip in enumerate(chips):
                q = 2 * chip[0] + chip[1]
                _remote(half(a, q, 1 - c), half(a, q, 1 - c), ssem, rsem, 3 * n + a * 3 + r, sib).wait_recv()
        for cp in first + passed:
            cp.wait_send()

    return pl.pallas_call(
        body, name="gather_weights",
        in_specs=[HBM_SPEC] * n, out_specs=[HBM_SPEC] * n,
        out_shape=[jax.ShapeDtypeStruct((N_CHIPS,) + s.shape, s.dtype) for s in shards],
        scratch_shapes=[pltpu.SemaphoreType.DMA((6 * n,)), pltpu.SemaphoreType.DMA((6 * n,))],
        compiler_params=_params(has_side_effects=True),
    )(*shards)


def _gather_over_ici(shards):
    n = len(shards)
    halves = [s.shape[1] // 2 for s in shards]

    def copies(src, out, ssem, rsem):
        x, y, c, chips = _place()
        me_q = 2 * x + y
        res = []
        for a in range(n):
            rows = pl.ds(c * halves[a], halves[a])
            mine = src[a].at[:, rows, :]
            for r, chip in enumerate(chips):
                dev = (*chip, c)
                res.append((_remote(mine, out[a].at[me_q, :, rows, :], ssem, rsem, a * 3 + r, dev),
                            _remote(mine, out[a].at[2 * chip[0] + chip[1], :, rows, :], ssem, rsem, a * 3 + r, dev)))
        return res

    return _Hosted(list(shards), [jax.ShapeDtypeStruct((N_CHIPS,) + s.shape, s.dtype) for s in shards], 3 * n, copies)


def _pass_over_d2d(gathered):
    n = len(gathered)
    halves = [g.shape[2] // 2 for g in gathered]

    def copies(_, out, ssem, rsem):
        x, y, c, chips = _place()
        sib = (x, y, 1 - c)
        res = []
        for a in range(n):
            for r, chip in enumerate(chips):
                q = 2 * chip[0] + chip[1]
                mine = out[a].at[q, :, pl.ds(c * halves[a], halves[a]), :]
                theirs = out[a].at[q, :, pl.ds((1 - c) * halves[a], halves[a]), :]
                res.append((_remote(mine, mine, ssem, rsem, a * 3 + r, sib),
                            _remote(theirs, theirs, ssem, rsem, a * 3 + r, sib)))
        return res

    return _Hosted(list(gathered), [jax.ShapeDtypeStruct(g.shape, g.dtype) for g in gathered], 3 * n, copies,
                   in_place=True)


def _pass_to_sibling(gathered):
    n = len(gathered)
    halves = [g.shape[2] // 2 for g in gathered]

    def body(*refs):
        out = refs[n:2 * n]
        ssem, rsem = refs[2 * n:]
        x, y, c, chips = _place()
        sib = (x, y, 1 - c)

        def half(a, q, cc):
            return out[a].at[q, :, pl.ds(cc * halves[a], halves[a]), :]

        cps = []
        for a in range(n):
            for r, chip in enumerate(chips):
                q = 2 * chip[0] + chip[1]
                cps.append(_remote(half(a, q, c), half(a, q, c), ssem, rsem, a * 3 + r, sib))
        for cp in cps:
            cp.start()
        for a in range(n):
            for r, chip in enumerate(chips):
                q = 2 * chip[0] + chip[1]
                _remote(half(a, q, 1 - c), half(a, q, 1 - c), ssem, rsem, a * 3 + r, sib).wait_recv()
        for cp in cps:
            cp.wait_send()

    return pl.pallas_call(
        body, name="pass_to_sibling",
        in_specs=[HBM_SPEC] * n, out_specs=[HBM_SPEC] * n,
        out_shape=[jax.ShapeDtypeStruct(g.shape, g.dtype) for g in gathered],
        input_output_aliases={a: a for a in range(n)},
        scratch_shapes=[pltpu.SemaphoreType.DMA((3 * n,)), pltpu.SemaphoreType.DMA((3 * n,))],
        compiler_params=_params(has_side_effects=True),
    )(*gathered)


def _scatter_over_ici(parts):
    n = len(parts)

    def copies(src, out, ssem, rsem):
        x, y, c, chips = _place()
        res = []
        for a in range(n):
            for r, chip in enumerate(chips):
                cp = _remote(src[a].at[2 * chip[0] + chip[1]], out[a].at[r], ssem, rsem, a * 3 + r, (*chip, c))
                res.append((cp, cp))
        return res

    return _Hosted(list(parts), [jax.ShapeDtypeStruct((3,) + p.shape[1:], F32) for p in parts], 3 * n, copies)


def _swap_over_d2d(grads):
    n = len(grads)
    halves = [g.shape[1] // 2 for g in grads]

    def copies(src, out, ssem, rsem):
        x, y, c, _ = _place()
        res = []
        for a in range(n):
            cp = _remote(src[a].at[:, pl.ds((1 - c) * halves[a], halves[a]), :], out[a], ssem, rsem, a, (x, y, 1 - c))
            res.append((cp, cp))
        return res

    return _Hosted(list(grads), [jax.ShapeDtypeStruct((N_CHIPS, h, g.shape[2]), F32) for g, h in zip(grads, halves)],
                   n, copies)


def _swap_halves(grads):
    n = len(grads)
    halves = [g.shape[1] // 2 for g in grads]

    def body(*refs):
        src, out = refs[:n], refs[n:2 * n]
        ssem, rsem = refs[2 * n:]
        x, y, c, _ = _place()
        cps = [_remote(src[a].at[:, pl.ds((1 - c) * halves[a], halves[a]), :], out[a], ssem, rsem, a, (x, y, 1 - c))
               for a in range(n)]
        for cp in cps:
            cp.start()
        for cp in cps:
            cp.wait()

    return pl.pallas_call(
        body, name="swap_halves",
        in_specs=[HBM_SPEC] * n, out_specs=[HBM_SPEC] * n,
        out_shape=[jax.ShapeDtypeStruct((N_CHIPS, h, g.shape[2]), F32) for g, h in zip(grads, halves)],
        scratch_shapes=[pltpu.SemaphoreType.DMA((n,)), pltpu.SemaphoreType.DMA((n,))],
        compiler_params=_params(has_side_effects=True),
    )(*grads)


def _scatter_chips(parts):
    n = len(parts)

    def body(*refs):
        src, out = refs[:n], refs[n:2 * n]
        ssem, rsem = refs[2 * n:]
        x, y, c, chips = _place()
        cps = []
        for a in range(n):
            for r, chip in enumerate(chips):
                cps.append(_remote(src[a].at[2 * chip[0] + chip[1]], out[a].at[r], ssem, rsem, a * 3 + r, (*chip, c)))
        for cp in cps:
            cp.start()
        for cp in cps:
            cp.wait()

    return pl.pallas_call(
        body, name="scatter_chips",
        in_specs=[HBM_SPEC] * n, out_specs=[HBM_SPEC] * n,
        out_shape=[jax.ShapeDtypeStruct((3,) + p.shape[1:], F32) for p in parts],
        scratch_shapes=[pltpu.SemaphoreType.DMA((3 * n,)), pltpu.SemaphoreType.DMA((3 * n,))],
        compiler_params=_params(has_side_effects=True),
    )(*parts)


def _swap_reduced_over_d2d(reduced):
    n = len(reduced)

    def copies(src, out, ssem, rsem):
        x, y, c, _ = _place()
        res = []
        for a in range(n):
            cp = _remote(src[a], out[a], ssem, rsem, a, (x, y, 1 - c))
            res.append((cp, cp))
        return res

    return _Hosted(list(reduced), [jax.ShapeDtypeStruct(r.shape, F32) for r in reduced], n, copies)


def _swap_reduced(reduced):
    n = len(reduced)

    def body(*refs):
        src, out = refs[:n], refs[n:2 * n]
        ssem, rsem = refs[2 * n:]
        x, y, c, _ = _place()
        cps = [_remote(src[a], out[a], ssem, rsem, a, (x, y, 1 - c)) for a in range(n)]
        for cp in cps:
            cp.start()
        for cp in cps:
            cp.wait()

    return pl.pallas_call(
        body, name="swap_reduced",
        in_specs=[HBM_SPEC] * n, out_specs=[HBM_SPEC] * n,
        out_shape=[jax.ShapeDtypeStruct(r.shape, F32) for r in reduced],
        scratch_shapes=[pltpu.SemaphoreType.DMA((n,)), pltpu.SemaphoreType.DMA((n,))],
        compiler_params=_params(has_side_effects=True),
    )(*reduced)


def _allreduce_small(buf, hosted=None):
    R, L = buf.shape

    def body(*refs):
        (buf_ref, out_ref, pair_ref, chip_ref, ssem, rsem), start, wait = _host(hosted, refs, 1, 1, True, True)
        start()
        x, y, c, chips = _place()
        me_q = 2 * x + y
        pair_ref[c] = buf_ref[...]
        to_sib = _remote(buf_ref, pair_ref.at[c], ssem, rsem, 0, (x, y, 1 - c))
        to_sib.start()
        _remote(buf_ref, pair_ref.at[1 - c], ssem, rsem, 0, (x, y, 1 - c)).wait_recv()
        chip_ref[me_q] = pair_ref[0] + pair_ref[1]
        cps = [_remote(chip_ref.at[me_q], chip_ref.at[me_q], ssem, rsem, 1 + r, (*chip, c))
               for r, chip in enumerate(chips)]
        for cp in cps:
            cp.start()
        for r, chip in enumerate(chips):
            q = 2 * chip[0] + chip[1]
            _remote(chip_ref.at[q], chip_ref.at[q], ssem, rsem, 1 + r, (*chip, c)).wait_recv()
        out_ref[...] = (chip_ref[0] + chip_ref[1]) + (chip_ref[2] + chip_ref[3])
        to_sib.wait_send()
        for cp in cps:
            cp.wait_send()
        wait()

    h_ins = hosted.ins if hosted else []
    res = pl.pallas_call(
        body, name="allreduce_small",
        in_specs=[VMEM_SPEC] + [HBM_SPEC] * len(h_ins), out_specs=[VMEM_SPEC] + [HBM_SPEC] * len(h_ins),
        out_shape=[jax.ShapeDtypeStruct((R, L), F32)] + (hosted.out_shapes if hosted else []),
        scratch_shapes=[pltpu.VMEM((2, R, L), F32), pltpu.VMEM((N_CHIPS, R, L), F32),
                        pltpu.SemaphoreType.DMA((4,)), pltpu.SemaphoreType.DMA((4,))]
        + (hosted.sems() if hosted else []),
        compiler_params=_params(has_side_effects=True),
    )(buf, *h_ins)
    return res[0], res[1:]


def _pack(arrays):
    flat = jnp.concatenate([a.reshape(-1) for a in arrays])
    pad = (-flat.shape[0]) % (8 * LANES)
    return jnp.pad(flat, (0, pad)).reshape(-1, LANES)


def _unpack(buf, like):
    flat = buf.reshape(-1)
    out, off = [], 0
    for a in like:
        out.append(flat[off:off + a.size].reshape(a.shape))
        off += a.size
    return out


def _block_diag(pw):
    rows = []
    for gi in range(len(POOL_WINDOWS)):
        blocks = [pw[gi] if gj == gi else jnp.zeros_like(pw[gi]) for gj in range(len(POOL_WINDOWS))]
        rows.append(jnp.concatenate(blocks, axis=1))
    return jnp.concatenate(rows, axis=0)


def kernel(x, norm1, w_in, pool_w, pool_scale, sg_norm, sg_w, sg_b, w_out, norm2, w_up, w_down, final_norm, loss_target, m_norm1, m_w_in, m_pool_w, m_pool_scale, m_sg_norm, m_sg_w, m_sg_b, m_w_out, m_norm2, m_w_up, m_w_down, m_final_norm, v_norm1, v_w_in, v_pool_w, v_pool_scale, v_sg_norm, v_sg_w, v_sg_b, v_w_out, v_norm2, v_w_up, v_w_down, v_final_norm):
    depth = norm1.shape[0]
    T = x.shape[1]
    xs = x.reshape(T, D_MODEL)
    target = loss_target.reshape(T, D_MODEL)

    assert depth == 2
    c_idx = lax.axis_index("c")
    q_idx = 2 * lax.axis_index("x") + lax.axis_index("y")
    own = [w.astype(BF16) for w in (w_in, w_out, w_up, w_down)]
    gathered = {(0, 0): _gather_weights([own[0][:1]])[0]}

    def full(a, l, axis):
        blocks = lax.dynamic_update_slice(gathered[(a, l)], own[a][l][None, None], (q_idx, 0, 0, 0))[:, 0]
        if axis is None:
            return blocks
        if axis == 0:
            return blocks.reshape(-1, blocks.shape[-1])
        return jnp.concatenate([blocks[q] for q in range(N_CHIPS)], axis=axis)

    half_way = {}

    def gather_behind(call, keys, at_once):
        res, over_ici = call(_gather_over_ici([own[a][l:l + 1] for a, l in keys]))
        gathered.update(zip(keys[:at_once], _pass_to_sibling(over_ici[:at_once])))
        half_way.update(zip(keys[at_once:], over_ici[at_once:]))
        return res

    def pass_behind(call, keys):
        res, done = call(_pass_over_d2d([half_way.pop(k) for k in keys]))
        gathered.update(zip(keys, done))
        return res

    tril = jnp.tril(jnp.ones((CHUNK, CHUNK), F32))
    saved = []
    cur = xs
    wi, wo, wu, wd = {}, {}, {}, {}
    for l in range(depth):
        wbd = _block_diag(pool_w[l]).astype(BF16)
        wm = sg_w[l] * tril
        wm_s = wm.reshape(SG_HEADS * CHUNK, CHUNK).astype(BF16)
        wmt_s = jnp.swapaxes(wm, 1, 2).reshape(SG_HEADS * CHUNK, CHUNK).astype(BF16)
        bias = jnp.repeat(sg_b[l].T, SB_HD, axis=1)
        n1, n2 = norm1[l][None], norm2[l][None]
        psc, sgn = pool_scale[l][None], sg_norm[l][None]
        wi[l] = full(0, l, 1)
        proj, h, qkv = _inproj_fwd(cur, n1, wi[l])
        ya = _pool_fwd(proj, wbd, psc)
        yb = _sg_fwd(proj, wm_s, bias, sgn)
        if l == 0:
            yc = gather_behind(lambda hosted: _attn_fwd(qkv, hosted), [(1, 0), (2, 0), (3, 0)], 1)
            wo[l] = full(1, l, 0)
            x1, ymix = pass_behind(lambda hosted: _outproj_fwd(cur, ya, yb, yc, wo[l], hosted), [(2, 0), (3, 0)])
        else:
            yc = pass_behind(lambda hosted: _attn_fwd(qkv, hosted), [(1, l), (2, l), (3, l)])
            wo[l] = full(1, l, 0)
            (x1, ymix), _ = _outproj_fwd(cur, ya, yb, yc, wo[l])
        wu[l], wd[l] = full(2, l, None), full(3, l, 0)
        if l == 0:
            x2, h2, u, act = gather_behind(lambda hosted: _mlp_fwd(x1, n2, wu[l], wd[l], hosted),
                                           [(0, 1), (1, 1), (2, 1), (3, 1)], 1)
        else:
            (x2, h2, u, act), _ = _mlp_fwd(x1, n2, wu[l], wd[l])
        saved.append(dict(x0=cur, x1=x1, proj=proj, h=h, qkv=qkv, yc=yc, ymix=ymix, h2=h2, u=u, act=act,
                          wbd=wbd, wm_s=wm_s, wmt_s=wmt_s, bias=bias, n1=n1, n2=n2, psc=psc, sgn=sgn))
        cur = x2

    loss_row, dcur, d_final = _loss_head(cur, final_norm[None], target)

    small = [None] * depth
    grads, parts, reduced = {}, {}, {}

    def pair_up(keys, swapped):
        parts.update(zip(keys, _add_pairs([grads[k] for k in keys], swapped, c_idx)))

    def chip_up(keys, arrived):
        reduced.update(zip(keys, _add_chips([parts[k] for k in keys], arrived, q_idx)))

    for l in reversed(range(depth)):
        s = saved[l]
        if l == 0:
            keys = [(2, 1), (3, 1)]
            (dx1, du, d_n2), arrived = _mlp_bwd(dcur, s["x1"], s["n2"], s["u"], wu[l], wd[l],
                                                _scatter_over_ici([parts[k] for k in keys]))
            chip_up(keys, arrived)
        else:
            (dx1, du, d_n2), _ = _mlp_bwd(dcur, s["x1"], s["n2"], s["u"], wu[l], wd[l])
        grads[(2, l)] = _tn_matmul(s["h2"], du, "grad_w_up", n_split=N_CHIPS)
        grads[(3, l)] = _tn_matmul(s["act"], dcur, "grad_w_down")[0].reshape(N_CHIPS, D_FF // N_CHIPS, D_MODEL)
        dymix = _nt_matmul(dx1, wo[l])
        grads[(1, l)] = _tn_matmul(s["ymix"], dx1, "grad_w_out")[0].reshape(N_CHIPS, D_MODEL // N_CHIPS, D_MODEL)
        da_in, d_wbd, d_psc = _pool_bwd(s["proj"], dymix, s["wbd"], s["psc"])
        if l == 0:
            keys = [(0, 1), (1, 0), (2, 0), (3, 0)]
            (du_pre, dv_pre, d_wm, d_bias, d_sgn), swapped = _sg_bwd(
                s["proj"], dymix, s["wm_s"], s["wmt_s"], s["bias"], s["sgn"], _swap_over_d2d([grads[k] for k in keys]))
            pair_up(keys, swapped)
            keys = [(1, 1)] + keys
            (dq, dk, dv), arrived = _attn_bwd(s["qkv"], s["yc"], dymix, _scatter_over_ici([parts[k] for k in keys]))
            chip_up(keys, arrived)
        else:
            (du_pre, dv_pre, d_wm, d_bias, d_sgn), _ = _sg_bwd(s["proj"], dymix, s["wm_s"], s["wmt_s"], s["bias"], s["sgn"])
            keys = [(1, l), (2, l), (3, l)]
            (dq, dk, dv), swapped = _attn_bwd(s["qkv"], s["yc"], dymix, _swap_over_d2d([grads[k] for k in keys]))
            pair_up(keys, swapped)
        pieces = [da_in, du_pre, dv_pre, dq, dk, dv]
        if l == 0:
            keys = sorted(reduced)
            g_in_l, swapped = _inproj_grad(s["h"], pieces, _swap_reduced_over_d2d([reduced[k] for k in keys]))
            theirs = dict(zip(keys, swapped))
        else:
            g_in_l, _ = _inproj_grad(s["h"], pieces)
        grads[(0, l)] = g_in_l[0].reshape(D_MODEL, N_CHIPS, IN_COLS // N_CHIPS).transpose(1, 0, 2)
        if l == 0:
            keys = [(0, 0)]
            pair_up(keys, _swap_halves([grads[k] for k in keys]))
            (dx0, d_n1), arrived = _inproj_bwd(pieces, wi[l], s["x0"], s["n1"], dx1,
                                               _scatter_over_ici([parts[k] for k in keys]))
            chip_up(keys, arrived)
        else:
            (dx0, d_n1), _ = _inproj_bwd(pieces, wi[l], s["x0"], s["n1"], dx1)
        d_pw = jnp.stack([d_wbd[gi * POOL_GW:(gi + 1) * POOL_GW, gi * POOL_GW:(gi + 1) * POOL_GW]
                          for gi in range(len(POOL_WINDOWS))])
        small[l] = dict(norm1=d_n1[0], pool_w=d_pw, pool_scale=d_psc[0], sg_norm=d_sgn[0],
                        sg_w=d_wm.reshape(SG_HEADS, CHUNK, CHUNK), sg_b=d_bias[:, :SG_HEADS].T, norm2=d_n2[0])
        dcur = dx0
    grad_x = dcur.reshape(x.shape)

    names =["norm1", "pool_w", "pool_scale", "sg_norm", "sg_w", "sg_b", "norm2"]
    slot = jnp.zeros((1,), F32)
    small_w = [norm1, pool_w, pool_scale, sg_norm, sg_w, sg_b, norm2, final_norm, slot]
    small_m = [m_norm1, m_pool_w, m_pool_scale, m_sg_norm, m_sg_w, m_sg_b, m_norm2, m_final_norm, slot]
    small_v = [v_norm1, v_pool_w, v_pool_scale, v_sg_norm, v_sg_w, v_sg_b, v_norm2, v_final_norm, slot]
    small_g = [jnp.stack([small[l][k] for l in range(depth)]) for k in names] + [d_final[0], loss_row[0, :1]]
    keys = [(0, 0)]
    g_packed, _ = _allreduce_small(_pack(small_g))
    theirs.update(zip(keys, _swap_reduced([reduced[k] for k in keys])))

    def joined(a):
        layers = []
        for l in range(depth):
            mine, other = reduced[(a, l)], theirs[(a, l)]
            layers.append(jnp.where(c_idx == 0, jnp.concatenate([mine, other]), jnp.concatenate([other, mine])))
        return jnp.stack(layers)

    gw_in, gw_out, gw_up, gw_down = [joined(a) for a in range(4)]

    loss = _unpack(g_packed, small_w)[-1][0]
    s_delta, s_m, s_v = _elementwise(_adamw, "adamw_small", [_pack(small_w), g_packed, _pack(small_m), _pack(small_v)], 3)
    gs = dict(zip(names + ["final_norm"], _unpack(g_packed, small_w)))
    ds = dict(zip(names + ["final_norm"], _unpack(s_delta, small_w)))
    ms = dict(zip(names + ["final_norm"], _unpack(s_m, small_w)))
    vs = dict(zip(names + ["final_norm"], _unpack(s_v, small_w)))

    big_g = dict(w_in=gw_in, w_out=gw_out, w_up=gw_up, w_down=gw_down)
    big_w = dict(w_in=(w_in, m_w_in, v_w_in), w_out=(w_out, m_w_out, v_w_out),
                 w_up=(w_up, m_w_up, v_w_up), w_down=(w_down, m_w_down, v_w_down))
    for k, (w, m, v) in big_w.items():
        operands = [w, big_g[k], m, v]
        if k == "w_in":
            operands = [jnp.swapaxes(o, 1, 2) for o in operands]
        ds[k], ms[k], vs[k] = _elementwise(_adamw, "adamw_" + k, operands, 3)
        if k == "w_in":
            ds[k], ms[k], vs[k] = [jnp.swapaxes(o, 1, 2) for o in (ds[k], ms[k], vs[k])]
        gs[k] = big_g[k]

    order = ["norm1", "w_in", "pool_w", "pool_scale", "sg_norm", "sg_w", "sg_b", "w_out", "norm2", "w_up", "w_down",
             "final_norm"]
    return (loss, grad_x, *[gs[k] for k in order], *[ds[k] for k in order], *[ms[k] for k in order],
            *[vs[k] for k in order])
```

```python
import functools

import jax
import jax.numpy as jnp
from jax import lax
from jax.experimental import pallas as pl
from jax.experimental.pallas import tpu as pltpu

F32 = jnp.float32
BF16 = jnp.bfloat16
MESH = pl.DeviceIdType.MESH
AXES = ("x", "y", "c")

EPS = 1e-6
D_MODEL = 1024
POOL_WIDTH = 256
SG_WIDTH = 256
SB_WIDTH = 512
POOL_WINDOWS = (2, 4, 8, 16)
POOL_GW = 64
POOL_HALO = 16
CHUNK = 128
SG_HEADS = 4
SB_HD = 64
SB_SCALE = 0.125
IN_COLS = 2304
QKV_OFF = 768
D_FF = 4096
N_CHIPS = 4
LANES = 128
VMEM_LIMIT = 56 * 1024 * 1024
MLP_CHUNK = 512
ATTN_TILE = 256
UNDERFLOW = -104.0

ADAM_LR = 0.001
ADAM_B1 = 0.9
ADAM_B2 = 0.999
ADAM_EPS = 1e-08
ADAM_WD = 0.01
ADAM_STEP = 10

HBM_SPEC = pl.BlockSpec(memory_space=pl.ANY)
VMEM_SPEC = pl.BlockSpec(memory_space=pltpu.VMEM)


def _params(**kw):
    return pltpu.CompilerParams(vmem_limit_bytes=VMEM_LIMIT, **kw)


def _tile(n, pref):
    if n <= pref:
        return n
    for t in range(pref - pref % LANES, 0, -LANES):
        if n % t == 0:
            return t
    raise ValueError((n, pref))


def _nn(a, b):
    return jnp.dot(a, b, preferred_element_type=F32)


def _nt(a, b):
    return lax.dot_general(a, b, (((1,), (1,)), ((), ())), preferred_element_type=F32)


def _tn(a, b):
    return lax.dot_general(a, b, (((0,), (0,)), ((), ())), preferred_element_type=F32)


def _rms_fwd(x, g):
    r = lax.rsqrt(jnp.mean(x * x, axis=-1, keepdims=True) + EPS)
    xhat = x * r
    return xhat * g, xhat, r


def _rms_bwd(dy, xhat, r, g):
    dxhat = dy * g
    dx = r * (dxhat - xhat * jnp.mean(dxhat * xhat, axis=-1, keepdims=True))
    return dx, dy * xhat


_GELU_K = 0.7978845608028654
_GELU_C = 0.044715


def _gelu(x):
    return 0.5 * x * (1.0 + jnp.tanh(_GELU_K * (x + _GELU_C * x * x * x)))


def _gelu_grad(x):
    t = jnp.tanh(_GELU_K * (x + _GELU_C * x * x * x))
    return 0.5 * (1.0 + t) + 0.5 * x * (1.0 - t * t) * _GELU_K * (1.0 + 3.0 * _GELU_C * x * x)


def _inproj_fwd(x, g, w):
    T, D = x.shape
    N = w.shape[1]
    tt = _tile(T, 512)

    def body(x_ref, g_ref, w_ref, proj_ref, h_ref, qkv_ref):
        h, _, _ = _rms_fwd(x_ref[...], g_ref[...])
        hb = h.astype(BF16)
        h_ref[...] = hb
        p = _nn(hb, w_ref[...])
        proj_ref[...] = p[:, :QKV_OFF]
        qkv_ref[...] = p[:, QKV_OFF:].astype(BF16)

    return pl.pallas_call(
        body, name="inproj_fwd", grid=(T // tt,),
        in_specs=[pl.BlockSpec((tt, D), lambda i: (i, 0)), pl.BlockSpec((1, D), lambda i: (0, 0)),
                  pl.BlockSpec((D, N), lambda i: (0, 0))],
        out_specs=[pl.BlockSpec((tt, QKV_OFF), lambda i: (i, 0)), pl.BlockSpec((tt, D), lambda i: (i, 0)),
                   pl.BlockSpec((tt, N - QKV_OFF), lambda i: (i, 0))],
        out_shape=[jax.ShapeDtypeStruct((T, QKV_OFF), F32), jax.ShapeDtypeStruct((T, D), BF16),
                   jax.ShapeDtypeStruct((T, N - QKV_OFF), BF16)],
        compiler_params=_params(),
    )(x, g, w)


def _inproj_bwd(pieces, w, x, g, dres, hosted=None):
    T, D = x.shape
    N = w.shape[1]
    tt = _tile(T, 512)
    nt = T // tt
    widths = [p.shape[1] for p in pieces]
    offs = [sum(widths[:k]) for k in range(len(widths))]
    assert sum(widths) == N
    n_p = len(pieces)

    def body(*refs):
        i = pl.program_id(0)
        own, start, wait = _host(hosted, refs, n_p + 4, 2, i == 0, i == nt - 1)
        start()
        p_refs = own[:n_p]
        w_ref, x_ref, g_ref, dres_ref, dx_ref, dg_ref, dproj_ref = own[n_p:]
        for p_ref, o, wd in zip(p_refs, offs, widths):
            dproj_ref[:, o:o + wd] = p_ref[...].astype(BF16)
        dh = _nt(dproj_ref[...], w_ref[...])
        gv = g_ref[...]
        _, xhat, r = _rms_fwd(x_ref[...], gv)
        dx, dgrow = _rms_bwd(dh, xhat, r, gv)
        dx_ref[...] = dres_ref[...] + dx

        @pl.when(i == 0)
        def _():
            dg_ref[...] = jnp.zeros_like(dg_ref)

        dg_ref[...] += jnp.sum(dgrow, axis=0, keepdims=True)
        wait()

    h_ins = hosted.ins if hosted else []
    res = pl.pallas_call(
        body, name="inproj_bwd_hosting" if hosted else "inproj_bwd", grid=(nt,),
        in_specs=[pl.BlockSpec((tt, wd), lambda i: (i, 0)) for wd in widths] + [
            pl.BlockSpec((D, N), lambda i: (0, 0)), pl.BlockSpec((tt, D), lambda i: (i, 0)),
            pl.BlockSpec((1, D), lambda i: (0, 0)), pl.BlockSpec((tt, D), lambda i: (i, 0))] + [HBM_SPEC] * len(h_ins),
        out_specs=[pl.BlockSpec((tt, D), lambda i: (i, 0)), pl.BlockSpec((1, D), lambda i: (0, 0))]
        + [HBM_SPEC] * len(h_ins),
        out_shape=[jax.ShapeDtypeStruct((T, D), F32), jax.ShapeDtypeStruct((1, D), F32)]
        + (hosted.out_shapes if hosted else []),
        scratch_shapes=[pltpu.VMEM((tt, N), BF16)] + (hosted.sems() if hosted else []),
        compiler_params=_params(has_side_effects=hosted is not None),
    )(*pieces, w, x, g, dres, *h_ins)
    return res[:2], res[2:]


def _inproj_grad(h, pieces, hosted=None):
    T, D = h.shape
    tt = _tile(T, 1024)
    nt = T // tt
    widths = [p.shape[1] for p in pieces]
    offs = [sum(widths[:k]) for k in range(len(widths))]
    N = sum(widths)
    n_p = len(pieces)

    def body(*refs):
        t = pl.program_id(0)
        own, start, wait = _host(hosted, refs, n_p + 1, 1, t == 0, t == nt - 1)
        start()
        h_ref, p_refs, o_ref = own[0], own[1:1 + n_p], own[1 + n_p]

        @pl.when(t == 0)
        def _():
            o_ref[...] = jnp.zeros_like(o_ref)

        hv = h_ref[...]
        for p_ref, o, wd in zip(p_refs, offs, widths):
            o_ref[:, o:o + wd] += _tn(hv, p_ref[...].astype(BF16))
        wait()

    h_ins = hosted.ins if hosted else []
    res = pl.pallas_call(
        body, name="grad_w_in_hosting" if hosted else "grad_w_in", grid=(nt,),
        in_specs=[pl.BlockSpec((tt, D), lambda t: (t, 0))] + [pl.BlockSpec((tt, wd), lambda t: (t, 0)) for wd in widths]
        + [HBM_SPEC] * len(h_ins),
        out_specs=[pl.BlockSpec((None, D, N), lambda t: (0, 0, 0))] + [HBM_SPEC] * len(h_ins),
        out_shape=[jax.ShapeDtypeStruct((1, D, N), F32)] + (hosted.out_shapes if hosted else []),
        scratch_shapes=hosted.sems() if hosted else [],
        compiler_params=_params(has_side_effects=hosted is not None),
    )(h, *pieces, *h_ins)
    return res[0], res[1:]


def _pool_select(s2, s4, s8, s16, grp):
    return jnp.where(grp == 0, s2, jnp.where(grp == 1, s4, jnp.where(grp == 2, s8, s16)))


def _pool_count(t_glob, grp):
    win = jnp.where(grp == 0, 2, jnp.where(grp == 1, 4, jnp.where(grp == 2, 8, 16)))
    return jnp.minimum(t_glob + 1, win).astype(F32)


def _pool_diff(a, halo, base, tt):
    n = tt + POOL_HALO
    ext = jnp.concatenate([halo, a], axis=0)
    s2 = ext + pltpu.roll(ext, 1, 0)
    s4 = s2 + pltpu.roll(s2, 2, 0)
    s8 = s4 + pltpu.roll(s4, 4, 0)
    s16 = s8 + pltpu.roll(s8, 8, 0)
    grp = lax.broadcasted_iota(jnp.int32, (n, POOL_WIDTH), 1) // POOL_GW
    t_glob = lax.broadcasted_iota(jnp.int32, (n, POOL_WIDTH), 0) + (base - POOL_HALO)
    pooled = _pool_select(s2, s4, s8, s16, grp) / _pool_count(t_glob, grp)
    return pooled[POOL_HALO:] - a


def _pool_specs(T, tt):
    hb = tt // POOL_HALO
    return [pl.BlockSpec((tt, POOL_WIDTH), lambda i: (i, 0)),
            pl.BlockSpec((POOL_HALO, POOL_WIDTH), lambda i: (jnp.maximum(i * hb - 1, 0), 0))]


def _pool_fwd(proj, wbd, scale):
    T = proj.shape[0]
    tt = _tile(T, 512)

    def body(a_ref, halo_ref, w_ref, sc_ref, y_ref):
        i = pl.program_id(0)
        halo = jnp.where(i > 0, halo_ref[...], 0.0)
        d = _pool_diff(a_ref[...], halo, i * tt, tt)
        y_ref[...] = _nn(d.astype(BF16), w_ref[...]) * sc_ref[...]

    return pl.pallas_call(
        body, name="pool_fwd", grid=(T // tt,),
        in_specs=_pool_specs(T, tt) + [pl.BlockSpec((POOL_WIDTH, POOL_WIDTH), lambda i: (0, 0)),
                                       pl.BlockSpec((1, POOL_WIDTH), lambda i: (0, 0))],
        out_specs=pl.BlockSpec((tt, POOL_WIDTH), lambda i: (i, 0)),
        out_shape=jax.ShapeDtypeStruct((T, POOL_WIDTH), F32),
        compiler_params=_params(),
    )(proj, proj, wbd, scale)


def _pool_bwd(proj, dymix, wbd, scale):
    T = proj.shape[0]
    tt = _tile(T, 512)
    hb = tt // POOL_HALO
    nblk = T // tt
    n = tt + POOL_HALO

    def body(a_ref, halo_ref, dy_ref, dyn_ref, w_ref, sc_ref, da_ref, dw_ref, dsc_ref):
        i = pl.program_id(0)
        halo = jnp.where(i > 0, halo_ref[...], 0.0)
        d = _pool_diff(a_ref[...], halo, i * tt, tt)
        db = d.astype(BF16)
        wv = w_ref[...]
        sc = sc_ref[...]
        dy = dy_ref[...]
        dys = dy * sc

        @pl.when(i == 0)
        def _():
            dw_ref[...] = jnp.zeros_like(dw_ref)
            dsc_ref[...] = jnp.zeros_like(dsc_ref)

        dsc_ref[...] += jnp.sum(dy * _nn(db, wv), axis=0, keepdims=True)
        dw_ref[...] += _tn(db, dys.astype(BF16))
        dyn = jnp.where(i < nblk - 1, dyn_ref[...], 0.0) * sc
        dd = _nt(jnp.concatenate([dys, dyn], axis=0).astype(BF16), wv)
        grp = lax.broadcasted_iota(jnp.int32, (n, POOL_WIDTH), 1) // POOL_GW
        t_glob = lax.broadcasted_iota(jnp.int32, (n, POOL_WIDTH), 0) + i * tt
        e = dd / _pool_count(t_glob, grp)
        r2 = e + pltpu.roll(e, n - 1, 0)
        r4 = r2 + pltpu.roll(r2, n - 2, 0)
        r8 = r4 + pltpu.roll(r4, n - 4, 0)
        r16 = r8 + pltpu.roll(r8, n - 8, 0)
        da_ref[...] = (_pool_select(r2, r4, r8, r16, grp) - dd)[:tt].astype(BF16)

    return pl.pallas_call(
        body, name="pool_bwd", grid=(nblk,),
        in_specs=_pool_specs(T, tt) + [
            pl.BlockSpec((tt, POOL_WIDTH), lambda i: (i, 0)),
            pl.BlockSpec((POOL_HALO, POOL_WIDTH), lambda i: (jnp.minimum((i + 1) * hb, T // POOL_HALO - 1), 0)),
            pl.BlockSpec((POOL_WIDTH, POOL_WIDTH), lambda i: (0, 0)), pl.BlockSpec((1, POOL_WIDTH), lambda i: (0, 0))],
        out_specs=[pl.BlockSpec((tt, POOL_WIDTH), lambda i: (i, 0)),
                   pl.BlockSpec((POOL_WIDTH, POOL_WIDTH), lambda i: (0, 0)),
                   pl.BlockSpec((1, POOL_WIDTH), lambda i: (0, 0))],
        out_shape=[jax.ShapeDtypeStruct((T, POOL_WIDTH), BF16),
                   jax.ShapeDtypeStruct((POOL_WIDTH, POOL_WIDTH), F32),
                   jax.ShapeDtypeStruct((1, POOL_WIDTH), F32)],
        compiler_params=_params(),
    )(proj, proj, dymix, dymix, wbd, scale)


def _head_select(stacked, grp):
    out = jnp.where(grp == 0, stacked[0:CHUNK], 0.0)
    for h in range(1, SG_HEADS):
        out = out + jnp.where(grp == h, stacked[h * CHUNK:(h + 1) * CHUNK], 0.0)
    return out


def _sg_specs(tt):
    return [pl.BlockSpec((tt, SG_WIDTH), lambda i: (i, 1)), pl.BlockSpec((tt, SG_WIDTH), lambda i: (i, 2))]


def _sg_fwd(proj, wm, bias, g):
    T = proj.shape[0]
    tt = _tile(T, 512)

    def body(u_ref, v_ref, wm_ref, b_ref, g_ref, y_ref):
        zu = _gelu(u_ref[...])
        vn, _, _ = _rms_fwd(_gelu(v_ref[...]), g_ref[...])
        grp = lax.broadcasted_iota(jnp.int32, (CHUNK, SG_WIDTH), 1) // SB_HD
        for n in range(tt // CHUNK):
            rows = slice(n * CHUNK, (n + 1) * CHUNK)
            sv = _head_select(_nn(wm_ref[...], vn[rows].astype(BF16)), grp) + b_ref[...]
            y_ref[rows, :] = zu[rows] * sv

    return pl.pallas_call(
        body, name="sg_fwd", grid=(T // tt,),
        in_specs=_sg_specs(tt) + [pl.BlockSpec((SG_HEADS * CHUNK, CHUNK), lambda i: (0, 0)),
                                  pl.BlockSpec((CHUNK, SG_WIDTH), lambda i: (0, 0)),
                                  pl.BlockSpec((1, SG_WIDTH), lambda i: (0, 0))],
        out_specs=pl.BlockSpec((tt, SG_WIDTH), lambda i: (i, 0)),
        out_shape=jax.ShapeDtypeStruct((T, SG_WIDTH), F32),
        compiler_params=_params(),
    )(proj, proj, wm, bias, g)


def _sg_bwd(proj, dymix, wm, wmt, bias, g, hosted=None):
    T = proj.shape[0]
    tt = _tile(T, 512)
    nblk = T // tt

    def body(*refs):
        i = pl.program_id(0)
        (u_ref, v_ref, dy_ref, wm_ref, wmt_ref, b_ref, g_ref, du_ref, dv_ref, dw_ref, db_ref, dg_ref,
         dvn_ref, dbias_ref), start, wait = _host(hosted, refs, 7, 5, i == 0, i == nblk - 1)
        start()
        up, vp = u_ref[...], v_ref[...]
        gv = g_ref[...]
        zu, zv = _gelu(up), _gelu(vp)
        vn, xhat, r = _rms_fwd(zv, gv)
        gu = _gelu_grad(up)
        grp = lax.broadcasted_iota(jnp.int32, (CHUNK, SG_WIDTH), 1) // SB_HD

        @pl.when(i == 0)
        def _():
            dw_ref[...] = jnp.zeros_like(dw_ref)
            dbias_ref[...] = jnp.zeros_like(dbias_ref)
            dg_ref[...] = jnp.zeros_like(dg_ref)

        for n in range(tt // CHUNK):
            rows = slice(n * CHUNK, (n + 1) * CHUNK)
            vc = vn[rows].astype(BF16)
            sv = _head_select(_nn(wm_ref[...], vc), grp) + b_ref[...]
            dy = dy_ref[rows, :]
            du_ref[rows, :] = (dy * sv * gu[rows]).astype(BF16)
            dsv = dy * zu[rows]
            dsvb = dsv.astype(BF16)
            dvn_ref[rows, :] = _head_select(_nn(wmt_ref[...], dsvb), grp)
            stacked = jnp.concatenate([jnp.where(grp == h, dsv, 0.0) for h in range(SG_HEADS)], axis=0)
            dw_ref[...] += _nt(stacked.astype(BF16), vc)
            dbias_ref[...] += dsv

        dzv, dgrow = _rms_bwd(dvn_ref[...], xhat, r, gv)
        dg_ref[...] += jnp.sum(dgrow, axis=0, keepdims=True)
        dv_ref[...] = (dzv * _gelu_grad(vp)).astype(BF16)

        @pl.when(i == nblk - 1)
        def _():
            t_i = lax.broadcasted_iota(jnp.int32, (SG_HEADS * CHUNK, CHUNK), 0) % CHUNK
            s_i = lax.broadcasted_iota(jnp.int32, (SG_HEADS * CHUNK, CHUNK), 1)
            dw_ref[...] = jnp.where(s_i <= t_i, dw_ref[...], 0.0)
            lane = lax.broadcasted_iota(jnp.int32, (CHUNK, LANES), 1)
            acc = jnp.zeros((CHUNK, LANES), F32)
            for h in range(SG_HEADS):
                tot = jnp.sum(jnp.where(grp == h, dbias_ref[...], 0.0), axis=1, keepdims=True)
                acc = acc + jnp.where(lane == h, tot, 0.0)
            db_ref[...] = acc

        wait()

    h_ins = hosted.ins if hosted else []
    res = pl.pallas_call(
        body, name="sg_bwd_hosting" if hosted else "sg_bwd", grid=(nblk,),
        in_specs=_sg_specs(tt) + [pl.BlockSpec((tt, SG_WIDTH), lambda i: (i, 1)),
                                  pl.BlockSpec((SG_HEADS * CHUNK, CHUNK), lambda i: (0, 0)),
                                  pl.BlockSpec((SG_HEADS * CHUNK, CHUNK), lambda i: (0, 0)),
                                  pl.BlockSpec((CHUNK, SG_WIDTH), lambda i: (0, 0)),
                                  pl.BlockSpec((1, SG_WIDTH), lambda i: (0, 0))] + [HBM_SPEC] * len(h_ins),
        out_specs=[pl.BlockSpec((tt, SG_WIDTH), lambda i: (i, 0)), pl.BlockSpec((tt, SG_WIDTH), lambda i: (i, 0)),
                   pl.BlockSpec((SG_HEADS * CHUNK, CHUNK), lambda i: (0, 0)),
                   pl.BlockSpec((CHUNK, LANES), lambda i: (0, 0)), pl.BlockSpec((1, SG_WIDTH), lambda i: (0, 0))]
        + [HBM_SPEC] * len(h_ins),
        out_shape=[jax.ShapeDtypeStruct((T, SG_WIDTH), BF16), jax.ShapeDtypeStruct((T, SG_WIDTH), BF16),
                   jax.ShapeDtypeStruct((SG_HEADS * CHUNK, CHUNK), F32),
                   jax.ShapeDtypeStruct((CHUNK, LANES), F32), jax.ShapeDtypeStruct((1, SG_WIDTH), F32)]
        + (hosted.out_shapes if hosted else []),
        scratch_shapes=[pltpu.VMEM((tt, SG_WIDTH), F32), pltpu.VMEM((CHUNK, SG_WIDTH), F32)]
        + (hosted.sems() if hosted else []),
        compiler_params=_params(has_side_effects=hosted is not None),
    )(proj, proj, dymix, wm, wmt, bias, g, *h_ins)
    return res[:5], res[5:]


def _split_dot(x, u):
    hi = x.astype(BF16)
    lo = (x - hi.astype(F32)).astype(BF16)
    return _nn(hi, u) + _nn(lo, u)


def _sb_logits(z):
    lb = jnp.minimum(z, 0.0) - jnp.log(1.0 + jnp.exp(-jnp.abs(z)))
    return lb, lb - z


ATTN_STRIP = 256
ATTN_SUBS = 2


def _by_strips(n_rows, fn):
    parts = None
    for r in range(0, n_rows, ATTN_STRIP):
        res = fn(slice(r, r + ATTN_STRIP))
        parts = [[v] for v in res] if parts is None else [p + [v] for p, v in zip(parts, res)]
    return [jnp.concatenate(p, axis=0) for p in parts]


def _attn_qkv_specs(tq, T):
    base = (IN_COLS - 3 * SB_WIDTH - QKV_OFF) // LANES
    nb = SB_WIDTH // LANES
    return [pl.BlockSpec((tq, LANES), lambda p, i: (i, base + p)),
            pl.BlockSpec((T, LANES), lambda p, i: (0, base + nb + p)),
            pl.BlockSpec((T, LANES), lambda p, i: (0, base + 2 * nb + p))]


class _Hosted:
    def __init__(self, ins, out_shapes, n_sems, copies, in_place=False):
        self.ins, self.out_shapes, self.n_sems, self.copies = ins, out_shapes, n_sems, copies
        self.in_place = in_place

    @property
    def n(self):
        return len(self.ins)

    def aliases(self, n_in, n_out):
        return {n_in + k: n_out + k for k in range(self.n)} if self.in_place else {}

    def sems(self):
        return [pltpu.SemaphoreType.DMA((self.n_sems,)), pltpu.SemaphoreType.DMA((self.n_sems,))]

    def start(self, src, dst, ssem, rsem):
        for send, _ in self.copies(src, dst, ssem, rsem):
            send.start()

    def wait(self, src, dst, ssem, rsem):
        for send, recv in self.copies(src, dst, ssem, rsem):
            recv.wait_recv()
            send.wait_send()


def _host(hosted, refs, n_in, n_out, first, last):
    if hosted is None:
        return refs, lambda: None, lambda: None
    n = hosted.n
    own_in, h_in = refs[:n_in], refs[n_in:n_in + n]
    own_out, h_out = refs[n_in + n:n_in + n + n_out], refs[n_in + n + n_out:n_in + 2 * n + n_out]
    rest = refs[n_in + 2 * n + n_out:]
    ssem, rsem = rest[-2:]

    def start():
        if first is True:
            hosted.start(h_in, h_out, ssem, rsem)
        else:
            pl.when(first)(lambda: hosted.start(h_in, h_out, ssem, rsem))

    def wait():
        if last is True:
            hosted.wait(h_in, h_out, ssem, rsem)
        else:
            pl.when(last)(lambda: hosted.wait(h_in, h_out, ssem, rsem))

    return own_in + own_out + rest[:-2], start, wait


def _attn_fwd(qkv, hosted=None):
    T = qkv.shape[0]
    tk = _tile(T, ATTN_TILE)
    n_sub = ATTN_SUBS if T % (ATTN_SUBS * tk) == 0 else 1
    tq = n_sub * tk
    n_p, nq = SB_WIDTH // LANES, T // tq

    def body(*refs):
        p, i = pl.program_id(0), pl.program_id(1)
        (q_ref, k_ref, v_ref, o_ref), start, wait = _host(
            hosted, refs, 3, 1, jnp.logical_and(p == 0, i == 0), jnp.logical_and(p == n_p - 1, i == nq - 1))
        start()
        lane = lax.broadcasted_iota(jnp.int32, (tk, LANES), 1)
        row = lax.broadcasted_iota(jnp.int32, (tk, tk), 0)
        col = lax.broadcasted_iota(jnp.int32, (tk, tk), 1)
        after = jnp.where(row > col, 1.0, 0.0).astype(BF16)
        valid = col < row
        qh = {}
        for sb in range(n_sub):
            q = q_ref[sb * tk:(sb + 1) * tk, :].astype(F32)
            for hh in range(2):
                qh[(sb, hh)] = jnp.where((lane // SB_HD) == hh, q * SB_SCALE, 0.0).astype(BF16)

        def tiles(todo, state):
            chains = [(n, hh) for n in range(len(todo)) for hh in range(2)]
            kv = []
            for _, j, _ in todo:
                ks = pl.ds(pl.multiple_of(j * tk, tk), tk)
                kv.append((k_ref[ks, :], v_ref[ks, :]))
            z = {(n, hh): _nt(qh[(todo[n][0], hh)], kv[n][0]) for n, hh in chains}
            lb, lmb, lm_sum = {}, {}, {}
            for n, hh in chains:
                def logits(rows, z=z[(n, hh)], mask=todo[n][2]):
                    lb, lm = _sb_logits(z[rows])
                    if mask is not None:
                        lm = jnp.where(mask[rows], lm, 0.0)
                    return lb, lm.astype(BF16), jnp.sum(lm, axis=1, keepdims=True)

                lb[(n, hh)], lmb[(n, hh)], lm_sum[(n, hh)] = _by_strips(tk, logits)
            x = {c: _nn(lmb[c], after) for c in chains}
            new = dict(state)
            for n, hh in chains:
                key = (todo[n][0], hh)
                carry, acc = new[key]

                def weights(rows, lb=lb[(n, hh)], x=x[(n, hh)], carry=carry, mask=todo[n][2]):
                    a = jnp.exp(lb[rows] + x[rows] + carry[rows])
                    if mask is not None:
                        a = jnp.where(mask[rows], a, 0.0)
                    return (a.astype(BF16),)

                (ab,) = _by_strips(tk, weights)
                new[key] = (carry + lm_sum[(n, hh)], acc + _nn(ab, kv[n][1]))
            return new

        def live(state, sb):
            return jnp.maximum(jnp.max(state[(sb, 0)][0]), jnp.max(state[(sb, 1)][0]))

        first = n_sub * i
        zero = (jnp.zeros((tk, 1), F32), jnp.zeros((tk, LANES), F32))
        todo = []
        for sb in range(n_sub):
            gate = jnp.broadcast_to(first > 0, (tk, tk)) if sb == 0 else None
            todo += [(sb, first + sb, valid), (sb, jnp.maximum(first + sb - 1, 0), gate)]
        state = tiles(todo, {(sb, hh): zero for sb in range(n_sub) for hh in range(2)})
        for sb in range(n_sub):
            def cond(st):
                return jnp.logical_and(st[0] >= 0, st[2] > UNDERFLOW)

            def step(st, sb=sb):
                mine = tiles([(sb, st[0], None)], st[1])
                return st[0] - 1, mine, live(mine, sb)

            mine = {k: v for k, v in state.items() if k[0] == sb}
            _, mine, _ = lax.while_loop(cond, step, (first + sb - 2, mine, live(mine, sb)))
            o_ref[sb * tk:(sb + 1) * tk, :] = jnp.where(lane < SB_HD, mine[(sb, 0)][1], mine[(sb, 1)][1])
        wait()

    h_ins = hosted.ins if hosted else []
    res = pl.pallas_call(
        body, name="attn_fwd_hosting" if hosted else "attn_fwd", grid=(n_p, nq),
        in_specs=_attn_qkv_specs(tq, T) + [HBM_SPEC] * len(h_ins),
        out_specs=[pl.BlockSpec((tq, LANES), lambda p, i: (i, p))] + [HBM_SPEC] * len(h_ins),
        out_shape=[jax.ShapeDtypeStruct((T, SB_WIDTH), F32)] + (hosted.out_shapes if hosted else []),
        input_output_aliases=hosted.aliases(3, 1) if hosted else {},
        scratch_shapes=hosted.sems() if hosted else [],
        compiler_params=_params(has_side_effects=hosted is not None),
    )(qkv, qkv, qkv, *h_ins)
    return res[0], res[1:]


def _attn_bwd(qkv, o, dymix, hosted=None):
    T = qkv.shape[0]
    tk = _tile(T, ATTN_TILE)
    n_sub = ATTN_SUBS if T % (ATTN_SUBS * tk) == 0 else 1
    tq = n_sub * tk
    n_p, nq = SB_WIDTH // LANES, T // tq
    yc_blk = (POOL_WIDTH + SG_WIDTH) // LANES

    def body(*refs):
        p, i = pl.program_id(0), pl.program_id(1)
        (q_ref, k_ref, v_ref, o_ref, do_ref, dq_ref, dk_ref, dv_ref), start, wait = _host(
            hosted, refs, 5, 3, jnp.logical_and(p == 0, i == 0), jnp.logical_and(p == n_p - 1, i == nq - 1))
        start()
        lane = lax.broadcasted_iota(jnp.int32, (tk, LANES), 1)
        row = lax.broadcasted_iota(jnp.int32, (tk, tk), 0)
        col = lax.broadcasted_iota(jnp.int32, (tk, tk), 1)
        after = jnp.where(row > col, 1.0, 0.0).astype(BF16)
        from_here = jnp.where(row >= col, 1.0, 0.0).astype(BF16)
        from_here2 = jnp.concatenate([from_here, from_here], axis=0)
        valid = col < row

        @pl.when(i == 0)
        def _():
            dk_ref[...] = jnp.zeros_like(dk_ref)
            dv_ref[...] = jnp.zeros_like(dv_ref)

        qh, dohb, delta = {}, {}, {}
        for sb in range(n_sub):
            rows = slice(sb * tk, (sb + 1) * tk)
            q, ov, dov = q_ref[rows, :].astype(F32), o_ref[rows, :], do_ref[rows, :]
            for hh in range(2):
                head = (lane // SB_HD) == hh
                qh[(sb, hh)] = jnp.where(head, q * SB_SCALE, 0.0).astype(BF16)
                dohb[(sb, hh)] = jnp.where(head, dov, 0.0).astype(BF16)
                delta[(sb, hh)] = jnp.sum(dohb[(sb, hh)].astype(F32) * ov, axis=1, keepdims=True)

        def tiles(todo, state):
            chains = [(n, hh) for n in range(len(todo)) for hh in range(2)]
            kv, where = [], []
            for _, j, _ in todo:
                ks = pl.ds(pl.multiple_of(j * tk, tk), tk)
                where.append(ks)
                kv.append((k_ref[ks, :], v_ref[ks, :]))
            z = {(n, hh): _nt(qh[(todo[n][0], hh)], kv[n][0]) for n, hh in chains}
            da = {(n, hh): _nt(dohb[(todo[n][0], hh)], kv[n][1]) for n, hh in chains}
            lb, lmb, lm_sum = {}, {}, {}
            for n, hh in chains:
                def logits(rows, z=z[(n, hh)], mask=todo[n][2]):
                    lb, lm = _sb_logits(z[rows])
                    if mask is not None:
                        lm = jnp.where(mask[rows], lm, 0.0)
                    return lb, lm.astype(BF16), jnp.sum(lm, axis=1, keepdims=True)

                lb[(n, hh)], lmb[(n, hh)], lm_sum[(n, hh)] = _by_strips(tk, logits)
            x = {c: _nn(lmb[c], after) for c in chains}
            c_a = {k: v[0] for k, v in state.items()}
            ab, g, g_split, g_sum = {}, {}, {}, {}
            for n, hh in chains:
                key = (todo[n][0], hh)

                def weights(rows, lb=lb[(n, hh)], x=x[(n, hh)], da=da[(n, hh)], c_a=c_a[key], mask=todo[n][2]):
                    a = jnp.exp(lb[rows] + x[rows] + c_a[rows])
                    if mask is not None:
                        a = jnp.where(mask[rows], a, 0.0)
                    ab = a.astype(BF16)
                    g = da[rows] * ab.astype(F32)
                    hi = g.astype(BF16)
                    lo = (g - hi.astype(F32)).astype(BF16)
                    return ab, g, jnp.concatenate([hi, lo], axis=1), jnp.sum(g, axis=1, keepdims=True)

                ab[(n, hh)], g[(n, hh)], g_split[(n, hh)], g_sum[(n, hh)] = _by_strips(tk, weights)
                c_a[key] = c_a[key] + lm_sum[(n, hh)]
            right = {c: _nn(g_split[c], from_here2) for c in chains}
            c_r = {k: v[1] for k, v in state.items()}
            dzb = {}
            for n, hh in chains:
                key = (todo[n][0], hh)

                def logit_grads(rows, lb=lb[(n, hh)], g=g[(n, hh)], right=right[(n, hh)], c_r=c_r[key],
                                delta=delta[key], mask=todo[n][2]):
                    sig = jnp.exp(lb[rows])
                    left = delta[rows] - (c_r[rows] + right[rows])
                    dz = g[rows] * (1.0 - sig) - left * sig
                    if mask is not None:
                        dz = jnp.where(mask[rows], dz, 0.0)
                    return (dz.astype(BF16),)

                (dzb[(n, hh)],) = _by_strips(tk, logit_grads)
                c_r[key] = c_r[key] + g_sum[(n, hh)]
            dqa = {k: v[2] for k, v in state.items()}
            for n in range(len(todo)):
                sb = todo[n][0]
                dk_ref[where[n], :] += _tn(dzb[(n, 0)], qh[(sb, 0)]) + _tn(dzb[(n, 1)], qh[(sb, 1)])
                dv_ref[where[n], :] += _tn(ab[(n, 0)], dohb[(sb, 0)]) + _tn(ab[(n, 1)], dohb[(sb, 1)])
                for hh in range(2):
                    dqa[(sb, hh)] = dqa[(sb, hh)] + _nn(dzb[(n, hh)], kv[n][0])
            return {k: (c_a[k], c_r[k], dqa[k]) for k in state}

        def live(state, sb):
            return jnp.maximum(jnp.max(state[(sb, 0)][0]), jnp.max(state[(sb, 1)][0]))

        first = n_sub * i
        zero = (jnp.zeros((tk, 1), F32), jnp.zeros((tk, 1), F32), jnp.zeros((tk, LANES), F32))
        todo = []
        for sb in range(n_sub):
            gate = jnp.broadcast_to(first > 0, (tk, tk)) if sb == 0 else None
            todo += [(sb, first + sb, valid), (sb, jnp.maximum(first + sb - 1, 0), gate)]
        state = tiles(todo, {(sb, hh): zero for sb in range(n_sub) for hh in range(2)})
        for sb in range(n_sub):
            def cond(st):
                return jnp.logical_and(st[0] >= 0, st[2] > UNDERFLOW)

            def step(st, sb=sb):
                mine = tiles([(sb, st[0], None)], st[1])
                return st[0] - 1, mine, live(mine, sb)

            mine = {k: v for k, v in state.items() if k[0] == sb}
            _, mine, _ = lax.while_loop(cond, step, (first + sb - 2, mine, live(mine, sb)))
            dq_ref[sb * tk:(sb + 1) * tk, :] = (
                jnp.where(lane < SB_HD, mine[(sb, 0)][2], mine[(sb, 1)][2]) * SB_SCALE).astype(BF16)
        wait()

    h_ins = hosted.ins if hosted else []
    res = pl.pallas_call(
        body, name="attn_bwd_hosting" if hosted else "attn_bwd", grid=(n_p, nq),
        in_specs=_attn_qkv_specs(tq, T) + [pl.BlockSpec((tq, LANES), lambda p, i: (i, p)),
                                           pl.BlockSpec((tq, LANES), lambda p, i: (i, yc_blk + p))]
        + [HBM_SPEC] * len(h_ins),
        out_specs=[pl.BlockSpec((tq, LANES), lambda p, i: (i, p)), pl.BlockSpec((T, LANES), lambda p, i: (0, p)),
                   pl.BlockSpec((T, LANES), lambda p, i: (0, p))] + [HBM_SPEC] * len(h_ins),
        out_shape=[jax.ShapeDtypeStruct((T, SB_WIDTH), BF16)] + [jax.ShapeDtypeStruct((T, SB_WIDTH), F32)] * 2
        + (hosted.out_shapes if hosted else []),
        scratch_shapes=hosted.sems() if hosted else [],
        compiler_params=_params(has_side_effects=hosted is not None),
    )(qkv, qkv, qkv, o, dymix, *h_ins)
    return res[:3], res[3:]


def _outproj_fwd(x, ya, yb, yc, w, hosted=None):
    T, D = x.shape
    tt = _tile(T, 512)
    nt = T // tt

    def body(*refs):
        i = pl.program_id(0)
        (x_ref, ya_ref, yb_ref, yc_ref, w_ref, x1_ref, ymix_ref), start, wait = _host(
            hosted, refs, 5, 2, i == 0, i == nt - 1)
        start()
        ymix_ref[:, 0:POOL_WIDTH] = ya_ref[...].astype(BF16)
        ymix_ref[:, POOL_WIDTH:POOL_WIDTH + SG_WIDTH] = yb_ref[...].astype(BF16)
        ymix_ref[:, POOL_WIDTH + SG_WIDTH:] = yc_ref[...].astype(BF16)
        x1_ref[...] = x_ref[...] + _nn(ymix_ref[...], w_ref[...])
        wait()

    row = lambda width: pl.BlockSpec((tt, width), lambda i: (i, 0))
    h_ins = hosted.ins if hosted else []
    res = pl.pallas_call(
        body, name="outproj_fwd_hosting" if hosted else "outproj_fwd", grid=(nt,),
        in_specs=[row(D), row(POOL_WIDTH), row(SG_WIDTH), row(SB_WIDTH), pl.BlockSpec((D, D), lambda i: (0, 0))]
        + [HBM_SPEC] * len(h_ins),
        out_specs=[row(D), row(D)] + [HBM_SPEC] * len(h_ins),
        out_shape=[jax.ShapeDtypeStruct((T, D), F32), jax.ShapeDtypeStruct((T, D), BF16)]
        + (hosted.out_shapes if hosted else []),
        input_output_aliases=hosted.aliases(5, 2) if hosted else {},
        scratch_shapes=hosted.sems() if hosted else [],
        compiler_params=_params(has_side_effects=hosted is not None),
    )(x, ya, yb, yc, w, *h_ins)
    return res[:2], res[2:]


def _nt_matmul(a, w):
    T, N = a.shape
    K = w.shape[0]
    tt = _tile(T, 512)

    def body(a_ref, w_ref, o_ref):
        o_ref[...] = _nt(a_ref[...].astype(BF16), w_ref[...])

    return pl.pallas_call(
        body, name="nt_matmul", grid=(T // tt,),
        in_specs=[pl.BlockSpec((tt, N), lambda i: (i, 0)), pl.BlockSpec((K, N), lambda i: (0, 0))],
        out_specs=pl.BlockSpec((tt, K), lambda i: (i, 0)),
        out_shape=jax.ShapeDtypeStruct((T, K), F32),
        compiler_params=_params(),
    )(a, w)


def _tn_matmul(a, b, name, n_split=1, hosted=None):
    T, K = a.shape
    N = b.shape[1]
    tk = _tile(K, 1024)
    tn = _tile(N // n_split, 1024)
    tt = _tile(T, 2048)
    nper = N // n_split // tn
    nk, nn, nt = K // tk, N // tn, T // tt

    def body(*refs):
        k, n, t = pl.program_id(0), pl.program_id(1), pl.program_id(2)
        (a_ref, b_ref, o_ref), start, wait = _host(
            hosted, refs, 2, 1, jnp.logical_and(jnp.logical_and(k == 0, n == 0), t == 0),
            jnp.logical_and(jnp.logical_and(k == nk - 1, n == nn - 1), t == nt - 1))
        start()

        @pl.when(t == 0)
        def _():
            o_ref[...] = jnp.zeros_like(o_ref)

        o_ref[...] += _tn(a_ref[...], b_ref[...].astype(BF16))
        wait()

    h_ins = hosted.ins if hosted else []
    res = pl.pallas_call(
        body, name=name + "_hosting" if hosted else name, grid=(nk, nn, nt),
        in_specs=[pl.BlockSpec((tt, tk), lambda k, n, t: (t, k)), pl.BlockSpec((tt, tn), lambda k, n, t: (t, n))]
        + [HBM_SPEC] * len(h_ins),
        out_specs=[pl.BlockSpec((None, tk, tn), lambda k, n, t: (n // nper, k, n % nper))] + [HBM_SPEC] * len(h_ins),
        out_shape=[jax.ShapeDtypeStruct((n_split, K, N // n_split), F32)] + (hosted.out_shapes if hosted else []),
        scratch_shapes=hosted.sems() if hosted else [],
        compiler_params=_params(has_side_effects=hosted is not None),
    )(a, b, *h_ins)
    return (res[0], res[1:]) if hosted else res[0]


def _mlp_fwd(x, g, w_up, w_down, hosted=None):
    T, D = x.shape
    n_blk, _, width = w_up.shape
    F = n_blk * width
    tt = _tile(T, 1024)
    fc = _tile(width, MLP_CHUNK)
    per = width // fc
    nc = F // fc
    nt = T // tt

    def body(*refs):
        i, c = pl.program_id(0), pl.program_id(1)
        (x_ref, g_ref, wu_ref, wd_ref, y_ref, h_ref, u_ref, a_ref), start, wait = _host(
            hosted, refs, 4, 4, jnp.logical_and(i == 0, c == 0), jnp.logical_and(i == nt - 1, c == nc - 1))
        start()

        @pl.when(c == 0)
        def _():
            xv = x_ref[...]
            h, _, _ = _rms_fwd(xv, g_ref[...])
            h_ref[...] = h.astype(BF16)
            y_ref[...] = xv

        u = _nn(h_ref[...], wu_ref[...])
        u_ref[...] = u.astype(BF16)
        a = jnp.square(jnp.maximum(u, 0.0)).astype(BF16)
        a_ref[...] = a
        y_ref[...] += _nn(a, wd_ref[...])
        wait()

    h_ins = hosted.ins if hosted else []
    res = pl.pallas_call(
        body, name="mlp_fwd_hosting" if hosted else "mlp_fwd", grid=(nt, nc),
        in_specs=[pl.BlockSpec((tt, D), lambda i, c: (i, 0)), pl.BlockSpec((1, D), lambda i, c: (0, 0)),
                  pl.BlockSpec((None, D, fc), lambda i, c: (c // per, 0, c % per)),
                  pl.BlockSpec((fc, D), lambda i, c: (c, 0))]
        + [HBM_SPEC] * len(h_ins),
        out_specs=[pl.BlockSpec((tt, D), lambda i, c: (i, 0)), pl.BlockSpec((tt, D), lambda i, c: (i, 0)),
                   pl.BlockSpec((tt, fc), lambda i, c: (i, c)), pl.BlockSpec((tt, fc), lambda i, c: (i, c))]
        + [HBM_SPEC] * len(h_ins),
        out_shape=[jax.ShapeDtypeStruct((T, D), F32), jax.ShapeDtypeStruct((T, D), BF16),
                   jax.ShapeDtypeStruct((T, F), BF16), jax.ShapeDtypeStruct((T, F), BF16)]
        + (hosted.out_shapes if hosted else []),
        scratch_shapes=hosted.sems() if hosted else [],
        compiler_params=_params(has_side_effects=hosted is not None),
    )(x, g, w_up, w_down, *h_ins)
    return res[:4], res[4:]


def _mlp_bwd(dy, x, g, u, w_up, w_down, hosted=None):
    T, D = x.shape
    n_blk, _, width = w_up.shape
    F = n_blk * width
    tt = _tile(T, 1024)
    fc = _tile(width, MLP_CHUNK)
    per = width // fc
    nc = F // fc
    nt = T // tt

    def body(*refs):
        i, c = pl.program_id(0), pl.program_id(1)
        (dy_ref, x_ref, g_ref, u_ref, wu_ref, wd_ref, dx_ref, du_ref, dg_ref, dyb_ref, dh_ref), start, wait = _host(
            hosted, refs, 6, 3, jnp.logical_and(i == 0, c == 0), jnp.logical_and(i == nt - 1, c == nc - 1))
        start()

        @pl.when(c == 0)
        def _():
            dyb_ref[...] = dy_ref[...].astype(BF16)
            dh_ref[...] = jnp.zeros_like(dh_ref)

        @pl.when(jnp.logical_and(i == 0, c == 0))
        def _():
            dg_ref[...] = jnp.zeros_like(dg_ref)

        da = _nt(dyb_ref[...], wd_ref[...])
        du = (da * (2.0 * jnp.maximum(u_ref[...].astype(F32), 0.0))).astype(BF16)
        du_ref[...] = du
        dh_ref[...] += _nt(du, wu_ref[...])

        @pl.when(c == nc - 1)
        def _():
            gv = g_ref[...]
            _, xhat, r = _rms_fwd(x_ref[...], gv)
            dx, dgrow = _rms_bwd(dh_ref[...], xhat, r, gv)
            dx_ref[...] = dy_ref[...] + dx
            dg_ref[...] += jnp.sum(dgrow, axis=0, keepdims=True)

        wait()

    h_ins = hosted.ins if hosted else []
    res = pl.pallas_call(
        body, name="mlp_bwd_hosting" if hosted else "mlp_bwd", grid=(nt, nc),
        in_specs=[pl.BlockSpec((tt, D), lambda i, c: (i, 0)), pl.BlockSpec((tt, D), lambda i, c: (i, 0)),
                  pl.BlockSpec((1, D), lambda i, c: (0, 0)), pl.BlockSpec((tt, fc), lambda i, c: (i, c)),
                  pl.BlockSpec((None, D, fc), lambda i, c: (c // per, 0, c % per)),
                  pl.BlockSpec((fc, D), lambda i, c: (c, 0))]
        + [HBM_SPEC] * len(h_ins),
        out_specs=[pl.BlockSpec((tt, D), lambda i, c: (i, 0)), pl.BlockSpec((tt, fc), lambda i, c: (i, c)),
                   pl.BlockSpec((1, D), lambda i, c: (0, 0))] + [HBM_SPEC] * len(h_ins),
        out_shape=[jax.ShapeDtypeStruct((T, D), F32), jax.ShapeDtypeStruct((T, F), BF16),
                   jax.ShapeDtypeStruct((1, D), F32)] + (hosted.out_shapes if hosted else []),
        scratch_shapes=[pltpu.VMEM((tt, D), BF16), pltpu.VMEM((tt, D), F32)] + (hosted.sems() if hosted else []),
        compiler_params=_params(has_side_effects=hosted is not None),
    )(dy, x, g, u, w_up, w_down, *h_ins)
    return res[:3], res[3:]


def _loss_head(x, g, target):
    T, D = x.shape
    tt = _tile(T, 512)

    def body(x_ref, g_ref, t_ref, loss_ref, dx_ref, dg_ref):
        gv = g_ref[...]
        y, xhat, r = _rms_fwd(x_ref[...], gv)
        err = y - t_ref[...]
        dx, dgrow = _rms_bwd(err * (1.0 / D), xhat, r, gv)
        dx_ref[...] = dx

        @pl.when(pl.program_id(0) == 0)
        def _():
            loss_ref[...] = jnp.zeros_like(loss_ref)
            dg_ref[...] = jnp.zeros_like(dg_ref)

        loss_ref[...] += 0.5 * jnp.sum(jnp.mean(err * err, axis=-1, keepdims=True), axis=0, keepdims=True)
        dg_ref[...] += jnp.sum(dgrow, axis=0, keepdims=True)

    return pl.pallas_call(
        body, name="loss_head", grid=(T // tt,),
        in_specs=[pl.BlockSpec((tt, D), lambda i: (i, 0)), pl.BlockSpec((1, D), lambda i: (0, 0)),
                  pl.BlockSpec((tt, D), lambda i: (i, 0))],
        out_specs=[pl.BlockSpec((1, LANES), lambda i: (0, 0)), pl.BlockSpec((tt, D), lambda i: (i, 0)),
                   pl.BlockSpec((1, D), lambda i: (0, 0))],
        out_shape=[jax.ShapeDtypeStruct((1, LANES), F32), jax.ShapeDtypeStruct((T, D), F32),
                   jax.ShapeDtypeStruct((1, D), F32)],
        compiler_params=_params(),
    )(x, g, target)


def _rows(shape, pref=512):
    last = shape[-1]
    rows = 1
    for s in shape[:-1]:
        rows *= s
    tr = rows
    if rows * last > 256 * 1024:
        for cand in (pref, 256, 128, 64, 32, 16, 8):
            if rows % cand == 0:
                tr = cand
                break
    return rows, last, tr


def _elementwise(fn, name, ins, n_out, out_dtype=F32):
    shape = ins[0].shape
    rows, last, tr = _rows(shape)
    flat = [a.reshape(rows, last) for a in ins]
    n_in = len(ins)

    def body(*refs):
        res = fn(*[r[...] for r in refs[:n_in]])
        if n_out == 1:
            res = (res,)
        for r, v in zip(refs[n_in:], res):
            r[...] = v.astype(r.dtype)

    spec = pl.BlockSpec((tr, last), lambda i: (i, 0))
    outs = pl.pallas_call(
        body, name=name, grid=(rows // tr,),
        in_specs=[spec] * n_in, out_specs=[spec] * n_out,
        out_shape=[jax.ShapeDtypeStruct((rows, last), out_dtype)] * n_out,
        compiler_params=_params(),
    )(*flat)
    return [o.reshape(shape) for o in outs]


def _add_pairs(gs, os, c_idx):
    n = len(gs)
    halves = [(g.shape[1] // 2, g.shape[2]) for g in gs]

    def body(c_ref, *refs):
        for a in range(n):
            refs[2 * n + a][...] = refs[2 * a][...] + refs[2 * a + 1][...]

    in_specs = []
    for h, C in halves:
        in_specs += [pl.BlockSpec((None, h, C), lambda q, c: (q, c[0], 0)), pl.BlockSpec((None, h, C), lambda q, c: (q, 0, 0))]
    return pl.pallas_call(
        body, name="add_pairs",
        grid_spec=pltpu.PrefetchScalarGridSpec(
            num_scalar_prefetch=1, grid=(N_CHIPS,), in_specs=in_specs,
            out_specs=[pl.BlockSpec((None, h, C), lambda q, c: (q, 0, 0)) for h, C in halves]),
        out_shape=[jax.ShapeDtypeStruct((N_CHIPS, h, C), F32) for h, C in halves],
        compiler_params=_params(),
    )(c_idx.astype(jnp.int32).reshape(1), *[x for pair in zip(gs, os) for x in pair])


def _add_chips(ps, rs, q_idx):
    n = len(ps)
    steps = 2
    blocks = [(p.shape[1] // steps, p.shape[2]) for p in ps]

    def body(q_ref, *refs):
        for a in range(n):
            p_ref, r0_ref, r1_ref, r2_ref = refs[4 * a:4 * a + 4]
            refs[4 * n + a][...] = (p_ref[...] + r0_ref[...]) + (r1_ref[...] + r2_ref[...])

    def arrived(tr, C, k):
        return pl.BlockSpec((None, tr, C), lambda i, q: (k, i, 0))

    in_specs, operands = [], []
    for (tr, C), p, r in zip(blocks, ps, rs):
        in_specs += [pl.BlockSpec((None, tr, C), lambda i, q: (q[0], i, 0)), arrived(tr, C, 0), arrived(tr, C, 1),
                     arrived(tr, C, 2)]
        operands += [p, r, r, r]
    return pl.pallas_call(
        body, name="add_chips",
        grid_spec=pltpu.PrefetchScalarGridSpec(
            num_scalar_prefetch=1, grid=(steps,), in_specs=in_specs,
            out_specs=[pl.BlockSpec((tr, C), lambda i, q: (i, 0)) for tr, C in blocks]),
        out_shape=[jax.ShapeDtypeStruct((p.shape[1], p.shape[2]), F32) for p in ps],
        compiler_params=_params(),
    )(q_idx.astype(jnp.int32).reshape(1), *operands)


def _adamw(w, g, m, v):
    m = ADAM_B1 * m + (1.0 - ADAM_B1) * g
    v = ADAM_B2 * v + (1.0 - ADAM_B2) * jnp.square(g)
    m_hat = m / (1.0 - ADAM_B1 ** ADAM_STEP)
    v_hat = v / (1.0 - ADAM_B2 ** ADAM_STEP)
    delta = -ADAM_LR * (m_hat / (jnp.sqrt(v_hat) + ADAM_EPS) + ADAM_WD * w)
    return delta, m, v


def _place():
    x, y, c = lax.axis_index("x"), lax.axis_index("y"), lax.axis_index("c")
    chips = [(1 - x, y), (x, 1 - y), (1 - x, 1 - y)]
    return x, y, c, chips


def _remote(src, dst, ssem, rsem, k, dev):
    return pltpu.make_async_remote_copy(src_ref=src, dst_ref=dst, send_sem=ssem.at[k], recv_sem=rsem.at[k],
                                        device_id=dev, device_id_type=MESH)


def _gather_weights(shards):
    n = len(shards)
    halves = [s.shape[1] // 2 for s in shards]

    def body(*refs):
        src, out = refs[:n], refs[n:2 * n]
        ssem, rsem = refs[2 * n:]
        x, y, c, chips = _place()
        me_q = 2 * x + y
        sib = (x, y, 1 - c)

        def half(a, q, cc):
            return out[a].at[q, :, pl.ds(cc * halves[a], halves[a]), :]

        first = []
        for a in range(n):
            mine = src[a].at[:, pl.ds(c * halves[a], halves[a]), :]
            for r, chip in enumerate(chips):
                first.append(_remote(mine, half(a, me_q, c), ssem, rsem, a * 3 + r, (*chip, c)))
        for cp in first:
            cp.start()
        passed = []
        for a in range(n):
            for r, chip in enumerate(chips):
                q = 2 * chip[0] + chip[1]
                k = a * 3 + r
                _remote(half(a, q, c), half(a, q, c), ssem, rsem, k, (*chip, c)).wait_recv()
                cp = _remote(half(a, q, c), half(a, q, c), ssem, rsem, 3 * n + k, sib)
                cp.start()
                passed.append(cp)
        for a in range(n):
            for r, chip in enumerate(chips):
                q = 2 * chip[0] + chip[1]
                _remote(half(a, q, 1 - c), half(a, q, 1 - c), ssem, rsem, 3 * n + a * 3 + r, sib).wait_recv()
        for cp in first + passed:
            cp.wait_send()

    return pl.pallas_call(
        body, name="gather_weights",
        in_specs=[HBM_SPEC] * n, out_specs=[HBM_SPEC] * n,
        out_shape=[jax.ShapeDtypeStruct((N_CHIPS,) + s.shape, s.dtype) for s in shards],
        scratch_shapes=[pltpu.SemaphoreType.DMA((6 * n,)), pltpu.SemaphoreType.DMA((6 * n,))],
        compiler_params=_params(has_side_effects=True),
    )(*shards)


def _gather_over_ici(shards):
    n = len(shards)
    halves = [s.shape[1] // 2 for s in shards]

    def copies(src, out, ssem, rsem):
        x, y, c, chips = _place()
        me_q = 2 * x + y
        res = []
        for a in range(n):
            rows = pl.ds(c * halves[a], halves[a])
            mine = src[a].at[:, rows, :]
            for r, chip in enumerate(chips):
                dev = (*chip, c)
                res.append((_remote(mine, out[a].at[me_q, :, rows, :], ssem, rsem, a * 3 + r, dev),
                            _remote(mine, out[a].at[2 * chip[0] + chip[1], :, rows, :], ssem, rsem, a * 3 + r, dev)))
        return res

    return _Hosted(list(shards), [jax.ShapeDtypeStruct((N_CHIPS,) + s.shape, s.dtype) for s in shards], 3 * n, copies)


def _pass_over_d2d(gathered):
    n = len(gathered)
    halves = [g.shape[2] // 2 for g in gathered]

    def copies(_, out, ssem, rsem):
        x, y, c, chips = _place()
        sib = (x, y, 1 - c)
        res = []
        for a in range(n):
            for r, chip in enumerate(chips):
                q = 2 * chip[0] + chip[1]
                mine = out[a].at[q, :, pl.ds(c * halves[a], halves[a]), :]
                theirs = out[a].at[q, :, pl.ds((1 - c) * halves[a], halves[a]), :]
                res.append((_remote(mine, mine, ssem, rsem, a * 3 + r, sib),
                            _remote(theirs, theirs, ssem, rsem, a * 3 + r, sib)))
        return res

    return _Hosted(list(gathered), [jax.ShapeDtypeStruct(g.shape, g.dtype) for g in gathered], 3 * n, copies,
                   in_place=True)


def _pass_to_sibling(gathered):
    n = len(gathered)
    halves = [g.shape[2] // 2 for g in gathered]

    def body(*refs):
        out = refs[n:2 * n]
        ssem, rsem = refs[2 * n:]
        x, y, c, chips = _place()
        sib = (x, y, 1 - c)

        def half(a, q, cc):
            return out[a].at[q, :, pl.ds(cc * halves[a], halves[a]), :]

        cps = []
        for a in range(n):
            for r, chip in enumerate(chips):
                q = 2 * chip[0] + chip[1]
                cps.append(_remote(half(a, q, c), half(a, q, c), ssem, rsem, a * 3 + r, sib))
        for cp in cps:
            cp.start()
        for a in range(n):
            for r, chip in enumerate(chips):
                q = 2 * chip[0] + chip[1]
                _remote(half(a, q, 1 - c), half(a, q, 1 - c), ssem, rsem, a * 3 + r, sib).wait_recv()
        for cp in cps:
            cp.wait_send()

    return pl.pallas_call(
        body, name="pass_to_sibling",
        in_specs=[HBM_SPEC] * n, out_specs=[HBM_SPEC] * n,
        out_shape=[jax.ShapeDtypeStruct(g.shape, g.dtype) for g in gathered],
        input_output_aliases={a: a for a in range(n)},
        scratch_shapes=[pltpu.SemaphoreType.DMA((3 * n,)), pltpu.SemaphoreType.DMA((3 * n,))],
        compiler_params=_params(has_side_effects=True),
    )(*gathered)


def _scatter_over_ici(parts):
    n = len(parts)

    def copies(src, out, ssem, rsem):
        x, y, c, chips = _place()
        res = []
        for a in range(n):
            for r, chip in enumerate(chips):
                cp = _remote(src[a].at[2 * chip[0] + chip[1]], out[a].at[r], ssem, rsem, a * 3 + r, (*chip, c))
                res.append((cp, cp))
        return res

    return _Hosted(list(parts), [jax.ShapeDtypeStruct((3,) + p.shape[1:], F32) for p in parts], 3 * n, copies)


def _swap_over_d2d(grads):
    n = len(grads)
    halves = [g.shape[1] // 2 for g in grads]

    def copies(src, out, ssem, rsem):
        x, y, c, _ = _place()
        res = []
        for a in range(n):
            cp = _remote(src[a].at[:, pl.ds((1 - c) * halves[a], halves[a]), :], out[a], ssem, rsem, a, (x, y, 1 - c))
            res.append((cp, cp))
        return res

    return _Hosted(list(grads), [jax.ShapeDtypeStruct((N_CHIPS, h, g.shape[2]), F32) for g, h in zip(grads, halves)],
                   n, copies)


def _swap_halves(grads):
    n = len(grads)
    halves = [g.shape[1] // 2 for g in grads]

    def body(*refs):
        src, out = refs[:n], refs[n:2 * n]
        ssem, rsem = refs[2 * n:]
        x, y, c, _ = _place()
        cps = [_remote(src[a].at[:, pl.ds((1 - c) * halves[a], halves[a]), :], out[a], ssem, rsem, a, (x, y, 1 - c))
               for a in range(n)]
        for cp in cps:
            cp.start()
        for cp in cps:
            cp.wait()

    return pl.pallas_call(
        body, name="swap_halves",
        in_specs=[HBM_SPEC] * n, out_specs=[HBM_SPEC] * n,
        out_shape=[jax.ShapeDtypeStruct((N_CHIPS, h, g.shape[2]), F32) for g, h in zip(grads, halves)],
        scratch_shapes=[pltpu.SemaphoreType.DMA((n,)), pltpu.SemaphoreType.DMA((n,))],
        compiler_params=_params(has_side_effects=True),
    )(*grads)


def _scatter_chips(parts):
    n = len(parts)

    def body(*refs):
        src, out = refs[:n], refs[n:2 * n]
        ssem, rsem = refs[2 * n:]
        x, y, c, chips = _place()
        cps = []
        for a in range(n):
            for r, chip in enumerate(chips):
                cps.append(_remote(src[a].at[2 * chip[0] + chip[1]], out[a].at[r], ssem, rsem, a * 3 + r, (*chip, c)))
        for cp in cps:
            cp.start()
        for cp in cps:
            cp.wait()

    return pl.pallas_call(
        body, name="scatter_chips",
        in_specs=[HBM_SPEC] * n, out_specs=[HBM_SPEC] * n,
        out_shape=[jax.ShapeDtypeStruct((3,) + p.shape[1:], F32) for p in parts],
        scratch_shapes=[pltpu.SemaphoreType.DMA((3 * n,)), pltpu.SemaphoreType.DMA((3 * n,))],
        compiler_params=_params(has_side_effects=True),
    )(*parts)


def _swap_reduced_over_d2d(reduced):
    n = len(reduced)

    def copies(src, out, ssem, rsem):
        x, y, c, _ = _place()
        res = []
        for a in range(n):
            cp = _remote(src[a], out[a], ssem, rsem, a, (x, y, 1 - c))
            res.append((cp, cp))
        return res

    return _Hosted(list(reduced), [jax.ShapeDtypeStruct(r.shape, F32) for r in reduced], n, copies)


def _swap_reduced(reduced):
    n = len(reduced)

    def body(*refs):
        src, out = refs[:n], refs[n:2 * n]
        ssem, rsem = refs[2 * n:]
        x, y, c, _ = _place()
        cps = [_remote(src[a], out[a], ssem, rsem, a, (x, y, 1 - c)) for a in range(n)]
        for cp in cps:
            cp.start()
        for cp in cps:
            cp.wait()

    return pl.pallas_call(
        body, name="swap_reduced",
        in_specs=[HBM_SPEC] * n, out_specs=[HBM_SPEC] * n,
        out_shape=[jax.ShapeDtypeStruct(r.shape, F32) for r in reduced],
        scratch_shapes=[pltpu.SemaphoreType.DMA((n,)), pltpu.SemaphoreType.DMA((n,))],
        compiler_params=_params(has_side_effects=True),
    )(*reduced)


def _allreduce_small(buf, hosted=None):
    R, L = buf.shape

    def body(*refs):
        (buf_ref, out_ref, pair_ref, chip_ref, ssem, rsem), start, wait = _host(hosted, refs, 1, 1, True, True)
        start()
        x, y, c, chips = _place()
        me_q = 2 * x + y
        pair_ref[c] = buf_ref[...]
        to_sib = _remote(buf_ref, pair_ref.at[c], ssem, rsem, 0, (x, y, 1 - c))
        to_sib.start()
        _remote(buf_ref, pair_ref.at[1 - c], ssem, rsem, 0, (x, y, 1 - c)).wait_recv()
        chip_ref[me_q] = pair_ref[0] + pair_ref[1]
        cps = [_remote(chip_ref.at[me_q], chip_ref.at[me_q], ssem, rsem, 1 + r, (*chip, c))
               for r, chip in enumerate(chips)]
        for cp in cps:
            cp.start()
        for r, chip in enumerate(chips):
            q = 2 * chip[0] + chip[1]
            _remote(chip_ref.at[q], chip_ref.at[q], ssem, rsem, 1 + r, (*chip, c)).wait_recv()
        out_ref[...] = (chip_ref[0] + chip_ref[1]) + (chip_ref[2] + chip_ref[3])
        to_sib.wait_send()
        for cp in cps:
            cp.wait_send()
        wait()

    h_ins = hosted.ins if hosted else []
    res = pl.pallas_call(
        body, name="allreduce_small",
        in_specs=[VMEM_SPEC] + [HBM_SPEC] * len(h_ins), out_specs=[VMEM_SPEC] + [HBM_SPEC] * len(h_ins),
        out_shape=[jax.ShapeDtypeStruct((R, L), F32)] + (hosted.out_shapes if hosted else []),
        scratch_shapes=[pltpu.VMEM((2, R, L), F32), pltpu.VMEM((N_CHIPS, R, L), F32),
                        pltpu.SemaphoreType.DMA((4,)), pltpu.SemaphoreType.DMA((4,))]
        + (hosted.sems() if hosted else []),
        compiler_params=_params(has_side_effects=True),
    )(buf, *h_ins)
    return res[0], res[1:]


def _pack(arrays):
    flat = jnp.concatenate([a.reshape(-1) for a in arrays])
    pad = (-flat.shape[0]) % (8 * LANES)
    return jnp.pad(flat, (0, pad)).reshape(-1, LANES)


def _unpack(buf, like):
    flat = buf.reshape(-1)
    out, off = [], 0
    for a in like:
        out.append(flat[off:off + a.size].reshape(a.shape))
        off += a.size
    return out


def _block_diag(pw):
    rows = []
    for gi in range(len(POOL_WINDOWS)):
        blocks = [pw[gi] if gj == gi else jnp.zeros_like(pw[gi]) for gj in range(len(POOL_WINDOWS))]
        rows.append(jnp.concatenate(blocks, axis=1))
    return jnp.concatenate(rows, axis=0)


def kernel(x, norm1, w_in, pool_w, pool_scale, sg_norm, sg_w, sg_b, w_out, norm2, w_up, w_down, final_norm, loss_target, m_norm1, m_w_in, m_pool_w, m_pool_scale, m_sg_norm, m_sg_w, m_sg_b, m_w_out, m_norm2, m_w_up, m_w_down, m_final_norm, v_norm1, v_w_in, v_pool_w, v_pool_scale, v_sg_norm, v_sg_w, v_sg_b, v_w_out, v_norm2, v_w_up, v_w_down, v_final_norm):
    depth = norm1.shape[0]
    T = x.shape[1]
    xs = x.reshape(T, D_MODEL)
    target = loss_target.reshape(T, D_MODEL)

    assert depth == 2
    c_idx = lax.axis_index("c")
    q_idx = 2 * lax.axis_index("x") + lax.axis_index("y")
    own = [w.astype(BF16) for w in (w_in, w_out, w_up, w_down)]
    gathered = {(0, 0): _gather_weights([own[0][:1]])[0]}

    def full(a, l, axis):
        blocks = lax.dynamic_update_slice(gathered[(a, l)], own[a][l][None, None], (q_idx, 0, 0, 0))[:, 0]
        if axis is None:
            return blocks
        if axis == 0:
            return blocks.reshape(-1, blocks.shape[-1])
        return jnp.concatenate([blocks[q] for q in range(N_CHIPS)], axis=axis)

    half_way = {}

    def gather_behind(call, keys, at_once):
        res, over_ici = call(_gather_over_ici([own[a][l:l + 1] for a, l in keys]))
        gathered.update(zip(keys[:at_once], _pass_to_sibling(over_ici[:at_once])))
        half_way.update(zip(keys[at_once:], over_ici[at_once:]))
        return res

    def pass_behind(call, keys):
        res, done = call(_pass_over_d2d([half_way.pop(k) for k in keys]))
        gathered.update(zip(keys, done))
        return res

    tril = jnp.tril(jnp.ones((CHUNK, CHUNK), F32))
    saved = []
    cur = xs
    wi, wo, wu, wd = {}, {}, {}, {}
    for l in range(depth):
        wbd = _block_diag(pool_w[l]).astype(BF16)
        wm = sg_w[l] * tril
        wm_s = wm.reshape(SG_HEADS * CHUNK, CHUNK).astype(BF16)
        wmt_s = jnp.swapaxes(wm, 1, 2).reshape(SG_HEADS * CHUNK, CHUNK).astype(BF16)
        bias = jnp.repeat(sg_b[l].T, SB_HD, axis=1)
        n1, n2 = norm1[l][None], norm2[l][None]
        psc, sgn = pool_scale[l][None], sg_norm[l][None]
        wi[l] = full(0, l, 1)
        proj, h, qkv = _inproj_fwd(cur, n1, wi[l])
        ya = _pool_fwd(proj, wbd, psc)
        yb = _sg_fwd(proj, wm_s, bias, sgn)
        if l == 0:
            yc = gather_behind(lambda hosted: _attn_fwd(qkv, hosted), [(1, 0), (2, 0), (3, 0)], 1)
            wo[l] = full(1, l, 0)
            x1, ymix = pass_behind(lambda hosted: _outproj_fwd(cur, ya, yb, yc, wo[l], hosted), [(2, 0), (3, 0)])
        else:
            yc = pass_behind(lambda hosted: _attn_fwd(qkv, hosted), [(1, l), (2, l), (3, l)])
            wo[l] = full(1, l, 0)
            (x1, ymix), _ = _outproj_fwd(cur, ya, yb, yc, wo[l])
        wu[l], wd[l] = full(2, l, None), full(3, l, 0)
        if l == 0:
            x2, h2, u, act = gather_behind(lambda hosted: _mlp_fwd(x1, n2, wu[l], wd[l], hosted),
                                           [(0, 1), (1, 1), (2, 1), (3, 1)], 1)
        else:
            (x2, h2, u, act), _ = _mlp_fwd(x1, n2, wu[l], wd[l])
        saved.append(dict(x0=cur, x1=x1, proj=proj, h=h, qkv=qkv, yc=yc, ymix=ymix, h2=h2, u=u, act=act,
                          wbd=wbd, wm_s=wm_s, wmt_s=wmt_s, bias=bias, n1=n1, n2=n2, psc=psc, sgn=sgn))
        cur = x2

    loss_row, dcur, d_final = _loss_head(cur, final_norm[None], target)

    small = [None] * depth
    grads, parts, reduced = {}, {}, {}

    def pair_up(keys, swapped):
        parts.update(zip(keys, _add_pairs([grads[k] for k in keys], swapped, c_idx)))

    def chip_up(keys, arrived):
        reduced.update(zip(keys, _add_chips([parts[k] for k in keys], arrived, q_idx)))

    for l in reversed(range(depth)):
        s = saved[l]
        if l == 0:
            keys = [(2, 1), (3, 1)]
            (dx1, du, d_n2), arrived = _mlp_bwd(dcur, s["x1"], s["n2"], s["u"], wu[l], wd[l],
                                                _scatter_over_ici([parts[k] for k in keys]))
            chip_up(keys, arrived)
        else:
            (dx1, du, d_n2), _ = _mlp_bwd(dcur, s["x1"], s["n2"], s["u"], wu[l], wd[l])
        grads[(2, l)] = _tn_matmul(s["h2"], du, "grad_w_up", n_split=N_CHIPS)
        grads[(3, l)] = _tn_matmul(s["act"], dcur, "grad_w_down")[0].reshape(N_CHIPS, D_FF // N_CHIPS, D_MODEL)
        dymix = _nt_matmul(dx1, wo[l])
        grads[(1, l)] = _tn_matmul(s["ymix"], dx1, "grad_w_out")[0].reshape(N_CHIPS, D_MODEL // N_CHIPS, D_MODEL)
        da_in, d_wbd, d_psc = _pool_bwd(s["proj"], dymix, s["wbd"], s["psc"])
        if l == 0:
            keys = [(0, 1), (1, 0), (2, 0), (3, 0)]
            (du_pre, dv_pre, d_wm, d_bias, d_sgn), swapped = _sg_bwd(
                s["proj"], dymix, s["wm_s"], s["wmt_s"], s["bias"], s["sgn"], _swap_over_d2d([grads[k] for k in keys]))
            pair_up(keys, swapped)
            keys = [(1, 1)] + keys
            (dq, dk, dv), arrived = _attn_bwd(s["qkv"], s["yc"], dymix, _scatter_over_ici([parts[k] for k in keys]))
            chip_up(keys, arrived)
        else:
            (du_pre, dv_pre, d_wm, d_bias, d_sgn), _ = _sg_bwd(s["proj"], dymix, s["wm_s"], s["wmt_s"], s["bias"], s["sgn"])
            keys = [(1, l), (2, l), (3, l)]
            (dq, dk, dv), swapped = _attn_bwd(s["qkv"], s["yc"], dymix, _swap_over_d2d([grads[k] for k in keys]))
            pair_up(keys, swapped)
        pieces = [da_in, du_pre, dv_pre, dq, dk, dv]
        if l == 0:
            keys = sorted(reduced)
            g_in_l, swapped = _inproj_grad(s["h"], pieces, _swap_reduced_over_d2d([reduced[k] for k in keys]))
            theirs = dict(zip(keys, swapped))
        else:
            g_in_l, _ = _inproj_grad(s["h"], pieces)
        grads[(0, l)] = g_in_l[0].reshape(D_MODEL, N_CHIPS, IN_COLS // N_CHIPS).transpose(1, 0, 2)
        if l == 0:
            keys = [(0, 0)]
            pair_up(keys, _swap_halves([grads[k] for k in keys]))
            (dx0, d_n1), arrived = _inproj_bwd(pieces, wi[l], s["x0"], s["n1"], dx1,
                                               _scatter_over_ici([parts[k] for k in keys]))
            chip_up(keys, arrived)
        else:
            (dx0, d_n1), _ = _inproj_bwd(pieces, wi[l], s["x0"], s["n1"], dx1)
        d_pw = jnp.stack([d_wbd[gi * POOL_GW:(gi + 1) * POOL_GW, gi * POOL_GW:(gi + 1) * POOL_GW]
                          for gi in range(len(POOL_WINDOWS))])
        small[l] = dict(norm1=d_n1[0], pool_w=d_pw, pool_scale=d_psc[0], sg_norm=d_sgn[0],
                        sg_w=d_wm.reshape(SG_HEADS, CHUNK, CHUNK), sg_b=d_bias[:, :SG_HEADS].T, norm2=d_n2[0])
        dcur = dx0
    grad_x = dcur.reshape(x.shape)

    names =["norm1", "pool_w", "pool_scale", "sg_norm", "sg_w", "sg_b", "norm2"]
    slot = jnp.zeros((1,), F32)
    small_w = [norm1, pool_w, pool_scale, sg_norm, sg_w, sg_b, norm2, final_norm, slot]
    small_m = [m_norm1, m_pool_w, m_pool_scale, m_sg_norm, m_sg_w, m_sg_b, m_norm2, m_final_norm, slot]
    small_v = [v_norm1, v_pool_w, v_pool_scale, v_sg_norm, v_sg_w, v_sg_b, v_norm2, v_final_norm, slot]
    small_g = [jnp.stack([small[l][k] for l in range(depth)]) for k in names] + [d_final[0], loss_row[0, :1]]
    keys = [(0, 0)]
    g_packed, _ = _allreduce_small(_pack(small_g))
    theirs.update(zip(keys, _swap_reduced([reduced[k] for k in keys])))

    def joined(a):
        layers = []
        for l in range(depth):
            mine, other = reduced[(a, l)], theirs[(a, l)]
            layers.append(jnp.where(c_idx == 0, jnp.concatenate([mine, other]), jnp.concatenate([other, mine])))
        return jnp.stack(layers)

    gw_in, gw_out, gw_up, gw_down = [joined(a) for a in range(4)]

    loss = _unpack(g_packed, small_w)[-1][0]
    s_delta, s_m, s_v = _elementwise(_adamw, "adamw_small", [_pack(small_w), g_packed, _pack(small_m), _pack(small_v)], 3)
    gs = dict(zip(names + ["final_norm"], _unpack(g_packed, small_w)))
    ds = dict(zip(names + ["final_norm"], _unpack(s_delta, small_w)))
    ms = dict(zip(names + ["final_norm"], _unpack(s_m, small_w)))
    vs = dict(zip(names + ["final_norm"], _unpack(s_v, small_w)))

    big_g = dict(w_in=gw_in, w_out=gw_out, w_up=gw_up, w_down=gw_down)
    big_w = dict(w_in=(w_in, m_w_in, v_w_in), w_out=(w_out, m_w_out, v_w_out),
                 w_up=(w_up, m_w_up, v_w_up), w_down=(w_down, m_w_down, v_w_down))
    for k, (w, m, v) in big_w.items():
        operands = [w, big_g[k], m, v]
        if k == "w_in":
            operands = [jnp.swapaxes(o, 1, 2) for o in operands]
        ds[k], ms[k], vs[k] = _elementwise(_adamw, "adamw_" + k, operands, 3)
        if k == "w_in":
            ds[k], ms[k], vs[k] = [jnp.swapaxes(o, 1, 2) for o in (ds[k], ms[k], vs[k])]
        gs[k] = big_g[k]

    order = ["norm1", "w_in", "pool_w", "pool_scale", "sg_norm", "sg_w", "sg_b", "w_out", "norm2", "w_up", "w_down",
             "final_norm"]
    return (loss, grad_x, *[gs[k] for k in order], *[ds[k] for k in order], *[ms[k] for k in order],
            *[vs[k] for k in order])
```

```python
import functools

import jax
import jax.numpy as jnp
from jax import lax
from jax.experimental import pallas as pl
from jax.experimental.pallas import tpu as pltpu

F32 = jnp.float32
BF16 = jnp.bfloat16
MESH = pl.DeviceIdType.MESH
AXES = ("x", "y", "c")

EPS = 1e-6
D_MODEL = 1024
POOL_WIDTH = 256
SG_WIDTH = 256
SB_WIDTH = 512
POOL_WINDOWS = (2, 4, 8, 16)
POOL_GW = 64
POOL_HALO = 16
CHUNK = 128
SG_HEADS = 4
SB_HD = 64
SB_SCALE = 0.125
IN_COLS = 2304
QKV_OFF = 768
D_FF = 4096
N_CHIPS = 4
LANES = 128
VMEM_LIMIT = 56 * 1024 * 1024
MLP_CHUNK = 512
ATTN_TILE = 256
UNDERFLOW = -104.0

ADAM_LR = 0.001
ADAM_B1 = 0.9
ADAM_B2 = 0.999
ADAM_EPS = 1e-08
ADAM_WD = 0.01
ADAM_STEP = 10

HBM_SPEC = pl.BlockSpec(memory_space=pl.ANY)
VMEM_SPEC = pl.BlockSpec(memory_space=pltpu.VMEM)


def _params(**kw):
    return pltpu.CompilerParams(vmem_limit_bytes=VMEM_LIMIT, **kw)


def _tile(n, pref):
    if n <= pref:
        return n
    for t in range(pref - pref % LANES, 0, -LANES):
        if n % t == 0:
            return t
    raise ValueError((n, pref))


def _nn(a, b):
    return jnp.dot(a, b, preferred_element_type=F32)


def _nt(a, b):
    return lax.dot_general(a, b, (((1,), (1,)), ((), ())), preferred_element_type=F32)


def _tn(a, b):
    return lax.dot_general(a, b, (((0,), (0,)), ((), ())), preferred_element_type=F32)


def _rms_fwd(x, g):
    r = lax.rsqrt(jnp.mean(x * x, axis=-1, keepdims=True) + EPS)
    xhat = x * r
    return xhat * g, xhat, r


def _rms_bwd(dy, xhat, r, g):
    dxhat = dy * g
    dx = r * (dxhat - xhat * jnp.mean(dxhat * xhat, axis=-1, keepdims=True))
    return dx, dy * xhat


_GELU_K = 0.7978845608028654
_GELU_C = 0.044715


def _gelu(x):
    return 0.5 * x * (1.0 + jnp.tanh(_GELU_K * (x + _GELU_C * x * x * x)))


def _gelu_grad(x):
    t = jnp.tanh(_GELU_K * (x + _GELU_C * x * x * x))
    return 0.5 * (1.0 + t) + 0.5 * x * (1.0 - t * t) * _GELU_K * (1.0 + 3.0 * _GELU_C * x * x)


def _inproj_fwd(x, g, w):
    T, D = x.shape
    N = w.shape[1]
    tt = _tile(T, 512)

    def body(x_ref, g_ref, w_ref, proj_ref, h_ref, qkv_ref):
        h, _, _ = _rms_fwd(x_ref[...], g_ref[...])
        hb = h.astype(BF16)
        h_ref[...] = hb
        p = _nn(hb, w_ref[...])
        proj_ref[...] = p[:, :QKV_OFF]
        qkv_ref[...] = p[:, QKV_OFF:].astype(BF16)

    return pl.pallas_call(
        body, name="inproj_fwd", grid=(T // tt,),
        in_specs=[pl.BlockSpec((tt, D), lambda i: (i, 0)), pl.BlockSpec((1, D), lambda i: (0, 0)),
                  pl.BlockSpec((D, N), lambda i: (0, 0))],
        out_specs=[pl.BlockSpec((tt, QKV_OFF), lambda i: (i, 0)), pl.BlockSpec((tt, D), lambda i: (i, 0)),
                   pl.BlockSpec((tt, N - QKV_OFF), lambda i: (i, 0))],
        out_shape=[jax.ShapeDtypeStruct((T, QKV_OFF), F32), jax.ShapeDtypeStruct((T, D), BF16),
                   jax.ShapeDtypeStruct((T, N - QKV_OFF), BF16)],
        compiler_params=_params(),
    )(x, g, w)


def _inproj_bwd(pieces, w, x, g, dres, hosted=None):
    T, D = x.shape
    N = w.shape[1]
    tt = _tile(T, 512)
    nt = T // tt
    widths = [p.shape[1] for p in pieces]
    offs = [sum(widths[:k]) for k in range(len(widths))]
    assert sum(widths) == N
    n_p = len(pieces)

    def body(*refs):
        i = pl.program_id(0)
        own, start, wait = _host(hosted, refs, n_p + 4, 2, i == 0, i == nt - 1)
        start()
        p_refs = own[:n_p]
        w_ref, x_ref, g_ref, dres_ref, dx_ref, dg_ref, dproj_ref = own[n_p:]
        for p_ref, o, wd in zip(p_refs, offs, widths):
            dproj_ref[:, o:o + wd] = p_ref[...].astype(BF16)
        dh = _nt(dproj_ref[...], w_ref[...])
        gv = g_ref[...]
        _, xhat, r = _rms_fwd(x_ref[...], gv)
        dx, dgrow = _rms_bwd(dh, xhat, r, gv)
        dx_ref[...] = dres_ref[...] + dx

        @pl.when(i == 0)
        def _():
            dg_ref[...] = jnp.zeros_like(dg_ref)

        dg_ref[...] += jnp.sum(dgrow, axis=0, keepdims=True)
        wait()

    h_ins = hosted.ins if hosted else []
    res = pl.pallas_call(
        body, name="inproj_bwd_hosting" if hosted else "inproj_bwd", grid=(nt,),
        in_specs=[pl.BlockSpec((tt, wd), lambda i: (i, 0)) for wd in widths] + [
            pl.BlockSpec((D, N), lambda i: (0, 0)), pl.BlockSpec((tt, D), lambda i: (i, 0)),
            pl.BlockSpec((1, D), lambda i: (0, 0)), pl.BlockSpec((tt, D), lambda i: (i, 0))] + [HBM_SPEC] * len(h_ins),
        out_specs=[pl.BlockSpec((tt, D), lambda i: (i, 0)), pl.BlockSpec((1, D), lambda i: (0, 0))]
        + [HBM_SPEC] * len(h_ins),
        out_shape=[jax.ShapeDtypeStruct((T, D), F32), jax.ShapeDtypeStruct((1, D), F32)]
        + (hosted.out_shapes if hosted else []),
        scratch_shapes=[pltpu.VMEM((tt, N), BF16)] + (hosted.sems() if hosted else []),
        compiler_params=_params(has_side_effects=hosted is not None),
    )(*pieces, w, x, g, dres, *h_ins)
    return res[:2], res[2:]


def _inproj_grad(h, pieces, hosted=None):
    T, D = h.shape
    tt = _tile(T, 1024)
    nt = T // tt
    widths = [p.shape[1] for p in pieces]
    offs = [sum(widths[:k]) for k in range(len(widths))]
    N = sum(widths)
    n_p = len(pieces)

    def body(*refs):
        t = pl.program_id(0)
        own, start, wait = _host(hosted, refs, n_p + 1, 1, t == 0, t == nt - 1)
        start()
        h_ref, p_refs, o_ref = own[0], own[1:1 + n_p], own[1 + n_p]

        @pl.when(t == 0)
        def _():
            o_ref[...] = jnp.zeros_like(o_ref)

        hv = h_ref[...]
        for p_ref, o, wd in zip(p_refs, offs, widths):
            o_ref[:, o:o + wd] += _tn(hv, p_ref[...].astype(BF16))
        wait()

    h_ins = hosted.ins if hosted else []
    res = pl.pallas_call(
        body, name="grad_w_in_hosting" if hosted else "grad_w_in", grid=(nt,),
        in_specs=[pl.BlockSpec((tt, D), lambda t: (t, 0))] + [pl.BlockSpec((tt, wd), lambda t: (t, 0)) for wd in widths]
        + [HBM_SPEC] * len(h_ins),
        out_specs=[pl.BlockSpec((None, D, N), lambda t: (0, 0, 0))] + [HBM_SPEC] * len(h_ins),
        out_shape=[jax.ShapeDtypeStruct((1, D, N), F32)] + (hosted.out_shapes if hosted else []),
        scratch_shapes=hosted.sems() if hosted else [],
        compiler_params=_params(has_side_effects=hosted is not None),
    )(h, *pieces, *h_ins)
    return res[0], res[1:]


def _pool_select(s2, s4, s8, s16, grp):
    return jnp.where(grp == 0, s2, jnp.where(grp == 1, s4, jnp.where(grp == 2, s8, s16)))


def _pool_count(t_glob, grp):
    win = jnp.where(grp == 0, 2, jnp.where(grp == 1, 4, jnp.where(grp == 2, 8, 16)))
    return jnp.minimum(t_glob + 1, win).astype(F32)


def _pool_diff(a, halo, base, tt):
    n = tt + POOL_HALO
    ext = jnp.concatenate([halo, a], axis=0)
    s2 = ext + pltpu.roll(ext, 1, 0)
    s4 = s2 + pltpu.roll(s2, 2, 0)
    s8 = s4 + pltpu.roll(s4, 4, 0)
    s16 = s8 + pltpu.roll(s8, 8, 0)
    grp = lax.broadcasted_iota(jnp.int32, (n, POOL_WIDTH), 1) // POOL_GW
    t_glob = lax.broadcasted_iota(jnp.int32, (n, POOL_WIDTH), 0) + (base - POOL_HALO)
    pooled = _pool_select(s2, s4, s8, s16, grp) / _pool_count(t_glob, grp)
    return pooled[POOL_HALO:] - a


def _pool_specs(T, tt):
    hb = tt // POOL_HALO
    return [pl.BlockSpec((tt, POOL_WIDTH), lambda i: (i, 0)),
            pl.BlockSpec((POOL_HALO, POOL_WIDTH), lambda i: (jnp.maximum(i * hb - 1, 0), 0))]


def _pool_fwd(proj, wbd, scale):
    T = proj.shape[0]
    tt = _tile(T, 512)

    def body(a_ref, halo_ref, w_ref, sc_ref, y_ref):
        i = pl.program_id(0)
        halo = jnp.where(i > 0, halo_ref[...], 0.0)
        d = _pool_diff(a_ref[...], halo, i * tt, tt)
        y_ref[...] = _nn(d.astype(BF16), w_ref[...]) * sc_ref[...]

    return pl.pallas_call(
        body, name="pool_fwd", grid=(T // tt,),
        in_specs=_pool_specs(T, tt) + [pl.BlockSpec((POOL_WIDTH, POOL_WIDTH), lambda i: (0, 0)),
                                       pl.BlockSpec((1, POOL_WIDTH), lambda i: (0, 0))],
        out_specs=pl.BlockSpec((tt, POOL_WIDTH), lambda i: (i, 0)),
        out_shape=jax.ShapeDtypeStruct((T, POOL_WIDTH), F32),
        compiler_params=_params(),
    )(proj, proj, wbd, scale)


def _pool_bwd(proj, dymix, wbd, scale):
    T = proj.shape[0]
    tt = _tile(T, 512)
    hb = tt // POOL_HALO
    nblk = T // tt
    n = tt + POOL_HALO

    def body(a_ref, halo_ref, dy_ref, dyn_ref, w_ref, sc_ref, da_ref, dw_ref, dsc_ref):
        i = pl.program_id(0)
        halo = jnp.where(i > 0, halo_ref[...], 0.0)
        d = _pool_diff(a_ref[...], halo, i * tt, tt)
        db = d.astype(BF16)
        wv = w_ref[...]
        sc = sc_ref[...]
        dy = dy_ref[...]
        dys = dy * sc

        @pl.when(i == 0)
        def _():
            dw_ref[...] = jnp.zeros_like(dw_ref)
            dsc_ref[...] = jnp.zeros_like(dsc_ref)

        dsc_ref[...] += jnp.sum(dy * _nn(db, wv), axis=0, keepdims=True)
        dw_ref[...] += _tn(db, dys.astype(BF16))
        dyn = jnp.where(i < nblk - 1, dyn_ref[...], 0.0) * sc
        dd = _nt(jnp.concatenate([dys, dyn], axis=0).astype(BF16), wv)
        grp = lax.broadcasted_iota(jnp.int32, (n, POOL_WIDTH), 1) // POOL_GW
        t_glob = lax.broadcasted_iota(jnp.int32, (n, POOL_WIDTH), 0) + i * tt
        e = dd / _pool_count(t_glob, grp)
        r2 = e + pltpu.roll(e, n - 1, 0)
        r4 = r2 + pltpu.roll(r2, n - 2, 0)
        r8 = r4 + pltpu.roll(r4, n - 4, 0)
        r16 = r8 + pltpu.roll(r8, n - 8, 0)
        da_ref[...] = (_pool_select(r2, r4, r8, r16, grp) - dd)[:tt].astype(BF16)

    return pl.pallas_call(
        body, name="pool_bwd", grid=(nblk,),
        in_specs=_pool_specs(T, tt) + [
            pl.BlockSpec((tt, POOL_WIDTH), lambda i: (i, 0)),
            pl.BlockSpec((POOL_HALO, POOL_WIDTH), lambda i: (jnp.minimum((i + 1) * hb, T // POOL_HALO - 1), 0)),
            pl.BlockSpec((POOL_WIDTH, POOL_WIDTH), lambda i: (0, 0)), pl.BlockSpec((1, POOL_WIDTH), lambda i: (0, 0))],
        out_specs=[pl.BlockSpec((tt, POOL_WIDTH), lambda i: (i, 0)),
                   pl.BlockSpec((POOL_WIDTH, POOL_WIDTH), lambda i: (0, 0)),
                   pl.BlockSpec((1, POOL_WIDTH), lambda i: (0, 0))],
        out_shape=[jax.ShapeDtypeStruct((T, POOL_WIDTH), BF16),
                   jax.ShapeDtypeStruct((POOL_WIDTH, POOL_WIDTH), F32),
                   jax.ShapeDtypeStruct((1, POOL_WIDTH), F32)],
        compiler_params=_params(),
    )(proj, proj, dymix, dymix, wbd, scale)


def _head_select(stacked, grp):
    out = jnp.where(grp == 0, stacked[0:CHUNK], 0.0)
    for h in range(1, SG_HEADS):
        out = out + jnp.where(grp == h, stacked[h * CHUNK:(h + 1) * CHUNK], 0.0)
    return out


def _sg_specs(tt):
    return [pl.BlockSpec((tt, SG_WIDTH), lambda i: (i, 1)), pl.BlockSpec((tt, SG_WIDTH), lambda i: (i, 2))]


def _sg_fwd(proj, wm, bias, g):
    T = proj.shape[0]
    tt = _tile(T, 512)

    def body(u_ref, v_ref, wm_ref, b_ref, g_ref, y_ref):
        zu = _gelu(u_ref[...])
        vn, _, _ = _rms_fwd(_gelu(v_ref[...]), g_ref[...])
        grp = lax.broadcasted_iota(jnp.int32, (CHUNK, SG_WIDTH), 1) // SB_HD
        for n in range(tt // CHUNK):
            rows = slice(n * CHUNK, (n + 1) * CHUNK)
            sv = _head_select(_nn(wm_ref[...], vn[rows].astype(BF16)), grp) + b_ref[...]
            y_ref[rows, :] = zu[rows] * sv

    return pl.pallas_call(
        body, name="sg_fwd", grid=(T // tt,),
        in_specs=_sg_specs(tt) + [pl.BlockSpec((SG_HEADS * CHUNK, CHUNK), lambda i: (0, 0)),
                                  pl.BlockSpec((CHUNK, SG_WIDTH), lambda i: (0, 0)),
                                  pl.BlockSpec((1, SG_WIDTH), lambda i: (0, 0))],
        out_specs=pl.BlockSpec((tt, SG_WIDTH), lambda i: (i, 0)),
        out_shape=jax.ShapeDtypeStruct((T, SG_WIDTH), F32),
        compiler_params=_params(),
    )(proj, proj, wm, bias, g)


def _sg_bwd(proj, dymix, wm, wmt, bias, g, hosted=None):
    T = proj.shape[0]
    tt = _tile(T, 512)
    nblk = T // tt

    def body(*refs):
        i = pl.program_id(0)
        (u_ref, v_ref, dy_ref, wm_ref, wmt_ref, b_ref, g_ref, du_ref, dv_ref, dw_ref, db_ref, dg_ref,
         dvn_ref, dbias_ref), start, wait = _host(hosted, refs, 7, 5, i == 0, i == nblk - 1)
        start()
        up, vp = u_ref[...], v_ref[...]
        gv = g_ref[...]
        zu, zv = _gelu(up), _gelu(vp)
        vn, xhat, r = _rms_fwd(zv, gv)
        gu = _gelu_grad(up)
        grp = lax.broadcasted_iota(jnp.int32, (CHUNK, SG_WIDTH), 1) // SB_HD

        @pl.when(i == 0)
        def _():
            dw_ref[...] = jnp.zeros_like(dw_ref)
            dbias_ref[...] = jnp.zeros_like(dbias_ref)
            dg_ref[...] = jnp.zeros_like(dg_ref)

        for n in range(tt // CHUNK):
            rows = slice(n * CHUNK, (n + 1) * CHUNK)
            vc = vn[rows].astype(BF16)
            sv = _head_select(_nn(wm_ref[...], vc), grp) + b_ref[...]
            dy = dy_ref[rows, :]
            du_ref[rows, :] = (dy * sv * gu[rows]).astype(BF16)
            dsv = dy * zu[rows]
            dsvb = dsv.astype(BF16)
            dvn_ref[rows, :] = _head_select(_nn(wmt_ref[...], dsvb), grp)
            stacked = jnp.concatenate([jnp.where(grp == h, dsv, 0.0) for h in range(SG_HEADS)], axis=0)
            dw_ref[...] += _nt(stacked.astype(BF16), vc)
            dbias_ref[...] += dsv

        dzv, dgrow = _rms_bwd(dvn_ref[...], xhat, r, gv)
        dg_ref[...] += jnp.sum(dgrow, axis=0, keepdims=True)
        dv_ref[...] = (dzv * _gelu_grad(vp)).astype(BF16)

        @pl.when(i == nblk - 1)
        def _():
            t_i = lax.broadcasted_iota(jnp.int32, (SG_HEADS * CHUNK, CHUNK), 0) % CHUNK
            s_i = lax.broadcasted_iota(jnp.int32, (SG_HEADS * CHUNK, CHUNK), 1)
            dw_ref[...] = jnp.where(s_i <= t_i, dw_ref[...], 0.0)
            lane = lax.broadcasted_iota(jnp.int32, (CHUNK, LANES), 1)
            acc = jnp.zeros((CHUNK, LANES), F32)
            for h in range(SG_HEADS):
                tot = jnp.sum(jnp.where(grp == h, dbias_ref[...], 0.0), axis=1, keepdims=True)
                acc = acc + jnp.where(lane == h, tot, 0.0)
            db_ref[...] = acc

        wait()

    h_ins = hosted.ins if hosted else []
    res = pl.pallas_call(
        body, name="sg_bwd_hosting" if hosted else "sg_bwd", grid=(nblk,),
        in_specs=_sg_specs(tt) + [pl.BlockSpec((tt, SG_WIDTH), lambda i: (i, 1)),
                                  pl.BlockSpec((SG_HEADS * CHUNK, CHUNK), lambda i: (0, 0)),
                                  pl.BlockSpec((SG_HEADS * CHUNK, CHUNK), lambda i: (0, 0)),
                                  pl.BlockSpec((CHUNK, SG_WIDTH), lambda i: (0, 0)),
                                  pl.BlockSpec((1, SG_WIDTH), lambda i: (0, 0))] + [HBM_SPEC] * len(h_ins),
        out_specs=[pl.BlockSpec((tt, SG_WIDTH), lambda i: (i, 0)), pl.BlockSpec((tt, SG_WIDTH), lambda i: (i, 0)),
                   pl.BlockSpec((SG_HEADS * CHUNK, CHUNK), lambda i: (0, 0)),
                   pl.BlockSpec((CHUNK, LANES), lambda i: (0, 0)), pl.BlockSpec((1, SG_WIDTH), lambda i: (0, 0))]
        + [HBM_SPEC] * len(h_ins),
        out_shape=[jax.ShapeDtypeStruct((T, SG_WIDTH), BF16), jax.ShapeDtypeStruct((T, SG_WIDTH), BF16),
                   jax.ShapeDtypeStruct((SG_HEADS * CHUNK, CHUNK), F32),
                   jax.ShapeDtypeStruct((CHUNK, LANES), F32), jax.ShapeDtypeStruct((1, SG_WIDTH), F32)]
        + (hosted.out_shapes if hosted else []),
        scratch_shapes=[pltpu.VMEM((tt, SG_WIDTH), F32), pltpu.VMEM((CHUNK, SG_WIDTH), F32)]
        + (hosted.sems() if hosted else []),
        compiler_params=_params(has_side_effects=hosted is not None),
    )(proj, proj, dymix, wm, wmt, bias, g, *h_ins)
    return res[:5], res[5:]


def _split_dot(x, u):
    hi = x.astype(BF16)
    lo = (x - hi.astype(F32)).astype(BF16)
    return _nn(hi, u) + _nn(lo, u)


def _sb_logits(z):
    lb = jnp.minimum(z, 0.0) - jnp.log(1.0 + jnp.exp(-jnp.abs(z)))
    return lb, lb - z


ATTN_STRIP = 256
ATTN_SUBS = 2


def _by_strips(n_rows, fn):
    parts = None
    for r in range(0, n_rows, ATTN_STRIP):
        res = fn(slice(r, r + ATTN_STRIP))
        parts = [[v] for v in res] if parts is None else [p + [v] for p, v in zip(parts, res)]
    return [jnp.concatenate(p, axis=0) for p in parts]


def _attn_qkv_specs(tq, T):
    base = (IN_COLS - 3 * SB_WIDTH - QKV_OFF) // LANES
    nb = SB_WIDTH // LANES
    return [pl.BlockSpec((tq, LANES), lambda p, i: (i, base + p)),
            pl.BlockSpec((T, LANES), lambda p, i: (0, base + nb + p)),
            pl.BlockSpec((T, LANES), lambda p, i: (0, base + 2 * nb + p))]


class _Hosted:
    def __init__(self, ins, out_shapes, n_sems, copies, in_place=False):
        self.ins, self.out_shapes, self.n_sems, self.copies = ins, out_shapes, n_sems, copies
        self.in_place = in_place

    @property
    def n(self):
        return len(self.ins)

    def aliases(self, n_in, n_out):
        return {n_in + k: n_out + k for k in range(self.n)} if self.in_place else {}

    def sems(self):
        return [pltpu.SemaphoreType.DMA((self.n_sems,)), pltpu.SemaphoreType.DMA((self.n_sems,))]

    def start(self, src, dst, ssem, rsem):
        for send, _ in self.copies(src, dst, ssem, rsem):
            send.start()

    def wait(self, src, dst, ssem, rsem):
        for send, recv in self.copies(src, dst, ssem, rsem):
            recv.wait_recv()
            send.wait_send()


def _host(hosted, refs, n_in, n_out, first, last):
    if hosted is None:
        return refs, lambda: None, lambda: None
    n = hosted.n
    own_in, h_in = refs[:n_in], refs[n_in:n_in + n]
    own_out, h_out = refs[n_in + n:n_in + n + n_out], refs[n_in + n + n_out:n_in + 2 * n + n_out]
    rest = refs[n_in + 2 * n + n_out:]
    ssem, rsem = rest[-2:]

    def start():
        if first is True:
            hosted.start(h_in, h_out, ssem, rsem)
        else:
            pl.when(first)(lambda: hosted.start(h_in, h_out, ssem, rsem))

    def wait():
        if last is True:
            hosted.wait(h_in, h_out, ssem, rsem)
        else:
            pl.when(last)(lambda: hosted.wait(h_in, h_out, ssem, rsem))

    return own_in + own_out + rest[:-2], start, wait


def _attn_fwd(qkv, hosted=None):
    T = qkv.shape[0]
    tk = _tile(T, ATTN_TILE)
    n_sub = ATTN_SUBS if T % (ATTN_SUBS * tk) == 0 else 1
    tq = n_sub * tk
    n_p, nq = SB_WIDTH // LANES, T // tq

    def body(*refs):
        p, i = pl.program_id(0), pl.program_id(1)
        (q_ref, k_ref, v_ref, o_ref), start, wait = _host(
            hosted, refs, 3, 1, jnp.logical_and(p == 0, i == 0), jnp.logical_and(p == n_p - 1, i == nq - 1))
        start()
        lane = lax.broadcasted_iota(jnp.int32, (tk, LANES), 1)
        row = lax.broadcasted_iota(jnp.int32, (tk, tk), 0)
        col = lax.broadcasted_iota(jnp.int32, (tk, tk), 1)
        after = jnp.where(row > col, 1.0, 0.0).astype(BF16)
        valid = col < row
        qh = {}
        for sb in range(n_sub):
            q = q_ref[sb * tk:(sb + 1) * tk, :].astype(F32)
            for hh in range(2):
                qh[(sb, hh)] = jnp.where((lane // SB_HD) == hh, q * SB_SCALE, 0.0).astype(BF16)

        def tiles(todo, state):
            chains = [(n, hh) for n in range(len(todo)) for hh in range(2)]
            kv = []
            for _, j, _ in todo:
                ks = pl.ds(pl.multiple_of(j * tk, tk), tk)
                kv.append((k_ref[ks, :], v_ref[ks, :]))
            z = {(n, hh): _nt(qh[(todo[n][0], hh)], kv[n][0]) for n, hh in chains}
            lb, lmb, lm_sum = {}, {}, {}
            for n, hh in chains:
                def logits(rows, z=z[(n, hh)], mask=todo[n][2]):
                    lb, lm = _sb_logits(z[rows])
                    if mask is not None:
                        lm = jnp.where(mask[rows], lm, 0.0)
                    return lb, lm.astype(BF16), jnp.sum(lm, axis=1, keepdims=True)

                lb[(n, hh)], lmb[(n, hh)], lm_sum[(n, hh)] = _by_strips(tk, logits)
            x = {c: _nn(lmb[c], after) for c in chains}
            new = dict(state)
            for n, hh in chains:
                key = (todo[n][0], hh)
                carry, acc = new[key]

                def weights(rows, lb=lb[(n, hh)], x=x[(n, hh)], carry=carry, mask=todo[n][2]):
                    a = jnp.exp(lb[rows] + x[rows] + carry[rows])
                    if mask is not None:
                        a = jnp.where(mask[rows], a, 0.0)
                    return (a.astype(BF16),)

                (ab,) = _by_strips(tk, weights)
                new[key] = (carry + lm_sum[(n, hh)], acc + _nn(ab, kv[n][1]))
            return new

        def live(state, sb):
            return jnp.maximum(jnp.max(state[(sb, 0)][0]), jnp.max(state[(sb, 1)][0]))

        first = n_sub * i
        zero = (jnp.zeros((tk, 1), F32), jnp.zeros((tk, LANES), F32))
        todo = []
        for sb in range(n_sub):
            gate = jnp.broadcast_to(first > 0, (tk, tk)) if sb == 0 else None
            todo += [(sb, first + sb, valid), (sb, jnp.maximum(first + sb - 1, 0), gate)]
        state = tiles(todo, {(sb, hh): zero for sb in range(n_sub) for hh in range(2)})
        for sb in range(n_sub):
            def cond(st):
                return jnp.logical_and(st[0] >= 0, st[2] > UNDERFLOW)

            def step(st, sb=sb):
                mine = tiles([(sb, st[0], None)], st[1])
                return st[0] - 1, mine, live(mine, sb)

            mine = {k: v for k, v in state.items() if k[0] == sb}
            _, mine, _ = lax.while_loop(cond, step, (first + sb - 2, mine, live(mine, sb)))
            o_ref[sb * tk:(sb + 1) * tk, :] = jnp.where(lane < SB_HD, mine[(sb, 0)][1], mine[(sb, 1)][1])
        wait()

    h_ins = hosted.ins if hosted else []
    res = pl.pallas_call(
        body, name="attn_fwd_hosting" if hosted else "attn_fwd", grid=(n_p, nq),
        in_specs=_attn_qkv_specs(tq, T) + [HBM_SPEC] * len(h_ins),
        out_specs=[pl.BlockSpec((tq, LANES), lambda p, i: (i, p))] + [HBM_SPEC] * len(h_ins),
        out_shape=[jax.ShapeDtypeStruct((T, SB_WIDTH), F32)] + (hosted.out_shapes if hosted else []),
        input_output_aliases=hosted.aliases(3, 1) if hosted else {},
        scratch_shapes=hosted.sems() if hosted else [],
        compiler_params=_params(has_side_effects=hosted is not None),
    )(qkv, qkv, qkv, *h_ins)
    return res[0], res[1:]


def _attn_bwd(qkv, o, dymix, hosted=None):
    T = qkv.shape[0]
    tk = _tile(T, ATTN_TILE)
    n_sub = ATTN_SUBS if T % (ATTN_SUBS * tk) == 0 else 1
    tq = n_sub * tk
    n_p, nq = SB_WIDTH // LANES, T // tq
    yc_blk = (POOL_WIDTH + SG_WIDTH) // LANES

    def body(*refs):
        p, i = pl.program_id(0), pl.program_id(1)
        (q_ref, k_ref, v_ref, o_ref, do_ref, dq_ref, dk_ref, dv_ref), start, wait = _host(
            hosted, refs, 5, 3, jnp.logical_and(p == 0, i == 0), jnp.logical_and(p == n_p - 1, i == nq - 1))
        start()
        lane = lax.broadcasted_iota(jnp.int32, (tk, LANES), 1)
        row = lax.broadcasted_iota(jnp.int32, (tk, tk), 0)
        col = lax.broadcasted_iota(jnp.int32, (tk, tk), 1)
        after = jnp.where(row > col, 1.0, 0.0).astype(BF16)
        from_here = jnp.where(row >= col, 1.0, 0.0).astype(BF16)
        from_here2 = jnp.concatenate([from_here, from_here], axis=0)
        valid = col < row

        @pl.when(i == 0)
        def _():
            dk_ref[...] = jnp.zeros_like(dk_ref)
            dv_ref[...] = jnp.zeros_like(dv_ref)

        qh, dohb, delta = {}, {}, {}
        for sb in range(n_sub):
            rows = slice(sb * tk, (sb + 1) * tk)
            q, ov, dov = q_ref[rows, :].astype(F32), o_ref[rows, :], do_ref[rows, :]
            for hh in range(2):
                head = (lane // SB_HD) == hh
                qh[(sb, hh)] = jnp.where(head, q * SB_SCALE, 0.0).astype(BF16)
                dohb[(sb, hh)] = jnp.where(head, dov, 0.0).astype(BF16)
                delta[(sb, hh)] = jnp.sum(dohb[(sb, hh)].astype(F32) * ov, axis=1, keepdims=True)

        def tiles(todo, state):
            chains = [(n, hh) for n in range(len(todo)) for hh in range(2)]
            kv, where = [], []
            for _, j, _ in todo:
                ks = pl.ds(pl.multiple_of(j * tk, tk), tk)
                where.append(ks)
                kv.append((k_ref[ks, :], v_ref[ks, :]))
            z = {(n, hh): _nt(qh[(todo[n][0], hh)], kv[n][0]) for n, hh in chains}
            da = {(n, hh): _nt(dohb[(todo[n][0], hh)], kv[n][1]) for n, hh in chains}
            lb, lmb, lm_sum = {}, {}, {}
            for n, hh in chains:
                def logits(rows, z=z[(n, hh)], mask=todo[n][2]):
                    lb, lm = _sb_logits(z[rows])
                    if mask is not None:
                        lm = jnp.where(mask[rows], lm, 0.0)
                    return lb, lm.astype(BF16), jnp.sum(lm, axis=1, keepdims=True)

                lb[(n, hh)], lmb[(n, hh)], lm_sum[(n, hh)] = _by_strips(tk, logits)
            x = {c: _nn(lmb[c], after) for c in chains}
            c_a = {k: v[0] for k, v in state.items()}
            ab, g, g_split, g_sum = {}, {}, {}, {}
            for n, hh in chains:
                key = (todo[n][0], hh)

                def weights(rows, lb=lb[(n, hh)], x=x[(n, hh)], da=da[(n, hh)], c_a=c_a[key], mask=todo[n][2]):
                    a = jnp.exp(lb[rows] + x[rows] + c_a[rows])
                    if mask is not None:
                        a = jnp.where(mask[rows], a, 0.0)
                    ab = a.astype(BF16)
                    g = da[rows] * ab.astype(F32)
                    hi = g.astype(BF16)
                    lo = (g - hi.astype(F32)).astype(BF16)
                    return ab, g, jnp.concatenate([hi, lo], axis=1), jnp.sum(g, axis=1, keepdims=True)

                ab[(n, hh)], g[(n, hh)], g_split[(n, hh)], g_sum[(n, hh)] = _by_strips(tk, weights)
                c_a[key] = c_a[key] + lm_sum[(n, hh)]
            right = {c: _nn(g_split[c], from_here2) for c in chains}
            c_r = {k: v[1] for k, v in state.items()}
            dzb = {}
            for n, hh in chains:
                key = (todo[n][0], hh)

                def logit_grads(rows, lb=lb[(n, hh)], g=g[(n, hh)], right=right[(n, hh)], c_r=c_r[key],
                                delta=delta[key], mask=todo[n][2]):
                    sig = jnp.exp(lb[rows])
                    left = delta[rows] - (c_r[rows] + right[rows])
                    dz = g[rows] * (1.0 - sig) - left * sig
                    if mask is not None:
                        dz = jnp.where(mask[rows], dz, 0.0)
                    return (dz.astype(BF16),)

                (dzb[(n, hh)],) = _by_strips(tk, logit_grads)
                c_r[key] = c_r[key] + g_sum[(n, hh)]
            dqa = {k: v[2] for k, v in state.items()}
            for n in range(len(todo)):
                sb = todo[n][0]
                dk_ref[where[n], :] += _tn(dzb[(n, 0)], qh[(sb, 0)]) + _tn(dzb[(n, 1)], qh[(sb, 1)])
                dv_ref[where[n], :] += _tn(ab[(n, 0)], dohb[(sb, 0)]) + _tn(ab[(n, 1)], dohb[(sb, 1)])
                for hh in range(2):
                    dqa[(sb, hh)] = dqa[(sb, hh)] + _nn(dzb[(n, hh)], kv[n][0])
            return {k: (c_a[k], c_r[k], dqa[k]) for k in state}

        def live(state, sb):
            return jnp.maximum(jnp.max(state[(sb, 0)][0]), jnp.max(state[(sb, 1)][0]))

        first = n_sub * i
        zero = (jnp.zeros((tk, 1), F32), jnp.zeros((tk, 1), F32), jnp.zeros((tk, LANES), F32))
        todo = []
        for sb in range(n_sub):
            gate = jnp.broadcast_to(first > 0, (tk, tk)) if sb == 0 else None
            todo += [(sb, first + sb, valid), (sb, jnp.maximum(first + sb - 1, 0), gate)]
        state = tiles(todo, {(sb, hh): zero for sb in range(n_sub) for hh in range(2)})
        for sb in range(n_sub):
            def cond(st):
                return jnp.logical_and(st[0] >= 0, st[2] > UNDERFLOW)

            def step(st, sb=sb):
                mine = tiles([(sb, st[0], None)], st[1])
                return st[0] - 1, mine, live(mine, sb)

            mine = {k: v for k, v in state.items() if k[0] == sb}
            _, mine, _ = lax.while_loop(cond, step, (first + sb - 2, mine, live(mine, sb)))
            dq_ref[sb * tk:(sb + 1) * tk, :] = (
                jnp.where(lane < SB_HD, mine[(sb, 0)][2], mine[(sb, 1)][2]) * SB_SCALE).astype(BF16)
        wait()

    h_ins = hosted.ins if hosted else []
    res = pl.pallas_call(
        body, name="attn_bwd_hosting" if hosted else "attn_bwd", grid=(n_p, nq),
        in_specs=_attn_qkv_specs(tq, T) + [pl.BlockSpec((tq, LANES), lambda p, i: (i, p)),
                                           pl.BlockSpec((tq, LANES), lambda p, i: (i, yc_blk + p))]
        + [HBM_SPEC] * len(h_ins),
        out_specs=[pl.BlockSpec((tq, LANES), lambda p, i: (i, p)), pl.BlockSpec((T, LANES), lambda p, i: (0, p)),
                   pl.BlockSpec((T, LANES), lambda p, i: (0, p))] + [HBM_SPEC] * len(h_ins),
        out_shape=[jax.ShapeDtypeStruct((T, SB_WIDTH), BF16)] + [jax.ShapeDtypeStruct((T, SB_WIDTH), F32)] * 2
        + (hosted.out_shapes if hosted else []),
        scratch_shapes=hosted.sems() if hosted else [],
        compiler_params=_params(has_side_effects=hosted is not None),
    )(qkv, qkv, qkv, o, dymix, *h_ins)
    return res[:3], res[3:]


def _outproj_fwd(x, ya, yb, yc, w, hosted=None):
    T, D = x.shape
    tt = _tile(T, 512)
    nt = T // tt

    def body(*refs):
        i = pl.program_id(0)
        (x_ref, ya_ref, yb_ref, yc_ref, w_ref, x1_ref, ymix_ref), start, wait = _host(
            hosted, refs, 5, 2, i == 0, i == nt - 1)
        start()
        ymix_ref[:, 0:POOL_WIDTH] = ya_ref[...].astype(BF16)
        ymix_ref[:, POOL_WIDTH:POOL_WIDTH + SG_WIDTH] = yb_ref[...].astype(BF16)
        ymix_ref[:, POOL_WIDTH + SG_WIDTH:] = yc_ref[...].astype(BF16)
        x1_ref[...] = x_ref[...] + _nn(ymix_ref[...], w_ref[...])
        wait()

    row = lambda width: pl.BlockSpec((tt, width), lambda i: (i, 0))
    h_ins = hosted.ins if hosted else []
    res = pl.pallas_call(
        body, name="outproj_fwd_hosting" if hosted else "outproj_fwd", grid=(nt,),
        in_specs=[row(D), row(POOL_WIDTH), row(SG_WIDTH), row(SB_WIDTH), pl.BlockSpec((D, D), lambda i: (0, 0))]
        + [HBM_SPEC] * len(h_ins),
        out_specs=[row(D), row(D)] + [HBM_SPEC] * len(h_ins),
        out_shape=[jax.ShapeDtypeStruct((T, D), F32), jax.ShapeDtypeStruct((T, D), BF16)]
        + (hosted.out_shapes if hosted else []),
        input_output_aliases=hosted.aliases(5, 2) if hosted else {},
        scratch_shapes=hosted.sems() if hosted else [],
        compiler_params=_params(has_side_effects=hosted is not None),
    )(x, ya, yb, yc, w, *h_ins)
    return res[:2], res[2:]


def _nt_matmul(a, w):
    T, N = a.shape
    K = w.shape[0]
    tt = _tile(T, 512)

    def body(a_ref, w_ref, o_ref):
        o_ref[...] = _nt(a_ref[...].astype(BF16), w_ref[...])

    return pl.pallas_call(
        body, name="nt_matmul", grid=(T // tt,),
        in_specs=[pl.BlockSpec((tt, N), lambda i: (i, 0)), pl.BlockSpec((K, N), lambda i: (0, 0))],
        out_specs=pl.BlockSpec((tt, K), lambda i: (i, 0)),
        out_shape=jax.ShapeDtypeStruct((T, K), F32),
        compiler_params=_params(),
    )(a, w)


def _tn_matmul(a, b, name, n_split=1, hosted=None):
    T, K = a.shape
    N = b.shape[1]
    tk = _tile(K, 1024)
    tn = _tile(N // n_split, 1024)
    tt = _tile(T, 2048)
    nper = N // n_split // tn
    nk, nn, nt = K // tk, N // tn, T // tt

    def body(*refs):
        k, n, t = pl.program_id(0), pl.program_id(1), pl.program_id(2)
        (a_ref, b_ref, o_ref), start, wait = _host(
            hosted, refs, 2, 1, jnp.logical_and(jnp.logical_and(k == 0, n == 0), t == 0),
            jnp.logical_and(jnp.logical_and(k == nk - 1, n == nn - 1), t == nt - 1))
        start()

        @pl.when(t == 0)
        def _():
            o_ref[...] = jnp.zeros_like(o_ref)

        o_ref[...] += _tn(a_ref[...], b_ref[...].astype(BF16))
        wait()

    h_ins = hosted.ins if hosted else []
    res = pl.pallas_call(
        body, name=name + "_hosting" if hosted else name, grid=(nk, nn, nt),
        in_specs=[pl.BlockSpec((tt, tk), lambda k, n, t: (t, k)), pl.BlockSpec((tt, tn), lambda k, n, t: (t, n))]
        + [HBM_SPEC] * len(h_ins),
        out_specs=[pl.BlockSpec((None, tk, tn), lambda k, n, t: (n // nper, k, n % nper))] + [HBM_SPEC] * len(h_ins),
        out_shape=[jax.ShapeDtypeStruct((n_split, K, N // n_split), F32)] + (hosted.out_shapes if hosted else []),
        scratch_shapes=hosted.sems() if hosted else [],
        compiler_params=_params(has_side_effects=hosted is not None),
    )(a, b, *h_ins)
    return (res[0], res[1:]) if hosted else res[0]


def _mlp_fwd(x, g, w_up, w_down, hosted=None):
    T, D = x.shape
    n_blk, _, width = w_up.shape
    F = n_blk * width
    tt = _tile(T, 1024)
    fc = _tile(width, MLP_CHUNK)
    per = width // fc
    nc = F // fc
    nt = T // tt

    def body(*refs):
        i, c = pl.program_id(0), pl.program_id(1)
        (x_ref, g_ref, wu_ref, wd_ref, y_ref, h_ref, u_ref, a_ref), start, wait = _host(
            hosted, refs, 4, 4, jnp.logical_and(i == 0, c == 0), jnp.logical_and(i == nt - 1, c == nc - 1))
        start()

        @pl.when(c == 0)
        def _():
            xv = x_ref[...]
            h, _, _ = _rms_fwd(xv, g_ref[...])
            h_ref[...] = h.astype(BF16)
            y_ref[...] = xv

        u = _nn(h_ref[...], wu_ref[...])
        u_ref[...] = u.astype(BF16)
        a = jnp.square(jnp.maximum(u, 0.0)).astype(BF16)
        a_ref[...] = a
        y_ref[...] += _nn(a, wd_ref[...])
        wait()

    h_ins = hosted.ins if hosted else []
    res = pl.pallas_call(
        body, name="mlp_fwd_hosting" if hosted else "mlp_fwd", grid=(nt, nc),
        in_specs=[pl.BlockSpec((tt, D), lambda i, c: (i, 0)), pl.BlockSpec((1, D), lambda i, c: (0, 0)),
                  pl.BlockSpec((None, D, fc), lambda i, c: (c // per, 0, c % per)),
                  pl.BlockSpec((fc, D), lambda i, c: (c, 0))]
        + [HBM_SPEC] * len(h_ins),
        out_specs=[pl.BlockSpec((tt, D), lambda i, c: (i, 0)), pl.BlockSpec((tt, D), lambda i, c: (i, 0)),
                   pl.BlockSpec((tt, fc), lambda i, c: (i, c)), pl.BlockSpec((tt, fc), lambda i, c: (i, c))]
        + [HBM_SPEC] * len(h_ins),
        out_shape=[jax.ShapeDtypeStruct((T, D), F32), jax.ShapeDtypeStruct((T, D), BF16),
                   jax.ShapeDtypeStruct((T, F), BF16), jax.ShapeDtypeStruct((T, F), BF16)]
        + (hosted.out_shapes if hosted else []),
        scratch_shapes=hosted.sems() if hosted else [],
        compiler_params=_params(has_side_effects=hosted is not None),
    )(x, g, w_up, w_down, *h_ins)
    return res[:4], res[4:]


def _mlp_bwd(dy, x, g, u, w_up, w_down, hosted=None):
    T, D = x.shape
    n_blk, _, width = w_up.shape
    F = n_blk * width
    tt = _tile(T, 1024)
    fc = _tile(width, MLP_CHUNK)
    per = width // fc
    nc = F // fc
    nt = T // tt

    def body(*refs):
        i, c = pl.program_id(0), pl.program_id(1)
        (dy_ref, x_ref, g_ref, u_ref, wu_ref, wd_ref, dx_ref, du_ref, dg_ref, dyb_ref, dh_ref), start, wait = _host(
            hosted, refs, 6, 3, jnp.logical_and(i == 0, c == 0), jnp.logical_and(i == nt - 1, c == nc - 1))
        start()

        @pl.when(c == 0)
        def _():
            dyb_ref[...] = dy_ref[...].astype(BF16)
            dh_ref[...] = jnp.zeros_like(dh_ref)

        @pl.when(jnp.logical_and(i == 0, c == 0))
        def _():
            dg_ref[...] = jnp.zeros_like(dg_ref)

        da = _nt(dyb_ref[...], wd_ref[...])
        du = (da * (2.0 * jnp.maximum(u_ref[...].astype(F32), 0.0))).astype(BF16)
        du_ref[...] = du
        dh_ref[...] += _nt(du, wu_ref[...])

        @pl.when(c == nc - 1)
        def _():
            gv = g_ref[...]
            _, xhat, r = _rms_fwd(x_ref[...], gv)
            dx, dgrow = _rms_bwd(dh_ref[...], xhat, r, gv)
            dx_ref[...] = dy_ref[...] + dx
            dg_ref[...] += jnp.sum(dgrow, axis=0, keepdims=True)

        wait()

    h_ins = hosted.ins if hosted else []
    res = pl.pallas_call(
        body, name="mlp_bwd_hosting" if hosted else "mlp_bwd", grid=(nt, nc),
        in_specs=[pl.BlockSpec((tt, D), lambda i, c: (i, 0)), pl.BlockSpec((tt, D), lambda i, c: (i, 0)),
                  pl.BlockSpec((1, D), lambda i, c: (0, 0)), pl.BlockSpec((tt, fc), lambda i, c: (i, c)),
                  pl.BlockSpec((None, D, fc), lambda i, c: (c // per, 0, c % per)),
                  pl.BlockSpec((fc, D), lambda i, c: (c, 0))]
        + [HBM_SPEC] * len(h_ins),
        out_specs=[pl.BlockSpec((tt, D), lambda i, c: (i, 0)), pl.BlockSpec((tt, fc), lambda i, c: (i, c)),
                   pl.BlockSpec((1, D), lambda i, c: (0, 0))] + [HBM_SPEC] * len(h_ins),
        out_shape=[jax.ShapeDtypeStruct((T, D), F32), jax.ShapeDtypeStruct((T, F), BF16),
                   jax.ShapeDtypeStruct((1, D), F32)] + (hosted.out_shapes if hosted else []),
        scratch_shapes=[pltpu.VMEM((tt, D), BF16), pltpu.VMEM((tt, D), F32)] + (hosted.sems() if hosted else []),
        compiler_params=_params(has_side_effects=hosted is not None),
    )(dy, x, g, u, w_up, w_down, *h_ins)
    return res[:3], res[3:]


def _loss_head(x, g, target):
    T, D = x.shape
    tt = _tile(T, 512)

    def body(x_ref, g_ref, t_ref, loss_ref, dx_ref, dg_ref):
        gv = g_ref[...]
        y, xhat, r = _rms_fwd(x_ref[...], gv)
        err = y - t_ref[...]
        dx, dgrow = _rms_bwd(err * (1.0 / D), xhat, r, gv)
        dx_ref[...] = dx

        @pl.when(pl.program_id(0) == 0)
        def _():
            loss_ref[...] = jnp.zeros_like(loss_ref)
            dg_ref[...] = jnp.zeros_like(dg_ref)

        loss_ref[...] += 0.5 * jnp.sum(jnp.mean(err * err, axis=-1, keepdims=True), axis=0, keepdims=True)
        dg_ref[...] += jnp.sum(dgrow, axis=0, keepdims=True)

    return pl.pallas_call(
        body, name="loss_head", grid=(T // tt,),
        in_specs=[pl.BlockSpec((tt, D), lambda i: (i, 0)), pl.BlockSpec((1, D), lambda i: (0, 0)),
                  pl.BlockSpec((tt, D), lambda i: (i, 0))],
        out_specs=[pl.BlockSpec((1, LANES), lambda i: (0, 0)), pl.BlockSpec((tt, D), lambda i: (i, 0)),
                   pl.BlockSpec((1, D), lambda i: (0, 0))],
        out_shape=[jax.ShapeDtypeStruct((1, LANES), F32), jax.ShapeDtypeStruct((T, D), F32),
                   jax.ShapeDtypeStruct((1, D), F32)],
        compiler_params=_params(),
    )(x, g, target)


def _rows(shape, pref=512):
    last = shape[-1]
    rows = 1
    for s in shape[:-1]:
        rows *= s
    tr = rows
    if rows * last > 256 * 1024:
        for cand in (pref, 256, 128, 64, 32, 16, 8):
            if rows % cand == 0:
                tr = cand
                break
    return rows, last, tr


def _elementwise(fn, name, ins, n_out, out_dtype=F32):
    shape = ins[0].shape
    rows, last, tr = _rows(shape)
    flat = [a.reshape(rows, last) for a in ins]
    n_in = len(ins)

    def body(*refs):
        res = fn(*[r[...] for r in refs[:n_in]])
        if n_out == 1:
            res = (res,)
        for r, v in zip(refs[n_in:], res):
            r[...] = v.astype(r.dtype)

    spec = pl.BlockSpec((tr, last), lambda i: (i, 0))
    outs = pl.pallas_call(
        body, name=name, grid=(rows // tr,),
        in_specs=[spec] * n_in, out_specs=[spec] * n_out,
        out_shape=[jax.ShapeDtypeStruct((rows, last), out_dtype)] * n_out,
        compiler_params=_params(),
    )(*flat)
    return [o.reshape(shape) for o in outs]


def _add_pairs(gs, os, c_idx):
    n = len(gs)
    halves = [(g.shape[1] // 2, g.shape[2]) for g in gs]

    def body(c_ref, *refs):
        for a in range(n):
            refs[2 * n + a][...] = refs[2 * a][...] + refs[2 * a + 1][...]

    in_specs = []
    for h, C in halves:
        in_specs += [pl.BlockSpec((None, h, C), lambda q, c: (q, c[0], 0)), pl.BlockSpec((None, h, C), lambda q, c: (q, 0, 0))]
    return pl.pallas_call(
        body, name="add_pairs",
        grid_spec=pltpu.PrefetchScalarGridSpec(
            num_scalar_prefetch=1, grid=(N_CHIPS,), in_specs=in_specs,
            out_specs=[pl.BlockSpec((None, h, C), lambda q, c: (q, 0, 0)) for h, C in halves]),
        out_shape=[jax.ShapeDtypeStruct((N_CHIPS, h, C), F32) for h, C in halves],
        compiler_params=_params(),
    )(c_idx.astype(jnp.int32).reshape(1), *[x for pair in zip(gs, os) for x in pair])


def _add_chips(ps, rs, q_idx):
    n = len(ps)
    steps = 2
    blocks = [(p.shape[1] // steps, p.shape[2]) for p in ps]

    def body(q_ref, *refs):
        for a in range(n):
            p_ref, r0_ref, r1_ref, r2_ref = refs[4 * a:4 * a + 4]
            refs[4 * n + a][...] = (p_ref[...] + r0_ref[...]) + (r1_ref[...] + r2_ref[...])

    def arrived(tr, C, k):
        return pl.BlockSpec((None, tr, C), lambda i, q: (k, i, 0))

    in_specs, operands = [], []
    for (tr, C), p, r in zip(blocks, ps, rs):
        in_specs += [pl.BlockSpec((None, tr, C), lambda i, q: (q[0], i, 0)), arrived(tr, C, 0), arrived(tr, C, 1),
                     arrived(tr, C, 2)]
        operands += [p, r, r, r]
    return pl.pallas_call(
        body, name="add_chips",
        grid_spec=pltpu.PrefetchScalarGridSpec(
            num_scalar_prefetch=1, grid=(steps,), in_specs=in_specs,
            out_specs=[pl.BlockSpec((tr, C), lambda i, q: (i, 0)) for tr, C in blocks]),
        out_shape=[jax.ShapeDtypeStruct((p.shape[1], p.shape[2]), F32) for p in ps],
        compiler_params=_params(),
    )(q_idx.astype(jnp.int32).reshape(1), *operands)


def _adamw(w, g, m, v):
    m = ADAM_B1 * m + (1.0 - ADAM_B1) * g
    v = ADAM_B2 * v + (1.0 - ADAM_B2) * jnp.square(g)
    m_hat = m / (1.0 - ADAM_B1 ** ADAM_STEP)
    v_hat = v / (1.0 - ADAM_B2 ** ADAM_STEP)
    delta = -ADAM_LR * (m_hat / (jnp.sqrt(v_hat) + ADAM_EPS) + ADAM_WD * w)
    return delta, m, v


def _place():
    x, y, c = lax.axis_index("x"), lax.axis_index("y"), lax.axis_index("c")
    chips = [(1 - x, y), (x, 1 - y), (1 - x, 1 - y)]
    return x, y, c, chips


def _remote(src, dst, ssem, rsem, k, dev):
    return pltpu.make_async_remote_copy(src_ref=src, dst_ref=dst, send_sem=ssem.at[k], recv_sem=rsem.at[k],
                                        device_id=dev, device_id_type=MESH)


def _gather_weights(shards):
    n = len(shards)
    halves = [s.shape[1] // 2 for s in shards]

    def body(*refs):
        src, out = refs[:n], refs[n:2 * n]
        ssem, rsem = refs[2 * n:]
        x, y, c, chips = _place()
        me_q = 2 * x + y
        sib = (x, y, 1 - c)

        def half(a, q, cc):
            return out[a].at[q, :, pl.ds(cc * halves[a], halves[a]), :]

        first = []
        for a in range(n):
            mine = src[a].at[:, pl.ds(c * halves[a], halves[a]), :]
            for r, chip in enumerate(chips):
                first.append(_remote(mine, half(a, me_q, c), ssem, rsem, a * 3 + r, (*chip, c)))
        for cp in first:
            cp.start()
        passed = []
        for a in range(n):
            for r, chip in enumerate(chips):
                q = 2 * chip[0] + chip[1]
                k = a * 3 + r
                _remote(half(a, q, c), half(a, q, c), ssem, rsem, k, (*chip, c)).wait_recv()
                cp = _remote(half(a, q, c), half(a, q, c), ssem, rsem, 3 * n + k, sib)
                cp.start()
                passed.append(cp)
        for a in range(n):
            for r, chip in enumerate(chips):
                q = 2 * chip[0] + chip[1]
                _remote(half(a, q, 1 - c), half(a, q, 1 - c), ssem, rsem, 3 * n + a * 3 + r, sib).wait_recv()
        for cp in first + passed:
            cp.wait_send()

    return pl.pallas_call(
        body, name="gather_weights",
        in_specs=[HBM_SPEC] * n, out_specs=[HBM_SPEC] * n,
        out_shape=[jax.ShapeDtypeStruct((N_CHIPS,) + s.shape, s.dtype) for s in shards],
        scratch_shapes=[pltpu.SemaphoreType.DMA((6 * n,)), pltpu.SemaphoreType.DMA((6 * n,))],
        compiler_params=_params(has_side_effects=True),
    )(*shards)


def _gather_over_ici(shards):
    n = len(shards)
    halves = [s.shape[1] // 2 for s in shards]

    def copies(src, out, ssem, rsem):
        x, y, c, chips = _place()
        me_q = 2 * x + y
        res = []
        for a in range(n):
            rows = pl.ds(c * halves[a], halves[a])
            mine = src[a].at[:, rows, :]
            for r, chip in enumerate(chips):
                dev = (*chip, c)
                res.append((_remote(mine, out[a].at[me_q, :, rows, :], ssem, rsem, a * 3 + r, dev),
                            _remote(mine, out[a].at[2 * chip[0] + chip[1], :, rows, :], ssem, rsem, a * 3 + r, dev)))
        return res

    return _Hosted(list(shards), [jax.ShapeDtypeStruct((N_CHIPS,) + s.shape, s.dtype) for s in shards], 3 * n, copies)


def _pass_over_d2d(gathered):
    n = len(gathered)
    halves = [g.shape[2] // 2 for g in gathered]

    def copies(_, out, ssem, rsem):
        x, y, c, chips = _place()
        sib = (x, y, 1 - c)
        res = []
        for a in range(n):
            for r, chip in enumerate(chips):
                q = 2 * chip[0] + chip[1]
                mine = out[a].at[q, :, pl.ds(c * halves[a], halves[a]), :]
                theirs = out[a].at[q, :, pl.ds((1 - c) * halves[a], halves[a]), :]
                res.append((_remote(mine, mine, ssem, rsem, a * 3 + r, sib),
                            _remote(theirs, theirs, ssem, rsem, a * 3 + r, sib)))
        return res

    return _Hosted(list(gathered), [jax.ShapeDtypeStruct(g.shape, g.dtype) for g in gathered], 3 * n, copies,
                   in_place=True)


def _pass_to_sibling(gathered):
    n = len(gathered)
    halves = [g.shape[2] // 2 for g in gathered]

    def body(*refs):
        out = refs[n:2 * n]
        ssem, rsem = refs[2 * n:]
        x, y, c, chips = _place()
        sib = (x, y, 1 - c)

        def half(a, q, cc):
            return out[a].at[q, :, pl.ds(cc * halves[a], halves[a]), :]

        cps = []
        for a in range(n):
            for r, chip in enumerate(chips):
                q = 2 * chip[0] + chip[1]
                cps.append(_remote(half(a, q, c), half(a, q, c), ssem, rsem, a * 3 + r, sib))
        for cp in cps:
            cp.start()
        for a in range(n):
            for r, chip in enumerate(chips):
                q = 2 * chip[0] + chip[1]
                _remote(half(a, q, 1 - c), half(a, q, 1 - c), ssem, rsem, a * 3 + r, sib).wait_recv()
        for cp in cps:
            cp.wait_send()

    return pl.pallas_call(
        body, name="pass_to_sibling",
        in_specs=[HBM_SPEC] * n, out_specs=[HBM_SPEC] * n,
        out_shape=[jax.ShapeDtypeStruct(g.shape, g.dtype) for g in gathered],
        input_output_aliases={a: a for a in range(n)},
        scratch_shapes=[pltpu.SemaphoreType.DMA((3 * n,)), pltpu.SemaphoreType.DMA((3 * n,))],
        compiler_params=_params(has_side_effects=True),
    )(*gathered)


def _scatter_over_ici(parts):
    n = len(parts)

    def copies(src, out, ssem, rsem):
        x, y, c, chips = _place()
        res = []
        for a in range(n):
            for r, chip in enumerate(chips):
                cp = _remote(src[a].at[2 * chip[0] + chip[1]], out[a].at[r], ssem, rsem, a * 3 + r, (*chip, c))
                res.append((cp, cp))
        return res

    return _Hosted(list(parts), [jax.ShapeDtypeStruct((3,) + p.shape[1:], F32) for p in parts], 3 * n, copies)


def _swap_over_d2d(grads):
    n = len(grads)
    halves = [g.shape[1] // 2 for g in grads]

    def copies(src, out, ssem, rsem):
        x, y, c, _ = _place()
        res = []
        for a in range(n):
            cp = _remote(src[a].at[:, pl.ds((1 - c) * halves[a], halves[a]), :], out[a], ssem, rsem, a, (x, y, 1 - c))
            res.append((cp, cp))
        return res

    return _Hosted(list(grads), [jax.ShapeDtypeStruct((N_CHIPS, h, g.shape[2]), F32) for g, h in zip(grads, halves)],
                   n, copies)


def _swap_halves(grads):
    n = len(grads)
    halves = [g.shape[1] // 2 for g in grads]

    def body(*refs):
        src, out = refs[:n], refs[n:2 * n]
        ssem, rsem = refs[2 * n:]
        x, y, c, _ = _place()
        cps = [_remote(src[a].at[:, pl.ds((1 - c) * halves[a], halves[a]), :], out[a], ssem, rsem, a, (x, y, 1 - c))
               for a in range(n)]
        for cp in cps:
            cp.start()
        for cp in cps:
            cp.wait()

    return pl.pallas_call(
        body, name="swap_halves",
        in_specs=[HBM_SPEC] * n, out_specs=[HBM_SPEC] * n,
        out_shape=[jax.ShapeDtypeStruct((N_CHIPS, h, g.shape[2]), F32) for g, h in zip(grads, halves)],
        scratch_shapes=[pltpu.SemaphoreType.DMA((n,)), pltpu.SemaphoreType.DMA((n,))],
        compiler_params=_params(has_side_effects=True),
    )(*grads)


def _scatter_chips(parts):
    n = len(parts)

    def body(*refs):
        src, out = refs[:n], refs[n:2 * n]
        ssem, rsem = refs[2 * n:]
        x, y, c, chips = _place()
        cps = []
        for a in range(n):
            for r, chip in enumerate(chips):
                cps.append(_remote(src[a].at[2 * chip[0] + chip[1]], out[a].at[r], ssem, rsem, a * 3 + r, (*chip, c)))
        for cp in cps:
            cp.start()
        for cp in cps:
            cp.wait()

    return pl.pallas_call(
        body, name="scatter_chips",
        in_specs=[HBM_SPEC] * n, out_specs=[HBM_SPEC] * n,
        out_shape=[jax.ShapeDtypeStruct((3,) + p.shape[1:], F32) for p in parts],
        scratch_shapes=[pltpu.SemaphoreType.DMA((3 * n,)), pltpu.SemaphoreType.DMA((3 * n,))],
        compiler_params=_params(has_side_effects=True),
    )(*parts)


def _swap_reduced_over_d2d(reduced):
    n = len(reduced)

    def copies(src, out, ssem, rsem):
        x, y, c, _ = _place()
        res = []
        for a in range(n):
            cp = _remote(src[a], out[a], ssem, rsem, a, (x, y, 1 - c))
            res.append((cp, cp))
        return res

    return _Hosted(list(reduced), [jax.ShapeDtypeStruct(r.shape, F32) for r in reduced], n, copies)


def _swap_reduced(reduced):
    n = len(reduced)

    def body(*refs):
        src, out = refs[:n], refs[n:2 * n]
        ssem, rsem = refs[2 * n:]
        x, y, c, _ = _place()
        cps = [_remote(src[a], out[a], ssem, rsem, a, (x, y, 1 - c)) for a in range(n)]
        for cp in cps:
            cp.start()
        for cp in cps:
            cp.wait()

    return pl.pallas_call(
        body, name="swap_reduced",
        in_specs=[HBM_SPEC] * n, out_specs=[HBM_SPEC] * n,
        out_shape=[jax.ShapeDtypeStruct(r.shape, F32) for r in reduced],
        scratch_shapes=[pltpu.SemaphoreType.DMA((n,)), pltpu.SemaphoreType.DMA((n,))],
        compiler_params=_params(has_side_effects=True),
    )(*reduced)


def _allreduce_small(buf, hosted=None):
    R, L = buf.shape

    def body(*refs):
        (buf_ref, out_ref, pair_ref, chip_ref, ssem, rsem), start, wait = _host(hosted, refs, 1, 1, True, True)
        start()
        x, y, c, chips = _place()
        me_q = 2 * x + y
        pair_ref[c] = buf_ref[...]
        to_sib = _remote(buf_ref, pair_ref.at[c], ssem, rsem, 0, (x, y, 1 - c))
        to_sib.start()
        _remote(buf_ref, pair_ref.at[1 - c], ssem, rsem, 0, (x, y, 1 - c)).wait_recv()
        chip_ref[me_q] = pair_ref[0] + pair_ref[1]
        cps = [_remote(chip_ref.at[me_q], chip_ref.at[me_q], ssem, rsem, 1 + r, (*chip, c))
               for r, chip in enumerate(chips)]
        for cp in cps:
            cp.start()
        for r, chip in enumerate(chips):
            q = 2 * chip[0] + chip[1]
            _remote(chip_ref.at[q], chip_ref.at[q], ssem, rsem, 1 + r, (*chip, c)).wait_recv()
        out_ref[...] = (chip_ref[0] + chip_ref[1]) + (chip_ref[2] + chip_ref[3])
        to_sib.wait_send()
        for cp in cps:
            cp.wait_send()
        wait()

    h_ins = hosted.ins if hosted else []
    res = pl.pallas_call(
        body, name="allreduce_small",
        in_specs=[VMEM_SPEC] + [HBM_SPEC] * len(h_ins), out_specs=[VMEM_SPEC] + [HBM_SPEC] * len(h_ins),
        out_shape=[jax.ShapeDtypeStruct((R, L), F32)] + (hosted.out_shapes if hosted else []),
        scratch_shapes=[pltpu.VMEM((2, R, L), F32), pltpu.VMEM((N_CHIPS, R, L), F32),
                        pltpu.SemaphoreType.DMA((4,)), pltpu.SemaphoreType.DMA((4,))]
        + (hosted.sems() if hosted else []),
        compiler_params=_params(has_side_effects=True),
    )(buf, *h_ins)
    return res[0], res[1:]


def _pack(arrays):
    flat = jnp.concatenate([a.reshape(-1) for a in arrays])
    pad = (-flat.shape[0]) % (8 * LANES)
    return jnp.pad(flat, (0, pad)).reshape(-1, LANES)


def _unpack(buf, like):
    flat = buf.reshape(-1)
    out, off = [], 0
    for a in like:
        out.append(flat[off:off + a.size].reshape(a.shape))
        off += a.size
    return out


def _block_diag(pw):
    rows = []
    for gi in range(len(POOL_WINDOWS)):
        blocks = [pw[gi] if gj == gi else jnp.zeros_like(pw[gi]) for gj in range(len(POOL_WINDOWS))]
        rows.append(jnp.concatenate(blocks, axis=1))
    return jnp.concatenate(rows, axis=0)


def kernel(x, norm1, w_in, pool_w, pool_scale, sg_norm, sg_w, sg_b, w_out, norm2, w_up, w_down, final_norm, loss_target, m_norm1, m_w_in, m_pool_w, m_pool_scale, m_sg_norm, m_sg_w, m_sg_b, m_w_out, m_norm2, m_w_up, m_w_down, m_final_norm, v_norm1, v_w_in, v_pool_w, v_pool_scale, v_sg_norm, v_sg_w, v_sg_b, v_w_out, v_norm2, v_w_up, v_w_down, v_final_norm):
    depth = norm1.shape[0]
    T = x.shape[1]
    xs = x.reshape(T, D_MODEL)
    target = loss_target.reshape(T, D_MODEL)

    assert depth == 2
    c_idx = lax.axis_index("c")
    q_idx = 2 * lax.axis_index("x") + lax.axis_index("y")
    own = [w.astype(BF16) for w in (w_in, w_out, w_up, w_down)]
    gathered = {(0, 0): _gather_weights([own[0][:1]])[0]}

    def full(a, l, axis):
        blocks = lax.dynamic_update_slice(gathered[(a, l)], own[a][l][None, None], (q_idx, 0, 0, 0))[:, 0]
        if axis is None:
            return blocks
        if axis == 0:
            return blocks.reshape(-1, blocks.shape[-1])
        return jnp.concatenate([blocks[q] for q in range(N_CHIPS)], axis=axis)

    half_way = {}

    def gather_behind(call, keys, at_once):
        res, over_ici = call(_gather_over_ici([own[a][l:l + 1] for a, l in keys]))
        gathered.update(zip(keys[:at_once], _pass_to_sibling(over_ici[:at_once])))
        half_way.update(zip(keys[at_once:], over_ici[at_once:]))
        return res

    def pass_behind(call, keys):
        res, done = call(_pass_over_d2d([half_way.pop(k) for k in keys]))
        gathered.update(zip(keys, done))
        return res

    tril = jnp.tril(jnp.ones((CHUNK, CHUNK), F32))
    saved = []
    cur = xs
    wi, wo, wu, wd = {}, {}, {}, {}
    for l in range(depth):
        wbd = _block_diag(pool_w[l]).astype(BF16)
        wm = sg_w[l] * tril
        wm_s = wm.reshape(SG_HEADS * CHUNK, CHUNK).astype(BF16)
        wmt_s = jnp.swapaxes(wm, 1, 2).reshape(SG_HEADS * CHUNK, CHUNK).astype(BF16)
        bias = jnp.repeat(sg_b[l].T, SB_HD, axis=1)
        n1, n2 = norm1[l][None], norm2[l][None]
        psc, sgn = pool_scale[l][None], sg_norm[l][None]
        wi[l] = full(0, l, 1)
        proj, h, qkv = _inproj_fwd(cur, n1, wi[l])
        ya = _pool_fwd(proj, wbd, psc)
        yb = _sg_fwd(proj, wm_s, bias, sgn)
        if l == 0:
            yc = gather_behind(lambda hosted: _attn_fwd(qkv, hosted), [(1, 0), (2, 0), (3, 0)], 1)
            wo[l] = full(1, l, 0)
            x1, ymix = pass_behind(lambda hosted: _outproj_fwd(cur, ya, yb, yc, wo[l], hosted), [(2, 0), (3, 0)])
        else:
            yc = pass_behind(lambda hosted: _attn_fwd(qkv, hosted), [(1, l), (2, l), (3, l)])
            wo[l] = full(1, l, 0)
            (x1, ymix), _ = _outproj_fwd(cur, ya, yb, yc, wo[l])
        wu[l], wd[l] = full(2, l, None), full(3, l, 0)
        if l == 0:
            x2, h2, u, act = gather_behind(lambda hosted: _mlp_fwd(x1, n2, wu[l], wd[l], hosted),
                                           [(0, 1), (1, 1), (2, 1), (3, 1)], 1)
        else:
            (x2, h2, u, act), _ = _mlp_fwd(x1, n2, wu[l], wd[l])
        saved.append(dict(x0=cur, x1=x1, proj=proj, h=h, qkv=qkv, yc=yc, ymix=ymix, h2=h2, u=u, act=act,
                          wbd=wbd, wm_s=wm_s, wmt_s=wmt_s, bias=bias, n1=n1, n2=n2, psc=psc, sgn=sgn))
        cur = x2

    loss_row, dcur, d_final = _loss_head(cur, final_norm[None], target)

    small = [None] * depth
    grads, parts, reduced = {}, {}, {}

    def pair_up(keys, swapped):
        parts.update(zip(keys, _add_pairs([grads[k] for k in keys], swapped, c_idx)))

    def chip_up(keys, arrived):
        reduced.update(zip(keys, _add_chips([parts[k] for k in keys], arrived, q_idx)))

    for l in reversed(range(depth)):
        s = saved[l]
        if l == 0:
            keys = [(2, 1), (3, 1)]
            (dx1, du, d_n2), arrived = _mlp_bwd(dcur, s["x1"], s["n2"], s["u"], wu[l], wd[l],
                                                _scatter_over_ici([parts[k] for k in keys]))
            chip_up(keys, arrived)
        else:
            (dx1, du, d_n2), _ = _mlp_bwd(dcur, s["x1"], s["n2"], s["u"], wu[l], wd[l])
        if l == 0:
            keys = [(0, 1)]
            grads[(2, l)], arrived = _tn_matmul(s["h2"], du, "grad_w_up", n_split=N_CHIPS,
                                                hosted=_scatter_over_ici([parts[k] for k in keys]))
            chip_up(keys, arrived)
            keys = [(1, 1)]
            g_down, arrived = _tn_matmul(s["act"], dcur, "grad_w_down",
                                         hosted=_scatter_over_ici([parts[k] for k in keys]))
            chip_up(keys, arrived)
        else:
            grads[(2, l)] = _tn_matmul(s["h2"], du, "grad_w_up", n_split=N_CHIPS)
            g_down = _tn_matmul(s["act"], dcur, "grad_w_down")
        grads[(3, l)] = g_down[0].reshape(N_CHIPS, D_FF // N_CHIPS, D_MODEL)
        dymix = _nt_matmul(dx1, wo[l])
        grads[(1, l)] = _tn_matmul(s["ymix"], dx1, "grad_w_out")[0].reshape(N_CHIPS, D_MODEL // N_CHIPS, D_MODEL)
        da_in, d_wbd, d_psc = _pool_bwd(s["proj"], dymix, s["wbd"], s["psc"])
        if l == 0:
            keys = [(1, 0), (2, 0), (3, 0)]
            (du_pre, dv_pre, d_wm, d_bias, d_sgn), swapped = _sg_bwd(
                s["proj"], dymix, s["wm_s"], s["wmt_s"], s["bias"], s["sgn"], _swap_over_d2d([grads[k] for k in keys]))
            pair_up(keys, swapped)
            (dq, dk, dv), arrived = _attn_bwd(s["qkv"], s["yc"], dymix, _scatter_over_ici([parts[k] for k in keys]))
            chip_up(keys, arrived)
        else:
            (du_pre, dv_pre, d_wm, d_bias, d_sgn), _ = _sg_bwd(s["proj"], dymix, s["wm_s"], s["wmt_s"], s["bias"], s["sgn"])
            keys = [(1, l), (2, l), (3, l)]
            (dq, dk, dv), swapped = _attn_bwd(s["qkv"], s["yc"], dymix, _swap_over_d2d([grads[k] for k in keys]))
            pair_up(keys, swapped)
        pieces = [da_in, du_pre, dv_pre, dq, dk, dv]
        if l == 0:
            keys = sorted(reduced)
            g_in_l, swapped = _inproj_grad(s["h"], pieces, _swap_reduced_over_d2d([reduced[k] for k in keys]))
            theirs = dict(zip(keys, swapped))
        else:
            g_in_l, _ = _inproj_grad(s["h"], pieces)
        grads[(0, l)] = g_in_l[0].reshape(D_MODEL, N_CHIPS, IN_COLS // N_CHIPS).transpose(1, 0, 2)
        if l == 0:
            keys = [(0, 0)]
            pair_up(keys, _swap_halves([grads[k] for k in keys]))
            (dx0, d_n1), arrived = _inproj_bwd(pieces, wi[l], s["x0"], s["n1"], dx1,
                                               _scatter_over_ici([parts[k] for k in keys]))
            chip_up(keys, arrived)
        else:
            keys = [(0, l)]
            (dx0, d_n1), swapped = _inproj_bwd(pieces, wi[l], s["x0"], s["n1"], dx1,
                                               _swap_over_d2d([grads[k] for k in keys]))
            pair_up(keys, swapped)
        d_pw = jnp.stack([d_wbd[gi * POOL_GW:(gi + 1) * POOL_GW, gi * POOL_GW:(gi + 1) * POOL_GW]
                          for gi in range(len(POOL_WINDOWS))])
        small[l] = dict(norm1=d_n1[0], pool_w=d_pw, pool_scale=d_psc[0], sg_norm=d_sgn[0],
                        sg_w=d_wm.reshape(SG_HEADS, CHUNK, CHUNK), sg_b=d_bias[:, :SG_HEADS].T, norm2=d_n2[0])
        dcur = dx0
    grad_x = dcur.reshape(x.shape)

    names =["norm1", "pool_w", "pool_scale", "sg_norm", "sg_w", "sg_b", "norm2"]
    slot = jnp.zeros((1,), F32)
    small_w = [norm1, pool_w, pool_scale, sg_norm, sg_w, sg_b, norm2, final_norm, slot]
    small_m = [m_norm1, m_pool_w, m_pool_scale, m_sg_norm, m_sg_w, m_sg_b, m_norm2, m_final_norm, slot]
    small_v = [v_norm1, v_pool_w, v_pool_scale, v_sg_norm, v_sg_w, v_sg_b, v_norm2, v_final_norm, slot]
    small_g = [jnp.stack([small[l][k] for l in range(depth)]) for k in names] + [d_final[0], loss_row[0, :1]]
    keys = [(0, 0)]
    g_packed, _ = _allreduce_small(_pack(small_g))
    theirs.update(zip(keys, _swap_reduced([reduced[k] for k in keys])))

    def joined(a):
        layers = []
        for l in range(depth):
            mine, other = reduced[(a, l)], theirs[(a, l)]
            layers.append(jnp.where(c_idx == 0, jnp.concatenate([mine, other]), jnp.concatenate([other, mine])))
        return jnp.stack(layers)

    gw_in, gw_out, gw_up, gw_down = [joined(a) for a in range(4)]

    loss = _unpack(g_packed, small_w)[-1][0]
    s_delta, s_m, s_v = _elementwise(_adamw, "adamw_small", [_pack(small_w), g_packed, _pack(small_m), _pack(small_v)], 3)
    gs = dict(zip(names + ["final_norm"], _unpack(g_packed, small_w)))
    ds = dict(zip(names + ["final_norm"], _unpack(s_delta, small_w)))
    ms = dict(zip(names + ["final_norm"], _unpack(s_m, small_w)))
    vs = dict(zip(names + ["final_norm"], _unpack(s_v, small_w)))

    big_g = dict(w_in=gw_in, w_out=gw_out, w_up=gw_up, w_down=gw_down)
    big_w = dict(w_in=(w_in, m_w_in, v_w_in), w_out=(w_out, m_w_out, v_w_out),
                 w_up=(w_up, m_w_up, v_w_up), w_down=(w_down, m_w_down, v_w_down))
    for k, (w, m, v) in big_w.items():
        operands = [w, big_g[k], m, v]
        if k == "w_in":
            operands = [jnp.swapaxes(o, 1, 2) for o in operands]
        ds[k], ms[k], vs[k] = _elementwise(_adamw, "adamw_" + k, operands, 3)
        if k == "w_in":
            ds[k], ms[k], vs[k] = [jnp.swapaxes(o, 1, 2) for o in (ds[k], ms[k], vs[k])]
        gs[k] = big_g[k]

    order = ["norm1", "w_in", "pool_w", "pool_scale", "sg_norm", "sg_w", "sg_b", "w_out", "norm2", "w_up", "w_down",
             "final_norm"]
    return (loss, grad_x, *[gs[k] for k in order], *[ds[k] for k in order], *[ms[k] for k in order],
            *[vs[k] for k in order])
```

```python
import functools

import jax
import jax.numpy as jnp
from jax import lax
from jax.experimental import pallas as pl
from jax.experimental.pallas import tpu as pltpu

F32 = jnp.float32
BF16 = jnp.bfloat16
MESH = pl.DeviceIdType.MESH
AXES = ("x", "y", "c")

EPS = 1e-6
D_MODEL = 1024
POOL_WIDTH = 256
SG_WIDTH = 256
SB_WIDTH = 512
POOL_WINDOWS = (2, 4, 8, 16)
POOL_GW = 64
POOL_HALO = 16
CHUNK = 128
SG_HEADS = 4
SB_HD = 64
SB_SCALE = 0.125
IN_COLS = 2304
QKV_OFF = 768
D_FF = 4096
N_CHIPS = 4
LANES = 128
VMEM_LIMIT = 56 * 1024 * 1024
MLP_CHUNK = 512
ATTN_TILE = 256
UNDERFLOW = -104.0

ADAM_LR = 0.001
ADAM_B1 = 0.9
ADAM_B2 = 0.999
ADAM_EPS = 1e-08
ADAM_WD = 0.01
ADAM_STEP = 10

HBM_SPEC = pl.BlockSpec(memory_space=pl.ANY)
VMEM_SPEC = pl.BlockSpec(memory_space=pltpu.VMEM)


def _params(**kw):
    return pltpu.CompilerParams(vmem_limit_bytes=VMEM_LIMIT, **kw)


def _tile(n, pref):
    if n <= pref:
        return n
    for t in range(pref - pref % LANES, 0, -LANES):
        if n % t == 0:
            return t
    raise ValueError((n, pref))


def _nn(a, b):
    return jnp.dot(a, b, preferred_element_type=F32)


def _nt(a, b):
    return lax.dot_general(a, b, (((1,), (1,)), ((), ())), preferred_element_type=F32)


def _tn(a, b):
    return lax.dot_general(a, b, (((0,), (0,)), ((), ())), preferred_element_type=F32)


def _rms_fwd(x, g):
    r = lax.rsqrt(jnp.mean(x * x, axis=-1, keepdims=True) + EPS)
    xhat = x * r
    return xhat * g, xhat, r


def _rms_bwd(dy, xhat, r, g):
    dxhat = dy * g
    dx = r * (dxhat - xhat * jnp.mean(dxhat * xhat, axis=-1, keepdims=True))
    return dx, dy * xhat


_GELU_K = 0.7978845608028654
_GELU_C = 0.044715


def _gelu(x):
    return 0.5 * x * (1.0 + jnp.tanh(_GELU_K * (x + _GELU_C * x * x * x)))


def _gelu_and_grad(x):
    x2 = x * x
    t = jnp.tanh(_GELU_K * (x + _GELU_C * x2 * x))
    half = 0.5 * (1.0 + t)
    return x * half, half + 0.5 * x * (1.0 - t * t) * _GELU_K * (1.0 + 3.0 * _GELU_C * x2)


def _inproj_fwd(x, g, w):
    T, D = x.shape
    N = w.shape[1]
    tt = _tile(T, 512)

    def body(x_ref, g_ref, w_ref, proj_ref, h_ref, qkv_ref):
        h, _, _ = _rms_fwd(x_ref[...], g_ref[...])
        hb = h.astype(BF16)
        h_ref[...] = hb
        p = _nn(hb, w_ref[...])
        proj_ref[...] = p[:, :QKV_OFF]
        qkv_ref[...] = p[:, QKV_OFF:].astype(BF16)

    return pl.pallas_call(
        body, name="inproj_fwd", grid=(T // tt,),
        in_specs=[pl.BlockSpec((tt, D), lambda i: (i, 0)), pl.BlockSpec((1, D), lambda i: (0, 0)),
                  pl.BlockSpec((D, N), lambda i: (0, 0))],
        out_specs=[pl.BlockSpec((tt, QKV_OFF), lambda i: (i, 0)), pl.BlockSpec((tt, D), lambda i: (i, 0)),
                   pl.BlockSpec((tt, N - QKV_OFF), lambda i: (i, 0))],
        out_shape=[jax.ShapeDtypeStruct((T, QKV_OFF), F32), jax.ShapeDtypeStruct((T, D), BF16),
                   jax.ShapeDtypeStruct((T, N - QKV_OFF), BF16)],
        compiler_params=_params(),
    )(x, g, w)


def _inproj_bwd(pieces, w, x, g, dres, hosted=None):
    T, D = x.shape
    N = w.shape[1]
    tt = _tile(T, 512)
    nt = T // tt
    widths = [p.shape[1] for p in pieces]
    offs = [sum(widths[:k]) for k in range(len(widths))]
    assert sum(widths) == N
    n_p = len(pieces)

    def body(*refs):
        i = pl.program_id(0)
        own, start, wait = _host(hosted, refs, n_p + 4, 2, i == 0, i == nt - 1)
        start()
        p_refs = own[:n_p]
        w_ref, x_ref, g_ref, dres_ref, dx_ref, dg_ref, dproj_ref = own[n_p:]
        for p_ref, o, wd in zip(p_refs, offs, widths):
            dproj_ref[:, o:o + wd] = p_ref[...].astype(BF16)
        dh = _nt(dproj_ref[...], w_ref[...])
        gv = g_ref[...]
        _, xhat, r = _rms_fwd(x_ref[...], gv)
        dx, dgrow = _rms_bwd(dh, xhat, r, gv)
        dx_ref[...] = dres_ref[...] + dx

        @pl.when(i == 0)
        def _():
            dg_ref[...] = jnp.zeros_like(dg_ref)

        dg_ref[...] += jnp.sum(dgrow, axis=0, keepdims=True)
        wait()

    h_ins = hosted.ins if hosted else []
    res = pl.pallas_call(
        body, name="inproj_bwd_hosting" if hosted else "inproj_bwd", grid=(nt,),
        in_specs=[pl.BlockSpec((tt, wd), lambda i: (i, 0)) for wd in widths] + [
            pl.BlockSpec((D, N), lambda i: (0, 0)), pl.BlockSpec((tt, D), lambda i: (i, 0)),
            pl.BlockSpec((1, D), lambda i: (0, 0)), pl.BlockSpec((tt, D), lambda i: (i, 0))] + [HBM_SPEC] * len(h_ins),
        out_specs=[pl.BlockSpec((tt, D), lambda i: (i, 0)), pl.BlockSpec((1, D), lambda i: (0, 0))]
        + [HBM_SPEC] * len(h_ins),
        out_shape=[jax.ShapeDtypeStruct((T, D), F32), jax.ShapeDtypeStruct((1, D), F32)]
        + (hosted.out_shapes if hosted else []),
        scratch_shapes=[pltpu.VMEM((tt, N), BF16)] + (hosted.sems() if hosted else []),
        compiler_params=_params(has_side_effects=hosted is not None),
    )(*pieces, w, x, g, dres, *h_ins)
    return res[:2], res[2:]


def _inproj_grad(h, pieces, hosted=None):
    T, D = h.shape
    tt = _tile(T, 1024)
    nt = T // tt
    widths = [p.shape[1] for p in pieces]
    offs = [sum(widths[:k]) for k in range(len(widths))]
    N = sum(widths)
    n_p = len(pieces)

    def body(*refs):
        t = pl.program_id(0)
        own, start, wait = _host(hosted, refs, n_p + 1, 1, t == 0, t == nt - 1)
        start()
        h_ref, p_refs, o_ref = own[0], own[1:1 + n_p], own[1 + n_p]

        @pl.when(t == 0)
        def _():
            o_ref[...] = jnp.zeros_like(o_ref)

        hv = h_ref[...]
        for p_ref, o, wd in zip(p_refs, offs, widths):
            o_ref[:, o:o + wd] += _tn(hv, p_ref[...].astype(BF16))
        wait()

    h_ins = hosted.ins if hosted else []
    res = pl.pallas_call(
        body, name="grad_w_in_hosting" if hosted else "grad_w_in", grid=(nt,),
        in_specs=[pl.BlockSpec((tt, D), lambda t: (t, 0))] + [pl.BlockSpec((tt, wd), lambda t: (t, 0)) for wd in widths]
        + [HBM_SPEC] * len(h_ins),
        out_specs=[pl.BlockSpec((None, D, N), lambda t: (0, 0, 0))] + [HBM_SPEC] * len(h_ins),
        out_shape=[jax.ShapeDtypeStruct((1, D, N), F32)] + (hosted.out_shapes if hosted else []),
        scratch_shapes=hosted.sems() if hosted else [],
        compiler_params=_params(has_side_effects=hosted is not None),
    )(h, *pieces, *h_ins)
    return res[0], res[1:]


def _pool_select(s2, s4, s8, s16, grp):
    return jnp.where(grp == 0, s2, jnp.where(grp == 1, s4, jnp.where(grp == 2, s8, s16)))


def _pool_count(t_glob, grp):
    win = jnp.where(grp == 0, 2, jnp.where(grp == 1, 4, jnp.where(grp == 2, 8, 16)))
    return jnp.minimum(t_glob + 1, win).astype(F32)


def _pool_diff(a, halo, base, tt):
    n = tt + POOL_HALO
    ext = jnp.concatenate([halo, a], axis=0)
    s2 = ext + pltpu.roll(ext, 1, 0)
    s4 = s2 + pltpu.roll(s2, 2, 0)
    s8 = s4 + pltpu.roll(s4, 4, 0)
    s16 = s8 + pltpu.roll(s8, 8, 0)
    grp = lax.broadcasted_iota(jnp.int32, (n, POOL_WIDTH), 1) // POOL_GW
    t_glob = lax.broadcasted_iota(jnp.int32, (n, POOL_WIDTH), 0) + (base - POOL_HALO)
    pooled = _pool_select(s2, s4, s8, s16, grp) / _pool_count(t_glob, grp)
    return pooled[POOL_HALO:] - a


def _pool_specs(T, tt):
    hb = tt // POOL_HALO
    return [pl.BlockSpec((tt, POOL_WIDTH), lambda i: (i, 0)),
            pl.BlockSpec((POOL_HALO, POOL_WIDTH), lambda i: (jnp.maximum(i * hb - 1, 0), 0))]


def _pool_fwd(proj, wbd, scale):
    T = proj.shape[0]
    tt = _tile(T, 512)

    def body(a_ref, halo_ref, w_ref, sc_ref, y_ref):
        i = pl.program_id(0)
        halo = jnp.where(i > 0, halo_ref[...], 0.0)
        d = _pool_diff(a_ref[...], halo, i * tt, tt)
        y_ref[...] = _nn(d.astype(BF16), w_ref[...]) * sc_ref[...]

    return pl.pallas_call(
        body, name="pool_fwd", grid=(T // tt,),
        in_specs=_pool_specs(T, tt) + [pl.BlockSpec((POOL_WIDTH, POOL_WIDTH), lambda i: (0, 0)),
                                       pl.BlockSpec((1, POOL_WIDTH), lambda i: (0, 0))],
        out_specs=pl.BlockSpec((tt, POOL_WIDTH), lambda i: (i, 0)),
        out_shape=jax.ShapeDtypeStruct((T, POOL_WIDTH), F32),
        compiler_params=_params(),
    )(proj, proj, wbd, scale)


def _pool_bwd(proj, dymix, wbd, scale):
    T = proj.shape[0]
    tt = _tile(T, 512)
    hb = tt // POOL_HALO
    nblk = T // tt
    n = tt + POOL_HALO

    def body(a_ref, halo_ref, dy_ref, dyn_ref, w_ref, sc_ref, da_ref, dw_ref, dsc_ref):
        i = pl.program_id(0)
        halo = jnp.where(i > 0, halo_ref[...], 0.0)
        d = _pool_diff(a_ref[...], halo, i * tt, tt)
        db = d.astype(BF16)
        wv = w_ref[...]
        sc = sc_ref[...]
        dy = dy_ref[...]
        dys = dy * sc

        @pl.when(i == 0)
        def _():
            dw_ref[...] = jnp.zeros_like(dw_ref)
            dsc_ref[...] = jnp.zeros_like(dsc_ref)

        dsc_ref[...] += jnp.sum(dy * _nn(db, wv), axis=0, keepdims=True)
        dw_ref[...] += _tn(db, dys.astype(BF16))
        dyn = jnp.where(i < nblk - 1, dyn_ref[...], 0.0) * sc
        dd = _nt(jnp.concatenate([dys, dyn], axis=0).astype(BF16), wv)
        grp = lax.broadcasted_iota(jnp.int32, (n, POOL_WIDTH), 1) // POOL_GW
        t_glob = lax.broadcasted_iota(jnp.int32, (n, POOL_WIDTH), 0) + i * tt
        e = dd / _pool_count(t_glob, grp)
        r2 = e + pltpu.roll(e, n - 1, 0)
        r4 = r2 + pltpu.roll(r2, n - 2, 0)
        r8 = r4 + pltpu.roll(r4, n - 4, 0)
        r16 = r8 + pltpu.roll(r8, n - 8, 0)
        da_ref[...] = (_pool_select(r2, r4, r8, r16, grp) - dd)[:tt].astype(BF16)

    return pl.pallas_call(
        body, name="pool_bwd", grid=(nblk,),
        in_specs=_pool_specs(T, tt) + [
            pl.BlockSpec((tt, POOL_WIDTH), lambda i: (i, 0)),
            pl.BlockSpec((POOL_HALO, POOL_WIDTH), lambda i: (jnp.minimum((i + 1) * hb, T // POOL_HALO - 1), 0)),
            pl.BlockSpec((POOL_WIDTH, POOL_WIDTH), lambda i: (0, 0)), pl.BlockSpec((1, POOL_WIDTH), lambda i: (0, 0))],
        out_specs=[pl.BlockSpec((tt, POOL_WIDTH), lambda i: (i, 0)),
                   pl.BlockSpec((POOL_WIDTH, POOL_WIDTH), lambda i: (0, 0)),
                   pl.BlockSpec((1, POOL_WIDTH), lambda i: (0, 0))],
        out_shape=[jax.ShapeDtypeStruct((T, POOL_WIDTH), BF16),
                   jax.ShapeDtypeStruct((POOL_WIDTH, POOL_WIDTH), F32),
                   jax.ShapeDtypeStruct((1, POOL_WIDTH), F32)],
        compiler_params=_params(),
    )(proj, proj, dymix, dymix, wbd, scale)


def _head_select(stacked, grp):
    out = jnp.where(grp == 0, stacked[0:CHUNK], 0.0)
    for h in range(1, SG_HEADS):
        out = out + jnp.where(grp == h, stacked[h * CHUNK:(h + 1) * CHUNK], 0.0)
    return out


def _sg_specs(tt):
    return [pl.BlockSpec((tt, SG_WIDTH), lambda i: (i, 1)), pl.BlockSpec((tt, SG_WIDTH), lambda i: (i, 2))]


def _sg_fwd(proj, wm, bias, g):
    T = proj.shape[0]
    tt = _tile(T, 512)

    def body(u_ref, v_ref, wm_ref, b_ref, g_ref, y_ref):
        zu = _gelu(u_ref[...])
        vn, _, _ = _rms_fwd(_gelu(v_ref[...]), g_ref[...])
        grp = lax.broadcasted_iota(jnp.int32, (CHUNK, SG_WIDTH), 1) // SB_HD
        for n in range(tt // CHUNK):
            rows = slice(n * CHUNK, (n + 1) * CHUNK)
            sv = _head_select(_nn(wm_ref[...], vn[rows].astype(BF16)), grp) + b_ref[...]
            y_ref[rows, :] = zu[rows] * sv

    return pl.pallas_call(
        body, name="sg_fwd", grid=(T // tt,),
        in_specs=_sg_specs(tt) + [pl.BlockSpec((SG_HEADS * CHUNK, CHUNK), lambda i: (0, 0)),
                                  pl.BlockSpec((CHUNK, SG_WIDTH), lambda i: (0, 0)),
                                  pl.BlockSpec((1, SG_WIDTH), lambda i: (0, 0))],
        out_specs=pl.BlockSpec((tt, SG_WIDTH), lambda i: (i, 0)),
        out_shape=jax.ShapeDtypeStruct((T, SG_WIDTH), F32),
        compiler_params=_params(),
    )(proj, proj, wm, bias, g)


def _sg_bwd(proj, dymix, wm, wmt, bias, g, hosted=None):
    T = proj.shape[0]
    tt = _tile(T, 512)
    nblk = T // tt

    def body(*refs):
        i = pl.program_id(0)
        (u_ref, v_ref, dy_ref, wm_ref, wmt_ref, b_ref, g_ref, du_ref, dv_ref, dw_ref, db_ref, dg_ref,
         dvn_ref, dbias_ref), start, wait = _host(hosted, refs, 7, 5, i == 0, i == nblk - 1)
        start()
        up, vp = u_ref[...], v_ref[...]
        gv = g_ref[...]
        (zu, gu), (zv, gvp) = _gelu_and_grad(up), _gelu_and_grad(vp)
        vn, xhat, r = _rms_fwd(zv, gv)
        grp = lax.broadcasted_iota(jnp.int32, (CHUNK, SG_WIDTH), 1) // SB_HD

        @pl.when(i == 0)
        def _():
            dw_ref[...] = jnp.zeros_like(dw_ref)
            dbias_ref[...] = jnp.zeros_like(dbias_ref)
            dg_ref[...] = jnp.zeros_like(dg_ref)

        for n in range(tt // CHUNK):
            rows = slice(n * CHUNK, (n + 1) * CHUNK)
            vc = vn[rows].astype(BF16)
            sv = _head_select(_nn(wm_ref[...], vc), grp) + b_ref[...]
            dy = dy_ref[rows, :]
            du_ref[rows, :] = (dy * sv * gu[rows]).astype(BF16)
            dsv = dy * zu[rows]
            dsvb = dsv.astype(BF16)
            dvn_ref[rows, :] = _head_select(_nn(wmt_ref[...], dsvb), grp)
            stacked = jnp.concatenate([jnp.where(grp == h, dsv, 0.0) for h in range(SG_HEADS)], axis=0)
            dw_ref[...] += _nt(stacked.astype(BF16), vc)
            dbias_ref[...] += dsv

        dzv, dgrow = _rms_bwd(dvn_ref[...], xhat, r, gv)
        dg_ref[...] += jnp.sum(dgrow, axis=0, keepdims=True)
        dv_ref[...] = (dzv * gvp).astype(BF16)

        @pl.when(i == nblk - 1)
        def _():
            t_i = lax.broadcasted_iota(jnp.int32, (SG_HEADS * CHUNK, CHUNK), 0) % CHUNK
            s_i = lax.broadcasted_iota(jnp.int32, (SG_HEADS * CHUNK, CHUNK), 1)
            dw_ref[...] = jnp.where(s_i <= t_i, dw_ref[...], 0.0)
            lane = lax.broadcasted_iota(jnp.int32, (CHUNK, LANES), 1)
            acc = jnp.zeros((CHUNK, LANES), F32)
            for h in range(SG_HEADS):
                tot = jnp.sum(jnp.where(grp == h, dbias_ref[...], 0.0), axis=1, keepdims=True)
                acc = acc + jnp.where(lane == h, tot, 0.0)
            db_ref[...] = acc

        wait()

    h_ins = hosted.ins if hosted else []
    res = pl.pallas_call(
        body, name="sg_bwd_hosting" if hosted else "sg_bwd", grid=(nblk,),
        in_specs=_sg_specs(tt) + [pl.BlockSpec((tt, SG_WIDTH), lambda i: (i, 1)),
                                  pl.BlockSpec((SG_HEADS * CHUNK, CHUNK), lambda i: (0, 0)),
                                  pl.BlockSpec((SG_HEADS * CHUNK, CHUNK), lambda i: (0, 0)),
                                  pl.BlockSpec((CHUNK, SG_WIDTH), lambda i: (0, 0)),
                                  pl.BlockSpec((1, SG_WIDTH), lambda i: (0, 0))] + [HBM_SPEC] * len(h_ins),
        out_specs=[pl.BlockSpec((tt, SG_WIDTH), lambda i: (i, 0)), pl.BlockSpec((tt, SG_WIDTH), lambda i: (i, 0)),
                   pl.BlockSpec((SG_HEADS * CHUNK, CHUNK), lambda i: (0, 0)),
                   pl.BlockSpec((CHUNK, LANES), lambda i: (0, 0)), pl.BlockSpec((1, SG_WIDTH), lambda i: (0, 0))]
        + [HBM_SPEC] * len(h_ins),
        out_shape=[jax.ShapeDtypeStruct((T, SG_WIDTH), BF16), jax.ShapeDtypeStruct((T, SG_WIDTH), BF16),
                   jax.ShapeDtypeStruct((SG_HEADS * CHUNK, CHUNK), F32),
                   jax.ShapeDtypeStruct((CHUNK, LANES), F32), jax.ShapeDtypeStruct((1, SG_WIDTH), F32)]
        + (hosted.out_shapes if hosted else []),
        scratch_shapes=[pltpu.VMEM((tt, SG_WIDTH), F32), pltpu.VMEM((CHUNK, SG_WIDTH), F32)]
        + (hosted.sems() if hosted else []),
        compiler_params=_params(has_side_effects=hosted is not None),
    )(proj, proj, dymix, wm, wmt, bias, g, *h_ins)
    return res[:5], res[5:]


def _split_dot(x, u):
    hi = x.astype(BF16)
    lo = (x - hi.astype(F32)).astype(BF16)
    return _nn(hi, u) + _nn(lo, u)


def _sb_logits(z):
    lb = jnp.minimum(z, 0.0) - jnp.log(1.0 + jnp.exp(-jnp.abs(z)))
    return lb, lb - z


ATTN_STRIP = 256
ATTN_SUBS = 4


def _by_strips(n_rows, fn):
    parts = None
    for r in range(0, n_rows, ATTN_STRIP):
        res = fn(slice(r, r + ATTN_STRIP))
        parts = [[v] for v in res] if parts is None else [p + [v] for p, v in zip(parts, res)]
    return [jnp.concatenate(p, axis=0) for p in parts]


def _attn_qkv_specs(tq, T):
    base = (IN_COLS - 3 * SB_WIDTH - QKV_OFF) // LANES
    nb = SB_WIDTH // LANES
    return [pl.BlockSpec((tq, LANES), lambda p, i: (i, base + p)),
            pl.BlockSpec((T, LANES), lambda p, i: (0, base + nb + p)),
            pl.BlockSpec((T, LANES), lambda p, i: (0, base + 2 * nb + p))]


class _Hosted:
    def __init__(self, ins, out_shapes, n_sems, copies, in_place=False):
        self.ins, self.out_shapes, self.n_sems, self.copies = ins, out_shapes, n_sems, copies
        self.in_place = in_place

    @property
    def n(self):
        return len(self.ins)

    def aliases(self, n_in, n_out):
        return {n_in + k: n_out + k for k in range(self.n)} if self.in_place else {}

    def sems(self):
        return [pltpu.SemaphoreType.DMA((self.n_sems,)), pltpu.SemaphoreType.DMA((self.n_sems,))]

    def start(self, src, dst, ssem, rsem):
        for send, _ in self.copies(src, dst, ssem, rsem):
            send.start()

    def wait(self, src, dst, ssem, rsem):
        for send, recv in self.copies(src, dst, ssem, rsem):
            recv.wait_recv()
            send.wait_send()


def _host(hosted, refs, n_in, n_out, first, last):
    if hosted is None:
        return refs, lambda: None, lambda: None
    n = hosted.n
    own_in, h_in = refs[:n_in], refs[n_in:n_in + n]
    own_out, h_out = refs[n_in + n:n_in + n + n_out], refs[n_in + n + n_out:n_in + 2 * n + n_out]
    rest = refs[n_in + 2 * n + n_out:]
    ssem, rsem = rest[-2:]

    def start():
        if first is True:
            hosted.start(h_in, h_out, ssem, rsem)
        else:
            pl.when(first)(lambda: hosted.start(h_in, h_out, ssem, rsem))

    def wait():
        if last is True:
            hosted.wait(h_in, h_out, ssem, rsem)
        else:
            pl.when(last)(lambda: hosted.wait(h_in, h_out, ssem, rsem))

    return own_in + own_out + rest[:-2], start, wait


def _attn_fwd(qkv, hosted=None):
    T = qkv.shape[0]
    tk = _tile(T, ATTN_TILE)
    n_sub = ATTN_SUBS if T % (ATTN_SUBS * tk) == 0 else 1
    tq = n_sub * tk
    n_p, nq = SB_WIDTH // LANES, T // tq

    def body(*refs):
        p, i = pl.program_id(0), pl.program_id(1)
        (q_ref, k_ref, v_ref, o_ref), start, wait = _host(
            hosted, refs, 3, 1, jnp.logical_and(p == 0, i == 0), jnp.logical_and(p == n_p - 1, i == nq - 1))
        start()
        lane = lax.broadcasted_iota(jnp.int32, (tk, LANES), 1)
        row = lax.broadcasted_iota(jnp.int32, (tk, tk), 0)
        col = lax.broadcasted_iota(jnp.int32, (tk, tk), 1)
        after = jnp.where(row > col, 1.0, 0.0).astype(BF16)
        valid = col < row
        qh = {}
        for sb in range(n_sub):
            q = q_ref[sb * tk:(sb + 1) * tk, :].astype(F32)
            for hh in range(2):
                qh[(sb, hh)] = jnp.where((lane // SB_HD) == hh, q * SB_SCALE, 0.0).astype(BF16)

        def tiles(todo, state):
            chains = [(n, hh) for n in range(len(todo)) for hh in range(2)]
            kv = []
            for _, j, _ in todo:
                ks = pl.ds(pl.multiple_of(j * tk, tk), tk)
                kv.append((k_ref[ks, :], v_ref[ks, :]))
            z = {(n, hh): _nt(qh[(todo[n][0], hh)], kv[n][0]) for n, hh in chains}
            lb, lmb, lm_sum = {}, {}, {}
            for n, hh in chains:
                def logits(rows, z=z[(n, hh)], mask=todo[n][2]):
                    lb, lm = _sb_logits(z[rows])
                    if mask is not None:
                        lm = jnp.where(mask[rows], lm, 0.0)
                    return lb, lm.astype(BF16), jnp.sum(lm, axis=1, keepdims=True)

                lb[(n, hh)], lmb[(n, hh)], lm_sum[(n, hh)] = _by_strips(tk, logits)
            x = {c: _nn(lmb[c], after) for c in chains}
            new = dict(state)
            for n, hh in chains:
                key = (todo[n][0], hh)
                carry, acc = new[key]

                def weights(rows, lb=lb[(n, hh)], x=x[(n, hh)], carry=carry, mask=todo[n][2]):
                    a = jnp.exp(lb[rows] + x[rows] + carry[rows])
                    if mask is not None:
                        a = jnp.where(mask[rows], a, 0.0)
                    return (a.astype(BF16),)

                (ab,) = _by_strips(tk, weights)
                new[key] = (carry + lm_sum[(n, hh)], acc + _nn(ab, kv[n][1]))
            return new

        def live(state, sb):
            return jnp.maximum(jnp.max(state[(sb, 0)][0]), jnp.max(state[(sb, 1)][0]))

        first = n_sub * i
        zero = (jnp.zeros((tk, 1), F32), jnp.zeros((tk, LANES), F32))
        todo = []
        for sb in range(n_sub):
            gate = jnp.broadcast_to(first > 0, (tk, tk)) if sb == 0 else None
            todo += [(sb, first + sb, valid), (sb, jnp.maximum(first + sb - 1, 0), gate)]
        state = tiles(todo, {(sb, hh): zero for sb in range(n_sub) for hh in range(2)})
        for sb in range(n_sub):
            def cond(st):
                return jnp.logical_and(st[0] >= 0, st[2] > UNDERFLOW)

            def step(st, sb=sb):
                mine = tiles([(sb, st[0], None)], st[1])
                return st[0] - 1, mine, live(mine, sb)

            mine = {k: v for k, v in state.items() if k[0] == sb}
            _, mine, _ = lax.while_loop(cond, step, (first + sb - 2, mine, live(mine, sb)))
            o_ref[sb * tk:(sb + 1) * tk, :] = jnp.where(lane < SB_HD, mine[(sb, 0)][1], mine[(sb, 1)][1])
        wait()

    h_ins = hosted.ins if hosted else []
    res = pl.pallas_call(
        body, name="attn_fwd_hosting" if hosted else "attn_fwd", grid=(n_p, nq),
        in_specs=_attn_qkv_specs(tq, T) + [HBM_SPEC] * len(h_ins),
        out_specs=[pl.BlockSpec((tq, LANES), lambda p, i: (i, p))] + [HBM_SPEC] * len(h_ins),
        out_shape=[jax.ShapeDtypeStruct((T, SB_WIDTH), F32)] + (hosted.out_shapes if hosted else []),
        input_output_aliases=hosted.aliases(3, 1) if hosted else {},
        scratch_shapes=hosted.sems() if hosted else [],
        compiler_params=_params(has_side_effects=hosted is not None),
    )(qkv, qkv, qkv, *h_ins)
    return res[0], res[1:]


def _attn_bwd(qkv, o, dymix, hosted=None):
    T = qkv.shape[0]
    tk = _tile(T, ATTN_TILE)
    n_sub = ATTN_SUBS if T % (ATTN_SUBS * tk) == 0 else 1
    tq = n_sub * tk
    n_p, nq = SB_WIDTH // LANES, T // tq
    yc_blk = (POOL_WIDTH + SG_WIDTH) // LANES

    def body(*refs):
        p, i = pl.program_id(0), pl.program_id(1)
        (q_ref, k_ref, v_ref, o_ref, do_ref, dq_ref, dk_ref, dv_ref), start, wait = _host(
            hosted, refs, 5, 3, jnp.logical_and(p == 0, i == 0), jnp.logical_and(p == n_p - 1, i == nq - 1))
        start()
        lane = lax.broadcasted_iota(jnp.int32, (tk, LANES), 1)
        row = lax.broadcasted_iota(jnp.int32, (tk, tk), 0)
        col = lax.broadcasted_iota(jnp.int32, (tk, tk), 1)
        after = jnp.where(row > col, 1.0, 0.0).astype(BF16)
        from_here = jnp.where(row >= col, 1.0, 0.0).astype(BF16)
        from_here2 = jnp.concatenate([from_here, from_here], axis=0)
        valid = col < row

        @pl.when(i == 0)
        def _():
            dk_ref[...] = jnp.zeros_like(dk_ref)
            dv_ref[...] = jnp.zeros_like(dv_ref)

        qh, dohb, delta = {}, {}, {}
        for sb in range(n_sub):
            rows = slice(sb * tk, (sb + 1) * tk)
            q, ov, dov = q_ref[rows, :].astype(F32), o_ref[rows, :], do_ref[rows, :]
            for hh in range(2):
                head = (lane // SB_HD) == hh
                qh[(sb, hh)] = jnp.where(head, q * SB_SCALE, 0.0).astype(BF16)
                dohb[(sb, hh)] = jnp.where(head, dov, 0.0).astype(BF16)
                delta[(sb, hh)] = jnp.sum(dohb[(sb, hh)].astype(F32) * ov, axis=1, keepdims=True)

        def tiles(todo, state):
            chains = [(n, hh) for n in range(len(todo)) for hh in range(2)]
            kv, where = [], []
            for _, j, _ in todo:
                ks = pl.ds(pl.multiple_of(j * tk, tk), tk)
                where.append(ks)
                kv.append((k_ref[ks, :], v_ref[ks, :]))
            z = {(n, hh): _nt(qh[(todo[n][0], hh)], kv[n][0]) for n, hh in chains}
            da = {(n, hh): _nt(dohb[(todo[n][0], hh)], kv[n][1]) for n, hh in chains}
            lb, lmb, lm_sum = {}, {}, {}
            for n, hh in chains:
                def logits(rows, z=z[(n, hh)], mask=todo[n][2]):
                    lb, lm = _sb_logits(z[rows])
                    if mask is not None:
                        lm = jnp.where(mask[rows], lm, 0.0)
                    return lb, lm.astype(BF16), jnp.sum(lm, axis=1, keepdims=True)

                lb[(n, hh)], lmb[(n, hh)], lm_sum[(n, hh)] = _by_strips(tk, logits)
            x = {c: _nn(lmb[c], after) for c in chains}
            c_a = {k: v[0] for k, v in state.items()}
            ab, g, g_split, g_sum = {}, {}, {}, {}
            for n, hh in chains:
                key = (todo[n][0], hh)

                def weights(rows, lb=lb[(n, hh)], x=x[(n, hh)], da=da[(n, hh)], c_a=c_a[key], mask=todo[n][2]):
                    a = jnp.exp(lb[rows] + x[rows] + c_a[rows])
                    if mask is not None:
                        a = jnp.where(mask[rows], a, 0.0)
                    ab = a.astype(BF16)
                    g = da[rows] * ab.astype(F32)
                    hi = g.astype(BF16)
                    lo = (g - hi.astype(F32)).astype(BF16)
                    return ab, g, jnp.concatenate([hi, lo], axis=1), jnp.sum(g, axis=1, keepdims=True)

                ab[(n, hh)], g[(n, hh)], g_split[(n, hh)], g_sum[(n, hh)] = _by_strips(tk, weights)
                c_a[key] = c_a[key] + lm_sum[(n, hh)]
            right = {c: _nn(g_split[c], from_here2) for c in chains}
            c_r = {k: v[1] for k, v in state.items()}
            dzb = {}
            for n, hh in chains:
                key = (todo[n][0], hh)

                def logit_grads(rows, lb=lb[(n, hh)], g=g[(n, hh)], right=right[(n, hh)], c_r=c_r[key],
                                delta=delta[key], mask=todo[n][2]):
                    sig = jnp.exp(lb[rows])
                    left = delta[rows] - (c_r[rows] + right[rows])
                    dz = g[rows] * (1.0 - sig) - left * sig
                    if mask is not None:
                        dz = jnp.where(mask[rows], dz, 0.0)
                    return (dz.astype(BF16),)

                (dzb[(n, hh)],) = _by_strips(tk, logit_grads)
                c_r[key] = c_r[key] + g_sum[(n, hh)]
            dqa = {k: v[2] for k, v in state.items()}
            for n in range(len(todo)):
                sb = todo[n][0]
                dk_ref[where[n], :] += _tn(dzb[(n, 0)], qh[(sb, 0)]) + _tn(dzb[(n, 1)], qh[(sb, 1)])
                dv_ref[where[n], :] += _tn(ab[(n, 0)], dohb[(sb, 0)]) + _tn(ab[(n, 1)], dohb[(sb, 1)])
                for hh in range(2):
                    dqa[(sb, hh)] = dqa[(sb, hh)] + _nn(dzb[(n, hh)], kv[n][0])
            return {k: (c_a[k], c_r[k], dqa[k]) for k in state}

        def live(state, sb):
            return jnp.maximum(jnp.max(state[(sb, 0)][0]), jnp.max(state[(sb, 1)][0]))

        first = n_sub * i
        zero = (jnp.zeros((tk, 1), F32), jnp.zeros((tk, 1), F32), jnp.zeros((tk, LANES), F32))
        todo = []
        for sb in range(n_sub):
            gate = jnp.broadcast_to(first > 0, (tk, tk)) if sb == 0 else None
            todo += [(sb, first + sb, valid), (sb, jnp.maximum(first + sb - 1, 0), gate)]
        state = tiles(todo, {(sb, hh): zero for sb in range(n_sub) for hh in range(2)})
        for sb in range(n_sub):
            def cond(st):
                return jnp.logical_and(st[0] >= 0, st[2] > UNDERFLOW)

            def step(st, sb=sb):
                mine = tiles([(sb, st[0], None)], st[1])
                return st[0] - 1, mine, live(mine, sb)

            mine = {k: v for k, v in state.items() if k[0] == sb}
            _, mine, _ = lax.while_loop(cond, step, (first + sb - 2, mine, live(mine, sb)))
            dq_ref[sb * tk:(sb + 1) * tk, :] = (
                jnp.where(lane < SB_HD, mine[(sb, 0)][2], mine[(sb, 1)][2]) * SB_SCALE).astype(BF16)
        wait()

    h_ins = hosted.ins if hosted else []
    res = pl.pallas_call(
        body, name="attn_bwd_hosting" if hosted else "attn_bwd", grid=(n_p, nq),
        in_specs=_attn_qkv_specs(tq, T) + [pl.BlockSpec((tq, LANES), lambda p, i: (i, p)),
                                           pl.BlockSpec((tq, LANES), lambda p, i: (i, yc_blk + p))]
        + [HBM_SPEC] * len(h_ins),
        out_specs=[pl.BlockSpec((tq, LANES), lambda p, i: (i, p)), pl.BlockSpec((T, LANES), lambda p, i: (0, p)),
                   pl.BlockSpec((T, LANES), lambda p, i: (0, p))] + [HBM_SPEC] * len(h_ins),
        out_shape=[jax.ShapeDtypeStruct((T, SB_WIDTH), BF16)] + [jax.ShapeDtypeStruct((T, SB_WIDTH), F32)] * 2
        + (hosted.out_shapes if hosted else []),
        scratch_shapes=hosted.sems() if hosted else [],
        compiler_params=_params(has_side_effects=hosted is not None),
    )(qkv, qkv, qkv, o, dymix, *h_ins)
    return res[:3], res[3:]


def _outproj_fwd(x, ya, yb, yc, w, hosted=None):
    T, D = x.shape
    tt = _tile(T, 512)
    nt = T // tt

    def body(*refs):
        i = pl.program_id(0)
        (x_ref, ya_ref, yb_ref, yc_ref, w_ref, x1_ref, ymix_ref), start, wait = _host(
            hosted, refs, 5, 2, i == 0, i == nt - 1)
        start()
        ymix_ref[:, 0:POOL_WIDTH] = ya_ref[...].astype(BF16)
        ymix_ref[:, POOL_WIDTH:POOL_WIDTH + SG_WIDTH] = yb_ref[...].astype(BF16)
        ymix_ref[:, POOL_WIDTH + SG_WIDTH:] = yc_ref[...].astype(BF16)
        x1_ref[...] = x_ref[...] + _nn(ymix_ref[...], w_ref[...])
        wait()

    row = lambda width: pl.BlockSpec((tt, width), lambda i: (i, 0))
    h_ins = hosted.ins if hosted else []
    res = pl.pallas_call(
        body, name="outproj_fwd_hosting" if hosted else "outproj_fwd", grid=(nt,),
        in_specs=[row(D), row(POOL_WIDTH), row(SG_WIDTH), row(SB_WIDTH), pl.BlockSpec((D, D), lambda i: (0, 0))]
        + [HBM_SPEC] * len(h_ins),
        out_specs=[row(D), row(D)] + [HBM_SPEC] * len(h_ins),
        out_shape=[jax.ShapeDtypeStruct((T, D), F32), jax.ShapeDtypeStruct((T, D), BF16)]
        + (hosted.out_shapes if hosted else []),
        input_output_aliases=hosted.aliases(5, 2) if hosted else {},
        scratch_shapes=hosted.sems() if hosted else [],
        compiler_params=_params(has_side_effects=hosted is not None),
    )(x, ya, yb, yc, w, *h_ins)
    return res[:2], res[2:]


def _nt_matmul(a, w):
    T, N = a.shape
    K = w.shape[0]
    tt = _tile(T, 512)

    def body(a_ref, w_ref, o_ref):
        o_ref[...] = _nt(a_ref[...].astype(BF16), w_ref[...])

    return pl.pallas_call(
        body, name="nt_matmul", grid=(T // tt,),
        in_specs=[pl.BlockSpec((tt, N), lambda i: (i, 0)), pl.BlockSpec((K, N), lambda i: (0, 0))],
        out_specs=pl.BlockSpec((tt, K), lambda i: (i, 0)),
        out_shape=jax.ShapeDtypeStruct((T, K), F32),
        compiler_params=_params(),
    )(a, w)


def _tn_matmul(a, b, name, n_split=1, hosted=None):
    T, K = a.shape
    N = b.shape[1]
    tk = _tile(K, 1024)
    tn = _tile(N // n_split, 1024)
    tt = _tile(T, 2048)
    nper = N // n_split // tn
    nk, nn, nt = K // tk, N // tn, T // tt

    def body(*refs):
        k, n, t = pl.program_id(0), pl.program_id(1), pl.program_id(2)
        (a_ref, b_ref, o_ref), start, wait = _host(
            hosted, refs, 2, 1, jnp.logical_and(jnp.logical_and(k == 0, n == 0), t == 0),
            jnp.logical_and(jnp.logical_and(k == nk - 1, n == nn - 1), t == nt - 1))
        start()

        @pl.when(t == 0)
        def _():
            o_ref[...] = jnp.zeros_like(o_ref)

        o_ref[...] += _tn(a_ref[...], b_ref[...].astype(BF16))
        wait()

    h_ins = hosted.ins if hosted else []
    res = pl.pallas_call(
        body, name=name + "_hosting" if hosted else name, grid=(nk, nn, nt),
        in_specs=[pl.BlockSpec((tt, tk), lambda k, n, t: (t, k)), pl.BlockSpec((tt, tn), lambda k, n, t: (t, n))]
        + [HBM_SPEC] * len(h_ins),
        out_specs=[pl.BlockSpec((None, tk, tn), lambda k, n, t: (n // nper, k, n % nper))] + [HBM_SPEC] * len(h_ins),
        out_shape=[jax.ShapeDtypeStruct((n_split, K, N // n_split), F32)] + (hosted.out_shapes if hosted else []),
        scratch_shapes=hosted.sems() if hosted else [],
        compiler_params=_params(has_side_effects=hosted is not None),
    )(a, b, *h_ins)
    return (res[0], res[1:]) if hosted else res[0]


def _mlp_fwd(x, g, w_up, w_down, hosted=None):
    T, D = x.shape
    n_blk, _, width = w_up.shape
    F = n_blk * width
    tt = _tile(T, 1024)
    fc = _tile(width, MLP_CHUNK)
    per = width // fc
    nc = F // fc
    nt = T // tt

    def body(*refs):
        i, c = pl.program_id(0), pl.program_id(1)
        (x_ref, g_ref, wu_ref, wd_ref, y_ref, h_ref, u_ref, a_ref), start, wait = _host(
            hosted, refs, 4, 4, jnp.logical_and(i == 0, c == 0), jnp.logical_and(i == nt - 1, c == nc - 1))
        start()

        @pl.when(c == 0)
        def _():
            xv = x_ref[...]
            h, _, _ = _rms_fwd(xv, g_ref[...])
            h_ref[...] = h.astype(BF16)
            y_ref[...] = xv

        u = _nn(h_ref[...], wu_ref[...])
        u_ref[...] = u.astype(BF16)
        a = jnp.square(jnp.maximum(u, 0.0)).astype(BF16)
        a_ref[...] = a
        y_ref[...] += _nn(a, wd_ref[...])
        wait()

    h_ins = hosted.ins if hosted else []
    res = pl.pallas_call(
        body, name="mlp_fwd_hosting" if hosted else "mlp_fwd", grid=(nt, nc),
        in_specs=[pl.BlockSpec((tt, D), lambda i, c: (i, 0)), pl.BlockSpec((1, D), lambda i, c: (0, 0)),
                  pl.BlockSpec((None, D, fc), lambda i, c: (c // per, 0, c % per)),
                  pl.BlockSpec((fc, D), lambda i, c: (c, 0))]
        + [HBM_SPEC] * len(h_ins),
        out_specs=[pl.BlockSpec((tt, D), lambda i, c: (i, 0)), pl.BlockSpec((tt, D), lambda i, c: (i, 0)),
                   pl.BlockSpec((tt, fc), lambda i, c: (i, c)), pl.BlockSpec((tt, fc), lambda i, c: (i, c))]
        + [HBM_SPEC] * len(h_ins),
        out_shape=[jax.ShapeDtypeStruct((T, D), F32), jax.ShapeDtypeStruct((T, D), BF16),
                   jax.ShapeDtypeStruct((T, F), BF16), jax.ShapeDtypeStruct((T, F), BF16)]
        + (hosted.out_shapes if hosted else []),
        scratch_shapes=hosted.sems() if hosted else [],
        compiler_params=_params(has_side_effects=hosted is not None),
    )(x, g, w_up, w_down, *h_ins)
    return res[:4], res[4:]


def _mlp_bwd(dy, x, g, u, w_up, w_down, hosted=None):
    T, D = x.shape
    n_blk, _, width = w_up.shape
    F = n_blk * width
    tt = _tile(T, 1024)
    fc = _tile(width, MLP_CHUNK)
    per = width // fc
    nc = F // fc
    nt = T // tt

    def body(*refs):
        i, c = pl.program_id(0), pl.program_id(1)
        (dy_ref, x_ref, g_ref, u_ref, wu_ref, wd_ref, dx_ref, du_ref, dg_ref, dyb_ref, dh_ref), start, wait = _host(
            hosted, refs, 6, 3, jnp.logical_and(i == 0, c == 0), jnp.logical_and(i == nt - 1, c == nc - 1))
        start()

        @pl.when(c == 0)
        def _():
            dyb_ref[...] = dy_ref[...].astype(BF16)
            dh_ref[...] = jnp.zeros_like(dh_ref)

        @pl.when(jnp.logical_and(i == 0, c == 0))
        def _():
            dg_ref[...] = jnp.zeros_like(dg_ref)

        da = _nt(dyb_ref[...], wd_ref[...])
        du = (da * (2.0 * jnp.maximum(u_ref[...].astype(F32), 0.0))).astype(BF16)
        du_ref[...] = du
        dh_ref[...] += _nt(du, wu_ref[...])

        @pl.when(c == nc - 1)
        def _():
            gv = g_ref[...]
            _, xhat, r = _rms_fwd(x_ref[...], gv)
            dx, dgrow = _rms_bwd(dh_ref[...], xhat, r, gv)
            dx_ref[...] = dy_ref[...] + dx
            dg_ref[...] += jnp.sum(dgrow, axis=0, keepdims=True)

        wait()

    h_ins = hosted.ins if hosted else []
    res = pl.pallas_call(
        body, name="mlp_bwd_hosting" if hosted else "mlp_bwd", grid=(nt, nc),
        in_specs=[pl.BlockSpec((tt, D), lambda i, c: (i, 0)), pl.BlockSpec((tt, D), lambda i, c: (i, 0)),
                  pl.BlockSpec((1, D), lambda i, c: (0, 0)), pl.BlockSpec((tt, fc), lambda i, c: (i, c)),
                  pl.BlockSpec((None, D, fc), lambda i, c: (c // per, 0, c % per)),
                  pl.BlockSpec((fc, D), lambda i, c: (c, 0))]
        + [HBM_SPEC] * len(h_ins),
        out_specs=[pl.BlockSpec((tt, D), lambda i, c: (i, 0)), pl.BlockSpec((tt, fc), lambda i, c: (i, c)),
                   pl.BlockSpec((1, D), lambda i, c: (0, 0))] + [HBM_SPEC] * len(h_ins),
        out_shape=[jax.ShapeDtypeStruct((T, D), F32), jax.ShapeDtypeStruct((T, F), BF16),
                   jax.ShapeDtypeStruct((1, D), F32)] + (hosted.out_shapes if hosted else []),
        scratch_shapes=[pltpu.VMEM((tt, D), BF16), pltpu.VMEM((tt, D), F32)] + (hosted.sems() if hosted else []),
        compiler_params=_params(has_side_effects=hosted is not None),
    )(dy, x, g, u, w_up, w_down, *h_ins)
    return res[:3], res[3:]


def _loss_head(x, g, target):
    T, D = x.shape
    tt = _tile(T, 512)

    def body(x_ref, g_ref, t_ref, loss_ref, dx_ref, dg_ref):
        gv = g_ref[...]
        y, xhat, r = _rms_fwd(x_ref[...], gv)
        err = y - t_ref[...]
        dx, dgrow = _rms_bwd(err * (1.0 / D), xhat, r, gv)
        dx_ref[...] = dx

        @pl.when(pl.program_id(0) == 0)
        def _():
            loss_ref[...] = jnp.zeros_like(loss_ref)
            dg_ref[...] = jnp.zeros_like(dg_ref)

        loss_ref[...] += 0.5 * jnp.sum(jnp.mean(err * err, axis=-1, keepdims=True), axis=0, keepdims=True)
        dg_ref[...] += jnp.sum(dgrow, axis=0, keepdims=True)

    return pl.pallas_call(
        body, name="loss_head", grid=(T // tt,),
        in_specs=[pl.BlockSpec((tt, D), lambda i: (i, 0)), pl.BlockSpec((1, D), lambda i: (0, 0)),
                  pl.BlockSpec((tt, D), lambda i: (i, 0))],
        out_specs=[pl.BlockSpec((1, LANES), lambda i: (0, 0)), pl.BlockSpec((tt, D), lambda i: (i, 0)),
                   pl.BlockSpec((1, D), lambda i: (0, 0))],
        out_shape=[jax.ShapeDtypeStruct((1, LANES), F32), jax.ShapeDtypeStruct((T, D), F32),
                   jax.ShapeDtypeStruct((1, D), F32)],
        compiler_params=_params(),
    )(x, g, target)


def _rows(shape, pref=512):
    last = shape[-1]
    rows = 1
    for s in shape[:-1]:
        rows *= s
    tr = rows
    if rows * last > 256 * 1024:
        for cand in (pref, 256, 128, 64, 32, 16, 8):
            if rows % cand == 0:
                tr = cand
                break
    return rows, last, tr


def _elementwise(fn, name, ins, n_out, out_dtype=F32):
    shape = ins[0].shape
    rows, last, tr = _rows(shape)
    flat = [a.reshape(rows, last) for a in ins]
    n_in = len(ins)

    def body(*refs):
        res = fn(*[r[...] for r in refs[:n_in]])
        if n_out == 1:
            res = (res,)
        for r, v in zip(refs[n_in:], res):
            r[...] = v.astype(r.dtype)

    spec = pl.BlockSpec((tr, last), lambda i: (i, 0))
    outs = pl.pallas_call(
        body, name=name, grid=(rows // tr,),
        in_specs=[spec] * n_in, out_specs=[spec] * n_out,
        out_shape=[jax.ShapeDtypeStruct((rows, last), out_dtype)] * n_out,
        compiler_params=_params(),
    )(*flat)
    return [o.reshape(shape) for o in outs]


def _add_pairs(gs, os, c_idx):
    n = len(gs)
    halves = [(g.shape[1] // 2, g.shape[2]) for g in gs]

    def body(c_ref, *refs):
        for a in range(n):
            refs[2 * n + a][...] = refs[2 * a][...] + refs[2 * a + 1][...]

    in_specs = []
    for h, C in halves:
        in_specs += [pl.BlockSpec((None, h, C), lambda q, c: (q, c[0], 0)), pl.BlockSpec((None, h, C), lambda q, c: (q, 0, 0))]
    return pl.pallas_call(
        body, name="add_pairs",
        grid_spec=pltpu.PrefetchScalarGridSpec(
            num_scalar_prefetch=1, grid=(N_CHIPS,), in_specs=in_specs,
            out_specs=[pl.BlockSpec((None, h, C), lambda q, c: (q, 0, 0)) for h, C in halves]),
        out_shape=[jax.ShapeDtypeStruct((N_CHIPS, h, C), F32) for h, C in halves],
        compiler_params=_params(),
    )(c_idx.astype(jnp.int32).reshape(1), *[x for pair in zip(gs, os) for x in pair])


def _add_chips(ps, rs, q_idx):
    n = len(ps)
    steps = 2
    blocks = [(p.shape[1] // steps, p.shape[2]) for p in ps]

    def body(q_ref, *refs):
        for a in range(n):
            p_ref, r0_ref, r1_ref, r2_ref = refs[4 * a:4 * a + 4]
            refs[4 * n + a][...] = (p_ref[...] + r0_ref[...]) + (r1_ref[...] + r2_ref[...])

    def arrived(tr, C, k):
        return pl.BlockSpec((None, tr, C), lambda i, q: (k, i, 0))

    in_specs, operands = [], []
    for (tr, C), p, r in zip(blocks, ps, rs):
        in_specs += [pl.BlockSpec((None, tr, C), lambda i, q: (q[0], i, 0)), arrived(tr, C, 0), arrived(tr, C, 1),
                     arrived(tr, C, 2)]
        operands += [p, r, r, r]
    return pl.pallas_call(
        body, name="add_chips",
        grid_spec=pltpu.PrefetchScalarGridSpec(
            num_scalar_prefetch=1, grid=(steps,), in_specs=in_specs,
            out_specs=[pl.BlockSpec((tr, C), lambda i, q: (i, 0)) for tr, C in blocks]),
        out_shape=[jax.ShapeDtypeStruct((p.shape[1], p.shape[2]), F32) for p in ps],
        compiler_params=_params(),
    )(q_idx.astype(jnp.int32).reshape(1), *operands)


def _adamw(w, g, m, v):
    m = ADAM_B1 * m + (1.0 - ADAM_B1) * g
    v = ADAM_B2 * v + (1.0 - ADAM_B2) * jnp.square(g)
    m_hat = m / (1.0 - ADAM_B1 ** ADAM_STEP)
    v_hat = v / (1.0 - ADAM_B2 ** ADAM_STEP)
    delta = -ADAM_LR * (m_hat / (jnp.sqrt(v_hat) + ADAM_EPS) + ADAM_WD * w)
    return delta, m, v


def _place():
    x, y, c = lax.axis_index("x"), lax.axis_index("y"), lax.axis_index("c")
    chips = [(1 - x, y), (x, 1 - y), (1 - x, 1 - y)]
    return x, y, c, chips


def _remote(src, dst, ssem, rsem, k, dev):
    return pltpu.make_async_remote_copy(src_ref=src, dst_ref=dst, send_sem=ssem.at[k], recv_sem=rsem.at[k],
                                        device_id=dev, device_id_type=MESH)


def _gather_weights(shards):
    n = len(shards)
    halves = [s.shape[1] // 2 for s in shards]

    def body(*refs):
        src, out = refs[:n], refs[n:2 * n]
        ssem, rsem = refs[2 * n:]
        x, y, c, chips = _place()
        me_q = 2 * x + y
        sib = (x, y, 1 - c)

        def half(a, q, cc):
            return out[a].at[q, :, pl.ds(cc * halves[a], halves[a]), :]

        first = []
        for a in range(n):
            mine = src[a].at[:, pl.ds(c * halves[a], halves[a]), :]
            for r, chip in enumerate(chips):
                first.append(_remote(mine, half(a, me_q, c), ssem, rsem, a * 3 + r, (*chip, c)))
        for cp in first:
            cp.start()
        passed = []
        for a in range(n):
            for r, chip in enumerate(chips):
                q = 2 * chip[0] + chip[1]
                k = a * 3 + r
                _remote(half(a, q, c), half(a, q, c), ssem, rsem, k, (*chip, c)).wait_recv()
                cp = _remote(half(a, q, c), half(a, q, c), ssem, rsem, 3 * n + k, sib)
                cp.start()
                passed.append(cp)
        for a in range(n):
            for r, chip in enumerate(chips):
                q = 2 * chip[0] + chip[1]
                _remote(half(a, q, 1 - c), half(a, q, 1 - c), ssem, rsem, 3 * n + a * 3 + r, sib).wait_recv()
        for cp in first + passed:
            cp.wait_send()

    return pl.pallas_call(
        body, name="gather_weights",
        in_specs=[HBM_SPEC] * n, out_specs=[HBM_SPEC] * n,
        out_shape=[jax.ShapeDtypeStruct((N_CHIPS,) + s.shape, s.dtype) for s in shards],
        scratch_shapes=[pltpu.SemaphoreType.DMA((6 * n,)), pltpu.SemaphoreType.DMA((6 * n,))],
        compiler_params=_params(has_side_effects=True),
    )(*shards)


def _gather_over_ici(shards):
    n = len(shards)
    halves = [s.shape[1] // 2 for s in shards]

    def copies(src, out, ssem, rsem):
        x, y, c, chips = _place()
        me_q = 2 * x + y
        res = []
        for a in range(n):
            rows = pl.ds(c * halves[a], halves[a])
            mine = src[a].at[:, rows, :]
            for r, chip in enumerate(chips):
                dev = (*chip, c)
                res.append((_remote(mine, out[a].at[me_q, :, rows, :], ssem, rsem, a * 3 + r, dev),
                            _remote(mine, out[a].at[2 * chip[0] + chip[1], :, rows, :], ssem, rsem, a * 3 + r, dev)))
        return res

    return _Hosted(list(shards), [jax.ShapeDtypeStruct((N_CHIPS,) + s.shape, s.dtype) for s in shards], 3 * n, copies)


def _pass_over_d2d(gathered):
    n = len(gathered)
    halves = [g.shape[2] // 2 for g in gathered]

    def copies(_, out, ssem, rsem):
        x, y, c, chips = _place()
        sib = (x, y, 1 - c)
        res = []
        for a in range(n):
            for r, chip in enumerate(chips):
                q = 2 * chip[0] + chip[1]
                mine = out[a].at[q, :, pl.ds(c * halves[a], halves[a]), :]
                theirs = out[a].at[q, :, pl.ds((1 - c) * halves[a], halves[a]), :]
                res.append((_remote(mine, mine, ssem, rsem, a * 3 + r, sib),
                            _remote(theirs, theirs, ssem, rsem, a * 3 + r, sib)))
        return res

    return _Hosted(list(gathered), [jax.ShapeDtypeStruct(g.shape, g.dtype) for g in gathered], 3 * n, copies,
                   in_place=True)


def _pass_to_sibling(gathered):
    n = len(gathered)
    halves = [g.shape[2] // 2 for g in gathered]

    def body(*refs):
        out = refs[n:2 * n]
        ssem, rsem = refs[2 * n:]
        x, y, c, chips = _place()
        sib = (x, y, 1 - c)

        def half(a, q, cc):
            return out[a].at[q, :, pl.ds(cc * halves[a], halves[a]), :]

        cps = []
        for a in range(n):
            for r, chip in enumerate(chips):
                q = 2 * chip[0] + chip[1]
                cps.append(_remote(half(a, q, c), half(a, q, c), ssem, rsem, a * 3 + r, sib))
        for cp in cps:
            cp.start()
        for a in range(n):
            for r, chip in enumerate(chips):
                q = 2 * chip[0] + chip[1]
                _remote(half(a, q, 1 - c), half(a, q, 1 - c), ssem, rsem, a * 3 + r, sib).wait_recv()
        for cp in cps:
            cp.wait_send()

    return pl.pallas_call(
        body, name="pass_to_sibling",
        in_specs=[HBM_SPEC] * n, out_specs=[HBM_SPEC] * n,
        out_shape=[jax.ShapeDtypeStruct(g.shape, g.dtype) for g in gathered],
        input_output_aliases={a: a for a in range(n)},
        scratch_shapes=[pltpu.SemaphoreType.DMA((3 * n,)), pltpu.SemaphoreType.DMA((3 * n,))],
        compiler_params=_params(has_side_effects=True),
    )(*gathered)


def _scatter_over_ici(parts):
    n = len(parts)

    def copies(src, out, ssem, rsem):
        x, y, c, chips = _place()
        res = []
        for a in range(n):
            for r, chip in enumerate(chips):
                cp = _remote(src[a].at[2 * chip[0] + chip[1]], out[a].at[r], ssem, rsem, a * 3 + r, (*chip, c))
                res.append((cp, cp))
        return res

    return _Hosted(list(parts), [jax.ShapeDtypeStruct((3,) + p.shape[1:], F32) for p in parts], 3 * n, copies)


def _swap_over_d2d(grads):
    n = len(grads)
    halves = [g.shape[1] // 2 for g in grads]

    def copies(src, out, ssem, rsem):
        x, y, c, _ = _place()
        res = []
        for a in range(n):
            cp = _remote(src[a].at[:, pl.ds((1 - c) * halves[a], halves[a]), :], out[a], ssem, rsem, a, (x, y, 1 - c))
            res.append((cp, cp))
        return res

    return _Hosted(list(grads), [jax.ShapeDtypeStruct((N_CHIPS, h, g.shape[2]), F32) for g, h in zip(grads, halves)],
                   n, copies)


def _swap_halves(grads):
    n = len(grads)
    halves = [g.shape[1] // 2 for g in grads]

    def body(*refs):
        src, out = refs[:n], refs[n:2 * n]
        ssem, rsem = refs[2 * n:]
        x, y, c, _ = _place()
        cps = [_remote(src[a].at[:, pl.ds((1 - c) * halves[a], halves[a]), :], out[a], ssem, rsem, a, (x, y, 1 - c))
               for a in range(n)]
        for cp in cps:
            cp.start()
        for cp in cps:
            cp.wait()

    return pl.pallas_call(
        body, name="swap_halves",
        in_specs=[HBM_SPEC] * n, out_specs=[HBM_SPEC] * n,
        out_shape=[jax.ShapeDtypeStruct((N_CHIPS, h, g.shape[2]), F32) for g, h in zip(grads, halves)],
        scratch_shapes=[pltpu.SemaphoreType.DMA((n,)), pltpu.SemaphoreType.DMA((n,))],
        compiler_params=_params(has_side_effects=True),
    )(*grads)


def _scatter_chips(parts):
    n = len(parts)

    def body(*refs):
        src, out = refs[:n], refs[n:2 * n]
        ssem, rsem = refs[2 * n:]
        x, y, c, chips = _place()
        cps = []
        for a in range(n):
            for r, chip in enumerate(chips):
                cps.append(_remote(src[a].at[2 * chip[0] + chip[1]], out[a].at[r], ssem, rsem, a * 3 + r, (*chip, c)))
        for cp in cps:
            cp.start()
        for cp in cps:
            cp.wait()

    return pl.pallas_call(
        body, name="scatter_chips",
        in_specs=[HBM_SPEC] * n, out_specs=[HBM_SPEC] * n,
        out_shape=[jax.ShapeDtypeStruct((3,) + p.shape[1:], F32) for p in parts],
        scratch_shapes=[pltpu.SemaphoreType.DMA((3 * n,)), pltpu.SemaphoreType.DMA((3 * n,))],
        compiler_params=_params(has_side_effects=True),
    )(*parts)


def _swap_reduced_over_d2d(reduced):
    n = len(reduced)

    def copies(src, out, ssem, rsem):
        x, y, c, _ = _place()
        res = []
        for a in range(n):
            cp = _remote(src[a], out[a], ssem, rsem, a, (x, y, 1 - c))
            res.append((cp, cp))
        return res

    return _Hosted(list(reduced), [jax.ShapeDtypeStruct(r.shape, F32) for r in reduced], n, copies)


def _swap_reduced(reduced):
    n = len(reduced)

    def body(*refs):
        src, out = refs[:n], refs[n:2 * n]
        ssem, rsem = refs[2 * n:]
        x, y, c, _ = _place()
        cps = [_remote(src[a], out[a], ssem, rsem, a, (x, y, 1 - c)) for a in range(n)]
        for cp in cps:
            cp.start()
        for cp in cps:
            cp.wait()

    return pl.pallas_call(
        body, name="swap_reduced",
        in_specs=[HBM_SPEC] * n, out_specs=[HBM_SPEC] * n,
        out_shape=[jax.ShapeDtypeStruct(r.shape, F32) for r in reduced],
        scratch_shapes=[pltpu.SemaphoreType.DMA((n,)), pltpu.SemaphoreType.DMA((n,))],
        compiler_params=_params(has_side_effects=True),
    )(*reduced)


def _allreduce_small(buf, hosted=None):
    R, L = buf.shape

    def body(*refs):
        (buf_ref, out_ref, pair_ref, chip_ref, ssem, rsem), start, wait = _host(hosted, refs, 1, 1, True, True)
        start()
        x, y, c, chips = _place()
        me_q = 2 * x + y
        pair_ref[c] = buf_ref[...]
        to_sib = _remote(buf_ref, pair_ref.at[c], ssem, rsem, 0, (x, y, 1 - c))
        to_sib.start()
        _remote(buf_ref, pair_ref.at[1 - c], ssem, rsem, 0, (x, y, 1 - c)).wait_recv()
        chip_ref[me_q] = pair_ref[0] + pair_ref[1]
        cps = [_remote(chip_ref.at[me_q], chip_ref.at[me_q], ssem, rsem, 1 + r, (*chip, c))
               for r, chip in enumerate(chips)]
        for cp in cps:
            cp.start()
        for r, chip in enumerate(chips):
            q = 2 * chip[0] + chip[1]
            _remote(chip_ref.at[q], chip_ref.at[q], ssem, rsem, 1 + r, (*chip, c)).wait_recv()
        out_ref[...] = (chip_ref[0] + chip_ref[1]) + (chip_ref[2] + chip_ref[3])
        to_sib.wait_send()
        for cp in cps:
            cp.wait_send()
        wait()

    h_ins = hosted.ins if hosted else []
    res = pl.pallas_call(
        body, name="allreduce_small",
        in_specs=[VMEM_SPEC] + [HBM_SPEC] * len(h_ins), out_specs=[VMEM_SPEC] + [HBM_SPEC] * len(h_ins),
        out_shape=[jax.ShapeDtypeStruct((R, L), F32)] + (hosted.out_shapes if hosted else []),
        scratch_shapes=[pltpu.VMEM((2, R, L), F32), pltpu.VMEM((N_CHIPS, R, L), F32),
                        pltpu.SemaphoreType.DMA((4,)), pltpu.SemaphoreType.DMA((4,))]
        + (hosted.sems() if hosted else []),
        compiler_params=_params(has_side_effects=True),
    )(buf, *h_ins)
    return res[0], res[1:]


def _pack(arrays):
    flat = jnp.concatenate([a.reshape(-1) for a in arrays])
    pad = (-flat.shape[0]) % (8 * LANES)
    return jnp.pad(flat, (0, pad)).reshape(-1, LANES)


def _unpack(buf, like):
    flat = buf.reshape(-1)
    out, off = [], 0
    for a in like:
        out.append(flat[off:off + a.size].reshape(a.shape))
        off += a.size
    return out


def _block_diag(pw):
    rows = []
    for gi in range(len(POOL_WINDOWS)):
        blocks = [pw[gi] if gj == gi else jnp.zeros_like(pw[gi]) for gj in range(len(POOL_WINDOWS))]
        rows.append(jnp.concatenate(blocks, axis=1))
    return jnp.concatenate(rows, axis=0)


def kernel(x, norm1, w_in, pool_w, pool_scale, sg_norm, sg_w, sg_b, w_out, norm2, w_up, w_down, final_norm, loss_target, m_norm1, m_w_in, m_pool_w, m_pool_scale, m_sg_norm, m_sg_w, m_sg_b, m_w_out, m_norm2, m_w_up, m_w_down, m_final_norm, v_norm1, v_w_in, v_pool_w, v_pool_scale, v_sg_norm, v_sg_w, v_sg_b, v_w_out, v_norm2, v_w_up, v_w_down, v_final_norm):
    depth = norm1.shape[0]
    T = x.shape[1]
    xs = x.reshape(T, D_MODEL)
    target = loss_target.reshape(T, D_MODEL)

    assert depth == 2
    c_idx = lax.axis_index("c")
    q_idx = 2 * lax.axis_index("x") + lax.axis_index("y")
    own = [w.astype(BF16) for w in (w_in, w_out, w_up, w_down)]
    gathered = {(0, 0): _gather_weights([own[0][:1]])[0]}

    def full(a, l, axis):
        blocks = lax.dynamic_update_slice(gathered[(a, l)], own[a][l][None, None], (q_idx, 0, 0, 0))[:, 0]
        if axis is None:
            return blocks
        if axis == 0:
            return blocks.reshape(-1, blocks.shape[-1])
        return jnp.concatenate([blocks[q] for q in range(N_CHIPS)], axis=axis)

    half_way = {}

    def gather_behind(call, keys, at_once):
        res, over_ici = call(_gather_over_ici([own[a][l:l + 1] for a, l in keys]))
        gathered.update(zip(keys[:at_once], _pass_to_sibling(over_ici[:at_once])))
        half_way.update(zip(keys[at_once:], over_ici[at_once:]))
        return res

    def pass_behind(call, keys):
        res, done = call(_pass_over_d2d([half_way.pop(k) for k in keys]))
        gathered.update(zip(keys, done))
        return res

    tril = jnp.tril(jnp.ones((CHUNK, CHUNK), F32))
    saved = []
    cur = xs
    wi, wo, wu, wd = {}, {}, {}, {}
    for l in range(depth):
        wbd = _block_diag(pool_w[l]).astype(BF16)
        wm = sg_w[l] * tril
        wm_s = wm.reshape(SG_HEADS * CHUNK, CHUNK).astype(BF16)
        wmt_s = jnp.swapaxes(wm, 1, 2).reshape(SG_HEADS * CHUNK, CHUNK).astype(BF16)
        bias = jnp.repeat(sg_b[l].T, SB_HD, axis=1)
        n1, n2 = norm1[l][None], norm2[l][None]
        psc, sgn = pool_scale[l][None], sg_norm[l][None]
        wi[l] = full(0, l, 1)
        proj, h, qkv = _inproj_fwd(cur, n1, wi[l])
        ya = _pool_fwd(proj, wbd, psc)
        yb = _sg_fwd(proj, wm_s, bias, sgn)
        if l == 0:
            yc = gather_behind(lambda hosted: _attn_fwd(qkv, hosted), [(1, 0), (2, 0), (3, 0)], 1)
            wo[l] = full(1, l, 0)
            x1, ymix = pass_behind(lambda hosted: _outproj_fwd(cur, ya, yb, yc, wo[l], hosted), [(2, 0), (3, 0)])
        else:
            yc = pass_behind(lambda hosted: _attn_fwd(qkv, hosted), [(1, l), (2, l), (3, l)])
            wo[l] = full(1, l, 0)
            (x1, ymix), _ = _outproj_fwd(cur, ya, yb, yc, wo[l])
        wu[l], wd[l] = full(2, l, None), full(3, l, 0)
        if l == 0:
            x2, h2, u, act = gather_behind(lambda hosted: _mlp_fwd(x1, n2, wu[l], wd[l], hosted),
                                           [(0, 1), (1, 1), (2, 1), (3, 1)], 1)
        else:
            (x2, h2, u, act), _ = _mlp_fwd(x1, n2, wu[l], wd[l])
        saved.append(dict(x0=cur, x1=x1, proj=proj, h=h, qkv=qkv, yc=yc, ymix=ymix, h2=h2, u=u, act=act,
                          wbd=wbd, wm_s=wm_s, wmt_s=wmt_s, bias=bias, n1=n1, n2=n2, psc=psc, sgn=sgn))
        cur = x2

    loss_row, dcur, d_final = _loss_head(cur, final_norm[None], target)

    small = [None] * depth
    grads, parts, reduced = {}, {}, {}

    def pair_up(keys, swapped):
        parts.update(zip(keys, _add_pairs([grads[k] for k in keys], swapped, c_idx)))

    def chip_up(keys, arrived):
        reduced.update(zip(keys, _add_chips([parts[k] for k in keys], arrived, q_idx)))

    for l in reversed(range(depth)):
        s = saved[l]
        if l == 0:
            keys = [(2, 1), (3, 1)]
            (dx1, du, d_n2), arrived = _mlp_bwd(dcur, s["x1"], s["n2"], s["u"], wu[l], wd[l],
                                                _scatter_over_ici([parts[k] for k in keys]))
            chip_up(keys, arrived)
        else:
            (dx1, du, d_n2), _ = _mlp_bwd(dcur, s["x1"], s["n2"], s["u"], wu[l], wd[l])
        if l == 0:
            keys = [(0, 1)]
            grads[(2, l)], arrived = _tn_matmul(s["h2"], du, "grad_w_up", n_split=N_CHIPS,
                                                hosted=_scatter_over_ici([parts[k] for k in keys]))
            chip_up(keys, arrived)
            keys = [(1, 1)]
            g_down, arrived = _tn_matmul(s["act"], dcur, "grad_w_down",
                                         hosted=_scatter_over_ici([parts[k] for k in keys]))
            chip_up(keys, arrived)
        else:
            grads[(2, l)] = _tn_matmul(s["h2"], du, "grad_w_up", n_split=N_CHIPS)
            g_down = _tn_matmul(s["act"], dcur, "grad_w_down")
        grads[(3, l)] = g_down[0].reshape(N_CHIPS, D_FF // N_CHIPS, D_MODEL)
        dymix = _nt_matmul(dx1, wo[l])
        grads[(1, l)] = _tn_matmul(s["ymix"], dx1, "grad_w_out")[0].reshape(N_CHIPS, D_MODEL // N_CHIPS, D_MODEL)
        da_in, d_wbd, d_psc = _pool_bwd(s["proj"], dymix, s["wbd"], s["psc"])
        if l == 0:
            keys = [(1, 0), (2, 0), (3, 0)]
            (du_pre, dv_pre, d_wm, d_bias, d_sgn), swapped = _sg_bwd(
                s["proj"], dymix, s["wm_s"], s["wmt_s"], s["bias"], s["sgn"], _swap_over_d2d([grads[k] for k in keys]))
            pair_up(keys, swapped)
            (dq, dk, dv), arrived = _attn_bwd(s["qkv"], s["yc"], dymix, _scatter_over_ici([parts[k] for k in keys]))
            chip_up(keys, arrived)
        else:
            (du_pre, dv_pre, d_wm, d_bias, d_sgn), _ = _sg_bwd(s["proj"], dymix, s["wm_s"], s["wmt_s"], s["bias"], s["sgn"])
            keys = [(1, l), (2, l), (3, l)]
            (dq, dk, dv), swapped = _attn_bwd(s["qkv"], s["yc"], dymix, _swap_over_d2d([grads[k] for k in keys]))
            pair_up(keys, swapped)
        pieces = [da_in, du_pre, dv_pre, dq, dk, dv]
        if l == 0:
            keys = sorted(reduced)
            g_in_l, swapped = _inproj_grad(s["h"], pieces, _swap_reduced_over_d2d([reduced[k] for k in keys]))
            theirs = dict(zip(keys, swapped))
        else:
            g_in_l, _ = _inproj_grad(s["h"], pieces)
        grads[(0, l)] = g_in_l[0].reshape(D_MODEL, N_CHIPS, IN_COLS // N_CHIPS).transpose(1, 0, 2)
        if l == 0:
            keys = [(0, 0)]
            pair_up(keys, _swap_halves([grads[k] for k in keys]))
            (dx0, d_n1), arrived = _inproj_bwd(pieces, wi[l], s["x0"], s["n1"], dx1,
                                               _scatter_over_ici([parts[k] for k in keys]))
            chip_up(keys, arrived)
        else:
            keys = [(0, l)]
            (dx0, d_n1), swapped = _inproj_bwd(pieces, wi[l], s["x0"], s["n1"], dx1,
                                               _swap_over_d2d([grads[k] for k in keys]))
            pair_up(keys, swapped)
        d_pw = jnp.stack([d_wbd[gi * POOL_GW:(gi + 1) * POOL_GW, gi * POOL_GW:(gi + 1) * POOL_GW]
                          for gi in range(len(POOL_WINDOWS))])
        small[l] = dict(norm1=d_n1[0], pool_w=d_pw, pool_scale=d_psc[0], sg_norm=d_sgn[0],
                        sg_w=d_wm.reshape(SG_HEADS, CHUNK, CHUNK), sg_b=d_bias[:, :SG_HEADS].T, norm2=d_n2[0])
        dcur = dx0
    grad_x = dcur.reshape(x.shape)

    names =["norm1", "pool_w", "pool_scale", "sg_norm", "sg_w", "sg_b", "norm2"]
    slot = jnp.zeros((1,), F32)
    small_w = [norm1, pool_w, pool_scale, sg_norm, sg_w, sg_b, norm2, final_norm, slot]
    small_m = [m_norm1, m_pool_w, m_pool_scale, m_sg_norm, m_sg_w, m_sg_b, m_norm2, m_final_norm, slot]
    small_v = [v_norm1, v_pool_w, v_pool_scale, v_sg_norm, v_sg_w, v_sg_b, v_norm2, v_final_norm, slot]
    small_g = [jnp.stack([small[l][k] for l in range(depth)]) for k in names] + [d_final[0], loss_row[0, :1]]
    keys = [(0, 0)]
    g_packed, _ = _allreduce_small(_pack(small_g))
    theirs.update(zip(keys, _swap_reduced([reduced[k] for k in keys])))

    def joined(a):
        layers = []
        for l in range(depth):
            mine, other = reduced[(a, l)], theirs[(a, l)]
            layers.append(jnp.where(c_idx == 0, jnp.concatenate([mine, other]), jnp.concatenate([other, mine])))
        return jnp.stack(layers)

    gw_in, gw_out, gw_up, gw_down = [joined(a) for a in range(4)]

    loss = _unpack(g_packed, small_w)[-1][0]
    s_delta, s_m, s_v = _elementwise(_adamw, "adamw_small", [_pack(small_w), g_packed, _pack(small_m), _pack(small_v)], 3)
    gs = dict(zip(names + ["final_norm"], _unpack(g_packed, small_w)))
    ds = dict(zip(names + ["final_norm"], _unpack(s_delta, small_w)))
    ms = dict(zip(names + ["final_norm"], _unpack(s_m, small_w)))
    vs = dict(zip(names + ["final_norm"], _unpack(s_v, small_w)))

    big_g = dict(w_in=gw_in, w_out=gw_out, w_up=gw_up, w_down=gw_down)
    big_w = dict(w_in=(w_in, m_w_in, v_w_in), w_out=(w_out, m_w_out, v_w_out),
                 w_up=(w_up, m_w_up, v_w_up), w_down=(w_down, m_w_down, v_w_down))
    for k, (w, m, v) in big_w.items():
        operands = [w, big_g[k], m, v]
        if k == "w_in":
            operands = [jnp.swapaxes(o, 1, 2) for o in operands]
        ds[k], ms[k], vs[k] = _elementwise(_adamw, "adamw_" + k, operands, 3)
        if k == "w_in":
            ds[k], ms[k], vs[k] = [jnp.swapaxes(o, 1, 2) for o in (ds[k], ms[k], vs[k])]
        gs[k] = big_g[k]

    order = ["norm1", "w_in", "pool_w", "pool_scale", "sg_norm", "sg_w", "sg_b", "w_out", "norm2", "w_up", "w_down",
             "final_norm"]
    return (loss, grad_x, *[gs[k] for k in order], *[ds[k] for k in order], *[ms[k] for k in order],
            *[vs[k] for k in order])
```

```python
import jax
import jax.numpy as jnp
from jax import lax
from jax.experimental import pallas as pl
from jax.experimental.pallas import tpu as pltpu

F32 = jnp.float32
BF16 = jnp.bfloat16
MESH = pl.DeviceIdType.MESH

EPS = 1e-6
D_MODEL = 1024
POOL_WIDTH = 256
SG_WIDTH = 256
SB_WIDTH = 512
POOL_WINDOWS = (2, 4, 8, 16)
POOL_GW = 64
POOL_HALO = 16
CHUNK = 128
SG_HEADS = 4
SB_HD = 64
SB_SCALE = 0.125
IN_COLS = 2304
QKV_OFF = 768
D_FF = 4096
N_CHIPS = 4
LANES = 128
VMEM_LIMIT = 56 * 1024 * 1024
MLP_CHUNK = 512
ATTN_TILE = 256
UNDERFLOW = -104.0

ADAM_LR = 0.001
ADAM_B1 = 0.9
ADAM_B2 = 0.999
ADAM_EPS = 1e-08
ADAM_WD = 0.01
ADAM_STEP = 10

HBM_SPEC = pl.BlockSpec(memory_space=pl.ANY)
VMEM_SPEC = pl.BlockSpec(memory_space=pltpu.VMEM)


def _params(**kw):
    return pltpu.CompilerParams(vmem_limit_bytes=VMEM_LIMIT, **kw)


def _tile(n, pref):
    if n <= pref:
        return n
    for t in range(pref - pref % LANES, 0, -LANES):
        if n % t == 0:
            return t
    raise ValueError((n, pref))


def _nn(a, b):
    return jnp.dot(a, b, preferred_element_type=F32)


def _nt(a, b):
    return lax.dot_general(a, b, (((1,), (1,)), ((), ())), preferred_element_type=F32)


def _tn(a, b):
    return lax.dot_general(a, b, (((0,), (0,)), ((), ())), preferred_element_type=F32)


def _rms_fwd(x, g):
    r = lax.rsqrt(jnp.mean(x * x, axis=-1, keepdims=True) + EPS)
    xhat = x * r
    return xhat * g, xhat, r


def _rms_bwd(dy, xhat, r, g):
    dxhat = dy * g
    dx = r * (dxhat - xhat * jnp.mean(dxhat * xhat, axis=-1, keepdims=True))
    return dx, dy * xhat


_GELU_K = 0.7978845608028654
_GELU_C = 0.044715


def _gelu(x):
    return 0.5 * x * (1.0 + jnp.tanh(_GELU_K * (x + _GELU_C * x * x * x)))


def _gelu_and_grad(x):
    x2 = x * x
    t = jnp.tanh(_GELU_K * (x + _GELU_C * x2 * x))
    half = 0.5 * (1.0 + t)
    return x * half, half + 0.5 * x * (1.0 - t * t) * _GELU_K * (1.0 + 3.0 * _GELU_C * x2)


def _inproj_fwd(x, g, w):
    T, D = x.shape
    N = w.shape[1]
    tt = _tile(T, 512)

    def body(x_ref, g_ref, w_ref, proj_ref, h_ref, qkv_ref):
        h, _, _ = _rms_fwd(x_ref[...], g_ref[...])
        hb = h.astype(BF16)
        h_ref[...] = hb
        p = _nn(hb, w_ref[...])
        proj_ref[...] = p[:, :QKV_OFF]
        qkv_ref[...] = p[:, QKV_OFF:].astype(BF16)

    return pl.pallas_call(
        body, name="inproj_fwd", grid=(T // tt,),
        in_specs=[pl.BlockSpec((tt, D), lambda i: (i, 0)), pl.BlockSpec((1, D), lambda i: (0, 0)),
                  pl.BlockSpec((D, N), lambda i: (0, 0))],
        out_specs=[pl.BlockSpec((tt, QKV_OFF), lambda i: (i, 0)), pl.BlockSpec((tt, D), lambda i: (i, 0)),
                   pl.BlockSpec((tt, N - QKV_OFF), lambda i: (i, 0))],
        out_shape=[jax.ShapeDtypeStruct((T, QKV_OFF), F32), jax.ShapeDtypeStruct((T, D), BF16),
                   jax.ShapeDtypeStruct((T, N - QKV_OFF), BF16)],
        compiler_params=_params(),
    )(x, g, w)


def _inproj_bwd(pieces, w, x, g, dres, hosted=None):
    T, D = x.shape
    N = w.shape[1]
    tt = _tile(T, 512)
    nt = T // tt
    widths = [p.shape[1] for p in pieces]
    offs = [sum(widths[:k]) for k in range(len(widths))]
    assert sum(widths) == N
    n_p = len(pieces)

    def body(*refs):
        i = pl.program_id(0)
        own, start, wait = _host(hosted, refs, n_p + 4, 2, i == 0, i == nt - 1)
        start()
        p_refs = own[:n_p]
        w_ref, x_ref, g_ref, dres_ref, dx_ref, dg_ref, dproj_ref = own[n_p:]
        for p_ref, o, wd in zip(p_refs, offs, widths):
            dproj_ref[:, o:o + wd] = p_ref[...].astype(BF16)
        dh = _nt(dproj_ref[...], w_ref[...])
        gv = g_ref[...]
        _, xhat, r = _rms_fwd(x_ref[...], gv)
        dx, dgrow = _rms_bwd(dh, xhat, r, gv)
        dx_ref[...] = dres_ref[...] + dx

        @pl.when(i == 0)
        def _():
            dg_ref[...] = jnp.zeros_like(dg_ref)

        dg_ref[...] += jnp.sum(dgrow, axis=0, keepdims=True)
        wait()

    h_ins = hosted.ins if hosted else []
    res = pl.pallas_call(
        body, name="inproj_bwd_hosting" if hosted else "inproj_bwd", grid=(nt,),
        in_specs=[pl.BlockSpec((tt, wd), lambda i: (i, 0)) for wd in widths] + [
            pl.BlockSpec((D, N), lambda i: (0, 0)), pl.BlockSpec((tt, D), lambda i: (i, 0)),
            pl.BlockSpec((1, D), lambda i: (0, 0)), pl.BlockSpec((tt, D), lambda i: (i, 0))] + [HBM_SPEC] * len(h_ins),
        out_specs=[pl.BlockSpec((tt, D), lambda i: (i, 0)), pl.BlockSpec((1, D), lambda i: (0, 0))]
        + [HBM_SPEC] * len(h_ins),
        out_shape=[jax.ShapeDtypeStruct((T, D), F32), jax.ShapeDtypeStruct((1, D), F32)]
        + (hosted.out_shapes if hosted else []),
        scratch_shapes=[pltpu.VMEM((tt, N), BF16)] + (hosted.sems() if hosted else []),
        compiler_params=_params(has_side_effects=hosted is not None),
    )(*pieces, w, x, g, dres, *h_ins)
    return res[:2], res[2:]


def _inproj_grad(h, pieces, hosted=None):
    T, D = h.shape
    tt = _tile(T, 1024)
    nt = T // tt
    widths = [p.shape[1] for p in pieces]
    offs = [sum(widths[:k]) for k in range(len(widths))]
    N = sum(widths)
    n_p = len(pieces)

    def body(*refs):
        t = pl.program_id(0)
        own, start, wait = _host(hosted, refs, n_p + 1, 1, t == 0, t == nt - 1)
        start()
        h_ref, p_refs, o_ref = own[0], own[1:1 + n_p], own[1 + n_p]

        @pl.when(t == 0)
        def _():
            o_ref[...] = jnp.zeros_like(o_ref)

        hv = h_ref[...]
        for p_ref, o, wd in zip(p_refs, offs, widths):
            o_ref[:, o:o + wd] += _tn(hv, p_ref[...].astype(BF16))
        wait()

    h_ins = hosted.ins if hosted else []
    res = pl.pallas_call(
        body, name="grad_w_in_hosting" if hosted else "grad_w_in", grid=(nt,),
        in_specs=[pl.BlockSpec((tt, D), lambda t: (t, 0))] + [pl.BlockSpec((tt, wd), lambda t: (t, 0)) for wd in widths]
        + [HBM_SPEC] * len(h_ins),
        out_specs=[pl.BlockSpec((None, D, N), lambda t: (0, 0, 0))] + [HBM_SPEC] * len(h_ins),
        out_shape=[jax.ShapeDtypeStruct((1, D, N), F32)] + (hosted.out_shapes if hosted else []),
        scratch_shapes=hosted.sems() if hosted else [],
        compiler_params=_params(has_side_effects=hosted is not None),
    )(h, *pieces, *h_ins)
    return res[0], res[1:]


def _pool_select(s2, s4, s8, s16, grp):
    return jnp.where(grp == 0, s2, jnp.where(grp == 1, s4, jnp.where(grp == 2, s8, s16)))


def _pool_count(t_glob, grp):
    win = jnp.where(grp == 0, 2, jnp.where(grp == 1, 4, jnp.where(grp == 2, 8, 16)))
    return jnp.minimum(t_glob + 1, win).astype(F32)


def _pool_diff(a, halo, base, tt):
    n = tt + POOL_HALO
    ext = jnp.concatenate([halo, a], axis=0)
    s2 = ext + pltpu.roll(ext, 1, 0)
    s4 = s2 + pltpu.roll(s2, 2, 0)
    s8 = s4 + pltpu.roll(s4, 4, 0)
    s16 = s8 + pltpu.roll(s8, 8, 0)
    grp = lax.broadcasted_iota(jnp.int32, (n, POOL_WIDTH), 1) // POOL_GW
    t_glob = lax.broadcasted_iota(jnp.int32, (n, POOL_WIDTH), 0) + (base - POOL_HALO)
    pooled = _pool_select(s2, s4, s8, s16, grp) / _pool_count(t_glob, grp)
    return pooled[POOL_HALO:] - a


def _pool_specs(T, tt):
    hb = tt // POOL_HALO
    return [pl.BlockSpec((tt, POOL_WIDTH), lambda i: (i, 0)),
            pl.BlockSpec((POOL_HALO, POOL_WIDTH), lambda i: (jnp.maximum(i * hb - 1, 0), 0))]


def _pool_fwd(proj, wbd, scale):
    T = proj.shape[0]
    tt = _tile(T, 512)

    def body(a_ref, halo_ref, w_ref, sc_ref, y_ref):
        i = pl.program_id(0)
        halo = jnp.where(i > 0, halo_ref[...], 0.0)
        d = _pool_diff(a_ref[...], halo, i * tt, tt)
        y_ref[...] = _nn(d.astype(BF16), w_ref[...]) * sc_ref[...]

    return pl.pallas_call(
        body, name="pool_fwd", grid=(T // tt,),
        in_specs=_pool_specs(T, tt) + [pl.BlockSpec((POOL_WIDTH, POOL_WIDTH), lambda i: (0, 0)),
                                       pl.BlockSpec((1, POOL_WIDTH), lambda i: (0, 0))],
        out_specs=pl.BlockSpec((tt, POOL_WIDTH), lambda i: (i, 0)),
        out_shape=jax.ShapeDtypeStruct((T, POOL_WIDTH), F32),
        compiler_params=_params(),
    )(proj, proj, wbd, scale)


def _pool_bwd(proj, dymix, wbd, scale):
    T = proj.shape[0]
    tt = _tile(T, 512)
    hb = tt // POOL_HALO
    nblk = T // tt
    n = tt + POOL_HALO

    def body(a_ref, halo_ref, dy_ref, dyn_ref, w_ref, sc_ref, da_ref, dw_ref, dsc_ref):
        i = pl.program_id(0)
        halo = jnp.where(i > 0, halo_ref[...], 0.0)
        d = _pool_diff(a_ref[...], halo, i * tt, tt)
        db = d.astype(BF16)
        wv = w_ref[...]
        sc = sc_ref[...]
        dy = dy_ref[...]
        dys = dy * sc

        @pl.when(i == 0)
        def _():
            dw_ref[...] = jnp.zeros_like(dw_ref)
            dsc_ref[...] = jnp.zeros_like(dsc_ref)

        dsc_ref[...] += jnp.sum(dy * _nn(db, wv), axis=0, keepdims=True)
        dw_ref[...] += _tn(db, dys.astype(BF16))
        dyn = jnp.where(i < nblk - 1, dyn_ref[...], 0.0) * sc
        dd = _nt(jnp.concatenate([dys, dyn], axis=0).astype(BF16), wv)
        grp = lax.broadcasted_iota(jnp.int32, (n, POOL_WIDTH), 1) // POOL_GW
        t_glob = lax.broadcasted_iota(jnp.int32, (n, POOL_WIDTH), 0) + i * tt
        e = dd / _pool_count(t_glob, grp)
        r2 = e + pltpu.roll(e, n - 1, 0)
        r4 = r2 + pltpu.roll(r2, n - 2, 0)
        r8 = r4 + pltpu.roll(r4, n - 4, 0)
        r16 = r8 + pltpu.roll(r8, n - 8, 0)
        da_ref[...] = (_pool_select(r2, r4, r8, r16, grp) - dd)[:tt].astype(BF16)

    return pl.pallas_call(
        body, name="pool_bwd", grid=(nblk,),
        in_specs=_pool_specs(T, tt) + [
            pl.BlockSpec((tt, POOL_WIDTH), lambda i: (i, 0)),
            pl.BlockSpec((POOL_HALO, POOL_WIDTH), lambda i: (jnp.minimum((i + 1) * hb, T // POOL_HALO - 1), 0)),
            pl.BlockSpec((POOL_WIDTH, POOL_WIDTH), lambda i: (0, 0)), pl.BlockSpec((1, POOL_WIDTH), lambda i: (0, 0))],
        out_specs=[pl.BlockSpec((tt, POOL_WIDTH), lambda i: (i, 0)),
                   pl.BlockSpec((POOL_WIDTH, POOL_WIDTH), lambda i: (0, 0)),
                   pl.BlockSpec((1, POOL_WIDTH), lambda i: (0, 0))],
        out_shape=[jax.ShapeDtypeStruct((T, POOL_WIDTH), BF16),
                   jax.ShapeDtypeStruct((POOL_WIDTH, POOL_WIDTH), F32),
                   jax.ShapeDtypeStruct((1, POOL_WIDTH), F32)],
        compiler_params=_params(),
    )(proj, proj, dymix, dymix, wbd, scale)


def _head_select(stacked, grp):
    out = jnp.where(grp == 0, stacked[0:CHUNK], 0.0)
    for h in range(1, SG_HEADS):
        out = out + jnp.where(grp == h, stacked[h * CHUNK:(h + 1) * CHUNK], 0.0)
    return out


def _sg_specs(tt):
    return [pl.BlockSpec((tt, SG_WIDTH), lambda i: (i, 1)), pl.BlockSpec((tt, SG_WIDTH), lambda i: (i, 2))]


def _sg_fwd(proj, wm, bias, g):
    T = proj.shape[0]
    tt = _tile(T, 512)

    def body(u_ref, v_ref, wm_ref, b_ref, g_ref, y_ref):
        zu = _gelu(u_ref[...])
        vn, _, _ = _rms_fwd(_gelu(v_ref[...]), g_ref[...])
        grp = lax.broadcasted_iota(jnp.int32, (CHUNK, SG_WIDTH), 1) // SB_HD
        for n in range(tt // CHUNK):
            rows = slice(n * CHUNK, (n + 1) * CHUNK)
            sv = _head_select(_nn(wm_ref[...], vn[rows].astype(BF16)), grp) + b_ref[...]
            y_ref[rows, :] = zu[rows] * sv

    return pl.pallas_call(
        body, name="sg_fwd", grid=(T // tt,),
        in_specs=_sg_specs(tt) + [pl.BlockSpec((SG_HEADS * CHUNK, CHUNK), lambda i: (0, 0)),
                                  pl.BlockSpec((CHUNK, SG_WIDTH), lambda i: (0, 0)),
                                  pl.BlockSpec((1, SG_WIDTH), lambda i: (0, 0))],
        out_specs=pl.BlockSpec((tt, SG_WIDTH), lambda i: (i, 0)),
        out_shape=jax.ShapeDtypeStruct((T, SG_WIDTH), F32),
        compiler_params=_params(),
    )(proj, proj, wm, bias, g)


def _sg_bwd(proj, dymix, wm, wmt, bias, g, hosted=None):
    T = proj.shape[0]
    tt = _tile(T, 512)
    nblk = T // tt

    def body(*refs):
        i = pl.program_id(0)
        (u_ref, v_ref, dy_ref, wm_ref, wmt_ref, b_ref, g_ref, du_ref, dv_ref, dw_ref, db_ref, dg_ref,
         dvn_ref, dbias_ref), start, wait = _host(hosted, refs, 7, 5, i == 0, i == nblk - 1)
        start()
        up, vp = u_ref[...], v_ref[...]
        gv = g_ref[...]
        (zu, gu), (zv, gvp) = _gelu_and_grad(up), _gelu_and_grad(vp)
        vn, xhat, r = _rms_fwd(zv, gv)
        grp = lax.broadcasted_iota(jnp.int32, (CHUNK, SG_WIDTH), 1) // SB_HD

        @pl.when(i == 0)
        def _():
            dw_ref[...] = jnp.zeros_like(dw_ref)
            dbias_ref[...] = jnp.zeros_like(dbias_ref)
            dg_ref[...] = jnp.zeros_like(dg_ref)

        for n in range(tt // CHUNK):
            rows = slice(n * CHUNK, (n + 1) * CHUNK)
            vc = vn[rows].astype(BF16)
            sv = _head_select(_nn(wm_ref[...], vc), grp) + b_ref[...]
            dy = dy_ref[rows, :]
            du_ref[rows, :] = (dy * sv * gu[rows]).astype(BF16)
            dsv = dy * zu[rows]
            dsvb = dsv.astype(BF16)
            dvn_ref[rows, :] = _head_select(_nn(wmt_ref[...], dsvb), grp)
            stacked = jnp.concatenate([jnp.where(grp == h, dsv, 0.0) for h in range(SG_HEADS)], axis=0)
            dw_ref[...] += _nt(stacked.astype(BF16), vc)
            dbias_ref[...] += dsv

        dzv, dgrow = _rms_bwd(dvn_ref[...], xhat, r, gv)
        dg_ref[...] += jnp.sum(dgrow, axis=0, keepdims=True)
        dv_ref[...] = (dzv * gvp).astype(BF16)

        @pl.when(i == nblk - 1)
        def _():
            t_i = lax.broadcasted_iota(jnp.int32, (SG_HEADS * CHUNK, CHUNK), 0) % CHUNK
            s_i = lax.broadcasted_iota(jnp.int32, (SG_HEADS * CHUNK, CHUNK), 1)
            dw_ref[...] = jnp.where(s_i <= t_i, dw_ref[...], 0.0)
            lane = lax.broadcasted_iota(jnp.int32, (CHUNK, LANES), 1)
            acc = jnp.zeros((CHUNK, LANES), F32)
            for h in range(SG_HEADS):
                tot = jnp.sum(jnp.where(grp == h, dbias_ref[...], 0.0), axis=1, keepdims=True)
                acc = acc + jnp.where(lane == h, tot, 0.0)
            db_ref[...] = acc

        wait()

    h_ins = hosted.ins if hosted else []
    res = pl.pallas_call(
        body, name="sg_bwd_hosting" if hosted else "sg_bwd", grid=(nblk,),
        in_specs=_sg_specs(tt) + [pl.BlockSpec((tt, SG_WIDTH), lambda i: (i, 1)),
                                  pl.BlockSpec((SG_HEADS * CHUNK, CHUNK), lambda i: (0, 0)),
                                  pl.BlockSpec((SG_HEADS * CHUNK, CHUNK), lambda i: (0, 0)),
                                  pl.BlockSpec((CHUNK, SG_WIDTH), lambda i: (0, 0)),
                                  pl.BlockSpec((1, SG_WIDTH), lambda i: (0, 0))] + [HBM_SPEC] * len(h_ins),
        out_specs=[pl.BlockSpec((tt, SG_WIDTH), lambda i: (i, 0)), pl.BlockSpec((tt, SG_WIDTH), lambda i: (i, 0)),
                   pl.BlockSpec((SG_HEADS * CHUNK, CHUNK), lambda i: (0, 0)),
                   pl.BlockSpec((CHUNK, LANES), lambda i: (0, 0)), pl.BlockSpec((1, SG_WIDTH), lambda i: (0, 0))]
        + [HBM_SPEC] * len(h_ins),
        out_shape=[jax.ShapeDtypeStruct((T, SG_WIDTH), BF16), jax.ShapeDtypeStruct((T, SG_WIDTH), BF16),
                   jax.ShapeDtypeStruct((SG_HEADS * CHUNK, CHUNK), F32),
                   jax.ShapeDtypeStruct((CHUNK, LANES), F32), jax.ShapeDtypeStruct((1, SG_WIDTH), F32)]
        + (hosted.out_shapes if hosted else []),
        scratch_shapes=[pltpu.VMEM((tt, SG_WIDTH), F32), pltpu.VMEM((CHUNK, SG_WIDTH), F32)]
        + (hosted.sems() if hosted else []),
        compiler_params=_params(has_side_effects=hosted is not None),
    )(proj, proj, dymix, wm, wmt, bias, g, *h_ins)
    return res[:5], res[5:]


def _sb_logits(z):
    lb = jnp.minimum(z, 0.0) - jnp.log(1.0 + jnp.exp(-jnp.abs(z)))
    return lb, lb - z


ATTN_STRIP = 256
ATTN_SUBS = 4


def _by_strips(n_rows, fn):
    parts = None
    for r in range(0, n_rows, ATTN_STRIP):
        res = fn(slice(r, r + ATTN_STRIP))
        parts = [[v] for v in res] if parts is None else [p + [v] for p, v in zip(parts, res)]
    return [jnp.concatenate(p, axis=0) for p in parts]


def _attn_qkv_specs(tq, T):
    base = (IN_COLS - 3 * SB_WIDTH - QKV_OFF) // LANES
    nb = SB_WIDTH // LANES
    return [pl.BlockSpec((tq, LANES), lambda p, i: (i, base + p)),
            pl.BlockSpec((T, LANES), lambda p, i: (0, base + nb + p)),
            pl.BlockSpec((T, LANES), lambda p, i: (0, base + 2 * nb + p))]


class _Hosted:
    def __init__(self, ins, out_shapes, n_sems, copies, in_place=False):
        self.ins, self.out_shapes, self.n_sems, self.copies = ins, out_shapes, n_sems, copies
        self.in_place = in_place

    @property
    def n(self):
        return len(self.ins)

    def aliases(self, n_in, n_out):
        return {n_in + k: n_out + k for k in range(self.n)} if self.in_place else {}

    def sems(self):
        return [pltpu.SemaphoreType.DMA((self.n_sems,)), pltpu.SemaphoreType.DMA((self.n_sems,))]

    def start(self, src, dst, ssem, rsem):
        for send, _ in self.copies(src, dst, ssem, rsem):
            send.start()

    def wait(self, src, dst, ssem, rsem):
        for send, recv in self.copies(src, dst, ssem, rsem):
            recv.wait_recv()
            send.wait_send()


def _host(hosted, refs, n_in, n_out, first, last):
    if hosted is None:
        return refs, lambda: None, lambda: None
    n = hosted.n
    own_in, h_in = refs[:n_in], refs[n_in:n_in + n]
    own_out, h_out = refs[n_in + n:n_in + n + n_out], refs[n_in + n + n_out:n_in + 2 * n + n_out]
    rest = refs[n_in + 2 * n + n_out:]
    ssem, rsem = rest[-2:]

    def start():
        if first is True:
            hosted.start(h_in, h_out, ssem, rsem)
        else:
            pl.when(first)(lambda: hosted.start(h_in, h_out, ssem, rsem))

    def wait():
        if last is True:
            hosted.wait(h_in, h_out, ssem, rsem)
        else:
            pl.when(last)(lambda: hosted.wait(h_in, h_out, ssem, rsem))

    return own_in + own_out + rest[:-2], start, wait


def _attn_fwd(qkv, hosted=None):
    T = qkv.shape[0]
    tk = _tile(T, ATTN_TILE)
    n_sub = ATTN_SUBS if T % (ATTN_SUBS * tk) == 0 else 1
    tq = n_sub * tk
    n_p, nq = SB_WIDTH // LANES, T // tq

    def body(*refs):
        p, i = pl.program_id(0), pl.program_id(1)
        (q_ref, k_ref, v_ref, o_ref), start, wait = _host(
            hosted, refs, 3, 1, jnp.logical_and(p == 0, i == 0), jnp.logical_and(p == n_p - 1, i == nq - 1))
        start()
        lane = lax.broadcasted_iota(jnp.int32, (tk, LANES), 1)
        row = lax.broadcasted_iota(jnp.int32, (tk, tk), 0)
        col = lax.broadcasted_iota(jnp.int32, (tk, tk), 1)
        after = jnp.where(row > col, 1.0, 0.0).astype(BF16)
        valid = col < row
        qh = {}
        for sb in range(n_sub):
            q = q_ref[sb * tk:(sb + 1) * tk, :].astype(F32)
            for hh in range(2):
                qh[(sb, hh)] = jnp.where((lane // SB_HD) == hh, q * SB_SCALE, 0.0).astype(BF16)

        def tiles(todo, state):
            chains = [(n, hh) for n in range(len(todo)) for hh in range(2)]
            kv = []
            for _, j, _ in todo:
                ks = pl.ds(pl.multiple_of(j * tk, tk), tk)
                kv.append((k_ref[ks, :], v_ref[ks, :]))
            z = {(n, hh): _nt(qh[(todo[n][0], hh)], kv[n][0]) for n, hh in chains}
            lb, lmb, lm_sum = {}, {}, {}
            for n, hh in chains:
                def logits(rows, z=z[(n, hh)], mask=todo[n][2]):
                    lb, lm = _sb_logits(z[rows])
                    if mask is not None:
                        lm = jnp.where(mask[rows], lm, 0.0)
                    return lb, lm.astype(BF16), jnp.sum(lm, axis=1, keepdims=True)

                lb[(n, hh)], lmb[(n, hh)], lm_sum[(n, hh)] = _by_strips(tk, logits)
            x = {c: _nn(lmb[c], after) for c in chains}
            new = dict(state)
            for n, hh in chains:
                key = (todo[n][0], hh)
                carry, acc = new[key]

                def weights(rows, lb=lb[(n, hh)], x=x[(n, hh)], carry=carry, mask=todo[n][2]):
                    a = jnp.exp(lb[rows] + x[rows] + carry[rows])
                    if mask is not None:
                        a = jnp.where(mask[rows], a, 0.0)
                    return (a.astype(BF16),)

                (ab,) = _by_strips(tk, weights)
                new[key] = (carry + lm_sum[(n, hh)], acc + _nn(ab, kv[n][1]))
            return new

        def live(state, sb):
            return jnp.maximum(jnp.max(state[(sb, 0)][0]), jnp.max(state[(sb, 1)][0]))

        first = n_sub * i
        zero = (jnp.zeros((tk, 1), F32), jnp.zeros((tk, LANES), F32))
        todo = []
        for sb in range(n_sub):
            gate = jnp.broadcast_to(first > 0, (tk, tk)) if sb == 0 else None
            todo += [(sb, first + sb, valid), (sb, jnp.maximum(first + sb - 1, 0), gate)]
        state = tiles(todo, {(sb, hh): zero for sb in range(n_sub) for hh in range(2)})
        for sb in range(n_sub):
            def cond(st):
                return jnp.logical_and(st[0] >= 0, st[2] > UNDERFLOW)

            def step(st, sb=sb):
                mine = tiles([(sb, st[0], None)], st[1])
                return st[0] - 1, mine, live(mine, sb)

            mine = {k: v for k, v in state.items() if k[0] == sb}
            _, mine, _ = lax.while_loop(cond, step, (first + sb - 2, mine, live(mine, sb)))
            o_ref[sb * tk:(sb + 1) * tk, :] = jnp.where(lane < SB_HD, mine[(sb, 0)][1], mine[(sb, 1)][1])
        wait()

    h_ins = hosted.ins if hosted else []
    res = pl.pallas_call(
        body, name="attn_fwd_hosting" if hosted else "attn_fwd", grid=(n_p, nq),
        in_specs=_attn_qkv_specs(tq, T) + [HBM_SPEC] * len(h_ins),
        out_specs=[pl.BlockSpec((tq, LANES), lambda p, i: (i, p))] + [HBM_SPEC] * len(h_ins),
        out_shape=[jax.ShapeDtypeStruct((T, SB_WIDTH), F32)] + (hosted.out_shapes if hosted else []),
        input_output_aliases=hosted.aliases(3, 1) if hosted else {},
        scratch_shapes=hosted.sems() if hosted else [],
        compiler_params=_params(has_side_effects=hosted is not None),
    )(qkv, qkv, qkv, *h_ins)
    return res[0], res[1:]


def _attn_bwd(qkv, o, dymix, hosted=None):
    T = qkv.shape[0]
    tk = _tile(T, ATTN_TILE)
    n_sub = ATTN_SUBS if T % (ATTN_SUBS * tk) == 0 else 1
    tq = n_sub * tk
    n_p, nq = SB_WIDTH // LANES, T // tq
    yc_blk = (POOL_WIDTH + SG_WIDTH) // LANES

    def body(*refs):
        p, i = pl.program_id(0), pl.program_id(1)
        (q_ref, k_ref, v_ref, o_ref, do_ref, dq_ref, dk_ref, dv_ref), start, wait = _host(
            hosted, refs, 5, 3, jnp.logical_and(p == 0, i == 0), jnp.logical_and(p == n_p - 1, i == nq - 1))
        start()
        lane = lax.broadcasted_iota(jnp.int32, (tk, LANES), 1)
        row = lax.broadcasted_iota(jnp.int32, (tk, tk), 0)
        col = lax.broadcasted_iota(jnp.int32, (tk, tk), 1)
        after = jnp.where(row > col, 1.0, 0.0).astype(BF16)
        from_here = jnp.where(row >= col, 1.0, 0.0).astype(BF16)
        from_here2 = jnp.concatenate([from_here, from_here], axis=0)
        valid = col < row

        @pl.when(i == 0)
        def _():
            dk_ref[...] = jnp.zeros_like(dk_ref)
            dv_ref[...] = jnp.zeros_like(dv_ref)

        qh, dohb, delta = {}, {}, {}
        for sb in range(n_sub):
            rows = slice(sb * tk, (sb + 1) * tk)
            q, ov, dov = q_ref[rows, :].astype(F32), o_ref[rows, :], do_ref[rows, :]
            for hh in range(2):
                head = (lane // SB_HD) == hh
                qh[(sb, hh)] = jnp.where(head, q * SB_SCALE, 0.0).astype(BF16)
                dohb[(sb, hh)] = jnp.where(head, dov, 0.0).astype(BF16)
                delta[(sb, hh)] = jnp.sum(dohb[(sb, hh)].astype(F32) * ov, axis=1, keepdims=True)

        def tiles(todo, state):
            chains = [(n, hh) for n in range(len(todo)) for hh in range(2)]
            kv, where = [], []
            for _, j, _ in todo:
                ks = pl.ds(pl.multiple_of(j * tk, tk), tk)
                where.append(ks)
                kv.append((k_ref[ks, :], v_ref[ks, :]))
            z = {(n, hh): _nt(qh[(todo[n][0], hh)], kv[n][0]) for n, hh in chains}
            da = {(n, hh): _nt(dohb[(todo[n][0], hh)], kv[n][1]) for n, hh in chains}
            lb, lmb, lm_sum = {}, {}, {}
            for n, hh in chains:
                def logits(rows, z=z[(n, hh)], mask=todo[n][2]):
                    lb, lm = _sb_logits(z[rows])
                    if mask is not None:
                        lm = jnp.where(mask[rows], lm, 0.0)
                    return lb, lm.astype(BF16), jnp.sum(lm, axis=1, keepdims=True)

                lb[(n, hh)], lmb[(n, hh)], lm_sum[(n, hh)] = _by_strips(tk, logits)
            x = {c: _nn(lmb[c], after) for c in chains}
            c_a = {k: v[0] for k, v in state.items()}
            ab, g, g_split, g_sum = {}, {}, {}, {}
            for n, hh in chains:
                key = (todo[n][0], hh)

                def weights(rows, lb=lb[(n, hh)], x=x[(n, hh)], da=da[(n, hh)], c_a=c_a[key], mask=todo[n][2]):
                    a = jnp.exp(lb[rows] + x[rows] + c_a[rows])
                    if mask is not None:
                        a = jnp.where(mask[rows], a, 0.0)
                    ab = a.astype(BF16)
                    g = da[rows] * ab.astype(F32)
                    hi = g.astype(BF16)
                    lo = (g - hi.astype(F32)).astype(BF16)
                    return ab, g, jnp.concatenate([hi, lo], axis=1), jnp.sum(g, axis=1, keepdims=True)

                ab[(n, hh)], g[(n, hh)], g_split[(n, hh)], g_sum[(n, hh)] = _by_strips(tk, weights)
                c_a[key] = c_a[key] + lm_sum[(n, hh)]
            right = {c: _nn(g_split[c], from_here2) for c in chains}
            c_r = {k: v[1] for k, v in state.items()}
            dzb = {}
            for n, hh in chains:
                key = (todo[n][0], hh)

                def logit_grads(rows, lb=lb[(n, hh)], g=g[(n, hh)], right=right[(n, hh)], c_r=c_r[key],
                                delta=delta[key], mask=todo[n][2]):
                    sig = jnp.exp(lb[rows])
                    left = delta[rows] - (c_r[rows] + right[rows])
                    dz = g[rows] * (1.0 - sig) - left * sig
                    if mask is not None:
                        dz = jnp.where(mask[rows], dz, 0.0)
                    return (dz.astype(BF16),)

                (dzb[(n, hh)],) = _by_strips(tk, logit_grads)
                c_r[key] = c_r[key] + g_sum[(n, hh)]
            dqa = {k: v[2] for k, v in state.items()}
            for n in range(len(todo)):
                sb = todo[n][0]
                dk_ref[where[n], :] += _tn(dzb[(n, 0)], qh[(sb, 0)]) + _tn(dzb[(n, 1)], qh[(sb, 1)])
                dv_ref[where[n], :] += _tn(ab[(n, 0)], dohb[(sb, 0)]) + _tn(ab[(n, 1)], dohb[(sb, 1)])
                for hh in range(2):
                    dqa[(sb, hh)] = dqa[(sb, hh)] + _nn(dzb[(n, hh)], kv[n][0])
            return {k: (c_a[k], c_r[k], dqa[k]) for k in state}

        def live(state, sb):
            return jnp.maximum(jnp.max(state[(sb, 0)][0]), jnp.max(state[(sb, 1)][0]))

        first = n_sub * i
        zero = (jnp.zeros((tk, 1), F32), jnp.zeros((tk, 1), F32), jnp.zeros((tk, LANES), F32))
        todo = []
        for sb in range(n_sub):
            gate = jnp.broadcast_to(first > 0, (tk, tk)) if sb == 0 else None
            todo += [(sb, first + sb, valid), (sb, jnp.maximum(first + sb - 1, 0), gate)]
        state = tiles(todo, {(sb, hh): zero for sb in range(n_sub) for hh in range(2)})
        for sb in range(n_sub):
            def cond(st):
                return jnp.logical_and(st[0] >= 0, st[2] > UNDERFLOW)

            def step(st, sb=sb):
                mine = tiles([(sb, st[0], None)], st[1])
                return st[0] - 1, mine, live(mine, sb)

            mine = {k: v for k, v in state.items() if k[0] == sb}
            _, mine, _ = lax.while_loop(cond, step, (first + sb - 2, mine, live(mine, sb)))
            dq_ref[sb * tk:(sb + 1) * tk, :] = (
                jnp.where(lane < SB_HD, mine[(sb, 0)][2], mine[(sb, 1)][2]) * SB_SCALE).astype(BF16)
        wait()

    h_ins = hosted.ins if hosted else []
    res = pl.pallas_call(
        body, name="attn_bwd_hosting" if hosted else "attn_bwd", grid=(n_p, nq),
        in_specs=_attn_qkv_specs(tq, T) + [pl.BlockSpec((tq, LANES), lambda p, i: (i, p)),
                                           pl.BlockSpec((tq, LANES), lambda p, i: (i, yc_blk + p))]
        + [HBM_SPEC] * len(h_ins),
        out_specs=[pl.BlockSpec((tq, LANES), lambda p, i: (i, p)), pl.BlockSpec((T, LANES), lambda p, i: (0, p)),
                   pl.BlockSpec((T, LANES), lambda p, i: (0, p))] + [HBM_SPEC] * len(h_ins),
        out_shape=[jax.ShapeDtypeStruct((T, SB_WIDTH), BF16)] + [jax.ShapeDtypeStruct((T, SB_WIDTH), F32)] * 2
        + (hosted.out_shapes if hosted else []),
        scratch_shapes=hosted.sems() if hosted else [],
        compiler_params=_params(has_side_effects=hosted is not None),
    )(qkv, qkv, qkv, o, dymix, *h_ins)
    return res[:3], res[3:]


def _outproj_fwd(x, ya, yb, yc, w, hosted=None):
    T, D = x.shape
    tt = _tile(T, 512)
    nt = T // tt

    def body(*refs):
        i = pl.program_id(0)
        (x_ref, ya_ref, yb_ref, yc_ref, w_ref, x1_ref, ymix_ref), start, wait = _host(
            hosted, refs, 5, 2, i == 0, i == nt - 1)
        start()
        ymix_ref[:, 0:POOL_WIDTH] = ya_ref[...].astype(BF16)
        ymix_ref[:, POOL_WIDTH:POOL_WIDTH + SG_WIDTH] = yb_ref[...].astype(BF16)
        ymix_ref[:, POOL_WIDTH + SG_WIDTH:] = yc_ref[...].astype(BF16)
        x1_ref[...] = x_ref[...] + _nn(ymix_ref[...], w_ref[...])
        wait()

    row = lambda width: pl.BlockSpec((tt, width), lambda i: (i, 0))
    h_ins = hosted.ins if hosted else []
    res = pl.pallas_call(
        body, name="outproj_fwd_hosting" if hosted else "outproj_fwd", grid=(nt,),
        in_specs=[row(D), row(POOL_WIDTH), row(SG_WIDTH), row(SB_WIDTH), pl.BlockSpec((D, D), lambda i: (0, 0))]
        + [HBM_SPEC] * len(h_ins),
        out_specs=[row(D), row(D)] + [HBM_SPEC] * len(h_ins),
        out_shape=[jax.ShapeDtypeStruct((T, D), F32), jax.ShapeDtypeStruct((T, D), BF16)]
        + (hosted.out_shapes if hosted else []),
        input_output_aliases=hosted.aliases(5, 2) if hosted else {},
        scratch_shapes=hosted.sems() if hosted else [],
        compiler_params=_params(has_side_effects=hosted is not None),
    )(x, ya, yb, yc, w, *h_ins)
    return res[:2], res[2:]


def _nt_matmul(a, w):
    T, N = a.shape
    K = w.shape[0]
    tt = _tile(T, 512)

    def body(a_ref, w_ref, o_ref):
        o_ref[...] = _nt(a_ref[...].astype(BF16), w_ref[...])

    return pl.pallas_call(
        body, name="nt_matmul", grid=(T // tt,),
        in_specs=[pl.BlockSpec((tt, N), lambda i: (i, 0)), pl.BlockSpec((K, N), lambda i: (0, 0))],
        out_specs=pl.BlockSpec((tt, K), lambda i: (i, 0)),
        out_shape=jax.ShapeDtypeStruct((T, K), F32),
        compiler_params=_params(),
    )(a, w)


def _tn_matmul(a, b, name, n_split=1, hosted=None):
    T, K = a.shape
    N = b.shape[1]
    tk = _tile(K, 1024)
    tn = _tile(N // n_split, 1024)
    tt = _tile(T, 2048)
    nper = N // n_split // tn
    nk, nn, nt = K // tk, N // tn, T // tt

    def body(*refs):
        k, n, t = pl.program_id(0), pl.program_id(1), pl.program_id(2)
        (a_ref, b_ref, o_ref), start, wait = _host(
            hosted, refs, 2, 1, jnp.logical_and(jnp.logical_and(k == 0, n == 0), t == 0),
            jnp.logical_and(jnp.logical_and(k == nk - 1, n == nn - 1), t == nt - 1))
        start()

        @pl.when(t == 0)
        def _():
            o_ref[...] = jnp.zeros_like(o_ref)

        o_ref[...] += _tn(a_ref[...], b_ref[...].astype(BF16))
        wait()

    h_ins = hosted.ins if hosted else []
    res = pl.pallas_call(
        body, name=name + "_hosting" if hosted else name, grid=(nk, nn, nt),
        in_specs=[pl.BlockSpec((tt, tk), lambda k, n, t: (t, k)), pl.BlockSpec((tt, tn), lambda k, n, t: (t, n))]
        + [HBM_SPEC] * len(h_ins),
        out_specs=[pl.BlockSpec((None, tk, tn), lambda k, n, t: (n // nper, k, n % nper))] + [HBM_SPEC] * len(h_ins),
        out_shape=[jax.ShapeDtypeStruct((n_split, K, N // n_split), F32)] + (hosted.out_shapes if hosted else []),
        scratch_shapes=hosted.sems() if hosted else [],
        compiler_params=_params(has_side_effects=hosted is not None),
    )(a, b, *h_ins)
    return (res[0], res[1:]) if hosted else res[0]


def _mlp_fwd(x, g, w_up, w_down, hosted=None):
    T, D = x.shape
    n_blk, _, width = w_up.shape
    F = n_blk * width
    tt = _tile(T, 1024)
    fc = _tile(width, MLP_CHUNK)
    per = width // fc
    nc = F // fc
    nt = T // tt

    def body(*refs):
        i, c = pl.program_id(0), pl.program_id(1)
        (x_ref, g_ref, wu_ref, wd_ref, y_ref, h_ref, u_ref, a_ref), start, wait = _host(
            hosted, refs, 4, 4, jnp.logical_and(i == 0, c == 0), jnp.logical_and(i == nt - 1, c == nc - 1))
        start()

        @pl.when(c == 0)
        def _():
            xv = x_ref[...]
            h, _, _ = _rms_fwd(xv, g_ref[...])
            h_ref[...] = h.astype(BF16)
            y_ref[...] = xv

        r = jnp.maximum(_nn(h_ref[...], wu_ref[...]), 0.0)
        u_ref[...] = (2.0 * r).astype(BF16)
        a = jnp.square(r).astype(BF16)
        a_ref[...] = a
        y_ref[...] += _nn(a, wd_ref[...])
        wait()

    h_ins = hosted.ins if hosted else []
    res = pl.pallas_call(
        body, name="mlp_fwd_hosting" if hosted else "mlp_fwd", grid=(nt, nc),
        in_specs=[pl.BlockSpec((tt, D), lambda i, c: (i, 0)), pl.BlockSpec((1, D), lambda i, c: (0, 0)),
                  pl.BlockSpec((None, D, fc), lambda i, c: (c // per, 0, c % per)),
                  pl.BlockSpec((fc, D), lambda i, c: (c, 0))]
        + [HBM_SPEC] * len(h_ins),
        out_specs=[pl.BlockSpec((tt, D), lambda i, c: (i, 0)), pl.BlockSpec((tt, D), lambda i, c: (i, 0)),
                   pl.BlockSpec((tt, fc), lambda i, c: (i, c)), pl.BlockSpec((tt, fc), lambda i, c: (i, c))]
        + [HBM_SPEC] * len(h_ins),
        out_shape=[jax.ShapeDtypeStruct((T, D), F32), jax.ShapeDtypeStruct((T, D), BF16),
                   jax.ShapeDtypeStruct((T, F), BF16), jax.ShapeDtypeStruct((T, F), BF16)]
        + (hosted.out_shapes if hosted else []),
        scratch_shapes=hosted.sems() if hosted else [],
        compiler_params=_params(has_side_effects=hosted is not None),
    )(x, g, w_up, w_down, *h_ins)
    return res[:4], res[4:]


def _mlp_bwd(dy, x, g, u, w_up, w_down, hosted=None):
    T, D = x.shape
    n_blk, _, width = w_up.shape
    F = n_blk * width
    tt = _tile(T, 1024)
    fc = _tile(width, MLP_CHUNK)
    per = width // fc
    nc = F // fc
    nt = T // tt

    def body(*refs):
        i, c = pl.program_id(0), pl.program_id(1)
        (dy_ref, x_ref, g_ref, u_ref, wu_ref, wd_ref, dx_ref, du_ref, dg_ref, dyb_ref, dh_ref), start, wait = _host(
            hosted, refs, 6, 3, jnp.logical_and(i == 0, c == 0), jnp.logical_and(i == nt - 1, c == nc - 1))
        start()

        @pl.when(c == 0)
        def _():
            dyb_ref[...] = dy_ref[...].astype(BF16)
            dh_ref[...] = jnp.zeros_like(dh_ref)

        @pl.when(jnp.logical_and(i == 0, c == 0))
        def _():
            dg_ref[...] = jnp.zeros_like(dg_ref)

        da = _nt(dyb_ref[...], wd_ref[...])
        du = (da * u_ref[...].astype(F32)).astype(BF16)
        du_ref[...] = du
        dh_ref[...] += _nt(du, wu_ref[...])

        @pl.when(c == nc - 1)
        def _():
            gv = g_ref[...]
            _, xhat, r = _rms_fwd(x_ref[...], gv)
            dx, dgrow = _rms_bwd(dh_ref[...], xhat, r, gv)
            dx_ref[...] = dy_ref[...] + dx
            dg_ref[...] += jnp.sum(dgrow, axis=0, keepdims=True)

        wait()

    h_ins = hosted.ins if hosted else []
    res = pl.pallas_call(
        body, name="mlp_bwd_hosting" if hosted else "mlp_bwd", grid=(nt, nc),
        in_specs=[pl.BlockSpec((tt, D), lambda i, c: (i, 0)), pl.BlockSpec((tt, D), lambda i, c: (i, 0)),
                  pl.BlockSpec((1, D), lambda i, c: (0, 0)), pl.BlockSpec((tt, fc), lambda i, c: (i, c)),
                  pl.BlockSpec((None, D, fc), lambda i, c: (c // per, 0, c % per)),
                  pl.BlockSpec((fc, D), lambda i, c: (c, 0))]
        + [HBM_SPEC] * len(h_ins),
        out_specs=[pl.BlockSpec((tt, D), lambda i, c: (i, 0)), pl.BlockSpec((tt, fc), lambda i, c: (i, c)),
                   pl.BlockSpec((1, D), lambda i, c: (0, 0))] + [HBM_SPEC] * len(h_ins),
        out_shape=[jax.ShapeDtypeStruct((T, D), F32), jax.ShapeDtypeStruct((T, F), BF16),
                   jax.ShapeDtypeStruct((1, D), F32)] + (hosted.out_shapes if hosted else []),
        scratch_shapes=[pltpu.VMEM((tt, D), BF16), pltpu.VMEM((tt, D), F32)] + (hosted.sems() if hosted else []),
        compiler_params=_params(has_side_effects=hosted is not None),
    )(dy, x, g, u, w_up, w_down, *h_ins)
    return res[:3], res[3:]


def _loss_head(x, g, target):
    T, D = x.shape
    tt = _tile(T, 512)

    def body(x_ref, g_ref, t_ref, loss_ref, dx_ref, dg_ref):
        gv = g_ref[...]
        y, xhat, r = _rms_fwd(x_ref[...], gv)
        err = y - t_ref[...]
        dx, dgrow = _rms_bwd(err * (1.0 / D), xhat, r, gv)
        dx_ref[...] = dx

        @pl.when(pl.program_id(0) == 0)
        def _():
            loss_ref[...] = jnp.zeros_like(loss_ref)
            dg_ref[...] = jnp.zeros_like(dg_ref)

        loss_ref[...] += 0.5 * jnp.sum(jnp.mean(err * err, axis=-1, keepdims=True), axis=0, keepdims=True)
        dg_ref[...] += jnp.sum(dgrow, axis=0, keepdims=True)

    return pl.pallas_call(
        body, name="loss_head", grid=(T // tt,),
        in_specs=[pl.BlockSpec((tt, D), lambda i: (i, 0)), pl.BlockSpec((1, D), lambda i: (0, 0)),
                  pl.BlockSpec((tt, D), lambda i: (i, 0))],
        out_specs=[pl.BlockSpec((1, LANES), lambda i: (0, 0)), pl.BlockSpec((tt, D), lambda i: (i, 0)),
                   pl.BlockSpec((1, D), lambda i: (0, 0))],
        out_shape=[jax.ShapeDtypeStruct((1, LANES), F32), jax.ShapeDtypeStruct((T, D), F32),
                   jax.ShapeDtypeStruct((1, D), F32)],
        compiler_params=_params(),
    )(x, g, target)


def _rows(shape, pref=512):
    last = shape[-1]
    rows = 1
    for s in shape[:-1]:
        rows *= s
    tr = rows
    if rows * last > 256 * 1024:
        for cand in (pref, 256, 128, 64, 32, 16, 8):
            if rows % cand == 0:
                tr = cand
                break
    return rows, last, tr


def _elementwise(fn, name, ins, n_out, out_dtype=F32):
    shape = ins[0].shape
    rows, last, tr = _rows(shape)
    flat = [a.reshape(rows, last) for a in ins]
    n_in = len(ins)

    def body(*refs):
        res = fn(*[r[...] for r in refs[:n_in]])
        if n_out == 1:
            res = (res,)
        for r, v in zip(refs[n_in:], res):
            r[...] = v.astype(r.dtype)

    spec = pl.BlockSpec((tr, last), lambda i: (i, 0))
    outs = pl.pallas_call(
        body, name=name, grid=(rows // tr,),
        in_specs=[spec] * n_in, out_specs=[spec] * n_out,
        out_shape=[jax.ShapeDtypeStruct((rows, last), out_dtype)] * n_out,
        compiler_params=_params(),
    )(*flat)
    return [o.reshape(shape) for o in outs]


def _add_pairs(gs, os, c_idx):
    n = len(gs)
    halves = [(g.shape[1] // 2, g.shape[2]) for g in gs]

    def body(c_ref, *refs):
        for a in range(n):
            refs[2 * n + a][...] = refs[2 * a][...] + refs[2 * a + 1][...]

    in_specs = []
    for h, C in halves:
        in_specs += [pl.BlockSpec((None, h, C), lambda q, c: (q, c[0], 0)), pl.BlockSpec((None, h, C), lambda q, c: (q, 0, 0))]
    return pl.pallas_call(
        body, name="add_pairs",
        grid_spec=pltpu.PrefetchScalarGridSpec(
            num_scalar_prefetch=1, grid=(N_CHIPS,), in_specs=in_specs,
            out_specs=[pl.BlockSpec((None, h, C), lambda q, c: (q, 0, 0)) for h, C in halves]),
        out_shape=[jax.ShapeDtypeStruct((N_CHIPS, h, C), F32) for h, C in halves],
        compiler_params=_params(),
    )(c_idx.astype(jnp.int32).reshape(1), *[x for pair in zip(gs, os) for x in pair])


def _add_chips(ps, rs, q_idx):
    n = len(ps)
    steps = 2
    blocks = [(p.shape[1] // steps, p.shape[2]) for p in ps]

    def body(q_ref, *refs):
        for a in range(n):
            p_ref, r0_ref, r1_ref, r2_ref = refs[4 * a:4 * a + 4]
            refs[4 * n + a][...] = (p_ref[...] + r0_ref[...]) + (r1_ref[...] + r2_ref[...])

    def arrived(tr, C, k):
        return pl.BlockSpec((None, tr, C), lambda i, q: (k, i, 0))

    in_specs, operands = [], []
    for (tr, C), p, r in zip(blocks, ps, rs):
        in_specs += [pl.BlockSpec((None, tr, C), lambda i, q: (q[0], i, 0)), arrived(tr, C, 0), arrived(tr, C, 1),
                     arrived(tr, C, 2)]
        operands += [p, r, r, r]
    return pl.pallas_call(
        body, name="add_chips",
        grid_spec=pltpu.PrefetchScalarGridSpec(
            num_scalar_prefetch=1, grid=(steps,), in_specs=in_specs,
            out_specs=[pl.BlockSpec((tr, C), lambda i, q: (i, 0)) for tr, C in blocks]),
        out_shape=[jax.ShapeDtypeStruct((p.shape[1], p.shape[2]), F32) for p in ps],
        compiler_params=_params(),
    )(q_idx.astype(jnp.int32).reshape(1), *operands)


def _adamw(w, g, m, v):
    m = ADAM_B1 * m + (1.0 - ADAM_B1) * g
    v = ADAM_B2 * v + (1.0 - ADAM_B2) * jnp.square(g)
    m_hat = m / (1.0 - ADAM_B1 ** ADAM_STEP)
    v_hat = v / (1.0 - ADAM_B2 ** ADAM_STEP)
    delta = -ADAM_LR * (m_hat / (jnp.sqrt(v_hat) + ADAM_EPS) + ADAM_WD * w)
    return delta, m, v


def _place():
    x, y, c = lax.axis_index("x"), lax.axis_index("y"), lax.axis_index("c")
    chips = [(1 - x, y), (x, 1 - y), (1 - x, 1 - y)]
    return x, y, c, chips


def _remote(src, dst, ssem, rsem, k, dev):
    return pltpu.make_async_remote_copy(src_ref=src, dst_ref=dst, send_sem=ssem.at[k], recv_sem=rsem.at[k],
                                        device_id=dev, device_id_type=MESH)


def _gather_weights(shards):
    n = len(shards)
    halves = [s.shape[1] // 2 for s in shards]

    def body(*refs):
        src, out = refs[:n], refs[n:2 * n]
        ssem, rsem = refs[2 * n:]
        x, y, c, chips = _place()
        me_q = 2 * x + y
        sib = (x, y, 1 - c)

        def half(a, q, cc):
            return out[a].at[q, :, pl.ds(cc * halves[a], halves[a]), :]

        first = []
        for a in range(n):
            mine = src[a].at[:, pl.ds(c * halves[a], halves[a]), :]
            for r, chip in enumerate(chips):
                first.append(_remote(mine, half(a, me_q, c), ssem, rsem, a * 3 + r, (*chip, c)))
        for cp in first:
            cp.start()
        passed = []
        for a in range(n):
            for r, chip in enumerate(chips):
                q = 2 * chip[0] + chip[1]
                k = a * 3 + r
                _remote(half(a, q, c), half(a, q, c), ssem, rsem, k, (*chip, c)).wait_recv()
                cp = _remote(half(a, q, c), half(a, q, c), ssem, rsem, 3 * n + k, sib)
                cp.start()
                passed.append(cp)
        for a in range(n):
            for r, chip in enumerate(chips):
                q = 2 * chip[0] + chip[1]
                _remote(half(a, q, 1 - c), half(a, q, 1 - c), ssem, rsem, 3 * n + a * 3 + r, sib).wait_recv()
        for cp in first + passed:
            cp.wait_send()

    return pl.pallas_call(
        body, name="gather_weights",
        in_specs=[HBM_SPEC] * n, out_specs=[HBM_SPEC] * n,
        out_shape=[jax.ShapeDtypeStruct((N_CHIPS,) + s.shape, s.dtype) for s in shards],
        scratch_shapes=[pltpu.SemaphoreType.DMA((6 * n,)), pltpu.SemaphoreType.DMA((6 * n,))],
        compiler_params=_params(has_side_effects=True),
    )(*shards)


def _gather_over_ici(shards):
    n = len(shards)
    halves = [s.shape[1] // 2 for s in shards]

    def copies(src, out, ssem, rsem):
        x, y, c, chips = _place()
        me_q = 2 * x + y
        res = []
        for a in range(n):
            rows = pl.ds(c * halves[a], halves[a])
            mine = src[a].at[:, rows, :]
            for r, chip in enumerate(chips):
                dev = (*chip, c)
                res.append((_remote(mine, out[a].at[me_q, :, rows, :], ssem, rsem, a * 3 + r, dev),
                            _remote(mine, out[a].at[2 * chip[0] + chip[1], :, rows, :], ssem, rsem, a * 3 + r, dev)))
        return res

    return _Hosted(list(shards), [jax.ShapeDtypeStruct((N_CHIPS,) + s.shape, s.dtype) for s in shards], 3 * n, copies)


def _pass_over_d2d(gathered):
    n = len(gathered)
    halves = [g.shape[2] // 2 for g in gathered]

    def copies(_, out, ssem, rsem):
        x, y, c, chips = _place()
        sib = (x, y, 1 - c)
        res = []
        for a in range(n):
            for r, chip in enumerate(chips):
                q = 2 * chip[0] + chip[1]
                mine = out[a].at[q, :, pl.ds(c * halves[a], halves[a]), :]
                theirs = out[a].at[q, :, pl.ds((1 - c) * halves[a], halves[a]), :]
                res.append((_remote(mine, mine, ssem, rsem, a * 3 + r, sib),
                            _remote(theirs, theirs, ssem, rsem, a * 3 + r, sib)))
        return res

    return _Hosted(list(gathered), [jax.ShapeDtypeStruct(g.shape, g.dtype) for g in gathered], 3 * n, copies,
                   in_place=True)


def _pass_to_sibling(gathered):
    n = len(gathered)
    halves = [g.shape[2] // 2 for g in gathered]

    def body(*refs):
        out = refs[n:2 * n]
        ssem, rsem = refs[2 * n:]
        x, y, c, chips = _place()
        sib = (x, y, 1 - c)

        def half(a, q, cc):
            return out[a].at[q, :, pl.ds(cc * halves[a], halves[a]), :]

        cps = []
        for a in range(n):
            for r, chip in enumerate(chips):
                q = 2 * chip[0] + chip[1]
                cps.append(_remote(half(a, q, c), half(a, q, c), ssem, rsem, a * 3 + r, sib))
        for cp in cps:
            cp.start()
        for a in range(n):
            for r, chip in enumerate(chips):
                q = 2 * chip[0] + chip[1]
                _remote(half(a, q, 1 - c), half(a, q, 1 - c), ssem, rsem, a * 3 + r, sib).wait_recv()
        for cp in cps:
            cp.wait_send()

    return pl.pallas_call(
        body, name="pass_to_sibling",
        in_specs=[HBM_SPEC] * n, out_specs=[HBM_SPEC] * n,
        out_shape=[jax.ShapeDtypeStruct(g.shape, g.dtype) for g in gathered],
        input_output_aliases={a: a for a in range(n)},
        scratch_shapes=[pltpu.SemaphoreType.DMA((3 * n,)), pltpu.SemaphoreType.DMA((3 * n,))],
        compiler_params=_params(has_side_effects=True),
    )(*gathered)


def _scatter_over_ici(parts):
    n = len(parts)

    def copies(src, out, ssem, rsem):
        x, y, c, chips = _place()
        res = []
        for a in range(n):
            for r, chip in enumerate(chips):
                cp = _remote(src[a].at[2 * chip[0] + chip[1]], out[a].at[r], ssem, rsem, a * 3 + r, (*chip, c))
                res.append((cp, cp))
        return res

    return _Hosted(list(parts), [jax.ShapeDtypeStruct((3,) + p.shape[1:], F32) for p in parts], 3 * n, copies)


def _swap_over_d2d(grads):
    n = len(grads)
    halves = [g.shape[1] // 2 for g in grads]

    def copies(src, out, ssem, rsem):
        x, y, c, _ = _place()
        res = []
        for a in range(n):
            cp = _remote(src[a].at[:, pl.ds((1 - c) * halves[a], halves[a]), :], out[a], ssem, rsem, a, (x, y, 1 - c))
            res.append((cp, cp))
        return res

    return _Hosted(list(grads), [jax.ShapeDtypeStruct((N_CHIPS, h, g.shape[2]), F32) for g, h in zip(grads, halves)],
                   n, copies)


def _swap_halves(grads):
    n = len(grads)
    halves = [g.shape[1] // 2 for g in grads]

    def body(*refs):
        src, out = refs[:n], refs[n:2 * n]
        ssem, rsem = refs[2 * n:]
        x, y, c, _ = _place()
        cps = [_remote(src[a].at[:, pl.ds((1 - c) * halves[a], halves[a]), :], out[a], ssem, rsem, a, (x, y, 1 - c))
               for a in range(n)]
        for cp in cps:
            cp.start()
        for cp in cps:
            cp.wait()

    return pl.pallas_call(
        body, name="swap_halves",
        in_specs=[HBM_SPEC] * n, out_specs=[HBM_SPEC] * n,
        out_shape=[jax.ShapeDtypeStruct((N_CHIPS, h, g.shape[2]), F32) for g, h in zip(grads, halves)],
        scratch_shapes=[pltpu.SemaphoreType.DMA((n,)), pltpu.SemaphoreType.DMA((n,))],
        compiler_params=_params(has_side_effects=True),
    )(*grads)


def _swap_reduced_over_d2d(reduced):
    n = len(reduced)

    def copies(src, out, ssem, rsem):
        x, y, c, _ = _place()
        res = []
        for a in range(n):
            cp = _remote(src[a], out[a], ssem, rsem, a, (x, y, 1 - c))
            res.append((cp, cp))
        return res

    return _Hosted(list(reduced), [jax.ShapeDtypeStruct(r.shape, F32) for r in reduced], n, copies)


def _swap_reduced(reduced):
    n = len(reduced)

    def body(*refs):
        src, out = refs[:n], refs[n:2 * n]
        ssem, rsem = refs[2 * n:]
        x, y, c, _ = _place()
        cps = [_remote(src[a], out[a], ssem, rsem, a, (x, y, 1 - c)) for a in range(n)]
        for cp in cps:
            cp.start()
        for cp in cps:
            cp.wait()

    return pl.pallas_call(
        body, name="swap_reduced",
        in_specs=[HBM_SPEC] * n, out_specs=[HBM_SPEC] * n,
        out_shape=[jax.ShapeDtypeStruct(r.shape, F32) for r in reduced],
        scratch_shapes=[pltpu.SemaphoreType.DMA((n,)), pltpu.SemaphoreType.DMA((n,))],
        compiler_params=_params(has_side_effects=True),
    )(*reduced)


def _allreduce_small(buf, hosted=None):
    R, L = buf.shape

    def body(*refs):
        (buf_ref, out_ref, pair_ref, chip_ref, ssem, rsem), start, wait = _host(hosted, refs, 1, 1, True, True)
        start()
        x, y, c, chips = _place()
        me_q = 2 * x + y
        pair_ref[c] = buf_ref[...]
        to_sib = _remote(buf_ref, pair_ref.at[c], ssem, rsem, 0, (x, y, 1 - c))
        to_sib.start()
        _remote(buf_ref, pair_ref.at[1 - c], ssem, rsem, 0, (x, y, 1 - c)).wait_recv()
        chip_ref[me_q] = pair_ref[0] + pair_ref[1]
        cps = [_remote(chip_ref.at[me_q], chip_ref.at[me_q], ssem, rsem, 1 + r, (*chip, c))
               for r, chip in enumerate(chips)]
        for cp in cps:
            cp.start()
        for r, chip in enumerate(chips):
            q = 2 * chip[0] + chip[1]
            _remote(chip_ref.at[q], chip_ref.at[q], ssem, rsem, 1 + r, (*chip, c)).wait_recv()
        out_ref[...] = (chip_ref[0] + chip_ref[1]) + (chip_ref[2] + chip_ref[3])
        to_sib.wait_send()
        for cp in cps:
            cp.wait_send()
        wait()

    h_ins = hosted.ins if hosted else []
    res = pl.pallas_call(
        body, name="allreduce_small",
        in_specs=[VMEM_SPEC] + [HBM_SPEC] * len(h_ins), out_specs=[VMEM_SPEC] + [HBM_SPEC] * len(h_ins),
        out_shape=[jax.ShapeDtypeStruct((R, L), F32)] + (hosted.out_shapes if hosted else []),
        scratch_shapes=[pltpu.VMEM((2, R, L), F32), pltpu.VMEM((N_CHIPS, R, L), F32),
                        pltpu.SemaphoreType.DMA((4,)), pltpu.SemaphoreType.DMA((4,))]
        + (hosted.sems() if hosted else []),
        compiler_params=_params(has_side_effects=True),
    )(buf, *h_ins)
    return res[0], res[1:]


def _pack(arrays):
    flat = jnp.concatenate([a.reshape(-1) for a in arrays])
    pad = (-flat.shape[0]) % (8 * LANES)
    return jnp.pad(flat, (0, pad)).reshape(-1, LANES)


def _unpack(buf, like):
    flat = buf.reshape(-1)
    out, off = [], 0
    for a in like:
        out.append(flat[off:off + a.size].reshape(a.shape))
        off += a.size
    return out


def _block_diag(pw):
    rows = []
    for gi in range(len(POOL_WINDOWS)):
        blocks = [pw[gi] if gj == gi else jnp.zeros_like(pw[gi]) for gj in range(len(POOL_WINDOWS))]
        rows.append(jnp.concatenate(blocks, axis=1))
    return jnp.concatenate(rows, axis=0)


def kernel(x, norm1, w_in, pool_w, pool_scale, sg_norm, sg_w, sg_b, w_out, norm2, w_up, w_down, final_norm, loss_target, m_norm1, m_w_in, m_pool_w, m_pool_scale, m_sg_norm, m_sg_w, m_sg_b, m_w_out, m_norm2, m_w_up, m_w_down, m_final_norm, v_norm1, v_w_in, v_pool_w, v_pool_scale, v_sg_norm, v_sg_w, v_sg_b, v_w_out, v_norm2, v_w_up, v_w_down, v_final_norm):
    depth = norm1.shape[0]
    T = x.shape[1]
    xs = x.reshape(T, D_MODEL)
    target = loss_target.reshape(T, D_MODEL)

    assert depth == 2
    c_idx = lax.axis_index("c")
    q_idx = 2 * lax.axis_index("x") + lax.axis_index("y")
    own = [w.astype(BF16) for w in (w_in, w_out, w_up, w_down)]
    gathered = {(0, 0): _gather_weights([own[0][:1]])[0]}

    def full(a, l, axis):
        blocks = lax.dynamic_update_slice(gathered[(a, l)], own[a][l][None, None], (q_idx, 0, 0, 0))[:, 0]
        if axis is None:
            return blocks
        if axis == 0:
            return blocks.reshape(-1, blocks.shape[-1])
        return jnp.concatenate([blocks[q] for q in range(N_CHIPS)], axis=axis)

    half_way = {}

    def gather_behind(call, keys, at_once):
        res, over_ici = call(_gather_over_ici([own[a][l:l + 1] for a, l in keys]))
        gathered.update(zip(keys[:at_once], _pass_to_sibling(over_ici[:at_once])))
        half_way.update(zip(keys[at_once:], over_ici[at_once:]))
        return res

    def pass_behind(call, keys):
        res, done = call(_pass_over_d2d([half_way.pop(k) for k in keys]))
        gathered.update(zip(keys, done))
        return res

    tril = jnp.tril(jnp.ones((CHUNK, CHUNK), F32))
    saved = []
    cur = xs
    wi, wo, wu, wd = {}, {}, {}, {}
    for l in range(depth):
        wbd = _block_diag(pool_w[l]).astype(BF16)
        wm = sg_w[l] * tril
        wm_s = wm.reshape(SG_HEADS * CHUNK, CHUNK).astype(BF16)
        wmt_s = jnp.swapaxes(wm, 1, 2).reshape(SG_HEADS * CHUNK, CHUNK).astype(BF16)
        bias = jnp.repeat(sg_b[l].T, SB_HD, axis=1)
        n1, n2 = norm1[l][None], norm2[l][None]
        psc, sgn = pool_scale[l][None], sg_norm[l][None]
        wi[l] = full(0, l, 1)
        proj, h, qkv = _inproj_fwd(cur, n1, wi[l])
        ya = _pool_fwd(proj, wbd, psc)
        yb = _sg_fwd(proj, wm_s, bias, sgn)
        if l == 0:
            yc = gather_behind(lambda hosted: _attn_fwd(qkv, hosted), [(1, 0), (2, 0), (3, 0)], 1)
            wo[l] = full(1, l, 0)
            x1, ymix = pass_behind(lambda hosted: _outproj_fwd(cur, ya, yb, yc, wo[l], hosted), [(2, 0), (3, 0)])
        else:
            yc = pass_behind(lambda hosted: _attn_fwd(qkv, hosted), [(1, l), (2, l), (3, l)])
            wo[l] = full(1, l, 0)
            (x1, ymix), _ = _outproj_fwd(cur, ya, yb, yc, wo[l])
        wu[l], wd[l] = full(2, l, None), full(3, l, 0)
        if l == 0:
            x2, h2, u, act = gather_behind(lambda hosted: _mlp_fwd(x1, n2, wu[l], wd[l], hosted),
                                           [(0, 1), (1, 1), (2, 1), (3, 1)], 1)
        else:
            (x2, h2, u, act), _ = _mlp_fwd(x1, n2, wu[l], wd[l])
        saved.append(dict(x0=cur, x1=x1, proj=proj, h=h, qkv=qkv, yc=yc, ymix=ymix, h2=h2, u=u, act=act,
                          wbd=wbd, wm_s=wm_s, wmt_s=wmt_s, bias=bias, n1=n1, n2=n2, psc=psc, sgn=sgn))
        cur = x2

    loss_row, dcur, d_final = _loss_head(cur, final_norm[None], target)

    small = [None] * depth
    grads, parts, reduced = {}, {}, {}

    def pair_up(keys, swapped):
        parts.update(zip(keys, _add_pairs([grads[k] for k in keys], swapped, c_idx)))

    def chip_up(keys, arrived):
        reduced.update(zip(keys, _add_chips([parts[k] for k in keys], arrived, q_idx)))

    for l in reversed(range(depth)):
        s = saved[l]
        if l == 0:
            keys = [(2, 1), (3, 1)]
            (dx1, du, d_n2), arrived = _mlp_bwd(dcur, s["x1"], s["n2"], s["u"], wu[l], wd[l],
                                                _scatter_over_ici([parts[k] for k in keys]))
            chip_up(keys, arrived)
        else:
            (dx1, du, d_n2), _ = _mlp_bwd(dcur, s["x1"], s["n2"], s["u"], wu[l], wd[l])
        if l == 0:
            keys = [(0, 1)]
            grads[(2, l)], arrived = _tn_matmul(s["h2"], du, "grad_w_up", n_split=N_CHIPS,
                                                hosted=_scatter_over_ici([parts[k] for k in keys]))
            chip_up(keys, arrived)
            keys = [(1, 1)]
            g_down, arrived = _tn_matmul(s["act"], dcur, "grad_w_down",
                                         hosted=_scatter_over_ici([parts[k] for k in keys]))
            chip_up(keys, arrived)
        else:
            grads[(2, l)] = _tn_matmul(s["h2"], du, "grad_w_up", n_split=N_CHIPS)
            g_down = _tn_matmul(s["act"], dcur, "grad_w_down")
        grads[(3, l)] = g_down[0].reshape(N_CHIPS, D_FF // N_CHIPS, D_MODEL)
        dymix = _nt_matmul(dx1, wo[l])
        grads[(1, l)] = _tn_matmul(s["ymix"], dx1, "grad_w_out")[0].reshape(N_CHIPS, D_MODEL // N_CHIPS, D_MODEL)
        da_in, d_wbd, d_psc = _pool_bwd(s["proj"], dymix, s["wbd"], s["psc"])
        if l == 0:
            keys = [(1, 0), (2, 0), (3, 0)]
            (du_pre, dv_pre, d_wm, d_bias, d_sgn), swapped = _sg_bwd(
                s["proj"], dymix, s["wm_s"], s["wmt_s"], s["bias"], s["sgn"], _swap_over_d2d([grads[k] for k in keys]))
            pair_up(keys, swapped)
            (dq, dk, dv), arrived = _attn_bwd(s["qkv"], s["yc"], dymix, _scatter_over_ici([parts[k] for k in keys]))
            chip_up(keys, arrived)
        else:
            (du_pre, dv_pre, d_wm, d_bias, d_sgn), _ = _sg_bwd(s["proj"], dymix, s["wm_s"], s["wmt_s"], s["bias"], s["sgn"])
            keys = [(1, l), (2, l), (3, l)]
            (dq, dk, dv), swapped = _attn_bwd(s["qkv"], s["yc"], dymix, _swap_over_d2d([grads[k] for k in keys]))
            pair_up(keys, swapped)
        pieces = [da_in, du_pre, dv_pre, dq, dk, dv]
        if l == 0:
            keys = sorted(reduced)
            g_in_l, swapped = _inproj_grad(s["h"], pieces, _swap_reduced_over_d2d([reduced[k] for k in keys]))
            theirs = dict(zip(keys, swapped))
        else:
            g_in_l, _ = _inproj_grad(s["h"], pieces)
        grads[(0, l)] = g_in_l[0].reshape(D_MODEL, N_CHIPS, IN_COLS // N_CHIPS).transpose(1, 0, 2)
        if l == 0:
            keys = [(0, 0)]
            pair_up(keys, _swap_halves([grads[k] for k in keys]))
            (dx0, d_n1), arrived = _inproj_bwd(pieces, wi[l], s["x0"], s["n1"], dx1,
                                               _scatter_over_ici([parts[k] for k in keys]))
            chip_up(keys, arrived)
        else:
            keys = [(0, l)]
            (dx0, d_n1), swapped = _inproj_bwd(pieces, wi[l], s["x0"], s["n1"], dx1,
                                               _swap_over_d2d([grads[k] for k in keys]))
            pair_up(keys, swapped)
        d_pw = jnp.stack([d_wbd[gi * POOL_GW:(gi + 1) * POOL_GW, gi * POOL_GW:(gi + 1) * POOL_GW]
                          for gi in range(len(POOL_WINDOWS))])
        small[l] = dict(norm1=d_n1[0], pool_w=d_pw, pool_scale=d_psc[0], sg_norm=d_sgn[0],
                        sg_w=d_wm.reshape(SG_HEADS, CHUNK, CHUNK), sg_b=d_bias[:, :SG_HEADS].T, norm2=d_n2[0])
        dcur = dx0
    grad_x = dcur.reshape(x.shape)

    names = ["norm1", "pool_w", "pool_scale", "sg_norm", "sg_w", "sg_b", "norm2"]
    slot = jnp.zeros((1,), F32)
    small_w = [norm1, pool_w, pool_scale, sg_norm, sg_w, sg_b, norm2, final_norm, slot]
    small_m = [m_norm1, m_pool_w, m_pool_scale, m_sg_norm, m_sg_w, m_sg_b, m_norm2, m_final_norm, slot]
    small_v = [v_norm1, v_pool_w, v_pool_scale, v_sg_norm, v_sg_w, v_sg_b, v_norm2, v_final_norm, slot]
    small_g = [jnp.stack([small[l][k] for l in range(depth)]) for k in names] + [d_final[0], loss_row[0, :1]]
    keys = [(0, 0)]
    g_packed, _ = _allreduce_small(_pack(small_g))
    theirs.update(zip(keys, _swap_reduced([reduced[k] for k in keys])))

    def joined(a):
        layers = []
        for l in range(depth):
            mine, other = reduced[(a, l)], theirs[(a, l)]
            layers.append(jnp.where(c_idx == 0, jnp.concatenate([mine, other]), jnp.concatenate([other, mine])))
        return jnp.stack(layers)

    gw_in, gw_out, gw_up, gw_down = [joined(a) for a in range(4)]

    loss = _unpack(g_packed, small_w)[-1][0]
    s_delta, s_m, s_v = _elementwise(_adamw, "adamw_small", [_pack(small_w), g_packed, _pack(small_m), _pack(small_v)], 3)
    gs = dict(zip(names + ["final_norm"], _unpack(g_packed, small_w)))
    ds = dict(zip(names + ["final_norm"], _unpack(s_delta, small_w)))
    ms = dict(zip(names + ["final_norm"], _unpack(s_m, small_w)))
    vs = dict(zip(names + ["final_norm"], _unpack(s_v, small_w)))

    big_g = dict(w_in=gw_in, w_out=gw_out, w_up=gw_up, w_down=gw_down)
    big_w = dict(w_in=(w_in, m_w_in, v_w_in), w_out=(w_out, m_w_out, v_w_out),
                 w_up=(w_up, m_w_up, v_w_up), w_down=(w_down, m_w_down, v_w_down))
    for k, (w, m, v) in big_w.items():
        operands = [w, big_g[k], m, v]
        if k == "w_in":
            operands = [jnp.swapaxes(o, 1, 2) for o in operands]
        ds[k], ms[k], vs[k] = _elementwise(_adamw, "adamw_" + k, operands, 3)
        if k == "w_in":
            ds[k], ms[k], vs[k] = [jnp.swapaxes(o, 1, 2) for o in (ds[k], ms[k], vs[k])]
        gs[k] = big_g[k]

    order = ["norm1", "w_in", "pool_w", "pool_scale", "sg_norm", "sg_w", "sg_b", "w_out", "norm2", "w_up", "w_down",
             "final_norm"]
    return (loss, grad_x, *[gs[k] for k in order], *[ds[k] for k in order], *[ms[k] for k in order],
            *[vs[k] for k in order])
```

```python
import jax
import jax.numpy as jnp
from jax import lax
from jax.experimental import pallas as pl
from jax.experimental.pallas import tpu as pltpu

F32 = jnp.float32
BF16 = jnp.bfloat16
MESH = pl.DeviceIdType.MESH

EPS = 1e-6
D_MODEL = 1024
POOL_WIDTH = 256
SG_WIDTH = 256
SB_WIDTH = 512
POOL_WINDOWS = (2, 4, 8, 16)
POOL_GW = 64
POOL_HALO = 16
CHUNK = 128
SG_HEADS = 4
SB_HD = 64
SB_SCALE = 0.125
IN_COLS = 2304
QKV_OFF = 768
D_FF = 4096
N_CHIPS = 4
LANES = 128
VMEM_LIMIT = 56 * 1024 * 1024
MLP_CHUNK = 512
ATTN_TILE = 256
UNDERFLOW = -104.0

ADAM_LR = 0.001
ADAM_B1 = 0.9
ADAM_B2 = 0.999
ADAM_EPS = 1e-08
ADAM_WD = 0.01
ADAM_STEP = 10

HBM_SPEC = pl.BlockSpec(memory_space=pl.ANY)
VMEM_SPEC = pl.BlockSpec(memory_space=pltpu.VMEM)


def _params(**kw):
    return pltpu.CompilerParams(vmem_limit_bytes=VMEM_LIMIT, **kw)


def _tile(n, pref):
    if n <= pref:
        return n
    for t in range(pref - pref % LANES, 0, -LANES):
        if n % t == 0:
            return t
    raise ValueError((n, pref))


def _nn(a, b):
    return jnp.dot(a, b, preferred_element_type=F32)


def _nt(a, b):
    return lax.dot_general(a, b, (((1,), (1,)), ((), ())), preferred_element_type=F32)


def _tn(a, b):
    return lax.dot_general(a, b, (((0,), (0,)), ((), ())), preferred_element_type=F32)


def _rms_fwd(x, g):
    r = lax.rsqrt(jnp.mean(x * x, axis=-1, keepdims=True) + EPS)
    xhat = x * r
    return xhat * g, xhat, r


def _rms_bwd(dy, xhat, r, g):
    dxhat = dy * g
    dx = r * (dxhat - xhat * jnp.mean(dxhat * xhat, axis=-1, keepdims=True))
    return dx, dy * xhat


_GELU_K = 0.7978845608028654
_GELU_C = 0.044715


def _gelu(x):
    return 0.5 * x * (1.0 + jnp.tanh(_GELU_K * (x + _GELU_C * x * x * x)))


def _gelu_and_grad(x):
    x2 = x * x
    t = jnp.tanh(_GELU_K * (x + _GELU_C * x2 * x))
    half = 0.5 * (1.0 + t)
    return x * half, half + 0.5 * x * (1.0 - t * t) * _GELU_K * (1.0 + 3.0 * _GELU_C * x2)


def _inproj_fwd(x, g, w):
    T, D = x.shape
    N = w.shape[1]
    tt = _tile(T, 1024)

    def body(x_ref, g_ref, w_ref, proj_ref, h_ref, qkv_ref):
        h, _, _ = _rms_fwd(x_ref[...], g_ref[...])
        hb = h.astype(BF16)
        h_ref[...] = hb
        p = _nn(hb, w_ref[...])
        proj_ref[...] = p[:, :QKV_OFF]
        qkv_ref[...] = p[:, QKV_OFF:].astype(BF16)

    return pl.pallas_call(
        body, name="inproj_fwd", grid=(T // tt,),
        in_specs=[pl.BlockSpec((tt, D), lambda i: (i, 0)), pl.BlockSpec((1, D), lambda i: (0, 0)),
                  pl.BlockSpec((D, N), lambda i: (0, 0))],
        out_specs=[pl.BlockSpec((tt, QKV_OFF), lambda i: (i, 0)), pl.BlockSpec((tt, D), lambda i: (i, 0)),
                   pl.BlockSpec((tt, N - QKV_OFF), lambda i: (i, 0))],
        out_shape=[jax.ShapeDtypeStruct((T, QKV_OFF), F32), jax.ShapeDtypeStruct((T, D), BF16),
                   jax.ShapeDtypeStruct((T, N - QKV_OFF), BF16)],
        compiler_params=_params(),
    )(x, g, w)


def _inproj_bwd(pieces, w, x, g, dres, hosted=None):
    T, D = x.shape
    N = w.shape[1]
    tt = _tile(T, 512)
    nt = T // tt
    widths = [p.shape[1] for p in pieces]
    offs = [sum(widths[:k]) for k in range(len(widths))]
    assert sum(widths) == N
    n_p = len(pieces)

    def body(*refs):
        i = pl.program_id(0)
        own, start, wait = _host(hosted, refs, n_p + 4, 2, i == 0, i == nt - 1)
        start()
        p_refs = own[:n_p]
        w_ref, x_ref, g_ref, dres_ref, dx_ref, dg_ref, dproj_ref = own[n_p:]
        for p_ref, o, wd in zip(p_refs, offs, widths):
            dproj_ref[:, o:o + wd] = p_ref[...].astype(BF16)
        dh = _nt(dproj_ref[...], w_ref[...])
        gv = g_ref[...]
        _, xhat, r = _rms_fwd(x_ref[...], gv)
        dx, dgrow = _rms_bwd(dh, xhat, r, gv)
        dx_ref[...] = dres_ref[...] + dx

        @pl.when(i == 0)
        def _():
            dg_ref[...] = jnp.zeros_like(dg_ref)

        dg_ref[...] += jnp.sum(dgrow, axis=0, keepdims=True)
        wait()

    h_ins = hosted.ins if hosted else []
    res = pl.pallas_call(
        body, name="inproj_bwd_hosting" if hosted else "inproj_bwd", grid=(nt,),
        in_specs=[pl.BlockSpec((tt, wd), lambda i: (i, 0)) for wd in widths] + [
            pl.BlockSpec((D, N), lambda i: (0, 0)), pl.BlockSpec((tt, D), lambda i: (i, 0)),
            pl.BlockSpec((1, D), lambda i: (0, 0)), pl.BlockSpec((tt, D), lambda i: (i, 0))] + [HBM_SPEC] * len(h_ins),
        out_specs=[pl.BlockSpec((tt, D), lambda i: (i, 0)), pl.BlockSpec((1, D), lambda i: (0, 0))]
        + [HBM_SPEC] * len(h_ins),
        out_shape=[jax.ShapeDtypeStruct((T, D), F32), jax.ShapeDtypeStruct((1, D), F32)]
        + (hosted.out_shapes if hosted else []),
        scratch_shapes=[pltpu.VMEM((tt, N), BF16)] + (hosted.sems() if hosted else []),
        compiler_params=_params(has_side_effects=hosted is not None),
    )(*pieces, w, x, g, dres, *h_ins)
    return res[:2], res[2:]


def _inproj_grad(h, pieces, hosted=None):
    T, D = h.shape
    tt = _tile(T, 1024)
    nt = T // tt
    widths = [p.shape[1] for p in pieces]
    offs = [sum(widths[:k]) for k in range(len(widths))]
    N = sum(widths)
    n_p = len(pieces)

    def body(*refs):
        t = pl.program_id(0)
        own, start, wait = _host(hosted, refs, n_p + 1, 1, t == 0, t == nt - 1)
        start()
        h_ref, p_refs, o_ref = own[0], own[1:1 + n_p], own[1 + n_p]

        @pl.when(t == 0)
        def _():
            o_ref[...] = jnp.zeros_like(o_ref)

        hv = h_ref[...]
        for p_ref, o, wd in zip(p_refs, offs, widths):
            o_ref[:, o:o + wd] += _tn(hv, p_ref[...].astype(BF16))
        wait()

    h_ins = hosted.ins if hosted else []
    res = pl.pallas_call(
        body, name="grad_w_in_hosting" if hosted else "grad_w_in", grid=(nt,),
        in_specs=[pl.BlockSpec((tt, D), lambda t: (t, 0))] + [pl.BlockSpec((tt, wd), lambda t: (t, 0)) for wd in widths]
        + [HBM_SPEC] * len(h_ins),
        out_specs=[pl.BlockSpec((None, D, N), lambda t: (0, 0, 0))] + [HBM_SPEC] * len(h_ins),
        out_shape=[jax.ShapeDtypeStruct((1, D, N), F32)] + (hosted.out_shapes if hosted else []),
        scratch_shapes=hosted.sems() if hosted else [],
        compiler_params=_params(has_side_effects=hosted is not None),
    )(h, *pieces, *h_ins)
    return res[0], res[1:]


def _pool_select(s2, s4, s8, s16, grp):
    return jnp.where(grp == 0, s2, jnp.where(grp == 1, s4, jnp.where(grp == 2, s8, s16)))


def _pool_count(t_glob, grp):
    win = jnp.where(grp == 0, 2, jnp.where(grp == 1, 4, jnp.where(grp == 2, 8, 16)))
    return jnp.minimum(t_glob + 1, win).astype(F32)


def _pool_diff(a, halo, base, tt):
    n = tt + POOL_HALO
    ext = jnp.concatenate([halo, a], axis=0)
    s2 = ext + pltpu.roll(ext, 1, 0)
    s4 = s2 + pltpu.roll(s2, 2, 0)
    s8 = s4 + pltpu.roll(s4, 4, 0)
    s16 = s8 + pltpu.roll(s8, 8, 0)
    grp = lax.broadcasted_iota(jnp.int32, (n, POOL_WIDTH), 1) // POOL_GW
    t_glob = lax.broadcasted_iota(jnp.int32, (n, POOL_WIDTH), 0) + (base - POOL_HALO)
    pooled = _pool_select(s2, s4, s8, s16, grp) / _pool_count(t_glob, grp)
    return pooled[POOL_HALO:] - a


def _pool_specs(T, tt):
    hb = tt // POOL_HALO
    return [pl.BlockSpec((tt, POOL_WIDTH), lambda i: (i, 0)),
            pl.BlockSpec((POOL_HALO, POOL_WIDTH), lambda i: (jnp.maximum(i * hb - 1, 0), 0))]


def _pool_fwd(proj, wbd, scale):
    T = proj.shape[0]
    tt = _tile(T, 512)

    def body(a_ref, halo_ref, w_ref, sc_ref, y_ref):
        i = pl.program_id(0)
        halo = jnp.where(i > 0, halo_ref[...], 0.0)
        d = _pool_diff(a_ref[...], halo, i * tt, tt)
        y_ref[...] = _nn(d.astype(BF16), w_ref[...]) * sc_ref[...]

    return pl.pallas_call(
        body, name="pool_fwd", grid=(T // tt,),
        in_specs=_pool_specs(T, tt) + [pl.BlockSpec((POOL_WIDTH, POOL_WIDTH), lambda i: (0, 0)),
                                       pl.BlockSpec((1, POOL_WIDTH), lambda i: (0, 0))],
        out_specs=pl.BlockSpec((tt, POOL_WIDTH), lambda i: (i, 0)),
        out_shape=jax.ShapeDtypeStruct((T, POOL_WIDTH), F32),
        compiler_params=_params(),
    )(proj, proj, wbd, scale)


def _pool_bwd(proj, dymix, wbd, scale):
    T = proj.shape[0]
    tt = _tile(T, 512)
    hb = tt // POOL_HALO
    nblk = T // tt
    n = tt + POOL_HALO

    def body(a_ref, halo_ref, dy_ref, dyn_ref, w_ref, sc_ref, da_ref, dw_ref, dsc_ref):
        i = pl.program_id(0)
        halo = jnp.where(i > 0, halo_ref[...], 0.0)
        d = _pool_diff(a_ref[...], halo, i * tt, tt)
        db = d.astype(BF16)
        wv = w_ref[...]
        sc = sc_ref[...]
        dy = dy_ref[...]
        dys = dy * sc

        @pl.when(i == 0)
        def _():
            dw_ref[...] = jnp.zeros_like(dw_ref)
            dsc_ref[...] = jnp.zeros_like(dsc_ref)

        dsc_ref[...] += jnp.sum(dy * _nn(db, wv), axis=0, keepdims=True)
        dw_ref[...] += _tn(db, dys.astype(BF16))
        dyn = jnp.where(i < nblk - 1, dyn_ref[...], 0.0) * sc
        dd = _nt(jnp.concatenate([dys, dyn], axis=0).astype(BF16), wv)
        grp = lax.broadcasted_iota(jnp.int32, (n, POOL_WIDTH), 1) // POOL_GW
        t_glob = lax.broadcasted_iota(jnp.int32, (n, POOL_WIDTH), 0) + i * tt
        e = dd / _pool_count(t_glob, grp)
        r2 = e + pltpu.roll(e, n - 1, 0)
        r4 = r2 + pltpu.roll(r2, n - 2, 0)
        r8 = r4 + pltpu.roll(r4, n - 4, 0)
        r16 = r8 + pltpu.roll(r8, n - 8, 0)
        da_ref[...] = (_pool_select(r2, r4, r8, r16, grp) - dd)[:tt].astype(BF16)

    return pl.pallas_call(
        body, name="pool_bwd", grid=(nblk,),
        in_specs=_pool_specs(T, tt) + [
            pl.BlockSpec((tt, POOL_WIDTH), lambda i: (i, 0)),
            pl.BlockSpec((POOL_HALO, POOL_WIDTH), lambda i: (jnp.minimum((i + 1) * hb, T // POOL_HALO - 1), 0)),
            pl.BlockSpec((POOL_WIDTH, POOL_WIDTH), lambda i: (0, 0)), pl.BlockSpec((1, POOL_WIDTH), lambda i: (0, 0))],
        out_specs=[pl.BlockSpec((tt, POOL_WIDTH), lambda i: (i, 0)),
                   pl.BlockSpec((POOL_WIDTH, POOL_WIDTH), lambda i: (0, 0)),
                   pl.BlockSpec((1, POOL_WIDTH), lambda i: (0, 0))],
        out_shape=[jax.ShapeDtypeStruct((T, POOL_WIDTH), BF16),
                   jax.ShapeDtypeStruct((POOL_WIDTH, POOL_WIDTH), F32),
                   jax.ShapeDtypeStruct((1, POOL_WIDTH), F32)],
        compiler_params=_params(),
    )(proj, proj, dymix, dymix, wbd, scale)


def _head_select(stacked, grp):
    out = jnp.where(grp == 0, stacked[0:CHUNK], 0.0)
    for h in range(1, SG_HEADS):
        out = out + jnp.where(grp == h, stacked[h * CHUNK:(h + 1) * CHUNK], 0.0)
    return out


def _sg_specs(tt):
    return [pl.BlockSpec((tt, SG_WIDTH), lambda i: (i, 1)), pl.BlockSpec((tt, SG_WIDTH), lambda i: (i, 2))]


def _sg_fwd(proj, wm, bias, g):
    T = proj.shape[0]
    tt = _tile(T, 512)

    def body(u_ref, v_ref, wm_ref, b_ref, g_ref, y_ref):
        zu = _gelu(u_ref[...])
        vn, _, _ = _rms_fwd(_gelu(v_ref[...]), g_ref[...])
        grp = lax.broadcasted_iota(jnp.int32, (CHUNK, SG_WIDTH), 1) // SB_HD
        for n in range(tt // CHUNK):
            rows = slice(n * CHUNK, (n + 1) * CHUNK)
            sv = _head_select(_nn(wm_ref[...], vn[rows].astype(BF16)), grp) + b_ref[...]
            y_ref[rows, :] = zu[rows] * sv

    return pl.pallas_call(
        body, name="sg_fwd", grid=(T // tt,),
        in_specs=_sg_specs(tt) + [pl.BlockSpec((SG_HEADS * CHUNK, CHUNK), lambda i: (0, 0)),
                                  pl.BlockSpec((CHUNK, SG_WIDTH), lambda i: (0, 0)),
                                  pl.BlockSpec((1, SG_WIDTH), lambda i: (0, 0))],
        out_specs=pl.BlockSpec((tt, SG_WIDTH), lambda i: (i, 0)),
        out_shape=jax.ShapeDtypeStruct((T, SG_WIDTH), F32),
        compiler_params=_params(),
    )(proj, proj, wm, bias, g)


def _sg_bwd(proj, dymix, wm, wmt, bias, g, hosted=None):
    T = proj.shape[0]
    tt = _tile(T, 512)
    nblk = T // tt

    def body(*refs):
        i = pl.program_id(0)
        (u_ref, v_ref, dy_ref, wm_ref, wmt_ref, b_ref, g_ref, du_ref, dv_ref, dw_ref, db_ref, dg_ref,
         dvn_ref, dbias_ref), start, wait = _host(hosted, refs, 7, 5, i == 0, i == nblk - 1)
        start()
        up, vp = u_ref[...], v_ref[...]
        gv = g_ref[...]
        (zu, gu), (zv, gvp) = _gelu_and_grad(up), _gelu_and_grad(vp)
        vn, xhat, r = _rms_fwd(zv, gv)
        grp = lax.broadcasted_iota(jnp.int32, (CHUNK, SG_WIDTH), 1) // SB_HD

        @pl.when(i == 0)
        def _():
            dw_ref[...] = jnp.zeros_like(dw_ref)
            dbias_ref[...] = jnp.zeros_like(dbias_ref)
            dg_ref[...] = jnp.zeros_like(dg_ref)

        for n in range(tt // CHUNK):
            rows = slice(n * CHUNK, (n + 1) * CHUNK)
            vc = vn[rows].astype(BF16)
            sv = _head_select(_nn(wm_ref[...], vc), grp) + b_ref[...]
            dy = dy_ref[rows, :]
            du_ref[rows, :] = (dy * sv * gu[rows]).astype(BF16)
            dsv = dy * zu[rows]
            dsvb = dsv.astype(BF16)
            dvn_ref[rows, :] = _head_select(_nn(wmt_ref[...], dsvb), grp)
            stacked = jnp.concatenate([jnp.where(grp == h, dsv, 0.0) for h in range(SG_HEADS)], axis=0)
            dw_ref[...] += _nt(stacked.astype(BF16), vc)
            dbias_ref[...] += dsv

        dzv, dgrow = _rms_bwd(dvn_ref[...], xhat, r, gv)
        dg_ref[...] += jnp.sum(dgrow, axis=0, keepdims=True)
        dv_ref[...] = (dzv * gvp).astype(BF16)

        @pl.when(i == nblk - 1)
        def _():
            t_i = lax.broadcasted_iota(jnp.int32, (SG_HEADS * CHUNK, CHUNK), 0) % CHUNK
            s_i = lax.broadcasted_iota(jnp.int32, (SG_HEADS * CHUNK, CHUNK), 1)
            dw_ref[...] = jnp.where(s_i <= t_i, dw_ref[...], 0.0)
            lane = lax.broadcasted_iota(jnp.int32, (CHUNK, LANES), 1)
            acc = jnp.zeros((CHUNK, LANES), F32)
            for h in range(SG_HEADS):
                tot = jnp.sum(jnp.where(grp == h, dbias_ref[...], 0.0), axis=1, keepdims=True)
                acc = acc + jnp.where(lane == h, tot, 0.0)
            db_ref[...] = acc

        wait()

    h_ins = hosted.ins if hosted else []
    res = pl.pallas_call(
        body, name="sg_bwd_hosting" if hosted else "sg_bwd", grid=(nblk,),
        in_specs=_sg_specs(tt) + [pl.BlockSpec((tt, SG_WIDTH), lambda i: (i, 1)),
                                  pl.BlockSpec((SG_HEADS * CHUNK, CHUNK), lambda i: (0, 0)),
                                  pl.BlockSpec((SG_HEADS * CHUNK, CHUNK), lambda i: (0, 0)),
                                  pl.BlockSpec((CHUNK, SG_WIDTH), lambda i: (0, 0)),
                                  pl.BlockSpec((1, SG_WIDTH), lambda i: (0, 0))] + [HBM_SPEC] * len(h_ins),
        out_specs=[pl.BlockSpec((tt, SG_WIDTH), lambda i: (i, 0)), pl.BlockSpec((tt, SG_WIDTH), lambda i: (i, 0)),
                   pl.BlockSpec((SG_HEADS * CHUNK, CHUNK), lambda i: (0, 0)),
                   pl.BlockSpec((CHUNK, LANES), lambda i: (0, 0)), pl.BlockSpec((1, SG_WIDTH), lambda i: (0, 0))]
        + [HBM_SPEC] * len(h_ins),
        out_shape=[jax.ShapeDtypeStruct((T, SG_WIDTH), BF16), jax.ShapeDtypeStruct((T, SG_WIDTH), BF16),
                   jax.ShapeDtypeStruct((SG_HEADS * CHUNK, CHUNK), F32),
                   jax.ShapeDtypeStruct((CHUNK, LANES), F32), jax.ShapeDtypeStruct((1, SG_WIDTH), F32)]
        + (hosted.out_shapes if hosted else []),
        scratch_shapes=[pltpu.VMEM((tt, SG_WIDTH), F32), pltpu.VMEM((CHUNK, SG_WIDTH), F32)]
        + (hosted.sems() if hosted else []),
        compiler_params=_params(has_side_effects=hosted is not None),
    )(proj, proj, dymix, wm, wmt, bias, g, *h_ins)
    return res[:5], res[5:]


def _sb_logits(z):
    lb = jnp.minimum(z, 0.0) - jnp.log(1.0 + jnp.exp(-jnp.abs(z)))
    return lb, lb - z


ATTN_STRIP = 256
ATTN_SUBS = 4


def _by_strips(n_rows, fn):
    parts = None
    for r in range(0, n_rows, ATTN_STRIP):
        res = fn(slice(r, r + ATTN_STRIP))
        parts = [[v] for v in res] if parts is None else [p + [v] for p, v in zip(parts, res)]
    return [jnp.concatenate(p, axis=0) for p in parts]


def _attn_qkv_specs(tq, T):
    base = (IN_COLS - 3 * SB_WIDTH - QKV_OFF) // LANES
    nb = SB_WIDTH // LANES
    return [pl.BlockSpec((tq, LANES), lambda p, i: (i, base + p)),
            pl.BlockSpec((T, LANES), lambda p, i: (0, base + nb + p)),
            pl.BlockSpec((T, LANES), lambda p, i: (0, base + 2 * nb + p))]


class _Hosted:
    def __init__(self, ins, out_shapes, n_sems, copies, in_place=False):
        self.ins, self.out_shapes, self.n_sems, self.copies = ins, out_shapes, n_sems, copies
        self.in_place = in_place

    @property
    def n(self):
        return len(self.ins)

    def aliases(self, n_in, n_out):
        return {n_in + k: n_out + k for k in range(self.n)} if self.in_place else {}

    def sems(self):
        return [pltpu.SemaphoreType.DMA((self.n_sems,)), pltpu.SemaphoreType.DMA((self.n_sems,))]

    def start(self, src, dst, ssem, rsem):
        for send, _ in self.copies(src, dst, ssem, rsem):
            send.start()

    def wait(self, src, dst, ssem, rsem):
        for send, recv in self.copies(src, dst, ssem, rsem):
            recv.wait_recv()
            send.wait_send()


def _host(hosted, refs, n_in, n_out, first, last):
    if hosted is None:
        return refs, lambda: None, lambda: None
    n = hosted.n
    own_in, h_in = refs[:n_in], refs[n_in:n_in + n]
    own_out, h_out = refs[n_in + n:n_in + n + n_out], refs[n_in + n + n_out:n_in + 2 * n + n_out]
    rest = refs[n_in + 2 * n + n_out:]
    ssem, rsem = rest[-2:]

    def start():
        if first is True:
            hosted.start(h_in, h_out, ssem, rsem)
        else:
            pl.when(first)(lambda: hosted.start(h_in, h_out, ssem, rsem))

    def wait():
        if last is True:
            hosted.wait(h_in, h_out, ssem, rsem)
        else:
            pl.when(last)(lambda: hosted.wait(h_in, h_out, ssem, rsem))

    return own_in + own_out + rest[:-2], start, wait


def _attn_fwd(qkv, hosted=None):
    T = qkv.shape[0]
    tk = _tile(T, ATTN_TILE)
    n_sub = ATTN_SUBS if T % (ATTN_SUBS * tk) == 0 else 1
    tq = n_sub * tk
    n_p, nq = SB_WIDTH // LANES, T // tq

    def body(*refs):
        p, i = pl.program_id(0), pl.program_id(1)
        (q_ref, k_ref, v_ref, o_ref), start, wait = _host(
            hosted, refs, 3, 1, jnp.logical_and(p == 0, i == 0), jnp.logical_and(p == n_p - 1, i == nq - 1))
        start()
        lane = lax.broadcasted_iota(jnp.int32, (tk, LANES), 1)
        row = lax.broadcasted_iota(jnp.int32, (tk, tk), 0)
        col = lax.broadcasted_iota(jnp.int32, (tk, tk), 1)
        after = jnp.where(row > col, 1.0, 0.0).astype(BF16)
        valid = col < row
        qh = {}
        for sb in range(n_sub):
            q = q_ref[sb * tk:(sb + 1) * tk, :].astype(F32)
            for hh in range(2):
                qh[(sb, hh)] = jnp.where((lane // SB_HD) == hh, q * SB_SCALE, 0.0).astype(BF16)

        def tiles(todo, state):
            chains = [(n, hh) for n in range(len(todo)) for hh in range(2)]
            kv = []
            for _, j, _ in todo:
                ks = pl.ds(pl.multiple_of(j * tk, tk), tk)
                kv.append((k_ref[ks, :], v_ref[ks, :]))
            z = {(n, hh): _nt(qh[(todo[n][0], hh)], kv[n][0]) for n, hh in chains}
            lb, lmb, lm_sum = {}, {}, {}
            for n, hh in chains:
                def logits(rows, z=z[(n, hh)], mask=todo[n][2]):
                    lb, lm = _sb_logits(z[rows])
                    if mask is not None:
                        lm = jnp.where(mask[rows], lm, 0.0)
                    return lb, lm.astype(BF16), jnp.sum(lm, axis=1, keepdims=True)

                lb[(n, hh)], lmb[(n, hh)], lm_sum[(n, hh)] = _by_strips(tk, logits)
            x = {c: _nn(lmb[c], after) for c in chains}
            new = dict(state)
            for n, hh in chains:
                key = (todo[n][0], hh)
                carry, acc = new[key]

                def weights(rows, lb=lb[(n, hh)], x=x[(n, hh)], carry=carry, mask=todo[n][2]):
                    a = jnp.exp(lb[rows] + x[rows] + carry[rows])
                    if mask is not None:
                        a = jnp.where(mask[rows], a, 0.0)
                    return (a.astype(BF16),)

                (ab,) = _by_strips(tk, weights)
                new[key] = (carry + lm_sum[(n, hh)], acc + _nn(ab, kv[n][1]))
            return new

        def live(state, sb):
            return jnp.maximum(jnp.max(state[(sb, 0)][0]), jnp.max(state[(sb, 1)][0]))

        first = n_sub * i
        zero = (jnp.zeros((tk, 1), F32), jnp.zeros((tk, LANES), F32))
        todo = []
        for sb in range(n_sub):
            gate = jnp.broadcast_to(first > 0, (tk, tk)) if sb == 0 else None
            todo += [(sb, first + sb, valid), (sb, jnp.maximum(first + sb - 1, 0), gate)]
        state = tiles(todo, {(sb, hh): zero for sb in range(n_sub) for hh in range(2)})
        for sb in range(n_sub):
            def cond(st):
                return jnp.logical_and(st[0] >= 0, st[2] > UNDERFLOW)

            def step(st, sb=sb):
                mine = tiles([(sb, st[0], None)], st[1])
                return st[0] - 1, mine, live(mine, sb)

            mine = {k: v for k, v in state.items() if k[0] == sb}
            _, mine, _ = lax.while_loop(cond, step, (first + sb - 2, mine, live(mine, sb)))
            o_ref[sb * tk:(sb + 1) * tk, :] = jnp.where(lane < SB_HD, mine[(sb, 0)][1], mine[(sb, 1)][1])
        wait()

    h_ins = hosted.ins if hosted else []
    res = pl.pallas_call(
        body, name="attn_fwd_hosting" if hosted else "attn_fwd", grid=(n_p, nq),
        in_specs=_attn_qkv_specs(tq, T) + [HBM_SPEC] * len(h_ins),
        out_specs=[pl.BlockSpec((tq, LANES), lambda p, i: (i, p))] + [HBM_SPEC] * len(h_ins),
        out_shape=[jax.ShapeDtypeStruct((T, SB_WIDTH), F32)] + (hosted.out_shapes if hosted else []),
        input_output_aliases=hosted.aliases(3, 1) if hosted else {},
        scratch_shapes=hosted.sems() if hosted else [],
        compiler_params=_params(has_side_effects=hosted is not None),
    )(qkv, qkv, qkv, *h_ins)
    return res[0], res[1:]


def _attn_bwd(qkv, o, dymix, hosted=None):
    T = qkv.shape[0]
    tk = _tile(T, ATTN_TILE)
    n_sub = ATTN_SUBS if T % (ATTN_SUBS * tk) == 0 else 1
    tq = n_sub * tk
    n_p, nq = SB_WIDTH // LANES, T // tq
    yc_blk = (POOL_WIDTH + SG_WIDTH) // LANES

    def body(*refs):
        p, i = pl.program_id(0), pl.program_id(1)
        (q_ref, k_ref, v_ref, o_ref, do_ref, dq_ref, dk_ref, dv_ref), start, wait = _host(
            hosted, refs, 5, 3, jnp.logical_and(p == 0, i == 0), jnp.logical_and(p == n_p - 1, i == nq - 1))
        start()
        lane = lax.broadcasted_iota(jnp.int32, (tk, LANES), 1)
        row = lax.broadcasted_iota(jnp.int32, (tk, tk), 0)
        col = lax.broadcasted_iota(jnp.int32, (tk, tk), 1)
        after = jnp.where(row > col, 1.0, 0.0).astype(BF16)
        from_here = jnp.where(row >= col, 1.0, 0.0).astype(BF16)
        from_here2 = jnp.concatenate([from_here, from_here], axis=0)
        valid = col < row

        @pl.when(i == 0)
        def _():
            dk_ref[...] = jnp.zeros_like(dk_ref)
            dv_ref[...] = jnp.zeros_like(dv_ref)

        qh, dohb, delta = {}, {}, {}
        for sb in range(n_sub):
            rows = slice(sb * tk, (sb + 1) * tk)
            q, ov, dov = q_ref[rows, :].astype(F32), o_ref[rows, :], do_ref[rows, :]
            for hh in range(2):
                head = (lane // SB_HD) == hh
                qh[(sb, hh)] = jnp.where(head, q * SB_SCALE, 0.0).astype(BF16)
                dohb[(sb, hh)] = jnp.where(head, dov, 0.0).astype(BF16)
                delta[(sb, hh)] = jnp.sum(dohb[(sb, hh)].astype(F32) * ov, axis=1, keepdims=True)

        def tiles(todo, state):
            chains = [(n, hh) for n in range(len(todo)) for hh in range(2)]
            kv, where = [], []
            for _, j, _ in todo:
                ks = pl.ds(pl.multiple_of(j * tk, tk), tk)
                where.append(ks)
                kv.append((k_ref[ks, :], v_ref[ks, :]))
            z = {(n, hh): _nt(qh[(todo[n][0], hh)], kv[n][0]) for n, hh in chains}
            da = {(n, hh): _nt(dohb[(todo[n][0], hh)], kv[n][1]) for n, hh in chains}
            lb, lmb, lm_sum = {}, {}, {}
            for n, hh in chains:
                def logits(rows, z=z[(n, hh)], mask=todo[n][2]):
                    lb, lm = _sb_logits(z[rows])
                    if mask is not None:
                        lm = jnp.where(mask[rows], lm, 0.0)
                    return lb, lm.astype(BF16), jnp.sum(lm, axis=1, keepdims=True)

                lb[(n, hh)], lmb[(n, hh)], lm_sum[(n, hh)] = _by_strips(tk, logits)
            x = {c: _nn(lmb[c], after) for c in chains}
            c_a = {k: v[0] for k, v in state.items()}
            ab, g, g_split, g_sum = {}, {}, {}, {}
            for n, hh in chains:
                key = (todo[n][0], hh)

                def weights(rows, lb=lb[(n, hh)], x=x[(n, hh)], da=da[(n, hh)], c_a=c_a[key], mask=todo[n][2]):
                    a = jnp.exp(lb[rows] + x[rows] + c_a[rows])
                    if mask is not None:
                        a = jnp.where(mask[rows], a, 0.0)
                    ab = a.astype(BF16)
                    g = da[rows] * ab.astype(F32)
                    hi = g.astype(BF16)
                    lo = (g - hi.astype(F32)).astype(BF16)
                    return ab, g, jnp.concatenate([hi, lo], axis=1), jnp.sum(g, axis=1, keepdims=True)

                ab[(n, hh)], g[(n, hh)], g_split[(n, hh)], g_sum[(n, hh)] = _by_strips(tk, weights)
                c_a[key] = c_a[key] + lm_sum[(n, hh)]
            right = {c: _nn(g_split[c], from_here2) for c in chains}
            c_r = {k: v[1] for k, v in state.items()}
            dzb = {}
            for n, hh in chains:
                key = (todo[n][0], hh)

                def logit_grads(rows, lb=lb[(n, hh)], g=g[(n, hh)], right=right[(n, hh)], c_r=c_r[key],
                                delta=delta[key], mask=todo[n][2]):
                    sig = jnp.exp(lb[rows])
                    left = delta[rows] - (c_r[rows] + right[rows])
                    dz = g[rows] * (1.0 - sig) - left * sig
                    if mask is not None:
                        dz = jnp.where(mask[rows], dz, 0.0)
                    return (dz.astype(BF16),)

                (dzb[(n, hh)],) = _by_strips(tk, logit_grads)
                c_r[key] = c_r[key] + g_sum[(n, hh)]
            dqa = {k: v[2] for k, v in state.items()}
            for n in range(len(todo)):
                sb = todo[n][0]
                dk_ref[where[n], :] += _tn(dzb[(n, 0)], qh[(sb, 0)]) + _tn(dzb[(n, 1)], qh[(sb, 1)])
                dv_ref[where[n], :] += _tn(ab[(n, 0)], dohb[(sb, 0)]) + _tn(ab[(n, 1)], dohb[(sb, 1)])
                for hh in range(2):
                    dqa[(sb, hh)] = dqa[(sb, hh)] + _nn(dzb[(n, hh)], kv[n][0])
            return {k: (c_a[k], c_r[k], dqa[k]) for k in state}

        def live(state, sb):
            return jnp.maximum(jnp.max(state[(sb, 0)][0]), jnp.max(state[(sb, 1)][0]))

        first = n_sub * i
        zero = (jnp.zeros((tk, 1), F32), jnp.zeros((tk, 1), F32), jnp.zeros((tk, LANES), F32))
        todo = []
        for sb in range(n_sub):
            gate = jnp.broadcast_to(first > 0, (tk, tk)) if sb == 0 else None
            todo += [(sb, first + sb, valid), (sb, jnp.maximum(first + sb - 1, 0), gate)]
        state = tiles(todo, {(sb, hh): zero for sb in range(n_sub) for hh in range(2)})
        for sb in range(n_sub):
            def cond(st):
                return jnp.logical_and(st[0] >= 0, st[2] > UNDERFLOW)

            def step(st, sb=sb):
                mine = tiles([(sb, st[0], None)], st[1])
                return st[0] - 1, mine, live(mine, sb)

            mine = {k: v for k, v in state.items() if k[0] == sb}
            _, mine, _ = lax.while_loop(cond, step, (first + sb - 2, mine, live(mine, sb)))
            dq_ref[sb * tk:(sb + 1) * tk, :] = (
                jnp.where(lane < SB_HD, mine[(sb, 0)][2], mine[(sb, 1)][2]) * SB_SCALE).astype(BF16)
        wait()

    h_ins = hosted.ins if hosted else []
    res = pl.pallas_call(
        body, name="attn_bwd_hosting" if hosted else "attn_bwd", grid=(n_p, nq),
        in_specs=_attn_qkv_specs(tq, T) + [pl.BlockSpec((tq, LANES), lambda p, i: (i, p)),
                                           pl.BlockSpec((tq, LANES), lambda p, i: (i, yc_blk + p))]
        + [HBM_SPEC] * len(h_ins),
        out_specs=[pl.BlockSpec((tq, LANES), lambda p, i: (i, p)), pl.BlockSpec((T, LANES), lambda p, i: (0, p)),
                   pl.BlockSpec((T, LANES), lambda p, i: (0, p))] + [HBM_SPEC] * len(h_ins),
        out_shape=[jax.ShapeDtypeStruct((T, SB_WIDTH), BF16)] + [jax.ShapeDtypeStruct((T, SB_WIDTH), F32)] * 2
        + (hosted.out_shapes if hosted else []),
        scratch_shapes=hosted.sems() if hosted else [],
        compiler_params=_params(has_side_effects=hosted is not None),
    )(qkv, qkv, qkv, o, dymix, *h_ins)
    return res[:3], res[3:]


def _outproj_fwd(x, ya, yb, yc, w, hosted=None):
    T, D = x.shape
    tt = _tile(T, 1024)
    nt = T // tt

    def body(*refs):
        i = pl.program_id(0)
        (x_ref, ya_ref, yb_ref, yc_ref, w_ref, x1_ref, ymix_ref), start, wait = _host(
            hosted, refs, 5, 2, i == 0, i == nt - 1)
        start()
        ymix_ref[:, 0:POOL_WIDTH] = ya_ref[...].astype(BF16)
        ymix_ref[:, POOL_WIDTH:POOL_WIDTH + SG_WIDTH] = yb_ref[...].astype(BF16)
        ymix_ref[:, POOL_WIDTH + SG_WIDTH:] = yc_ref[...].astype(BF16)
        x1_ref[...] = x_ref[...] + _nn(ymix_ref[...], w_ref[...])
        wait()

    row = lambda width: pl.BlockSpec((tt, width), lambda i: (i, 0))
    h_ins = hosted.ins if hosted else []
    res = pl.pallas_call(
        body, name="outproj_fwd_hosting" if hosted else "outproj_fwd", grid=(nt,),
        in_specs=[row(D), row(POOL_WIDTH), row(SG_WIDTH), row(SB_WIDTH), pl.BlockSpec((D, D), lambda i: (0, 0))]
        + [HBM_SPEC] * len(h_ins),
        out_specs=[row(D), row(D)] + [HBM_SPEC] * len(h_ins),
        out_shape=[jax.ShapeDtypeStruct((T, D), F32), jax.ShapeDtypeStruct((T, D), BF16)]
        + (hosted.out_shapes if hosted else []),
        input_output_aliases=hosted.aliases(5, 2) if hosted else {},
        scratch_shapes=hosted.sems() if hosted else [],
        compiler_params=_params(has_side_effects=hosted is not None),
    )(x, ya, yb, yc, w, *h_ins)
    return res[:2], res[2:]


def _nt_matmul(a, w):
    T, N = a.shape
    K = w.shape[0]
    tt = _tile(T, 1024)

    def body(a_ref, w_ref, o_ref):
        o_ref[...] = _nt(a_ref[...].astype(BF16), w_ref[...])

    return pl.pallas_call(
        body, name="nt_matmul", grid=(T // tt,),
        in_specs=[pl.BlockSpec((tt, N), lambda i: (i, 0)), pl.BlockSpec((K, N), lambda i: (0, 0))],
        out_specs=pl.BlockSpec((tt, K), lambda i: (i, 0)),
        out_shape=jax.ShapeDtypeStruct((T, K), F32),
        compiler_params=_params(),
    )(a, w)


def _tn_matmul(a, b, name, n_split=1, hosted=None):
    T, K = a.shape
    N = b.shape[1]
    tk = _tile(K, 1024)
    tn = _tile(N // n_split, 1024)
    tt = _tile(T, 2048)
    nper = N // n_split // tn
    nk, nn, nt = K // tk, N // tn, T // tt

    def body(*refs):
        k, n, t = pl.program_id(0), pl.program_id(1), pl.program_id(2)
        (a_ref, b_ref, o_ref), start, wait = _host(
            hosted, refs, 2, 1, jnp.logical_and(jnp.logical_and(k == 0, n == 0), t == 0),
            jnp.logical_and(jnp.logical_and(k == nk - 1, n == nn - 1), t == nt - 1))
        start()

        @pl.when(t == 0)
        def _():
            o_ref[...] = jnp.zeros_like(o_ref)

        o_ref[...] += _tn(a_ref[...], b_ref[...].astype(BF16))
        wait()

    h_ins = hosted.ins if hosted else []
    res = pl.pallas_call(
        body, name=name + "_hosting" if hosted else name, grid=(nk, nn, nt),
        in_specs=[pl.BlockSpec((tt, tk), lambda k, n, t: (t, k)), pl.BlockSpec((tt, tn), lambda k, n, t: (t, n))]
        + [HBM_SPEC] * len(h_ins),
        out_specs=[pl.BlockSpec((None, tk, tn), lambda k, n, t: (n // nper, k, n % nper))] + [HBM_SPEC] * len(h_ins),
        out_shape=[jax.ShapeDtypeStruct((n_split, K, N // n_split), F32)] + (hosted.out_shapes if hosted else []),
        scratch_shapes=hosted.sems() if hosted else [],
        compiler_params=_params(has_side_effects=hosted is not None),
    )(a, b, *h_ins)
    return (res[0], res[1:]) if hosted else res[0]


def _mlp_fwd(x, g, w_up, w_down, hosted=None):
    T, D = x.shape
    n_blk, _, width = w_up.shape
    F = n_blk * width
    tt = _tile(T, 1024)
    fc = _tile(width, MLP_CHUNK)
    per = width // fc
    nc = F // fc
    nt = T // tt

    def body(*refs):
        i, c = pl.program_id(0), pl.program_id(1)
        (x_ref, g_ref, wu_ref, wd_ref, y_ref, h_ref, u_ref, a_ref), start, wait = _host(
            hosted, refs, 4, 4, jnp.logical_and(i == 0, c == 0), jnp.logical_and(i == nt - 1, c == nc - 1))
        start()

        @pl.when(c == 0)
        def _():
            xv = x_ref[...]
            h, _, _ = _rms_fwd(xv, g_ref[...])
            h_ref[...] = h.astype(BF16)
            y_ref[...] = xv

        r = jnp.maximum(_nn(h_ref[...], wu_ref[...]), 0.0)
        u_ref[...] = (2.0 * r).astype(BF16)
        a = jnp.square(r).astype(BF16)
        a_ref[...] = a
        y_ref[...] += _nn(a, wd_ref[...])
        wait()

    h_ins = hosted.ins if hosted else []
    res = pl.pallas_call(
        body, name="mlp_fwd_hosting" if hosted else "mlp_fwd", grid=(nt, nc),
        in_specs=[pl.BlockSpec((tt, D), lambda i, c: (i, 0)), pl.BlockSpec((1, D), lambda i, c: (0, 0)),
                  pl.BlockSpec((None, D, fc), lambda i, c: (c // per, 0, c % per)),
                  pl.BlockSpec((fc, D), lambda i, c: (c, 0))]
        + [HBM_SPEC] * len(h_ins),
        out_specs=[pl.BlockSpec((tt, D), lambda i, c: (i, 0)), pl.BlockSpec((tt, D), lambda i, c: (i, 0)),
                   pl.BlockSpec((tt, fc), lambda i, c: (i, c)), pl.BlockSpec((tt, fc), lambda i, c: (i, c))]
        + [HBM_SPEC] * len(h_ins),
        out_shape=[jax.ShapeDtypeStruct((T, D), F32), jax.ShapeDtypeStruct((T, D), BF16),
                   jax.ShapeDtypeStruct((T, F), BF16), jax.ShapeDtypeStruct((T, F), BF16)]
        + (hosted.out_shapes if hosted else []),
        scratch_shapes=hosted.sems() if hosted else [],
        compiler_params=_params(has_side_effects=hosted is not None),
    )(x, g, w_up, w_down, *h_ins)
    return res[:4], res[4:]


def _mlp_bwd(dy, x, g, u, w_up, w_down, hosted=None):
    T, D = x.shape
    n_blk, _, width = w_up.shape
    F = n_blk * width
    tt = _tile(T, 1024)
    fc = _tile(width, MLP_CHUNK)
    per = width // fc
    nc = F // fc
    nt = T // tt

    def body(*refs):
        i, c = pl.program_id(0), pl.program_id(1)
        (dy_ref, x_ref, g_ref, u_ref, wu_ref, wd_ref, dx_ref, du_ref, dg_ref, dyb_ref, dh_ref), start, wait = _host(
            hosted, refs, 6, 3, jnp.logical_and(i == 0, c == 0), jnp.logical_and(i == nt - 1, c == nc - 1))
        start()

        @pl.when(c == 0)
        def _():
            dyb_ref[...] = dy_ref[...].astype(BF16)
            dh_ref[...] = jnp.zeros_like(dh_ref)

        @pl.when(jnp.logical_and(i == 0, c == 0))
        def _():
            dg_ref[...] = jnp.zeros_like(dg_ref)

        da = _nt(dyb_ref[...], wd_ref[...])
        du = (da * u_ref[...].astype(F32)).astype(BF16)
        du_ref[...] = du
        dh_ref[...] += _nt(du, wu_ref[...])

        @pl.when(c == nc - 1)
        def _():
            gv = g_ref[...]
            _, xhat, r = _rms_fwd(x_ref[...], gv)
            dx, dgrow = _rms_bwd(dh_ref[...], xhat, r, gv)
            dx_ref[...] = dy_ref[...] + dx
            dg_ref[...] += jnp.sum(dgrow, axis=0, keepdims=True)

        wait()

    h_ins = hosted.ins if hosted else []
    res = pl.pallas_call(
        body, name="mlp_bwd_hosting" if hosted else "mlp_bwd", grid=(nt, nc),
        in_specs=[pl.BlockSpec((tt, D), lambda i, c: (i, 0)), pl.BlockSpec((tt, D), lambda i, c: (i, 0)),
                  pl.BlockSpec((1, D), lambda i, c: (0, 0)), pl.BlockSpec((tt, fc), lambda i, c: (i, c)),
                  pl.BlockSpec((None, D, fc), lambda i, c: (c // per, 0, c % per)),
                  pl.BlockSpec((fc, D), lambda i, c: (c, 0))]
        + [HBM_SPEC] * len(h_ins),
        out_specs=[pl.BlockSpec((tt, D), lambda i, c: (i, 0)), pl.BlockSpec((tt, fc), lambda i, c: (i, c)),
                   pl.BlockSpec((1, D), lambda i, c: (0, 0))] + [HBM_SPEC] * len(h_ins),
        out_shape=[jax.ShapeDtypeStruct((T, D), F32), jax.ShapeDtypeStruct((T, F), BF16),
                   jax.ShapeDtypeStruct((1, D), F32)] + (hosted.out_shapes if hosted else []),
        scratch_shapes=[pltpu.VMEM((tt, D), BF16), pltpu.VMEM((tt, D), F32)] + (hosted.sems() if hosted else []),
        compiler_params=_params(has_side_effects=hosted is not None),
    )(dy, x, g, u, w_up, w_down, *h_ins)
    return res[:3], res[3:]


def _loss_head(x, g, target):
    T, D = x.shape
    tt = _tile(T, 1024)

    def body(x_ref, g_ref, t_ref, loss_ref, dx_ref, dg_ref):
        gv = g_ref[...]
        y, xhat, r = _rms_fwd(x_ref[...], gv)
        err = y - t_ref[...]
        dx, dgrow = _rms_bwd(err * (1.0 / D), xhat, r, gv)
        dx_ref[...] = dx

        @pl.when(pl.program_id(0) == 0)
        def _():
            loss_ref[...] = jnp.zeros_like(loss_ref)
            dg_ref[...] = jnp.zeros_like(dg_ref)

        loss_ref[...] += 0.5 * jnp.sum(jnp.mean(err * err, axis=-1, keepdims=True), axis=0, keepdims=True)
        dg_ref[...] += jnp.sum(dgrow, axis=0, keepdims=True)

    return pl.pallas_call(
        body, name="loss_head", grid=(T // tt,),
        in_specs=[pl.BlockSpec((tt, D), lambda i: (i, 0)), pl.BlockSpec((1, D), lambda i: (0, 0)),
                  pl.BlockSpec((tt, D), lambda i: (i, 0))],
        out_specs=[pl.BlockSpec((1, LANES), lambda i: (0, 0)), pl.BlockSpec((tt, D), lambda i: (i, 0)),
                   pl.BlockSpec((1, D), lambda i: (0, 0))],
        out_shape=[jax.ShapeDtypeStruct((1, LANES), F32), jax.ShapeDtypeStruct((T, D), F32),
                   jax.ShapeDtypeStruct((1, D), F32)],
        compiler_params=_params(),
    )(x, g, target)


def _rows(shape, pref=512):
    last = shape[-1]
    rows = 1
    for s in shape[:-1]:
        rows *= s
    tr = rows
    if rows * last > 256 * 1024:
        for cand in (pref, 256, 128, 64, 32, 16, 8):
            if rows % cand == 0:
                tr = cand
                break
    return rows, last, tr


def _elementwise(fn, name, ins, n_out, out_dtype=F32):
    shape = ins[0].shape
    rows, last, tr = _rows(shape)
    flat = [a.reshape(rows, last) for a in ins]
    n_in = len(ins)

    def body(*refs):
        res = fn(*[r[...] for r in refs[:n_in]])
        if n_out == 1:
            res = (res,)
        for r, v in zip(refs[n_in:], res):
            r[...] = v.astype(r.dtype)

    spec = pl.BlockSpec((tr, last), lambda i: (i, 0))
    outs = pl.pallas_call(
        body, name=name, grid=(rows // tr,),
        in_specs=[spec] * n_in, out_specs=[spec] * n_out,
        out_shape=[jax.ShapeDtypeStruct((rows, last), out_dtype)] * n_out,
        compiler_params=_params(),
    )(*flat)
    return [o.reshape(shape) for o in outs]


def _add_pairs(gs, os, c_idx):
    n = len(gs)
    halves = [(g.shape[1] // 2, g.shape[2]) for g in gs]

    def body(c_ref, *refs):
        for a in range(n):
            refs[2 * n + a][...] = refs[2 * a][...] + refs[2 * a + 1][...]

    in_specs = []
    for h, C in halves:
        in_specs += [pl.BlockSpec((None, h, C), lambda q, c: (q, c[0], 0)), pl.BlockSpec((None, h, C), lambda q, c: (q, 0, 0))]
    return pl.pallas_call(
        body, name="add_pairs",
        grid_spec=pltpu.PrefetchScalarGridSpec(
            num_scalar_prefetch=1, grid=(N_CHIPS,), in_specs=in_specs,
            out_specs=[pl.BlockSpec((None, h, C), lambda q, c: (q, 0, 0)) for h, C in halves]),
        out_shape=[jax.ShapeDtypeStruct((N_CHIPS, h, C), F32) for h, C in halves],
        compiler_params=_params(),
    )(c_idx.astype(jnp.int32).reshape(1), *[x for pair in zip(gs, os) for x in pair])


def _add_chips(ps, rs, q_idx):
    n = len(ps)
    steps = 2
    blocks = [(p.shape[1] // steps, p.shape[2]) for p in ps]

    def body(q_ref, *refs):
        for a in range(n):
            p_ref, r0_ref, r1_ref, r2_ref = refs[4 * a:4 * a + 4]
            refs[4 * n + a][...] = (p_ref[...] + r0_ref[...]) + (r1_ref[...] + r2_ref[...])

    def arrived(tr, C, k):
        return pl.BlockSpec((None, tr, C), lambda i, q: (k, i, 0))

    in_specs, operands = [], []
    for (tr, C), p, r in zip(blocks, ps, rs):
        in_specs += [pl.BlockSpec((None, tr, C), lambda i, q: (q[0], i, 0)), arrived(tr, C, 0), arrived(tr, C, 1),
                     arrived(tr, C, 2)]
        operands += [p, r, r, r]
    return pl.pallas_call(
        body, name="add_chips",
        grid_spec=pltpu.PrefetchScalarGridSpec(
            num_scalar_prefetch=1, grid=(steps,), in_specs=in_specs,
            out_specs=[pl.BlockSpec((tr, C), lambda i, q: (i, 0)) for tr, C in blocks]),
        out_shape=[jax.ShapeDtypeStruct((p.shape[1], p.shape[2]), F32) for p in ps],
        compiler_params=_params(),
    )(q_idx.astype(jnp.int32).reshape(1), *operands)


def _adamw(w, g, m, v):
    m = ADAM_B1 * m + (1.0 - ADAM_B1) * g
    v = ADAM_B2 * v + (1.0 - ADAM_B2) * jnp.square(g)
    m_hat = m / (1.0 - ADAM_B1 ** ADAM_STEP)
    v_hat = v / (1.0 - ADAM_B2 ** ADAM_STEP)
    delta = -ADAM_LR * (m_hat / (jnp.sqrt(v_hat) + ADAM_EPS) + ADAM_WD * w)
    return delta, m, v


def _place():
    x, y, c = lax.axis_index("x"), lax.axis_index("y"), lax.axis_index("c")
    chips = [(1 - x, y), (x, 1 - y), (1 - x, 1 - y)]
    return x, y, c, chips


def _remote(src, dst, ssem, rsem, k, dev):
    return pltpu.make_async_remote_copy(src_ref=src, dst_ref=dst, send_sem=ssem.at[k], recv_sem=rsem.at[k],
                                        device_id=dev, device_id_type=MESH)


def _gather_weights(shards):
    n = len(shards)
    halves = [s.shape[1] // 2 for s in shards]

    def body(*refs):
        src, out = refs[:n], refs[n:2 * n]
        ssem, rsem = refs[2 * n:]
        x, y, c, chips = _place()
        me_q = 2 * x + y
        sib = (x, y, 1 - c)

        def half(a, q, cc):
            return out[a].at[q, :, pl.ds(cc * halves[a], halves[a]), :]

        first = []
        for a in range(n):
            mine = src[a].at[:, pl.ds(c * halves[a], halves[a]), :]
            for r, chip in enumerate(chips):
                first.append(_remote(mine, half(a, me_q, c), ssem, rsem, a * 3 + r, (*chip, c)))
        for cp in first:
            cp.start()
        passed = []
        for a in range(n):
            for r, chip in enumerate(chips):
                q = 2 * chip[0] + chip[1]
                k = a * 3 + r
                _remote(half(a, q, c), half(a, q, c), ssem, rsem, k, (*chip, c)).wait_recv()
                cp = _remote(half(a, q, c), half(a, q, c), ssem, rsem, 3 * n + k, sib)
                cp.start()
                passed.append(cp)
        for a in range(n):
            for r, chip in enumerate(chips):
                q = 2 * chip[0] + chip[1]
                _remote(half(a, q, 1 - c), half(a, q, 1 - c), ssem, rsem, 3 * n + a * 3 + r, sib).wait_recv()
        for cp in first + passed:
            cp.wait_send()

    return pl.pallas_call(
        body, name="gather_weights",
        in_specs=[HBM_SPEC] * n, out_specs=[HBM_SPEC] * n,
        out_shape=[jax.ShapeDtypeStruct((N_CHIPS,) + s.shape, s.dtype) for s in shards],
        scratch_shapes=[pltpu.SemaphoreType.DMA((6 * n,)), pltpu.SemaphoreType.DMA((6 * n,))],
        compiler_params=_params(has_side_effects=True),
    )(*shards)


def _gather_over_ici(shards):
    n = len(shards)
    halves = [s.shape[1] // 2 for s in shards]

    def copies(src, out, ssem, rsem):
        x, y, c, chips = _place()
        me_q = 2 * x + y
        res = []
        for a in range(n):
            rows = pl.ds(c * halves[a], halves[a])
            mine = src[a].at[:, rows, :]
            for r, chip in enumerate(chips):
                dev = (*chip, c)
                res.append((_remote(mine, out[a].at[me_q, :, rows, :], ssem, rsem, a * 3 + r, dev),
                            _remote(mine, out[a].at[2 * chip[0] + chip[1], :, rows, :], ssem, rsem, a * 3 + r, dev)))
        return res

    return _Hosted(list(shards), [jax.ShapeDtypeStruct((N_CHIPS,) + s.shape, s.dtype) for s in shards], 3 * n, copies)


def _pass_over_d2d(gathered):
    n = len(gathered)
    halves = [g.shape[2] // 2 for g in gathered]

    def copies(_, out, ssem, rsem):
        x, y, c, chips = _place()
        sib = (x, y, 1 - c)
        res = []
        for a in range(n):
            for r, chip in enumerate(chips):
                q = 2 * chip[0] + chip[1]
                mine = out[a].at[q, :, pl.ds(c * halves[a], halves[a]), :]
                theirs = out[a].at[q, :, pl.ds((1 - c) * halves[a], halves[a]), :]
                res.append((_remote(mine, mine, ssem, rsem, a * 3 + r, sib),
                            _remote(theirs, theirs, ssem, rsem, a * 3 + r, sib)))
        return res

    return _Hosted(list(gathered), [jax.ShapeDtypeStruct(g.shape, g.dtype) for g in gathered], 3 * n, copies,
                   in_place=True)


def _pass_to_sibling(gathered):
    n = len(gathered)
    halves = [g.shape[2] // 2 for g in gathered]

    def body(*refs):
        out = refs[n:2 * n]
        ssem, rsem = refs[2 * n:]
        x, y, c, chips = _place()
        sib = (x, y, 1 - c)

        def half(a, q, cc):
            return out[a].at[q, :, pl.ds(cc * halves[a], halves[a]), :]

        cps = []
        for a in range(n):
            for r, chip in enumerate(chips):
                q = 2 * chip[0] + chip[1]
                cps.append(_remote(half(a, q, c), half(a, q, c), ssem, rsem, a * 3 + r, sib))
        for cp in cps:
            cp.start()
        for a in range(n):
            for r, chip in enumerate(chips):
                q = 2 * chip[0] + chip[1]
                _remote(half(a, q, 1 - c), half(a, q, 1 - c), ssem, rsem, a * 3 + r, sib).wait_recv()
        for cp in cps:
            cp.wait_send()

    return pl.pallas_call(
        body, name="pass_to_sibling",
        in_specs=[HBM_SPEC] * n, out_specs=[HBM_SPEC] * n,
        out_shape=[jax.ShapeDtypeStruct(g.shape, g.dtype) for g in gathered],
        input_output_aliases={a: a for a in range(n)},
        scratch_shapes=[pltpu.SemaphoreType.DMA((3 * n,)), pltpu.SemaphoreType.DMA((3 * n,))],
        compiler_params=_params(has_side_effects=True),
    )(*gathered)


def _scatter_over_ici(parts):
    n = len(parts)

    def copies(src, out, ssem, rsem):
        x, y, c, chips = _place()
        res = []
        for a in range(n):
            for r, chip in enumerate(chips):
                cp = _remote(src[a].at[2 * chip[0] + chip[1]], out[a].at[r], ssem, rsem, a * 3 + r, (*chip, c))
                res.append((cp, cp))
        return res

    return _Hosted(list(parts), [jax.ShapeDtypeStruct((3,) + p.shape[1:], F32) for p in parts], 3 * n, copies)


def _swap_over_d2d(grads):
    n = len(grads)
    halves = [g.shape[1] // 2 for g in grads]

    def copies(src, out, ssem, rsem):
        x, y, c, _ = _place()
        res = []
        for a in range(n):
            cp = _remote(src[a].at[:, pl.ds((1 - c) * halves[a], halves[a]), :], out[a], ssem, rsem, a, (x, y, 1 - c))
            res.append((cp, cp))
        return res

    return _Hosted(list(grads), [jax.ShapeDtypeStruct((N_CHIPS, h, g.shape[2]), F32) for g, h in zip(grads, halves)],
                   n, copies)


def _swap_halves(grads):
    n = len(grads)
    halves = [g.shape[1] // 2 for g in grads]

    def body(*refs):
        src, out = refs[:n], refs[n:2 * n]
        ssem, rsem = refs[2 * n:]
        x, y, c, _ = _place()
        cps = [_remote(src[a].at[:, pl.ds((1 - c) * halves[a], halves[a]), :], out[a], ssem, rsem, a, (x, y, 1 - c))
               for a in range(n)]
        for cp in cps:
            cp.start()
        for cp in cps:
            cp.wait()

    return pl.pallas_call(
        body, name="swap_halves",
        in_specs=[HBM_SPEC] * n, out_specs=[HBM_SPEC] * n,
        out_shape=[jax.ShapeDtypeStruct((N_CHIPS, h, g.shape[2]), F32) for g, h in zip(grads, halves)],
        scratch_shapes=[pltpu.SemaphoreType.DMA((n,)), pltpu.SemaphoreType.DMA((n,))],
        compiler_params=_params(has_side_effects=True),
    )(*grads)


def _swap_reduced_over_d2d(reduced):
    n = len(reduced)

    def copies(src, out, ssem, rsem):
        x, y, c, _ = _place()
        res = []
        for a in range(n):
            cp = _remote(src[a], out[a], ssem, rsem, a, (x, y, 1 - c))
            res.append((cp, cp))
        return res

    return _Hosted(list(reduced), [jax.ShapeDtypeStruct(r.shape, F32) for r in reduced], n, copies)


def _swap_reduced(reduced):
    n = len(reduced)

    def body(*refs):
        src, out = refs[:n], refs[n:2 * n]
        ssem, rsem = refs[2 * n:]
        x, y, c, _ = _place()
        cps = [_remote(src[a], out[a], ssem, rsem, a, (x, y, 1 - c)) for a in range(n)]
        for cp in cps:
            cp.start()
        for cp in cps:
            cp.wait()

    return pl.pallas_call(
        body, name="swap_reduced",
        in_specs=[HBM_SPEC] * n, out_specs=[HBM_SPEC] * n,
        out_shape=[jax.ShapeDtypeStruct(r.shape, F32) for r in reduced],
        scratch_shapes=[pltpu.SemaphoreType.DMA((n,)), pltpu.SemaphoreType.DMA((n,))],
        compiler_params=_params(has_side_effects=True),
    )(*reduced)


def _allreduce_small(buf, hosted=None):
    R, L = buf.shape

    def body(*refs):
        (buf_ref, out_ref, pair_ref, chip_ref, ssem, rsem), start, wait = _host(hosted, refs, 1, 1, True, True)
        start()
        x, y, c, chips = _place()
        me_q = 2 * x + y
        pair_ref[c] = buf_ref[...]
        to_sib = _remote(buf_ref, pair_ref.at[c], ssem, rsem, 0, (x, y, 1 - c))
        to_sib.start()
        _remote(buf_ref, pair_ref.at[1 - c], ssem, rsem, 0, (x, y, 1 - c)).wait_recv()
        chip_ref[me_q] = pair_ref[0] + pair_ref[1]
        cps = [_remote(chip_ref.at[me_q], chip_ref.at[me_q], ssem, rsem, 1 + r, (*chip, c))
               for r, chip in enumerate(chips)]
        for cp in cps:
            cp.start()
        for r, chip in enumerate(chips):
            q = 2 * chip[0] + chip[1]
            _remote(chip_ref.at[q], chip_ref.at[q], ssem, rsem, 1 + r, (*chip, c)).wait_recv()
        out_ref[...] = (chip_ref[0] + chip_ref[1]) + (chip_ref[2] + chip_ref[3])
        to_sib.wait_send()
        for cp in cps:
            cp.wait_send()
        wait()

    h_ins = hosted.ins if hosted else []
    res = pl.pallas_call(
        body, name="allreduce_small",
        in_specs=[VMEM_SPEC] + [HBM_SPEC] * len(h_ins), out_specs=[VMEM_SPEC] + [HBM_SPEC] * len(h_ins),
        out_shape=[jax.ShapeDtypeStruct((R, L), F32)] + (hosted.out_shapes if hosted else []),
        scratch_shapes=[pltpu.VMEM((2, R, L), F32), pltpu.VMEM((N_CHIPS, R, L), F32),
                        pltpu.SemaphoreType.DMA((4,)), pltpu.SemaphoreType.DMA((4,))]
        + (hosted.sems() if hosted else []),
        compiler_params=_params(has_side_effects=True),
    )(buf, *h_ins)
    return res[0], res[1:]


def _pack(arrays):
    flat = jnp.concatenate([a.reshape(-1) for a in arrays])
    pad = (-flat.shape[0]) % (8 * LANES)
    return jnp.pad(flat, (0, pad)).reshape(-1, LANES)


def _unpack(buf, like):
    flat = buf.reshape(-1)
    out, off = [], 0
    for a in like:
        out.append(flat[off:off + a.size].reshape(a.shape))
        off += a.size
    return out


def _block_diag(pw):
    rows = []
    for gi in range(len(POOL_WINDOWS)):
        blocks = [pw[gi] if gj == gi else jnp.zeros_like(pw[gi]) for gj in range(len(POOL_WINDOWS))]
        rows.append(jnp.concatenate(blocks, axis=1))
    return jnp.concatenate(rows, axis=0)


def kernel(x, norm1, w_in, pool_w, pool_scale, sg_norm, sg_w, sg_b, w_out, norm2, w_up, w_down, final_norm, loss_target, m_norm1, m_w_in, m_pool_w, m_pool_scale, m_sg_norm, m_sg_w, m_sg_b, m_w_out, m_norm2, m_w_up, m_w_down, m_final_norm, v_norm1, v_w_in, v_pool_w, v_pool_scale, v_sg_norm, v_sg_w, v_sg_b, v_w_out, v_norm2, v_w_up, v_w_down, v_final_norm):
    depth = norm1.shape[0]
    T = x.shape[1]
    xs = x.reshape(T, D_MODEL)
    target = loss_target.reshape(T, D_MODEL)

    assert depth == 2
    c_idx = lax.axis_index("c")
    q_idx = 2 * lax.axis_index("x") + lax.axis_index("y")
    own = [w.astype(BF16) for w in (w_in, w_out, w_up, w_down)]
    gathered = {(0, 0): _gather_weights([own[0][:1]])[0]}

    def full(a, l, axis):
        blocks = lax.dynamic_update_slice(gathered[(a, l)], own[a][l][None, None], (q_idx, 0, 0, 0))[:, 0]
        if axis is None:
            return blocks
        if axis == 0:
            return blocks.reshape(-1, blocks.shape[-1])
        return jnp.concatenate([blocks[q] for q in range(N_CHIPS)], axis=axis)

    half_way = {}

    def gather_behind(call, keys, at_once):
        res, over_ici = call(_gather_over_ici([own[a][l:l + 1] for a, l in keys]))
        gathered.update(zip(keys[:at_once], _pass_to_sibling(over_ici[:at_once])))
        half_way.update(zip(keys[at_once:], over_ici[at_once:]))
        return res

    def pass_behind(call, keys):
        res, done = call(_pass_over_d2d([half_way.pop(k) for k in keys]))
        gathered.update(zip(keys, done))
        return res

    tril = jnp.tril(jnp.ones((CHUNK, CHUNK), F32))
    saved = []
    cur = xs
    wi, wo, wu, wd = {}, {}, {}, {}
    for l in range(depth):
        wbd = _block_diag(pool_w[l]).astype(BF16)
        wm = sg_w[l] * tril
        wm_s = wm.reshape(SG_HEADS * CHUNK, CHUNK).astype(BF16)
        wmt_s = jnp.swapaxes(wm, 1, 2).reshape(SG_HEADS * CHUNK, CHUNK).astype(BF16)
        bias = jnp.repeat(sg_b[l].T, SB_HD, axis=1)
        n1, n2 = norm1[l][None], norm2[l][None]
        psc, sgn = pool_scale[l][None], sg_norm[l][None]
        wi[l] = full(0, l, 1)
        proj, h, qkv = _inproj_fwd(cur, n1, wi[l])
        ya = _pool_fwd(proj, wbd, psc)
        yb = _sg_fwd(proj, wm_s, bias, sgn)
        if l == 0:
            yc = gather_behind(lambda hosted: _attn_fwd(qkv, hosted), [(1, 0), (2, 0), (3, 0)], 1)
            wo[l] = full(1, l, 0)
            x1, ymix = pass_behind(lambda hosted: _outproj_fwd(cur, ya, yb, yc, wo[l], hosted), [(2, 0), (3, 0)])
        else:
            yc = pass_behind(lambda hosted: _attn_fwd(qkv, hosted), [(1, l), (2, l), (3, l)])
            wo[l] = full(1, l, 0)
            (x1, ymix), _ = _outproj_fwd(cur, ya, yb, yc, wo[l])
        wu[l], wd[l] = full(2, l, None), full(3, l, 0)
        if l == 0:
            x2, h2, u, act = gather_behind(lambda hosted: _mlp_fwd(x1, n2, wu[l], wd[l], hosted),
                                           [(0, 1), (1, 1), (2, 1), (3, 1)], 1)
        else:
            (x2, h2, u, act), _ = _mlp_fwd(x1, n2, wu[l], wd[l])
        saved.append(dict(x0=cur, x1=x1, proj=proj, h=h, qkv=qkv, yc=yc, ymix=ymix, h2=h2, u=u, act=act,
                          wbd=wbd, wm_s=wm_s, wmt_s=wmt_s, bias=bias, n1=n1, n2=n2, psc=psc, sgn=sgn))
        cur = x2

    loss_row, dcur, d_final = _loss_head(cur, final_norm[None], target)

    small = [None] * depth
    grads, parts, reduced = {}, {}, {}

    def pair_up(keys, swapped):
        parts.update(zip(keys, _add_pairs([grads[k] for k in keys], swapped, c_idx)))

    def chip_up(keys, arrived):
        reduced.update(zip(keys, _add_chips([parts[k] for k in keys], arrived, q_idx)))

    for l in reversed(range(depth)):
        s = saved[l]
        if l == 0:
            keys = [(2, 1), (3, 1)]
            (dx1, du, d_n2), arrived = _mlp_bwd(dcur, s["x1"], s["n2"], s["u"], wu[l], wd[l],
                                                _scatter_over_ici([parts[k] for k in keys]))
            chip_up(keys, arrived)
        else:
            (dx1, du, d_n2), _ = _mlp_bwd(dcur, s["x1"], s["n2"], s["u"], wu[l], wd[l])
        if l == 0:
            keys = [(0, 1)]
            grads[(2, l)], arrived = _tn_matmul(s["h2"], du, "grad_w_up", n_split=N_CHIPS,
                                                hosted=_scatter_over_ici([parts[k] for k in keys]))
            chip_up(keys, arrived)
            keys = [(1, 1)]
            g_down, arrived = _tn_matmul(s["act"], dcur, "grad_w_down",
                                         hosted=_scatter_over_ici([parts[k] for k in keys]))
            chip_up(keys, arrived)
        else:
            grads[(2, l)] = _tn_matmul(s["h2"], du, "grad_w_up", n_split=N_CHIPS)
            g_down = _tn_matmul(s["act"], dcur, "grad_w_down")
        grads[(3, l)] = g_down[0].reshape(N_CHIPS, D_FF // N_CHIPS, D_MODEL)
        dymix = _nt_matmul(dx1, wo[l])
        grads[(1, l)] = _tn_matmul(s["ymix"], dx1, "grad_w_out")[0].reshape(N_CHIPS, D_MODEL // N_CHIPS, D_MODEL)
        da_in, d_wbd, d_psc = _pool_bwd(s["proj"], dymix, s["wbd"], s["psc"])
        if l == 0:
            keys = [(1, 0), (2, 0), (3, 0)]
            (du_pre, dv_pre, d_wm, d_bias, d_sgn), swapped = _sg_bwd(
                s["proj"], dymix, s["wm_s"], s["wmt_s"], s["bias"], s["sgn"], _swap_over_d2d([grads[k] for k in keys]))
            pair_up(keys, swapped)
            (dq, dk, dv), arrived = _attn_bwd(s["qkv"], s["yc"], dymix, _scatter_over_ici([parts[k] for k in keys]))
            chip_up(keys, arrived)
        else:
            (du_pre, dv_pre, d_wm, d_bias, d_sgn), _ = _sg_bwd(s["proj"], dymix, s["wm_s"], s["wmt_s"], s["bias"], s["sgn"])
            keys = [(1, l), (2, l), (3, l)]
            (dq, dk, dv), swapped = _attn_bwd(s["qkv"], s["yc"], dymix, _swap_over_d2d([grads[k] for k in keys]))
            pair_up(keys, swapped)
        pieces = [da_in, du_pre, dv_pre, dq, dk, dv]
        if l == 0:
            keys = sorted(reduced)
            g_in_l, swapped = _inproj_grad(s["h"], pieces, _swap_reduced_over_d2d([reduced[k] for k in keys]))
            theirs = dict(zip(keys, swapped))
        else:
            g_in_l, _ = _inproj_grad(s["h"], pieces)
        grads[(0, l)] = g_in_l[0].reshape(D_MODEL, N_CHIPS, IN_COLS // N_CHIPS).transpose(1, 0, 2)
        if l == 0:
            keys = [(0, 0)]
            pair_up(keys, _swap_halves([grads[k] for k in keys]))
            (dx0, d_n1), arrived = _inproj_bwd(pieces, wi[l], s["x0"], s["n1"], dx1,
                                               _scatter_over_ici([parts[k] for k in keys]))
            chip_up(keys, arrived)
        else:
            keys = [(0, l)]
            (dx0, d_n1), swapped = _inproj_bwd(pieces, wi[l], s["x0"], s["n1"], dx1,
                                               _swap_over_d2d([grads[k] for k in keys]))
            pair_up(keys, swapped)
        d_pw = jnp.stack([d_wbd[gi * POOL_GW:(gi + 1) * POOL_GW, gi * POOL_GW:(gi + 1) * POOL_GW]
                          for gi in range(len(POOL_WINDOWS))])
        small[l] = dict(norm1=d_n1[0], pool_w=d_pw, pool_scale=d_psc[0], sg_norm=d_sgn[0],
                        sg_w=d_wm.reshape(SG_HEADS, CHUNK, CHUNK), sg_b=d_bias[:, :SG_HEADS].T, norm2=d_n2[0])
        dcur = dx0
    grad_x = dcur.reshape(x.shape)

    names = ["norm1", "pool_w", "pool_scale", "sg_norm", "sg_w", "sg_b", "norm2"]
    slot = jnp.zeros((1,), F32)
    small_w = [norm1, pool_w, pool_scale, sg_norm, sg_w, sg_b, norm2, final_norm, slot]
    small_m = [m_norm1, m_pool_w, m_pool_scale, m_sg_norm, m_sg_w, m_sg_b, m_norm2, m_final_norm, slot]
    small_v = [v_norm1, v_pool_w, v_pool_scale, v_sg_norm, v_sg_w, v_sg_b, v_norm2, v_final_norm, slot]
    small_g = [jnp.stack([small[l][k] for l in range(depth)]) for k in names] + [d_final[0], loss_row[0, :1]]
    keys = [(0, 0)]
    g_packed, _ = _allreduce_small(_pack(small_g))
    theirs.update(zip(keys, _swap_reduced([reduced[k] for k in keys])))

    def joined(a):
        layers = []
        for l in range(depth):
            mine, other = reduced[(a, l)], theirs[(a, l)]
            layers.append(jnp.where(c_idx == 0, jnp.concatenate([mine, other]), jnp.concatenate([other, mine])))
        return jnp.stack(layers)

    gw_in, gw_out, gw_up, gw_down = [joined(a) for a in range(4)]

    loss = _unpack(g_packed, small_w)[-1][0]
    s_delta, s_m, s_v = _elementwise(_adamw, "adamw_small", [_pack(small_w), g_packed, _pack(small_m), _pack(small_v)], 3)
    gs = dict(zip(names + ["final_norm"], _unpack(g_packed, small_w)))
    ds = dict(zip(names + ["final_norm"], _unpack(s_delta, small_w)))
    ms = dict(zip(names + ["final_norm"], _unpack(s_m, small_w)))
    vs = dict(zip(names + ["final_norm"], _unpack(s_v, small_w)))

    big_g = dict(w_in=gw_in, w_out=gw_out, w_up=gw_up, w_down=gw_down)
    big_w = dict(w_in=(w_in, m_w_in, v_w_in), w_out=(w_out, m_w_out, v_w_out),
                 w_up=(w_up, m_w_up, v_w_up), w_down=(w_down, m_w_down, v_w_down))
    for k, (w, m, v) in big_w.items():
        operands = [w, big_g[k], m, v]
        if k == "w_in":
            operands = [jnp.swapaxes(o, 1, 2) for o in operands]
        ds[k], ms[k], vs[k] = _elementwise(_adamw, "adamw_" + k, operands, 3)
        if k == "w_in":
            ds[k], ms[k], vs[k] = [jnp.swapaxes(o, 1, 2) for o in (ds[k], ms[k], vs[k])]
        gs[k] = big_g[k]

    order = ["norm1", "w_in", "pool_w", "pool_scale", "sg_norm", "sg_w", "sg_b", "w_out", "norm2", "w_up", "w_down",
             "final_norm"]
    return (loss, grad_x, *[gs[k] for k in order], *[ds[k] for k in order], *[ms[k] for k in order],
            *[vs[k] for k in order])
```

```python
import jax
import jax.numpy as jnp
from jax import lax
from jax.experimental import pallas as pl
from jax.experimental.pallas import tpu as pltpu

F32 = jnp.float32
BF16 = jnp.bfloat16
MESH = pl.DeviceIdType.MESH

EPS = 1e-6
D_MODEL = 1024
POOL_WIDTH = 256
SG_WIDTH = 256
SB_WIDTH = 512
POOL_WINDOWS = (2, 4, 8, 16)
POOL_GW = 64
POOL_HALO = 16
CHUNK = 128
SG_HEADS = 4
SB_HD = 64
SB_SCALE = 0.125
IN_COLS = 2304
QKV_OFF = 768
D_FF = 4096
N_CHIPS = 4
LANES = 128
VMEM_LIMIT = 56 * 1024 * 1024
MLP_CHUNK = 512
ATTN_TILE = 256
UNDERFLOW = -104.0

ADAM_LR = 0.001
ADAM_B1 = 0.9
ADAM_B2 = 0.999
ADAM_EPS = 1e-08
ADAM_WD = 0.01
ADAM_STEP = 10

HBM_SPEC = pl.BlockSpec(memory_space=pl.ANY)
VMEM_SPEC = pl.BlockSpec(memory_space=pltpu.VMEM)


def _params(**kw):
    return pltpu.CompilerParams(vmem_limit_bytes=VMEM_LIMIT, **kw)


def _tile(n, pref):
    if n <= pref:
        return n
    for t in range(pref - pref % LANES, 0, -LANES):
        if n % t == 0:
            return t
    raise ValueError((n, pref))


def _nn(a, b):
    return jnp.dot(a, b, preferred_element_type=F32)


def _nt(a, b):
    return lax.dot_general(a, b, (((1,), (1,)), ((), ())), preferred_element_type=F32)


def _tn(a, b):
    return lax.dot_general(a, b, (((0,), (0,)), ((), ())), preferred_element_type=F32)


def _rms_fwd(x, g):
    r = lax.rsqrt(jnp.mean(x * x, axis=-1, keepdims=True) + EPS)
    xhat = x * r
    return xhat * g, xhat, r


def _rms_bwd(dy, xhat, r, g):
    dxhat = dy * g
    dx = r * (dxhat - xhat * jnp.mean(dxhat * xhat, axis=-1, keepdims=True))
    return dx, dy * xhat


_GELU_K = 0.7978845608028654
_GELU_C = 0.044715


def _gelu(x):
    return 0.5 * x * (1.0 + jnp.tanh(_GELU_K * (x + _GELU_C * x * x * x)))


def _gelu_and_grad(x):
    x2 = x * x
    t = jnp.tanh(_GELU_K * (x + _GELU_C * x2 * x))
    half = 0.5 * (1.0 + t)
    return x * half, half + 0.5 * x * (1.0 - t * t) * _GELU_K * (1.0 + 3.0 * _GELU_C * x2)


def _inproj_fwd(x, g, w):
    T, D = x.shape
    N = w.shape[1]
    tt = _tile(T, 1024)

    def body(x_ref, g_ref, w_ref, proj_ref, h_ref, qkv_ref):
        h, _, _ = _rms_fwd(x_ref[...], g_ref[...])
        hb = h.astype(BF16)
        h_ref[...] = hb
        p = _nn(hb, w_ref[...])
        proj_ref[...] = p[:, :QKV_OFF]
        qkv_ref[...] = p[:, QKV_OFF:].astype(BF16)

    return pl.pallas_call(
        body, name="inproj_fwd", grid=(T // tt,),
        in_specs=[pl.BlockSpec((tt, D), lambda i: (i, 0)), pl.BlockSpec((1, D), lambda i: (0, 0)),
                  pl.BlockSpec((D, N), lambda i: (0, 0))],
        out_specs=[pl.BlockSpec((tt, QKV_OFF), lambda i: (i, 0)), pl.BlockSpec((tt, D), lambda i: (i, 0)),
                   pl.BlockSpec((tt, N - QKV_OFF), lambda i: (i, 0))],
        out_shape=[jax.ShapeDtypeStruct((T, QKV_OFF), F32), jax.ShapeDtypeStruct((T, D), BF16),
                   jax.ShapeDtypeStruct((T, N - QKV_OFF), BF16)],
        compiler_params=_params(),
    )(x, g, w)


def _inproj_bwd(pieces, w, x, g, dres, hosted=None):
    T, D = x.shape
    N = w.shape[1]
    tt = _tile(T, 512)
    nt = T // tt
    widths = [p.shape[1] for p in pieces]
    offs = [sum(widths[:k]) for k in range(len(widths))]
    assert sum(widths) == N
    n_p = len(pieces)

    def body(*refs):
        i = pl.program_id(0)
        own, start, wait = _host(hosted, refs, n_p + 4, 2, i == 0, i == nt - 1)
        start()
        p_refs = own[:n_p]
        w_ref, x_ref, g_ref, dres_ref, dx_ref, dg_ref, dproj_ref = own[n_p:]
        for p_ref, o, wd in zip(p_refs, offs, widths):
            dproj_ref[:, o:o + wd] = p_ref[...].astype(BF16)
        dh = _nt(dproj_ref[...], w_ref[...])
        gv = g_ref[...]
        _, xhat, r = _rms_fwd(x_ref[...], gv)
        dx, dgrow = _rms_bwd(dh, xhat, r, gv)
        dx_ref[...] = dres_ref[...] + dx

        @pl.when(i == 0)
        def _():
            dg_ref[...] = jnp.zeros_like(dg_ref)

        dg_ref[...] += jnp.sum(dgrow, axis=0, keepdims=True)
        wait()

    h_ins = hosted.ins if hosted else []
    res = pl.pallas_call(
        body, name="inproj_bwd_hosting" if hosted else "inproj_bwd", grid=(nt,),
        in_specs=[pl.BlockSpec((tt, wd), lambda i: (i, 0)) for wd in widths] + [
            pl.BlockSpec((D, N), lambda i: (0, 0)), pl.BlockSpec((tt, D), lambda i: (i, 0)),
            pl.BlockSpec((1, D), lambda i: (0, 0)), pl.BlockSpec((tt, D), lambda i: (i, 0))] + [HBM_SPEC] * len(h_ins),
        out_specs=[pl.BlockSpec((tt, D), lambda i: (i, 0)), pl.BlockSpec((1, D), lambda i: (0, 0))]
        + [HBM_SPEC] * len(h_ins),
        out_shape=[jax.ShapeDtypeStruct((T, D), F32), jax.ShapeDtypeStruct((1, D), F32)]
        + (hosted.out_shapes if hosted else []),
        scratch_shapes=[pltpu.VMEM((tt, N), BF16)] + (hosted.sems() if hosted else []),
        compiler_params=_params(has_side_effects=hosted is not None),
    )(*pieces, w, x, g, dres, *h_ins)
    return res[:2], res[2:]


def _inproj_grad(h, pieces, hosted=None):
    T, D = h.shape
    tt = _tile(T, 1024)
    nt = T // tt
    widths = [p.shape[1] for p in pieces]
    offs = [sum(widths[:k]) for k in range(len(widths))]
    N = sum(widths)
    n_p = len(pieces)

    def body(*refs):
        t = pl.program_id(0)
        own, start, wait = _host(hosted, refs, n_p + 1, 1, t == 0, t == nt - 1)
        start()
        h_ref, p_refs, o_ref = own[0], own[1:1 + n_p], own[1 + n_p]

        @pl.when(t == 0)
        def _():
            o_ref[...] = jnp.zeros_like(o_ref)

        hv = h_ref[...]
        for p_ref, o, wd in zip(p_refs, offs, widths):
            o_ref[:, o:o + wd] += _tn(hv, p_ref[...].astype(BF16))
        wait()

    h_ins = hosted.ins if hosted else []
    res = pl.pallas_call(
        body, name="grad_w_in_hosting" if hosted else "grad_w_in", grid=(nt,),
        in_specs=[pl.BlockSpec((tt, D), lambda t: (t, 0))] + [pl.BlockSpec((tt, wd), lambda t: (t, 0)) for wd in widths]
        + [HBM_SPEC] * len(h_ins),
        out_specs=[pl.BlockSpec((None, D, N), lambda t: (0, 0, 0))] + [HBM_SPEC] * len(h_ins),
        out_shape=[jax.ShapeDtypeStruct((1, D, N), F32)] + (hosted.out_shapes if hosted else []),
        scratch_shapes=hosted.sems() if hosted else [],
        compiler_params=_params(has_side_effects=hosted is not None),
    )(h, *pieces, *h_ins)
    return res[0], res[1:]


def _pool_select(s2, s4, s8, s16, grp):
    return jnp.where(grp == 0, s2, jnp.where(grp == 1, s4, jnp.where(grp == 2, s8, s16)))


def _pool_count(t_glob, grp):
    win = jnp.where(grp == 0, 2, jnp.where(grp == 1, 4, jnp.where(grp == 2, 8, 16)))
    return jnp.minimum(t_glob + 1, win).astype(F32)


def _pool_diff(a, halo, base, tt):
    n = tt + POOL_HALO
    ext = jnp.concatenate([halo, a], axis=0)
    s2 = ext + pltpu.roll(ext, 1, 0)
    s4 = s2 + pltpu.roll(s2, 2, 0)
    s8 = s4 + pltpu.roll(s4, 4, 0)
    s16 = s8 + pltpu.roll(s8, 8, 0)
    grp = lax.broadcasted_iota(jnp.int32, (n, POOL_WIDTH), 1) // POOL_GW
    t_glob = lax.broadcasted_iota(jnp.int32, (n, POOL_WIDTH), 0) + (base - POOL_HALO)
    pooled = _pool_select(s2, s4, s8, s16, grp) / _pool_count(t_glob, grp)
    return pooled[POOL_HALO:] - a


def _pool_specs(T, tt):
    hb = tt // POOL_HALO
    return [pl.BlockSpec((tt, POOL_WIDTH), lambda i: (i, 0)),
            pl.BlockSpec((POOL_HALO, POOL_WIDTH), lambda i: (jnp.maximum(i * hb - 1, 0), 0))]


def _pool_fwd(proj, wbd, scale):
    T = proj.shape[0]
    tt = _tile(T, 1024)

    def body(a_ref, halo_ref, w_ref, sc_ref, y_ref):
        i = pl.program_id(0)
        halo = jnp.where(i > 0, halo_ref[...], 0.0)
        d = _pool_diff(a_ref[...], halo, i * tt, tt)
        y_ref[...] = _nn(d.astype(BF16), w_ref[...]) * sc_ref[...]

    return pl.pallas_call(
        body, name="pool_fwd", grid=(T // tt,),
        in_specs=_pool_specs(T, tt) + [pl.BlockSpec((POOL_WIDTH, POOL_WIDTH), lambda i: (0, 0)),
                                       pl.BlockSpec((1, POOL_WIDTH), lambda i: (0, 0))],
        out_specs=pl.BlockSpec((tt, POOL_WIDTH), lambda i: (i, 0)),
        out_shape=jax.ShapeDtypeStruct((T, POOL_WIDTH), F32),
        compiler_params=_params(),
    )(proj, proj, wbd, scale)


def _pool_bwd(proj, dymix, wbd, scale):
    T = proj.shape[0]
    tt = _tile(T, 1024)
    hb = tt // POOL_HALO
    nblk = T // tt
    n = tt + POOL_HALO

    def body(a_ref, halo_ref, dy_ref, dyn_ref, w_ref, sc_ref, da_ref, dw_ref, dsc_ref):
        i = pl.program_id(0)
        halo = jnp.where(i > 0, halo_ref[...], 0.0)
        d = _pool_diff(a_ref[...], halo, i * tt, tt)
        db = d.astype(BF16)
        wv = w_ref[...]
        sc = sc_ref[...]
        dy = dy_ref[...]
        dys = dy * sc

        @pl.when(i == 0)
        def _():
            dw_ref[...] = jnp.zeros_like(dw_ref)
            dsc_ref[...] = jnp.zeros_like(dsc_ref)

        dsc_ref[...] += jnp.sum(dy * _nn(db, wv), axis=0, keepdims=True)
        dw_ref[...] += _tn(db, dys.astype(BF16))
        dyn = jnp.where(i < nblk - 1, dyn_ref[...], 0.0) * sc
        dd = _nt(jnp.concatenate([dys, dyn], axis=0).astype(BF16), wv)
        grp = lax.broadcasted_iota(jnp.int32, (n, POOL_WIDTH), 1) // POOL_GW
        t_glob = lax.broadcasted_iota(jnp.int32, (n, POOL_WIDTH), 0) + i * tt
        e = dd / _pool_count(t_glob, grp)
        r2 = e + pltpu.roll(e, n - 1, 0)
        r4 = r2 + pltpu.roll(r2, n - 2, 0)
        r8 = r4 + pltpu.roll(r4, n - 4, 0)
        r16 = r8 + pltpu.roll(r8, n - 8, 0)
        da_ref[...] = (_pool_select(r2, r4, r8, r16, grp) - dd)[:tt].astype(BF16)

    return pl.pallas_call(
        body, name="pool_bwd", grid=(nblk,),
        in_specs=_pool_specs(T, tt) + [
            pl.BlockSpec((tt, POOL_WIDTH), lambda i: (i, 0)),
            pl.BlockSpec((POOL_HALO, POOL_WIDTH), lambda i: (jnp.minimum((i + 1) * hb, T // POOL_HALO - 1), 0)),
            pl.BlockSpec((POOL_WIDTH, POOL_WIDTH), lambda i: (0, 0)), pl.BlockSpec((1, POOL_WIDTH), lambda i: (0, 0))],
        out_specs=[pl.BlockSpec((tt, POOL_WIDTH), lambda i: (i, 0)),
                   pl.BlockSpec((POOL_WIDTH, POOL_WIDTH), lambda i: (0, 0)),
                   pl.BlockSpec((1, POOL_WIDTH), lambda i: (0, 0))],
        out_shape=[jax.ShapeDtypeStruct((T, POOL_WIDTH), BF16),
                   jax.ShapeDtypeStruct((POOL_WIDTH, POOL_WIDTH), F32),
                   jax.ShapeDtypeStruct((1, POOL_WIDTH), F32)],
        compiler_params=_params(),
    )(proj, proj, dymix, dymix, wbd, scale)


def _head_select(stacked, grp):
    out = jnp.where(grp == 0, stacked[0:CHUNK], 0.0)
    for h in range(1, SG_HEADS):
        out = out + jnp.where(grp == h, stacked[h * CHUNK:(h + 1) * CHUNK], 0.0)
    return out


def _sg_specs(tt):
    return [pl.BlockSpec((tt, SG_WIDTH), lambda i: (i, 1)), pl.BlockSpec((tt, SG_WIDTH), lambda i: (i, 2))]


def _sg_fwd(proj, wm, bias, g):
    T = proj.shape[0]
    tt = _tile(T, 1024)

    def body(u_ref, v_ref, wm_ref, b_ref, g_ref, y_ref):
        zu = _gelu(u_ref[...])
        vn, _, _ = _rms_fwd(_gelu(v_ref[...]), g_ref[...])
        grp = lax.broadcasted_iota(jnp.int32, (CHUNK, SG_WIDTH), 1) // SB_HD
        for n in range(tt // CHUNK):
            rows = slice(n * CHUNK, (n + 1) * CHUNK)
            sv = _head_select(_nn(wm_ref[...], vn[rows].astype(BF16)), grp) + b_ref[...]
            y_ref[rows, :] = zu[rows] * sv

    return pl.pallas_call(
        body, name="sg_fwd", grid=(T // tt,),
        in_specs=_sg_specs(tt) + [pl.BlockSpec((SG_HEADS * CHUNK, CHUNK), lambda i: (0, 0)),
                                  pl.BlockSpec((CHUNK, SG_WIDTH), lambda i: (0, 0)),
                                  pl.BlockSpec((1, SG_WIDTH), lambda i: (0, 0))],
        out_specs=pl.BlockSpec((tt, SG_WIDTH), lambda i: (i, 0)),
        out_shape=jax.ShapeDtypeStruct((T, SG_WIDTH), F32),
        compiler_params=_params(),
    )(proj, proj, wm, bias, g)


def _sg_bwd(proj, dymix, wm, wmt, bias, g, hosted=None):
    T = proj.shape[0]
    tt = _tile(T, 1024)
    nblk = T // tt

    def body(*refs):
        i = pl.program_id(0)
        (u_ref, v_ref, dy_ref, wm_ref, wmt_ref, b_ref, g_ref, du_ref, dv_ref, dw_ref, db_ref, dg_ref,
         dvn_ref, dbias_ref), start, wait = _host(hosted, refs, 7, 5, i == 0, i == nblk - 1)
        start()
        up, vp = u_ref[...], v_ref[...]
        gv = g_ref[...]
        (zu, gu), (zv, gvp) = _gelu_and_grad(up), _gelu_and_grad(vp)
        vn, xhat, r = _rms_fwd(zv, gv)
        grp = lax.broadcasted_iota(jnp.int32, (CHUNK, SG_WIDTH), 1) // SB_HD

        @pl.when(i == 0)
        def _():
            dw_ref[...] = jnp.zeros_like(dw_ref)
            dbias_ref[...] = jnp.zeros_like(dbias_ref)
            dg_ref[...] = jnp.zeros_like(dg_ref)

        for n in range(tt // CHUNK):
            rows = slice(n * CHUNK, (n + 1) * CHUNK)
            vc = vn[rows].astype(BF16)
            sv = _head_select(_nn(wm_ref[...], vc), grp) + b_ref[...]
            dy = dy_ref[rows, :]
            du_ref[rows, :] = (dy * sv * gu[rows]).astype(BF16)
            dsv = dy * zu[rows]
            dsvb = dsv.astype(BF16)
            dvn_ref[rows, :] = _head_select(_nn(wmt_ref[...], dsvb), grp)
            stacked = jnp.concatenate([jnp.where(grp == h, dsv, 0.0) for h in range(SG_HEADS)], axis=0)
            dw_ref[...] += _nt(stacked.astype(BF16), vc)
            dbias_ref[...] += dsv

        dzv, dgrow = _rms_bwd(dvn_ref[...], xhat, r, gv)
        dg_ref[...] += jnp.sum(dgrow, axis=0, keepdims=True)
        dv_ref[...] = (dzv * gvp).astype(BF16)

        @pl.when(i == nblk - 1)
        def _():
            t_i = lax.broadcasted_iota(jnp.int32, (SG_HEADS * CHUNK, CHUNK), 0) % CHUNK
            s_i = lax.broadcasted_iota(jnp.int32, (SG_HEADS * CHUNK, CHUNK), 1)
            dw_ref[...] = jnp.where(s_i <= t_i, dw_ref[...], 0.0)
            lane = lax.broadcasted_iota(jnp.int32, (CHUNK, LANES), 1)
            acc = jnp.zeros((CHUNK, LANES), F32)
            for h in range(SG_HEADS):
                tot = jnp.sum(jnp.where(grp == h, dbias_ref[...], 0.0), axis=1, keepdims=True)
                acc = acc + jnp.where(lane == h, tot, 0.0)
            db_ref[...] = acc

        wait()

    h_ins = hosted.ins if hosted else []
    res = pl.pallas_call(
        body, name="sg_bwd_hosting" if hosted else "sg_bwd", grid=(nblk,),
        in_specs=_sg_specs(tt) + [pl.BlockSpec((tt, SG_WIDTH), lambda i: (i, 1)),
                                  pl.BlockSpec((SG_HEADS * CHUNK, CHUNK), lambda i: (0, 0)),
                                  pl.BlockSpec((SG_HEADS * CHUNK, CHUNK), lambda i: (0, 0)),
                                  pl.BlockSpec((CHUNK, SG_WIDTH), lambda i: (0, 0)),
                                  pl.BlockSpec((1, SG_WIDTH), lambda i: (0, 0))] + [HBM_SPEC] * len(h_ins),
        out_specs=[pl.BlockSpec((tt, SG_WIDTH), lambda i: (i, 0)), pl.BlockSpec((tt, SG_WIDTH), lambda i: (i, 0)),
                   pl.BlockSpec((SG_HEADS * CHUNK, CHUNK), lambda i: (0, 0)),
                   pl.BlockSpec((CHUNK, LANES), lambda i: (0, 0)), pl.BlockSpec((1, SG_WIDTH), lambda i: (0, 0))]
        + [HBM_SPEC] * len(h_ins),
        out_shape=[jax.ShapeDtypeStruct((T, SG_WIDTH), BF16), jax.ShapeDtypeStruct((T, SG_WIDTH), BF16),
                   jax.ShapeDtypeStruct((SG_HEADS * CHUNK, CHUNK), F32),
                   jax.ShapeDtypeStruct((CHUNK, LANES), F32), jax.ShapeDtypeStruct((1, SG_WIDTH), F32)]
        + (hosted.out_shapes if hosted else []),
        scratch_shapes=[pltpu.VMEM((tt, SG_WIDTH), F32), pltpu.VMEM((CHUNK, SG_WIDTH), F32)]
        + (hosted.sems() if hosted else []),
        compiler_params=_params(has_side_effects=hosted is not None),
    )(proj, proj, dymix, wm, wmt, bias, g, *h_ins)
    return res[:5], res[5:]


def _sb_logits(z):
    lb = jnp.minimum(z, 0.0) - jnp.log(1.0 + jnp.exp(-jnp.abs(z)))
    return lb, lb - z


ATTN_STRIP = 256
ATTN_SUBS = 4


def _by_strips(n_rows, fn):
    parts = None
    for r in range(0, n_rows, ATTN_STRIP):
        res = fn(slice(r, r + ATTN_STRIP))
        parts = [[v] for v in res] if parts is None else [p + [v] for p, v in zip(parts, res)]
    return [jnp.concatenate(p, axis=0) for p in parts]


def _attn_qkv_specs(tq, T):
    base = (IN_COLS - 3 * SB_WIDTH - QKV_OFF) // LANES
    nb = SB_WIDTH // LANES
    return [pl.BlockSpec((tq, LANES), lambda p, i: (i, base + p)),
            pl.BlockSpec((T, LANES), lambda p, i: (0, base + nb + p)),
            pl.BlockSpec((T, LANES), lambda p, i: (0, base + 2 * nb + p))]


class _Hosted:
    def __init__(self, ins, out_shapes, n_sems, copies, in_place=False):
        self.ins, self.out_shapes, self.n_sems, self.copies = ins, out_shapes, n_sems, copies
        self.in_place = in_place

    @property
    def n(self):
        return len(self.ins)

    def aliases(self, n_in, n_out):
        return {n_in + k: n_out + k for k in range(self.n)} if self.in_place else {}

    def sems(self):
        return [pltpu.SemaphoreType.DMA((self.n_sems,)), pltpu.SemaphoreType.DMA((self.n_sems,))]

    def start(self, src, dst, ssem, rsem):
        for send, _ in self.copies(src, dst, ssem, rsem):
            send.start()

    def wait(self, src, dst, ssem, rsem):
        for send, recv in self.copies(src, dst, ssem, rsem):
            recv.wait_recv()
            send.wait_send()


def _host(hosted, refs, n_in, n_out, first, last):
    if hosted is None:
        return refs, lambda: None, lambda: None
    n = hosted.n
    own_in, h_in = refs[:n_in], refs[n_in:n_in + n]
    own_out, h_out = refs[n_in + n:n_in + n + n_out], refs[n_in + n + n_out:n_in + 2 * n + n_out]
    rest = refs[n_in + 2 * n + n_out:]
    ssem, rsem = rest[-2:]

    def start():
        if first is True:
            hosted.start(h_in, h_out, ssem, rsem)
        else:
            pl.when(first)(lambda: hosted.start(h_in, h_out, ssem, rsem))

    def wait():
        if last is True:
            hosted.wait(h_in, h_out, ssem, rsem)
        else:
            pl.when(last)(lambda: hosted.wait(h_in, h_out, ssem, rsem))

    return own_in + own_out + rest[:-2], start, wait


def _attn_fwd(qkv, hosted=None):
    T = qkv.shape[0]
    tk = _tile(T, ATTN_TILE)
    n_sub = ATTN_SUBS if T % (ATTN_SUBS * tk) == 0 else 1
    tq = n_sub * tk
    n_p, nq = SB_WIDTH // LANES, T // tq

    def body(*refs):
        p, i = pl.program_id(0), pl.program_id(1)
        (q_ref, k_ref, v_ref, o_ref), start, wait = _host(
            hosted, refs, 3, 1, jnp.logical_and(p == 0, i == 0), jnp.logical_and(p == n_p - 1, i == nq - 1))
        start()
        lane = lax.broadcasted_iota(jnp.int32, (tk, LANES), 1)
        row = lax.broadcasted_iota(jnp.int32, (tk, tk), 0)
        col = lax.broadcasted_iota(jnp.int32, (tk, tk), 1)
        after = jnp.where(row > col, 1.0, 0.0).astype(BF16)
        valid = col < row
        qh = {}
        for sb in range(n_sub):
            q = q_ref[sb * tk:(sb + 1) * tk, :].astype(F32)
            for hh in range(2):
                qh[(sb, hh)] = jnp.where((lane // SB_HD) == hh, q * SB_SCALE, 0.0).astype(BF16)

        def tiles(todo, state):
            chains = [(n, hh) for n in range(len(todo)) for hh in range(2)]
            kv = []
            for _, j, _ in todo:
                ks = pl.ds(pl.multiple_of(j * tk, tk), tk)
                kv.append((k_ref[ks, :], v_ref[ks, :]))
            z = {(n, hh): _nt(qh[(todo[n][0], hh)], kv[n][0]) for n, hh in chains}
            lb, lmb, lm_sum = {}, {}, {}
            for n, hh in chains:
                def logits(rows, z=z[(n, hh)], mask=todo[n][2]):
                    lb, lm = _sb_logits(z[rows])
                    if mask is not None:
                        lm = jnp.where(mask[rows], lm, 0.0)
                    return lb, lm.astype(BF16), jnp.sum(lm, axis=1, keepdims=True)

                lb[(n, hh)], lmb[(n, hh)], lm_sum[(n, hh)] = _by_strips(tk, logits)
            x = {c: _nn(lmb[c], after) for c in chains}
            new = dict(state)
            for n, hh in chains:
                key = (todo[n][0], hh)
                carry, acc = new[key]

                def weights(rows, lb=lb[(n, hh)], x=x[(n, hh)], carry=carry, mask=todo[n][2]):
                    a = jnp.exp(lb[rows] + x[rows] + carry[rows])
                    if mask is not None:
                        a = jnp.where(mask[rows], a, 0.0)
                    return (a.astype(BF16),)

                (ab,) = _by_strips(tk, weights)
                new[key] = (carry + lm_sum[(n, hh)], acc + _nn(ab, kv[n][1]))
            return new

        def live(state, sb):
            return jnp.maximum(jnp.max(state[(sb, 0)][0]), jnp.max(state[(sb, 1)][0]))

        first = n_sub * i
        zero = (jnp.zeros((tk, 1), F32), jnp.zeros((tk, LANES), F32))
        todo = []
        for sb in range(n_sub):
            gate = jnp.broadcast_to(first > 0, (tk, tk)) if sb == 0 else None
            todo += [(sb, first + sb, valid), (sb, jnp.maximum(first + sb - 1, 0), gate)]
        state = tiles(todo, {(sb, hh): zero for sb in range(n_sub) for hh in range(2)})
        for sb in range(n_sub):
            def cond(st):
                return jnp.logical_and(st[0] >= 0, st[2] > UNDERFLOW)

            def step(st, sb=sb):
                mine = tiles([(sb, st[0], None)], st[1])
                return st[0] - 1, mine, live(mine, sb)

            mine = {k: v for k, v in state.items() if k[0] == sb}
            _, mine, _ = lax.while_loop(cond, step, (first + sb - 2, mine, live(mine, sb)))
            o_ref[sb * tk:(sb + 1) * tk, :] = jnp.where(lane < SB_HD, mine[(sb, 0)][1], mine[(sb, 1)][1])
        wait()

    h_ins = hosted.ins if hosted else []
    res = pl.pallas_call(
        body, name="attn_fwd_hosting" if hosted else "attn_fwd", grid=(n_p, nq),
        in_specs=_attn_qkv_specs(tq, T) + [HBM_SPEC] * len(h_ins),
        out_specs=[pl.BlockSpec((tq, LANES), lambda p, i: (i, p))] + [HBM_SPEC] * len(h_ins),
        out_shape=[jax.ShapeDtypeStruct((T, SB_WIDTH), F32)] + (hosted.out_shapes if hosted else []),
        input_output_aliases=hosted.aliases(3, 1) if hosted else {},
        scratch_shapes=hosted.sems() if hosted else [],
        compiler_params=_params(has_side_effects=hosted is not None),
    )(qkv, qkv, qkv, *h_ins)
    return res[0], res[1:]


def _attn_bwd(qkv, o, dymix, hosted=None):
    T = qkv.shape[0]
    tk = _tile(T, ATTN_TILE)
    n_sub = ATTN_SUBS if T % (ATTN_SUBS * tk) == 0 else 1
    tq = n_sub * tk
    n_p, nq = SB_WIDTH // LANES, T // tq
    yc_blk = (POOL_WIDTH + SG_WIDTH) // LANES

    def body(*refs):
        p, i = pl.program_id(0), pl.program_id(1)
        (q_ref, k_ref, v_ref, o_ref, do_ref, dq_ref, dk_ref, dv_ref), start, wait = _host(
            hosted, refs, 5, 3, jnp.logical_and(p == 0, i == 0), jnp.logical_and(p == n_p - 1, i == nq - 1))
        start()
        lane = lax.broadcasted_iota(jnp.int32, (tk, LANES), 1)
        row = lax.broadcasted_iota(jnp.int32, (tk, tk), 0)
        col = lax.broadcasted_iota(jnp.int32, (tk, tk), 1)
        after = jnp.where(row > col, 1.0, 0.0).astype(BF16)
        from_here = jnp.where(row >= col, 1.0, 0.0).astype(BF16)
        from_here2 = jnp.concatenate([from_here, from_here], axis=0)
        valid = col < row

        @pl.when(i == 0)
        def _():
            dk_ref[...] = jnp.zeros_like(dk_ref)
            dv_ref[...] = jnp.zeros_like(dv_ref)

        qh, dohb, delta = {}, {}, {}
        for sb in range(n_sub):
            rows = slice(sb * tk, (sb + 1) * tk)
            q, ov, dov = q_ref[rows, :].astype(F32), o_ref[rows, :], do_ref[rows, :]
            for hh in range(2):
                head = (lane // SB_HD) == hh
                qh[(sb, hh)] = jnp.where(head, q * SB_SCALE, 0.0).astype(BF16)
                dohb[(sb, hh)] = jnp.where(head, dov, 0.0).astype(BF16)
                delta[(sb, hh)] = jnp.sum(dohb[(sb, hh)].astype(F32) * ov, axis=1, keepdims=True)

        def tiles(todo, state):
            chains = [(n, hh) for n in range(len(todo)) for hh in range(2)]
            kv, where = [], []
            for _, j, _ in todo:
                ks = pl.ds(pl.multiple_of(j * tk, tk), tk)
                where.append(ks)
                kv.append((k_ref[ks, :], v_ref[ks, :]))
            z = {(n, hh): _nt(qh[(todo[n][0], hh)], kv[n][0]) for n, hh in chains}
            da = {(n, hh): _nt(dohb[(todo[n][0], hh)], kv[n][1]) for n, hh in chains}
            lb, lmb, lm_sum = {}, {}, {}
            for n, hh in chains:
                def logits(rows, z=z[(n, hh)], mask=todo[n][2]):
                    lb, lm = _sb_logits(z[rows])
                    if mask is not None:
                        lm = jnp.where(mask[rows], lm, 0.0)
                    return lb, lm.astype(BF16), jnp.sum(lm, axis=1, keepdims=True)

                lb[(n, hh)], lmb[(n, hh)], lm_sum[(n, hh)] = _by_strips(tk, logits)
            x = {c: _nn(lmb[c], after) for c in chains}
            c_a = {k: v[0] for k, v in state.items()}
            ab, g, g_split, g_sum = {}, {}, {}, {}
            for n, hh in chains:
                key = (todo[n][0], hh)

                def weights(rows, lb=lb[(n, hh)], x=x[(n, hh)], da=da[(n, hh)], c_a=c_a[key], mask=todo[n][2]):
                    a = jnp.exp(lb[rows] + x[rows] + c_a[rows])
                    if mask is not None:
                        a = jnp.where(mask[rows], a, 0.0)
                    ab = a.astype(BF16)
                    g = da[rows] * ab.astype(F32)
                    hi = g.astype(BF16)
                    lo = (g - hi.astype(F32)).astype(BF16)
                    return ab, g, jnp.concatenate([hi, lo], axis=1), jnp.sum(g, axis=1, keepdims=True)

                ab[(n, hh)], g[(n, hh)], g_split[(n, hh)], g_sum[(n, hh)] = _by_strips(tk, weights)
                c_a[key] = c_a[key] + lm_sum[(n, hh)]
            right = {c: _nn(g_split[c], from_here2) for c in chains}
            c_r = {k: v[1] for k, v in state.items()}
            dzb = {}
            for n, hh in chains:
                key = (todo[n][0], hh)

                def logit_grads(rows, lb=lb[(n, hh)], g=g[(n, hh)], right=right[(n, hh)], c_r=c_r[key],
                                delta=delta[key], mask=todo[n][2]):
                    sig = jnp.exp(lb[rows])
                    left = delta[rows] - (c_r[rows] + right[rows])
                    dz = g[rows] * (1.0 - sig) - left * sig
                    if mask is not None:
                        dz = jnp.where(mask[rows], dz, 0.0)
                    return (dz.astype(BF16),)

                (dzb[(n, hh)],) = _by_strips(tk, logit_grads)
                c_r[key] = c_r[key] + g_sum[(n, hh)]
            dqa = {k: v[2] for k, v in state.items()}
            for n in range(len(todo)):
                sb = todo[n][0]
                dk_ref[where[n], :] += _tn(dzb[(n, 0)], qh[(sb, 0)]) + _tn(dzb[(n, 1)], qh[(sb, 1)])
                dv_ref[where[n], :] += _tn(ab[(n, 0)], dohb[(sb, 0)]) + _tn(ab[(n, 1)], dohb[(sb, 1)])
                for hh in range(2):
                    dqa[(sb, hh)] = dqa[(sb, hh)] + _nn(dzb[(n, hh)], kv[n][0])
            return {k: (c_a[k], c_r[k], dqa[k]) for k in state}

        def live(state, sb):
            return jnp.maximum(jnp.max(state[(sb, 0)][0]), jnp.max(state[(sb, 1)][0]))

        first = n_sub * i
        zero = (jnp.zeros((tk, 1), F32), jnp.zeros((tk, 1), F32), jnp.zeros((tk, LANES), F32))
        todo = []
        for sb in range(n_sub):
            gate = jnp.broadcast_to(first > 0, (tk, tk)) if sb == 0 else None
            todo += [(sb, first + sb, valid), (sb, jnp.maximum(first + sb - 1, 0), gate)]
        state = tiles(todo, {(sb, hh): zero for sb in range(n_sub) for hh in range(2)})
        for sb in range(n_sub):
            def cond(st):
                return jnp.logical_and(st[0] >= 0, st[2] > UNDERFLOW)

            def step(st, sb=sb):
                mine = tiles([(sb, st[0], None)], st[1])
                return st[0] - 1, mine, live(mine, sb)

            mine = {k: v for k, v in state.items() if k[0] == sb}
            _, mine, _ = lax.while_loop(cond, step, (first + sb - 2, mine, live(mine, sb)))
            dq_ref[sb * tk:(sb + 1) * tk, :] = (
                jnp.where(lane < SB_HD, mine[(sb, 0)][2], mine[(sb, 1)][2]) * SB_SCALE).astype(BF16)
        wait()

    h_ins = hosted.ins if hosted else []
    res = pl.pallas_call(
        body, name="attn_bwd_hosting" if hosted else "attn_bwd", grid=(n_p, nq),
        in_specs=_attn_qkv_specs(tq, T) + [pl.BlockSpec((tq, LANES), lambda p, i: (i, p)),
                                           pl.BlockSpec((tq, LANES), lambda p, i: (i, yc_blk + p))]
        + [HBM_SPEC] * len(h_ins),
        out_specs=[pl.BlockSpec((tq, LANES), lambda p, i: (i, p)), pl.BlockSpec((T, LANES), lambda p, i: (0, p)),
                   pl.BlockSpec((T, LANES), lambda p, i: (0, p))] + [HBM_SPEC] * len(h_ins),
        out_shape=[jax.ShapeDtypeStruct((T, SB_WIDTH), BF16)] + [jax.ShapeDtypeStruct((T, SB_WIDTH), F32)] * 2
        + (hosted.out_shapes if hosted else []),
        scratch_shapes=hosted.sems() if hosted else [],
        compiler_params=_params(has_side_effects=hosted is not None),
    )(qkv, qkv, qkv, o, dymix, *h_ins)
    return res[:3], res[3:]


def _outproj_fwd(x, ya, yb, yc, w, hosted=None):
    T, D = x.shape
    tt = _tile(T, 1024)
    nt = T // tt

    def body(*refs):
        i = pl.program_id(0)
        (x_ref, ya_ref, yb_ref, yc_ref, w_ref, x1_ref, ymix_ref), start, wait = _host(
            hosted, refs, 5, 2, i == 0, i == nt - 1)
        start()
        ymix_ref[:, 0:POOL_WIDTH] = ya_ref[...].astype(BF16)
        ymix_ref[:, POOL_WIDTH:POOL_WIDTH + SG_WIDTH] = yb_ref[...].astype(BF16)
        ymix_ref[:, POOL_WIDTH + SG_WIDTH:] = yc_ref[...].astype(BF16)
        x1_ref[...] = x_ref[...] + _nn(ymix_ref[...], w_ref[...])
        wait()

    row = lambda width: pl.BlockSpec((tt, width), lambda i: (i, 0))
    h_ins = hosted.ins if hosted else []
    res = pl.pallas_call(
        body, name="outproj_fwd_hosting" if hosted else "outproj_fwd", grid=(nt,),
        in_specs=[row(D), row(POOL_WIDTH), row(SG_WIDTH), row(SB_WIDTH), pl.BlockSpec((D, D), lambda i: (0, 0))]
        + [HBM_SPEC] * len(h_ins),
        out_specs=[row(D), row(D)] + [HBM_SPEC] * len(h_ins),
        out_shape=[jax.ShapeDtypeStruct((T, D), F32), jax.ShapeDtypeStruct((T, D), BF16)]
        + (hosted.out_shapes if hosted else []),
        input_output_aliases=hosted.aliases(5, 2) if hosted else {},
        scratch_shapes=hosted.sems() if hosted else [],
        compiler_params=_params(has_side_effects=hosted is not None),
    )(x, ya, yb, yc, w, *h_ins)
    return res[:2], res[2:]


def _nt_matmul(a, w):
    T, N = a.shape
    K = w.shape[0]
    tt = _tile(T, 1024)

    def body(a_ref, w_ref, o_ref):
        o_ref[...] = _nt(a_ref[...].astype(BF16), w_ref[...])

    return pl.pallas_call(
        body, name="nt_matmul", grid=(T // tt,),
        in_specs=[pl.BlockSpec((tt, N), lambda i: (i, 0)), pl.BlockSpec((K, N), lambda i: (0, 0))],
        out_specs=pl.BlockSpec((tt, K), lambda i: (i, 0)),
        out_shape=jax.ShapeDtypeStruct((T, K), F32),
        compiler_params=_params(),
    )(a, w)


def _tn_matmul(a, b, name, n_split=1, hosted=None):
    T, K = a.shape
    N = b.shape[1]
    tk = _tile(K, 1024)
    tn = _tile(N // n_split, 1024)
    tt = _tile(T, 2048)
    nper = N // n_split // tn
    nk, nn, nt = K // tk, N // tn, T // tt

    def body(*refs):
        k, n, t = pl.program_id(0), pl.program_id(1), pl.program_id(2)
        (a_ref, b_ref, o_ref), start, wait = _host(
            hosted, refs, 2, 1, jnp.logical_and(jnp.logical_and(k == 0, n == 0), t == 0),
            jnp.logical_and(jnp.logical_and(k == nk - 1, n == nn - 1), t == nt - 1))
        start()

        @pl.when(t == 0)
        def _():
            o_ref[...] = jnp.zeros_like(o_ref)

        o_ref[...] += _tn(a_ref[...], b_ref[...].astype(BF16))
        wait()

    h_ins = hosted.ins if hosted else []
    res = pl.pallas_call(
        body, name=name + "_hosting" if hosted else name, grid=(nk, nn, nt),
        in_specs=[pl.BlockSpec((tt, tk), lambda k, n, t: (t, k)), pl.BlockSpec((tt, tn), lambda k, n, t: (t, n))]
        + [HBM_SPEC] * len(h_ins),
        out_specs=[pl.BlockSpec((None, tk, tn), lambda k, n, t: (n // nper, k, n % nper))] + [HBM_SPEC] * len(h_ins),
        out_shape=[jax.ShapeDtypeStruct((n_split, K, N // n_split), F32)] + (hosted.out_shapes if hosted else []),
        scratch_shapes=hosted.sems() if hosted else [],
        compiler_params=_params(has_side_effects=hosted is not None),
    )(a, b, *h_ins)
    return (res[0], res[1:]) if hosted else res[0]


def _mlp_fwd(x, g, w_up, w_down, hosted=None):
    T, D = x.shape
    n_blk, _, width = w_up.shape
    F = n_blk * width
    tt = _tile(T, 1024)
    fc = _tile(width, MLP_CHUNK)
    per = width // fc
    nc = F // fc
    nt = T // tt

    def body(*refs):
        i, c = pl.program_id(0), pl.program_id(1)
        (x_ref, g_ref, wu_ref, wd_ref, y_ref, h_ref, u_ref, a_ref), start, wait = _host(
            hosted, refs, 4, 4, jnp.logical_and(i == 0, c == 0), jnp.logical_and(i == nt - 1, c == nc - 1))
        start()

        @pl.when(c == 0)
        def _():
            xv = x_ref[...]
            h, _, _ = _rms_fwd(xv, g_ref[...])
            h_ref[...] = h.astype(BF16)
            y_ref[...] = xv

        r = jnp.maximum(_nn(h_ref[...], wu_ref[...]), 0.0)
        u_ref[...] = (2.0 * r).astype(BF16)
        a = jnp.square(r).astype(BF16)
        a_ref[...] = a
        y_ref[...] += _nn(a, wd_ref[...])
        wait()

    h_ins = hosted.ins if hosted else []
    res = pl.pallas_call(
        body, name="mlp_fwd_hosting" if hosted else "mlp_fwd", grid=(nt, nc),
        in_specs=[pl.BlockSpec((tt, D), lambda i, c: (i, 0)), pl.BlockSpec((1, D), lambda i, c: (0, 0)),
                  pl.BlockSpec((None, D, fc), lambda i, c: (c // per, 0, c % per)),
                  pl.BlockSpec((fc, D), lambda i, c: (c, 0))]
        + [HBM_SPEC] * len(h_ins),
        out_specs=[pl.BlockSpec((tt, D), lambda i, c: (i, 0)), pl.BlockSpec((tt, D), lambda i, c: (i, 0)),
                   pl.BlockSpec((tt, fc), lambda i, c: (i, c)), pl.BlockSpec((tt, fc), lambda i, c: (i, c))]
        + [HBM_SPEC] * len(h_ins),
        out_shape=[jax.ShapeDtypeStruct((T, D), F32), jax.ShapeDtypeStruct((T, D), BF16),
                   jax.ShapeDtypeStruct((T, F), BF16), jax.ShapeDtypeStruct((T, F), BF16)]
        + (hosted.out_shapes if hosted else []),
        scratch_shapes=hosted.sems() if hosted else [],
        compiler_params=_params(has_side_effects=hosted is not None),
    )(x, g, w_up, w_down, *h_ins)
    return res[:4], res[4:]


def _mlp_bwd(dy, x, g, u, w_up, w_down, hosted=None):
    T, D = x.shape
    n_blk, _, width = w_up.shape
    F = n_blk * width
    tt = _tile(T, 1024)
    fc = _tile(width, MLP_CHUNK)
    per = width // fc
    nc = F // fc
    nt = T // tt

    def body(*refs):
        i, c = pl.program_id(0), pl.program_id(1)
        (dy_ref, x_ref, g_ref, u_ref, wu_ref, wd_ref, dx_ref, du_ref, dg_ref, dyb_ref, dh_ref), start, wait = _host(
            hosted, refs, 6, 3, jnp.logical_and(i == 0, c == 0), jnp.logical_and(i == nt - 1, c == nc - 1))
        start()

        @pl.when(c == 0)
        def _():
            dyb_ref[...] = dy_ref[...].astype(BF16)
            dh_ref[...] = jnp.zeros_like(dh_ref)

        @pl.when(jnp.logical_and(i == 0, c == 0))
        def _():
            dg_ref[...] = jnp.zeros_like(dg_ref)

        da = _nt(dyb_ref[...], wd_ref[...])
        du = (da * u_ref[...].astype(F32)).astype(BF16)
        du_ref[...] = du
        dh_ref[...] += _nt(du, wu_ref[...])

        @pl.when(c == nc - 1)
        def _():
            gv = g_ref[...]
            _, xhat, r = _rms_fwd(x_ref[...], gv)
            dx, dgrow = _rms_bwd(dh_ref[...], xhat, r, gv)
            dx_ref[...] = dy_ref[...] + dx
            dg_ref[...] += jnp.sum(dgrow, axis=0, keepdims=True)

        wait()

    h_ins = hosted.ins if hosted else []
    res = pl.pallas_call(
        body, name="mlp_bwd_hosting" if hosted else "mlp_bwd", grid=(nt, nc),
        in_specs=[pl.BlockSpec((tt, D), lambda i, c: (i, 0)), pl.BlockSpec((tt, D), lambda i, c: (i, 0)),
                  pl.BlockSpec((1, D), lambda i, c: (0, 0)), pl.BlockSpec((tt, fc), lambda i, c: (i, c)),
                  pl.BlockSpec((None, D, fc), lambda i, c: (c // per, 0, c % per)),
                  pl.BlockSpec((fc, D), lambda i, c: (c, 0))]
        + [HBM_SPEC] * len(h_ins),
        out_specs=[pl.BlockSpec((tt, D), lambda i, c: (i, 0)), pl.BlockSpec((tt, fc), lambda i, c: (i, c)),
                   pl.BlockSpec((1, D), lambda i, c: (0, 0))] + [HBM_SPEC] * len(h_ins),
        out_shape=[jax.ShapeDtypeStruct((T, D), F32), jax.ShapeDtypeStruct((T, F), BF16),
                   jax.ShapeDtypeStruct((1, D), F32)] + (hosted.out_shapes if hosted else []),
        scratch_shapes=[pltpu.VMEM((tt, D), BF16), pltpu.VMEM((tt, D), F32)] + (hosted.sems() if hosted else []),
        compiler_params=_params(has_side_effects=hosted is not None),
    )(dy, x, g, u, w_up, w_down, *h_ins)
    return res[:3], res[3:]


def _loss_head(x, g, target):
    T, D = x.shape
    tt = _tile(T, 1024)

    def body(x_ref, g_ref, t_ref, loss_ref, dx_ref, dg_ref):
        gv = g_ref[...]
        y, xhat, r = _rms_fwd(x_ref[...], gv)
        err = y - t_ref[...]
        dx, dgrow = _rms_bwd(err * (1.0 / D), xhat, r, gv)
        dx_ref[...] = dx

        @pl.when(pl.program_id(0) == 0)
        def _():
            loss_ref[...] = jnp.zeros_like(loss_ref)
            dg_ref[...] = jnp.zeros_like(dg_ref)

        loss_ref[...] += 0.5 * jnp.sum(jnp.mean(err * err, axis=-1, keepdims=True), axis=0, keepdims=True)
        dg_ref[...] += jnp.sum(dgrow, axis=0, keepdims=True)

    return pl.pallas_call(
        body, name="loss_head", grid=(T // tt,),
        in_specs=[pl.BlockSpec((tt, D), lambda i: (i, 0)), pl.BlockSpec((1, D), lambda i: (0, 0)),
                  pl.BlockSpec((tt, D), lambda i: (i, 0))],
        out_specs=[pl.BlockSpec((1, LANES), lambda i: (0, 0)), pl.BlockSpec((tt, D), lambda i: (i, 0)),
                   pl.BlockSpec((1, D), lambda i: (0, 0))],
        out_shape=[jax.ShapeDtypeStruct((1, LANES), F32), jax.ShapeDtypeStruct((T, D), F32),
                   jax.ShapeDtypeStruct((1, D), F32)],
        compiler_params=_params(),
    )(x, g, target)


def _rows(shape, pref=512):
    last = shape[-1]
    rows = 1
    for s in shape[:-1]:
        rows *= s
    tr = rows
    if rows * last > 256 * 1024:
        for cand in (pref, 256, 128, 64, 32, 16, 8):
            if rows % cand == 0:
                tr = cand
                break
    return rows, last, tr


def _elementwise(fn, name, ins, n_out, out_dtype=F32):
    shape = ins[0].shape
    rows, last, tr = _rows(shape)
    flat = [a.reshape(rows, last) for a in ins]
    n_in = len(ins)

    def body(*refs):
        res = fn(*[r[...] for r in refs[:n_in]])
        if n_out == 1:
            res = (res,)
        for r, v in zip(refs[n_in:], res):
            r[...] = v.astype(r.dtype)

    spec = pl.BlockSpec((tr, last), lambda i: (i, 0))
    outs = pl.pallas_call(
        body, name=name, grid=(rows // tr,),
        in_specs=[spec] * n_in, out_specs=[spec] * n_out,
        out_shape=[jax.ShapeDtypeStruct((rows, last), out_dtype)] * n_out,
        compiler_params=_params(),
    )(*flat)
    return [o.reshape(shape) for o in outs]


def _add_pairs(gs, os, c_idx):
    n = len(gs)
    halves = [(g.shape[1] // 2, g.shape[2]) for g in gs]

    def body(c_ref, *refs):
        for a in range(n):
            refs[2 * n + a][...] = refs[2 * a][...] + refs[2 * a + 1][...]

    in_specs = []
    for h, C in halves:
        in_specs += [pl.BlockSpec((None, h, C), lambda q, c: (q, c[0], 0)), pl.BlockSpec((None, h, C), lambda q, c: (q, 0, 0))]
    return pl.pallas_call(
        body, name="add_pairs",
        grid_spec=pltpu.PrefetchScalarGridSpec(
            num_scalar_prefetch=1, grid=(N_CHIPS,), in_specs=in_specs,
            out_specs=[pl.BlockSpec((None, h, C), lambda q, c: (q, 0, 0)) for h, C in halves]),
        out_shape=[jax.ShapeDtypeStruct((N_CHIPS, h, C), F32) for h, C in halves],
        compiler_params=_params(),
    )(c_idx.astype(jnp.int32).reshape(1), *[x for pair in zip(gs, os) for x in pair])


def _add_chips(ps, rs, q_idx):
    n = len(ps)
    steps = 2
    blocks = [(p.shape[1] // steps, p.shape[2]) for p in ps]

    def body(q_ref, *refs):
        for a in range(n):
            p_ref, r0_ref, r1_ref, r2_ref = refs[4 * a:4 * a + 4]
            refs[4 * n + a][...] = (p_ref[...] + r0_ref[...]) + (r1_ref[...] + r2_ref[...])

    def arrived(tr, C, k):
        return pl.BlockSpec((None, tr, C), lambda i, q: (k, i, 0))

    in_specs, operands = [], []
    for (tr, C), p, r in zip(blocks, ps, rs):
        in_specs += [pl.BlockSpec((None, tr, C), lambda i, q: (q[0], i, 0)), arrived(tr, C, 0), arrived(tr, C, 1),
                     arrived(tr, C, 2)]
        operands += [p, r, r, r]
    return pl.pallas_call(
        body, name="add_chips",
        grid_spec=pltpu.PrefetchScalarGridSpec(
            num_scalar_prefetch=1, grid=(steps,), in_specs=in_specs,
            out_specs=[pl.BlockSpec((tr, C), lambda i, q: (i, 0)) for tr, C in blocks]),
        out_shape=[jax.ShapeDtypeStruct((p.shape[1], p.shape[2]), F32) for p in ps],
        compiler_params=_params(),
    )(q_idx.astype(jnp.int32).reshape(1), *operands)


def _adamw(w, g, m, v):
    m = ADAM_B1 * m + (1.0 - ADAM_B1) * g
    v = ADAM_B2 * v + (1.0 - ADAM_B2) * jnp.square(g)
    m_hat = m / (1.0 - ADAM_B1 ** ADAM_STEP)
    v_hat = v / (1.0 - ADAM_B2 ** ADAM_STEP)
    delta = -ADAM_LR * (m_hat / (jnp.sqrt(v_hat) + ADAM_EPS) + ADAM_WD * w)
    return delta, m, v


def _place():
    x, y, c = lax.axis_index("x"), lax.axis_index("y"), lax.axis_index("c")
    chips = [(1 - x, y), (x, 1 - y), (1 - x, 1 - y)]
    return x, y, c, chips


def _remote(src, dst, ssem, rsem, k, dev):
    return pltpu.make_async_remote_copy(src_ref=src, dst_ref=dst, send_sem=ssem.at[k], recv_sem=rsem.at[k],
                                        device_id=dev, device_id_type=MESH)


def _gather_weights(shards):
    n = len(shards)
    halves = [s.shape[1] // 2 for s in shards]

    def body(*refs):
        src, out = refs[:n], refs[n:2 * n]
        ssem, rsem = refs[2 * n:]
        x, y, c, chips = _place()
        me_q = 2 * x + y
        sib = (x, y, 1 - c)

        def half(a, q, cc):
            return out[a].at[q, :, pl.ds(cc * halves[a], halves[a]), :]

        first = []
        for a in range(n):
            mine = src[a].at[:, pl.ds(c * halves[a], halves[a]), :]
            for r, chip in enumerate(chips):
                first.append(_remote(mine, half(a, me_q, c), ssem, rsem, a * 3 + r, (*chip, c)))
        for cp in first:
            cp.start()
        passed = []
        for a in range(n):
            for r, chip in enumerate(chips):
                q = 2 * chip[0] + chip[1]
                k = a * 3 + r
                _remote(half(a, q, c), half(a, q, c), ssem, rsem, k, (*chip, c)).wait_recv()
                cp = _remote(half(a, q, c), half(a, q, c), ssem, rsem, 3 * n + k, sib)
                cp.start()
                passed.append(cp)
        for a in range(n):
            for r, chip in enumerate(chips):
                q = 2 * chip[0] + chip[1]
                _remote(half(a, q, 1 - c), half(a, q, 1 - c), ssem, rsem, 3 * n + a * 3 + r, sib).wait_recv()
        for cp in first + passed:
            cp.wait_send()

    return pl.pallas_call(
        body, name="gather_weights",
        in_specs=[HBM_SPEC] * n, out_specs=[HBM_SPEC] * n,
        out_shape=[jax.ShapeDtypeStruct((N_CHIPS,) + s.shape, s.dtype) for s in shards],
        scratch_shapes=[pltpu.SemaphoreType.DMA((6 * n,)), pltpu.SemaphoreType.DMA((6 * n,))],
        compiler_params=_params(has_side_effects=True),
    )(*shards)


def _gather_over_ici(shards):
    n = len(shards)
    halves = [s.shape[1] // 2 for s in shards]

    def copies(src, out, ssem, rsem):
        x, y, c, chips = _place()
        me_q = 2 * x + y
        res = []
        for a in range(n):
            rows = pl.ds(c * halves[a], halves[a])
            mine = src[a].at[:, rows, :]
            for r, chip in enumerate(chips):
                dev = (*chip, c)
                res.append((_remote(mine, out[a].at[me_q, :, rows, :], ssem, rsem, a * 3 + r, dev),
                            _remote(mine, out[a].at[2 * chip[0] + chip[1], :, rows, :], ssem, rsem, a * 3 + r, dev)))
        return res

    return _Hosted(list(shards), [jax.ShapeDtypeStruct((N_CHIPS,) + s.shape, s.dtype) for s in shards], 3 * n, copies)


def _pass_over_d2d(gathered):
    n = len(gathered)
    halves = [g.shape[2] // 2 for g in gathered]

    def copies(_, out, ssem, rsem):
        x, y, c, chips = _place()
        sib = (x, y, 1 - c)
        res = []
        for a in range(n):
            for r, chip in enumerate(chips):
                q = 2 * chip[0] + chip[1]
                mine = out[a].at[q, :, pl.ds(c * halves[a], halves[a]), :]
                theirs = out[a].at[q, :, pl.ds((1 - c) * halves[a], halves[a]), :]
                res.append((_remote(mine, mine, ssem, rsem, a * 3 + r, sib),
                            _remote(theirs, theirs, ssem, rsem, a * 3 + r, sib)))
        return res

    return _Hosted(list(gathered), [jax.ShapeDtypeStruct(g.shape, g.dtype) for g in gathered], 3 * n, copies,
                   in_place=True)


def _pass_to_sibling(gathered):
    n = len(gathered)
    halves = [g.shape[2] // 2 for g in gathered]

    def body(*refs):
        out = refs[n:2 * n]
        ssem, rsem = refs[2 * n:]
        x, y, c, chips = _place()
        sib = (x, y, 1 - c)

        def half(a, q, cc):
            return out[a].at[q, :, pl.ds(cc * halves[a], halves[a]), :]

        cps = []
        for a in range(n):
            for r, chip in enumerate(chips):
                q = 2 * chip[0] + chip[1]
                cps.append(_remote(half(a, q, c), half(a, q, c), ssem, rsem, a * 3 + r, sib))
        for cp in cps:
            cp.start()
        for a in range(n):
            for r, chip in enumerate(chips):
                q = 2 * chip[0] + chip[1]
                _remote(half(a, q, 1 - c), half(a, q, 1 - c), ssem, rsem, a * 3 + r, sib).wait_recv()
        for cp in cps:
            cp.wait_send()

    return pl.pallas_call(
        body, name="pass_to_sibling",
        in_specs=[HBM_SPEC] * n, out_specs=[HBM_SPEC] * n,
        out_shape=[jax.ShapeDtypeStruct(g.shape, g.dtype) for g in gathered],
        input_output_aliases={a: a for a in range(n)},
        scratch_shapes=[pltpu.SemaphoreType.DMA((3 * n,)), pltpu.SemaphoreType.DMA((3 * n,))],
        compiler_params=_params(has_side_effects=True),
    )(*gathered)


def _scatter_over_ici(parts):
    n = len(parts)

    def copies(src, out, ssem, rsem):
        x, y, c, chips = _place()
        res = []
        for a in range(n):
            for r, chip in enumerate(chips):
                cp = _remote(src[a].at[2 * chip[0] + chip[1]], out[a].at[r], ssem, rsem, a * 3 + r, (*chip, c))
                res.append((cp, cp))
        return res

    return _Hosted(list(parts), [jax.ShapeDtypeStruct((3,) + p.shape[1:], F32) for p in parts], 3 * n, copies)


def _swap_over_d2d(grads):
    n = len(grads)
    halves = [g.shape[1] // 2 for g in grads]

    def copies(src, out, ssem, rsem):
        x, y, c, _ = _place()
        res = []
        for a in range(n):
            cp = _remote(src[a].at[:, pl.ds((1 - c) * halves[a], halves[a]), :], out[a], ssem, rsem, a, (x, y, 1 - c))
            res.append((cp, cp))
        return res

    return _Hosted(list(grads), [jax.ShapeDtypeStruct((N_CHIPS, h, g.shape[2]), F32) for g, h in zip(grads, halves)],
                   n, copies)


def _swap_halves(grads):
    n = len(grads)
    halves = [g.shape[1] // 2 for g in grads]

    def body(*refs):
        src, out = refs[:n], refs[n:2 * n]
        ssem, rsem = refs[2 * n:]
        x, y, c, _ = _place()
        cps = [_remote(src[a].at[:, pl.ds((1 - c) * halves[a], halves[a]), :], out[a], ssem, rsem, a, (x, y, 1 - c))
               for a in range(n)]
        for cp in cps:
            cp.start()
        for cp in cps:
            cp.wait()

    return pl.pallas_call(
        body, name="swap_halves",
        in_specs=[HBM_SPEC] * n, out_specs=[HBM_SPEC] * n,
        out_shape=[jax.ShapeDtypeStruct((N_CHIPS, h, g.shape[2]), F32) for g, h in zip(grads, halves)],
        scratch_shapes=[pltpu.SemaphoreType.DMA((n,)), pltpu.SemaphoreType.DMA((n,))],
        compiler_params=_params(has_side_effects=True),
    )(*grads)


def _swap_reduced_over_d2d(reduced):
    n = len(reduced)

    def copies(src, out, ssem, rsem):
        x, y, c, _ = _place()
        res = []
        for a in range(n):
            cp = _remote(src[a], out[a], ssem, rsem, a, (x, y, 1 - c))
            res.append((cp, cp))
        return res

    return _Hosted(list(reduced), [jax.ShapeDtypeStruct(r.shape, F32) for r in reduced], n, copies)


def _swap_reduced(reduced):
    n = len(reduced)

    def body(*refs):
        src, out = refs[:n], refs[n:2 * n]
        ssem, rsem = refs[2 * n:]
        x, y, c, _ = _place()
        cps = [_remote(src[a], out[a], ssem, rsem, a, (x, y, 1 - c)) for a in range(n)]
        for cp in cps:
            cp.start()
        for cp in cps:
            cp.wait()

    return pl.pallas_call(
        body, name="swap_reduced",
        in_specs=[HBM_SPEC] * n, out_specs=[HBM_SPEC] * n,
        out_shape=[jax.ShapeDtypeStruct(r.shape, F32) for r in reduced],
        scratch_shapes=[pltpu.SemaphoreType.DMA((n,)), pltpu.SemaphoreType.DMA((n,))],
        compiler_params=_params(has_side_effects=True),
    )(*reduced)


def _allreduce_small(buf, hosted=None):
    R, L = buf.shape

    def body(*refs):
        (buf_ref, out_ref, pair_ref, chip_ref, ssem, rsem), start, wait = _host(hosted, refs, 1, 1, True, True)
        start()
        x, y, c, chips = _place()
        me_q = 2 * x + y
        pair_ref[c] = buf_ref[...]
        to_sib = _remote(buf_ref, pair_ref.at[c], ssem, rsem, 0, (x, y, 1 - c))
        to_sib.start()
        _remote(buf_ref, pair_ref.at[1 - c], ssem, rsem, 0, (x, y, 1 - c)).wait_recv()
        chip_ref[me_q] = pair_ref[0] + pair_ref[1]
        cps = [_remote(chip_ref.at[me_q], chip_ref.at[me_q], ssem, rsem, 1 + r, (*chip, c))
               for r, chip in enumerate(chips)]
        for cp in cps:
            cp.start()
        for r, chip in enumerate(chips):
            q = 2 * chip[0] + chip[1]
            _remote(chip_ref.at[q], chip_ref.at[q], ssem, rsem, 1 + r, (*chip, c)).wait_recv()
        out_ref[...] = (chip_ref[0] + chip_ref[1]) + (chip_ref[2] + chip_ref[3])
        to_sib.wait_send()
        for cp in cps:
            cp.wait_send()
        wait()

    h_ins = hosted.ins if hosted else []
    res = pl.pallas_call(
        body, name="allreduce_small",
        in_specs=[VMEM_SPEC] + [HBM_SPEC] * len(h_ins), out_specs=[VMEM_SPEC] + [HBM_SPEC] * len(h_ins),
        out_shape=[jax.ShapeDtypeStruct((R, L), F32)] + (hosted.out_shapes if hosted else []),
        scratch_shapes=[pltpu.VMEM((2, R, L), F32), pltpu.VMEM((N_CHIPS, R, L), F32),
                        pltpu.SemaphoreType.DMA((4,)), pltpu.SemaphoreType.DMA((4,))]
        + (hosted.sems() if hosted else []),
        compiler_params=_params(has_side_effects=True),
    )(buf, *h_ins)
    return res[0], res[1:]


def _pack(arrays):
    flat = jnp.concatenate([a.reshape(-1) for a in arrays])
    pad = (-flat.shape[0]) % (8 * LANES)
    return jnp.pad(flat, (0, pad)).reshape(-1, LANES)


def _unpack(buf, like):
    flat = buf.reshape(-1)
    out, off = [], 0
    for a in like:
        out.append(flat[off:off + a.size].reshape(a.shape))
        off += a.size
    return out


def _block_diag(pw):
    rows = []
    for gi in range(len(POOL_WINDOWS)):
        blocks = [pw[gi] if gj == gi else jnp.zeros_like(pw[gi]) for gj in range(len(POOL_WINDOWS))]
        rows.append(jnp.concatenate(blocks, axis=1))
    return jnp.concatenate(rows, axis=0)


def kernel(x, norm1, w_in, pool_w, pool_scale, sg_norm, sg_w, sg_b, w_out, norm2, w_up, w_down, final_norm, loss_target, m_norm1, m_w_in, m_pool_w, m_pool_scale, m_sg_norm, m_sg_w, m_sg_b, m_w_out, m_norm2, m_w_up, m_w_down, m_final_norm, v_norm1, v_w_in, v_pool_w, v_pool_scale, v_sg_norm, v_sg_w, v_sg_b, v_w_out, v_norm2, v_w_up, v_w_down, v_final_norm):
    depth = norm1.shape[0]
    T = x.shape[1]
    xs = x.reshape(T, D_MODEL)
    target = loss_target.reshape(T, D_MODEL)

    assert depth == 2
    c_idx = lax.axis_index("c")
    q_idx = 2 * lax.axis_index("x") + lax.axis_index("y")
    own = [w.astype(BF16) for w in (w_in, w_out, w_up, w_down)]
    gathered = {(0, 0): _gather_weights([own[0][:1]])[0]}

    def full(a, l, axis):
        blocks = lax.dynamic_update_slice(gathered[(a, l)], own[a][l][None, None], (q_idx, 0, 0, 0))[:, 0]
        if axis is None:
            return blocks
        if axis == 0:
            return blocks.reshape(-1, blocks.shape[-1])
        return jnp.concatenate([blocks[q] for q in range(N_CHIPS)], axis=axis)

    half_way = {}

    def gather_behind(call, keys, at_once):
        res, over_ici = call(_gather_over_ici([own[a][l:l + 1] for a, l in keys]))
        gathered.update(zip(keys[:at_once], _pass_to_sibling(over_ici[:at_once])))
        half_way.update(zip(keys[at_once:], over_ici[at_once:]))
        return res

    def pass_behind(call, keys):
        res, done = call(_pass_over_d2d([half_way.pop(k) for k in keys]))
        gathered.update(zip(keys, done))
        return res

    tril = jnp.tril(jnp.ones((CHUNK, CHUNK), F32))
    saved = []
    cur = xs
    wi, wo, wu, wd = {}, {}, {}, {}
    for l in range(depth):
        wbd = _block_diag(pool_w[l]).astype(BF16)
        wm = sg_w[l] * tril
        wm_s = wm.reshape(SG_HEADS * CHUNK, CHUNK).astype(BF16)
        wmt_s = jnp.swapaxes(wm, 1, 2).reshape(SG_HEADS * CHUNK, CHUNK).astype(BF16)
        bias = jnp.repeat(sg_b[l].T, SB_HD, axis=1)
        n1, n2 = norm1[l][None], norm2[l][None]
        psc, sgn = pool_scale[l][None], sg_norm[l][None]
        wi[l] = full(0, l, 1)
        proj, h, qkv = _inproj_fwd(cur, n1, wi[l])
        ya = _pool_fwd(proj, wbd, psc)
        yb = _sg_fwd(proj, wm_s, bias, sgn)
        if l == 0:
            yc = gather_behind(lambda hosted: _attn_fwd(qkv, hosted), [(1, 0), (2, 0), (3, 0)], 1)
            wo[l] = full(1, l, 0)
            x1, ymix = pass_behind(lambda hosted: _outproj_fwd(cur, ya, yb, yc, wo[l], hosted), [(2, 0), (3, 0)])
        else:
            yc = pass_behind(lambda hosted: _attn_fwd(qkv, hosted), [(1, l), (2, l), (3, l)])
            wo[l] = full(1, l, 0)
            (x1, ymix), _ = _outproj_fwd(cur, ya, yb, yc, wo[l])
        wu[l], wd[l] = full(2, l, None), full(3, l, 0)
        if l == 0:
            x2, h2, u, act = gather_behind(lambda hosted: _mlp_fwd(x1, n2, wu[l], wd[l], hosted),
                                           [(0, 1), (1, 1), (2, 1), (3, 1)], 1)
        else:
            (x2, h2, u, act), _ = _mlp_fwd(x1, n2, wu[l], wd[l])
        saved.append(dict(x0=cur, x1=x1, proj=proj, h=h, qkv=qkv, yc=yc, ymix=ymix, h2=h2, u=u, act=act,
                          wbd=wbd, wm_s=wm_s, wmt_s=wmt_s, bias=bias, n1=n1, n2=n2, psc=psc, sgn=sgn))
        cur = x2

    loss_row, dcur, d_final = _loss_head(cur, final_norm[None], target)

    small = [None] * depth
    grads, parts, reduced = {}, {}, {}

    def pair_up(keys, swapped):
        parts.update(zip(keys, _add_pairs([grads[k] for k in keys], swapped, c_idx)))

    def chip_up(keys, arrived):
        reduced.update(zip(keys, _add_chips([parts[k] for k in keys], arrived, q_idx)))

    for l in reversed(range(depth)):
        s = saved[l]
        if l == 0:
            keys = [(2, 1), (3, 1)]
            (dx1, du, d_n2), arrived = _mlp_bwd(dcur, s["x1"], s["n2"], s["u"], wu[l], wd[l],
                                                _scatter_over_ici([parts[k] for k in keys]))
            chip_up(keys, arrived)
        else:
            (dx1, du, d_n2), _ = _mlp_bwd(dcur, s["x1"], s["n2"], s["u"], wu[l], wd[l])
        if l == 0:
            keys = [(0, 1)]
            grads[(2, l)], arrived = _tn_matmul(s["h2"], du, "grad_w_up", n_split=N_CHIPS,
                                                hosted=_scatter_over_ici([parts[k] for k in keys]))
            chip_up(keys, arrived)
            keys = [(1, 1)]
            g_down, arrived = _tn_matmul(s["act"], dcur, "grad_w_down",
                                         hosted=_scatter_over_ici([parts[k] for k in keys]))
            chip_up(keys, arrived)
        else:
            grads[(2, l)] = _tn_matmul(s["h2"], du, "grad_w_up", n_split=N_CHIPS)
            g_down = _tn_matmul(s["act"], dcur, "grad_w_down")
        grads[(3, l)] = g_down[0].reshape(N_CHIPS, D_FF // N_CHIPS, D_MODEL)
        dymix = _nt_matmul(dx1, wo[l])
        grads[(1, l)] = _tn_matmul(s["ymix"], dx1, "grad_w_out")[0].reshape(N_CHIPS, D_MODEL // N_CHIPS, D_MODEL)
        da_in, d_wbd, d_psc = _pool_bwd(s["proj"], dymix, s["wbd"], s["psc"])
        if l == 0:
            keys = [(1, 0), (2, 0), (3, 0)]
            (du_pre, dv_pre, d_wm, d_bias, d_sgn), swapped = _sg_bwd(
                s["proj"], dymix, s["wm_s"], s["wmt_s"], s["bias"], s["sgn"], _swap_over_d2d([grads[k] for k in keys]))
            pair_up(keys, swapped)
            (dq, dk, dv), arrived = _attn_bwd(s["qkv"], s["yc"], dymix, _scatter_over_ici([parts[k] for k in keys]))
            chip_up(keys, arrived)
        else:
            (du_pre, dv_pre, d_wm, d_bias, d_sgn), _ = _sg_bwd(s["proj"], dymix, s["wm_s"], s["wmt_s"], s["bias"], s["sgn"])
            keys = [(1, l), (2, l), (3, l)]
            (dq, dk, dv), swapped = _attn_bwd(s["qkv"], s["yc"], dymix, _swap_over_d2d([grads[k] for k in keys]))
            pair_up(keys, swapped)
        pieces = [da_in, du_pre, dv_pre, dq, dk, dv]
        if l == 0:
            keys = sorted(reduced)
            g_in_l, swapped = _inproj_grad(s["h"], pieces, _swap_reduced_over_d2d([reduced[k] for k in keys]))
            theirs = dict(zip(keys, swapped))
        else:
            g_in_l, _ = _inproj_grad(s["h"], pieces)
        grads[(0, l)] = g_in_l[0].reshape(D_MODEL, N_CHIPS, IN_COLS // N_CHIPS).transpose(1, 0, 2)
        if l == 0:
            keys = [(0, 0)]
            pair_up(keys, _swap_halves([grads[k] for k in keys]))
            (dx0, d_n1), arrived = _inproj_bwd(pieces, wi[l], s["x0"], s["n1"], dx1,
                                               _scatter_over_ici([parts[k] for k in keys]))
            chip_up(keys, arrived)
        else:
            keys = [(0, l)]
            (dx0, d_n1), swapped = _inproj_bwd(pieces, wi[l], s["x0"], s["n1"], dx1,
                                               _swap_over_d2d([grads[k] for k in keys]))
            pair_up(keys, swapped)
        d_pw = jnp.stack([d_wbd[gi * POOL_GW:(gi + 1) * POOL_GW, gi * POOL_GW:(gi + 1) * POOL_GW]
                          for gi in range(len(POOL_WINDOWS))])
        small[l] = dict(norm1=d_n1[0], pool_w=d_pw, pool_scale=d_psc[0], sg_norm=d_sgn[0],
                        sg_w=d_wm.reshape(SG_HEADS, CHUNK, CHUNK), sg_b=d_bias[:, :SG_HEADS].T, norm2=d_n2[0])
        dcur = dx0
    grad_x = dcur.reshape(x.shape)

    names = ["norm1", "pool_w", "pool_scale", "sg_norm", "sg_w", "sg_b", "norm2"]
    slot = jnp.zeros((1,), F32)
    small_w = [norm1, pool_w, pool_scale, sg_norm, sg_w, sg_b, norm2, final_norm, slot]
    small_m = [m_norm1, m_pool_w, m_pool_scale, m_sg_norm, m_sg_w, m_sg_b, m_norm2, m_final_norm, slot]
    small_v = [v_norm1, v_pool_w, v_pool_scale, v_sg_norm, v_sg_w, v_sg_b, v_norm2, v_final_norm, slot]
    small_g = [jnp.stack([small[l][k] for l in range(depth)]) for k in names] + [d_final[0], loss_row[0, :1]]
    keys = [(0, 0)]
    g_packed, _ = _allreduce_small(_pack(small_g))
    theirs.update(zip(keys, _swap_reduced([reduced[k] for k in keys])))

    def joined(a):
        layers = []
        for l in range(depth):
            mine, other = reduced[(a, l)], theirs[(a, l)]
            layers.append(jnp.where(c_idx == 0, jnp.concatenate([mine, other]), jnp.concatenate([other, mine])))
        return jnp.stack(layers)

    gw_in, gw_out, gw_up, gw_down = [joined(a) for a in range(4)]

    loss = _unpack(g_packed, small_w)[-1][0]
    s_delta, s_m, s_v = _elementwise(_adamw, "adamw_small", [_pack(small_w), g_packed, _pack(small_m), _pack(small_v)], 3)
    gs = dict(zip(names + ["final_norm"], _unpack(g_packed, small_w)))
    ds = dict(zip(names + ["final_norm"], _unpack(s_delta, small_w)))
    ms = dict(zip(names + ["final_norm"], _unpack(s_m, small_w)))
    vs = dict(zip(names + ["final_norm"], _unpack(s_v, small_w)))

    big_g = dict(w_in=gw_in, w_out=gw_out, w_up=gw_up, w_down=gw_down)
    big_w = dict(w_in=(w_in, m_w_in, v_w_in), w_out=(w_out, m_w_out, v_w_out),
                 w_up=(w_up, m_w_up, v_w_up), w_down=(w_down, m_w_down, v_w_down))
    for k, (w, m, v) in big_w.items():
        operands = [w, big_g[k], m, v]
        if k == "w_in":
            operands = [jnp.swapaxes(o, 1, 2) for o in operands]
        ds[k], ms[k], vs[k] = _elementwise(_adamw, "adamw_" + k, operands, 3)
        if k == "w_in":
            ds[k], ms[k], vs[k] = [jnp.swapaxes(o, 1, 2) for o in (ds[k], ms[k], vs[k])]
        gs[k] = big_g[k]

    order = ["norm1", "w_in", "pool_w", "pool_scale", "sg_norm", "sg_w", "sg_b", "w_out", "norm2", "w_up", "w_down",
             "final_norm"]
    return (loss, grad_x, *[gs[k] for k in order], *[ds[k] for k in order], *[ms[k] for k in order],
            *[vs[k] for k in order])
```

```python
import jax
import jax.numpy as jnp
from jax import lax
from jax.experimental import pallas as pl
from jax.experimental.pallas import tpu as pltpu

F32 = jnp.float32
BF16 = jnp.bfloat16
MESH = pl.DeviceIdType.MESH

EPS = 1e-6
D_MODEL = 1024
POOL_WIDTH = 256
SG_WIDTH = 256
SB_WIDTH = 512
POOL_WINDOWS = (2, 4, 8, 16)
POOL_GW = 64
POOL_HALO = 16
CHUNK = 128
SG_HEADS = 4
SB_HD = 64
SB_SCALE = 0.125
IN_COLS = 2304
QKV_OFF = 768
D_FF = 4096
N_CHIPS = 4
LANES = 128
VMEM_LIMIT = 56 * 1024 * 1024
MLP_CHUNK = 512
ATTN_TILE = 256
UNDERFLOW = -104.0

ADAM_LR = 0.001
ADAM_B1 = 0.9
ADAM_B2 = 0.999
ADAM_EPS = 1e-08
ADAM_WD = 0.01
ADAM_STEP = 10

HBM_SPEC = pl.BlockSpec(memory_space=pl.ANY)
VMEM_SPEC = pl.BlockSpec(memory_space=pltpu.VMEM)


def _params(**kw):
    return pltpu.CompilerParams(vmem_limit_bytes=VMEM_LIMIT, **kw)


def _tile(n, pref):
    if n <= pref:
        return n
    for t in range(pref - pref % LANES, 0, -LANES):
        if n % t == 0:
            return t
    raise ValueError((n, pref))


def _nn(a, b):
    return jnp.dot(a, b, preferred_element_type=F32)


def _nt(a, b):
    return lax.dot_general(a, b, (((1,), (1,)), ((), ())), preferred_element_type=F32)


def _tn(a, b):
    return lax.dot_general(a, b, (((0,), (0,)), ((), ())), preferred_element_type=F32)


def _rms_fwd(x, g):
    r = lax.rsqrt(jnp.mean(x * x, axis=-1, keepdims=True) + EPS)
    xhat = x * r
    return xhat * g, xhat, r


def _rms_bwd(dy, xhat, r, g):
    dxhat = dy * g
    dx = r * (dxhat - xhat * jnp.mean(dxhat * xhat, axis=-1, keepdims=True))
    return dx, dy * xhat


_GELU_K = 0.7978845608028654
_GELU_C = 0.044715


def _gelu(x):
    return 0.5 * x * (1.0 + jnp.tanh(_GELU_K * (x + _GELU_C * x * x * x)))


def _gelu_and_grad(x):
    x2 = x * x
    t = jnp.tanh(_GELU_K * (x + _GELU_C * x2 * x))
    half = 0.5 * (1.0 + t)
    return x * half, half + 0.5 * x * (1.0 - t * t) * _GELU_K * (1.0 + 3.0 * _GELU_C * x2)


def _inproj_fwd(x, g, w):
    T, D = x.shape
    N = w.shape[1]
    tt = _tile(T, 1024)

    def body(x_ref, g_ref, w_ref, proj_ref, h_ref, qkv_ref):
        h, _, _ = _rms_fwd(x_ref[...], g_ref[...])
        hb = h.astype(BF16)
        h_ref[...] = hb
        p = _nn(hb, w_ref[...])
        proj_ref[...] = p[:, :QKV_OFF]
        qkv_ref[...] = p[:, QKV_OFF:].astype(BF16)

    return pl.pallas_call(
        body, name="inproj_fwd", grid=(T // tt,),
        in_specs=[pl.BlockSpec((tt, D), lambda i: (i, 0)), pl.BlockSpec((1, D), lambda i: (0, 0)),
                  pl.BlockSpec((D, N), lambda i: (0, 0))],
        out_specs=[pl.BlockSpec((tt, QKV_OFF), lambda i: (i, 0)), pl.BlockSpec((tt, D), lambda i: (i, 0)),
                   pl.BlockSpec((tt, N - QKV_OFF), lambda i: (i, 0))],
        out_shape=[jax.ShapeDtypeStruct((T, QKV_OFF), F32), jax.ShapeDtypeStruct((T, D), BF16),
                   jax.ShapeDtypeStruct((T, N - QKV_OFF), BF16)],
        compiler_params=_params(),
    )(x, g, w)


def _inproj_bwd(pieces, w, x, g, dres, hosted=None):
    T, D = x.shape
    N = w.shape[1]
    tt = _tile(T, 512)
    nt = T // tt
    widths = [p.shape[1] for p in pieces]
    offs = [sum(widths[:k]) for k in range(len(widths))]
    assert sum(widths) == N
    n_p = len(pieces)

    def body(*refs):
        i = pl.program_id(0)
        own, start, wait = _host(hosted, refs, n_p + 4, 2, i == 0, i == nt - 1)
        start()
        p_refs = own[:n_p]
        w_ref, x_ref, g_ref, dres_ref, dx_ref, dg_ref, dproj_ref = own[n_p:]
        for p_ref, o, wd in zip(p_refs, offs, widths):
            dproj_ref[:, o:o + wd] = p_ref[...].astype(BF16)
        dh = _nt(dproj_ref[...], w_ref[...])
        gv = g_ref[...]
        _, xhat, r = _rms_fwd(x_ref[...], gv)
        dx, dgrow = _rms_bwd(dh, xhat, r, gv)
        dx_ref[...] = dres_ref[...] + dx

        @pl.when(i == 0)
        def _():
            dg_ref[...] = jnp.zeros_like(dg_ref)

        dg_ref[...] += jnp.sum(dgrow, axis=0, keepdims=True)
        wait()

    h_ins = hosted.ins if hosted else []
    res = pl.pallas_call(
        body, name="inproj_bwd_hosting" if hosted else "inproj_bwd", grid=(nt,),
        in_specs=[pl.BlockSpec((tt, wd), lambda i: (i, 0)) for wd in widths] + [
            pl.BlockSpec((D, N), lambda i: (0, 0)), pl.BlockSpec((tt, D), lambda i: (i, 0)),
            pl.BlockSpec((1, D), lambda i: (0, 0)), pl.BlockSpec((tt, D), lambda i: (i, 0))] + [HBM_SPEC] * len(h_ins),
        out_specs=[pl.BlockSpec((tt, D), lambda i: (i, 0)), pl.BlockSpec((1, D), lambda i: (0, 0))]
        + [HBM_SPEC] * len(h_ins),
        out_shape=[jax.ShapeDtypeStruct((T, D), F32), jax.ShapeDtypeStruct((1, D), F32)]
        + (hosted.out_shapes if hosted else []),
        scratch_shapes=[pltpu.VMEM((tt, N), BF16)] + (hosted.sems() if hosted else []),
        compiler_params=_params(has_side_effects=hosted is not None),
    )(*pieces, w, x, g, dres, *h_ins)
    return res[:2], res[2:]


def _inproj_grad(h, pieces, hosted=None):
    T, D = h.shape
    tt = _tile(T, 1024)
    nt = T // tt
    widths = [p.shape[1] for p in pieces]
    offs = [sum(widths[:k]) for k in range(len(widths))]
    N = sum(widths)
    n_p = len(pieces)

    def body(*refs):
        t = pl.program_id(0)
        own, start, wait = _host(hosted, refs, n_p + 1, 1, t == 0, t == nt - 1)
        start()
        h_ref, p_refs, o_ref = own[0], own[1:1 + n_p], own[1 + n_p]

        @pl.when(t == 0)
        def _():
            o_ref[...] = jnp.zeros_like(o_ref)

        hv = h_ref[...]
        for p_ref, o, wd in zip(p_refs, offs, widths):
            o_ref[:, o:o + wd] += _tn(hv, p_ref[...].astype(BF16))
        wait()

    h_ins = hosted.ins if hosted else []
    res = pl.pallas_call(
        body, name="grad_w_in_hosting" if hosted else "grad_w_in", grid=(nt,),
        in_specs=[pl.BlockSpec((tt, D), lambda t: (t, 0))] + [pl.BlockSpec((tt, wd), lambda t: (t, 0)) for wd in widths]
        + [HBM_SPEC] * len(h_ins),
        out_specs=[pl.BlockSpec((None, D, N), lambda t: (0, 0, 0))] + [HBM_SPEC] * len(h_ins),
        out_shape=[jax.ShapeDtypeStruct((1, D, N), F32)] + (hosted.out_shapes if hosted else []),
        scratch_shapes=hosted.sems() if hosted else [],
        compiler_params=_params(has_side_effects=hosted is not None),
    )(h, *pieces, *h_ins)
    return res[0], res[1:]


def _pool_select(s2, s4, s8, s16, grp):
    return jnp.where(grp == 0, s2, jnp.where(grp == 1, s4, jnp.where(grp == 2, s8, s16)))


def _pool_count(t_glob, grp):
    win = jnp.where(grp == 0, 2, jnp.where(grp == 1, 4, jnp.where(grp == 2, 8, 16)))
    return jnp.minimum(t_glob + 1, win).astype(F32)


def _pool_diff(a, halo, base, tt):
    n = tt + POOL_HALO
    ext = jnp.concatenate([halo, a], axis=0)
    s2 = ext + pltpu.roll(ext, 1, 0)
    s4 = s2 + pltpu.roll(s2, 2, 0)
    s8 = s4 + pltpu.roll(s4, 4, 0)
    s16 = s8 + pltpu.roll(s8, 8, 0)
    grp = lax.broadcasted_iota(jnp.int32, (n, POOL_WIDTH), 1) // POOL_GW
    t_glob = lax.broadcasted_iota(jnp.int32, (n, POOL_WIDTH), 0) + (base - POOL_HALO)
    pooled = _pool_select(s2, s4, s8, s16, grp) / _pool_count(t_glob, grp)
    return pooled[POOL_HALO:] - a


def _pool_specs(T, tt):
    hb = tt // POOL_HALO
    return [pl.BlockSpec((tt, POOL_WIDTH), lambda i: (i, 0)),
            pl.BlockSpec((POOL_HALO, POOL_WIDTH), lambda i: (jnp.maximum(i * hb - 1, 0), 0))]


def _pool_fwd(proj, wbd, scale):
    T = proj.shape[0]
    tt = _tile(T, 1024)

    def body(a_ref, halo_ref, w_ref, sc_ref, y_ref):
        i = pl.program_id(0)
        halo = jnp.where(i > 0, halo_ref[...], 0.0)
        d = _pool_diff(a_ref[...], halo, i * tt, tt)
        y_ref[...] = _nn(d.astype(BF16), w_ref[...]) * sc_ref[...]

    return pl.pallas_call(
        body, name="pool_fwd", grid=(T // tt,),
        in_specs=_pool_specs(T, tt) + [pl.BlockSpec((POOL_WIDTH, POOL_WIDTH), lambda i: (0, 0)),
                                       pl.BlockSpec((1, POOL_WIDTH), lambda i: (0, 0))],
        out_specs=pl.BlockSpec((tt, POOL_WIDTH), lambda i: (i, 0)),
        out_shape=jax.ShapeDtypeStruct((T, POOL_WIDTH), F32),
        compiler_params=_params(),
    )(proj, proj, wbd, scale)


def _pool_bwd(proj, dymix, wbd, scale):
    T = proj.shape[0]
    tt = _tile(T, 1024)
    hb = tt // POOL_HALO
    nblk = T // tt
    n = tt + POOL_HALO

    def body(a_ref, halo_ref, dy_ref, dyn_ref, w_ref, sc_ref, da_ref, dw_ref, dsc_ref):
        i = pl.program_id(0)
        halo = jnp.where(i > 0, halo_ref[...], 0.0)
        d = _pool_diff(a_ref[...], halo, i * tt, tt)
        db = d.astype(BF16)
        wv = w_ref[...]
        sc = sc_ref[...]
        dy = dy_ref[...]
        dys = dy * sc

        @pl.when(i == 0)
        def _():
            dw_ref[...] = jnp.zeros_like(dw_ref)
            dsc_ref[...] = jnp.zeros_like(dsc_ref)

        dsc_ref[...] += jnp.sum(dy * _nn(db, wv), axis=0, keepdims=True)
        dw_ref[...] += _tn(db, dys.astype(BF16))
        dyn = jnp.where(i < nblk - 1, dyn_ref[...], 0.0) * sc
        dd = _nt(jnp.concatenate([dys, dyn], axis=0).astype(BF16), wv)
        grp = lax.broadcasted_iota(jnp.int32, (n, POOL_WIDTH), 1) // POOL_GW
        t_glob = lax.broadcasted_iota(jnp.int32, (n, POOL_WIDTH), 0) + i * tt
        e = dd / _pool_count(t_glob, grp)
        r2 = e + pltpu.roll(e, n - 1, 0)
        r4 = r2 + pltpu.roll(r2, n - 2, 0)
        r8 = r4 + pltpu.roll(r4, n - 4, 0)
        r16 = r8 + pltpu.roll(r8, n - 8, 0)
        da_ref[...] = (_pool_select(r2, r4, r8, r16, grp) - dd)[:tt].astype(BF16)

    return pl.pallas_call(
        body, name="pool_bwd", grid=(nblk,),
        in_specs=_pool_specs(T, tt) + [
            pl.BlockSpec((tt, POOL_WIDTH), lambda i: (i, 0)),
            pl.BlockSpec((POOL_HALO, POOL_WIDTH), lambda i: (jnp.minimum((i + 1) * hb, T // POOL_HALO - 1), 0)),
            pl.BlockSpec((POOL_WIDTH, POOL_WIDTH), lambda i: (0, 0)), pl.BlockSpec((1, POOL_WIDTH), lambda i: (0, 0))],
        out_specs=[pl.BlockSpec((tt, POOL_WIDTH), lambda i: (i, 0)),
                   pl.BlockSpec((POOL_WIDTH, POOL_WIDTH), lambda i: (0, 0)),
                   pl.BlockSpec((1, POOL_WIDTH), lambda i: (0, 0))],
        out_shape=[jax.ShapeDtypeStruct((T, POOL_WIDTH), BF16),
                   jax.ShapeDtypeStruct((POOL_WIDTH, POOL_WIDTH), F32),
                   jax.ShapeDtypeStruct((1, POOL_WIDTH), F32)],
        compiler_params=_params(),
    )(proj, proj, dymix, dymix, wbd, scale)


def _head_select(stacked, grp):
    out = jnp.where(grp == 0, stacked[0:CHUNK], 0.0)
    for h in range(1, SG_HEADS):
        out = out + jnp.where(grp == h, stacked[h * CHUNK:(h + 1) * CHUNK], 0.0)
    return out


def _sg_specs(tt):
    return [pl.BlockSpec((tt, SG_WIDTH), lambda i: (i, 1)), pl.BlockSpec((tt, SG_WIDTH), lambda i: (i, 2))]


def _sg_fwd(proj, wm, bias, g):
    T = proj.shape[0]
    tt = _tile(T, 1024)

    def body(u_ref, v_ref, wm_ref, b_ref, g_ref, y_ref):
        zu = _gelu(u_ref[...])
        vn, _, _ = _rms_fwd(_gelu(v_ref[...]), g_ref[...])
        grp = lax.broadcasted_iota(jnp.int32, (CHUNK, SG_WIDTH), 1) // SB_HD
        for n in range(tt // CHUNK):
            rows = slice(n * CHUNK, (n + 1) * CHUNK)
            sv = _head_select(_nn(wm_ref[...], vn[rows].astype(BF16)), grp) + b_ref[...]
            y_ref[rows, :] = zu[rows] * sv

    return pl.pallas_call(
        body, name="sg_fwd", grid=(T // tt,),
        in_specs=_sg_specs(tt) + [pl.BlockSpec((SG_HEADS * CHUNK, CHUNK), lambda i: (0, 0)),
                                  pl.BlockSpec((CHUNK, SG_WIDTH), lambda i: (0, 0)),
                                  pl.BlockSpec((1, SG_WIDTH), lambda i: (0, 0))],
        out_specs=pl.BlockSpec((tt, SG_WIDTH), lambda i: (i, 0)),
        out_shape=jax.ShapeDtypeStruct((T, SG_WIDTH), F32),
        compiler_params=_params(),
    )(proj, proj, wm, bias, g)


def _sg_bwd(proj, dymix, wm, wmt, bias, g, hosted=None):
    T = proj.shape[0]
    tt = _tile(T, 1024)
    nblk = T // tt

    def body(*refs):
        i = pl.program_id(0)
        (u_ref, v_ref, dy_ref, wm_ref, wmt_ref, b_ref, g_ref, du_ref, dv_ref, dw_ref, db_ref, dg_ref,
         dvn_ref, dbias_ref), start, wait = _host(hosted, refs, 7, 5, i == 0, i == nblk - 1)
        start()
        up, vp = u_ref[...], v_ref[...]
        gv = g_ref[...]
        (zu, gu), (zv, gvp) = _gelu_and_grad(up), _gelu_and_grad(vp)
        vn, xhat, r = _rms_fwd(zv, gv)
        grp = lax.broadcasted_iota(jnp.int32, (CHUNK, SG_WIDTH), 1) // SB_HD

        @pl.when(i == 0)
        def _():
            dw_ref[...] = jnp.zeros_like(dw_ref)
            dbias_ref[...] = jnp.zeros_like(dbias_ref)
            dg_ref[...] = jnp.zeros_like(dg_ref)

        for n in range(tt // CHUNK):
            rows = slice(n * CHUNK, (n + 1) * CHUNK)
            vc = vn[rows].astype(BF16)
            sv = _head_select(_nn(wm_ref[...], vc), grp) + b_ref[...]
            dy = dy_ref[rows, :]
            du_ref[rows, :] = (dy * sv * gu[rows]).astype(BF16)
            dsv = dy * zu[rows]
            dsvb = dsv.astype(BF16)
            dvn_ref[rows, :] = _head_select(_nn(wmt_ref[...], dsvb), grp)
            stacked = jnp.concatenate([jnp.where(grp == h, dsv, 0.0) for h in range(SG_HEADS)], axis=0)
            dw_ref[...] += _nt(stacked.astype(BF16), vc)
            dbias_ref[...] += dsv

        dzv, dgrow = _rms_bwd(dvn_ref[...], xhat, r, gv)
        dg_ref[...] += jnp.sum(dgrow, axis=0, keepdims=True)
        dv_ref[...] = (dzv * gvp).astype(BF16)

        @pl.when(i == nblk - 1)
        def _():
            t_i = lax.broadcasted_iota(jnp.int32, (SG_HEADS * CHUNK, CHUNK), 0) % CHUNK
            s_i = lax.broadcasted_iota(jnp.int32, (SG_HEADS * CHUNK, CHUNK), 1)
            dw_ref[...] = jnp.where(s_i <= t_i, dw_ref[...], 0.0)
            lane = lax.broadcasted_iota(jnp.int32, (CHUNK, LANES), 1)
            acc = jnp.zeros((CHUNK, LANES), F32)
            for h in range(SG_HEADS):
                tot = jnp.sum(jnp.where(grp == h, dbias_ref[...], 0.0), axis=1, keepdims=True)
                acc = acc + jnp.where(lane == h, tot, 0.0)
            db_ref[...] = acc

        wait()

    h_ins = hosted.ins if hosted else []
    res = pl.pallas_call(
        body, name="sg_bwd_hosting" if hosted else "sg_bwd", grid=(nblk,),
        in_specs=_sg_specs(tt) + [pl.BlockSpec((tt, SG_WIDTH), lambda i: (i, 1)),
                                  pl.BlockSpec((SG_HEADS * CHUNK, CHUNK), lambda i: (0, 0)),
                                  pl.BlockSpec((SG_HEADS * CHUNK, CHUNK), lambda i: (0, 0)),
                                  pl.BlockSpec((CHUNK, SG_WIDTH), lambda i: (0, 0)),
                                  pl.BlockSpec((1, SG_WIDTH), lambda i: (0, 0))] + [HBM_SPEC] * len(h_ins),
        out_specs=[pl.BlockSpec((tt, SG_WIDTH), lambda i: (i, 0)), pl.BlockSpec((tt, SG_WIDTH), lambda i: (i, 0)),
                   pl.BlockSpec((SG_HEADS * CHUNK, CHUNK), lambda i: (0, 0)),
                   pl.BlockSpec((CHUNK, LANES), lambda i: (0, 0)), pl.BlockSpec((1, SG_WIDTH), lambda i: (0, 0))]
        + [HBM_SPEC] * len(h_ins),
        out_shape=[jax.ShapeDtypeStruct((T, SG_WIDTH), BF16), jax.ShapeDtypeStruct((T, SG_WIDTH), BF16),
                   jax.ShapeDtypeStruct((SG_HEADS * CHUNK, CHUNK), F32),
                   jax.ShapeDtypeStruct((CHUNK, LANES), F32), jax.ShapeDtypeStruct((1, SG_WIDTH), F32)]
        + (hosted.out_shapes if hosted else []),
        scratch_shapes=[pltpu.VMEM((tt, SG_WIDTH), F32), pltpu.VMEM((CHUNK, SG_WIDTH), F32)]
        + (hosted.sems() if hosted else []),
        compiler_params=_params(has_side_effects=hosted is not None),
    )(proj, proj, dymix, wm, wmt, bias, g, *h_ins)
    return res[:5], res[5:]


def _sb_logits(z):
    lb = jnp.minimum(z, 0.0) - jnp.log(1.0 + jnp.exp(-jnp.abs(z)))
    return lb, lb - z


ATTN_STRIP = 256
ATTN_SUBS = 4


def _by_strips(n_rows, fn):
    parts = None
    for r in range(0, n_rows, ATTN_STRIP):
        res = fn(slice(r, r + ATTN_STRIP))
        parts = [[v] for v in res] if parts is None else [p + [v] for p, v in zip(parts, res)]
    return [jnp.concatenate(p, axis=0) for p in parts]


def _attn_qkv_specs(tq, T):
    base = (IN_COLS - 3 * SB_WIDTH - QKV_OFF) // LANES
    nb = SB_WIDTH // LANES
    return [pl.BlockSpec((tq, LANES), lambda p, i: (i, base + p)),
            pl.BlockSpec((T, LANES), lambda p, i: (0, base + nb + p)),
            pl.BlockSpec((T, LANES), lambda p, i: (0, base + 2 * nb + p))]


class _Hosted:
    def __init__(self, ins, out_shapes, n_sems, copies, in_place=False):
        self.ins, self.out_shapes, self.n_sems, self.copies = ins, out_shapes, n_sems, copies
        self.in_place = in_place

    @property
    def n(self):
        return len(self.ins)

    def aliases(self, n_in, n_out):
        return {n_in + k: n_out + k for k in range(self.n)} if self.in_place else {}

    def sems(self):
        return [pltpu.SemaphoreType.DMA((self.n_sems,)), pltpu.SemaphoreType.DMA((self.n_sems,))]

    def start(self, src, dst, ssem, rsem):
        for send, _ in self.copies(src, dst, ssem, rsem):
            send.start()

    def wait(self, src, dst, ssem, rsem):
        for send, recv in self.copies(src, dst, ssem, rsem):
            recv.wait_recv()
            send.wait_send()


def _host(hosted, refs, n_in, n_out, first, last):
    if hosted is None:
        return refs, lambda: None, lambda: None
    n = hosted.n
    own_in, h_in = refs[:n_in], refs[n_in:n_in + n]
    own_out, h_out = refs[n_in + n:n_in + n + n_out], refs[n_in + n + n_out:n_in + 2 * n + n_out]
    rest = refs[n_in + 2 * n + n_out:]
    ssem, rsem = rest[-2:]

    def start():
        if first is True:
            hosted.start(h_in, h_out, ssem, rsem)
        else:
            pl.when(first)(lambda: hosted.start(h_in, h_out, ssem, rsem))

    def wait():
        if last is True:
            hosted.wait(h_in, h_out, ssem, rsem)
        else:
            pl.when(last)(lambda: hosted.wait(h_in, h_out, ssem, rsem))

    return own_in + own_out + rest[:-2], start, wait


def _attn_fwd(qkv, hosted=None):
    T = qkv.shape[0]
    tk = _tile(T, ATTN_TILE)
    n_sub = ATTN_SUBS if T % (ATTN_SUBS * tk) == 0 else 1
    tq = n_sub * tk
    n_p, nq = SB_WIDTH // LANES, T // tq

    def body(*refs):
        p, i = pl.program_id(0), pl.program_id(1)
        (q_ref, k_ref, v_ref, o_ref), start, wait = _host(
            hosted, refs, 3, 1, jnp.logical_and(p == 0, i == 0), jnp.logical_and(p == n_p - 1, i == nq - 1))
        start()
        lane = lax.broadcasted_iota(jnp.int32, (tk, LANES), 1)
        row = lax.broadcasted_iota(jnp.int32, (tk, tk), 0)
        col = lax.broadcasted_iota(jnp.int32, (tk, tk), 1)
        after = jnp.where(row > col, 1.0, 0.0).astype(BF16)
        valid = col < row
        qh = {}
        for sb in range(n_sub):
            q = q_ref[sb * tk:(sb + 1) * tk, :].astype(F32)
            for hh in range(2):
                qh[(sb, hh)] = jnp.where((lane // SB_HD) == hh, q * SB_SCALE, 0.0).astype(BF16)

        def tiles(todo, state):
            chains = [(n, hh) for n in range(len(todo)) for hh in range(2)]
            kv = []
            for _, j, _ in todo:
                ks = pl.ds(pl.multiple_of(j * tk, tk), tk)
                kv.append((k_ref[ks, :], v_ref[ks, :]))
            z = {(n, hh): _nt(qh[(todo[n][0], hh)], kv[n][0]) for n, hh in chains}
            lb, lmb, lm_sum = {}, {}, {}
            for n, hh in chains:
                def logits(rows, z=z[(n, hh)], mask=todo[n][2]):
                    lb, lm = _sb_logits(z[rows])
                    if mask is not None:
                        lm = jnp.where(mask[rows], lm, 0.0)
                    return lb, lm.astype(BF16), jnp.sum(lm, axis=1, keepdims=True)

                lb[(n, hh)], lmb[(n, hh)], lm_sum[(n, hh)] = _by_strips(tk, logits)
            x = {c: _nn(lmb[c], after) for c in chains}
            new = dict(state)
            for n, hh in chains:
                key = (todo[n][0], hh)
                carry, acc = new[key]

                def weights(rows, lb=lb[(n, hh)], x=x[(n, hh)], carry=carry, mask=todo[n][2]):
                    a = jnp.exp(lb[rows] + x[rows] + carry[rows])
                    if mask is not None:
                        a = jnp.where(mask[rows], a, 0.0)
                    return (a.astype(BF16),)

                (ab,) = _by_strips(tk, weights)
                new[key] = (carry + lm_sum[(n, hh)], acc + _nn(ab, kv[n][1]))
            return new

        def live(state, sb):
            return jnp.maximum(jnp.max(state[(sb, 0)][0]), jnp.max(state[(sb, 1)][0]))

        first = n_sub * i
        zero = (jnp.zeros((tk, 1), F32), jnp.zeros((tk, LANES), F32))
        todo = []
        for sb in range(n_sub):
            gate = jnp.broadcast_to(first > 0, (tk, tk)) if sb == 0 else None
            todo += [(sb, first + sb, valid), (sb, jnp.maximum(first + sb - 1, 0), gate)]
        state = tiles(todo, {(sb, hh): zero for sb in range(n_sub) for hh in range(2)})
        for sb in range(n_sub):
            def cond(st):
                return jnp.logical_and(st[0] >= 0, st[2] > UNDERFLOW)

            def step(st, sb=sb):
                mine = tiles([(sb, st[0], None)], st[1])
                return st[0] - 1, mine, live(mine, sb)

            mine = {k: v for k, v in state.items() if k[0] == sb}
            _, mine, _ = lax.while_loop(cond, step, (first + sb - 2, mine, live(mine, sb)))
            o_ref[sb * tk:(sb + 1) * tk, :] = jnp.where(lane < SB_HD, mine[(sb, 0)][1], mine[(sb, 1)][1])
        wait()

    h_ins = hosted.ins if hosted else []
    res = pl.pallas_call(
        body, name="attn_fwd_hosting" if hosted else "attn_fwd", grid=(n_p, nq),
        in_specs=_attn_qkv_specs(tq, T) + [HBM_SPEC] * len(h_ins),
        out_specs=[pl.BlockSpec((tq, LANES), lambda p, i: (i, p))] + [HBM_SPEC] * len(h_ins),
        out_shape=[jax.ShapeDtypeStruct((T, SB_WIDTH), F32)] + (hosted.out_shapes if hosted else []),
        input_output_aliases=hosted.aliases(3, 1) if hosted else {},
        scratch_shapes=hosted.sems() if hosted else [],
        compiler_params=_params(has_side_effects=hosted is not None),
    )(qkv, qkv, qkv, *h_ins)
    return res[0], res[1:]


def _attn_bwd(qkv, o, dymix, hosted=None):
    T = qkv.shape[0]
    tk = _tile(T, ATTN_TILE)
    n_sub = ATTN_SUBS if T % (ATTN_SUBS * tk) == 0 else 1
    tq = n_sub * tk
    n_p, nq = SB_WIDTH // LANES, T // tq
    yc_blk = (POOL_WIDTH + SG_WIDTH) // LANES

    def body(*refs):
        p, i = pl.program_id(0), pl.program_id(1)
        (q_ref, k_ref, v_ref, o_ref, do_ref, dq_ref, dk_ref, dv_ref), start, wait = _host(
            hosted, refs, 5, 3, jnp.logical_and(p == 0, i == 0), jnp.logical_and(p == n_p - 1, i == nq - 1))
        start()
        lane = lax.broadcasted_iota(jnp.int32, (tk, LANES), 1)
        row = lax.broadcasted_iota(jnp.int32, (tk, tk), 0)
        col = lax.broadcasted_iota(jnp.int32, (tk, tk), 1)
        after = jnp.where(row > col, 1.0, 0.0).astype(BF16)
        from_here = jnp.where(row >= col, 1.0, 0.0).astype(BF16)
        from_here2 = jnp.concatenate([from_here, from_here], axis=0)
        valid = col < row

        @pl.when(i == 0)
        def _():
            dk_ref[...] = jnp.zeros_like(dk_ref)
            dv_ref[...] = jnp.zeros_like(dv_ref)

        qh, dohb, delta = {}, {}, {}
        for sb in range(n_sub):
            rows = slice(sb * tk, (sb + 1) * tk)
            q, ov, dov = q_ref[rows, :].astype(F32), o_ref[rows, :], do_ref[rows, :]
            for hh in range(2):
                head = (lane // SB_HD) == hh
                qh[(sb, hh)] = jnp.where(head, q * SB_SCALE, 0.0).astype(BF16)
                dohb[(sb, hh)] = jnp.where(head, dov, 0.0).astype(BF16)
                delta[(sb, hh)] = jnp.sum(dohb[(sb, hh)].astype(F32) * ov, axis=1, keepdims=True)

        def tiles(todo, state):
            chains = [(n, hh) for n in range(len(todo)) for hh in range(2)]
            kv, where = [], []
            for _, j, _ in todo:
                ks = pl.ds(pl.multiple_of(j * tk, tk), tk)
                where.append(ks)
                kv.append((k_ref[ks, :], v_ref[ks, :]))
            z = {(n, hh): _nt(qh[(todo[n][0], hh)], kv[n][0]) for n, hh in chains}
            da = {(n, hh): _nt(dohb[(todo[n][0], hh)], kv[n][1]) for n, hh in chains}
            lb, lmb, lm_sum = {}, {}, {}
            for n, hh in chains:
                def logits(rows, z=z[(n, hh)], mask=todo[n][2]):
                    lb, lm = _sb_logits(z[rows])
                    if mask is not None:
                        lm = jnp.where(mask[rows], lm, 0.0)
                    return lb, lm.astype(BF16), jnp.sum(lm, axis=1, keepdims=True)

                lb[(n, hh)], lmb[(n, hh)], lm_sum[(n, hh)] = _by_strips(tk, logits)
            x = {c: _nn(lmb[c], after) for c in chains}
            c_a = {k: v[0] for k, v in state.items()}
            ab, g, g_split, g_sum = {}, {}, {}, {}
            for n, hh in chains:
                key = (todo[n][0], hh)

                def weights(rows, lb=lb[(n, hh)], x=x[(n, hh)], da=da[(n, hh)], c_a=c_a[key], mask=todo[n][2]):
                    a = jnp.exp(lb[rows] + x[rows] + c_a[rows])
                    if mask is not None:
                        a = jnp.where(mask[rows], a, 0.0)
                    ab = a.astype(BF16)
                    g = da[rows] * ab.astype(F32)
                    hi = g.astype(BF16)
                    lo = (g - hi.astype(F32)).astype(BF16)
                    return ab, g, jnp.concatenate([hi, lo], axis=1), jnp.sum(g, axis=1, keepdims=True)

                ab[(n, hh)], g[(n, hh)], g_split[(n, hh)], g_sum[(n, hh)] = _by_strips(tk, weights)
                c_a[key] = c_a[key] + lm_sum[(n, hh)]
            right = {c: _nn(g_split[c], from_here2) for c in chains}
            c_r = {k: v[1] for k, v in state.items()}
            dzb = {}
            for n, hh in chains:
                key = (todo[n][0], hh)

                def logit_grads(rows, lb=lb[(n, hh)], g=g[(n, hh)], right=right[(n, hh)], c_r=c_r[key],
                                delta=delta[key], mask=todo[n][2]):
                    sig = jnp.exp(lb[rows])
                    left = delta[rows] - (c_r[rows] + right[rows])
                    dz = g[rows] * (1.0 - sig) - left * sig
                    if mask is not None:
                        dz = jnp.where(mask[rows], dz, 0.0)
                    return (dz.astype(BF16),)

                (dzb[(n, hh)],) = _by_strips(tk, logit_grads)
                c_r[key] = c_r[key] + g_sum[(n, hh)]
            dqa = {k: v[2] for k, v in state.items()}
            for n in range(len(todo)):
                sb = todo[n][0]
                dk_ref[where[n], :] += _tn(dzb[(n, 0)], qh[(sb, 0)]) + _tn(dzb[(n, 1)], qh[(sb, 1)])
                dv_ref[where[n], :] += _tn(ab[(n, 0)], dohb[(sb, 0)]) + _tn(ab[(n, 1)], dohb[(sb, 1)])
                for hh in range(2):
                    dqa[(sb, hh)] = dqa[(sb, hh)] + _nn(dzb[(n, hh)], kv[n][0])
            return {k: (c_a[k], c_r[k], dqa[k]) for k in state}

        def live(state, sb):
            return jnp.maximum(jnp.max(state[(sb, 0)][0]), jnp.max(state[(sb, 1)][0]))

        first = n_sub * i
        zero = (jnp.zeros((tk, 1), F32), jnp.zeros((tk, 1), F32), jnp.zeros((tk, LANES), F32))
        todo = []
        for sb in range(n_sub):
            gate = jnp.broadcast_to(first > 0, (tk, tk)) if sb == 0 else None
            todo += [(sb, first + sb, valid), (sb, jnp.maximum(first + sb - 1, 0), gate)]
        state = tiles(todo, {(sb, hh): zero for sb in range(n_sub) for hh in range(2)})
        for sb in range(n_sub):
            def cond(st):
                return jnp.logical_and(st[0] >= 0, st[2] > UNDERFLOW)

            def step(st, sb=sb):
                mine = tiles([(sb, st[0], None)], st[1])
                return st[0] - 1, mine, live(mine, sb)

            mine = {k: v for k, v in state.items() if k[0] == sb}
            _, mine, _ = lax.while_loop(cond, step, (first + sb - 2, mine, live(mine, sb)))
            dq_ref[sb * tk:(sb + 1) * tk, :] = (
                jnp.where(lane < SB_HD, mine[(sb, 0)][2], mine[(sb, 1)][2]) * SB_SCALE).astype(BF16)
        wait()

    h_ins = hosted.ins if hosted else []
    res = pl.pallas_call(
        body, name="attn_bwd_hosting" if hosted else "attn_bwd", grid=(n_p, nq),
        in_specs=_attn_qkv_specs(tq, T) + [pl.BlockSpec((tq, LANES), lambda p, i: (i, p)),
                                           pl.BlockSpec((tq, LANES), lambda p, i: (i, yc_blk + p))]
        + [HBM_SPEC] * len(h_ins),
        out_specs=[pl.BlockSpec((tq, LANES), lambda p, i: (i, p)), pl.BlockSpec((T, LANES), lambda p, i: (0, p)),
                   pl.BlockSpec((T, LANES), lambda p, i: (0, p))] + [HBM_SPEC] * len(h_ins),
        out_shape=[jax.ShapeDtypeStruct((T, SB_WIDTH), BF16)] + [jax.ShapeDtypeStruct((T, SB_WIDTH), F32)] * 2
        + (hosted.out_shapes if hosted else []),
        scratch_shapes=hosted.sems() if hosted else [],
        compiler_params=_params(has_side_effects=hosted is not None),
    )(qkv, qkv, qkv, o, dymix, *h_ins)
    return res[:3], res[3:]


def _outproj_fwd(x, ya, yb, yc, w, hosted=None):
    T, D = x.shape
    tt = _tile(T, 1024)
    nt = T // tt

    def body(*refs):
        i = pl.program_id(0)
        (x_ref, ya_ref, yb_ref, yc_ref, w_ref, x1_ref, ymix_ref), start, wait = _host(
            hosted, refs, 5, 2, i == 0, i == nt - 1)
        start()
        ymix_ref[:, 0:POOL_WIDTH] = ya_ref[...].astype(BF16)
        ymix_ref[:, POOL_WIDTH:POOL_WIDTH + SG_WIDTH] = yb_ref[...].astype(BF16)
        ymix_ref[:, POOL_WIDTH + SG_WIDTH:] = yc_ref[...].astype(BF16)
        x1_ref[...] = x_ref[...] + _nn(ymix_ref[...], w_ref[...])
        wait()

    row = lambda width: pl.BlockSpec((tt, width), lambda i: (i, 0))
    h_ins = hosted.ins if hosted else []
    res = pl.pallas_call(
        body, name="outproj_fwd_hosting" if hosted else "outproj_fwd", grid=(nt,),
        in_specs=[row(D), row(POOL_WIDTH), row(SG_WIDTH), row(SB_WIDTH), pl.BlockSpec((D, D), lambda i: (0, 0))]
        + [HBM_SPEC] * len(h_ins),
        out_specs=[row(D), row(D)] + [HBM_SPEC] * len(h_ins),
        out_shape=[jax.ShapeDtypeStruct((T, D), F32), jax.ShapeDtypeStruct((T, D), BF16)]
        + (hosted.out_shapes if hosted else []),
        input_output_aliases=hosted.aliases(5, 2) if hosted else {},
        scratch_shapes=hosted.sems() if hosted else [],
        compiler_params=_params(has_side_effects=hosted is not None),
    )(x, ya, yb, yc, w, *h_ins)
    return res[:2], res[2:]


def _nt_matmul(a, w):
    T, N = a.shape
    K = w.shape[0]
    tt = _tile(T, 2048)

    def body(a_ref, w_ref, o_ref):
        o_ref[...] = _nt(a_ref[...].astype(BF16), w_ref[...])

    return pl.pallas_call(
        body, name="nt_matmul", grid=(T // tt,),
        in_specs=[pl.BlockSpec((tt, N), lambda i: (i, 0)), pl.BlockSpec((K, N), lambda i: (0, 0))],
        out_specs=pl.BlockSpec((tt, K), lambda i: (i, 0)),
        out_shape=jax.ShapeDtypeStruct((T, K), F32),
        compiler_params=_params(),
    )(a, w)


def _tn_matmul(a, b, name, n_split=1, hosted=None):
    T, K = a.shape
    N = b.shape[1]
    tk = _tile(K, 1024)
    tn = _tile(N // n_split, 1024)
    tt = _tile(T, 2048)
    nper = N // n_split // tn
    nk, nn, nt = K // tk, N // tn, T // tt

    def body(*refs):
        k, n, t = pl.program_id(0), pl.program_id(1), pl.program_id(2)
        (a_ref, b_ref, o_ref), start, wait = _host(
            hosted, refs, 2, 1, jnp.logical_and(jnp.logical_and(k == 0, n == 0), t == 0),
            jnp.logical_and(jnp.logical_and(k == nk - 1, n == nn - 1), t == nt - 1))
        start()

        @pl.when(t == 0)
        def _():
            o_ref[...] = jnp.zeros_like(o_ref)

        o_ref[...] += _tn(a_ref[...], b_ref[...].astype(BF16))
        wait()

    h_ins = hosted.ins if hosted else []
    res = pl.pallas_call(
        body, name=name + "_hosting" if hosted else name, grid=(nk, nn, nt),
        in_specs=[pl.BlockSpec((tt, tk), lambda k, n, t: (t, k)), pl.BlockSpec((tt, tn), lambda k, n, t: (t, n))]
        + [HBM_SPEC] * len(h_ins),
        out_specs=[pl.BlockSpec((None, tk, tn), lambda k, n, t: (n // nper, k, n % nper))] + [HBM_SPEC] * len(h_ins),
        out_shape=[jax.ShapeDtypeStruct((n_split, K, N // n_split), F32)] + (hosted.out_shapes if hosted else []),
        scratch_shapes=hosted.sems() if hosted else [],
        compiler_params=_params(has_side_effects=hosted is not None),
    )(a, b, *h_ins)
    return (res[0], res[1:]) if hosted else res[0]


def _mlp_fwd(x, g, w_up, w_down, hosted=None):
    T, D = x.shape
    n_blk, _, width = w_up.shape
    F = n_blk * width
    tt = _tile(T, 1024)
    fc = _tile(width, MLP_CHUNK)
    per = width // fc
    nc = F // fc
    nt = T // tt

    def body(*refs):
        i, c = pl.program_id(0), pl.program_id(1)
        (x_ref, g_ref, wu_ref, wd_ref, y_ref, h_ref, u_ref, a_ref), start, wait = _host(
            hosted, refs, 4, 4, jnp.logical_and(i == 0, c == 0), jnp.logical_and(i == nt - 1, c == nc - 1))
        start()

        @pl.when(c == 0)
        def _():
            xv = x_ref[...]
            h, _, _ = _rms_fwd(xv, g_ref[...])
            h_ref[...] = h.astype(BF16)
            y_ref[...] = xv

        r = jnp.maximum(_nn(h_ref[...], wu_ref[...]), 0.0)
        u_ref[...] = (2.0 * r).astype(BF16)
        a = jnp.square(r).astype(BF16)
        a_ref[...] = a
        y_ref[...] += _nn(a, wd_ref[...])
        wait()

    h_ins = hosted.ins if hosted else []
    res = pl.pallas_call(
        body, name="mlp_fwd_hosting" if hosted else "mlp_fwd", grid=(nt, nc),
        in_specs=[pl.BlockSpec((tt, D), lambda i, c: (i, 0)), pl.BlockSpec((1, D), lambda i, c: (0, 0)),
                  pl.BlockSpec((None, D, fc), lambda i, c: (c // per, 0, c % per)),
                  pl.BlockSpec((fc, D), lambda i, c: (c, 0))]
        + [HBM_SPEC] * len(h_ins),
        out_specs=[pl.BlockSpec((tt, D), lambda i, c: (i, 0)), pl.BlockSpec((tt, D), lambda i, c: (i, 0)),
                   pl.BlockSpec((tt, fc), lambda i, c: (i, c)), pl.BlockSpec((tt, fc), lambda i, c: (i, c))]
        + [HBM_SPEC] * len(h_ins),
        out_shape=[jax.ShapeDtypeStruct((T, D), F32), jax.ShapeDtypeStruct((T, D), BF16),
                   jax.ShapeDtypeStruct((T, F), BF16), jax.ShapeDtypeStruct((T, F), BF16)]
        + (hosted.out_shapes if hosted else []),
        scratch_shapes=hosted.sems() if hosted else [],
        compiler_params=_params(has_side_effects=hosted is not None),
    )(x, g, w_up, w_down, *h_ins)
    return res[:4], res[4:]


def _mlp_bwd(dy, x, g, u, w_up, w_down, hosted=None):
    T, D = x.shape
    n_blk, _, width = w_up.shape
    F = n_blk * width
    tt = _tile(T, 1024)
    fc = _tile(width, MLP_CHUNK)
    per = width // fc
    nc = F // fc
    nt = T // tt

    def body(*refs):
        i, c = pl.program_id(0), pl.program_id(1)
        (dy_ref, x_ref, g_ref, u_ref, wu_ref, wd_ref, dx_ref, du_ref, dg_ref, dyb_ref, dh_ref), start, wait = _host(
            hosted, refs, 6, 3, jnp.logical_and(i == 0, c == 0), jnp.logical_and(i == nt - 1, c == nc - 1))
        start()

        @pl.when(c == 0)
        def _():
            dyb_ref[...] = dy_ref[...].astype(BF16)
            dh_ref[...] = jnp.zeros_like(dh_ref)

        @pl.when(jnp.logical_and(i == 0, c == 0))
        def _():
            dg_ref[...] = jnp.zeros_like(dg_ref)

        da = _nt(dyb_ref[...], wd_ref[...])
        du = (da * u_ref[...].astype(F32)).astype(BF16)
        du_ref[...] = du
        dh_ref[...] += _nt(du, wu_ref[...])

        @pl.when(c == nc - 1)
        def _():
            gv = g_ref[...]
            _, xhat, r = _rms_fwd(x_ref[...], gv)
            dx, dgrow = _rms_bwd(dh_ref[...], xhat, r, gv)
            dx_ref[...] = dy_ref[...] + dx
            dg_ref[...] += jnp.sum(dgrow, axis=0, keepdims=True)

        wait()

    h_ins = hosted.ins if hosted else []
    res = pl.pallas_call(
        body, name="mlp_bwd_hosting" if hosted else "mlp_bwd", grid=(nt, nc),
        in_specs=[pl.BlockSpec((tt, D), lambda i, c: (i, 0)), pl.BlockSpec((tt, D), lambda i, c: (i, 0)),
                  pl.BlockSpec((1, D), lambda i, c: (0, 0)), pl.BlockSpec((tt, fc), lambda i, c: (i, c)),
                  pl.BlockSpec((None, D, fc), lambda i, c: (c // per, 0, c % per)),
                  pl.BlockSpec((fc, D), lambda i, c: (c, 0))]
        + [HBM_SPEC] * len(h_ins),
        out_specs=[pl.BlockSpec((tt, D), lambda i, c: (i, 0)), pl.BlockSpec((tt, fc), lambda i, c: (i, c)),
                   pl.BlockSpec((1, D), lambda i, c: (0, 0))] + [HBM_SPEC] * len(h_ins),
        out_shape=[jax.ShapeDtypeStruct((T, D), F32), jax.ShapeDtypeStruct((T, F), BF16),
                   jax.ShapeDtypeStruct((1, D), F32)] + (hosted.out_shapes if hosted else []),
        scratch_shapes=[pltpu.VMEM((tt, D), BF16), pltpu.VMEM((tt, D), F32)] + (hosted.sems() if hosted else []),
        compiler_params=_params(has_side_effects=hosted is not None),
    )(dy, x, g, u, w_up, w_down, *h_ins)
    return res[:3], res[3:]


def _loss_head(x, g, target):
    T, D = x.shape
    tt = _tile(T, 1024)

    def body(x_ref, g_ref, t_ref, loss_ref, dx_ref, dg_ref):
        gv = g_ref[...]
        y, xhat, r = _rms_fwd(x_ref[...], gv)
        err = y - t_ref[...]
        dx, dgrow = _rms_bwd(err * (1.0 / D), xhat, r, gv)
        dx_ref[...] = dx

        @pl.when(pl.program_id(0) == 0)
        def _():
            loss_ref[...] = jnp.zeros_like(loss_ref)
            dg_ref[...] = jnp.zeros_like(dg_ref)

        loss_ref[...] += 0.5 * jnp.sum(jnp.mean(err * err, axis=-1, keepdims=True), axis=0, keepdims=True)
        dg_ref[...] += jnp.sum(dgrow, axis=0, keepdims=True)

    return pl.pallas_call(
        body, name="loss_head", grid=(T // tt,),
        in_specs=[pl.BlockSpec((tt, D), lambda i: (i, 0)), pl.BlockSpec((1, D), lambda i: (0, 0)),
                  pl.BlockSpec((tt, D), lambda i: (i, 0))],
        out_specs=[pl.BlockSpec((1, LANES), lambda i: (0, 0)), pl.BlockSpec((tt, D), lambda i: (i, 0)),
                   pl.BlockSpec((1, D), lambda i: (0, 0))],
        out_shape=[jax.ShapeDtypeStruct((1, LANES), F32), jax.ShapeDtypeStruct((T, D), F32),
                   jax.ShapeDtypeStruct((1, D), F32)],
        compiler_params=_params(),
    )(x, g, target)


def _rows(shape, pref=512):
    last = shape[-1]
    rows = 1
    for s in shape[:-1]:
        rows *= s
    tr = rows
    if rows * last > 256 * 1024:
        for cand in (pref, 256, 128, 64, 32, 16, 8):
            if rows % cand == 0:
                tr = cand
                break
    return rows, last, tr


def _elementwise(fn, name, ins, n_out, out_dtype=F32):
    shape = ins[0].shape
    rows, last, tr = _rows(shape)
    flat = [a.reshape(rows, last) for a in ins]
    n_in = len(ins)

    def body(*refs):
        res = fn(*[r[...] for r in refs[:n_in]])
        if n_out == 1:
            res = (res,)
        for r, v in zip(refs[n_in:], res):
            r[...] = v.astype(r.dtype)

    spec = pl.BlockSpec((tr, last), lambda i: (i, 0))
    outs = pl.pallas_call(
        body, name=name, grid=(rows // tr,),
        in_specs=[spec] * n_in, out_specs=[spec] * n_out,
        out_shape=[jax.ShapeDtypeStruct((rows, last), out_dtype)] * n_out,
        compiler_params=_params(),
    )(*flat)
    return [o.reshape(shape) for o in outs]


def _add_pairs(gs, os, c_idx):
    n = len(gs)
    halves = [(g.shape[1] // 2, g.shape[2]) for g in gs]

    def body(c_ref, *refs):
        for a in range(n):
            refs[2 * n + a][...] = refs[2 * a][...] + refs[2 * a + 1][...]

    in_specs = []
    for h, C in halves:
        in_specs += [pl.BlockSpec((None, h, C), lambda q, c: (q, c[0], 0)), pl.BlockSpec((None, h, C), lambda q, c: (q, 0, 0))]
    return pl.pallas_call(
        body, name="add_pairs",
        grid_spec=pltpu.PrefetchScalarGridSpec(
            num_scalar_prefetch=1, grid=(N_CHIPS,), in_specs=in_specs,
            out_specs=[pl.BlockSpec((None, h, C), lambda q, c: (q, 0, 0)) for h, C in halves]),
        out_shape=[jax.ShapeDtypeStruct((N_CHIPS, h, C), F32) for h, C in halves],
        compiler_params=_params(),
    )(c_idx.astype(jnp.int32).reshape(1), *[x for pair in zip(gs, os) for x in pair])


def _add_chips(ps, rs, q_idx):
    n = len(ps)
    steps = 2
    blocks = [(p.shape[1] // steps, p.shape[2]) for p in ps]

    def body(q_ref, *refs):
        for a in range(n):
            p_ref, r0_ref, r1_ref, r2_ref = refs[4 * a:4 * a + 4]
            refs[4 * n + a][...] = (p_ref[...] + r0_ref[...]) + (r1_ref[...] + r2_ref[...])

    def arrived(tr, C, k):
        return pl.BlockSpec((None, tr, C), lambda i, q: (k, i, 0))

    in_specs, operands = [], []
    for (tr, C), p, r in zip(blocks, ps, rs):
        in_specs += [pl.BlockSpec((None, tr, C), lambda i, q: (q[0], i, 0)), arrived(tr, C, 0), arrived(tr, C, 1),
                     arrived(tr, C, 2)]
        operands += [p, r, r, r]
    return pl.pallas_call(
        body, name="add_chips",
        grid_spec=pltpu.PrefetchScalarGridSpec(
            num_scalar_prefetch=1, grid=(steps,), in_specs=in_specs,
            out_specs=[pl.BlockSpec((tr, C), lambda i, q: (i, 0)) for tr, C in blocks]),
        out_shape=[jax.ShapeDtypeStruct((p.shape[1], p.shape[2]), F32) for p in ps],
        compiler_params=_params(),
    )(q_idx.astype(jnp.int32).reshape(1), *operands)


def _adamw(w, g, m, v):
    m = ADAM_B1 * m + (1.0 - ADAM_B1) * g
    v = ADAM_B2 * v + (1.0 - ADAM_B2) * jnp.square(g)
    m_hat = m / (1.0 - ADAM_B1 ** ADAM_STEP)
    v_hat = v / (1.0 - ADAM_B2 ** ADAM_STEP)
    delta = -ADAM_LR * (m_hat / (jnp.sqrt(v_hat) + ADAM_EPS) + ADAM_WD * w)
    return delta, m, v


def _place():
    x, y, c = lax.axis_index("x"), lax.axis_index("y"), lax.axis_index("c")
    chips = [(1 - x, y), (x, 1 - y), (1 - x, 1 - y)]
    return x, y, c, chips


def _remote(src, dst, ssem, rsem, k, dev):
    return pltpu.make_async_remote_copy(src_ref=src, dst_ref=dst, send_sem=ssem.at[k], recv_sem=rsem.at[k],
                                        device_id=dev, device_id_type=MESH)


def _gather_weights(shards):
    n = len(shards)
    halves = [s.shape[1] // 2 for s in shards]

    def body(*refs):
        src, out = refs[:n], refs[n:2 * n]
        ssem, rsem = refs[2 * n:]
        x, y, c, chips = _place()
        me_q = 2 * x + y
        sib = (x, y, 1 - c)

        def half(a, q, cc):
            return out[a].at[q, :, pl.ds(cc * halves[a], halves[a]), :]

        first = []
        for a in range(n):
            mine = src[a].at[:, pl.ds(c * halves[a], halves[a]), :]
            for r, chip in enumerate(chips):
                first.append(_remote(mine, half(a, me_q, c), ssem, rsem, a * 3 + r, (*chip, c)))
        for cp in first:
            cp.start()
        passed = []
        for a in range(n):
            for r, chip in enumerate(chips):
                q = 2 * chip[0] + chip[1]
                k = a * 3 + r
                _remote(half(a, q, c), half(a, q, c), ssem, rsem, k, (*chip, c)).wait_recv()
                cp = _remote(half(a, q, c), half(a, q, c), ssem, rsem, 3 * n + k, sib)
                cp.start()
                passed.append(cp)
        for a in range(n):
            for r, chip in enumerate(chips):
                q = 2 * chip[0] + chip[1]
                _remote(half(a, q, 1 - c), half(a, q, 1 - c), ssem, rsem, 3 * n + a * 3 + r, sib).wait_recv()
        for cp in first + passed:
            cp.wait_send()

    return pl.pallas_call(
        body, name="gather_weights",
        in_specs=[HBM_SPEC] * n, out_specs=[HBM_SPEC] * n,
        out_shape=[jax.ShapeDtypeStruct((N_CHIPS,) + s.shape, s.dtype) for s in shards],
        scratch_shapes=[pltpu.SemaphoreType.DMA((6 * n,)), pltpu.SemaphoreType.DMA((6 * n,))],
        compiler_params=_params(has_side_effects=True),
    )(*shards)


def _gather_over_ici(shards):
    n = len(shards)
    halves = [s.shape[1] // 2 for s in shards]

    def copies(src, out, ssem, rsem):
        x, y, c, chips = _place()
        me_q = 2 * x + y
        res = []
        for a in range(n):
            rows = pl.ds(c * halves[a], halves[a])
            mine = src[a].at[:, rows, :]
            for r, chip in enumerate(chips):
                dev = (*chip, c)
                res.append((_remote(mine, out[a].at[me_q, :, rows, :], ssem, rsem, a * 3 + r, dev),
                            _remote(mine, out[a].at[2 * chip[0] + chip[1], :, rows, :], ssem, rsem, a * 3 + r, dev)))
        return res

    return _Hosted(list(shards), [jax.ShapeDtypeStruct((N_CHIPS,) + s.shape, s.dtype) for s in shards], 3 * n, copies)


def _pass_over_d2d(gathered):
    n = len(gathered)
    halves = [g.shape[2] // 2 for g in gathered]

    def copies(_, out, ssem, rsem):
        x, y, c, chips = _place()
        sib = (x, y, 1 - c)
        res = []
        for a in range(n):
            for r, chip in enumerate(chips):
                q = 2 * chip[0] + chip[1]
                mine = out[a].at[q, :, pl.ds(c * halves[a], halves[a]), :]
                theirs = out[a].at[q, :, pl.ds((1 - c) * halves[a], halves[a]), :]
                res.append((_remote(mine, mine, ssem, rsem, a * 3 + r, sib),
                            _remote(theirs, theirs, ssem, rsem, a * 3 + r, sib)))
        return res

    return _Hosted(list(gathered), [jax.ShapeDtypeStruct(g.shape, g.dtype) for g in gathered], 3 * n, copies,
                   in_place=True)


def _pass_to_sibling(gathered):
    n = len(gathered)
    halves = [g.shape[2] // 2 for g in gathered]

    def body(*refs):
        out = refs[n:2 * n]
        ssem, rsem = refs[2 * n:]
        x, y, c, chips = _place()
        sib = (x, y, 1 - c)

        def half(a, q, cc):
            return out[a].at[q, :, pl.ds(cc * halves[a], halves[a]), :]

        cps = []
        for a in range(n):
            for r, chip in enumerate(chips):
                q = 2 * chip[0] + chip[1]
                cps.append(_remote(half(a, q, c), half(a, q, c), ssem, rsem, a * 3 + r, sib))
        for cp in cps:
            cp.start()
        for a in range(n):
            for r, chip in enumerate(chips):
                q = 2 * chip[0] + chip[1]
                _remote(half(a, q, 1 - c), half(a, q, 1 - c), ssem, rsem, a * 3 + r, sib).wait_recv()
        for cp in cps:
            cp.wait_send()

    return pl.pallas_call(
        body, name="pass_to_sibling",
        in_specs=[HBM_SPEC] * n, out_specs=[HBM_SPEC] * n,
        out_shape=[jax.ShapeDtypeStruct(g.shape, g.dtype) for g in gathered],
        input_output_aliases={a: a for a in range(n)},
        scratch_shapes=[pltpu.SemaphoreType.DMA((3 * n,)), pltpu.SemaphoreType.DMA((3 * n,))],
        compiler_params=_params(has_side_effects=True),
    )(*gathered)


def _scatter_over_ici(parts):
    n = len(parts)

    def copies(src, out, ssem, rsem):
        x, y, c, chips = _place()
        res = []
        for a in range(n):
            for r, chip in enumerate(chips):
                cp = _remote(src[a].at[2 * chip[0] + chip[1]], out[a].at[r], ssem, rsem, a * 3 + r, (*chip, c))
                res.append((cp, cp))
        return res

    return _Hosted(list(parts), [jax.ShapeDtypeStruct((3,) + p.shape[1:], F32) for p in parts], 3 * n, copies)


def _swap_over_d2d(grads):
    n = len(grads)
    halves = [g.shape[1] // 2 for g in grads]

    def copies(src, out, ssem, rsem):
        x, y, c, _ = _place()
        res = []
        for a in range(n):
            cp = _remote(src[a].at[:, pl.ds((1 - c) * halves[a], halves[a]), :], out[a], ssem, rsem, a, (x, y, 1 - c))
            res.append((cp, cp))
        return res

    return _Hosted(list(grads), [jax.ShapeDtypeStruct((N_CHIPS, h, g.shape[2]), F32) for g, h in zip(grads, halves)],
                   n, copies)


def _swap_halves(grads):
    n = len(grads)
    halves = [g.shape[1] // 2 for g in grads]

    def body(*refs):
        src, out = refs[:n], refs[n:2 * n]
        ssem, rsem = refs[2 * n:]
        x, y, c, _ = _place()
        cps = [_remote(src[a].at[:, pl.ds((1 - c) * halves[a], halves[a]), :], out[a], ssem, rsem, a, (x, y, 1 - c))
               for a in range(n)]
        for cp in cps:
            cp.start()
        for cp in cps:
            cp.wait()

    return pl.pallas_call(
        body, name="swap_halves",
        in_specs=[HBM_SPEC] * n, out_specs=[HBM_SPEC] * n,
        out_shape=[jax.ShapeDtypeStruct((N_CHIPS, h, g.shape[2]), F32) for g, h in zip(grads, halves)],
        scratch_shapes=[pltpu.SemaphoreType.DMA((n,)), pltpu.SemaphoreType.DMA((n,))],
        compiler_params=_params(has_side_effects=True),
    )(*grads)


def _swap_reduced_over_d2d(reduced):
    n = len(reduced)

    def copies(src, out, ssem, rsem):
        x, y, c, _ = _place()
        res = []
        for a in range(n):
            cp = _remote(src[a], out[a], ssem, rsem, a, (x, y, 1 - c))
            res.append((cp, cp))
        return res

    return _Hosted(list(reduced), [jax.ShapeDtypeStruct(r.shape, F32) for r in reduced], n, copies)


def _swap_reduced(reduced):
    n = len(reduced)

    def body(*refs):
        src, out = refs[:n], refs[n:2 * n]
        ssem, rsem = refs[2 * n:]
        x, y, c, _ = _place()
        cps = [_remote(src[a], out[a], ssem, rsem, a, (x, y, 1 - c)) for a in range(n)]
        for cp in cps:
            cp.start()
        for cp in cps:
            cp.wait()

    return pl.pallas_call(
        body, name="swap_reduced",
        in_specs=[HBM_SPEC] * n, out_specs=[HBM_SPEC] * n,
        out_shape=[jax.ShapeDtypeStruct(r.shape, F32) for r in reduced],
        scratch_shapes=[pltpu.SemaphoreType.DMA((n,)), pltpu.SemaphoreType.DMA((n,))],
        compiler_params=_params(has_side_effects=True),
    )(*reduced)


def _allreduce_small(buf, hosted=None):
    R, L = buf.shape

    def body(*refs):
        (buf_ref, out_ref, pair_ref, chip_ref, ssem, rsem), start, wait = _host(hosted, refs, 1, 1, True, True)
        start()
        x, y, c, chips = _place()
        me_q = 2 * x + y
        pair_ref[c] = buf_ref[...]
        to_sib = _remote(buf_ref, pair_ref.at[c], ssem, rsem, 0, (x, y, 1 - c))
        to_sib.start()
        _remote(buf_ref, pair_ref.at[1 - c], ssem, rsem, 0, (x, y, 1 - c)).wait_recv()
        chip_ref[me_q] = pair_ref[0] + pair_ref[1]
        cps = [_remote(chip_ref.at[me_q], chip_ref.at[me_q], ssem, rsem, 1 + r, (*chip, c))
               for r, chip in enumerate(chips)]
        for cp in cps:
            cp.start()
        for r, chip in enumerate(chips):
            q = 2 * chip[0] + chip[1]
            _remote(chip_ref.at[q], chip_ref.at[q], ssem, rsem, 1 + r, (*chip, c)).wait_recv()
        out_ref[...] = (chip_ref[0] + chip_ref[1]) + (chip_ref[2] + chip_ref[3])
        to_sib.wait_send()
        for cp in cps:
            cp.wait_send()
        wait()

    h_ins = hosted.ins if hosted else []
    res = pl.pallas_call(
        body, name="allreduce_small",
        in_specs=[VMEM_SPEC] + [HBM_SPEC] * len(h_ins), out_specs=[VMEM_SPEC] + [HBM_SPEC] * len(h_ins),
        out_shape=[jax.ShapeDtypeStruct((R, L), F32)] + (hosted.out_shapes if hosted else []),
        scratch_shapes=[pltpu.VMEM((2, R, L), F32), pltpu.VMEM((N_CHIPS, R, L), F32),
                        pltpu.SemaphoreType.DMA((4,)), pltpu.SemaphoreType.DMA((4,))]
        + (hosted.sems() if hosted else []),
        compiler_params=_params(has_side_effects=True),
    )(buf, *h_ins)
    return res[0], res[1:]


def _pack(arrays):
    flat = jnp.concatenate([a.reshape(-1) for a in arrays])
    pad = (-flat.shape[0]) % (8 * LANES)
    return jnp.pad(flat, (0, pad)).reshape(-1, LANES)


def _unpack(buf, like):
    flat = buf.reshape(-1)
    out, off = [], 0
    for a in like:
        out.append(flat[off:off + a.size].reshape(a.shape))
        off += a.size
    return out


def _block_diag(pw):
    rows = []
    for gi in range(len(POOL_WINDOWS)):
        blocks = [pw[gi] if gj == gi else jnp.zeros_like(pw[gi]) for gj in range(len(POOL_WINDOWS))]
        rows.append(jnp.concatenate(blocks, axis=1))
    return jnp.concatenate(rows, axis=0)


def kernel(x, norm1, w_in, pool_w, pool_scale, sg_norm, sg_w, sg_b, w_out, norm2, w_up, w_down, final_norm, loss_target, m_norm1, m_w_in, m_pool_w, m_pool_scale, m_sg_norm, m_sg_w, m_sg_b, m_w_out, m_norm2, m_w_up, m_w_down, m_final_norm, v_norm1, v_w_in, v_pool_w, v_pool_scale, v_sg_norm, v_sg_w, v_sg_b, v_w_out, v_norm2, v_w_up, v_w_down, v_final_norm):
    depth = norm1.shape[0]
    T = x.shape[1]
    xs = x.reshape(T, D_MODEL)
    target = loss_target.reshape(T, D_MODEL)

    assert depth == 2
    c_idx = lax.axis_index("c")
    q_idx = 2 * lax.axis_index("x") + lax.axis_index("y")
    own = [w.astype(BF16) for w in (w_in, w_out, w_up, w_down)]
    gathered = {(0, 0): _gather_weights([own[0][:1]])[0]}

    def full(a, l, axis):
        blocks = lax.dynamic_update_slice(gathered[(a, l)], own[a][l][None, None], (q_idx, 0, 0, 0))[:, 0]
        if axis is None:
            return blocks
        if axis == 0:
            return blocks.reshape(-1, blocks.shape[-1])
        return jnp.concatenate([blocks[q] for q in range(N_CHIPS)], axis=axis)

    half_way = {}

    def gather_behind(call, keys, at_once):
        res, over_ici = call(_gather_over_ici([own[a][l:l + 1] for a, l in keys]))
        gathered.update(zip(keys[:at_once], _pass_to_sibling(over_ici[:at_once])))
        half_way.update(zip(keys[at_once:], over_ici[at_once:]))
        return res

    def pass_behind(call, keys):
        res, done = call(_pass_over_d2d([half_way.pop(k) for k in keys]))
        gathered.update(zip(keys, done))
        return res

    tril = jnp.tril(jnp.ones((CHUNK, CHUNK), F32))
    saved = []
    cur = xs
    wi, wo, wu, wd = {}, {}, {}, {}
    for l in range(depth):
        wbd = _block_diag(pool_w[l]).astype(BF16)
        wm = sg_w[l] * tril
        wm_s = wm.reshape(SG_HEADS * CHUNK, CHUNK).astype(BF16)
        wmt_s = jnp.swapaxes(wm, 1, 2).reshape(SG_HEADS * CHUNK, CHUNK).astype(BF16)
        bias = jnp.repeat(sg_b[l].T, SB_HD, axis=1)
        n1, n2 = norm1[l][None], norm2[l][None]
        psc, sgn = pool_scale[l][None], sg_norm[l][None]
        wi[l] = full(0, l, 1)
        proj, h, qkv = _inproj_fwd(cur, n1, wi[l])
        ya = _pool_fwd(proj, wbd, psc)
        yb = _sg_fwd(proj, wm_s, bias, sgn)
        if l == 0:
            yc = gather_behind(lambda hosted: _attn_fwd(qkv, hosted), [(1, 0), (2, 0), (3, 0)], 1)
            wo[l] = full(1, l, 0)
            x1, ymix = pass_behind(lambda hosted: _outproj_fwd(cur, ya, yb, yc, wo[l], hosted), [(2, 0), (3, 0)])
        else:
            yc = pass_behind(lambda hosted: _attn_fwd(qkv, hosted), [(1, l), (2, l), (3, l)])
            wo[l] = full(1, l, 0)
            (x1, ymix), _ = _outproj_fwd(cur, ya, yb, yc, wo[l])
        wu[l], wd[l] = full(2, l, None), full(3, l, 0)
        if l == 0:
            x2, h2, u, act = gather_behind(lambda hosted: _mlp_fwd(x1, n2, wu[l], wd[l], hosted),
                                           [(0, 1), (1, 1), (2, 1), (3, 1)], 1)
        else:
            (x2, h2, u, act), _ = _mlp_fwd(x1, n2, wu[l], wd[l])
        saved.append(dict(x0=cur, x1=x1, proj=proj, h=h, qkv=qkv, yc=yc, ymix=ymix, h2=h2, u=u, act=act,
                          wbd=wbd, wm_s=wm_s, wmt_s=wmt_s, bias=bias, n1=n1, n2=n2, psc=psc, sgn=sgn))
        cur = x2

    loss_row, dcur, d_final = _loss_head(cur, final_norm[None], target)

    small = [None] * depth
    grads, parts, reduced = {}, {}, {}

    def pair_up(keys, swapped):
        parts.update(zip(keys, _add_pairs([grads[k] for k in keys], swapped, c_idx)))

    def chip_up(keys, arrived):
        reduced.update(zip(keys, _add_chips([parts[k] for k in keys], arrived, q_idx)))

    for l in reversed(range(depth)):
        s = saved[l]
        if l == 0:
            keys = [(2, 1), (3, 1)]
            (dx1, du, d_n2), arrived = _mlp_bwd(dcur, s["x1"], s["n2"], s["u"], wu[l], wd[l],
                                                _scatter_over_ici([parts[k] for k in keys]))
            chip_up(keys, arrived)
        else:
            (dx1, du, d_n2), _ = _mlp_bwd(dcur, s["x1"], s["n2"], s["u"], wu[l], wd[l])
        if l == 0:
            keys = [(0, 1)]
            grads[(2, l)], arrived = _tn_matmul(s["h2"], du, "grad_w_up", n_split=N_CHIPS,
                                                hosted=_scatter_over_ici([parts[k] for k in keys]))
            chip_up(keys, arrived)
            keys = [(1, 1)]
            g_down, arrived = _tn_matmul(s["act"], dcur, "grad_w_down",
                                         hosted=_scatter_over_ici([parts[k] for k in keys]))
            chip_up(keys, arrived)
        else:
            grads[(2, l)] = _tn_matmul(s["h2"], du, "grad_w_up", n_split=N_CHIPS)
            g_down = _tn_matmul(s["act"], dcur, "grad_w_down")
        grads[(3, l)] = g_down[0].reshape(N_CHIPS, D_FF // N_CHIPS, D_MODEL)
        dymix = _nt_matmul(dx1, wo[l])
        grads[(1, l)] = _tn_matmul(s["ymix"], dx1, "grad_w_out")[0].reshape(N_CHIPS, D_MODEL // N_CHIPS, D_MODEL)
        da_in, d_wbd, d_psc = _pool_bwd(s["proj"], dymix, s["wbd"], s["psc"])
        if l == 0:
            keys = [(1, 0), (2, 0), (3, 0)]
            (du_pre, dv_pre, d_wm, d_bias, d_sgn), swapped = _sg_bwd(
                s["proj"], dymix, s["wm_s"], s["wmt_s"], s["bias"], s["sgn"], _swap_over_d2d([grads[k] for k in keys]))
            pair_up(keys, swapped)
            (dq, dk, dv), arrived = _attn_bwd(s["qkv"], s["yc"], dymix, _scatter_over_ici([parts[k] for k in keys]))
            chip_up(keys, arrived)
        else:
            (du_pre, dv_pre, d_wm, d_bias, d_sgn), _ = _sg_bwd(s["proj"], dymix, s["wm_s"], s["wmt_s"], s["bias"], s["sgn"])
            keys = [(1, l), (2, l), (3, l)]
            (dq, dk, dv), swapped = _attn_bwd(s["qkv"], s["yc"], dymix, _swap_over_d2d([grads[k] for k in keys]))
            pair_up(keys, swapped)
        pieces = [da_in, du_pre, dv_pre, dq, dk, dv]
        if l == 0:
            keys = sorted(reduced)
            g_in_l, swapped = _inproj_grad(s["h"], pieces, _swap_reduced_over_d2d([reduced[k] for k in keys]))
            theirs = dict(zip(keys, swapped))
        else:
            g_in_l, _ = _inproj_grad(s["h"], pieces)
        grads[(0, l)] = g_in_l[0].reshape(D_MODEL, N_CHIPS, IN_COLS // N_CHIPS).transpose(1, 0, 2)
        if l == 0:
            keys = [(0, 0)]
            pair_up(keys, _swap_halves([grads[k] for k in keys]))
            (dx0, d_n1), arrived = _inproj_bwd(pieces, wi[l], s["x0"], s["n1"], dx1,
                                               _scatter_over_ici([parts[k] for k in keys]))
            chip_up(keys, arrived)
        else:
            keys = [(0, l)]
            (dx0, d_n1), swapped = _inproj_bwd(pieces, wi[l], s["x0"], s["n1"], dx1,
                                               _swap_over_d2d([grads[k] for k in keys]))
            pair_up(keys, swapped)
        d_pw = jnp.stack([d_wbd[gi * POOL_GW:(gi + 1) * POOL_GW, gi * POOL_GW:(gi + 1) * POOL_GW]
                          for gi in range(len(POOL_WINDOWS))])
        small[l] = dict(norm1=d_n1[0], pool_w=d_pw, pool_scale=d_psc[0], sg_norm=d_sgn[0],
                        sg_w=d_wm.reshape(SG_HEADS, CHUNK, CHUNK), sg_b=d_bias[:, :SG_HEADS].T, norm2=d_n2[0])
        dcur = dx0
    grad_x = dcur.reshape(x.shape)

    names = ["norm1", "pool_w", "pool_scale", "sg_norm", "sg_w", "sg_b", "norm2"]
    slot = jnp.zeros((1,), F32)
    small_w = [norm1, pool_w, pool_scale, sg_norm, sg_w, sg_b, norm2, final_norm, slot]
    small_m = [m_norm1, m_pool_w, m_pool_scale, m_sg_norm, m_sg_w, m_sg_b, m_norm2, m_final_norm, slot]
    small_v = [v_norm1, v_pool_w, v_pool_scale, v_sg_norm, v_sg_w, v_sg_b, v_norm2, v_final_norm, slot]
    small_g = [jnp.stack([small[l][k] for l in range(depth)]) for k in names] + [d_final[0], loss_row[0, :1]]
    keys = [(0, 0)]
    g_packed, _ = _allreduce_small(_pack(small_g))
    theirs.update(zip(keys, _swap_reduced([reduced[k] for k in keys])))

    def joined(a):
        layers = []
        for l in range(depth):
            mine, other = reduced[(a, l)], theirs[(a, l)]
            layers.append(jnp.where(c_idx == 0, jnp.concatenate([mine, other]), jnp.concatenate([other, mine])))
        return jnp.stack(layers)

    gw_in, gw_out, gw_up, gw_down = [joined(a) for a in range(4)]

    loss = _unpack(g_packed, small_w)[-1][0]
    s_delta, s_m, s_v = _elementwise(_adamw, "adamw_small", [_pack(small_w), g_packed, _pack(small_m), _pack(small_v)], 3)
    gs = dict(zip(names + ["final_norm"], _unpack(g_packed, small_w)))
    ds = dict(zip(names + ["final_norm"], _unpack(s_delta, small_w)))
    ms = dict(zip(names + ["final_norm"], _unpack(s_m, small_w)))
    vs = dict(zip(names + ["final_norm"], _unpack(s_v, small_w)))

    big_g = dict(w_in=gw_in, w_out=gw_out, w_up=gw_up, w_down=gw_down)
    big_w = dict(w_in=(w_in, m_w_in, v_w_in), w_out=(w_out, m_w_out, v_w_out),
                 w_up=(w_up, m_w_up, v_w_up), w_down=(w_down, m_w_down, v_w_down))
    for k, (w, m, v) in big_w.items():
        operands = [w, big_g[k], m, v]
        if k == "w_in":
            operands = [jnp.swapaxes(o, 1, 2) for o in operands]
        ds[k], ms[k], vs[k] = _elementwise(_adamw, "adamw_" + k, operands, 3)
        if k == "w_in":
            ds[k], ms[k], vs[k] = [jnp.swapaxes(o, 1, 2) for o in (ds[k], ms[k], vs[k])]
        gs[k] = big_g[k]

    order = ["norm1", "w_in", "pool_w", "pool_scale", "sg_norm", "sg_w", "sg_b", "w_out", "norm2", "w_up", "w_down",
             "final_norm"]
    return (loss, grad_x, *[gs[k] for k in order], *[ds[k] for k in order], *[ms[k] for k in order],
            *[vs[k] for k in order])
```

```python
import jax
import jax.numpy as jnp
from jax import lax
from jax.experimental import pallas as pl
from jax.experimental.pallas import tpu as pltpu

F32 = jnp.float32
BF16 = jnp.bfloat16
MESH = pl.DeviceIdType.MESH

EPS = 1e-6
D_MODEL = 1024
POOL_WIDTH = 256
SG_WIDTH = 256
SB_WIDTH = 512
POOL_WINDOWS = (2, 4, 8, 16)
POOL_GW = 64
POOL_HALO = 16
CHUNK = 128
SG_HEADS = 4
SB_HD = 64
SB_SCALE = 0.125
IN_COLS = 2304
QKV_OFF = 768
D_FF = 4096
N_CHIPS = 4
LANES = 128
VMEM_LIMIT = 56 * 1024 * 1024
MLP_CHUNK = 512
ATTN_TILE = 256
UNDERFLOW = -104.0

ADAM_LR = 0.001
ADAM_B1 = 0.9
ADAM_B2 = 0.999
ADAM_EPS = 1e-08
ADAM_WD = 0.01
ADAM_STEP = 10

HBM_SPEC = pl.BlockSpec(memory_space=pl.ANY)
VMEM_SPEC = pl.BlockSpec(memory_space=pltpu.VMEM)


def _params(**kw):
    return pltpu.CompilerParams(vmem_limit_bytes=VMEM_LIMIT, **kw)


def _tile(n, pref):
    if n <= pref:
        return n
    for t in range(pref - pref % LANES, 0, -LANES):
        if n % t == 0:
            return t
    raise ValueError((n, pref))


def _nn(a, b):
    return jnp.dot(a, b, preferred_element_type=F32)


def _nt(a, b):
    return lax.dot_general(a, b, (((1,), (1,)), ((), ())), preferred_element_type=F32)


def _tn(a, b):
    return lax.dot_general(a, b, (((0,), (0,)), ((), ())), preferred_element_type=F32)


def _rms_fwd(x, g):
    r = lax.rsqrt(jnp.mean(x * x, axis=-1, keepdims=True) + EPS)
    xhat = x * r
    return xhat * g, xhat, r


def _rms_bwd(dy, xhat, r, g):
    dxhat = dy * g
    dx = r * (dxhat - xhat * jnp.mean(dxhat * xhat, axis=-1, keepdims=True))
    return dx, dy * xhat


_GELU_K = 0.7978845608028654
_GELU_C = 0.044715


def _gelu(x):
    return 0.5 * x * (1.0 + jnp.tanh(_GELU_K * (x + _GELU_C * x * x * x)))


def _gelu_and_grad(x):
    x2 = x * x
    t = jnp.tanh(_GELU_K * (x + _GELU_C * x2 * x))
    half = 0.5 * (1.0 + t)
    return x * half, half + 0.5 * x * (1.0 - t * t) * _GELU_K * (1.0 + 3.0 * _GELU_C * x2)


def _inproj_fwd(x, g, w):
    T, D = x.shape
    N = w.shape[1]
    tt = _tile(T, 1024)

    def body(x_ref, g_ref, w_ref, proj_ref, h_ref, qkv_ref):
        h, _, _ = _rms_fwd(x_ref[...], g_ref[...])
        hb = h.astype(BF16)
        h_ref[...] = hb
        p = _nn(hb, w_ref[...])
        proj_ref[...] = p[:, :QKV_OFF]
        qkv_ref[...] = p[:, QKV_OFF:].astype(BF16)

    return pl.pallas_call(
        body, name="inproj_fwd", grid=(T // tt,),
        in_specs=[pl.BlockSpec((tt, D), lambda i: (i, 0)), pl.BlockSpec((1, D), lambda i: (0, 0)),
                  pl.BlockSpec((D, N), lambda i: (0, 0))],
        out_specs=[pl.BlockSpec((tt, QKV_OFF), lambda i: (i, 0)), pl.BlockSpec((tt, D), lambda i: (i, 0)),
                   pl.BlockSpec((tt, N - QKV_OFF), lambda i: (i, 0))],
        out_shape=[jax.ShapeDtypeStruct((T, QKV_OFF), F32), jax.ShapeDtypeStruct((T, D), BF16),
                   jax.ShapeDtypeStruct((T, N - QKV_OFF), BF16)],
        compiler_params=_params(),
    )(x, g, w)


def _inproj_bwd(pieces, w, x, g, dres, hosted=None):
    T, D = x.shape
    N = w.shape[1]
    tt = _tile(T, 512)
    nt = T // tt
    widths = [p.shape[1] for p in pieces]
    offs = [sum(widths[:k]) for k in range(len(widths))]
    assert sum(widths) == N
    n_p = len(pieces)

    def body(*refs):
        i = pl.program_id(0)
        own, start, wait = _host(hosted, refs, n_p + 4, 2, i == 0, i == nt - 1)
        start()
        p_refs = own[:n_p]
        w_ref, x_ref, g_ref, dres_ref, dx_ref, dg_ref, dproj_ref = own[n_p:]
        for p_ref, o, wd in zip(p_refs, offs, widths):
            dproj_ref[:, o:o + wd] = p_ref[...].astype(BF16)
        dh = _nt(dproj_ref[...], w_ref[...])
        gv = g_ref[...]
        _, xhat, r = _rms_fwd(x_ref[...], gv)
        dx, dgrow = _rms_bwd(dh, xhat, r, gv)
        dx_ref[...] = dres_ref[...] + dx

        @pl.when(i == 0)
        def _():
            dg_ref[...] = jnp.zeros_like(dg_ref)

        dg_ref[...] += jnp.sum(dgrow, axis=0, keepdims=True)
        wait()

    h_ins = hosted.ins if hosted else []
    res = pl.pallas_call(
        body, name="inproj_bwd_hosting" if hosted else "inproj_bwd", grid=(nt,),
        in_specs=[pl.BlockSpec((tt, wd), lambda i: (i, 0)) for wd in widths] + [
            pl.BlockSpec((D, N), lambda i: (0, 0)), pl.BlockSpec((tt, D), lambda i: (i, 0)),
            pl.BlockSpec((1, D), lambda i: (0, 0)), pl.BlockSpec((tt, D), lambda i: (i, 0))] + [HBM_SPEC] * len(h_ins),
        out_specs=[pl.BlockSpec((tt, D), lambda i: (i, 0)), pl.BlockSpec((1, D), lambda i: (0, 0))]
        + [HBM_SPEC] * len(h_ins),
        out_shape=[jax.ShapeDtypeStruct((T, D), F32), jax.ShapeDtypeStruct((1, D), F32)]
        + (hosted.out_shapes if hosted else []),
        scratch_shapes=[pltpu.VMEM((tt, N), BF16)] + (hosted.sems() if hosted else []),
        compiler_params=_params(has_side_effects=hosted is not None),
    )(*pieces, w, x, g, dres, *h_ins)
    return res[:2], res[2:]


def _inproj_grad(h, pieces, hosted=None):
    T, D = h.shape
    tt = _tile(T, 1024)
    nt = T // tt
    widths = [p.shape[1] for p in pieces]
    offs = [sum(widths[:k]) for k in range(len(widths))]
    N = sum(widths)
    n_p = len(pieces)

    def body(*refs):
        t = pl.program_id(0)
        own, start, wait = _host(hosted, refs, n_p + 1, 1, t == 0, t == nt - 1)
        start()
        h_ref, p_refs, o_ref = own[0], own[1:1 + n_p], own[1 + n_p]

        @pl.when(t == 0)
        def _():
            o_ref[...] = jnp.zeros_like(o_ref)

        hv = h_ref[...]
        for p_ref, o, wd in zip(p_refs, offs, widths):
            o_ref[:, o:o + wd] += _tn(hv, p_ref[...].astype(BF16))
        wait()

    h_ins = hosted.ins if hosted else []
    res = pl.pallas_call(
        body, name="grad_w_in_hosting" if hosted else "grad_w_in", grid=(nt,),
        in_specs=[pl.BlockSpec((tt, D), lambda t: (t, 0))] + [pl.BlockSpec((tt, wd), lambda t: (t, 0)) for wd in widths]
        + [HBM_SPEC] * len(h_ins),
        out_specs=[pl.BlockSpec((None, D, N), lambda t: (0, 0, 0))] + [HBM_SPEC] * len(h_ins),
        out_shape=[jax.ShapeDtypeStruct((1, D, N), F32)] + (hosted.out_shapes if hosted else []),
        scratch_shapes=hosted.sems() if hosted else [],
        compiler_params=_params(has_side_effects=hosted is not None),
    )(h, *pieces, *h_ins)
    return res[0], res[1:]


def _pool_select(s2, s4, s8, s16, grp):
    return jnp.where(grp == 0, s2, jnp.where(grp == 1, s4, jnp.where(grp == 2, s8, s16)))


def _pool_count(t_glob, grp):
    win = jnp.where(grp == 0, 2, jnp.where(grp == 1, 4, jnp.where(grp == 2, 8, 16)))
    return jnp.minimum(t_glob + 1, win).astype(F32)


def _pool_diff(a, halo, base, tt):
    n = tt + POOL_HALO
    ext = jnp.concatenate([halo, a], axis=0)
    s2 = ext + pltpu.roll(ext, 1, 0)
    s4 = s2 + pltpu.roll(s2, 2, 0)
    s8 = s4 + pltpu.roll(s4, 4, 0)
    s16 = s8 + pltpu.roll(s8, 8, 0)
    grp = lax.broadcasted_iota(jnp.int32, (n, POOL_WIDTH), 1) // POOL_GW
    t_glob = lax.broadcasted_iota(jnp.int32, (n, POOL_WIDTH), 0) + (base - POOL_HALO)
    pooled = _pool_select(s2, s4, s8, s16, grp) / _pool_count(t_glob, grp)
    return pooled[POOL_HALO:] - a


def _pool_specs(T, tt):
    hb = tt // POOL_HALO
    return [pl.BlockSpec((tt, POOL_WIDTH), lambda i: (i, 0)),
            pl.BlockSpec((POOL_HALO, POOL_WIDTH), lambda i: (jnp.maximum(i * hb - 1, 0), 0))]


def _pool_fwd(proj, wbd, scale):
    T = proj.shape[0]
    tt = _tile(T, 1024)

    def body(a_ref, halo_ref, w_ref, sc_ref, y_ref):
        i = pl.program_id(0)
        halo = jnp.where(i > 0, halo_ref[...], 0.0)
        d = _pool_diff(a_ref[...], halo, i * tt, tt)
        y_ref[...] = _nn(d.astype(BF16), w_ref[...]) * sc_ref[...]

    return pl.pallas_call(
        body, name="pool_fwd", grid=(T // tt,),
        in_specs=_pool_specs(T, tt) + [pl.BlockSpec((POOL_WIDTH, POOL_WIDTH), lambda i: (0, 0)),
                                       pl.BlockSpec((1, POOL_WIDTH), lambda i: (0, 0))],
        out_specs=pl.BlockSpec((tt, POOL_WIDTH), lambda i: (i, 0)),
        out_shape=jax.ShapeDtypeStruct((T, POOL_WIDTH), F32),
        compiler_params=_params(),
    )(proj, proj, wbd, scale)


def _pool_bwd(proj, dymix, wbd, scale):
    T = proj.shape[0]
    tt = _tile(T, 1024)
    hb = tt // POOL_HALO
    nblk = T // tt
    n = tt + POOL_HALO

    def body(a_ref, halo_ref, dy_ref, dyn_ref, w_ref, sc_ref, da_ref, dw_ref, dsc_ref):
        i = pl.program_id(0)
        halo = jnp.where(i > 0, halo_ref[...], 0.0)
        d = _pool_diff(a_ref[...], halo, i * tt, tt)
        db = d.astype(BF16)
        wv = w_ref[...]
        sc = sc_ref[...]
        dy = dy_ref[...]
        dys = dy * sc

        @pl.when(i == 0)
        def _():
            dw_ref[...] = jnp.zeros_like(dw_ref)
            dsc_ref[...] = jnp.zeros_like(dsc_ref)

        dsc_ref[...] += jnp.sum(dy * _nn(db, wv), axis=0, keepdims=True)
        dw_ref[...] += _tn(db, dys.astype(BF16))
        dyn = jnp.where(i < nblk - 1, dyn_ref[...], 0.0) * sc
        dd = _nt(jnp.concatenate([dys, dyn], axis=0).astype(BF16), wv)
        grp = lax.broadcasted_iota(jnp.int32, (n, POOL_WIDTH), 1) // POOL_GW
        t_glob = lax.broadcasted_iota(jnp.int32, (n, POOL_WIDTH), 0) + i * tt
        e = dd / _pool_count(t_glob, grp)
        r2 = e + pltpu.roll(e, n - 1, 0)
        r4 = r2 + pltpu.roll(r2, n - 2, 0)
        r8 = r4 + pltpu.roll(r4, n - 4, 0)
        r16 = r8 + pltpu.roll(r8, n - 8, 0)
        da_ref[...] = (_pool_select(r2, r4, r8, r16, grp) - dd)[:tt].astype(BF16)

    return pl.pallas_call(
        body, name="pool_bwd", grid=(nblk,),
        in_specs=_pool_specs(T, tt) + [
            pl.BlockSpec((tt, POOL_WIDTH), lambda i: (i, 0)),
            pl.BlockSpec((POOL_HALO, POOL_WIDTH), lambda i: (jnp.minimum((i + 1) * hb, T // POOL_HALO - 1), 0)),
            pl.BlockSpec((POOL_WIDTH, POOL_WIDTH), lambda i: (0, 0)), pl.BlockSpec((1, POOL_WIDTH), lambda i: (0, 0))],
        out_specs=[pl.BlockSpec((tt, POOL_WIDTH), lambda i: (i, 0)),
                   pl.BlockSpec((POOL_WIDTH, POOL_WIDTH), lambda i: (0, 0)),
                   pl.BlockSpec((1, POOL_WIDTH), lambda i: (0, 0))],
        out_shape=[jax.ShapeDtypeStruct((T, POOL_WIDTH), BF16),
                   jax.ShapeDtypeStruct((POOL_WIDTH, POOL_WIDTH), F32),
                   jax.ShapeDtypeStruct((1, POOL_WIDTH), F32)],
        compiler_params=_params(),
    )(proj, proj, dymix, dymix, wbd, scale)


def _head_select(stacked, grp):
    out = jnp.where(grp == 0, stacked[0:CHUNK], 0.0)
    for h in range(1, SG_HEADS):
        out = out + jnp.where(grp == h, stacked[h * CHUNK:(h + 1) * CHUNK], 0.0)
    return out


def _sg_specs(tt):
    return [pl.BlockSpec((tt, SG_WIDTH), lambda i: (i, 1)), pl.BlockSpec((tt, SG_WIDTH), lambda i: (i, 2))]


def _sg_fwd(proj, wm, bias, g):
    T = proj.shape[0]
    tt = _tile(T, 1024)

    def body(u_ref, v_ref, wm_ref, b_ref, g_ref, y_ref):
        zu = _gelu(u_ref[...])
        vn, _, _ = _rms_fwd(_gelu(v_ref[...]), g_ref[...])
        grp = lax.broadcasted_iota(jnp.int32, (CHUNK, SG_WIDTH), 1) // SB_HD
        for n in range(tt // CHUNK):
            rows = slice(n * CHUNK, (n + 1) * CHUNK)
            sv = _head_select(_nn(wm_ref[...], vn[rows].astype(BF16)), grp) + b_ref[...]
            y_ref[rows, :] = zu[rows] * sv

    return pl.pallas_call(
        body, name="sg_fwd", grid=(T // tt,),
        in_specs=_sg_specs(tt) + [pl.BlockSpec((SG_HEADS * CHUNK, CHUNK), lambda i: (0, 0)),
                                  pl.BlockSpec((CHUNK, SG_WIDTH), lambda i: (0, 0)),
                                  pl.BlockSpec((1, SG_WIDTH), lambda i: (0, 0))],
        out_specs=pl.BlockSpec((tt, SG_WIDTH), lambda i: (i, 0)),
        out_shape=jax.ShapeDtypeStruct((T, SG_WIDTH), F32),
        compiler_params=_params(),
    )(proj, proj, wm, bias, g)


def _sg_bwd(proj, dymix, wm, wmt, bias, g, hosted=None):
    T = proj.shape[0]
    tt = _tile(T, 1024)
    nblk = T // tt

    def body(*refs):
        i = pl.program_id(0)
        (u_ref, v_ref, dy_ref, wm_ref, wmt_ref, b_ref, g_ref, du_ref, dv_ref, dw_ref, db_ref, dg_ref,
         dvn_ref, dbias_ref), start, wait = _host(hosted, refs, 7, 5, i == 0, i == nblk - 1)
        start()
        up, vp = u_ref[...], v_ref[...]
        gv = g_ref[...]
        (zu, gu), (zv, gvp) = _gelu_and_grad(up), _gelu_and_grad(vp)
        vn, xhat, r = _rms_fwd(zv, gv)
        grp = lax.broadcasted_iota(jnp.int32, (CHUNK, SG_WIDTH), 1) // SB_HD

        @pl.when(i == 0)
        def _():
            dw_ref[...] = jnp.zeros_like(dw_ref)
            dbias_ref[...] = jnp.zeros_like(dbias_ref)
            dg_ref[...] = jnp.zeros_like(dg_ref)

        for n in range(tt // CHUNK):
            rows = slice(n * CHUNK, (n + 1) * CHUNK)
            vc = vn[rows].astype(BF16)
            sv = _head_select(_nn(wm_ref[...], vc), grp) + b_ref[...]
            dy = dy_ref[rows, :]
            du_ref[rows, :] = (dy * sv * gu[rows]).astype(BF16)
            dsv = dy * zu[rows]
            dsvb = dsv.astype(BF16)
            dvn_ref[rows, :] = _head_select(_nn(wmt_ref[...], dsvb), grp)
            stacked = jnp.concatenate([jnp.where(grp == h, dsv, 0.0) for h in range(SG_HEADS)], axis=0)
            dw_ref[...] += _nt(stacked.astype(BF16), vc)
            dbias_ref[...] += dsv

        dzv, dgrow = _rms_bwd(dvn_ref[...], xhat, r, gv)
        dg_ref[...] += jnp.sum(dgrow, axis=0, keepdims=True)
        dv_ref[...] = (dzv * gvp).astype(BF16)

        @pl.when(i == nblk - 1)
        def _():
            t_i = lax.broadcasted_iota(jnp.int32, (SG_HEADS * CHUNK, CHUNK), 0) % CHUNK
            s_i = lax.broadcasted_iota(jnp.int32, (SG_HEADS * CHUNK, CHUNK), 1)
            dw_ref[...] = jnp.where(s_i <= t_i, dw_ref[...], 0.0)
            lane = lax.broadcasted_iota(jnp.int32, (CHUNK, LANES), 1)
            acc = jnp.zeros((CHUNK, LANES), F32)
            for h in range(SG_HEADS):
                tot = jnp.sum(jnp.where(grp == h, dbias_ref[...], 0.0), axis=1, keepdims=True)
                acc = acc + jnp.where(lane == h, tot, 0.0)
            db_ref[...] = acc

        wait()

    h_ins = hosted.ins if hosted else []
    res = pl.pallas_call(
        body, name="sg_bwd_hosting" if hosted else "sg_bwd", grid=(nblk,),
        in_specs=_sg_specs(tt) + [pl.BlockSpec((tt, SG_WIDTH), lambda i: (i, 1)),
                                  pl.BlockSpec((SG_HEADS * CHUNK, CHUNK), lambda i: (0, 0)),
                                  pl.BlockSpec((SG_HEADS * CHUNK, CHUNK), lambda i: (0, 0)),
                                  pl.BlockSpec((CHUNK, SG_WIDTH), lambda i: (0, 0)),
                                  pl.BlockSpec((1, SG_WIDTH), lambda i: (0, 0))] + [HBM_SPEC] * len(h_ins),
        out_specs=[pl.BlockSpec((tt, SG_WIDTH), lambda i: (i, 0)), pl.BlockSpec((tt, SG_WIDTH), lambda i: (i, 0)),
                   pl.BlockSpec((SG_HEADS * CHUNK, CHUNK), lambda i: (0, 0)),
                   pl.BlockSpec((CHUNK, LANES), lambda i: (0, 0)), pl.BlockSpec((1, SG_WIDTH), lambda i: (0, 0))]
        + [HBM_SPEC] * len(h_ins),
        out_shape=[jax.ShapeDtypeStruct((T, SG_WIDTH), BF16), jax.ShapeDtypeStruct((T, SG_WIDTH), BF16),
                   jax.ShapeDtypeStruct((SG_HEADS * CHUNK, CHUNK), F32),
                   jax.ShapeDtypeStruct((CHUNK, LANES), F32), jax.ShapeDtypeStruct((1, SG_WIDTH), F32)]
        + (hosted.out_shapes if hosted else []),
        scratch_shapes=[pltpu.VMEM((tt, SG_WIDTH), F32), pltpu.VMEM((CHUNK, SG_WIDTH), F32)]
        + (hosted.sems() if hosted else []),
        compiler_params=_params(has_side_effects=hosted is not None),
    )(proj, proj, dymix, wm, wmt, bias, g, *h_ins)
    return res[:5], res[5:]


def _sb_logits(z):
    lb = jnp.minimum(z, 0.0) - jnp.log(1.0 + jnp.exp(-jnp.abs(z)))
    return lb, lb - z


ATTN_STRIP = 256
ATTN_SUBS = 4


def _by_strips(n_rows, fn):
    parts = None
    for r in range(0, n_rows, ATTN_STRIP):
        res = fn(slice(r, r + ATTN_STRIP))
        parts = [[v] for v in res] if parts is None else [p + [v] for p, v in zip(parts, res)]
    return [jnp.concatenate(p, axis=0) for p in parts]


def _attn_qkv_specs(tq, T):
    base = (IN_COLS - 3 * SB_WIDTH - QKV_OFF) // LANES
    nb = SB_WIDTH // LANES
    return [pl.BlockSpec((tq, LANES), lambda p, i: (i, base + p)),
            pl.BlockSpec((T, LANES), lambda p, i: (0, base + nb + p)),
            pl.BlockSpec((T, LANES), lambda p, i: (0, base + 2 * nb + p))]


class _Hosted:
    def __init__(self, ins, out_shapes, n_sems, copies, in_place=False):
        self.ins, self.out_shapes, self.n_sems, self.copies = ins, out_shapes, n_sems, copies
        self.in_place = in_place

    @property
    def n(self):
        return len(self.ins)

    def aliases(self, n_in, n_out):
        return {n_in + k: n_out + k for k in range(self.n)} if self.in_place else {}

    def sems(self):
        return [pltpu.SemaphoreType.DMA((self.n_sems,)), pltpu.SemaphoreType.DMA((self.n_sems,))]

    def start(self, src, dst, ssem, rsem):
        for send, _ in self.copies(src, dst, ssem, rsem):
            send.start()

    def wait(self, src, dst, ssem, rsem):
        for send, recv in self.copies(src, dst, ssem, rsem):
            recv.wait_recv()
            send.wait_send()


def _host(hosted, refs, n_in, n_out, first, last):
    if hosted is None:
        return refs, lambda: None, lambda: None
    n = hosted.n
    own_in, h_in = refs[:n_in], refs[n_in:n_in + n]
    own_out, h_out = refs[n_in + n:n_in + n + n_out], refs[n_in + n + n_out:n_in + 2 * n + n_out]
    rest = refs[n_in + 2 * n + n_out:]
    ssem, rsem = rest[-2:]

    def start():
        if first is True:
            hosted.start(h_in, h_out, ssem, rsem)
        else:
            pl.when(first)(lambda: hosted.start(h_in, h_out, ssem, rsem))

    def wait():
        if last is True:
            hosted.wait(h_in, h_out, ssem, rsem)
        else:
            pl.when(last)(lambda: hosted.wait(h_in, h_out, ssem, rsem))

    return own_in + own_out + rest[:-2], start, wait


def _attn_fwd(qkv, hosted=None):
    T = qkv.shape[0]
    tk = _tile(T, ATTN_TILE)
    n_sub = ATTN_SUBS if T % (ATTN_SUBS * tk) == 0 else 1
    tq = n_sub * tk
    n_p, nq = SB_WIDTH // LANES, T // tq

    def body(*refs):
        p, i = pl.program_id(0), pl.program_id(1)
        (q_ref, k_ref, v_ref, o_ref), start, wait = _host(
            hosted, refs, 3, 1, jnp.logical_and(p == 0, i == 0), jnp.logical_and(p == n_p - 1, i == nq - 1))
        start()
        lane = lax.broadcasted_iota(jnp.int32, (tk, LANES), 1)
        row = lax.broadcasted_iota(jnp.int32, (tk, tk), 0)
        col = lax.broadcasted_iota(jnp.int32, (tk, tk), 1)
        after = jnp.where(row > col, 1.0, 0.0).astype(BF16)
        valid = col < row
        qh = {}
        for sb in range(n_sub):
            q = q_ref[sb * tk:(sb + 1) * tk, :].astype(F32)
            for hh in range(2):
                qh[(sb, hh)] = jnp.where((lane // SB_HD) == hh, q * SB_SCALE, 0.0).astype(BF16)

        def tiles(todo, state):
            chains = [(n, hh) for n in range(len(todo)) for hh in range(2)]
            kv = []
            for _, j, _ in todo:
                ks = pl.ds(pl.multiple_of(j * tk, tk), tk)
                kv.append((k_ref[ks, :], v_ref[ks, :]))
            z = {(n, hh): _nt(qh[(todo[n][0], hh)], kv[n][0]) for n, hh in chains}
            lb, lmb, lm_sum = {}, {}, {}
            for n, hh in chains:
                def logits(rows, z=z[(n, hh)], mask=todo[n][2]):
                    lb, lm = _sb_logits(z[rows])
                    if mask is not None:
                        lm = jnp.where(mask[rows], lm, 0.0)
                    return lb, lm.astype(BF16), jnp.sum(lm, axis=1, keepdims=True)

                lb[(n, hh)], lmb[(n, hh)], lm_sum[(n, hh)] = _by_strips(tk, logits)
            x = {c: _nn(lmb[c], after) for c in chains}
            new = dict(state)
            for n, hh in chains:
                key = (todo[n][0], hh)
                carry, acc = new[key]

                def weights(rows, lb=lb[(n, hh)], x=x[(n, hh)], carry=carry, mask=todo[n][2]):
                    a = jnp.exp(lb[rows] + x[rows] + carry[rows])
                    if mask is not None:
                        a = jnp.where(mask[rows], a, 0.0)
                    return (a.astype(BF16),)

                (ab,) = _by_strips(tk, weights)
                new[key] = (carry + lm_sum[(n, hh)], acc + _nn(ab, kv[n][1]))
            return new

        def live(state, sb):
            return jnp.maximum(jnp.max(state[(sb, 0)][0]), jnp.max(state[(sb, 1)][0]))

        first = n_sub * i
        zero = (jnp.zeros((tk, 1), F32), jnp.zeros((tk, LANES), F32))
        todo = []
        for sb in range(n_sub):
            gate = jnp.broadcast_to(first > 0, (tk, tk)) if sb == 0 else None
            todo += [(sb, first + sb, valid), (sb, jnp.maximum(first + sb - 1, 0), gate)]
        state = tiles(todo, {(sb, hh): zero for sb in range(n_sub) for hh in range(2)})
        for sb in range(n_sub):
            def cond(st):
                return jnp.logical_and(st[0] >= 0, st[2] > UNDERFLOW)

            def step(st, sb=sb):
                mine = tiles([(sb, st[0], None)], st[1])
                return st[0] - 1, mine, live(mine, sb)

            mine = {k: v for k, v in state.items() if k[0] == sb}
            _, mine, _ = lax.while_loop(cond, step, (first + sb - 2, mine, live(mine, sb)))
            o_ref[sb * tk:(sb + 1) * tk, :] = jnp.where(lane < SB_HD, mine[(sb, 0)][1], mine[(sb, 1)][1])
        wait()

    h_ins = hosted.ins if hosted else []
    res = pl.pallas_call(
        body, name="attn_fwd_hosting" if hosted else "attn_fwd", grid=(n_p, nq),
        in_specs=_attn_qkv_specs(tq, T) + [HBM_SPEC] * len(h_ins),
        out_specs=[pl.BlockSpec((tq, LANES), lambda p, i: (i, p))] + [HBM_SPEC] * len(h_ins),
        out_shape=[jax.ShapeDtypeStruct((T, SB_WIDTH), F32)] + (hosted.out_shapes if hosted else []),
        input_output_aliases=hosted.aliases(3, 1) if hosted else {},
        scratch_shapes=hosted.sems() if hosted else [],
        compiler_params=_params(has_side_effects=hosted is not None),
    )(qkv, qkv, qkv, *h_ins)
    return res[0], res[1:]


def _attn_bwd(qkv, o, dymix, hosted=None):
    T = qkv.shape[0]
    tk = _tile(T, ATTN_TILE)
    n_sub = ATTN_SUBS if T % (ATTN_SUBS * tk) == 0 else 1
    tq = n_sub * tk
    n_p, nq = SB_WIDTH // LANES, T // tq
    yc_blk = (POOL_WIDTH + SG_WIDTH) // LANES

    def body(*refs):
        p, i = pl.program_id(0), pl.program_id(1)
        (q_ref, k_ref, v_ref, o_ref, do_ref, dq_ref, dk_ref, dv_ref), start, wait = _host(
            hosted, refs, 5, 3, jnp.logical_and(p == 0, i == 0), jnp.logical_and(p == n_p - 1, i == nq - 1))
        start()
        lane = lax.broadcasted_iota(jnp.int32, (tk, LANES), 1)
        row = lax.broadcasted_iota(jnp.int32, (tk, tk), 0)
        col = lax.broadcasted_iota(jnp.int32, (tk, tk), 1)
        after = jnp.where(row > col, 1.0, 0.0).astype(BF16)
        from_here = jnp.where(row >= col, 1.0, 0.0).astype(BF16)
        from_here2 = jnp.concatenate([from_here, from_here], axis=0)
        valid = col < row

        @pl.when(i == 0)
        def _():
            dk_ref[...] = jnp.zeros_like(dk_ref)
            dv_ref[...] = jnp.zeros_like(dv_ref)

        qh, dohb, delta = {}, {}, {}
        for sb in range(n_sub):
            rows = slice(sb * tk, (sb + 1) * tk)
            q, ov, dov = q_ref[rows, :].astype(F32), o_ref[rows, :], do_ref[rows, :]
            for hh in range(2):
                head = (lane // SB_HD) == hh
                qh[(sb, hh)] = jnp.where(head, q * SB_SCALE, 0.0).astype(BF16)
                dohb[(sb, hh)] = jnp.where(head, dov, 0.0).astype(BF16)
                delta[(sb, hh)] = jnp.sum(dohb[(sb, hh)].astype(F32) * ov, axis=1, keepdims=True)

        def tiles(todo, state):
            chains = [(n, hh) for n in range(len(todo)) for hh in range(2)]
            kv, where = [], []
            for _, j, _ in todo:
                ks = pl.ds(pl.multiple_of(j * tk, tk), tk)
                where.append(ks)
                kv.append((k_ref[ks, :], v_ref[ks, :]))
            z = {(n, hh): _nt(qh[(todo[n][0], hh)], kv[n][0]) for n, hh in chains}
            da = {(n, hh): _nt(dohb[(todo[n][0], hh)], kv[n][1]) for n, hh in chains}
            lb, lmb, lm_sum = {}, {}, {}
            for n, hh in chains:
                def logits(rows, z=z[(n, hh)], mask=todo[n][2]):
                    lb, lm = _sb_logits(z[rows])
                    if mask is not None:
                        lm = jnp.where(mask[rows], lm, 0.0)
                    return lb, lm.astype(BF16), jnp.sum(lm, axis=1, keepdims=True)

                lb[(n, hh)], lmb[(n, hh)], lm_sum[(n, hh)] = _by_strips(tk, logits)
            x = {c: _nn(lmb[c], after) for c in chains}
            c_a = {k: v[0] for k, v in state.items()}
            ab, g, g_split, g_sum = {}, {}, {}, {}
            for n, hh in chains:
                key = (todo[n][0], hh)

                def weights(rows, lb=lb[(n, hh)], x=x[(n, hh)], da=da[(n, hh)], c_a=c_a[key], mask=todo[n][2]):
                    a = jnp.exp(lb[rows] + x[rows] + c_a[rows])
                    if mask is not None:
                        a = jnp.where(mask[rows], a, 0.0)
                    ab = a.astype(BF16)
                    g = da[rows] * ab.astype(F32)
                    hi = g.astype(BF16)
                    lo = (g - hi.astype(F32)).astype(BF16)
                    return ab, g, jnp.concatenate([hi, lo], axis=1), jnp.sum(g, axis=1, keepdims=True)

                ab[(n, hh)], g[(n, hh)], g_split[(n, hh)], g_sum[(n, hh)] = _by_strips(tk, weights)
                c_a[key] = c_a[key] + lm_sum[(n, hh)]
            right = {c: _nn(g_split[c], from_here2) for c in chains}
            c_r = {k: v[1] for k, v in state.items()}
            dzb = {}
            for n, hh in chains:
                key = (todo[n][0], hh)

                def logit_grads(rows, lb=lb[(n, hh)], g=g[(n, hh)], right=right[(n, hh)], c_r=c_r[key],
                                delta=delta[key], mask=todo[n][2]):
                    sig = jnp.exp(lb[rows])
                    left = delta[rows] - (c_r[rows] + right[rows])
                    dz = g[rows] * (1.0 - sig) - left * sig
                    if mask is not None:
                        dz = jnp.where(mask[rows], dz, 0.0)
                    return (dz.astype(BF16),)

                (dzb[(n, hh)],) = _by_strips(tk, logit_grads)
                c_r[key] = c_r[key] + g_sum[(n, hh)]
            dqa = {k: v[2] for k, v in state.items()}
            for n in range(len(todo)):
                sb = todo[n][0]
                dk_ref[where[n], :] += _tn(dzb[(n, 0)], qh[(sb, 0)]) + _tn(dzb[(n, 1)], qh[(sb, 1)])
                dv_ref[where[n], :] += _tn(ab[(n, 0)], dohb[(sb, 0)]) + _tn(ab[(n, 1)], dohb[(sb, 1)])
                for hh in range(2):
                    dqa[(sb, hh)] = dqa[(sb, hh)] + _nn(dzb[(n, hh)], kv[n][0])
            return {k: (c_a[k], c_r[k], dqa[k]) for k in state}

        def live(state, sb):
            return jnp.maximum(jnp.max(state[(sb, 0)][0]), jnp.max(state[(sb, 1)][0]))

        first = n_sub * i
        zero = (jnp.zeros((tk, 1), F32), jnp.zeros((tk, 1), F32), jnp.zeros((tk, LANES), F32))
        todo = []
        for sb in range(n_sub):
            gate = jnp.broadcast_to(first > 0, (tk, tk)) if sb == 0 else None
            todo += [(sb, first + sb, valid), (sb, jnp.maximum(first + sb - 1, 0), gate)]
        state = tiles(todo, {(sb, hh): zero for sb in range(n_sub) for hh in range(2)})
        for sb in range(n_sub):
            def cond(st):
                return jnp.logical_and(st[0] >= 0, st[2] > UNDERFLOW)

            def step(st, sb=sb):
                mine = tiles([(sb, st[0], None)], st[1])
                return st[0] - 1, mine, live(mine, sb)

            mine = {k: v for k, v in state.items() if k[0] == sb}
            _, mine, _ = lax.while_loop(cond, step, (first + sb - 2, mine, live(mine, sb)))
            dq_ref[sb * tk:(sb + 1) * tk, :] = (
                jnp.where(lane < SB_HD, mine[(sb, 0)][2], mine[(sb, 1)][2]) * SB_SCALE).astype(BF16)
        wait()

    h_ins = hosted.ins if hosted else []
    res = pl.pallas_call(
        body, name="attn_bwd_hosting" if hosted else "attn_bwd", grid=(n_p, nq),
        in_specs=_attn_qkv_specs(tq, T) + [pl.BlockSpec((tq, LANES), lambda p, i: (i, p)),
                                           pl.BlockSpec((tq, LANES), lambda p, i: (i, yc_blk + p))]
        + [HBM_SPEC] * len(h_ins),
        out_specs=[pl.BlockSpec((tq, LANES), lambda p, i: (i, p)), pl.BlockSpec((T, LANES), lambda p, i: (0, p)),
                   pl.BlockSpec((T, LANES), lambda p, i: (0, p))] + [HBM_SPEC] * len(h_ins),
        out_shape=[jax.ShapeDtypeStruct((T, SB_WIDTH), BF16)] + [jax.ShapeDtypeStruct((T, SB_WIDTH), F32)] * 2
        + (hosted.out_shapes if hosted else []),
        scratch_shapes=hosted.sems() if hosted else [],
        compiler_params=_params(has_side_effects=hosted is not None),
    )(qkv, qkv, qkv, o, dymix, *h_ins)
    return res[:3], res[3:]


def _outproj_fwd(x, ya, yb, yc, w, hosted=None):
    T, D = x.shape
    tt = _tile(T, 1024)
    nt = T // tt

    def body(*refs):
        i = pl.program_id(0)
        (x_ref, ya_ref, yb_ref, yc_ref, w_ref, x1_ref, ymix_ref), start, wait = _host(
            hosted, refs, 5, 2, i == 0, i == nt - 1)
        start()
        ymix_ref[:, 0:POOL_WIDTH] = ya_ref[...].astype(BF16)
        ymix_ref[:, POOL_WIDTH:POOL_WIDTH + SG_WIDTH] = yb_ref[...].astype(BF16)
        ymix_ref[:, POOL_WIDTH + SG_WIDTH:] = yc_ref[...].astype(BF16)
        x1_ref[...] = x_ref[...] + _nn(ymix_ref[...], w_ref[...])
        wait()

    row = lambda width: pl.BlockSpec((tt, width), lambda i: (i, 0))
    h_ins = hosted.ins if hosted else []
    res = pl.pallas_call(
        body, name="outproj_fwd_hosting" if hosted else "outproj_fwd", grid=(nt,),
        in_specs=[row(D), row(POOL_WIDTH), row(SG_WIDTH), row(SB_WIDTH), pl.BlockSpec((D, D), lambda i: (0, 0))]
        + [HBM_SPEC] * len(h_ins),
        out_specs=[row(D), row(D)] + [HBM_SPEC] * len(h_ins),
        out_shape=[jax.ShapeDtypeStruct((T, D), F32), jax.ShapeDtypeStruct((T, D), BF16)]
        + (hosted.out_shapes if hosted else []),
        input_output_aliases=hosted.aliases(5, 2) if hosted else {},
        scratch_shapes=hosted.sems() if hosted else [],
        compiler_params=_params(has_side_effects=hosted is not None),
    )(x, ya, yb, yc, w, *h_ins)
    return res[:2], res[2:]


def _nt_matmul(a, w):
    T, N = a.shape
    K = w.shape[0]
    tt = _tile(T, 2048)

    def body(a_ref, w_ref, o_ref):
        o_ref[...] = _nt(a_ref[...].astype(BF16), w_ref[...])

    return pl.pallas_call(
        body, name="nt_matmul", grid=(T // tt,),
        in_specs=[pl.BlockSpec((tt, N), lambda i: (i, 0)), pl.BlockSpec((K, N), lambda i: (0, 0))],
        out_specs=pl.BlockSpec((tt, K), lambda i: (i, 0)),
        out_shape=jax.ShapeDtypeStruct((T, K), F32),
        compiler_params=_params(),
    )(a, w)


def _tn_matmul(a, b, name, n_split=1, hosted=None):
    T, K = a.shape
    N = b.shape[1]
    tk = _tile(K, 1024)
    tn = _tile(N // n_split, 1024)
    tt = _tile(T, 2048)
    nper = N // n_split // tn
    nk, nn, nt = K // tk, N // tn, T // tt

    def body(*refs):
        k, n, t = pl.program_id(0), pl.program_id(1), pl.program_id(2)
        (a_ref, b_ref, o_ref), start, wait = _host(
            hosted, refs, 2, 1, jnp.logical_and(jnp.logical_and(k == 0, n == 0), t == 0),
            jnp.logical_and(jnp.logical_and(k == nk - 1, n == nn - 1), t == nt - 1))
        start()

        @pl.when(t == 0)
        def _():
            o_ref[...] = jnp.zeros_like(o_ref)

        o_ref[...] += _tn(a_ref[...], b_ref[...].astype(BF16))
        wait()

    h_ins = hosted.ins if hosted else []
    res = pl.pallas_call(
        body, name=name + "_hosting" if hosted else name, grid=(nk, nn, nt),
        in_specs=[pl.BlockSpec((tt, tk), lambda k, n, t: (t, k)), pl.BlockSpec((tt, tn), lambda k, n, t: (t, n))]
        + [HBM_SPEC] * len(h_ins),
        out_specs=[pl.BlockSpec((None, tk, tn), lambda k, n, t: (n // nper, k, n % nper))] + [HBM_SPEC] * len(h_ins),
        out_shape=[jax.ShapeDtypeStruct((n_split, K, N // n_split), F32)] + (hosted.out_shapes if hosted else []),
        scratch_shapes=hosted.sems() if hosted else [],
        compiler_params=_params(has_side_effects=hosted is not None),
    )(a, b, *h_ins)
    return (res[0], res[1:]) if hosted else res[0]


def _mlp_fwd(x, g, w_up, w_down, hosted=None):
    T, D = x.shape
    n_blk, _, width = w_up.shape
    F = n_blk * width
    tt = _tile(T, 1024)
    fc = _tile(width, MLP_CHUNK)
    per = width // fc
    nc = F // fc
    nt = T // tt

    def body(*refs):
        i, c = pl.program_id(0), pl.program_id(1)
        (x_ref, g_ref, wu_ref, wd_ref, y_ref, h_ref, u_ref, a_ref), start, wait = _host(
            hosted, refs, 4, 4, jnp.logical_and(i == 0, c == 0), jnp.logical_and(i == nt - 1, c == nc - 1))
        start()

        @pl.when(c == 0)
        def _():
            xv = x_ref[...]
            h, _, _ = _rms_fwd(xv, g_ref[...])
            h_ref[...] = h.astype(BF16)
            y_ref[...] = xv

        r = jnp.maximum(_nn(h_ref[...], wu_ref[...]), 0.0)
        u_ref[...] = (2.0 * r).astype(BF16)
        a = jnp.square(r).astype(BF16)
        a_ref[...] = a
        y_ref[...] += _nn(a, wd_ref[...])
        wait()

    h_ins = hosted.ins if hosted else []
    res = pl.pallas_call(
        body, name="mlp_fwd_hosting" if hosted else "mlp_fwd", grid=(nt, nc),
        in_specs=[pl.BlockSpec((tt, D), lambda i, c: (i, 0)), pl.BlockSpec((1, D), lambda i, c: (0, 0)),
                  pl.BlockSpec((None, D, fc), lambda i, c: (c // per, 0, c % per)),
                  pl.BlockSpec((fc, D), lambda i, c: (c, 0))]
        + [HBM_SPEC] * len(h_ins),
        out_specs=[pl.BlockSpec((tt, D), lambda i, c: (i, 0)), pl.BlockSpec((tt, D), lambda i, c: (i, 0)),
                   pl.BlockSpec((tt, fc), lambda i, c: (i, c)), pl.BlockSpec((tt, fc), lambda i, c: (i, c))]
        + [HBM_SPEC] * len(h_ins),
        out_shape=[jax.ShapeDtypeStruct((T, D), F32), jax.ShapeDtypeStruct((T, D), BF16),
                   jax.ShapeDtypeStruct((T, F), BF16), jax.ShapeDtypeStruct((T, F), BF16)]
        + (hosted.out_shapes if hosted else []),
        scratch_shapes=hosted.sems() if hosted else [],
        compiler_params=_params(has_side_effects=hosted is not None),
    )(x, g, w_up, w_down, *h_ins)
    return res[:4], res[4:]


def _mlp_bwd(dy, x, g, u, w_up, w_down, hosted=None):
    T, D = x.shape
    n_blk, _, width = w_up.shape
    F = n_blk * width
    tt = _tile(T, 1024)
    fc = _tile(width, MLP_CHUNK)
    per = width // fc
    nc = F // fc
    nt = T // tt

    def body(*refs):
        i, c = pl.program_id(0), pl.program_id(1)
        (dy_ref, x_ref, g_ref, u_ref, wu_ref, wd_ref, dx_ref, du_ref, dg_ref, dxb_ref, dyb_ref, dh_ref), start, wait = _host(
            hosted, refs, 6, 4, jnp.logical_and(i == 0, c == 0), jnp.logical_and(i == nt - 1, c == nc - 1))
        start()

        @pl.when(c == 0)
        def _():
            dyb_ref[...] = dy_ref[...].astype(BF16)
            dh_ref[...] = jnp.zeros_like(dh_ref)

        @pl.when(jnp.logical_and(i == 0, c == 0))
        def _():
            dg_ref[...] = jnp.zeros_like(dg_ref)

        da = _nt(dyb_ref[...], wd_ref[...])
        du = (da * u_ref[...].astype(F32)).astype(BF16)
        du_ref[...] = du
        dh_ref[...] += _nt(du, wu_ref[...])

        @pl.when(c == nc - 1)
        def _():
            gv = g_ref[...]
            _, xhat, r = _rms_fwd(x_ref[...], gv)
            dx, dgrow = _rms_bwd(dh_ref[...], xhat, r, gv)
            dx_new = dy_ref[...] + dx
            dx_ref[...] = dx_new
            dxb_ref[...] = dx_new.astype(BF16)
            dg_ref[...] += jnp.sum(dgrow, axis=0, keepdims=True)

        wait()

    h_ins = hosted.ins if hosted else []
    res = pl.pallas_call(
        body, name="mlp_bwd_hosting" if hosted else "mlp_bwd", grid=(nt, nc),
        in_specs=[pl.BlockSpec((tt, D), lambda i, c: (i, 0)), pl.BlockSpec((tt, D), lambda i, c: (i, 0)),
                  pl.BlockSpec((1, D), lambda i, c: (0, 0)), pl.BlockSpec((tt, fc), lambda i, c: (i, c)),
                  pl.BlockSpec((None, D, fc), lambda i, c: (c // per, 0, c % per)),
                  pl.BlockSpec((fc, D), lambda i, c: (c, 0))]
        + [HBM_SPEC] * len(h_ins),
        out_specs=[pl.BlockSpec((tt, D), lambda i, c: (i, 0)), pl.BlockSpec((tt, fc), lambda i, c: (i, c)),
                   pl.BlockSpec((1, D), lambda i, c: (0, 0)), pl.BlockSpec((tt, D), lambda i, c: (i, 0))]
        + [HBM_SPEC] * len(h_ins),
        out_shape=[jax.ShapeDtypeStruct((T, D), F32), jax.ShapeDtypeStruct((T, F), BF16),
                   jax.ShapeDtypeStruct((1, D), F32), jax.ShapeDtypeStruct((T, D), BF16)]
        + (hosted.out_shapes if hosted else []),
        scratch_shapes=[pltpu.VMEM((tt, D), BF16), pltpu.VMEM((tt, D), F32)] + (hosted.sems() if hosted else []),
        compiler_params=_params(has_side_effects=hosted is not None),
    )(dy, x, g, u, w_up, w_down, *h_ins)
    return res[:4], res[4:]


def _loss_head(x, g, target):
    T, D = x.shape
    tt = _tile(T, 1024)

    def body(x_ref, g_ref, t_ref, loss_ref, dx_ref, dg_ref):
        gv = g_ref[...]
        y, xhat, r = _rms_fwd(x_ref[...], gv)
        err = y - t_ref[...]
        dx, dgrow = _rms_bwd(err * (1.0 / D), xhat, r, gv)
        dx_ref[...] = dx

        @pl.when(pl.program_id(0) == 0)
        def _():
            loss_ref[...] = jnp.zeros_like(loss_ref)
            dg_ref[...] = jnp.zeros_like(dg_ref)

        loss_ref[...] += 0.5 * jnp.sum(jnp.mean(err * err, axis=-1, keepdims=True), axis=0, keepdims=True)
        dg_ref[...] += jnp.sum(dgrow, axis=0, keepdims=True)

    return pl.pallas_call(
        body, name="loss_head", grid=(T // tt,),
        in_specs=[pl.BlockSpec((tt, D), lambda i: (i, 0)), pl.BlockSpec((1, D), lambda i: (0, 0)),
                  pl.BlockSpec((tt, D), lambda i: (i, 0))],
        out_specs=[pl.BlockSpec((1, LANES), lambda i: (0, 0)), pl.BlockSpec((tt, D), lambda i: (i, 0)),
                   pl.BlockSpec((1, D), lambda i: (0, 0))],
        out_shape=[jax.ShapeDtypeStruct((1, LANES), F32), jax.ShapeDtypeStruct((T, D), F32),
                   jax.ShapeDtypeStruct((1, D), F32)],
        compiler_params=_params(),
    )(x, g, target)


def _rows(shape, pref=512):
    last = shape[-1]
    rows = 1
    for s in shape[:-1]:
        rows *= s
    tr = rows
    if rows * last > 256 * 1024:
        for cand in (pref, 256, 128, 64, 32, 16, 8):
            if rows % cand == 0:
                tr = cand
                break
    return rows, last, tr


def _elementwise(fn, name, ins, n_out, out_dtype=F32):
    shape = ins[0].shape
    rows, last, tr = _rows(shape)
    flat = [a.reshape(rows, last) for a in ins]
    n_in = len(ins)

    def body(*refs):
        res = fn(*[r[...] for r in refs[:n_in]])
        if n_out == 1:
            res = (res,)
        for r, v in zip(refs[n_in:], res):
            r[...] = v.astype(r.dtype)

    spec = pl.BlockSpec((tr, last), lambda i: (i, 0))
    outs = pl.pallas_call(
        body, name=name, grid=(rows // tr,),
        in_specs=[spec] * n_in, out_specs=[spec] * n_out,
        out_shape=[jax.ShapeDtypeStruct((rows, last), out_dtype)] * n_out,
        compiler_params=_params(),
    )(*flat)
    return [o.reshape(shape) for o in outs]


def _add_pairs(gs, os, c_idx):
    n = len(gs)
    halves = [(g.shape[1] // 2, g.shape[2]) for g in gs]

    def body(c_ref, *refs):
        for a in range(n):
            refs[2 * n + a][...] = refs[2 * a][...] + refs[2 * a + 1][...]

    in_specs = []
    for h, C in halves:
        in_specs += [pl.BlockSpec((None, h, C), lambda q, c: (q, c[0], 0)), pl.BlockSpec((None, h, C), lambda q, c: (q, 0, 0))]
    return pl.pallas_call(
        body, name="add_pairs",
        grid_spec=pltpu.PrefetchScalarGridSpec(
            num_scalar_prefetch=1, grid=(N_CHIPS,), in_specs=in_specs,
            out_specs=[pl.BlockSpec((None, h, C), lambda q, c: (q, 0, 0)) for h, C in halves]),
        out_shape=[jax.ShapeDtypeStruct((N_CHIPS, h, C), F32) for h, C in halves],
        compiler_params=_params(),
    )(c_idx.astype(jnp.int32).reshape(1), *[x for pair in zip(gs, os) for x in pair])


def _add_chips(ps, rs, q_idx):
    n = len(ps)
    steps = 2
    blocks = [(p.shape[1] // steps, p.shape[2]) for p in ps]

    def body(q_ref, *refs):
        for a in range(n):
            p_ref, r0_ref, r1_ref, r2_ref = refs[4 * a:4 * a + 4]
            refs[4 * n + a][...] = (p_ref[...] + r0_ref[...]) + (r1_ref[...] + r2_ref[...])

    def arrived(tr, C, k):
        return pl.BlockSpec((None, tr, C), lambda i, q: (k, i, 0))

    in_specs, operands = [], []
    for (tr, C), p, r in zip(blocks, ps, rs):
        in_specs += [pl.BlockSpec((None, tr, C), lambda i, q: (q[0], i, 0)), arrived(tr, C, 0), arrived(tr, C, 1),
                     arrived(tr, C, 2)]
        operands += [p, r, r, r]
    return pl.pallas_call(
        body, name="add_chips",
        grid_spec=pltpu.PrefetchScalarGridSpec(
            num_scalar_prefetch=1, grid=(steps,), in_specs=in_specs,
            out_specs=[pl.BlockSpec((tr, C), lambda i, q: (i, 0)) for tr, C in blocks]),
        out_shape=[jax.ShapeDtypeStruct((p.shape[1], p.shape[2]), F32) for p in ps],
        compiler_params=_params(),
    )(q_idx.astype(jnp.int32).reshape(1), *operands)


def _adamw(w, g, m, v):
    m = ADAM_B1 * m + (1.0 - ADAM_B1) * g
    v = ADAM_B2 * v + (1.0 - ADAM_B2) * jnp.square(g)
    m_hat = m / (1.0 - ADAM_B1 ** ADAM_STEP)
    v_hat = v / (1.0 - ADAM_B2 ** ADAM_STEP)
    delta = -ADAM_LR * (m_hat / (jnp.sqrt(v_hat) + ADAM_EPS) + ADAM_WD * w)
    return delta, m, v


def _place():
    x, y, c = lax.axis_index("x"), lax.axis_index("y"), lax.axis_index("c")
    chips = [(1 - x, y), (x, 1 - y), (1 - x, 1 - y)]
    return x, y, c, chips


def _remote(src, dst, ssem, rsem, k, dev):
    return pltpu.make_async_remote_copy(src_ref=src, dst_ref=dst, send_sem=ssem.at[k], recv_sem=rsem.at[k],
                                        device_id=dev, device_id_type=MESH)


def _gather_weights(shards):
    n = len(shards)
    halves = [s.shape[1] // 2 for s in shards]

    def body(*refs):
        src, out = refs[:n], refs[n:2 * n]
        ssem, rsem = refs[2 * n:]
        x, y, c, chips = _place()
        me_q = 2 * x + y
        sib = (x, y, 1 - c)

        def half(a, q, cc):
            return out[a].at[q, :, pl.ds(cc * halves[a], halves[a]), :]

        first = []
        for a in range(n):
            mine = src[a].at[:, pl.ds(c * halves[a], halves[a]), :]
            for r, chip in enumerate(chips):
                first.append(_remote(mine, half(a, me_q, c), ssem, rsem, a * 3 + r, (*chip, c)))
        for cp in first:
            cp.start()
        passed = []
        for a in range(n):
            for r, chip in enumerate(chips):
                q = 2 * chip[0] + chip[1]
                k = a * 3 + r
                _remote(half(a, q, c), half(a, q, c), ssem, rsem, k, (*chip, c)).wait_recv()
                cp = _remote(half(a, q, c), half(a, q, c), ssem, rsem, 3 * n + k, sib)
                cp.start()
                passed.append(cp)
        for a in range(n):
            for r, chip in enumerate(chips):
                q = 2 * chip[0] + chip[1]
                _remote(half(a, q, 1 - c), half(a, q, 1 - c), ssem, rsem, 3 * n + a * 3 + r, sib).wait_recv()
        for cp in first + passed:
            cp.wait_send()

    return pl.pallas_call(
        body, name="gather_weights",
        in_specs=[HBM_SPEC] * n, out_specs=[HBM_SPEC] * n,
        out_shape=[jax.ShapeDtypeStruct((N_CHIPS,) + s.shape, s.dtype) for s in shards],
        scratch_shapes=[pltpu.SemaphoreType.DMA((6 * n,)), pltpu.SemaphoreType.DMA((6 * n,))],
        compiler_params=_params(has_side_effects=True),
    )(*shards)


def _gather_over_ici(shards):
    n = len(shards)
    halves = [s.shape[1] // 2 for s in shards]

    def copies(src, out, ssem, rsem):
        x, y, c, chips = _place()
        me_q = 2 * x + y
        res = []
        for a in range(n):
            rows = pl.ds(c * halves[a], halves[a])
            mine = src[a].at[:, rows, :]
            for r, chip in enumerate(chips):
                dev = (*chip, c)
                res.append((_remote(mine, out[a].at[me_q, :, rows, :], ssem, rsem, a * 3 + r, dev),
                            _remote(mine, out[a].at[2 * chip[0] + chip[1], :, rows, :], ssem, rsem, a * 3 + r, dev)))
        return res

    return _Hosted(list(shards), [jax.ShapeDtypeStruct((N_CHIPS,) + s.shape, s.dtype) for s in shards], 3 * n, copies)


def _pass_over_d2d(gathered):
    n = len(gathered)
    halves = [g.shape[2] // 2 for g in gathered]

    def copies(_, out, ssem, rsem):
        x, y, c, chips = _place()
        sib = (x, y, 1 - c)
        res = []
        for a in range(n):
            for r, chip in enumerate(chips):
                q = 2 * chip[0] + chip[1]
                mine = out[a].at[q, :, pl.ds(c * halves[a], halves[a]), :]
                theirs = out[a].at[q, :, pl.ds((1 - c) * halves[a], halves[a]), :]
                res.append((_remote(mine, mine, ssem, rsem, a * 3 + r, sib),
                            _remote(theirs, theirs, ssem, rsem, a * 3 + r, sib)))
        return res

    return _Hosted(list(gathered), [jax.ShapeDtypeStruct(g.shape, g.dtype) for g in gathered], 3 * n, copies,
                   in_place=True)


def _pass_to_sibling(gathered):
    n = len(gathered)
    halves = [g.shape[2] // 2 for g in gathered]

    def body(*refs):
        out = refs[n:2 * n]
        ssem, rsem = refs[2 * n:]
        x, y, c, chips = _place()
        sib = (x, y, 1 - c)

        def half(a, q, cc):
            return out[a].at[q, :, pl.ds(cc * halves[a], halves[a]), :]

        cps = []
        for a in range(n):
            for r, chip in enumerate(chips):
                q = 2 * chip[0] + chip[1]
                cps.append(_remote(half(a, q, c), half(a, q, c), ssem, rsem, a * 3 + r, sib))
        for cp in cps:
            cp.start()
        for a in range(n):
            for r, chip in enumerate(chips):
                q = 2 * chip[0] + chip[1]
                _remote(half(a, q, 1 - c), half(a, q, 1 - c), ssem, rsem, a * 3 + r, sib).wait_recv()
        for cp in cps:
            cp.wait_send()

    return pl.pallas_call(
        body, name="pass_to_sibling",
        in_specs=[HBM_SPEC] * n, out_specs=[HBM_SPEC] * n,
        out_shape=[jax.ShapeDtypeStruct(g.shape, g.dtype) for g in gathered],
        input_output_aliases={a: a for a in range(n)},
        scratch_shapes=[pltpu.SemaphoreType.DMA((3 * n,)), pltpu.SemaphoreType.DMA((3 * n,))],
        compiler_params=_params(has_side_effects=True),
    )(*gathered)


def _scatter_over_ici(parts):
    n = len(parts)

    def copies(src, out, ssem, rsem):
        x, y, c, chips = _place()
        res = []
        for a in range(n):
            for r, chip in enumerate(chips):
                cp = _remote(src[a].at[2 * chip[0] + chip[1]], out[a].at[r], ssem, rsem, a * 3 + r, (*chip, c))
                res.append((cp, cp))
        return res

    return _Hosted(list(parts), [jax.ShapeDtypeStruct((3,) + p.shape[1:], F32) for p in parts], 3 * n, copies)


def _swap_over_d2d(grads):
    n = len(grads)
    halves = [g.shape[1] // 2 for g in grads]

    def copies(src, out, ssem, rsem):
        x, y, c, _ = _place()
        res = []
        for a in range(n):
            cp = _remote(src[a].at[:, pl.ds((1 - c) * halves[a], halves[a]), :], out[a], ssem, rsem, a, (x, y, 1 - c))
            res.append((cp, cp))
        return res

    return _Hosted(list(grads), [jax.ShapeDtypeStruct((N_CHIPS, h, g.shape[2]), F32) for g, h in zip(grads, halves)],
                   n, copies)


def _swap_halves(grads):
    n = len(grads)
    halves = [g.shape[1] // 2 for g in grads]

    def body(*refs):
        src, out = refs[:n], refs[n:2 * n]
        ssem, rsem = refs[2 * n:]
        x, y, c, _ = _place()
        cps = [_remote(src[a].at[:, pl.ds((1 - c) * halves[a], halves[a]), :], out[a], ssem, rsem, a, (x, y, 1 - c))
               for a in range(n)]
        for cp in cps:
            cp.start()
        for cp in cps:
            cp.wait()

    return pl.pallas_call(
        body, name="swap_halves",
        in_specs=[HBM_SPEC] * n, out_specs=[HBM_SPEC] * n,
        out_shape=[jax.ShapeDtypeStruct((N_CHIPS, h, g.shape[2]), F32) for g, h in zip(grads, halves)],
        scratch_shapes=[pltpu.SemaphoreType.DMA((n,)), pltpu.SemaphoreType.DMA((n,))],
        compiler_params=_params(has_side_effects=True),
    )(*grads)


def _swap_reduced_over_d2d(reduced):
    n = len(reduced)

    def copies(src, out, ssem, rsem):
        x, y, c, _ = _place()
        res = []
        for a in range(n):
            cp = _remote(src[a], out[a], ssem, rsem, a, (x, y, 1 - c))
            res.append((cp, cp))
        return res

    return _Hosted(list(reduced), [jax.ShapeDtypeStruct(r.shape, F32) for r in reduced], n, copies)


def _swap_reduced(reduced):
    n = len(reduced)

    def body(*refs):
        src, out = refs[:n], refs[n:2 * n]
        ssem, rsem = refs[2 * n:]
        x, y, c, _ = _place()
        cps = [_remote(src[a], out[a], ssem, rsem, a, (x, y, 1 - c)) for a in range(n)]
        for cp in cps:
            cp.start()
        for cp in cps:
            cp.wait()

    return pl.pallas_call(
        body, name="swap_reduced",
        in_specs=[HBM_SPEC] * n, out_specs=[HBM_SPEC] * n,
        out_shape=[jax.ShapeDtypeStruct(r.shape, F32) for r in reduced],
        scratch_shapes=[pltpu.SemaphoreType.DMA((n,)), pltpu.SemaphoreType.DMA((n,))],
        compiler_params=_params(has_side_effects=True),
    )(*reduced)


def _allreduce_small(buf, hosted=None):
    R, L = buf.shape

    def body(*refs):
        (buf_ref, out_ref, pair_ref, chip_ref, ssem, rsem), start, wait = _host(hosted, refs, 1, 1, True, True)
        start()
        x, y, c, chips = _place()
        me_q = 2 * x + y
        pair_ref[c] = buf_ref[...]
        to_sib = _remote(buf_ref, pair_ref.at[c], ssem, rsem, 0, (x, y, 1 - c))
        to_sib.start()
        _remote(buf_ref, pair_ref.at[1 - c], ssem, rsem, 0, (x, y, 1 - c)).wait_recv()
        chip_ref[me_q] = pair_ref[0] + pair_ref[1]
        cps = [_remote(chip_ref.at[me_q], chip_ref.at[me_q], ssem, rsem, 1 + r, (*chip, c))
               for r, chip in enumerate(chips)]
        for cp in cps:
            cp.start()
        for r, chip in enumerate(chips):
            q = 2 * chip[0] + chip[1]
            _remote(chip_ref.at[q], chip_ref.at[q], ssem, rsem, 1 + r, (*chip, c)).wait_recv()
        out_ref[...] = (chip_ref[0] + chip_ref[1]) + (chip_ref[2] + chip_ref[3])
        to_sib.wait_send()
        for cp in cps:
            cp.wait_send()
        wait()

    h_ins = hosted.ins if hosted else []
    res = pl.pallas_call(
        body, name="allreduce_small",
        in_specs=[VMEM_SPEC] + [HBM_SPEC] * len(h_ins), out_specs=[VMEM_SPEC] + [HBM_SPEC] * len(h_ins),
        out_shape=[jax.ShapeDtypeStruct((R, L), F32)] + (hosted.out_shapes if hosted else []),
        scratch_shapes=[pltpu.VMEM((2, R, L), F32), pltpu.VMEM((N_CHIPS, R, L), F32),
                        pltpu.SemaphoreType.DMA((4,)), pltpu.SemaphoreType.DMA((4,))]
        + (hosted.sems() if hosted else []),
        compiler_params=_params(has_side_effects=True),
    )(buf, *h_ins)
    return res[0], res[1:]


def _pack(arrays):
    flat = jnp.concatenate([a.reshape(-1) for a in arrays])
    pad = (-flat.shape[0]) % (8 * LANES)
    return jnp.pad(flat, (0, pad)).reshape(-1, LANES)


def _unpack(buf, like):
    flat = buf.reshape(-1)
    out, off = [], 0
    for a in like:
        out.append(flat[off:off + a.size].reshape(a.shape))
        off += a.size
    return out


def _block_diag(pw):
    rows = []
    for gi in range(len(POOL_WINDOWS)):
        blocks = [pw[gi] if gj == gi else jnp.zeros_like(pw[gi]) for gj in range(len(POOL_WINDOWS))]
        rows.append(jnp.concatenate(blocks, axis=1))
    return jnp.concatenate(rows, axis=0)


def kernel(x, norm1, w_in, pool_w, pool_scale, sg_norm, sg_w, sg_b, w_out, norm2, w_up, w_down, final_norm, loss_target, m_norm1, m_w_in, m_pool_w, m_pool_scale, m_sg_norm, m_sg_w, m_sg_b, m_w_out, m_norm2, m_w_up, m_w_down, m_final_norm, v_norm1, v_w_in, v_pool_w, v_pool_scale, v_sg_norm, v_sg_w, v_sg_b, v_w_out, v_norm2, v_w_up, v_w_down, v_final_norm):
    depth = norm1.shape[0]
    T = x.shape[1]
    xs = x.reshape(T, D_MODEL)
    target = loss_target.reshape(T, D_MODEL)

    assert depth == 2
    c_idx = lax.axis_index("c")
    q_idx = 2 * lax.axis_index("x") + lax.axis_index("y")
    own = [w.astype(BF16) for w in (w_in, w_out, w_up, w_down)]
    gathered = {(0, 0): _gather_weights([own[0][:1]])[0]}

    def full(a, l, axis):
        blocks = lax.dynamic_update_slice(gathered[(a, l)], own[a][l][None, None], (q_idx, 0, 0, 0))[:, 0]
        if axis is None:
            return blocks
        if axis == 0:
            return blocks.reshape(-1, blocks.shape[-1])
        return jnp.concatenate([blocks[q] for q in range(N_CHIPS)], axis=axis)

    half_way = {}

    def gather_behind(call, keys, at_once):
        res, over_ici = call(_gather_over_ici([own[a][l:l + 1] for a, l in keys]))
        gathered.update(zip(keys[:at_once], _pass_to_sibling(over_ici[:at_once])))
        half_way.update(zip(keys[at_once:], over_ici[at_once:]))
        return res

    def pass_behind(call, keys):
        res, done = call(_pass_over_d2d([half_way.pop(k) for k in keys]))
        gathered.update(zip(keys, done))
        return res

    tril = jnp.tril(jnp.ones((CHUNK, CHUNK), F32))
    saved = []
    cur = xs
    wi, wo, wu, wd = {}, {}, {}, {}
    for l in range(depth):
        wbd = _block_diag(pool_w[l]).astype(BF16)
        wm = sg_w[l] * tril
        wm_s = wm.reshape(SG_HEADS * CHUNK, CHUNK).astype(BF16)
        wmt_s = jnp.swapaxes(wm, 1, 2).reshape(SG_HEADS * CHUNK, CHUNK).astype(BF16)
        bias = jnp.repeat(sg_b[l].T, SB_HD, axis=1)
        n1, n2 = norm1[l][None], norm2[l][None]
        psc, sgn = pool_scale[l][None], sg_norm[l][None]
        wi[l] = full(0, l, 1)
        proj, h, qkv = _inproj_fwd(cur, n1, wi[l])
        ya = _pool_fwd(proj, wbd, psc)
        yb = _sg_fwd(proj, wm_s, bias, sgn)
        if l == 0:
            yc = gather_behind(lambda hosted: _attn_fwd(qkv, hosted), [(1, 0), (2, 0), (3, 0)], 1)
            wo[l] = full(1, l, 0)
            x1, ymix = pass_behind(lambda hosted: _outproj_fwd(cur, ya, yb, yc, wo[l], hosted), [(2, 0), (3, 0)])
        else:
            yc = pass_behind(lambda hosted: _attn_fwd(qkv, hosted), [(1, l), (2, l), (3, l)])
            wo[l] = full(1, l, 0)
            (x1, ymix), _ = _outproj_fwd(cur, ya, yb, yc, wo[l])
        wu[l], wd[l] = full(2, l, None), full(3, l, 0)
        if l == 0:
            x2, h2, u, act = gather_behind(lambda hosted: _mlp_fwd(x1, n2, wu[l], wd[l], hosted),
                                           [(0, 1), (1, 1), (2, 1), (3, 1)], 1)
        else:
            (x2, h2, u, act), _ = _mlp_fwd(x1, n2, wu[l], wd[l])
        saved.append(dict(x0=cur, x1=x1, proj=proj, h=h, qkv=qkv, yc=yc, ymix=ymix, h2=h2, u=u, act=act,
                          wbd=wbd, wm_s=wm_s, wmt_s=wmt_s, bias=bias, n1=n1, n2=n2, psc=psc, sgn=sgn))
        cur = x2

    loss_row, dcur, d_final = _loss_head(cur, final_norm[None], target)

    small = [None] * depth
    grads, parts, reduced = {}, {}, {}

    def pair_up(keys, swapped):
        parts.update(zip(keys, _add_pairs([grads[k] for k in keys], swapped, c_idx)))

    def chip_up(keys, arrived):
        reduced.update(zip(keys, _add_chips([parts[k] for k in keys], arrived, q_idx)))

    for l in reversed(range(depth)):
        s = saved[l]
        if l == 0:
            keys = [(2, 1), (3, 1)]
            (dx1, du, d_n2, dx1b), arrived = _mlp_bwd(dcur, s["x1"], s["n2"], s["u"], wu[l], wd[l],
                                                _scatter_over_ici([parts[k] for k in keys]))
            chip_up(keys, arrived)
        else:
            (dx1, du, d_n2, dx1b), _ = _mlp_bwd(dcur, s["x1"], s["n2"], s["u"], wu[l], wd[l])
        if l == 0:
            keys = [(0, 1)]
            grads[(2, l)], arrived = _tn_matmul(s["h2"], du, "grad_w_up", n_split=N_CHIPS,
                                                hosted=_scatter_over_ici([parts[k] for k in keys]))
            chip_up(keys, arrived)
            keys = [(1, 1)]
            g_down, arrived = _tn_matmul(s["act"], dcur, "grad_w_down",
                                         hosted=_scatter_over_ici([parts[k] for k in keys]))
            chip_up(keys, arrived)
        else:
            grads[(2, l)] = _tn_matmul(s["h2"], du, "grad_w_up", n_split=N_CHIPS)
            g_down = _tn_matmul(s["act"], dcur, "grad_w_down")
        grads[(3, l)] = g_down[0].reshape(N_CHIPS, D_FF // N_CHIPS, D_MODEL)
        dymix = _nt_matmul(dx1b, wo[l])
        grads[(1, l)] = _tn_matmul(s["ymix"], dx1b, "grad_w_out")[0].reshape(N_CHIPS, D_MODEL // N_CHIPS, D_MODEL)
        da_in, d_wbd, d_psc = _pool_bwd(s["proj"], dymix, s["wbd"], s["psc"])
        if l == 0:
            keys = [(1, 0), (2, 0), (3, 0)]
            (du_pre, dv_pre, d_wm, d_bias, d_sgn), swapped = _sg_bwd(
                s["proj"], dymix, s["wm_s"], s["wmt_s"], s["bias"], s["sgn"], _swap_over_d2d([grads[k] for k in keys]))
            pair_up(keys, swapped)
            (dq, dk, dv), arrived = _attn_bwd(s["qkv"], s["yc"], dymix, _scatter_over_ici([parts[k] for k in keys]))
            chip_up(keys, arrived)
        else:
            (du_pre, dv_pre, d_wm, d_bias, d_sgn), _ = _sg_bwd(s["proj"], dymix, s["wm_s"], s["wmt_s"], s["bias"], s["sgn"])
            keys = [(1, l), (2, l), (3, l)]
            (dq, dk, dv), swapped = _attn_bwd(s["qkv"], s["yc"], dymix, _swap_over_d2d([grads[k] for k in keys]))
            pair_up(keys, swapped)
        pieces = [da_in, du_pre, dv_pre, dq, dk, dv]
        if l == 0:
            keys = sorted(reduced)
            g_in_l, swapped = _inproj_grad(s["h"], pieces, _swap_reduced_over_d2d([reduced[k] for k in keys]))
            theirs = dict(zip(keys, swapped))
        else:
            g_in_l, _ = _inproj_grad(s["h"], pieces)
        grads[(0, l)] = g_in_l[0].reshape(D_MODEL, N_CHIPS, IN_COLS // N_CHIPS).transpose(1, 0, 2)
        if l == 0:
            keys = [(0, 0)]
            pair_up(keys, _swap_halves([grads[k] for k in keys]))
            (dx0, d_n1), arrived = _inproj_bwd(pieces, wi[l], s["x0"], s["n1"], dx1,
                                               _scatter_over_ici([parts[k] for k in keys]))
            chip_up(keys, arrived)
        else:
            keys = [(0, l)]
            (dx0, d_n1), swapped = _inproj_bwd(pieces, wi[l], s["x0"], s["n1"], dx1,
                                               _swap_over_d2d([grads[k] for k in keys]))
            pair_up(keys, swapped)
        d_pw = jnp.stack([d_wbd[gi * POOL_GW:(gi + 1) * POOL_GW, gi * POOL_GW:(gi + 1) * POOL_GW]
                          for gi in range(len(POOL_WINDOWS))])
        small[l] = dict(norm1=d_n1[0], pool_w=d_pw, pool_scale=d_psc[0], sg_norm=d_sgn[0],
                        sg_w=d_wm.reshape(SG_HEADS, CHUNK, CHUNK), sg_b=d_bias[:, :SG_HEADS].T, norm2=d_n2[0])
        dcur = dx0
    grad_x = dcur.reshape(x.shape)

    names = ["norm1", "pool_w", "pool_scale", "sg_norm", "sg_w", "sg_b", "norm2"]
    slot = jnp.zeros((1,), F32)
    small_w = [norm1, pool_w, pool_scale, sg_norm, sg_w, sg_b, norm2, final_norm, slot]
    small_m = [m_norm1, m_pool_w, m_pool_scale, m_sg_norm, m_sg_w, m_sg_b, m_norm2, m_final_norm, slot]
    small_v = [v_norm1, v_pool_w, v_pool_scale, v_sg_norm, v_sg_w, v_sg_b, v_norm2, v_final_norm, slot]
    small_g = [jnp.stack([small[l][k] for l in range(depth)]) for k in names] + [d_final[0], loss_row[0, :1]]
    keys = [(0, 0)]
    g_packed, _ = _allreduce_small(_pack(small_g))
    theirs.update(zip(keys, _swap_reduced([reduced[k] for k in keys])))

    def joined(a):
        layers = []
        for l in range(depth):
            mine, other = reduced[(a, l)], theirs[(a, l)]
            layers.append(jnp.where(c_idx == 0, jnp.concatenate([mine, other]), jnp.concatenate([other, mine])))
        return jnp.stack(layers)

    gw_in, gw_out, gw_up, gw_down = [joined(a) for a in range(4)]

    loss = _unpack(g_packed, small_w)[-1][0]
    s_delta, s_m, s_v = _elementwise(_adamw, "adamw_small", [_pack(small_w), g_packed, _pack(small_m), _pack(small_v)], 3)
    gs = dict(zip(names + ["final_norm"], _unpack(g_packed, small_w)))
    ds = dict(zip(names + ["final_norm"], _unpack(s_delta, small_w)))
    ms = dict(zip(names + ["final_norm"], _unpack(s_m, small_w)))
    vs = dict(zip(names + ["final_norm"], _unpack(s_v, small_w)))

    big_g = dict(w_in=gw_in, w_out=gw_out, w_up=gw_up, w_down=gw_down)
    big_w = dict(w_in=(w_in, m_w_in, v_w_in), w_out=(w_out, m_w_out, v_w_out),
                 w_up=(w_up, m_w_up, v_w_up), w_down=(w_down, m_w_down, v_w_down))
    for k, (w, m, v) in big_w.items():
        operands = [w, big_g[k], m, v]
        if k == "w_in":
            operands = [jnp.swapaxes(o, 1, 2) for o in operands]
        ds[k], ms[k], vs[k] = _elementwise(_adamw, "adamw_" + k, operands, 3)
        if k == "w_in":
            ds[k], ms[k], vs[k] = [jnp.swapaxes(o, 1, 2) for o in (ds[k], ms[k], vs[k])]
        gs[k] = big_g[k]

    order = ["norm1", "w_in", "pool_w", "pool_scale", "sg_norm", "sg_w", "sg_b", "w_out", "norm2", "w_up", "w_down",
             "final_norm"]
    return (loss, grad_x, *[gs[k] for k in order], *[ds[k] for k in order], *[ms[k] for k in order],
            *[vs[k] for k in order])
```
